```python
import math
import jax, jax.numpy as jnp
from jax import lax
import numpy as np

D_MODEL = 1024
BATCH = 8
SEQ = 4096
DEPTH = 4

N_MIXERS = 2
SSM_GROUP = 16
SSM_GROUPS = D_MODEL // SSM_GROUP
SSM_STATE = 64
CONV_WIDTH = 3
FFN_HIDDEN = ((8 * D_MODEL + 3 * 256 - 1) // (3 * 256)) * 256
N_SSM_LAYERS = (DEPTH + 1) // 2
N_CONV_LAYERS = DEPTH // 2
RMS_EPS = 1e-6
DT_MIN = 1e-3
DT_MAX = 1e-1

kernel_name = "hybrid_s5_shortconv_adaln"


def rms_norm(x, g):
    x32 = x.astype(jnp.float32)
    y = x32 * lax.rsqrt(jnp.mean(x32 * x32, axis=-1, keepdims=True) + RMS_EPS)
    return (y * g.astype(jnp.float32)).astype(x.dtype)


def modulate(h, shift, scale):
    return h * (1.0 + scale) + shift


def _ssm_combine(e1, e2):
    a1r, a1i, b1r, b1i = e1
    a2r, a2i, b2r, b2i = e2
    ar = a2r * a1r - a2i * a1i
    ai = a2r * a1i + a2i * a1r
    br = a2r * b1r - a2i * b1i + b2r
    bi = a2r * b1i + a2i * b1r + b2i
    return (ar, ai, br, bi)


def s5_mixer(h, a_re, a_im, log_step, b_re, b_im, c_re, c_im, d, w_out):
    bsz, seqlen, _ = h.shape
    f32 = jnp.float32
    u = h.astype(f32).reshape(bsz, seqlen, SSM_GROUPS, SSM_GROUP)
    lr = jnp.minimum(a_re.astype(f32), -1e-4)
    li = a_im.astype(f32)
    dt = jnp.exp(log_step.astype(f32))[:, None]
    mag = jnp.exp(lr * dt)
    abr = mag * jnp.cos(li * dt)
    abi = mag * jnp.sin(li * dt)
    den = lr * lr + li * li
    qr = ((abr - 1.0) * lr + abi * li) / den
    qi = (abi * lr - (abr - 1.0) * li) / den
    br, bim = b_re.astype(f32), b_im.astype(f32)
    bbar_re = qr[..., None] * br - qi[..., None] * bim
    bbar_im = qr[..., None] * bim + qi[..., None] * br
    bu_re = jnp.einsum('blgh,gph->blgp', u, bbar_re)
    bu_im = jnp.einsum('blgh,gph->blgp', u, bbar_im)
    a_r = jnp.broadcast_to(abr[None, None], (1, seqlen, SSM_GROUPS, SSM_STATE))
    a_i = jnp.broadcast_to(abi[None, None], (1, seqlen, SSM_GROUPS, SSM_STATE))
    _, _, xr, xi = lax.associative_scan(_ssm_combine, (a_r, a_i, bu_re, bu_im), axis=1)
    y = (jnp.einsum('ghp,blgp->blgh', c_re.astype(f32), xr)
         - jnp.einsum('ghp,blgp->blgh', c_im.astype(f32), xi))
    y = y.reshape(bsz, seqlen, D_MODEL) + d.astype(f32) * h.astype(f32)
    y = jax.nn.gelu(y).astype(h.dtype)
    val, gate = jnp.split(y @ w_out, 2, axis=-1)
    return val * jax.nn.sigmoid(gate)


def short_conv_mixer(h, w_in, conv_w, w_out):
    bg, cg, v = jnp.split(h @ w_in, 3, axis=-1)
    cv = cg * v
    conv = lax.conv_general_dilated(
        cv, conv_w[:, None, :].astype(cv.dtype), window_strides=(1,),
        padding=[(CONV_WIDTH - 1, 0)], dimension_numbers=('NWC', 'WIO', 'NWC'),
        feature_group_count=D_MODEL)
    return (bg * conv) @ w_out


def swiglu(h, w_in, w_out):
    g, u = jnp.split(h @ w_in, 2, axis=-1)
    return (jax.nn.silu(g) * u) @ w_out


def _fwd_setup_inputs(seed: int = 0) -> dict:
    key = jax.random.key(seed)
    ks = jax.random.split(key, 24)
    D, G, P, H, F = D_MODEL, SSM_GROUPS, SSM_STATE, SSM_GROUP, FFN_HIDDEN
    nrm = jax.random.normal
    x = nrm(ks[0], (BATCH, SEQ, D), jnp.float32)
    c = nrm(ks[1], (BATCH, D), jnp.float32)
    norm1_g = 1.0 + 0.02 * nrm(ks[2], (DEPTH, D), jnp.float32)
    norm2_g = 1.0 + 0.02 * nrm(ks[3], (DEPTH, D), jnp.float32)
    w_ada = 0.5 * D ** -0.5 * nrm(ks[4], (DEPTH, D, 6 * D), jnp.float32)
    b_ada = 0.01 * nrm(ks[5], (DEPTH, 6 * D), jnp.float32)
    n_idx = jnp.arange(P, dtype=jnp.float32)
    ssm_a_re = -0.5 + 0.01 * nrm(ks[6], (N_SSM_LAYERS, G, P), jnp.float32)
    ssm_a_im = math.pi * n_idx + 0.01 * nrm(ks[7], (N_SSM_LAYERS, G, P), jnp.float32)
    ssm_log_step = jax.random.uniform(ks[8], (N_SSM_LAYERS, G), jnp.float32,
                                      math.log(DT_MIN), math.log(DT_MAX))
    ssm_b_re = (2 * H) ** -0.5 * nrm(ks[9], (N_SSM_LAYERS, G, P, H), jnp.float32)
    ssm_b_im = (2 * H) ** -0.5 * nrm(ks[10], (N_SSM_LAYERS, G, P, H), jnp.float32)
    ssm_c_re = (2 * P) ** -0.5 * nrm(ks[11], (N_SSM_LAYERS, G, H, P), jnp.float32)
    ssm_c_im = (2 * P) ** -0.5 * nrm(ks[12], (N_SSM_LAYERS, G, H, P), jnp.float32)
    ssm_d = nrm(ks[13], (N_SSM_LAYERS, D), jnp.float32)
    ssm_w_out = D ** -0.5 * nrm(ks[14], (N_SSM_LAYERS, D, 2 * D), jnp.float32)
    conv_w_in = D ** -0.5 * nrm(ks[15], (N_CONV_LAYERS, D, 3 * D), jnp.float32)
    conv_w = CONV_WIDTH ** -0.5 * nrm(ks[16], (N_CONV_LAYERS, CONV_WIDTH, D), jnp.float32)
    conv_w_out = D ** -0.5 * nrm(ks[17], (N_CONV_LAYERS, D, D), jnp.float32)
    w_ffn_in = D ** -0.5 * nrm(ks[18], (DEPTH, D, 2 * F), jnp.float32)
    w_ffn_out = F ** -0.5 * nrm(ks[19], (DEPTH, F, D), jnp.float32)
    final_g = 1.0 + 0.02 * nrm(ks[20], (D,), jnp.float32)
    return {"x": x, "c": c, "norm1_g": norm1_g, "norm2_g": norm2_g,
            "w_ada": w_ada, "b_ada": b_ada,
            "ssm_a_re": ssm_a_re, "ssm_a_im": ssm_a_im, "ssm_log_step": ssm_log_step,
            "ssm_b_re": ssm_b_re, "ssm_b_im": ssm_b_im,
            "ssm_c_re": ssm_c_re, "ssm_c_im": ssm_c_im,
            "ssm_d": ssm_d, "ssm_w_out": ssm_w_out,
            "conv_w_in": conv_w_in, "conv_w": conv_w, "conv_w_out": conv_w_out,
            "w_ffn_in": w_ffn_in, "w_ffn_out": w_ffn_out, "final_g": final_g}


def _fwd_reference(x, c, norm1_g, norm2_g, w_ada, b_ada,
              ssm_a_re, ssm_a_im, ssm_log_step, ssm_b_re, ssm_b_im,
              ssm_c_re, ssm_c_im, ssm_d, ssm_w_out,
              conv_w_in, conv_w, conv_w_out,
              w_ffn_in, w_ffn_out, final_g):
    c_act = jax.nn.silu(c)
    for i in range(DEPTH):
        mods = c_act @ w_ada[i] + b_ada[i]
        sh1, sc1, g1, sh2, sc2, g2 = [m[:, None, :] for m in jnp.split(mods, 6, axis=-1)]
        h = modulate(rms_norm(x, norm1_g[i]), sh1, sc1)
        j = i // N_MIXERS
        if i % N_MIXERS == 0:
            mix = s5_mixer(h, ssm_a_re[j], ssm_a_im[j], ssm_log_step[j],
                           ssm_b_re[j], ssm_b_im[j], ssm_c_re[j], ssm_c_im[j],
                           ssm_d[j], ssm_w_out[j])
        else:
            mix = short_conv_mixer(h, conv_w_in[j], conv_w[j], conv_w_out[j])
        x = x + g1 * mix
        h = modulate(rms_norm(x, norm2_g[i]), sh2, sc2)
        x = x + g2 * swiglu(h, w_ffn_in[i], w_ffn_out[i])
    return rms_norm(x, final_g)


import jax as _jax
import jax.numpy as _jnp

TWIN_FORMAT = 'train_step'
FWD_PARAMS = ['x', 'c', 'norm1_g', 'norm2_g', 'w_ada', 'b_ada', 'ssm_a_re', 'ssm_a_im', 'ssm_log_step', 'ssm_b_re', 'ssm_b_im', 'ssm_c_re', 'ssm_c_im', 'ssm_d', 'ssm_w_out', 'conv_w_in', 'conv_w', 'conv_w_out', 'w_ffn_in', 'w_ffn_out', 'final_g']
TWIN_WEIGHTS = ['norm1_g', 'norm2_g', 'w_ada', 'b_ada', 'ssm_a_re', 'ssm_a_im', 'ssm_log_step', 'ssm_b_re', 'ssm_b_im', 'ssm_c_re', 'ssm_c_im', 'ssm_d', 'ssm_w_out', 'conv_w_in', 'conv_w', 'conv_w_out', 'w_ffn_in', 'w_ffn_out', 'final_g']
TWIN_DIFF_INPUT = 'x'
TWIN_INPUTS = ['x', 'c', 'norm1_g', 'norm2_g', 'w_ada', 'b_ada', 'ssm_a_re', 'ssm_a_im', 'ssm_log_step', 'ssm_b_re', 'ssm_b_im', 'ssm_c_re', 'ssm_c_im', 'ssm_d', 'ssm_w_out', 'conv_w_in', 'conv_w', 'conv_w_out', 'w_ffn_in', 'w_ffn_out', 'final_g', 'loss_target', 'm_norm1_g', 'm_norm2_g', 'm_w_ada', 'm_b_ada', 'm_ssm_a_re', 'm_ssm_a_im', 'm_ssm_log_step', 'm_ssm_b_re', 'm_ssm_b_im', 'm_ssm_c_re', 'm_ssm_c_im', 'm_ssm_d', 'm_ssm_w_out', 'm_conv_w_in', 'm_conv_w', 'm_conv_w_out', 'm_w_ffn_in', 'm_w_ffn_out', 'm_final_g', 'v_norm1_g', 'v_norm2_g', 'v_w_ada', 'v_b_ada', 'v_ssm_a_re', 'v_ssm_a_im', 'v_ssm_log_step', 'v_ssm_b_re', 'v_ssm_b_im', 'v_ssm_c_re', 'v_ssm_c_im', 'v_ssm_d', 'v_ssm_w_out', 'v_conv_w_in', 'v_conv_w', 'v_conv_w_out', 'v_w_ffn_in', 'v_w_ffn_out', 'v_final_g']
TWIN_OUTPUTS = ['loss', 'grad_x', 'grad_norm1_g', 'grad_norm2_g', 'grad_w_ada', 'grad_b_ada', 'grad_ssm_a_re', 'grad_ssm_a_im', 'grad_ssm_log_step', 'grad_ssm_b_re', 'grad_ssm_b_im', 'grad_ssm_c_re', 'grad_ssm_c_im', 'grad_ssm_d', 'grad_ssm_w_out', 'grad_conv_w_in', 'grad_conv_w', 'grad_conv_w_out', 'grad_w_ffn_in', 'grad_w_ffn_out', 'grad_final_g', 'delta_norm1_g', 'delta_norm2_g', 'delta_w_ada', 'delta_b_ada', 'delta_ssm_a_re', 'delta_ssm_a_im', 'delta_ssm_log_step', 'delta_ssm_b_re', 'delta_ssm_b_im', 'delta_ssm_c_re', 'delta_ssm_c_im', 'delta_ssm_d', 'delta_ssm_w_out', 'delta_conv_w_in', 'delta_conv_w', 'delta_conv_w_out', 'delta_w_ffn_in', 'delta_w_ffn_out', 'delta_final_g', 'new_m_norm1_g', 'new_m_norm2_g', 'new_m_w_ada', 'new_m_b_ada', 'new_m_ssm_a_re', 'new_m_ssm_a_im', 'new_m_ssm_log_step', 'new_m_ssm_b_re', 'new_m_ssm_b_im', 'new_m_ssm_c_re', 'new_m_ssm_c_im', 'new_m_ssm_d', 'new_m_ssm_w_out', 'new_m_conv_w_in', 'new_m_conv_w', 'new_m_conv_w_out', 'new_m_w_ffn_in', 'new_m_w_ffn_out', 'new_m_final_g', 'new_v_norm1_g', 'new_v_norm2_g', 'new_v_w_ada', 'new_v_b_ada', 'new_v_ssm_a_re', 'new_v_ssm_a_im', 'new_v_ssm_log_step', 'new_v_ssm_b_re', 'new_v_ssm_b_im', 'new_v_ssm_c_re', 'new_v_ssm_c_im', 'new_v_ssm_d', 'new_v_ssm_w_out', 'new_v_conv_w_in', 'new_v_conv_w', 'new_v_conv_w_out', 'new_v_w_ffn_in', 'new_v_w_ffn_out', 'new_v_final_g']
TWIN_LEAF_KINDS = {'loss': 'loss', 'grad_x': 'grad_x', 'grad_norm1_g': 'grad_w', 'grad_norm2_g': 'grad_w', 'grad_w_ada': 'grad_w', 'grad_b_ada': 'grad_w', 'grad_ssm_a_re': 'grad_w', 'grad_ssm_a_im': 'grad_w', 'grad_ssm_log_step': 'grad_w', 'grad_ssm_b_re': 'grad_w', 'grad_ssm_b_im': 'grad_w', 'grad_ssm_c_re': 'grad_w', 'grad_ssm_c_im': 'grad_w', 'grad_ssm_d': 'grad_w', 'grad_ssm_w_out': 'grad_w', 'grad_conv_w_in': 'grad_w', 'grad_conv_w': 'grad_w', 'grad_conv_w_out': 'grad_w', 'grad_w_ffn_in': 'grad_w', 'grad_w_ffn_out': 'grad_w', 'grad_final_g': 'grad_w', 'delta_norm1_g': 'delta_w', 'delta_norm2_g': 'delta_w', 'delta_w_ada': 'delta_w', 'delta_b_ada': 'delta_w', 'delta_ssm_a_re': 'delta_w', 'delta_ssm_a_im': 'delta_w', 'delta_ssm_log_step': 'delta_w', 'delta_ssm_b_re': 'delta_w', 'delta_ssm_b_im': 'delta_w', 'delta_ssm_c_re': 'delta_w', 'delta_ssm_c_im': 'delta_w', 'delta_ssm_d': 'delta_w', 'delta_ssm_w_out': 'delta_w', 'delta_conv_w_in': 'delta_w', 'delta_conv_w': 'delta_w', 'delta_conv_w_out': 'delta_w', 'delta_w_ffn_in': 'delta_w', 'delta_w_ffn_out': 'delta_w', 'delta_final_g': 'delta_w', 'new_m_norm1_g': 'new_m', 'new_m_norm2_g': 'new_m', 'new_m_w_ada': 'new_m', 'new_m_b_ada': 'new_m', 'new_m_ssm_a_re': 'new_m', 'new_m_ssm_a_im': 'new_m', 'new_m_ssm_log_step': 'new_m', 'new_m_ssm_b_re': 'new_m', 'new_m_ssm_b_im': 'new_m', 'new_m_ssm_c_re': 'new_m', 'new_m_ssm_c_im': 'new_m', 'new_m_ssm_d': 'new_m', 'new_m_ssm_w_out': 'new_m', 'new_m_conv_w_in': 'new_m', 'new_m_conv_w': 'new_m', 'new_m_conv_w_out': 'new_m', 'new_m_w_ffn_in': 'new_m', 'new_m_w_ffn_out': 'new_m', 'new_m_final_g': 'new_m', 'new_v_norm1_g': 'new_v', 'new_v_norm2_g': 'new_v', 'new_v_w_ada': 'new_v', 'new_v_b_ada': 'new_v', 'new_v_ssm_a_re': 'new_v', 'new_v_ssm_a_im': 'new_v', 'new_v_ssm_log_step': 'new_v', 'new_v_ssm_b_re': 'new_v', 'new_v_ssm_b_im': 'new_v', 'new_v_ssm_c_re': 'new_v', 'new_v_ssm_c_im': 'new_v', 'new_v_ssm_d': 'new_v', 'new_v_ssm_w_out': 'new_v', 'new_v_conv_w_in': 'new_v', 'new_v_conv_w': 'new_v', 'new_v_conv_w_out': 'new_v', 'new_v_w_ffn_in': 'new_v', 'new_v_w_ffn_out': 'new_v', 'new_v_final_g': 'new_v'}


def _forward(args):
    return _fwd_reference(*[args[k] for k in FWD_PARAMS])


def _output_shape():
    out = _jax.eval_shape(lambda: _forward(_fwd_setup_inputs(0)))
    return out.shape, out.dtype

N_MICROBATCH = 1
ADAM_LR = 0.001
ADAM_B1 = 0.9
ADAM_B2 = 0.999
ADAM_EPS = 1e-08
ADAM_WD = 0.01
ADAM_STEP = 10
PER_EXAMPLE_BATCH_AXIS = {'x': 0, 'c': 0, 'loss_target': 0}
SHARED_INPUTS = []
_WEIGHT_DTYPES = {'norm1_g': _jnp.float32, 'norm2_g': _jnp.float32, 'w_ada': _jnp.float32, 'b_ada': _jnp.float32, 'ssm_a_re': _jnp.float32, 'ssm_a_im': _jnp.float32, 'ssm_log_step': _jnp.float32, 'ssm_b_re': _jnp.float32, 'ssm_b_im': _jnp.float32, 'ssm_c_re': _jnp.float32, 'ssm_c_im': _jnp.float32, 'ssm_d': _jnp.float32, 'ssm_w_out': _jnp.float32, 'conv_w_in': _jnp.float32, 'conv_w': _jnp.float32, 'conv_w_out': _jnp.float32, 'w_ffn_in': _jnp.float32, 'w_ffn_out': _jnp.float32, 'final_g': _jnp.float32}
MOMENT_SCALE = {'norm1_g': 8.047491e-02, 'norm2_g': 5.756150e-02, 'w_ada': 6.533048e-02, 'b_ada': 1.082991e-01, 'ssm_a_re': 1.829695e-03, 'ssm_a_im': 1.863664e-03, 'ssm_log_step': 9.306277e-01, 'ssm_b_re': 1.086366e-03, 'ssm_b_im': 1.097035e-03, 'ssm_c_re': 2.120056e-03, 'ssm_c_im': 2.128492e-03, 'ssm_d': 2.770504e-02, 'ssm_w_out': 1.870274e-02, 'conv_w_in': 6.579861e-02, 'conv_w': 6.560004e-02, 'conv_w_out': 6.564044e-02, 'w_ffn_in': 2.457055e-02, 'w_ffn_out': 4.007925e-02, 'final_g': 3.214871e+01}


def _to_microbatches(a, axis):
    t = _jnp.moveaxis(a, axis, 0)
    t = t.reshape((N_MICROBATCH, t.shape[0] // N_MICROBATCH) + t.shape[1:])
    return _jnp.moveaxis(t, 1, axis + 1)


def setup_inputs(seed: int = 0) -> dict:
    inp = _fwd_setup_inputs(seed)
    key = _jax.random.fold_in(_jax.random.key(seed), 7919)
    shape, _ = _output_shape()
    out = dict(inp)
    out["loss_target"] = _jax.random.normal(_jax.random.fold_in(key, 0), shape, _jnp.float32)
    for i, name in enumerate(TWIN_WEIGHTS):
        w = inp[name].astype(_jnp.float32)
        if MOMENT_SCALE is None:
            s = _jnp.sqrt(_jnp.mean(_jnp.square(w)) + 1e-30)
        else:
            s = MOMENT_SCALE[name]
        km, kv = _jax.random.split(_jax.random.fold_in(key, i + 1))
        out[name] = w
        out["m_" + name] = s * _jax.random.normal(km, w.shape, _jnp.float32)
        out["v_" + name] = (s * s) * _jax.random.uniform(kv, w.shape, _jnp.float32, 0.5, 1.5)
    if N_MICROBATCH > 1:
        for name, axis in PER_EXAMPLE_BATCH_AXIS.items():
            out[name] = _to_microbatches(out[name], axis)
    return {'x': out['x'], 'c': out['c'], 'norm1_g': out['norm1_g'], 'norm2_g': out['norm2_g'], 'w_ada': out['w_ada'], 'b_ada': out['b_ada'], 'ssm_a_re': out['ssm_a_re'], 'ssm_a_im': out['ssm_a_im'], 'ssm_log_step': out['ssm_log_step'], 'ssm_b_re': out['ssm_b_re'], 'ssm_b_im': out['ssm_b_im'], 'ssm_c_re': out['ssm_c_re'], 'ssm_c_im': out['ssm_c_im'], 'ssm_d': out['ssm_d'], 'ssm_w_out': out['ssm_w_out'], 'conv_w_in': out['conv_w_in'], 'conv_w': out['conv_w'], 'conv_w_out': out['conv_w_out'], 'w_ffn_in': out['w_ffn_in'], 'w_ffn_out': out['w_ffn_out'], 'final_g': out['final_g'], 'loss_target': out['loss_target'], 'm_norm1_g': out['m_norm1_g'], 'm_norm2_g': out['m_norm2_g'], 'm_w_ada': out['m_w_ada'], 'm_b_ada': out['m_b_ada'], 'm_ssm_a_re': out['m_ssm_a_re'], 'm_ssm_a_im': out['m_ssm_a_im'], 'm_ssm_log_step': out['m_ssm_log_step'], 'm_ssm_b_re': out['m_ssm_b_re'], 'm_ssm_b_im': out['m_ssm_b_im'], 'm_ssm_c_re': out['m_ssm_c_re'], 'm_ssm_c_im': out['m_ssm_c_im'], 'm_ssm_d': out['m_ssm_d'], 'm_ssm_w_out': out['m_ssm_w_out'], 'm_conv_w_in': out['m_conv_w_in'], 'm_conv_w': out['m_conv_w'], 'm_conv_w_out': out['m_conv_w_out'], 'm_w_ffn_in': out['m_w_ffn_in'], 'm_w_ffn_out': out['m_w_ffn_out'], 'm_final_g': out['m_final_g'], 'v_norm1_g': out['v_norm1_g'], 'v_norm2_g': out['v_norm2_g'], 'v_w_ada': out['v_w_ada'], 'v_b_ada': out['v_b_ada'], 'v_ssm_a_re': out['v_ssm_a_re'], 'v_ssm_a_im': out['v_ssm_a_im'], 'v_ssm_log_step': out['v_ssm_log_step'], 'v_ssm_b_re': out['v_ssm_b_re'], 'v_ssm_b_im': out['v_ssm_b_im'], 'v_ssm_c_re': out['v_ssm_c_re'], 'v_ssm_c_im': out['v_ssm_c_im'], 'v_ssm_d': out['v_ssm_d'], 'v_ssm_w_out': out['v_ssm_w_out'], 'v_conv_w_in': out['v_conv_w_in'], 'v_conv_w': out['v_conv_w'], 'v_conv_w_out': out['v_conv_w_out'], 'v_w_ffn_in': out['v_w_ffn_in'], 'v_w_ffn_out': out['v_w_ffn_out'], 'v_final_g': out['v_final_g']}


def _loss(weights, diff, rest, loss_target):
    with _jax.named_scope("forward"):
        args = {**rest, TWIN_DIFF_INPUT: diff, **{k: w.astype(_WEIGHT_DTYPES[k]) for k, w in weights.items()}}
        y = _forward(args)
    with _jax.named_scope("loss_head"):
        err = _jnp.square(y.astype(_jnp.float32) - loss_target)
        return 0.5 * _jnp.sum(_jnp.mean(err, axis=-1)) if err.ndim else 0.5 * err


def _adamw(w, g, m, v):
    m = ADAM_B1 * m + (1.0 - ADAM_B1) * g
    v = ADAM_B2 * v + (1.0 - ADAM_B2) * _jnp.square(g)
    m_hat = m / (1.0 - ADAM_B1 ** ADAM_STEP)
    v_hat = v / (1.0 - ADAM_B2 ** ADAM_STEP)
    delta = -ADAM_LR * (m_hat / (_jnp.sqrt(v_hat) + ADAM_EPS) + ADAM_WD * w)
    return delta, m, v


def reference(x, c, norm1_g, norm2_g, w_ada, b_ada, ssm_a_re, ssm_a_im, ssm_log_step, ssm_b_re, ssm_b_im, ssm_c_re, ssm_c_im, ssm_d, ssm_w_out, conv_w_in, conv_w, conv_w_out, w_ffn_in, w_ffn_out, final_g, loss_target, m_norm1_g, m_norm2_g, m_w_ada, m_b_ada, m_ssm_a_re, m_ssm_a_im, m_ssm_log_step, m_ssm_b_re, m_ssm_b_im, m_ssm_c_re, m_ssm_c_im, m_ssm_d, m_ssm_w_out, m_conv_w_in, m_conv_w, m_conv_w_out, m_w_ffn_in, m_w_ffn_out, m_final_g, v_norm1_g, v_norm2_g, v_w_ada, v_b_ada, v_ssm_a_re, v_ssm_a_im, v_ssm_log_step, v_ssm_b_re, v_ssm_b_im, v_ssm_c_re, v_ssm_c_im, v_ssm_d, v_ssm_w_out, v_conv_w_in, v_conv_w, v_conv_w_out, v_w_ffn_in, v_w_ffn_out, v_final_g):
    given = dict(x=x, c=c, norm1_g=norm1_g, norm2_g=norm2_g, w_ada=w_ada, b_ada=b_ada, ssm_a_re=ssm_a_re, ssm_a_im=ssm_a_im, ssm_log_step=ssm_log_step, ssm_b_re=ssm_b_re, ssm_b_im=ssm_b_im, ssm_c_re=ssm_c_re, ssm_c_im=ssm_c_im, ssm_d=ssm_d, ssm_w_out=ssm_w_out, conv_w_in=conv_w_in, conv_w=conv_w, conv_w_out=conv_w_out, w_ffn_in=w_ffn_in, w_ffn_out=w_ffn_out, final_g=final_g, loss_target=loss_target, m_norm1_g=m_norm1_g, m_norm2_g=m_norm2_g, m_w_ada=m_w_ada, m_b_ada=m_b_ada, m_ssm_a_re=m_ssm_a_re, m_ssm_a_im=m_ssm_a_im, m_ssm_log_step=m_ssm_log_step, m_ssm_b_re=m_ssm_b_re, m_ssm_b_im=m_ssm_b_im, m_ssm_c_re=m_ssm_c_re, m_ssm_c_im=m_ssm_c_im, m_ssm_d=m_ssm_d, m_ssm_w_out=m_ssm_w_out, m_conv_w_in=m_conv_w_in, m_conv_w=m_conv_w, m_conv_w_out=m_conv_w_out, m_w_ffn_in=m_w_ffn_in, m_w_ffn_out=m_w_ffn_out, m_final_g=m_final_g, v_norm1_g=v_norm1_g, v_norm2_g=v_norm2_g, v_w_ada=v_w_ada, v_b_ada=v_b_ada, v_ssm_a_re=v_ssm_a_re, v_ssm_a_im=v_ssm_a_im, v_ssm_log_step=v_ssm_log_step, v_ssm_b_re=v_ssm_b_re, v_ssm_b_im=v_ssm_b_im, v_ssm_c_re=v_ssm_c_re, v_ssm_c_im=v_ssm_c_im, v_ssm_d=v_ssm_d, v_ssm_w_out=v_ssm_w_out, v_conv_w_in=v_conv_w_in, v_conv_w=v_conv_w, v_conv_w_out=v_conv_w_out, v_w_ffn_in=v_w_ffn_in, v_w_ffn_out=v_w_ffn_out, v_final_g=v_final_g)
    weights = {n: given[n] for n in TWIN_WEIGHTS}
    shared = {n: given[n] for n in SHARED_INPUTS}
    per_example = {n: given[n] for n in ['x', 'c']}
    grad_fn = _jax.value_and_grad(_loss, argnums=(0, 1))

    def one_microbatch(ex, loss_target):
        ex = dict(ex)
        diff = ex.pop(TWIN_DIFF_INPUT)
        return grad_fn(weights, diff, {**shared, **ex}, loss_target)

    if N_MICROBATCH == 1:
        loss, (grad_w, grad_x) = one_microbatch(per_example, given["loss_target"])
    else:
        def body(carry, xs):
            loss_sum, grad_sum = carry
            l_k, (gw_k, gx_k) = one_microbatch(xs[0], xs[1])
            with _jax.named_scope("update"):
                return (loss_sum + l_k, _jax.tree.map(_jnp.add, grad_sum, gw_k)), gx_k

        init = (_jnp.zeros((), _jnp.float32), _jax.tree.map(_jnp.zeros_like, weights))
        (loss, grad_w), grad_x = _jax.lax.scan(body, init, (per_example, given["loss_target"]))
    with _jax.named_scope("update"):
        delta_w, new_m, new_v = {}, {}, {}
        for n in TWIN_WEIGHTS:
            delta_w[n], new_m[n], new_v[n] = _adamw(weights[n], grad_w[n], given["m_" + n], given["v_" + n])
    return (loss, grad_x, *[grad_w[n] for n in TWIN_WEIGHTS], *[delta_w[n] for n in TWIN_WEIGHTS],
            *[new_m[n] for n in TWIN_WEIGHTS], *[new_v[n] for n in TWIN_WEIGHTS])
```

```python
import functools
import math

import jax
import jax.numpy as jnp
from jax import lax
from jax.experimental import pallas as pl
from jax.experimental.pallas import tpu as pltpu

F32 = jnp.float32
BF = jnp.bfloat16
MESH = pl.DeviceIdType.MESH
ANY = pl.BlockSpec(memory_space=pl.ANY)

N_DEV = 8
N_CHIP = 4
DEPTH = 4
SSM_GROUP = 16
SSM_STATE = 64
S5_BLOCK = 256
RMS_EPS = 1e-6
ADAM_LR, ADAM_B1, ADAM_B2, ADAM_EPS, ADAM_WD, ADAM_STEP = 0.001, 0.9, 0.999, 1e-08, 0.01, 10
V7X_VMEM_BYTES = 64 * 1024 * 1024
VMEM_LIMIT = V7X_VMEM_BYTES - 12 * 1024 * 1024
SLAB_W = 1024
GELU_C = math.sqrt(2.0 / math.pi)
GELU_A = 0.044715


def _cp(*sem):
    return pltpu.CompilerParams(dimension_semantics=sem if sem else None, vmem_limit_bytes=VMEM_LIMIT)


def _tile(n, prefs):
    for p in prefs:
        if p <= n and n % p == 0:
            return p
    return n


def _axes():
    return lax.axis_index("x"), lax.axis_index("y"), lax.axis_index("c")


def _flip(v, k):
    return 1 - v if k else v


def gather8(v, name):
    R, C = v.shape

    def body(v_ref, o_ref, ssem, rsem, lsem):
        x, y, c = _axes()
        me = 4 * x + 2 * y + c
        loc = pltpu.make_async_copy(v_ref, o_ref.at[me], lsem)
        loc.start()
        copies = []
        for k in range(1, N_DEV):
            peer = (_flip(x, (k >> 2) & 1), _flip(y, (k >> 1) & 1), _flip(c, k & 1))
            cp = pltpu.make_async_remote_copy(src_ref=v_ref, dst_ref=o_ref.at[me], send_sem=ssem.at[k - 1],
                                              recv_sem=rsem.at[k - 1], device_id=peer, device_id_type=MESH)
            cp.start()
            copies.append(cp)
        for cp in copies:
            cp.wait()
        loc.wait()

    return pl.pallas_call(
        body, name=name,
        out_shape=jax.ShapeDtypeStruct((N_DEV, R, C), v.dtype),
        in_specs=[pl.BlockSpec(memory_space=pltpu.VMEM)],
        out_specs=pl.BlockSpec(memory_space=pltpu.VMEM),
        scratch_shapes=[pltpu.SemaphoreType.DMA((N_DEV - 1,)), pltpu.SemaphoreType.DMA((N_DEV - 1,)),
                        pltpu.SemaphoreType.DMA],
        compiler_params=pltpu.CompilerParams(vmem_limit_bytes=VMEM_LIMIT),
    )(v)


def gather_weights(shards):
    n = len(shards)

    def body(*refs):
        src, dst = refs[:n], refs[n:2 * n]
        ssem, rsem, lsem = refs[2 * n:]
        x, y, c = _axes()
        chip = 2 * x + y
        pending = []
        for a in range(n):
            loc = pltpu.make_async_copy(src[a], dst[a].at[:, chip], lsem.at[a])
            loc.start()
            pending.append(loc)
            for k in range(1, N_CHIP):
                peer = (_flip(x, k >> 1), _flip(y, k & 1), c)
                cp = pltpu.make_async_remote_copy(src_ref=src[a], dst_ref=dst[a].at[:, chip],
                                                  send_sem=ssem.at[3 * a + k - 1], recv_sem=rsem.at[3 * a + k - 1],
                                                  device_id=peer, device_id_type=MESH)
                cp.start()
                pending.append(cp)
        for p in pending:
            p.wait()

    outs = [jax.ShapeDtypeStruct((s.shape[0], N_CHIP) + s.shape[1:], s.dtype) for s in shards]
    return pl.pallas_call(
        body, name="gather_weights", out_shape=outs, in_specs=[ANY] * n, out_specs=[ANY] * n,
        scratch_shapes=[pltpu.SemaphoreType.DMA((3 * n,)), pltpu.SemaphoreType.DMA((3 * n,)),
                        pltpu.SemaphoreType.DMA((n,))],
        compiler_params=pltpu.CompilerParams(vmem_limit_bytes=VMEM_LIMIT),
    )(*shards)


def scatter_grads(groups):
    flat = [a for g in groups for a in g]
    n, ng = len(flat), len(groups)

    def body(*refs):
        src = refs[:n]
        mine, sib = refs[n:n + ng], refs[n + ng:n + 2 * ng]
        ssem, rsem, lsem = refs[n + 2 * ng:]
        x, y, c = _axes()
        chip = 2 * x + y
        sibling = (x, y, 1 - c)
        local, sends, fwd, from_sib = [], [], [], []
        a = 0
        for g in range(ng):
            for i in range(len(groups[g])):
                def copy(k, s, d, to, a=a):
                    return pltpu.make_async_remote_copy(src_ref=s, dst_ref=d, send_sem=ssem.at[7 * a + k],
                                                        recv_sem=rsem.at[7 * a + k], device_id=to,
                                                        device_id_type=MESH)
                loc = pltpu.make_async_copy(src[a].at[chip], mine[g].at[i, chip], lsem.at[a])
                loc.start()
                local.append(loc)
                own = copy(3, src[a].at[chip], sib[g].at[i, chip], sibling)
                own.start()
                sends.append(own)
                from_sib.append(copy(3, src[a].at[chip], sib[g].at[i, chip], sibling))
                for k in range(1, N_CHIP):
                    px, py = _flip(x, k >> 1), _flip(y, k & 1)
                    pchip = 2 * px + py
                    cp = copy(k - 1, src[a].at[pchip], mine[g].at[i, chip], (px, py, c))
                    cp.start()
                    sends.append(cp)
                    fwd.append((copy(k - 1, src[a].at[pchip], mine[g].at[i, pchip], (px, py, c)),
                                copy(3 + k, mine[g].at[i, pchip], sib[g].at[i, pchip], sibling)))
                    from_sib.append(copy(3 + k, mine[g].at[i, pchip], sib[g].at[i, pchip], sibling))
                a += 1
        for arrive, onward in fwd:
            arrive.wait_recv()
            onward.start()
        for cp in sends:
            cp.wait_send()
        for arrive, onward in fwd:
            onward.wait_send()
        for cp in from_sib:
            cp.wait_recv()
        for loc in local:
            loc.wait()

    outs = []
    for g in groups:
        outs.append(jax.ShapeDtypeStruct((len(g),) + g[0].shape, g[0].dtype))
    outs = outs + outs
    res = pl.pallas_call(
        body, name="scatter_grads", out_shape=outs, in_specs=[ANY] * n, out_specs=[ANY] * (2 * ng),
        scratch_shapes=[pltpu.SemaphoreType.DMA((7 * n,)), pltpu.SemaphoreType.DMA((7 * n,)),
                        pltpu.SemaphoreType.DMA((n,))],
        compiler_params=pltpu.CompilerParams(vmem_limit_bytes=VMEM_LIMIT),
    )(*flat)
    return list(zip(res[:ng], res[ng:]))


def mm_nn(a, w, out_dtype, name, res=None, gate=None):
    M, K = a.shape
    S, _, Ns = w.shape
    TM = _tile(M, (512, 256))
    TN = _tile(Ns, (1408, 1024, 768, 512, 256, 128))
    nj = Ns // TN
    fused = res is not None

    def body(*refs):
        if fused:
            a_ref, w_ref, r_ref, g_ref, f_ref, o_ref = refs
        else:
            a_ref, w_ref, f_ref = refs
        f = jnp.dot(a_ref[...], w_ref[...], preferred_element_type=F32)
        f_ref[...] = f.astype(f_ref.dtype)
        if fused:
            o_ref[...] = r_ref[...] + g_ref[...] * f

    col = lambda i, s, j: (i, s * nj + j)
    in_specs = [pl.BlockSpec((TM, K), lambda i, s, j: (i, 0)), pl.BlockSpec((None, K, TN), lambda i, s, j: (s, 0, j))]
    out_specs = [pl.BlockSpec((TM, TN), col)]
    out_shape = [jax.ShapeDtypeStruct((M, S * Ns), out_dtype)]
    args = [a, w]
    if fused:
        in_specs += [pl.BlockSpec((TM, TN), col), pl.BlockSpec((1, TN), lambda i, s, j: (0, s * nj + j))]
        out_specs.append(pl.BlockSpec((TM, TN), col))
        out_shape.append(jax.ShapeDtypeStruct((M, S * Ns), F32))
        args += [res, gate]
    out = pl.pallas_call(body, name=name, grid=(M // TM, S, nj), in_specs=in_specs, out_specs=out_specs,
                         out_shape=out_shape, compiler_params=_cp("parallel", "parallel", "parallel"))(*args)
    return tuple(out) if fused else out[0]


def mm_nt(g, w, out_dtype, name):
    M = g.shape[0]
    S, K, Ns = w.shape
    TM = _tile(M, (512, 256))
    TN = _tile(Ns, (1408, 1024, 768, 512, 256, 128))
    nj = Ns // TN
    nred = S * nj

    def body(g_ref, w_ref, o_ref, acc):
        n = pl.program_id(1)

        @pl.when(n == 0)
        def _():
            acc[...] = jnp.zeros_like(acc)

        acc[...] += lax.dot_general(g_ref[...], w_ref[...], (((1,), (1,)), ((), ())), preferred_element_type=F32)

        @pl.when(n == nred - 1)
        def _():
            o_ref[...] = acc[...].astype(o_ref.dtype)

    return pl.pallas_call(
        body, name=name, grid=(M // TM, nred),
        in_specs=[pl.BlockSpec((TM, TN), lambda i, n: (i, n)),
                  pl.BlockSpec((None, K, TN), lambda i, n: (n // nj, 0, n % nj))],
        out_specs=pl.BlockSpec((TM, K), lambda i, n: (i, 0)),
        out_shape=jax.ShapeDtypeStruct((M, K), out_dtype),
        scratch_shapes=[pltpu.VMEM((TM, K), F32)],
        compiler_params=_cp("parallel", "arbitrary"))(g, w)


def mm_tn(a, g, S, name):
    M, K = a.shape
    Ns = g.shape[1] // S
    TM = _tile(M, (512, 256))
    TK = _tile(K, (512, 256, 128))
    TN = _tile(Ns, (1408, 1024, 768, 512, 256, 128))
    nj = Ns // TN
    nm = M // TM

    def body(a_ref, g_ref, o_ref, acc):
        m = pl.program_id(2)

        @pl.when(m == 0)
        def _():
            acc[...] = jnp.zeros_like(acc)

        acc[...] += lax.dot_general(a_ref[...], g_ref[...], (((0,), (0,)), ((), ())), preferred_element_type=F32)

        @pl.when(m == nm - 1)
        def _():
            o_ref[...] = acc[...].astype(o_ref.dtype)

    return pl.pallas_call(
        body, name=name, grid=(K // TK, S * nj, nm),
        in_specs=[pl.BlockSpec((TM, TK), lambda k, n, m: (m, k)), pl.BlockSpec((TM, TN), lambda k, n, m: (m, n))],
        out_specs=pl.BlockSpec((None, TK, TN), lambda k, n, m: (n // nj, k, n % nj)),
        out_shape=jax.ShapeDtypeStruct((S, K, Ns), BF),
        scratch_shapes=[pltpu.VMEM((TK, TN), F32)],
        compiler_params=_cp("parallel", "parallel", "arbitrary"))(a, g)


def _rows(TL, D):
    return pl.BlockSpec((TL, D), lambda i: (i, 0))


def _fixed(R, D):
    return pl.BlockSpec((R, D), lambda i: (0, 0))


def _rowsum8(v):
    T, D = v.shape
    return jnp.sum(v.reshape(T // 8, 8, D), axis=0)


def _norm_parts(xv):
    r = lax.rsqrt(jnp.mean(xv * xv, axis=-1, keepdims=True) + RMS_EPS)
    return xv * r, r


def norm_mod(x, gamma, mods, k_shift, out_dtype, name):
    L, D = x.shape
    TL = _tile(L, (512, 256))

    def body(x_ref, g_ref, m_ref, o_ref):
        xn, _ = _norm_parts(x_ref[...])
        sh, sc = m_ref[k_shift:k_shift + 1, :], m_ref[k_shift + 1:k_shift + 2, :]
        o_ref[...] = ((xn * g_ref[...]) * (1.0 + sc) + sh).astype(o_ref.dtype)

    return pl.pallas_call(body, name=name, grid=(L // TL,),
                          in_specs=[_rows(TL, D), _fixed(1, D), _fixed(6, D)], out_specs=_rows(TL, D),
                          out_shape=jax.ShapeDtypeStruct((L, D), out_dtype), compiler_params=_cp("parallel"))(x, gamma, mods)


def norm_bwd(dh, x, dres, gamma, mods, k_shift, name):
    L, D = x.shape
    TL = _tile(L, (512, 256))

    def body(dh_ref, x_ref, dr_ref, g_ref, m_ref, dx_ref, s_ref, acc):
        i = pl.program_id(0)

        @pl.when(i == 0)
        def _():
            acc[...] = jnp.zeros_like(acc)

        xn, r = _norm_parts(x_ref[...])
        dh_v = dh_ref[...].astype(F32)
        gam = g_ref[...]
        sc = m_ref[k_shift + 1:k_shift + 2, :]
        dn = dh_v * (1.0 + sc)
        dxn = dn * gam
        dx_ref[...] = dr_ref[...] + r * (dxn - xn * jnp.mean(dxn * xn, axis=-1, keepdims=True))
        acc[0] += _rowsum8(dh_v)
        acc[1] += _rowsum8(dh_v * (xn * gam))
        acc[2] += _rowsum8(dn * xn)

        @pl.when(i == pl.num_programs(0) - 1)
        def _():
            s_ref[...] = jnp.zeros_like(s_ref)
            for q in range(3):
                s_ref[q:q + 1, :] = jnp.sum(acc[q], axis=0, keepdims=True)

    return pl.pallas_call(
        body, name=name, grid=(L // TL,),
        in_specs=[_rows(TL, D), _rows(TL, D), _rows(TL, D), _fixed(1, D), _fixed(6, D)],
        out_specs=[_rows(TL, D), _fixed(8, D)],
        out_shape=[jax.ShapeDtypeStruct((L, D), F32), jax.ShapeDtypeStruct((8, D), F32)],
        scratch_shapes=[pltpu.VMEM((3, 8, D), F32)], compiler_params=_cp("arbitrary"))(dh, x, dres, gamma, mods)


def gate_bwd(dx, f, mods, k_gate, name):
    L, D = dx.shape
    TL = _tile(L, (512, 256))

    def body(dx_ref, f_ref, m_ref, o_ref, s_ref, acc):
        i = pl.program_id(0)

        @pl.when(i == 0)
        def _():
            acc[...] = jnp.zeros_like(acc)

        dxv = dx_ref[...]
        o_ref[...] = (dxv * m_ref[k_gate:k_gate + 1, :]).astype(o_ref.dtype)
        acc[...] += _rowsum8(dxv * f_ref[...].astype(F32))

        @pl.when(i == pl.num_programs(0) - 1)
        def _():
            s_ref[...] = jnp.zeros_like(s_ref)
            s_ref[0:1, :] = jnp.sum(acc[...], axis=0, keepdims=True)

    return pl.pallas_call(
        body, name=name, grid=(L // TL,), in_specs=[_rows(TL, D), _rows(TL, D), _fixed(6, D)],
        out_specs=[_rows(TL, D), _fixed(8, D)],
        out_shape=[jax.ShapeDtypeStruct((L, D), BF), jax.ShapeDtypeStruct((8, D), F32)],
        scratch_shapes=[pltpu.VMEM((8, D), F32)], compiler_params=_cp("arbitrary"))(dx, f, mods)


def swiglu_act(gu, name):
    L, F2 = gu.shape
    F = F2 // 2
    TL = _tile(L, (256,))

    def body(gu_ref, o_ref):
        g = gu_ref[:, :F].astype(F32)
        u = gu_ref[:, F:].astype(F32)
        o_ref[...] = (g * jax.nn.sigmoid(g) * u).astype(o_ref.dtype)

    return pl.pallas_call(body, name=name, grid=(L // TL,), in_specs=[_rows(TL, F2)], out_specs=_rows(TL, F),
                          out_shape=jax.ShapeDtypeStruct((L, F), BF), compiler_params=_cp("parallel"))(gu)


def swiglu_bwd(da, gu, name):
    L, F2 = gu.shape
    F = F2 // 2
    TL = _tile(L, (256,))

    def body(da_ref, gu_ref, o_ref):
        g = gu_ref[:, :F].astype(F32)
        u = gu_ref[:, F:].astype(F32)
        d = da_ref[...].astype(F32)
        s = jax.nn.sigmoid(g)
        o_ref[:, :F] = (d * u * (s + g * s * (1.0 - s))).astype(o_ref.dtype)
        o_ref[:, F:] = (d * g * s).astype(o_ref.dtype)

    return pl.pallas_call(body, name=name, grid=(L // TL,), in_specs=[_rows(TL, F), _rows(TL, F2)],
                          out_specs=_rows(TL, F2), out_shape=jax.ShapeDtypeStruct((L, F2), BF),
                          compiler_params=_cp("parallel"))(da, gu)


def glu_res(o, x, mods, k_gate, name):
    L, D = x.shape
    TL = _tile(L, (512, 256))

    def body(o_ref, x_ref, m_ref, mix_ref, y_ref):
        mix = o_ref[:, :D].astype(F32) * jax.nn.sigmoid(o_ref[:, D:].astype(F32))
        mix_ref[...] = mix.astype(mix_ref.dtype)
        y_ref[...] = x_ref[...] + m_ref[k_gate:k_gate + 1, :] * mix

    return pl.pallas_call(
        body, name=name, grid=(L // TL,), in_specs=[_rows(TL, 2 * D), _rows(TL, D), _fixed(6, D)],
        out_specs=[_rows(TL, D), _rows(TL, D)],
        out_shape=[jax.ShapeDtypeStruct((L, D), BF), jax.ShapeDtypeStruct((L, D), F32)],
        compiler_params=_cp("parallel"))(o, x, mods)


def glu_bwd(dmix, o, name):
    L, D2 = o.shape
    D = D2 // 2
    TL = _tile(L, (512, 256))

    def body(d_ref, o_ref, do_ref):
        d = d_ref[...].astype(F32)
        val = o_ref[:, :D].astype(F32)
        s = jax.nn.sigmoid(o_ref[:, D:].astype(F32))
        do_ref[:, :D] = (d * s).astype(do_ref.dtype)
        do_ref[:, D:] = (d * val * s * (1.0 - s)).astype(do_ref.dtype)

    return pl.pallas_call(body, name=name, grid=(L // TL,), in_specs=[_rows(TL, D), _rows(TL, D2)],
                          out_specs=_rows(TL, D2), out_shape=jax.ShapeDtypeStruct((L, D2), BF),
                          compiler_params=_cp("parallel"))(dmix, o)


def final_loss(x, target, gamma, name):
    L, D = x.shape
    TL = _tile(L, (512, 256))

    def body(x_ref, t_ref, g_ref, l_ref, dx_ref, s_ref, acc, lacc):
        i = pl.program_id(0)

        @pl.when(i == 0)
        def _():
            acc[...] = jnp.zeros_like(acc)
            lacc[...] = jnp.zeros_like(lacc)

        xn, r = _norm_parts(x_ref[...])
        gam = g_ref[...]
        e = xn * gam - t_ref[...]
        lacc[...] += jnp.sum(0.5 * jnp.mean(e * e, axis=-1, keepdims=True), axis=0, keepdims=True)
        dy = e * (1.0 / D)
        dxn = dy * gam
        dx_ref[...] = r * (dxn - xn * jnp.mean(dxn * xn, axis=-1, keepdims=True))
        acc[...] += _rowsum8(dy * xn)

        @pl.when(i == pl.num_programs(0) - 1)
        def _():
            s_ref[...] = jnp.zeros_like(s_ref)
            s_ref[0:1, :] = jnp.sum(acc[...], axis=0, keepdims=True)
            l_ref[...] = jnp.broadcast_to(lacc[...], l_ref.shape)

    return pl.pallas_call(
        body, name=name, grid=(L // TL,), in_specs=[_rows(TL, D), _rows(TL, D), _fixed(1, D)],
        out_specs=[_fixed(8, 128), _rows(TL, D), _fixed(8, D)],
        out_shape=[jax.ShapeDtypeStruct((8, 128), F32), jax.ShapeDtypeStruct((L, D), F32),
                   jax.ShapeDtypeStruct((8, D), F32)],
        scratch_shapes=[pltpu.VMEM((8, D), F32), pltpu.VMEM((1, 1), F32)],
        compiler_params=_cp("arbitrary"))(x, target, gamma)


def _col(L, TC, off):
    return pl.BlockSpec((L, TC), lambda j: (0, off + j))


def _shift_down(v, k, row):
    return jnp.where(row >= k, pltpu.roll(v, k, 0), 0.0)


def _shift_up(v, k, row, L):
    return jnp.where(row < L - k, pltpu.roll(v, L - k, 0), 0.0)


def conv_fwd(p, w, name):
    L, D3 = p.shape
    D = D3 // 3
    TC = _tile(D, (128,))
    nc = D // TC

    def body(b_ref, c_ref, v_ref, w_ref, o_ref):
        row = lax.broadcasted_iota(jnp.int32, (L, TC), 0)
        cv = c_ref[...].astype(F32) * v_ref[...].astype(F32)
        conv = w_ref[2:3, :] * cv + w_ref[1:2, :] * _shift_down(cv, 1, row) + w_ref[0:1, :] * _shift_down(cv, 2, row)
        o_ref[...] = (b_ref[...].astype(F32) * conv).astype(o_ref.dtype)

    return pl.pallas_call(
        body, name=name, grid=(nc,),
        in_specs=[_col(L, TC, 0), _col(L, TC, nc), _col(L, TC, 2 * nc), pl.BlockSpec((3, TC), lambda j: (0, j))],
        out_specs=_col(L, TC, 0), out_shape=jax.ShapeDtypeStruct((L, D), BF), compiler_params=_cp("parallel"))(p, p, p, w)


def conv_bwd(dm, p, w, name):
    L, D3 = p.shape
    D = D3 // 3
    TC = _tile(D, (128,))
    nc = D // TC

    def body(dm_ref, b_ref, c_ref, v_ref, w_ref, db_ref, dc_ref, dv_ref, dw_ref):
        row = lax.broadcasted_iota(jnp.int32, (L, TC), 0)
        cg, vv = c_ref[...].astype(F32), v_ref[...].astype(F32)
        cv = cg * vv
        cv1, cv2 = _shift_down(cv, 1, row), _shift_down(cv, 2, row)
        conv = w_ref[2:3, :] * cv + w_ref[1:2, :] * cv1 + w_ref[0:1, :] * cv2
        dmv = dm_ref[...].astype(F32)
        db_ref[...] = (dmv * conv).astype(db_ref.dtype)
        dconv = dmv * b_ref[...].astype(F32)
        dcv = (w_ref[2:3, :] * dconv + w_ref[1:2, :] * _shift_up(dconv, 1, row, L)
               + w_ref[0:1, :] * _shift_up(dconv, 2, row, L))
        dc_ref[...] = (dcv * vv).astype(dc_ref.dtype)
        dv_ref[...] = (dcv * cg).astype(dv_ref.dtype)
        dw_ref[...] = jnp.zeros_like(dw_ref)
        dw_ref[0:1, :] = jnp.sum(dconv * cv2, axis=0, keepdims=True)
        dw_ref[1:2, :] = jnp.sum(dconv * cv1, axis=0, keepdims=True)
        dw_ref[2:3, :] = jnp.sum(dconv * cv, axis=0, keepdims=True)

    one = jax.ShapeDtypeStruct((L, D), BF)
    return pl.pallas_call(
        body, name=name, grid=(nc,),
        in_specs=[_col(L, TC, 0), _col(L, TC, 0), _col(L, TC, nc), _col(L, TC, 2 * nc),
                  pl.BlockSpec((3, TC), lambda j: (0, j))],
        out_specs=[_col(L, TC, 0), _col(L, TC, 0), _col(L, TC, 0), pl.BlockSpec((8, TC), lambda j: (0, j))],
        out_shape=[one, one, one, jax.ShapeDtypeStruct((8, D), F32)],
        compiler_params=_cp("parallel"))(dm, p, p, p, w)


def _gelu(y):
    return 0.5 * y * (1.0 + jnp.tanh(GELU_C * (y + GELU_A * y * y * y)))


def _gelu_grad(y):
    th = jnp.tanh(GELU_C * (y + GELU_A * y * y * y))
    return 0.5 * (1.0 + th) + 0.5 * y * (1.0 - th * th) * GELU_C * (1.0 + 3.0 * GELU_A * y * y)


def _cmul_add(br, bi, ar, ai, sr, si):
    return br + ar * sr - ai * si, bi + ar * si + ai * sr


def s5_fwd(h, bblk, cblk, pw, dvec, name):
    L, D = h.shape
    nkb, KB, W2 = bblk.shape
    W = W2 // 2
    TL = _tile(L, (512, 256))
    ngrp = TL // 8

    def body(h_ref, b_ref, c_ref, pw_ref, d_ref, s_ref, y_ref, z_ref, bu, carry):
        t = pl.program_id(1)

        @pl.when(t == 0)
        def _():
            carry[...] = jnp.zeros_like(carry)

        hv = h_ref[...]
        bu[...] = jnp.dot(hv.astype(BF), b_ref[...], preferred_element_type=F32)

        def grp(j, cr_ci):
            cr, ci = cr_ci
            r0 = pl.multiple_of(j * 8, 8)
            br, bi = bu[pl.ds(r0, 8), :W], bu[pl.ds(r0, 8), W:]
            for k, off in ((1, 0), (2, 8), (4, 16)):
                br, bi = _cmul_add(br, bi, pw_ref[off:off + 8, :W], pw_ref[off:off + 8, W:],
                                   pltpu.roll(br, k, 0), pltpu.roll(bi, k, 0))
            xr, xi = _cmul_add(br, bi, pw_ref[24:32, :W], pw_ref[24:32, W:], cr, ci)
            bu[pl.ds(r0, 8), :W] = xr
            bu[pl.ds(r0, 8), W:] = xi
            return jnp.broadcast_to(xr[7:8], (8, W)), jnp.broadcast_to(xi[7:8], (8, W))

        cr, ci = lax.fori_loop(0, ngrp, grp, (carry[0], carry[1]))
        carry[0] = cr
        carry[1] = ci
        sv = bu[...]
        s_ref[...] = sv
        y = jnp.dot(sv.astype(BF), c_ref[...], preferred_element_type=F32) + d_ref[...] * hv
        y_ref[...] = y
        z_ref[...] = _gelu(y).astype(z_ref.dtype)

    blk = lambda kb, t: (t, kb)
    return pl.pallas_call(
        body, name=name, grid=(nkb, L // TL),
        in_specs=[pl.BlockSpec((TL, KB), blk), pl.BlockSpec((None, KB, W2), lambda kb, t: (kb, 0, 0)),
                  pl.BlockSpec((None, W2, KB), lambda kb, t: (kb, 0, 0)),
                  pl.BlockSpec((None, 32, W2), lambda kb, t: (kb, 0, 0)), pl.BlockSpec((1, KB), lambda kb, t: (0, kb))],
        out_specs=[pl.BlockSpec((TL, W2), blk), pl.BlockSpec((TL, KB), blk), pl.BlockSpec((TL, KB), blk)],
        out_shape=[jax.ShapeDtypeStruct((L, nkb * W2), F32), jax.ShapeDtypeStruct((L, D), F32),
                   jax.ShapeDtypeStruct((L, D), BF)],
        scratch_shapes=[pltpu.VMEM((TL, W2), F32), pltpu.VMEM((2, 8, W), F32)],
        compiler_params=_cp("parallel", "arbitrary"))(h, bblk, cblk, pw, dvec)


def s5_bwd(dz, y, h, s, ct, bt, pwr, dvec, name):
    L, D = h.shape
    nkb, KB, W2 = ct.shape
    W = W2 // 2
    TL = _tile(L, (512, 256))
    ngrp = TL // 8
    nt = L // TL

    def body(dz_ref, y_ref, h_ref, s_ref, sp_ref, ct_ref, bt_ref, pw_ref, d_ref,
             dh_ref, dd_ref, da_ref, db_ref, dc_ref, g, carry):
        t = pl.program_id(1)

        @pl.when(t == 0)
        def _():
            carry[...] = jnp.zeros_like(carry)
            dd_ref[...] = jnp.zeros_like(dd_ref)
            da_ref[...] = jnp.zeros_like(da_ref)
            db_ref[...] = jnp.zeros_like(db_ref)
            dc_ref[...] = jnp.zeros_like(dc_ref)

        hv = h_ref[...]
        dy = dz_ref[...].astype(F32) * _gelu_grad(y_ref[...])
        dd_ref[...] += _rowsum8(dy * hv)
        dyb = dy.astype(BF)
        g[...] = jnp.dot(dyb, ct_ref[...], preferred_element_type=F32)

        def grp(jj, cr_ci):
            cr, ci = cr_ci
            r0 = pl.multiple_of((ngrp - 1 - jj) * 8, 8)
            gr, gi = g[pl.ds(r0, 8), :W], g[pl.ds(r0, 8), W:]
            for k, off in ((1, 0), (2, 8), (4, 16)):
                gr, gi = _cmul_add(gr, gi, pw_ref[off:off + 8, :W], pw_ref[off:off + 8, W:],
                                   pltpu.roll(gr, 8 - k, 0), pltpu.roll(gi, 8 - k, 0))
            gr, gi = _cmul_add(gr, gi, pw_ref[24:32, :W], pw_ref[24:32, W:], cr, ci)
            g[pl.ds(r0, 8), :W] = gr
            g[pl.ds(r0, 8), W:] = gi
            return jnp.broadcast_to(gr[0:1], (8, W)), jnp.broadcast_to(gi[0:1], (8, W))

        cr, ci = lax.fori_loop(0, ngrp, grp, (carry[0], carry[1]))
        carry[0] = cr
        carry[1] = ci

        first = lax.broadcasted_iota(jnp.int32, (8, W), 0) == 0
        live = jnp.where(t == nt - 1, 0.0, 1.0)

        def prev_rows(cur_r, cur_i, before_r, before_i):
            return (jnp.where(first, pltpu.roll(before_r, 1, 0), pltpu.roll(cur_r, 1, 0)),
                    jnp.where(first, pltpu.roll(before_i, 1, 0), pltpu.roll(cur_i, 1, 0)))

        def dab(j, acc):
            ar, ai = acc
            r0 = pl.multiple_of(j * 8, 8)
            rb = pl.multiple_of(j * 8 - 8, 8)
            pr, pi = prev_rows(s_ref[pl.ds(r0, 8), :W], s_ref[pl.ds(r0, 8), W:],
                               s_ref[pl.ds(rb, 8), :W], s_ref[pl.ds(rb, 8), W:])
            gr, gi = g[pl.ds(r0, 8), :W], g[pl.ds(r0, 8), W:]
            return ar + pr * gr + pi * gi, ai + pr * gi - pi * gr

        pr, pi = prev_rows(s_ref[0:8, :W], s_ref[0:8, W:], sp_ref[:, :W] * live, sp_ref[:, W:] * live)
        gr, gi = g[0:8, :W], g[0:8, W:]
        ar, ai = lax.fori_loop(1, ngrp, dab, (pr * gr + pi * gi, pr * gi - pi * gr))
        da_ref[:, :W] += ar
        da_ref[:, W:] += ai

        gb = g[...].astype(BF)
        dh_ref[...] = dy * d_ref[...] + jnp.dot(gb, bt_ref[...], preferred_element_type=F32)
        tn = (((0,), (0,)), ((), ()))
        db_ref[...] += lax.dot_general(hv.astype(BF), gb, tn, preferred_element_type=F32)
        dc_ref[...] += lax.dot_general(dyb, s_ref[...].astype(BF), tn, preferred_element_type=F32)

    rev = lambda kb, t: (nt - 1 - t, kb)
    grp8 = TL // 8
    prev = lambda kb, t: (jnp.maximum((nt - 1 - t) * grp8 - 1, 0), kb)
    per_kb = lambda kb, t: (kb, 0, 0)
    return pl.pallas_call(
        body, name=name, grid=(nkb, nt),
        in_specs=[pl.BlockSpec((TL, KB), rev), pl.BlockSpec((TL, KB), rev), pl.BlockSpec((TL, KB), rev),
                  pl.BlockSpec((TL, W2), rev), pl.BlockSpec((8, W2), prev),
                  pl.BlockSpec((None, KB, W2), per_kb), pl.BlockSpec((None, W2, KB), per_kb),
                  pl.BlockSpec((None, 32, W2), per_kb), pl.BlockSpec((1, KB), lambda kb, t: (0, kb))],
        out_specs=[pl.BlockSpec((TL, KB), rev), pl.BlockSpec((8, KB), lambda kb, t: (0, kb)),
                   pl.BlockSpec((None, 8, W2), per_kb), pl.BlockSpec((None, KB, W2), per_kb),
                   pl.BlockSpec((None, KB, W2), per_kb)],
        out_shape=[jax.ShapeDtypeStruct((L, D), F32), jax.ShapeDtypeStruct((8, D), F32),
                   jax.ShapeDtypeStruct((nkb, 8, W2), F32), jax.ShapeDtypeStruct((nkb, KB, W2), F32),
                   jax.ShapeDtypeStruct((nkb, KB, W2), F32)],
        scratch_shapes=[pltpu.VMEM((TL, W2), F32), pltpu.VMEM((2, 8, W), F32)],
        compiler_params=_cp("parallel", "arbitrary"))(dz, y, h, s, s, ct, bt, pwr, dvec)


def _discretise(a_re, a_im, log_step, b_re, b_im):
    lr = jnp.minimum(a_re, -1e-4)
    li = a_im
    dt = jnp.exp(log_step)[:, None]
    mag = jnp.exp(lr * dt)
    abr = mag * jnp.cos(li * dt)
    abi = mag * jnp.sin(li * dt)
    den = lr * lr + li * li
    qr = ((abr - 1.0) * lr + abi * li) / den
    qi = (abi * lr - (abr - 1.0) * li) / den
    bbar_re = qr[..., None] * b_re - qi[..., None] * b_im
    bbar_im = qr[..., None] * b_im + qi[..., None] * b_re
    return abr, abi, bbar_re, bbar_im


def _block_diag(m_re, m_im, nkb):
    G, H, P = m_re.shape
    GL = G // nkb
    eye = jnp.eye(GL, dtype=m_re.dtype)[None, :, None, None, :, None]
    m = jnp.stack([m_re, m_im], axis=2).reshape(nkb, GL, H, 2, 1, P)
    return (m * eye).reshape(nkb, GL * H, 2 * GL * P)


def _block_diag_extract(blk, G):
    nkb, R, C = blk.shape
    GL = G // nkb
    H, P = R // GL, C // (2 * GL)
    eye = jnp.eye(GL, dtype=blk.dtype)[None, :, None, None, :, None]
    m = jnp.sum(blk.reshape(nkb, GL, H, 2, GL, P) * eye, axis=4)
    return m[:, :, :, 0].reshape(G, H, P), m[:, :, :, 1].reshape(G, H, P)


def _scan_powers(a_re, a_im, log_step, nkb, conj):
    G, P = a_re.shape
    lr = jnp.minimum(a_re, -1e-4)
    dt = jnp.exp(log_step)[:, None]
    n = jnp.arange(1, 9, dtype=F32)[:, None, None]
    mag = jnp.exp(n * (lr * dt)[None])
    ang = n * (a_im * dt)[None]
    pr, pi = mag * jnp.cos(ang), mag * jnp.sin(ang)
    if conj:
        pi = -pi
    row = jnp.arange(8)[:, None, None]

    def table(q):
        out = []
        for k in (1, 2, 4):
            keep = (row <= 7 - k) if conj else (row >= k)
            out.append(jnp.where(keep, q[k - 1][None], 0.0))
        out.append(q[::-1] if conj else q)
        return jnp.concatenate(out, axis=0)

    GL = G // nkb
    t = jnp.stack([table(pr), table(pi)], axis=1)
    t = t.reshape(32, 2, nkb, GL * P).transpose(2, 0, 1, 3)
    return t.reshape(nkb, 32, 2 * GL * P)


def ada_mods(c_all, w_ada, b_sh, name):
    nl, D, NA = w_ada.shape

    def body(c_ref, w_ref, b_ref, o_ref):
        cv = c_ref[...]
        act = cv * jax.nn.sigmoid(cv)
        o_ref[...] = jnp.dot(act, w_ref[...], preferred_element_type=F32, precision=lax.Precision.HIGHEST) + b_ref[...]

    return pl.pallas_call(
        body, name=name, grid=(nl,),
        in_specs=[pl.BlockSpec((8, D), lambda i: (0, 0)), pl.BlockSpec((None, D, NA), lambda i: (i, 0, 0)),
                  pl.BlockSpec((None, 1, NA), lambda i: (i, 0, 0))],
        out_specs=pl.BlockSpec((None, 8, NA), lambda i: (i, 0, 0)),
        out_shape=jax.ShapeDtypeStruct((nl, 8, NA), F32), compiler_params=_cp("parallel"))(c_all, w_ada, b_sh)


def _adamw(w, g, m, v):
    m = ADAM_B1 * m + (1.0 - ADAM_B1) * g
    v = ADAM_B2 * v + (1.0 - ADAM_B2) * (g * g)
    m_hat = m / (1.0 - ADAM_B1 ** ADAM_STEP)
    v_hat = v / (1.0 - ADAM_B2 ** ADAM_STEP)
    return -ADAM_LR * (m_hat / (jnp.sqrt(v_hat) + ADAM_EPS) + ADAM_WD * w), m, v


def _adam_rows(R, C):
    cap = max(8, (256 * 1024) // C)
    for t in range(min(R, cap), 0, -1):
        if R % t == 0 and (t % 8 == 0 or t == R):
            return t
    return R


def adamw_ada(c_t, dm, w, m, v, name):
    nl, D, NA = w.shape
    TK = _tile(D, (128,))

    def body(c_ref, dm_ref, w_ref, m_ref, v_ref, g_ref, d_ref, nm_ref, nv_ref):
        cv = c_ref[...]
        act = cv * jax.nn.sigmoid(cv)
        g = act[:, 0:1] * dm_ref[0:1, :]
        for b in range(1, 8):
            g = g + act[:, b:b + 1] * dm_ref[b:b + 1, :]
        g_ref[...] = g
        d_ref[...], nm_ref[...], nv_ref[...] = _adamw(w_ref[...], g, m_ref[...], v_ref[...])

    big = pl.BlockSpec((None, TK, NA), lambda i, k: (i, k, 0))
    shape = jax.ShapeDtypeStruct(w.shape, F32)
    return pl.pallas_call(
        body, name=name, grid=(nl, D // TK),
        in_specs=[pl.BlockSpec((TK, 8), lambda i, k: (k, 0)), pl.BlockSpec((None, 8, NA), lambda i, k: (i, 0, 0)),
                  big, big, big],
        out_specs=[big] * 4, out_shape=[shape] * 4, compiler_params=_cp("parallel", "parallel"))(c_t, dm, w, m, v)


def adamw_sharded(w, m, v, mine, sib, name):
    nl, R, C = w.shape
    TR = _adam_rows(R, C)

    def body(w_ref, m_ref, v_ref, a_ref, b_ref, g_ref, d_ref, nm_ref, nv_ref):
        def four(r):
            return ((r[0].astype(F32) + r[1].astype(F32)) + r[2].astype(F32)) + r[3].astype(F32)
        g = four(a_ref) + four(b_ref)
        g_ref[...] = g
        d_ref[...], nm_ref[...], nv_ref[...] = _adamw(w_ref[...], g, m_ref[...], v_ref[...])

    big = pl.BlockSpec((None, TR, C), lambda i, r: (i, r, 0))
    part = pl.BlockSpec((None, N_CHIP, TR, C), lambda i, r: (i, 0, r, 0))
    shape = jax.ShapeDtypeStruct(w.shape, F32)
    return pl.pallas_call(
        body, name=name, grid=(nl, R // TR), in_specs=[big, big, big, part, part],
        out_specs=[big] * 4, out_shape=[shape] * 4, compiler_params=_cp("parallel", "parallel"))(w, m, v, mine, sib)


def adamw_slab(parts, w, m, v, name):
    _, R, C = parts.shape
    TR = _tile(R, (40, 8))

    def body(p_ref, w_ref, m_ref, v_ref, g_ref, d_ref, nm_ref, nv_ref):
        g = p_ref[0]
        for d in range(1, N_DEV):
            g = g + p_ref[d]
        g_ref[...] = g
        d_ref[...], nm_ref[...], nv_ref[...] = _adamw(w_ref[...], g, m_ref[...], v_ref[...])

    big = pl.BlockSpec((TR, C), lambda r: (r, 0))
    shape = jax.ShapeDtypeStruct((R, C), F32)
    return pl.pallas_call(
        body, name=name, grid=(R // TR,), in_specs=[pl.BlockSpec((N_DEV, TR, C), lambda r: (0, r, 0)), big, big, big],
        out_specs=[big] * 4, out_shape=[shape] * 4, compiler_params=_cp("parallel"))(parts, w, m, v)


def adamw_plain(w, m, v, g, name):
    def body(w_ref, m_ref, v_ref, g_ref, d_ref, nm_ref, nv_ref):
        d_ref[...], nm_ref[...], nv_ref[...] = _adamw(w_ref[...], g_ref[...], m_ref[...], v_ref[...])

    shape = jax.ShapeDtypeStruct(w.shape, F32)
    return pl.pallas_call(body, name=name, out_shape=[shape] * 3,
                          compiler_params=pltpu.CompilerParams(vmem_limit_bytes=VMEM_LIMIT))(w, m, v, g)


def _slab_rows(a):
    n = a.size
    rows = -(-n // SLAB_W)
    return -(-rows // 8) * 8


def _pack(arrs):
    out = []
    for a in arrs:
        rows = _slab_rows(a)
        flat = a.reshape(-1).astype(F32)
        flat = jnp.pad(flat, (0, rows * SLAB_W - flat.shape[0]))
        out.append(flat.reshape(rows, SLAB_W))
    return jnp.concatenate(out, axis=0)


def _unpack(slab, like):
    out, r = [], 0
    for a in like:
        rows = _slab_rows(a)
        out.append(slab[r:r + rows].reshape(-1)[:a.size].reshape(a.shape))
        r += rows
    return out


WEIGHTS = ['norm1_g', 'norm2_g', 'w_ada', 'b_ada', 'ssm_a_re', 'ssm_a_im', 'ssm_log_step', 'ssm_b_re', 'ssm_b_im',
           'ssm_c_re', 'ssm_c_im', 'ssm_d', 'ssm_w_out', 'conv_w_in', 'conv_w', 'conv_w_out', 'w_ffn_in',
           'w_ffn_out', 'final_g']
SLAB = ['norm1_g', 'norm2_g', 'b_ada', 'ssm_a_re', 'ssm_a_im', 'ssm_log_step', 'ssm_b_re', 'ssm_b_im', 'ssm_c_re',
        'ssm_c_im', 'ssm_d', 'final_g']
SHARDED = ['ssm_w_out', 'conv_w_in', 'conv_w_out', 'w_ffn_in', 'w_ffn_out']


def kernel(x, c, norm1_g, norm2_g, w_ada, b_ada, ssm_a_re, ssm_a_im, ssm_log_step, ssm_b_re, ssm_b_im, ssm_c_re, ssm_c_im, ssm_d, ssm_w_out, conv_w_in, conv_w, conv_w_out, w_ffn_in, w_ffn_out, final_g, loss_target, m_norm1_g, m_norm2_g, m_w_ada, m_b_ada, m_ssm_a_re, m_ssm_a_im, m_ssm_log_step, m_ssm_b_re, m_ssm_b_im, m_ssm_c_re, m_ssm_c_im, m_ssm_d, m_ssm_w_out, m_conv_w_in, m_conv_w, m_conv_w_out, m_w_ffn_in, m_w_ffn_out, m_final_g, v_norm1_g, v_norm2_g, v_w_ada, v_b_ada, v_ssm_a_re, v_ssm_a_im, v_ssm_log_step, v_ssm_b_re, v_ssm_b_im, v_ssm_c_re, v_ssm_c_im, v_ssm_d, v_ssm_w_out, v_conv_w_in, v_conv_w, v_conv_w_out, v_w_ffn_in, v_w_ffn_out, v_final_g):
    given = dict(locals())
    W = {n: given[n] for n in WEIGHTS}
    Mo = {n: given["m_" + n] for n in WEIGHTS}
    Vo = {n: given["v_" + n] for n in WEIGHTS}

    xs = x[0]
    tgt = loss_target[0]
    L, D = xs.shape
    nlayer = norm1_g.shape[0]
    NA = w_ada.shape[2]
    G = ssm_a_re.shape[1]
    nkb = D // S5_BLOCK
    ax, ay, ac = _axes()
    me = 4 * ax + 2 * ay + ac
    chip = 2 * ax + ay

    c_all = gather8(jnp.broadcast_to(c, (8, D)), "gather_c")[:, 0, :]
    b_sh = lax.dynamic_slice_in_dim(b_ada, chip * NA, NA, axis=1)[:, None, :]
    mods_part = ada_mods(c_all, w_ada, b_sh, "ada_mods")
    mg = gather8(mods_part.reshape(nlayer * 8, NA), "gather_mods")
    mg = mg.reshape(N_CHIP, 2, nlayer, 8, NA)[:, 0]
    mods_all = lax.dynamic_index_in_dim(mg, me, axis=2, keepdims=False)
    mods_all = jnp.transpose(mods_all, (1, 0, 2)).reshape(nlayer, 6, D)

    full = dict(zip(SHARDED, gather_weights([W[n].astype(BF) for n in SHARDED])))
    cw_parts = gather8(_pack([conv_w]), "gather_conv_w")
    nconv = conv_w.shape[0]
    cw_full = jnp.stack([_unpack(cw_parts[2 * q], [conv_w])[0] for q in range(N_CHIP)], axis=2)
    cw_full = cw_full.reshape(nconv, 3, D)

    s5 = []
    for j in range(ssm_a_re.shape[0]):
        disc, disc_vjp = jax.vjp(_discretise, ssm_a_re[j], ssm_a_im[j], ssm_log_step[j], ssm_b_re[j], ssm_b_im[j])
        _, _, bbar_re, bbar_im = disc
        bblk = _block_diag(jnp.swapaxes(bbar_re, 1, 2), jnp.swapaxes(bbar_im, 1, 2), nkb)
        ctb = _block_diag(ssm_c_re[j], -ssm_c_im[j], nkb)
        s5.append(dict(
            vjp=disc_vjp, bblk=bblk.astype(BF), bt=jnp.swapaxes(bblk, 1, 2).astype(BF),
            ct=ctb.astype(BF), cblk=jnp.swapaxes(ctb, 1, 2).astype(BF),
            pw=_scan_powers(ssm_a_re[j], ssm_a_im[j], ssm_log_step[j], nkb, False),
            pwr=_scan_powers(ssm_a_re[j], ssm_a_im[j], ssm_log_step[j], nkb, True)))

    saved = []
    xcur = xs
    for i in range(nlayer):
        j = i // 2
        mods = mods_all[i]
        sv = dict(x=xcur)
        if i % 2 == 0:
            h = norm_mod(xcur, norm1_g[i:i + 1], mods, 0, F32, "norm_mod_s5")
            states, yv, z = s5_fwd(h, s5[j]["bblk"], s5[j]["cblk"], s5[j]["pw"], ssm_d[j:j + 1], "s5_fwd")
            o = mm_nn(z, full["ssm_w_out"][j], BF, "mm_ssm_out")
            mix, x2 = glu_res(o, xcur, mods, 2, "glu_res")
            sv.update(h=h, states=states, y=yv, z=z, o=o)
        else:
            h = norm_mod(xcur, norm1_g[i:i + 1], mods, 0, BF, "norm_mod")
            p = mm_nn(h, full["conv_w_in"][j], BF, "mm_conv_in")
            mc = conv_fwd(p, cw_full[j], "conv_fwd")
            mix, x2 = mm_nn(mc, full["conv_w_out"][j].reshape(1, D, D), BF, "mm_conv_out", res=xcur, gate=mods[2:3])
            sv.update(h=h, p=p, mc=mc)
        h2 = norm_mod(x2, norm2_g[i:i + 1], mods, 3, BF, "norm_mod")
        gu = mm_nn(h2, full["w_ffn_in"][i], BF, "mm_ffn_in")
        act = swiglu_act(gu, "swiglu_act")
        F = act.shape[1]
        ff, x3 = mm_nn(act, full["w_ffn_out"][i].reshape(1, F, D), BF, "mm_ffn_out", res=x2, gate=mods[5:6])
        sv.update(mix=mix, x2=x2, h2=h2, gu=gu, act=act, ff=ff)
        saved.append(sv)
        xcur = x3

    loss_blk, dx, dfinal = final_loss(xcur, tgt, final_g[None, :], "final_loss")

    gfull = {n: [None] * W[n].shape[0] for n in SHARDED}
    dmods = [None] * nlayer
    dnorm1, dnorm2 = [None] * nlayer, [None] * nlayer
    dconv_w = [None] * nconv
    ds5 = [None] * ssm_a_re.shape[0]
    for i in reversed(range(nlayer)):
        j = i // 2
        mods = mods_all[i]
        sv = saved[i]
        F = sv["act"].shape[1]
        dff, dg2 = gate_bwd(dx, sv["ff"], mods, 5, "gate_bwd")
        gfull["w_ffn_out"][i] = mm_tn(sv["act"], dff, 1, "mm_tn_ffn_out").reshape(N_CHIP, F // N_CHIP, D)
        dact = mm_nt(dff, full["w_ffn_out"][i].reshape(1, F, D), BF, "mm_nt_ffn_out")
        dgu = swiglu_bwd(dact, sv["gu"], "swiglu_bwd")
        gfull["w_ffn_in"][i] = mm_tn(sv["h2"], dgu, N_CHIP, "mm_tn_ffn_in")
        dh2 = mm_nt(dgu, full["w_ffn_in"][i], F32, "mm_nt_ffn_in")
        dx2, s2 = norm_bwd(dh2, sv["x2"], dx, norm2_g[i:i + 1], mods, 3, "norm_bwd")
        dmix, dg1 = gate_bwd(dx2, sv["mix"], mods, 2, "gate_bwd")
        if i % 2 == 0:
            do = glu_bwd(dmix, sv["o"], "glu_bwd")
            gfull["ssm_w_out"][j] = mm_tn(sv["z"], do, N_CHIP, "mm_tn_ssm_out")
            dz = mm_nt(do, full["ssm_w_out"][j], F32, "mm_nt_ssm_out")
            dh, dd, dab, db, dc = s5_bwd(dz, sv["y"], sv["h"], sv["states"], s5[j]["ct"], s5[j]["bt"], s5[j]["pwr"],
                                         ssm_d[j:j + 1], "s5_bwd")
            ds5[j] = (dd, dab, db, dc)
        else:
            gfull["conv_w_out"][j] = mm_tn(sv["mc"], dmix, 1, "mm_tn_conv_out").reshape(N_CHIP, D // N_CHIP, D)
            dmc = mm_nt(dmix, full["conv_w_out"][j].reshape(1, D, D), BF, "mm_nt_conv_out")
            dbg, dcg, dvv, dcw = conv_bwd(dmc, sv["p"], cw_full[j], "conv_bwd")
            dp = jnp.concatenate([dbg, dcg, dvv], axis=1)
            gfull["conv_w_in"][j] = mm_tn(sv["h"], dp, N_CHIP, "mm_tn_conv_in")
            dh = mm_nt(dp, full["conv_w_in"][j], F32, "mm_nt_conv_in")
            dconv_w[j] = dcw[0:3]
        dx, s1 = norm_bwd(dh, sv["x"], dx2, norm1_g[i:i + 1], mods, 0, "norm_bwd")
        dmods[i] = jnp.concatenate([s1[0:2], dg1[0:1], s2[0:2], dg2[0:1]], axis=0).reshape(6 * D)
        dnorm1[i], dnorm2[i] = s1[2], s2[2]

    small = dict(norm1_g=jnp.stack(dnorm1), norm2_g=jnp.stack(dnorm2), b_ada=jnp.stack(dmods), final_g=dfinal[0])
    per = {n: [] for n in ('ssm_a_re', 'ssm_a_im', 'ssm_log_step', 'ssm_b_re', 'ssm_b_im', 'ssm_c_re', 'ssm_c_im', 'ssm_d')}
    GL = G // nkb
    for j, (dd, dab, db, dc) in enumerate(ds5):
        dab = jnp.sum(dab, axis=1).reshape(nkb, 2, GL, SSM_STATE)
        g_abr, g_abi = dab[:, 0].reshape(G, SSM_STATE), dab[:, 1].reshape(G, SSM_STATE)
        gb_re, gb_im = _block_diag_extract(db, G)
        gc_re, gc_im = _block_diag_extract(dc, G)
        ga_re, ga_im, gls, gbr, gbi = s5[j]["vjp"]((g_abr, g_abi, jnp.swapaxes(gb_re, 1, 2), jnp.swapaxes(gb_im, 1, 2)))
        for n, val in zip(per, (ga_re, ga_im, gls, gbr, gbi, gc_re, -gc_im, jnp.sum(dd, axis=0))):
            per[n].append(val)
    small.update({n: jnp.stack(vals) for n, vals in per.items()})
    dcw_full = jnp.stack(dconv_w)

    slab_like = [W[n] for n in SLAB] + [dcw_full]
    parts = gather8(_pack([small[n] for n in SLAB] + [dcw_full]), "gather_small")
    g_slab, d_slab, m_slab, v_slab = adamw_slab(
        parts, _pack([W[n] for n in SLAB] + [jnp.zeros_like(dcw_full)]),
        _pack([Mo[n] for n in SLAB] + [jnp.zeros_like(dcw_full)]),
        _pack([Vo[n] for n in SLAB] + [jnp.ones_like(dcw_full)]), "adamw_slab")
    out = {}
    for k, slab in zip(("g", "d", "m", "v"), (g_slab, d_slab, m_slab, v_slab)):
        for n, val in zip(SLAB, _unpack(slab, slab_like)):
            out[k, n] = val
    g_cw = lax.dynamic_slice_in_dim(_unpack(g_slab, slab_like)[-1], chip * conv_w.shape[2], conv_w.shape[2], axis=2)
    out["g", "conv_w"] = g_cw
    out["d", "conv_w"], out["m", "conv_w"], out["v", "conv_w"] = [
        r.reshape(conv_w.shape) for r in adamw_plain(conv_w.reshape(-1, conv_w.shape[2]), m_conv_w.reshape(-1, conv_w.shape[2]),
                                                     v_conv_w.reshape(-1, conv_w.shape[2]), g_cw.reshape(-1, conv_w.shape[2]),
                                                     "adamw_conv_w")]

    r0 = sum(_slab_rows(W[n]) for n in SLAB[:2])
    dm_all = parts[:, r0:r0 + _slab_rows(b_ada)].reshape(N_DEV, -1)[:, :b_ada.size].reshape(N_DEV, nlayer, N_CHIP, NA)
    dm_sh = jnp.transpose(lax.dynamic_index_in_dim(dm_all, chip, axis=2, keepdims=False), (1, 0, 2))
    res = adamw_ada(jnp.transpose(c_all), dm_sh, w_ada, m_w_ada, v_w_ada, "adamw_ada")
    out["g", "w_ada"], out["d", "w_ada"], out["m", "w_ada"], out["v", "w_ada"] = res

    pairs = scatter_grads([gfull[n] for n in SHARDED])
    for n, (mine, sib) in zip(SHARDED, pairs):
        w = W[n]
        shp = w.shape
        w3 = lambda a: a.reshape(shp[0], shp[1], shp[2])
        r = adamw_sharded(w3(w), w3(Mo[n]), w3(Vo[n]), mine, sib, "adamw_" + n)
        out["g", n], out["d", n], out["m", n], out["v", n] = r

    loss = lax.psum(loss_blk[0, 0], ("x", "y", "c"))
    return (loss, dx[None], *[out["g", n] for n in WEIGHTS], *[out["d", n] for n in WEIGHTS],
            *[out["m", n] for n in WEIGHTS], *[out["v", n] for n in WEIGHTS])
```

```python
import functools
import math

import jax
import jax.numpy as jnp
from jax import lax
from jax.experimental import pallas as pl
from jax.experimental.pallas import tpu as pltpu

F32 = jnp.float32
BF = jnp.bfloat16
MESH = pl.DeviceIdType.MESH
ANY = pl.BlockSpec(memory_space=pl.ANY)

N_DEV = 8
N_CHIP = 4
DEPTH = 4
SSM_GROUP = 16
SSM_STATE = 64
S5_BLOCK = 256
RMS_EPS = 1e-6
ADAM_LR, ADAM_B1, ADAM_B2, ADAM_EPS, ADAM_WD, ADAM_STEP = 0.001, 0.9, 0.999, 1e-08, 0.01, 10
V7X_VMEM_BYTES = 64 * 1024 * 1024
VMEM_LIMIT = V7X_VMEM_BYTES - 12 * 1024 * 1024
SLAB_W = 1024
GELU_C = math.sqrt(2.0 / math.pi)
GELU_A = 0.044715


def _cp(*sem):
    return pltpu.CompilerParams(dimension_semantics=sem if sem else None, vmem_limit_bytes=VMEM_LIMIT)


def _tile(n, prefs):
    for p in prefs:
        if p <= n and n % p == 0:
            return p
    return n


def _axes():
    return lax.axis_index("x"), lax.axis_index("y"), lax.axis_index("c")


def _flip(v, k):
    return 1 - v if k else v


def gather8(v, name):
    R, C = v.shape

    def body(v_ref, o_ref, ssem, rsem, lsem):
        x, y, c = _axes()
        me = 4 * x + 2 * y + c
        loc = pltpu.make_async_copy(v_ref, o_ref.at[me], lsem)
        loc.start()
        copies = []
        for k in range(1, N_DEV):
            peer = (_flip(x, (k >> 2) & 1), _flip(y, (k >> 1) & 1), _flip(c, k & 1))
            cp = pltpu.make_async_remote_copy(src_ref=v_ref, dst_ref=o_ref.at[me], send_sem=ssem.at[k - 1],
                                              recv_sem=rsem.at[k - 1], device_id=peer, device_id_type=MESH)
            cp.start()
            copies.append(cp)
        for cp in copies:
            cp.wait()
        loc.wait()

    return pl.pallas_call(
        body, name=name,
        out_shape=jax.ShapeDtypeStruct((N_DEV, R, C), v.dtype),
        in_specs=[pl.BlockSpec(memory_space=pltpu.VMEM)],
        out_specs=pl.BlockSpec(memory_space=pltpu.VMEM),
        scratch_shapes=[pltpu.SemaphoreType.DMA((N_DEV - 1,)), pltpu.SemaphoreType.DMA((N_DEV - 1,)),
                        pltpu.SemaphoreType.DMA],
        compiler_params=pltpu.CompilerParams(vmem_limit_bytes=VMEM_LIMIT),
    )(v)


HBM = pl.BlockSpec(memory_space=pltpu.HBM)
SEM = pl.BlockSpec(memory_space=pltpu.SEMAPHORE)
EFFECT = pltpu.SideEffectType.DATAFLOW_SIDE_EFFECTING


def _in_hbm(a):
    return pltpu.with_memory_space_constraint(a, pltpu.HBM)


def _chip_peers(x, y, c):
    out = []
    for k in range(1, N_CHIP):
        px, py = _flip(x, k >> 1), _flip(y, k & 1)
        out.append(((px, py, c), 2 * px + py))
    return out


def gather_start(shards, name):
    n = len(shards)

    def body(*refs):
        src, land = refs[:n], refs[n:2 * n]
        ssem, rsem, lsem = refs[2 * n:2 * n + 3]
        token = refs[-1]
        x, y, c = _axes()
        chip = 2 * x + y
        for a in range(n):
            pltpu.make_async_copy(src[a], land[a].at[chip], lsem.at[a]).start()
            for k, (peer, _) in enumerate(_chip_peers(x, y, c)):
                pltpu.make_async_remote_copy(src_ref=src[a], dst_ref=land[a].at[chip], send_sem=ssem.at[3 * a + k],
                                             recv_sem=rsem.at[3 * a + k], device_id=peer, device_id_type=MESH).start()
        token[...] = jnp.zeros_like(token)

    lands = [lax.empty((N_CHIP,) + s.shape, s.dtype) for s in shards]
    out_shape = ([pltpu.SemaphoreType.DMA((3 * n,)), pltpu.SemaphoreType.DMA((3 * n,)), pltpu.SemaphoreType.DMA((n,))]
                 + [pltpu.HBM(s.shape, s.dtype) for s in shards] + [pltpu.HBM(l.shape, l.dtype) for l in lands]
                 + [jax.ShapeDtypeStruct((8, 128), F32)])
    res = pl.pallas_call(
        body, name=name, out_shape=out_shape, in_specs=[HBM] * (2 * n),
        out_specs=[SEM, SEM, SEM] + [HBM] * (2 * n) + [pl.BlockSpec(memory_space=pltpu.VMEM)],
        input_output_aliases={a: 3 + a for a in range(2 * n)},
        compiler_params=pltpu.CompilerParams(has_side_effects=EFFECT),
    )(*[_in_hbm(s) for s in shards], *[_in_hbm(l) for l in lands])
    return tuple(res[:3]), list(res[3:3 + n]), list(res[3 + n:3 + 2 * n]), res[-1]


def gather_wait(sems, srcs, lands, idx, after, name):
    m = len(idx)

    def body(*refs):
        src, land = refs[:m], refs[m:2 * m]
        ssem, rsem, lsem = refs[2 * m:2 * m + 3]
        x, y, c = _axes()
        chip = 2 * x + y
        for j, a in enumerate(idx):
            for k, (peer, pchip) in enumerate(_chip_peers(x, y, c)):
                cp = pltpu.make_async_remote_copy(src_ref=src[j], dst_ref=land[j].at[pchip], send_sem=ssem.at[3 * a + k],
                                                  recv_sem=rsem.at[3 * a + k], device_id=peer, device_id_type=MESH)
                cp.wait_send()
                cp.wait_recv()
            pltpu.make_async_copy(src[j], land[j].at[chip], lsem.at[a]).wait()

    s_in = [srcs[a] for a in idx]
    l_in = [lands[a] for a in idx]
    res = pl.pallas_call(
        body, name=name,
        out_shape=[pltpu.HBM(s.shape, s.dtype) for s in s_in] + [pltpu.HBM(l.shape, l.dtype) for l in l_in],
        in_specs=[HBM] * (2 * m) + [SEM, SEM, SEM, ANY], out_specs=[HBM] * (2 * m),
        input_output_aliases={a: a for a in range(2 * m)},
        compiler_params=pltpu.CompilerParams(has_side_effects=EFFECT),
    )(*s_in, *l_in, *sems, after)
    return list(res[m:])


def scatter_start(grads, lands, slot, name):
    n = len(grads)

    def body(*refs):
        src, land = refs[:n], refs[n:2 * n]
        ssem, rsem, lsem = refs[2 * n:2 * n + 3]
        x, y, c = _axes()
        chip = 2 * x + y
        for a in range(n):
            pltpu.make_async_copy(src[a].at[chip], land[a].at[slot[a], chip], lsem.at[a]).start()
            for k, (peer, pchip) in enumerate(_chip_peers(x, y, c)):
                pltpu.make_async_remote_copy(src_ref=src[a].at[pchip], dst_ref=land[a].at[slot[a], chip],
                                             send_sem=ssem.at[3 * a + k], recv_sem=rsem.at[3 * a + k],
                                             device_id=peer, device_id_type=MESH).start()

    out_shape = ([pltpu.SemaphoreType.DMA((3 * n,)), pltpu.SemaphoreType.DMA((3 * n,)), pltpu.SemaphoreType.DMA((n,))]
                 + [pltpu.HBM(g.shape, g.dtype) for g in grads] + [pltpu.HBM(l.shape, l.dtype) for l in lands])
    res = pl.pallas_call(
        body, name=name, out_shape=out_shape, in_specs=[HBM] * (2 * n), out_specs=[SEM, SEM, SEM] + [HBM] * (2 * n),
        input_output_aliases={a: 3 + a for a in range(2 * n)},
        compiler_params=pltpu.CompilerParams(has_side_effects=EFFECT),
    )(*[_in_hbm(g) for g in grads], *[_in_hbm(l) for l in lands])
    return tuple(res[:3]), list(res[3:3 + n]), list(res[3 + n:])


def scatter_wait(sems, grads, lands, slot, name):
    n = len(grads)

    def body(*refs):
        src, land = refs[:n], refs[n:2 * n]
        ssem, rsem, lsem = refs[2 * n:2 * n + 3]
        x, y, c = _axes()
        chip = 2 * x + y
        for a in range(n):
            for k, (peer, pchip) in enumerate(_chip_peers(x, y, c)):
                cp = pltpu.make_async_remote_copy(src_ref=src[a].at[pchip], dst_ref=land[a].at[slot[a], pchip],
                                                  send_sem=ssem.at[3 * a + k], recv_sem=rsem.at[3 * a + k],
                                                  device_id=peer, device_id_type=MESH)
                cp.wait_send()
                cp.wait_recv()
            pltpu.make_async_copy(src[a].at[chip], land[a].at[slot[a], chip], lsem.at[a]).wait()

    res = pl.pallas_call(
        body, name=name,
        out_shape=[pltpu.HBM(g.shape, g.dtype) for g in grads] + [pltpu.HBM(l.shape, l.dtype) for l in lands],
        in_specs=[HBM] * (2 * n) + [SEM, SEM, SEM], out_specs=[HBM] * (2 * n),
        input_output_aliases={a: a for a in range(2 * n)},
        compiler_params=pltpu.CompilerParams(has_side_effects=EFFECT),
    )(*grads, *lands, *sems)
    return list(res[n:])


def reduce4(land, name):
    nl, _, R, C = land.shape
    TR = _adam_rows(R, C)

    def body(l_ref, o_ref):
        o_ref[...] = ((l_ref[0].astype(F32) + l_ref[1].astype(F32)) + l_ref[2].astype(F32)) + l_ref[3].astype(F32)

    return pl.pallas_call(
        body, name=name, grid=(nl, R // TR),
        in_specs=[pl.BlockSpec((None, N_CHIP, TR, C), lambda i, r: (i, 0, r, 0))],
        out_specs=pl.BlockSpec((None, TR, C), lambda i, r: (i, r, 0)),
        out_shape=jax.ShapeDtypeStruct((nl, R, C), F32), compiler_params=_cp("parallel", "parallel"))(land)


def swap_siblings(arrs, name):
    n = len(arrs)

    def body(*refs):
        src, dst = refs[:n], refs[n:2 * n]
        ssem, rsem = refs[2 * n:]
        x, y, c = _axes()
        cps = [pltpu.make_async_remote_copy(src_ref=src[a], dst_ref=dst[a], send_sem=ssem.at[a], recv_sem=rsem.at[a],
                                            device_id=(x, y, 1 - c), device_id_type=MESH) for a in range(n)]
        for cp in cps:
            cp.start()
        for cp in cps:
            cp.wait()

    return pl.pallas_call(
        body, name=name, out_shape=[jax.ShapeDtypeStruct(a.shape, a.dtype) for a in arrs],
        in_specs=[ANY] * n, out_specs=[ANY] * n,
        scratch_shapes=[pltpu.SemaphoreType.DMA((n,)), pltpu.SemaphoreType.DMA((n,))],
        compiler_params=pltpu.CompilerParams(vmem_limit_bytes=VMEM_LIMIT),
    )(*arrs)


def gather_weights(shards):
    n = len(shards)

    def body(*refs):
        src, dst = refs[:n], refs[n:2 * n]
        ssem, rsem, lsem = refs[2 * n:]
        x, y, c = _axes()
        chip = 2 * x + y
        pending = []
        for a in range(n):
            loc = pltpu.make_async_copy(src[a], dst[a].at[:, chip], lsem.at[a])
            loc.start()
            pending.append(loc)
            for k in range(1, N_CHIP):
                peer = (_flip(x, k >> 1), _flip(y, k & 1), c)
                cp = pltpu.make_async_remote_copy(src_ref=src[a], dst_ref=dst[a].at[:, chip],
                                                  send_sem=ssem.at[3 * a + k - 1], recv_sem=rsem.at[3 * a + k - 1],
                                                  device_id=peer, device_id_type=MESH)
                cp.start()
                pending.append(cp)
        for p in pending:
            p.wait()

    outs = [jax.ShapeDtypeStruct((s.shape[0], N_CHIP) + s.shape[1:], s.dtype) for s in shards]
    return pl.pallas_call(
        body, name="gather_weights", out_shape=outs, in_specs=[ANY] * n, out_specs=[ANY] * n,
        scratch_shapes=[pltpu.SemaphoreType.DMA((3 * n,)), pltpu.SemaphoreType.DMA((3 * n,)),
                        pltpu.SemaphoreType.DMA((n,))],
        compiler_params=pltpu.CompilerParams(vmem_limit_bytes=VMEM_LIMIT),
    )(*shards)


def scatter_grads(groups):
    flat = [a for g in groups for a in g]
    n, ng = len(flat), len(groups)

    def body(*refs):
        src = refs[:n]
        mine, sib = refs[n:n + ng], refs[n + ng:n + 2 * ng]
        ssem, rsem, lsem = refs[n + 2 * ng:]
        x, y, c = _axes()
        chip = 2 * x + y
        sibling = (x, y, 1 - c)
        local, sends, fwd, from_sib = [], [], [], []
        a = 0
        for g in range(ng):
            for i in range(len(groups[g])):
                def copy(k, s, d, to, a=a):
                    return pltpu.make_async_remote_copy(src_ref=s, dst_ref=d, send_sem=ssem.at[7 * a + k],
                                                        recv_sem=rsem.at[7 * a + k], device_id=to,
                                                        device_id_type=MESH)
                loc = pltpu.make_async_copy(src[a].at[chip], mine[g].at[i, chip], lsem.at[a])
                loc.start()
                local.append(loc)
                own = copy(3, src[a].at[chip], sib[g].at[i, chip], sibling)
                own.start()
                sends.append(own)
                from_sib.append(copy(3, src[a].at[chip], sib[g].at[i, chip], sibling))
                for k in range(1, N_CHIP):
                    px, py = _flip(x, k >> 1), _flip(y, k & 1)
                    pchip = 2 * px + py
                    cp = copy(k - 1, src[a].at[pchip], mine[g].at[i, chip], (px, py, c))
                    cp.start()
                    sends.append(cp)
                    fwd.append((copy(k - 1, src[a].at[pchip], mine[g].at[i, pchip], (px, py, c)),
                                copy(3 + k, mine[g].at[i, pchip], sib[g].at[i, pchip], sibling)))
                    from_sib.append(copy(3 + k, mine[g].at[i, pchip], sib[g].at[i, pchip], sibling))
                a += 1
        for arrive, onward in fwd:
            arrive.wait_recv()
            onward.start()
        for cp in sends:
            cp.wait_send()
        for arrive, onward in fwd:
            onward.wait_send()
        for cp in from_sib:
            cp.wait_recv()
        for loc in local:
            loc.wait()

    outs = []
    for g in groups:
        outs.append(jax.ShapeDtypeStruct((len(g),) + g[0].shape, g[0].dtype))
    outs = outs + outs
    res = pl.pallas_call(
        body, name="scatter_grads", out_shape=outs, in_specs=[ANY] * n, out_specs=[ANY] * (2 * ng),
        scratch_shapes=[pltpu.SemaphoreType.DMA((7 * n,)), pltpu.SemaphoreType.DMA((7 * n,)),
                        pltpu.SemaphoreType.DMA((n,))],
        compiler_params=pltpu.CompilerParams(vmem_limit_bytes=VMEM_LIMIT),
    )(*flat)
    return list(zip(res[:ng], res[ng:]))


def mm_nn(a, w, out_dtype, name, res=None, gate=None):
    M, K = a.shape
    S, _, Ns = w.shape
    TM = _tile(M, (512, 256))
    TN = _tile(Ns, (1408, 1024, 768, 512, 256, 128))
    nj = Ns // TN
    fused = res is not None

    def body(*refs):
        if fused:
            a_ref, w_ref, r_ref, g_ref, f_ref, o_ref = refs
        else:
            a_ref, w_ref, f_ref = refs
        f = jnp.dot(a_ref[...], w_ref[...], preferred_element_type=F32)
        f_ref[...] = f.astype(f_ref.dtype)
        if fused:
            o_ref[...] = r_ref[...] + g_ref[...] * f

    col = lambda i, s, j: (i, s * nj + j)
    in_specs = [pl.BlockSpec((TM, K), lambda i, s, j: (i, 0)), pl.BlockSpec((None, K, TN), lambda i, s, j: (s, 0, j))]
    out_specs = [pl.BlockSpec((TM, TN), col)]
    out_shape = [jax.ShapeDtypeStruct((M, S * Ns), out_dtype)]
    args = [a, w]
    if fused:
        in_specs += [pl.BlockSpec((TM, TN), col), pl.BlockSpec((1, TN), lambda i, s, j: (0, s * nj + j))]
        out_specs.append(pl.BlockSpec((TM, TN), col))
        out_shape.append(jax.ShapeDtypeStruct((M, S * Ns), F32))
        args += [res, gate]
    out = pl.pallas_call(body, name=name, grid=(M // TM, S, nj), in_specs=in_specs, out_specs=out_specs,
                         out_shape=out_shape, compiler_params=_cp("parallel", "parallel", "parallel"))(*args)
    return tuple(out) if fused else out[0]


def mm_nt(g, w, out_dtype, name):
    M = g.shape[0]
    S, K, Ns = w.shape
    TM = _tile(M, (512, 256))
    TN = _tile(Ns, (1408, 1024, 768, 512, 256, 128))
    nj = Ns // TN
    nred = S * nj

    def body(g_ref, w_ref, o_ref, acc):
        n = pl.program_id(1)

        @pl.when(n == 0)
        def _():
            acc[...] = jnp.zeros_like(acc)

        acc[...] += lax.dot_general(g_ref[...], w_ref[...], (((1,), (1,)), ((), ())), preferred_element_type=F32)

        @pl.when(n == nred - 1)
        def _():
            o_ref[...] = acc[...].astype(o_ref.dtype)

    return pl.pallas_call(
        body, name=name, grid=(M // TM, nred),
        in_specs=[pl.BlockSpec((TM, TN), lambda i, n: (i, n)),
                  pl.BlockSpec((None, K, TN), lambda i, n: (n // nj, 0, n % nj))],
        out_specs=pl.BlockSpec((TM, K), lambda i, n: (i, 0)),
        out_shape=jax.ShapeDtypeStruct((M, K), out_dtype),
        scratch_shapes=[pltpu.VMEM((TM, K), F32)],
        compiler_params=_cp("parallel", "arbitrary"))(g, w)


def mm_tn(a, g, S, name):
    M, K = a.shape
    Ns = g.shape[1] // S
    TM = _tile(M, (512, 256))
    TK = _tile(K, (512, 256, 128))
    TN = _tile(Ns, (1408, 1024, 768, 512, 256, 128))
    nj = Ns // TN
    nm = M // TM

    def body(a_ref, g_ref, o_ref, acc):
        m = pl.program_id(2)

        @pl.when(m == 0)
        def _():
            acc[...] = jnp.zeros_like(acc)

        acc[...] += lax.dot_general(a_ref[...], g_ref[...], (((0,), (0,)), ((), ())), preferred_element_type=F32)

        @pl.when(m == nm - 1)
        def _():
            o_ref[...] = acc[...].astype(o_ref.dtype)

    return pl.pallas_call(
        body, name=name, grid=(K // TK, S * nj, nm),
        in_specs=[pl.BlockSpec((TM, TK), lambda k, n, m: (m, k)), pl.BlockSpec((TM, TN), lambda k, n, m: (m, n))],
        out_specs=pl.BlockSpec((None, TK, TN), lambda k, n, m: (n // nj, k, n % nj)),
        out_shape=jax.ShapeDtypeStruct((S, K, Ns), BF),
        scratch_shapes=[pltpu.VMEM((TK, TN), F32)],
        compiler_params=_cp("parallel", "parallel", "arbitrary"))(a, g)


def _rows(TL, D):
    return pl.BlockSpec((TL, D), lambda i: (i, 0))


def _fixed(R, D):
    return pl.BlockSpec((R, D), lambda i: (0, 0))


def _rowsum8(v):
    T, D = v.shape
    return jnp.sum(v.reshape(T // 8, 8, D), axis=0)


def _norm_parts(xv):
    r = lax.rsqrt(jnp.mean(xv * xv, axis=-1, keepdims=True) + RMS_EPS)
    return xv * r, r


def norm_mod(x, gamma, mods, k_shift, out_dtype, name):
    L, D = x.shape
    TL = _tile(L, (512, 256))

    def body(x_ref, g_ref, m_ref, o_ref):
        xn, _ = _norm_parts(x_ref[...])
        sh, sc = m_ref[k_shift:k_shift + 1, :], m_ref[k_shift + 1:k_shift + 2, :]
        o_ref[...] = ((xn * g_ref[...]) * (1.0 + sc) + sh).astype(o_ref.dtype)

    return pl.pallas_call(body, name=name, grid=(L // TL,),
                          in_specs=[_rows(TL, D), _fixed(1, D), _fixed(6, D)], out_specs=_rows(TL, D),
                          out_shape=jax.ShapeDtypeStruct((L, D), out_dtype), compiler_params=_cp("parallel"))(x, gamma, mods)


def norm_bwd(dh, x, dres, gamma, mods, k_shift, name):
    L, D = x.shape
    TL = _tile(L, (512, 256))

    def body(dh_ref, x_ref, dr_ref, g_ref, m_ref, dx_ref, s_ref, acc):
        i = pl.program_id(0)

        @pl.when(i == 0)
        def _():
            acc[...] = jnp.zeros_like(acc)

        xn, r = _norm_parts(x_ref[...])
        dh_v = dh_ref[...].astype(F32)
        gam = g_ref[...]
        sc = m_ref[k_shift + 1:k_shift + 2, :]
        dn = dh_v * (1.0 + sc)
        dxn = dn * gam
        dx_ref[...] = dr_ref[...] + r * (dxn - xn * jnp.mean(dxn * xn, axis=-1, keepdims=True))
        acc[0] += _rowsum8(dh_v)
        acc[1] += _rowsum8(dh_v * (xn * gam))
        acc[2] += _rowsum8(dn * xn)

        @pl.when(i == pl.num_programs(0) - 1)
        def _():
            s_ref[...] = jnp.zeros_like(s_ref)
            for q in range(3):
                s_ref[q:q + 1, :] = jnp.sum(acc[q], axis=0, keepdims=True)

    return pl.pallas_call(
        body, name=name, grid=(L // TL,),
        in_specs=[_rows(TL, D), _rows(TL, D), _rows(TL, D), _fixed(1, D), _fixed(6, D)],
        out_specs=[_rows(TL, D), _fixed(8, D)],
        out_shape=[jax.ShapeDtypeStruct((L, D), F32), jax.ShapeDtypeStruct((8, D), F32)],
        scratch_shapes=[pltpu.VMEM((3, 8, D), F32)], compiler_params=_cp("arbitrary"))(dh, x, dres, gamma, mods)


def gate_bwd(dx, f, mods, k_gate, name):
    L, D = dx.shape
    TL = _tile(L, (512, 256))

    def body(dx_ref, f_ref, m_ref, o_ref, s_ref, acc):
        i = pl.program_id(0)

        @pl.when(i == 0)
        def _():
            acc[...] = jnp.zeros_like(acc)

        dxv = dx_ref[...]
        o_ref[...] = (dxv * m_ref[k_gate:k_gate + 1, :]).astype(o_ref.dtype)
        acc[...] += _rowsum8(dxv * f_ref[...].astype(F32))

        @pl.when(i == pl.num_programs(0) - 1)
        def _():
            s_ref[...] = jnp.zeros_like(s_ref)
            s_ref[0:1, :] = jnp.sum(acc[...], axis=0, keepdims=True)

    return pl.pallas_call(
        body, name=name, grid=(L // TL,), in_specs=[_rows(TL, D), _rows(TL, D), _fixed(6, D)],
        out_specs=[_rows(TL, D), _fixed(8, D)],
        out_shape=[jax.ShapeDtypeStruct((L, D), BF), jax.ShapeDtypeStruct((8, D), F32)],
        scratch_shapes=[pltpu.VMEM((8, D), F32)], compiler_params=_cp("arbitrary"))(dx, f, mods)


def swiglu_act(gu, name):
    L, F2 = gu.shape
    F = F2 // 2
    TL = _tile(L, (256,))

    def body(gu_ref, o_ref):
        g = gu_ref[:, :F].astype(F32)
        u = gu_ref[:, F:].astype(F32)
        o_ref[...] = (g * jax.nn.sigmoid(g) * u).astype(o_ref.dtype)

    return pl.pallas_call(body, name=name, grid=(L // TL,), in_specs=[_rows(TL, F2)], out_specs=_rows(TL, F),
                          out_shape=jax.ShapeDtypeStruct((L, F), BF), compiler_params=_cp("parallel"))(gu)


def swiglu_bwd(da, gu, name):
    L, F2 = gu.shape
    F = F2 // 2
    TL = _tile(L, (256,))

    def body(da_ref, gu_ref, o_ref):
        g = gu_ref[:, :F].astype(F32)
        u = gu_ref[:, F:].astype(F32)
        d = da_ref[...].astype(F32)
        s = jax.nn.sigmoid(g)
        o_ref[:, :F] = (d * u * (s + g * s * (1.0 - s))).astype(o_ref.dtype)
        o_ref[:, F:] = (d * g * s).astype(o_ref.dtype)

    return pl.pallas_call(body, name=name, grid=(L // TL,), in_specs=[_rows(TL, F), _rows(TL, F2)],
                          out_specs=_rows(TL, F2), out_shape=jax.ShapeDtypeStruct((L, F2), BF),
                          compiler_params=_cp("parallel"))(da, gu)


def glu_res(o, x, mods, k_gate, name):
    L, D = x.shape
    TL = _tile(L, (512, 256))

    def body(o_ref, x_ref, m_ref, mix_ref, y_ref):
        mix = o_ref[:, :D].astype(F32) * jax.nn.sigmoid(o_ref[:, D:].astype(F32))
        mix_ref[...] = mix.astype(mix_ref.dtype)
        y_ref[...] = x_ref[...] + m_ref[k_gate:k_gate + 1, :] * mix

    return pl.pallas_call(
        body, name=name, grid=(L // TL,), in_specs=[_rows(TL, 2 * D), _rows(TL, D), _fixed(6, D)],
        out_specs=[_rows(TL, D), _rows(TL, D)],
        out_shape=[jax.ShapeDtypeStruct((L, D), BF), jax.ShapeDtypeStruct((L, D), F32)],
        compiler_params=_cp("parallel"))(o, x, mods)


def glu_bwd(dmix, o, name):
    L, D2 = o.shape
    D = D2 // 2
    TL = _tile(L, (512, 256))

    def body(d_ref, o_ref, do_ref):
        d = d_ref[...].astype(F32)
        val = o_ref[:, :D].astype(F32)
        s = jax.nn.sigmoid(o_ref[:, D:].astype(F32))
        do_ref[:, :D] = (d * s).astype(do_ref.dtype)
        do_ref[:, D:] = (d * val * s * (1.0 - s)).astype(do_ref.dtype)

    return pl.pallas_call(body, name=name, grid=(L // TL,), in_specs=[_rows(TL, D), _rows(TL, D2)],
                          out_specs=_rows(TL, D2), out_shape=jax.ShapeDtypeStruct((L, D2), BF),
                          compiler_params=_cp("parallel"))(dmix, o)


def final_loss(x, target, gamma, name):
    L, D = x.shape
    TL = _tile(L, (512, 256))

    def body(x_ref, t_ref, g_ref, l_ref, dx_ref, s_ref, acc, lacc):
        i = pl.program_id(0)

        @pl.when(i == 0)
        def _():
            acc[...] = jnp.zeros_like(acc)
            lacc[...] = jnp.zeros_like(lacc)

        xn, r = _norm_parts(x_ref[...])
        gam = g_ref[...]
        e = xn * gam - t_ref[...]
        lacc[...] += jnp.sum(0.5 * jnp.mean(e * e, axis=-1, keepdims=True), axis=0, keepdims=True)
        dy = e * (1.0 / D)
        dxn = dy * gam
        dx_ref[...] = r * (dxn - xn * jnp.mean(dxn * xn, axis=-1, keepdims=True))
        acc[...] += _rowsum8(dy * xn)

        @pl.when(i == pl.num_programs(0) - 1)
        def _():
            s_ref[...] = jnp.zeros_like(s_ref)
            s_ref[0:1, :] = jnp.sum(acc[...], axis=0, keepdims=True)
            l_ref[...] = jnp.broadcast_to(lacc[...], l_ref.shape)

    return pl.pallas_call(
        body, name=name, grid=(L // TL,), in_specs=[_rows(TL, D), _rows(TL, D), _fixed(1, D)],
        out_specs=[_fixed(8, 128), _rows(TL, D), _fixed(8, D)],
        out_shape=[jax.ShapeDtypeStruct((8, 128), F32), jax.ShapeDtypeStruct((L, D), F32),
                   jax.ShapeDtypeStruct((8, D), F32)],
        scratch_shapes=[pltpu.VMEM((8, D), F32), pltpu.VMEM((1, 1), F32)],
        compiler_params=_cp("arbitrary"))(x, target, gamma)


def _col(L, TC, off):
    return pl.BlockSpec((L, TC), lambda j: (0, off + j))


def _shift_down(v, k, row):
    return jnp.where(row >= k, pltpu.roll(v, k, 0), 0.0)


def _shift_up(v, k, row, L):
    return jnp.where(row < L - k, pltpu.roll(v, L - k, 0), 0.0)


def conv_fwd(p, w, name):
    L, D3 = p.shape
    D = D3 // 3
    TC = _tile(D, (128,))
    nc = D // TC

    def body(b_ref, c_ref, v_ref, w_ref, o_ref):
        row = lax.broadcasted_iota(jnp.int32, (L, TC), 0)
        cv = c_ref[...].astype(F32) * v_ref[...].astype(F32)
        conv = w_ref[2:3, :] * cv + w_ref[1:2, :] * _shift_down(cv, 1, row) + w_ref[0:1, :] * _shift_down(cv, 2, row)
        o_ref[...] = (b_ref[...].astype(F32) * conv).astype(o_ref.dtype)

    return pl.pallas_call(
        body, name=name, grid=(nc,),
        in_specs=[_col(L, TC, 0), _col(L, TC, nc), _col(L, TC, 2 * nc), pl.BlockSpec((3, TC), lambda j: (0, j))],
        out_specs=_col(L, TC, 0), out_shape=jax.ShapeDtypeStruct((L, D), BF), compiler_params=_cp("parallel"))(p, p, p, w)


def conv_bwd(dm, p, w, name):
    L, D3 = p.shape
    D = D3 // 3
    TC = _tile(D, (128,))
    nc = D // TC

    def body(dm_ref, b_ref, c_ref, v_ref, w_ref, db_ref, dc_ref, dv_ref, dw_ref):
        row = lax.broadcasted_iota(jnp.int32, (L, TC), 0)
        cg, vv = c_ref[...].astype(F32), v_ref[...].astype(F32)
        cv = cg * vv
        cv1, cv2 = _shift_down(cv, 1, row), _shift_down(cv, 2, row)
        conv = w_ref[2:3, :] * cv + w_ref[1:2, :] * cv1 + w_ref[0:1, :] * cv2
        dmv = dm_ref[...].astype(F32)
        db_ref[...] = (dmv * conv).astype(db_ref.dtype)
        dconv = dmv * b_ref[...].astype(F32)
        dcv = (w_ref[2:3, :] * dconv + w_ref[1:2, :] * _shift_up(dconv, 1, row, L)
               + w_ref[0:1, :] * _shift_up(dconv, 2, row, L))
        dc_ref[...] = (dcv * vv).astype(dc_ref.dtype)
        dv_ref[...] = (dcv * cg).astype(dv_ref.dtype)
        dw_ref[...] = jnp.zeros_like(dw_ref)
        dw_ref[0:1, :] = jnp.sum(dconv * cv2, axis=0, keepdims=True)
        dw_ref[1:2, :] = jnp.sum(dconv * cv1, axis=0, keepdims=True)
        dw_ref[2:3, :] = jnp.sum(dconv * cv, axis=0, keepdims=True)

    one = jax.ShapeDtypeStruct((L, D), BF)
    return pl.pallas_call(
        body, name=name, grid=(nc,),
        in_specs=[_col(L, TC, 0), _col(L, TC, 0), _col(L, TC, nc), _col(L, TC, 2 * nc),
                  pl.BlockSpec((3, TC), lambda j: (0, j))],
        out_specs=[_col(L, TC, 0), _col(L, TC, 0), _col(L, TC, 0), pl.BlockSpec((8, TC), lambda j: (0, j))],
        out_shape=[one, one, one, jax.ShapeDtypeStruct((8, D), F32)],
        compiler_params=_cp("parallel"))(dm, p, p, p, w)


def _gelu(y):
    return 0.5 * y * (1.0 + jnp.tanh(GELU_C * (y + GELU_A * y * y * y)))


def _gelu_grad(y):
    th = jnp.tanh(GELU_C * (y + GELU_A * y * y * y))
    return 0.5 * (1.0 + th) + 0.5 * y * (1.0 - th * th) * GELU_C * (1.0 + 3.0 * GELU_A * y * y)


def _cmul_add(br, bi, ar, ai, sr, si):
    return br + ar * sr - ai * si, bi + ar * si + ai * sr


def s5_fwd(h, bblk, cblk, pw, dvec, name):
    L, D = h.shape
    nkb, KB, W2 = bblk.shape
    W = W2 // 2
    TL = _tile(L, (512, 256))
    ngrp = TL // 8

    def body(h_ref, b_ref, c_ref, pw_ref, d_ref, s_ref, y_ref, z_ref, bu, carry):
        t = pl.program_id(1)

        @pl.when(t == 0)
        def _():
            carry[...] = jnp.zeros_like(carry)

        hv = h_ref[...]
        bu[...] = jnp.dot(hv.astype(BF), b_ref[...], preferred_element_type=F32)

        def grp(j, cr_ci):
            cr, ci = cr_ci
            r0 = pl.multiple_of(j * 8, 8)
            br, bi = bu[pl.ds(r0, 8), :W], bu[pl.ds(r0, 8), W:]
            for k, off in ((1, 0), (2, 8), (4, 16)):
                br, bi = _cmul_add(br, bi, pw_ref[off:off + 8, :W], pw_ref[off:off + 8, W:],
                                   pltpu.roll(br, k, 0), pltpu.roll(bi, k, 0))
            xr, xi = _cmul_add(br, bi, pw_ref[24:32, :W], pw_ref[24:32, W:], cr, ci)
            bu[pl.ds(r0, 8), :W] = xr
            bu[pl.ds(r0, 8), W:] = xi
            return jnp.broadcast_to(xr[7:8], (8, W)), jnp.broadcast_to(xi[7:8], (8, W))

        cr, ci = lax.fori_loop(0, ngrp, grp, (carry[0], carry[1]))
        carry[0] = cr
        carry[1] = ci
        sv = bu[...]
        s_ref[...] = sv
        y = jnp.dot(sv.astype(BF), c_ref[...], preferred_element_type=F32) + d_ref[...] * hv
        y_ref[...] = y
        z_ref[...] = _gelu(y).astype(z_ref.dtype)

    blk = lambda kb, t: (t, kb)
    return pl.pallas_call(
        body, name=name, grid=(nkb, L // TL),
        in_specs=[pl.BlockSpec((TL, KB), blk), pl.BlockSpec((None, KB, W2), lambda kb, t: (kb, 0, 0)),
                  pl.BlockSpec((None, W2, KB), lambda kb, t: (kb, 0, 0)),
                  pl.BlockSpec((None, 32, W2), lambda kb, t: (kb, 0, 0)), pl.BlockSpec((1, KB), lambda kb, t: (0, kb))],
        out_specs=[pl.BlockSpec((TL, W2), blk), pl.BlockSpec((TL, KB), blk), pl.BlockSpec((TL, KB), blk)],
        out_shape=[jax.ShapeDtypeStruct((L, nkb * W2), F32), jax.ShapeDtypeStruct((L, D), F32),
                   jax.ShapeDtypeStruct((L, D), BF)],
        scratch_shapes=[pltpu.VMEM((TL, W2), F32), pltpu.VMEM((2, 8, W), F32)],
        compiler_params=_cp("parallel", "arbitrary"))(h, bblk, cblk, pw, dvec)


def s5_bwd(dz, y, h, s, ct, bt, pwr, dvec, name):
    L, D = h.shape
    nkb, KB, W2 = ct.shape
    W = W2 // 2
    TL = _tile(L, (512, 256))
    ngrp = TL // 8
    nt = L // TL

    def body(dz_ref, y_ref, h_ref, s_ref, sp_ref, ct_ref, bt_ref, pw_ref, d_ref,
             dh_ref, dd_ref, da_ref, db_ref, dc_ref, g, carry):
        t = pl.program_id(1)

        @pl.when(t == 0)
        def _():
            carry[...] = jnp.zeros_like(carry)
            dd_ref[...] = jnp.zeros_like(dd_ref)
            da_ref[...] = jnp.zeros_like(da_ref)
            db_ref[...] = jnp.zeros_like(db_ref)
            dc_ref[...] = jnp.zeros_like(dc_ref)

        hv = h_ref[...]
        dy = dz_ref[...].astype(F32) * _gelu_grad(y_ref[...])
        dd_ref[...] += _rowsum8(dy * hv)
        dyb = dy.astype(BF)
        g[...] = jnp.dot(dyb, ct_ref[...], preferred_element_type=F32)

        def grp(jj, cr_ci):
            cr, ci = cr_ci
            r0 = pl.multiple_of((ngrp - 1 - jj) * 8, 8)
            gr, gi = g[pl.ds(r0, 8), :W], g[pl.ds(r0, 8), W:]
            for k, off in ((1, 0), (2, 8), (4, 16)):
                gr, gi = _cmul_add(gr, gi, pw_ref[off:off + 8, :W], pw_ref[off:off + 8, W:],
                                   pltpu.roll(gr, 8 - k, 0), pltpu.roll(gi, 8 - k, 0))
            gr, gi = _cmul_add(gr, gi, pw_ref[24:32, :W], pw_ref[24:32, W:], cr, ci)
            g[pl.ds(r0, 8), :W] = gr
            g[pl.ds(r0, 8), W:] = gi
            return jnp.broadcast_to(gr[0:1], (8, W)), jnp.broadcast_to(gi[0:1], (8, W))

        cr, ci = lax.fori_loop(0, ngrp, grp, (carry[0], carry[1]))
        carry[0] = cr
        carry[1] = ci

        first = lax.broadcasted_iota(jnp.int32, (8, W), 0) == 0
        live = jnp.where(t == nt - 1, 0.0, 1.0)

        def prev_rows(cur_r, cur_i, before_r, before_i):
            return (jnp.where(first, pltpu.roll(before_r, 1, 0), pltpu.roll(cur_r, 1, 0)),
                    jnp.where(first, pltpu.roll(before_i, 1, 0), pltpu.roll(cur_i, 1, 0)))

        def dab(j, acc):
            ar, ai = acc
            r0 = pl.multiple_of(j * 8, 8)
            rb = pl.multiple_of(j * 8 - 8, 8)
            pr, pi = prev_rows(s_ref[pl.ds(r0, 8), :W], s_ref[pl.ds(r0, 8), W:],
                               s_ref[pl.ds(rb, 8), :W], s_ref[pl.ds(rb, 8), W:])
            gr, gi = g[pl.ds(r0, 8), :W], g[pl.ds(r0, 8), W:]
            return ar + pr * gr + pi * gi, ai + pr * gi - pi * gr

        pr, pi = prev_rows(s_ref[0:8, :W], s_ref[0:8, W:], sp_ref[:, :W] * live, sp_ref[:, W:] * live)
        gr, gi = g[0:8, :W], g[0:8, W:]
        ar, ai = lax.fori_loop(1, ngrp, dab, (pr * gr + pi * gi, pr * gi - pi * gr))
        da_ref[:, :W] += ar
        da_ref[:, W:] += ai

        gb = g[...].astype(BF)
        dh_ref[...] = dy * d_ref[...] + jnp.dot(gb, bt_ref[...], preferred_element_type=F32)
        tn = (((0,), (0,)), ((), ()))
        db_ref[...] += lax.dot_general(hv.astype(BF), gb, tn, preferred_element_type=F32)
        dc_ref[...] += lax.dot_general(dyb, s_ref[...].astype(BF), tn, preferred_element_type=F32)

    rev = lambda kb, t: (nt - 1 - t, kb)
    grp8 = TL // 8
    prev = lambda kb, t: (jnp.maximum((nt - 1 - t) * grp8 - 1, 0), kb)
    per_kb = lambda kb, t: (kb, 0, 0)
    return pl.pallas_call(
        body, name=name, grid=(nkb, nt),
        in_specs=[pl.BlockSpec((TL, KB), rev), pl.BlockSpec((TL, KB), rev), pl.BlockSpec((TL, KB), rev),
                  pl.BlockSpec((TL, W2), rev), pl.BlockSpec((8, W2), prev),
                  pl.BlockSpec((None, KB, W2), per_kb), pl.BlockSpec((None, W2, KB), per_kb),
                  pl.BlockSpec((None, 32, W2), per_kb), pl.BlockSpec((1, KB), lambda kb, t: (0, kb))],
        out_specs=[pl.BlockSpec((TL, KB), rev), pl.BlockSpec((8, KB), lambda kb, t: (0, kb)),
                   pl.BlockSpec((None, 8, W2), per_kb), pl.BlockSpec((None, KB, W2), per_kb),
                   pl.BlockSpec((None, KB, W2), per_kb)],
        out_shape=[jax.ShapeDtypeStruct((L, D), F32), jax.ShapeDtypeStruct((8, D), F32),
                   jax.ShapeDtypeStruct((nkb, 8, W2), F32), jax.ShapeDtypeStruct((nkb, KB, W2), F32),
                   jax.ShapeDtypeStruct((nkb, KB, W2), F32)],
        scratch_shapes=[pltpu.VMEM((TL, W2), F32), pltpu.VMEM((2, 8, W), F32)],
        compiler_params=_cp("parallel", "arbitrary"))(dz, y, h, s, s, ct, bt, pwr, dvec)


def _discretise(a_re, a_im, log_step, b_re, b_im):
    lr = jnp.minimum(a_re, -1e-4)
    li = a_im
    dt = jnp.exp(log_step)[:, None]
    mag = jnp.exp(lr * dt)
    abr = mag * jnp.cos(li * dt)
    abi = mag * jnp.sin(li * dt)
    den = lr * lr + li * li
    qr = ((abr - 1.0) * lr + abi * li) / den
    qi = (abi * lr - (abr - 1.0) * li) / den
    bbar_re = qr[..., None] * b_re - qi[..., None] * b_im
    bbar_im = qr[..., None] * b_im + qi[..., None] * b_re
    return abr, abi, bbar_re, bbar_im


def _block_diag(m_re, m_im, nkb):
    G, H, P = m_re.shape
    GL = G // nkb
    eye = jnp.eye(GL, dtype=m_re.dtype)[None, :, None, None, :, None]
    m = jnp.stack([m_re, m_im], axis=2).reshape(nkb, GL, H, 2, 1, P)
    return (m * eye).reshape(nkb, GL * H, 2 * GL * P)


def _block_diag_extract(blk, G):
    nkb, R, C = blk.shape
    GL = G // nkb
    H, P = R // GL, C // (2 * GL)
    eye = jnp.eye(GL, dtype=blk.dtype)[None, :, None, None, :, None]
    m = jnp.sum(blk.reshape(nkb, GL, H, 2, GL, P) * eye, axis=4)
    return m[:, :, :, 0].reshape(G, H, P), m[:, :, :, 1].reshape(G, H, P)


def _scan_powers(a_re, a_im, log_step, nkb, conj):
    G, P = a_re.shape
    lr = jnp.minimum(a_re, -1e-4)
    dt = jnp.exp(log_step)[:, None]
    n = jnp.arange(1, 9, dtype=F32)[:, None, None]
    mag = jnp.exp(n * (lr * dt)[None])
    ang = n * (a_im * dt)[None]
    pr, pi = mag * jnp.cos(ang), mag * jnp.sin(ang)
    if conj:
        pi = -pi
    row = jnp.arange(8)[:, None, None]

    def table(q):
        out = []
        for k in (1, 2, 4):
            keep = (row <= 7 - k) if conj else (row >= k)
            out.append(jnp.where(keep, q[k - 1][None], 0.0))
        out.append(q[::-1] if conj else q)
        return jnp.concatenate(out, axis=0)

    GL = G // nkb
    t = jnp.stack([table(pr), table(pi)], axis=1)
    t = t.reshape(32, 2, nkb, GL * P).transpose(2, 0, 1, 3)
    return t.reshape(nkb, 32, 2 * GL * P)


def ada_mods(c_all, w_ada, b_sh, name):
    nl, D, NA = w_ada.shape

    def body(c_ref, w_ref, b_ref, o_ref):
        cv = c_ref[...]
        act = cv * jax.nn.sigmoid(cv)
        o_ref[...] = jnp.dot(act, w_ref[...], preferred_element_type=F32, precision=lax.Precision.HIGHEST) + b_ref[...]

    return pl.pallas_call(
        body, name=name, grid=(nl,),
        in_specs=[pl.BlockSpec((8, D), lambda i: (0, 0)), pl.BlockSpec((None, D, NA), lambda i: (i, 0, 0)),
                  pl.BlockSpec((None, 1, NA), lambda i: (i, 0, 0))],
        out_specs=pl.BlockSpec((None, 8, NA), lambda i: (i, 0, 0)),
        out_shape=jax.ShapeDtypeStruct((nl, 8, NA), F32), compiler_params=_cp("parallel"))(c_all, w_ada, b_sh)


def _adamw(w, g, m, v):
    m = ADAM_B1 * m + (1.0 - ADAM_B1) * g
    v = ADAM_B2 * v + (1.0 - ADAM_B2) * (g * g)
    m_hat = m / (1.0 - ADAM_B1 ** ADAM_STEP)
    v_hat = v / (1.0 - ADAM_B2 ** ADAM_STEP)
    return -ADAM_LR * (m_hat / (jnp.sqrt(v_hat) + ADAM_EPS) + ADAM_WD * w), m, v


def _adam_rows(R, C):
    cap = max(8, (256 * 1024) // C)
    for t in range(min(R, cap), 0, -1):
        if R % t == 0 and (t % 8 == 0 or t == R):
            return t
    return R


def adamw_ada(c_t, dm, w, m, v, name):
    nl, D, NA = w.shape
    TK = _tile(D, (128,))

    def body(c_ref, dm_ref, w_ref, m_ref, v_ref, g_ref, d_ref, nm_ref, nv_ref):
        cv = c_ref[...]
        act = cv * jax.nn.sigmoid(cv)
        g = act[:, 0:1] * dm_ref[0:1, :]
        for b in range(1, 8):
            g = g + act[:, b:b + 1] * dm_ref[b:b + 1, :]
        g_ref[...] = g
        d_ref[...], nm_ref[...], nv_ref[...] = _adamw(w_ref[...], g, m_ref[...], v_ref[...])

    big = pl.BlockSpec((None, TK, NA), lambda i, k: (i, k, 0))
    shape = jax.ShapeDtypeStruct(w.shape, F32)
    return pl.pallas_call(
        body, name=name, grid=(nl, D // TK),
        in_specs=[pl.BlockSpec((TK, 8), lambda i, k: (k, 0)), pl.BlockSpec((None, 8, NA), lambda i, k: (i, 0, 0)),
                  big, big, big],
        out_specs=[big] * 4, out_shape=[shape] * 4, compiler_params=_cp("parallel", "parallel"))(c_t, dm, w, m, v)


def adamw_sharded(w, m, v, ga, gb, name):
    nl, R, C = w.shape
    TR = _adam_rows(R, C)

    def body(w_ref, m_ref, v_ref, a_ref, b_ref, g_ref, d_ref, nm_ref, nv_ref):
        g = a_ref[...] + b_ref[...]
        g_ref[...] = g
        d_ref[...], nm_ref[...], nv_ref[...] = _adamw(w_ref[...], g, m_ref[...], v_ref[...])

    big = pl.BlockSpec((None, TR, C), lambda i, r: (i, r, 0))
    shape = jax.ShapeDtypeStruct(w.shape, F32)
    return pl.pallas_call(
        body, name=name, grid=(nl, R // TR), in_specs=[big] * 5,
        out_specs=[big] * 4, out_shape=[shape] * 4, compiler_params=_cp("parallel", "parallel"))(w, m, v, ga, gb)


def adamw_slab(parts, w, m, v, name):
    _, R, C = parts.shape
    TR = _tile(R, (40, 8))

    def body(p_ref, w_ref, m_ref, v_ref, g_ref, d_ref, nm_ref, nv_ref):
        g = p_ref[0]
        for d in range(1, N_DEV):
            g = g + p_ref[d]
        g_ref[...] = g
        d_ref[...], nm_ref[...], nv_ref[...] = _adamw(w_ref[...], g, m_ref[...], v_ref[...])

    big = pl.BlockSpec((TR, C), lambda r: (r, 0))
    shape = jax.ShapeDtypeStruct((R, C), F32)
    return pl.pallas_call(
        body, name=name, grid=(R // TR,), in_specs=[pl.BlockSpec((N_DEV, TR, C), lambda r: (0, r, 0)), big, big, big],
        out_specs=[big] * 4, out_shape=[shape] * 4, compiler_params=_cp("parallel"))(parts, w, m, v)


def adamw_plain(w, m, v, g, name):
    def body(w_ref, m_ref, v_ref, g_ref, d_ref, nm_ref, nv_ref):
        d_ref[...], nm_ref[...], nv_ref[...] = _adamw(w_ref[...], g_ref[...], m_ref[...], v_ref[...])

    shape = jax.ShapeDtypeStruct(w.shape, F32)
    return pl.pallas_call(body, name=name, out_shape=[shape] * 3,
                          compiler_params=pltpu.CompilerParams(vmem_limit_bytes=VMEM_LIMIT))(w, m, v, g)


def _slab_rows(a):
    n = a.size
    rows = -(-n // SLAB_W)
    return -(-rows // 8) * 8


def _pack(arrs):
    out = []
    for a in arrs:
        rows = _slab_rows(a)
        flat = a.reshape(-1).astype(F32)
        flat = jnp.pad(flat, (0, rows * SLAB_W - flat.shape[0]))
        out.append(flat.reshape(rows, SLAB_W))
    return jnp.concatenate(out, axis=0)


def _unpack(slab, like):
    out, r = [], 0
    for a in like:
        rows = _slab_rows(a)
        out.append(slab[r:r + rows].reshape(-1)[:a.size].reshape(a.shape))
        r += rows
    return out


WEIGHTS = ['norm1_g', 'norm2_g', 'w_ada', 'b_ada', 'ssm_a_re', 'ssm_a_im', 'ssm_log_step', 'ssm_b_re', 'ssm_b_im',
           'ssm_c_re', 'ssm_c_im', 'ssm_d', 'ssm_w_out', 'conv_w_in', 'conv_w', 'conv_w_out', 'w_ffn_in',
           'w_ffn_out', 'final_g']
SLAB = ['norm1_g', 'norm2_g', 'b_ada', 'ssm_a_re', 'ssm_a_im', 'ssm_log_step', 'ssm_b_re', 'ssm_b_im', 'ssm_c_re',
        'ssm_c_im', 'ssm_d', 'final_g']
SHARDED = ['ssm_w_out', 'conv_w_in', 'conv_w_out', 'w_ffn_in', 'w_ffn_out']


def kernel(x, c, norm1_g, norm2_g, w_ada, b_ada, ssm_a_re, ssm_a_im, ssm_log_step, ssm_b_re, ssm_b_im, ssm_c_re, ssm_c_im, ssm_d, ssm_w_out, conv_w_in, conv_w, conv_w_out, w_ffn_in, w_ffn_out, final_g, loss_target, m_norm1_g, m_norm2_g, m_w_ada, m_b_ada, m_ssm_a_re, m_ssm_a_im, m_ssm_log_step, m_ssm_b_re, m_ssm_b_im, m_ssm_c_re, m_ssm_c_im, m_ssm_d, m_ssm_w_out, m_conv_w_in, m_conv_w, m_conv_w_out, m_w_ffn_in, m_w_ffn_out, m_final_g, v_norm1_g, v_norm2_g, v_w_ada, v_b_ada, v_ssm_a_re, v_ssm_a_im, v_ssm_log_step, v_ssm_b_re, v_ssm_b_im, v_ssm_c_re, v_ssm_c_im, v_ssm_d, v_ssm_w_out, v_conv_w_in, v_conv_w, v_conv_w_out, v_w_ffn_in, v_w_ffn_out, v_final_g):
    given = dict(locals())
    W = {n: given[n] for n in WEIGHTS}
    Mo = {n: given["m_" + n] for n in WEIGHTS}
    Vo = {n: given["v_" + n] for n in WEIGHTS}

    xs = x[0]
    tgt = loss_target[0]
    L, D = xs.shape
    nlayer = norm1_g.shape[0]
    NA = w_ada.shape[2]
    G = ssm_a_re.shape[1]
    nkb = D // S5_BLOCK
    ax, ay, ac = _axes()
    me = 4 * ax + 2 * ay + ac
    chip = 2 * ax + ay

    use = []
    for i in range(nlayer):
        use += [("ssm_w_out", i // 2, i)] if i % 2 == 0 else [("conv_w_in", i // 2, i), ("conv_w_out", i // 2, i)]
        use += [("w_ffn_in", i, i), ("w_ffn_out", i, i)]
    g_sems, g_srcs, g_lands, token = gather_start([W[n][j].astype(BF) for n, j, _ in use], "gather_start")

    def layer_weights(i, after):
        idx = [a for a, (_, _, li) in enumerate(use) if li == i]
        got = gather_wait(g_sems, g_srcs, g_lands, idx, after, "gather_wait%d" % i)
        return {use[a][0]: w for a, w in zip(idx, got)}

    c_all = gather8(jnp.broadcast_to(c + token[0:1, 0:1], (8, D)), "gather_c")[:, 0, :]
    b_sh = lax.dynamic_slice_in_dim(b_ada, chip * NA, NA, axis=1)[:, None, :]
    mods_part = ada_mods(c_all, w_ada, b_sh, "ada_mods")
    mg = gather8(mods_part.reshape(nlayer * 8, NA), "gather_mods")
    mg = mg.reshape(N_CHIP, 2, nlayer, 8, NA)[:, 0]
    mods_all = lax.dynamic_index_in_dim(mg, me, axis=2, keepdims=False)
    mods_all = jnp.transpose(mods_all, (1, 0, 2)).reshape(nlayer, 6, D)

    cw_parts = gather8(_pack([conv_w]), "gather_conv_w")
    nconv = conv_w.shape[0]
    cw_full = jnp.stack([_unpack(cw_parts[2 * q], [conv_w])[0] for q in range(N_CHIP)], axis=2)
    cw_full = cw_full.reshape(nconv, 3, D)

    s5 = []
    for j in range(ssm_a_re.shape[0]):
        disc, disc_vjp = jax.vjp(_discretise, ssm_a_re[j], ssm_a_im[j], ssm_log_step[j], ssm_b_re[j], ssm_b_im[j])
        _, _, bbar_re, bbar_im = disc
        bblk = _block_diag(jnp.swapaxes(bbar_re, 1, 2), jnp.swapaxes(bbar_im, 1, 2), nkb)
        ctb = _block_diag(ssm_c_re[j], -ssm_c_im[j], nkb)
        s5.append(dict(
            vjp=disc_vjp, bblk=bblk.astype(BF), bt=jnp.swapaxes(bblk, 1, 2).astype(BF),
            ct=ctb.astype(BF), cblk=jnp.swapaxes(ctb, 1, 2).astype(BF),
            pw=_scan_powers(ssm_a_re[j], ssm_a_im[j], ssm_log_step[j], nkb, False),
            pwr=_scan_powers(ssm_a_re[j], ssm_a_im[j], ssm_log_step[j], nkb, True)))

    saved = []
    xcur = xs
    for i in range(nlayer):
        j = i // 2
        mods = mods_all[i]
        full = layer_weights(i, xcur)
        sv = dict(x=xcur, w=full)
        if i % 2 == 0:
            h = norm_mod(xcur, norm1_g[i:i + 1], mods, 0, F32, "norm_mod_s5")
            states, yv, z = s5_fwd(h, s5[j]["bblk"], s5[j]["cblk"], s5[j]["pw"], ssm_d[j:j + 1], "s5_fwd")
            o = mm_nn(z, full["ssm_w_out"], BF, "mm_ssm_out")
            mix, x2 = glu_res(o, xcur, mods, 2, "glu_res")
            sv.update(h=h, states=states, y=yv, z=z, o=o)
        else:
            h = norm_mod(xcur, norm1_g[i:i + 1], mods, 0, BF, "norm_mod")
            p = mm_nn(h, full["conv_w_in"], BF, "mm_conv_in")
            mc = conv_fwd(p, cw_full[j], "conv_fwd")
            mix, x2 = mm_nn(mc, full["conv_w_out"].reshape(1, D, D), BF, "mm_conv_out", res=xcur, gate=mods[2:3])
            sv.update(h=h, p=p, mc=mc)
        h2 = norm_mod(x2, norm2_g[i:i + 1], mods, 3, BF, "norm_mod")
        gu = mm_nn(h2, full["w_ffn_in"], BF, "mm_ffn_in")
        act = swiglu_act(gu, "swiglu_act")
        F = act.shape[1]
        ff, x3 = mm_nn(act, full["w_ffn_out"].reshape(1, F, D), BF, "mm_ffn_out", res=x2, gate=mods[5:6])
        sv.update(mix=mix, x2=x2, h2=h2, gu=gu, act=act, ff=ff)
        saved.append(sv)
        xcur = x3

    loss_blk, dx, dfinal = final_loss(xcur, tgt, final_g[None, :], "final_loss")

    gland = {n: lax.empty((W[n].shape[0], N_CHIP) + W[n].shape[1:], BF) for n in SHARDED}
    in_flight = []
    dmods = [None] * nlayer
    dnorm1, dnorm2 = [None] * nlayer, [None] * nlayer
    dconv_w = [None] * nconv
    ds5 = [None] * ssm_a_re.shape[0]
    for i in reversed(range(nlayer)):
        j = i // 2
        mods = mods_all[i]
        sv = saved[i]
        full = sv["w"]
        gfull = {}
        F = sv["act"].shape[1]
        dff, dg2 = gate_bwd(dx, sv["ff"], mods, 5, "gate_bwd")
        gfull["w_ffn_out"] = mm_tn(sv["act"], dff, 1, "mm_tn_ffn_out").reshape(N_CHIP, F // N_CHIP, D)
        dact = mm_nt(dff, full["w_ffn_out"].reshape(1, F, D), BF, "mm_nt_ffn_out")
        dgu = swiglu_bwd(dact, sv["gu"], "swiglu_bwd")
        gfull["w_ffn_in"] = mm_tn(sv["h2"], dgu, N_CHIP, "mm_tn_ffn_in")
        dh2 = mm_nt(dgu, full["w_ffn_in"], F32, "mm_nt_ffn_in")
        dx2, s2 = norm_bwd(dh2, sv["x2"], dx, norm2_g[i:i + 1], mods, 3, "norm_bwd")
        dmix, dg1 = gate_bwd(dx2, sv["mix"], mods, 2, "gate_bwd")
        if i % 2 == 0:
            do = glu_bwd(dmix, sv["o"], "glu_bwd")
            gfull["ssm_w_out"] = mm_tn(sv["z"], do, N_CHIP, "mm_tn_ssm_out")
            dz = mm_nt(do, full["ssm_w_out"], F32, "mm_nt_ssm_out")
            dh, dd, dab, db, dc = s5_bwd(dz, sv["y"], sv["h"], sv["states"], s5[j]["ct"], s5[j]["bt"], s5[j]["pwr"],
                                         ssm_d[j:j + 1], "s5_bwd")
            ds5[j] = (dd, dab, db, dc)
        else:
            gfull["conv_w_out"] = mm_tn(sv["mc"], dmix, 1, "mm_tn_conv_out").reshape(N_CHIP, D // N_CHIP, D)
            dmc = mm_nt(dmix, full["conv_w_out"].reshape(1, D, D), BF, "mm_nt_conv_out")
            dbg, dcg, dvv, dcw = conv_bwd(dmc, sv["p"], cw_full[j], "conv_bwd")
            dp = jnp.concatenate([dbg, dcg, dvv], axis=1)
            gfull["conv_w_in"] = mm_tn(sv["h"], dp, N_CHIP, "mm_tn_conv_in")
            dh = mm_nt(dp, full["conv_w_in"], F32, "mm_nt_conv_in")
            dconv_w[j] = dcw[0:3]
        dx, s1 = norm_bwd(dh, sv["x"], dx2, norm1_g[i:i + 1], mods, 0, "norm_bwd")
        dmods[i] = jnp.concatenate([s1[0:2], dg1[0:1], s2[0:2], dg2[0:1]], axis=0).reshape(6 * D)
        dnorm1[i], dnorm2[i] = s1[2], s2[2]
        names = list(gfull)
        slot = [i if n.startswith("w_ffn") else j for n in names]
        sems, thru, lands = scatter_start([gfull[n] for n in names], [gland[n] for n in names], slot,
                                          "scatter_start%d" % i)
        gland.update(zip(names, lands))
        in_flight.append((names, slot, sems, thru, i))

    small = dict(norm1_g=jnp.stack(dnorm1), norm2_g=jnp.stack(dnorm2), b_ada=jnp.stack(dmods), final_g=dfinal[0])
    per = {n: [] for n in ('ssm_a_re', 'ssm_a_im', 'ssm_log_step', 'ssm_b_re', 'ssm_b_im', 'ssm_c_re', 'ssm_c_im', 'ssm_d')}
    GL = G // nkb
    for j, (dd, dab, db, dc) in enumerate(ds5):
        dab = jnp.sum(dab, axis=1).reshape(nkb, 2, GL, SSM_STATE)
        g_abr, g_abi = dab[:, 0].reshape(G, SSM_STATE), dab[:, 1].reshape(G, SSM_STATE)
        gb_re, gb_im = _block_diag_extract(db, G)
        gc_re, gc_im = _block_diag_extract(dc, G)
        ga_re, ga_im, gls, gbr, gbi = s5[j]["vjp"]((g_abr, g_abi, jnp.swapaxes(gb_re, 1, 2), jnp.swapaxes(gb_im, 1, 2)))
        for n, val in zip(per, (ga_re, ga_im, gls, gbr, gbi, gc_re, -gc_im, jnp.sum(dd, axis=0))):
            per[n].append(val)
    small.update({n: jnp.stack(vals) for n, vals in per.items()})
    dcw_full = jnp.stack(dconv_w)

    slab_like = [W[n] for n in SLAB] + [dcw_full]
    parts = gather8(_pack([small[n] for n in SLAB] + [dcw_full]), "gather_small")
    g_slab, d_slab, m_slab, v_slab = adamw_slab(
        parts, _pack([W[n] for n in SLAB] + [jnp.zeros_like(dcw_full)]),
        _pack([Mo[n] for n in SLAB] + [jnp.zeros_like(dcw_full)]),
        _pack([Vo[n] for n in SLAB] + [jnp.ones_like(dcw_full)]), "adamw_slab")
    out = {}
    for k, slab in zip(("g", "d", "m", "v"), (g_slab, d_slab, m_slab, v_slab)):
        for n, val in zip(SLAB, _unpack(slab, slab_like)):
            out[k, n] = val
    g_cw = lax.dynamic_slice_in_dim(_unpack(g_slab, slab_like)[-1], chip * conv_w.shape[2], conv_w.shape[2], axis=2)
    out["g", "conv_w"] = g_cw
    out["d", "conv_w"], out["m", "conv_w"], out["v", "conv_w"] = [
        r.reshape(conv_w.shape) for r in adamw_plain(conv_w.reshape(-1, conv_w.shape[2]), m_conv_w.reshape(-1, conv_w.shape[2]),
                                                     v_conv_w.reshape(-1, conv_w.shape[2]), g_cw.reshape(-1, conv_w.shape[2]),
                                                     "adamw_conv_w")]

    r0 = sum(_slab_rows(W[n]) for n in SLAB[:2])
    dm_all = parts[:, r0:r0 + _slab_rows(b_ada)].reshape(N_DEV, -1)[:, :b_ada.size].reshape(N_DEV, nlayer, N_CHIP, NA)
    dm_sh = jnp.transpose(lax.dynamic_index_in_dim(dm_all, chip, axis=2, keepdims=False), (1, 0, 2))
    res = adamw_ada(jnp.transpose(c_all), dm_sh, w_ada, m_w_ada, v_w_ada, "adamw_ada")
    out["g", "w_ada"], out["d", "w_ada"], out["m", "w_ada"], out["v", "w_ada"] = res

    for names, slot, sems, thru, i in in_flight:
        gland.update(zip(names, scatter_wait(sems, thru, [gland[n] for n in names], slot, "scatter_wait%d" % i)))
    mine = [reduce4(gland[n], "reduce4_" + n) for n in SHARDED]
    theirs = swap_siblings(mine, "swap_siblings")
    for n, ga, gb in zip(SHARDED, mine, theirs):
        r = adamw_sharded(W[n], Mo[n], Vo[n], ga, gb, "adamw_" + n)
        out["g", n], out["d", n], out["m", n], out["v", n] = r

    loss = lax.psum(loss_blk[0, 0], ("x", "y", "c"))
    return (loss, dx[None], *[out["g", n] for n in WEIGHTS], *[out["d", n] for n in WEIGHTS],
            *[out["m", n] for n in WEIGHTS], *[out["v", n] for n in WEIGHTS])
```

```python
import functools
import math

import jax
import jax.numpy as jnp
from jax import lax
from jax.experimental import pallas as pl
from jax.experimental.pallas import tpu as pltpu

F32 = jnp.float32
BF = jnp.bfloat16
MESH = pl.DeviceIdType.MESH
ANY = pl.BlockSpec(memory_space=pl.ANY)

N_DEV = 8
N_CHIP = 4
DEPTH = 4
SSM_GROUP = 16
SSM_STATE = 64
S5_BLOCK = 256
RMS_EPS = 1e-6
ADAM_LR, ADAM_B1, ADAM_B2, ADAM_EPS, ADAM_WD, ADAM_STEP = 0.001, 0.9, 0.999, 1e-08, 0.01, 10
V7X_VMEM_BYTES = 64 * 1024 * 1024
VMEM_LIMIT = V7X_VMEM_BYTES - 12 * 1024 * 1024
SLAB_W = 1024
GELU_C = math.sqrt(2.0 / math.pi)
GELU_A = 0.044715


def _cp(*sem):
    return pltpu.CompilerParams(dimension_semantics=sem if sem else None, vmem_limit_bytes=VMEM_LIMIT)


def _tile(n, prefs):
    for p in prefs:
        if p <= n and n % p == 0:
            return p
    return n


def _axes():
    return lax.axis_index("x"), lax.axis_index("y"), lax.axis_index("c")


def _flip(v, k):
    return 1 - v if k else v


def gather8(v, name):
    R, C = v.shape

    def body(v_ref, o_ref, ssem, rsem, lsem):
        x, y, c = _axes()
        me = 4 * x + 2 * y + c
        loc = pltpu.make_async_copy(v_ref, o_ref.at[me], lsem)
        loc.start()
        copies = []
        for k in range(1, N_DEV):
            peer = (_flip(x, (k >> 2) & 1), _flip(y, (k >> 1) & 1), _flip(c, k & 1))
            cp = pltpu.make_async_remote_copy(src_ref=v_ref, dst_ref=o_ref.at[me], send_sem=ssem.at[k - 1],
                                              recv_sem=rsem.at[k - 1], device_id=peer, device_id_type=MESH)
            cp.start()
            copies.append(cp)
        for cp in copies:
            cp.wait()
        loc.wait()

    return pl.pallas_call(
        body, name=name,
        out_shape=jax.ShapeDtypeStruct((N_DEV, R, C), v.dtype),
        in_specs=[pl.BlockSpec(memory_space=pltpu.VMEM)],
        out_specs=pl.BlockSpec(memory_space=pltpu.VMEM),
        scratch_shapes=[pltpu.SemaphoreType.DMA((N_DEV - 1,)), pltpu.SemaphoreType.DMA((N_DEV - 1,)),
                        pltpu.SemaphoreType.DMA],
        compiler_params=pltpu.CompilerParams(vmem_limit_bytes=VMEM_LIMIT),
    )(v)


HBM = pl.BlockSpec(memory_space=pltpu.HBM)
SEM = pl.BlockSpec(memory_space=pltpu.SEMAPHORE)
EFFECT = pltpu.SideEffectType.DATAFLOW_SIDE_EFFECTING


def _in_hbm(a):
    return pltpu.with_memory_space_constraint(a, pltpu.HBM)


def _chip_peers(x, y, c):
    out = []
    for k in range(1, N_CHIP):
        px, py = _flip(x, k >> 1), _flip(y, k & 1)
        out.append(((px, py, c), 2 * px + py))
    return out


def gather_start(shards, after, name):
    n = len(shards)

    def body(*refs):
        src, land = refs[:n], refs[n:2 * n]
        ssem, rsem, lsem = refs[2 * n + 1:2 * n + 4]
        x, y, c = _axes()
        chip = 2 * x + y
        for a in range(n):
            pltpu.make_async_copy(src[a], land[a].at[chip], lsem.at[a]).start()
            for k, (peer, _) in enumerate(_chip_peers(x, y, c)):
                pltpu.make_async_remote_copy(src_ref=src[a], dst_ref=land[a].at[chip], send_sem=ssem.at[3 * a + k],
                                             recv_sem=rsem.at[3 * a + k], device_id=peer, device_id_type=MESH).start()

    lands = [lax.empty((N_CHIP,) + s.shape, s.dtype) for s in shards]
    out_shape = ([pltpu.SemaphoreType.DMA((3 * n,)), pltpu.SemaphoreType.DMA((3 * n,)), pltpu.SemaphoreType.DMA((n,))]
                 + [pltpu.HBM(s.shape, s.dtype) for s in shards] + [pltpu.HBM(l.shape, l.dtype) for l in lands])
    res = pl.pallas_call(
        body, name=name, out_shape=out_shape, in_specs=[HBM] * (2 * n) + [ANY],
        out_specs=[SEM, SEM, SEM] + [HBM] * (2 * n),
        input_output_aliases={a: 3 + a for a in range(2 * n)},
        compiler_params=pltpu.CompilerParams(has_side_effects=EFFECT),
    )(*[_in_hbm(s) for s in shards], *[_in_hbm(l) for l in lands], after)
    return tuple(res[:3]), list(res[3:3 + n]), list(res[3 + n:3 + 2 * n])


def gather_wait(sems, srcs, lands, idx, after, name):
    m = len(idx)

    def body(*refs):
        src, land = refs[:m], refs[m:2 * m]
        ssem, rsem, lsem = refs[2 * m:2 * m + 3]
        x, y, c = _axes()
        chip = 2 * x + y
        for j, a in enumerate(idx):
            for k, (peer, pchip) in enumerate(_chip_peers(x, y, c)):
                cp = pltpu.make_async_remote_copy(src_ref=src[j], dst_ref=land[j].at[pchip], send_sem=ssem.at[3 * a + k],
                                                  recv_sem=rsem.at[3 * a + k], device_id=peer, device_id_type=MESH)
                cp.wait_send()
                cp.wait_recv()
            pltpu.make_async_copy(src[j], land[j].at[chip], lsem.at[a]).wait()

    s_in = [srcs[a] for a in idx]
    l_in = [lands[a] for a in idx]
    res = pl.pallas_call(
        body, name=name,
        out_shape=[pltpu.HBM(s.shape, s.dtype) for s in s_in] + [pltpu.HBM(l.shape, l.dtype) for l in l_in],
        in_specs=[HBM] * (2 * m) + [SEM, SEM, SEM, ANY], out_specs=[HBM] * (2 * m),
        input_output_aliases={a: a for a in range(2 * m)},
        compiler_params=pltpu.CompilerParams(has_side_effects=EFFECT),
    )(*s_in, *l_in, *sems, after)
    return list(res[m:])


def scatter_start(grads, lands, slot, after, name):
    n = len(grads)

    def body(*refs):
        src, land = refs[:n], refs[n:2 * n]
        ssem, rsem, lsem = refs[2 * n + 1:2 * n + 4]
        token = refs[-1]
        x, y, c = _axes()
        chip = 2 * x + y
        for a in range(n):
            pltpu.make_async_copy(src[a].at[chip], land[a].at[slot[a], chip], lsem.at[a]).start()
            for k, (peer, pchip) in enumerate(_chip_peers(x, y, c)):
                pltpu.make_async_remote_copy(src_ref=src[a].at[pchip], dst_ref=land[a].at[slot[a], chip],
                                             send_sem=ssem.at[3 * a + k], recv_sem=rsem.at[3 * a + k],
                                             device_id=peer, device_id_type=MESH).start()
        token[...] = jnp.zeros_like(token)

    out_shape = ([pltpu.SemaphoreType.DMA((3 * n,)), pltpu.SemaphoreType.DMA((3 * n,)), pltpu.SemaphoreType.DMA((n,))]
                 + [pltpu.HBM(g.shape, g.dtype) for g in grads] + [pltpu.HBM(l.shape, l.dtype) for l in lands]
                 + [jax.ShapeDtypeStruct((8, 128), F32)])
    res = pl.pallas_call(
        body, name=name, out_shape=out_shape, in_specs=[HBM] * (2 * n) + [ANY],
        out_specs=[SEM, SEM, SEM] + [HBM] * (2 * n) + [pl.BlockSpec(memory_space=pltpu.VMEM)],
        input_output_aliases={a: 3 + a for a in range(2 * n)},
        compiler_params=pltpu.CompilerParams(has_side_effects=EFFECT),
    )(*[_in_hbm(g) for g in grads], *[_in_hbm(l) for l in lands], after)
    return tuple(res[:3]), list(res[3:3 + n]), list(res[3 + n:3 + 2 * n]), res[-1]


def scatter_wait(sems, grads, lands, slot, after, name):
    n = len(grads)

    def body(*refs):
        src, land = refs[:n], refs[n:2 * n]
        ssem, rsem, lsem = refs[2 * n:2 * n + 3]
        x, y, c = _axes()
        chip = 2 * x + y
        for a in range(n):
            for k, (peer, pchip) in enumerate(_chip_peers(x, y, c)):
                cp = pltpu.make_async_remote_copy(src_ref=src[a].at[pchip], dst_ref=land[a].at[slot[a], pchip],
                                                  send_sem=ssem.at[3 * a + k], recv_sem=rsem.at[3 * a + k],
                                                  device_id=peer, device_id_type=MESH)
                cp.wait_send()
                cp.wait_recv()
            pltpu.make_async_copy(src[a].at[chip], land[a].at[slot[a], chip], lsem.at[a]).wait()

    res = pl.pallas_call(
        body, name=name,
        out_shape=[pltpu.HBM(g.shape, g.dtype) for g in grads] + [pltpu.HBM(l.shape, l.dtype) for l in lands],
        in_specs=[HBM] * (2 * n) + [SEM, SEM, SEM, ANY], out_specs=[HBM] * (2 * n),
        input_output_aliases={a: a for a in range(2 * n)},
        compiler_params=pltpu.CompilerParams(has_side_effects=EFFECT),
    )(*grads, *lands, *sems, after)
    return list(res[n:])


def reduce4(land, name):
    nl, _, R, C = land.shape
    TR = _adam_rows(R, C)

    def body(l_ref, o_ref):
        o_ref[...] = ((l_ref[0].astype(F32) + l_ref[1].astype(F32)) + l_ref[2].astype(F32)) + l_ref[3].astype(F32)

    return pl.pallas_call(
        body, name=name, grid=(nl, R // TR),
        in_specs=[pl.BlockSpec((None, N_CHIP, TR, C), lambda i, r: (i, 0, r, 0))],
        out_specs=pl.BlockSpec((None, TR, C), lambda i, r: (i, r, 0)),
        out_shape=jax.ShapeDtypeStruct((nl, R, C), F32), compiler_params=_cp("parallel", "parallel"))(land)


def swap_siblings(arrs, name):
    n = len(arrs)

    def body(*refs):
        src, dst = refs[:n], refs[n:2 * n]
        ssem, rsem = refs[2 * n:]
        x, y, c = _axes()
        cps = [pltpu.make_async_remote_copy(src_ref=src[a], dst_ref=dst[a], send_sem=ssem.at[a], recv_sem=rsem.at[a],
                                            device_id=(x, y, 1 - c), device_id_type=MESH) for a in range(n)]
        for cp in cps:
            cp.start()
        for cp in cps:
            cp.wait()

    return pl.pallas_call(
        body, name=name, out_shape=[jax.ShapeDtypeStruct(a.shape, a.dtype) for a in arrs],
        in_specs=[ANY] * n, out_specs=[ANY] * n,
        scratch_shapes=[pltpu.SemaphoreType.DMA((n,)), pltpu.SemaphoreType.DMA((n,))],
        compiler_params=pltpu.CompilerParams(vmem_limit_bytes=VMEM_LIMIT),
    )(*arrs)


def gather_weights(shards):
    n = len(shards)

    def body(*refs):
        src, dst = refs[:n], refs[n:2 * n]
        ssem, rsem, lsem = refs[2 * n:]
        x, y, c = _axes()
        chip = 2 * x + y
        pending = []
        for a in range(n):
            loc = pltpu.make_async_copy(src[a], dst[a].at[:, chip], lsem.at[a])
            loc.start()
            pending.append(loc)
            for k in range(1, N_CHIP):
                peer = (_flip(x, k >> 1), _flip(y, k & 1), c)
                cp = pltpu.make_async_remote_copy(src_ref=src[a], dst_ref=dst[a].at[:, chip],
                                                  send_sem=ssem.at[3 * a + k - 1], recv_sem=rsem.at[3 * a + k - 1],
                                                  device_id=peer, device_id_type=MESH)
                cp.start()
                pending.append(cp)
        for p in pending:
            p.wait()

    outs = [jax.ShapeDtypeStruct((s.shape[0], N_CHIP) + s.shape[1:], s.dtype) for s in shards]
    return pl.pallas_call(
        body, name="gather_weights", out_shape=outs, in_specs=[ANY] * n, out_specs=[ANY] * n,
        scratch_shapes=[pltpu.SemaphoreType.DMA((3 * n,)), pltpu.SemaphoreType.DMA((3 * n,)),
                        pltpu.SemaphoreType.DMA((n,))],
        compiler_params=pltpu.CompilerParams(vmem_limit_bytes=VMEM_LIMIT),
    )(*shards)


def scatter_grads(groups):
    flat = [a for g in groups for a in g]
    n, ng = len(flat), len(groups)

    def body(*refs):
        src = refs[:n]
        mine, sib = refs[n:n + ng], refs[n + ng:n + 2 * ng]
        ssem, rsem, lsem = refs[n + 2 * ng:]
        x, y, c = _axes()
        chip = 2 * x + y
        sibling = (x, y, 1 - c)
        local, sends, fwd, from_sib = [], [], [], []
        a = 0
        for g in range(ng):
            for i in range(len(groups[g])):
                def copy(k, s, d, to, a=a):
                    return pltpu.make_async_remote_copy(src_ref=s, dst_ref=d, send_sem=ssem.at[7 * a + k],
                                                        recv_sem=rsem.at[7 * a + k], device_id=to,
                                                        device_id_type=MESH)
                loc = pltpu.make_async_copy(src[a].at[chip], mine[g].at[i, chip], lsem.at[a])
                loc.start()
                local.append(loc)
                own = copy(3, src[a].at[chip], sib[g].at[i, chip], sibling)
                own.start()
                sends.append(own)
                from_sib.append(copy(3, src[a].at[chip], sib[g].at[i, chip], sibling))
                for k in range(1, N_CHIP):
                    px, py = _flip(x, k >> 1), _flip(y, k & 1)
                    pchip = 2 * px + py
                    cp = copy(k - 1, src[a].at[pchip], mine[g].at[i, chip], (px, py, c))
                    cp.start()
                    sends.append(cp)
                    fwd.append((copy(k - 1, src[a].at[pchip], mine[g].at[i, pchip], (px, py, c)),
                                copy(3 + k, mine[g].at[i, pchip], sib[g].at[i, pchip], sibling)))
                    from_sib.append(copy(3 + k, mine[g].at[i, pchip], sib[g].at[i, pchip], sibling))
                a += 1
        for arrive, onward in fwd:
            arrive.wait_recv()
            onward.start()
        for cp in sends:
            cp.wait_send()
        for arrive, onward in fwd:
            onward.wait_send()
        for cp in from_sib:
            cp.wait_recv()
        for loc in local:
            loc.wait()

    outs = []
    for g in groups:
        outs.append(jax.ShapeDtypeStruct((len(g),) + g[0].shape, g[0].dtype))
    outs = outs + outs
    res = pl.pallas_call(
        body, name="scatter_grads", out_shape=outs, in_specs=[ANY] * n, out_specs=[ANY] * (2 * ng),
        scratch_shapes=[pltpu.SemaphoreType.DMA((7 * n,)), pltpu.SemaphoreType.DMA((7 * n,)),
                        pltpu.SemaphoreType.DMA((n,))],
        compiler_params=pltpu.CompilerParams(vmem_limit_bytes=VMEM_LIMIT),
    )(*flat)
    return list(zip(res[:ng], res[ng:]))


def mm_nn(a, w, out_dtype, name, res=None, gate=None):
    M, K = a.shape
    S, _, Ns = w.shape
    TM = _tile(M, (512, 256))
    TN = _tile(Ns, (1408, 1024, 768, 512, 256, 128))
    nj = Ns // TN
    fused = res is not None

    def body(*refs):
        if fused:
            a_ref, w_ref, r_ref, g_ref, f_ref, o_ref = refs
        else:
            a_ref, w_ref, f_ref = refs
        f = jnp.dot(a_ref[...], w_ref[...], preferred_element_type=F32)
        f_ref[...] = f.astype(f_ref.dtype)
        if fused:
            o_ref[...] = r_ref[...] + g_ref[...] * f

    col = lambda i, s, j: (i, s * nj + j)
    in_specs = [pl.BlockSpec((TM, K), lambda i, s, j: (i, 0)), pl.BlockSpec((None, K, TN), lambda i, s, j: (s, 0, j))]
    out_specs = [pl.BlockSpec((TM, TN), col)]
    out_shape = [jax.ShapeDtypeStruct((M, S * Ns), out_dtype)]
    args = [a, w]
    if fused:
        in_specs += [pl.BlockSpec((TM, TN), col), pl.BlockSpec((1, TN), lambda i, s, j: (0, s * nj + j))]
        out_specs.append(pl.BlockSpec((TM, TN), col))
        out_shape.append(jax.ShapeDtypeStruct((M, S * Ns), F32))
        args += [res, gate]
    out = pl.pallas_call(body, name=name, grid=(M // TM, S, nj), in_specs=in_specs, out_specs=out_specs,
                         out_shape=out_shape, compiler_params=_cp("parallel", "parallel", "parallel"))(*args)
    return tuple(out) if fused else out[0]


def mm_nt(g, w, out_dtype, name):
    M = g.shape[0]
    S, K, Ns = w.shape
    TM = _tile(M, (512, 256))
    TN = _tile(Ns, (1408, 1024, 768, 512, 256, 128))
    nj = Ns // TN
    nred = S * nj

    def body(g_ref, w_ref, o_ref, acc):
        n = pl.program_id(1)

        @pl.when(n == 0)
        def _():
            acc[...] = jnp.zeros_like(acc)

        acc[...] += lax.dot_general(g_ref[...], w_ref[...], (((1,), (1,)), ((), ())), preferred_element_type=F32)

        @pl.when(n == nred - 1)
        def _():
            o_ref[...] = acc[...].astype(o_ref.dtype)

    return pl.pallas_call(
        body, name=name, grid=(M // TM, nred),
        in_specs=[pl.BlockSpec((TM, TN), lambda i, n: (i, n)),
                  pl.BlockSpec((None, K, TN), lambda i, n: (n // nj, 0, n % nj))],
        out_specs=pl.BlockSpec((TM, K), lambda i, n: (i, 0)),
        out_shape=jax.ShapeDtypeStruct((M, K), out_dtype),
        scratch_shapes=[pltpu.VMEM((TM, K), F32)],
        compiler_params=_cp("parallel", "arbitrary"))(g, w)


def mm_tn(a, g, S, name):
    M, K = a.shape
    Ns = g.shape[1] // S
    TM = _tile(M, (512, 256))
    TK = _tile(K, (512, 256, 128))
    TN = _tile(Ns, (1408, 1024, 768, 512, 256, 128))
    nj = Ns // TN
    nm = M // TM

    def body(a_ref, g_ref, o_ref, acc):
        m = pl.program_id(2)

        @pl.when(m == 0)
        def _():
            acc[...] = jnp.zeros_like(acc)

        acc[...] += lax.dot_general(a_ref[...], g_ref[...], (((0,), (0,)), ((), ())), preferred_element_type=F32)

        @pl.when(m == nm - 1)
        def _():
            o_ref[...] = acc[...].astype(o_ref.dtype)

    return pl.pallas_call(
        body, name=name, grid=(K // TK, S * nj, nm),
        in_specs=[pl.BlockSpec((TM, TK), lambda k, n, m: (m, k)), pl.BlockSpec((TM, TN), lambda k, n, m: (m, n))],
        out_specs=pl.BlockSpec((None, TK, TN), lambda k, n, m: (n // nj, k, n % nj)),
        out_shape=jax.ShapeDtypeStruct((S, K, Ns), BF),
        scratch_shapes=[pltpu.VMEM((TK, TN), F32)],
        compiler_params=_cp("parallel", "parallel", "arbitrary"))(a, g)


def _rows(TL, D):
    return pl.BlockSpec((TL, D), lambda i: (i, 0))


def _fixed(R, D):
    return pl.BlockSpec((R, D), lambda i: (0, 0))


def _rowsum8(v):
    T, D = v.shape
    return jnp.sum(v.reshape(T // 8, 8, D), axis=0)


def _norm_parts(xv):
    r = lax.rsqrt(jnp.mean(xv * xv, axis=-1, keepdims=True) + RMS_EPS)
    return xv * r, r


def norm_mod(x, gamma, mods, k_shift, out_dtype, name):
    L, D = x.shape
    TL = _tile(L, (512, 256))

    def body(x_ref, g_ref, m_ref, o_ref):
        xn, _ = _norm_parts(x_ref[...])
        sh, sc = m_ref[k_shift:k_shift + 1, :], m_ref[k_shift + 1:k_shift + 2, :]
        o_ref[...] = ((xn * g_ref[...]) * (1.0 + sc) + sh).astype(o_ref.dtype)

    return pl.pallas_call(body, name=name, grid=(L // TL,),
                          in_specs=[_rows(TL, D), _fixed(1, D), _fixed(6, D)], out_specs=_rows(TL, D),
                          out_shape=jax.ShapeDtypeStruct((L, D), out_dtype), compiler_params=_cp("parallel"))(x, gamma, mods)


def norm_bwd(dh, x, dres, gamma, mods, k_shift, name):
    L, D = x.shape
    TL = _tile(L, (512, 256))

    def body(dh_ref, x_ref, dr_ref, g_ref, m_ref, dx_ref, s_ref, acc):
        i = pl.program_id(0)

        @pl.when(i == 0)
        def _():
            acc[...] = jnp.zeros_like(acc)

        xn, r = _norm_parts(x_ref[...])
        dh_v = dh_ref[...].astype(F32)
        gam = g_ref[...]
        sc = m_ref[k_shift + 1:k_shift + 2, :]
        dn = dh_v * (1.0 + sc)
        dxn = dn * gam
        dx_ref[...] = dr_ref[...] + r * (dxn - xn * jnp.mean(dxn * xn, axis=-1, keepdims=True))
        acc[0] += _rowsum8(dh_v)
        acc[1] += _rowsum8(dh_v * (xn * gam))
        acc[2] += _rowsum8(dn * xn)

        @pl.when(i == pl.num_programs(0) - 1)
        def _():
            s_ref[...] = jnp.zeros_like(s_ref)
            for q in range(3):
                s_ref[q:q + 1, :] = jnp.sum(acc[q], axis=0, keepdims=True)

    return pl.pallas_call(
        body, name=name, grid=(L // TL,),
        in_specs=[_rows(TL, D), _rows(TL, D), _rows(TL, D), _fixed(1, D), _fixed(6, D)],
        out_specs=[_rows(TL, D), _fixed(8, D)],
        out_shape=[jax.ShapeDtypeStruct((L, D), F32), jax.ShapeDtypeStruct((8, D), F32)],
        scratch_shapes=[pltpu.VMEM((3, 8, D), F32)], compiler_params=_cp("arbitrary"))(dh, x, dres, gamma, mods)


def gate_bwd(dx, f, mods, k_gate, name):
    L, D = dx.shape
    TL = _tile(L, (512, 256))

    def body(dx_ref, f_ref, m_ref, o_ref, s_ref, acc):
        i = pl.program_id(0)

        @pl.when(i == 0)
        def _():
            acc[...] = jnp.zeros_like(acc)

        dxv = dx_ref[...]
        o_ref[...] = (dxv * m_ref[k_gate:k_gate + 1, :]).astype(o_ref.dtype)
        acc[...] += _rowsum8(dxv * f_ref[...].astype(F32))

        @pl.when(i == pl.num_programs(0) - 1)
        def _():
            s_ref[...] = jnp.zeros_like(s_ref)
            s_ref[0:1, :] = jnp.sum(acc[...], axis=0, keepdims=True)

    return pl.pallas_call(
        body, name=name, grid=(L // TL,), in_specs=[_rows(TL, D), _rows(TL, D), _fixed(6, D)],
        out_specs=[_rows(TL, D), _fixed(8, D)],
        out_shape=[jax.ShapeDtypeStruct((L, D), BF), jax.ShapeDtypeStruct((8, D), F32)],
        scratch_shapes=[pltpu.VMEM((8, D), F32)], compiler_params=_cp("arbitrary"))(dx, f, mods)


def swiglu_act(gu, name):
    L, F2 = gu.shape
    F = F2 // 2
    TL = _tile(L, (256,))

    def body(gu_ref, o_ref):
        g = gu_ref[:, :F].astype(F32)
        u = gu_ref[:, F:].astype(F32)
        o_ref[...] = (g * jax.nn.sigmoid(g) * u).astype(o_ref.dtype)

    return pl.pallas_call(body, name=name, grid=(L // TL,), in_specs=[_rows(TL, F2)], out_specs=_rows(TL, F),
                          out_shape=jax.ShapeDtypeStruct((L, F), BF), compiler_params=_cp("parallel"))(gu)


def swiglu_bwd(da, gu, name):
    L, F2 = gu.shape
    F = F2 // 2
    TL = _tile(L, (256,))

    def body(da_ref, gu_ref, o_ref):
        g = gu_ref[:, :F].astype(F32)
        u = gu_ref[:, F:].astype(F32)
        d = da_ref[...].astype(F32)
        s = jax.nn.sigmoid(g)
        o_ref[:, :F] = (d * u * (s + g * s * (1.0 - s))).astype(o_ref.dtype)
        o_ref[:, F:] = (d * g * s).astype(o_ref.dtype)

    return pl.pallas_call(body, name=name, grid=(L // TL,), in_specs=[_rows(TL, F), _rows(TL, F2)],
                          out_specs=_rows(TL, F2), out_shape=jax.ShapeDtypeStruct((L, F2), BF),
                          compiler_params=_cp("parallel"))(da, gu)


def glu_res(o, x, mods, k_gate, name):
    L, D = x.shape
    TL = _tile(L, (512, 256))

    def body(o_ref, x_ref, m_ref, mix_ref, y_ref):
        mix = o_ref[:, :D].astype(F32) * jax.nn.sigmoid(o_ref[:, D:].astype(F32))
        mix_ref[...] = mix.astype(mix_ref.dtype)
        y_ref[...] = x_ref[...] + m_ref[k_gate:k_gate + 1, :] * mix

    return pl.pallas_call(
        body, name=name, grid=(L // TL,), in_specs=[_rows(TL, 2 * D), _rows(TL, D), _fixed(6, D)],
        out_specs=[_rows(TL, D), _rows(TL, D)],
        out_shape=[jax.ShapeDtypeStruct((L, D), BF), jax.ShapeDtypeStruct((L, D), F32)],
        compiler_params=_cp("parallel"))(o, x, mods)


def glu_bwd(dmix, o, name):
    L, D2 = o.shape
    D = D2 // 2
    TL = _tile(L, (512, 256))

    def body(d_ref, o_ref, do_ref):
        d = d_ref[...].astype(F32)
        val = o_ref[:, :D].astype(F32)
        s = jax.nn.sigmoid(o_ref[:, D:].astype(F32))
        do_ref[:, :D] = (d * s).astype(do_ref.dtype)
        do_ref[:, D:] = (d * val * s * (1.0 - s)).astype(do_ref.dtype)

    return pl.pallas_call(body, name=name, grid=(L // TL,), in_specs=[_rows(TL, D), _rows(TL, D2)],
                          out_specs=_rows(TL, D2), out_shape=jax.ShapeDtypeStruct((L, D2), BF),
                          compiler_params=_cp("parallel"))(dmix, o)


def final_loss(x, target, gamma, name):
    L, D = x.shape
    TL = _tile(L, (512, 256))

    def body(x_ref, t_ref, g_ref, l_ref, dx_ref, s_ref, acc, lacc):
        i = pl.program_id(0)

        @pl.when(i == 0)
        def _():
            acc[...] = jnp.zeros_like(acc)
            lacc[...] = jnp.zeros_like(lacc)

        xn, r = _norm_parts(x_ref[...])
        gam = g_ref[...]
        e = xn * gam - t_ref[...]
        lacc[...] += jnp.sum(0.5 * jnp.mean(e * e, axis=-1, keepdims=True), axis=0, keepdims=True)
        dy = e * (1.0 / D)
        dxn = dy * gam
        dx_ref[...] = r * (dxn - xn * jnp.mean(dxn * xn, axis=-1, keepdims=True))
        acc[...] += _rowsum8(dy * xn)

        @pl.when(i == pl.num_programs(0) - 1)
        def _():
            s_ref[...] = jnp.zeros_like(s_ref)
            s_ref[0:1, :] = jnp.sum(acc[...], axis=0, keepdims=True)
            l_ref[...] = jnp.broadcast_to(lacc[...], l_ref.shape)

    return pl.pallas_call(
        body, name=name, grid=(L // TL,), in_specs=[_rows(TL, D), _rows(TL, D), _fixed(1, D)],
        out_specs=[_fixed(8, 128), _rows(TL, D), _fixed(8, D)],
        out_shape=[jax.ShapeDtypeStruct((8, 128), F32), jax.ShapeDtypeStruct((L, D), F32),
                   jax.ShapeDtypeStruct((8, D), F32)],
        scratch_shapes=[pltpu.VMEM((8, D), F32), pltpu.VMEM((1, 1), F32)],
        compiler_params=_cp("arbitrary"))(x, target, gamma)


def _col(L, TC, off):
    return pl.BlockSpec((L, TC), lambda j: (0, off + j))


def _shift_down(v, k, row):
    return jnp.where(row >= k, pltpu.roll(v, k, 0), 0.0)


def _shift_up(v, k, row, L):
    return jnp.where(row < L - k, pltpu.roll(v, L - k, 0), 0.0)


def conv_fwd(p, w, name):
    L, D3 = p.shape
    D = D3 // 3
    TC = _tile(D, (128,))
    nc = D // TC

    def body(b_ref, c_ref, v_ref, w_ref, o_ref):
        row = lax.broadcasted_iota(jnp.int32, (L, TC), 0)
        cv = c_ref[...].astype(F32) * v_ref[...].astype(F32)
        conv = w_ref[2:3, :] * cv + w_ref[1:2, :] * _shift_down(cv, 1, row) + w_ref[0:1, :] * _shift_down(cv, 2, row)
        o_ref[...] = (b_ref[...].astype(F32) * conv).astype(o_ref.dtype)

    return pl.pallas_call(
        body, name=name, grid=(nc,),
        in_specs=[_col(L, TC, 0), _col(L, TC, nc), _col(L, TC, 2 * nc), pl.BlockSpec((3, TC), lambda j: (0, j))],
        out_specs=_col(L, TC, 0), out_shape=jax.ShapeDtypeStruct((L, D), BF), compiler_params=_cp("parallel"))(p, p, p, w)


def conv_bwd(dm, p, w, name):
    L, D3 = p.shape
    D = D3 // 3
    TC = _tile(D, (128,))
    nc = D // TC

    def body(dm_ref, b_ref, c_ref, v_ref, w_ref, db_ref, dc_ref, dv_ref, dw_ref):
        row = lax.broadcasted_iota(jnp.int32, (L, TC), 0)
        cg, vv = c_ref[...].astype(F32), v_ref[...].astype(F32)
        cv = cg * vv
        cv1, cv2 = _shift_down(cv, 1, row), _shift_down(cv, 2, row)
        conv = w_ref[2:3, :] * cv + w_ref[1:2, :] * cv1 + w_ref[0:1, :] * cv2
        dmv = dm_ref[...].astype(F32)
        db_ref[...] = (dmv * conv).astype(db_ref.dtype)
        dconv = dmv * b_ref[...].astype(F32)
        dcv = (w_ref[2:3, :] * dconv + w_ref[1:2, :] * _shift_up(dconv, 1, row, L)
               + w_ref[0:1, :] * _shift_up(dconv, 2, row, L))
        dc_ref[...] = (dcv * vv).astype(dc_ref.dtype)
        dv_ref[...] = (dcv * cg).astype(dv_ref.dtype)
        dw_ref[...] = jnp.zeros_like(dw_ref)
        dw_ref[0:1, :] = jnp.sum(dconv * cv2, axis=0, keepdims=True)
        dw_ref[1:2, :] = jnp.sum(dconv * cv1, axis=0, keepdims=True)
        dw_ref[2:3, :] = jnp.sum(dconv * cv, axis=0, keepdims=True)

    one = jax.ShapeDtypeStruct((L, D), BF)
    return pl.pallas_call(
        body, name=name, grid=(nc,),
        in_specs=[_col(L, TC, 0), _col(L, TC, 0), _col(L, TC, nc), _col(L, TC, 2 * nc),
                  pl.BlockSpec((3, TC), lambda j: (0, j))],
        out_specs=[_col(L, TC, 0), _col(L, TC, 0), _col(L, TC, 0), pl.BlockSpec((8, TC), lambda j: (0, j))],
        out_shape=[one, one, one, jax.ShapeDtypeStruct((8, D), F32)],
        compiler_params=_cp("parallel"))(dm, p, p, p, w)


def _gelu(y):
    return 0.5 * y * (1.0 + jnp.tanh(GELU_C * (y + GELU_A * y * y * y)))


def _gelu_grad(y):
    th = jnp.tanh(GELU_C * (y + GELU_A * y * y * y))
    return 0.5 * (1.0 + th) + 0.5 * y * (1.0 - th * th) * GELU_C * (1.0 + 3.0 * GELU_A * y * y)


def _cmul_add(br, bi, ar, ai, sr, si):
    return br + ar * sr - ai * si, bi + ar * si + ai * sr


def s5_fwd(h, bblk, cblk, pw, dvec, name):
    L, D = h.shape
    nkb, KB, W2 = bblk.shape
    W = W2 // 2
    TL = _tile(L, (512, 256))
    ngrp = TL // 8

    def body(h_ref, b_ref, c_ref, pw_ref, d_ref, s_ref, y_ref, z_ref, bu, carry):
        t = pl.program_id(1)

        @pl.when(t == 0)
        def _():
            carry[...] = jnp.zeros_like(carry)

        hv = h_ref[...]
        bu[...] = jnp.dot(hv.astype(BF), b_ref[...], preferred_element_type=F32)

        def grp(j, cr_ci):
            cr, ci = cr_ci
            r0 = pl.multiple_of(j * 8, 8)
            br, bi = bu[pl.ds(r0, 8), :W], bu[pl.ds(r0, 8), W:]
            for k, off in ((1, 0), (2, 8), (4, 16)):
                br, bi = _cmul_add(br, bi, pw_ref[off:off + 8, :W], pw_ref[off:off + 8, W:],
                                   pltpu.roll(br, k, 0), pltpu.roll(bi, k, 0))
            xr, xi = _cmul_add(br, bi, pw_ref[24:32, :W], pw_ref[24:32, W:], cr, ci)
            bu[pl.ds(r0, 8), :W] = xr
            bu[pl.ds(r0, 8), W:] = xi
            return jnp.broadcast_to(xr[7:8], (8, W)), jnp.broadcast_to(xi[7:8], (8, W))

        cr, ci = lax.fori_loop(0, ngrp, grp, (carry[0], carry[1]))
        carry[0] = cr
        carry[1] = ci
        sv = bu[...]
        s_ref[...] = sv
        y = jnp.dot(sv.astype(BF), c_ref[...], preferred_element_type=F32) + d_ref[...] * hv
        y_ref[...] = y
        z_ref[...] = _gelu(y).astype(z_ref.dtype)

    blk = lambda kb, t: (t, kb)
    return pl.pallas_call(
        body, name=name, grid=(nkb, L // TL),
        in_specs=[pl.BlockSpec((TL, KB), blk), pl.BlockSpec((None, KB, W2), lambda kb, t: (kb, 0, 0)),
                  pl.BlockSpec((None, W2, KB), lambda kb, t: (kb, 0, 0)),
                  pl.BlockSpec((None, 32, W2), lambda kb, t: (kb, 0, 0)), pl.BlockSpec((1, KB), lambda kb, t: (0, kb))],
        out_specs=[pl.BlockSpec((TL, W2), blk), pl.BlockSpec((TL, KB), blk), pl.BlockSpec((TL, KB), blk)],
        out_shape=[jax.ShapeDtypeStruct((L, nkb * W2), F32), jax.ShapeDtypeStruct((L, D), F32),
                   jax.ShapeDtypeStruct((L, D), BF)],
        scratch_shapes=[pltpu.VMEM((TL, W2), F32), pltpu.VMEM((2, 8, W), F32)],
        compiler_params=_cp("parallel", "arbitrary"))(h, bblk, cblk, pw, dvec)


def s5_bwd(dz, y, h, s, ct, bt, pwr, dvec, name):
    L, D = h.shape
    nkb, KB, W2 = ct.shape
    W = W2 // 2
    TL = _tile(L, (512, 256))
    ngrp = TL // 8
    nt = L // TL

    def body(dz_ref, y_ref, h_ref, s_ref, sp_ref, ct_ref, bt_ref, pw_ref, d_ref,
             dh_ref, dd_ref, da_ref, db_ref, dc_ref, g, carry):
        t = pl.program_id(1)

        @pl.when(t == 0)
        def _():
            carry[...] = jnp.zeros_like(carry)
            dd_ref[...] = jnp.zeros_like(dd_ref)
            da_ref[...] = jnp.zeros_like(da_ref)
            db_ref[...] = jnp.zeros_like(db_ref)
            dc_ref[...] = jnp.zeros_like(dc_ref)

        hv = h_ref[...]
        dy = dz_ref[...].astype(F32) * _gelu_grad(y_ref[...])
        dd_ref[...] += _rowsum8(dy * hv)
        dyb = dy.astype(BF)
        g[...] = jnp.dot(dyb, ct_ref[...], preferred_element_type=F32)

        def grp(jj, cr_ci):
            cr, ci = cr_ci
            r0 = pl.multiple_of((ngrp - 1 - jj) * 8, 8)
            gr, gi = g[pl.ds(r0, 8), :W], g[pl.ds(r0, 8), W:]
            for k, off in ((1, 0), (2, 8), (4, 16)):
                gr, gi = _cmul_add(gr, gi, pw_ref[off:off + 8, :W], pw_ref[off:off + 8, W:],
                                   pltpu.roll(gr, 8 - k, 0), pltpu.roll(gi, 8 - k, 0))
            gr, gi = _cmul_add(gr, gi, pw_ref[24:32, :W], pw_ref[24:32, W:], cr, ci)
            g[pl.ds(r0, 8), :W] = gr
            g[pl.ds(r0, 8), W:] = gi
            return jnp.broadcast_to(gr[0:1], (8, W)), jnp.broadcast_to(gi[0:1], (8, W))

        cr, ci = lax.fori_loop(0, ngrp, grp, (carry[0], carry[1]))
        carry[0] = cr
        carry[1] = ci

        first = lax.broadcasted_iota(jnp.int32, (8, W), 0) == 0
        live = jnp.where(t == nt - 1, 0.0, 1.0)

        def prev_rows(cur_r, cur_i, before_r, before_i):
            return (jnp.where(first, pltpu.roll(before_r, 1, 0), pltpu.roll(cur_r, 1, 0)),
                    jnp.where(first, pltpu.roll(before_i, 1, 0), pltpu.roll(cur_i, 1, 0)))

        def dab(j, acc):
            ar, ai = acc
            r0 = pl.multiple_of(j * 8, 8)
            rb = pl.multiple_of(j * 8 - 8, 8)
            pr, pi = prev_rows(s_ref[pl.ds(r0, 8), :W], s_ref[pl.ds(r0, 8), W:],
                               s_ref[pl.ds(rb, 8), :W], s_ref[pl.ds(rb, 8), W:])
            gr, gi = g[pl.ds(r0, 8), :W], g[pl.ds(r0, 8), W:]
            return ar + pr * gr + pi * gi, ai + pr * gi - pi * gr

        pr, pi = prev_rows(s_ref[0:8, :W], s_ref[0:8, W:], sp_ref[:, :W] * live, sp_ref[:, W:] * live)
        gr, gi = g[0:8, :W], g[0:8, W:]
        ar, ai = lax.fori_loop(1, ngrp, dab, (pr * gr + pi * gi, pr * gi - pi * gr))
        da_ref[:, :W] += ar
        da_ref[:, W:] += ai

        gb = g[...].astype(BF)
        dh_ref[...] = dy * d_ref[...] + jnp.dot(gb, bt_ref[...], preferred_element_type=F32)
        tn = (((0,), (0,)), ((), ()))
        db_ref[...] += lax.dot_general(hv.astype(BF), gb, tn, preferred_element_type=F32)
        dc_ref[...] += lax.dot_general(dyb, s_ref[...].astype(BF), tn, preferred_element_type=F32)

    rev = lambda kb, t: (nt - 1 - t, kb)
    grp8 = TL // 8
    prev = lambda kb, t: (jnp.maximum((nt - 1 - t) * grp8 - 1, 0), kb)
    per_kb = lambda kb, t: (kb, 0, 0)
    return pl.pallas_call(
        body, name=name, grid=(nkb, nt),
        in_specs=[pl.BlockSpec((TL, KB), rev), pl.BlockSpec((TL, KB), rev), pl.BlockSpec((TL, KB), rev),
                  pl.BlockSpec((TL, W2), rev), pl.BlockSpec((8, W2), prev),
                  pl.BlockSpec((None, KB, W2), per_kb), pl.BlockSpec((None, W2, KB), per_kb),
                  pl.BlockSpec((None, 32, W2), per_kb), pl.BlockSpec((1, KB), lambda kb, t: (0, kb))],
        out_specs=[pl.BlockSpec((TL, KB), rev), pl.BlockSpec((8, KB), lambda kb, t: (0, kb)),
                   pl.BlockSpec((None, 8, W2), per_kb), pl.BlockSpec((None, KB, W2), per_kb),
                   pl.BlockSpec((None, KB, W2), per_kb)],
        out_shape=[jax.ShapeDtypeStruct((L, D), F32), jax.ShapeDtypeStruct((8, D), F32),
                   jax.ShapeDtypeStruct((nkb, 8, W2), F32), jax.ShapeDtypeStruct((nkb, KB, W2), F32),
                   jax.ShapeDtypeStruct((nkb, KB, W2), F32)],
        scratch_shapes=[pltpu.VMEM((TL, W2), F32), pltpu.VMEM((2, 8, W), F32)],
        compiler_params=_cp("parallel", "arbitrary"))(dz, y, h, s, s, ct, bt, pwr, dvec)


def _discretise(a_re, a_im, log_step, b_re, b_im):
    lr = jnp.minimum(a_re, -1e-4)
    li = a_im
    dt = jnp.exp(log_step)[:, None]
    mag = jnp.exp(lr * dt)
    abr = mag * jnp.cos(li * dt)
    abi = mag * jnp.sin(li * dt)
    den = lr * lr + li * li
    qr = ((abr - 1.0) * lr + abi * li) / den
    qi = (abi * lr - (abr - 1.0) * li) / den
    bbar_re = qr[..., None] * b_re - qi[..., None] * b_im
    bbar_im = qr[..., None] * b_im + qi[..., None] * b_re
    return abr, abi, bbar_re, bbar_im


def _block_diag(m_re, m_im, nkb):
    G, H, P = m_re.shape
    GL = G // nkb
    eye = jnp.eye(GL, dtype=m_re.dtype)[None, :, None, None, :, None]
    m = jnp.stack([m_re, m_im], axis=2).reshape(nkb, GL, H, 2, 1, P)
    return (m * eye).reshape(nkb, GL * H, 2 * GL * P)


def _block_diag_extract(blk, G):
    nkb, R, C = blk.shape
    GL = G // nkb
    H, P = R // GL, C // (2 * GL)
    eye = jnp.eye(GL, dtype=blk.dtype)[None, :, None, None, :, None]
    m = jnp.sum(blk.reshape(nkb, GL, H, 2, GL, P) * eye, axis=4)
    return m[:, :, :, 0].reshape(G, H, P), m[:, :, :, 1].reshape(G, H, P)


def _scan_powers(a_re, a_im, log_step, nkb, conj):
    G, P = a_re.shape
    lr = jnp.minimum(a_re, -1e-4)
    dt = jnp.exp(log_step)[:, None]
    n = jnp.arange(1, 9, dtype=F32)[:, None, None]
    mag = jnp.exp(n * (lr * dt)[None])
    ang = n * (a_im * dt)[None]
    pr, pi = mag * jnp.cos(ang), mag * jnp.sin(ang)
    if conj:
        pi = -pi
    row = jnp.arange(8)[:, None, None]

    def table(q):
        out = []
        for k in (1, 2, 4):
            keep = (row <= 7 - k) if conj else (row >= k)
            out.append(jnp.where(keep, q[k - 1][None], 0.0))
        out.append(q[::-1] if conj else q)
        return jnp.concatenate(out, axis=0)

    GL = G // nkb
    t = jnp.stack([table(pr), table(pi)], axis=1)
    t = t.reshape(32, 2, nkb, GL * P).transpose(2, 0, 1, 3)
    return t.reshape(nkb, 32, 2 * GL * P)


def ada_mods(c_all, w_ada, b_sh, name):
    nl, D, NA = w_ada.shape

    def body(c_ref, w_ref, b_ref, o_ref):
        cv = c_ref[...]
        act = cv * jax.nn.sigmoid(cv)
        o_ref[...] = jnp.dot(act, w_ref[...], preferred_element_type=F32, precision=lax.Precision.HIGHEST) + b_ref[...]

    return pl.pallas_call(
        body, name=name, grid=(nl,),
        in_specs=[pl.BlockSpec((8, D), lambda i: (0, 0)), pl.BlockSpec((None, D, NA), lambda i: (i, 0, 0)),
                  pl.BlockSpec((None, 1, NA), lambda i: (i, 0, 0))],
        out_specs=pl.BlockSpec((None, 8, NA), lambda i: (i, 0, 0)),
        out_shape=jax.ShapeDtypeStruct((nl, 8, NA), F32), compiler_params=_cp("parallel"))(c_all, w_ada, b_sh)


def _adamw(w, g, m, v):
    m = ADAM_B1 * m + (1.0 - ADAM_B1) * g
    v = ADAM_B2 * v + (1.0 - ADAM_B2) * (g * g)
    m_hat = m / (1.0 - ADAM_B1 ** ADAM_STEP)
    v_hat = v / (1.0 - ADAM_B2 ** ADAM_STEP)
    return -ADAM_LR * (m_hat / (jnp.sqrt(v_hat) + ADAM_EPS) + ADAM_WD * w), m, v


def _adam_rows(R, C):
    cap = max(8, (256 * 1024) // C)
    for t in range(min(R, cap), 0, -1):
        if R % t == 0 and (t % 8 == 0 or t == R):
            return t
    return R


def adamw_ada(c_t, dm, w, m, v, name):
    nl, D, NA = w.shape
    TK = _tile(D, (128,))

    def body(c_ref, dm_ref, w_ref, m_ref, v_ref, g_ref, d_ref, nm_ref, nv_ref):
        cv = c_ref[...]
        act = cv * jax.nn.sigmoid(cv)
        g = act[:, 0:1] * dm_ref[0:1, :]
        for b in range(1, 8):
            g = g + act[:, b:b + 1] * dm_ref[b:b + 1, :]
        g_ref[...] = g
        d_ref[...], nm_ref[...], nv_ref[...] = _adamw(w_ref[...], g, m_ref[...], v_ref[...])

    big = pl.BlockSpec((None, TK, NA), lambda i, k: (i, k, 0))
    shape = jax.ShapeDtypeStruct(w.shape, F32)
    return pl.pallas_call(
        body, name=name, grid=(nl, D // TK),
        in_specs=[pl.BlockSpec((TK, 8), lambda i, k: (k, 0)), pl.BlockSpec((None, 8, NA), lambda i, k: (i, 0, 0)),
                  big, big, big],
        out_specs=[big] * 4, out_shape=[shape] * 4, compiler_params=_cp("parallel", "parallel"))(c_t, dm, w, m, v)


def adamw_sharded(w, m, v, ga, gb, name):
    nl, R, C = w.shape
    TR = _adam_rows(R, C)

    def body(w_ref, m_ref, v_ref, a_ref, b_ref, g_ref, d_ref, nm_ref, nv_ref):
        g = a_ref[...] + b_ref[...]
        g_ref[...] = g
        d_ref[...], nm_ref[...], nv_ref[...] = _adamw(w_ref[...], g, m_ref[...], v_ref[...])

    big = pl.BlockSpec((None, TR, C), lambda i, r: (i, r, 0))
    shape = jax.ShapeDtypeStruct(w.shape, F32)
    return pl.pallas_call(
        body, name=name, grid=(nl, R // TR), in_specs=[big] * 5,
        out_specs=[big] * 4, out_shape=[shape] * 4, compiler_params=_cp("parallel", "parallel"))(w, m, v, ga, gb)


def adamw_slab(parts, w, m, v, name):
    _, R, C = parts.shape
    TR = _tile(R, (40, 8))

    def body(p_ref, w_ref, m_ref, v_ref, g_ref, d_ref, nm_ref, nv_ref):
        g = p_ref[0]
        for d in range(1, N_DEV):
            g = g + p_ref[d]
        g_ref[...] = g
        d_ref[...], nm_ref[...], nv_ref[...] = _adamw(w_ref[...], g, m_ref[...], v_ref[...])

    big = pl.BlockSpec((TR, C), lambda r: (r, 0))
    shape = jax.ShapeDtypeStruct((R, C), F32)
    return pl.pallas_call(
        body, name=name, grid=(R // TR,), in_specs=[pl.BlockSpec((N_DEV, TR, C), lambda r: (0, r, 0)), big, big, big],
        out_specs=[big] * 4, out_shape=[shape] * 4, compiler_params=_cp("parallel"))(parts, w, m, v)


def adamw_plain(w, m, v, g, name):
    def body(w_ref, m_ref, v_ref, g_ref, d_ref, nm_ref, nv_ref):
        d_ref[...], nm_ref[...], nv_ref[...] = _adamw(w_ref[...], g_ref[...], m_ref[...], v_ref[...])

    shape = jax.ShapeDtypeStruct(w.shape, F32)
    return pl.pallas_call(body, name=name, out_shape=[shape] * 3,
                          compiler_params=pltpu.CompilerParams(vmem_limit_bytes=VMEM_LIMIT))(w, m, v, g)


def _slab_rows(a):
    n = a.size
    rows = -(-n // SLAB_W)
    return -(-rows // 8) * 8


def _pack(arrs):
    out = []
    for a in arrs:
        rows = _slab_rows(a)
        flat = a.reshape(-1).astype(F32)
        flat = jnp.pad(flat, (0, rows * SLAB_W - flat.shape[0]))
        out.append(flat.reshape(rows, SLAB_W))
    return jnp.concatenate(out, axis=0)


def _unpack(slab, like):
    out, r = [], 0
    for a in like:
        rows = _slab_rows(a)
        out.append(slab[r:r + rows].reshape(-1)[:a.size].reshape(a.shape))
        r += rows
    return out


WEIGHTS = ['norm1_g', 'norm2_g', 'w_ada', 'b_ada', 'ssm_a_re', 'ssm_a_im', 'ssm_log_step', 'ssm_b_re', 'ssm_b_im',
           'ssm_c_re', 'ssm_c_im', 'ssm_d', 'ssm_w_out', 'conv_w_in', 'conv_w', 'conv_w_out', 'w_ffn_in',
           'w_ffn_out', 'final_g']
SLAB = ['norm1_g', 'norm2_g', 'b_ada', 'ssm_a_re', 'ssm_a_im', 'ssm_log_step', 'ssm_b_re', 'ssm_b_im', 'ssm_c_re',
        'ssm_c_im', 'ssm_d', 'final_g']
SHARDED = ['ssm_w_out', 'conv_w_in', 'conv_w_out', 'w_ffn_in', 'w_ffn_out']


def kernel(x, c, norm1_g, norm2_g, w_ada, b_ada, ssm_a_re, ssm_a_im, ssm_log_step, ssm_b_re, ssm_b_im, ssm_c_re, ssm_c_im, ssm_d, ssm_w_out, conv_w_in, conv_w, conv_w_out, w_ffn_in, w_ffn_out, final_g, loss_target, m_norm1_g, m_norm2_g, m_w_ada, m_b_ada, m_ssm_a_re, m_ssm_a_im, m_ssm_log_step, m_ssm_b_re, m_ssm_b_im, m_ssm_c_re, m_ssm_c_im, m_ssm_d, m_ssm_w_out, m_conv_w_in, m_conv_w, m_conv_w_out, m_w_ffn_in, m_w_ffn_out, m_final_g, v_norm1_g, v_norm2_g, v_w_ada, v_b_ada, v_ssm_a_re, v_ssm_a_im, v_ssm_log_step, v_ssm_b_re, v_ssm_b_im, v_ssm_c_re, v_ssm_c_im, v_ssm_d, v_ssm_w_out, v_conv_w_in, v_conv_w, v_conv_w_out, v_w_ffn_in, v_w_ffn_out, v_final_g):
    given = dict(locals())
    W = {n: given[n] for n in WEIGHTS}
    Mo = {n: given["m_" + n] for n in WEIGHTS}
    Vo = {n: given["v_" + n] for n in WEIGHTS}

    xs = x[0]
    tgt = loss_target[0]
    L, D = xs.shape
    nlayer = norm1_g.shape[0]
    NA = w_ada.shape[2]
    G = ssm_a_re.shape[1]
    nkb = D // S5_BLOCK
    ax, ay, ac = _axes()
    me = 4 * ax + 2 * ay + ac
    chip = 2 * ax + ay

    c_all = gather8(jnp.broadcast_to(c, (8, D)), "gather_c")[:, 0, :]
    b_sh = lax.dynamic_slice_in_dim(b_ada, chip * NA, NA, axis=1)[:, None, :]
    mods_part = ada_mods(c_all, w_ada, b_sh, "ada_mods")
    mg = gather8(mods_part.reshape(nlayer * 8, NA), "gather_mods")
    mg = mg.reshape(N_CHIP, 2, nlayer, 8, NA)[:, 0]
    mods_all = lax.dynamic_index_in_dim(mg, me, axis=2, keepdims=False)
    mods_all = jnp.transpose(mods_all, (1, 0, 2)).reshape(nlayer, 6, D)

    cw_parts = gather8(_pack([conv_w]), "gather_conv_w")
    nconv = conv_w.shape[0]
    cw_full = jnp.stack([_unpack(cw_parts[2 * q], [conv_w])[0] for q in range(N_CHIP)], axis=2)
    cw_full = cw_full.reshape(nconv, 3, D)

    use = []
    for i in range(nlayer):
        use += [("ssm_w_out", i // 2, i)] if i % 2 == 0 else [("conv_w_in", i // 2, i), ("conv_w_out", i // 2, i)]
        use += [("w_ffn_in", i, i), ("w_ffn_out", i, i)]
    g_sems, g_srcs, g_lands = gather_start([W[n][j].astype(BF) for n, j, _ in use], cw_full + mods_all[0, 0:3],
                                           "gather_start")

    def layer_weights(i, after):
        idx = [a for a, (_, _, li) in enumerate(use) if li == i]
        got = gather_wait(g_sems, g_srcs, g_lands, idx, after, "gather_wait%d" % i)
        return {use[a][0]: w for a, w in zip(idx, got)}

    s5 = []
    for j in range(ssm_a_re.shape[0]):
        disc, disc_vjp = jax.vjp(_discretise, ssm_a_re[j], ssm_a_im[j], ssm_log_step[j], ssm_b_re[j], ssm_b_im[j])
        _, _, bbar_re, bbar_im = disc
        bblk = _block_diag(jnp.swapaxes(bbar_re, 1, 2), jnp.swapaxes(bbar_im, 1, 2), nkb)
        ctb = _block_diag(ssm_c_re[j], -ssm_c_im[j], nkb)
        s5.append(dict(
            vjp=disc_vjp, bblk=bblk.astype(BF), bt=jnp.swapaxes(bblk, 1, 2).astype(BF),
            ct=ctb.astype(BF), cblk=jnp.swapaxes(ctb, 1, 2).astype(BF),
            pw=_scan_powers(ssm_a_re[j], ssm_a_im[j], ssm_log_step[j], nkb, False),
            pwr=_scan_powers(ssm_a_re[j], ssm_a_im[j], ssm_log_step[j], nkb, True)))

    saved = []
    xcur = xs
    for i in range(nlayer):
        j = i // 2
        mods = mods_all[i]
        full = layer_weights(i, xcur)
        sv = dict(x=xcur, w=full)
        if i % 2 == 0:
            h = norm_mod(xcur, norm1_g[i:i + 1], mods, 0, F32, "norm_mod_s5")
            states, yv, z = s5_fwd(h, s5[j]["bblk"], s5[j]["cblk"], s5[j]["pw"], ssm_d[j:j + 1], "s5_fwd")
            o = mm_nn(z, full["ssm_w_out"], BF, "mm_ssm_out")
            mix, x2 = glu_res(o, xcur, mods, 2, "glu_res")
            sv.update(h=h, states=states, y=yv, z=z, o=o)
        else:
            h = norm_mod(xcur, norm1_g[i:i + 1], mods, 0, BF, "norm_mod")
            p = mm_nn(h, full["conv_w_in"], BF, "mm_conv_in")
            mc = conv_fwd(p, cw_full[j], "conv_fwd")
            mix, x2 = mm_nn(mc, full["conv_w_out"].reshape(1, D, D), BF, "mm_conv_out", res=xcur, gate=mods[2:3])
            sv.update(h=h, p=p, mc=mc)
        h2 = norm_mod(x2, norm2_g[i:i + 1], mods, 3, BF, "norm_mod")
        gu = mm_nn(h2, full["w_ffn_in"], BF, "mm_ffn_in")
        act = swiglu_act(gu, "swiglu_act")
        F = act.shape[1]
        ff, x3 = mm_nn(act, full["w_ffn_out"].reshape(1, F, D), BF, "mm_ffn_out", res=x2, gate=mods[5:6])
        sv.update(mix=mix, x2=x2, h2=h2, gu=gu, act=act, ff=ff)
        saved.append(sv)
        xcur = x3

    loss_blk, dx, dfinal = final_loss(xcur, tgt, final_g[None, :], "final_loss")

    gland = {n: lax.empty((W[n].shape[0], N_CHIP) + W[n].shape[1:], BF) for n in SHARDED}
    in_flight = []
    dmods = [None] * nlayer
    dnorm1, dnorm2 = [None] * nlayer, [None] * nlayer
    dconv_w = [None] * nconv
    ds5 = [None] * ssm_a_re.shape[0]
    token = jnp.zeros((8, 128), F32)
    for i in reversed(range(nlayer)):
        j = i // 2
        mods = mods_all[i] + token[0:1, 0:1]
        sv = saved[i]
        full = sv["w"]
        gfull = {}
        F = sv["act"].shape[1]
        dff, dg2 = gate_bwd(dx, sv["ff"], mods, 5, "gate_bwd")
        gfull["w_ffn_out"] = mm_tn(sv["act"], dff, 1, "mm_tn_ffn_out").reshape(N_CHIP, F // N_CHIP, D)
        dact = mm_nt(dff, full["w_ffn_out"].reshape(1, F, D), BF, "mm_nt_ffn_out")
        dgu = swiglu_bwd(dact, sv["gu"], "swiglu_bwd")
        gfull["w_ffn_in"] = mm_tn(sv["h2"], dgu, N_CHIP, "mm_tn_ffn_in")
        dh2 = mm_nt(dgu, full["w_ffn_in"], F32, "mm_nt_ffn_in")
        dx2, s2 = norm_bwd(dh2, sv["x2"], dx, norm2_g[i:i + 1], mods, 3, "norm_bwd")
        dmix, dg1 = gate_bwd(dx2, sv["mix"], mods, 2, "gate_bwd")
        if i % 2 == 0:
            do = glu_bwd(dmix, sv["o"], "glu_bwd")
            gfull["ssm_w_out"] = mm_tn(sv["z"], do, N_CHIP, "mm_tn_ssm_out")
            dz = mm_nt(do, full["ssm_w_out"], F32, "mm_nt_ssm_out")
            dh, dd, dab, db, dc = s5_bwd(dz, sv["y"], sv["h"], sv["states"], s5[j]["ct"], s5[j]["bt"], s5[j]["pwr"],
                                         ssm_d[j:j + 1], "s5_bwd")
            ds5[j] = (dd, dab, db, dc)
        else:
            gfull["conv_w_out"] = mm_tn(sv["mc"], dmix, 1, "mm_tn_conv_out").reshape(N_CHIP, D // N_CHIP, D)
            dmc = mm_nt(dmix, full["conv_w_out"].reshape(1, D, D), BF, "mm_nt_conv_out")
            dbg, dcg, dvv, dcw = conv_bwd(dmc, sv["p"], cw_full[j], "conv_bwd")
            dp = jnp.concatenate([dbg, dcg, dvv], axis=1)
            gfull["conv_w_in"] = mm_tn(sv["h"], dp, N_CHIP, "mm_tn_conv_in")
            dh = mm_nt(dp, full["conv_w_in"], F32, "mm_nt_conv_in")
            dconv_w[j] = dcw[0:3]
        dx, s1 = norm_bwd(dh, sv["x"], dx2, norm1_g[i:i + 1], mods, 0, "norm_bwd")
        dmods[i] = jnp.concatenate([s1[0:2], dg1[0:1], s2[0:2], dg2[0:1]], axis=0).reshape(6 * D)
        dnorm1[i], dnorm2[i] = s1[2], s2[2]
        names = list(gfull)
        slot = [i if n.startswith("w_ffn") else j for n in names]
        if i > 0:
            sems, thru, lands, token = scatter_start([gfull[n] for n in names], [gland[n] for n in names], slot, dx,
                                                     "scatter_start%d" % i)
            gland.update(zip(names, lands))
            in_flight.append((names, slot, sems, thru, i))

    small = dict(norm1_g=jnp.stack(dnorm1), norm2_g=jnp.stack(dnorm2), b_ada=jnp.stack(dmods), final_g=dfinal[0])
    per = {n: [] for n in ('ssm_a_re', 'ssm_a_im', 'ssm_log_step', 'ssm_b_re', 'ssm_b_im', 'ssm_c_re', 'ssm_c_im', 'ssm_d')}
    GL = G // nkb
    for j, (dd, dab, db, dc) in enumerate(ds5):
        dab = jnp.sum(dab, axis=1).reshape(nkb, 2, GL, SSM_STATE)
        g_abr, g_abi = dab[:, 0].reshape(G, SSM_STATE), dab[:, 1].reshape(G, SSM_STATE)
        gb_re, gb_im = _block_diag_extract(db, G)
        gc_re, gc_im = _block_diag_extract(dc, G)
        ga_re, ga_im, gls, gbr, gbi = s5[j]["vjp"]((g_abr, g_abi, jnp.swapaxes(gb_re, 1, 2), jnp.swapaxes(gb_im, 1, 2)))
        for n, val in zip(per, (ga_re, ga_im, gls, gbr, gbi, gc_re, -gc_im, jnp.sum(dd, axis=0))):
            per[n].append(val)
    small.update({n: jnp.stack(vals) for n, vals in per.items()})
    dcw_full = jnp.stack(dconv_w)

    slab_like = [W[n] for n in SLAB] + [dcw_full]
    parts = gather8(_pack([small[n] for n in SLAB] + [dcw_full]), "gather_small")
    g_slab, d_slab, m_slab, v_slab = adamw_slab(
        parts, _pack([W[n] for n in SLAB] + [jnp.zeros_like(dcw_full)]),
        _pack([Mo[n] for n in SLAB] + [jnp.zeros_like(dcw_full)]),
        _pack([Vo[n] for n in SLAB] + [jnp.ones_like(dcw_full)]), "adamw_slab")
    sems, thru, lands, token = scatter_start([gfull[n] for n in names], [gland[n] for n in names], slot, g_slab,
                                             "scatter_start0")
    gland.update(zip(names, lands))
    in_flight.append((names, slot, sems, thru, 0))
    out = {}
    for k, slab in zip(("g", "d", "m", "v"), (g_slab, d_slab, m_slab, v_slab)):
        for n, val in zip(SLAB, _unpack(slab, slab_like)):
            out[k, n] = val
    g_cw = lax.dynamic_slice_in_dim(_unpack(g_slab, slab_like)[-1], chip * conv_w.shape[2], conv_w.shape[2], axis=2)
    out["g", "conv_w"] = g_cw
    out["d", "conv_w"], out["m", "conv_w"], out["v", "conv_w"] = [
        r.reshape(conv_w.shape) for r in adamw_plain(conv_w.reshape(-1, conv_w.shape[2]), m_conv_w.reshape(-1, conv_w.shape[2]),
                                                     v_conv_w.reshape(-1, conv_w.shape[2]), g_cw.reshape(-1, conv_w.shape[2]),
                                                     "adamw_conv_w")]

    r0 = sum(_slab_rows(W[n]) for n in SLAB[:2])
    dm_all = parts[:, r0:r0 + _slab_rows(b_ada)].reshape(N_DEV, -1)[:, :b_ada.size].reshape(N_DEV, nlayer, N_CHIP, NA)
    dm_sh = jnp.transpose(lax.dynamic_index_in_dim(dm_all, chip, axis=2, keepdims=False), (1, 0, 2))
    res = adamw_ada(jnp.transpose(c_all), dm_sh, w_ada, m_w_ada, v_w_ada, "adamw_ada")
    out["g", "w_ada"], out["d", "w_ada"], out["m", "w_ada"], out["v", "w_ada"] = res

    for names, slot, sems, thru, i in in_flight:
        gland.update(zip(names, scatter_wait(sems, thru, [gland[n] for n in names], slot, out["g", "w_ada"],
                                             "scatter_wait%d" % i)))
    mine = [reduce4(gland[n], "reduce4_" + n) for n in SHARDED]
    theirs = swap_siblings(mine, "swap_siblings")
    for n, ga, gb in zip(SHARDED, mine, theirs):
        r = adamw_sharded(W[n], Mo[n], Vo[n], ga, gb, "adamw_" + n)
        out["g", n], out["d", n], out["m", n], out["v", n] = r

    loss = lax.psum(loss_blk[0, 0], ("x", "y", "c"))
    return (loss, dx[None], *[out["g", n] for n in WEIGHTS], *[out["d", n] for n in WEIGHTS],
            *[out["m", n] for n in WEIGHTS], *[out["v", n] for n in WEIGHTS])
```

```python
import functools
import math

import jax
import jax.numpy as jnp
from jax import lax
from jax.experimental import pallas as pl
from jax.experimental.pallas import tpu as pltpu

F32 = jnp.float32
BF = jnp.bfloat16
MESH = pl.DeviceIdType.MESH
ANY = pl.BlockSpec(memory_space=pl.ANY)

N_DEV = 8
N_CHIP = 4
DEPTH = 4
SSM_GROUP = 16
SSM_STATE = 64
S5_BLOCK = 256
RMS_EPS = 1e-6
ADAM_LR, ADAM_B1, ADAM_B2, ADAM_EPS, ADAM_WD, ADAM_STEP = 0.001, 0.9, 0.999, 1e-08, 0.01, 10
V7X_VMEM_BYTES = 64 * 1024 * 1024
VMEM_LIMIT = V7X_VMEM_BYTES - 12 * 1024 * 1024
SLAB_W = 1024
GELU_C = math.sqrt(2.0 / math.pi)
GELU_A = 0.044715


def _cp(*sem):
    return pltpu.CompilerParams(dimension_semantics=sem if sem else None, vmem_limit_bytes=VMEM_LIMIT)


def _tile(n, prefs):
    for p in prefs:
        if p <= n and n % p == 0:
            return p
    return n


def _axes():
    return lax.axis_index("x"), lax.axis_index("y"), lax.axis_index("c")


def _flip(v, k):
    return 1 - v if k else v


def gather8(v, name):
    R, C = v.shape

    def body(v_ref, o_ref, ssem, rsem, lsem):
        x, y, c = _axes()
        me = 4 * x + 2 * y + c
        loc = pltpu.make_async_copy(v_ref, o_ref.at[me], lsem)
        loc.start()
        copies = []
        for k in range(1, N_DEV):
            peer = (_flip(x, (k >> 2) & 1), _flip(y, (k >> 1) & 1), _flip(c, k & 1))
            cp = pltpu.make_async_remote_copy(src_ref=v_ref, dst_ref=o_ref.at[me], send_sem=ssem.at[k - 1],
                                              recv_sem=rsem.at[k - 1], device_id=peer, device_id_type=MESH)
            cp.start()
            copies.append(cp)
        for cp in copies:
            cp.wait()
        loc.wait()

    return pl.pallas_call(
        body, name=name,
        out_shape=jax.ShapeDtypeStruct((N_DEV, R, C), v.dtype),
        in_specs=[pl.BlockSpec(memory_space=pltpu.VMEM)],
        out_specs=pl.BlockSpec(memory_space=pltpu.VMEM),
        scratch_shapes=[pltpu.SemaphoreType.DMA((N_DEV - 1,)), pltpu.SemaphoreType.DMA((N_DEV - 1,)),
                        pltpu.SemaphoreType.DMA],
        compiler_params=pltpu.CompilerParams(vmem_limit_bytes=VMEM_LIMIT),
    )(v)


HBM = pl.BlockSpec(memory_space=pltpu.HBM)
SEM = pl.BlockSpec(memory_space=pltpu.SEMAPHORE)
EFFECT = pltpu.SideEffectType.DATAFLOW_SIDE_EFFECTING


def _in_hbm(a):
    return pltpu.with_memory_space_constraint(a, pltpu.HBM)


def _chip_peers(x, y, c):
    out = []
    for k in range(1, N_CHIP):
        px, py = _flip(x, k >> 1), _flip(y, k & 1)
        out.append(((px, py, c), 2 * px + py))
    return out


def gather_start(shards, after, name):
    n = len(shards)

    def body(*refs):
        src, land = refs[:n], refs[n:2 * n]
        ssem, rsem, lsem = refs[2 * n + 1:2 * n + 4]
        x, y, c = _axes()
        chip = 2 * x + y
        for a in range(n):
            pltpu.make_async_copy(src[a], land[a].at[chip], lsem.at[a]).start()
            for k, (peer, _) in enumerate(_chip_peers(x, y, c)):
                pltpu.make_async_remote_copy(src_ref=src[a], dst_ref=land[a].at[chip], send_sem=ssem.at[3 * a + k],
                                             recv_sem=rsem.at[3 * a + k], device_id=peer, device_id_type=MESH).start()

    lands = [lax.empty((N_CHIP,) + s.shape, s.dtype) for s in shards]
    out_shape = ([pltpu.SemaphoreType.DMA((3 * n,)), pltpu.SemaphoreType.DMA((3 * n,)), pltpu.SemaphoreType.DMA((n,))]
                 + [pltpu.HBM(s.shape, s.dtype) for s in shards] + [pltpu.HBM(l.shape, l.dtype) for l in lands])
    res = pl.pallas_call(
        body, name=name, out_shape=out_shape, in_specs=[HBM] * (2 * n) + [ANY],
        out_specs=[SEM, SEM, SEM] + [HBM] * (2 * n),
        input_output_aliases={a: 3 + a for a in range(2 * n)},
        compiler_params=pltpu.CompilerParams(has_side_effects=EFFECT),
    )(*[_in_hbm(s) for s in shards], *[_in_hbm(l) for l in lands], after)
    return tuple(res[:3]), list(res[3:3 + n]), list(res[3 + n:3 + 2 * n])


def gather_wait(sems, srcs, lands, idx, after, name):
    m = len(idx)

    def body(*refs):
        src, land = refs[:m], refs[m:2 * m]
        ssem, rsem, lsem = refs[2 * m:2 * m + 3]
        x, y, c = _axes()
        chip = 2 * x + y
        for j, a in enumerate(idx):
            for k, (peer, pchip) in enumerate(_chip_peers(x, y, c)):
                cp = pltpu.make_async_remote_copy(src_ref=src[j], dst_ref=land[j].at[pchip], send_sem=ssem.at[3 * a + k],
                                                  recv_sem=rsem.at[3 * a + k], device_id=peer, device_id_type=MESH)
                cp.wait_send()
                cp.wait_recv()
            pltpu.make_async_copy(src[j], land[j].at[chip], lsem.at[a]).wait()

    s_in = [srcs[a] for a in idx]
    l_in = [lands[a] for a in idx]
    res = pl.pallas_call(
        body, name=name,
        out_shape=[pltpu.HBM(s.shape, s.dtype) for s in s_in] + [pltpu.HBM(l.shape, l.dtype) for l in l_in],
        in_specs=[HBM] * (2 * m) + [SEM, SEM, SEM, ANY], out_specs=[HBM] * (2 * m),
        input_output_aliases={a: a for a in range(2 * m)},
        compiler_params=pltpu.CompilerParams(has_side_effects=EFFECT),
    )(*s_in, *l_in, *sems, after)
    return list(res[m:])


def scatter_start(grads, lands, slot, after, name):
    n = len(grads)

    def body(*refs):
        src, land = refs[:n], refs[n:2 * n]
        ssem, rsem, lsem = refs[2 * n + 1:2 * n + 4]
        token = refs[-1]
        x, y, c = _axes()
        chip = 2 * x + y
        for a in range(n):
            pltpu.make_async_copy(src[a].at[chip], land[a].at[slot[a], chip], lsem.at[a]).start()
            for k, (peer, pchip) in enumerate(_chip_peers(x, y, c)):
                pltpu.make_async_remote_copy(src_ref=src[a].at[pchip], dst_ref=land[a].at[slot[a], chip],
                                             send_sem=ssem.at[3 * a + k], recv_sem=rsem.at[3 * a + k],
                                             device_id=peer, device_id_type=MESH).start()
        token[...] = jnp.zeros_like(token)

    out_shape = ([pltpu.SemaphoreType.DMA((3 * n,)), pltpu.SemaphoreType.DMA((3 * n,)), pltpu.SemaphoreType.DMA((n,))]
                 + [pltpu.HBM(g.shape, g.dtype) for g in grads] + [pltpu.HBM(l.shape, l.dtype) for l in lands]
                 + [jax.ShapeDtypeStruct((8, 128), F32)])
    res = pl.pallas_call(
        body, name=name, out_shape=out_shape, in_specs=[HBM] * (2 * n) + [ANY],
        out_specs=[SEM, SEM, SEM] + [HBM] * (2 * n) + [pl.BlockSpec(memory_space=pltpu.VMEM)],
        input_output_aliases={a: 3 + a for a in range(2 * n)},
        compiler_params=pltpu.CompilerParams(has_side_effects=EFFECT),
    )(*[_in_hbm(g) for g in grads], *[_in_hbm(l) for l in lands], after)
    return tuple(res[:3]), list(res[3:3 + n]), list(res[3 + n:3 + 2 * n]), res[-1]


def scatter_wait(sems, grads, lands, slot, after, name):
    n = len(grads)

    def body(*refs):
        src, land = refs[:n], refs[n:2 * n]
        ssem, rsem, lsem = refs[2 * n:2 * n + 3]
        x, y, c = _axes()
        chip = 2 * x + y
        for a in range(n):
            for k, (peer, pchip) in enumerate(_chip_peers(x, y, c)):
                cp = pltpu.make_async_remote_copy(src_ref=src[a].at[pchip], dst_ref=land[a].at[slot[a], pchip],
                                                  send_sem=ssem.at[3 * a + k], recv_sem=rsem.at[3 * a + k],
                                                  device_id=peer, device_id_type=MESH)
                cp.wait_send()
                cp.wait_recv()
            pltpu.make_async_copy(src[a].at[chip], land[a].at[slot[a], chip], lsem.at[a]).wait()

    res = pl.pallas_call(
        body, name=name,
        out_shape=[pltpu.HBM(g.shape, g.dtype) for g in grads] + [pltpu.HBM(l.shape, l.dtype) for l in lands],
        in_specs=[HBM] * (2 * n) + [SEM, SEM, SEM, ANY], out_specs=[HBM] * (2 * n),
        input_output_aliases={a: a for a in range(2 * n)},
        compiler_params=pltpu.CompilerParams(has_side_effects=EFFECT),
    )(*grads, *lands, *sems, after)
    return list(res[n:])


def reduce4(land, name):
    nl, _, R, C = land.shape
    TR = _adam_rows(R, C)

    def body(l_ref, o_ref):
        o_ref[...] = ((l_ref[0].astype(F32) + l_ref[1].astype(F32)) + l_ref[2].astype(F32)) + l_ref[3].astype(F32)

    return pl.pallas_call(
        body, name=name, grid=(nl, R // TR),
        in_specs=[pl.BlockSpec((None, N_CHIP, TR, C), lambda i, r: (i, 0, r, 0))],
        out_specs=pl.BlockSpec((None, TR, C), lambda i, r: (i, r, 0)),
        out_shape=jax.ShapeDtypeStruct((nl, R, C), F32), compiler_params=_cp("parallel", "parallel"))(land)


def swap_siblings(arrs, name):
    n = len(arrs)

    def body(*refs):
        src, dst = refs[:n], refs[n:2 * n]
        ssem, rsem = refs[2 * n:]
        x, y, c = _axes()
        cps = [pltpu.make_async_remote_copy(src_ref=src[a], dst_ref=dst[a], send_sem=ssem.at[a], recv_sem=rsem.at[a],
                                            device_id=(x, y, 1 - c), device_id_type=MESH) for a in range(n)]
        for cp in cps:
            cp.start()
        for cp in cps:
            cp.wait()

    return pl.pallas_call(
        body, name=name, out_shape=[jax.ShapeDtypeStruct(a.shape, a.dtype) for a in arrs],
        in_specs=[ANY] * n, out_specs=[ANY] * n,
        scratch_shapes=[pltpu.SemaphoreType.DMA((n,)), pltpu.SemaphoreType.DMA((n,))],
        compiler_params=pltpu.CompilerParams(vmem_limit_bytes=VMEM_LIMIT),
    )(*arrs)


def swap_start(arrs, name):
    n = len(arrs)

    def body(*refs):
        src, land = refs[:n], refs[n:2 * n]
        ssem, rsem = refs[2 * n:2 * n + 2]
        token = refs[-1]
        x, y, c = _axes()
        for a in range(n):
            pltpu.make_async_remote_copy(src_ref=src[a], dst_ref=land[a], send_sem=ssem.at[a], recv_sem=rsem.at[a],
                                         device_id=(x, y, 1 - c), device_id_type=MESH).start()
        token[...] = jnp.zeros_like(token)

    lands = [lax.empty(a.shape, a.dtype) for a in arrs]
    out_shape = ([pltpu.SemaphoreType.DMA((n,)), pltpu.SemaphoreType.DMA((n,))]
                 + [pltpu.HBM(a.shape, a.dtype) for a in arrs] * 2 + [jax.ShapeDtypeStruct((8, 128), F32)])
    res = pl.pallas_call(
        body, name=name, out_shape=out_shape, in_specs=[HBM] * (2 * n),
        out_specs=[SEM, SEM] + [HBM] * (2 * n) + [pl.BlockSpec(memory_space=pltpu.VMEM)],
        input_output_aliases={a: 2 + a for a in range(2 * n)},
        compiler_params=pltpu.CompilerParams(has_side_effects=EFFECT),
    )(*[_in_hbm(a) for a in arrs], *[_in_hbm(l) for l in lands])
    return tuple(res[:2]), list(res[2:2 + n]), list(res[2 + n:2 + 2 * n]), res[-1]


def swap_wait(sems, srcs, lands, after, name):
    n = len(srcs)

    def body(*refs):
        src, land = refs[:n], refs[n:2 * n]
        ssem, rsem = refs[2 * n:2 * n + 2]
        x, y, c = _axes()
        for a in range(n):
            cp = pltpu.make_async_remote_copy(src_ref=src[a], dst_ref=land[a], send_sem=ssem.at[a],
                                              recv_sem=rsem.at[a], device_id=(x, y, 1 - c), device_id_type=MESH)
            cp.wait_send()
            cp.wait_recv()

    res = pl.pallas_call(
        body, name=name, out_shape=[pltpu.HBM(a.shape, a.dtype) for a in srcs] * 2,
        in_specs=[HBM] * (2 * n) + [SEM, SEM, ANY], out_specs=[HBM] * (2 * n),
        input_output_aliases={a: a for a in range(2 * n)},
        compiler_params=pltpu.CompilerParams(has_side_effects=EFFECT),
    )(*srcs, *lands, *sems, after)
    return list(res[n:])


def reduce8(slab, dm, name):
    RT, C = slab.shape
    P = RT // N_DEV
    R = dm.shape[0]

    def body(s_ref, dm_ref, o_ref, dmo_ref, recv, s1, r1, s2, r2, s3, r3):
        x, y, c = _axes()
        me = 4 * x + 2 * y + c
        mine = pl.ds(pl.multiple_of(me * P, 8), P)
        parts, dms = [], []
        for k in range(1, N_DEV):
            px, py, pc = _flip(x, (k >> 2) & 1), _flip(y, (k >> 1) & 1), _flip(c, k & 1)
            theirs = pl.ds(pl.multiple_of((4 * px + 2 * py + pc) * P, 8), P)
            cp = pltpu.make_async_remote_copy(src_ref=s_ref.at[theirs], dst_ref=recv.at[me], send_sem=s1.at[k - 1],
                                              recv_sem=r1.at[k - 1], device_id=(px, py, pc), device_id_type=MESH)
            cp.start()
            parts.append(cp)
            cd = pltpu.make_async_remote_copy(src_ref=dm_ref, dst_ref=dmo_ref.at[me], send_sem=s3.at[k - 1],
                                              recv_sem=r3.at[k - 1], device_id=(px, py, pc), device_id_type=MESH)
            cd.start()
            dms.append(cd)
        dmo_ref[me] = dm_ref[...]
        recv[me] = s_ref[mine, :]
        for cp in parts:
            cp.wait()
        tot = recv[0]
        for d in range(1, N_DEV):
            tot = tot + recv[d]
        o_ref[mine, :] = tot
        out = []
        for k in range(1, N_DEV):
            peer = (_flip(x, (k >> 2) & 1), _flip(y, (k >> 1) & 1), _flip(c, k & 1))
            cp = pltpu.make_async_remote_copy(src_ref=o_ref.at[mine], dst_ref=o_ref.at[mine], send_sem=s2.at[k - 1],
                                              recv_sem=r2.at[k - 1], device_id=peer, device_id_type=MESH)
            cp.start()
            out.append(cp)
        for cp in out + dms:
            cp.wait()

    sems = [pltpu.SemaphoreType.DMA((N_DEV - 1,))] * 6
    return pl.pallas_call(
        body, name=name,
        out_shape=[jax.ShapeDtypeStruct((RT, C), F32), jax.ShapeDtypeStruct((N_DEV, R, C), F32)],
        in_specs=[pl.BlockSpec(memory_space=pltpu.VMEM)] * 2, out_specs=[pl.BlockSpec(memory_space=pltpu.VMEM)] * 2,
        scratch_shapes=[pltpu.VMEM((N_DEV, P, C), F32)] + sems,
        compiler_params=pltpu.CompilerParams(vmem_limit_bytes=VMEM_LIMIT),
    )(slab, dm)


def mm_nn(a, w, out_dtype, name, res=None, gate=None):
    M, K = a.shape
    S, _, Ns = w.shape
    TM = _tile(M, (512, 256))
    TN = _tile(Ns, (1408, 1024, 768, 512, 256, 128))
    nj = Ns // TN
    fused = res is not None

    def body(*refs):
        if fused:
            a_ref, w_ref, r_ref, g_ref, f_ref, o_ref = refs
        else:
            a_ref, w_ref, f_ref = refs
        f = jnp.dot(a_ref[...], w_ref[...], preferred_element_type=F32)
        f_ref[...] = f.astype(f_ref.dtype)
        if fused:
            o_ref[...] = r_ref[...] + g_ref[...] * f

    col = lambda i, s, j: (i, s * nj + j)
    in_specs = [pl.BlockSpec((TM, K), lambda i, s, j: (i, 0)), pl.BlockSpec((None, K, TN), lambda i, s, j: (s, 0, j))]
    out_specs = [pl.BlockSpec((TM, TN), col)]
    out_shape = [jax.ShapeDtypeStruct((M, S * Ns), out_dtype)]
    args = [a, w]
    if fused:
        in_specs += [pl.BlockSpec((TM, TN), col), pl.BlockSpec((1, TN), lambda i, s, j: (0, s * nj + j))]
        out_specs.append(pl.BlockSpec((TM, TN), col))
        out_shape.append(jax.ShapeDtypeStruct((M, S * Ns), F32))
        args += [res, gate]
    out = pl.pallas_call(body, name=name, grid=(M // TM, S, nj), in_specs=in_specs, out_specs=out_specs,
                         out_shape=out_shape, compiler_params=_cp("parallel", "parallel", "parallel"))(*args)
    return tuple(out) if fused else out[0]


def mm_nt(g, w, out_dtype, name):
    M = g.shape[0]
    S, K, Ns = w.shape
    TM = _tile(M, (512, 256))
    TN = _tile(Ns, (1408, 1024, 768, 512, 256, 128))
    nj = Ns // TN
    nred = S * nj

    def body(g_ref, w_ref, o_ref, acc):
        n = pl.program_id(1)

        @pl.when(n == 0)
        def _():
            acc[...] = jnp.zeros_like(acc)

        acc[...] += lax.dot_general(g_ref[...], w_ref[...], (((1,), (1,)), ((), ())), preferred_element_type=F32)

        @pl.when(n == nred - 1)
        def _():
            o_ref[...] = acc[...].astype(o_ref.dtype)

    return pl.pallas_call(
        body, name=name, grid=(M // TM, nred),
        in_specs=[pl.BlockSpec((TM, TN), lambda i, n: (i, n)),
                  pl.BlockSpec((None, K, TN), lambda i, n: (n // nj, 0, n % nj))],
        out_specs=pl.BlockSpec((TM, K), lambda i, n: (i, 0)),
        out_shape=jax.ShapeDtypeStruct((M, K), out_dtype),
        scratch_shapes=[pltpu.VMEM((TM, K), F32)],
        compiler_params=_cp("parallel", "arbitrary"))(g, w)


def mm_tn(a, g, S, name):
    M, K = a.shape
    Ns = g.shape[1] // S
    TM = _tile(M, (512, 256))
    TK = _tile(K, (512, 256, 128))
    TN = _tile(Ns, (1408, 1024, 768, 512, 256, 128))
    nj = Ns // TN
    nm = M // TM

    def body(a_ref, g_ref, o_ref, acc):
        m = pl.program_id(2)

        @pl.when(m == 0)
        def _():
            acc[...] = jnp.zeros_like(acc)

        acc[...] += lax.dot_general(a_ref[...], g_ref[...], (((0,), (0,)), ((), ())), preferred_element_type=F32)

        @pl.when(m == nm - 1)
        def _():
            o_ref[...] = acc[...].astype(o_ref.dtype)

    return pl.pallas_call(
        body, name=name, grid=(K // TK, S * nj, nm),
        in_specs=[pl.BlockSpec((TM, TK), lambda k, n, m: (m, k)), pl.BlockSpec((TM, TN), lambda k, n, m: (m, n))],
        out_specs=pl.BlockSpec((None, TK, TN), lambda k, n, m: (n // nj, k, n % nj)),
        out_shape=jax.ShapeDtypeStruct((S, K, Ns), BF),
        scratch_shapes=[pltpu.VMEM((TK, TN), F32)],
        compiler_params=_cp("parallel", "parallel", "arbitrary"))(a, g)


def _rows(TL, D):
    return pl.BlockSpec((TL, D), lambda i: (i, 0))


def _fixed(R, D):
    return pl.BlockSpec((R, D), lambda i: (0, 0))


def _rowsum8(v):
    T, D = v.shape
    return jnp.sum(v.reshape(T // 8, 8, D), axis=0)


def _norm_parts(xv):
    r = lax.rsqrt(jnp.mean(xv * xv, axis=-1, keepdims=True) + RMS_EPS)
    return xv * r, r


def norm_mod(x, gamma, mods, k_shift, out_dtype, name):
    L, D = x.shape
    TL = _tile(L, (512, 256))

    def body(x_ref, g_ref, m_ref, o_ref):
        xn, _ = _norm_parts(x_ref[...])
        sh, sc = m_ref[k_shift:k_shift + 1, :], m_ref[k_shift + 1:k_shift + 2, :]
        o_ref[...] = ((xn * g_ref[...]) * (1.0 + sc) + sh).astype(o_ref.dtype)

    return pl.pallas_call(body, name=name, grid=(L // TL,),
                          in_specs=[_rows(TL, D), _fixed(1, D), _fixed(6, D)], out_specs=_rows(TL, D),
                          out_shape=jax.ShapeDtypeStruct((L, D), out_dtype), compiler_params=_cp("parallel"))(x, gamma, mods)


def norm_bwd(dh, x, dres, gamma, mods, k_shift, name):
    L, D = x.shape
    TL = _tile(L, (512, 256))

    def body(dh_ref, x_ref, dr_ref, g_ref, m_ref, dx_ref, s_ref, acc):
        i = pl.program_id(0)

        @pl.when(i == 0)
        def _():
            acc[...] = jnp.zeros_like(acc)

        xn, r = _norm_parts(x_ref[...])
        dh_v = dh_ref[...].astype(F32)
        gam = g_ref[...]
        sc = m_ref[k_shift + 1:k_shift + 2, :]
        dn = dh_v * (1.0 + sc)
        dxn = dn * gam
        dx_ref[...] = dr_ref[...] + r * (dxn - xn * jnp.mean(dxn * xn, axis=-1, keepdims=True))
        acc[0] += _rowsum8(dh_v)
        acc[1] += _rowsum8(dh_v * (xn * gam))
        acc[2] += _rowsum8(dn * xn)

        @pl.when(i == pl.num_programs(0) - 1)
        def _():
            s_ref[...] = jnp.zeros_like(s_ref)
            for q in range(3):
                s_ref[q:q + 1, :] = jnp.sum(acc[q], axis=0, keepdims=True)

    return pl.pallas_call(
        body, name=name, grid=(L // TL,),
        in_specs=[_rows(TL, D), _rows(TL, D), _rows(TL, D), _fixed(1, D), _fixed(6, D)],
        out_specs=[_rows(TL, D), _fixed(8, D)],
        out_shape=[jax.ShapeDtypeStruct((L, D), F32), jax.ShapeDtypeStruct((8, D), F32)],
        scratch_shapes=[pltpu.VMEM((3, 8, D), F32)], compiler_params=_cp("arbitrary"))(dh, x, dres, gamma, mods)


def gate_bwd(dx, f, mods, k_gate, name):
    L, D = dx.shape
    TL = _tile(L, (512, 256))

    def body(dx_ref, f_ref, m_ref, o_ref, s_ref, acc):
        i = pl.program_id(0)

        @pl.when(i == 0)
        def _():
            acc[...] = jnp.zeros_like(acc)

        dxv = dx_ref[...]
        o_ref[...] = (dxv * m_ref[k_gate:k_gate + 1, :]).astype(o_ref.dtype)
        acc[...] += _rowsum8(dxv * f_ref[...].astype(F32))

        @pl.when(i == pl.num_programs(0) - 1)
        def _():
            s_ref[...] = jnp.zeros_like(s_ref)
            s_ref[0:1, :] = jnp.sum(acc[...], axis=0, keepdims=True)

    return pl.pallas_call(
        body, name=name, grid=(L // TL,), in_specs=[_rows(TL, D), _rows(TL, D), _fixed(6, D)],
        out_specs=[_rows(TL, D), _fixed(8, D)],
        out_shape=[jax.ShapeDtypeStruct((L, D), BF), jax.ShapeDtypeStruct((8, D), F32)],
        scratch_shapes=[pltpu.VMEM((8, D), F32)], compiler_params=_cp("arbitrary"))(dx, f, mods)


def swiglu_act(gu, name):
    L, F2 = gu.shape
    F = F2 // 2
    TL = _tile(L, (256,))

    def body(gu_ref, o_ref):
        g = gu_ref[:, :F].astype(F32)
        u = gu_ref[:, F:].astype(F32)
        o_ref[...] = (g * jax.nn.sigmoid(g) * u).astype(o_ref.dtype)

    return pl.pallas_call(body, name=name, grid=(L // TL,), in_specs=[_rows(TL, F2)], out_specs=_rows(TL, F),
                          out_shape=jax.ShapeDtypeStruct((L, F), BF), compiler_params=_cp("parallel"))(gu)


def swiglu_bwd(da, gu, name):
    L, F2 = gu.shape
    F = F2 // 2
    TL = _tile(L, (256,))

    def body(da_ref, gu_ref, o_ref):
        g = gu_ref[:, :F].astype(F32)
        u = gu_ref[:, F:].astype(F32)
        d = da_ref[...].astype(F32)
        s = jax.nn.sigmoid(g)
        o_ref[:, :F] = (d * u * (s + g * s * (1.0 - s))).astype(o_ref.dtype)
        o_ref[:, F:] = (d * g * s).astype(o_ref.dtype)

    return pl.pallas_call(body, name=name, grid=(L // TL,), in_specs=[_rows(TL, F), _rows(TL, F2)],
                          out_specs=_rows(TL, F2), out_shape=jax.ShapeDtypeStruct((L, F2), BF),
                          compiler_params=_cp("parallel"))(da, gu)


def glu_res(o, x, mods, k_gate, name):
    L, D = x.shape
    TL = _tile(L, (512, 256))

    def body(o_ref, x_ref, m_ref, mix_ref, y_ref):
        mix = o_ref[:, :D].astype(F32) * jax.nn.sigmoid(o_ref[:, D:].astype(F32))
        mix_ref[...] = mix.astype(mix_ref.dtype)
        y_ref[...] = x_ref[...] + m_ref[k_gate:k_gate + 1, :] * mix

    return pl.pallas_call(
        body, name=name, grid=(L // TL,), in_specs=[_rows(TL, 2 * D), _rows(TL, D), _fixed(6, D)],
        out_specs=[_rows(TL, D), _rows(TL, D)],
        out_shape=[jax.ShapeDtypeStruct((L, D), BF), jax.ShapeDtypeStruct((L, D), F32)],
        compiler_params=_cp("parallel"))(o, x, mods)


def glu_bwd(dmix, o, name):
    L, D2 = o.shape
    D = D2 // 2
    TL = _tile(L, (512, 256))

    def body(d_ref, o_ref, do_ref):
        d = d_ref[...].astype(F32)
        val = o_ref[:, :D].astype(F32)
        s = jax.nn.sigmoid(o_ref[:, D:].astype(F32))
        do_ref[:, :D] = (d * s).astype(do_ref.dtype)
        do_ref[:, D:] = (d * val * s * (1.0 - s)).astype(do_ref.dtype)

    return pl.pallas_call(body, name=name, grid=(L // TL,), in_specs=[_rows(TL, D), _rows(TL, D2)],
                          out_specs=_rows(TL, D2), out_shape=jax.ShapeDtypeStruct((L, D2), BF),
                          compiler_params=_cp("parallel"))(dmix, o)


def final_loss(x, target, gamma, name):
    L, D = x.shape
    TL = _tile(L, (512, 256))

    def body(x_ref, t_ref, g_ref, l_ref, dx_ref, s_ref, acc, lacc):
        i = pl.program_id(0)

        @pl.when(i == 0)
        def _():
            acc[...] = jnp.zeros_like(acc)
            lacc[...] = jnp.zeros_like(lacc)

        xn, r = _norm_parts(x_ref[...])
        gam = g_ref[...]
        e = xn * gam - t_ref[...]
        lacc[...] += jnp.sum(0.5 * jnp.mean(e * e, axis=-1, keepdims=True), axis=0, keepdims=True)
        dy = e * (1.0 / D)
        dxn = dy * gam
        dx_ref[...] = r * (dxn - xn * jnp.mean(dxn * xn, axis=-1, keepdims=True))
        acc[...] += _rowsum8(dy * xn)

        @pl.when(i == pl.num_programs(0) - 1)
        def _():
            s_ref[...] = jnp.zeros_like(s_ref)
            s_ref[0:1, :] = jnp.sum(acc[...], axis=0, keepdims=True)
            l_ref[...] = jnp.broadcast_to(lacc[...], l_ref.shape)

    return pl.pallas_call(
        body, name=name, grid=(L // TL,), in_specs=[_rows(TL, D), _rows(TL, D), _fixed(1, D)],
        out_specs=[_fixed(8, 128), _rows(TL, D), _fixed(8, D)],
        out_shape=[jax.ShapeDtypeStruct((8, 128), F32), jax.ShapeDtypeStruct((L, D), F32),
                   jax.ShapeDtypeStruct((8, D), F32)],
        scratch_shapes=[pltpu.VMEM((8, D), F32), pltpu.VMEM((1, 1), F32)],
        compiler_params=_cp("arbitrary"))(x, target, gamma)


def _col(L, TC, off):
    return pl.BlockSpec((L, TC), lambda j: (0, off + j))


def _shift_down(v, k, row):
    return jnp.where(row >= k, pltpu.roll(v, k, 0), 0.0)


def _shift_up(v, k, row, L):
    return jnp.where(row < L - k, pltpu.roll(v, L - k, 0), 0.0)


def conv_fwd(p, w, name):
    L, D3 = p.shape
    D = D3 // 3
    TC = _tile(D, (128,))
    nc = D // TC

    def body(b_ref, c_ref, v_ref, w_ref, o_ref):
        row = lax.broadcasted_iota(jnp.int32, (L, TC), 0)
        cv = c_ref[...].astype(F32) * v_ref[...].astype(F32)
        conv = w_ref[2:3, :] * cv + w_ref[1:2, :] * _shift_down(cv, 1, row) + w_ref[0:1, :] * _shift_down(cv, 2, row)
        o_ref[...] = (b_ref[...].astype(F32) * conv).astype(o_ref.dtype)

    return pl.pallas_call(
        body, name=name, grid=(nc,),
        in_specs=[_col(L, TC, 0), _col(L, TC, nc), _col(L, TC, 2 * nc), pl.BlockSpec((3, TC), lambda j: (0, j))],
        out_specs=_col(L, TC, 0), out_shape=jax.ShapeDtypeStruct((L, D), BF), compiler_params=_cp("parallel"))(p, p, p, w)


def conv_bwd(dm, p, w, name):
    L, D3 = p.shape
    D = D3 // 3
    TC = _tile(D, (128,))
    nc = D // TC

    def body(dm_ref, b_ref, c_ref, v_ref, w_ref, db_ref, dc_ref, dv_ref, dw_ref):
        row = lax.broadcasted_iota(jnp.int32, (L, TC), 0)
        cg, vv = c_ref[...].astype(F32), v_ref[...].astype(F32)
        cv = cg * vv
        cv1, cv2 = _shift_down(cv, 1, row), _shift_down(cv, 2, row)
        conv = w_ref[2:3, :] * cv + w_ref[1:2, :] * cv1 + w_ref[0:1, :] * cv2
        dmv = dm_ref[...].astype(F32)
        db_ref[...] = (dmv * conv).astype(db_ref.dtype)
        dconv = dmv * b_ref[...].astype(F32)
        dcv = (w_ref[2:3, :] * dconv + w_ref[1:2, :] * _shift_up(dconv, 1, row, L)
               + w_ref[0:1, :] * _shift_up(dconv, 2, row, L))
        dc_ref[...] = (dcv * vv).astype(dc_ref.dtype)
        dv_ref[...] = (dcv * cg).astype(dv_ref.dtype)
        dw_ref[...] = jnp.zeros_like(dw_ref)
        dw_ref[0:1, :] = jnp.sum(dconv * cv2, axis=0, keepdims=True)
        dw_ref[1:2, :] = jnp.sum(dconv * cv1, axis=0, keepdims=True)
        dw_ref[2:3, :] = jnp.sum(dconv * cv, axis=0, keepdims=True)

    one = jax.ShapeDtypeStruct((L, D), BF)
    return pl.pallas_call(
        body, name=name, grid=(nc,),
        in_specs=[_col(L, TC, 0), _col(L, TC, 0), _col(L, TC, nc), _col(L, TC, 2 * nc),
                  pl.BlockSpec((3, TC), lambda j: (0, j))],
        out_specs=[_col(L, TC, 0), _col(L, TC, 0), _col(L, TC, 0), pl.BlockSpec((8, TC), lambda j: (0, j))],
        out_shape=[one, one, one, jax.ShapeDtypeStruct((8, D), F32)],
        compiler_params=_cp("parallel"))(dm, p, p, p, w)


def _gelu(y):
    return 0.5 * y * (1.0 + jnp.tanh(GELU_C * (y + GELU_A * y * y * y)))


def _gelu_grad(y):
    th = jnp.tanh(GELU_C * (y + GELU_A * y * y * y))
    return 0.5 * (1.0 + th) + 0.5 * y * (1.0 - th * th) * GELU_C * (1.0 + 3.0 * GELU_A * y * y)


def _cmul_add(br, bi, ar, ai, sr, si):
    return br + ar * sr - ai * si, bi + ar * si + ai * sr


def s5_fwd(h, bblk, cblk, pw, dvec, name):
    L, D = h.shape
    nkb, KB, W2 = bblk.shape
    W = W2 // 2
    TL = _tile(L, (512, 256))
    ngrp = TL // 8

    def body(h_ref, b_ref, c_ref, pw_ref, d_ref, s_ref, y_ref, z_ref, bu, carry):
        t = pl.program_id(1)

        @pl.when(t == 0)
        def _():
            carry[...] = jnp.zeros_like(carry)

        hv = h_ref[...]
        bu[...] = jnp.dot(hv.astype(BF), b_ref[...], preferred_element_type=F32)

        def grp(j, cr_ci):
            cr, ci = cr_ci
            r0 = pl.multiple_of(j * 8, 8)
            br, bi = bu[pl.ds(r0, 8), :W], bu[pl.ds(r0, 8), W:]
            for k, off in ((1, 0), (2, 8), (4, 16)):
                br, bi = _cmul_add(br, bi, pw_ref[off:off + 8, :W], pw_ref[off:off + 8, W:],
                                   pltpu.roll(br, k, 0), pltpu.roll(bi, k, 0))
            xr, xi = _cmul_add(br, bi, pw_ref[24:32, :W], pw_ref[24:32, W:], cr, ci)
            bu[pl.ds(r0, 8), :W] = xr
            bu[pl.ds(r0, 8), W:] = xi
            return jnp.broadcast_to(xr[7:8], (8, W)), jnp.broadcast_to(xi[7:8], (8, W))

        cr, ci = lax.fori_loop(0, ngrp, grp, (carry[0], carry[1]))
        carry[0] = cr
        carry[1] = ci
        sv = bu[...]
        s_ref[...] = sv
        y = jnp.dot(sv.astype(BF), c_ref[...], preferred_element_type=F32) + d_ref[...] * hv
        y_ref[...] = y
        z_ref[...] = _gelu(y).astype(z_ref.dtype)

    blk = lambda kb, t: (t, kb)
    return pl.pallas_call(
        body, name=name, grid=(nkb, L // TL),
        in_specs=[pl.BlockSpec((TL, KB), blk), pl.BlockSpec((None, KB, W2), lambda kb, t: (kb, 0, 0)),
                  pl.BlockSpec((None, W2, KB), lambda kb, t: (kb, 0, 0)),
                  pl.BlockSpec((None, 32, W2), lambda kb, t: (kb, 0, 0)), pl.BlockSpec((1, KB), lambda kb, t: (0, kb))],
        out_specs=[pl.BlockSpec((TL, W2), blk), pl.BlockSpec((TL, KB), blk), pl.BlockSpec((TL, KB), blk)],
        out_shape=[jax.ShapeDtypeStruct((L, nkb * W2), F32), jax.ShapeDtypeStruct((L, D), F32),
                   jax.ShapeDtypeStruct((L, D), BF)],
        scratch_shapes=[pltpu.VMEM((TL, W2), F32), pltpu.VMEM((2, 8, W), F32)],
        compiler_params=_cp("parallel", "arbitrary"))(h, bblk, cblk, pw, dvec)


def s5_bwd(dz, y, h, s, ct, bt, pwr, dvec, name):
    L, D = h.shape
    nkb, KB, W2 = ct.shape
    W = W2 // 2
    TL = _tile(L, (512, 256))
    ngrp = TL // 8
    nt = L // TL

    def body(dz_ref, y_ref, h_ref, s_ref, sp_ref, ct_ref, bt_ref, pw_ref, d_ref,
             dh_ref, dd_ref, da_ref, db_ref, dc_ref, g, carry):
        t = pl.program_id(1)

        @pl.when(t == 0)
        def _():
            carry[...] = jnp.zeros_like(carry)
            dd_ref[...] = jnp.zeros_like(dd_ref)
            da_ref[...] = jnp.zeros_like(da_ref)
            db_ref[...] = jnp.zeros_like(db_ref)
            dc_ref[...] = jnp.zeros_like(dc_ref)

        hv = h_ref[...]
        dy = dz_ref[...].astype(F32) * _gelu_grad(y_ref[...])
        dd_ref[...] += _rowsum8(dy * hv)
        dyb = dy.astype(BF)
        g[...] = jnp.dot(dyb, ct_ref[...], preferred_element_type=F32)

        def grp(jj, cr_ci):
            cr, ci = cr_ci
            r0 = pl.multiple_of((ngrp - 1 - jj) * 8, 8)
            gr, gi = g[pl.ds(r0, 8), :W], g[pl.ds(r0, 8), W:]
            for k, off in ((1, 0), (2, 8), (4, 16)):
                gr, gi = _cmul_add(gr, gi, pw_ref[off:off + 8, :W], pw_ref[off:off + 8, W:],
                                   pltpu.roll(gr, 8 - k, 0), pltpu.roll(gi, 8 - k, 0))
            gr, gi = _cmul_add(gr, gi, pw_ref[24:32, :W], pw_ref[24:32, W:], cr, ci)
            g[pl.ds(r0, 8), :W] = gr
            g[pl.ds(r0, 8), W:] = gi
            return jnp.broadcast_to(gr[0:1], (8, W)), jnp.broadcast_to(gi[0:1], (8, W))

        cr, ci = lax.fori_loop(0, ngrp, grp, (carry[0], carry[1]))
        carry[0] = cr
        carry[1] = ci

        first = lax.broadcasted_iota(jnp.int32, (8, W), 0) == 0
        live = jnp.where(t == nt - 1, 0.0, 1.0)

        def prev_rows(cur_r, cur_i, before_r, before_i):
            return (jnp.where(first, pltpu.roll(before_r, 1, 0), pltpu.roll(cur_r, 1, 0)),
                    jnp.where(first, pltpu.roll(before_i, 1, 0), pltpu.roll(cur_i, 1, 0)))

        def dab(j, acc):
            ar, ai = acc
            r0 = pl.multiple_of(j * 8, 8)
            rb = pl.multiple_of(j * 8 - 8, 8)
            pr, pi = prev_rows(s_ref[pl.ds(r0, 8), :W], s_ref[pl.ds(r0, 8), W:],
                               s_ref[pl.ds(rb, 8), :W], s_ref[pl.ds(rb, 8), W:])
            gr, gi = g[pl.ds(r0, 8), :W], g[pl.ds(r0, 8), W:]
            return ar + pr * gr + pi * gi, ai + pr * gi - pi * gr

        pr, pi = prev_rows(s_ref[0:8, :W], s_ref[0:8, W:], sp_ref[:, :W] * live, sp_ref[:, W:] * live)
        gr, gi = g[0:8, :W], g[0:8, W:]
        ar, ai = lax.fori_loop(1, ngrp, dab, (pr * gr + pi * gi, pr * gi - pi * gr))
        da_ref[:, :W] += ar
        da_ref[:, W:] += ai

        gb = g[...].astype(BF)
        dh_ref[...] = dy * d_ref[...] + jnp.dot(gb, bt_ref[...], preferred_element_type=F32)
        tn = (((0,), (0,)), ((), ()))
        db_ref[...] += lax.dot_general(hv.astype(BF), gb, tn, preferred_element_type=F32)
        dc_ref[...] += lax.dot_general(dyb, s_ref[...].astype(BF), tn, preferred_element_type=F32)

    rev = lambda kb, t: (nt - 1 - t, kb)
    grp8 = TL // 8
    prev = lambda kb, t: (jnp.maximum((nt - 1 - t) * grp8 - 1, 0), kb)
    per_kb = lambda kb, t: (kb, 0, 0)
    return pl.pallas_call(
        body, name=name, grid=(nkb, nt),
        in_specs=[pl.BlockSpec((TL, KB), rev), pl.BlockSpec((TL, KB), rev), pl.BlockSpec((TL, KB), rev),
                  pl.BlockSpec((TL, W2), rev), pl.BlockSpec((8, W2), prev),
                  pl.BlockSpec((None, KB, W2), per_kb), pl.BlockSpec((None, W2, KB), per_kb),
                  pl.BlockSpec((None, 32, W2), per_kb), pl.BlockSpec((1, KB), lambda kb, t: (0, kb))],
        out_specs=[pl.BlockSpec((TL, KB), rev), pl.BlockSpec((8, KB), lambda kb, t: (0, kb)),
                   pl.BlockSpec((None, 8, W2), per_kb), pl.BlockSpec((None, KB, W2), per_kb),
                   pl.BlockSpec((None, KB, W2), per_kb)],
        out_shape=[jax.ShapeDtypeStruct((L, D), F32), jax.ShapeDtypeStruct((8, D), F32),
                   jax.ShapeDtypeStruct((nkb, 8, W2), F32), jax.ShapeDtypeStruct((nkb, KB, W2), F32),
                   jax.ShapeDtypeStruct((nkb, KB, W2), F32)],
        scratch_shapes=[pltpu.VMEM((TL, W2), F32), pltpu.VMEM((2, 8, W), F32)],
        compiler_params=_cp("parallel", "arbitrary"))(dz, y, h, s, s, ct, bt, pwr, dvec)


def _discretise(a_re, a_im, log_step, b_re, b_im):
    lr = jnp.minimum(a_re, -1e-4)
    li = a_im
    dt = jnp.exp(log_step)[:, None]
    mag = jnp.exp(lr * dt)
    abr = mag * jnp.cos(li * dt)
    abi = mag * jnp.sin(li * dt)
    den = lr * lr + li * li
    qr = ((abr - 1.0) * lr + abi * li) / den
    qi = (abi * lr - (abr - 1.0) * li) / den
    bbar_re = qr[..., None] * b_re - qi[..., None] * b_im
    bbar_im = qr[..., None] * b_im + qi[..., None] * b_re
    return abr, abi, bbar_re, bbar_im


def _block_diag(m_re, m_im, nkb):
    G, H, P = m_re.shape
    GL = G // nkb
    eye = jnp.eye(GL, dtype=BF)
    m = jnp.stack([m_re, m_im], axis=2).astype(BF).reshape(nkb, GL, H, 2, P)
    blk = (m[:, :, :, :, None, :] * eye[None, :, None, None, :, None]).reshape(nkb, GL * H, 2 * GL * P)
    mt = jnp.transpose(m, (0, 3, 4, 1, 2))
    blk_t = (mt[:, :, None] * eye[None, None, :, None, :, None]).reshape(nkb, 2 * GL * P, GL * H)
    return blk, blk_t


def _block_diag_extract(blk, G):
    nkb, R, C = blk.shape
    GL = G // nkb
    H, P = R // GL, C // (2 * GL)
    eye = jnp.eye(GL, dtype=blk.dtype)[None, :, None, None, :, None]
    m = jnp.sum(blk.reshape(nkb, GL, H, 2, GL, P) * eye, axis=4)
    return m[:, :, :, 0].reshape(G, H, P), m[:, :, :, 1].reshape(G, H, P)


def _scan_powers(a_re, a_im, log_step, nkb, conj):
    G, P = a_re.shape
    lr = jnp.minimum(a_re, -1e-4)
    dt = jnp.exp(log_step)[:, None]
    n = jnp.arange(1, 9, dtype=F32)[:, None, None]
    mag = jnp.exp(n * (lr * dt)[None])
    ang = n * (a_im * dt)[None]
    pr, pi = mag * jnp.cos(ang), mag * jnp.sin(ang)
    if conj:
        pi = -pi
    row = jnp.arange(8)[:, None, None]

    def table(q):
        out = []
        for k in (1, 2, 4):
            keep = (row <= 7 - k) if conj else (row >= k)
            out.append(jnp.where(keep, q[k - 1][None], 0.0))
        out.append(q[::-1] if conj else q)
        return jnp.concatenate(out, axis=0)

    GL = G // nkb
    t = jnp.stack([table(pr), table(pi)], axis=1)
    t = t.reshape(32, 2, nkb, GL * P).transpose(2, 0, 1, 3)
    return t.reshape(nkb, 32, 2 * GL * P)


def ada_mods(c_all, w_ada, b_sh, name):
    nl, D, NA = w_ada.shape

    def body(c_ref, w_ref, b_ref, o_ref):
        cv = c_ref[...]
        act = cv * jax.nn.sigmoid(cv)
        o_ref[...] = jnp.dot(act, w_ref[...], preferred_element_type=F32, precision=lax.Precision.HIGHEST) + b_ref[...]

    return pl.pallas_call(
        body, name=name, grid=(nl,),
        in_specs=[pl.BlockSpec((8, D), lambda i: (0, 0)), pl.BlockSpec((None, D, NA), lambda i: (i, 0, 0)),
                  pl.BlockSpec((None, 1, NA), lambda i: (i, 0, 0))],
        out_specs=pl.BlockSpec((None, 8, NA), lambda i: (i, 0, 0)),
        out_shape=jax.ShapeDtypeStruct((nl, 8, NA), F32), compiler_params=_cp("parallel"))(c_all, w_ada, b_sh)


def _adamw(w, g, m, v):
    m = ADAM_B1 * m + (1.0 - ADAM_B1) * g
    v = ADAM_B2 * v + (1.0 - ADAM_B2) * (g * g)
    m_hat = m / (1.0 - ADAM_B1 ** ADAM_STEP)
    v_hat = v / (1.0 - ADAM_B2 ** ADAM_STEP)
    return -ADAM_LR * (m_hat / (jnp.sqrt(v_hat) + ADAM_EPS) + ADAM_WD * w), m, v


def _adam_rows(R, C):
    cap = max(8, (256 * 1024) // C)
    for t in range(min(R, cap), 0, -1):
        if R % t == 0 and (t % 8 == 0 or t == R):
            return t
    return R


def adamw_ada(c_t, dm, w, m, v, name):
    nl, D, NA = w.shape
    TK = _tile(D, (128,))

    def body(c_ref, dm_ref, w_ref, m_ref, v_ref, g_ref, d_ref, nm_ref, nv_ref):
        cv = c_ref[...]
        act = cv * jax.nn.sigmoid(cv)
        g = act[:, 0:1] * dm_ref[0:1, :]
        for b in range(1, 8):
            g = g + act[:, b:b + 1] * dm_ref[b:b + 1, :]
        g_ref[...] = g
        d_ref[...], nm_ref[...], nv_ref[...] = _adamw(w_ref[...], g, m_ref[...], v_ref[...])

    big = pl.BlockSpec((None, TK, NA), lambda i, k: (i, k, 0))
    shape = jax.ShapeDtypeStruct(w.shape, F32)
    return pl.pallas_call(
        body, name=name, grid=(nl, D // TK),
        in_specs=[pl.BlockSpec((TK, 8), lambda i, k: (k, 0)), pl.BlockSpec((None, 8, NA), lambda i, k: (i, 0, 0)),
                  big, big, big],
        out_specs=[big] * 4, out_shape=[shape] * 4, compiler_params=_cp("parallel", "parallel"))(c_t, dm, w, m, v)


def adamw_sharded(w, m, v, ga, gb, name):
    nl, R, C = w.shape
    TR = _adam_rows(R, C)

    def body(w_ref, m_ref, v_ref, a_ref, b_ref, g_ref, d_ref, nm_ref, nv_ref):
        g = a_ref[...] + b_ref[...]
        g_ref[...] = g
        d_ref[...], nm_ref[...], nv_ref[...] = _adamw(w_ref[...], g, m_ref[...], v_ref[...])

    big = pl.BlockSpec((None, TR, C), lambda i, r: (i, r, 0))
    shape = jax.ShapeDtypeStruct(w.shape, F32)
    return pl.pallas_call(
        body, name=name, grid=(nl, R // TR), in_specs=[big] * 5,
        out_specs=[big] * 4, out_shape=[shape] * 4, compiler_params=_cp("parallel", "parallel"))(w, m, v, ga, gb)


def adamw_slab(g, w, m, v, name):
    R, C = g.shape
    TR = _tile(R, (160, 80, 40, 8))

    def body(g_ref, w_ref, m_ref, v_ref, d_ref, nm_ref, nv_ref):
        d_ref[...], nm_ref[...], nv_ref[...] = _adamw(w_ref[...], g_ref[...], m_ref[...], v_ref[...])

    big = pl.BlockSpec((TR, C), lambda r: (r, 0))
    shape = jax.ShapeDtypeStruct((R, C), F32)
    return pl.pallas_call(
        body, name=name, grid=(R // TR,), in_specs=[big] * 4,
        out_specs=[big] * 3, out_shape=[shape] * 3, compiler_params=_cp("parallel"))(g, w, m, v)


def adamw_plain(w, m, v, g, name):
    def body(w_ref, m_ref, v_ref, g_ref, d_ref, nm_ref, nv_ref):
        d_ref[...], nm_ref[...], nv_ref[...] = _adamw(w_ref[...], g_ref[...], m_ref[...], v_ref[...])

    shape = jax.ShapeDtypeStruct(w.shape, F32)
    return pl.pallas_call(body, name=name, out_shape=[shape] * 3,
                          compiler_params=pltpu.CompilerParams(vmem_limit_bytes=VMEM_LIMIT))(w, m, v, g)


def _slab_rows(a):
    n = a.size
    rows = -(-n // SLAB_W)
    return -(-rows // 8) * 8


def _pack(arrs, pad_rows_to=0):
    out = []
    for a in arrs:
        rows = _slab_rows(a)
        flat = a.reshape(-1).astype(F32)
        flat = jnp.pad(flat, (0, rows * SLAB_W - flat.shape[0]))
        out.append(flat.reshape(rows, SLAB_W))
    total = sum(o.shape[0] for o in out)
    if pad_rows_to and total % pad_rows_to:
        out.append(jnp.zeros((pad_rows_to - total % pad_rows_to, SLAB_W), F32))
    return jnp.concatenate(out, axis=0)


def _unpack(slab, like):
    out, r = [], 0
    for a in like:
        rows = _slab_rows(a)
        out.append(slab[r:r + rows].reshape(-1)[:a.size].reshape(a.shape))
        r += rows
    return out


WEIGHTS = ['norm1_g', 'norm2_g', 'w_ada', 'b_ada', 'ssm_a_re', 'ssm_a_im', 'ssm_log_step', 'ssm_b_re', 'ssm_b_im',
           'ssm_c_re', 'ssm_c_im', 'ssm_d', 'ssm_w_out', 'conv_w_in', 'conv_w', 'conv_w_out', 'w_ffn_in',
           'w_ffn_out', 'final_g']
SLAB = ['norm1_g', 'norm2_g', 'b_ada', 'ssm_a_re', 'ssm_a_im', 'ssm_log_step', 'ssm_b_re', 'ssm_b_im', 'ssm_c_re',
        'ssm_c_im', 'ssm_d', 'final_g']
SHARDED = ['ssm_w_out', 'conv_w_in', 'conv_w_out', 'w_ffn_in', 'w_ffn_out']


def kernel(x, c, norm1_g, norm2_g, w_ada, b_ada, ssm_a_re, ssm_a_im, ssm_log_step, ssm_b_re, ssm_b_im, ssm_c_re, ssm_c_im, ssm_d, ssm_w_out, conv_w_in, conv_w, conv_w_out, w_ffn_in, w_ffn_out, final_g, loss_target, m_norm1_g, m_norm2_g, m_w_ada, m_b_ada, m_ssm_a_re, m_ssm_a_im, m_ssm_log_step, m_ssm_b_re, m_ssm_b_im, m_ssm_c_re, m_ssm_c_im, m_ssm_d, m_ssm_w_out, m_conv_w_in, m_conv_w, m_conv_w_out, m_w_ffn_in, m_w_ffn_out, m_final_g, v_norm1_g, v_norm2_g, v_w_ada, v_b_ada, v_ssm_a_re, v_ssm_a_im, v_ssm_log_step, v_ssm_b_re, v_ssm_b_im, v_ssm_c_re, v_ssm_c_im, v_ssm_d, v_ssm_w_out, v_conv_w_in, v_conv_w, v_conv_w_out, v_w_ffn_in, v_w_ffn_out, v_final_g):
    given = dict(locals())
    W = {n: given[n] for n in WEIGHTS}
    Mo = {n: given["m_" + n] for n in WEIGHTS}
    Vo = {n: given["v_" + n] for n in WEIGHTS}

    xs = x[0]
    tgt = loss_target[0]
    L, D = xs.shape
    nlayer = norm1_g.shape[0]
    NA = w_ada.shape[2]
    G = ssm_a_re.shape[1]
    nkb = D // S5_BLOCK
    ax, ay, ac = _axes()
    me = 4 * ax + 2 * ay + ac
    chip = 2 * ax + ay

    c_all = gather8(jnp.broadcast_to(c, (8, D)), "gather_c")[:, 0, :]
    b_sh = lax.dynamic_slice_in_dim(b_ada, chip * NA, NA, axis=1)[:, None, :]
    mods_part = ada_mods(c_all, w_ada, b_sh, "ada_mods")
    mg = gather8(mods_part.reshape(nlayer * 8, NA), "gather_mods")
    mg = mg.reshape(N_CHIP, 2, nlayer, 8, NA)[:, 0]
    mods_all = lax.dynamic_index_in_dim(mg, me, axis=2, keepdims=False)
    mods_all = jnp.transpose(mods_all, (1, 0, 2)).reshape(nlayer, 6, D)

    cw_parts = gather8(_pack([conv_w]), "gather_conv_w")
    nconv = conv_w.shape[0]
    cw_full = jnp.stack([_unpack(cw_parts[2 * q], [conv_w])[0] for q in range(N_CHIP)], axis=2)
    cw_full = cw_full.reshape(nconv, 3, D)

    use = []
    for i in range(nlayer):
        use += [("ssm_w_out", i // 2, i)] if i % 2 == 0 else [("conv_w_in", i // 2, i), ("conv_w_out", i // 2, i)]
        use += [("w_ffn_in", i, i), ("w_ffn_out", i, i)]
    g_sems, g_srcs, g_lands = gather_start([W[n][j].astype(BF) for n, j, _ in use], cw_full + mods_all[0, 0:3],
                                           "gather_start")

    def layer_weights(i, after):
        idx = [a for a, (_, _, li) in enumerate(use) if li == i]
        got = gather_wait(g_sems, g_srcs, g_lands, idx, after, "gather_wait%d" % i)
        return {use[a][0]: w for a, w in zip(idx, got)}

    s5 = []
    for j in range(ssm_a_re.shape[0]):
        disc, disc_vjp = jax.vjp(_discretise, ssm_a_re[j], ssm_a_im[j], ssm_log_step[j], ssm_b_re[j], ssm_b_im[j])
        _, _, bbar_re, bbar_im = disc
        bblk, bt = _block_diag(jnp.swapaxes(bbar_re, 1, 2), jnp.swapaxes(bbar_im, 1, 2), nkb)
        ct, cblk = _block_diag(ssm_c_re[j], -ssm_c_im[j], nkb)
        s5.append(dict(
            vjp=disc_vjp, bblk=bblk, bt=bt, ct=ct, cblk=cblk,
            pw=_scan_powers(ssm_a_re[j], ssm_a_im[j], ssm_log_step[j], nkb, False),
            pwr=_scan_powers(ssm_a_re[j], ssm_a_im[j], ssm_log_step[j], nkb, True)))

    saved = []
    xcur = xs
    for i in range(nlayer):
        j = i // 2
        mods = mods_all[i]
        sv = dict(x=xcur)
        if i % 2 == 0:
            h = norm_mod(xcur, norm1_g[i:i + 1], mods, 0, F32, "norm_mod_s5")
            states, yv, z = s5_fwd(h, s5[j]["bblk"], s5[j]["cblk"], s5[j]["pw"], ssm_d[j:j + 1], "s5_fwd")
            full = layer_weights(i, z)
            o = mm_nn(z, full["ssm_w_out"], BF, "mm_ssm_out")
            mix, x2 = glu_res(o, xcur, mods, 2, "glu_res")
            sv.update(h=h, states=states, y=yv, z=z, o=o)
        else:
            h = norm_mod(xcur, norm1_g[i:i + 1], mods, 0, BF, "norm_mod")
            full = layer_weights(i, h)
            p = mm_nn(h, full["conv_w_in"], BF, "mm_conv_in")
            mc = conv_fwd(p, cw_full[j], "conv_fwd")
            mix, x2 = mm_nn(mc, full["conv_w_out"].reshape(1, D, D), BF, "mm_conv_out", res=xcur, gate=mods[2:3])
            sv.update(h=h, p=p, mc=mc)
        h2 = norm_mod(x2, norm2_g[i:i + 1], mods, 3, BF, "norm_mod")
        gu = mm_nn(h2, full["w_ffn_in"], BF, "mm_ffn_in")
        act = swiglu_act(gu, "swiglu_act")
        F = act.shape[1]
        ff, x3 = mm_nn(act, full["w_ffn_out"].reshape(1, F, D), BF, "mm_ffn_out", res=x2, gate=mods[5:6])
        sv.update(mix=mix, x2=x2, h2=h2, gu=gu, act=act, ff=ff, w=full)
        saved.append(sv)
        xcur = x3

    loss_blk, dx, dfinal = final_loss(xcur, tgt, final_g[None, :], "final_loss")

    gland = {n: lax.empty((W[n].shape[0], N_CHIP) + W[n].shape[1:], BF) for n in SHARDED}
    in_flight = []
    dmods = [None] * nlayer
    dnorm1, dnorm2 = [None] * nlayer, [None] * nlayer
    dconv_w = [None] * nconv
    ds5 = [None] * ssm_a_re.shape[0]
    token = jnp.zeros((8, 128), F32)

    def send_grads(names, grads, slot, after, name):
        sems, thru, lands, tok = scatter_start([grads[n] for n in names], [gland[n] for n in names], slot, after, name)
        gland.update(zip(names, lands))
        in_flight.append((names, slot, sems, thru, name))
        return tok

    def land_grads(group, after):
        for names, slot, sems, thru, name in in_flight:
            if names[0] in group:
                got = scatter_wait(sems, thru, [gland[n] for n in names], slot, after, name.replace("scatter", "landed"))
                gland.update(zip(names, got))

    for i in reversed(range(nlayer)):
        j = i // 2
        mods = mods_all[i] + token[0:1, 0:1]
        sv = saved[i]
        full = sv["w"]
        gfull = {}
        F = sv["act"].shape[1]
        dff, dg2 = gate_bwd(dx, sv["ff"], mods, 5, "gate_bwd")
        gfull["w_ffn_out"] = mm_tn(sv["act"], dff, 1, "mm_tn_ffn_out").reshape(N_CHIP, F // N_CHIP, D)
        dact = mm_nt(dff, full["w_ffn_out"].reshape(1, F, D), BF, "mm_nt_ffn_out")
        dgu = swiglu_bwd(dact, sv["gu"], "swiglu_bwd")
        gfull["w_ffn_in"] = mm_tn(sv["h2"], dgu, N_CHIP, "mm_tn_ffn_in")
        dh2 = mm_nt(dgu, full["w_ffn_in"], F32, "mm_nt_ffn_in")
        token = send_grads(["w_ffn_out", "w_ffn_in"], gfull, [i, i], dh2, "scatter_ffn%d" % i)
        mods = mods + token[0:1, 0:1]
        dx2, s2 = norm_bwd(dh2, sv["x2"], dx, norm2_g[i:i + 1], mods, 3, "norm_bwd")
        dmix, dg1 = gate_bwd(dx2, sv["mix"], mods, 2, "gate_bwd")
        if i % 2 == 0:
            do = glu_bwd(dmix, sv["o"], "glu_bwd")
            gfull["ssm_w_out"] = mm_tn(sv["z"], do, N_CHIP, "mm_tn_ssm_out")
            dz = mm_nt(do, full["ssm_w_out"], F32, "mm_nt_ssm_out")
            dh, dd, dab, db, dc = s5_bwd(dz, sv["y"], sv["h"], sv["states"], s5[j]["ct"], s5[j]["bt"], s5[j]["pwr"],
                                         ssm_d[j:j + 1], "s5_bwd")
            ds5[j] = (dd, dab, db, dc)
        else:
            gfull["conv_w_out"] = mm_tn(sv["mc"], dmix, 1, "mm_tn_conv_out").reshape(N_CHIP, D // N_CHIP, D)
            dmc = mm_nt(dmix, full["conv_w_out"].reshape(1, D, D), BF, "mm_nt_conv_out")
            dbg, dcg, dvv, dcw = conv_bwd(dmc, sv["p"], cw_full[j], "conv_bwd")
            dp = jnp.concatenate([dbg, dcg, dvv], axis=1)
            gfull["conv_w_in"] = mm_tn(sv["h"], dp, N_CHIP, "mm_tn_conv_in")
            dh = mm_nt(dp, full["conv_w_in"], F32, "mm_nt_conv_in")
            dconv_w[j] = dcw[0:3]
        dx, s1 = norm_bwd(dh, sv["x"], dx2, norm1_g[i:i + 1], mods, 0, "norm_bwd")
        dmods[i] = jnp.concatenate([s1[0:2], dg1[0:1], s2[0:2], dg2[0:1]], axis=0).reshape(6 * D)
        dnorm1[i], dnorm2[i] = s1[2], s2[2]
        names = ["ssm_w_out"] if i % 2 == 0 else ["conv_w_out", "conv_w_in"]
        token = send_grads(names, gfull, [j] * len(names), dx, "scatter_mix%d" % i)

    small = dict(norm1_g=jnp.stack(dnorm1), norm2_g=jnp.stack(dnorm2), b_ada=jnp.stack(dmods), final_g=dfinal[0])
    per = {n: [] for n in ('ssm_a_re', 'ssm_a_im', 'ssm_log_step', 'ssm_b_re', 'ssm_b_im', 'ssm_c_re', 'ssm_c_im', 'ssm_d')}
    GL = G // nkb
    for j, (dd, dab, db, dc) in enumerate(ds5):
        dab = jnp.sum(dab, axis=1).reshape(nkb, 2, GL, SSM_STATE)
        g_abr, g_abi = dab[:, 0].reshape(G, SSM_STATE), dab[:, 1].reshape(G, SSM_STATE)
        gb_re, gb_im = _block_diag_extract(db, G)
        gc_re, gc_im = _block_diag_extract(dc, G)
        ga_re, ga_im, gls, gbr, gbi = s5[j]["vjp"]((g_abr, g_abi, jnp.swapaxes(gb_re, 1, 2), jnp.swapaxes(gb_im, 1, 2)))
        for n, val in zip(per, (ga_re, ga_im, gls, gbr, gbi, gc_re, -gc_im, jnp.sum(dd, axis=0))):
            per[n].append(val)
    small.update({n: jnp.stack(vals) for n, vals in per.items()})
    dcw_full = jnp.stack(dconv_w)

    slab_like = [W[n] for n in SLAB] + [dcw_full]
    rows64 = 8 * N_DEV
    g_slab, dm_all = reduce8(_pack([small[n] for n in SLAB] + [dcw_full], rows64), _pack([small["b_ada"]]),
                             "reduce_small")
    d_slab, m_slab, v_slab = adamw_slab(
        g_slab, _pack([W[n] for n in SLAB] + [jnp.zeros_like(dcw_full)], rows64),
        _pack([Mo[n] for n in SLAB] + [jnp.zeros_like(dcw_full)], rows64),
        _pack([Vo[n] for n in SLAB] + [jnp.ones_like(dcw_full)], rows64), "adamw_slab")
    out = {}
    for k, slab in zip(("g", "d", "m", "v"), (g_slab, d_slab, m_slab, v_slab)):
        for n, val in zip(SLAB, _unpack(slab, slab_like)):
            out[k, n] = val
    g_cw = lax.dynamic_slice_in_dim(_unpack(g_slab, slab_like)[-1], chip * conv_w.shape[2], conv_w.shape[2], axis=2)
    out["g", "conv_w"] = g_cw
    out["d", "conv_w"], out["m", "conv_w"], out["v", "conv_w"] = [
        r.reshape(conv_w.shape) for r in adamw_plain(conv_w.reshape(-1, conv_w.shape[2]), m_conv_w.reshape(-1, conv_w.shape[2]),
                                                     v_conv_w.reshape(-1, conv_w.shape[2]), g_cw.reshape(-1, conv_w.shape[2]),
                                                     "adamw_conv_w")]

    early = [n for n in SHARDED if n != "ssm_w_out"]
    land_grads(early, g_slab)
    mine = [reduce4(gland[n], "reduce4_" + n) for n in early]
    w_sems, w_srcs, w_lands, token = swap_start(mine, "swap_start")

    dm_all = dm_all.reshape(N_DEV, -1)[:, :b_ada.size].reshape(N_DEV, nlayer, N_CHIP, NA)
    dm_sh = jnp.transpose(lax.dynamic_index_in_dim(dm_all, chip, axis=2, keepdims=False), (1, 0, 2))
    res = adamw_ada(jnp.transpose(c_all) + token[0:1, 0:1], dm_sh, w_ada, m_w_ada, v_w_ada, "adamw_ada")
    out["g", "w_ada"], out["d", "w_ada"], out["m", "w_ada"], out["v", "w_ada"] = res

    theirs = swap_wait(w_sems, w_srcs, w_lands, out["g", "w_ada"], "swap_wait")
    for n, ga, gb in zip(early, mine, theirs):
        r = adamw_sharded(W[n], Mo[n], Vo[n], ga, gb, "adamw_" + n)
        out["g", n], out["d", n], out["m", n], out["v", n] = r

    land_grads(["ssm_w_out"], out["g", "w_ffn_out"])
    ga = reduce4(gland["ssm_w_out"], "reduce4_ssm_w_out")
    gb = swap_siblings([ga], "swap_siblings")[0]
    r = adamw_sharded(ssm_w_out, m_ssm_w_out, v_ssm_w_out, ga, gb, "adamw_ssm_w_out")
    out["g", "ssm_w_out"], out["d", "ssm_w_out"], out["m", "ssm_w_out"], out["v", "ssm_w_out"] = r

    loss = lax.psum(loss_blk[0, 0], ("x", "y", "c"))
    return (loss, dx[None], *[out["g", n] for n in WEIGHTS], *[out["d", n] for n in WEIGHTS],
            *[out["m", n] for n in WEIGHTS], *[out["v", n] for n in WEIGHTS])
```

```python
import functools
import math

import jax
import jax.numpy as jnp
from jax import lax
from jax.experimental import pallas as pl
from jax.experimental.pallas import tpu as pltpu

F32 = jnp.float32
BF = jnp.bfloat16
MESH = pl.DeviceIdType.MESH
ANY = pl.BlockSpec(memory_space=pl.ANY)

N_DEV = 8
N_CHIP = 4
DEPTH = 4
SSM_GROUP = 16
SSM_STATE = 64
S5_BLOCK = 256
RMS_EPS = 1e-6
ADAM_LR, ADAM_B1, ADAM_B2, ADAM_EPS, ADAM_WD, ADAM_STEP = 0.001, 0.9, 0.999, 1e-08, 0.01, 10
V7X_VMEM_BYTES = 64 * 1024 * 1024
VMEM_LIMIT = V7X_VMEM_BYTES - 12 * 1024 * 1024
SLAB_W = 1024
GELU_C = math.sqrt(2.0 / math.pi)
GELU_A = 0.044715


def _cp(*sem):
    return pltpu.CompilerParams(dimension_semantics=sem if sem else None, vmem_limit_bytes=VMEM_LIMIT)


def _tile(n, prefs):
    for p in prefs:
        if p <= n and n % p == 0:
            return p
    return n


def _axes():
    return lax.axis_index("x"), lax.axis_index("y"), lax.axis_index("c")


def _flip(v, k):
    return 1 - v if k else v


def gather8(v, name):
    R, C = v.shape

    def body(v_ref, o_ref, ssem, rsem, lsem):
        x, y, c = _axes()
        me = 4 * x + 2 * y + c
        loc = pltpu.make_async_copy(v_ref, o_ref.at[me], lsem)
        loc.start()
        copies = []
        for k in range(1, N_DEV):
            peer = (_flip(x, (k >> 2) & 1), _flip(y, (k >> 1) & 1), _flip(c, k & 1))
            cp = pltpu.make_async_remote_copy(src_ref=v_ref, dst_ref=o_ref.at[me], send_sem=ssem.at[k - 1],
                                              recv_sem=rsem.at[k - 1], device_id=peer, device_id_type=MESH)
            cp.start()
            copies.append(cp)
        for cp in copies:
            cp.wait()
        loc.wait()

    return pl.pallas_call(
        body, name=name,
        out_shape=jax.ShapeDtypeStruct((N_DEV, R, C), v.dtype),
        in_specs=[pl.BlockSpec(memory_space=pltpu.VMEM)],
        out_specs=pl.BlockSpec(memory_space=pltpu.VMEM),
        scratch_shapes=[pltpu.SemaphoreType.DMA((N_DEV - 1,)), pltpu.SemaphoreType.DMA((N_DEV - 1,)),
                        pltpu.SemaphoreType.DMA],
        compiler_params=pltpu.CompilerParams(vmem_limit_bytes=VMEM_LIMIT),
    )(v)


HBM = pl.BlockSpec(memory_space=pltpu.HBM)
SEM = pl.BlockSpec(memory_space=pltpu.SEMAPHORE)
EFFECT = pltpu.SideEffectType.DATAFLOW_SIDE_EFFECTING


def _in_hbm(a):
    return pltpu.with_memory_space_constraint(a, pltpu.HBM)


def _chip_peers(x, y, c):
    out = []
    for k in range(1, N_CHIP):
        px, py = _flip(x, k >> 1), _flip(y, k & 1)
        out.append(((px, py, c), 2 * px + py))
    return out


def gather_start(shards, after, name):
    n = len(shards)

    def body(*refs):
        src, land = refs[:n], refs[n:2 * n]
        ssem, rsem, lsem = refs[2 * n + 1:2 * n + 4]
        token = refs[-1]
        x, y, c = _axes()
        chip = 2 * x + y
        for a in range(n):
            pltpu.make_async_copy(src[a], land[a].at[chip], lsem.at[a]).start()
            for k, (peer, _) in enumerate(_chip_peers(x, y, c)):
                pltpu.make_async_remote_copy(src_ref=src[a], dst_ref=land[a].at[chip], send_sem=ssem.at[3 * a + k],
                                             recv_sem=rsem.at[3 * a + k], device_id=peer, device_id_type=MESH).start()
        token[...] = jnp.zeros_like(token)

    lands = [lax.empty((N_CHIP,) + s.shape, s.dtype) for s in shards]
    out_shape = ([pltpu.SemaphoreType.DMA((3 * n,)), pltpu.SemaphoreType.DMA((3 * n,)), pltpu.SemaphoreType.DMA((n,))]
                 + [pltpu.HBM(s.shape, s.dtype) for s in shards] + [pltpu.HBM(l.shape, l.dtype) for l in lands]
                 + [jax.ShapeDtypeStruct((8, 128), F32)])
    res = pl.pallas_call(
        body, name=name, out_shape=out_shape, in_specs=[HBM] * (2 * n) + [ANY],
        out_specs=[SEM, SEM, SEM] + [HBM] * (2 * n) + [pl.BlockSpec(memory_space=pltpu.VMEM)],
        input_output_aliases={a: 3 + a for a in range(2 * n)},
        compiler_params=pltpu.CompilerParams(has_side_effects=EFFECT),
    )(*[_in_hbm(s) for s in shards], *[_in_hbm(l) for l in lands], after)
    return tuple(res[:3]), list(res[3:3 + n]), list(res[3 + n:3 + 2 * n]), res[-1]


def gather_wait(sems, srcs, lands, idx, after, name):
    m = len(idx)

    def body(*refs):
        src, land = refs[:m], refs[m:2 * m]
        ssem, rsem, lsem = refs[2 * m:2 * m + 3]
        x, y, c = _axes()
        chip = 2 * x + y
        for j, a in enumerate(idx):
            for k, (peer, pchip) in enumerate(_chip_peers(x, y, c)):
                cp = pltpu.make_async_remote_copy(src_ref=src[j], dst_ref=land[j].at[pchip], send_sem=ssem.at[3 * a + k],
                                                  recv_sem=rsem.at[3 * a + k], device_id=peer, device_id_type=MESH)
                cp.wait_send()
                cp.wait_recv()
            pltpu.make_async_copy(src[j], land[j].at[chip], lsem.at[a]).wait()

    s_in = [srcs[a] for a in idx]
    l_in = [lands[a] for a in idx]
    res = pl.pallas_call(
        body, name=name,
        out_shape=[pltpu.HBM(s.shape, s.dtype) for s in s_in] + [pltpu.HBM(l.shape, l.dtype) for l in l_in],
        in_specs=[HBM] * (2 * m) + [SEM, SEM, SEM, ANY], out_specs=[HBM] * (2 * m),
        input_output_aliases={a: a for a in range(2 * m)},
        compiler_params=pltpu.CompilerParams(has_side_effects=EFFECT),
    )(*s_in, *l_in, *sems, after)
    return list(res[m:])


def scatter_start(grads, lands, slot, after, name):
    n = len(grads)

    def body(*refs):
        src, land = refs[:n], refs[n:2 * n]
        ssem, rsem, lsem = refs[2 * n + 1:2 * n + 4]
        token = refs[-1]
        x, y, c = _axes()
        chip = 2 * x + y
        for a in range(n):
            pltpu.make_async_copy(src[a].at[chip], land[a].at[slot[a], chip], lsem.at[a]).start()
            for k, (peer, pchip) in enumerate(_chip_peers(x, y, c)):
                pltpu.make_async_remote_copy(src_ref=src[a].at[pchip], dst_ref=land[a].at[slot[a], chip],
                                             send_sem=ssem.at[3 * a + k], recv_sem=rsem.at[3 * a + k],
                                             device_id=peer, device_id_type=MESH).start()
        token[...] = jnp.zeros_like(token)

    out_shape = ([pltpu.SemaphoreType.DMA((3 * n,)), pltpu.SemaphoreType.DMA((3 * n,)), pltpu.SemaphoreType.DMA((n,))]
                 + [pltpu.HBM(g.shape, g.dtype) for g in grads] + [pltpu.HBM(l.shape, l.dtype) for l in lands]
                 + [jax.ShapeDtypeStruct((8, 128), F32)])
    res = pl.pallas_call(
        body, name=name, out_shape=out_shape, in_specs=[HBM] * (2 * n) + [ANY],
        out_specs=[SEM, SEM, SEM] + [HBM] * (2 * n) + [pl.BlockSpec(memory_space=pltpu.VMEM)],
        input_output_aliases={a: 3 + a for a in range(2 * n)},
        compiler_params=pltpu.CompilerParams(has_side_effects=EFFECT),
    )(*[_in_hbm(g) for g in grads], *[_in_hbm(l) for l in lands], after)
    return tuple(res[:3]), list(res[3:3 + n]), list(res[3 + n:3 + 2 * n]), res[-1]


def scatter_wait(sems, grads, lands, slot, after, name):
    n = len(grads)

    def body(*refs):
        src, land = refs[:n], refs[n:2 * n]
        ssem, rsem, lsem = refs[2 * n:2 * n + 3]
        x, y, c = _axes()
        chip = 2 * x + y
        for a in range(n):
            for k, (peer, pchip) in enumerate(_chip_peers(x, y, c)):
                cp = pltpu.make_async_remote_copy(src_ref=src[a].at[pchip], dst_ref=land[a].at[slot[a], pchip],
                                                  send_sem=ssem.at[3 * a + k], recv_sem=rsem.at[3 * a + k],
                                                  device_id=peer, device_id_type=MESH)
                cp.wait_send()
                cp.wait_recv()
            pltpu.make_async_copy(src[a].at[chip], land[a].at[slot[a], chip], lsem.at[a]).wait()

    res = pl.pallas_call(
        body, name=name,
        out_shape=[pltpu.HBM(g.shape, g.dtype) for g in grads] + [pltpu.HBM(l.shape, l.dtype) for l in lands],
        in_specs=[HBM] * (2 * n) + [SEM, SEM, SEM, ANY], out_specs=[HBM] * (2 * n),
        input_output_aliases={a: a for a in range(2 * n)},
        compiler_params=pltpu.CompilerParams(has_side_effects=EFFECT),
    )(*grads, *lands, *sems, after)
    return list(res[n:])


def reduce4(land, name):
    nl, _, R, C = land.shape
    TR = _adam_rows(R, C)

    def body(l_ref, o_ref):
        o_ref[...] = ((l_ref[0].astype(F32) + l_ref[1].astype(F32)) + l_ref[2].astype(F32)) + l_ref[3].astype(F32)

    return pl.pallas_call(
        body, name=name, grid=(nl, R // TR),
        in_specs=[pl.BlockSpec((None, N_CHIP, TR, C), lambda i, r: (i, 0, r, 0))],
        out_specs=pl.BlockSpec((None, TR, C), lambda i, r: (i, r, 0)),
        out_shape=jax.ShapeDtypeStruct((nl, R, C), F32), compiler_params=_cp("parallel", "parallel"))(land)


def swap_siblings(arrs, name):
    n = len(arrs)

    def body(*refs):
        src, dst = refs[:n], refs[n:2 * n]
        ssem, rsem = refs[2 * n:]
        x, y, c = _axes()
        cps = [pltpu.make_async_remote_copy(src_ref=src[a], dst_ref=dst[a], send_sem=ssem.at[a], recv_sem=rsem.at[a],
                                            device_id=(x, y, 1 - c), device_id_type=MESH) for a in range(n)]
        for cp in cps:
            cp.start()
        for cp in cps:
            cp.wait()

    return pl.pallas_call(
        body, name=name, out_shape=[jax.ShapeDtypeStruct(a.shape, a.dtype) for a in arrs],
        in_specs=[ANY] * n, out_specs=[ANY] * n,
        scratch_shapes=[pltpu.SemaphoreType.DMA((n,)), pltpu.SemaphoreType.DMA((n,))],
        compiler_params=pltpu.CompilerParams(vmem_limit_bytes=VMEM_LIMIT),
    )(*arrs)


def swap_start(arrs, name):
    n = len(arrs)

    def body(*refs):
        src, land = refs[:n], refs[n:2 * n]
        ssem, rsem = refs[2 * n:2 * n + 2]
        token = refs[-1]
        x, y, c = _axes()
        for a in range(n):
            pltpu.make_async_remote_copy(src_ref=src[a], dst_ref=land[a], send_sem=ssem.at[a], recv_sem=rsem.at[a],
                                         device_id=(x, y, 1 - c), device_id_type=MESH).start()
        token[...] = jnp.zeros_like(token)

    lands = [lax.empty(a.shape, a.dtype) for a in arrs]
    out_shape = ([pltpu.SemaphoreType.DMA((n,)), pltpu.SemaphoreType.DMA((n,))]
                 + [pltpu.HBM(a.shape, a.dtype) for a in arrs] * 2 + [jax.ShapeDtypeStruct((8, 128), F32)])
    res = pl.pallas_call(
        body, name=name, out_shape=out_shape, in_specs=[HBM] * (2 * n),
        out_specs=[SEM, SEM] + [HBM] * (2 * n) + [pl.BlockSpec(memory_space=pltpu.VMEM)],
        input_output_aliases={a: 2 + a for a in range(2 * n)},
        compiler_params=pltpu.CompilerParams(has_side_effects=EFFECT),
    )(*[_in_hbm(a) for a in arrs], *[_in_hbm(l) for l in lands])
    return tuple(res[:2]), list(res[2:2 + n]), list(res[2 + n:2 + 2 * n]), res[-1]


def swap_wait(sems, srcs, lands, after, name):
    n = len(srcs)

    def body(*refs):
        src, land = refs[:n], refs[n:2 * n]
        ssem, rsem = refs[2 * n:2 * n + 2]
        x, y, c = _axes()
        for a in range(n):
            cp = pltpu.make_async_remote_copy(src_ref=src[a], dst_ref=land[a], send_sem=ssem.at[a],
                                              recv_sem=rsem.at[a], device_id=(x, y, 1 - c), device_id_type=MESH)
            cp.wait_send()
            cp.wait_recv()

    res = pl.pallas_call(
        body, name=name, out_shape=[pltpu.HBM(a.shape, a.dtype) for a in srcs] * 2,
        in_specs=[HBM] * (2 * n) + [SEM, SEM, ANY], out_specs=[HBM] * (2 * n),
        input_output_aliases={a: a for a in range(2 * n)},
        compiler_params=pltpu.CompilerParams(has_side_effects=EFFECT),
    )(*srcs, *lands, *sems, after)
    return list(res[:n]), list(res[n:])


def reduce8(slab, dm, name):
    RT, C = slab.shape
    P = RT // N_DEV
    R = dm.shape[0]

    def body(s_ref, dm_ref, o_ref, dmo_ref, recv, s1, r1, s2, r2, s3, r3):
        x, y, c = _axes()
        me = 4 * x + 2 * y + c
        mine = pl.ds(pl.multiple_of(me * P, 8), P)
        parts, dms = [], []
        for k in range(1, N_DEV):
            px, py, pc = _flip(x, (k >> 2) & 1), _flip(y, (k >> 1) & 1), _flip(c, k & 1)
            theirs = pl.ds(pl.multiple_of((4 * px + 2 * py + pc) * P, 8), P)
            cp = pltpu.make_async_remote_copy(src_ref=s_ref.at[theirs], dst_ref=recv.at[me], send_sem=s1.at[k - 1],
                                              recv_sem=r1.at[k - 1], device_id=(px, py, pc), device_id_type=MESH)
            cp.start()
            parts.append(cp)
            cd = pltpu.make_async_remote_copy(src_ref=dm_ref, dst_ref=dmo_ref.at[me], send_sem=s3.at[k - 1],
                                              recv_sem=r3.at[k - 1], device_id=(px, py, pc), device_id_type=MESH)
            cd.start()
            dms.append(cd)
        dmo_ref[me] = dm_ref[...]
        recv[me] = s_ref[mine, :]
        for cp in parts:
            cp.wait()
        tot = recv[0]
        for d in range(1, N_DEV):
            tot = tot + recv[d]
        o_ref[mine, :] = tot
        out = []
        for k in range(1, N_DEV):
            peer = (_flip(x, (k >> 2) & 1), _flip(y, (k >> 1) & 1), _flip(c, k & 1))
            cp = pltpu.make_async_remote_copy(src_ref=o_ref.at[mine], dst_ref=o_ref.at[mine], send_sem=s2.at[k - 1],
                                              recv_sem=r2.at[k - 1], device_id=peer, device_id_type=MESH)
            cp.start()
            out.append(cp)
        for cp in out + dms:
            cp.wait()

    sems = [pltpu.SemaphoreType.DMA((N_DEV - 1,))] * 6
    return pl.pallas_call(
        body, name=name,
        out_shape=[jax.ShapeDtypeStruct((RT, C), F32), jax.ShapeDtypeStruct((N_DEV, R, C), F32)],
        in_specs=[pl.BlockSpec(memory_space=pltpu.VMEM)] * 2, out_specs=[pl.BlockSpec(memory_space=pltpu.VMEM)] * 2,
        scratch_shapes=[pltpu.VMEM((N_DEV, P, C), F32)] + sems,
        compiler_params=pltpu.CompilerParams(vmem_limit_bytes=VMEM_LIMIT),
    )(slab, dm)


def mm_nn(a, w, out_dtype, name, res=None, gate=None):
    M, K = a.shape
    S, _, Ns = w.shape
    TM = _tile(M, (512, 256))
    TN = _tile(Ns, (1408, 1024, 768, 512, 256, 128))
    nj = Ns // TN
    fused = res is not None

    def body(*refs):
        if fused:
            a_ref, w_ref, r_ref, g_ref, f_ref, o_ref = refs
        else:
            a_ref, w_ref, f_ref = refs
        f = jnp.dot(a_ref[...], w_ref[...], preferred_element_type=F32)
        f_ref[...] = f.astype(f_ref.dtype)
        if fused:
            o_ref[...] = r_ref[...] + g_ref[...] * f

    col = lambda i, s, j: (i, s * nj + j)
    in_specs = [pl.BlockSpec((TM, K), lambda i, s, j: (i, 0)), pl.BlockSpec((None, K, TN), lambda i, s, j: (s, 0, j))]
    out_specs = [pl.BlockSpec((TM, TN), col)]
    out_shape = [jax.ShapeDtypeStruct((M, S * Ns), out_dtype)]
    args = [a, w]
    if fused:
        in_specs += [pl.BlockSpec((TM, TN), col), pl.BlockSpec((1, TN), lambda i, s, j: (0, s * nj + j))]
        out_specs.append(pl.BlockSpec((TM, TN), col))
        out_shape.append(jax.ShapeDtypeStruct((M, S * Ns), F32))
        args += [res, gate]
    out = pl.pallas_call(body, name=name, grid=(M // TM, S, nj), in_specs=in_specs, out_specs=out_specs,
                         out_shape=out_shape, compiler_params=_cp("parallel", "parallel", "parallel"))(*args)
    return tuple(out) if fused else out[0]


def mm_nt(g, w, out_dtype, name):
    M = g.shape[0]
    S, K, Ns = w.shape
    TM = _tile(M, (512, 256))
    TN = _tile(Ns, (1408, 1024, 768, 512, 256, 128))
    nj = Ns // TN
    nred = S * nj

    def body(g_ref, w_ref, o_ref, acc):
        n = pl.program_id(1)

        @pl.when(n == 0)
        def _():
            acc[...] = jnp.zeros_like(acc)

        acc[...] += lax.dot_general(g_ref[...], w_ref[...], (((1,), (1,)), ((), ())), preferred_element_type=F32)

        @pl.when(n == nred - 1)
        def _():
            o_ref[...] = acc[...].astype(o_ref.dtype)

    return pl.pallas_call(
        body, name=name, grid=(M // TM, nred),
        in_specs=[pl.BlockSpec((TM, TN), lambda i, n: (i, n)),
                  pl.BlockSpec((None, K, TN), lambda i, n: (n // nj, 0, n % nj))],
        out_specs=pl.BlockSpec((TM, K), lambda i, n: (i, 0)),
        out_shape=jax.ShapeDtypeStruct((M, K), out_dtype),
        scratch_shapes=[pltpu.VMEM((TM, K), F32)],
        compiler_params=_cp("parallel", "arbitrary"))(g, w)


def mm_tn(a, g, S, name):
    M, K = a.shape
    Ns = g.shape[1] // S
    TM = _tile(M, (512, 256))
    TK = _tile(K, (512, 256, 128))
    TN = _tile(Ns, (1408, 1024, 768, 512, 256, 128))
    nj = Ns // TN
    nm = M // TM

    def body(a_ref, g_ref, o_ref, acc):
        m = pl.program_id(2)

        @pl.when(m == 0)
        def _():
            acc[...] = jnp.zeros_like(acc)

        acc[...] += lax.dot_general(a_ref[...], g_ref[...], (((0,), (0,)), ((), ())), preferred_element_type=F32)

        @pl.when(m == nm - 1)
        def _():
            o_ref[...] = acc[...].astype(o_ref.dtype)

    return pl.pallas_call(
        body, name=name, grid=(K // TK, S * nj, nm),
        in_specs=[pl.BlockSpec((TM, TK), lambda k, n, m: (m, k)), pl.BlockSpec((TM, TN), lambda k, n, m: (m, n))],
        out_specs=pl.BlockSpec((None, TK, TN), lambda k, n, m: (n // nj, k, n % nj)),
        out_shape=jax.ShapeDtypeStruct((S, K, Ns), BF),
        scratch_shapes=[pltpu.VMEM((TK, TN), F32)],
        compiler_params=_cp("parallel", "parallel", "arbitrary"))(a, g)


def _rows(TL, D):
    return pl.BlockSpec((TL, D), lambda i: (i, 0))


def _fixed(R, D):
    return pl.BlockSpec((R, D), lambda i: (0, 0))


def _rowsum8(v):
    T, D = v.shape
    return jnp.sum(v.reshape(T // 8, 8, D), axis=0)


def _norm_parts(xv):
    r = lax.rsqrt(jnp.mean(xv * xv, axis=-1, keepdims=True) + RMS_EPS)
    return xv * r, r


def norm_mod(x, gamma, mods, k_shift, out_dtype, name):
    L, D = x.shape
    TL = _tile(L, (512, 256))

    def body(x_ref, g_ref, m_ref, o_ref):
        xn, _ = _norm_parts(x_ref[...])
        sh, sc = m_ref[k_shift:k_shift + 1, :], m_ref[k_shift + 1:k_shift + 2, :]
        o_ref[...] = ((xn * g_ref[...]) * (1.0 + sc) + sh).astype(o_ref.dtype)

    return pl.pallas_call(body, name=name, grid=(L // TL,),
                          in_specs=[_rows(TL, D), _fixed(1, D), _fixed(6, D)], out_specs=_rows(TL, D),
                          out_shape=jax.ShapeDtypeStruct((L, D), out_dtype), compiler_params=_cp("parallel"))(x, gamma, mods)


def norm_bwd(dh, x, dres, gamma, mods, k_shift, name):
    L, D = x.shape
    TL = _tile(L, (512, 256))

    def body(dh_ref, x_ref, dr_ref, g_ref, m_ref, dx_ref, s_ref, acc):
        i = pl.program_id(0)

        @pl.when(i == 0)
        def _():
            acc[...] = jnp.zeros_like(acc)

        xn, r = _norm_parts(x_ref[...])
        dh_v = dh_ref[...].astype(F32)
        gam = g_ref[...]
        sc = m_ref[k_shift + 1:k_shift + 2, :]
        dn = dh_v * (1.0 + sc)
        dxn = dn * gam
        dx_ref[...] = dr_ref[...] + r * (dxn - xn * jnp.mean(dxn * xn, axis=-1, keepdims=True))
        acc[0] += _rowsum8(dh_v)
        acc[1] += _rowsum8(dh_v * (xn * gam))
        acc[2] += _rowsum8(dn * xn)

        @pl.when(i == pl.num_programs(0) - 1)
        def _():
            s_ref[...] = jnp.zeros_like(s_ref)
            for q in range(3):
                s_ref[q:q + 1, :] = jnp.sum(acc[q], axis=0, keepdims=True)

    return pl.pallas_call(
        body, name=name, grid=(L // TL,),
        in_specs=[_rows(TL, D), _rows(TL, D), _rows(TL, D), _fixed(1, D), _fixed(6, D)],
        out_specs=[_rows(TL, D), _fixed(8, D)],
        out_shape=[jax.ShapeDtypeStruct((L, D), F32), jax.ShapeDtypeStruct((8, D), F32)],
        scratch_shapes=[pltpu.VMEM((3, 8, D), F32)], compiler_params=_cp("arbitrary"))(dh, x, dres, gamma, mods)


def gate_bwd(dx, f, mods, k_gate, name):
    L, D = dx.shape
    TL = _tile(L, (512, 256))

    def body(dx_ref, f_ref, m_ref, o_ref, s_ref, acc):
        i = pl.program_id(0)

        @pl.when(i == 0)
        def _():
            acc[...] = jnp.zeros_like(acc)

        dxv = dx_ref[...]
        o_ref[...] = (dxv * m_ref[k_gate:k_gate + 1, :]).astype(o_ref.dtype)
        acc[...] += _rowsum8(dxv * f_ref[...].astype(F32))

        @pl.when(i == pl.num_programs(0) - 1)
        def _():
            s_ref[...] = jnp.zeros_like(s_ref)
            s_ref[0:1, :] = jnp.sum(acc[...], axis=0, keepdims=True)

    return pl.pallas_call(
        body, name=name, grid=(L // TL,), in_specs=[_rows(TL, D), _rows(TL, D), _fixed(6, D)],
        out_specs=[_rows(TL, D), _fixed(8, D)],
        out_shape=[jax.ShapeDtypeStruct((L, D), BF), jax.ShapeDtypeStruct((8, D), F32)],
        scratch_shapes=[pltpu.VMEM((8, D), F32)], compiler_params=_cp("arbitrary"))(dx, f, mods)


def swiglu_act(gu, name):
    L, F2 = gu.shape
    F = F2 // 2
    TL = _tile(L, (256,))

    def body(gu_ref, o_ref):
        g = gu_ref[:, :F].astype(F32)
        u = gu_ref[:, F:].astype(F32)
        o_ref[...] = (g * jax.nn.sigmoid(g) * u).astype(o_ref.dtype)

    return pl.pallas_call(body, name=name, grid=(L // TL,), in_specs=[_rows(TL, F2)], out_specs=_rows(TL, F),
                          out_shape=jax.ShapeDtypeStruct((L, F), BF), compiler_params=_cp("parallel"))(gu)


def swiglu_bwd(da, gu, name):
    L, F2 = gu.shape
    F = F2 // 2
    TL = _tile(L, (256,))

    def body(da_ref, gu_ref, o_ref):
        g = gu_ref[:, :F].astype(F32)
        u = gu_ref[:, F:].astype(F32)
        d = da_ref[...].astype(F32)
        s = jax.nn.sigmoid(g)
        o_ref[:, :F] = (d * u * (s + g * s * (1.0 - s))).astype(o_ref.dtype)
        o_ref[:, F:] = (d * g * s).astype(o_ref.dtype)

    return pl.pallas_call(body, name=name, grid=(L // TL,), in_specs=[_rows(TL, F), _rows(TL, F2)],
                          out_specs=_rows(TL, F2), out_shape=jax.ShapeDtypeStruct((L, F2), BF),
                          compiler_params=_cp("parallel"))(da, gu)


def glu_res(o, x, mods, k_gate, name):
    L, D = x.shape
    TL = _tile(L, (512, 256))

    def body(o_ref, x_ref, m_ref, mix_ref, y_ref):
        mix = o_ref[:, :D].astype(F32) * jax.nn.sigmoid(o_ref[:, D:].astype(F32))
        mix_ref[...] = mix.astype(mix_ref.dtype)
        y_ref[...] = x_ref[...] + m_ref[k_gate:k_gate + 1, :] * mix

    return pl.pallas_call(
        body, name=name, grid=(L // TL,), in_specs=[_rows(TL, 2 * D), _rows(TL, D), _fixed(6, D)],
        out_specs=[_rows(TL, D), _rows(TL, D)],
        out_shape=[jax.ShapeDtypeStruct((L, D), BF), jax.ShapeDtypeStruct((L, D), F32)],
        compiler_params=_cp("parallel"))(o, x, mods)


def glu_bwd(dmix, o, name):
    L, D2 = o.shape
    D = D2 // 2
    TL = _tile(L, (512, 256))

    def body(d_ref, o_ref, do_ref):
        d = d_ref[...].astype(F32)
        val = o_ref[:, :D].astype(F32)
        s = jax.nn.sigmoid(o_ref[:, D:].astype(F32))
        do_ref[:, :D] = (d * s).astype(do_ref.dtype)
        do_ref[:, D:] = (d * val * s * (1.0 - s)).astype(do_ref.dtype)

    return pl.pallas_call(body, name=name, grid=(L // TL,), in_specs=[_rows(TL, D), _rows(TL, D2)],
                          out_specs=_rows(TL, D2), out_shape=jax.ShapeDtypeStruct((L, D2), BF),
                          compiler_params=_cp("parallel"))(dmix, o)


def final_loss(x, target, gamma, name):
    L, D = x.shape
    TL = _tile(L, (512, 256))

    def body(x_ref, t_ref, g_ref, l_ref, dx_ref, s_ref, acc, lacc):
        i = pl.program_id(0)

        @pl.when(i == 0)
        def _():
            acc[...] = jnp.zeros_like(acc)
            lacc[...] = jnp.zeros_like(lacc)

        xn, r = _norm_parts(x_ref[...])
        gam = g_ref[...]
        e = xn * gam - t_ref[...]
        lacc[...] += jnp.sum(0.5 * jnp.mean(e * e, axis=-1, keepdims=True), axis=0, keepdims=True)
        dy = e * (1.0 / D)
        dxn = dy * gam
        dx_ref[...] = r * (dxn - xn * jnp.mean(dxn * xn, axis=-1, keepdims=True))
        acc[...] += _rowsum8(dy * xn)

        @pl.when(i == pl.num_programs(0) - 1)
        def _():
            s_ref[...] = jnp.zeros_like(s_ref)
            s_ref[0:1, :] = jnp.sum(acc[...], axis=0, keepdims=True)
            l_ref[...] = jnp.broadcast_to(lacc[...], l_ref.shape)

    return pl.pallas_call(
        body, name=name, grid=(L // TL,), in_specs=[_rows(TL, D), _rows(TL, D), _fixed(1, D)],
        out_specs=[_fixed(8, 128), _rows(TL, D), _fixed(8, D)],
        out_shape=[jax.ShapeDtypeStruct((8, 128), F32), jax.ShapeDtypeStruct((L, D), F32),
                   jax.ShapeDtypeStruct((8, D), F32)],
        scratch_shapes=[pltpu.VMEM((8, D), F32), pltpu.VMEM((1, 1), F32)],
        compiler_params=_cp("arbitrary"))(x, target, gamma)


def _col(L, TC, off):
    return pl.BlockSpec((L, TC), lambda j: (0, off + j))


def _shift_down(v, k, row):
    return jnp.where(row >= k, pltpu.roll(v, k, 0), 0.0)


def _shift_up(v, k, row, L):
    return jnp.where(row < L - k, pltpu.roll(v, L - k, 0), 0.0)


def conv_fwd(p, w, name):
    L, D3 = p.shape
    D = D3 // 3
    TC = _tile(D, (128,))
    nc = D // TC

    def body(b_ref, c_ref, v_ref, w_ref, o_ref):
        row = lax.broadcasted_iota(jnp.int32, (L, TC), 0)
        cv = c_ref[...].astype(F32) * v_ref[...].astype(F32)
        conv = w_ref[2:3, :] * cv + w_ref[1:2, :] * _shift_down(cv, 1, row) + w_ref[0:1, :] * _shift_down(cv, 2, row)
        o_ref[...] = (b_ref[...].astype(F32) * conv).astype(o_ref.dtype)

    return pl.pallas_call(
        body, name=name, grid=(nc,),
        in_specs=[_col(L, TC, 0), _col(L, TC, nc), _col(L, TC, 2 * nc), pl.BlockSpec((3, TC), lambda j: (0, j))],
        out_specs=_col(L, TC, 0), out_shape=jax.ShapeDtypeStruct((L, D), BF), compiler_params=_cp("parallel"))(p, p, p, w)


def conv_bwd(dm, p, w, name):
    L, D3 = p.shape
    D = D3 // 3
    TC = _tile(D, (128,))
    nc = D // TC

    def body(dm_ref, b_ref, c_ref, v_ref, w_ref, db_ref, dc_ref, dv_ref, dw_ref):
        row = lax.broadcasted_iota(jnp.int32, (L, TC), 0)
        cg, vv = c_ref[...].astype(F32), v_ref[...].astype(F32)
        cv = cg * vv
        cv1, cv2 = _shift_down(cv, 1, row), _shift_down(cv, 2, row)
        conv = w_ref[2:3, :] * cv + w_ref[1:2, :] * cv1 + w_ref[0:1, :] * cv2
        dmv = dm_ref[...].astype(F32)
        db_ref[...] = (dmv * conv).astype(db_ref.dtype)
        dconv = dmv * b_ref[...].astype(F32)
        dcv = (w_ref[2:3, :] * dconv + w_ref[1:2, :] * _shift_up(dconv, 1, row, L)
               + w_ref[0:1, :] * _shift_up(dconv, 2, row, L))
        dc_ref[...] = (dcv * vv).astype(dc_ref.dtype)
        dv_ref[...] = (dcv * cg).astype(dv_ref.dtype)
        dw_ref[...] = jnp.zeros_like(dw_ref)
        dw_ref[0:1, :] = jnp.sum(dconv * cv2, axis=0, keepdims=True)
        dw_ref[1:2, :] = jnp.sum(dconv * cv1, axis=0, keepdims=True)
        dw_ref[2:3, :] = jnp.sum(dconv * cv, axis=0, keepdims=True)

    one = jax.ShapeDtypeStruct((L, D), BF)
    return pl.pallas_call(
        body, name=name, grid=(nc,),
        in_specs=[_col(L, TC, 0), _col(L, TC, 0), _col(L, TC, nc), _col(L, TC, 2 * nc),
                  pl.BlockSpec((3, TC), lambda j: (0, j))],
        out_specs=[_col(L, TC, 0), _col(L, TC, 0), _col(L, TC, 0), pl.BlockSpec((8, TC), lambda j: (0, j))],
        out_shape=[one, one, one, jax.ShapeDtypeStruct((8, D), F32)],
        compiler_params=_cp("parallel"))(dm, p, p, p, w)


def _gelu(y):
    return 0.5 * y * (1.0 + jnp.tanh(GELU_C * (y + GELU_A * y * y * y)))


def _gelu_grad(y):
    th = jnp.tanh(GELU_C * (y + GELU_A * y * y * y))
    return 0.5 * (1.0 + th) + 0.5 * y * (1.0 - th * th) * GELU_C * (1.0 + 3.0 * GELU_A * y * y)


def _cmul_add(br, bi, ar, ai, sr, si):
    return br + ar * sr - ai * si, bi + ar * si + ai * sr


def _log2(n):
    k = n.bit_length() - 1
    assert 1 << k == n
    return k


def _replicate(P2, W2, P, GLP, transposed):
    shape = (W2, P2) if transposed else (P2, W2)
    k = lax.broadcasted_iota(jnp.int32, shape, 1 if transposed else 0)
    c = lax.broadcasted_iota(jnp.int32, shape, 0 if transposed else 1)
    return ((k >> _log2(P)) == (c >> _log2(GLP))) & ((k & (P - 1)) == (c & (P - 1)))


def _on_diagonal(KB, W2, H, P, GLP, transposed):
    shape = (W2, KB) if transposed else (KB, W2)
    r = lax.broadcasted_iota(jnp.int32, shape, 1 if transposed else 0)
    c = lax.broadcasted_iota(jnp.int32, shape, 0 if transposed else 1)
    return (r >> _log2(H)) == ((c & (GLP - 1)) >> _log2(P))


def _expand(t, dims, transposed):
    KB, W2, H, P, GLP = dims
    rep = _replicate(2 * P, W2, P, GLP, transposed).astype(t.dtype)
    wide = jnp.dot(rep, t, preferred_element_type=F32) if transposed else jnp.dot(t, rep, preferred_element_type=F32)
    return jnp.where(_on_diagonal(KB, W2, H, P, GLP, transposed), wide, 0.0).astype(t.dtype)


def _extract(acc, dims):
    KB, W2, H, P, GLP = dims
    rep = _replicate(2 * P, W2, P, GLP, True).astype(F32)
    kept = jnp.where(_on_diagonal(KB, W2, H, P, GLP, False), acc, 0.0)
    return jnp.dot(kept, rep, preferred_element_type=F32, precision=lax.Precision.HIGHEST)


def _cmul(ar, ai, sr, si):
    return ar * sr - ai * si, ar * si + ai * sr


LANES = 128


def _cols(ref, base, n, rows):
    return jnp.concatenate([ref[base + q, rows, :] for q in range(n)], axis=1)


def _set_cols(ref, base, n, rows, val):
    for q in range(n):
        ref[base + q, rows, :] = val[:, q * LANES:(q + 1) * LANES]


def _strided_s5_fwd(h, tb, tct, pw, dvec, name):
    L, D = h.shape
    nkb, KB, P2 = tb.shape
    P = P2 // 2
    W = (KB // SSM_GROUP) * P
    W2 = 2 * W
    dims = (KB, W2, SSM_GROUP, P, W)
    TL = _tile(L, (512, 256))
    CH = TL // 8
    NC = W // LANES

    def body(h_ref, tb_ref, tct_ref, pw_ref, d_ref, s_ref, y_ref, z_ref, bw, cw, carry):
        t = pl.program_id(1)

        @pl.when(t == 0)
        def _():
            carry[...] = jnp.zeros_like(carry)
            bw[...] = _expand(tb_ref[...], dims, False)
            cw[...] = _expand(tct_ref[...], dims, True)

        hv = h_ref[...]
        _set_cols(s_ref, 0, 2 * NC, slice(None), jnp.dot(hv.astype(BF), bw[...], preferred_element_type=F32))
        ar, ai = pw_ref[0:8, :W], pw_ref[0:8, W:]
        xr = xi = jnp.zeros((8, W), F32)
        for j in range(CH):
            rows = pl.ds(j, 8, stride=CH)
            xr, xi = _cmul_add(_cols(s_ref, 0, NC, rows), _cols(s_ref, NC, NC, rows), ar, ai, xr, xi)
            _set_cols(s_ref, 0, NC, rows, xr)
            _set_cols(s_ref, NC, NC, rows, xi)
        for k, off in ((1, 8), (2, 16), (4, 24)):
            xr, xi = _cmul_add(xr, xi, pw_ref[off:off + 8, :W], pw_ref[off:off + 8, W:],
                               pltpu.roll(xr, k, 0), pltpu.roll(xi, k, 0))
        xr, xi = _cmul_add(xr, xi, pw_ref[32:40, :W], pw_ref[32:40, W:], carry[0], carry[1])
        first = lax.broadcasted_iota(jnp.int32, (8, W), 0) == 0
        cr = jnp.where(first, carry[0], pltpu.roll(xr, 1, 0))
        ci = jnp.where(first, carry[1], pltpu.roll(xi, 1, 0))
        carry[0] = jnp.broadcast_to(xr[7:8], (8, W))
        carry[1] = jnp.broadcast_to(xi[7:8], (8, W))
        for j in range(CH):
            rows = pl.ds(j, 8, stride=CH)
            cr, ci = _cmul(ar, ai, cr, ci)
            _set_cols(s_ref, 0, NC, rows, _cols(s_ref, 0, NC, rows) + cr)
            _set_cols(s_ref, NC, NC, rows, _cols(s_ref, NC, NC, rows) + ci)
        sv = _cols(s_ref, 0, 2 * NC, slice(None))
        y = jnp.dot(sv.astype(BF), cw[...], preferred_element_type=F32) + d_ref[...] * hv
        y_ref[...] = y
        z_ref[...] = _gelu(y).astype(z_ref.dtype)

    blk = lambda kb, t: (t, kb)
    per_kb = lambda kb, t: (kb, 0, 0)
    return pl.pallas_call(
        body, name=name, grid=(nkb, L // TL),
        in_specs=[pl.BlockSpec((TL, KB), blk), pl.BlockSpec((None, KB, P2), per_kb),
                  pl.BlockSpec((None, P2, KB), per_kb), pl.BlockSpec((None, 40, W2), per_kb),
                  pl.BlockSpec((1, KB), lambda kb, t: (0, kb))],
        out_specs=[pl.BlockSpec((2 * NC, TL, LANES), lambda kb, t: (kb, t, 0)), pl.BlockSpec((TL, KB), blk),
                   pl.BlockSpec((TL, KB), blk)],
        out_shape=[jax.ShapeDtypeStruct((nkb * 2 * NC, L, LANES), F32), jax.ShapeDtypeStruct((L, D), F32),
                   jax.ShapeDtypeStruct((L, D), BF)],
        scratch_shapes=[pltpu.VMEM((KB, W2), BF), pltpu.VMEM((W2, KB), BF), pltpu.VMEM((2, 8, W), F32)],
        compiler_params=_cp("parallel", "arbitrary"))(h, tb, tct, pw, dvec)


def _strided_s5_bwd(dz, y, h, s, tc, tbt, pwr, dvec, name):
    L, D = h.shape
    nkb, KB, P2 = tc.shape
    P = P2 // 2
    W = (KB // SSM_GROUP) * P
    W2 = 2 * W
    dims = (KB, W2, SSM_GROUP, P, W)
    TL = _tile(L, (512, 256))
    CH = TL // 8
    NC = W // LANES
    nt = L // TL

    def body(dz_ref, y_ref, h_ref, s_ref, sp_ref, tc_ref, tbt_ref, pw_ref, d_ref,
             dh_ref, dd_ref, da_ref, db_ref, dc_ref, g, ctw, btw, dbacc, dcacc, carry):
        t = pl.program_id(1)

        @pl.when(t == 0)
        def _():
            carry[...] = jnp.zeros_like(carry)
            dd_ref[...] = jnp.zeros_like(dd_ref)
            da_ref[...] = jnp.zeros_like(da_ref)
            dbacc[...] = jnp.zeros_like(dbacc)
            dcacc[...] = jnp.zeros_like(dcacc)
            ctw[...] = _expand(tc_ref[...], dims, False)
            btw[...] = _expand(tbt_ref[...], dims, True)

        hv = h_ref[...]
        dy = dz_ref[...].astype(F32) * _gelu_grad(y_ref[...])
        dd_ref[...] += _rowsum8(dy * hv)
        dyb = dy.astype(BF)
        _set_cols(g, 0, 2 * NC, slice(None), jnp.dot(dyb, ctw[...], preferred_element_type=F32))
        ar, ai = pw_ref[0:8, :W], pw_ref[0:8, W:]
        gr = gi = jnp.zeros((8, W), F32)
        for j in reversed(range(CH)):
            rows = pl.ds(j, 8, stride=CH)
            gr, gi = _cmul_add(_cols(g, 0, NC, rows), _cols(g, NC, NC, rows), ar, ai, gr, gi)
            _set_cols(g, 0, NC, rows, gr)
            _set_cols(g, NC, NC, rows, gi)
        for k, off in ((1, 8), (2, 16), (4, 24)):
            gr, gi = _cmul_add(gr, gi, pw_ref[off:off + 8, :W], pw_ref[off:off + 8, W:],
                               pltpu.roll(gr, 8 - k, 0), pltpu.roll(gi, 8 - k, 0))
        gr, gi = _cmul_add(gr, gi, pw_ref[32:40, :W], pw_ref[32:40, W:], carry[0], carry[1])
        sub = lax.broadcasted_iota(jnp.int32, (8, W), 0)
        cr = jnp.where(sub == 7, carry[0], pltpu.roll(gr, 7, 0))
        ci = jnp.where(sub == 7, carry[1], pltpu.roll(gi, 7, 0))
        carry[0] = jnp.broadcast_to(gr[0:1], (8, W))
        carry[1] = jnp.broadcast_to(gi[0:1], (8, W))
        live = jnp.where(t == nt - 1, 0.0, 1.0)
        accr = acci = jnp.zeros((8, W), F32)
        for j in reversed(range(CH)):
            rows = pl.ds(j, 8, stride=CH)
            cr, ci = _cmul(ar, ai, cr, ci)
            gr, gi = _cols(g, 0, NC, rows) + cr, _cols(g, NC, NC, rows) + ci
            _set_cols(g, 0, NC, rows, gr)
            _set_cols(g, NC, NC, rows, gi)
            if j > 0:
                before = pl.ds(j - 1, 8, stride=CH)
                pr, pi = _cols(s_ref, 0, NC, before), _cols(s_ref, NC, NC, before)
            else:
                last = pl.ds(CH - 1, 8, stride=CH)
                pr = jnp.where(sub == 0, _cols(sp_ref, 0, NC, slice(7, 8)) * live,
                               pltpu.roll(_cols(s_ref, 0, NC, last), 1, 0))
                pi = jnp.where(sub == 0, _cols(sp_ref, NC, NC, slice(7, 8)) * live,
                               pltpu.roll(_cols(s_ref, NC, NC, last), 1, 0))
            accr = accr + pr * gr + pi * gi
            acci = acci + pr * gi - pi * gr
        da_ref[:, :W] += accr
        da_ref[:, W:] += acci

        gb = _cols(g, 0, 2 * NC, slice(None)).astype(BF)
        dh_ref[...] = dy * d_ref[...] + jnp.dot(gb, btw[...], preferred_element_type=F32)
        tn = (((0,), (0,)), ((), ()))
        dbacc[...] += lax.dot_general(hv.astype(BF), gb, tn, preferred_element_type=F32)
        dcacc[...] += lax.dot_general(dyb, _cols(s_ref, 0, 2 * NC, slice(None)).astype(BF), tn,
                                      preferred_element_type=F32)

        @pl.when(t == nt - 1)
        def _():
            db_ref[...] = _extract(dbacc[...], dims)
            dc_ref[...] = _extract(dcacc[...], dims)

    rev = lambda kb, t: (nt - 1 - t, kb)
    per_kb = lambda kb, t: (kb, 0, 0)
    return pl.pallas_call(
        body, name=name, grid=(nkb, nt),
        in_specs=[pl.BlockSpec((TL, KB), rev), pl.BlockSpec((TL, KB), rev), pl.BlockSpec((TL, KB), rev),
                  pl.BlockSpec((2 * NC, TL, LANES), lambda kb, t: (kb, nt - 1 - t, 0)),
                  pl.BlockSpec((2 * NC, 8, LANES), lambda kb, t: (kb, jnp.maximum((nt - 1 - t) * CH - 1, 0), 0)),
                  pl.BlockSpec((None, KB, P2), per_kb), pl.BlockSpec((None, P2, KB), per_kb),
                  pl.BlockSpec((None, 40, W2), per_kb), pl.BlockSpec((1, KB), lambda kb, t: (0, kb))],
        out_specs=[pl.BlockSpec((TL, KB), rev), pl.BlockSpec((8, KB), lambda kb, t: (0, kb)),
                   pl.BlockSpec((None, 8, W2), per_kb), pl.BlockSpec((None, KB, P2), per_kb),
                   pl.BlockSpec((None, KB, P2), per_kb)],
        out_shape=[jax.ShapeDtypeStruct((L, D), F32), jax.ShapeDtypeStruct((8, D), F32),
                   jax.ShapeDtypeStruct((nkb, 8, W2), F32), jax.ShapeDtypeStruct((nkb, KB, P2), F32),
                   jax.ShapeDtypeStruct((nkb, KB, P2), F32)],
        scratch_shapes=[pltpu.VMEM((2 * NC, TL, LANES), F32), pltpu.VMEM((KB, W2), BF), pltpu.VMEM((W2, KB), BF),
                        pltpu.VMEM((KB, W2), F32), pltpu.VMEM((KB, W2), F32), pltpu.VMEM((2, 8, W), F32)],
        compiler_params=_cp("parallel", "arbitrary"))(dz, y, h, s, s, tc, tbt, pwr, dvec)


def _chunk_order(TL, CH, transposed):
    out_row = lax.broadcasted_iota(jnp.int32, (TL, TL), 1 if transposed else 0)
    in_row = lax.broadcasted_iota(jnp.int32, (TL, TL), 0 if transposed else 1)
    return in_row == ((out_row & 7) << _log2(CH)) + (out_row >> 3)


def _reorder(perm, v):
    hi = v.astype(perm.dtype)
    lo = (v - hi.astype(F32)).astype(perm.dtype)
    return jnp.dot(perm, hi, preferred_element_type=F32) + jnp.dot(perm, lo, preferred_element_type=F32)


def s5_fwd(h, tb, tct, pw, dvec, name):
    L, D = h.shape
    nkb, KB, P2 = tb.shape
    P = P2 // 2
    W = (KB // SSM_GROUP) * P
    W2 = 2 * W
    dims = (KB, W2, SSM_GROUP, P, W)
    TL = _tile(L, (512, 256))
    CH = TL // 8

    def body(h_ref, tb_ref, tct_ref, pw_ref, d_ref, s_ref, y_ref, z_ref, bw, cw, perm, unperm, carry):
        t = pl.program_id(1)

        @pl.when(t == 0)
        def _():
            carry[...] = jnp.zeros_like(carry)
            bw[...] = _expand(tb_ref[...], dims, False)
            cw[...] = _expand(tct_ref[...], dims, True)
            perm[...] = _chunk_order(TL, CH, False).astype(perm.dtype)
            unperm[...] = _chunk_order(TL, CH, True).astype(perm.dtype)

        hp = _reorder(perm[...], h_ref[...])
        s_ref[...] = jnp.dot(hp.astype(BF), bw[...], preferred_element_type=F32)
        ar, ai = pw_ref[0:8, :W], pw_ref[0:8, W:]

        def own(j, x):
            rows = pl.ds(pl.multiple_of(j * 8, 8), 8)
            xr, xi = _cmul_add(s_ref[rows, :W], s_ref[rows, W:], ar, ai, x[0], x[1])
            s_ref[rows, :W] = xr
            s_ref[rows, W:] = xi
            return xr, xi

        zero = jnp.zeros((8, W), F32)
        xr, xi = lax.fori_loop(0, CH, own, (zero, zero))
        for k, off in ((1, 8), (2, 16), (4, 24)):
            xr, xi = _cmul_add(xr, xi, pw_ref[off:off + 8, :W], pw_ref[off:off + 8, W:],
                               pltpu.roll(xr, k, 0), pltpu.roll(xi, k, 0))
        xr, xi = _cmul_add(xr, xi, pw_ref[32:40, :W], pw_ref[32:40, W:], carry[0], carry[1])
        first = lax.broadcasted_iota(jnp.int32, (8, W), 0) == 0
        cr = jnp.where(first, carry[0], pltpu.roll(xr, 1, 0))
        ci = jnp.where(first, carry[1], pltpu.roll(xi, 1, 0))
        carry[0] = jnp.broadcast_to(xr[7:8], (8, W))
        carry[1] = jnp.broadcast_to(xi[7:8], (8, W))

        def carried(j, c):
            rows = pl.ds(pl.multiple_of(j * 8, 8), 8)
            cr, ci = _cmul(ar, ai, c[0], c[1])
            s_ref[rows, :W] = s_ref[rows, :W] + cr
            s_ref[rows, W:] = s_ref[rows, W:] + ci
            return cr, ci

        lax.fori_loop(0, CH, carried, (cr, ci))
        y = jnp.dot(s_ref[...].astype(BF), cw[...], preferred_element_type=F32) + d_ref[...] * hp
        y_ref[...] = y
        z_ref[...] = jnp.dot(unperm[...], _gelu(y).astype(BF), preferred_element_type=F32).astype(z_ref.dtype)

    blk = lambda kb, t: (t, kb)
    per_kb = lambda kb, t: (kb, 0, 0)
    return pl.pallas_call(
        body, name=name, grid=(nkb, L // TL),
        in_specs=[pl.BlockSpec((TL, KB), blk), pl.BlockSpec((None, KB, P2), per_kb),
                  pl.BlockSpec((None, P2, KB), per_kb), pl.BlockSpec((None, 40, W2), per_kb),
                  pl.BlockSpec((1, KB), lambda kb, t: (0, kb))],
        out_specs=[pl.BlockSpec((TL, W2), blk), pl.BlockSpec((TL, KB), blk), pl.BlockSpec((TL, KB), blk)],
        out_shape=[jax.ShapeDtypeStruct((L, nkb * W2), F32), jax.ShapeDtypeStruct((L, D), F32),
                   jax.ShapeDtypeStruct((L, D), BF)],
        scratch_shapes=[pltpu.VMEM((KB, W2), BF), pltpu.VMEM((W2, KB), BF), pltpu.VMEM((TL, TL), BF),
                        pltpu.VMEM((TL, TL), BF), pltpu.VMEM((2, 8, W), F32)],
        compiler_params=_cp("parallel", "arbitrary"))(h, tb, tct, pw, dvec)


def s5_bwd(dz, y, h, s, tc, tbt, pwr, dvec, name):
    L, D = h.shape
    nkb, KB, P2 = tc.shape
    P = P2 // 2
    W = (KB // SSM_GROUP) * P
    W2 = 2 * W
    dims = (KB, W2, SSM_GROUP, P, W)
    TL = _tile(L, (512, 256))
    CH = TL // 8
    nt = L // TL

    def body(dz_ref, y_ref, h_ref, s_ref, sp_ref, tc_ref, tbt_ref, pw_ref, d_ref,
             dh_ref, dd_ref, da_ref, db_ref, dc_ref, g, ctw, btw, dbacc, dcacc, perm, unperm, carry):
        t = pl.program_id(1)

        @pl.when(t == 0)
        def _():
            carry[...] = jnp.zeros_like(carry)
            dd_ref[...] = jnp.zeros_like(dd_ref)
            da_ref[...] = jnp.zeros_like(da_ref)
            dbacc[...] = jnp.zeros_like(dbacc)
            dcacc[...] = jnp.zeros_like(dcacc)
            ctw[...] = _expand(tc_ref[...], dims, False)
            btw[...] = _expand(tbt_ref[...], dims, True)
            perm[...] = _chunk_order(TL, CH, False).astype(perm.dtype)
            unperm[...] = _chunk_order(TL, CH, True).astype(perm.dtype)

        hp = _reorder(perm[...], h_ref[...])
        dy = _reorder(perm[...], dz_ref[...].astype(F32)) * _gelu_grad(y_ref[...])
        dd_ref[...] += _rowsum8(dy * hp)
        dyb = dy.astype(BF)
        g[...] = jnp.dot(dyb, ctw[...], preferred_element_type=F32)
        ar, ai = pw_ref[0:8, :W], pw_ref[0:8, W:]

        def own(jj, x):
            rows = pl.ds(pl.multiple_of((CH - 1 - jj) * 8, 8), 8)
            gr, gi = _cmul_add(g[rows, :W], g[rows, W:], ar, ai, x[0], x[1])
            g[rows, :W] = gr
            g[rows, W:] = gi
            return gr, gi

        zero = jnp.zeros((8, W), F32)
        gr, gi = lax.fori_loop(0, CH, own, (zero, zero))
        for k, off in ((1, 8), (2, 16), (4, 24)):
            gr, gi = _cmul_add(gr, gi, pw_ref[off:off + 8, :W], pw_ref[off:off + 8, W:],
                               pltpu.roll(gr, 8 - k, 0), pltpu.roll(gi, 8 - k, 0))
        gr, gi = _cmul_add(gr, gi, pw_ref[32:40, :W], pw_ref[32:40, W:], carry[0], carry[1])
        sub = lax.broadcasted_iota(jnp.int32, (8, W), 0)
        cr = jnp.where(sub == 7, carry[0], pltpu.roll(gr, 7, 0))
        ci = jnp.where(sub == 7, carry[1], pltpu.roll(gi, 7, 0))
        carry[0] = jnp.broadcast_to(gr[0:1], (8, W))
        carry[1] = jnp.broadcast_to(gi[0:1], (8, W))

        def carried(jj, c):
            j = CH - 1 - jj
            rows = pl.ds(pl.multiple_of(j * 8, 8), 8)
            before = pl.ds(pl.multiple_of(j * 8 - 8, 8), 8)
            cr, ci = _cmul(ar, ai, c[0], c[1])
            gr, gi = g[rows, :W] + cr, g[rows, W:] + ci
            g[rows, :W] = gr
            g[rows, W:] = gi
            pr, pi = s_ref[before, :W], s_ref[before, W:]
            return cr, ci, c[2] + pr * gr + pi * gi, c[3] + pr * gi - pi * gr

        cr, ci, accr, acci = lax.fori_loop(0, CH - 1, carried, (cr, ci, zero, zero))
        live = jnp.where(t == nt - 1, 0.0, 1.0)
        cr, ci = _cmul(ar, ai, cr, ci)
        gr, gi = g[0:8, :W] + cr, g[0:8, W:] + ci
        g[0:8, :W] = gr
        g[0:8, W:] = gi
        pr = jnp.where(sub == 0, sp_ref[7:8, :W] * live, pltpu.roll(s_ref[TL - 8:TL, :W], 1, 0))
        pi = jnp.where(sub == 0, sp_ref[7:8, W:] * live, pltpu.roll(s_ref[TL - 8:TL, W:], 1, 0))
        da_ref[:, :W] += accr + pr * gr + pi * gi
        da_ref[:, W:] += acci + pr * gi - pi * gr

        gb = g[...].astype(BF)
        dh = dy * d_ref[...] + jnp.dot(gb, btw[...], preferred_element_type=F32)
        dh_ref[...] = _reorder(unperm[...], dh)
        tn = (((0,), (0,)), ((), ()))
        dbacc[...] += lax.dot_general(hp.astype(BF), gb, tn, preferred_element_type=F32)
        dcacc[...] += lax.dot_general(dyb, s_ref[...].astype(BF), tn, preferred_element_type=F32)

        @pl.when(t == nt - 1)
        def _():
            db_ref[...] = _extract(dbacc[...], dims)
            dc_ref[...] = _extract(dcacc[...], dims)

    rev = lambda kb, t: (nt - 1 - t, kb)
    prev = lambda kb, t: (jnp.maximum((nt - 1 - t) * CH - 1, 0), kb)
    per_kb = lambda kb, t: (kb, 0, 0)
    return pl.pallas_call(
        body, name=name, grid=(nkb, nt),
        in_specs=[pl.BlockSpec((TL, KB), rev), pl.BlockSpec((TL, KB), rev), pl.BlockSpec((TL, KB), rev),
                  pl.BlockSpec((TL, W2), rev), pl.BlockSpec((8, W2), prev),
                  pl.BlockSpec((None, KB, P2), per_kb), pl.BlockSpec((None, P2, KB), per_kb),
                  pl.BlockSpec((None, 40, W2), per_kb), pl.BlockSpec((1, KB), lambda kb, t: (0, kb))],
        out_specs=[pl.BlockSpec((TL, KB), rev), pl.BlockSpec((8, KB), lambda kb, t: (0, kb)),
                   pl.BlockSpec((None, 8, W2), per_kb), pl.BlockSpec((None, KB, P2), per_kb),
                   pl.BlockSpec((None, KB, P2), per_kb)],
        out_shape=[jax.ShapeDtypeStruct((L, D), F32), jax.ShapeDtypeStruct((8, D), F32),
                   jax.ShapeDtypeStruct((nkb, 8, W2), F32), jax.ShapeDtypeStruct((nkb, KB, P2), F32),
                   jax.ShapeDtypeStruct((nkb, KB, P2), F32)],
        scratch_shapes=[pltpu.VMEM((TL, W2), F32), pltpu.VMEM((KB, W2), BF), pltpu.VMEM((W2, KB), BF),
                        pltpu.VMEM((KB, W2), F32), pltpu.VMEM((KB, W2), F32), pltpu.VMEM((TL, TL), BF),
                        pltpu.VMEM((TL, TL), BF), pltpu.VMEM((2, 8, W), F32)],
        compiler_params=_cp("parallel", "arbitrary"))(dz, y, h, s, s, tc, tbt, pwr, dvec)


def _discretise(a_re, a_im, log_step, b_re, b_im):
    lr = jnp.minimum(a_re, -1e-4)
    li = a_im
    dt = jnp.exp(log_step)[:, None]
    mag = jnp.exp(lr * dt)
    abr = mag * jnp.cos(li * dt)
    abi = mag * jnp.sin(li * dt)
    den = lr * lr + li * li
    qr = ((abr - 1.0) * lr + abi * li) / den
    qi = (abi * lr - (abr - 1.0) * li) / den
    bbar_re = qr[..., None] * b_re - qi[..., None] * b_im
    bbar_im = qr[..., None] * b_im + qi[..., None] * b_re
    return abr, abi, bbar_re, bbar_im


def _compact(m_re, m_im, nkb):
    G, H, P = m_re.shape
    t = jnp.stack([m_re, m_im], axis=2).reshape(nkb, (G // nkb) * H, 2 * P).astype(BF)
    return t, jnp.swapaxes(t, 1, 2)


def _scan_powers(abr, abi, nkb, conj, CH):
    G, P = abr.shape
    if conj:
        abi = -abi

    def cmul(u, v):
        return u[0] * v[0] - u[1] * v[1], u[0] * v[1] + u[1] * v[0]

    q = (abr, abi)
    for _ in range(_log2(CH)):
        q = cmul(q, q)
    pows = [q]
    for _ in range(7):
        pows.append(cmul(pows[-1], q))
    row = jnp.arange(8)[:, None, None]

    def table(part):
        out = [jnp.broadcast_to((abr, abi)[part][None], (8, G, P))]
        for k in (1, 2, 4):
            keep = (row <= 7 - k) if conj else (row >= k)
            out.append(jnp.where(keep, pows[k - 1][part][None], 0.0))
        ends = jnp.stack([p[part] for p in pows])
        out.append(ends[::-1] if conj else ends)
        return jnp.concatenate(out, axis=0)

    GL = G // nkb
    t = jnp.stack([table(0), table(1)], axis=1)
    t = t.reshape(40, 2, nkb, GL * P).transpose(2, 0, 1, 3)
    return t.reshape(nkb, 40, 2 * GL * P)


def ada_mods(c_all, w_ada, b_sh, name):
    nl, D, NA = w_ada.shape

    def body(c_ref, w_ref, b_ref, o_ref):
        cv = c_ref[...]
        act = cv * jax.nn.sigmoid(cv)
        o_ref[...] = jnp.dot(act, w_ref[...], preferred_element_type=F32, precision=lax.Precision.HIGHEST) + b_ref[...]

    return pl.pallas_call(
        body, name=name, grid=(nl,),
        in_specs=[pl.BlockSpec((8, D), lambda i: (0, 0)), pl.BlockSpec((None, D, NA), lambda i: (i, 0, 0)),
                  pl.BlockSpec((None, 1, NA), lambda i: (i, 0, 0))],
        out_specs=pl.BlockSpec((None, 8, NA), lambda i: (i, 0, 0)),
        out_shape=jax.ShapeDtypeStruct((nl, 8, NA), F32), compiler_params=_cp("parallel"))(c_all, w_ada, b_sh)


def _adamw(w, g, m, v):
    m = ADAM_B1 * m + (1.0 - ADAM_B1) * g
    v = ADAM_B2 * v + (1.0 - ADAM_B2) * (g * g)
    m_hat = m / (1.0 - ADAM_B1 ** ADAM_STEP)
    v_hat = v / (1.0 - ADAM_B2 ** ADAM_STEP)
    return -ADAM_LR * (m_hat / (jnp.sqrt(v_hat) + ADAM_EPS) + ADAM_WD * w), m, v


def _adam_rows(R, C):
    cap = max(8, (256 * 1024) // C)
    for t in range(min(R, cap), 0, -1):
        if R % t == 0 and (t % 8 == 0 or t == R):
            return t
    return R


def adamw_ada(c_t, dm, w, m, v, name):
    nl, D, NA = w.shape
    TK = _tile(D, (128,))

    def body(c_ref, dm_ref, w_ref, m_ref, v_ref, g_ref, d_ref, nm_ref, nv_ref):
        cv = c_ref[...]
        act = cv * jax.nn.sigmoid(cv)
        g = act[:, 0:1] * dm_ref[0:1, :]
        for b in range(1, 8):
            g = g + act[:, b:b + 1] * dm_ref[b:b + 1, :]
        g_ref[...] = g
        d_ref[...], nm_ref[...], nv_ref[...] = _adamw(w_ref[...], g, m_ref[...], v_ref[...])

    big = pl.BlockSpec((None, TK, NA), lambda i, k: (i, k, 0))
    shape = jax.ShapeDtypeStruct(w.shape, F32)
    return pl.pallas_call(
        body, name=name, grid=(nl, D // TK),
        in_specs=[pl.BlockSpec((TK, 8), lambda i, k: (k, 0)), pl.BlockSpec((None, 8, NA), lambda i, k: (i, 0, 0)),
                  big, big, big],
        out_specs=[big] * 4, out_shape=[shape] * 4, compiler_params=_cp("parallel", "parallel"))(c_t, dm, w, m, v)


def adamw_sharded(w, m, v, ga, gb, name):
    nl, R, C = w.shape
    TR = _adam_rows(R, C)

    def body(w_ref, m_ref, v_ref, a_ref, b_ref, g_ref, d_ref, nm_ref, nv_ref):
        g = a_ref[...] + b_ref[...]
        g_ref[...] = g
        d_ref[...], nm_ref[...], nv_ref[...] = _adamw(w_ref[...], g, m_ref[...], v_ref[...])

    big = pl.BlockSpec((None, TR, C), lambda i, r: (i, r, 0))
    shape = jax.ShapeDtypeStruct(w.shape, F32)
    return pl.pallas_call(
        body, name=name, grid=(nl, R // TR), in_specs=[big] * 5,
        out_specs=[big] * 4, out_shape=[shape] * 4, compiler_params=_cp("parallel", "parallel"))(w, m, v, ga, gb)


def adamw_slab(g, w, m, v, name):
    R, C = g.shape
    TR = _tile(R, (160, 80, 40, 8))

    def body(g_ref, w_ref, m_ref, v_ref, d_ref, nm_ref, nv_ref):
        d_ref[...], nm_ref[...], nv_ref[...] = _adamw(w_ref[...], g_ref[...], m_ref[...], v_ref[...])

    big = pl.BlockSpec((TR, C), lambda r: (r, 0))
    shape = jax.ShapeDtypeStruct((R, C), F32)
    return pl.pallas_call(
        body, name=name, grid=(R // TR,), in_specs=[big] * 4,
        out_specs=[big] * 3, out_shape=[shape] * 3, compiler_params=_cp("parallel"))(g, w, m, v)


def adamw_plain(w, m, v, g, name):
    def body(w_ref, m_ref, v_ref, g_ref, d_ref, nm_ref, nv_ref):
        d_ref[...], nm_ref[...], nv_ref[...] = _adamw(w_ref[...], g_ref[...], m_ref[...], v_ref[...])

    shape = jax.ShapeDtypeStruct(w.shape, F32)
    return pl.pallas_call(body, name=name, out_shape=[shape] * 3,
                          compiler_params=pltpu.CompilerParams(vmem_limit_bytes=VMEM_LIMIT))(w, m, v, g)


def _slab_rows(a):
    n = a.size
    rows = -(-n // SLAB_W)
    return -(-rows // 8) * 8


def _pack(arrs, pad_rows_to=0):
    out = []
    for a in arrs:
        rows = _slab_rows(a)
        flat = a.reshape(-1).astype(F32)
        flat = jnp.pad(flat, (0, rows * SLAB_W - flat.shape[0]))
        out.append(flat.reshape(rows, SLAB_W))
    total = sum(o.shape[0] for o in out)
    if pad_rows_to and total % pad_rows_to:
        out.append(jnp.zeros((pad_rows_to - total % pad_rows_to, SLAB_W), F32))
    return jnp.concatenate(out, axis=0)


def _unpack(slab, like):
    out, r = [], 0
    for a in like:
        rows = _slab_rows(a)
        out.append(slab[r:r + rows].reshape(-1)[:a.size].reshape(a.shape))
        r += rows
    return out


WEIGHTS = ['norm1_g', 'norm2_g', 'w_ada', 'b_ada', 'ssm_a_re', 'ssm_a_im', 'ssm_log_step', 'ssm_b_re', 'ssm_b_im',
           'ssm_c_re', 'ssm_c_im', 'ssm_d', 'ssm_w_out', 'conv_w_in', 'conv_w', 'conv_w_out', 'w_ffn_in',
           'w_ffn_out', 'final_g']
SLAB = ['norm1_g', 'norm2_g', 'b_ada', 'ssm_a_re', 'ssm_a_im', 'ssm_log_step', 'ssm_b_re', 'ssm_b_im', 'ssm_c_re',
        'ssm_c_im', 'ssm_d', 'final_g']
SHARDED = ['ssm_w_out', 'conv_w_in', 'conv_w_out', 'w_ffn_in', 'w_ffn_out']


def kernel(x, c, norm1_g, norm2_g, w_ada, b_ada, ssm_a_re, ssm_a_im, ssm_log_step, ssm_b_re, ssm_b_im, ssm_c_re, ssm_c_im, ssm_d, ssm_w_out, conv_w_in, conv_w, conv_w_out, w_ffn_in, w_ffn_out, final_g, loss_target, m_norm1_g, m_norm2_g, m_w_ada, m_b_ada, m_ssm_a_re, m_ssm_a_im, m_ssm_log_step, m_ssm_b_re, m_ssm_b_im, m_ssm_c_re, m_ssm_c_im, m_ssm_d, m_ssm_w_out, m_conv_w_in, m_conv_w, m_conv_w_out, m_w_ffn_in, m_w_ffn_out, m_final_g, v_norm1_g, v_norm2_g, v_w_ada, v_b_ada, v_ssm_a_re, v_ssm_a_im, v_ssm_log_step, v_ssm_b_re, v_ssm_b_im, v_ssm_c_re, v_ssm_c_im, v_ssm_d, v_ssm_w_out, v_conv_w_in, v_conv_w, v_conv_w_out, v_w_ffn_in, v_w_ffn_out, v_final_g):
    given = dict(locals())
    W = {n: given[n] for n in WEIGHTS}
    Mo = {n: given["m_" + n] for n in WEIGHTS}
    Vo = {n: given["v_" + n] for n in WEIGHTS}

    xs = x[0]
    tgt = loss_target[0]
    L, D = xs.shape
    nlayer = norm1_g.shape[0]
    NA = w_ada.shape[2]
    G = ssm_a_re.shape[1]
    nkb = D // S5_BLOCK
    ax, ay, ac = _axes()
    me = 4 * ax + 2 * ay + ac
    chip = 2 * ax + ay

    c_all = gather8(jnp.broadcast_to(c, (8, D)), "gather_c")[:, 0, :]
    b_sh = lax.dynamic_slice_in_dim(b_ada, chip * NA, NA, axis=1)[:, None, :]
    mods_part = ada_mods(c_all, w_ada, b_sh, "ada_mods")
    mg = gather8(mods_part.reshape(nlayer * 8, NA), "gather_mods")
    mg = mg.reshape(N_CHIP, 2, nlayer, 8, NA)[:, 0]
    mods_all = lax.dynamic_index_in_dim(mg, me, axis=2, keepdims=False)
    mods_all = jnp.transpose(mods_all, (1, 0, 2)).reshape(nlayer, 6, D)

    cw_parts = gather8(_pack([conv_w]), "gather_conv_w")
    nconv = conv_w.shape[0]
    cw_full = jnp.stack([_unpack(cw_parts[2 * q], [conv_w])[0] for q in range(N_CHIP)], axis=2)
    cw_full = cw_full.reshape(nconv, 3, D)

    use = []
    for i in range(nlayer):
        use += [("ssm_w_out", i // 2, i)] if i % 2 == 0 else [("conv_w_in", i // 2, i), ("conv_w_out", i // 2, i)]
        use += [("w_ffn_in", i, i), ("w_ffn_out", i, i)]
    g_sems, g_srcs, g_lands, token = gather_start([W[n][j].astype(BF) for n, j, _ in use],
                                                  cw_full + mods_all[0, 0:3], "gather_start")
    mods_all = mods_all + token[0:1, 0:1]

    def layer_weights(i, after):
        idx = [a for a, (_, _, li) in enumerate(use) if li == i]
        got = gather_wait(g_sems, g_srcs, g_lands, idx, after, "gather_wait%d" % i)
        return {use[a][0]: w for a, w in zip(idx, got)}

    s5 = []
    for j in range(ssm_a_re.shape[0]):
        disc, disc_vjp = jax.vjp(_discretise, ssm_a_re[j], ssm_a_im[j], ssm_log_step[j], ssm_b_re[j], ssm_b_im[j])
        abr, abi, bbar_re, bbar_im = disc
        tb, tbt = _compact(jnp.swapaxes(bbar_re, 1, 2), jnp.swapaxes(bbar_im, 1, 2), nkb)
        tc, tct = _compact(ssm_c_re[j], -ssm_c_im[j], nkb)
        chunk = _tile(L, (512, 256)) // 8
        s5.append(dict(vjp=disc_vjp, tb=tb, tbt=tbt, tc=tc, tct=tct, pw=_scan_powers(abr, abi, nkb, False, chunk),
                       pwr=_scan_powers(abr, abi, nkb, True, chunk)))

    saved = []
    xcur = xs
    for i in range(nlayer):
        j = i // 2
        mods = mods_all[i]
        sv = dict(x=xcur)
        if i % 2 == 0:
            h = norm_mod(xcur, norm1_g[i:i + 1], mods, 0, F32, "norm_mod_s5")
            states, yv, z = s5_fwd(h, s5[j]["tb"], s5[j]["tct"], s5[j]["pw"], ssm_d[j:j + 1], "s5_fwd")
            full = layer_weights(i, z)
            o = mm_nn(z, full["ssm_w_out"], BF, "mm_ssm_out")
            mix, x2 = glu_res(o, xcur, mods, 2, "glu_res")
            sv.update(h=h, states=states, y=yv, z=z, o=o)
        else:
            h = norm_mod(xcur, norm1_g[i:i + 1], mods, 0, BF, "norm_mod")
            full = layer_weights(i, h)
            p = mm_nn(h, full["conv_w_in"], BF, "mm_conv_in")
            mc = conv_fwd(p, cw_full[j], "conv_fwd")
            mix, x2 = mm_nn(mc, full["conv_w_out"].reshape(1, D, D), BF, "mm_conv_out", res=xcur, gate=mods[2:3])
            sv.update(h=h, p=p, mc=mc)
        h2 = norm_mod(x2, norm2_g[i:i + 1], mods, 3, BF, "norm_mod")
        gu = mm_nn(h2, full["w_ffn_in"], BF, "mm_ffn_in")
        act = swiglu_act(gu, "swiglu_act")
        F = act.shape[1]
        ff, x3 = mm_nn(act, full["w_ffn_out"].reshape(1, F, D), BF, "mm_ffn_out", res=x2, gate=mods[5:6])
        sv.update(mix=mix, x2=x2, h2=h2, gu=gu, act=act, ff=ff, w=full)
        saved.append(sv)
        xcur = x3

    loss_blk, dx, dfinal = final_loss(xcur, tgt, final_g[None, :], "final_loss")

    gland = {n: lax.empty((W[n].shape[0], N_CHIP) + W[n].shape[1:], BF) for n in SHARDED}
    in_flight = []
    dmods = [None] * nlayer
    dnorm1, dnorm2 = [None] * nlayer, [None] * nlayer
    dconv_w = [None] * nconv
    ds5 = [None] * ssm_a_re.shape[0]
    token = jnp.zeros((8, 128), F32)

    def send_grads(names, grads, slot, after, name):
        sems, thru, lands, tok = scatter_start([grads[n] for n in names], [gland[n] for n in names], slot, after, name)
        gland.update(zip(names, lands))
        in_flight.append((names, slot, sems, thru, name))
        return tok

    def land_grads(group, after):
        for names, slot, sems, thru, name in in_flight:
            if names[0] in group:
                got = scatter_wait(sems, thru, [gland[n] for n in names], slot, after, name.replace("scatter", "landed"))
                gland.update(zip(names, got))

    for i in reversed(range(nlayer)):
        j = i // 2
        mods = mods_all[i] + token[0:1, 0:1]
        sv = saved[i]
        full = sv["w"]
        gfull = {}
        F = sv["act"].shape[1]
        dff, dg2 = gate_bwd(dx, sv["ff"], mods, 5, "gate_bwd")
        gfull["w_ffn_out"] = mm_tn(sv["act"], dff, 1, "mm_tn_ffn_out").reshape(N_CHIP, F // N_CHIP, D)
        dact = mm_nt(dff, full["w_ffn_out"].reshape(1, F, D), BF, "mm_nt_ffn_out")
        dgu = swiglu_bwd(dact, sv["gu"], "swiglu_bwd")
        gfull["w_ffn_in"] = mm_tn(sv["h2"], dgu, N_CHIP, "mm_tn_ffn_in")
        dh2 = mm_nt(dgu, full["w_ffn_in"], F32, "mm_nt_ffn_in")
        token = send_grads(["w_ffn_out", "w_ffn_in"], gfull, [i, i], dh2, "scatter_ffn%d" % i)
        mods = mods + token[0:1, 0:1]
        dx2, s2 = norm_bwd(dh2, sv["x2"], dx, norm2_g[i:i + 1], mods, 3, "norm_bwd")
        dmix, dg1 = gate_bwd(dx2, sv["mix"], mods, 2, "gate_bwd")
        if i % 2 == 0:
            do = glu_bwd(dmix, sv["o"], "glu_bwd")
            gfull["ssm_w_out"] = mm_tn(sv["z"], do, N_CHIP, "mm_tn_ssm_out")
            dz = mm_nt(do, full["ssm_w_out"], F32, "mm_nt_ssm_out")
            dh, dd, dab, db, dc = s5_bwd(dz, sv["y"], sv["h"], sv["states"], s5[j]["tc"], s5[j]["tbt"], s5[j]["pwr"],
                                         ssm_d[j:j + 1], "s5_bwd")
            ds5[j] = (dd, dab, db, dc)
        else:
            gfull["conv_w_out"] = mm_tn(sv["mc"], dmix, 1, "mm_tn_conv_out").reshape(N_CHIP, D // N_CHIP, D)
            dmc = mm_nt(dmix, full["conv_w_out"].reshape(1, D, D), BF, "mm_nt_conv_out")
            dbg, dcg, dvv, dcw = conv_bwd(dmc, sv["p"], cw_full[j], "conv_bwd")
            dp = jnp.concatenate([dbg, dcg, dvv], axis=1)
            gfull["conv_w_in"] = mm_tn(sv["h"], dp, N_CHIP, "mm_tn_conv_in")
            dh = mm_nt(dp, full["conv_w_in"], F32, "mm_nt_conv_in")
            dconv_w[j] = dcw[0:3]
        dx, s1 = norm_bwd(dh, sv["x"], dx2, norm1_g[i:i + 1], mods, 0, "norm_bwd")
        dmods[i] = jnp.concatenate([s1[0:2], dg1[0:1], s2[0:2], dg2[0:1]], axis=0).reshape(6 * D)
        dnorm1[i], dnorm2[i] = s1[2], s2[2]
        names = ["ssm_w_out"] if i % 2 == 0 else ["conv_w_out", "conv_w_in"]
        token = send_grads(names, gfull, [j] * len(names), dx, "scatter_mix%d" % i)

    small = dict(norm1_g=jnp.stack(dnorm1), norm2_g=jnp.stack(dnorm2), b_ada=jnp.stack(dmods), final_g=dfinal[0])
    per = {n: [] for n in ('ssm_a_re', 'ssm_a_im', 'ssm_log_step', 'ssm_b_re', 'ssm_b_im', 'ssm_c_re', 'ssm_c_im', 'ssm_d')}
    GL = G // nkb
    for j, (dd, dab, db, dc) in enumerate(ds5):
        dab = jnp.sum(dab, axis=1).reshape(nkb, 2, GL, SSM_STATE)
        g_abr, g_abi = dab[:, 0].reshape(G, SSM_STATE), dab[:, 1].reshape(G, SSM_STATE)
        db, dc = db.reshape(G, SSM_GROUP, 2, SSM_STATE), dc.reshape(G, SSM_GROUP, 2, SSM_STATE)
        gb_re, gb_im, gc_re, gc_im = db[:, :, 0], db[:, :, 1], dc[:, :, 0], dc[:, :, 1]
        ga_re, ga_im, gls, gbr, gbi = s5[j]["vjp"]((g_abr, g_abi, jnp.swapaxes(gb_re, 1, 2), jnp.swapaxes(gb_im, 1, 2)))
        for n, val in zip(per, (ga_re, ga_im, gls, gbr, gbi, gc_re, -gc_im, jnp.sum(dd, axis=0))):
            per[n].append(val)
    small.update({n: jnp.stack(vals) for n, vals in per.items()})
    dcw_full = jnp.stack(dconv_w)

    slab_like = [W[n] for n in SLAB] + [dcw_full]
    rows64 = 8 * N_DEV
    g_slab, dm_all = reduce8(_pack([small[n] for n in SLAB] + [dcw_full], rows64), _pack([small["b_ada"]]),
                             "reduce_small")
    d_slab, m_slab, v_slab = adamw_slab(
        g_slab, _pack([W[n] for n in SLAB] + [jnp.zeros_like(dcw_full)], rows64),
        _pack([Mo[n] for n in SLAB] + [jnp.zeros_like(dcw_full)], rows64),
        _pack([Vo[n] for n in SLAB] + [jnp.ones_like(dcw_full)], rows64), "adamw_slab")
    out = {}
    for k, slab in zip(("g", "d", "m", "v"), (g_slab, d_slab, m_slab, v_slab)):
        for n, val in zip(SLAB, _unpack(slab, slab_like)):
            out[k, n] = val
    g_cw = lax.dynamic_slice_in_dim(_unpack(g_slab, slab_like)[-1], chip * conv_w.shape[2], conv_w.shape[2], axis=2)
    out["g", "conv_w"] = g_cw
    out["d", "conv_w"], out["m", "conv_w"], out["v", "conv_w"] = [
        r.reshape(conv_w.shape) for r in adamw_plain(conv_w.reshape(-1, conv_w.shape[2]), m_conv_w.reshape(-1, conv_w.shape[2]),
                                                     v_conv_w.reshape(-1, conv_w.shape[2]), g_cw.reshape(-1, conv_w.shape[2]),
                                                     "adamw_conv_w")]

    early = [n for n in SHARDED if n != "ssm_w_out"]
    land_grads(early, g_slab)
    mine = [reduce4(gland[n], "reduce4_" + n) for n in early]
    w_sems, w_srcs, w_lands, token = swap_start(mine, "swap_start")

    dm_all = dm_all.reshape(N_DEV, -1)[:, :b_ada.size].reshape(N_DEV, nlayer, N_CHIP, NA)
    dm_sh = jnp.transpose(lax.dynamic_index_in_dim(dm_all, chip, axis=2, keepdims=False), (1, 0, 2))
    res = adamw_ada(jnp.transpose(c_all) + token[0:1, 0:1], dm_sh, w_ada, m_w_ada, v_w_ada, "adamw_ada")
    out["g", "w_ada"], out["d", "w_ada"], out["m", "w_ada"], out["v", "w_ada"] = res

    mine, theirs = swap_wait(w_sems, w_srcs, w_lands, out["g", "w_ada"], "swap_wait")
    for n, ga, gb in zip(early, mine, theirs):
        r = adamw_sharded(W[n], Mo[n], Vo[n], ga, gb, "adamw_" + n)
        out["g", n], out["d", n], out["m", n], out["v", n] = r

    land_grads(["ssm_w_out"], out["g", "w_ffn_out"])
    ga = reduce4(gland["ssm_w_out"], "reduce4_ssm_w_out")
    gb = swap_siblings([ga], "swap_siblings")[0]
    r = adamw_sharded(ssm_w_out, m_ssm_w_out, v_ssm_w_out, ga, gb, "adamw_ssm_w_out")
    out["g", "ssm_w_out"], out["d", "ssm_w_out"], out["m", "ssm_w_out"], out["v", "ssm_w_out"] = r

    loss = lax.psum(loss_blk[0, 0], ("x", "y", "c"))
    return (loss, dx[None], *[out["g", n] for n in WEIGHTS], *[out["d", n] for n in WEIGHTS],
            *[out["m", n] for n in WEIGHTS], *[out["v", n] for n in WEIGHTS])
```

```python
import functools
import math

import jax
import jax.numpy as jnp
from jax import lax
from jax.experimental import pallas as pl
from jax.experimental.pallas import tpu as pltpu

F32 = jnp.float32
BF = jnp.bfloat16
MESH = pl.DeviceIdType.MESH
ANY = pl.BlockSpec(memory_space=pl.ANY)

N_DEV = 8
N_CHIP = 4
DEPTH = 4
SSM_GROUP = 16
SSM_STATE = 64
S5_BLOCK = 256
RMS_EPS = 1e-6
ADAM_LR, ADAM_B1, ADAM_B2, ADAM_EPS, ADAM_WD, ADAM_STEP = 0.001, 0.9, 0.999, 1e-08, 0.01, 10
V7X_VMEM_BYTES = 64 * 1024 * 1024
VMEM_LIMIT = V7X_VMEM_BYTES - 12 * 1024 * 1024
SLAB_W = 1024
GELU_C = math.sqrt(2.0 / math.pi)
GELU_A = 0.044715


def _cp(*sem):
    return pltpu.CompilerParams(dimension_semantics=sem if sem else None, vmem_limit_bytes=VMEM_LIMIT)


def _tile(n, prefs):
    for p in prefs:
        if p <= n and n % p == 0:
            return p
    return n


def _axes():
    return lax.axis_index("x"), lax.axis_index("y"), lax.axis_index("c")


def _flip(v, k):
    return 1 - v if k else v


def gather8(v, name):
    R, C = v.shape

    def body(v_ref, o_ref, ssem, rsem, lsem):
        x, y, c = _axes()
        me = 4 * x + 2 * y + c
        loc = pltpu.make_async_copy(v_ref, o_ref.at[me], lsem)
        loc.start()
        copies = []
        for k in range(1, N_DEV):
            peer = (_flip(x, (k >> 2) & 1), _flip(y, (k >> 1) & 1), _flip(c, k & 1))
            cp = pltpu.make_async_remote_copy(src_ref=v_ref, dst_ref=o_ref.at[me], send_sem=ssem.at[k - 1],
                                              recv_sem=rsem.at[k - 1], device_id=peer, device_id_type=MESH)
            cp.start()
            copies.append(cp)
        for cp in copies:
            cp.wait()
        loc.wait()

    return pl.pallas_call(
        body, name=name,
        out_shape=jax.ShapeDtypeStruct((N_DEV, R, C), v.dtype),
        in_specs=[pl.BlockSpec(memory_space=pltpu.VMEM)],
        out_specs=pl.BlockSpec(memory_space=pltpu.VMEM),
        scratch_shapes=[pltpu.SemaphoreType.DMA((N_DEV - 1,)), pltpu.SemaphoreType.DMA((N_DEV - 1,)),
                        pltpu.SemaphoreType.DMA],
        compiler_params=pltpu.CompilerParams(vmem_limit_bytes=VMEM_LIMIT),
    )(v)


HBM = pl.BlockSpec(memory_space=pltpu.HBM)
SEM = pl.BlockSpec(memory_space=pltpu.SEMAPHORE)
EFFECT = pltpu.SideEffectType.DATAFLOW_SIDE_EFFECTING


def _in_hbm(a):
    return pltpu.with_memory_space_constraint(a, pltpu.HBM)


def _chip_peers(x, y, c):
    out = []
    for k in range(1, N_CHIP):
        px, py = _flip(x, k >> 1), _flip(y, k & 1)
        out.append(((px, py, c), 2 * px + py))
    return out


def gather_start(shards, after, name):
    n = len(shards)

    def body(*refs):
        src, land = refs[:n], refs[n:2 * n]
        ssem, rsem, lsem = refs[2 * n + 1:2 * n + 4]
        token = refs[-1]
        x, y, c = _axes()
        chip = 2 * x + y
        for a in range(n):
            pltpu.make_async_copy(src[a], land[a].at[chip], lsem.at[a]).start()
            for k, (peer, _) in enumerate(_chip_peers(x, y, c)):
                pltpu.make_async_remote_copy(src_ref=src[a], dst_ref=land[a].at[chip], send_sem=ssem.at[3 * a + k],
                                             recv_sem=rsem.at[3 * a + k], device_id=peer, device_id_type=MESH).start()
        token[...] = jnp.zeros_like(token)

    lands = [lax.empty((N_CHIP,) + s.shape, s.dtype) for s in shards]
    out_shape = ([pltpu.SemaphoreType.DMA((3 * n,)), pltpu.SemaphoreType.DMA((3 * n,)), pltpu.SemaphoreType.DMA((n,))]
                 + [pltpu.HBM(s.shape, s.dtype) for s in shards] + [pltpu.HBM(l.shape, l.dtype) for l in lands]
                 + [jax.ShapeDtypeStruct((8, 128), F32)])
    res = pl.pallas_call(
        body, name=name, out_shape=out_shape, in_specs=[HBM] * (2 * n) + [ANY],
        out_specs=[SEM, SEM, SEM] + [HBM] * (2 * n) + [pl.BlockSpec(memory_space=pltpu.VMEM)],
        input_output_aliases={a: 3 + a for a in range(2 * n)},
        compiler_params=pltpu.CompilerParams(has_side_effects=EFFECT),
    )(*[_in_hbm(s) for s in shards], *[_in_hbm(l) for l in lands], after)
    return tuple(res[:3]), list(res[3:3 + n]), list(res[3 + n:3 + 2 * n]), res[-1]


def gather_wait(sems, srcs, lands, idx, after, name):
    m = len(idx)

    def body(*refs):
        src, land = refs[:m], refs[m:2 * m]
        ssem, rsem, lsem = refs[2 * m:2 * m + 3]
        x, y, c = _axes()
        chip = 2 * x + y
        for j, a in enumerate(idx):
            for k, (peer, pchip) in enumerate(_chip_peers(x, y, c)):
                cp = pltpu.make_async_remote_copy(src_ref=src[j], dst_ref=land[j].at[pchip], send_sem=ssem.at[3 * a + k],
                                                  recv_sem=rsem.at[3 * a + k], device_id=peer, device_id_type=MESH)
                cp.wait_send()
                cp.wait_recv()
            pltpu.make_async_copy(src[j], land[j].at[chip], lsem.at[a]).wait()

    s_in = [srcs[a] for a in idx]
    l_in = [lands[a] for a in idx]
    res = pl.pallas_call(
        body, name=name,
        out_shape=[pltpu.HBM(s.shape, s.dtype) for s in s_in] + [pltpu.HBM(l.shape, l.dtype) for l in l_in],
        in_specs=[HBM] * (2 * m) + [SEM, SEM, SEM, ANY], out_specs=[HBM] * (2 * m),
        input_output_aliases={a: a for a in range(2 * m)},
        compiler_params=pltpu.CompilerParams(has_side_effects=EFFECT),
    )(*s_in, *l_in, *sems, after)
    return list(res[m:])


def scatter_start(grads, lands, slot, after, name):
    n = len(grads)

    def body(*refs):
        src, land = refs[:n], refs[n:2 * n]
        ssem, rsem, lsem = refs[2 * n + 1:2 * n + 4]
        token = refs[-1]
        x, y, c = _axes()
        chip = 2 * x + y
        for a in range(n):
            pltpu.make_async_copy(src[a].at[chip], land[a].at[slot[a], chip], lsem.at[a]).start()
            for k, (peer, pchip) in enumerate(_chip_peers(x, y, c)):
                pltpu.make_async_remote_copy(src_ref=src[a].at[pchip], dst_ref=land[a].at[slot[a], chip],
                                             send_sem=ssem.at[3 * a + k], recv_sem=rsem.at[3 * a + k],
                                             device_id=peer, device_id_type=MESH).start()
        token[...] = jnp.zeros_like(token)

    out_shape = ([pltpu.SemaphoreType.DMA((3 * n,)), pltpu.SemaphoreType.DMA((3 * n,)), pltpu.SemaphoreType.DMA((n,))]
                 + [pltpu.HBM(g.shape, g.dtype) for g in grads] + [pltpu.HBM(l.shape, l.dtype) for l in lands]
                 + [jax.ShapeDtypeStruct((8, 128), F32)])
    res = pl.pallas_call(
        body, name=name, out_shape=out_shape, in_specs=[HBM] * (2 * n) + [ANY],
        out_specs=[SEM, SEM, SEM] + [HBM] * (2 * n) + [pl.BlockSpec(memory_space=pltpu.VMEM)],
        input_output_aliases={a: 3 + a for a in range(2 * n)},
        compiler_params=pltpu.CompilerParams(has_side_effects=EFFECT),
    )(*[_in_hbm(g) for g in grads], *[_in_hbm(l) for l in lands], after)
    return tuple(res[:3]), list(res[3:3 + n]), list(res[3 + n:3 + 2 * n]), res[-1]


def scatter_wait(sems, grads, lands, slot, after, name):
    n = len(grads)

    def body(*refs):
        src, land = refs[:n], refs[n:2 * n]
        ssem, rsem, lsem = refs[2 * n:2 * n + 3]
        x, y, c = _axes()
        chip = 2 * x + y
        for a in range(n):
            for k, (peer, pchip) in enumerate(_chip_peers(x, y, c)):
                cp = pltpu.make_async_remote_copy(src_ref=src[a].at[pchip], dst_ref=land[a].at[slot[a], pchip],
                                                  send_sem=ssem.at[3 * a + k], recv_sem=rsem.at[3 * a + k],
                                                  device_id=peer, device_id_type=MESH)
                cp.wait_send()
                cp.wait_recv()
            pltpu.make_async_copy(src[a].at[chip], land[a].at[slot[a], chip], lsem.at[a]).wait()

    res = pl.pallas_call(
        body, name=name,
        out_shape=[pltpu.HBM(g.shape, g.dtype) for g in grads] + [pltpu.HBM(l.shape, l.dtype) for l in lands],
        in_specs=[HBM] * (2 * n) + [SEM, SEM, SEM, ANY], out_specs=[HBM] * (2 * n),
        input_output_aliases={a: a for a in range(2 * n)},
        compiler_params=pltpu.CompilerParams(has_side_effects=EFFECT),
    )(*grads, *lands, *sems, after)
    return list(res[n:])


def reduce4(land, name):
    nl, _, R, C = land.shape
    TR = _adam_rows(R, C)

    def body(l_ref, o_ref):
        o_ref[...] = ((l_ref[0].astype(F32) + l_ref[1].astype(F32)) + l_ref[2].astype(F32)) + l_ref[3].astype(F32)

    return pl.pallas_call(
        body, name=name, grid=(nl, R // TR),
        in_specs=[pl.BlockSpec((None, N_CHIP, TR, C), lambda i, r: (i, 0, r, 0))],
        out_specs=pl.BlockSpec((None, TR, C), lambda i, r: (i, r, 0)),
        out_shape=jax.ShapeDtypeStruct((nl, R, C), F32), compiler_params=_cp("parallel", "parallel"))(land)


def swap_siblings(arrs, name):
    n = len(arrs)

    def body(*refs):
        src, dst = refs[:n], refs[n:2 * n]
        ssem, rsem = refs[2 * n:]
        x, y, c = _axes()
        cps = [pltpu.make_async_remote_copy(src_ref=src[a], dst_ref=dst[a], send_sem=ssem.at[a], recv_sem=rsem.at[a],
                                            device_id=(x, y, 1 - c), device_id_type=MESH) for a in range(n)]
        for cp in cps:
            cp.start()
        for cp in cps:
            cp.wait()

    return pl.pallas_call(
        body, name=name, out_shape=[jax.ShapeDtypeStruct(a.shape, a.dtype) for a in arrs],
        in_specs=[ANY] * n, out_specs=[ANY] * n,
        scratch_shapes=[pltpu.SemaphoreType.DMA((n,)), pltpu.SemaphoreType.DMA((n,))],
        compiler_params=pltpu.CompilerParams(vmem_limit_bytes=VMEM_LIMIT),
    )(*arrs)


def swap_start(arrs, name):
    n = len(arrs)

    def body(*refs):
        src, land = refs[:n], refs[n:2 * n]
        ssem, rsem = refs[2 * n:2 * n + 2]
        token = refs[-1]
        x, y, c = _axes()
        for a in range(n):
            pltpu.make_async_remote_copy(src_ref=src[a], dst_ref=land[a], send_sem=ssem.at[a], recv_sem=rsem.at[a],
                                         device_id=(x, y, 1 - c), device_id_type=MESH).start()
        token[...] = jnp.zeros_like(token)

    lands = [lax.empty(a.shape, a.dtype) for a in arrs]
    out_shape = ([pltpu.SemaphoreType.DMA((n,)), pltpu.SemaphoreType.DMA((n,))]
                 + [pltpu.HBM(a.shape, a.dtype) for a in arrs] * 2 + [jax.ShapeDtypeStruct((8, 128), F32)])
    res = pl.pallas_call(
        body, name=name, out_shape=out_shape, in_specs=[HBM] * (2 * n),
        out_specs=[SEM, SEM] + [HBM] * (2 * n) + [pl.BlockSpec(memory_space=pltpu.VMEM)],
        input_output_aliases={a: 2 + a for a in range(2 * n)},
        compiler_params=pltpu.CompilerParams(has_side_effects=EFFECT),
    )(*[_in_hbm(a) for a in arrs], *[_in_hbm(l) for l in lands])
    return tuple(res[:2]), list(res[2:2 + n]), list(res[2 + n:2 + 2 * n]), res[-1]


def swap_wait(sems, srcs, lands, after, name):
    n = len(srcs)

    def body(*refs):
        src, land = refs[:n], refs[n:2 * n]
        ssem, rsem = refs[2 * n:2 * n + 2]
        x, y, c = _axes()
        for a in range(n):
            cp = pltpu.make_async_remote_copy(src_ref=src[a], dst_ref=land[a], send_sem=ssem.at[a],
                                              recv_sem=rsem.at[a], device_id=(x, y, 1 - c), device_id_type=MESH)
            cp.wait_send()
            cp.wait_recv()

    res = pl.pallas_call(
        body, name=name, out_shape=[pltpu.HBM(a.shape, a.dtype) for a in srcs] * 2,
        in_specs=[HBM] * (2 * n) + [SEM, SEM, ANY], out_specs=[HBM] * (2 * n),
        input_output_aliases={a: a for a in range(2 * n)},
        compiler_params=pltpu.CompilerParams(has_side_effects=EFFECT),
    )(*srcs, *lands, *sems, after)
    return list(res[:n]), list(res[n:])


def reduce8(slab, dm, name):
    RT, C = slab.shape
    P = RT // N_DEV
    R = dm.shape[0]

    def body(s_ref, dm_ref, o_ref, dmo_ref, recv, s1, r1, s2, r2, s3, r3):
        x, y, c = _axes()
        me = 4 * x + 2 * y + c
        mine = pl.ds(pl.multiple_of(me * P, 8), P)
        parts, dms = [], []
        for k in range(1, N_DEV):
            px, py, pc = _flip(x, (k >> 2) & 1), _flip(y, (k >> 1) & 1), _flip(c, k & 1)
            theirs = pl.ds(pl.multiple_of((4 * px + 2 * py + pc) * P, 8), P)
            cp = pltpu.make_async_remote_copy(src_ref=s_ref.at[theirs], dst_ref=recv.at[me], send_sem=s1.at[k - 1],
                                              recv_sem=r1.at[k - 1], device_id=(px, py, pc), device_id_type=MESH)
            cp.start()
            parts.append(cp)
            cd = pltpu.make_async_remote_copy(src_ref=dm_ref, dst_ref=dmo_ref.at[me], send_sem=s3.at[k - 1],
                                              recv_sem=r3.at[k - 1], device_id=(px, py, pc), device_id_type=MESH)
            cd.start()
            dms.append(cd)
        dmo_ref[me] = dm_ref[...]
        recv[me] = s_ref[mine, :]
        for cp in parts:
            cp.wait()
        tot = recv[0]
        for d in range(1, N_DEV):
            tot = tot + recv[d]
        o_ref[mine, :] = tot
        out = []
        for k in range(1, N_DEV):
            peer = (_flip(x, (k >> 2) & 1), _flip(y, (k >> 1) & 1), _flip(c, k & 1))
            cp = pltpu.make_async_remote_copy(src_ref=o_ref.at[mine], dst_ref=o_ref.at[mine], send_sem=s2.at[k - 1],
                                              recv_sem=r2.at[k - 1], device_id=peer, device_id_type=MESH)
            cp.start()
            out.append(cp)
        for cp in out + dms:
            cp.wait()

    sems = [pltpu.SemaphoreType.DMA((N_DEV - 1,))] * 6
    return pl.pallas_call(
        body, name=name,
        out_shape=[jax.ShapeDtypeStruct((RT, C), F32), jax.ShapeDtypeStruct((N_DEV, R, C), F32)],
        in_specs=[pl.BlockSpec(memory_space=pltpu.VMEM)] * 2, out_specs=[pl.BlockSpec(memory_space=pltpu.VMEM)] * 2,
        scratch_shapes=[pltpu.VMEM((N_DEV, P, C), F32)] + sems,
        compiler_params=pltpu.CompilerParams(vmem_limit_bytes=VMEM_LIMIT),
    )(slab, dm)


def mm_nn(a, w, out_dtype, name, res=None, gate=None):
    M, K = a.shape
    S, _, Ns = w.shape
    TM = _tile(M, (1024, 512, 256) if K <= 1024 else (512, 256))
    TN = _tile(Ns, (1408, 1024, 768, 512, 256, 128))
    nj = Ns // TN
    fused = res is not None

    def body(*refs):
        if fused:
            a_ref, w_ref, r_ref, g_ref, f_ref, o_ref = refs
        else:
            a_ref, w_ref, f_ref = refs
        f = jnp.dot(a_ref[...], w_ref[...], preferred_element_type=F32)
        f_ref[...] = f.astype(f_ref.dtype)
        if fused:
            o_ref[...] = r_ref[...] + g_ref[...] * f

    col = lambda s, j, i: (i, s * nj + j)
    in_specs = [pl.BlockSpec((TM, K), lambda s, j, i: (i, 0)), pl.BlockSpec((None, K, TN), lambda s, j, i: (s, 0, j))]
    out_specs = [pl.BlockSpec((TM, TN), col)]
    out_shape = [jax.ShapeDtypeStruct((M, S * Ns), out_dtype)]
    args = [a, w]
    if fused:
        in_specs += [pl.BlockSpec((TM, TN), col), pl.BlockSpec((1, TN), lambda s, j, i: (0, s * nj + j))]
        out_specs.append(pl.BlockSpec((TM, TN), col))
        out_shape.append(jax.ShapeDtypeStruct((M, S * Ns), F32))
        args += [res, gate]
    out = pl.pallas_call(body, name=name, grid=(S, nj, M // TM), in_specs=in_specs, out_specs=out_specs,
                         out_shape=out_shape, compiler_params=_cp("parallel", "parallel", "parallel"))(*args)
    return tuple(out) if fused else out[0]


def mm_nt(g, w, out_dtype, name):
    g3 = g if g.ndim == 3 else g[None]
    Q, M, F = g3.shape
    S, K, Ns = w.shape
    TM = _tile(M, (1024, 512, 256) if K <= 1024 else (512, 256))
    TN = _tile(Ns, (1408, 1024, 768, 512, 256, 128))
    nj = Ns // TN
    nred = S * nj
    per_part = F // TN

    def body(g_ref, w_ref, o_ref, acc):
        n = pl.program_id(1)

        @pl.when(n == 0)
        def _():
            acc[...] = jnp.zeros_like(acc)

        acc[...] += lax.dot_general(g_ref[...], w_ref[...], (((1,), (1,)), ((), ())), preferred_element_type=F32)

        @pl.when(n == nred - 1)
        def _():
            o_ref[...] = acc[...].astype(o_ref.dtype)

    return pl.pallas_call(
        body, name=name, grid=(M // TM, nred),
        in_specs=[pl.BlockSpec((None, TM, TN), lambda i, n: (n // per_part, i, n % per_part)),
                  pl.BlockSpec((None, K, TN), lambda i, n: (n // nj, 0, n % nj))],
        out_specs=pl.BlockSpec((TM, K), lambda i, n: (i, 0)),
        out_shape=jax.ShapeDtypeStruct((M, K), out_dtype),
        scratch_shapes=[pltpu.VMEM((TM, K), F32)],
        compiler_params=_cp("parallel", "arbitrary"))(g3, w)


def mm_tn(a, g, S, name):
    M, K = a.shape
    g3 = g if g.ndim == 3 else g[None]
    Q, _, F = g3.shape
    Ns = Q * F // S
    TK = _tile(K, (256, 128))
    TN = _tile(Ns, (1408, 1024, 768, 512, 256, 128))
    nj = Ns // TN
    per_part = F // TN

    def body(a_ref, g_ref, o_ref):
        o_ref[...] = lax.dot_general(a_ref[...], g_ref[...], (((0,), (0,)), ((), ())),
                                     preferred_element_type=F32).astype(o_ref.dtype)

    return pl.pallas_call(
        body, name=name, grid=(S * nj, K // TK),
        in_specs=[pl.BlockSpec((M, TK), lambda n, k: (0, k)),
                  pl.BlockSpec((None, M, TN), lambda n, k: (n // per_part, 0, n % per_part))],
        out_specs=pl.BlockSpec((None, TK, TN), lambda n, k: (n // nj, k, n % nj)),
        out_shape=jax.ShapeDtypeStruct((S, K, Ns), BF),
        compiler_params=_cp("parallel", "parallel"))(a, g3)


def _rows(TL, D):
    return pl.BlockSpec((TL, D), lambda i: (i, 0))


def _fixed(R, D):
    return pl.BlockSpec((R, D), lambda i: (0, 0))


def _rowsum8(v):
    T, D = v.shape
    return jnp.sum(v.reshape(T // 8, 8, D), axis=0)


def _norm_parts(xv):
    r = lax.rsqrt(jnp.mean(xv * xv, axis=-1, keepdims=True) + RMS_EPS)
    return xv * r, r


def norm_mod(x, gamma, mods, k_shift, out_dtype, name):
    L, D = x.shape
    TL = _tile(L, (512, 256))

    def body(x_ref, g_ref, m_ref, o_ref):
        xn, _ = _norm_parts(x_ref[...])
        sh, sc = m_ref[k_shift:k_shift + 1, :], m_ref[k_shift + 1:k_shift + 2, :]
        o_ref[...] = ((xn * g_ref[...]) * (1.0 + sc) + sh).astype(o_ref.dtype)

    return pl.pallas_call(body, name=name, grid=(L // TL,),
                          in_specs=[_rows(TL, D), _fixed(1, D), _fixed(6, D)], out_specs=_rows(TL, D),
                          out_shape=jax.ShapeDtypeStruct((L, D), out_dtype), compiler_params=_cp("parallel"))(x, gamma, mods)


def norm_bwd(dh, x, dres, gamma, mods, k_shift, name):
    L, D = x.shape
    TL = _tile(L, (512, 256))

    def body(dh_ref, x_ref, dr_ref, g_ref, m_ref, dx_ref, s_ref, acc):
        i = pl.program_id(0)

        @pl.when(i == 0)
        def _():
            acc[...] = jnp.zeros_like(acc)

        xn, r = _norm_parts(x_ref[...])
        dh_v = dh_ref[...].astype(F32)
        gam = g_ref[...]
        sc = m_ref[k_shift + 1:k_shift + 2, :]
        dn = dh_v * (1.0 + sc)
        dxn = dn * gam
        dx_ref[...] = dr_ref[...] + r * (dxn - xn * jnp.mean(dxn * xn, axis=-1, keepdims=True))
        acc[0] += _rowsum8(dh_v)
        acc[1] += _rowsum8(dh_v * (xn * gam))
        acc[2] += _rowsum8(dn * xn)

        @pl.when(i == pl.num_programs(0) - 1)
        def _():
            s_ref[...] = jnp.zeros_like(s_ref)
            for q in range(3):
                s_ref[q:q + 1, :] = jnp.sum(acc[q], axis=0, keepdims=True)

    return pl.pallas_call(
        body, name=name, grid=(L // TL,),
        in_specs=[_rows(TL, D), _rows(TL, D), _rows(TL, D), _fixed(1, D), _fixed(6, D)],
        out_specs=[_rows(TL, D), _fixed(8, D)],
        out_shape=[jax.ShapeDtypeStruct((L, D), F32), jax.ShapeDtypeStruct((8, D), F32)],
        scratch_shapes=[pltpu.VMEM((3, 8, D), F32)], compiler_params=_cp("arbitrary"))(dh, x, dres, gamma, mods)


def gate_bwd(dx, f, mods, k_gate, name):
    L, D = dx.shape
    TL = _tile(L, (512, 256))

    def body(dx_ref, f_ref, m_ref, o_ref, s_ref, acc):
        i = pl.program_id(0)

        @pl.when(i == 0)
        def _():
            acc[...] = jnp.zeros_like(acc)

        dxv = dx_ref[...]
        o_ref[...] = (dxv * m_ref[k_gate:k_gate + 1, :]).astype(o_ref.dtype)
        acc[...] += _rowsum8(dxv * f_ref[...].astype(F32))

        @pl.when(i == pl.num_programs(0) - 1)
        def _():
            s_ref[...] = jnp.zeros_like(s_ref)
            s_ref[0:1, :] = jnp.sum(acc[...], axis=0, keepdims=True)

    return pl.pallas_call(
        body, name=name, grid=(L // TL,), in_specs=[_rows(TL, D), _rows(TL, D), _fixed(6, D)],
        out_specs=[_rows(TL, D), _fixed(8, D)],
        out_shape=[jax.ShapeDtypeStruct((L, D), BF), jax.ShapeDtypeStruct((8, D), F32)],
        scratch_shapes=[pltpu.VMEM((8, D), F32)], compiler_params=_cp("arbitrary"))(dx, f, mods)


def ffn_in_act(a, w, name):
    M, K = a.shape
    S, _, Ns = w.shape
    half = S // 2
    TM = _tile(M, (512, 256))
    TN = _tile(Ns, (1408, 1024, 768, 512, 256, 128))
    nj = Ns // TN

    def body(a_ref, wg_ref, wu_ref, gu_ref, act_ref):
        av = a_ref[...]
        g = jnp.dot(av, wg_ref[...], preferred_element_type=F32)
        u = jnp.dot(av, wu_ref[...], preferred_element_type=F32)
        gu_ref[0] = g.astype(gu_ref.dtype)
        gu_ref[1] = u.astype(gu_ref.dtype)
        act_ref[...] = (g * jax.nn.sigmoid(g) * u).astype(act_ref.dtype)

    return pl.pallas_call(
        body, name=name, grid=(half, nj, M // TM),
        in_specs=[pl.BlockSpec((TM, K), lambda s, j, i: (i, 0)),
                  pl.BlockSpec((None, K, TN), lambda s, j, i: (s, 0, j)),
                  pl.BlockSpec((None, K, TN), lambda s, j, i: (s + half, 0, j))],
        out_specs=[pl.BlockSpec((2, TM, TN), lambda s, j, i: (0, i, s * nj + j)),
                   pl.BlockSpec((TM, TN), lambda s, j, i: (i, s * nj + j))],
        out_shape=[jax.ShapeDtypeStruct((2, M, half * Ns), BF), jax.ShapeDtypeStruct((M, half * Ns), BF)],
        compiler_params=_cp("parallel", "parallel", "parallel"))(a, w, w)


def ffn_out_bwd(dff, w2, gu, name):
    M, D = dff.shape
    F = w2.shape[0]
    TM = _tile(M, (512, 256))
    TF = _tile(F, (1408, 1024, 512, 256, 128))

    def body(d_ref, w_ref, gu_ref, o_ref):
        da = lax.dot_general(d_ref[...], w_ref[...], (((1,), (1,)), ((), ())), preferred_element_type=F32)
        g = gu_ref[0].astype(F32)
        u = gu_ref[1].astype(F32)
        s = jax.nn.sigmoid(g)
        o_ref[0] = (da * u * (s + g * s * (1.0 - s))).astype(o_ref.dtype)
        o_ref[1] = (da * g * s).astype(o_ref.dtype)

    part = pl.BlockSpec((2, TM, TF), lambda f, i: (0, i, f))
    return pl.pallas_call(
        body, name=name, grid=(F // TF, M // TM),
        in_specs=[pl.BlockSpec((TM, D), lambda f, i: (i, 0)), pl.BlockSpec((TF, D), lambda f, i: (f, 0)), part],
        out_specs=part, out_shape=jax.ShapeDtypeStruct((2, M, F), BF),
        compiler_params=_cp("parallel", "parallel"))(dff, w2, gu)


def swiglu_act(gu, name):
    L, F2 = gu.shape
    F = F2 // 2
    TL = _tile(L, (256,))

    def body(gu_ref, o_ref):
        g = gu_ref[:, :F].astype(F32)
        u = gu_ref[:, F:].astype(F32)
        o_ref[...] = (g * jax.nn.sigmoid(g) * u).astype(o_ref.dtype)

    return pl.pallas_call(body, name=name, grid=(L // TL,), in_specs=[_rows(TL, F2)], out_specs=_rows(TL, F),
                          out_shape=jax.ShapeDtypeStruct((L, F), BF), compiler_params=_cp("parallel"))(gu)


def swiglu_bwd(da, gu, name):
    L, F2 = gu.shape
    F = F2 // 2
    TL = _tile(L, (256,))

    def body(da_ref, gu_ref, o_ref):
        g = gu_ref[:, :F].astype(F32)
        u = gu_ref[:, F:].astype(F32)
        d = da_ref[...].astype(F32)
        s = jax.nn.sigmoid(g)
        o_ref[:, :F] = (d * u * (s + g * s * (1.0 - s))).astype(o_ref.dtype)
        o_ref[:, F:] = (d * g * s).astype(o_ref.dtype)

    return pl.pallas_call(body, name=name, grid=(L // TL,), in_specs=[_rows(TL, F), _rows(TL, F2)],
                          out_specs=_rows(TL, F2), out_shape=jax.ShapeDtypeStruct((L, F2), BF),
                          compiler_params=_cp("parallel"))(da, gu)


def glu_res(o, x, mods, k_gate, name):
    L, D = x.shape
    TL = _tile(L, (512, 256))

    def body(o_ref, x_ref, m_ref, mix_ref, y_ref):
        mix = o_ref[:, :D].astype(F32) * jax.nn.sigmoid(o_ref[:, D:].astype(F32))
        mix_ref[...] = mix.astype(mix_ref.dtype)
        y_ref[...] = x_ref[...] + m_ref[k_gate:k_gate + 1, :] * mix

    return pl.pallas_call(
        body, name=name, grid=(L // TL,), in_specs=[_rows(TL, 2 * D), _rows(TL, D), _fixed(6, D)],
        out_specs=[_rows(TL, D), _rows(TL, D)],
        out_shape=[jax.ShapeDtypeStruct((L, D), BF), jax.ShapeDtypeStruct((L, D), F32)],
        compiler_params=_cp("parallel"))(o, x, mods)


def glu_bwd(dmix, o, name):
    L, D2 = o.shape
    D = D2 // 2
    TL = _tile(L, (512, 256))

    def body(d_ref, o_ref, do_ref):
        d = d_ref[...].astype(F32)
        val = o_ref[:, :D].astype(F32)
        s = jax.nn.sigmoid(o_ref[:, D:].astype(F32))
        do_ref[:, :D] = (d * s).astype(do_ref.dtype)
        do_ref[:, D:] = (d * val * s * (1.0 - s)).astype(do_ref.dtype)

    return pl.pallas_call(body, name=name, grid=(L // TL,), in_specs=[_rows(TL, D), _rows(TL, D2)],
                          out_specs=_rows(TL, D2), out_shape=jax.ShapeDtypeStruct((L, D2), BF),
                          compiler_params=_cp("parallel"))(dmix, o)


def final_loss(x, target, gamma, name):
    L, D = x.shape
    TL = _tile(L, (512, 256))

    def body(x_ref, t_ref, g_ref, l_ref, dx_ref, s_ref, acc, lacc):
        i = pl.program_id(0)

        @pl.when(i == 0)
        def _():
            acc[...] = jnp.zeros_like(acc)
            lacc[...] = jnp.zeros_like(lacc)

        xn, r = _norm_parts(x_ref[...])
        gam = g_ref[...]
        e = xn * gam - t_ref[...]
        lacc[...] += jnp.sum(0.5 * jnp.mean(e * e, axis=-1, keepdims=True), axis=0, keepdims=True)
        dy = e * (1.0 / D)
        dxn = dy * gam
        dx_ref[...] = r * (dxn - xn * jnp.mean(dxn * xn, axis=-1, keepdims=True))
        acc[...] += _rowsum8(dy * xn)

        @pl.when(i == pl.num_programs(0) - 1)
        def _():
            s_ref[...] = jnp.zeros_like(s_ref)
            s_ref[0:1, :] = jnp.sum(acc[...], axis=0, keepdims=True)
            l_ref[...] = jnp.broadcast_to(lacc[...], l_ref.shape)

    return pl.pallas_call(
        body, name=name, grid=(L // TL,), in_specs=[_rows(TL, D), _rows(TL, D), _fixed(1, D)],
        out_specs=[_fixed(8, 128), _rows(TL, D), _fixed(8, D)],
        out_shape=[jax.ShapeDtypeStruct((8, 128), F32), jax.ShapeDtypeStruct((L, D), F32),
                   jax.ShapeDtypeStruct((8, D), F32)],
        scratch_shapes=[pltpu.VMEM((8, D), F32), pltpu.VMEM((1, 1), F32)],
        compiler_params=_cp("arbitrary"))(x, target, gamma)


def _col(L, TC, off):
    return pl.BlockSpec((L, TC), lambda j: (0, off + j))


def _shift_down(v, k, row):
    return jnp.where(row >= k, pltpu.roll(v, k, 0), 0.0)


def _shift_up(v, k, row, L):
    return jnp.where(row < L - k, pltpu.roll(v, L - k, 0), 0.0)


def conv_fwd(p, w, name):
    L, D3 = p.shape
    D = D3 // 3
    TC = _tile(D, (128,))
    nc = D // TC

    def body(b_ref, c_ref, v_ref, w_ref, o_ref):
        row = lax.broadcasted_iota(jnp.int32, (L, TC), 0)
        cv = c_ref[...].astype(F32) * v_ref[...].astype(F32)
        conv = w_ref[2:3, :] * cv + w_ref[1:2, :] * _shift_down(cv, 1, row) + w_ref[0:1, :] * _shift_down(cv, 2, row)
        o_ref[...] = (b_ref[...].astype(F32) * conv).astype(o_ref.dtype)

    return pl.pallas_call(
        body, name=name, grid=(nc,),
        in_specs=[_col(L, TC, 0), _col(L, TC, nc), _col(L, TC, 2 * nc), pl.BlockSpec((3, TC), lambda j: (0, j))],
        out_specs=_col(L, TC, 0), out_shape=jax.ShapeDtypeStruct((L, D), BF), compiler_params=_cp("parallel"))(p, p, p, w)


def conv_bwd(dm, p, w, name):
    L, D3 = p.shape
    D = D3 // 3
    TC = _tile(D, (128,))
    nc = D // TC

    def body(dm_ref, b_ref, c_ref, v_ref, w_ref, db_ref, dc_ref, dv_ref, dw_ref):
        row = lax.broadcasted_iota(jnp.int32, (L, TC), 0)
        cg, vv = c_ref[...].astype(F32), v_ref[...].astype(F32)
        cv = cg * vv
        cv1, cv2 = _shift_down(cv, 1, row), _shift_down(cv, 2, row)
        conv = w_ref[2:3, :] * cv + w_ref[1:2, :] * cv1 + w_ref[0:1, :] * cv2
        dmv = dm_ref[...].astype(F32)
        db_ref[...] = (dmv * conv).astype(db_ref.dtype)
        dconv = dmv * b_ref[...].astype(F32)
        dcv = (w_ref[2:3, :] * dconv + w_ref[1:2, :] * _shift_up(dconv, 1, row, L)
               + w_ref[0:1, :] * _shift_up(dconv, 2, row, L))
        dc_ref[...] = (dcv * vv).astype(dc_ref.dtype)
        dv_ref[...] = (dcv * cg).astype(dv_ref.dtype)
        dw_ref[...] = jnp.zeros_like(dw_ref)
        dw_ref[0:1, :] = jnp.sum(dconv * cv2, axis=0, keepdims=True)
        dw_ref[1:2, :] = jnp.sum(dconv * cv1, axis=0, keepdims=True)
        dw_ref[2:3, :] = jnp.sum(dconv * cv, axis=0, keepdims=True)

    one = jax.ShapeDtypeStruct((L, D), BF)
    return pl.pallas_call(
        body, name=name, grid=(nc,),
        in_specs=[_col(L, TC, 0), _col(L, TC, 0), _col(L, TC, nc), _col(L, TC, 2 * nc),
                  pl.BlockSpec((3, TC), lambda j: (0, j))],
        out_specs=[_col(L, TC, 0), _col(L, TC, 0), _col(L, TC, 0), pl.BlockSpec((8, TC), lambda j: (0, j))],
        out_shape=[one, one, one, jax.ShapeDtypeStruct((8, D), F32)],
        compiler_params=_cp("parallel"))(dm, p, p, p, w)


def _gelu(y):
    return 0.5 * y * (1.0 + jnp.tanh(GELU_C * (y + GELU_A * y * y * y)))


def _gelu_grad(y):
    th = jnp.tanh(GELU_C * (y + GELU_A * y * y * y))
    return 0.5 * (1.0 + th) + 0.5 * y * (1.0 - th * th) * GELU_C * (1.0 + 3.0 * GELU_A * y * y)


def _cmul_add(br, bi, ar, ai, sr, si):
    return br + ar * sr - ai * si, bi + ar * si + ai * sr


def _log2(n):
    k = n.bit_length() - 1
    assert 1 << k == n
    return k


def _replicate(P2, W2, P, GLP, transposed):
    shape = (W2, P2) if transposed else (P2, W2)
    k = lax.broadcasted_iota(jnp.int32, shape, 1 if transposed else 0)
    c = lax.broadcasted_iota(jnp.int32, shape, 0 if transposed else 1)
    return ((k >> _log2(P)) == (c >> _log2(GLP))) & ((k & (P - 1)) == (c & (P - 1)))


def _on_diagonal(KB, W2, H, P, GLP, transposed):
    shape = (W2, KB) if transposed else (KB, W2)
    r = lax.broadcasted_iota(jnp.int32, shape, 1 if transposed else 0)
    c = lax.broadcasted_iota(jnp.int32, shape, 0 if transposed else 1)
    return (r >> _log2(H)) == ((c & (GLP - 1)) >> _log2(P))


def _expand(t, dims, transposed):
    KB, W2, H, P, GLP = dims
    rep = _replicate(2 * P, W2, P, GLP, transposed).astype(t.dtype)
    wide = jnp.dot(rep, t, preferred_element_type=F32) if transposed else jnp.dot(t, rep, preferred_element_type=F32)
    return jnp.where(_on_diagonal(KB, W2, H, P, GLP, transposed), wide, 0.0).astype(t.dtype)


def _extract(acc, dims):
    KB, W2, H, P, GLP = dims
    rep = _replicate(2 * P, W2, P, GLP, True).astype(F32)
    kept = jnp.where(_on_diagonal(KB, W2, H, P, GLP, False), acc, 0.0)
    return jnp.dot(kept, rep, preferred_element_type=F32, precision=lax.Precision.HIGHEST)


def _cmul(ar, ai, sr, si):
    return ar * sr - ai * si, ar * si + ai * sr


LANES = 128


def _cols(ref, base, n, rows):
    return jnp.concatenate([ref[base + q, rows, :] for q in range(n)], axis=1)


def _set_cols(ref, base, n, rows, val):
    for q in range(n):
        ref[base + q, rows, :] = val[:, q * LANES:(q + 1) * LANES]


def _strided_s5_fwd(h, tb, tct, pw, dvec, name):
    L, D = h.shape
    nkb, KB, P2 = tb.shape
    P = P2 // 2
    W = (KB // SSM_GROUP) * P
    W2 = 2 * W
    dims = (KB, W2, SSM_GROUP, P, W)
    TL = _tile(L, (512, 256))
    CH = TL // 8
    NC = W // LANES

    def body(h_ref, tb_ref, tct_ref, pw_ref, d_ref, s_ref, y_ref, z_ref, bw, cw, carry):
        t = pl.program_id(1)

        @pl.when(t == 0)
        def _():
            carry[...] = jnp.zeros_like(carry)
            bw[...] = _expand(tb_ref[...], dims, False)
            cw[...] = _expand(tct_ref[...], dims, True)

        hv = h_ref[...]
        _set_cols(s_ref, 0, 2 * NC, slice(None), jnp.dot(hv.astype(BF), bw[...], preferred_element_type=F32))
        ar, ai = pw_ref[0:8, :W], pw_ref[0:8, W:]
        xr = xi = jnp.zeros((8, W), F32)
        for j in range(CH):
            rows = pl.ds(j, 8, stride=CH)
            xr, xi = _cmul_add(_cols(s_ref, 0, NC, rows), _cols(s_ref, NC, NC, rows), ar, ai, xr, xi)
            _set_cols(s_ref, 0, NC, rows, xr)
            _set_cols(s_ref, NC, NC, rows, xi)
        for k, off in ((1, 8), (2, 16), (4, 24)):
            xr, xi = _cmul_add(xr, xi, pw_ref[off:off + 8, :W], pw_ref[off:off + 8, W:],
                               pltpu.roll(xr, k, 0), pltpu.roll(xi, k, 0))
        xr, xi = _cmul_add(xr, xi, pw_ref[32:40, :W], pw_ref[32:40, W:], carry[0], carry[1])
        first = lax.broadcasted_iota(jnp.int32, (8, W), 0) == 0
        cr = jnp.where(first, carry[0], pltpu.roll(xr, 1, 0))
        ci = jnp.where(first, carry[1], pltpu.roll(xi, 1, 0))
        carry[0] = jnp.broadcast_to(xr[7:8], (8, W))
        carry[1] = jnp.broadcast_to(xi[7:8], (8, W))
        for j in range(CH):
            rows = pl.ds(j, 8, stride=CH)
            cr, ci = _cmul(ar, ai, cr, ci)
            _set_cols(s_ref, 0, NC, rows, _cols(s_ref, 0, NC, rows) + cr)
            _set_cols(s_ref, NC, NC, rows, _cols(s_ref, NC, NC, rows) + ci)
        sv = _cols(s_ref, 0, 2 * NC, slice(None))
        y = jnp.dot(sv.astype(BF), cw[...], preferred_element_type=F32) + d_ref[...] * hv
        y_ref[...] = y
        z_ref[...] = _gelu(y).astype(z_ref.dtype)

    blk = lambda kb, t: (t, kb)
    per_kb = lambda kb, t: (kb, 0, 0)
    return pl.pallas_call(
        body, name=name, grid=(nkb, L // TL),
        in_specs=[pl.BlockSpec((TL, KB), blk), pl.BlockSpec((None, KB, P2), per_kb),
                  pl.BlockSpec((None, P2, KB), per_kb), pl.BlockSpec((None, 40, W2), per_kb),
                  pl.BlockSpec((1, KB), lambda kb, t: (0, kb))],
        out_specs=[pl.BlockSpec((2 * NC, TL, LANES), lambda kb, t: (kb, t, 0)), pl.BlockSpec((TL, KB), blk),
                   pl.BlockSpec((TL, KB), blk)],
        out_shape=[jax.ShapeDtypeStruct((nkb * 2 * NC, L, LANES), F32), jax.ShapeDtypeStruct((L, D), F32),
                   jax.ShapeDtypeStruct((L, D), BF)],
        scratch_shapes=[pltpu.VMEM((KB, W2), BF), pltpu.VMEM((W2, KB), BF), pltpu.VMEM((2, 8, W), F32)],
        compiler_params=_cp("parallel", "arbitrary"))(h, tb, tct, pw, dvec)


def _strided_s5_bwd(dz, y, h, s, tc, tbt, pwr, dvec, name):
    L, D = h.shape
    nkb, KB, P2 = tc.shape
    P = P2 // 2
    W = (KB // SSM_GROUP) * P
    W2 = 2 * W
    dims = (KB, W2, SSM_GROUP, P, W)
    TL = _tile(L, (512, 256))
    CH = TL // 8
    NC = W // LANES
    nt = L // TL

    def body(dz_ref, y_ref, h_ref, s_ref, sp_ref, tc_ref, tbt_ref, pw_ref, d_ref,
             dh_ref, dd_ref, da_ref, db_ref, dc_ref, g, ctw, btw, dbacc, dcacc, carry):
        t = pl.program_id(1)

        @pl.when(t == 0)
        def _():
            carry[...] = jnp.zeros_like(carry)
            dd_ref[...] = jnp.zeros_like(dd_ref)
            da_ref[...] = jnp.zeros_like(da_ref)
            dbacc[...] = jnp.zeros_like(dbacc)
            dcacc[...] = jnp.zeros_like(dcacc)
            ctw[...] = _expand(tc_ref[...], dims, False)
            btw[...] = _expand(tbt_ref[...], dims, True)

        hv = h_ref[...]
        dy = dz_ref[...].astype(F32) * _gelu_grad(y_ref[...])
        dd_ref[...] += _rowsum8(dy * hv)
        dyb = dy.astype(BF)
        _set_cols(g, 0, 2 * NC, slice(None), jnp.dot(dyb, ctw[...], preferred_element_type=F32))
        ar, ai = pw_ref[0:8, :W], pw_ref[0:8, W:]
        gr = gi = jnp.zeros((8, W), F32)
        for j in reversed(range(CH)):
            rows = pl.ds(j, 8, stride=CH)
            gr, gi = _cmul_add(_cols(g, 0, NC, rows), _cols(g, NC, NC, rows), ar, ai, gr, gi)
            _set_cols(g, 0, NC, rows, gr)
            _set_cols(g, NC, NC, rows, gi)
        for k, off in ((1, 8), (2, 16), (4, 24)):
            gr, gi = _cmul_add(gr, gi, pw_ref[off:off + 8, :W], pw_ref[off:off + 8, W:],
                               pltpu.roll(gr, 8 - k, 0), pltpu.roll(gi, 8 - k, 0))
        gr, gi = _cmul_add(gr, gi, pw_ref[32:40, :W], pw_ref[32:40, W:], carry[0], carry[1])
        sub = lax.broadcasted_iota(jnp.int32, (8, W), 0)
        cr = jnp.where(sub == 7, carry[0], pltpu.roll(gr, 7, 0))
        ci = jnp.where(sub == 7, carry[1], pltpu.roll(gi, 7, 0))
        carry[0] = jnp.broadcast_to(gr[0:1], (8, W))
        carry[1] = jnp.broadcast_to(gi[0:1], (8, W))
        live = jnp.where(t == nt - 1, 0.0, 1.0)
        accr = acci = jnp.zeros((8, W), F32)
        for j in reversed(range(CH)):
            rows = pl.ds(j, 8, stride=CH)
            cr, ci = _cmul(ar, ai, cr, ci)
            gr, gi = _cols(g, 0, NC, rows) + cr, _cols(g, NC, NC, rows) + ci
            _set_cols(g, 0, NC, rows, gr)
            _set_cols(g, NC, NC, rows, gi)
            if j > 0:
                before = pl.ds(j - 1, 8, stride=CH)
                pr, pi = _cols(s_ref, 0, NC, before), _cols(s_ref, NC, NC, before)
            else:
                last = pl.ds(CH - 1, 8, stride=CH)
                pr = jnp.where(sub == 0, _cols(sp_ref, 0, NC, slice(7, 8)) * live,
                               pltpu.roll(_cols(s_ref, 0, NC, last), 1, 0))
                pi = jnp.where(sub == 0, _cols(sp_ref, NC, NC, slice(7, 8)) * live,
                               pltpu.roll(_cols(s_ref, NC, NC, last), 1, 0))
            accr = accr + pr * gr + pi * gi
            acci = acci + pr * gi - pi * gr
        da_ref[:, :W] += accr
        da_ref[:, W:] += acci

        gb = _cols(g, 0, 2 * NC, slice(None)).astype(BF)
        dh_ref[...] = dy * d_ref[...] + jnp.dot(gb, btw[...], preferred_element_type=F32)
        tn = (((0,), (0,)), ((), ()))
        dbacc[...] += lax.dot_general(hv.astype(BF), gb, tn, preferred_element_type=F32)
        dcacc[...] += lax.dot_general(dyb, _cols(s_ref, 0, 2 * NC, slice(None)).astype(BF), tn,
                                      preferred_element_type=F32)

        @pl.when(t == nt - 1)
        def _():
            db_ref[...] = _extract(dbacc[...], dims)
            dc_ref[...] = _extract(dcacc[...], dims)

    rev = lambda kb, t: (nt - 1 - t, kb)
    per_kb = lambda kb, t: (kb, 0, 0)
    return pl.pallas_call(
        body, name=name, grid=(nkb, nt),
        in_specs=[pl.BlockSpec((TL, KB), rev), pl.BlockSpec((TL, KB), rev), pl.BlockSpec((TL, KB), rev),
                  pl.BlockSpec((2 * NC, TL, LANES), lambda kb, t: (kb, nt - 1 - t, 0)),
                  pl.BlockSpec((2 * NC, 8, LANES), lambda kb, t: (kb, jnp.maximum((nt - 1 - t) * CH - 1, 0), 0)),
                  pl.BlockSpec((None, KB, P2), per_kb), pl.BlockSpec((None, P2, KB), per_kb),
                  pl.BlockSpec((None, 40, W2), per_kb), pl.BlockSpec((1, KB), lambda kb, t: (0, kb))],
        out_specs=[pl.BlockSpec((TL, KB), rev), pl.BlockSpec((8, KB), lambda kb, t: (0, kb)),
                   pl.BlockSpec((None, 8, W2), per_kb), pl.BlockSpec((None, KB, P2), per_kb),
                   pl.BlockSpec((None, KB, P2), per_kb)],
        out_shape=[jax.ShapeDtypeStruct((L, D), F32), jax.ShapeDtypeStruct((8, D), F32),
                   jax.ShapeDtypeStruct((nkb, 8, W2), F32), jax.ShapeDtypeStruct((nkb, KB, P2), F32),
                   jax.ShapeDtypeStruct((nkb, KB, P2), F32)],
        scratch_shapes=[pltpu.VMEM((2 * NC, TL, LANES), F32), pltpu.VMEM((KB, W2), BF), pltpu.VMEM((W2, KB), BF),
                        pltpu.VMEM((KB, W2), F32), pltpu.VMEM((KB, W2), F32), pltpu.VMEM((2, 8, W), F32)],
        compiler_params=_cp("parallel", "arbitrary"))(dz, y, h, s, s, tc, tbt, pwr, dvec)


def _chunk_order(TL, CH, transposed):
    out_row = lax.broadcasted_iota(jnp.int32, (TL, TL), 1 if transposed else 0)
    in_row = lax.broadcasted_iota(jnp.int32, (TL, TL), 0 if transposed else 1)
    return in_row == ((out_row & 7) << _log2(CH)) + (out_row >> 3)


def _reorder(perm, v):
    hi = v.astype(perm.dtype)
    lo = (v - hi.astype(F32)).astype(perm.dtype)
    return jnp.dot(perm, hi, preferred_element_type=F32) + jnp.dot(perm, lo, preferred_element_type=F32)


def s5_fwd(h, tb, tct, pw, dvec, name):
    L, D = h.shape
    nkb, KB, P2 = tb.shape
    P = P2 // 2
    W = (KB // SSM_GROUP) * P
    W2 = 2 * W
    dims = (KB, W2, SSM_GROUP, P, W)
    TL = _tile(L, (512, 256))
    CH = TL // 8

    def body(h_ref, tb_ref, tct_ref, pw_ref, d_ref, s_ref, y_ref, z_ref, bw, cw, perm, unperm, carry):
        t = pl.program_id(1)

        @pl.when(t == 0)
        def _():
            carry[...] = jnp.zeros_like(carry)
            bw[...] = _expand(tb_ref[...], dims, False)
            cw[...] = _expand(tct_ref[...], dims, True)
            perm[...] = _chunk_order(TL, CH, False).astype(perm.dtype)
            unperm[...] = _chunk_order(TL, CH, True).astype(perm.dtype)

        hp = _reorder(perm[...], h_ref[...])
        s_ref[...] = jnp.dot(hp.astype(BF), bw[...], preferred_element_type=F32)
        ar, ai = pw_ref[0:8, :W], pw_ref[0:8, W:]

        def own(j, x):
            rows = pl.ds(pl.multiple_of(j * 8, 8), 8)
            xr, xi = _cmul_add(s_ref[rows, :W], s_ref[rows, W:], ar, ai, x[0], x[1])
            s_ref[rows, :W] = xr
            s_ref[rows, W:] = xi
            return xr, xi

        zero = jnp.zeros((8, W), F32)
        xr, xi = lax.fori_loop(0, CH, own, (zero, zero))
        for k, off in ((1, 8), (2, 16), (4, 24)):
            xr, xi = _cmul_add(xr, xi, pw_ref[off:off + 8, :W], pw_ref[off:off + 8, W:],
                               pltpu.roll(xr, k, 0), pltpu.roll(xi, k, 0))
        xr, xi = _cmul_add(xr, xi, pw_ref[32:40, :W], pw_ref[32:40, W:], carry[0], carry[1])
        first = lax.broadcasted_iota(jnp.int32, (8, W), 0) == 0
        cr = jnp.where(first, carry[0], pltpu.roll(xr, 1, 0))
        ci = jnp.where(first, carry[1], pltpu.roll(xi, 1, 0))
        carry[0] = jnp.broadcast_to(xr[7:8], (8, W))
        carry[1] = jnp.broadcast_to(xi[7:8], (8, W))

        def carried(j, c):
            rows = pl.ds(pl.multiple_of(j * 8, 8), 8)
            cr, ci = _cmul(ar, ai, c[0], c[1])
            s_ref[rows, :W] = s_ref[rows, :W] + cr
            s_ref[rows, W:] = s_ref[rows, W:] + ci
            return cr, ci

        lax.fori_loop(0, CH, carried, (cr, ci))
        y = jnp.dot(s_ref[...].astype(BF), cw[...], preferred_element_type=F32) + d_ref[...] * hp
        y_ref[...] = y
        z_ref[...] = jnp.dot(unperm[...], _gelu(y).astype(BF), preferred_element_type=F32).astype(z_ref.dtype)

    blk = lambda kb, t: (t, kb)
    per_kb = lambda kb, t: (kb, 0, 0)
    return pl.pallas_call(
        body, name=name, grid=(nkb, L // TL),
        in_specs=[pl.BlockSpec((TL, KB), blk), pl.BlockSpec((None, KB, P2), per_kb),
                  pl.BlockSpec((None, P2, KB), per_kb), pl.BlockSpec((None, 40, W2), per_kb),
                  pl.BlockSpec((1, KB), lambda kb, t: (0, kb))],
        out_specs=[pl.BlockSpec((TL, W2), blk), pl.BlockSpec((TL, KB), blk), pl.BlockSpec((TL, KB), blk)],
        out_shape=[jax.ShapeDtypeStruct((L, nkb * W2), F32), jax.ShapeDtypeStruct((L, D), F32),
                   jax.ShapeDtypeStruct((L, D), BF)],
        scratch_shapes=[pltpu.VMEM((KB, W2), BF), pltpu.VMEM((W2, KB), BF), pltpu.VMEM((TL, TL), BF),
                        pltpu.VMEM((TL, TL), BF), pltpu.VMEM((2, 8, W), F32)],
        compiler_params=_cp("parallel", "arbitrary"))(h, tb, tct, pw, dvec)


def s5_bwd(dz, y, h, s, tc, tbt, pwr, dvec, name):
    L, D = h.shape
    nkb, KB, P2 = tc.shape
    P = P2 // 2
    W = (KB // SSM_GROUP) * P
    W2 = 2 * W
    dims = (KB, W2, SSM_GROUP, P, W)
    TL = _tile(L, (512, 256))
    CH = TL // 8
    nt = L // TL

    def body(dz_ref, y_ref, h_ref, s_ref, sp_ref, tc_ref, tbt_ref, pw_ref, d_ref,
             dh_ref, dd_ref, da_ref, db_ref, dc_ref, g, ctw, btw, dbacc, dcacc, perm, unperm, carry):
        t = pl.program_id(1)

        @pl.when(t == 0)
        def _():
            carry[...] = jnp.zeros_like(carry)
            dd_ref[...] = jnp.zeros_like(dd_ref)
            da_ref[...] = jnp.zeros_like(da_ref)
            dbacc[...] = jnp.zeros_like(dbacc)
            dcacc[...] = jnp.zeros_like(dcacc)
            ctw[...] = _expand(tc_ref[...], dims, False)
            btw[...] = _expand(tbt_ref[...], dims, True)
            perm[...] = _chunk_order(TL, CH, False).astype(perm.dtype)
            unperm[...] = _chunk_order(TL, CH, True).astype(perm.dtype)

        hp = jnp.dot(perm[...], h_ref[...].astype(BF), preferred_element_type=F32)
        dy = jnp.dot(perm[...], dz_ref[...].astype(BF), preferred_element_type=F32) * _gelu_grad(y_ref[...])
        dd_ref[...] += _rowsum8(dy * hp)
        dyb = dy.astype(BF)
        g[...] = jnp.dot(dyb, ctw[...], preferred_element_type=F32)
        ar, ai = pw_ref[0:8, :W], pw_ref[0:8, W:]

        def own(jj, x):
            rows = pl.ds(pl.multiple_of((CH - 1 - jj) * 8, 8), 8)
            gr, gi = _cmul_add(g[rows, :W], g[rows, W:], ar, ai, x[0], x[1])
            g[rows, :W] = gr
            g[rows, W:] = gi
            return gr, gi

        zero = jnp.zeros((8, W), F32)
        gr, gi = lax.fori_loop(0, CH, own, (zero, zero))
        for k, off in ((1, 8), (2, 16), (4, 24)):
            gr, gi = _cmul_add(gr, gi, pw_ref[off:off + 8, :W], pw_ref[off:off + 8, W:],
                               pltpu.roll(gr, 8 - k, 0), pltpu.roll(gi, 8 - k, 0))
        gr, gi = _cmul_add(gr, gi, pw_ref[32:40, :W], pw_ref[32:40, W:], carry[0], carry[1])
        sub = lax.broadcasted_iota(jnp.int32, (8, W), 0)
        cr = jnp.where(sub == 7, carry[0], pltpu.roll(gr, 7, 0))
        ci = jnp.where(sub == 7, carry[1], pltpu.roll(gi, 7, 0))
        carry[0] = jnp.broadcast_to(gr[0:1], (8, W))
        carry[1] = jnp.broadcast_to(gi[0:1], (8, W))

        def carried(jj, c):
            j = CH - 1 - jj
            rows = pl.ds(pl.multiple_of(j * 8, 8), 8)
            before = pl.ds(pl.multiple_of(j * 8 - 8, 8), 8)
            cr, ci = _cmul(ar, ai, c[0], c[1])
            gr, gi = g[rows, :W] + cr, g[rows, W:] + ci
            g[rows, :W] = gr
            g[rows, W:] = gi
            pr, pi = s_ref[before, :W], s_ref[before, W:]
            return cr, ci, c[2] + pr * gr + pi * gi, c[3] + pr * gi - pi * gr

        cr, ci, accr, acci = lax.fori_loop(0, CH - 1, carried, (cr, ci, zero, zero))
        live = jnp.where(t == nt - 1, 0.0, 1.0)
        cr, ci = _cmul(ar, ai, cr, ci)
        gr, gi = g[0:8, :W] + cr, g[0:8, W:] + ci
        g[0:8, :W] = gr
        g[0:8, W:] = gi
        pr = jnp.where(sub == 0, sp_ref[7:8, :W] * live, pltpu.roll(s_ref[TL - 8:TL, :W], 1, 0))
        pi = jnp.where(sub == 0, sp_ref[7:8, W:] * live, pltpu.roll(s_ref[TL - 8:TL, W:], 1, 0))
        da_ref[:, :W] += accr + pr * gr + pi * gi
        da_ref[:, W:] += acci + pr * gi - pi * gr

        gb = g[...].astype(BF)
        dh = dy * d_ref[...] + jnp.dot(gb, btw[...], preferred_element_type=F32)
        dh_ref[...] = _reorder(unperm[...], dh)
        tn = (((0,), (0,)), ((), ()))
        dbacc[...] += lax.dot_general(hp.astype(BF), gb, tn, preferred_element_type=F32)
        dcacc[...] += lax.dot_general(dyb, s_ref[...].astype(BF), tn, preferred_element_type=F32)

        @pl.when(t == nt - 1)
        def _():
            db_ref[...] = _extract(dbacc[...], dims)
            dc_ref[...] = _extract(dcacc[...], dims)

    rev = lambda kb, t: (nt - 1 - t, kb)
    prev = lambda kb, t: (jnp.maximum((nt - 1 - t) * CH - 1, 0), kb)
    per_kb = lambda kb, t: (kb, 0, 0)
    return pl.pallas_call(
        body, name=name, grid=(nkb, nt),
        in_specs=[pl.BlockSpec((TL, KB), rev), pl.BlockSpec((TL, KB), rev), pl.BlockSpec((TL, KB), rev),
                  pl.BlockSpec((TL, W2), rev), pl.BlockSpec((8, W2), prev),
                  pl.BlockSpec((None, KB, P2), per_kb), pl.BlockSpec((None, P2, KB), per_kb),
                  pl.BlockSpec((None, 40, W2), per_kb), pl.BlockSpec((1, KB), lambda kb, t: (0, kb))],
        out_specs=[pl.BlockSpec((TL, KB), rev), pl.BlockSpec((8, KB), lambda kb, t: (0, kb)),
                   pl.BlockSpec((None, 8, W2), per_kb), pl.BlockSpec((None, KB, P2), per_kb),
                   pl.BlockSpec((None, KB, P2), per_kb)],
        out_shape=[jax.ShapeDtypeStruct((L, D), F32), jax.ShapeDtypeStruct((8, D), F32),
                   jax.ShapeDtypeStruct((nkb, 8, W2), F32), jax.ShapeDtypeStruct((nkb, KB, P2), F32),
                   jax.ShapeDtypeStruct((nkb, KB, P2), F32)],
        scratch_shapes=[pltpu.VMEM((TL, W2), F32), pltpu.VMEM((KB, W2), BF), pltpu.VMEM((W2, KB), BF),
                        pltpu.VMEM((KB, W2), F32), pltpu.VMEM((KB, W2), F32), pltpu.VMEM((TL, TL), BF),
                        pltpu.VMEM((TL, TL), BF), pltpu.VMEM((2, 8, W), F32)],
        compiler_params=_cp("parallel", "arbitrary"))(dz, y, h, s, s, tc, tbt, pwr, dvec)


def _discretise(a_re, a_im, log_step, b_re, b_im):
    lr = jnp.minimum(a_re, -1e-4)
    li = a_im
    dt = jnp.exp(log_step)[:, None]
    mag = jnp.exp(lr * dt)
    abr = mag * jnp.cos(li * dt)
    abi = mag * jnp.sin(li * dt)
    den = lr * lr + li * li
    qr = ((abr - 1.0) * lr + abi * li) / den
    qi = (abi * lr - (abr - 1.0) * li) / den
    bbar_re = qr[..., None] * b_re - qi[..., None] * b_im
    bbar_im = qr[..., None] * b_im + qi[..., None] * b_re
    return abr, abi, bbar_re, bbar_im


def _compact(m_re, m_im, nkb):
    G, H, P = m_re.shape
    t = jnp.stack([m_re, m_im], axis=2).reshape(nkb, (G // nkb) * H, 2 * P).astype(BF)
    return t, jnp.swapaxes(t, 1, 2)


def _scan_powers(abr, abi, nkb, conj, CH):
    G, P = abr.shape
    if conj:
        abi = -abi

    def cmul(u, v):
        return u[0] * v[0] - u[1] * v[1], u[0] * v[1] + u[1] * v[0]

    q = (abr, abi)
    for _ in range(_log2(CH)):
        q = cmul(q, q)
    pows = [q]
    for _ in range(7):
        pows.append(cmul(pows[-1], q))
    row = jnp.arange(8)[:, None, None]

    def table(part):
        out = [jnp.broadcast_to((abr, abi)[part][None], (8, G, P))]
        for k in (1, 2, 4):
            keep = (row <= 7 - k) if conj else (row >= k)
            out.append(jnp.where(keep, pows[k - 1][part][None], 0.0))
        ends = jnp.stack([p[part] for p in pows])
        out.append(ends[::-1] if conj else ends)
        return jnp.concatenate(out, axis=0)

    GL = G // nkb
    t = jnp.stack([table(0), table(1)], axis=1)
    t = t.reshape(40, 2, nkb, GL * P).transpose(2, 0, 1, 3)
    return t.reshape(nkb, 40, 2 * GL * P)


def ada_mods(c_all, w_ada, b_sh, name):
    nl, D, NA = w_ada.shape

    def body(c_ref, w_ref, b_ref, o_ref):
        cv = c_ref[...]
        act = cv * jax.nn.sigmoid(cv)
        o_ref[...] = jnp.dot(act, w_ref[...], preferred_element_type=F32, precision=lax.Precision.HIGHEST) + b_ref[...]

    return pl.pallas_call(
        body, name=name, grid=(nl,),
        in_specs=[pl.BlockSpec((8, D), lambda i: (0, 0)), pl.BlockSpec((None, D, NA), lambda i: (i, 0, 0)),
                  pl.BlockSpec((None, 1, NA), lambda i: (i, 0, 0))],
        out_specs=pl.BlockSpec((None, 8, NA), lambda i: (i, 0, 0)),
        out_shape=jax.ShapeDtypeStruct((nl, 8, NA), F32), compiler_params=_cp("parallel"))(c_all, w_ada, b_sh)


def _adamw(w, g, m, v):
    m = ADAM_B1 * m + (1.0 - ADAM_B1) * g
    v = ADAM_B2 * v + (1.0 - ADAM_B2) * (g * g)
    m_hat = m / (1.0 - ADAM_B1 ** ADAM_STEP)
    v_hat = v / (1.0 - ADAM_B2 ** ADAM_STEP)
    return -ADAM_LR * (m_hat / (jnp.sqrt(v_hat) + ADAM_EPS) + ADAM_WD * w), m, v


def _adam_rows(R, C):
    cap = max(8, (256 * 1024) // C)
    for t in range(min(R, cap), 0, -1):
        if R % t == 0 and (t % 8 == 0 or t == R):
            return t
    return R


def adamw_ada(c_t, dm, w, m, v, name):
    nl, D, NA = w.shape
    TK = _tile(D, (128,))

    def body(c_ref, dm_ref, w_ref, m_ref, v_ref, g_ref, d_ref, nm_ref, nv_ref):
        cv = c_ref[...]
        act = cv * jax.nn.sigmoid(cv)
        g = act[:, 0:1] * dm_ref[0:1, :]
        for b in range(1, 8):
            g = g + act[:, b:b + 1] * dm_ref[b:b + 1, :]
        g_ref[...] = g
        d_ref[...], nm_ref[...], nv_ref[...] = _adamw(w_ref[...], g, m_ref[...], v_ref[...])

    big = pl.BlockSpec((None, TK, NA), lambda i, k: (i, k, 0))
    shape = jax.ShapeDtypeStruct(w.shape, F32)
    return pl.pallas_call(
        body, name=name, grid=(nl, D // TK),
        in_specs=[pl.BlockSpec((TK, 8), lambda i, k: (k, 0)), pl.BlockSpec((None, 8, NA), lambda i, k: (i, 0, 0)),
                  big, big, big],
        out_specs=[big] * 4, out_shape=[shape] * 4, compiler_params=_cp("parallel", "parallel"))(c_t, dm, w, m, v)


def adamw_sharded(w, m, v, ga, gb, name):
    nl, R, C = w.shape
    TR = _adam_rows(R, C)

    def body(w_ref, m_ref, v_ref, a_ref, b_ref, g_ref, d_ref, nm_ref, nv_ref):
        g = a_ref[...] + b_ref[...]
        g_ref[...] = g
        d_ref[...], nm_ref[...], nv_ref[...] = _adamw(w_ref[...], g, m_ref[...], v_ref[...])

    big = pl.BlockSpec((None, TR, C), lambda i, r: (i, r, 0))
    shape = jax.ShapeDtypeStruct(w.shape, F32)
    return pl.pallas_call(
        body, name=name, grid=(nl, R // TR), in_specs=[big] * 5,
        out_specs=[big] * 4, out_shape=[shape] * 4, compiler_params=_cp("parallel", "parallel"))(w, m, v, ga, gb)


def adamw_slab(g, w, m, v, name):
    R, C = g.shape
    TR = _tile(R, (160, 80, 40, 8))

    def body(g_ref, w_ref, m_ref, v_ref, d_ref, nm_ref, nv_ref):
        d_ref[...], nm_ref[...], nv_ref[...] = _adamw(w_ref[...], g_ref[...], m_ref[...], v_ref[...])

    big = pl.BlockSpec((TR, C), lambda r: (r, 0))
    shape = jax.ShapeDtypeStruct((R, C), F32)
    return pl.pallas_call(
        body, name=name, grid=(R // TR,), in_specs=[big] * 4,
        out_specs=[big] * 3, out_shape=[shape] * 3, compiler_params=_cp("parallel"))(g, w, m, v)


def adamw_plain(w, m, v, g, name):
    def body(w_ref, m_ref, v_ref, g_ref, d_ref, nm_ref, nv_ref):
        d_ref[...], nm_ref[...], nv_ref[...] = _adamw(w_ref[...], g_ref[...], m_ref[...], v_ref[...])

    shape = jax.ShapeDtypeStruct(w.shape, F32)
    return pl.pallas_call(body, name=name, out_shape=[shape] * 3,
                          compiler_params=pltpu.CompilerParams(vmem_limit_bytes=VMEM_LIMIT))(w, m, v, g)


def _slab_rows(a):
    n = a.size
    rows = -(-n // SLAB_W)
    return -(-rows // 8) * 8


def _pack(arrs, pad_rows_to=0):
    out = []
    for a in arrs:
        rows = _slab_rows(a)
        flat = a.reshape(-1).astype(F32)
        flat = jnp.pad(flat, (0, rows * SLAB_W - flat.shape[0]))
        out.append(flat.reshape(rows, SLAB_W))
    total = sum(o.shape[0] for o in out)
    if pad_rows_to and total % pad_rows_to:
        out.append(jnp.zeros((pad_rows_to - total % pad_rows_to, SLAB_W), F32))
    return jnp.concatenate(out, axis=0)


def _unpack(slab, like):
    out, r = [], 0
    for a in like:
        rows = _slab_rows(a)
        out.append(slab[r:r + rows].reshape(-1)[:a.size].reshape(a.shape))
        r += rows
    return out


WEIGHTS = ['norm1_g', 'norm2_g', 'w_ada', 'b_ada', 'ssm_a_re', 'ssm_a_im', 'ssm_log_step', 'ssm_b_re', 'ssm_b_im',
           'ssm_c_re', 'ssm_c_im', 'ssm_d', 'ssm_w_out', 'conv_w_in', 'conv_w', 'conv_w_out', 'w_ffn_in',
           'w_ffn_out', 'final_g']
SLAB = ['norm1_g', 'norm2_g', 'b_ada', 'ssm_a_re', 'ssm_a_im', 'ssm_log_step', 'ssm_b_re', 'ssm_b_im', 'ssm_c_re',
        'ssm_c_im', 'ssm_d', 'final_g']
SHARDED = ['ssm_w_out', 'conv_w_in', 'conv_w_out', 'w_ffn_in', 'w_ffn_out']


def kernel(x, c, norm1_g, norm2_g, w_ada, b_ada, ssm_a_re, ssm_a_im, ssm_log_step, ssm_b_re, ssm_b_im, ssm_c_re, ssm_c_im, ssm_d, ssm_w_out, conv_w_in, conv_w, conv_w_out, w_ffn_in, w_ffn_out, final_g, loss_target, m_norm1_g, m_norm2_g, m_w_ada, m_b_ada, m_ssm_a_re, m_ssm_a_im, m_ssm_log_step, m_ssm_b_re, m_ssm_b_im, m_ssm_c_re, m_ssm_c_im, m_ssm_d, m_ssm_w_out, m_conv_w_in, m_conv_w, m_conv_w_out, m_w_ffn_in, m_w_ffn_out, m_final_g, v_norm1_g, v_norm2_g, v_w_ada, v_b_ada, v_ssm_a_re, v_ssm_a_im, v_ssm_log_step, v_ssm_b_re, v_ssm_b_im, v_ssm_c_re, v_ssm_c_im, v_ssm_d, v_ssm_w_out, v_conv_w_in, v_conv_w, v_conv_w_out, v_w_ffn_in, v_w_ffn_out, v_final_g):
    given = dict(locals())
    W = {n: given[n] for n in WEIGHTS}
    Mo = {n: given["m_" + n] for n in WEIGHTS}
    Vo = {n: given["v_" + n] for n in WEIGHTS}

    xs = x[0]
    tgt = loss_target[0]
    L, D = xs.shape
    nlayer = norm1_g.shape[0]
    NA = w_ada.shape[2]
    G = ssm_a_re.shape[1]
    nkb = D // S5_BLOCK
    ax, ay, ac = _axes()
    me = 4 * ax + 2 * ay + ac
    chip = 2 * ax + ay

    c_all = gather8(jnp.broadcast_to(c, (8, D)), "gather_c")[:, 0, :]
    b_sh = lax.dynamic_slice_in_dim(b_ada, chip * NA, NA, axis=1)[:, None, :]
    mods_part = ada_mods(c_all, w_ada, b_sh, "ada_mods")
    mg = gather8(mods_part.reshape(nlayer * 8, NA), "gather_mods")
    mg = mg.reshape(N_CHIP, 2, nlayer, 8, NA)[:, 0]
    mods_all = lax.dynamic_index_in_dim(mg, me, axis=2, keepdims=False)
    mods_all = jnp.transpose(mods_all, (1, 0, 2)).reshape(nlayer, 6, D)

    cw_parts = gather8(_pack([conv_w]), "gather_conv_w")
    nconv = conv_w.shape[0]
    cw_full = jnp.stack([_unpack(cw_parts[2 * q], [conv_w])[0] for q in range(N_CHIP)], axis=2)
    cw_full = cw_full.reshape(nconv, 3, D)

    use = []
    for i in range(nlayer):
        use += [("ssm_w_out", i // 2, i)] if i % 2 == 0 else [("conv_w_in", i // 2, i), ("conv_w_out", i // 2, i)]
        use += [("w_ffn_in", i, i), ("w_ffn_out", i, i)]
    g_sems, g_srcs, g_lands, token = gather_start([W[n][j].astype(BF) for n, j, _ in use],
                                                  cw_full + mods_all[0, 0:3], "gather_start")
    mods_all = mods_all + token[0:1, 0:1]

    def layer_weights(i, after):
        idx = [a for a, (_, _, li) in enumerate(use) if li == i]
        got = gather_wait(g_sems, g_srcs, g_lands, idx, after, "gather_wait%d" % i)
        return {use[a][0]: w for a, w in zip(idx, got)}

    s5 = []
    for j in range(ssm_a_re.shape[0]):
        disc, disc_vjp = jax.vjp(_discretise, ssm_a_re[j], ssm_a_im[j], ssm_log_step[j], ssm_b_re[j], ssm_b_im[j])
        abr, abi, bbar_re, bbar_im = disc
        tb, tbt = _compact(jnp.swapaxes(bbar_re, 1, 2), jnp.swapaxes(bbar_im, 1, 2), nkb)
        tc, tct = _compact(ssm_c_re[j], -ssm_c_im[j], nkb)
        chunk = _tile(L, (512, 256)) // 8
        s5.append(dict(vjp=disc_vjp, tb=tb, tbt=tbt, tc=tc, tct=tct, pw=_scan_powers(abr, abi, nkb, False, chunk),
                       pwr=_scan_powers(abr, abi, nkb, True, chunk)))

    saved = []
    xcur = xs
    for i in range(nlayer):
        j = i // 2
        mods = mods_all[i]
        sv = dict(x=xcur)
        if i % 2 == 0:
            h = norm_mod(xcur, norm1_g[i:i + 1], mods, 0, F32, "norm_mod_s5")
            states, yv, z = s5_fwd(h, s5[j]["tb"], s5[j]["tct"], s5[j]["pw"], ssm_d[j:j + 1], "s5_fwd")
            full = layer_weights(i, z)
            o = mm_nn(z, full["ssm_w_out"], BF, "mm_ssm_out")
            mix, x2 = glu_res(o, xcur, mods, 2, "glu_res")
            sv.update(h=h, states=states, y=yv, z=z, o=o)
        else:
            h = norm_mod(xcur, norm1_g[i:i + 1], mods, 0, BF, "norm_mod")
            full = layer_weights(i, h)
            p = mm_nn(h, full["conv_w_in"], BF, "mm_conv_in")
            mc = conv_fwd(p, cw_full[j], "conv_fwd")
            mix, x2 = mm_nn(mc, full["conv_w_out"].reshape(1, D, D), BF, "mm_conv_out", res=xcur, gate=mods[2:3])
            sv.update(h=h, p=p, mc=mc)
        h2 = norm_mod(x2, norm2_g[i:i + 1], mods, 3, BF, "norm_mod")
        gu, act = ffn_in_act(h2, full["w_ffn_in"], "ffn_in_act")
        F = act.shape[1]
        ff, x3 = mm_nn(act, full["w_ffn_out"].reshape(1, F, D), BF, "mm_ffn_out", res=x2, gate=mods[5:6])
        sv.update(mix=mix, x2=x2, h2=h2, gu=gu, act=act, ff=ff, w=full)
        saved.append(sv)
        xcur = x3

    loss_blk, dx, dfinal = final_loss(xcur, tgt, final_g[None, :], "final_loss")

    gland = {n: lax.empty((W[n].shape[0], N_CHIP) + W[n].shape[1:], BF) for n in SHARDED}
    in_flight = []
    dmods = [None] * nlayer
    dnorm1, dnorm2 = [None] * nlayer, [None] * nlayer
    dconv_w = [None] * nconv
    ds5 = [None] * ssm_a_re.shape[0]
    token = jnp.zeros((8, 128), F32)

    def send_grads(names, grads, slot, after, name):
        sems, thru, lands, tok = scatter_start([grads[n] for n in names], [gland[n] for n in names], slot, after, name)
        gland.update(zip(names, lands))
        in_flight.append((names, slot, sems, thru, name))
        return tok

    def land_grads(group, after):
        for names, slot, sems, thru, name in in_flight:
            if names[0] in group:
                got = scatter_wait(sems, thru, [gland[n] for n in names], slot, after, name.replace("scatter", "landed"))
                gland.update(zip(names, got))

    for i in reversed(range(nlayer)):
        j = i // 2
        mods = mods_all[i] + token[0:1, 0:1]
        sv = saved[i]
        full = sv["w"]
        gfull = {}
        F = sv["act"].shape[1]
        dff, dg2 = gate_bwd(dx, sv["ff"], mods, 5, "gate_bwd")
        gfull["w_ffn_out"] = mm_tn(sv["act"], dff, 1, "mm_tn_ffn_out").reshape(N_CHIP, F // N_CHIP, D)
        dgu = ffn_out_bwd(dff, full["w_ffn_out"].reshape(F, D), sv["gu"], "ffn_out_bwd")
        gfull["w_ffn_in"] = mm_tn(sv["h2"], dgu, N_CHIP, "mm_tn_ffn_in")
        dh2 = mm_nt(dgu, full["w_ffn_in"], F32, "mm_nt_ffn_in")
        token = send_grads(["w_ffn_out", "w_ffn_in"], gfull, [i, i], dh2, "scatter_ffn%d" % i)
        mods = mods + token[0:1, 0:1]
        dx2, s2 = norm_bwd(dh2, sv["x2"], dx, norm2_g[i:i + 1], mods, 3, "norm_bwd")
        dmix, dg1 = gate_bwd(dx2, sv["mix"], mods, 2, "gate_bwd")
        if i % 2 == 0:
            do = glu_bwd(dmix, sv["o"], "glu_bwd")
            gfull["ssm_w_out"] = mm_tn(sv["z"], do, N_CHIP, "mm_tn_ssm_out")
            dz = mm_nt(do, full["ssm_w_out"], BF, "mm_nt_ssm_out")
            dh, dd, dab, db, dc = s5_bwd(dz, sv["y"], sv["h"], sv["states"], s5[j]["tc"], s5[j]["tbt"], s5[j]["pwr"],
                                         ssm_d[j:j + 1], "s5_bwd")
            ds5[j] = (dd, dab, db, dc)
        else:
            gfull["conv_w_out"] = mm_tn(sv["mc"], dmix, 1, "mm_tn_conv_out").reshape(N_CHIP, D // N_CHIP, D)
            dmc = mm_nt(dmix, full["conv_w_out"].reshape(1, D, D), BF, "mm_nt_conv_out")
            dbg, dcg, dvv, dcw = conv_bwd(dmc, sv["p"], cw_full[j], "conv_bwd")
            dp = jnp.concatenate([dbg, dcg, dvv], axis=1)
            gfull["conv_w_in"] = mm_tn(sv["h"], dp, N_CHIP, "mm_tn_conv_in")
            dh = mm_nt(dp, full["conv_w_in"], F32, "mm_nt_conv_in")
            dconv_w[j] = dcw[0:3]
        dx, s1 = norm_bwd(dh, sv["x"], dx2, norm1_g[i:i + 1], mods, 0, "norm_bwd")
        dmods[i] = jnp.concatenate([s1[0:2], dg1[0:1], s2[0:2], dg2[0:1]], axis=0).reshape(6 * D)
        dnorm1[i], dnorm2[i] = s1[2], s2[2]
        names = ["ssm_w_out"] if i % 2 == 0 else ["conv_w_out", "conv_w_in"]
        token = send_grads(names, gfull, [j] * len(names), dx, "scatter_mix%d" % i)

    small = dict(norm1_g=jnp.stack(dnorm1), norm2_g=jnp.stack(dnorm2), b_ada=jnp.stack(dmods), final_g=dfinal[0])
    per = {n: [] for n in ('ssm_a_re', 'ssm_a_im', 'ssm_log_step', 'ssm_b_re', 'ssm_b_im', 'ssm_c_re', 'ssm_c_im', 'ssm_d')}
    GL = G // nkb
    for j, (dd, dab, db, dc) in enumerate(ds5):
        dab = jnp.sum(dab, axis=1).reshape(nkb, 2, GL, SSM_STATE)
        g_abr, g_abi = dab[:, 0].reshape(G, SSM_STATE), dab[:, 1].reshape(G, SSM_STATE)
        db, dc = db.reshape(G, SSM_GROUP, 2, SSM_STATE), dc.reshape(G, SSM_GROUP, 2, SSM_STATE)
        gb_re, gb_im, gc_re, gc_im = db[:, :, 0], db[:, :, 1], dc[:, :, 0], dc[:, :, 1]
        ga_re, ga_im, gls, gbr, gbi = s5[j]["vjp"]((g_abr, g_abi, jnp.swapaxes(gb_re, 1, 2), jnp.swapaxes(gb_im, 1, 2)))
        for n, val in zip(per, (ga_re, ga_im, gls, gbr, gbi, gc_re, -gc_im, jnp.sum(dd, axis=0))):
            per[n].append(val)
    small.update({n: jnp.stack(vals) for n, vals in per.items()})
    dcw_full = jnp.stack(dconv_w)

    slab_like = [W[n] for n in SLAB] + [dcw_full]
    rows64 = 8 * N_DEV
    g_slab, dm_all = reduce8(_pack([small[n] for n in SLAB] + [dcw_full], rows64), _pack([small["b_ada"]]),
                             "reduce_small")
    d_slab, m_slab, v_slab = adamw_slab(
        g_slab, _pack([W[n] for n in SLAB] + [jnp.zeros_like(dcw_full)], rows64),
        _pack([Mo[n] for n in SLAB] + [jnp.zeros_like(dcw_full)], rows64),
        _pack([Vo[n] for n in SLAB] + [jnp.ones_like(dcw_full)], rows64), "adamw_slab")
    out = {}
    for k, slab in zip(("g", "d", "m", "v"), (g_slab, d_slab, m_slab, v_slab)):
        for n, val in zip(SLAB, _unpack(slab, slab_like)):
            out[k, n] = val
    g_cw = lax.dynamic_slice_in_dim(_unpack(g_slab, slab_like)[-1], chip * conv_w.shape[2], conv_w.shape[2], axis=2)
    out["g", "conv_w"] = g_cw
    out["d", "conv_w"], out["m", "conv_w"], out["v", "conv_w"] = [
        r.reshape(conv_w.shape) for r in adamw_plain(conv_w.reshape(-1, conv_w.shape[2]), m_conv_w.reshape(-1, conv_w.shape[2]),
                                                     v_conv_w.reshape(-1, conv_w.shape[2]), g_cw.reshape(-1, conv_w.shape[2]),
                                                     "adamw_conv_w")]

    early = [n for n in SHARDED if n != "ssm_w_out"]
    land_grads(early, g_slab)
    mine = [reduce4(gland[n], "reduce4_" + n) for n in early]
    w_sems, w_srcs, w_lands, token = swap_start(mine, "swap_start")

    dm_all = dm_all.reshape(N_DEV, -1)[:, :b_ada.size].reshape(N_DEV, nlayer, N_CHIP, NA)
    dm_sh = jnp.transpose(lax.dynamic_index_in_dim(dm_all, chip, axis=2, keepdims=False), (1, 0, 2))
    res = adamw_ada(jnp.transpose(c_all) + token[0:1, 0:1], dm_sh, w_ada, m_w_ada, v_w_ada, "adamw_ada")
    out["g", "w_ada"], out["d", "w_ada"], out["m", "w_ada"], out["v", "w_ada"] = res

    mine, theirs = swap_wait(w_sems, w_srcs, w_lands, out["g", "w_ada"], "swap_wait")
    for n, ga, gb in zip(early, mine, theirs):
        r = adamw_sharded(W[n], Mo[n], Vo[n], ga, gb, "adamw_" + n)
        out["g", n], out["d", n], out["m", n], out["v", n] = r

    land_grads(["ssm_w_out"], out["g", "w_ffn_out"])
    ga = reduce4(gland["ssm_w_out"], "reduce4_ssm_w_out")
    gb = swap_siblings([ga], "swap_siblings")[0]
    r = adamw_sharded(ssm_w_out, m_ssm_w_out, v_ssm_w_out, ga, gb, "adamw_ssm_w_out")
    out["g", "ssm_w_out"], out["d", "ssm_w_out"], out["m", "ssm_w_out"], out["v", "ssm_w_out"] = r

    loss = lax.psum(loss_blk[0, 0], ("x", "y", "c"))
    return (loss, dx[None], *[out["g", n] for n in WEIGHTS], *[out["d", n] for n in WEIGHTS],
            *[out["m", n] for n in WEIGHTS], *[out["v", n] for n in WEIGHTS])
```

```python
import functools
import math

import jax
import jax.numpy as jnp
from jax import lax
from jax.experimental import pallas as pl
from jax.experimental.pallas import tpu as pltpu

F32 = jnp.float32
BF = jnp.bfloat16
MESH = pl.DeviceIdType.MESH
ANY = pl.BlockSpec(memory_space=pl.ANY)

N_DEV = 8
N_CHIP = 4
DEPTH = 4
SSM_GROUP = 16
SSM_STATE = 64
S5_BLOCK = 256
RMS_EPS = 1e-6
ADAM_LR, ADAM_B1, ADAM_B2, ADAM_EPS, ADAM_WD, ADAM_STEP = 0.001, 0.9, 0.999, 1e-08, 0.01, 10
V7X_VMEM_BYTES = 64 * 1024 * 1024
VMEM_LIMIT = V7X_VMEM_BYTES - 12 * 1024 * 1024
SLAB_W = 1024
GELU_C = math.sqrt(2.0 / math.pi)
GELU_A = 0.044715


def _cp(*sem):
    return pltpu.CompilerParams(dimension_semantics=sem if sem else None, vmem_limit_bytes=VMEM_LIMIT)


def _tile(n, prefs):
    for p in prefs:
        if p <= n and n % p == 0:
            return p
    return n


def _axes():
    return lax.axis_index("x"), lax.axis_index("y"), lax.axis_index("c")


def _flip(v, k):
    return 1 - v if k else v


def gather8(v, name):
    R, C = v.shape

    def body(v_ref, o_ref, ssem, rsem, lsem):
        x, y, c = _axes()
        me = 4 * x + 2 * y + c
        loc = pltpu.make_async_copy(v_ref, o_ref.at[me], lsem)
        loc.start()
        copies = []
        for k in range(1, N_DEV):
            peer = (_flip(x, (k >> 2) & 1), _flip(y, (k >> 1) & 1), _flip(c, k & 1))
            cp = pltpu.make_async_remote_copy(src_ref=v_ref, dst_ref=o_ref.at[me], send_sem=ssem.at[k - 1],
                                              recv_sem=rsem.at[k - 1], device_id=peer, device_id_type=MESH)
            cp.start()
            copies.append(cp)
        for cp in copies:
            cp.wait()
        loc.wait()

    return pl.pallas_call(
        body, name=name,
        out_shape=jax.ShapeDtypeStruct((N_DEV, R, C), v.dtype),
        in_specs=[pl.BlockSpec(memory_space=pltpu.VMEM)],
        out_specs=pl.BlockSpec(memory_space=pltpu.VMEM),
        scratch_shapes=[pltpu.SemaphoreType.DMA((N_DEV - 1,)), pltpu.SemaphoreType.DMA((N_DEV - 1,)),
                        pltpu.SemaphoreType.DMA],
        compiler_params=pltpu.CompilerParams(vmem_limit_bytes=VMEM_LIMIT),
    )(v)


HBM = pl.BlockSpec(memory_space=pltpu.HBM)
SEM = pl.BlockSpec(memory_space=pltpu.SEMAPHORE)
EFFECT = pltpu.SideEffectType.DATAFLOW_SIDE_EFFECTING


def _in_hbm(a):
    return pltpu.with_memory_space_constraint(a, pltpu.HBM)


def _chip_peers(x, y, c):
    out = []
    for k in range(1, N_CHIP):
        px, py = _flip(x, k >> 1), _flip(y, k & 1)
        out.append(((px, py, c), 2 * px + py))
    return out


def gather_start(shards, after, name):
    n = len(shards)

    def body(*refs):
        src, land = refs[:n], refs[n:2 * n]
        ssem, rsem, lsem = refs[2 * n + 1:2 * n + 4]
        token = refs[-1]
        x, y, c = _axes()
        chip = 2 * x + y
        for a in range(n):
            pltpu.make_async_copy(src[a], land[a].at[chip], lsem.at[a]).start()
            for k, (peer, _) in enumerate(_chip_peers(x, y, c)):
                pltpu.make_async_remote_copy(src_ref=src[a], dst_ref=land[a].at[chip], send_sem=ssem.at[3 * a + k],
                                             recv_sem=rsem.at[3 * a + k], device_id=peer, device_id_type=MESH).start()
        token[...] = jnp.zeros_like(token)

    lands = [lax.empty((N_CHIP,) + s.shape, s.dtype) for s in shards]
    out_shape = ([pltpu.SemaphoreType.DMA((3 * n,)), pltpu.SemaphoreType.DMA((3 * n,)), pltpu.SemaphoreType.DMA((n,))]
                 + [pltpu.HBM(s.shape, s.dtype) for s in shards] + [pltpu.HBM(l.shape, l.dtype) for l in lands]
                 + [jax.ShapeDtypeStruct((8, 128), F32)])
    res = pl.pallas_call(
        body, name=name, out_shape=out_shape, in_specs=[HBM] * (2 * n) + [ANY],
        out_specs=[SEM, SEM, SEM] + [HBM] * (2 * n) + [pl.BlockSpec(memory_space=pltpu.VMEM)],
        input_output_aliases={a: 3 + a for a in range(2 * n)},
        compiler_params=pltpu.CompilerParams(has_side_effects=EFFECT),
    )(*[_in_hbm(s) for s in shards], *[_in_hbm(l) for l in lands], after)
    return tuple(res[:3]), list(res[3:3 + n]), list(res[3 + n:3 + 2 * n]), res[-1]


def gather_wait(sems, srcs, lands, idx, after, name):
    m = len(idx)

    def body(*refs):
        src, land = refs[:m], refs[m:2 * m]
        ssem, rsem, lsem = refs[2 * m:2 * m + 3]
        x, y, c = _axes()
        chip = 2 * x + y
        for j, a in enumerate(idx):
            for k, (peer, pchip) in enumerate(_chip_peers(x, y, c)):
                cp = pltpu.make_async_remote_copy(src_ref=src[j], dst_ref=land[j].at[pchip], send_sem=ssem.at[3 * a + k],
                                                  recv_sem=rsem.at[3 * a + k], device_id=peer, device_id_type=MESH)
                cp.wait_send()
                cp.wait_recv()
            pltpu.make_async_copy(src[j], land[j].at[chip], lsem.at[a]).wait()

    s_in = [srcs[a] for a in idx]
    l_in = [lands[a] for a in idx]
    res = pl.pallas_call(
        body, name=name,
        out_shape=[pltpu.HBM(s.shape, s.dtype) for s in s_in] + [pltpu.HBM(l.shape, l.dtype) for l in l_in],
        in_specs=[HBM] * (2 * m) + [SEM, SEM, SEM, ANY], out_specs=[HBM] * (2 * m),
        input_output_aliases={a: a for a in range(2 * m)},
        compiler_params=pltpu.CompilerParams(has_side_effects=EFFECT),
    )(*s_in, *l_in, *sems, after)
    return list(res[m:])


def scatter_start(grads, lands, slot, after, name):
    n = len(grads)

    def body(*refs):
        src, land = refs[:n], refs[n:2 * n]
        ssem, rsem, lsem = refs[2 * n + 1:2 * n + 4]
        token = refs[-1]
        x, y, c = _axes()
        chip = 2 * x + y
        for a in range(n):
            pltpu.make_async_copy(src[a].at[chip], land[a].at[slot[a], chip], lsem.at[a]).start()
            for k, (peer, pchip) in enumerate(_chip_peers(x, y, c)):
                pltpu.make_async_remote_copy(src_ref=src[a].at[pchip], dst_ref=land[a].at[slot[a], chip],
                                             send_sem=ssem.at[3 * a + k], recv_sem=rsem.at[3 * a + k],
                                             device_id=peer, device_id_type=MESH).start()
        token[...] = jnp.zeros_like(token)

    out_shape = ([pltpu.SemaphoreType.DMA((3 * n,)), pltpu.SemaphoreType.DMA((3 * n,)), pltpu.SemaphoreType.DMA((n,))]
                 + [pltpu.HBM(g.shape, g.dtype) for g in grads] + [pltpu.HBM(l.shape, l.dtype) for l in lands]
                 + [jax.ShapeDtypeStruct((8, 128), F32)])
    res = pl.pallas_call(
        body, name=name, out_shape=out_shape, in_specs=[HBM] * (2 * n) + [ANY],
        out_specs=[SEM, SEM, SEM] + [HBM] * (2 * n) + [pl.BlockSpec(memory_space=pltpu.VMEM)],
        input_output_aliases={a: 3 + a for a in range(2 * n)},
        compiler_params=pltpu.CompilerParams(has_side_effects=EFFECT),
    )(*[_in_hbm(g) for g in grads], *[_in_hbm(l) for l in lands], after)
    return tuple(res[:3]), list(res[3:3 + n]), list(res[3 + n:3 + 2 * n]), res[-1]


def scatter_wait(sems, grads, lands, slot, after, name):
    n = len(grads)

    def body(*refs):
        src, land = refs[:n], refs[n:2 * n]
        ssem, rsem, lsem = refs[2 * n:2 * n + 3]
        x, y, c = _axes()
        chip = 2 * x + y
        for a in range(n):
            for k, (peer, pchip) in enumerate(_chip_peers(x, y, c)):
                cp = pltpu.make_async_remote_copy(src_ref=src[a].at[pchip], dst_ref=land[a].at[slot[a], pchip],
                                                  send_sem=ssem.at[3 * a + k], recv_sem=rsem.at[3 * a + k],
                                                  device_id=peer, device_id_type=MESH)
                cp.wait_send()
                cp.wait_recv()
            pltpu.make_async_copy(src[a].at[chip], land[a].at[slot[a], chip], lsem.at[a]).wait()

    res = pl.pallas_call(
        body, name=name,
        out_shape=[pltpu.HBM(g.shape, g.dtype) for g in grads] + [pltpu.HBM(l.shape, l.dtype) for l in lands],
        in_specs=[HBM] * (2 * n) + [SEM, SEM, SEM, ANY], out_specs=[HBM] * (2 * n),
        input_output_aliases={a: a for a in range(2 * n)},
        compiler_params=pltpu.CompilerParams(has_side_effects=EFFECT),
    )(*grads, *lands, *sems, after)
    return list(res[n:])


def reduce4(land, name):
    nl, _, R, C = land.shape
    TR = _adam_rows(R, C)

    def body(l_ref, o_ref):
        o_ref[...] = ((l_ref[0].astype(F32) + l_ref[1].astype(F32)) + l_ref[2].astype(F32)) + l_ref[3].astype(F32)

    return pl.pallas_call(
        body, name=name, grid=(nl, R // TR),
        in_specs=[pl.BlockSpec((None, N_CHIP, TR, C), lambda i, r: (i, 0, r, 0))],
        out_specs=pl.BlockSpec((None, TR, C), lambda i, r: (i, r, 0)),
        out_shape=jax.ShapeDtypeStruct((nl, R, C), F32), compiler_params=_cp("parallel", "parallel"))(land)


def swap_siblings(arrs, name):
    n = len(arrs)

    def body(*refs):
        src, dst = refs[:n], refs[n:2 * n]
        ssem, rsem = refs[2 * n:]
        x, y, c = _axes()
        cps = [pltpu.make_async_remote_copy(src_ref=src[a], dst_ref=dst[a], send_sem=ssem.at[a], recv_sem=rsem.at[a],
                                            device_id=(x, y, 1 - c), device_id_type=MESH) for a in range(n)]
        for cp in cps:
            cp.start()
        for cp in cps:
            cp.wait()

    return pl.pallas_call(
        body, name=name, out_shape=[jax.ShapeDtypeStruct(a.shape, a.dtype) for a in arrs],
        in_specs=[ANY] * n, out_specs=[ANY] * n,
        scratch_shapes=[pltpu.SemaphoreType.DMA((n,)), pltpu.SemaphoreType.DMA((n,))],
        compiler_params=pltpu.CompilerParams(vmem_limit_bytes=VMEM_LIMIT),
    )(*arrs)


def swap_start(arrs, name):
    n = len(arrs)

    def body(*refs):
        src, land = refs[:n], refs[n:2 * n]
        ssem, rsem = refs[2 * n:2 * n + 2]
        token = refs[-1]
        x, y, c = _axes()
        for a in range(n):
            pltpu.make_async_remote_copy(src_ref=src[a], dst_ref=land[a], send_sem=ssem.at[a], recv_sem=rsem.at[a],
                                         device_id=(x, y, 1 - c), device_id_type=MESH).start()
        token[...] = jnp.zeros_like(token)

    lands = [lax.empty(a.shape, a.dtype) for a in arrs]
    out_shape = ([pltpu.SemaphoreType.DMA((n,)), pltpu.SemaphoreType.DMA((n,))]
                 + [pltpu.HBM(a.shape, a.dtype) for a in arrs] * 2 + [jax.ShapeDtypeStruct((8, 128), F32)])
    res = pl.pallas_call(
        body, name=name, out_shape=out_shape, in_specs=[HBM] * (2 * n),
        out_specs=[SEM, SEM] + [HBM] * (2 * n) + [pl.BlockSpec(memory_space=pltpu.VMEM)],
        input_output_aliases={a: 2 + a for a in range(2 * n)},
        compiler_params=pltpu.CompilerParams(has_side_effects=EFFECT),
    )(*[_in_hbm(a) for a in arrs], *[_in_hbm(l) for l in lands])
    return tuple(res[:2]), list(res[2:2 + n]), list(res[2 + n:2 + 2 * n]), res[-1]


def swap_wait(sems, srcs, lands, after, name):
    n = len(srcs)

    def body(*refs):
        src, land = refs[:n], refs[n:2 * n]
        ssem, rsem = refs[2 * n:2 * n + 2]
        x, y, c = _axes()
        for a in range(n):
            cp = pltpu.make_async_remote_copy(src_ref=src[a], dst_ref=land[a], send_sem=ssem.at[a],
                                              recv_sem=rsem.at[a], device_id=(x, y, 1 - c), device_id_type=MESH)
            cp.wait_send()
            cp.wait_recv()

    res = pl.pallas_call(
        body, name=name, out_shape=[pltpu.HBM(a.shape, a.dtype) for a in srcs] * 2,
        in_specs=[HBM] * (2 * n) + [SEM, SEM, ANY], out_specs=[HBM] * (2 * n),
        input_output_aliases={a: a for a in range(2 * n)},
        compiler_params=pltpu.CompilerParams(has_side_effects=EFFECT),
    )(*srcs, *lands, *sems, after)
    return list(res[:n]), list(res[n:])


def reduce8(slab, dm, name):
    RT, C = slab.shape
    P = RT // N_DEV
    R = dm.shape[0]

    def body(s_ref, dm_ref, o_ref, dmo_ref, recv, s1, r1, s2, r2, s3, r3):
        x, y, c = _axes()
        me = 4 * x + 2 * y + c
        mine = pl.ds(pl.multiple_of(me * P, 8), P)
        parts, dms = [], []
        for k in range(1, N_DEV):
            px, py, pc = _flip(x, (k >> 2) & 1), _flip(y, (k >> 1) & 1), _flip(c, k & 1)
            theirs = pl.ds(pl.multiple_of((4 * px + 2 * py + pc) * P, 8), P)
            cp = pltpu.make_async_remote_copy(src_ref=s_ref.at[theirs], dst_ref=recv.at[me], send_sem=s1.at[k - 1],
                                              recv_sem=r1.at[k - 1], device_id=(px, py, pc), device_id_type=MESH)
            cp.start()
            parts.append(cp)
            cd = pltpu.make_async_remote_copy(src_ref=dm_ref, dst_ref=dmo_ref.at[me], send_sem=s3.at[k - 1],
                                              recv_sem=r3.at[k - 1], device_id=(px, py, pc), device_id_type=MESH)
            cd.start()
            dms.append(cd)
        dmo_ref[me] = dm_ref[...]
        recv[me] = s_ref[mine, :]
        for cp in parts:
            cp.wait()
        tot = recv[0]
        for d in range(1, N_DEV):
            tot = tot + recv[d]
        o_ref[mine, :] = tot
        out = []
        for k in range(1, N_DEV):
            peer = (_flip(x, (k >> 2) & 1), _flip(y, (k >> 1) & 1), _flip(c, k & 1))
            cp = pltpu.make_async_remote_copy(src_ref=o_ref.at[mine], dst_ref=o_ref.at[mine], send_sem=s2.at[k - 1],
                                              recv_sem=r2.at[k - 1], device_id=peer, device_id_type=MESH)
            cp.start()
            out.append(cp)
        for cp in out + dms:
            cp.wait()

    sems = [pltpu.SemaphoreType.DMA((N_DEV - 1,))] * 6
    return pl.pallas_call(
        body, name=name,
        out_shape=[jax.ShapeDtypeStruct((RT, C), F32), jax.ShapeDtypeStruct((N_DEV, R, C), F32)],
        in_specs=[pl.BlockSpec(memory_space=pltpu.VMEM)] * 2, out_specs=[pl.BlockSpec(memory_space=pltpu.VMEM)] * 2,
        scratch_shapes=[pltpu.VMEM((N_DEV, P, C), F32)] + sems,
        compiler_params=pltpu.CompilerParams(vmem_limit_bytes=VMEM_LIMIT),
    )(slab, dm)


def mm_nn(a, w, out_dtype, name, res=None, gate=None):
    M, K = a.shape
    S, _, Ns = w.shape
    TM = _tile(M, (1024, 512, 256) if K <= 1024 else (512, 256))
    TN = _tile(Ns, (1408, 1024, 768, 512, 256, 128))
    nj = Ns // TN
    fused = res is not None

    def body(*refs):
        if fused:
            a_ref, w_ref, r_ref, g_ref, f_ref, o_ref = refs
        else:
            a_ref, w_ref, f_ref = refs
        f = jnp.dot(a_ref[...], w_ref[...], preferred_element_type=F32)
        f_ref[...] = f.astype(f_ref.dtype)
        if fused:
            o_ref[...] = r_ref[...] + g_ref[...] * f

    col = lambda s, j, i: (i, s * nj + j)
    in_specs = [pl.BlockSpec((TM, K), lambda s, j, i: (i, 0)), pl.BlockSpec((None, K, TN), lambda s, j, i: (s, 0, j))]
    out_specs = [pl.BlockSpec((TM, TN), col)]
    out_shape = [jax.ShapeDtypeStruct((M, S * Ns), out_dtype)]
    args = [a, w]
    if fused:
        in_specs += [pl.BlockSpec((TM, TN), col), pl.BlockSpec((1, TN), lambda s, j, i: (0, s * nj + j))]
        out_specs.append(pl.BlockSpec((TM, TN), col))
        out_shape.append(jax.ShapeDtypeStruct((M, S * Ns), F32))
        args += [res, gate]
    out = pl.pallas_call(body, name=name, grid=(S, nj, M // TM), in_specs=in_specs, out_specs=out_specs,
                         out_shape=out_shape, compiler_params=_cp("parallel", "parallel", "parallel"))(*args)
    return tuple(out) if fused else out[0]


def mm_nt(g, w, out_dtype, name):
    g3 = g if g.ndim == 3 else g[None]
    Q, M, F = g3.shape
    S, K, Ns = w.shape
    TM = _tile(M, (1024, 512, 256) if K <= 1024 else (512, 256))
    TN = _tile(Ns, (1408, 1024, 768, 512, 256, 128))
    nj = Ns // TN
    nred = S * nj
    per_part = F // TN

    def body(g_ref, w_ref, o_ref, acc):
        n = pl.program_id(1)

        @pl.when(n == 0)
        def _():
            acc[...] = jnp.zeros_like(acc)

        acc[...] += lax.dot_general(g_ref[...], w_ref[...], (((1,), (1,)), ((), ())), preferred_element_type=F32)

        @pl.when(n == nred - 1)
        def _():
            o_ref[...] = acc[...].astype(o_ref.dtype)

    return pl.pallas_call(
        body, name=name, grid=(M // TM, nred),
        in_specs=[pl.BlockSpec((None, TM, TN), lambda i, n: (n // per_part, i, n % per_part)),
                  pl.BlockSpec((None, K, TN), lambda i, n: (n // nj, 0, n % nj))],
        out_specs=pl.BlockSpec((TM, K), lambda i, n: (i, 0)),
        out_shape=jax.ShapeDtypeStruct((M, K), out_dtype),
        scratch_shapes=[pltpu.VMEM((TM, K), F32)],
        compiler_params=_cp("parallel", "arbitrary"))(g3, w)


def mm_tn(a, g, S, name):
    M, K = a.shape
    g3 = g if g.ndim == 3 else g[None]
    Q, _, F = g3.shape
    Ns = Q * F // S
    TK = _tile(K, (256, 128))
    TN = _tile(Ns, (1408, 1024, 768, 512, 256, 128))
    nj = Ns // TN
    per_part = F // TN

    def body(a_ref, g_ref, o_ref):
        o_ref[...] = lax.dot_general(a_ref[...], g_ref[...], (((0,), (0,)), ((), ())),
                                     preferred_element_type=F32).astype(o_ref.dtype)

    return pl.pallas_call(
        body, name=name, grid=(S * nj, K // TK),
        in_specs=[pl.BlockSpec((M, TK), lambda n, k: (0, k)),
                  pl.BlockSpec((None, M, TN), lambda n, k: (n // per_part, 0, n % per_part))],
        out_specs=pl.BlockSpec((None, TK, TN), lambda n, k: (n // nj, k, n % nj)),
        out_shape=jax.ShapeDtypeStruct((S, K, Ns), BF),
        compiler_params=_cp("parallel", "parallel"))(a, g3)


def _rows(TL, D):
    return pl.BlockSpec((TL, D), lambda i: (i, 0))


def _fixed(R, D):
    return pl.BlockSpec((R, D), lambda i: (0, 0))


def _rowsum8(v):
    T, D = v.shape
    return jnp.sum(v.reshape(T // 8, 8, D), axis=0)


def _norm_parts(xv):
    r = lax.rsqrt(jnp.mean(xv * xv, axis=-1, keepdims=True) + RMS_EPS)
    return xv * r, r


def norm_mod(x, gamma, mods, k_shift, out_dtype, name):
    L, D = x.shape
    TL = _tile(L, (512, 256))

    def body(x_ref, g_ref, m_ref, o_ref):
        xn, _ = _norm_parts(x_ref[...])
        sh, sc = m_ref[k_shift:k_shift + 1, :], m_ref[k_shift + 1:k_shift + 2, :]
        o_ref[...] = ((xn * g_ref[...]) * (1.0 + sc) + sh).astype(o_ref.dtype)

    return pl.pallas_call(body, name=name, grid=(L // TL,),
                          in_specs=[_rows(TL, D), _fixed(1, D), _fixed(6, D)], out_specs=_rows(TL, D),
                          out_shape=jax.ShapeDtypeStruct((L, D), out_dtype), compiler_params=_cp("parallel"))(x, gamma, mods)


def norm_bwd(dh, x, dres, gamma, mods, k_shift, name, branch=None):
    L, D = x.shape
    TL = _tile(L, (512, 256))
    nacc = 4 if branch else 3

    def body(*refs):
        if branch:
            dh_ref, x_ref, dr_ref, g_ref, m_ref, f_ref, fm_ref, dx_ref, s_ref, df_ref, acc = refs
        else:
            dh_ref, x_ref, dr_ref, g_ref, m_ref, dx_ref, s_ref, acc = refs
        i = pl.program_id(0)

        @pl.when(i == 0)
        def _():
            acc[...] = jnp.zeros_like(acc)

        xn, r = _norm_parts(x_ref[...])
        dh_v = dh_ref[...].astype(F32)
        gam = g_ref[...]
        sc = m_ref[k_shift + 1:k_shift + 2, :]
        dn = dh_v * (1.0 + sc)
        dxn = dn * gam
        dx = dr_ref[...] + r * (dxn - xn * jnp.mean(dxn * xn, axis=-1, keepdims=True))
        dx_ref[...] = dx
        acc[0] += _rowsum8(dh_v)
        acc[1] += _rowsum8(dh_v * (xn * gam))
        acc[2] += _rowsum8(dn * xn)
        if branch:
            df_ref[...] = (dx * fm_ref[branch[2]:branch[2] + 1, :]).astype(df_ref.dtype)
            acc[3] += _rowsum8(dx * f_ref[...].astype(F32))

        @pl.when(i == pl.num_programs(0) - 1)
        def _():
            s_ref[...] = jnp.zeros_like(s_ref)
            for q in range(nacc):
                s_ref[q:q + 1, :] = jnp.sum(acc[q], axis=0, keepdims=True)

    in_specs = [_rows(TL, D), _rows(TL, D), _rows(TL, D), _fixed(1, D), _fixed(6, D)]
    out_specs = [_rows(TL, D), _fixed(8, D)]
    out_shape = [jax.ShapeDtypeStruct((L, D), F32), jax.ShapeDtypeStruct((8, D), F32)]
    args = [dh, x, dres, gamma, mods]
    if branch:
        in_specs += [_rows(TL, D), _fixed(6, D)]
        out_specs.append(_rows(TL, D))
        out_shape.append(jax.ShapeDtypeStruct((L, D), BF))
        args += [branch[0], branch[1]]
    return pl.pallas_call(
        body, name=name, grid=(L // TL,), in_specs=in_specs, out_specs=out_specs, out_shape=out_shape,
        scratch_shapes=[pltpu.VMEM((nacc, 8, D), F32)], compiler_params=_cp("arbitrary"))(*args)


def gate_bwd(dx, f, mods, k_gate, name):
    L, D = dx.shape
    TL = _tile(L, (512, 256))

    def body(dx_ref, f_ref, m_ref, o_ref, s_ref, acc):
        i = pl.program_id(0)

        @pl.when(i == 0)
        def _():
            acc[...] = jnp.zeros_like(acc)

        dxv = dx_ref[...]
        o_ref[...] = (dxv * m_ref[k_gate:k_gate + 1, :]).astype(o_ref.dtype)
        acc[...] += _rowsum8(dxv * f_ref[...].astype(F32))

        @pl.when(i == pl.num_programs(0) - 1)
        def _():
            s_ref[...] = jnp.zeros_like(s_ref)
            s_ref[0:1, :] = jnp.sum(acc[...], axis=0, keepdims=True)

    return pl.pallas_call(
        body, name=name, grid=(L // TL,), in_specs=[_rows(TL, D), _rows(TL, D), _fixed(6, D)],
        out_specs=[_rows(TL, D), _fixed(8, D)],
        out_shape=[jax.ShapeDtypeStruct((L, D), BF), jax.ShapeDtypeStruct((8, D), F32)],
        scratch_shapes=[pltpu.VMEM((8, D), F32)], compiler_params=_cp("arbitrary"))(dx, f, mods)


def ffn_in_act(a, w, name):
    M, K = a.shape
    S, _, Ns = w.shape
    half = S // 2
    TM = _tile(M, (512, 256))
    TN = _tile(Ns, (1408, 1024, 768, 512, 256, 128))
    nj = Ns // TN

    def body(a_ref, wg_ref, wu_ref, gu_ref, act_ref):
        av = a_ref[...]
        g = jnp.dot(av, wg_ref[...], preferred_element_type=F32)
        u = jnp.dot(av, wu_ref[...], preferred_element_type=F32)
        gu_ref[0] = g.astype(gu_ref.dtype)
        gu_ref[1] = u.astype(gu_ref.dtype)
        act_ref[...] = (g * jax.nn.sigmoid(g) * u).astype(act_ref.dtype)

    return pl.pallas_call(
        body, name=name, grid=(half, nj, M // TM),
        in_specs=[pl.BlockSpec((TM, K), lambda s, j, i: (i, 0)),
                  pl.BlockSpec((None, K, TN), lambda s, j, i: (s, 0, j)),
                  pl.BlockSpec((None, K, TN), lambda s, j, i: (s + half, 0, j))],
        out_specs=[pl.BlockSpec((2, TM, TN), lambda s, j, i: (0, i, s * nj + j)),
                   pl.BlockSpec((TM, TN), lambda s, j, i: (i, s * nj + j))],
        out_shape=[jax.ShapeDtypeStruct((2, M, half * Ns), BF), jax.ShapeDtypeStruct((M, half * Ns), BF)],
        compiler_params=_cp("parallel", "parallel", "parallel"))(a, w, w)


def ffn_out_bwd(dff, w2, gu, name):
    M, D = dff.shape
    F = w2.shape[0]
    TM = _tile(M, (512, 256))
    CW = _tile(F, (256, 128))

    def body(d_ref, w_ref, gu_ref, o_ref):
        dv = d_ref[...]
        for c in range(0, F, CW):
            da = lax.dot_general(dv, w_ref[c:c + CW, :], (((1,), (1,)), ((), ())), preferred_element_type=F32)
            g = gu_ref[0, :, c:c + CW].astype(F32)
            u = gu_ref[1, :, c:c + CW].astype(F32)
            s = jax.nn.sigmoid(g)
            o_ref[0, :, c:c + CW] = (da * u * (s + g * s * (1.0 - s))).astype(o_ref.dtype)
            o_ref[1, :, c:c + CW] = (da * g * s).astype(o_ref.dtype)

    part = pl.BlockSpec((2, TM, F), lambda i: (0, i, 0))
    return pl.pallas_call(
        body, name=name, grid=(M // TM,),
        in_specs=[pl.BlockSpec((TM, D), lambda i: (i, 0)), pl.BlockSpec((F, D), lambda i: (0, 0)), part],
        out_specs=part, out_shape=jax.ShapeDtypeStruct((2, M, F), BF),
        compiler_params=_cp("parallel"))(dff, w2, gu)


def swiglu_act(gu, name):
    L, F2 = gu.shape
    F = F2 // 2
    TL = _tile(L, (256,))

    def body(gu_ref, o_ref):
        g = gu_ref[:, :F].astype(F32)
        u = gu_ref[:, F:].astype(F32)
        o_ref[...] = (g * jax.nn.sigmoid(g) * u).astype(o_ref.dtype)

    return pl.pallas_call(body, name=name, grid=(L // TL,), in_specs=[_rows(TL, F2)], out_specs=_rows(TL, F),
                          out_shape=jax.ShapeDtypeStruct((L, F), BF), compiler_params=_cp("parallel"))(gu)


def swiglu_bwd(da, gu, name):
    L, F2 = gu.shape
    F = F2 // 2
    TL = _tile(L, (256,))

    def body(da_ref, gu_ref, o_ref):
        g = gu_ref[:, :F].astype(F32)
        u = gu_ref[:, F:].astype(F32)
        d = da_ref[...].astype(F32)
        s = jax.nn.sigmoid(g)
        o_ref[:, :F] = (d * u * (s + g * s * (1.0 - s))).astype(o_ref.dtype)
        o_ref[:, F:] = (d * g * s).astype(o_ref.dtype)

    return pl.pallas_call(body, name=name, grid=(L // TL,), in_specs=[_rows(TL, F), _rows(TL, F2)],
                          out_specs=_rows(TL, F2), out_shape=jax.ShapeDtypeStruct((L, F2), BF),
                          compiler_params=_cp("parallel"))(da, gu)


def glu_res(o, x, mods, k_gate, name):
    L, D = x.shape
    TL = _tile(L, (512, 256))

    def body(o_ref, x_ref, m_ref, mix_ref, y_ref):
        mix = o_ref[:, :D].astype(F32) * jax.nn.sigmoid(o_ref[:, D:].astype(F32))
        mix_ref[...] = mix.astype(mix_ref.dtype)
        y_ref[...] = x_ref[...] + m_ref[k_gate:k_gate + 1, :] * mix

    return pl.pallas_call(
        body, name=name, grid=(L // TL,), in_specs=[_rows(TL, 2 * D), _rows(TL, D), _fixed(6, D)],
        out_specs=[_rows(TL, D), _rows(TL, D)],
        out_shape=[jax.ShapeDtypeStruct((L, D), BF), jax.ShapeDtypeStruct((L, D), F32)],
        compiler_params=_cp("parallel"))(o, x, mods)


def glu_bwd(dmix, o, name):
    L, D2 = o.shape
    D = D2 // 2
    TL = _tile(L, (512, 256))

    def body(d_ref, o_ref, do_ref):
        d = d_ref[...].astype(F32)
        val = o_ref[:, :D].astype(F32)
        s = jax.nn.sigmoid(o_ref[:, D:].astype(F32))
        do_ref[:, :D] = (d * s).astype(do_ref.dtype)
        do_ref[:, D:] = (d * val * s * (1.0 - s)).astype(do_ref.dtype)

    return pl.pallas_call(body, name=name, grid=(L // TL,), in_specs=[_rows(TL, D), _rows(TL, D2)],
                          out_specs=_rows(TL, D2), out_shape=jax.ShapeDtypeStruct((L, D2), BF),
                          compiler_params=_cp("parallel"))(dmix, o)


def final_loss(x, target, gamma, f, fmods, k_gate, name):
    L, D = x.shape
    TL = _tile(L, (512, 256))

    def body(x_ref, t_ref, g_ref, f_ref, fm_ref, l_ref, dx_ref, s_ref, df_ref, acc, lacc):
        i = pl.program_id(0)

        @pl.when(i == 0)
        def _():
            acc[...] = jnp.zeros_like(acc)
            lacc[...] = jnp.zeros_like(lacc)

        xn, r = _norm_parts(x_ref[...])
        gam = g_ref[...]
        e = xn * gam - t_ref[...]
        lacc[...] += jnp.sum(0.5 * jnp.mean(e * e, axis=-1, keepdims=True), axis=0, keepdims=True)
        dy = e * (1.0 / D)
        dxn = dy * gam
        dx = r * (dxn - xn * jnp.mean(dxn * xn, axis=-1, keepdims=True))
        dx_ref[...] = dx
        df_ref[...] = (dx * fm_ref[k_gate:k_gate + 1, :]).astype(df_ref.dtype)
        acc[0] += _rowsum8(dy * xn)
        acc[1] += _rowsum8(dx * f_ref[...].astype(F32))

        @pl.when(i == pl.num_programs(0) - 1)
        def _():
            s_ref[...] = jnp.zeros_like(s_ref)
            for q in range(2):
                s_ref[q:q + 1, :] = jnp.sum(acc[q], axis=0, keepdims=True)
            l_ref[...] = jnp.broadcast_to(lacc[...], l_ref.shape)

    return pl.pallas_call(
        body, name=name, grid=(L // TL,),
        in_specs=[_rows(TL, D), _rows(TL, D), _fixed(1, D), _rows(TL, D), _fixed(6, D)],
        out_specs=[_fixed(8, 128), _rows(TL, D), _fixed(8, D), _rows(TL, D)],
        out_shape=[jax.ShapeDtypeStruct((8, 128), F32), jax.ShapeDtypeStruct((L, D), F32),
                   jax.ShapeDtypeStruct((8, D), F32), jax.ShapeDtypeStruct((L, D), BF)],
        scratch_shapes=[pltpu.VMEM((2, 8, D), F32), pltpu.VMEM((1, 1), F32)],
        compiler_params=_cp("arbitrary"))(x, target, gamma, f, fmods)


def _col(L, TC, off):
    return pl.BlockSpec((L, TC), lambda j: (0, off + j))


def _shift_down(v, k, row):
    return jnp.where(row >= k, pltpu.roll(v, k, 0), 0.0)


def _shift_up(v, k, row, L):
    return jnp.where(row < L - k, pltpu.roll(v, L - k, 0), 0.0)


def conv_fwd(p, w, name):
    L, D3 = p.shape
    D = D3 // 3
    TC = _tile(D, (128,))
    nc = D // TC

    def body(b_ref, c_ref, v_ref, w_ref, o_ref):
        row = lax.broadcasted_iota(jnp.int32, (L, TC), 0)
        cv = c_ref[...].astype(F32) * v_ref[...].astype(F32)
        conv = w_ref[2:3, :] * cv + w_ref[1:2, :] * _shift_down(cv, 1, row) + w_ref[0:1, :] * _shift_down(cv, 2, row)
        o_ref[...] = (b_ref[...].astype(F32) * conv).astype(o_ref.dtype)

    return pl.pallas_call(
        body, name=name, grid=(nc,),
        in_specs=[_col(L, TC, 0), _col(L, TC, nc), _col(L, TC, 2 * nc), pl.BlockSpec((3, TC), lambda j: (0, j))],
        out_specs=_col(L, TC, 0), out_shape=jax.ShapeDtypeStruct((L, D), BF), compiler_params=_cp("parallel"))(p, p, p, w)


def conv_bwd(dm, p, w, name):
    L, D3 = p.shape
    D = D3 // 3
    TC = _tile(D, (128,))
    nc = D // TC

    def body(dm_ref, b_ref, c_ref, v_ref, w_ref, db_ref, dc_ref, dv_ref, dw_ref):
        row = lax.broadcasted_iota(jnp.int32, (L, TC), 0)
        cg, vv = c_ref[...].astype(F32), v_ref[...].astype(F32)
        cv = cg * vv
        cv1, cv2 = _shift_down(cv, 1, row), _shift_down(cv, 2, row)
        conv = w_ref[2:3, :] * cv + w_ref[1:2, :] * cv1 + w_ref[0:1, :] * cv2
        dmv = dm_ref[...].astype(F32)
        db_ref[...] = (dmv * conv).astype(db_ref.dtype)
        dconv = dmv * b_ref[...].astype(F32)
        dcv = (w_ref[2:3, :] * dconv + w_ref[1:2, :] * _shift_up(dconv, 1, row, L)
               + w_ref[0:1, :] * _shift_up(dconv, 2, row, L))
        dc_ref[...] = (dcv * vv).astype(dc_ref.dtype)
        dv_ref[...] = (dcv * cg).astype(dv_ref.dtype)
        dw_ref[...] = jnp.zeros_like(dw_ref)
        dw_ref[0:1, :] = jnp.sum(dconv * cv2, axis=0, keepdims=True)
        dw_ref[1:2, :] = jnp.sum(dconv * cv1, axis=0, keepdims=True)
        dw_ref[2:3, :] = jnp.sum(dconv * cv, axis=0, keepdims=True)

    one = jax.ShapeDtypeStruct((L, D), BF)
    return pl.pallas_call(
        body, name=name, grid=(nc,),
        in_specs=[_col(L, TC, 0), _col(L, TC, 0), _col(L, TC, nc), _col(L, TC, 2 * nc),
                  pl.BlockSpec((3, TC), lambda j: (0, j))],
        out_specs=[_col(L, TC, 0), _col(L, TC, 0), _col(L, TC, 0), pl.BlockSpec((8, TC), lambda j: (0, j))],
        out_shape=[one, one, one, jax.ShapeDtypeStruct((8, D), F32)],
        compiler_params=_cp("parallel"))(dm, p, p, p, w)


def _gelu(y):
    return 0.5 * y * (1.0 + jnp.tanh(GELU_C * (y + GELU_A * y * y * y)))


def _gelu_grad(y):
    th = jnp.tanh(GELU_C * (y + GELU_A * y * y * y))
    return 0.5 * (1.0 + th) + 0.5 * y * (1.0 - th * th) * GELU_C * (1.0 + 3.0 * GELU_A * y * y)


def _cmul_add(br, bi, ar, ai, sr, si):
    return br + ar * sr - ai * si, bi + ar * si + ai * sr


def _log2(n):
    k = n.bit_length() - 1
    assert 1 << k == n
    return k


def _replicate(P2, W2, P, GLP, transposed):
    shape = (W2, P2) if transposed else (P2, W2)
    k = lax.broadcasted_iota(jnp.int32, shape, 1 if transposed else 0)
    c = lax.broadcasted_iota(jnp.int32, shape, 0 if transposed else 1)
    return ((k >> _log2(P)) == (c >> _log2(GLP))) & ((k & (P - 1)) == (c & (P - 1)))


def _on_diagonal(KB, W2, H, P, GLP, transposed):
    shape = (W2, KB) if transposed else (KB, W2)
    r = lax.broadcasted_iota(jnp.int32, shape, 1 if transposed else 0)
    c = lax.broadcasted_iota(jnp.int32, shape, 0 if transposed else 1)
    return (r >> _log2(H)) == ((c & (GLP - 1)) >> _log2(P))


def _expand(t, dims, transposed):
    KB, W2, H, P, GLP = dims
    rep = _replicate(2 * P, W2, P, GLP, transposed).astype(t.dtype)
    wide = jnp.dot(rep, t, preferred_element_type=F32) if transposed else jnp.dot(t, rep, preferred_element_type=F32)
    return jnp.where(_on_diagonal(KB, W2, H, P, GLP, transposed), wide, 0.0).astype(t.dtype)


def _extract(acc, dims):
    KB, W2, H, P, GLP = dims
    rep = _replicate(2 * P, W2, P, GLP, True).astype(F32)
    kept = jnp.where(_on_diagonal(KB, W2, H, P, GLP, False), acc, 0.0)
    return jnp.dot(kept, rep, preferred_element_type=F32, precision=lax.Precision.HIGHEST)


def _cmul(ar, ai, sr, si):
    return ar * sr - ai * si, ar * si + ai * sr


LANES = 128


def _cols(ref, base, n, rows):
    return jnp.concatenate([ref[base + q, rows, :] for q in range(n)], axis=1)


def _set_cols(ref, base, n, rows, val):
    for q in range(n):
        ref[base + q, rows, :] = val[:, q * LANES:(q + 1) * LANES]


def _strided_s5_fwd(h, tb, tct, pw, dvec, name):
    L, D = h.shape
    nkb, KB, P2 = tb.shape
    P = P2 // 2
    W = (KB // SSM_GROUP) * P
    W2 = 2 * W
    dims = (KB, W2, SSM_GROUP, P, W)
    TL = _tile(L, (512, 256))
    CH = TL // 8
    NC = W // LANES

    def body(h_ref, tb_ref, tct_ref, pw_ref, d_ref, s_ref, y_ref, z_ref, bw, cw, carry):
        t = pl.program_id(1)

        @pl.when(t == 0)
        def _():
            carry[...] = jnp.zeros_like(carry)
            bw[...] = _expand(tb_ref[...], dims, False)
            cw[...] = _expand(tct_ref[...], dims, True)

        hv = h_ref[...]
        _set_cols(s_ref, 0, 2 * NC, slice(None), jnp.dot(hv.astype(BF), bw[...], preferred_element_type=F32))
        ar, ai = pw_ref[0:8, :W], pw_ref[0:8, W:]
        xr = xi = jnp.zeros((8, W), F32)
        for j in range(CH):
            rows = pl.ds(j, 8, stride=CH)
            xr, xi = _cmul_add(_cols(s_ref, 0, NC, rows), _cols(s_ref, NC, NC, rows), ar, ai, xr, xi)
            _set_cols(s_ref, 0, NC, rows, xr)
            _set_cols(s_ref, NC, NC, rows, xi)
        for k, off in ((1, 8), (2, 16), (4, 24)):
            xr, xi = _cmul_add(xr, xi, pw_ref[off:off + 8, :W], pw_ref[off:off + 8, W:],
                               pltpu.roll(xr, k, 0), pltpu.roll(xi, k, 0))
        xr, xi = _cmul_add(xr, xi, pw_ref[32:40, :W], pw_ref[32:40, W:], carry[0], carry[1])
        first = lax.broadcasted_iota(jnp.int32, (8, W), 0) == 0
        cr = jnp.where(first, carry[0], pltpu.roll(xr, 1, 0))
        ci = jnp.where(first, carry[1], pltpu.roll(xi, 1, 0))
        carry[0] = jnp.broadcast_to(xr[7:8], (8, W))
        carry[1] = jnp.broadcast_to(xi[7:8], (8, W))
        for j in range(CH):
            rows = pl.ds(j, 8, stride=CH)
            cr, ci = _cmul(ar, ai, cr, ci)
            _set_cols(s_ref, 0, NC, rows, _cols(s_ref, 0, NC, rows) + cr)
            _set_cols(s_ref, NC, NC, rows, _cols(s_ref, NC, NC, rows) + ci)
        sv = _cols(s_ref, 0, 2 * NC, slice(None))
        y = jnp.dot(sv.astype(BF), cw[...], preferred_element_type=F32) + d_ref[...] * hv
        y_ref[...] = y
        z_ref[...] = _gelu(y).astype(z_ref.dtype)

    blk = lambda kb, t: (t, kb)
    per_kb = lambda kb, t: (kb, 0, 0)
    return pl.pallas_call(
        body, name=name, grid=(nkb, L // TL),
        in_specs=[pl.BlockSpec((TL, KB), blk), pl.BlockSpec((None, KB, P2), per_kb),
                  pl.BlockSpec((None, P2, KB), per_kb), pl.BlockSpec((None, 40, W2), per_kb),
                  pl.BlockSpec((1, KB), lambda kb, t: (0, kb))],
        out_specs=[pl.BlockSpec((2 * NC, TL, LANES), lambda kb, t: (kb, t, 0)), pl.BlockSpec((TL, KB), blk),
                   pl.BlockSpec((TL, KB), blk)],
        out_shape=[jax.ShapeDtypeStruct((nkb * 2 * NC, L, LANES), F32), jax.ShapeDtypeStruct((L, D), F32),
                   jax.ShapeDtypeStruct((L, D), BF)],
        scratch_shapes=[pltpu.VMEM((KB, W2), BF), pltpu.VMEM((W2, KB), BF), pltpu.VMEM((2, 8, W), F32)],
        compiler_params=_cp("parallel", "arbitrary"))(h, tb, tct, pw, dvec)


def _strided_s5_bwd(dz, y, h, s, tc, tbt, pwr, dvec, name):
    L, D = h.shape
    nkb, KB, P2 = tc.shape
    P = P2 // 2
    W = (KB // SSM_GROUP) * P
    W2 = 2 * W
    dims = (KB, W2, SSM_GROUP, P, W)
    TL = _tile(L, (512, 256))
    CH = TL // 8
    NC = W // LANES
    nt = L // TL

    def body(dz_ref, y_ref, h_ref, s_ref, sp_ref, tc_ref, tbt_ref, pw_ref, d_ref,
             dh_ref, dd_ref, da_ref, db_ref, dc_ref, g, ctw, btw, dbacc, dcacc, carry):
        t = pl.program_id(1)

        @pl.when(t == 0)
        def _():
            carry[...] = jnp.zeros_like(carry)
            dd_ref[...] = jnp.zeros_like(dd_ref)
            da_ref[...] = jnp.zeros_like(da_ref)
            dbacc[...] = jnp.zeros_like(dbacc)
            dcacc[...] = jnp.zeros_like(dcacc)
            ctw[...] = _expand(tc_ref[...], dims, False)
            btw[...] = _expand(tbt_ref[...], dims, True)

        hv = h_ref[...]
        dy = dz_ref[...].astype(F32) * _gelu_grad(y_ref[...])
        dd_ref[...] += _rowsum8(dy * hv)
        dyb = dy.astype(BF)
        _set_cols(g, 0, 2 * NC, slice(None), jnp.dot(dyb, ctw[...], preferred_element_type=F32))
        ar, ai = pw_ref[0:8, :W], pw_ref[0:8, W:]
        gr = gi = jnp.zeros((8, W), F32)
        for j in reversed(range(CH)):
            rows = pl.ds(j, 8, stride=CH)
            gr, gi = _cmul_add(_cols(g, 0, NC, rows), _cols(g, NC, NC, rows), ar, ai, gr, gi)
            _set_cols(g, 0, NC, rows, gr)
            _set_cols(g, NC, NC, rows, gi)
        for k, off in ((1, 8), (2, 16), (4, 24)):
            gr, gi = _cmul_add(gr, gi, pw_ref[off:off + 8, :W], pw_ref[off:off + 8, W:],
                               pltpu.roll(gr, 8 - k, 0), pltpu.roll(gi, 8 - k, 0))
        gr, gi = _cmul_add(gr, gi, pw_ref[32:40, :W], pw_ref[32:40, W:], carry[0], carry[1])
        sub = lax.broadcasted_iota(jnp.int32, (8, W), 0)
        cr = jnp.where(sub == 7, carry[0], pltpu.roll(gr, 7, 0))
        ci = jnp.where(sub == 7, carry[1], pltpu.roll(gi, 7, 0))
        carry[0] = jnp.broadcast_to(gr[0:1], (8, W))
        carry[1] = jnp.broadcast_to(gi[0:1], (8, W))
        live = jnp.where(t == nt - 1, 0.0, 1.0)
        accr = acci = jnp.zeros((8, W), F32)
        for j in reversed(range(CH)):
            rows = pl.ds(j, 8, stride=CH)
            cr, ci = _cmul(ar, ai, cr, ci)
            gr, gi = _cols(g, 0, NC, rows) + cr, _cols(g, NC, NC, rows) + ci
            _set_cols(g, 0, NC, rows, gr)
            _set_cols(g, NC, NC, rows, gi)
            if j > 0:
                before = pl.ds(j - 1, 8, stride=CH)
                pr, pi = _cols(s_ref, 0, NC, before), _cols(s_ref, NC, NC, before)
            else:
                last = pl.ds(CH - 1, 8, stride=CH)
                pr = jnp.where(sub == 0, _cols(sp_ref, 0, NC, slice(7, 8)) * live,
                               pltpu.roll(_cols(s_ref, 0, NC, last), 1, 0))
                pi = jnp.where(sub == 0, _cols(sp_ref, NC, NC, slice(7, 8)) * live,
                               pltpu.roll(_cols(s_ref, NC, NC, last), 1, 0))
            accr = accr + pr * gr + pi * gi
            acci = acci + pr * gi - pi * gr
        da_ref[:, :W] += accr
        da_ref[:, W:] += acci

        gb = _cols(g, 0, 2 * NC, slice(None)).astype(BF)
        dh_ref[...] = dy * d_ref[...] + jnp.dot(gb, btw[...], preferred_element_type=F32)
        tn = (((0,), (0,)), ((), ()))
        dbacc[...] += lax.dot_general(hv.astype(BF), gb, tn, preferred_element_type=F32)
        dcacc[...] += lax.dot_general(dyb, _cols(s_ref, 0, 2 * NC, slice(None)).astype(BF), tn,
                                      preferred_element_type=F32)

        @pl.when(t == nt - 1)
        def _():
            db_ref[...] = _extract(dbacc[...], dims)
            dc_ref[...] = _extract(dcacc[...], dims)

    rev = lambda kb, t: (nt - 1 - t, kb)
    per_kb = lambda kb, t: (kb, 0, 0)
    return pl.pallas_call(
        body, name=name, grid=(nkb, nt),
        in_specs=[pl.BlockSpec((TL, KB), rev), pl.BlockSpec((TL, KB), rev), pl.BlockSpec((TL, KB), rev),
                  pl.BlockSpec((2 * NC, TL, LANES), lambda kb, t: (kb, nt - 1 - t, 0)),
                  pl.BlockSpec((2 * NC, 8, LANES), lambda kb, t: (kb, jnp.maximum((nt - 1 - t) * CH - 1, 0), 0)),
                  pl.BlockSpec((None, KB, P2), per_kb), pl.BlockSpec((None, P2, KB), per_kb),
                  pl.BlockSpec((None, 40, W2), per_kb), pl.BlockSpec((1, KB), lambda kb, t: (0, kb))],
        out_specs=[pl.BlockSpec((TL, KB), rev), pl.BlockSpec((8, KB), lambda kb, t: (0, kb)),
                   pl.BlockSpec((None, 8, W2), per_kb), pl.BlockSpec((None, KB, P2), per_kb),
                   pl.BlockSpec((None, KB, P2), per_kb)],
        out_shape=[jax.ShapeDtypeStruct((L, D), F32), jax.ShapeDtypeStruct((8, D), F32),
                   jax.ShapeDtypeStruct((nkb, 8, W2), F32), jax.ShapeDtypeStruct((nkb, KB, P2), F32),
                   jax.ShapeDtypeStruct((nkb, KB, P2), F32)],
        scratch_shapes=[pltpu.VMEM((2 * NC, TL, LANES), F32), pltpu.VMEM((KB, W2), BF), pltpu.VMEM((W2, KB), BF),
                        pltpu.VMEM((KB, W2), F32), pltpu.VMEM((KB, W2), F32), pltpu.VMEM((2, 8, W), F32)],
        compiler_params=_cp("parallel", "arbitrary"))(dz, y, h, s, s, tc, tbt, pwr, dvec)


def _chunk_order(TL, CH, transposed):
    out_row = lax.broadcasted_iota(jnp.int32, (TL, TL), 1 if transposed else 0)
    in_row = lax.broadcasted_iota(jnp.int32, (TL, TL), 0 if transposed else 1)
    return in_row == ((out_row & 7) << _log2(CH)) + (out_row >> 3)


def _reorder(perm, v):
    hi = v.astype(perm.dtype)
    lo = (v - hi.astype(F32)).astype(perm.dtype)
    return jnp.dot(perm, hi, preferred_element_type=F32) + jnp.dot(perm, lo, preferred_element_type=F32)


def s5_fwd(h, tb, tct, pw, dvec, name):
    L, D = h.shape
    nkb, KB, P2 = tb.shape
    P = P2 // 2
    W = (KB // SSM_GROUP) * P
    W2 = 2 * W
    dims = (KB, W2, SSM_GROUP, P, W)
    TL = _tile(L, (512, 256))
    CH = TL // 8

    def body(h_ref, tb_ref, tct_ref, pw_ref, d_ref, s_ref, y_ref, z_ref, bw, cw, perm, unperm, carry):
        t = pl.program_id(1)

        @pl.when(t == 0)
        def _():
            carry[...] = jnp.zeros_like(carry)
            bw[...] = _expand(tb_ref[...], dims, False)
            cw[...] = _expand(tct_ref[...], dims, True)
            perm[...] = _chunk_order(TL, CH, False).astype(perm.dtype)
            unperm[...] = _chunk_order(TL, CH, True).astype(perm.dtype)

        hp = _reorder(perm[...], h_ref[...])
        s_ref[...] = jnp.dot(hp.astype(BF), bw[...], preferred_element_type=F32)
        ar, ai = pw_ref[0:8, :W], pw_ref[0:8, W:]

        def own(j, x):
            rows = pl.ds(pl.multiple_of(j * 8, 8), 8)
            xr, xi = _cmul_add(s_ref[rows, :W], s_ref[rows, W:], ar, ai, x[0], x[1])
            s_ref[rows, :W] = xr
            s_ref[rows, W:] = xi
            return xr, xi

        zero = jnp.zeros((8, W), F32)
        xr, xi = lax.fori_loop(0, CH, own, (zero, zero))
        for k, off in ((1, 8), (2, 16), (4, 24)):
            xr, xi = _cmul_add(xr, xi, pw_ref[off:off + 8, :W], pw_ref[off:off + 8, W:],
                               pltpu.roll(xr, k, 0), pltpu.roll(xi, k, 0))
        xr, xi = _cmul_add(xr, xi, pw_ref[32:40, :W], pw_ref[32:40, W:], carry[0], carry[1])
        first = lax.broadcasted_iota(jnp.int32, (8, W), 0) == 0
        cr = jnp.where(first, carry[0], pltpu.roll(xr, 1, 0))
        ci = jnp.where(first, carry[1], pltpu.roll(xi, 1, 0))
        carry[0] = jnp.broadcast_to(xr[7:8], (8, W))
        carry[1] = jnp.broadcast_to(xi[7:8], (8, W))

        def carried(j, c):
            rows = pl.ds(pl.multiple_of(j * 8, 8), 8)
            cr, ci = _cmul(ar, ai, c[0], c[1])
            s_ref[rows, :W] = s_ref[rows, :W] + cr
            s_ref[rows, W:] = s_ref[rows, W:] + ci
            return cr, ci

        lax.fori_loop(0, CH, carried, (cr, ci))
        y = jnp.dot(s_ref[...].astype(BF), cw[...], preferred_element_type=F32) + d_ref[...] * hp
        y_ref[...] = y
        z_ref[...] = jnp.dot(unperm[...], _gelu(y).astype(BF), preferred_element_type=F32).astype(z_ref.dtype)

    blk = lambda kb, t: (t, kb)
    per_kb = lambda kb, t: (kb, 0, 0)
    return pl.pallas_call(
        body, name=name, grid=(nkb, L // TL),
        in_specs=[pl.BlockSpec((TL, KB), blk), pl.BlockSpec((None, KB, P2), per_kb),
                  pl.BlockSpec((None, P2, KB), per_kb), pl.BlockSpec((None, 40, W2), per_kb),
                  pl.BlockSpec((1, KB), lambda kb, t: (0, kb))],
        out_specs=[pl.BlockSpec((TL, W2), blk), pl.BlockSpec((TL, KB), blk), pl.BlockSpec((TL, KB), blk)],
        out_shape=[jax.ShapeDtypeStruct((L, nkb * W2), F32), jax.ShapeDtypeStruct((L, D), F32),
                   jax.ShapeDtypeStruct((L, D), BF)],
        scratch_shapes=[pltpu.VMEM((KB, W2), BF), pltpu.VMEM((W2, KB), BF), pltpu.VMEM((TL, TL), BF),
                        pltpu.VMEM((TL, TL), BF), pltpu.VMEM((2, 8, W), F32)],
        compiler_params=_cp("parallel", "arbitrary"))(h, tb, tct, pw, dvec)


def s5_bwd(dz, y, h, s, tc, tbt, pwr, dvec, name):
    L, D = h.shape
    nkb, KB, P2 = tc.shape
    P = P2 // 2
    W = (KB // SSM_GROUP) * P
    W2 = 2 * W
    dims = (KB, W2, SSM_GROUP, P, W)
    TL = _tile(L, (512, 256))
    CH = TL // 8
    nt = L // TL

    def body(dz_ref, y_ref, h_ref, s_ref, sp_ref, tc_ref, tbt_ref, pw_ref, d_ref,
             dh_ref, dd_ref, da_ref, db_ref, dc_ref, g, ctw, btw, dbacc, dcacc, perm, unperm, carry):
        t = pl.program_id(1)

        @pl.when(t == 0)
        def _():
            carry[...] = jnp.zeros_like(carry)
            dd_ref[...] = jnp.zeros_like(dd_ref)
            da_ref[...] = jnp.zeros_like(da_ref)
            dbacc[...] = jnp.zeros_like(dbacc)
            dcacc[...] = jnp.zeros_like(dcacc)
            ctw[...] = _expand(tc_ref[...], dims, False)
            btw[...] = _expand(tbt_ref[...], dims, True)
            perm[...] = _chunk_order(TL, CH, False).astype(perm.dtype)
            unperm[...] = _chunk_order(TL, CH, True).astype(perm.dtype)

        hp = jnp.dot(perm[...], h_ref[...].astype(BF), preferred_element_type=F32)
        dy = jnp.dot(perm[...], dz_ref[...].astype(BF), preferred_element_type=F32) * _gelu_grad(y_ref[...])
        dd_ref[...] += _rowsum8(dy * hp)
        dyb = dy.astype(BF)
        g[...] = jnp.dot(dyb, ctw[...], preferred_element_type=F32)
        ar, ai = pw_ref[0:8, :W], pw_ref[0:8, W:]

        def own(jj, x):
            rows = pl.ds(pl.multiple_of((CH - 1 - jj) * 8, 8), 8)
            gr, gi = _cmul_add(g[rows, :W], g[rows, W:], ar, ai, x[0], x[1])
            g[rows, :W] = gr
            g[rows, W:] = gi
            return gr, gi

        zero = jnp.zeros((8, W), F32)
        gr, gi = lax.fori_loop(0, CH, own, (zero, zero))
        for k, off in ((1, 8), (2, 16), (4, 24)):
            gr, gi = _cmul_add(gr, gi, pw_ref[off:off + 8, :W], pw_ref[off:off + 8, W:],
                               pltpu.roll(gr, 8 - k, 0), pltpu.roll(gi, 8 - k, 0))
        gr, gi = _cmul_add(gr, gi, pw_ref[32:40, :W], pw_ref[32:40, W:], carry[0], carry[1])
        sub = lax.broadcasted_iota(jnp.int32, (8, W), 0)
        cr = jnp.where(sub == 7, carry[0], pltpu.roll(gr, 7, 0))
        ci = jnp.where(sub == 7, carry[1], pltpu.roll(gi, 7, 0))
        carry[0] = jnp.broadcast_to(gr[0:1], (8, W))
        carry[1] = jnp.broadcast_to(gi[0:1], (8, W))

        def carried(jj, c):
            j = CH - 1 - jj
            rows = pl.ds(pl.multiple_of(j * 8, 8), 8)
            before = pl.ds(pl.multiple_of(j * 8 - 8, 8), 8)
            cr, ci = _cmul(ar, ai, c[0], c[1])
            gr, gi = g[rows, :W] + cr, g[rows, W:] + ci
            g[rows, :W] = gr
            g[rows, W:] = gi
            pr, pi = s_ref[before, :W], s_ref[before, W:]
            return cr, ci, c[2] + pr * gr + pi * gi, c[3] + pr * gi - pi * gr

        cr, ci, accr, acci = lax.fori_loop(0, CH - 1, carried, (cr, ci, zero, zero))
        live = jnp.where(t == nt - 1, 0.0, 1.0)
        cr, ci = _cmul(ar, ai, cr, ci)
        gr, gi = g[0:8, :W] + cr, g[0:8, W:] + ci
        g[0:8, :W] = gr
        g[0:8, W:] = gi
        pr = jnp.where(sub == 0, sp_ref[7:8, :W] * live, pltpu.roll(s_ref[TL - 8:TL, :W], 1, 0))
        pi = jnp.where(sub == 0, sp_ref[7:8, W:] * live, pltpu.roll(s_ref[TL - 8:TL, W:], 1, 0))
        da_ref[:, :W] += accr + pr * gr + pi * gi
        da_ref[:, W:] += acci + pr * gi - pi * gr

        gb = g[...].astype(BF)
        dh = dy * d_ref[...] + jnp.dot(gb, btw[...], preferred_element_type=F32)
        dh_ref[...] = _reorder(unperm[...], dh)
        tn = (((0,), (0,)), ((), ()))
        dbacc[...] += lax.dot_general(hp.astype(BF), gb, tn, preferred_element_type=F32)
        dcacc[...] += lax.dot_general(dyb, s_ref[...].astype(BF), tn, preferred_element_type=F32)

        @pl.when(t == nt - 1)
        def _():
            db_ref[...] = _extract(dbacc[...], dims)
            dc_ref[...] = _extract(dcacc[...], dims)

    rev = lambda kb, t: (nt - 1 - t, kb)
    prev = lambda kb, t: (jnp.maximum((nt - 1 - t) * CH - 1, 0), kb)
    per_kb = lambda kb, t: (kb, 0, 0)
    return pl.pallas_call(
        body, name=name, grid=(nkb, nt),
        in_specs=[pl.BlockSpec((TL, KB), rev), pl.BlockSpec((TL, KB), rev), pl.BlockSpec((TL, KB), rev),
                  pl.BlockSpec((TL, W2), rev), pl.BlockSpec((8, W2), prev),
                  pl.BlockSpec((None, KB, P2), per_kb), pl.BlockSpec((None, P2, KB), per_kb),
                  pl.BlockSpec((None, 40, W2), per_kb), pl.BlockSpec((1, KB), lambda kb, t: (0, kb))],
        out_specs=[pl.BlockSpec((TL, KB), rev), pl.BlockSpec((8, KB), lambda kb, t: (0, kb)),
                   pl.BlockSpec((None, 8, W2), per_kb), pl.BlockSpec((None, KB, P2), per_kb),
                   pl.BlockSpec((None, KB, P2), per_kb)],
        out_shape=[jax.ShapeDtypeStruct((L, D), F32), jax.ShapeDtypeStruct((8, D), F32),
                   jax.ShapeDtypeStruct((nkb, 8, W2), F32), jax.ShapeDtypeStruct((nkb, KB, P2), F32),
                   jax.ShapeDtypeStruct((nkb, KB, P2), F32)],
        scratch_shapes=[pltpu.VMEM((TL, W2), F32), pltpu.VMEM((KB, W2), BF), pltpu.VMEM((W2, KB), BF),
                        pltpu.VMEM((KB, W2), F32), pltpu.VMEM((KB, W2), F32), pltpu.VMEM((TL, TL), BF),
                        pltpu.VMEM((TL, TL), BF), pltpu.VMEM((2, 8, W), F32)],
        compiler_params=_cp("parallel", "arbitrary"))(dz, y, h, s, s, tc, tbt, pwr, dvec)


def _discretise(a_re, a_im, log_step, b_re, b_im):
    lr = jnp.minimum(a_re, -1e-4)
    li = a_im
    dt = jnp.exp(log_step)[:, None]
    mag = jnp.exp(lr * dt)
    abr = mag * jnp.cos(li * dt)
    abi = mag * jnp.sin(li * dt)
    den = lr * lr + li * li
    qr = ((abr - 1.0) * lr + abi * li) / den
    qi = (abi * lr - (abr - 1.0) * li) / den
    bbar_re = qr[..., None] * b_re - qi[..., None] * b_im
    bbar_im = qr[..., None] * b_im + qi[..., None] * b_re
    return abr, abi, bbar_re, bbar_im


def _compact(m_re, m_im, nkb):
    G, H, P = m_re.shape
    t = jnp.stack([m_re, m_im], axis=2).reshape(nkb, (G // nkb) * H, 2 * P).astype(BF)
    return t, jnp.swapaxes(t, 1, 2)


def _scan_powers(abr, abi, nkb, conj, CH):
    G, P = abr.shape
    if conj:
        abi = -abi

    def cmul(u, v):
        return u[0] * v[0] - u[1] * v[1], u[0] * v[1] + u[1] * v[0]

    q = (abr, abi)
    for _ in range(_log2(CH)):
        q = cmul(q, q)
    pows = [q]
    for _ in range(7):
        pows.append(cmul(pows[-1], q))
    row = jnp.arange(8)[:, None, None]

    def table(part):
        out = [jnp.broadcast_to((abr, abi)[part][None], (8, G, P))]
        for k in (1, 2, 4):
            keep = (row <= 7 - k) if conj else (row >= k)
            out.append(jnp.where(keep, pows[k - 1][part][None], 0.0))
        ends = jnp.stack([p[part] for p in pows])
        out.append(ends[::-1] if conj else ends)
        return jnp.concatenate(out, axis=0)

    GL = G // nkb
    t = jnp.stack([table(0), table(1)], axis=1)
    t = t.reshape(40, 2, nkb, GL * P).transpose(2, 0, 1, 3)
    return t.reshape(nkb, 40, 2 * GL * P)


def ada_mods(c_all, w_ada, b_sh, name):
    nl, D, NA = w_ada.shape

    def body(c_ref, w_ref, b_ref, o_ref):
        cv = c_ref[...]
        act = cv * jax.nn.sigmoid(cv)
        o_ref[...] = jnp.dot(act, w_ref[...], preferred_element_type=F32, precision=lax.Precision.HIGHEST) + b_ref[...]

    return pl.pallas_call(
        body, name=name, grid=(nl,),
        in_specs=[pl.BlockSpec((8, D), lambda i: (0, 0)), pl.BlockSpec((None, D, NA), lambda i: (i, 0, 0)),
                  pl.BlockSpec((None, 1, NA), lambda i: (i, 0, 0))],
        out_specs=pl.BlockSpec((None, 8, NA), lambda i: (i, 0, 0)),
        out_shape=jax.ShapeDtypeStruct((nl, 8, NA), F32), compiler_params=_cp("parallel"))(c_all, w_ada, b_sh)


def _adamw(w, g, m, v):
    m = ADAM_B1 * m + (1.0 - ADAM_B1) * g
    v = ADAM_B2 * v + (1.0 - ADAM_B2) * (g * g)
    m_hat = m / (1.0 - ADAM_B1 ** ADAM_STEP)
    v_hat = v / (1.0 - ADAM_B2 ** ADAM_STEP)
    return -ADAM_LR * (m_hat / (jnp.sqrt(v_hat) + ADAM_EPS) + ADAM_WD * w), m, v


def _adam_rows(R, C):
    cap = max(8, (256 * 1024) // C)
    for t in range(min(R, cap), 0, -1):
        if R % t == 0 and (t % 8 == 0 or t == R):
            return t
    return R


def adamw_ada(c_t, dm, w, m, v, name):
    nl, D, NA = w.shape
    TK = _tile(D, (128,))

    def body(c_ref, dm_ref, w_ref, m_ref, v_ref, g_ref, d_ref, nm_ref, nv_ref):
        cv = c_ref[...]
        act = cv * jax.nn.sigmoid(cv)
        g = act[:, 0:1] * dm_ref[0:1, :]
        for b in range(1, 8):
            g = g + act[:, b:b + 1] * dm_ref[b:b + 1, :]
        g_ref[...] = g
        d_ref[...], nm_ref[...], nv_ref[...] = _adamw(w_ref[...], g, m_ref[...], v_ref[...])

    big = pl.BlockSpec((None, TK, NA), lambda i, k: (i, k, 0))
    shape = jax.ShapeDtypeStruct(w.shape, F32)
    return pl.pallas_call(
        body, name=name, grid=(nl, D // TK),
        in_specs=[pl.BlockSpec((TK, 8), lambda i, k: (k, 0)), pl.BlockSpec((None, 8, NA), lambda i, k: (i, 0, 0)),
                  big, big, big],
        out_specs=[big] * 4, out_shape=[shape] * 4, compiler_params=_cp("parallel", "parallel"))(c_t, dm, w, m, v)


def adamw_sharded(w, m, v, ga, gb, name):
    nl, R, C = w.shape
    TR = _adam_rows(R, C)

    def body(w_ref, m_ref, v_ref, a_ref, b_ref, g_ref, d_ref, nm_ref, nv_ref):
        g = a_ref[...] + b_ref[...]
        g_ref[...] = g
        d_ref[...], nm_ref[...], nv_ref[...] = _adamw(w_ref[...], g, m_ref[...], v_ref[...])

    big = pl.BlockSpec((None, TR, C), lambda i, r: (i, r, 0))
    shape = jax.ShapeDtypeStruct(w.shape, F32)
    return pl.pallas_call(
        body, name=name, grid=(nl, R // TR), in_specs=[big] * 5,
        out_specs=[big] * 4, out_shape=[shape] * 4, compiler_params=_cp("parallel", "parallel"))(w, m, v, ga, gb)


def adamw_slab(g, w, m, v, name):
    R, C = g.shape
    TR = _tile(R, (160, 80, 40, 8))

    def body(g_ref, w_ref, m_ref, v_ref, d_ref, nm_ref, nv_ref):
        d_ref[...], nm_ref[...], nv_ref[...] = _adamw(w_ref[...], g_ref[...], m_ref[...], v_ref[...])

    big = pl.BlockSpec((TR, C), lambda r: (r, 0))
    shape = jax.ShapeDtypeStruct((R, C), F32)
    return pl.pallas_call(
        body, name=name, grid=(R // TR,), in_specs=[big] * 4,
        out_specs=[big] * 3, out_shape=[shape] * 3, compiler_params=_cp("parallel"))(g, w, m, v)


def adamw_plain(w, m, v, g, name):
    def body(w_ref, m_ref, v_ref, g_ref, d_ref, nm_ref, nv_ref):
        d_ref[...], nm_ref[...], nv_ref[...] = _adamw(w_ref[...], g_ref[...], m_ref[...], v_ref[...])

    shape = jax.ShapeDtypeStruct(w.shape, F32)
    return pl.pallas_call(body, name=name, out_shape=[shape] * 3,
                          compiler_params=pltpu.CompilerParams(vmem_limit_bytes=VMEM_LIMIT))(w, m, v, g)


def _slab_rows(a):
    n = a.size
    rows = -(-n // SLAB_W)
    return -(-rows // 8) * 8


def _pack(arrs, pad_rows_to=0):
    out = []
    for a in arrs:
        rows = _slab_rows(a)
        flat = a.reshape(-1).astype(F32)
        flat = jnp.pad(flat, (0, rows * SLAB_W - flat.shape[0]))
        out.append(flat.reshape(rows, SLAB_W))
    total = sum(o.shape[0] for o in out)
    if pad_rows_to and total % pad_rows_to:
        out.append(jnp.zeros((pad_rows_to - total % pad_rows_to, SLAB_W), F32))
    return jnp.concatenate(out, axis=0)


def _unpack(slab, like):
    out, r = [], 0
    for a in like:
        rows = _slab_rows(a)
        out.append(slab[r:r + rows].reshape(-1)[:a.size].reshape(a.shape))
        r += rows
    return out


WEIGHTS = ['norm1_g', 'norm2_g', 'w_ada', 'b_ada', 'ssm_a_re', 'ssm_a_im', 'ssm_log_step', 'ssm_b_re', 'ssm_b_im',
           'ssm_c_re', 'ssm_c_im', 'ssm_d', 'ssm_w_out', 'conv_w_in', 'conv_w', 'conv_w_out', 'w_ffn_in',
           'w_ffn_out', 'final_g']
SLAB = ['norm1_g', 'norm2_g', 'b_ada', 'ssm_a_re', 'ssm_a_im', 'ssm_log_step', 'ssm_b_re', 'ssm_b_im', 'ssm_c_re',
        'ssm_c_im', 'ssm_d', 'final_g']
SHARDED = ['ssm_w_out', 'conv_w_in', 'conv_w_out', 'w_ffn_in', 'w_ffn_out']


def kernel(x, c, norm1_g, norm2_g, w_ada, b_ada, ssm_a_re, ssm_a_im, ssm_log_step, ssm_b_re, ssm_b_im, ssm_c_re, ssm_c_im, ssm_d, ssm_w_out, conv_w_in, conv_w, conv_w_out, w_ffn_in, w_ffn_out, final_g, loss_target, m_norm1_g, m_norm2_g, m_w_ada, m_b_ada, m_ssm_a_re, m_ssm_a_im, m_ssm_log_step, m_ssm_b_re, m_ssm_b_im, m_ssm_c_re, m_ssm_c_im, m_ssm_d, m_ssm_w_out, m_conv_w_in, m_conv_w, m_conv_w_out, m_w_ffn_in, m_w_ffn_out, m_final_g, v_norm1_g, v_norm2_g, v_w_ada, v_b_ada, v_ssm_a_re, v_ssm_a_im, v_ssm_log_step, v_ssm_b_re, v_ssm_b_im, v_ssm_c_re, v_ssm_c_im, v_ssm_d, v_ssm_w_out, v_conv_w_in, v_conv_w, v_conv_w_out, v_w_ffn_in, v_w_ffn_out, v_final_g):
    given = dict(locals())
    W = {n: given[n] for n in WEIGHTS}
    Mo = {n: given["m_" + n] for n in WEIGHTS}
    Vo = {n: given["v_" + n] for n in WEIGHTS}

    xs = x[0]
    tgt = loss_target[0]
    L, D = xs.shape
    nlayer = norm1_g.shape[0]
    NA = w_ada.shape[2]
    G = ssm_a_re.shape[1]
    nkb = D // S5_BLOCK
    ax, ay, ac = _axes()
    me = 4 * ax + 2 * ay + ac
    chip = 2 * ax + ay

    c_all = gather8(jnp.broadcast_to(c, (8, D)), "gather_c")[:, 0, :]
    b_sh = lax.dynamic_slice_in_dim(b_ada, chip * NA, NA, axis=1)[:, None, :]
    mods_part = ada_mods(c_all, w_ada, b_sh, "ada_mods")
    mg = gather8(mods_part.reshape(nlayer * 8, NA), "gather_mods")
    mg = mg.reshape(N_CHIP, 2, nlayer, 8, NA)[:, 0]
    mods_all = lax.dynamic_index_in_dim(mg, me, axis=2, keepdims=False)
    mods_all = jnp.transpose(mods_all, (1, 0, 2)).reshape(nlayer, 6, D)

    cw_parts = gather8(_pack([conv_w]), "gather_conv_w")
    nconv = conv_w.shape[0]
    cw_full = jnp.stack([_unpack(cw_parts[2 * q], [conv_w])[0] for q in range(N_CHIP)], axis=2)
    cw_full = cw_full.reshape(nconv, 3, D)

    use = []
    for i in range(nlayer):
        use += [("ssm_w_out", i // 2, i)] if i % 2 == 0 else [("conv_w_in", i // 2, i), ("conv_w_out", i // 2, i)]
        use += [("w_ffn_in", i, i), ("w_ffn_out", i, i)]
    g_sems, g_srcs, g_lands, token = gather_start([W[n][j].astype(BF) for n, j, _ in use],
                                                  cw_full + mods_all[0, 0:3], "gather_start")
    mods_all = mods_all + token[0:1, 0:1]

    def layer_weights(i, after):
        idx = [a for a, (_, _, li) in enumerate(use) if li == i]
        got = gather_wait(g_sems, g_srcs, g_lands, idx, after, "gather_wait%d" % i)
        return {use[a][0]: w for a, w in zip(idx, got)}

    s5 = []
    for j in range(ssm_a_re.shape[0]):
        disc, disc_vjp = jax.vjp(_discretise, ssm_a_re[j], ssm_a_im[j], ssm_log_step[j], ssm_b_re[j], ssm_b_im[j])
        abr, abi, bbar_re, bbar_im = disc
        tb, tbt = _compact(jnp.swapaxes(bbar_re, 1, 2), jnp.swapaxes(bbar_im, 1, 2), nkb)
        tc, tct = _compact(ssm_c_re[j], -ssm_c_im[j], nkb)
        chunk = _tile(L, (512, 256)) // 8
        s5.append(dict(vjp=disc_vjp, tb=tb, tbt=tbt, tc=tc, tct=tct, pw=_scan_powers(abr, abi, nkb, False, chunk),
                       pwr=_scan_powers(abr, abi, nkb, True, chunk)))

    saved = []
    xcur = xs
    for i in range(nlayer):
        j = i // 2
        mods = mods_all[i]
        sv = dict(x=xcur)
        if i % 2 == 0:
            h = norm_mod(xcur, norm1_g[i:i + 1], mods, 0, F32, "norm_mod_s5")
            states, yv, z = s5_fwd(h, s5[j]["tb"], s5[j]["tct"], s5[j]["pw"], ssm_d[j:j + 1], "s5_fwd")
            full = layer_weights(i, z)
            o = mm_nn(z, full["ssm_w_out"], BF, "mm_ssm_out")
            mix, x2 = glu_res(o, xcur, mods, 2, "glu_res")
            sv.update(h=h, states=states, y=yv, z=z, o=o)
        else:
            h = norm_mod(xcur, norm1_g[i:i + 1], mods, 0, BF, "norm_mod")
            full = layer_weights(i, h)
            p = mm_nn(h, full["conv_w_in"], BF, "mm_conv_in")
            mc = conv_fwd(p, cw_full[j], "conv_fwd")
            mix, x2 = mm_nn(mc, full["conv_w_out"].reshape(1, D, D), BF, "mm_conv_out", res=xcur, gate=mods[2:3])
            sv.update(h=h, p=p, mc=mc)
        h2 = norm_mod(x2, norm2_g[i:i + 1], mods, 3, BF, "norm_mod")
        gu, act = ffn_in_act(h2, full["w_ffn_in"], "ffn_in_act")
        F = act.shape[1]
        ff, x3 = mm_nn(act, full["w_ffn_out"].reshape(1, F, D), BF, "mm_ffn_out", res=x2, gate=mods[5:6])
        sv.update(mix=mix, x2=x2, h2=h2, gu=gu, act=act, ff=ff, w=full)
        saved.append(sv)
        xcur = x3

    loss_blk, dx, dfinal, dff = final_loss(xcur, tgt, final_g[None, :], saved[-1]["ff"], mods_all[nlayer - 1], 5,
                                           "final_loss")
    dg2 = dfinal[1:2]

    gland = {n: lax.empty((W[n].shape[0], N_CHIP) + W[n].shape[1:], BF) for n in SHARDED}
    in_flight = []
    dmods = [None] * nlayer
    dnorm1, dnorm2 = [None] * nlayer, [None] * nlayer
    dconv_w = [None] * nconv
    ds5 = [None] * ssm_a_re.shape[0]
    token = jnp.zeros((8, 128), F32)

    def send_grads(names, grads, slot, after, name):
        sems, thru, lands, tok = scatter_start([grads[n] for n in names], [gland[n] for n in names], slot, after, name)
        gland.update(zip(names, lands))
        in_flight.append((names, slot, sems, thru, name))
        return tok

    def land_grads(group, after):
        for names, slot, sems, thru, name in in_flight:
            if names[0] in group:
                got = scatter_wait(sems, thru, [gland[n] for n in names], slot, after, name.replace("scatter", "landed"))
                gland.update(zip(names, got))

    for i in reversed(range(nlayer)):
        j = i // 2
        mods = mods_all[i] + token[0:1, 0:1]
        sv = saved[i]
        full = sv["w"]
        gfull = {}
        F = sv["act"].shape[1]
        gfull["w_ffn_out"] = mm_tn(sv["act"], dff, 1, "mm_tn_ffn_out").reshape(N_CHIP, F // N_CHIP, D)
        dgu = ffn_out_bwd(dff, full["w_ffn_out"].reshape(F, D), sv["gu"], "ffn_out_bwd")
        gfull["w_ffn_in"] = mm_tn(sv["h2"], dgu, N_CHIP, "mm_tn_ffn_in")
        dh2 = mm_nt(dgu, full["w_ffn_in"], F32, "mm_nt_ffn_in")
        token = send_grads(["w_ffn_out", "w_ffn_in"], gfull, [i, i], dh2, "scatter_ffn%d" % i)
        mods = mods + token[0:1, 0:1]
        dx2, s2, dmix = norm_bwd(dh2, sv["x2"], dx, norm2_g[i:i + 1], mods, 3, "norm_bwd_mix",
                                 branch=(sv["mix"], mods, 2))
        dg1 = s2[3:4]
        if i % 2 == 0:
            do = glu_bwd(dmix, sv["o"], "glu_bwd")
            gfull["ssm_w_out"] = mm_tn(sv["z"], do, N_CHIP, "mm_tn_ssm_out")
            dz = mm_nt(do, full["ssm_w_out"], BF, "mm_nt_ssm_out")
            dh, dd, dab, db, dc = s5_bwd(dz, sv["y"], sv["h"], sv["states"], s5[j]["tc"], s5[j]["tbt"], s5[j]["pwr"],
                                         ssm_d[j:j + 1], "s5_bwd")
            ds5[j] = (dd, dab, db, dc)
        else:
            gfull["conv_w_out"] = mm_tn(sv["mc"], dmix, 1, "mm_tn_conv_out").reshape(N_CHIP, D // N_CHIP, D)
            dmc = mm_nt(dmix, full["conv_w_out"].reshape(1, D, D), BF, "mm_nt_conv_out")
            dbg, dcg, dvv, dcw = conv_bwd(dmc, sv["p"], cw_full[j], "conv_bwd")
            dp = jnp.concatenate([dbg, dcg, dvv], axis=1)
            gfull["conv_w_in"] = mm_tn(sv["h"], dp, N_CHIP, "mm_tn_conv_in")
            dh = mm_nt(dp, full["conv_w_in"], F32, "mm_nt_conv_in")
            dconv_w[j] = dcw[0:3]
        dmods_i = [s2[0:2], dg2]
        if i > 0:
            dx, s1, dff = norm_bwd(dh, sv["x"], dx2, norm1_g[i:i + 1], mods, 0, "norm_bwd_ffn",
                                   branch=(saved[i - 1]["ff"], mods_all[i - 1], 5))
            dg2 = s1[3:4]
        else:
            dx, s1 = norm_bwd(dh, sv["x"], dx2, norm1_g[i:i + 1], mods, 0, "norm_bwd")
        dmods[i] = jnp.concatenate([s1[0:2], dg1] + dmods_i, axis=0).reshape(6 * D)
        dnorm1[i], dnorm2[i] = s1[2], s2[2]
        names = ["ssm_w_out"] if i % 2 == 0 else ["conv_w_out", "conv_w_in"]
        token = send_grads(names, gfull, [j] * len(names), dx, "scatter_mix%d" % i)

    small = dict(norm1_g=jnp.stack(dnorm1), norm2_g=jnp.stack(dnorm2), b_ada=jnp.stack(dmods), final_g=dfinal[0])
    per = {n: [] for n in ('ssm_a_re', 'ssm_a_im', 'ssm_log_step', 'ssm_b_re', 'ssm_b_im', 'ssm_c_re', 'ssm_c_im', 'ssm_d')}
    GL = G // nkb
    for j, (dd, dab, db, dc) in enumerate(ds5):
        dab = jnp.sum(dab, axis=1).reshape(nkb, 2, GL, SSM_STATE)
        g_abr, g_abi = dab[:, 0].reshape(G, SSM_STATE), dab[:, 1].reshape(G, SSM_STATE)
        db, dc = db.reshape(G, SSM_GROUP, 2, SSM_STATE), dc.reshape(G, SSM_GROUP, 2, SSM_STATE)
        gb_re, gb_im, gc_re, gc_im = db[:, :, 0], db[:, :, 1], dc[:, :, 0], dc[:, :, 1]
        ga_re, ga_im, gls, gbr, gbi = s5[j]["vjp"]((g_abr, g_abi, jnp.swapaxes(gb_re, 1, 2), jnp.swapaxes(gb_im, 1, 2)))
        for n, val in zip(per, (ga_re, ga_im, gls, gbr, gbi, gc_re, -gc_im, jnp.sum(dd, axis=0))):
            per[n].append(val)
    small.update({n: jnp.stack(vals) for n, vals in per.items()})
    dcw_full = jnp.stack(dconv_w)

    slab_like = [W[n] for n in SLAB] + [dcw_full]
    rows64 = 8 * N_DEV
    g_slab, dm_all = reduce8(_pack([small[n] for n in SLAB] + [dcw_full], rows64), _pack([small["b_ada"]]),
                             "reduce_small")
    d_slab, m_slab, v_slab = adamw_slab(
        g_slab, _pack([W[n] for n in SLAB] + [jnp.zeros_like(dcw_full)], rows64),
        _pack([Mo[n] for n in SLAB] + [jnp.zeros_like(dcw_full)], rows64),
        _pack([Vo[n] for n in SLAB] + [jnp.ones_like(dcw_full)], rows64), "adamw_slab")
    out = {}
    for k, slab in zip(("g", "d", "m", "v"), (g_slab, d_slab, m_slab, v_slab)):
        for n, val in zip(SLAB, _unpack(slab, slab_like)):
            out[k, n] = val
    g_cw = lax.dynamic_slice_in_dim(_unpack(g_slab, slab_like)[-1], chip * conv_w.shape[2], conv_w.shape[2], axis=2)
    out["g", "conv_w"] = g_cw
    out["d", "conv_w"], out["m", "conv_w"], out["v", "conv_w"] = [
        r.reshape(conv_w.shape) for r in adamw_plain(conv_w.reshape(-1, conv_w.shape[2]), m_conv_w.reshape(-1, conv_w.shape[2]),
                                                     v_conv_w.reshape(-1, conv_w.shape[2]), g_cw.reshape(-1, conv_w.shape[2]),
                                                     "adamw_conv_w")]

    early = [n for n in SHARDED if n != "ssm_w_out"]
    land_grads(early, g_slab)
    mine = [reduce4(gland[n], "reduce4_" + n) for n in early]
    w_sems, w_srcs, w_lands, token = swap_start(mine, "swap_start")

    dm_all = dm_all.reshape(N_DEV, -1)[:, :b_ada.size].reshape(N_DEV, nlayer, N_CHIP, NA)
    dm_sh = jnp.transpose(lax.dynamic_index_in_dim(dm_all, chip, axis=2, keepdims=False), (1, 0, 2))
    res = adamw_ada(jnp.transpose(c_all) + token[0:1, 0:1], dm_sh, w_ada, m_w_ada, v_w_ada, "adamw_ada")
    out["g", "w_ada"], out["d", "w_ada"], out["m", "w_ada"], out["v", "w_ada"] = res

    mine, theirs = swap_wait(w_sems, w_srcs, w_lands, out["g", "w_ada"], "swap_wait")
    for n, ga, gb in zip(early, mine, theirs):
        r = adamw_sharded(W[n], Mo[n], Vo[n], ga, gb, "adamw_" + n)
        out["g", n], out["d", n], out["m", n], out["v", n] = r

    land_grads(["ssm_w_out"], out["g", "w_ffn_out"])
    ga = reduce4(gland["ssm_w_out"], "reduce4_ssm_w_out")
    gb = swap_siblings([ga], "swap_siblings")[0]
    r = adamw_sharded(ssm_w_out, m_ssm_w_out, v_ssm_w_out, ga, gb, "adamw_ssm_w_out")
    out["g", "ssm_w_out"], out["d", "ssm_w_out"], out["m", "ssm_w_out"], out["v", "ssm_w_out"] = r

    loss = lax.psum(loss_blk[0, 0], ("x", "y", "c"))
    return (loss, dx[None], *[out["g", n] for n in WEIGHTS], *[out["d", n] for n in WEIGHTS],
            *[out["m", n] for n in WEIGHTS], *[out["v", n] for n in WEIGHTS])
```

```python
import functools
import math

import jax
import jax.numpy as jnp
from jax import lax
from jax.experimental import pallas as pl
from jax.experimental.pallas import tpu as pltpu

F32 = jnp.float32
BF = jnp.bfloat16
MESH = pl.DeviceIdType.MESH
ANY = pl.BlockSpec(memory_space=pl.ANY)

N_DEV = 8
N_CHIP = 4
DEPTH = 4
SSM_GROUP = 16
SSM_STATE = 64
S5_BLOCK = 256
RMS_EPS = 1e-6
ADAM_LR, ADAM_B1, ADAM_B2, ADAM_EPS, ADAM_WD, ADAM_STEP = 0.001, 0.9, 0.999, 1e-08, 0.01, 10
V7X_VMEM_BYTES = 64 * 1024 * 1024
VMEM_LIMIT = V7X_VMEM_BYTES - 12 * 1024 * 1024
SLAB_W = 1024
GELU_C = math.sqrt(2.0 / math.pi)
GELU_A = 0.044715


def _cp(*sem):
    return pltpu.CompilerParams(dimension_semantics=sem if sem else None, vmem_limit_bytes=VMEM_LIMIT)


def _tile(n, prefs):
    for p in prefs:
        if p <= n and n % p == 0:
            return p
    return n


def _axes():
    return lax.axis_index("x"), lax.axis_index("y"), lax.axis_index("c")


def _flip(v, k):
    return 1 - v if k else v


def gather8(v, name):
    R, C = v.shape

    def body(v_ref, o_ref, ssem, rsem, lsem):
        x, y, c = _axes()
        me = 4 * x + 2 * y + c
        loc = pltpu.make_async_copy(v_ref, o_ref.at[me], lsem)
        loc.start()
        copies = []
        for k in range(1, N_DEV):
            peer = (_flip(x, (k >> 2) & 1), _flip(y, (k >> 1) & 1), _flip(c, k & 1))
            cp = pltpu.make_async_remote_copy(src_ref=v_ref, dst_ref=o_ref.at[me], send_sem=ssem.at[k - 1],
                                              recv_sem=rsem.at[k - 1], device_id=peer, device_id_type=MESH)
            cp.start()
            copies.append(cp)
        for cp in copies:
            cp.wait()
        loc.wait()

    return pl.pallas_call(
        body, name=name,
        out_shape=jax.ShapeDtypeStruct((N_DEV, R, C), v.dtype),
        in_specs=[pl.BlockSpec(memory_space=pltpu.VMEM)],
        out_specs=pl.BlockSpec(memory_space=pltpu.VMEM),
        scratch_shapes=[pltpu.SemaphoreType.DMA((N_DEV - 1,)), pltpu.SemaphoreType.DMA((N_DEV - 1,)),
                        pltpu.SemaphoreType.DMA],
        compiler_params=pltpu.CompilerParams(vmem_limit_bytes=VMEM_LIMIT),
    )(v)


HBM = pl.BlockSpec(memory_space=pltpu.HBM)
SEM = pl.BlockSpec(memory_space=pltpu.SEMAPHORE)
EFFECT = pltpu.SideEffectType.DATAFLOW_SIDE_EFFECTING


def _in_hbm(a):
    return pltpu.with_memory_space_constraint(a, pltpu.HBM)


def _chip_peers(x, y, c):
    out = []
    for k in range(1, N_CHIP):
        px, py = _flip(x, k >> 1), _flip(y, k & 1)
        out.append(((px, py, c), 2 * px + py))
    return out


def gather_start(shards, after, name):
    n = len(shards)

    def body(*refs):
        src, land = refs[:n], refs[n:2 * n]
        ssem, rsem, lsem = refs[2 * n + 1:2 * n + 4]
        token = refs[-1]
        x, y, c = _axes()
        chip = 2 * x + y
        for a in range(n):
            pltpu.make_async_copy(src[a], land[a].at[chip], lsem.at[a]).start()
            for k, (peer, _) in enumerate(_chip_peers(x, y, c)):
                pltpu.make_async_remote_copy(src_ref=src[a], dst_ref=land[a].at[chip], send_sem=ssem.at[3 * a + k],
                                             recv_sem=rsem.at[3 * a + k], device_id=peer, device_id_type=MESH).start()
        token[...] = jnp.zeros_like(token)

    lands = [lax.empty((N_CHIP,) + s.shape, s.dtype) for s in shards]
    out_shape = ([pltpu.SemaphoreType.DMA((3 * n,)), pltpu.SemaphoreType.DMA((3 * n,)), pltpu.SemaphoreType.DMA((n,))]
                 + [pltpu.HBM(s.shape, s.dtype) for s in shards] + [pltpu.HBM(l.shape, l.dtype) for l in lands]
                 + [jax.ShapeDtypeStruct((8, 128), F32)])
    res = pl.pallas_call(
        body, name=name, out_shape=out_shape, in_specs=[HBM] * (2 * n) + [ANY],
        out_specs=[SEM, SEM, SEM] + [HBM] * (2 * n) + [pl.BlockSpec(memory_space=pltpu.VMEM)],
        input_output_aliases={a: 3 + a for a in range(2 * n)},
        compiler_params=pltpu.CompilerParams(has_side_effects=EFFECT),
    )(*[_in_hbm(s) for s in shards], *[_in_hbm(l) for l in lands], after)
    return tuple(res[:3]), list(res[3:3 + n]), list(res[3 + n:3 + 2 * n]), res[-1]


def gather_wait(sems, srcs, lands, idx, after, name):
    m = len(idx)

    def body(*refs):
        src, land = refs[:m], refs[m:2 * m]
        ssem, rsem, lsem = refs[2 * m:2 * m + 3]
        x, y, c = _axes()
        chip = 2 * x + y
        for j, a in enumerate(idx):
            for k, (peer, pchip) in enumerate(_chip_peers(x, y, c)):
                cp = pltpu.make_async_remote_copy(src_ref=src[j], dst_ref=land[j].at[pchip], send_sem=ssem.at[3 * a + k],
                                                  recv_sem=rsem.at[3 * a + k], device_id=peer, device_id_type=MESH)
                cp.wait_send()
                cp.wait_recv()
            pltpu.make_async_copy(src[j], land[j].at[chip], lsem.at[a]).wait()

    s_in = [srcs[a] for a in idx]
    l_in = [lands[a] for a in idx]
    res = pl.pallas_call(
        body, name=name,
        out_shape=[pltpu.HBM(s.shape, s.dtype) for s in s_in] + [pltpu.HBM(l.shape, l.dtype) for l in l_in],
        in_specs=[HBM] * (2 * m) + [SEM, SEM, SEM, ANY], out_specs=[HBM] * (2 * m),
        input_output_aliases={a: a for a in range(2 * m)},
        compiler_params=pltpu.CompilerParams(has_side_effects=EFFECT),
    )(*s_in, *l_in, *sems, after)
    return list(res[m:])


def scatter_start(grads, lands, slot, after, name):
    n = len(grads)

    def body(*refs):
        src, land = refs[:n], refs[n:2 * n]
        ssem, rsem, lsem = refs[2 * n + 1:2 * n + 4]
        token = refs[-1]
        x, y, c = _axes()
        chip = 2 * x + y
        for a in range(n):
            pltpu.make_async_copy(src[a].at[chip], land[a].at[slot[a], chip], lsem.at[a]).start()
            for k, (peer, pchip) in enumerate(_chip_peers(x, y, c)):
                pltpu.make_async_remote_copy(src_ref=src[a].at[pchip], dst_ref=land[a].at[slot[a], chip],
                                             send_sem=ssem.at[3 * a + k], recv_sem=rsem.at[3 * a + k],
                                             device_id=peer, device_id_type=MESH).start()
        token[...] = jnp.zeros_like(token)

    out_shape = ([pltpu.SemaphoreType.DMA((3 * n,)), pltpu.SemaphoreType.DMA((3 * n,)), pltpu.SemaphoreType.DMA((n,))]
                 + [pltpu.HBM(g.shape, g.dtype) for g in grads] + [pltpu.HBM(l.shape, l.dtype) for l in lands]
                 + [jax.ShapeDtypeStruct((8, 128), F32)])
    res = pl.pallas_call(
        body, name=name, out_shape=out_shape, in_specs=[HBM] * (2 * n) + [ANY],
        out_specs=[SEM, SEM, SEM] + [HBM] * (2 * n) + [pl.BlockSpec(memory_space=pltpu.VMEM)],
        input_output_aliases={a: 3 + a for a in range(2 * n)},
        compiler_params=pltpu.CompilerParams(has_side_effects=EFFECT),
    )(*[_in_hbm(g) for g in grads], *[_in_hbm(l) for l in lands], after)
    return tuple(res[:3]), list(res[3:3 + n]), list(res[3 + n:3 + 2 * n]), res[-1]


def scatter_wait(sems, grads, lands, slot, after, name):
    n = len(grads)

    def body(*refs):
        src, land = refs[:n], refs[n:2 * n]
        ssem, rsem, lsem = refs[2 * n:2 * n + 3]
        x, y, c = _axes()
        chip = 2 * x + y
        for a in range(n):
            for k, (peer, pchip) in enumerate(_chip_peers(x, y, c)):
                cp = pltpu.make_async_remote_copy(src_ref=src[a].at[pchip], dst_ref=land[a].at[slot[a], pchip],
                                                  send_sem=ssem.at[3 * a + k], recv_sem=rsem.at[3 * a + k],
                                                  device_id=peer, device_id_type=MESH)
                cp.wait_send()
                cp.wait_recv()
            pltpu.make_async_copy(src[a].at[chip], land[a].at[slot[a], chip], lsem.at[a]).wait()

    res = pl.pallas_call(
        body, name=name,
        out_shape=[pltpu.HBM(g.shape, g.dtype) for g in grads] + [pltpu.HBM(l.shape, l.dtype) for l in lands],
        in_specs=[HBM] * (2 * n) + [SEM, SEM, SEM, ANY], out_specs=[HBM] * (2 * n),
        input_output_aliases={a: a for a in range(2 * n)},
        compiler_params=pltpu.CompilerParams(has_side_effects=EFFECT),
    )(*grads, *lands, *sems, after)
    return list(res[n:])


def reduce4(land, name):
    nl, _, R, C = land.shape
    TR = _adam_rows(R, C)

    def body(l_ref, o_ref):
        o_ref[...] = ((l_ref[0].astype(F32) + l_ref[1].astype(F32)) + l_ref[2].astype(F32)) + l_ref[3].astype(F32)

    return pl.pallas_call(
        body, name=name, grid=(nl, R // TR),
        in_specs=[pl.BlockSpec((None, N_CHIP, TR, C), lambda i, r: (i, 0, r, 0))],
        out_specs=pl.BlockSpec((None, TR, C), lambda i, r: (i, r, 0)),
        out_shape=jax.ShapeDtypeStruct((nl, R, C), F32), compiler_params=_cp("parallel", "parallel"))(land)


def swap_siblings(arrs, name):
    n = len(arrs)

    def body(*refs):
        src, dst = refs[:n], refs[n:2 * n]
        ssem, rsem = refs[2 * n:]
        x, y, c = _axes()
        cps = [pltpu.make_async_remote_copy(src_ref=src[a], dst_ref=dst[a], send_sem=ssem.at[a], recv_sem=rsem.at[a],
                                            device_id=(x, y, 1 - c), device_id_type=MESH) for a in range(n)]
        for cp in cps:
            cp.start()
        for cp in cps:
            cp.wait()

    return pl.pallas_call(
        body, name=name, out_shape=[jax.ShapeDtypeStruct(a.shape, a.dtype) for a in arrs],
        in_specs=[ANY] * n, out_specs=[ANY] * n,
        scratch_shapes=[pltpu.SemaphoreType.DMA((n,)), pltpu.SemaphoreType.DMA((n,))],
        compiler_params=pltpu.CompilerParams(vmem_limit_bytes=VMEM_LIMIT),
    )(*arrs)


def swap_start(arrs, name):
    n = len(arrs)

    def body(*refs):
        src, land = refs[:n], refs[n:2 * n]
        ssem, rsem = refs[2 * n:2 * n + 2]
        token = refs[-1]
        x, y, c = _axes()
        for a in range(n):
            pltpu.make_async_remote_copy(src_ref=src[a], dst_ref=land[a], send_sem=ssem.at[a], recv_sem=rsem.at[a],
                                         device_id=(x, y, 1 - c), device_id_type=MESH).start()
        token[...] = jnp.zeros_like(token)

    lands = [lax.empty(a.shape, a.dtype) for a in arrs]
    out_shape = ([pltpu.SemaphoreType.DMA((n,)), pltpu.SemaphoreType.DMA((n,))]
                 + [pltpu.HBM(a.shape, a.dtype) for a in arrs] * 2 + [jax.ShapeDtypeStruct((8, 128), F32)])
    res = pl.pallas_call(
        body, name=name, out_shape=out_shape, in_specs=[HBM] * (2 * n),
        out_specs=[SEM, SEM] + [HBM] * (2 * n) + [pl.BlockSpec(memory_space=pltpu.VMEM)],
        input_output_aliases={a: 2 + a for a in range(2 * n)},
        compiler_params=pltpu.CompilerParams(has_side_effects=EFFECT),
    )(*[_in_hbm(a) for a in arrs], *[_in_hbm(l) for l in lands])
    return tuple(res[:2]), list(res[2:2 + n]), list(res[2 + n:2 + 2 * n]), res[-1]


def swap_wait(sems, srcs, lands, after, name):
    n = len(srcs)

    def body(*refs):
        src, land = refs[:n], refs[n:2 * n]
        ssem, rsem = refs[2 * n:2 * n + 2]
        x, y, c = _axes()
        for a in range(n):
            cp = pltpu.make_async_remote_copy(src_ref=src[a], dst_ref=land[a], send_sem=ssem.at[a],
                                              recv_sem=rsem.at[a], device_id=(x, y, 1 - c), device_id_type=MESH)
            cp.wait_send()
            cp.wait_recv()

    res = pl.pallas_call(
        body, name=name, out_shape=[pltpu.HBM(a.shape, a.dtype) for a in srcs] * 2,
        in_specs=[HBM] * (2 * n) + [SEM, SEM, ANY], out_specs=[HBM] * (2 * n),
        input_output_aliases={a: a for a in range(2 * n)},
        compiler_params=pltpu.CompilerParams(has_side_effects=EFFECT),
    )(*srcs, *lands, *sems, after)
    return list(res[:n]), list(res[n:])


def reduce8(slab, dm, name):
    RT, C = slab.shape
    P = RT // N_DEV
    R = dm.shape[0]

    def body(s_ref, dm_ref, o_ref, dmo_ref, recv, s1, r1, s2, r2, s3, r3):
        x, y, c = _axes()
        me = 4 * x + 2 * y + c
        mine = pl.ds(pl.multiple_of(me * P, 8), P)
        parts, dms = [], []
        for k in range(1, N_DEV):
            px, py, pc = _flip(x, (k >> 2) & 1), _flip(y, (k >> 1) & 1), _flip(c, k & 1)
            theirs = pl.ds(pl.multiple_of((4 * px + 2 * py + pc) * P, 8), P)
            cp = pltpu.make_async_remote_copy(src_ref=s_ref.at[theirs], dst_ref=recv.at[me], send_sem=s1.at[k - 1],
                                              recv_sem=r1.at[k - 1], device_id=(px, py, pc), device_id_type=MESH)
            cp.start()
            parts.append(cp)
            cd = pltpu.make_async_remote_copy(src_ref=dm_ref, dst_ref=dmo_ref.at[me], send_sem=s3.at[k - 1],
                                              recv_sem=r3.at[k - 1], device_id=(px, py, pc), device_id_type=MESH)
            cd.start()
            dms.append(cd)
        dmo_ref[me] = dm_ref[...]
        recv[me] = s_ref[mine, :]
        for cp in parts:
            cp.wait()
        tot = recv[0]
        for d in range(1, N_DEV):
            tot = tot + recv[d]
        o_ref[mine, :] = tot
        out = []
        for k in range(1, N_DEV):
            peer = (_flip(x, (k >> 2) & 1), _flip(y, (k >> 1) & 1), _flip(c, k & 1))
            cp = pltpu.make_async_remote_copy(src_ref=o_ref.at[mine], dst_ref=o_ref.at[mine], send_sem=s2.at[k - 1],
                                              recv_sem=r2.at[k - 1], device_id=peer, device_id_type=MESH)
            cp.start()
            out.append(cp)
        for cp in out + dms:
            cp.wait()

    sems = [pltpu.SemaphoreType.DMA((N_DEV - 1,))] * 6
    return pl.pallas_call(
        body, name=name,
        out_shape=[jax.ShapeDtypeStruct((RT, C), F32), jax.ShapeDtypeStruct((N_DEV, R, C), F32)],
        in_specs=[pl.BlockSpec(memory_space=pltpu.VMEM)] * 2, out_specs=[pl.BlockSpec(memory_space=pltpu.VMEM)] * 2,
        scratch_shapes=[pltpu.VMEM((N_DEV, P, C), F32)] + sems,
        compiler_params=pltpu.CompilerParams(vmem_limit_bytes=VMEM_LIMIT),
    )(slab, dm)


def mm_nn(a, w, out_dtype, name, res=None, gate=None):
    M, K = a.shape
    S, _, Ns = w.shape
    TM = _tile(M, (1024, 512, 256) if K <= 1024 else (512, 256))
    TN = _tile(Ns, (1408, 1024, 768, 512, 256, 128))
    nj = Ns // TN
    fused = res is not None

    def body(*refs):
        if fused:
            a_ref, w_ref, r_ref, g_ref, f_ref, o_ref = refs
        else:
            a_ref, w_ref, f_ref = refs
        f = jnp.dot(a_ref[...], w_ref[...], preferred_element_type=F32)
        f_ref[...] = f.astype(f_ref.dtype)
        if fused:
            o_ref[...] = r_ref[...] + g_ref[...] * f

    col = lambda s, j, i: (i, s * nj + j)
    in_specs = [pl.BlockSpec((TM, K), lambda s, j, i: (i, 0)), pl.BlockSpec((None, K, TN), lambda s, j, i: (s, 0, j))]
    out_specs = [pl.BlockSpec((TM, TN), col)]
    out_shape = [jax.ShapeDtypeStruct((M, S * Ns), out_dtype)]
    args = [a, w]
    if fused:
        in_specs += [pl.BlockSpec((TM, TN), col), pl.BlockSpec((1, TN), lambda s, j, i: (0, s * nj + j))]
        out_specs.append(pl.BlockSpec((TM, TN), col))
        out_shape.append(jax.ShapeDtypeStruct((M, S * Ns), F32))
        args += [res, gate]
    out = pl.pallas_call(body, name=name, grid=(S, nj, M // TM), in_specs=in_specs, out_specs=out_specs,
                         out_shape=out_shape, compiler_params=_cp("parallel", "parallel", "parallel"))(*args)
    return tuple(out) if fused else out[0]


def mm_nt(g, w, out_dtype, name):
    g3 = g if g.ndim == 3 else g[None]
    Q, M, F = g3.shape
    S, K, Ns = w.shape
    TM = _tile(M, (1024, 512, 256) if K <= 1024 else (512, 256))
    TN = _tile(Ns, (1408, 1024, 768, 512, 256, 128))
    nj = Ns // TN
    nred = S * nj
    per_part = F // TN

    def body(g_ref, w_ref, o_ref, acc):
        n = pl.program_id(1)

        @pl.when(n == 0)
        def _():
            acc[...] = jnp.zeros_like(acc)

        acc[...] += lax.dot_general(g_ref[...], w_ref[...], (((1,), (1,)), ((), ())), preferred_element_type=F32)

        @pl.when(n == nred - 1)
        def _():
            o_ref[...] = acc[...].astype(o_ref.dtype)

    return pl.pallas_call(
        body, name=name, grid=(M // TM, nred),
        in_specs=[pl.BlockSpec((None, TM, TN), lambda i, n: (n // per_part, i, n % per_part)),
                  pl.BlockSpec((None, K, TN), lambda i, n: (n // nj, 0, n % nj))],
        out_specs=pl.BlockSpec((TM, K), lambda i, n: (i, 0)),
        out_shape=jax.ShapeDtypeStruct((M, K), out_dtype),
        scratch_shapes=[pltpu.VMEM((TM, K), F32)],
        compiler_params=_cp("parallel", "arbitrary"))(g3, w)


def mm_tn(a, g, S, name):
    M, K = a.shape
    g3 = g if g.ndim == 3 else g[None]
    Q, _, F = g3.shape
    Ns = Q * F // S
    TK = _tile(K, (256, 128))
    TN = _tile(Ns, (1408, 1024, 768, 512, 256, 128))
    nj = Ns // TN
    per_part = F // TN

    def body(a_ref, g_ref, o_ref):
        o_ref[...] = lax.dot_general(a_ref[...], g_ref[...], (((0,), (0,)), ((), ())),
                                     preferred_element_type=F32).astype(o_ref.dtype)

    return pl.pallas_call(
        body, name=name, grid=(S * nj, K // TK),
        in_specs=[pl.BlockSpec((M, TK), lambda n, k: (0, k)),
                  pl.BlockSpec((None, M, TN), lambda n, k: (n // per_part, 0, n % per_part))],
        out_specs=pl.BlockSpec((None, TK, TN), lambda n, k: (n // nj, k, n % nj)),
        out_shape=jax.ShapeDtypeStruct((S, K, Ns), BF),
        compiler_params=_cp("parallel", "parallel"))(a, g3)


def _rows(TL, D):
    return pl.BlockSpec((TL, D), lambda i: (i, 0))


def _fixed(R, D):
    return pl.BlockSpec((R, D), lambda i: (0, 0))


def _rowsum8(v):
    T, D = v.shape
    return jnp.sum(v.reshape(T // 8, 8, D), axis=0)


def _norm_parts(xv):
    r = lax.rsqrt(jnp.mean(xv * xv, axis=-1, keepdims=True) + RMS_EPS)
    return xv * r, r


def norm_mod(x, gamma, mods, k_shift, out_dtype, name):
    L, D = x.shape
    TL = _tile(L, (512, 256))

    def body(x_ref, g_ref, m_ref, o_ref):
        xn, _ = _norm_parts(x_ref[...])
        sh, sc = m_ref[k_shift:k_shift + 1, :], m_ref[k_shift + 1:k_shift + 2, :]
        o_ref[...] = ((xn * g_ref[...]) * (1.0 + sc) + sh).astype(o_ref.dtype)

    return pl.pallas_call(body, name=name, grid=(L // TL,),
                          in_specs=[_rows(TL, D), _fixed(1, D), _fixed(6, D)], out_specs=_rows(TL, D),
                          out_shape=jax.ShapeDtypeStruct((L, D), out_dtype), compiler_params=_cp("parallel"))(x, gamma, mods)


def norm_bwd(dh, x, dres, gamma, mods, k_shift, name, branch=None):
    L, D = x.shape
    TL = _tile(L, (512, 256))
    nacc = 4 if branch else 3

    def body(*refs):
        if branch:
            dh_ref, x_ref, dr_ref, g_ref, m_ref, f_ref, fm_ref, dx_ref, s_ref, df_ref, acc = refs
        else:
            dh_ref, x_ref, dr_ref, g_ref, m_ref, dx_ref, s_ref, acc = refs
        i = pl.program_id(0)

        @pl.when(i == 0)
        def _():
            acc[...] = jnp.zeros_like(acc)

        xn, r = _norm_parts(x_ref[...])
        dh_v = dh_ref[...].astype(F32)
        gam = g_ref[...]
        sc = m_ref[k_shift + 1:k_shift + 2, :]
        dn = dh_v * (1.0 + sc)
        dxn = dn * gam
        dx = dr_ref[...] + r * (dxn - xn * jnp.mean(dxn * xn, axis=-1, keepdims=True))
        dx_ref[...] = dx
        acc[0] += _rowsum8(dh_v)
        acc[1] += _rowsum8(dh_v * (xn * gam))
        acc[2] += _rowsum8(dn * xn)
        if branch:
            df_ref[...] = (dx * fm_ref[branch[2]:branch[2] + 1, :]).astype(df_ref.dtype)
            acc[3] += _rowsum8(dx * f_ref[...].astype(F32))

        @pl.when(i == pl.num_programs(0) - 1)
        def _():
            s_ref[...] = jnp.zeros_like(s_ref)
            for q in range(nacc):
                s_ref[q:q + 1, :] = jnp.sum(acc[q], axis=0, keepdims=True)

    in_specs = [_rows(TL, D), _rows(TL, D), _rows(TL, D), _fixed(1, D), _fixed(6, D)]
    out_specs = [_rows(TL, D), _fixed(8, D)]
    out_shape = [jax.ShapeDtypeStruct((L, D), F32), jax.ShapeDtypeStruct((8, D), F32)]
    args = [dh, x, dres, gamma, mods]
    if branch:
        in_specs += [_rows(TL, D), _fixed(6, D)]
        out_specs.append(_rows(TL, D))
        out_shape.append(jax.ShapeDtypeStruct((L, D), BF))
        args += [branch[0], branch[1]]
    return pl.pallas_call(
        body, name=name, grid=(L // TL,), in_specs=in_specs, out_specs=out_specs, out_shape=out_shape,
        scratch_shapes=[pltpu.VMEM((nacc, 8, D), F32)], compiler_params=_cp("arbitrary"))(*args)


def gate_bwd(dx, f, mods, k_gate, name):
    L, D = dx.shape
    TL = _tile(L, (512, 256))

    def body(dx_ref, f_ref, m_ref, o_ref, s_ref, acc):
        i = pl.program_id(0)

        @pl.when(i == 0)
        def _():
            acc[...] = jnp.zeros_like(acc)

        dxv = dx_ref[...]
        o_ref[...] = (dxv * m_ref[k_gate:k_gate + 1, :]).astype(o_ref.dtype)
        acc[...] += _rowsum8(dxv * f_ref[...].astype(F32))

        @pl.when(i == pl.num_programs(0) - 1)
        def _():
            s_ref[...] = jnp.zeros_like(s_ref)
            s_ref[0:1, :] = jnp.sum(acc[...], axis=0, keepdims=True)

    return pl.pallas_call(
        body, name=name, grid=(L // TL,), in_specs=[_rows(TL, D), _rows(TL, D), _fixed(6, D)],
        out_specs=[_rows(TL, D), _fixed(8, D)],
        out_shape=[jax.ShapeDtypeStruct((L, D), BF), jax.ShapeDtypeStruct((8, D), F32)],
        scratch_shapes=[pltpu.VMEM((8, D), F32)], compiler_params=_cp("arbitrary"))(dx, f, mods)


def ffn_in_act(a, w, name):
    M, K = a.shape
    S, _, Ns = w.shape
    half = S // 2
    TM = _tile(M, (512, 256))
    TN = _tile(Ns, (1408, 1024, 768, 512, 256, 128))
    nj = Ns // TN

    def body(a_ref, wg_ref, wu_ref, gu_ref, act_ref):
        av = a_ref[...]
        g = jnp.dot(av, wg_ref[...], preferred_element_type=F32)
        u = jnp.dot(av, wu_ref[...], preferred_element_type=F32)
        gu_ref[0] = g.astype(gu_ref.dtype)
        gu_ref[1] = u.astype(gu_ref.dtype)
        act_ref[...] = (g * jax.nn.sigmoid(g) * u).astype(act_ref.dtype)

    return pl.pallas_call(
        body, name=name, grid=(half, nj, M // TM),
        in_specs=[pl.BlockSpec((TM, K), lambda s, j, i: (i, 0)),
                  pl.BlockSpec((None, K, TN), lambda s, j, i: (s, 0, j)),
                  pl.BlockSpec((None, K, TN), lambda s, j, i: (s + half, 0, j))],
        out_specs=[pl.BlockSpec((2, TM, TN), lambda s, j, i: (0, i, s * nj + j)),
                   pl.BlockSpec((TM, TN), lambda s, j, i: (i, s * nj + j))],
        out_shape=[jax.ShapeDtypeStruct((2, M, half * Ns), BF), jax.ShapeDtypeStruct((M, half * Ns), BF)],
        compiler_params=_cp("parallel", "parallel", "parallel"))(a, w, w)


def ffn_out_bwd(dff, w2, gu, name):
    M, D = dff.shape
    F = w2.shape[0]
    TM = _tile(M, (512, 256))
    CW = _tile(F, (256, 128))

    def body(d_ref, w_ref, gu_ref, o_ref):
        dv = d_ref[...]
        for c in range(0, F, CW):
            da = lax.dot_general(dv, w_ref[c:c + CW, :], (((1,), (1,)), ((), ())), preferred_element_type=F32)
            g = gu_ref[0, :, c:c + CW].astype(F32)
            u = gu_ref[1, :, c:c + CW].astype(F32)
            s = jax.nn.sigmoid(g)
            o_ref[0, :, c:c + CW] = (da * u * (s + g * s * (1.0 - s))).astype(o_ref.dtype)
            o_ref[1, :, c:c + CW] = (da * g * s).astype(o_ref.dtype)

    part = pl.BlockSpec((2, TM, F), lambda i: (0, i, 0))
    return pl.pallas_call(
        body, name=name, grid=(M // TM,),
        in_specs=[pl.BlockSpec((TM, D), lambda i: (i, 0)), pl.BlockSpec((F, D), lambda i: (0, 0)), part],
        out_specs=part, out_shape=jax.ShapeDtypeStruct((2, M, F), BF),
        compiler_params=_cp("parallel"))(dff, w2, gu)


def swiglu_act(gu, name):
    L, F2 = gu.shape
    F = F2 // 2
    TL = _tile(L, (256,))

    def body(gu_ref, o_ref):
        g = gu_ref[:, :F].astype(F32)
        u = gu_ref[:, F:].astype(F32)
        o_ref[...] = (g * jax.nn.sigmoid(g) * u).astype(o_ref.dtype)

    return pl.pallas_call(body, name=name, grid=(L // TL,), in_specs=[_rows(TL, F2)], out_specs=_rows(TL, F),
                          out_shape=jax.ShapeDtypeStruct((L, F), BF), compiler_params=_cp("parallel"))(gu)


def swiglu_bwd(da, gu, name):
    L, F2 = gu.shape
    F = F2 // 2
    TL = _tile(L, (256,))

    def body(da_ref, gu_ref, o_ref):
        g = gu_ref[:, :F].astype(F32)
        u = gu_ref[:, F:].astype(F32)
        d = da_ref[...].astype(F32)
        s = jax.nn.sigmoid(g)
        o_ref[:, :F] = (d * u * (s + g * s * (1.0 - s))).astype(o_ref.dtype)
        o_ref[:, F:] = (d * g * s).astype(o_ref.dtype)

    return pl.pallas_call(body, name=name, grid=(L // TL,), in_specs=[_rows(TL, F), _rows(TL, F2)],
                          out_specs=_rows(TL, F2), out_shape=jax.ShapeDtypeStruct((L, F2), BF),
                          compiler_params=_cp("parallel"))(da, gu)


def glu_res(o, x, mods, k_gate, name):
    L, D = x.shape
    TL = _tile(L, (512, 256))

    def body(o_ref, x_ref, m_ref, mix_ref, y_ref):
        mix = o_ref[:, :D].astype(F32) * jax.nn.sigmoid(o_ref[:, D:].astype(F32))
        mix_ref[...] = mix.astype(mix_ref.dtype)
        y_ref[...] = x_ref[...] + m_ref[k_gate:k_gate + 1, :] * mix

    return pl.pallas_call(
        body, name=name, grid=(L // TL,), in_specs=[_rows(TL, 2 * D), _rows(TL, D), _fixed(6, D)],
        out_specs=[_rows(TL, D), _rows(TL, D)],
        out_shape=[jax.ShapeDtypeStruct((L, D), BF), jax.ShapeDtypeStruct((L, D), F32)],
        compiler_params=_cp("parallel"))(o, x, mods)


def glu_bwd(dmix, o, name):
    L, D2 = o.shape
    D = D2 // 2
    TL = _tile(L, (512, 256))

    def body(d_ref, o_ref, do_ref):
        d = d_ref[...].astype(F32)
        val = o_ref[:, :D].astype(F32)
        s = jax.nn.sigmoid(o_ref[:, D:].astype(F32))
        do_ref[:, :D] = (d * s).astype(do_ref.dtype)
        do_ref[:, D:] = (d * val * s * (1.0 - s)).astype(do_ref.dtype)

    return pl.pallas_call(body, name=name, grid=(L // TL,), in_specs=[_rows(TL, D), _rows(TL, D2)],
                          out_specs=_rows(TL, D2), out_shape=jax.ShapeDtypeStruct((L, D2), BF),
                          compiler_params=_cp("parallel"))(dmix, o)


def final_loss(x, target, gamma, f, fmods, k_gate, name):
    L, D = x.shape
    TL = _tile(L, (512, 256))

    def body(x_ref, t_ref, g_ref, f_ref, fm_ref, l_ref, dx_ref, s_ref, df_ref, acc, lacc):
        i = pl.program_id(0)

        @pl.when(i == 0)
        def _():
            acc[...] = jnp.zeros_like(acc)
            lacc[...] = jnp.zeros_like(lacc)

        xn, r = _norm_parts(x_ref[...])
        gam = g_ref[...]
        e = xn * gam - t_ref[...]
        lacc[...] += jnp.sum(0.5 * jnp.mean(e * e, axis=-1, keepdims=True), axis=0, keepdims=True)
        dy = e * (1.0 / D)
        dxn = dy * gam
        dx = r * (dxn - xn * jnp.mean(dxn * xn, axis=-1, keepdims=True))
        dx_ref[...] = dx
        df_ref[...] = (dx * fm_ref[k_gate:k_gate + 1, :]).astype(df_ref.dtype)
        acc[0] += _rowsum8(dy * xn)
        acc[1] += _rowsum8(dx * f_ref[...].astype(F32))

        @pl.when(i == pl.num_programs(0) - 1)
        def _():
            s_ref[...] = jnp.zeros_like(s_ref)
            for q in range(2):
                s_ref[q:q + 1, :] = jnp.sum(acc[q], axis=0, keepdims=True)
            l_ref[...] = jnp.broadcast_to(lacc[...], l_ref.shape)

    return pl.pallas_call(
        body, name=name, grid=(L // TL,),
        in_specs=[_rows(TL, D), _rows(TL, D), _fixed(1, D), _rows(TL, D), _fixed(6, D)],
        out_specs=[_fixed(8, 128), _rows(TL, D), _fixed(8, D), _rows(TL, D)],
        out_shape=[jax.ShapeDtypeStruct((8, 128), F32), jax.ShapeDtypeStruct((L, D), F32),
                   jax.ShapeDtypeStruct((8, D), F32), jax.ShapeDtypeStruct((L, D), BF)],
        scratch_shapes=[pltpu.VMEM((2, 8, D), F32), pltpu.VMEM((1, 1), F32)],
        compiler_params=_cp("arbitrary"))(x, target, gamma, f, fmods)


def _col(L, TC, off):
    return pl.BlockSpec((L, TC), lambda j: (0, off + j))


def _shift_down(v, k, row):
    return jnp.where(row >= k, pltpu.roll(v, k, 0), 0.0)


def _shift_up(v, k, row, L):
    return jnp.where(row < L - k, pltpu.roll(v, L - k, 0), 0.0)


def conv_fwd(p, w, name):
    L, D3 = p.shape
    D = D3 // 3
    TC = _tile(D, (128,))
    nc = D // TC

    def body(b_ref, c_ref, v_ref, w_ref, o_ref):
        row = lax.broadcasted_iota(jnp.int32, (L, TC), 0)
        cv = c_ref[...].astype(F32) * v_ref[...].astype(F32)
        conv = w_ref[2:3, :] * cv + w_ref[1:2, :] * _shift_down(cv, 1, row) + w_ref[0:1, :] * _shift_down(cv, 2, row)
        o_ref[...] = (b_ref[...].astype(F32) * conv).astype(o_ref.dtype)

    return pl.pallas_call(
        body, name=name, grid=(nc,),
        in_specs=[_col(L, TC, 0), _col(L, TC, nc), _col(L, TC, 2 * nc), pl.BlockSpec((3, TC), lambda j: (0, j))],
        out_specs=_col(L, TC, 0), out_shape=jax.ShapeDtypeStruct((L, D), BF), compiler_params=_cp("parallel"))(p, p, p, w)


def conv_bwd(dm, p, w, name):
    L, D3 = p.shape
    D = D3 // 3
    TC = _tile(D, (128,))
    nc = D // TC

    def body(dm_ref, b_ref, c_ref, v_ref, w_ref, db_ref, dc_ref, dv_ref, dw_ref):
        row = lax.broadcasted_iota(jnp.int32, (L, TC), 0)
        cg, vv = c_ref[...].astype(F32), v_ref[...].astype(F32)
        cv = cg * vv
        cv1, cv2 = _shift_down(cv, 1, row), _shift_down(cv, 2, row)
        conv = w_ref[2:3, :] * cv + w_ref[1:2, :] * cv1 + w_ref[0:1, :] * cv2
        dmv = dm_ref[...].astype(F32)
        db_ref[...] = (dmv * conv).astype(db_ref.dtype)
        dconv = dmv * b_ref[...].astype(F32)
        dcv = (w_ref[2:3, :] * dconv + w_ref[1:2, :] * _shift_up(dconv, 1, row, L)
               + w_ref[0:1, :] * _shift_up(dconv, 2, row, L))
        dc_ref[...] = (dcv * vv).astype(dc_ref.dtype)
        dv_ref[...] = (dcv * cg).astype(dv_ref.dtype)
        dw_ref[...] = jnp.zeros_like(dw_ref)
        dw_ref[0:1, :] = jnp.sum(dconv * cv2, axis=0, keepdims=True)
        dw_ref[1:2, :] = jnp.sum(dconv * cv1, axis=0, keepdims=True)
        dw_ref[2:3, :] = jnp.sum(dconv * cv, axis=0, keepdims=True)

    one = jax.ShapeDtypeStruct((L, D), BF)
    return pl.pallas_call(
        body, name=name, grid=(nc,),
        in_specs=[_col(L, TC, 0), _col(L, TC, 0), _col(L, TC, nc), _col(L, TC, 2 * nc),
                  pl.BlockSpec((3, TC), lambda j: (0, j))],
        out_specs=[_col(L, TC, 0), _col(L, TC, 0), _col(L, TC, 0), pl.BlockSpec((8, TC), lambda j: (0, j))],
        out_shape=[one, one, one, jax.ShapeDtypeStruct((8, D), F32)],
        compiler_params=_cp("parallel"))(dm, p, p, p, w)


def _gelu(y):
    return 0.5 * y * (1.0 + jnp.tanh(GELU_C * (y + GELU_A * y * y * y)))


def _gelu_grad(y):
    th = jnp.tanh(GELU_C * (y + GELU_A * y * y * y))
    return 0.5 * (1.0 + th) + 0.5 * y * (1.0 - th * th) * GELU_C * (1.0 + 3.0 * GELU_A * y * y)


def _cmul_add(br, bi, ar, ai, sr, si):
    return br + ar * sr - ai * si, bi + ar * si + ai * sr


def _log2(n):
    k = n.bit_length() - 1
    assert 1 << k == n
    return k


def _replicate(P2, W2, P, GLP, transposed):
    shape = (W2, P2) if transposed else (P2, W2)
    k = lax.broadcasted_iota(jnp.int32, shape, 1 if transposed else 0)
    c = lax.broadcasted_iota(jnp.int32, shape, 0 if transposed else 1)
    return ((k >> _log2(P)) == (c >> _log2(GLP))) & ((k & (P - 1)) == (c & (P - 1)))


def _on_diagonal(KB, W2, H, P, GLP, transposed):
    shape = (W2, KB) if transposed else (KB, W2)
    r = lax.broadcasted_iota(jnp.int32, shape, 1 if transposed else 0)
    c = lax.broadcasted_iota(jnp.int32, shape, 0 if transposed else 1)
    return (r >> _log2(H)) == ((c & (GLP - 1)) >> _log2(P))


def _expand(t, dims, transposed):
    KB, W2, H, P, GLP = dims
    rep = _replicate(2 * P, W2, P, GLP, transposed).astype(t.dtype)
    wide = jnp.dot(rep, t, preferred_element_type=F32) if transposed else jnp.dot(t, rep, preferred_element_type=F32)
    return jnp.where(_on_diagonal(KB, W2, H, P, GLP, transposed), wide, 0.0).astype(t.dtype)


def _extract(acc, dims):
    KB, W2, H, P, GLP = dims
    rep = _replicate(2 * P, W2, P, GLP, True).astype(F32)
    kept = jnp.where(_on_diagonal(KB, W2, H, P, GLP, False), acc, 0.0)
    return jnp.dot(kept, rep, preferred_element_type=F32, precision=lax.Precision.HIGHEST)


def _cmul(ar, ai, sr, si):
    return ar * sr - ai * si, ar * si + ai * sr


LANES = 128


def _cols(ref, base, n, rows):
    return jnp.concatenate([ref[base + q, rows, :] for q in range(n)], axis=1)


def _set_cols(ref, base, n, rows, val):
    for q in range(n):
        ref[base + q, rows, :] = val[:, q * LANES:(q + 1) * LANES]


def _strided_s5_fwd(h, tb, tct, pw, dvec, name):
    L, D = h.shape
    nkb, KB, P2 = tb.shape
    P = P2 // 2
    W = (KB // SSM_GROUP) * P
    W2 = 2 * W
    dims = (KB, W2, SSM_GROUP, P, W)
    TL = _tile(L, (512, 256))
    CH = TL // 8
    NC = W // LANES

    def body(h_ref, tb_ref, tct_ref, pw_ref, d_ref, s_ref, y_ref, z_ref, bw, cw, carry):
        t = pl.program_id(1)

        @pl.when(t == 0)
        def _():
            carry[...] = jnp.zeros_like(carry)
            bw[...] = _expand(tb_ref[...], dims, False)
            cw[...] = _expand(tct_ref[...], dims, True)

        hv = h_ref[...]
        _set_cols(s_ref, 0, 2 * NC, slice(None), jnp.dot(hv.astype(BF), bw[...], preferred_element_type=F32))
        ar, ai = pw_ref[0:8, :W], pw_ref[0:8, W:]
        xr = xi = jnp.zeros((8, W), F32)
        for j in range(CH):
            rows = pl.ds(j, 8, stride=CH)
            xr, xi = _cmul_add(_cols(s_ref, 0, NC, rows), _cols(s_ref, NC, NC, rows), ar, ai, xr, xi)
            _set_cols(s_ref, 0, NC, rows, xr)
            _set_cols(s_ref, NC, NC, rows, xi)
        for k, off in ((1, 8), (2, 16), (4, 24)):
            xr, xi = _cmul_add(xr, xi, pw_ref[off:off + 8, :W], pw_ref[off:off + 8, W:],
                               pltpu.roll(xr, k, 0), pltpu.roll(xi, k, 0))
        xr, xi = _cmul_add(xr, xi, pw_ref[32:40, :W], pw_ref[32:40, W:], carry[0], carry[1])
        first = lax.broadcasted_iota(jnp.int32, (8, W), 0) == 0
        cr = jnp.where(first, carry[0], pltpu.roll(xr, 1, 0))
        ci = jnp.where(first, carry[1], pltpu.roll(xi, 1, 0))
        carry[0] = jnp.broadcast_to(xr[7:8], (8, W))
        carry[1] = jnp.broadcast_to(xi[7:8], (8, W))
        for j in range(CH):
            rows = pl.ds(j, 8, stride=CH)
            cr, ci = _cmul(ar, ai, cr, ci)
            _set_cols(s_ref, 0, NC, rows, _cols(s_ref, 0, NC, rows) + cr)
            _set_cols(s_ref, NC, NC, rows, _cols(s_ref, NC, NC, rows) + ci)
        sv = _cols(s_ref, 0, 2 * NC, slice(None))
        y = jnp.dot(sv.astype(BF), cw[...], preferred_element_type=F32) + d_ref[...] * hv
        y_ref[...] = y
        z_ref[...] = _gelu(y).astype(z_ref.dtype)

    blk = lambda kb, t: (t, kb)
    per_kb = lambda kb, t: (kb, 0, 0)
    return pl.pallas_call(
        body, name=name, grid=(nkb, L // TL),
        in_specs=[pl.BlockSpec((TL, KB), blk), pl.BlockSpec((None, KB, P2), per_kb),
                  pl.BlockSpec((None, P2, KB), per_kb), pl.BlockSpec((None, 40, W2), per_kb),
                  pl.BlockSpec((1, KB), lambda kb, t: (0, kb))],
        out_specs=[pl.BlockSpec((2 * NC, TL, LANES), lambda kb, t: (kb, t, 0)), pl.BlockSpec((TL, KB), blk),
                   pl.BlockSpec((TL, KB), blk)],
        out_shape=[jax.ShapeDtypeStruct((nkb * 2 * NC, L, LANES), F32), jax.ShapeDtypeStruct((L, D), F32),
                   jax.ShapeDtypeStruct((L, D), BF)],
        scratch_shapes=[pltpu.VMEM((KB, W2), BF), pltpu.VMEM((W2, KB), BF), pltpu.VMEM((2, 8, W), F32)],
        compiler_params=_cp("parallel", "arbitrary"))(h, tb, tct, pw, dvec)


def _strided_s5_bwd(dz, y, h, s, tc, tbt, pwr, dvec, name):
    L, D = h.shape
    nkb, KB, P2 = tc.shape
    P = P2 // 2
    W = (KB // SSM_GROUP) * P
    W2 = 2 * W
    dims = (KB, W2, SSM_GROUP, P, W)
    TL = _tile(L, (512, 256))
    CH = TL // 8
    NC = W // LANES
    nt = L // TL

    def body(dz_ref, y_ref, h_ref, s_ref, sp_ref, tc_ref, tbt_ref, pw_ref, d_ref,
             dh_ref, dd_ref, da_ref, db_ref, dc_ref, g, ctw, btw, dbacc, dcacc, carry):
        t = pl.program_id(1)

        @pl.when(t == 0)
        def _():
            carry[...] = jnp.zeros_like(carry)
            dd_ref[...] = jnp.zeros_like(dd_ref)
            da_ref[...] = jnp.zeros_like(da_ref)
            dbacc[...] = jnp.zeros_like(dbacc)
            dcacc[...] = jnp.zeros_like(dcacc)
            ctw[...] = _expand(tc_ref[...], dims, False)
            btw[...] = _expand(tbt_ref[...], dims, True)

        hv = h_ref[...]
        dy = dz_ref[...].astype(F32) * _gelu_grad(y_ref[...])
        dd_ref[...] += _rowsum8(dy * hv)
        dyb = dy.astype(BF)
        _set_cols(g, 0, 2 * NC, slice(None), jnp.dot(dyb, ctw[...], preferred_element_type=F32))
        ar, ai = pw_ref[0:8, :W], pw_ref[0:8, W:]
        gr = gi = jnp.zeros((8, W), F32)
        for j in reversed(range(CH)):
            rows = pl.ds(j, 8, stride=CH)
            gr, gi = _cmul_add(_cols(g, 0, NC, rows), _cols(g, NC, NC, rows), ar, ai, gr, gi)
            _set_cols(g, 0, NC, rows, gr)
            _set_cols(g, NC, NC, rows, gi)
        for k, off in ((1, 8), (2, 16), (4, 24)):
            gr, gi = _cmul_add(gr, gi, pw_ref[off:off + 8, :W], pw_ref[off:off + 8, W:],
                               pltpu.roll(gr, 8 - k, 0), pltpu.roll(gi, 8 - k, 0))
        gr, gi = _cmul_add(gr, gi, pw_ref[32:40, :W], pw_ref[32:40, W:], carry[0], carry[1])
        sub = lax.broadcasted_iota(jnp.int32, (8, W), 0)
        cr = jnp.where(sub == 7, carry[0], pltpu.roll(gr, 7, 0))
        ci = jnp.where(sub == 7, carry[1], pltpu.roll(gi, 7, 0))
        carry[0] = jnp.broadcast_to(gr[0:1], (8, W))
        carry[1] = jnp.broadcast_to(gi[0:1], (8, W))
        live = jnp.where(t == nt - 1, 0.0, 1.0)
        accr = acci = jnp.zeros((8, W), F32)
        for j in reversed(range(CH)):
            rows = pl.ds(j, 8, stride=CH)
            cr, ci = _cmul(ar, ai, cr, ci)
            gr, gi = _cols(g, 0, NC, rows) + cr, _cols(g, NC, NC, rows) + ci
            _set_cols(g, 0, NC, rows, gr)
            _set_cols(g, NC, NC, rows, gi)
            if j > 0:
                before = pl.ds(j - 1, 8, stride=CH)
                pr, pi = _cols(s_ref, 0, NC, before), _cols(s_ref, NC, NC, before)
            else:
                last = pl.ds(CH - 1, 8, stride=CH)
                pr = jnp.where(sub == 0, _cols(sp_ref, 0, NC, slice(7, 8)) * live,
                               pltpu.roll(_cols(s_ref, 0, NC, last), 1, 0))
                pi = jnp.where(sub == 0, _cols(sp_ref, NC, NC, slice(7, 8)) * live,
                               pltpu.roll(_cols(s_ref, NC, NC, last), 1, 0))
            accr = accr + pr * gr + pi * gi
            acci = acci + pr * gi - pi * gr
        da_ref[:, :W] += accr
        da_ref[:, W:] += acci

        gb = _cols(g, 0, 2 * NC, slice(None)).astype(BF)
        dh_ref[...] = dy * d_ref[...] + jnp.dot(gb, btw[...], preferred_element_type=F32)
        tn = (((0,), (0,)), ((), ()))
        dbacc[...] += lax.dot_general(hv.astype(BF), gb, tn, preferred_element_type=F32)
        dcacc[...] += lax.dot_general(dyb, _cols(s_ref, 0, 2 * NC, slice(None)).astype(BF), tn,
                                      preferred_element_type=F32)

        @pl.when(t == nt - 1)
        def _():
            db_ref[...] = _extract(dbacc[...], dims)
            dc_ref[...] = _extract(dcacc[...], dims)

    rev = lambda kb, t: (nt - 1 - t, kb)
    per_kb = lambda kb, t: (kb, 0, 0)
    return pl.pallas_call(
        body, name=name, grid=(nkb, nt),
        in_specs=[pl.BlockSpec((TL, KB), rev), pl.BlockSpec((TL, KB), rev), pl.BlockSpec((TL, KB), rev),
                  pl.BlockSpec((2 * NC, TL, LANES), lambda kb, t: (kb, nt - 1 - t, 0)),
                  pl.BlockSpec((2 * NC, 8, LANES), lambda kb, t: (kb, jnp.maximum((nt - 1 - t) * CH - 1, 0), 0)),
                  pl.BlockSpec((None, KB, P2), per_kb), pl.BlockSpec((None, P2, KB), per_kb),
                  pl.BlockSpec((None, 40, W2), per_kb), pl.BlockSpec((1, KB), lambda kb, t: (0, kb))],
        out_specs=[pl.BlockSpec((TL, KB), rev), pl.BlockSpec((8, KB), lambda kb, t: (0, kb)),
                   pl.BlockSpec((None, 8, W2), per_kb), pl.BlockSpec((None, KB, P2), per_kb),
                   pl.BlockSpec((None, KB, P2), per_kb)],
        out_shape=[jax.ShapeDtypeStruct((L, D), F32), jax.ShapeDtypeStruct((8, D), F32),
                   jax.ShapeDtypeStruct((nkb, 8, W2), F32), jax.ShapeDtypeStruct((nkb, KB, P2), F32),
                   jax.ShapeDtypeStruct((nkb, KB, P2), F32)],
        scratch_shapes=[pltpu.VMEM((2 * NC, TL, LANES), F32), pltpu.VMEM((KB, W2), BF), pltpu.VMEM((W2, KB), BF),
                        pltpu.VMEM((KB, W2), F32), pltpu.VMEM((KB, W2), F32), pltpu.VMEM((2, 8, W), F32)],
        compiler_params=_cp("parallel", "arbitrary"))(dz, y, h, s, s, tc, tbt, pwr, dvec)


def _chunk_order(TL, CH, transposed):
    out_row = lax.broadcasted_iota(jnp.int32, (TL, TL), 1 if transposed else 0)
    in_row = lax.broadcasted_iota(jnp.int32, (TL, TL), 0 if transposed else 1)
    return in_row == ((out_row & 7) << _log2(CH)) + (out_row >> 3)


def _reorder(perm, v):
    hi = v.astype(perm.dtype)
    lo = (v - hi.astype(F32)).astype(perm.dtype)
    return jnp.dot(perm, hi, preferred_element_type=F32) + jnp.dot(perm, lo, preferred_element_type=F32)


def _interleave(main, side):
    n, m, k = len(main), len(side), 0
    for i, step in enumerate(main):
        step()
        while k < m and (k + 1) * n <= (i + 1) * m:
            side[k]()
            k += 1
    for step in side[k:]:
        step()


S5_CHUNK = 512


def s5_fwd(h, tb, tct, pw, dvec, name):
    L, D = h.shape
    nkb, KB, P2 = tb.shape
    P = P2 // 2
    W = (KB // SSM_GROUP) * P
    W2 = 2 * W
    dims = (KB, W2, SSM_GROUP, P, W)
    TL = _tile(L, (512, 256))
    CH = TL // 8
    NB = 2 if nkb % 2 == 0 else 1
    CK = min(S5_CHUNK, W2)

    def body(h_ref, tb_ref, tct_ref, pw_ref, d_ref, s_ref, y_ref, z_ref, bw, cw, perm, unperm, carry):
        t = pl.program_id(1)

        @pl.when(t == 0)
        def _():
            carry[...] = jnp.zeros_like(carry)
            for b in range(NB):
                bw[b] = _expand(tb_ref[b], dims, False)
                cw[b] = _expand(tct_ref[b], dims, True)
            perm[...] = _chunk_order(TL, CH, False).astype(perm.dtype)
            unperm[...] = _chunk_order(TL, CH, True).astype(perm.dtype)

        hp = _reorder(perm[...], h_ref[...])
        hpb = hp.astype(BF)
        first = lax.broadcasted_iota(jnp.int32, (8, W), 0) == 0

        def project(b):
            def chunk(c):
                def emit():
                    s_ref[:, b * W2 + c:b * W2 + c + CK] = jnp.dot(hpb[:, b * KB:(b + 1) * KB], bw[b, :, c:c + CK],
                                                                   preferred_element_type=F32)
                return emit
            return [chunk(c) for c in range(0, W2, CK)]

        def scan(b):
            re, im = slice(b * W2, b * W2 + W), slice(b * W2 + W, (b + 1) * W2)
            ar, ai = pw_ref[b, 0:8, :W], pw_ref[b, 0:8, W:]
            st = {"x": (jnp.zeros((8, W), F32), jnp.zeros((8, W), F32))}

            def own(j):
                def emit():
                    rows = slice(j * 8, j * 8 + 8)
                    xr, xi = _cmul_add(s_ref[rows, re], s_ref[rows, im], ar, ai, *st["x"])
                    s_ref[rows, re] = xr
                    s_ref[rows, im] = xi
                    st["x"] = (xr, xi)
                return emit

            def ends():
                xr, xi = st["x"]
                for k, off in ((1, 8), (2, 16), (4, 24)):
                    xr, xi = _cmul_add(xr, xi, pw_ref[b, off:off + 8, :W], pw_ref[b, off:off + 8, W:],
                                       pltpu.roll(xr, k, 0), pltpu.roll(xi, k, 0))
                xr, xi = _cmul_add(xr, xi, pw_ref[b, 32:40, :W], pw_ref[b, 32:40, W:], carry[b, 0], carry[b, 1])
                st["c"] = (jnp.where(first, carry[b, 0], pltpu.roll(xr, 1, 0)),
                           jnp.where(first, carry[b, 1], pltpu.roll(xi, 1, 0)))
                carry[b, 0] = jnp.broadcast_to(xr[7:8], (8, W))
                carry[b, 1] = jnp.broadcast_to(xi[7:8], (8, W))

            def carried(j):
                def emit():
                    rows = slice(j * 8, j * 8 + 8)
                    cr, ci = _cmul(ar, ai, *st["c"])
                    s_ref[rows, re] = s_ref[rows, re] + cr
                    s_ref[rows, im] = s_ref[rows, im] + ci
                    st["c"] = (cr, ci)
                return emit

            return [own(j) for j in range(CH)] + [ends] + [carried(j) for j in range(CH)]

        def readout(b):
            cols = slice(b * KB, (b + 1) * KB)
            acc = {}

            def chunk(c):
                def emit():
                    part = jnp.dot(s_ref[:, b * W2 + c:b * W2 + c + CK].astype(BF), cw[b, c:c + CK, :],
                                   preferred_element_type=F32)
                    acc["y"] = part if c == 0 else acc["y"] + part
                return emit

            def finish():
                y = acc["y"] + d_ref[:, cols] * hp[:, cols]
                y_ref[:, cols] = y
                z_ref[:, cols] = jnp.dot(unperm[...], _gelu(y).astype(BF),
                                         preferred_element_type=F32).astype(z_ref.dtype)

            return [chunk(c) for c in range(0, W2, CK)] + [finish]

        for emit in project(0):
            emit()
        for b in range(NB):
            side = (project(b + 1) if b + 1 < NB else []) + (readout(b - 1) if b > 0 else [])
            _interleave(scan(b), side)
        for emit in readout(NB - 1):
            emit()

    blk = lambda kb, t: (t, kb)
    per_kb = lambda kb, t: (kb, 0, 0)
    return pl.pallas_call(
        body, name=name, grid=(nkb // NB, L // TL),
        in_specs=[pl.BlockSpec((TL, NB * KB), blk), pl.BlockSpec((NB, KB, P2), per_kb),
                  pl.BlockSpec((NB, P2, KB), per_kb), pl.BlockSpec((NB, 40, W2), per_kb),
                  pl.BlockSpec((1, NB * KB), lambda kb, t: (0, kb))],
        out_specs=[pl.BlockSpec((TL, NB * W2), blk), pl.BlockSpec((TL, NB * KB), blk),
                   pl.BlockSpec((TL, NB * KB), blk)],
        out_shape=[jax.ShapeDtypeStruct((L, nkb * W2), F32), jax.ShapeDtypeStruct((L, D), F32),
                   jax.ShapeDtypeStruct((L, D), BF)],
        scratch_shapes=[pltpu.VMEM((NB, KB, W2), BF), pltpu.VMEM((NB, W2, KB), BF), pltpu.VMEM((TL, TL), BF),
                        pltpu.VMEM((TL, TL), BF), pltpu.VMEM((NB, 2, 8, W), F32)],
        compiler_params=_cp("parallel", "arbitrary"))(h, tb, tct, pw, dvec)


def _s5_fwd_one_block(h, tb, tct, pw, dvec, name):
    L, D = h.shape
    nkb, KB, P2 = tb.shape
    P = P2 // 2
    W = (KB // SSM_GROUP) * P
    W2 = 2 * W
    dims = (KB, W2, SSM_GROUP, P, W)
    TL = _tile(L, (512, 256))
    CH = TL // 8

    def body(h_ref, tb_ref, tct_ref, pw_ref, d_ref, s_ref, y_ref, z_ref, bw, cw, perm, unperm, carry):
        t = pl.program_id(1)

        @pl.when(t == 0)
        def _():
            carry[...] = jnp.zeros_like(carry)
            bw[...] = _expand(tb_ref[...], dims, False)
            cw[...] = _expand(tct_ref[...], dims, True)
            perm[...] = _chunk_order(TL, CH, False).astype(perm.dtype)
            unperm[...] = _chunk_order(TL, CH, True).astype(perm.dtype)

        hp = _reorder(perm[...], h_ref[...])
        s_ref[...] = jnp.dot(hp.astype(BF), bw[...], preferred_element_type=F32)
        ar, ai = pw_ref[0:8, :W], pw_ref[0:8, W:]

        def own(j, x):
            rows = pl.ds(pl.multiple_of(j * 8, 8), 8)
            xr, xi = _cmul_add(s_ref[rows, :W], s_ref[rows, W:], ar, ai, x[0], x[1])
            s_ref[rows, :W] = xr
            s_ref[rows, W:] = xi
            return xr, xi

        zero = jnp.zeros((8, W), F32)
        xr, xi = lax.fori_loop(0, CH, own, (zero, zero))
        for k, off in ((1, 8), (2, 16), (4, 24)):
            xr, xi = _cmul_add(xr, xi, pw_ref[off:off + 8, :W], pw_ref[off:off + 8, W:],
                               pltpu.roll(xr, k, 0), pltpu.roll(xi, k, 0))
        xr, xi = _cmul_add(xr, xi, pw_ref[32:40, :W], pw_ref[32:40, W:], carry[0], carry[1])
        first = lax.broadcasted_iota(jnp.int32, (8, W), 0) == 0
        cr = jnp.where(first, carry[0], pltpu.roll(xr, 1, 0))
        ci = jnp.where(first, carry[1], pltpu.roll(xi, 1, 0))
        carry[0] = jnp.broadcast_to(xr[7:8], (8, W))
        carry[1] = jnp.broadcast_to(xi[7:8], (8, W))

        def carried(j, c):
            rows = pl.ds(pl.multiple_of(j * 8, 8), 8)
            cr, ci = _cmul(ar, ai, c[0], c[1])
            s_ref[rows, :W] = s_ref[rows, :W] + cr
            s_ref[rows, W:] = s_ref[rows, W:] + ci
            return cr, ci

        lax.fori_loop(0, CH, carried, (cr, ci))
        y = jnp.dot(s_ref[...].astype(BF), cw[...], preferred_element_type=F32) + d_ref[...] * hp
        y_ref[...] = y
        z_ref[...] = jnp.dot(unperm[...], _gelu(y).astype(BF), preferred_element_type=F32).astype(z_ref.dtype)

    blk = lambda kb, t: (t, kb)
    per_kb = lambda kb, t: (kb, 0, 0)
    return pl.pallas_call(
        body, name=name, grid=(nkb, L // TL),
        in_specs=[pl.BlockSpec((TL, KB), blk), pl.BlockSpec((None, KB, P2), per_kb),
                  pl.BlockSpec((None, P2, KB), per_kb), pl.BlockSpec((None, 40, W2), per_kb),
                  pl.BlockSpec((1, KB), lambda kb, t: (0, kb))],
        out_specs=[pl.BlockSpec((TL, W2), blk), pl.BlockSpec((TL, KB), blk), pl.BlockSpec((TL, KB), blk)],
        out_shape=[jax.ShapeDtypeStruct((L, nkb * W2), F32), jax.ShapeDtypeStruct((L, D), F32),
                   jax.ShapeDtypeStruct((L, D), BF)],
        scratch_shapes=[pltpu.VMEM((KB, W2), BF), pltpu.VMEM((W2, KB), BF), pltpu.VMEM((TL, TL), BF),
                        pltpu.VMEM((TL, TL), BF), pltpu.VMEM((2, 8, W), F32)],
        compiler_params=_cp("parallel", "arbitrary"))(h, tb, tct, pw, dvec)


def s5_bwd(dz, y, h, s, tc, tbt, pwr, dvec, name):
    L, D = h.shape
    nkb, KB, P2 = tc.shape
    P = P2 // 2
    W = (KB // SSM_GROUP) * P
    W2 = 2 * W
    dims = (KB, W2, SSM_GROUP, P, W)
    TL = _tile(L, (512, 256))
    CH = TL // 8
    nt = L // TL
    NB = 2 if nkb % 2 == 0 else 1
    CK = min(S5_CHUNK, W2)
    tn = (((0,), (0,)), ((), ()))

    def body(dz_ref, y_ref, h_ref, s_ref, sp_ref, tc_ref, tbt_ref, pw_ref, d_ref,
             dh_ref, dd_ref, da_ref, db_ref, dc_ref, g, ctw, btw, dbacc, dcacc, dys, perm, unperm, carry):
        t = pl.program_id(1)

        @pl.when(t == 0)
        def _():
            carry[...] = jnp.zeros_like(carry)
            dd_ref[...] = jnp.zeros_like(dd_ref)
            da_ref[...] = jnp.zeros_like(da_ref)
            dbacc[...] = jnp.zeros_like(dbacc)
            dcacc[...] = jnp.zeros_like(dcacc)
            for b in range(NB):
                ctw[b] = _expand(tc_ref[b], dims, False)
                btw[b] = _expand(tbt_ref[b], dims, True)
            perm[...] = _chunk_order(TL, CH, False).astype(perm.dtype)
            unperm[...] = _chunk_order(TL, CH, True).astype(perm.dtype)

        hp = jnp.dot(perm[...], h_ref[...].astype(BF), preferred_element_type=F32)
        dy = jnp.dot(perm[...], dz_ref[...].astype(BF), preferred_element_type=F32) * _gelu_grad(y_ref[...])
        dd_ref[...] += _rowsum8(dy * hp)
        dys[...] = dy
        dyb = dy.astype(BF)
        hpb = hp.astype(BF)
        sub = lax.broadcasted_iota(jnp.int32, (8, W), 0)
        live = jnp.where(t == nt - 1, 0.0, 1.0)

        def lead(b):
            cols = slice(b * KB, (b + 1) * KB)

            def to_states(c):
                def emit():
                    g[b, :, c:c + CK] = jnp.dot(dyb[:, cols], ctw[b, :, c:c + CK], preferred_element_type=F32)
                return emit

            def d_c(c):
                def emit():
                    dcacc[b, :, c:c + CK] += lax.dot_general(dyb[:, cols],
                                                             s_ref[:, b * W2 + c:b * W2 + c + CK].astype(BF), tn,
                                                             preferred_element_type=F32)
                return emit

            return [f(c) for c in range(0, W2, CK) for f in (to_states, d_c)]

        def scan(b):
            re, im = slice(b * W2, b * W2 + W), slice(b * W2 + W, (b + 1) * W2)
            ar, ai = pw_ref[b, 0:8, :W], pw_ref[b, 0:8, W:]
            zero = jnp.zeros((8, W), F32)
            st = {"g": (zero, zero), "acc": (zero, zero)}

            def own(j):
                def emit():
                    rows = slice(j * 8, j * 8 + 8)
                    gr, gi = _cmul_add(g[b, rows, :W], g[b, rows, W:], ar, ai, *st["g"])
                    g[b, rows, :W] = gr
                    g[b, rows, W:] = gi
                    st["g"] = (gr, gi)
                return emit

            def ends():
                gr, gi = st["g"]
                for k, off in ((1, 8), (2, 16), (4, 24)):
                    gr, gi = _cmul_add(gr, gi, pw_ref[b, off:off + 8, :W], pw_ref[b, off:off + 8, W:],
                                       pltpu.roll(gr, 8 - k, 0), pltpu.roll(gi, 8 - k, 0))
                gr, gi = _cmul_add(gr, gi, pw_ref[b, 32:40, :W], pw_ref[b, 32:40, W:], carry[b, 0], carry[b, 1])
                st["c"] = (jnp.where(sub == 7, carry[b, 0], pltpu.roll(gr, 7, 0)),
                           jnp.where(sub == 7, carry[b, 1], pltpu.roll(gi, 7, 0)))
                carry[b, 0] = jnp.broadcast_to(gr[0:1], (8, W))
                carry[b, 1] = jnp.broadcast_to(gi[0:1], (8, W))

            def carried(j):
                def emit():
                    rows = slice(j * 8, j * 8 + 8)
                    cr, ci = _cmul(ar, ai, *st["c"])
                    gr, gi = g[b, rows, :W] + cr, g[b, rows, W:] + ci
                    g[b, rows, :W] = gr
                    g[b, rows, W:] = gi
                    if j > 0:
                        before = slice(j * 8 - 8, j * 8)
                        pr, pi = s_ref[before, re], s_ref[before, im]
                    else:
                        last = slice(TL - 8, TL)
                        pr = jnp.where(sub == 0, sp_ref[7:8, re] * live, pltpu.roll(s_ref[last, re], 1, 0))
                        pi = jnp.where(sub == 0, sp_ref[7:8, im] * live, pltpu.roll(s_ref[last, im], 1, 0))
                    accr, acci = st["acc"]
                    st["c"] = (cr, ci)
                    st["acc"] = (accr + pr * gr + pi * gi, acci + pr * gi - pi * gr)
                return emit

            def done():
                da_ref[b, :, :W] += st["acc"][0]
                da_ref[b, :, W:] += st["acc"][1]

            return ([own(j) for j in reversed(range(CH))] + [ends] + [carried(j) for j in reversed(range(CH))]
                    + [done])

        def tail(b):
            cols = slice(b * KB, (b + 1) * KB)
            acc = {}

            def d_u(c):
                def emit():
                    part = jnp.dot(g[b, :, c:c + CK].astype(BF), btw[b, c:c + CK, :], preferred_element_type=F32)
                    acc["u"] = part if c == 0 else acc["u"] + part
                return emit

            def d_b(c):
                def emit():
                    dbacc[b, :, c:c + CK] += lax.dot_general(hpb[:, cols], g[b, :, c:c + CK].astype(BF), tn,
                                                             preferred_element_type=F32)
                return emit

            def finish():
                dh_ref[:, cols] = _reorder(unperm[...], dys[:, cols] * d_ref[:, cols] + acc["u"])

            return [f(c) for c in range(0, W2, CK) for f in (d_u, d_b)] + [finish]

        for emit in lead(0):
            emit()
        for b in range(NB):
            side = (lead(b + 1) if b + 1 < NB else []) + (tail(b - 1) if b > 0 else [])
            _interleave(scan(b), side)
        for emit in tail(NB - 1):
            emit()

        @pl.when(t == nt - 1)
        def _():
            for b in range(NB):
                db_ref[b] = _extract(dbacc[b], dims)
                dc_ref[b] = _extract(dcacc[b], dims)

    rev = lambda kb, t: (nt - 1 - t, kb)
    prev = lambda kb, t: (jnp.maximum((nt - 1 - t) * CH - 1, 0), kb)
    per_kb = lambda kb, t: (kb, 0, 0)
    return pl.pallas_call(
        body, name=name, grid=(nkb // NB, nt),
        in_specs=[pl.BlockSpec((TL, NB * KB), rev), pl.BlockSpec((TL, NB * KB), rev),
                  pl.BlockSpec((TL, NB * KB), rev), pl.BlockSpec((TL, NB * W2), rev, pipeline_mode=pl.Buffered(1)),
                  pl.BlockSpec((8, NB * W2), prev), pl.BlockSpec((NB, KB, P2), per_kb),
                  pl.BlockSpec((NB, P2, KB), per_kb), pl.BlockSpec((NB, 40, W2), per_kb),
                  pl.BlockSpec((1, NB * KB), lambda kb, t: (0, kb))],
        out_specs=[pl.BlockSpec((TL, NB * KB), rev), pl.BlockSpec((8, NB * KB), lambda kb, t: (0, kb)),
                   pl.BlockSpec((NB, 8, W2), per_kb), pl.BlockSpec((NB, KB, P2), per_kb),
                   pl.BlockSpec((NB, KB, P2), per_kb)],
        out_shape=[jax.ShapeDtypeStruct((L, D), F32), jax.ShapeDtypeStruct((8, D), F32),
                   jax.ShapeDtypeStruct((nkb, 8, W2), F32), jax.ShapeDtypeStruct((nkb, KB, P2), F32),
                   jax.ShapeDtypeStruct((nkb, KB, P2), F32)],
        scratch_shapes=[pltpu.VMEM((NB, TL, W2), F32), pltpu.VMEM((NB, KB, W2), BF), pltpu.VMEM((NB, W2, KB), BF),
                        pltpu.VMEM((NB, KB, W2), F32), pltpu.VMEM((NB, KB, W2), F32), pltpu.VMEM((TL, NB * KB), F32),
                        pltpu.VMEM((TL, TL), BF), pltpu.VMEM((TL, TL), BF), pltpu.VMEM((NB, 2, 8, W), F32)],
        compiler_params=_cp("parallel", "arbitrary"))(dz, y, h, s, s, tc, tbt, pwr, dvec)


def _s5_bwd_one_block(dz, y, h, s, tc, tbt, pwr, dvec, name):
    L, D = h.shape
    nkb, KB, P2 = tc.shape
    P = P2 // 2
    W = (KB // SSM_GROUP) * P
    W2 = 2 * W
    dims = (KB, W2, SSM_GROUP, P, W)
    TL = _tile(L, (512, 256))
    CH = TL // 8
    nt = L // TL

    def body(dz_ref, y_ref, h_ref, s_ref, sp_ref, tc_ref, tbt_ref, pw_ref, d_ref,
             dh_ref, dd_ref, da_ref, db_ref, dc_ref, g, ctw, btw, dbacc, dcacc, perm, unperm, carry):
        t = pl.program_id(1)

        @pl.when(t == 0)
        def _():
            carry[...] = jnp.zeros_like(carry)
            dd_ref[...] = jnp.zeros_like(dd_ref)
            da_ref[...] = jnp.zeros_like(da_ref)
            dbacc[...] = jnp.zeros_like(dbacc)
            dcacc[...] = jnp.zeros_like(dcacc)
            ctw[...] = _expand(tc_ref[...], dims, False)
            btw[...] = _expand(tbt_ref[...], dims, True)
            perm[...] = _chunk_order(TL, CH, False).astype(perm.dtype)
            unperm[...] = _chunk_order(TL, CH, True).astype(perm.dtype)

        hp = jnp.dot(perm[...], h_ref[...].astype(BF), preferred_element_type=F32)
        dy = jnp.dot(perm[...], dz_ref[...].astype(BF), preferred_element_type=F32) * _gelu_grad(y_ref[...])
        dd_ref[...] += _rowsum8(dy * hp)
        dyb = dy.astype(BF)
        g[...] = jnp.dot(dyb, ctw[...], preferred_element_type=F32)
        ar, ai = pw_ref[0:8, :W], pw_ref[0:8, W:]

        def own(jj, x):
            rows = pl.ds(pl.multiple_of((CH - 1 - jj) * 8, 8), 8)
            gr, gi = _cmul_add(g[rows, :W], g[rows, W:], ar, ai, x[0], x[1])
            g[rows, :W] = gr
            g[rows, W:] = gi
            return gr, gi

        zero = jnp.zeros((8, W), F32)
        gr, gi = lax.fori_loop(0, CH, own, (zero, zero))
        for k, off in ((1, 8), (2, 16), (4, 24)):
            gr, gi = _cmul_add(gr, gi, pw_ref[off:off + 8, :W], pw_ref[off:off + 8, W:],
                               pltpu.roll(gr, 8 - k, 0), pltpu.roll(gi, 8 - k, 0))
        gr, gi = _cmul_add(gr, gi, pw_ref[32:40, :W], pw_ref[32:40, W:], carry[0], carry[1])
        sub = lax.broadcasted_iota(jnp.int32, (8, W), 0)
        cr = jnp.where(sub == 7, carry[0], pltpu.roll(gr, 7, 0))
        ci = jnp.where(sub == 7, carry[1], pltpu.roll(gi, 7, 0))
        carry[0] = jnp.broadcast_to(gr[0:1], (8, W))
        carry[1] = jnp.broadcast_to(gi[0:1], (8, W))

        def carried(jj, c):
            j = CH - 1 - jj
            rows = pl.ds(pl.multiple_of(j * 8, 8), 8)
            before = pl.ds(pl.multiple_of(j * 8 - 8, 8), 8)
            cr, ci = _cmul(ar, ai, c[0], c[1])
            gr, gi = g[rows, :W] + cr, g[rows, W:] + ci
            g[rows, :W] = gr
            g[rows, W:] = gi
            pr, pi = s_ref[before, :W], s_ref[before, W:]
            return cr, ci, c[2] + pr * gr + pi * gi, c[3] + pr * gi - pi * gr

        cr, ci, accr, acci = lax.fori_loop(0, CH - 1, carried, (cr, ci, zero, zero))
        live = jnp.where(t == nt - 1, 0.0, 1.0)
        cr, ci = _cmul(ar, ai, cr, ci)
        gr, gi = g[0:8, :W] + cr, g[0:8, W:] + ci
        g[0:8, :W] = gr
        g[0:8, W:] = gi
        pr = jnp.where(sub == 0, sp_ref[7:8, :W] * live, pltpu.roll(s_ref[TL - 8:TL, :W], 1, 0))
        pi = jnp.where(sub == 0, sp_ref[7:8, W:] * live, pltpu.roll(s_ref[TL - 8:TL, W:], 1, 0))
        da_ref[:, :W] += accr + pr * gr + pi * gi
        da_ref[:, W:] += acci + pr * gi - pi * gr

        gb = g[...].astype(BF)
        dh = dy * d_ref[...] + jnp.dot(gb, btw[...], preferred_element_type=F32)
        dh_ref[...] = _reorder(unperm[...], dh)
        tn = (((0,), (0,)), ((), ()))
        dbacc[...] += lax.dot_general(hp.astype(BF), gb, tn, preferred_element_type=F32)
        dcacc[...] += lax.dot_general(dyb, s_ref[...].astype(BF), tn, preferred_element_type=F32)

        @pl.when(t == nt - 1)
        def _():
            db_ref[...] = _extract(dbacc[...], dims)
            dc_ref[...] = _extract(dcacc[...], dims)

    rev = lambda kb, t: (nt - 1 - t, kb)
    prev = lambda kb, t: (jnp.maximum((nt - 1 - t) * CH - 1, 0), kb)
    per_kb = lambda kb, t: (kb, 0, 0)
    return pl.pallas_call(
        body, name=name, grid=(nkb, nt),
        in_specs=[pl.BlockSpec((TL, KB), rev), pl.BlockSpec((TL, KB), rev), pl.BlockSpec((TL, KB), rev),
                  pl.BlockSpec((TL, W2), rev), pl.BlockSpec((8, W2), prev),
                  pl.BlockSpec((None, KB, P2), per_kb), pl.BlockSpec((None, P2, KB), per_kb),
                  pl.BlockSpec((None, 40, W2), per_kb), pl.BlockSpec((1, KB), lambda kb, t: (0, kb))],
        out_specs=[pl.BlockSpec((TL, KB), rev), pl.BlockSpec((8, KB), lambda kb, t: (0, kb)),
                   pl.BlockSpec((None, 8, W2), per_kb), pl.BlockSpec((None, KB, P2), per_kb),
                   pl.BlockSpec((None, KB, P2), per_kb)],
        out_shape=[jax.ShapeDtypeStruct((L, D), F32), jax.ShapeDtypeStruct((8, D), F32),
                   jax.ShapeDtypeStruct((nkb, 8, W2), F32), jax.ShapeDtypeStruct((nkb, KB, P2), F32),
                   jax.ShapeDtypeStruct((nkb, KB, P2), F32)],
        scratch_shapes=[pltpu.VMEM((TL, W2), F32), pltpu.VMEM((KB, W2), BF), pltpu.VMEM((W2, KB), BF),
                        pltpu.VMEM((KB, W2), F32), pltpu.VMEM((KB, W2), F32), pltpu.VMEM((TL, TL), BF),
                        pltpu.VMEM((TL, TL), BF), pltpu.VMEM((2, 8, W), F32)],
        compiler_params=_cp("parallel", "arbitrary"))(dz, y, h, s, s, tc, tbt, pwr, dvec)


def _discretise(a_re, a_im, log_step, b_re, b_im):
    lr = jnp.minimum(a_re, -1e-4)
    li = a_im
    dt = jnp.exp(log_step)[:, None]
    mag = jnp.exp(lr * dt)
    abr = mag * jnp.cos(li * dt)
    abi = mag * jnp.sin(li * dt)
    den = lr * lr + li * li
    qr = ((abr - 1.0) * lr + abi * li) / den
    qi = (abi * lr - (abr - 1.0) * li) / den
    bbar_re = qr[..., None] * b_re - qi[..., None] * b_im
    bbar_im = qr[..., None] * b_im + qi[..., None] * b_re
    return abr, abi, bbar_re, bbar_im


def _compact(m_re, m_im, nkb):
    G, H, P = m_re.shape
    t = jnp.stack([m_re, m_im], axis=2).reshape(nkb, (G // nkb) * H, 2 * P).astype(BF)
    return t, jnp.swapaxes(t, 1, 2)


def _scan_powers(abr, abi, nkb, conj, CH):
    G, P = abr.shape
    if conj:
        abi = -abi

    def cmul(u, v):
        return u[0] * v[0] - u[1] * v[1], u[0] * v[1] + u[1] * v[0]

    q = (abr, abi)
    for _ in range(_log2(CH)):
        q = cmul(q, q)
    pows = [q]
    for _ in range(7):
        pows.append(cmul(pows[-1], q))
    row = jnp.arange(8)[:, None, None]

    def table(part):
        out = [jnp.broadcast_to((abr, abi)[part][None], (8, G, P))]
        for k in (1, 2, 4):
            keep = (row <= 7 - k) if conj else (row >= k)
            out.append(jnp.where(keep, pows[k - 1][part][None], 0.0))
        ends = jnp.stack([p[part] for p in pows])
        out.append(ends[::-1] if conj else ends)
        return jnp.concatenate(out, axis=0)

    GL = G // nkb
    t = jnp.stack([table(0), table(1)], axis=1)
    t = t.reshape(40, 2, nkb, GL * P).transpose(2, 0, 1, 3)
    return t.reshape(nkb, 40, 2 * GL * P)


def ada_mods(c_all, w_ada, b_sh, name):
    nl, D, NA = w_ada.shape

    def body(c_ref, w_ref, b_ref, o_ref):
        cv = c_ref[...]
        act = cv * jax.nn.sigmoid(cv)
        o_ref[...] = jnp.dot(act, w_ref[...], preferred_element_type=F32, precision=lax.Precision.HIGHEST) + b_ref[...]

    return pl.pallas_call(
        body, name=name, grid=(nl,),
        in_specs=[pl.BlockSpec((8, D), lambda i: (0, 0)), pl.BlockSpec((None, D, NA), lambda i: (i, 0, 0)),
                  pl.BlockSpec((None, 1, NA), lambda i: (i, 0, 0))],
        out_specs=pl.BlockSpec((None, 8, NA), lambda i: (i, 0, 0)),
        out_shape=jax.ShapeDtypeStruct((nl, 8, NA), F32), compiler_params=_cp("parallel"))(c_all, w_ada, b_sh)


def _adamw(w, g, m, v):
    m = ADAM_B1 * m + (1.0 - ADAM_B1) * g
    v = ADAM_B2 * v + (1.0 - ADAM_B2) * (g * g)
    m_hat = m / (1.0 - ADAM_B1 ** ADAM_STEP)
    v_hat = v / (1.0 - ADAM_B2 ** ADAM_STEP)
    return -ADAM_LR * (m_hat / (jnp.sqrt(v_hat) + ADAM_EPS) + ADAM_WD * w), m, v


def _adam_rows(R, C):
    cap = max(8, (256 * 1024) // C)
    for t in range(min(R, cap), 0, -1):
        if R % t == 0 and (t % 8 == 0 or t == R):
            return t
    return R


def adamw_ada(c_t, dm, w, m, v, name):
    nl, D, NA = w.shape
    TK = _tile(D, (128,))

    def body(c_ref, dm_ref, w_ref, m_ref, v_ref, g_ref, d_ref, nm_ref, nv_ref):
        cv = c_ref[...]
        act = cv * jax.nn.sigmoid(cv)
        g = act[:, 0:1] * dm_ref[0:1, :]
        for b in range(1, 8):
            g = g + act[:, b:b + 1] * dm_ref[b:b + 1, :]
        g_ref[...] = g
        d_ref[...], nm_ref[...], nv_ref[...] = _adamw(w_ref[...], g, m_ref[...], v_ref[...])

    big = pl.BlockSpec((None, TK, NA), lambda i, k: (i, k, 0))
    shape = jax.ShapeDtypeStruct(w.shape, F32)
    return pl.pallas_call(
        body, name=name, grid=(nl, D // TK),
        in_specs=[pl.BlockSpec((TK, 8), lambda i, k: (k, 0)), pl.BlockSpec((None, 8, NA), lambda i, k: (i, 0, 0)),
                  big, big, big],
        out_specs=[big] * 4, out_shape=[shape] * 4, compiler_params=_cp("parallel", "parallel"))(c_t, dm, w, m, v)


def adamw_sharded(w, m, v, ga, gb, name):
    nl, R, C = w.shape
    TR = _adam_rows(R, C)

    def body(w_ref, m_ref, v_ref, a_ref, b_ref, g_ref, d_ref, nm_ref, nv_ref):
        g = a_ref[...] + b_ref[...]
        g_ref[...] = g
        d_ref[...], nm_ref[...], nv_ref[...] = _adamw(w_ref[...], g, m_ref[...], v_ref[...])

    big = pl.BlockSpec((None, TR, C), lambda i, r: (i, r, 0))
    shape = jax.ShapeDtypeStruct(w.shape, F32)
    return pl.pallas_call(
        body, name=name, grid=(nl, R // TR), in_specs=[big] * 5,
        out_specs=[big] * 4, out_shape=[shape] * 4, compiler_params=_cp("parallel", "parallel"))(w, m, v, ga, gb)


def adamw_slab(g, w, m, v, name):
    R, C = g.shape
    TR = _tile(R, (160, 80, 40, 8))

    def body(g_ref, w_ref, m_ref, v_ref, d_ref, nm_ref, nv_ref):
        d_ref[...], nm_ref[...], nv_ref[...] = _adamw(w_ref[...], g_ref[...], m_ref[...], v_ref[...])

    big = pl.BlockSpec((TR, C), lambda r: (r, 0))
    shape = jax.ShapeDtypeStruct((R, C), F32)
    return pl.pallas_call(
        body, name=name, grid=(R // TR,), in_specs=[big] * 4,
        out_specs=[big] * 3, out_shape=[shape] * 3, compiler_params=_cp("parallel"))(g, w, m, v)


def adamw_plain(w, m, v, g, name):
    def body(w_ref, m_ref, v_ref, g_ref, d_ref, nm_ref, nv_ref):
        d_ref[...], nm_ref[...], nv_ref[...] = _adamw(w_ref[...], g_ref[...], m_ref[...], v_ref[...])

    shape = jax.ShapeDtypeStruct(w.shape, F32)
    return pl.pallas_call(body, name=name, out_shape=[shape] * 3,
                          compiler_params=pltpu.CompilerParams(vmem_limit_bytes=VMEM_LIMIT))(w, m, v, g)


def _slab_rows(a):
    n = a.size
    rows = -(-n // SLAB_W)
    return -(-rows // 8) * 8


def _pack(arrs, pad_rows_to=0):
    out = []
    for a in arrs:
        rows = _slab_rows(a)
        flat = a.reshape(-1).astype(F32)
        flat = jnp.pad(flat, (0, rows * SLAB_W - flat.shape[0]))
        out.append(flat.reshape(rows, SLAB_W))
    total = sum(o.shape[0] for o in out)
    if pad_rows_to and total % pad_rows_to:
        out.append(jnp.zeros((pad_rows_to - total % pad_rows_to, SLAB_W), F32))
    return jnp.concatenate(out, axis=0)


def _unpack(slab, like):
    out, r = [], 0
    for a in like:
        rows = _slab_rows(a)
        out.append(slab[r:r + rows].reshape(-1)[:a.size].reshape(a.shape))
        r += rows
    return out


WEIGHTS = ['norm1_g', 'norm2_g', 'w_ada', 'b_ada', 'ssm_a_re', 'ssm_a_im', 'ssm_log_step', 'ssm_b_re', 'ssm_b_im',
           'ssm_c_re', 'ssm_c_im', 'ssm_d', 'ssm_w_out', 'conv_w_in', 'conv_w', 'conv_w_out', 'w_ffn_in',
           'w_ffn_out', 'final_g']
SLAB = ['norm1_g', 'norm2_g', 'b_ada', 'ssm_a_re', 'ssm_a_im', 'ssm_log_step', 'ssm_b_re', 'ssm_b_im', 'ssm_c_re',
        'ssm_c_im', 'ssm_d', 'final_g']
SHARDED = ['ssm_w_out', 'conv_w_in', 'conv_w_out', 'w_ffn_in', 'w_ffn_out']


def kernel(x, c, norm1_g, norm2_g, w_ada, b_ada, ssm_a_re, ssm_a_im, ssm_log_step, ssm_b_re, ssm_b_im, ssm_c_re, ssm_c_im, ssm_d, ssm_w_out, conv_w_in, conv_w, conv_w_out, w_ffn_in, w_ffn_out, final_g, loss_target, m_norm1_g, m_norm2_g, m_w_ada, m_b_ada, m_ssm_a_re, m_ssm_a_im, m_ssm_log_step, m_ssm_b_re, m_ssm_b_im, m_ssm_c_re, m_ssm_c_im, m_ssm_d, m_ssm_w_out, m_conv_w_in, m_conv_w, m_conv_w_out, m_w_ffn_in, m_w_ffn_out, m_final_g, v_norm1_g, v_norm2_g, v_w_ada, v_b_ada, v_ssm_a_re, v_ssm_a_im, v_ssm_log_step, v_ssm_b_re, v_ssm_b_im, v_ssm_c_re, v_ssm_c_im, v_ssm_d, v_ssm_w_out, v_conv_w_in, v_conv_w, v_conv_w_out, v_w_ffn_in, v_w_ffn_out, v_final_g):
    given = dict(locals())
    W = {n: given[n] for n in WEIGHTS}
    Mo = {n: given["m_" + n] for n in WEIGHTS}
    Vo = {n: given["v_" + n] for n in WEIGHTS}

    xs = x[0]
    tgt = loss_target[0]
    L, D = xs.shape
    nlayer = norm1_g.shape[0]
    NA = w_ada.shape[2]
    G = ssm_a_re.shape[1]
    nkb = D // S5_BLOCK
    ax, ay, ac = _axes()
    me = 4 * ax + 2 * ay + ac
    chip = 2 * ax + ay

    c_all = gather8(jnp.broadcast_to(c, (8, D)), "gather_c")[:, 0, :]
    b_sh = lax.dynamic_slice_in_dim(b_ada, chip * NA, NA, axis=1)[:, None, :]
    mods_part = ada_mods(c_all, w_ada, b_sh, "ada_mods")
    mg = gather8(mods_part.reshape(nlayer * 8, NA), "gather_mods")
    mg = mg.reshape(N_CHIP, 2, nlayer, 8, NA)[:, 0]
    mods_all = lax.dynamic_index_in_dim(mg, me, axis=2, keepdims=False)
    mods_all = jnp.transpose(mods_all, (1, 0, 2)).reshape(nlayer, 6, D)

    cw_parts = gather8(_pack([conv_w]), "gather_conv_w")
    nconv = conv_w.shape[0]
    cw_full = jnp.stack([_unpack(cw_parts[2 * q], [conv_w])[0] for q in range(N_CHIP)], axis=2)
    cw_full = cw_full.reshape(nconv, 3, D)

    use = []
    for i in range(nlayer):
        use += [("ssm_w_out", i // 2, i)] if i % 2 == 0 else [("conv_w_in", i // 2, i), ("conv_w_out", i // 2, i)]
        use += [("w_ffn_in", i, i), ("w_ffn_out", i, i)]
    g_sems, g_srcs, g_lands, token = gather_start([W[n][j].astype(BF) for n, j, _ in use],
                                                  cw_full + mods_all[0, 0:3], "gather_start")
    mods_all = mods_all + token[0:1, 0:1]

    def layer_weights(i, after):
        idx = [a for a, (_, _, li) in enumerate(use) if li == i]
        got = gather_wait(g_sems, g_srcs, g_lands, idx, after, "gather_wait%d" % i)
        return {use[a][0]: w for a, w in zip(idx, got)}

    s5 = []
    for j in range(ssm_a_re.shape[0]):
        disc, disc_vjp = jax.vjp(_discretise, ssm_a_re[j], ssm_a_im[j], ssm_log_step[j], ssm_b_re[j], ssm_b_im[j])
        abr, abi, bbar_re, bbar_im = disc
        tb, tbt = _compact(jnp.swapaxes(bbar_re, 1, 2), jnp.swapaxes(bbar_im, 1, 2), nkb)
        tc, tct = _compact(ssm_c_re[j], -ssm_c_im[j], nkb)
        chunk = _tile(L, (512, 256)) // 8
        s5.append(dict(vjp=disc_vjp, tb=tb, tbt=tbt, tc=tc, tct=tct, pw=_scan_powers(abr, abi, nkb, False, chunk),
                       pwr=_scan_powers(abr, abi, nkb, True, chunk)))

    saved = []
    xcur = xs
    for i in range(nlayer):
        j = i // 2
        mods = mods_all[i]
        sv = dict(x=xcur)
        if i % 2 == 0:
            h = norm_mod(xcur, norm1_g[i:i + 1], mods, 0, F32, "norm_mod_s5")
            states, yv, z = s5_fwd(h, s5[j]["tb"], s5[j]["tct"], s5[j]["pw"], ssm_d[j:j + 1], "s5_fwd")
            full = layer_weights(i, z)
            o = mm_nn(z, full["ssm_w_out"], BF, "mm_ssm_out")
            mix, x2 = glu_res(o, xcur, mods, 2, "glu_res")
            sv.update(h=h, states=states, y=yv, z=z, o=o)
        else:
            h = norm_mod(xcur, norm1_g[i:i + 1], mods, 0, BF, "norm_mod")
            full = layer_weights(i, h)
            p = mm_nn(h, full["conv_w_in"], BF, "mm_conv_in")
            mc = conv_fwd(p, cw_full[j], "conv_fwd")
            mix, x2 = mm_nn(mc, full["conv_w_out"].reshape(1, D, D), BF, "mm_conv_out", res=xcur, gate=mods[2:3])
            sv.update(h=h, p=p, mc=mc)
        h2 = norm_mod(x2, norm2_g[i:i + 1], mods, 3, BF, "norm_mod")
        gu, act = ffn_in_act(h2, full["w_ffn_in"], "ffn_in_act")
        F = act.shape[1]
        ff, x3 = mm_nn(act, full["w_ffn_out"].reshape(1, F, D), BF, "mm_ffn_out", res=x2, gate=mods[5:6])
        sv.update(mix=mix, x2=x2, h2=h2, gu=gu, act=act, ff=ff, w=full)
        saved.append(sv)
        xcur = x3

    loss_blk, dx, dfinal, dff = final_loss(xcur, tgt, final_g[None, :], saved[-1]["ff"], mods_all[nlayer - 1], 5,
                                           "final_loss")
    dg2 = dfinal[1:2]

    gland = {n: lax.empty((W[n].shape[0], N_CHIP) + W[n].shape[1:], BF) for n in SHARDED}
    in_flight = []
    dmods = [None] * nlayer
    dnorm1, dnorm2 = [None] * nlayer, [None] * nlayer
    dconv_w = [None] * nconv
    ds5 = [None] * ssm_a_re.shape[0]
    token = jnp.zeros((8, 128), F32)

    def send_grads(names, grads, slot, after, name):
        sems, thru, lands, tok = scatter_start([grads[n] for n in names], [gland[n] for n in names], slot, after, name)
        gland.update(zip(names, lands))
        in_flight.append((names, slot, sems, thru, name))
        return tok

    def land_grads(group, after):
        for names, slot, sems, thru, name in in_flight:
            if names[0] in group:
                got = scatter_wait(sems, thru, [gland[n] for n in names], slot, after, name.replace("scatter", "landed"))
                gland.update(zip(names, got))

    for i in reversed(range(nlayer)):
        j = i // 2
        mods = mods_all[i] + token[0:1, 0:1]
        sv = saved[i]
        full = sv["w"]
        gfull = {}
        F = sv["act"].shape[1]
        gfull["w_ffn_out"] = mm_tn(sv["act"], dff, 1, "mm_tn_ffn_out").reshape(N_CHIP, F // N_CHIP, D)
        dgu = ffn_out_bwd(dff, full["w_ffn_out"].reshape(F, D), sv["gu"], "ffn_out_bwd")
        gfull["w_ffn_in"] = mm_tn(sv["h2"], dgu, N_CHIP, "mm_tn_ffn_in")
        dh2 = mm_nt(dgu, full["w_ffn_in"], F32, "mm_nt_ffn_in")
        token = send_grads(["w_ffn_out", "w_ffn_in"], gfull, [i, i], dh2, "scatter_ffn%d" % i)
        mods = mods + token[0:1, 0:1]
        dx2, s2, dmix = norm_bwd(dh2, sv["x2"], dx, norm2_g[i:i + 1], mods, 3, "norm_bwd_mix",
                                 branch=(sv["mix"], mods, 2))
        dg1 = s2[3:4]
        if i % 2 == 0:
            do = glu_bwd(dmix, sv["o"], "glu_bwd")
            gfull["ssm_w_out"] = mm_tn(sv["z"], do, N_CHIP, "mm_tn_ssm_out")
            dz = mm_nt(do, full["ssm_w_out"], BF, "mm_nt_ssm_out")
            dh, dd, dab, db, dc = s5_bwd(dz, sv["y"], sv["h"], sv["states"], s5[j]["tc"], s5[j]["tbt"], s5[j]["pwr"],
                                         ssm_d[j:j + 1], "s5_bwd")
            ds5[j] = (dd, dab, db, dc)
        else:
            gfull["conv_w_out"] = mm_tn(sv["mc"], dmix, 1, "mm_tn_conv_out").reshape(N_CHIP, D // N_CHIP, D)
            dmc = mm_nt(dmix, full["conv_w_out"].reshape(1, D, D), BF, "mm_nt_conv_out")
            dbg, dcg, dvv, dcw = conv_bwd(dmc, sv["p"], cw_full[j], "conv_bwd")
            dp = jnp.concatenate([dbg, dcg, dvv], axis=1)
            gfull["conv_w_in"] = mm_tn(sv["h"], dp, N_CHIP, "mm_tn_conv_in")
            dh = mm_nt(dp, full["conv_w_in"], F32, "mm_nt_conv_in")
            dconv_w[j] = dcw[0:3]
        dmods_i = [s2[0:2], dg2]
        if i > 0:
            dx, s1, dff = norm_bwd(dh, sv["x"], dx2, norm1_g[i:i + 1], mods, 0, "norm_bwd_ffn",
                                   branch=(saved[i - 1]["ff"], mods_all[i - 1], 5))
            dg2 = s1[3:4]
        else:
            dx, s1 = norm_bwd(dh, sv["x"], dx2, norm1_g[i:i + 1], mods, 0, "norm_bwd")
        dmods[i] = jnp.concatenate([s1[0:2], dg1] + dmods_i, axis=0).reshape(6 * D)
        dnorm1[i], dnorm2[i] = s1[2], s2[2]
        names = ["ssm_w_out"] if i % 2 == 0 else ["conv_w_out", "conv_w_in"]
        token = send_grads(names, gfull, [j] * len(names), dx, "scatter_mix%d" % i)

    small = dict(norm1_g=jnp.stack(dnorm1), norm2_g=jnp.stack(dnorm2), b_ada=jnp.stack(dmods), final_g=dfinal[0])
    per = {n: [] for n in ('ssm_a_re', 'ssm_a_im', 'ssm_log_step', 'ssm_b_re', 'ssm_b_im', 'ssm_c_re', 'ssm_c_im', 'ssm_d')}
    GL = G // nkb
    for j, (dd, dab, db, dc) in enumerate(ds5):
        dab = jnp.sum(dab, axis=1).reshape(nkb, 2, GL, SSM_STATE)
        g_abr, g_abi = dab[:, 0].reshape(G, SSM_STATE), dab[:, 1].reshape(G, SSM_STATE)
        db, dc = db.reshape(G, SSM_GROUP, 2, SSM_STATE), dc.reshape(G, SSM_GROUP, 2, SSM_STATE)
        gb_re, gb_im, gc_re, gc_im = db[:, :, 0], db[:, :, 1], dc[:, :, 0], dc[:, :, 1]
        ga_re, ga_im, gls, gbr, gbi = s5[j]["vjp"]((g_abr, g_abi, jnp.swapaxes(gb_re, 1, 2), jnp.swapaxes(gb_im, 1, 2)))
        for n, val in zip(per, (ga_re, ga_im, gls, gbr, gbi, gc_re, -gc_im, jnp.sum(dd, axis=0))):
            per[n].append(val)
    small.update({n: jnp.stack(vals) for n, vals in per.items()})
    dcw_full = jnp.stack(dconv_w)

    slab_like = [W[n] for n in SLAB] + [dcw_full]
    rows64 = 8 * N_DEV
    g_slab, dm_all = reduce8(_pack([small[n] for n in SLAB] + [dcw_full], rows64), _pack([small["b_ada"]]),
                             "reduce_small")
    d_slab, m_slab, v_slab = adamw_slab(
        g_slab, _pack([W[n] for n in SLAB] + [jnp.zeros_like(dcw_full)], rows64),
        _pack([Mo[n] for n in SLAB] + [jnp.zeros_like(dcw_full)], rows64),
        _pack([Vo[n] for n in SLAB] + [jnp.ones_like(dcw_full)], rows64), "adamw_slab")
    out = {}
    for k, slab in zip(("g", "d", "m", "v"), (g_slab, d_slab, m_slab, v_slab)):
        for n, val in zip(SLAB, _unpack(slab, slab_like)):
            out[k, n] = val
    g_cw = lax.dynamic_slice_in_dim(_unpack(g_slab, slab_like)[-1], chip * conv_w.shape[2], conv_w.shape[2], axis=2)
    out["g", "conv_w"] = g_cw
    out["d", "conv_w"], out["m", "conv_w"], out["v", "conv_w"] = [
        r.reshape(conv_w.shape) for r in adamw_plain(conv_w.reshape(-1, conv_w.shape[2]), m_conv_w.reshape(-1, conv_w.shape[2]),
                                                     v_conv_w.reshape(-1, conv_w.shape[2]), g_cw.reshape(-1, conv_w.shape[2]),
                                                     "adamw_conv_w")]

    early = [n for n in SHARDED if n != "ssm_w_out"]
    land_grads(early, g_slab)
    mine = [reduce4(gland[n], "reduce4_" + n) for n in early]
    w_sems, w_srcs, w_lands, token = swap_start(mine, "swap_start")

    dm_all = dm_all.reshape(N_DEV, -1)[:, :b_ada.size].reshape(N_DEV, nlayer, N_CHIP, NA)
    dm_sh = jnp.transpose(lax.dynamic_index_in_dim(dm_all, chip, axis=2, keepdims=False), (1, 0, 2))
    res = adamw_ada(jnp.transpose(c_all) + token[0:1, 0:1], dm_sh, w_ada, m_w_ada, v_w_ada, "adamw_ada")
    out["g", "w_ada"], out["d", "w_ada"], out["m", "w_ada"], out["v", "w_ada"] = res

    mine, theirs = swap_wait(w_sems, w_srcs, w_lands, out["g", "w_ada"], "swap_wait")
    for n, ga, gb in zip(early, mine, theirs):
        r = adamw_sharded(W[n], Mo[n], Vo[n], ga, gb, "adamw_" + n)
        out["g", n], out["d", n], out["m", n], out["v", n] = r

    land_grads(["ssm_w_out"], out["g", "w_ffn_out"])
    ga = reduce4(gland["ssm_w_out"], "reduce4_ssm_w_out")
    gb = swap_siblings([ga], "swap_siblings")[0]
    r = adamw_sharded(ssm_w_out, m_ssm_w_out, v_ssm_w_out, ga, gb, "adamw_ssm_w_out")
    out["g", "ssm_w_out"], out["d", "ssm_w_out"], out["m", "ssm_w_out"], out["v", "ssm_w_out"] = r

    loss = lax.psum(loss_blk[0, 0], ("x", "y", "c"))
    return (loss, dx[None], *[out["g", n] for n in WEIGHTS], *[out["d", n] for n in WEIGHTS],
            *[out["m", n] for n in WEIGHTS], *[out["v", n] for n in WEIGHTS])
```

```python
import functools
import math

import jax
import jax.numpy as jnp
from jax import lax
from jax.experimental import pallas as pl
from jax.experimental.pallas import tpu as pltpu

F32 = jnp.float32
BF = jnp.bfloat16
MESH = pl.DeviceIdType.MESH
ANY = pl.BlockSpec(memory_space=pl.ANY)

N_DEV = 8
N_CHIP = 4
DEPTH = 4
SSM_GROUP = 16
SSM_STATE = 64
S5_BLOCK = 256
RMS_EPS = 1e-6
ADAM_LR, ADAM_B1, ADAM_B2, ADAM_EPS, ADAM_WD, ADAM_STEP = 0.001, 0.9, 0.999, 1e-08, 0.01, 10
V7X_VMEM_BYTES = 64 * 1024 * 1024
VMEM_LIMIT = V7X_VMEM_BYTES - 12 * 1024 * 1024
SLAB_W = 1024
GELU_C = math.sqrt(2.0 / math.pi)
GELU_A = 0.044715


def _cp(*sem):
    return pltpu.CompilerParams(dimension_semantics=sem if sem else None, vmem_limit_bytes=VMEM_LIMIT)


def _tile(n, prefs):
    for p in prefs:
        if p <= n and n % p == 0:
            return p
    return n


def _axes():
    return lax.axis_index("x"), lax.axis_index("y"), lax.axis_index("c")


def _flip(v, k):
    return 1 - v if k else v


def gather8(v, name):
    R, C = v.shape

    def body(v_ref, o_ref, ssem, rsem, lsem):
        x, y, c = _axes()
        me = 4 * x + 2 * y + c
        loc = pltpu.make_async_copy(v_ref, o_ref.at[me], lsem)
        loc.start()
        copies = []
        for k in range(1, N_DEV):
            peer = (_flip(x, (k >> 2) & 1), _flip(y, (k >> 1) & 1), _flip(c, k & 1))
            cp = pltpu.make_async_remote_copy(src_ref=v_ref, dst_ref=o_ref.at[me], send_sem=ssem.at[k - 1],
                                              recv_sem=rsem.at[k - 1], device_id=peer, device_id_type=MESH)
            cp.start()
            copies.append(cp)
        for cp in copies:
            cp.wait()
        loc.wait()

    return pl.pallas_call(
        body, name=name,
        out_shape=jax.ShapeDtypeStruct((N_DEV, R, C), v.dtype),
        in_specs=[pl.BlockSpec(memory_space=pltpu.VMEM)],
        out_specs=pl.BlockSpec(memory_space=pltpu.VMEM),
        scratch_shapes=[pltpu.SemaphoreType.DMA((N_DEV - 1,)), pltpu.SemaphoreType.DMA((N_DEV - 1,)),
                        pltpu.SemaphoreType.DMA],
        compiler_params=pltpu.CompilerParams(vmem_limit_bytes=VMEM_LIMIT),
    )(v)


HBM = pl.BlockSpec(memory_space=pltpu.HBM)
SEM = pl.BlockSpec(memory_space=pltpu.SEMAPHORE)
EFFECT = pltpu.SideEffectType.DATAFLOW_SIDE_EFFECTING


def _in_hbm(a):
    return pltpu.with_memory_space_constraint(a, pltpu.HBM)


def _chip_peers(x, y, c):
    out = []
    for k in range(1, N_CHIP):
        px, py = _flip(x, k >> 1), _flip(y, k & 1)
        out.append(((px, py, c), 2 * px + py))
    return out


def gather_start(shards, after, name):
    n = len(shards)

    def body(*refs):
        src, land = refs[:n], refs[n:2 * n]
        ssem, rsem, lsem = refs[2 * n + 1:2 * n + 4]
        token = refs[-1]
        x, y, c = _axes()
        chip = 2 * x + y
        for a in range(n):
            pltpu.make_async_copy(src[a], land[a].at[chip], lsem.at[a]).start()
            for k, (peer, _) in enumerate(_chip_peers(x, y, c)):
                pltpu.make_async_remote_copy(src_ref=src[a], dst_ref=land[a].at[chip], send_sem=ssem.at[3 * a + k],
                                             recv_sem=rsem.at[3 * a + k], device_id=peer, device_id_type=MESH).start()
        token[...] = jnp.zeros_like(token)

    lands = [lax.empty((N_CHIP,) + s.shape, s.dtype) for s in shards]
    out_shape = ([pltpu.SemaphoreType.DMA((3 * n,)), pltpu.SemaphoreType.DMA((3 * n,)), pltpu.SemaphoreType.DMA((n,))]
                 + [pltpu.HBM(s.shape, s.dtype) for s in shards] + [pltpu.HBM(l.shape, l.dtype) for l in lands]
                 + [jax.ShapeDtypeStruct((8, 128), F32)])
    res = pl.pallas_call(
        body, name=name, out_shape=out_shape, in_specs=[HBM] * (2 * n) + [ANY],
        out_specs=[SEM, SEM, SEM] + [HBM] * (2 * n) + [pl.BlockSpec(memory_space=pltpu.VMEM)],
        input_output_aliases={a: 3 + a for a in range(2 * n)},
        compiler_params=pltpu.CompilerParams(has_side_effects=EFFECT),
    )(*[_in_hbm(s) for s in shards], *[_in_hbm(l) for l in lands], after)
    return tuple(res[:3]), list(res[3:3 + n]), list(res[3 + n:3 + 2 * n]), res[-1]


def gather_wait(sems, srcs, lands, idx, after, name):
    m = len(idx)

    def body(*refs):
        src, land = refs[:m], refs[m:2 * m]
        ssem, rsem, lsem = refs[2 * m:2 * m + 3]
        x, y, c = _axes()
        chip = 2 * x + y
        for j, a in enumerate(idx):
            for k, (peer, pchip) in enumerate(_chip_peers(x, y, c)):
                cp = pltpu.make_async_remote_copy(src_ref=src[j], dst_ref=land[j].at[pchip], send_sem=ssem.at[3 * a + k],
                                                  recv_sem=rsem.at[3 * a + k], device_id=peer, device_id_type=MESH)
                cp.wait_send()
                cp.wait_recv()
            pltpu.make_async_copy(src[j], land[j].at[chip], lsem.at[a]).wait()

    s_in = [srcs[a] for a in idx]
    l_in = [lands[a] for a in idx]
    res = pl.pallas_call(
        body, name=name,
        out_shape=[pltpu.HBM(s.shape, s.dtype) for s in s_in] + [pltpu.HBM(l.shape, l.dtype) for l in l_in],
        in_specs=[HBM] * (2 * m) + [SEM, SEM, SEM, ANY], out_specs=[HBM] * (2 * m),
        input_output_aliases={a: a for a in range(2 * m)},
        compiler_params=pltpu.CompilerParams(has_side_effects=EFFECT),
    )(*s_in, *l_in, *sems, after)
    return list(res[m:])


def scatter_start(grads, lands, slot, after, name):
    n = len(grads)

    def body(*refs):
        src, land = refs[:n], refs[n:2 * n]
        ssem, rsem, lsem = refs[2 * n + 1:2 * n + 4]
        token = refs[-1]
        x, y, c = _axes()
        chip = 2 * x + y
        for a in range(n):
            pltpu.make_async_copy(src[a].at[chip], land[a].at[slot[a], chip], lsem.at[a]).start()
            for k, (peer, pchip) in enumerate(_chip_peers(x, y, c)):
                pltpu.make_async_remote_copy(src_ref=src[a].at[pchip], dst_ref=land[a].at[slot[a], chip],
                                             send_sem=ssem.at[3 * a + k], recv_sem=rsem.at[3 * a + k],
                                             device_id=peer, device_id_type=MESH).start()
        token[...] = jnp.zeros_like(token)

    out_shape = ([pltpu.SemaphoreType.DMA((3 * n,)), pltpu.SemaphoreType.DMA((3 * n,)), pltpu.SemaphoreType.DMA((n,))]
                 + [pltpu.HBM(g.shape, g.dtype) for g in grads] + [pltpu.HBM(l.shape, l.dtype) for l in lands]
                 + [jax.ShapeDtypeStruct((8, 128), F32)])
    res = pl.pallas_call(
        body, name=name, out_shape=out_shape, in_specs=[HBM] * (2 * n) + [ANY],
        out_specs=[SEM, SEM, SEM] + [HBM] * (2 * n) + [pl.BlockSpec(memory_space=pltpu.VMEM)],
        input_output_aliases={a: 3 + a for a in range(2 * n)},
        compiler_params=pltpu.CompilerParams(has_side_effects=EFFECT),
    )(*[_in_hbm(g) for g in grads], *[_in_hbm(l) for l in lands], after)
    return tuple(res[:3]), list(res[3:3 + n]), list(res[3 + n:3 + 2 * n]), res[-1]


def scatter_wait(sems, grads, lands, slot, after, name):
    n = len(grads)

    def body(*refs):
        src, land = refs[:n], refs[n:2 * n]
        ssem, rsem, lsem = refs[2 * n:2 * n + 3]
        x, y, c = _axes()
        chip = 2 * x + y
        for a in range(n):
            for k, (peer, pchip) in enumerate(_chip_peers(x, y, c)):
                cp = pltpu.make_async_remote_copy(src_ref=src[a].at[pchip], dst_ref=land[a].at[slot[a], pchip],
                                                  send_sem=ssem.at[3 * a + k], recv_sem=rsem.at[3 * a + k],
                                                  device_id=peer, device_id_type=MESH)
                cp.wait_send()
                cp.wait_recv()
            pltpu.make_async_copy(src[a].at[chip], land[a].at[slot[a], chip], lsem.at[a]).wait()

    res = pl.pallas_call(
        body, name=name,
        out_shape=[pltpu.HBM(g.shape, g.dtype) for g in grads] + [pltpu.HBM(l.shape, l.dtype) for l in lands],
        in_specs=[HBM] * (2 * n) + [SEM, SEM, SEM, ANY], out_specs=[HBM] * (2 * n),
        input_output_aliases={a: a for a in range(2 * n)},
        compiler_params=pltpu.CompilerParams(has_side_effects=EFFECT),
    )(*grads, *lands, *sems, after)
    return list(res[n:])


def reduce4(land, name):
    nl, _, R, C = land.shape
    TR = _adam_rows(R, C)

    def body(l_ref, o_ref):
        o_ref[...] = ((l_ref[0].astype(F32) + l_ref[1].astype(F32)) + l_ref[2].astype(F32)) + l_ref[3].astype(F32)

    return pl.pallas_call(
        body, name=name, grid=(nl, R // TR),
        in_specs=[pl.BlockSpec((None, N_CHIP, TR, C), lambda i, r: (i, 0, r, 0))],
        out_specs=pl.BlockSpec((None, TR, C), lambda i, r: (i, r, 0)),
        out_shape=jax.ShapeDtypeStruct((nl, R, C), F32), compiler_params=_cp("parallel", "parallel"))(land)


def swap_siblings(arrs, name):
    n = len(arrs)

    def body(*refs):
        src, dst = refs[:n], refs[n:2 * n]
        ssem, rsem = refs[2 * n:]
        x, y, c = _axes()
        cps = [pltpu.make_async_remote_copy(src_ref=src[a], dst_ref=dst[a], send_sem=ssem.at[a], recv_sem=rsem.at[a],
                                            device_id=(x, y, 1 - c), device_id_type=MESH) for a in range(n)]
        for cp in cps:
            cp.start()
        for cp in cps:
            cp.wait()

    return pl.pallas_call(
        body, name=name, out_shape=[jax.ShapeDtypeStruct(a.shape, a.dtype) for a in arrs],
        in_specs=[ANY] * n, out_specs=[ANY] * n,
        scratch_shapes=[pltpu.SemaphoreType.DMA((n,)), pltpu.SemaphoreType.DMA((n,))],
        compiler_params=pltpu.CompilerParams(vmem_limit_bytes=VMEM_LIMIT),
    )(*arrs)


def swap_start(arrs, name):
    n = len(arrs)

    def body(*refs):
        src, land = refs[:n], refs[n:2 * n]
        ssem, rsem = refs[2 * n:2 * n + 2]
        token = refs[-1]
        x, y, c = _axes()
        for a in range(n):
            pltpu.make_async_remote_copy(src_ref=src[a], dst_ref=land[a], send_sem=ssem.at[a], recv_sem=rsem.at[a],
                                         device_id=(x, y, 1 - c), device_id_type=MESH).start()
        token[...] = jnp.zeros_like(token)

    lands = [lax.empty(a.shape, a.dtype) for a in arrs]
    out_shape = ([pltpu.SemaphoreType.DMA((n,)), pltpu.SemaphoreType.DMA((n,))]
                 + [pltpu.HBM(a.shape, a.dtype) for a in arrs] * 2 + [jax.ShapeDtypeStruct((8, 128), F32)])
    res = pl.pallas_call(
        body, name=name, out_shape=out_shape, in_specs=[HBM] * (2 * n),
        out_specs=[SEM, SEM] + [HBM] * (2 * n) + [pl.BlockSpec(memory_space=pltpu.VMEM)],
        input_output_aliases={a: 2 + a for a in range(2 * n)},
        compiler_params=pltpu.CompilerParams(has_side_effects=EFFECT),
    )(*[_in_hbm(a) for a in arrs], *[_in_hbm(l) for l in lands])
    return tuple(res[:2]), list(res[2:2 + n]), list(res[2 + n:2 + 2 * n]), res[-1]


def swap_wait(sems, srcs, lands, after, name):
    n = len(srcs)

    def body(*refs):
        src, land = refs[:n], refs[n:2 * n]
        ssem, rsem = refs[2 * n:2 * n + 2]
        x, y, c = _axes()
        for a in range(n):
            cp = pltpu.make_async_remote_copy(src_ref=src[a], dst_ref=land[a], send_sem=ssem.at[a],
                                              recv_sem=rsem.at[a], device_id=(x, y, 1 - c), device_id_type=MESH)
            cp.wait_send()
            cp.wait_recv()

    res = pl.pallas_call(
        body, name=name, out_shape=[pltpu.HBM(a.shape, a.dtype) for a in srcs] * 2,
        in_specs=[HBM] * (2 * n) + [SEM, SEM, ANY], out_specs=[HBM] * (2 * n),
        input_output_aliases={a: a for a in range(2 * n)},
        compiler_params=pltpu.CompilerParams(has_side_effects=EFFECT),
    )(*srcs, *lands, *sems, after)
    return list(res[:n]), list(res[n:])


def reduce8(slab, dm, name):
    RT, C = slab.shape
    P = RT // N_DEV
    R = dm.shape[0]

    def body(s_ref, dm_ref, o_ref, dmo_ref, recv, s1, r1, s2, r2, s3, r3):
        x, y, c = _axes()
        me = 4 * x + 2 * y + c
        mine = pl.ds(pl.multiple_of(me * P, 8), P)
        parts, dms = [], []
        for k in range(1, N_DEV):
            px, py, pc = _flip(x, (k >> 2) & 1), _flip(y, (k >> 1) & 1), _flip(c, k & 1)
            theirs = pl.ds(pl.multiple_of((4 * px + 2 * py + pc) * P, 8), P)
            cp = pltpu.make_async_remote_copy(src_ref=s_ref.at[theirs], dst_ref=recv.at[me], send_sem=s1.at[k - 1],
                                              recv_sem=r1.at[k - 1], device_id=(px, py, pc), device_id_type=MESH)
            cp.start()
            parts.append(cp)
            cd = pltpu.make_async_remote_copy(src_ref=dm_ref, dst_ref=dmo_ref.at[me], send_sem=s3.at[k - 1],
                                              recv_sem=r3.at[k - 1], device_id=(px, py, pc), device_id_type=MESH)
            cd.start()
            dms.append(cd)
        dmo_ref[me] = dm_ref[...]
        recv[me] = s_ref[mine, :]
        for cp in parts:
            cp.wait()
        tot = recv[0]
        for d in range(1, N_DEV):
            tot = tot + recv[d]
        o_ref[mine, :] = tot
        out = []
        for k in range(1, N_DEV):
            peer = (_flip(x, (k >> 2) & 1), _flip(y, (k >> 1) & 1), _flip(c, k & 1))
            cp = pltpu.make_async_remote_copy(src_ref=o_ref.at[mine], dst_ref=o_ref.at[mine], send_sem=s2.at[k - 1],
                                              recv_sem=r2.at[k - 1], device_id=peer, device_id_type=MESH)
            cp.start()
            out.append(cp)
        for cp in out + dms:
            cp.wait()

    sems = [pltpu.SemaphoreType.DMA((N_DEV - 1,))] * 6
    return pl.pallas_call(
        body, name=name,
        out_shape=[jax.ShapeDtypeStruct((RT, C), F32), jax.ShapeDtypeStruct((N_DEV, R, C), F32)],
        in_specs=[pl.BlockSpec(memory_space=pltpu.VMEM)] * 2, out_specs=[pl.BlockSpec(memory_space=pltpu.VMEM)] * 2,
        scratch_shapes=[pltpu.VMEM((N_DEV, P, C), F32)] + sems,
        compiler_params=pltpu.CompilerParams(vmem_limit_bytes=VMEM_LIMIT),
    )(slab, dm)


def mm_nn(a, w, out_dtype, name, res=None, gate=None):
    M, K = a.shape
    S, _, Ns = w.shape
    TM = _tile(M, (1024, 512, 256) if K <= 1024 else (512, 256))
    TN = _tile(Ns, (1408, 1024, 768, 512, 256, 128))
    nj = Ns // TN
    fused = res is not None

    def body(*refs):
        if fused:
            a_ref, w_ref, r_ref, g_ref, f_ref, o_ref = refs
        else:
            a_ref, w_ref, f_ref = refs
        f = jnp.dot(a_ref[...], w_ref[...], preferred_element_type=F32)
        f_ref[...] = f.astype(f_ref.dtype)
        if fused:
            o_ref[...] = r_ref[...] + g_ref[...] * f

    col = lambda s, j, i: (i, s * nj + j)
    in_specs = [pl.BlockSpec((TM, K), lambda s, j, i: (i, 0)), pl.BlockSpec((None, K, TN), lambda s, j, i: (s, 0, j))]
    out_specs = [pl.BlockSpec((TM, TN), col)]
    out_shape = [jax.ShapeDtypeStruct((M, S * Ns), out_dtype)]
    args = [a, w]
    if fused:
        in_specs += [pl.BlockSpec((TM, TN), col), pl.BlockSpec((1, TN), lambda s, j, i: (0, s * nj + j))]
        out_specs.append(pl.BlockSpec((TM, TN), col))
        out_shape.append(jax.ShapeDtypeStruct((M, S * Ns), F32))
        args += [res, gate]
    out = pl.pallas_call(body, name=name, grid=(S, nj, M // TM), in_specs=in_specs, out_specs=out_specs,
                         out_shape=out_shape, compiler_params=_cp("parallel", "parallel", "parallel"))(*args)
    return tuple(out) if fused else out[0]


def mm_nt(g, w, out_dtype, name):
    g3 = g if g.ndim == 3 else g[None]
    Q, M, F = g3.shape
    S, K, Ns = w.shape
    TM = _tile(M, (1024, 512, 256) if K <= 1024 else (512, 256))
    TN = _tile(Ns, (1408, 1024, 768, 512, 256, 128))
    nj = Ns // TN
    nred = S * nj
    per_part = F // TN

    def body(g_ref, w_ref, o_ref, acc):
        n = pl.program_id(1)

        @pl.when(n == 0)
        def _():
            acc[...] = jnp.zeros_like(acc)

        acc[...] += lax.dot_general(g_ref[...], w_ref[...], (((1,), (1,)), ((), ())), preferred_element_type=F32)

        @pl.when(n == nred - 1)
        def _():
            o_ref[...] = acc[...].astype(o_ref.dtype)

    return pl.pallas_call(
        body, name=name, grid=(M // TM, nred),
        in_specs=[pl.BlockSpec((None, TM, TN), lambda i, n: (n // per_part, i, n % per_part)),
                  pl.BlockSpec((None, K, TN), lambda i, n: (n // nj, 0, n % nj))],
        out_specs=pl.BlockSpec((TM, K), lambda i, n: (i, 0)),
        out_shape=jax.ShapeDtypeStruct((M, K), out_dtype),
        scratch_shapes=[pltpu.VMEM((TM, K), F32)],
        compiler_params=_cp("parallel", "arbitrary"))(g3, w)


def mm_tn(a, g, S, name):
    M, K = a.shape
    g3 = g if g.ndim == 3 else g[None]
    Q, _, F = g3.shape
    Ns = Q * F // S
    TK = _tile(K, (256, 128))
    TN = _tile(Ns, (1408, 1024, 768, 512, 256, 128))
    nj = Ns // TN
    per_part = F // TN

    def body(a_ref, g_ref, o_ref):
        o_ref[...] = lax.dot_general(a_ref[...], g_ref[...], (((0,), (0,)), ((), ())),
                                     preferred_element_type=F32).astype(o_ref.dtype)

    return pl.pallas_call(
        body, name=name, grid=(S * nj, K // TK),
        in_specs=[pl.BlockSpec((M, TK), lambda n, k: (0, k)),
                  pl.BlockSpec((None, M, TN), lambda n, k: (n // per_part, 0, n % per_part))],
        out_specs=pl.BlockSpec((None, TK, TN), lambda n, k: (n // nj, k, n % nj)),
        out_shape=jax.ShapeDtypeStruct((S, K, Ns), BF),
        compiler_params=_cp("parallel", "parallel"))(a, g3)


def _rows(TL, D):
    return pl.BlockSpec((TL, D), lambda i: (i, 0))


def _fixed(R, D):
    return pl.BlockSpec((R, D), lambda i: (0, 0))


def _rowsum8(v):
    T, D = v.shape
    return jnp.sum(v.reshape(T // 8, 8, D), axis=0)


def _norm_parts(xv):
    r = lax.rsqrt(jnp.mean(xv * xv, axis=-1, keepdims=True) + RMS_EPS)
    return xv * r, r


def norm_mod(x, gamma, mods, k_shift, out_dtype, name):
    L, D = x.shape
    TL = _tile(L, (512, 256))

    def body(x_ref, g_ref, m_ref, o_ref):
        xn, _ = _norm_parts(x_ref[...])
        sh, sc = m_ref[k_shift:k_shift + 1, :], m_ref[k_shift + 1:k_shift + 2, :]
        o_ref[...] = ((xn * g_ref[...]) * (1.0 + sc) + sh).astype(o_ref.dtype)

    return pl.pallas_call(body, name=name, grid=(L // TL,),
                          in_specs=[_rows(TL, D), _fixed(1, D), _fixed(6, D)], out_specs=_rows(TL, D),
                          out_shape=jax.ShapeDtypeStruct((L, D), out_dtype), compiler_params=_cp("parallel"))(x, gamma, mods)


def norm_bwd(dh, x, dres, gamma, mods, k_shift, name, branch=None):
    L, D = x.shape
    TL = _tile(L, (512, 256))
    nacc = 4 if branch else 3

    def body(*refs):
        if branch:
            dh_ref, x_ref, dr_ref, g_ref, m_ref, f_ref, fm_ref, dx_ref, s_ref, df_ref, acc = refs
        else:
            dh_ref, x_ref, dr_ref, g_ref, m_ref, dx_ref, s_ref, acc = refs
        i = pl.program_id(0)

        @pl.when(i == 0)
        def _():
            acc[...] = jnp.zeros_like(acc)

        xn, r = _norm_parts(x_ref[...])
        dh_v = dh_ref[...].astype(F32)
        gam = g_ref[...]
        sc = m_ref[k_shift + 1:k_shift + 2, :]
        dn = dh_v * (1.0 + sc)
        dxn = dn * gam
        dx = dr_ref[...] + r * (dxn - xn * jnp.mean(dxn * xn, axis=-1, keepdims=True))
        dx_ref[...] = dx
        acc[0] += _rowsum8(dh_v)
        acc[1] += _rowsum8(dh_v * (xn * gam))
        acc[2] += _rowsum8(dn * xn)
        if branch:
            df_ref[...] = (dx * fm_ref[branch[2]:branch[2] + 1, :]).astype(df_ref.dtype)
            acc[3] += _rowsum8(dx * f_ref[...].astype(F32))

        @pl.when(i == pl.num_programs(0) - 1)
        def _():
            s_ref[...] = jnp.zeros_like(s_ref)
            for q in range(nacc):
                s_ref[q:q + 1, :] = jnp.sum(acc[q], axis=0, keepdims=True)

    in_specs = [_rows(TL, D), _rows(TL, D), _rows(TL, D), _fixed(1, D), _fixed(6, D)]
    out_specs = [_rows(TL, D), _fixed(8, D)]
    out_shape = [jax.ShapeDtypeStruct((L, D), F32), jax.ShapeDtypeStruct((8, D), F32)]
    args = [dh, x, dres, gamma, mods]
    if branch:
        in_specs += [_rows(TL, D), _fixed(6, D)]
        out_specs.append(_rows(TL, D))
        out_shape.append(jax.ShapeDtypeStruct((L, D), BF))
        args += [branch[0], branch[1]]
    return pl.pallas_call(
        body, name=name, grid=(L // TL,), in_specs=in_specs, out_specs=out_specs, out_shape=out_shape,
        scratch_shapes=[pltpu.VMEM((nacc, 8, D), F32)], compiler_params=_cp("arbitrary"))(*args)


def gate_bwd(dx, f, mods, k_gate, name):
    L, D = dx.shape
    TL = _tile(L, (512, 256))

    def body(dx_ref, f_ref, m_ref, o_ref, s_ref, acc):
        i = pl.program_id(0)

        @pl.when(i == 0)
        def _():
            acc[...] = jnp.zeros_like(acc)

        dxv = dx_ref[...]
        o_ref[...] = (dxv * m_ref[k_gate:k_gate + 1, :]).astype(o_ref.dtype)
        acc[...] += _rowsum8(dxv * f_ref[...].astype(F32))

        @pl.when(i == pl.num_programs(0) - 1)
        def _():
            s_ref[...] = jnp.zeros_like(s_ref)
            s_ref[0:1, :] = jnp.sum(acc[...], axis=0, keepdims=True)

    return pl.pallas_call(
        body, name=name, grid=(L // TL,), in_specs=[_rows(TL, D), _rows(TL, D), _fixed(6, D)],
        out_specs=[_rows(TL, D), _fixed(8, D)],
        out_shape=[jax.ShapeDtypeStruct((L, D), BF), jax.ShapeDtypeStruct((8, D), F32)],
        scratch_shapes=[pltpu.VMEM((8, D), F32)], compiler_params=_cp("arbitrary"))(dx, f, mods)


def ffn_in_act(a, w, name):
    M, K = a.shape
    S, _, Ns = w.shape
    half = S // 2
    TM = _tile(M, (512, 256))
    TN = _tile(Ns, (1408, 1024, 768, 512, 256, 128))
    nj = Ns // TN

    def body(a_ref, wg_ref, wu_ref, gu_ref, act_ref):
        av = a_ref[...]
        g = jnp.dot(av, wg_ref[...], preferred_element_type=F32)
        u = jnp.dot(av, wu_ref[...], preferred_element_type=F32)
        gu_ref[0] = g.astype(gu_ref.dtype)
        gu_ref[1] = u.astype(gu_ref.dtype)
        act_ref[...] = (g * jax.nn.sigmoid(g) * u).astype(act_ref.dtype)

    return pl.pallas_call(
        body, name=name, grid=(half, nj, M // TM),
        in_specs=[pl.BlockSpec((TM, K), lambda s, j, i: (i, 0)),
                  pl.BlockSpec((None, K, TN), lambda s, j, i: (s, 0, j)),
                  pl.BlockSpec((None, K, TN), lambda s, j, i: (s + half, 0, j))],
        out_specs=[pl.BlockSpec((2, TM, TN), lambda s, j, i: (0, i, s * nj + j)),
                   pl.BlockSpec((TM, TN), lambda s, j, i: (i, s * nj + j))],
        out_shape=[jax.ShapeDtypeStruct((2, M, half * Ns), BF), jax.ShapeDtypeStruct((M, half * Ns), BF)],
        compiler_params=_cp("parallel", "parallel", "parallel"))(a, w, w)


def ffn_out_bwd(dff, w2, gu, name):
    M, D = dff.shape
    F = w2.shape[0]
    TM = _tile(M, (512, 256))
    CW = _tile(F, (256, 128))

    def body(d_ref, w_ref, gu_ref, o_ref):
        dv = d_ref[...]
        for c in range(0, F, CW):
            da = lax.dot_general(dv, w_ref[c:c + CW, :], (((1,), (1,)), ((), ())), preferred_element_type=F32)
            g = gu_ref[0, :, c:c + CW].astype(F32)
            u = gu_ref[1, :, c:c + CW].astype(F32)
            s = jax.nn.sigmoid(g)
            o_ref[0, :, c:c + CW] = (da * u * (s + g * s * (1.0 - s))).astype(o_ref.dtype)
            o_ref[1, :, c:c + CW] = (da * g * s).astype(o_ref.dtype)

    part = pl.BlockSpec((2, TM, F), lambda i: (0, i, 0))
    return pl.pallas_call(
        body, name=name, grid=(M // TM,),
        in_specs=[pl.BlockSpec((TM, D), lambda i: (i, 0)), pl.BlockSpec((F, D), lambda i: (0, 0)), part],
        out_specs=part, out_shape=jax.ShapeDtypeStruct((2, M, F), BF),
        compiler_params=_cp("parallel"))(dff, w2, gu)


def swiglu_act(gu, name):
    L, F2 = gu.shape
    F = F2 // 2
    TL = _tile(L, (256,))

    def body(gu_ref, o_ref):
        g = gu_ref[:, :F].astype(F32)
        u = gu_ref[:, F:].astype(F32)
        o_ref[...] = (g * jax.nn.sigmoid(g) * u).astype(o_ref.dtype)

    return pl.pallas_call(body, name=name, grid=(L // TL,), in_specs=[_rows(TL, F2)], out_specs=_rows(TL, F),
                          out_shape=jax.ShapeDtypeStruct((L, F), BF), compiler_params=_cp("parallel"))(gu)


def swiglu_bwd(da, gu, name):
    L, F2 = gu.shape
    F = F2 // 2
    TL = _tile(L, (256,))

    def body(da_ref, gu_ref, o_ref):
        g = gu_ref[:, :F].astype(F32)
        u = gu_ref[:, F:].astype(F32)
        d = da_ref[...].astype(F32)
        s = jax.nn.sigmoid(g)
        o_ref[:, :F] = (d * u * (s + g * s * (1.0 - s))).astype(o_ref.dtype)
        o_ref[:, F:] = (d * g * s).astype(o_ref.dtype)

    return pl.pallas_call(body, name=name, grid=(L // TL,), in_specs=[_rows(TL, F), _rows(TL, F2)],
                          out_specs=_rows(TL, F2), out_shape=jax.ShapeDtypeStruct((L, F2), BF),
                          compiler_params=_cp("parallel"))(da, gu)


def glu_res(o, x, mods, k_gate, name):
    L, D = x.shape
    TL = _tile(L, (512, 256))

    def body(o_ref, x_ref, m_ref, mix_ref, y_ref):
        mix = o_ref[:, :D].astype(F32) * jax.nn.sigmoid(o_ref[:, D:].astype(F32))
        mix_ref[...] = mix.astype(mix_ref.dtype)
        y_ref[...] = x_ref[...] + m_ref[k_gate:k_gate + 1, :] * mix

    return pl.pallas_call(
        body, name=name, grid=(L // TL,), in_specs=[_rows(TL, 2 * D), _rows(TL, D), _fixed(6, D)],
        out_specs=[_rows(TL, D), _rows(TL, D)],
        out_shape=[jax.ShapeDtypeStruct((L, D), BF), jax.ShapeDtypeStruct((L, D), F32)],
        compiler_params=_cp("parallel"))(o, x, mods)


def ssm_out_glu(z, w, x, mods, k_gate, name):
    M, K = z.shape
    S, _, Ns = w.shape
    half = S // 2
    TM = _tile(M, (1024, 512, 256))
    TN = _tile(Ns, (512, 256, 128))
    nj = Ns // TN

    def body(z_ref, wv_ref, wg_ref, x_ref, m_ref, o_ref, mix_ref, y_ref):
        zv = z_ref[...]
        val = jnp.dot(zv, wv_ref[...], preferred_element_type=F32)
        gate = jnp.dot(zv, wg_ref[...], preferred_element_type=F32)
        o_ref[0] = val.astype(o_ref.dtype)
        o_ref[1] = gate.astype(o_ref.dtype)
        mix = val * jax.nn.sigmoid(gate)
        mix_ref[...] = mix.astype(mix_ref.dtype)
        y_ref[...] = x_ref[...] + m_ref[k_gate:k_gate + 1, :] * mix

    col = lambda s, j, i: (i, s * nj + j)
    return pl.pallas_call(
        body, name=name, grid=(half, nj, M // TM),
        in_specs=[pl.BlockSpec((TM, K), lambda s, j, i: (i, 0)),
                  pl.BlockSpec((None, K, TN), lambda s, j, i: (s, 0, j)),
                  pl.BlockSpec((None, K, TN), lambda s, j, i: (s + half, 0, j)),
                  pl.BlockSpec((TM, TN), col), pl.BlockSpec((6, TN), lambda s, j, i: (0, s * nj + j))],
        out_specs=[pl.BlockSpec((2, TM, TN), lambda s, j, i: (0, i, s * nj + j)), pl.BlockSpec((TM, TN), col),
                   pl.BlockSpec((TM, TN), col)],
        out_shape=[jax.ShapeDtypeStruct((2, M, half * Ns), BF), jax.ShapeDtypeStruct((M, half * Ns), BF),
                   jax.ShapeDtypeStruct((M, half * Ns), F32)],
        compiler_params=_cp("parallel", "parallel", "parallel"))(z, w, w, x, mods)


def glu_bwd(dmix, o, name):
    _, L, D = o.shape
    TL = _tile(L, (512, 256))

    def body(d_ref, o_ref, do_ref):
        d = d_ref[...].astype(F32)
        val = o_ref[0].astype(F32)
        s = jax.nn.sigmoid(o_ref[1].astype(F32))
        do_ref[0] = (d * s).astype(do_ref.dtype)
        do_ref[1] = (d * val * s * (1.0 - s)).astype(do_ref.dtype)

    part = pl.BlockSpec((2, TL, D), lambda i: (0, i, 0))
    return pl.pallas_call(body, name=name, grid=(L // TL,), in_specs=[_rows(TL, D), part],
                          out_specs=part, out_shape=jax.ShapeDtypeStruct((2, L, D), BF),
                          compiler_params=_cp("parallel"))(dmix, o)


def final_loss(x, target, gamma, f, fmods, k_gate, name):
    L, D = x.shape
    TL = _tile(L, (512, 256))

    def body(x_ref, t_ref, g_ref, f_ref, fm_ref, l_ref, dx_ref, s_ref, df_ref, acc, lacc):
        i = pl.program_id(0)

        @pl.when(i == 0)
        def _():
            acc[...] = jnp.zeros_like(acc)
            lacc[...] = jnp.zeros_like(lacc)

        xn, r = _norm_parts(x_ref[...])
        gam = g_ref[...]
        e = xn * gam - t_ref[...]
        lacc[...] += jnp.sum(0.5 * jnp.mean(e * e, axis=-1, keepdims=True), axis=0, keepdims=True)
        dy = e * (1.0 / D)
        dxn = dy * gam
        dx = r * (dxn - xn * jnp.mean(dxn * xn, axis=-1, keepdims=True))
        dx_ref[...] = dx
        df_ref[...] = (dx * fm_ref[k_gate:k_gate + 1, :]).astype(df_ref.dtype)
        acc[0] += _rowsum8(dy * xn)
        acc[1] += _rowsum8(dx * f_ref[...].astype(F32))

        @pl.when(i == pl.num_programs(0) - 1)
        def _():
            s_ref[...] = jnp.zeros_like(s_ref)
            for q in range(2):
                s_ref[q:q + 1, :] = jnp.sum(acc[q], axis=0, keepdims=True)
            l_ref[...] = jnp.broadcast_to(lacc[...], l_ref.shape)

    return pl.pallas_call(
        body, name=name, grid=(L // TL,),
        in_specs=[_rows(TL, D), _rows(TL, D), _fixed(1, D), _rows(TL, D), _fixed(6, D)],
        out_specs=[_fixed(8, 128), _rows(TL, D), _fixed(8, D), _rows(TL, D)],
        out_shape=[jax.ShapeDtypeStruct((8, 128), F32), jax.ShapeDtypeStruct((L, D), F32),
                   jax.ShapeDtypeStruct((8, D), F32), jax.ShapeDtypeStruct((L, D), BF)],
        scratch_shapes=[pltpu.VMEM((2, 8, D), F32), pltpu.VMEM((1, 1), F32)],
        compiler_params=_cp("arbitrary"))(x, target, gamma, f, fmods)


def _col(L, TC, off):
    return pl.BlockSpec((L, TC), lambda j: (0, off + j))


def _shift_down(v, k, row):
    return jnp.where(row >= k, pltpu.roll(v, k, 0), 0.0)


def _shift_up(v, k, row, L):
    return jnp.where(row < L - k, pltpu.roll(v, L - k, 0), 0.0)


def conv_fwd(p, w, name):
    L, D3 = p.shape
    D = D3 // 3
    TC = _tile(D, (128,))
    nc = D // TC

    def body(b_ref, c_ref, v_ref, w_ref, o_ref):
        row = lax.broadcasted_iota(jnp.int32, (L, TC), 0)
        cv = c_ref[...].astype(F32) * v_ref[...].astype(F32)
        conv = w_ref[2:3, :] * cv + w_ref[1:2, :] * _shift_down(cv, 1, row) + w_ref[0:1, :] * _shift_down(cv, 2, row)
        o_ref[...] = (b_ref[...].astype(F32) * conv).astype(o_ref.dtype)

    return pl.pallas_call(
        body, name=name, grid=(nc,),
        in_specs=[_col(L, TC, 0), _col(L, TC, nc), _col(L, TC, 2 * nc), pl.BlockSpec((3, TC), lambda j: (0, j))],
        out_specs=_col(L, TC, 0), out_shape=jax.ShapeDtypeStruct((L, D), BF), compiler_params=_cp("parallel"))(p, p, p, w)


def conv_bwd(dm, p, w, name):
    L, D3 = p.shape
    D = D3 // 3
    TC = _tile(D, (128,))
    nc = D // TC

    def body(dm_ref, b_ref, c_ref, v_ref, w_ref, db_ref, dc_ref, dv_ref, dw_ref):
        row = lax.broadcasted_iota(jnp.int32, (L, TC), 0)
        cg, vv = c_ref[...].astype(F32), v_ref[...].astype(F32)
        cv = cg * vv
        cv1, cv2 = _shift_down(cv, 1, row), _shift_down(cv, 2, row)
        conv = w_ref[2:3, :] * cv + w_ref[1:2, :] * cv1 + w_ref[0:1, :] * cv2
        dmv = dm_ref[...].astype(F32)
        db_ref[...] = (dmv * conv).astype(db_ref.dtype)
        dconv = dmv * b_ref[...].astype(F32)
        dcv = (w_ref[2:3, :] * dconv + w_ref[1:2, :] * _shift_up(dconv, 1, row, L)
               + w_ref[0:1, :] * _shift_up(dconv, 2, row, L))
        dc_ref[...] = (dcv * vv).astype(dc_ref.dtype)
        dv_ref[...] = (dcv * cg).astype(dv_ref.dtype)
        dw_ref[...] = jnp.zeros_like(dw_ref)
        dw_ref[0:1, :] = jnp.sum(dconv * cv2, axis=0, keepdims=True)
        dw_ref[1:2, :] = jnp.sum(dconv * cv1, axis=0, keepdims=True)
        dw_ref[2:3, :] = jnp.sum(dconv * cv, axis=0, keepdims=True)

    one = jax.ShapeDtypeStruct((L, D), BF)
    return pl.pallas_call(
        body, name=name, grid=(nc,),
        in_specs=[_col(L, TC, 0), _col(L, TC, 0), _col(L, TC, nc), _col(L, TC, 2 * nc),
                  pl.BlockSpec((3, TC), lambda j: (0, j))],
        out_specs=[_col(L, TC, 0), _col(L, TC, 0), _col(L, TC, 0), pl.BlockSpec((8, TC), lambda j: (0, j))],
        out_shape=[one, one, one, jax.ShapeDtypeStruct((8, D), F32)],
        compiler_params=_cp("parallel"))(dm, p, p, p, w)


def _gelu(y):
    return 0.5 * y * (1.0 + jnp.tanh(GELU_C * (y + GELU_A * y * y * y)))


def _gelu_grad(y):
    th = jnp.tanh(GELU_C * (y + GELU_A * y * y * y))
    return 0.5 * (1.0 + th) + 0.5 * y * (1.0 - th * th) * GELU_C * (1.0 + 3.0 * GELU_A * y * y)


def _cmul_add(br, bi, ar, ai, sr, si):
    return br + ar * sr - ai * si, bi + ar * si + ai * sr


def _log2(n):
    k = n.bit_length() - 1
    assert 1 << k == n
    return k


def _replicate(P2, W2, P, GLP, transposed):
    shape = (W2, P2) if transposed else (P2, W2)
    k = lax.broadcasted_iota(jnp.int32, shape, 1 if transposed else 0)
    c = lax.broadcasted_iota(jnp.int32, shape, 0 if transposed else 1)
    return ((k >> _log2(P)) == (c >> _log2(GLP))) & ((k & (P - 1)) == (c & (P - 1)))


def _on_diagonal(KB, W2, H, P, GLP, transposed):
    shape = (W2, KB) if transposed else (KB, W2)
    r = lax.broadcasted_iota(jnp.int32, shape, 1 if transposed else 0)
    c = lax.broadcasted_iota(jnp.int32, shape, 0 if transposed else 1)
    return (r >> _log2(H)) == ((c & (GLP - 1)) >> _log2(P))


def _expand(t, dims, transposed):
    KB, W2, H, P, GLP = dims
    rep = _replicate(2 * P, W2, P, GLP, transposed).astype(t.dtype)
    wide = jnp.dot(rep, t, preferred_element_type=F32) if transposed else jnp.dot(t, rep, preferred_element_type=F32)
    return jnp.where(_on_diagonal(KB, W2, H, P, GLP, transposed), wide, 0.0).astype(t.dtype)


def _extract(acc, dims):
    KB, W2, H, P, GLP = dims
    rep = _replicate(2 * P, W2, P, GLP, True).astype(F32)
    kept = jnp.where(_on_diagonal(KB, W2, H, P, GLP, False), acc, 0.0)
    return jnp.dot(kept, rep, preferred_element_type=F32, precision=lax.Precision.HIGHEST)


def _cmul(ar, ai, sr, si):
    return ar * sr - ai * si, ar * si + ai * sr


LANES = 128


def _cols(ref, base, n, rows):
    return jnp.concatenate([ref[base + q, rows, :] for q in range(n)], axis=1)


def _set_cols(ref, base, n, rows, val):
    for q in range(n):
        ref[base + q, rows, :] = val[:, q * LANES:(q + 1) * LANES]


def _strided_s5_fwd(h, tb, tct, pw, dvec, name):
    L, D = h.shape
    nkb, KB, P2 = tb.shape
    P = P2 // 2
    W = (KB // SSM_GROUP) * P
    W2 = 2 * W
    dims = (KB, W2, SSM_GROUP, P, W)
    TL = _tile(L, (512, 256))
    CH = TL // 8
    NC = W // LANES

    def body(h_ref, tb_ref, tct_ref, pw_ref, d_ref, s_ref, y_ref, z_ref, bw, cw, carry):
        t = pl.program_id(1)

        @pl.when(t == 0)
        def _():
            carry[...] = jnp.zeros_like(carry)
            bw[...] = _expand(tb_ref[...], dims, False)
            cw[...] = _expand(tct_ref[...], dims, True)

        hv = h_ref[...]
        _set_cols(s_ref, 0, 2 * NC, slice(None), jnp.dot(hv.astype(BF), bw[...], preferred_element_type=F32))
        ar, ai = pw_ref[0:8, :W], pw_ref[0:8, W:]
        xr = xi = jnp.zeros((8, W), F32)
        for j in range(CH):
            rows = pl.ds(j, 8, stride=CH)
            xr, xi = _cmul_add(_cols(s_ref, 0, NC, rows), _cols(s_ref, NC, NC, rows), ar, ai, xr, xi)
            _set_cols(s_ref, 0, NC, rows, xr)
            _set_cols(s_ref, NC, NC, rows, xi)
        for k, off in ((1, 8), (2, 16), (4, 24)):
            xr, xi = _cmul_add(xr, xi, pw_ref[off:off + 8, :W], pw_ref[off:off + 8, W:],
                               pltpu.roll(xr, k, 0), pltpu.roll(xi, k, 0))
        xr, xi = _cmul_add(xr, xi, pw_ref[32:40, :W], pw_ref[32:40, W:], carry[0], carry[1])
        first = lax.broadcasted_iota(jnp.int32, (8, W), 0) == 0
        cr = jnp.where(first, carry[0], pltpu.roll(xr, 1, 0))
        ci = jnp.where(first, carry[1], pltpu.roll(xi, 1, 0))
        carry[0] = jnp.broadcast_to(xr[7:8], (8, W))
        carry[1] = jnp.broadcast_to(xi[7:8], (8, W))
        for j in range(CH):
            rows = pl.ds(j, 8, stride=CH)
            cr, ci = _cmul(ar, ai, cr, ci)
            _set_cols(s_ref, 0, NC, rows, _cols(s_ref, 0, NC, rows) + cr)
            _set_cols(s_ref, NC, NC, rows, _cols(s_ref, NC, NC, rows) + ci)
        sv = _cols(s_ref, 0, 2 * NC, slice(None))
        y = jnp.dot(sv.astype(BF), cw[...], preferred_element_type=F32) + d_ref[...] * hv
        y_ref[...] = y
        z_ref[...] = _gelu(y).astype(z_ref.dtype)

    blk = lambda kb, t: (t, kb)
    per_kb = lambda kb, t: (kb, 0, 0)
    return pl.pallas_call(
        body, name=name, grid=(nkb, L // TL),
        in_specs=[pl.BlockSpec((TL, KB), blk), pl.BlockSpec((None, KB, P2), per_kb),
                  pl.BlockSpec((None, P2, KB), per_kb), pl.BlockSpec((None, 40, W2), per_kb),
                  pl.BlockSpec((1, KB), lambda kb, t: (0, kb))],
        out_specs=[pl.BlockSpec((2 * NC, TL, LANES), lambda kb, t: (kb, t, 0)), pl.BlockSpec((TL, KB), blk),
                   pl.BlockSpec((TL, KB), blk)],
        out_shape=[jax.ShapeDtypeStruct((nkb * 2 * NC, L, LANES), F32), jax.ShapeDtypeStruct((L, D), F32),
                   jax.ShapeDtypeStruct((L, D), BF)],
        scratch_shapes=[pltpu.VMEM((KB, W2), BF), pltpu.VMEM((W2, KB), BF), pltpu.VMEM((2, 8, W), F32)],
        compiler_params=_cp("parallel", "arbitrary"))(h, tb, tct, pw, dvec)


def _strided_s5_bwd(dz, y, h, s, tc, tbt, pwr, dvec, name):
    L, D = h.shape
    nkb, KB, P2 = tc.shape
    P = P2 // 2
    W = (KB // SSM_GROUP) * P
    W2 = 2 * W
    dims = (KB, W2, SSM_GROUP, P, W)
    TL = _tile(L, (512, 256))
    CH = TL // 8
    NC = W // LANES
    nt = L // TL

    def body(dz_ref, y_ref, h_ref, s_ref, sp_ref, tc_ref, tbt_ref, pw_ref, d_ref,
             dh_ref, dd_ref, da_ref, db_ref, dc_ref, g, ctw, btw, dbacc, dcacc, carry):
        t = pl.program_id(1)

        @pl.when(t == 0)
        def _():
            carry[...] = jnp.zeros_like(carry)
            dd_ref[...] = jnp.zeros_like(dd_ref)
            da_ref[...] = jnp.zeros_like(da_ref)
            dbacc[...] = jnp.zeros_like(dbacc)
            dcacc[...] = jnp.zeros_like(dcacc)
            ctw[...] = _expand(tc_ref[...], dims, False)
            btw[...] = _expand(tbt_ref[...], dims, True)

        hv = h_ref[...]
        dy = dz_ref[...].astype(F32) * _gelu_grad(y_ref[...])
        dd_ref[...] += _rowsum8(dy * hv)
        dyb = dy.astype(BF)
        _set_cols(g, 0, 2 * NC, slice(None), jnp.dot(dyb, ctw[...], preferred_element_type=F32))
        ar, ai = pw_ref[0:8, :W], pw_ref[0:8, W:]
        gr = gi = jnp.zeros((8, W), F32)
        for j in reversed(range(CH)):
            rows = pl.ds(j, 8, stride=CH)
            gr, gi = _cmul_add(_cols(g, 0, NC, rows), _cols(g, NC, NC, rows), ar, ai, gr, gi)
            _set_cols(g, 0, NC, rows, gr)
            _set_cols(g, NC, NC, rows, gi)
        for k, off in ((1, 8), (2, 16), (4, 24)):
            gr, gi = _cmul_add(gr, gi, pw_ref[off:off + 8, :W], pw_ref[off:off + 8, W:],
                               pltpu.roll(gr, 8 - k, 0), pltpu.roll(gi, 8 - k, 0))
        gr, gi = _cmul_add(gr, gi, pw_ref[32:40, :W], pw_ref[32:40, W:], carry[0], carry[1])
        sub = lax.broadcasted_iota(jnp.int32, (8, W), 0)
        cr = jnp.where(sub == 7, carry[0], pltpu.roll(gr, 7, 0))
        ci = jnp.where(sub == 7, carry[1], pltpu.roll(gi, 7, 0))
        carry[0] = jnp.broadcast_to(gr[0:1], (8, W))
        carry[1] = jnp.broadcast_to(gi[0:1], (8, W))
        live = jnp.where(t == nt - 1, 0.0, 1.0)
        accr = acci = jnp.zeros((8, W), F32)
        for j in reversed(range(CH)):
            rows = pl.ds(j, 8, stride=CH)
            cr, ci = _cmul(ar, ai, cr, ci)
            gr, gi = _cols(g, 0, NC, rows) + cr, _cols(g, NC, NC, rows) + ci
            _set_cols(g, 0, NC, rows, gr)
            _set_cols(g, NC, NC, rows, gi)
            if j > 0:
                before = pl.ds(j - 1, 8, stride=CH)
                pr, pi = _cols(s_ref, 0, NC, before), _cols(s_ref, NC, NC, before)
            else:
                last = pl.ds(CH - 1, 8, stride=CH)
                pr = jnp.where(sub == 0, _cols(sp_ref, 0, NC, slice(7, 8)) * live,
                               pltpu.roll(_cols(s_ref, 0, NC, last), 1, 0))
                pi = jnp.where(sub == 0, _cols(sp_ref, NC, NC, slice(7, 8)) * live,
                               pltpu.roll(_cols(s_ref, NC, NC, last), 1, 0))
            accr = accr + pr * gr + pi * gi
            acci = acci + pr * gi - pi * gr
        da_ref[:, :W] += accr
        da_ref[:, W:] += acci

        gb = _cols(g, 0, 2 * NC, slice(None)).astype(BF)
        dh_ref[...] = dy * d_ref[...] + jnp.dot(gb, btw[...], preferred_element_type=F32)
        tn = (((0,), (0,)), ((), ()))
        dbacc[...] += lax.dot_general(hv.astype(BF), gb, tn, preferred_element_type=F32)
        dcacc[...] += lax.dot_general(dyb, _cols(s_ref, 0, 2 * NC, slice(None)).astype(BF), tn,
                                      preferred_element_type=F32)

        @pl.when(t == nt - 1)
        def _():
            db_ref[...] = _extract(dbacc[...], dims)
            dc_ref[...] = _extract(dcacc[...], dims)

    rev = lambda kb, t: (nt - 1 - t, kb)
    per_kb = lambda kb, t: (kb, 0, 0)
    return pl.pallas_call(
        body, name=name, grid=(nkb, nt),
        in_specs=[pl.BlockSpec((TL, KB), rev), pl.BlockSpec((TL, KB), rev), pl.BlockSpec((TL, KB), rev),
                  pl.BlockSpec((2 * NC, TL, LANES), lambda kb, t: (kb, nt - 1 - t, 0)),
                  pl.BlockSpec((2 * NC, 8, LANES), lambda kb, t: (kb, jnp.maximum((nt - 1 - t) * CH - 1, 0), 0)),
                  pl.BlockSpec((None, KB, P2), per_kb), pl.BlockSpec((None, P2, KB), per_kb),
                  pl.BlockSpec((None, 40, W2), per_kb), pl.BlockSpec((1, KB), lambda kb, t: (0, kb))],
        out_specs=[pl.BlockSpec((TL, KB), rev), pl.BlockSpec((8, KB), lambda kb, t: (0, kb)),
                   pl.BlockSpec((None, 8, W2), per_kb), pl.BlockSpec((None, KB, P2), per_kb),
                   pl.BlockSpec((None, KB, P2), per_kb)],
        out_shape=[jax.ShapeDtypeStruct((L, D), F32), jax.ShapeDtypeStruct((8, D), F32),
                   jax.ShapeDtypeStruct((nkb, 8, W2), F32), jax.ShapeDtypeStruct((nkb, KB, P2), F32),
                   jax.ShapeDtypeStruct((nkb, KB, P2), F32)],
        scratch_shapes=[pltpu.VMEM((2 * NC, TL, LANES), F32), pltpu.VMEM((KB, W2), BF), pltpu.VMEM((W2, KB), BF),
                        pltpu.VMEM((KB, W2), F32), pltpu.VMEM((KB, W2), F32), pltpu.VMEM((2, 8, W), F32)],
        compiler_params=_cp("parallel", "arbitrary"))(dz, y, h, s, s, tc, tbt, pwr, dvec)


def _chunk_order(TL, CH, transposed):
    out_row = lax.broadcasted_iota(jnp.int32, (TL, TL), 1 if transposed else 0)
    in_row = lax.broadcasted_iota(jnp.int32, (TL, TL), 0 if transposed else 1)
    return in_row == ((out_row & 7) << _log2(CH)) + (out_row >> 3)


def _reorder(perm, v):
    hi = v.astype(perm.dtype)
    lo = (v - hi.astype(F32)).astype(perm.dtype)
    return jnp.dot(perm, hi, preferred_element_type=F32) + jnp.dot(perm, lo, preferred_element_type=F32)


def _interleave(main, side):
    n, m, k = len(main), len(side), 0
    for i, step in enumerate(main):
        step()
        while k < m and (k + 1) * n <= (i + 1) * m:
            side[k]()
            k += 1
    for step in side[k:]:
        step()


S5_CHUNK = 512


def s5_fwd(h, tb, tct, pw, dvec, name):
    L, D = h.shape
    nkb, KB, P2 = tb.shape
    P = P2 // 2
    W = (KB // SSM_GROUP) * P
    W2 = 2 * W
    dims = (KB, W2, SSM_GROUP, P, W)
    TL = _tile(L, (512, 256))
    CH = TL // 8
    NB = 2 if nkb % 2 == 0 else 1
    CK = min(S5_CHUNK, W2)

    def body(h_ref, tb_ref, tct_ref, pw_ref, d_ref, s_ref, y_ref, z_ref, bw, cw, perm, unperm, carry):
        t = pl.program_id(1)

        @pl.when(t == 0)
        def _():
            carry[...] = jnp.zeros_like(carry)
            for b in range(NB):
                bw[b] = _expand(tb_ref[b], dims, False)
                cw[b] = _expand(tct_ref[b], dims, True)
            perm[...] = _chunk_order(TL, CH, False).astype(perm.dtype)
            unperm[...] = _chunk_order(TL, CH, True).astype(perm.dtype)

        hp = _reorder(perm[...], h_ref[...])
        hpb = hp.astype(BF)
        first = lax.broadcasted_iota(jnp.int32, (8, W), 0) == 0

        def project(b):
            def chunk(c):
                def emit():
                    s_ref[:, b * W2 + c:b * W2 + c + CK] = jnp.dot(hpb[:, b * KB:(b + 1) * KB], bw[b, :, c:c + CK],
                                                                   preferred_element_type=F32)
                return emit
            return [chunk(c) for c in range(0, W2, CK)]

        def scan(b):
            re, im = slice(b * W2, b * W2 + W), slice(b * W2 + W, (b + 1) * W2)
            ar, ai = pw_ref[b, 0:8, :W], pw_ref[b, 0:8, W:]
            st = {"x": (jnp.zeros((8, W), F32), jnp.zeros((8, W), F32))}

            def own(j):
                def emit():
                    rows = slice(j * 8, j * 8 + 8)
                    xr, xi = _cmul_add(s_ref[rows, re], s_ref[rows, im], ar, ai, *st["x"])
                    s_ref[rows, re] = xr
                    s_ref[rows, im] = xi
                    st["x"] = (xr, xi)
                return emit

            def ends():
                xr, xi = st["x"]
                for k, off in ((1, 8), (2, 16), (4, 24)):
                    xr, xi = _cmul_add(xr, xi, pw_ref[b, off:off + 8, :W], pw_ref[b, off:off + 8, W:],
                                       pltpu.roll(xr, k, 0), pltpu.roll(xi, k, 0))
                xr, xi = _cmul_add(xr, xi, pw_ref[b, 32:40, :W], pw_ref[b, 32:40, W:], carry[b, 0], carry[b, 1])
                st["c"] = (jnp.where(first, carry[b, 0], pltpu.roll(xr, 1, 0)),
                           jnp.where(first, carry[b, 1], pltpu.roll(xi, 1, 0)))
                carry[b, 0] = jnp.broadcast_to(xr[7:8], (8, W))
                carry[b, 1] = jnp.broadcast_to(xi[7:8], (8, W))

            def carried(j):
                def emit():
                    rows = slice(j * 8, j * 8 + 8)
                    cr, ci = _cmul(ar, ai, *st["c"])
                    s_ref[rows, re] = s_ref[rows, re] + cr
                    s_ref[rows, im] = s_ref[rows, im] + ci
                    st["c"] = (cr, ci)
                return emit

            return [own(j) for j in range(CH)] + [ends] + [carried(j) for j in range(CH)]

        def readout(b):
            cols = slice(b * KB, (b + 1) * KB)
            acc = {}

            def chunk(c):
                def emit():
                    part = jnp.dot(s_ref[:, b * W2 + c:b * W2 + c + CK].astype(BF), cw[b, c:c + CK, :],
                                   preferred_element_type=F32)
                    acc["y"] = part if c == 0 else acc["y"] + part
                return emit

            def finish():
                y = acc["y"] + d_ref[:, cols] * hp[:, cols]
                y_ref[:, cols] = y
                z_ref[:, cols] = jnp.dot(unperm[...], _gelu(y).astype(BF),
                                         preferred_element_type=F32).astype(z_ref.dtype)

            return [chunk(c) for c in range(0, W2, CK)] + [finish]

        for emit in project(0):
            emit()
        for b in range(NB):
            side = (project(b + 1) if b + 1 < NB else []) + (readout(b - 1) if b > 0 else [])
            _interleave(scan(b), side)
        for emit in readout(NB - 1):
            emit()

    blk = lambda kb, t: (t, kb)
    per_kb = lambda kb, t: (kb, 0, 0)
    return pl.pallas_call(
        body, name=name, grid=(nkb // NB, L // TL),
        in_specs=[pl.BlockSpec((TL, NB * KB), blk), pl.BlockSpec((NB, KB, P2), per_kb),
                  pl.BlockSpec((NB, P2, KB), per_kb), pl.BlockSpec((NB, 40, W2), per_kb),
                  pl.BlockSpec((1, NB * KB), lambda kb, t: (0, kb))],
        out_specs=[pl.BlockSpec((TL, NB * W2), blk), pl.BlockSpec((TL, NB * KB), blk),
                   pl.BlockSpec((TL, NB * KB), blk)],
        out_shape=[jax.ShapeDtypeStruct((L, nkb * W2), F32), jax.ShapeDtypeStruct((L, D), F32),
                   jax.ShapeDtypeStruct((L, D), BF)],
        scratch_shapes=[pltpu.VMEM((NB, KB, W2), BF), pltpu.VMEM((NB, W2, KB), BF), pltpu.VMEM((TL, TL), BF),
                        pltpu.VMEM((TL, TL), BF), pltpu.VMEM((NB, 2, 8, W), F32)],
        compiler_params=_cp("parallel", "arbitrary"))(h, tb, tct, pw, dvec)


def _s5_fwd_one_block(h, tb, tct, pw, dvec, name):
    L, D = h.shape
    nkb, KB, P2 = tb.shape
    P = P2 // 2
    W = (KB // SSM_GROUP) * P
    W2 = 2 * W
    dims = (KB, W2, SSM_GROUP, P, W)
    TL = _tile(L, (512, 256))
    CH = TL // 8

    def body(h_ref, tb_ref, tct_ref, pw_ref, d_ref, s_ref, y_ref, z_ref, bw, cw, perm, unperm, carry):
        t = pl.program_id(1)

        @pl.when(t == 0)
        def _():
            carry[...] = jnp.zeros_like(carry)
            bw[...] = _expand(tb_ref[...], dims, False)
            cw[...] = _expand(tct_ref[...], dims, True)
            perm[...] = _chunk_order(TL, CH, False).astype(perm.dtype)
            unperm[...] = _chunk_order(TL, CH, True).astype(perm.dtype)

        hp = _reorder(perm[...], h_ref[...])
        s_ref[...] = jnp.dot(hp.astype(BF), bw[...], preferred_element_type=F32)
        ar, ai = pw_ref[0:8, :W], pw_ref[0:8, W:]

        def own(j, x):
            rows = pl.ds(pl.multiple_of(j * 8, 8), 8)
            xr, xi = _cmul_add(s_ref[rows, :W], s_ref[rows, W:], ar, ai, x[0], x[1])
            s_ref[rows, :W] = xr
            s_ref[rows, W:] = xi
            return xr, xi

        zero = jnp.zeros((8, W), F32)
        xr, xi = lax.fori_loop(0, CH, own, (zero, zero))
        for k, off in ((1, 8), (2, 16), (4, 24)):
            xr, xi = _cmul_add(xr, xi, pw_ref[off:off + 8, :W], pw_ref[off:off + 8, W:],
                               pltpu.roll(xr, k, 0), pltpu.roll(xi, k, 0))
        xr, xi = _cmul_add(xr, xi, pw_ref[32:40, :W], pw_ref[32:40, W:], carry[0], carry[1])
        first = lax.broadcasted_iota(jnp.int32, (8, W), 0) == 0
        cr = jnp.where(first, carry[0], pltpu.roll(xr, 1, 0))
        ci = jnp.where(first, carry[1], pltpu.roll(xi, 1, 0))
        carry[0] = jnp.broadcast_to(xr[7:8], (8, W))
        carry[1] = jnp.broadcast_to(xi[7:8], (8, W))

        def carried(j, c):
            rows = pl.ds(pl.multiple_of(j * 8, 8), 8)
            cr, ci = _cmul(ar, ai, c[0], c[1])
            s_ref[rows, :W] = s_ref[rows, :W] + cr
            s_ref[rows, W:] = s_ref[rows, W:] + ci
            return cr, ci

        lax.fori_loop(0, CH, carried, (cr, ci))
        y = jnp.dot(s_ref[...].astype(BF), cw[...], preferred_element_type=F32) + d_ref[...] * hp
        y_ref[...] = y
        z_ref[...] = jnp.dot(unperm[...], _gelu(y).astype(BF), preferred_element_type=F32).astype(z_ref.dtype)

    blk = lambda kb, t: (t, kb)
    per_kb = lambda kb, t: (kb, 0, 0)
    return pl.pallas_call(
        body, name=name, grid=(nkb, L // TL),
        in_specs=[pl.BlockSpec((TL, KB), blk), pl.BlockSpec((None, KB, P2), per_kb),
                  pl.BlockSpec((None, P2, KB), per_kb), pl.BlockSpec((None, 40, W2), per_kb),
                  pl.BlockSpec((1, KB), lambda kb, t: (0, kb))],
        out_specs=[pl.BlockSpec((TL, W2), blk), pl.BlockSpec((TL, KB), blk), pl.BlockSpec((TL, KB), blk)],
        out_shape=[jax.ShapeDtypeStruct((L, nkb * W2), F32), jax.ShapeDtypeStruct((L, D), F32),
                   jax.ShapeDtypeStruct((L, D), BF)],
        scratch_shapes=[pltpu.VMEM((KB, W2), BF), pltpu.VMEM((W2, KB), BF), pltpu.VMEM((TL, TL), BF),
                        pltpu.VMEM((TL, TL), BF), pltpu.VMEM((2, 8, W), F32)],
        compiler_params=_cp("parallel", "arbitrary"))(h, tb, tct, pw, dvec)


def s5_bwd(dz, y, h, s, tc, tbt, pwr, dvec, name):
    L, D = h.shape
    nkb, KB, P2 = tc.shape
    P = P2 // 2
    W = (KB // SSM_GROUP) * P
    W2 = 2 * W
    dims = (KB, W2, SSM_GROUP, P, W)
    TL = _tile(L, (512, 256))
    CH = TL // 8
    nt = L // TL
    NB = 2 if nkb % 2 == 0 else 1
    CK = min(S5_CHUNK, W2)
    tn = (((0,), (0,)), ((), ()))

    def body(dz_ref, y_ref, h_ref, s_ref, sp_ref, tc_ref, tbt_ref, pw_ref, d_ref,
             dh_ref, dd_ref, da_ref, db_ref, dc_ref, g, ctw, btw, dbacc, dcacc, dys, perm, unperm, carry):
        t = pl.program_id(1)

        @pl.when(t == 0)
        def _():
            carry[...] = jnp.zeros_like(carry)
            dd_ref[...] = jnp.zeros_like(dd_ref)
            da_ref[...] = jnp.zeros_like(da_ref)
            dbacc[...] = jnp.zeros_like(dbacc)
            dcacc[...] = jnp.zeros_like(dcacc)
            for b in range(NB):
                ctw[b] = _expand(tc_ref[b], dims, False)
                btw[b] = _expand(tbt_ref[b], dims, True)
            perm[...] = _chunk_order(TL, CH, False).astype(perm.dtype)
            unperm[...] = _chunk_order(TL, CH, True).astype(perm.dtype)

        hp = jnp.dot(perm[...], h_ref[...].astype(BF), preferred_element_type=F32)
        dy = jnp.dot(perm[...], dz_ref[...].astype(BF), preferred_element_type=F32) * _gelu_grad(y_ref[...])
        dd_ref[...] += _rowsum8(dy * hp)
        dys[...] = dy
        dyb = dy.astype(BF)
        hpb = hp.astype(BF)
        sub = lax.broadcasted_iota(jnp.int32, (8, W), 0)
        live = jnp.where(t == nt - 1, 0.0, 1.0)

        def lead(b):
            cols = slice(b * KB, (b + 1) * KB)

            def to_states(c):
                def emit():
                    g[b, :, c:c + CK] = jnp.dot(dyb[:, cols], ctw[b, :, c:c + CK], preferred_element_type=F32)
                return emit

            def d_c(c):
                def emit():
                    dcacc[b, :, c:c + CK] += lax.dot_general(dyb[:, cols],
                                                             s_ref[:, b * W2 + c:b * W2 + c + CK].astype(BF), tn,
                                                             preferred_element_type=F32)
                return emit

            return [f(c) for c in range(0, W2, CK) for f in (to_states, d_c)]

        def scan(b):
            re, im = slice(b * W2, b * W2 + W), slice(b * W2 + W, (b + 1) * W2)
            ar, ai = pw_ref[b, 0:8, :W], pw_ref[b, 0:8, W:]
            zero = jnp.zeros((8, W), F32)
            st = {"g": (zero, zero), "acc": (zero, zero)}

            def own(j):
                def emit():
                    rows = slice(j * 8, j * 8 + 8)
                    gr, gi = _cmul_add(g[b, rows, :W], g[b, rows, W:], ar, ai, *st["g"])
                    g[b, rows, :W] = gr
                    g[b, rows, W:] = gi
                    st["g"] = (gr, gi)
                return emit

            def ends():
                gr, gi = st["g"]
                for k, off in ((1, 8), (2, 16), (4, 24)):
                    gr, gi = _cmul_add(gr, gi, pw_ref[b, off:off + 8, :W], pw_ref[b, off:off + 8, W:],
                                       pltpu.roll(gr, 8 - k, 0), pltpu.roll(gi, 8 - k, 0))
                gr, gi = _cmul_add(gr, gi, pw_ref[b, 32:40, :W], pw_ref[b, 32:40, W:], carry[b, 0], carry[b, 1])
                st["c"] = (jnp.where(sub == 7, carry[b, 0], pltpu.roll(gr, 7, 0)),
                           jnp.where(sub == 7, carry[b, 1], pltpu.roll(gi, 7, 0)))
                carry[b, 0] = jnp.broadcast_to(gr[0:1], (8, W))
                carry[b, 1] = jnp.broadcast_to(gi[0:1], (8, W))

            def carried(j):
                def emit():
                    rows = slice(j * 8, j * 8 + 8)
                    cr, ci = _cmul(ar, ai, *st["c"])
                    gr, gi = g[b, rows, :W] + cr, g[b, rows, W:] + ci
                    g[b, rows, :W] = gr
                    g[b, rows, W:] = gi
                    if j > 0:
                        before = slice(j * 8 - 8, j * 8)
                        pr, pi = s_ref[before, re], s_ref[before, im]
                    else:
                        last = slice(TL - 8, TL)
                        pr = jnp.where(sub == 0, sp_ref[7:8, re] * live, pltpu.roll(s_ref[last, re], 1, 0))
                        pi = jnp.where(sub == 0, sp_ref[7:8, im] * live, pltpu.roll(s_ref[last, im], 1, 0))
                    accr, acci = st["acc"]
                    st["c"] = (cr, ci)
                    st["acc"] = (accr + pr * gr + pi * gi, acci + pr * gi - pi * gr)
                return emit

            def done():
                da_ref[b, :, :W] += st["acc"][0]
                da_ref[b, :, W:] += st["acc"][1]

            return ([own(j) for j in reversed(range(CH))] + [ends] + [carried(j) for j in reversed(range(CH))]
                    + [done])

        def tail(b):
            cols = slice(b * KB, (b + 1) * KB)
            acc = {}

            def d_u(c):
                def emit():
                    part = jnp.dot(g[b, :, c:c + CK].astype(BF), btw[b, c:c + CK, :], preferred_element_type=F32)
                    acc["u"] = part if c == 0 else acc["u"] + part
                return emit

            def d_b(c):
                def emit():
                    dbacc[b, :, c:c + CK] += lax.dot_general(hpb[:, cols], g[b, :, c:c + CK].astype(BF), tn,
                                                             preferred_element_type=F32)
                return emit

            def finish():
                dh_ref[:, cols] = _reorder(unperm[...], dys[:, cols] * d_ref[:, cols] + acc["u"])

            return [f(c) for c in range(0, W2, CK) for f in (d_u, d_b)] + [finish]

        for emit in lead(0):
            emit()
        for b in range(NB):
            side = (lead(b + 1) if b + 1 < NB else []) + (tail(b - 1) if b > 0 else [])
            _interleave(scan(b), side)
        for emit in tail(NB - 1):
            emit()

        @pl.when(t == nt - 1)
        def _():
            for b in range(NB):
                db_ref[b] = _extract(dbacc[b], dims)
                dc_ref[b] = _extract(dcacc[b], dims)

    rev = lambda kb, t: (nt - 1 - t, kb)
    prev = lambda kb, t: (jnp.maximum((nt - 1 - t) * CH - 1, 0), kb)
    per_kb = lambda kb, t: (kb, 0, 0)
    return pl.pallas_call(
        body, name=name, grid=(nkb // NB, nt),
        in_specs=[pl.BlockSpec((TL, NB * KB), rev), pl.BlockSpec((TL, NB * KB), rev),
                  pl.BlockSpec((TL, NB * KB), rev), pl.BlockSpec((TL, NB * W2), rev),
                  pl.BlockSpec((8, NB * W2), prev), pl.BlockSpec((NB, KB, P2), per_kb),
                  pl.BlockSpec((NB, P2, KB), per_kb), pl.BlockSpec((NB, 40, W2), per_kb),
                  pl.BlockSpec((1, NB * KB), lambda kb, t: (0, kb))],
        out_specs=[pl.BlockSpec((TL, NB * KB), rev), pl.BlockSpec((8, NB * KB), lambda kb, t: (0, kb)),
                   pl.BlockSpec((NB, 8, W2), per_kb), pl.BlockSpec((NB, KB, P2), per_kb),
                   pl.BlockSpec((NB, KB, P2), per_kb)],
        out_shape=[jax.ShapeDtypeStruct((L, D), F32), jax.ShapeDtypeStruct((8, D), F32),
                   jax.ShapeDtypeStruct((nkb, 8, W2), F32), jax.ShapeDtypeStruct((nkb, KB, P2), F32),
                   jax.ShapeDtypeStruct((nkb, KB, P2), F32)],
        scratch_shapes=[pltpu.VMEM((NB, TL, W2), F32), pltpu.VMEM((NB, KB, W2), BF), pltpu.VMEM((NB, W2, KB), BF),
                        pltpu.VMEM((NB, KB, W2), F32), pltpu.VMEM((NB, KB, W2), F32), pltpu.VMEM((TL, NB * KB), F32),
                        pltpu.VMEM((TL, TL), BF), pltpu.VMEM((TL, TL), BF), pltpu.VMEM((NB, 2, 8, W), F32)],
        compiler_params=pltpu.CompilerParams(dimension_semantics=("parallel", "arbitrary"),
                                             vmem_limit_bytes=V7X_VMEM_BYTES - 4 * 1024 * 1024),
    )(dz, y, h, s, s, tc, tbt, pwr, dvec)


def _s5_bwd_one_block(dz, y, h, s, tc, tbt, pwr, dvec, name):
    L, D = h.shape
    nkb, KB, P2 = tc.shape
    P = P2 // 2
    W = (KB // SSM_GROUP) * P
    W2 = 2 * W
    dims = (KB, W2, SSM_GROUP, P, W)
    TL = _tile(L, (512, 256))
    CH = TL // 8
    nt = L // TL

    def body(dz_ref, y_ref, h_ref, s_ref, sp_ref, tc_ref, tbt_ref, pw_ref, d_ref,
             dh_ref, dd_ref, da_ref, db_ref, dc_ref, g, ctw, btw, dbacc, dcacc, perm, unperm, carry):
        t = pl.program_id(1)

        @pl.when(t == 0)
        def _():
            carry[...] = jnp.zeros_like(carry)
            dd_ref[...] = jnp.zeros_like(dd_ref)
            da_ref[...] = jnp.zeros_like(da_ref)
            dbacc[...] = jnp.zeros_like(dbacc)
            dcacc[...] = jnp.zeros_like(dcacc)
            ctw[...] = _expand(tc_ref[...], dims, False)
            btw[...] = _expand(tbt_ref[...], dims, True)
            perm[...] = _chunk_order(TL, CH, False).astype(perm.dtype)
            unperm[...] = _chunk_order(TL, CH, True).astype(perm.dtype)

        hp = jnp.dot(perm[...], h_ref[...].astype(BF), preferred_element_type=F32)
        dy = jnp.dot(perm[...], dz_ref[...].astype(BF), preferred_element_type=F32) * _gelu_grad(y_ref[...])
        dd_ref[...] += _rowsum8(dy * hp)
        dyb = dy.astype(BF)
        g[...] = jnp.dot(dyb, ctw[...], preferred_element_type=F32)
        ar, ai = pw_ref[0:8, :W], pw_ref[0:8, W:]

        def own(jj, x):
            rows = pl.ds(pl.multiple_of((CH - 1 - jj) * 8, 8), 8)
            gr, gi = _cmul_add(g[rows, :W], g[rows, W:], ar, ai, x[0], x[1])
            g[rows, :W] = gr
            g[rows, W:] = gi
            return gr, gi

        zero = jnp.zeros((8, W), F32)
        gr, gi = lax.fori_loop(0, CH, own, (zero, zero))
        for k, off in ((1, 8), (2, 16), (4, 24)):
            gr, gi = _cmul_add(gr, gi, pw_ref[off:off + 8, :W], pw_ref[off:off + 8, W:],
                               pltpu.roll(gr, 8 - k, 0), pltpu.roll(gi, 8 - k, 0))
        gr, gi = _cmul_add(gr, gi, pw_ref[32:40, :W], pw_ref[32:40, W:], carry[0], carry[1])
        sub = lax.broadcasted_iota(jnp.int32, (8, W), 0)
        cr = jnp.where(sub == 7, carry[0], pltpu.roll(gr, 7, 0))
        ci = jnp.where(sub == 7, carry[1], pltpu.roll(gi, 7, 0))
        carry[0] = jnp.broadcast_to(gr[0:1], (8, W))
        carry[1] = jnp.broadcast_to(gi[0:1], (8, W))

        def carried(jj, c):
            j = CH - 1 - jj
            rows = pl.ds(pl.multiple_of(j * 8, 8), 8)
            before = pl.ds(pl.multiple_of(j * 8 - 8, 8), 8)
            cr, ci = _cmul(ar, ai, c[0], c[1])
            gr, gi = g[rows, :W] + cr, g[rows, W:] + ci
            g[rows, :W] = gr
            g[rows, W:] = gi
            pr, pi = s_ref[before, :W], s_ref[before, W:]
            return cr, ci, c[2] + pr * gr + pi * gi, c[3] + pr * gi - pi * gr

        cr, ci, accr, acci = lax.fori_loop(0, CH - 1, carried, (cr, ci, zero, zero))
        live = jnp.where(t == nt - 1, 0.0, 1.0)
        cr, ci = _cmul(ar, ai, cr, ci)
        gr, gi = g[0:8, :W] + cr, g[0:8, W:] + ci
        g[0:8, :W] = gr
        g[0:8, W:] = gi
        pr = jnp.where(sub == 0, sp_ref[7:8, :W] * live, pltpu.roll(s_ref[TL - 8:TL, :W], 1, 0))
        pi = jnp.where(sub == 0, sp_ref[7:8, W:] * live, pltpu.roll(s_ref[TL - 8:TL, W:], 1, 0))
        da_ref[:, :W] += accr + pr * gr + pi * gi
        da_ref[:, W:] += acci + pr * gi - pi * gr

        gb = g[...].astype(BF)
        dh = dy * d_ref[...] + jnp.dot(gb, btw[...], preferred_element_type=F32)
        dh_ref[...] = _reorder(unperm[...], dh)
        tn = (((0,), (0,)), ((), ()))
        dbacc[...] += lax.dot_general(hp.astype(BF), gb, tn, preferred_element_type=F32)
        dcacc[...] += lax.dot_general(dyb, s_ref[...].astype(BF), tn, preferred_element_type=F32)

        @pl.when(t == nt - 1)
        def _():
            db_ref[...] = _extract(dbacc[...], dims)
            dc_ref[...] = _extract(dcacc[...], dims)

    rev = lambda kb, t: (nt - 1 - t, kb)
    prev = lambda kb, t: (jnp.maximum((nt - 1 - t) * CH - 1, 0), kb)
    per_kb = lambda kb, t: (kb, 0, 0)
    return pl.pallas_call(
        body, name=name, grid=(nkb, nt),
        in_specs=[pl.BlockSpec((TL, KB), rev), pl.BlockSpec((TL, KB), rev), pl.BlockSpec((TL, KB), rev),
                  pl.BlockSpec((TL, W2), rev), pl.BlockSpec((8, W2), prev),
                  pl.BlockSpec((None, KB, P2), per_kb), pl.BlockSpec((None, P2, KB), per_kb),
                  pl.BlockSpec((None, 40, W2), per_kb), pl.BlockSpec((1, KB), lambda kb, t: (0, kb))],
        out_specs=[pl.BlockSpec((TL, KB), rev), pl.BlockSpec((8, KB), lambda kb, t: (0, kb)),
                   pl.BlockSpec((None, 8, W2), per_kb), pl.BlockSpec((None, KB, P2), per_kb),
                   pl.BlockSpec((None, KB, P2), per_kb)],
        out_shape=[jax.ShapeDtypeStruct((L, D), F32), jax.ShapeDtypeStruct((8, D), F32),
                   jax.ShapeDtypeStruct((nkb, 8, W2), F32), jax.ShapeDtypeStruct((nkb, KB, P2), F32),
                   jax.ShapeDtypeStruct((nkb, KB, P2), F32)],
        scratch_shapes=[pltpu.VMEM((TL, W2), F32), pltpu.VMEM((KB, W2), BF), pltpu.VMEM((W2, KB), BF),
                        pltpu.VMEM((KB, W2), F32), pltpu.VMEM((KB, W2), F32), pltpu.VMEM((TL, TL), BF),
                        pltpu.VMEM((TL, TL), BF), pltpu.VMEM((2, 8, W), F32)],
        compiler_params=_cp("parallel", "arbitrary"))(dz, y, h, s, s, tc, tbt, pwr, dvec)


def _discretise(a_re, a_im, log_step, b_re, b_im):
    lr = jnp.minimum(a_re, -1e-4)
    li = a_im
    dt = jnp.exp(log_step)[:, None]
    mag = jnp.exp(lr * dt)
    abr = mag * jnp.cos(li * dt)
    abi = mag * jnp.sin(li * dt)
    den = lr * lr + li * li
    qr = ((abr - 1.0) * lr + abi * li) / den
    qi = (abi * lr - (abr - 1.0) * li) / den
    bbar_re = qr[..., None] * b_re - qi[..., None] * b_im
    bbar_im = qr[..., None] * b_im + qi[..., None] * b_re
    return abr, abi, bbar_re, bbar_im


def _compact(m_re, m_im, nkb):
    G, H, P = m_re.shape
    t = jnp.stack([m_re, m_im], axis=2).reshape(nkb, (G // nkb) * H, 2 * P).astype(BF)
    return t, jnp.swapaxes(t, 1, 2)


def _scan_powers(abr, abi, nkb, conj, CH):
    G, P = abr.shape
    if conj:
        abi = -abi

    def cmul(u, v):
        return u[0] * v[0] - u[1] * v[1], u[0] * v[1] + u[1] * v[0]

    q = (abr, abi)
    for _ in range(_log2(CH)):
        q = cmul(q, q)
    pows = [q]
    for _ in range(7):
        pows.append(cmul(pows[-1], q))
    row = jnp.arange(8)[:, None, None]

    def table(part):
        out = [jnp.broadcast_to((abr, abi)[part][None], (8, G, P))]
        for k in (1, 2, 4):
            keep = (row <= 7 - k) if conj else (row >= k)
            out.append(jnp.where(keep, pows[k - 1][part][None], 0.0))
        ends = jnp.stack([p[part] for p in pows])
        out.append(ends[::-1] if conj else ends)
        return jnp.concatenate(out, axis=0)

    GL = G // nkb
    t = jnp.stack([table(0), table(1)], axis=1)
    t = t.reshape(40, 2, nkb, GL * P).transpose(2, 0, 1, 3)
    return t.reshape(nkb, 40, 2 * GL * P)


def ada_mods(c_all, w_ada, b_sh, name):
    nl, D, NA = w_ada.shape

    def body(c_ref, w_ref, b_ref, o_ref):
        cv = c_ref[...]
        act = cv * jax.nn.sigmoid(cv)
        o_ref[...] = jnp.dot(act, w_ref[...], preferred_element_type=F32, precision=lax.Precision.HIGHEST) + b_ref[...]

    return pl.pallas_call(
        body, name=name, grid=(nl,),
        in_specs=[pl.BlockSpec((8, D), lambda i: (0, 0)), pl.BlockSpec((None, D, NA), lambda i: (i, 0, 0)),
                  pl.BlockSpec((None, 1, NA), lambda i: (i, 0, 0))],
        out_specs=pl.BlockSpec((None, 8, NA), lambda i: (i, 0, 0)),
        out_shape=jax.ShapeDtypeStruct((nl, 8, NA), F32), compiler_params=_cp("parallel"))(c_all, w_ada, b_sh)


def _adamw(w, g, m, v):
    m = ADAM_B1 * m + (1.0 - ADAM_B1) * g
    v = ADAM_B2 * v + (1.0 - ADAM_B2) * (g * g)
    m_hat = m / (1.0 - ADAM_B1 ** ADAM_STEP)
    v_hat = v / (1.0 - ADAM_B2 ** ADAM_STEP)
    return -ADAM_LR * (m_hat / (jnp.sqrt(v_hat) + ADAM_EPS) + ADAM_WD * w), m, v


def _adam_rows(R, C):
    cap = max(8, (256 * 1024) // C)
    for t in range(min(R, cap), 0, -1):
        if R % t == 0 and (t % 8 == 0 or t == R):
            return t
    return R


def adamw_ada(c_t, dm, w, m, v, name):
    nl, D, NA = w.shape
    TK = _tile(D, (256, 128))

    def body(c_ref, dm_ref, w_ref, m_ref, v_ref, g_ref, d_ref, nm_ref, nv_ref):
        cv = c_ref[...]
        act = cv * jax.nn.sigmoid(cv)
        g = jnp.dot(act, dm_ref[...], preferred_element_type=F32, precision=lax.Precision.HIGHEST)
        g_ref[...] = g
        d_ref[...], nm_ref[...], nv_ref[...] = _adamw(w_ref[...], g, m_ref[...], v_ref[...])

    big = pl.BlockSpec((None, TK, NA), lambda i, k: (i, k, 0))
    shape = jax.ShapeDtypeStruct(w.shape, F32)
    return pl.pallas_call(
        body, name=name, grid=(nl, D // TK),
        in_specs=[pl.BlockSpec((TK, 8), lambda i, k: (k, 0)), pl.BlockSpec((None, 8, NA), lambda i, k: (i, 0, 0)),
                  big, big, big],
        out_specs=[big] * 4, out_shape=[shape] * 4, compiler_params=_cp("parallel", "parallel"))(c_t, dm, w, m, v)


def adamw_sharded(w, m, v, ga, gb, name):
    nl, R, C = w.shape
    TR = _adam_rows(R, C)

    def body(w_ref, m_ref, v_ref, a_ref, b_ref, g_ref, d_ref, nm_ref, nv_ref):
        g = a_ref[...] + b_ref[...]
        g_ref[...] = g
        d_ref[...], nm_ref[...], nv_ref[...] = _adamw(w_ref[...], g, m_ref[...], v_ref[...])

    big = pl.BlockSpec((None, TR, C), lambda i, r: (i, r, 0))
    shape = jax.ShapeDtypeStruct(w.shape, F32)
    return pl.pallas_call(
        body, name=name, grid=(nl, R // TR), in_specs=[big] * 5,
        out_specs=[big] * 4, out_shape=[shape] * 4, compiler_params=_cp("parallel", "parallel"))(w, m, v, ga, gb)


def adamw_slab(g, w, m, v, name):
    R, C = g.shape
    TR = _tile(R, (160, 80, 40, 8))

    def body(g_ref, w_ref, m_ref, v_ref, d_ref, nm_ref, nv_ref):
        d_ref[...], nm_ref[...], nv_ref[...] = _adamw(w_ref[...], g_ref[...], m_ref[...], v_ref[...])

    big = pl.BlockSpec((TR, C), lambda r: (r, 0))
    shape = jax.ShapeDtypeStruct((R, C), F32)
    return pl.pallas_call(
        body, name=name, grid=(R // TR,), in_specs=[big] * 4,
        out_specs=[big] * 3, out_shape=[shape] * 3, compiler_params=_cp("parallel"))(g, w, m, v)


def adamw_plain(w, m, v, g, name):
    def body(w_ref, m_ref, v_ref, g_ref, d_ref, nm_ref, nv_ref):
        d_ref[...], nm_ref[...], nv_ref[...] = _adamw(w_ref[...], g_ref[...], m_ref[...], v_ref[...])

    shape = jax.ShapeDtypeStruct(w.shape, F32)
    return pl.pallas_call(body, name=name, out_shape=[shape] * 3,
                          compiler_params=pltpu.CompilerParams(vmem_limit_bytes=VMEM_LIMIT))(w, m, v, g)


def _slab_rows(a):
    n = a.size
    rows = -(-n // SLAB_W)
    return -(-rows // 8) * 8


def _pack(arrs, pad_rows_to=0):
    out = []
    for a in arrs:
        rows = _slab_rows(a)
        flat = a.reshape(-1).astype(F32)
        flat = jnp.pad(flat, (0, rows * SLAB_W - flat.shape[0]))
        out.append(flat.reshape(rows, SLAB_W))
    total = sum(o.shape[0] for o in out)
    if pad_rows_to and total % pad_rows_to:
        out.append(jnp.zeros((pad_rows_to - total % pad_rows_to, SLAB_W), F32))
    return jnp.concatenate(out, axis=0)


def _unpack(slab, like):
    out, r = [], 0
    for a in like:
        rows = _slab_rows(a)
        out.append(slab[r:r + rows].reshape(-1)[:a.size].reshape(a.shape))
        r += rows
    return out


WEIGHTS = ['norm1_g', 'norm2_g', 'w_ada', 'b_ada', 'ssm_a_re', 'ssm_a_im', 'ssm_log_step', 'ssm_b_re', 'ssm_b_im',
           'ssm_c_re', 'ssm_c_im', 'ssm_d', 'ssm_w_out', 'conv_w_in', 'conv_w', 'conv_w_out', 'w_ffn_in',
           'w_ffn_out', 'final_g']
SLAB = ['norm1_g', 'norm2_g', 'b_ada', 'ssm_a_re', 'ssm_a_im', 'ssm_log_step', 'ssm_b_re', 'ssm_b_im', 'ssm_c_re',
        'ssm_c_im', 'ssm_d', 'final_g']
SHARDED = ['ssm_w_out', 'conv_w_in', 'conv_w_out', 'w_ffn_in', 'w_ffn_out']


def kernel(x, c, norm1_g, norm2_g, w_ada, b_ada, ssm_a_re, ssm_a_im, ssm_log_step, ssm_b_re, ssm_b_im, ssm_c_re, ssm_c_im, ssm_d, ssm_w_out, conv_w_in, conv_w, conv_w_out, w_ffn_in, w_ffn_out, final_g, loss_target, m_norm1_g, m_norm2_g, m_w_ada, m_b_ada, m_ssm_a_re, m_ssm_a_im, m_ssm_log_step, m_ssm_b_re, m_ssm_b_im, m_ssm_c_re, m_ssm_c_im, m_ssm_d, m_ssm_w_out, m_conv_w_in, m_conv_w, m_conv_w_out, m_w_ffn_in, m_w_ffn_out, m_final_g, v_norm1_g, v_norm2_g, v_w_ada, v_b_ada, v_ssm_a_re, v_ssm_a_im, v_ssm_log_step, v_ssm_b_re, v_ssm_b_im, v_ssm_c_re, v_ssm_c_im, v_ssm_d, v_ssm_w_out, v_conv_w_in, v_conv_w, v_conv_w_out, v_w_ffn_in, v_w_ffn_out, v_final_g):
    given = dict(locals())
    W = {n: given[n] for n in WEIGHTS}
    Mo = {n: given["m_" + n] for n in WEIGHTS}
    Vo = {n: given["v_" + n] for n in WEIGHTS}

    xs = x[0]
    tgt = loss_target[0]
    L, D = xs.shape
    nlayer = norm1_g.shape[0]
    NA = w_ada.shape[2]
    G = ssm_a_re.shape[1]
    nkb = D // S5_BLOCK
    ax, ay, ac = _axes()
    me = 4 * ax + 2 * ay + ac
    chip = 2 * ax + ay

    c_all = gather8(jnp.broadcast_to(c, (8, D)), "gather_c")[:, 0, :]
    b_sh = lax.dynamic_slice_in_dim(b_ada, chip * NA, NA, axis=1)[:, None, :]
    mods_part = ada_mods(c_all, w_ada, b_sh, "ada_mods")
    mg = gather8(mods_part.reshape(nlayer * 8, NA), "gather_mods")
    mg = mg.reshape(N_CHIP, 2, nlayer, 8, NA)[:, 0]
    mods_all = lax.dynamic_index_in_dim(mg, me, axis=2, keepdims=False)
    mods_all = jnp.transpose(mods_all, (1, 0, 2)).reshape(nlayer, 6, D)

    cw_parts = gather8(_pack([conv_w]), "gather_conv_w")
    nconv = conv_w.shape[0]
    cw_full = jnp.stack([_unpack(cw_parts[2 * q], [conv_w])[0] for q in range(N_CHIP)], axis=2)
    cw_full = cw_full.reshape(nconv, 3, D)

    use = []
    for i in range(nlayer):
        use += [("ssm_w_out", i // 2, i)] if i % 2 == 0 else [("conv_w_in", i // 2, i), ("conv_w_out", i // 2, i)]
        use += [("w_ffn_in", i, i), ("w_ffn_out", i, i)]
    g_sems, g_srcs, g_lands, token = gather_start([W[n][j].astype(BF) for n, j, _ in use],
                                                  cw_full + mods_all[0, 0:3], "gather_start")
    mods_all = mods_all + token[0:1, 0:1]

    def layer_weights(i, after):
        idx = [a for a, (_, _, li) in enumerate(use) if li == i]
        got = gather_wait(g_sems, g_srcs, g_lands, idx, after, "gather_wait%d" % i)
        return {use[a][0]: w for a, w in zip(idx, got)}

    s5 = []
    for j in range(ssm_a_re.shape[0]):
        disc, disc_vjp = jax.vjp(_discretise, ssm_a_re[j], ssm_a_im[j], ssm_log_step[j], ssm_b_re[j], ssm_b_im[j])
        abr, abi, bbar_re, bbar_im = disc
        tb, tbt = _compact(jnp.swapaxes(bbar_re, 1, 2), jnp.swapaxes(bbar_im, 1, 2), nkb)
        tc, tct = _compact(ssm_c_re[j], -ssm_c_im[j], nkb)
        chunk = _tile(L, (512, 256)) // 8
        s5.append(dict(vjp=disc_vjp, tb=tb, tbt=tbt, tc=tc, tct=tct, pw=_scan_powers(abr, abi, nkb, False, chunk),
                       pwr=_scan_powers(abr, abi, nkb, True, chunk)))

    saved = []
    xcur = xs
    for i in range(nlayer):
        j = i // 2
        mods = mods_all[i]
        sv = dict(x=xcur)
        if i % 2 == 0:
            h = norm_mod(xcur, norm1_g[i:i + 1], mods, 0, F32, "norm_mod_s5")
            states, yv, z = s5_fwd(h, s5[j]["tb"], s5[j]["tct"], s5[j]["pw"], ssm_d[j:j + 1], "s5_fwd")
            full = layer_weights(i, z)
            o, mix, x2 = ssm_out_glu(z, full["ssm_w_out"], xcur, mods, 2, "ssm_out_glu")
            sv.update(h=h, states=states, y=yv, z=z, o=o)
        else:
            h = norm_mod(xcur, norm1_g[i:i + 1], mods, 0, BF, "norm_mod")
            full = layer_weights(i, h)
            p = mm_nn(h, full["conv_w_in"], BF, "mm_conv_in")
            mc = conv_fwd(p, cw_full[j], "conv_fwd")
            mix, x2 = mm_nn(mc, full["conv_w_out"].reshape(1, D, D), BF, "mm_conv_out", res=xcur, gate=mods[2:3])
            sv.update(h=h, p=p, mc=mc)
        h2 = norm_mod(x2, norm2_g[i:i + 1], mods, 3, BF, "norm_mod")
        gu, act = ffn_in_act(h2, full["w_ffn_in"], "ffn_in_act")
        F = act.shape[1]
        ff, x3 = mm_nn(act, full["w_ffn_out"].reshape(1, F, D), BF, "mm_ffn_out", res=x2, gate=mods[5:6])
        sv.update(mix=mix, x2=x2, h2=h2, gu=gu, act=act, ff=ff, w=full)
        saved.append(sv)
        xcur = x3

    loss_blk, dx, dfinal, dff = final_loss(xcur, tgt, final_g[None, :], saved[-1]["ff"], mods_all[nlayer - 1], 5,
                                           "final_loss")
    dg2 = dfinal[1:2]

    gland = {n: lax.empty((W[n].shape[0], N_CHIP) + W[n].shape[1:], BF) for n in SHARDED}
    in_flight = []
    dmods = [None] * nlayer
    dnorm1, dnorm2 = [None] * nlayer, [None] * nlayer
    dconv_w = [None] * nconv
    ds5 = [None] * ssm_a_re.shape[0]
    token = jnp.zeros((8, 128), F32)

    def send_grads(names, grads, slot, after, name):
        sems, thru, lands, tok = scatter_start([grads[n] for n in names], [gland[n] for n in names], slot, after, name)
        gland.update(zip(names, lands))
        in_flight.append((names, slot, sems, thru, name))
        return tok

    def land_grads(group, after):
        for names, slot, sems, thru, name in in_flight:
            if names[0] in group:
                got = scatter_wait(sems, thru, [gland[n] for n in names], slot, after, name.replace("scatter", "landed"))
                gland.update(zip(names, got))

    for i in reversed(range(nlayer)):
        j = i // 2
        mods = mods_all[i] + token[0:1, 0:1]
        sv = saved[i]
        full = sv["w"]
        gfull = {}
        F = sv["act"].shape[1]
        gfull["w_ffn_out"] = mm_tn(sv["act"], dff, 1, "mm_tn_ffn_out").reshape(N_CHIP, F // N_CHIP, D)
        dgu = ffn_out_bwd(dff, full["w_ffn_out"].reshape(F, D), sv["gu"], "ffn_out_bwd")
        gfull["w_ffn_in"] = mm_tn(sv["h2"], dgu, N_CHIP, "mm_tn_ffn_in")
        dh2 = mm_nt(dgu, full["w_ffn_in"], F32, "mm_nt_ffn_in")
        token = send_grads(["w_ffn_out", "w_ffn_in"], gfull, [i, i], dh2, "scatter_ffn%d" % i)
        mods = mods + token[0:1, 0:1]
        dx2, s2, dmix = norm_bwd(dh2, sv["x2"], dx, norm2_g[i:i + 1], mods, 3, "norm_bwd_mix",
                                 branch=(sv["mix"], mods, 2))
        dg1 = s2[3:4]
        if i % 2 == 0:
            do = glu_bwd(dmix, sv["o"], "glu_bwd")
            gfull["ssm_w_out"] = mm_tn(sv["z"], do, N_CHIP, "mm_tn_ssm_out")
            dz = mm_nt(do, full["ssm_w_out"], BF, "mm_nt_ssm_out")
            dh, dd, dab, db, dc = s5_bwd(dz, sv["y"], sv["h"], sv["states"], s5[j]["tc"], s5[j]["tbt"], s5[j]["pwr"],
                                         ssm_d[j:j + 1], "s5_bwd")
            ds5[j] = (dd, dab, db, dc)
        else:
            gfull["conv_w_out"] = mm_tn(sv["mc"], dmix, 1, "mm_tn_conv_out").reshape(N_CHIP, D // N_CHIP, D)
            dmc = mm_nt(dmix, full["conv_w_out"].reshape(1, D, D), BF, "mm_nt_conv_out")
            dbg, dcg, dvv, dcw = conv_bwd(dmc, sv["p"], cw_full[j], "conv_bwd")
            dp = jnp.concatenate([dbg, dcg, dvv], axis=1)
            gfull["conv_w_in"] = mm_tn(sv["h"], dp, N_CHIP, "mm_tn_conv_in")
            dh = mm_nt(dp, full["conv_w_in"], F32, "mm_nt_conv_in")
            dconv_w[j] = dcw[0:3]
        dmods_i = [s2[0:2], dg2]
        if i > 0:
            dx, s1, dff = norm_bwd(dh, sv["x"], dx2, norm1_g[i:i + 1], mods, 0, "norm_bwd_ffn",
                                   branch=(saved[i - 1]["ff"], mods_all[i - 1], 5))
            dg2 = s1[3:4]
        else:
            dx, s1 = norm_bwd(dh, sv["x"], dx2, norm1_g[i:i + 1], mods, 0, "norm_bwd")
        dmods[i] = jnp.concatenate([s1[0:2], dg1] + dmods_i, axis=0).reshape(6 * D)
        dnorm1[i], dnorm2[i] = s1[2], s2[2]
        names = ["ssm_w_out"] if i % 2 == 0 else ["conv_w_out", "conv_w_in"]
        token = send_grads(names, gfull, [j] * len(names), dx, "scatter_mix%d" % i)

    small = dict(norm1_g=jnp.stack(dnorm1), norm2_g=jnp.stack(dnorm2), b_ada=jnp.stack(dmods), final_g=dfinal[0])
    per = {n: [] for n in ('ssm_a_re', 'ssm_a_im', 'ssm_log_step', 'ssm_b_re', 'ssm_b_im', 'ssm_c_re', 'ssm_c_im', 'ssm_d')}
    GL = G // nkb
    for j, (dd, dab, db, dc) in enumerate(ds5):
        dab = jnp.sum(dab, axis=1).reshape(nkb, 2, GL, SSM_STATE)
        g_abr, g_abi = dab[:, 0].reshape(G, SSM_STATE), dab[:, 1].reshape(G, SSM_STATE)
        db, dc = db.reshape(G, SSM_GROUP, 2, SSM_STATE), dc.reshape(G, SSM_GROUP, 2, SSM_STATE)
        gb_re, gb_im, gc_re, gc_im = db[:, :, 0], db[:, :, 1], dc[:, :, 0], dc[:, :, 1]
        ga_re, ga_im, gls, gbr, gbi = s5[j]["vjp"]((g_abr, g_abi, jnp.swapaxes(gb_re, 1, 2), jnp.swapaxes(gb_im, 1, 2)))
        for n, val in zip(per, (ga_re, ga_im, gls, gbr, gbi, gc_re, -gc_im, jnp.sum(dd, axis=0))):
            per[n].append(val)
    small.update({n: jnp.stack(vals) for n, vals in per.items()})
    dcw_full = jnp.stack(dconv_w)

    slab_like = [W[n] for n in SLAB] + [dcw_full]
    rows64 = 8 * N_DEV
    g_slab, dm_all = reduce8(_pack([small[n] for n in SLAB] + [dcw_full], rows64), _pack([small["b_ada"]]),
                             "reduce_small")
    d_slab, m_slab, v_slab = adamw_slab(
        g_slab, _pack([W[n] for n in SLAB] + [jnp.zeros_like(dcw_full)], rows64),
        _pack([Mo[n] for n in SLAB] + [jnp.zeros_like(dcw_full)], rows64),
        _pack([Vo[n] for n in SLAB] + [jnp.ones_like(dcw_full)], rows64), "adamw_slab")
    out = {}
    for k, slab in zip(("g", "d", "m", "v"), (g_slab, d_slab, m_slab, v_slab)):
        for n, val in zip(SLAB, _unpack(slab, slab_like)):
            out[k, n] = val
    g_cw = lax.dynamic_slice_in_dim(_unpack(g_slab, slab_like)[-1], chip * conv_w.shape[2], conv_w.shape[2], axis=2)
    out["g", "conv_w"] = g_cw
    out["d", "conv_w"], out["m", "conv_w"], out["v", "conv_w"] = [
        r.reshape(conv_w.shape) for r in adamw_plain(conv_w.reshape(-1, conv_w.shape[2]), m_conv_w.reshape(-1, conv_w.shape[2]),
                                                     v_conv_w.reshape(-1, conv_w.shape[2]), g_cw.reshape(-1, conv_w.shape[2]),
                                                     "adamw_conv_w")]

    early = [n for n in SHARDED if n != "ssm_w_out"]
    land_grads(early, g_slab)
    mine = [reduce4(gland[n], "reduce4_" + n) for n in early]
    w_sems, w_srcs, w_lands, token = swap_start(mine, "swap_start")

    dm_all = dm_all.reshape(N_DEV, -1)[:, :b_ada.size].reshape(N_DEV, nlayer, N_CHIP, NA)
    dm_sh = jnp.transpose(lax.dynamic_index_in_dim(dm_all, chip, axis=2, keepdims=False), (1, 0, 2))
    res = adamw_ada(jnp.transpose(c_all) + token[0:1, 0:1], dm_sh, w_ada, m_w_ada, v_w_ada, "adamw_ada")
    out["g", "w_ada"], out["d", "w_ada"], out["m", "w_ada"], out["v", "w_ada"] = res

    mine, theirs = swap_wait(w_sems, w_srcs, w_lands, out["g", "w_ada"], "swap_wait")
    for n, ga, gb in zip(early, mine, theirs):
        r = adamw_sharded(W[n], Mo[n], Vo[n], ga, gb, "adamw_" + n)
        out["g", n], out["d", n], out["m", n], out["v", n] = r

    land_grads(["ssm_w_out"], out["g", "w_ffn_out"])
    ga = reduce4(gland["ssm_w_out"], "reduce4_ssm_w_out")
    gb = swap_siblings([ga], "swap_siblings")[0]
    r = adamw_sharded(ssm_w_out, m_ssm_w_out, v_ssm_w_out, ga, gb, "adamw_ssm_w_out")
    out["g", "ssm_w_out"], out["d", "ssm_w_out"], out["m", "ssm_w_out"], out["v", "ssm_w_out"] = r

    loss = lax.psum(loss_blk[0, 0], ("x", "y", "c"))
    return (loss, dx[None], *[out["g", n] for n in WEIGHTS], *[out["d", n] for n in WEIGHTS],
            *[out["m", n] for n in WEIGHTS], *[out["v", n] for n in WEIGHTS])
```

```python
import functools
import math

import jax
import jax.numpy as jnp
from jax import lax
from jax.experimental import pallas as pl
from jax.experimental.pallas import tpu as pltpu

F32 = jnp.float32
BF = jnp.bfloat16
MESH = pl.DeviceIdType.MESH
ANY = pl.BlockSpec(memory_space=pl.ANY)

N_DEV = 8
N_CHIP = 4
DEPTH = 4
SSM_GROUP = 16
SSM_STATE = 64
S5_BLOCK = 256
RMS_EPS = 1e-6
ADAM_LR, ADAM_B1, ADAM_B2, ADAM_EPS, ADAM_WD, ADAM_STEP = 0.001, 0.9, 0.999, 1e-08, 0.01, 10
V7X_VMEM_BYTES = 64 * 1024 * 1024
VMEM_LIMIT = V7X_VMEM_BYTES - 12 * 1024 * 1024
SLAB_W = 1024
GELU_C = math.sqrt(2.0 / math.pi)
GELU_A = 0.044715


def _cp(*sem):
    return pltpu.CompilerParams(dimension_semantics=sem if sem else None, vmem_limit_bytes=VMEM_LIMIT)


def _tile(n, prefs):
    for p in prefs:
        if p <= n and n % p == 0:
            return p
    return n


def _axes():
    return lax.axis_index("x"), lax.axis_index("y"), lax.axis_index("c")


def _flip(v, k):
    return 1 - v if k else v


def gather8(v, name):
    R, C = v.shape

    def body(v_ref, o_ref, ssem, rsem, lsem):
        x, y, c = _axes()
        me = 4 * x + 2 * y + c
        loc = pltpu.make_async_copy(v_ref, o_ref.at[me], lsem)
        loc.start()
        copies = []
        for k in range(1, N_DEV):
            peer = (_flip(x, (k >> 2) & 1), _flip(y, (k >> 1) & 1), _flip(c, k & 1))
            cp = pltpu.make_async_remote_copy(src_ref=v_ref, dst_ref=o_ref.at[me], send_sem=ssem.at[k - 1],
                                              recv_sem=rsem.at[k - 1], device_id=peer, device_id_type=MESH)
            cp.start()
            copies.append(cp)
        for cp in copies:
            cp.wait()
        loc.wait()

    return pl.pallas_call(
        body, name=name,
        out_shape=jax.ShapeDtypeStruct((N_DEV, R, C), v.dtype),
        in_specs=[pl.BlockSpec(memory_space=pltpu.VMEM)],
        out_specs=pl.BlockSpec(memory_space=pltpu.VMEM),
        scratch_shapes=[pltpu.SemaphoreType.DMA((N_DEV - 1,)), pltpu.SemaphoreType.DMA((N_DEV - 1,)),
                        pltpu.SemaphoreType.DMA],
        compiler_params=pltpu.CompilerParams(vmem_limit_bytes=VMEM_LIMIT),
    )(v)


HBM = pl.BlockSpec(memory_space=pltpu.HBM)
SEM = pl.BlockSpec(memory_space=pltpu.SEMAPHORE)
EFFECT = pltpu.SideEffectType.DATAFLOW_SIDE_EFFECTING


def _in_hbm(a):
    return pltpu.with_memory_space_constraint(a, pltpu.HBM)


def _chip_peers(x, y, c):
    out = []
    for k in range(1, N_CHIP):
        px, py = _flip(x, k >> 1), _flip(y, k & 1)
        out.append(((px, py, c), 2 * px + py))
    return out


def gather_start(shards, after, name):
    n = len(shards)

    def body(*refs):
        src, land = refs[:n], refs[n:2 * n]
        ssem, rsem, lsem = refs[2 * n + 1:2 * n + 4]
        token = refs[-1]
        x, y, c = _axes()
        chip = 2 * x + y
        for a in range(n):
            pltpu.make_async_copy(src[a], land[a].at[chip], lsem.at[a]).start()
            for k, (peer, _) in enumerate(_chip_peers(x, y, c)):
                pltpu.make_async_remote_copy(src_ref=src[a], dst_ref=land[a].at[chip], send_sem=ssem.at[3 * a + k],
                                             recv_sem=rsem.at[3 * a + k], device_id=peer, device_id_type=MESH).start()
        token[...] = jnp.zeros_like(token)

    lands = [lax.empty((N_CHIP,) + s.shape, s.dtype) for s in shards]
    out_shape = ([pltpu.SemaphoreType.DMA((3 * n,)), pltpu.SemaphoreType.DMA((3 * n,)), pltpu.SemaphoreType.DMA((n,))]
                 + [pltpu.HBM(s.shape, s.dtype) for s in shards] + [pltpu.HBM(l.shape, l.dtype) for l in lands]
                 + [jax.ShapeDtypeStruct((8, 128), F32)])
    res = pl.pallas_call(
        body, name=name, out_shape=out_shape, in_specs=[HBM] * (2 * n) + [ANY],
        out_specs=[SEM, SEM, SEM] + [HBM] * (2 * n) + [pl.BlockSpec(memory_space=pltpu.VMEM)],
        input_output_aliases={a: 3 + a for a in range(2 * n)},
        compiler_params=pltpu.CompilerParams(has_side_effects=EFFECT),
    )(*[_in_hbm(s) for s in shards], *[_in_hbm(l) for l in lands], after)
    return tuple(res[:3]), list(res[3:3 + n]), list(res[3 + n:3 + 2 * n]), res[-1]


def gather_wait(sems, srcs, lands, idx, after, name):
    m = len(idx)

    def body(*refs):
        src, land = refs[:m], refs[m:2 * m]
        ssem, rsem, lsem = refs[2 * m:2 * m + 3]
        x, y, c = _axes()
        chip = 2 * x + y
        for j, a in enumerate(idx):
            for k, (peer, pchip) in enumerate(_chip_peers(x, y, c)):
                cp = pltpu.make_async_remote_copy(src_ref=src[j], dst_ref=land[j].at[pchip], send_sem=ssem.at[3 * a + k],
                                                  recv_sem=rsem.at[3 * a + k], device_id=peer, device_id_type=MESH)
                cp.wait_send()
                cp.wait_recv()
            pltpu.make_async_copy(src[j], land[j].at[chip], lsem.at[a]).wait()

    s_in = [srcs[a] for a in idx]
    l_in = [lands[a] for a in idx]
    res = pl.pallas_call(
        body, name=name,
        out_shape=[pltpu.HBM(s.shape, s.dtype) for s in s_in] + [pltpu.HBM(l.shape, l.dtype) for l in l_in],
        in_specs=[HBM] * (2 * m) + [SEM, SEM, SEM, ANY], out_specs=[HBM] * (2 * m),
        input_output_aliases={a: a for a in range(2 * m)},
        compiler_params=pltpu.CompilerParams(has_side_effects=EFFECT),
    )(*s_in, *l_in, *sems, after)
    return list(res[m:])


def scatter_start(grads, lands, slot, after, name):
    n = len(grads)

    def body(*refs):
        src, land = refs[:n], refs[n:2 * n]
        ssem, rsem, lsem = refs[2 * n + 1:2 * n + 4]
        token = refs[-1]
        x, y, c = _axes()
        chip = 2 * x + y
        for a in range(n):
            pltpu.make_async_copy(src[a].at[chip], land[a].at[slot[a], chip], lsem.at[a]).start()
            for k, (peer, pchip) in enumerate(_chip_peers(x, y, c)):
                pltpu.make_async_remote_copy(src_ref=src[a].at[pchip], dst_ref=land[a].at[slot[a], chip],
                                             send_sem=ssem.at[3 * a + k], recv_sem=rsem.at[3 * a + k],
                                             device_id=peer, device_id_type=MESH).start()
        token[...] = jnp.zeros_like(token)

    out_shape = ([pltpu.SemaphoreType.DMA((3 * n,)), pltpu.SemaphoreType.DMA((3 * n,)), pltpu.SemaphoreType.DMA((n,))]
                 + [pltpu.HBM(g.shape, g.dtype) for g in grads] + [pltpu.HBM(l.shape, l.dtype) for l in lands]
                 + [jax.ShapeDtypeStruct((8, 128), F32)])
    res = pl.pallas_call(
        body, name=name, out_shape=out_shape, in_specs=[HBM] * (2 * n) + [ANY],
        out_specs=[SEM, SEM, SEM] + [HBM] * (2 * n) + [pl.BlockSpec(memory_space=pltpu.VMEM)],
        input_output_aliases={a: 3 + a for a in range(2 * n)},
        compiler_params=pltpu.CompilerParams(has_side_effects=EFFECT),
    )(*[_in_hbm(g) for g in grads], *[_in_hbm(l) for l in lands], after)
    return tuple(res[:3]), list(res[3:3 + n]), list(res[3 + n:3 + 2 * n]), res[-1]


def scatter_wait(sems, grads, lands, slot, after, name):
    n = len(grads)

    def body(*refs):
        src, land = refs[:n], refs[n:2 * n]
        ssem, rsem, lsem = refs[2 * n:2 * n + 3]
        x, y, c = _axes()
        chip = 2 * x + y
        for a in range(n):
            for k, (peer, pchip) in enumerate(_chip_peers(x, y, c)):
                cp = pltpu.make_async_remote_copy(src_ref=src[a].at[pchip], dst_ref=land[a].at[slot[a], pchip],
                                                  send_sem=ssem.at[3 * a + k], recv_sem=rsem.at[3 * a + k],
                                                  device_id=peer, device_id_type=MESH)
                cp.wait_send()
                cp.wait_recv()
            pltpu.make_async_copy(src[a].at[chip], land[a].at[slot[a], chip], lsem.at[a]).wait()

    res = pl.pallas_call(
        body, name=name,
        out_shape=[pltpu.HBM(g.shape, g.dtype) for g in grads] + [pltpu.HBM(l.shape, l.dtype) for l in lands],
        in_specs=[HBM] * (2 * n) + [SEM, SEM, SEM, ANY], out_specs=[HBM] * (2 * n),
        input_output_aliases={a: a for a in range(2 * n)},
        compiler_params=pltpu.CompilerParams(has_side_effects=EFFECT),
    )(*grads, *lands, *sems, after)
    return list(res[n:])


def reduce4(land, name):
    nl, _, R, C = land.shape
    TR = _adam_rows(R, C)

    def body(l_ref, o_ref):
        o_ref[...] = ((l_ref[0].astype(F32) + l_ref[1].astype(F32)) + l_ref[2].astype(F32)) + l_ref[3].astype(F32)

    return pl.pallas_call(
        body, name=name, grid=(nl, R // TR),
        in_specs=[pl.BlockSpec((None, N_CHIP, TR, C), lambda i, r: (i, 0, r, 0))],
        out_specs=pl.BlockSpec((None, TR, C), lambda i, r: (i, r, 0)),
        out_shape=jax.ShapeDtypeStruct((nl, R, C), F32), compiler_params=_cp("parallel", "parallel"))(land)


def swap_siblings(arrs, name):
    n = len(arrs)

    def body(*refs):
        src, dst = refs[:n], refs[n:2 * n]
        ssem, rsem = refs[2 * n:]
        x, y, c = _axes()
        cps = [pltpu.make_async_remote_copy(src_ref=src[a], dst_ref=dst[a], send_sem=ssem.at[a], recv_sem=rsem.at[a],
                                            device_id=(x, y, 1 - c), device_id_type=MESH) for a in range(n)]
        for cp in cps:
            cp.start()
        for cp in cps:
            cp.wait()

    return pl.pallas_call(
        body, name=name, out_shape=[jax.ShapeDtypeStruct(a.shape, a.dtype) for a in arrs],
        in_specs=[ANY] * n, out_specs=[ANY] * n,
        scratch_shapes=[pltpu.SemaphoreType.DMA((n,)), pltpu.SemaphoreType.DMA((n,))],
        compiler_params=pltpu.CompilerParams(vmem_limit_bytes=VMEM_LIMIT),
    )(*arrs)


def swap_start(arrs, name):
    n = len(arrs)

    def body(*refs):
        src, land = refs[:n], refs[n:2 * n]
        ssem, rsem = refs[2 * n:2 * n + 2]
        token = refs[-1]
        x, y, c = _axes()
        for a in range(n):
            pltpu.make_async_remote_copy(src_ref=src[a], dst_ref=land[a], send_sem=ssem.at[a], recv_sem=rsem.at[a],
                                         device_id=(x, y, 1 - c), device_id_type=MESH).start()
        token[...] = jnp.zeros_like(token)

    lands = [lax.empty(a.shape, a.dtype) for a in arrs]
    out_shape = ([pltpu.SemaphoreType.DMA((n,)), pltpu.SemaphoreType.DMA((n,))]
                 + [pltpu.HBM(a.shape, a.dtype) for a in arrs] * 2 + [jax.ShapeDtypeStruct((8, 128), F32)])
    res = pl.pallas_call(
        body, name=name, out_shape=out_shape, in_specs=[HBM] * (2 * n),
        out_specs=[SEM, SEM] + [HBM] * (2 * n) + [pl.BlockSpec(memory_space=pltpu.VMEM)],
        input_output_aliases={a: 2 + a for a in range(2 * n)},
        compiler_params=pltpu.CompilerParams(has_side_effects=EFFECT),
    )(*[_in_hbm(a) for a in arrs], *[_in_hbm(l) for l in lands])
    return tuple(res[:2]), list(res[2:2 + n]), list(res[2 + n:2 + 2 * n]), res[-1]


def swap_wait(sems, srcs, lands, after, name):
    n = len(srcs)

    def body(*refs):
        src, land = refs[:n], refs[n:2 * n]
        ssem, rsem = refs[2 * n:2 * n + 2]
        x, y, c = _axes()
        for a in range(n):
            cp = pltpu.make_async_remote_copy(src_ref=src[a], dst_ref=land[a], send_sem=ssem.at[a],
                                              recv_sem=rsem.at[a], device_id=(x, y, 1 - c), device_id_type=MESH)
            cp.wait_send()
            cp.wait_recv()

    res = pl.pallas_call(
        body, name=name, out_shape=[pltpu.HBM(a.shape, a.dtype) for a in srcs] * 2,
        in_specs=[HBM] * (2 * n) + [SEM, SEM, ANY], out_specs=[HBM] * (2 * n),
        input_output_aliases={a: a for a in range(2 * n)},
        compiler_params=pltpu.CompilerParams(has_side_effects=EFFECT),
    )(*srcs, *lands, *sems, after)
    return list(res[:n]), list(res[n:])


def _all_peers(x, y, c):
    out = []
    for k in range(1, N_DEV):
        px, py, pc = _flip(x, (k >> 2) & 1), _flip(y, (k >> 1) & 1), _flip(c, k & 1)
        out.append(((px, py, pc), 4 * px + 2 * py + pc))
    return out


def exchange_start(items, after, name):
    n = len(items)

    def body(*refs):
        src, land = refs[:n], refs[n:2 * n]
        ssem, rsem, lsem = refs[2 * n + 1:2 * n + 4]
        token = refs[-1]
        x, y, c = _axes()
        me = 4 * x + 2 * y + c
        for a, (_, scatter) in enumerate(items):
            pltpu.make_async_copy(src[a].at[me] if scatter else src[a], land[a].at[me], lsem.at[a]).start()
            for k, (peer, p) in enumerate(_all_peers(x, y, c)):
                pltpu.make_async_remote_copy(src_ref=src[a].at[p] if scatter else src[a], dst_ref=land[a].at[me],
                                             send_sem=ssem.at[7 * a + k], recv_sem=rsem.at[7 * a + k],
                                             device_id=peer, device_id_type=MESH).start()
        token[...] = jnp.zeros_like(token)

    srcs = [s for s, _ in items]
    lands = [lax.empty(s.shape if sc else (N_DEV,) + s.shape, s.dtype) for s, sc in items]
    out_shape = ([pltpu.SemaphoreType.DMA((7 * n,)), pltpu.SemaphoreType.DMA((7 * n,)), pltpu.SemaphoreType.DMA((n,))]
                 + [pltpu.HBM(s.shape, s.dtype) for s in srcs] + [pltpu.HBM(l.shape, l.dtype) for l in lands]
                 + [jax.ShapeDtypeStruct((8, 128), F32)])
    res = pl.pallas_call(
        body, name=name, out_shape=out_shape, in_specs=[HBM] * (2 * n) + [ANY],
        out_specs=[SEM, SEM, SEM] + [HBM] * (2 * n) + [pl.BlockSpec(memory_space=pltpu.VMEM)],
        input_output_aliases={a: 3 + a for a in range(2 * n)},
        compiler_params=pltpu.CompilerParams(has_side_effects=EFFECT),
    )(*[_in_hbm(s) for s in srcs], *[_in_hbm(l) for l in lands], after)
    return tuple(res[:3]), list(res[3:3 + n]), list(res[3 + n:3 + 2 * n]), res[-1]


def exchange_wait(sems, srcs, lands, scatter, after, name):
    n = len(srcs)

    def body(*refs):
        src, land = refs[:n], refs[n:2 * n]
        ssem, rsem, lsem = refs[2 * n:2 * n + 3]
        x, y, c = _axes()
        me = 4 * x + 2 * y + c
        for a in range(n):
            for k, (peer, p) in enumerate(_all_peers(x, y, c)):
                cp = pltpu.make_async_remote_copy(src_ref=src[a].at[p] if scatter[a] else src[a],
                                                  dst_ref=land[a].at[p], send_sem=ssem.at[7 * a + k],
                                                  recv_sem=rsem.at[7 * a + k], device_id=peer, device_id_type=MESH)
                cp.wait_send()
                cp.wait_recv()
            pltpu.make_async_copy(src[a].at[me] if scatter[a] else src[a], land[a].at[me], lsem.at[a]).wait()

    res = pl.pallas_call(
        body, name=name,
        out_shape=[pltpu.HBM(s.shape, s.dtype) for s in srcs] + [pltpu.HBM(l.shape, l.dtype) for l in lands],
        in_specs=[HBM] * (2 * n) + [SEM, SEM, SEM, ANY], out_specs=[HBM] * (2 * n),
        input_output_aliases={a: a for a in range(2 * n)},
        compiler_params=pltpu.CompilerParams(has_side_effects=EFFECT),
    )(*srcs, *lands, *sems, after)
    return list(res[n:])


def sum8(parts, name):
    _, P, C = parts.shape

    def body(p_ref, o_ref):
        tot = p_ref[0]
        for d in range(1, N_DEV):
            tot = tot + p_ref[d]
        o_ref[...] = tot

    return pl.pallas_call(body, name=name, out_shape=jax.ShapeDtypeStruct((P, C), F32),
                          compiler_params=pltpu.CompilerParams(vmem_limit_bytes=VMEM_LIMIT))(parts)


def reduce8(slab, dm, name):
    RT, C = slab.shape
    P = RT // N_DEV
    R = dm.shape[0]

    def body(s_ref, dm_ref, o_ref, dmo_ref, recv, s1, r1, s2, r2, s3, r3):
        x, y, c = _axes()
        me = 4 * x + 2 * y + c
        mine = pl.ds(pl.multiple_of(me * P, 8), P)
        parts, dms = [], []
        for k in range(1, N_DEV):
            px, py, pc = _flip(x, (k >> 2) & 1), _flip(y, (k >> 1) & 1), _flip(c, k & 1)
            theirs = pl.ds(pl.multiple_of((4 * px + 2 * py + pc) * P, 8), P)
            cp = pltpu.make_async_remote_copy(src_ref=s_ref.at[theirs], dst_ref=recv.at[me], send_sem=s1.at[k - 1],
                                              recv_sem=r1.at[k - 1], device_id=(px, py, pc), device_id_type=MESH)
            cp.start()
            parts.append(cp)
            cd = pltpu.make_async_remote_copy(src_ref=dm_ref, dst_ref=dmo_ref.at[me], send_sem=s3.at[k - 1],
                                              recv_sem=r3.at[k - 1], device_id=(px, py, pc), device_id_type=MESH)
            cd.start()
            dms.append(cd)
        dmo_ref[me] = dm_ref[...]
        recv[me] = s_ref[mine, :]
        for cp in parts:
            cp.wait()
        tot = recv[0]
        for d in range(1, N_DEV):
            tot = tot + recv[d]
        o_ref[mine, :] = tot
        out = []
        for k in range(1, N_DEV):
            peer = (_flip(x, (k >> 2) & 1), _flip(y, (k >> 1) & 1), _flip(c, k & 1))
            cp = pltpu.make_async_remote_copy(src_ref=o_ref.at[mine], dst_ref=o_ref.at[mine], send_sem=s2.at[k - 1],
                                              recv_sem=r2.at[k - 1], device_id=peer, device_id_type=MESH)
            cp.start()
            out.append(cp)
        for cp in out + dms:
            cp.wait()

    sems = [pltpu.SemaphoreType.DMA((N_DEV - 1,))] * 6
    return pl.pallas_call(
        body, name=name,
        out_shape=[jax.ShapeDtypeStruct((RT, C), F32), jax.ShapeDtypeStruct((N_DEV, R, C), F32)],
        in_specs=[pl.BlockSpec(memory_space=pltpu.VMEM)] * 2, out_specs=[pl.BlockSpec(memory_space=pltpu.VMEM)] * 2,
        scratch_shapes=[pltpu.VMEM((N_DEV, P, C), F32)] + sems,
        compiler_params=pltpu.CompilerParams(vmem_limit_bytes=VMEM_LIMIT),
    )(slab, dm)


def mm_nn(a, w, out_dtype, name, res=None, gate=None):
    M, K = a.shape
    S, _, Ns = w.shape
    TM = _tile(M, (1024, 512, 256) if K <= 1024 else (512, 256))
    TN = _tile(Ns, (1408, 1024, 768, 512, 256, 128))
    nj = Ns // TN
    fused = res is not None

    def body(*refs):
        if fused:
            a_ref, w_ref, r_ref, g_ref, f_ref, o_ref = refs
        else:
            a_ref, w_ref, f_ref = refs
        f = jnp.dot(a_ref[...], w_ref[...], preferred_element_type=F32)
        f_ref[...] = f.astype(f_ref.dtype)
        if fused:
            o_ref[...] = r_ref[...] + g_ref[...] * f

    col = lambda s, j, i: (i, s * nj + j)
    in_specs = [pl.BlockSpec((TM, K), lambda s, j, i: (i, 0)), pl.BlockSpec((None, K, TN), lambda s, j, i: (s, 0, j))]
    out_specs = [pl.BlockSpec((TM, TN), col)]
    out_shape = [jax.ShapeDtypeStruct((M, S * Ns), out_dtype)]
    args = [a, w]
    if fused:
        in_specs += [pl.BlockSpec((TM, TN), col), pl.BlockSpec((1, TN), lambda s, j, i: (0, s * nj + j))]
        out_specs.append(pl.BlockSpec((TM, TN), col))
        out_shape.append(jax.ShapeDtypeStruct((M, S * Ns), F32))
        args += [res, gate]
    out = pl.pallas_call(body, name=name, grid=(S, nj, M // TM), in_specs=in_specs, out_specs=out_specs,
                         out_shape=out_shape, compiler_params=_cp("parallel", "parallel", "parallel"))(*args)
    return tuple(out) if fused else out[0]


def mm_nt(g, w, out_dtype, name):
    g3 = g if g.ndim == 3 else g[None]
    Q, M, F = g3.shape
    S, K, Ns = w.shape
    TM = _tile(M, (1024, 512, 256) if K <= 1024 else (512, 256))
    TN = _tile(Ns, (1408, 1024, 768, 512, 256, 128))
    nj = Ns // TN
    nred = S * nj
    per_part = F // TN

    def body(g_ref, w_ref, o_ref, acc):
        n = pl.program_id(1)

        @pl.when(n == 0)
        def _():
            acc[...] = jnp.zeros_like(acc)

        acc[...] += lax.dot_general(g_ref[...], w_ref[...], (((1,), (1,)), ((), ())), preferred_element_type=F32)

        @pl.when(n == nred - 1)
        def _():
            o_ref[...] = acc[...].astype(o_ref.dtype)

    return pl.pallas_call(
        body, name=name, grid=(M // TM, nred),
        in_specs=[pl.BlockSpec((None, TM, TN), lambda i, n: (n // per_part, i, n % per_part)),
                  pl.BlockSpec((None, K, TN), lambda i, n: (n // nj, 0, n % nj))],
        out_specs=pl.BlockSpec((TM, K), lambda i, n: (i, 0)),
        out_shape=jax.ShapeDtypeStruct((M, K), out_dtype),
        scratch_shapes=[pltpu.VMEM((TM, K), F32)],
        compiler_params=_cp("parallel", "arbitrary"))(g3, w)


def mm_tn(a, g, S, name):
    M, K = a.shape
    g3 = g if g.ndim == 3 else g[None]
    Q, _, F = g3.shape
    Ns = Q * F // S
    TK = _tile(K, (256, 128))
    TN = _tile(Ns, (1408, 1024, 768, 512, 256, 128))
    nj = Ns // TN
    per_part = F // TN

    def body(a_ref, g_ref, o_ref):
        o_ref[...] = lax.dot_general(a_ref[...], g_ref[...], (((0,), (0,)), ((), ())),
                                     preferred_element_type=F32).astype(o_ref.dtype)

    return pl.pallas_call(
        body, name=name, grid=(S * nj, K // TK),
        in_specs=[pl.BlockSpec((M, TK), lambda n, k: (0, k)),
                  pl.BlockSpec((None, M, TN), lambda n, k: (n // per_part, 0, n % per_part))],
        out_specs=pl.BlockSpec((None, TK, TN), lambda n, k: (n // nj, k, n % nj)),
        out_shape=jax.ShapeDtypeStruct((S, K, Ns), BF),
        compiler_params=_cp("parallel", "parallel"))(a, g3)


def _rows(TL, D):
    return pl.BlockSpec((TL, D), lambda i: (i, 0))


def _fixed(R, D):
    return pl.BlockSpec((R, D), lambda i: (0, 0))


def _rowsum8(v):
    T, D = v.shape
    return jnp.sum(v.reshape(T // 8, 8, D), axis=0)


def _norm_parts(xv):
    r = lax.rsqrt(jnp.mean(xv * xv, axis=-1, keepdims=True) + RMS_EPS)
    return xv * r, r


def norm_mod(x, gamma, mods, k_shift, out_dtype, name):
    L, D = x.shape
    TL = _tile(L, (512, 256))

    def body(x_ref, g_ref, m_ref, o_ref):
        xn, _ = _norm_parts(x_ref[...])
        sh, sc = m_ref[k_shift:k_shift + 1, :], m_ref[k_shift + 1:k_shift + 2, :]
        o_ref[...] = ((xn * g_ref[...]) * (1.0 + sc) + sh).astype(o_ref.dtype)

    return pl.pallas_call(body, name=name, grid=(L // TL,),
                          in_specs=[_rows(TL, D), _fixed(1, D), _fixed(6, D)], out_specs=_rows(TL, D),
                          out_shape=jax.ShapeDtypeStruct((L, D), out_dtype), compiler_params=_cp("parallel"))(x, gamma, mods)


def norm_bwd(dh, x, dres, gamma, mods, k_shift, name, branch=None):
    L, D = x.shape
    TL = _tile(L, (512, 256))
    nacc = 4 if branch else 3

    def body(*refs):
        if branch:
            dh_ref, x_ref, dr_ref, g_ref, m_ref, f_ref, fm_ref, dx_ref, s_ref, df_ref, acc = refs
        else:
            dh_ref, x_ref, dr_ref, g_ref, m_ref, dx_ref, s_ref, acc = refs
        i = pl.program_id(0)

        @pl.when(i == 0)
        def _():
            acc[...] = jnp.zeros_like(acc)

        xn, r = _norm_parts(x_ref[...])
        dh_v = dh_ref[...].astype(F32)
        gam = g_ref[...]
        sc = m_ref[k_shift + 1:k_shift + 2, :]
        dn = dh_v * (1.0 + sc)
        dxn = dn * gam
        dx = dr_ref[...] + r * (dxn - xn * jnp.mean(dxn * xn, axis=-1, keepdims=True))
        dx_ref[...] = dx
        acc[0] += _rowsum8(dh_v)
        acc[1] += _rowsum8(dh_v * (xn * gam))
        acc[2] += _rowsum8(dn * xn)
        if branch:
            df_ref[...] = (dx * fm_ref[branch[2]:branch[2] + 1, :]).astype(df_ref.dtype)
            acc[3] += _rowsum8(dx * f_ref[...].astype(F32))

        @pl.when(i == pl.num_programs(0) - 1)
        def _():
            s_ref[...] = jnp.zeros_like(s_ref)
            for q in range(nacc):
                s_ref[q:q + 1, :] = jnp.sum(acc[q], axis=0, keepdims=True)

    in_specs = [_rows(TL, D), _rows(TL, D), _rows(TL, D), _fixed(1, D), _fixed(6, D)]
    out_specs = [_rows(TL, D), _fixed(8, D)]
    out_shape = [jax.ShapeDtypeStruct((L, D), F32), jax.ShapeDtypeStruct((8, D), F32)]
    args = [dh, x, dres, gamma, mods]
    if branch:
        in_specs += [_rows(TL, D), _fixed(6, D)]
        out_specs.append(_rows(TL, D))
        out_shape.append(jax.ShapeDtypeStruct((L, D), BF))
        args += [branch[0], branch[1]]
    return pl.pallas_call(
        body, name=name, grid=(L // TL,), in_specs=in_specs, out_specs=out_specs, out_shape=out_shape,
        scratch_shapes=[pltpu.VMEM((nacc, 8, D), F32)], compiler_params=_cp("arbitrary"))(*args)


def gate_bwd(dx, f, mods, k_gate, name):
    L, D = dx.shape
    TL = _tile(L, (512, 256))

    def body(dx_ref, f_ref, m_ref, o_ref, s_ref, acc):
        i = pl.program_id(0)

        @pl.when(i == 0)
        def _():
            acc[...] = jnp.zeros_like(acc)

        dxv = dx_ref[...]
        o_ref[...] = (dxv * m_ref[k_gate:k_gate + 1, :]).astype(o_ref.dtype)
        acc[...] += _rowsum8(dxv * f_ref[...].astype(F32))

        @pl.when(i == pl.num_programs(0) - 1)
        def _():
            s_ref[...] = jnp.zeros_like(s_ref)
            s_ref[0:1, :] = jnp.sum(acc[...], axis=0, keepdims=True)

    return pl.pallas_call(
        body, name=name, grid=(L // TL,), in_specs=[_rows(TL, D), _rows(TL, D), _fixed(6, D)],
        out_specs=[_rows(TL, D), _fixed(8, D)],
        out_shape=[jax.ShapeDtypeStruct((L, D), BF), jax.ShapeDtypeStruct((8, D), F32)],
        scratch_shapes=[pltpu.VMEM((8, D), F32)], compiler_params=_cp("arbitrary"))(dx, f, mods)


def ffn_in_act(a, w, name):
    M, K = a.shape
    S, _, Ns = w.shape
    half = S // 2
    TM = _tile(M, (512, 256))
    TN = _tile(Ns, (1408, 1024, 768, 512, 256, 128))
    nj = Ns // TN

    def body(a_ref, wg_ref, wu_ref, gu_ref, act_ref):
        av = a_ref[...]
        g = jnp.dot(av, wg_ref[...], preferred_element_type=F32)
        u = jnp.dot(av, wu_ref[...], preferred_element_type=F32)
        gu_ref[0] = g.astype(gu_ref.dtype)
        gu_ref[1] = u.astype(gu_ref.dtype)
        act_ref[...] = (g * jax.nn.sigmoid(g) * u).astype(act_ref.dtype)

    return pl.pallas_call(
        body, name=name, grid=(half, nj, M // TM),
        in_specs=[pl.BlockSpec((TM, K), lambda s, j, i: (i, 0)),
                  pl.BlockSpec((None, K, TN), lambda s, j, i: (s, 0, j)),
                  pl.BlockSpec((None, K, TN), lambda s, j, i: (s + half, 0, j))],
        out_specs=[pl.BlockSpec((2, TM, TN), lambda s, j, i: (0, i, s * nj + j)),
                   pl.BlockSpec((TM, TN), lambda s, j, i: (i, s * nj + j))],
        out_shape=[jax.ShapeDtypeStruct((2, M, half * Ns), BF), jax.ShapeDtypeStruct((M, half * Ns), BF)],
        compiler_params=_cp("parallel", "parallel", "parallel"))(a, w, w)


def ffn_out_bwd(dff, w2, gu, name):
    M, D = dff.shape
    F = w2.shape[0]
    TM = _tile(M, (512, 256))
    CW = _tile(F, (256, 128))

    def body(d_ref, w_ref, gu_ref, o_ref):
        dv = d_ref[...]
        for c in range(0, F, CW):
            da = lax.dot_general(dv, w_ref[c:c + CW, :], (((1,), (1,)), ((), ())), preferred_element_type=F32)
            g = gu_ref[0, :, c:c + CW].astype(F32)
            u = gu_ref[1, :, c:c + CW].astype(F32)
            s = jax.nn.sigmoid(g)
            o_ref[0, :, c:c + CW] = (da * u * (s + g * s * (1.0 - s))).astype(o_ref.dtype)
            o_ref[1, :, c:c + CW] = (da * g * s).astype(o_ref.dtype)

    part = pl.BlockSpec((2, TM, F), lambda i: (0, i, 0))
    return pl.pallas_call(
        body, name=name, grid=(M // TM,),
        in_specs=[pl.BlockSpec((TM, D), lambda i: (i, 0)), pl.BlockSpec((F, D), lambda i: (0, 0)), part],
        out_specs=part, out_shape=jax.ShapeDtypeStruct((2, M, F), BF),
        compiler_params=_cp("parallel"))(dff, w2, gu)


def swiglu_act(gu, name):
    L, F2 = gu.shape
    F = F2 // 2
    TL = _tile(L, (256,))

    def body(gu_ref, o_ref):
        g = gu_ref[:, :F].astype(F32)
        u = gu_ref[:, F:].astype(F32)
        o_ref[...] = (g * jax.nn.sigmoid(g) * u).astype(o_ref.dtype)

    return pl.pallas_call(body, name=name, grid=(L // TL,), in_specs=[_rows(TL, F2)], out_specs=_rows(TL, F),
                          out_shape=jax.ShapeDtypeStruct((L, F), BF), compiler_params=_cp("parallel"))(gu)


def swiglu_bwd(da, gu, name):
    L, F2 = gu.shape
    F = F2 // 2
    TL = _tile(L, (256,))

    def body(da_ref, gu_ref, o_ref):
        g = gu_ref[:, :F].astype(F32)
        u = gu_ref[:, F:].astype(F32)
        d = da_ref[...].astype(F32)
        s = jax.nn.sigmoid(g)
        o_ref[:, :F] = (d * u * (s + g * s * (1.0 - s))).astype(o_ref.dtype)
        o_ref[:, F:] = (d * g * s).astype(o_ref.dtype)

    return pl.pallas_call(body, name=name, grid=(L // TL,), in_specs=[_rows(TL, F), _rows(TL, F2)],
                          out_specs=_rows(TL, F2), out_shape=jax.ShapeDtypeStruct((L, F2), BF),
                          compiler_params=_cp("parallel"))(da, gu)


def glu_res(o, x, mods, k_gate, name):
    L, D = x.shape
    TL = _tile(L, (512, 256))

    def body(o_ref, x_ref, m_ref, mix_ref, y_ref):
        mix = o_ref[:, :D].astype(F32) * jax.nn.sigmoid(o_ref[:, D:].astype(F32))
        mix_ref[...] = mix.astype(mix_ref.dtype)
        y_ref[...] = x_ref[...] + m_ref[k_gate:k_gate + 1, :] * mix

    return pl.pallas_call(
        body, name=name, grid=(L // TL,), in_specs=[_rows(TL, 2 * D), _rows(TL, D), _fixed(6, D)],
        out_specs=[_rows(TL, D), _rows(TL, D)],
        out_shape=[jax.ShapeDtypeStruct((L, D), BF), jax.ShapeDtypeStruct((L, D), F32)],
        compiler_params=_cp("parallel"))(o, x, mods)


def ssm_out_glu(z, w, x, mods, k_gate, name):
    M, K = z.shape
    S, _, Ns = w.shape
    half = S // 2
    TM = _tile(M, (1024, 512, 256))
    TN = _tile(Ns, (512, 256, 128))
    nj = Ns // TN

    def body(z_ref, wv_ref, wg_ref, x_ref, m_ref, o_ref, mix_ref, y_ref):
        zv = z_ref[...]
        val = jnp.dot(zv, wv_ref[...], preferred_element_type=F32)
        gate = jnp.dot(zv, wg_ref[...], preferred_element_type=F32)
        o_ref[0] = val.astype(o_ref.dtype)
        o_ref[1] = gate.astype(o_ref.dtype)
        mix = val * jax.nn.sigmoid(gate)
        mix_ref[...] = mix.astype(mix_ref.dtype)
        y_ref[...] = x_ref[...] + m_ref[k_gate:k_gate + 1, :] * mix

    col = lambda s, j, i: (i, s * nj + j)
    return pl.pallas_call(
        body, name=name, grid=(half, nj, M // TM),
        in_specs=[pl.BlockSpec((TM, K), lambda s, j, i: (i, 0)),
                  pl.BlockSpec((None, K, TN), lambda s, j, i: (s, 0, j)),
                  pl.BlockSpec((None, K, TN), lambda s, j, i: (s + half, 0, j)),
                  pl.BlockSpec((TM, TN), col), pl.BlockSpec((6, TN), lambda s, j, i: (0, s * nj + j))],
        out_specs=[pl.BlockSpec((2, TM, TN), lambda s, j, i: (0, i, s * nj + j)), pl.BlockSpec((TM, TN), col),
                   pl.BlockSpec((TM, TN), col)],
        out_shape=[jax.ShapeDtypeStruct((2, M, half * Ns), BF), jax.ShapeDtypeStruct((M, half * Ns), BF),
                   jax.ShapeDtypeStruct((M, half * Ns), F32)],
        compiler_params=_cp("parallel", "parallel", "parallel"))(z, w, w, x, mods)


def glu_bwd(dmix, o, name):
    _, L, D = o.shape
    TL = _tile(L, (512, 256))

    def body(d_ref, o_ref, do_ref):
        d = d_ref[...].astype(F32)
        val = o_ref[0].astype(F32)
        s = jax.nn.sigmoid(o_ref[1].astype(F32))
        do_ref[0] = (d * s).astype(do_ref.dtype)
        do_ref[1] = (d * val * s * (1.0 - s)).astype(do_ref.dtype)

    part = pl.BlockSpec((2, TL, D), lambda i: (0, i, 0))
    return pl.pallas_call(body, name=name, grid=(L // TL,), in_specs=[_rows(TL, D), part],
                          out_specs=part, out_shape=jax.ShapeDtypeStruct((2, L, D), BF),
                          compiler_params=_cp("parallel"))(dmix, o)


def final_loss(x, target, gamma, f, fmods, k_gate, name):
    L, D = x.shape
    TL = _tile(L, (512, 256))

    def body(x_ref, t_ref, g_ref, f_ref, fm_ref, l_ref, dx_ref, s_ref, df_ref, acc, lacc):
        i = pl.program_id(0)

        @pl.when(i == 0)
        def _():
            acc[...] = jnp.zeros_like(acc)
            lacc[...] = jnp.zeros_like(lacc)

        xn, r = _norm_parts(x_ref[...])
        gam = g_ref[...]
        e = xn * gam - t_ref[...]
        lacc[...] += jnp.sum(0.5 * jnp.mean(e * e, axis=-1, keepdims=True), axis=0, keepdims=True)
        dy = e * (1.0 / D)
        dxn = dy * gam
        dx = r * (dxn - xn * jnp.mean(dxn * xn, axis=-1, keepdims=True))
        dx_ref[...] = dx
        df_ref[...] = (dx * fm_ref[k_gate:k_gate + 1, :]).astype(df_ref.dtype)
        acc[0] += _rowsum8(dy * xn)
        acc[1] += _rowsum8(dx * f_ref[...].astype(F32))

        @pl.when(i == pl.num_programs(0) - 1)
        def _():
            s_ref[...] = jnp.zeros_like(s_ref)
            for q in range(2):
                s_ref[q:q + 1, :] = jnp.sum(acc[q], axis=0, keepdims=True)
            l_ref[...] = jnp.broadcast_to(lacc[...], l_ref.shape)

    return pl.pallas_call(
        body, name=name, grid=(L // TL,),
        in_specs=[_rows(TL, D), _rows(TL, D), _fixed(1, D), _rows(TL, D), _fixed(6, D)],
        out_specs=[_fixed(8, 128), _rows(TL, D), _fixed(8, D), _rows(TL, D)],
        out_shape=[jax.ShapeDtypeStruct((8, 128), F32), jax.ShapeDtypeStruct((L, D), F32),
                   jax.ShapeDtypeStruct((8, D), F32), jax.ShapeDtypeStruct((L, D), BF)],
        scratch_shapes=[pltpu.VMEM((2, 8, D), F32), pltpu.VMEM((1, 1), F32)],
        compiler_params=_cp("arbitrary"))(x, target, gamma, f, fmods)


def _col(L, TC, off):
    return pl.BlockSpec((L, TC), lambda j: (0, off + j))


def _shift_down(v, k, row):
    return jnp.where(row >= k, pltpu.roll(v, k, 0), 0.0)


def _shift_up(v, k, row, L):
    return jnp.where(row < L - k, pltpu.roll(v, L - k, 0), 0.0)


def conv_fwd(p, w, name):
    L, D3 = p.shape
    D = D3 // 3
    TC = _tile(D, (128,))
    nc = D // TC

    def body(b_ref, c_ref, v_ref, w_ref, o_ref):
        row = lax.broadcasted_iota(jnp.int32, (L, TC), 0)
        cv = c_ref[...].astype(F32) * v_ref[...].astype(F32)
        conv = w_ref[2:3, :] * cv + w_ref[1:2, :] * _shift_down(cv, 1, row) + w_ref[0:1, :] * _shift_down(cv, 2, row)
        o_ref[...] = (b_ref[...].astype(F32) * conv).astype(o_ref.dtype)

    return pl.pallas_call(
        body, name=name, grid=(nc,),
        in_specs=[_col(L, TC, 0), _col(L, TC, nc), _col(L, TC, 2 * nc), pl.BlockSpec((3, TC), lambda j: (0, j))],
        out_specs=_col(L, TC, 0), out_shape=jax.ShapeDtypeStruct((L, D), BF), compiler_params=_cp("parallel"))(p, p, p, w)


def conv_bwd(dm, p, w, name):
    L, D3 = p.shape
    D = D3 // 3
    TC = _tile(D, (128,))
    nc = D // TC

    def body(dm_ref, b_ref, c_ref, v_ref, w_ref, db_ref, dc_ref, dv_ref, dw_ref):
        row = lax.broadcasted_iota(jnp.int32, (L, TC), 0)
        cg, vv = c_ref[...].astype(F32), v_ref[...].astype(F32)
        cv = cg * vv
        cv1, cv2 = _shift_down(cv, 1, row), _shift_down(cv, 2, row)
        conv = w_ref[2:3, :] * cv + w_ref[1:2, :] * cv1 + w_ref[0:1, :] * cv2
        dmv = dm_ref[...].astype(F32)
        db_ref[...] = (dmv * conv).astype(db_ref.dtype)
        dconv = dmv * b_ref[...].astype(F32)
        dcv = (w_ref[2:3, :] * dconv + w_ref[1:2, :] * _shift_up(dconv, 1, row, L)
               + w_ref[0:1, :] * _shift_up(dconv, 2, row, L))
        dc_ref[...] = (dcv * vv).astype(dc_ref.dtype)
        dv_ref[...] = (dcv * cg).astype(dv_ref.dtype)
        dw_ref[...] = jnp.zeros_like(dw_ref)
        dw_ref[0:1, :] = jnp.sum(dconv * cv2, axis=0, keepdims=True)
        dw_ref[1:2, :] = jnp.sum(dconv * cv1, axis=0, keepdims=True)
        dw_ref[2:3, :] = jnp.sum(dconv * cv, axis=0, keepdims=True)

    one = jax.ShapeDtypeStruct((L, D), BF)
    return pl.pallas_call(
        body, name=name, grid=(nc,),
        in_specs=[_col(L, TC, 0), _col(L, TC, 0), _col(L, TC, nc), _col(L, TC, 2 * nc),
                  pl.BlockSpec((3, TC), lambda j: (0, j))],
        out_specs=[_col(L, TC, 0), _col(L, TC, 0), _col(L, TC, 0), pl.BlockSpec((8, TC), lambda j: (0, j))],
        out_shape=[one, one, one, jax.ShapeDtypeStruct((8, D), F32)],
        compiler_params=_cp("parallel"))(dm, p, p, p, w)


def _gelu(y):
    return 0.5 * y * (1.0 + jnp.tanh(GELU_C * (y + GELU_A * y * y * y)))


def _gelu_grad(y):
    th = jnp.tanh(GELU_C * (y + GELU_A * y * y * y))
    return 0.5 * (1.0 + th) + 0.5 * y * (1.0 - th * th) * GELU_C * (1.0 + 3.0 * GELU_A * y * y)


def _cmul_add(br, bi, ar, ai, sr, si):
    return br + ar * sr - ai * si, bi + ar * si + ai * sr


def _log2(n):
    k = n.bit_length() - 1
    assert 1 << k == n
    return k


def _replicate(P2, W2, P, GLP, transposed):
    shape = (W2, P2) if transposed else (P2, W2)
    k = lax.broadcasted_iota(jnp.int32, shape, 1 if transposed else 0)
    c = lax.broadcasted_iota(jnp.int32, shape, 0 if transposed else 1)
    return ((k >> _log2(P)) == (c >> _log2(GLP))) & ((k & (P - 1)) == (c & (P - 1)))


def _on_diagonal(KB, W2, H, P, GLP, transposed):
    shape = (W2, KB) if transposed else (KB, W2)
    r = lax.broadcasted_iota(jnp.int32, shape, 1 if transposed else 0)
    c = lax.broadcasted_iota(jnp.int32, shape, 0 if transposed else 1)
    return (r >> _log2(H)) == ((c & (GLP - 1)) >> _log2(P))


def _expand(t, dims, transposed):
    KB, W2, H, P, GLP = dims
    rep = _replicate(2 * P, W2, P, GLP, transposed).astype(t.dtype)
    wide = jnp.dot(rep, t, preferred_element_type=F32) if transposed else jnp.dot(t, rep, preferred_element_type=F32)
    return jnp.where(_on_diagonal(KB, W2, H, P, GLP, transposed), wide, 0.0).astype(t.dtype)


def _extract(acc, dims):
    KB, W2, H, P, GLP = dims
    rep = _replicate(2 * P, W2, P, GLP, True).astype(F32)
    kept = jnp.where(_on_diagonal(KB, W2, H, P, GLP, False), acc, 0.0)
    return jnp.dot(kept, rep, preferred_element_type=F32, precision=lax.Precision.HIGHEST)


def _cmul(ar, ai, sr, si):
    return ar * sr - ai * si, ar * si + ai * sr


LANES = 128


def _cols(ref, base, n, rows):
    return jnp.concatenate([ref[base + q, rows, :] for q in range(n)], axis=1)


def _set_cols(ref, base, n, rows, val):
    for q in range(n):
        ref[base + q, rows, :] = val[:, q * LANES:(q + 1) * LANES]


def _strided_s5_fwd(h, tb, tct, pw, dvec, name):
    L, D = h.shape
    nkb, KB, P2 = tb.shape
    P = P2 // 2
    W = (KB // SSM_GROUP) * P
    W2 = 2 * W
    dims = (KB, W2, SSM_GROUP, P, W)
    TL = _tile(L, (512, 256))
    CH = TL // 8
    NC = W // LANES

    def body(h_ref, tb_ref, tct_ref, pw_ref, d_ref, s_ref, y_ref, z_ref, bw, cw, carry):
        t = pl.program_id(1)

        @pl.when(t == 0)
        def _():
            carry[...] = jnp.zeros_like(carry)
            bw[...] = _expand(tb_ref[...], dims, False)
            cw[...] = _expand(tct_ref[...], dims, True)

        hv = h_ref[...]
        _set_cols(s_ref, 0, 2 * NC, slice(None), jnp.dot(hv.astype(BF), bw[...], preferred_element_type=F32))
        ar, ai = pw_ref[0:8, :W], pw_ref[0:8, W:]
        xr = xi = jnp.zeros((8, W), F32)
        for j in range(CH):
            rows = pl.ds(j, 8, stride=CH)
            xr, xi = _cmul_add(_cols(s_ref, 0, NC, rows), _cols(s_ref, NC, NC, rows), ar, ai, xr, xi)
            _set_cols(s_ref, 0, NC, rows, xr)
            _set_cols(s_ref, NC, NC, rows, xi)
        for k, off in ((1, 8), (2, 16), (4, 24)):
            xr, xi = _cmul_add(xr, xi, pw_ref[off:off + 8, :W], pw_ref[off:off + 8, W:],
                               pltpu.roll(xr, k, 0), pltpu.roll(xi, k, 0))
        xr, xi = _cmul_add(xr, xi, pw_ref[32:40, :W], pw_ref[32:40, W:], carry[0], carry[1])
        first = lax.broadcasted_iota(jnp.int32, (8, W), 0) == 0
        cr = jnp.where(first, carry[0], pltpu.roll(xr, 1, 0))
        ci = jnp.where(first, carry[1], pltpu.roll(xi, 1, 0))
        carry[0] = jnp.broadcast_to(xr[7:8], (8, W))
        carry[1] = jnp.broadcast_to(xi[7:8], (8, W))
        for j in range(CH):
            rows = pl.ds(j, 8, stride=CH)
            cr, ci = _cmul(ar, ai, cr, ci)
            _set_cols(s_ref, 0, NC, rows, _cols(s_ref, 0, NC, rows) + cr)
            _set_cols(s_ref, NC, NC, rows, _cols(s_ref, NC, NC, rows) + ci)
        sv = _cols(s_ref, 0, 2 * NC, slice(None))
        y = jnp.dot(sv.astype(BF), cw[...], preferred_element_type=F32) + d_ref[...] * hv
        y_ref[...] = y
        z_ref[...] = _gelu(y).astype(z_ref.dtype)

    blk = lambda kb, t: (t, kb)
    per_kb = lambda kb, t: (kb, 0, 0)
    return pl.pallas_call(
        body, name=name, grid=(nkb, L // TL),
        in_specs=[pl.BlockSpec((TL, KB), blk), pl.BlockSpec((None, KB, P2), per_kb),
                  pl.BlockSpec((None, P2, KB), per_kb), pl.BlockSpec((None, 40, W2), per_kb),
                  pl.BlockSpec((1, KB), lambda kb, t: (0, kb))],
        out_specs=[pl.BlockSpec((2 * NC, TL, LANES), lambda kb, t: (kb, t, 0)), pl.BlockSpec((TL, KB), blk),
                   pl.BlockSpec((TL, KB), blk)],
        out_shape=[jax.ShapeDtypeStruct((nkb * 2 * NC, L, LANES), F32), jax.ShapeDtypeStruct((L, D), F32),
                   jax.ShapeDtypeStruct((L, D), BF)],
        scratch_shapes=[pltpu.VMEM((KB, W2), BF), pltpu.VMEM((W2, KB), BF), pltpu.VMEM((2, 8, W), F32)],
        compiler_params=_cp("parallel", "arbitrary"))(h, tb, tct, pw, dvec)


def _strided_s5_bwd(dz, y, h, s, tc, tbt, pwr, dvec, name):
    L, D = h.shape
    nkb, KB, P2 = tc.shape
    P = P2 // 2
    W = (KB // SSM_GROUP) * P
    W2 = 2 * W
    dims = (KB, W2, SSM_GROUP, P, W)
    TL = _tile(L, (512, 256))
    CH = TL // 8
    NC = W // LANES
    nt = L // TL

    def body(dz_ref, y_ref, h_ref, s_ref, sp_ref, tc_ref, tbt_ref, pw_ref, d_ref,
             dh_ref, dd_ref, da_ref, db_ref, dc_ref, g, ctw, btw, dbacc, dcacc, carry):
        t = pl.program_id(1)

        @pl.when(t == 0)
        def _():
            carry[...] = jnp.zeros_like(carry)
            dd_ref[...] = jnp.zeros_like(dd_ref)
            da_ref[...] = jnp.zeros_like(da_ref)
            dbacc[...] = jnp.zeros_like(dbacc)
            dcacc[...] = jnp.zeros_like(dcacc)
            ctw[...] = _expand(tc_ref[...], dims, False)
            btw[...] = _expand(tbt_ref[...], dims, True)

        hv = h_ref[...]
        dy = dz_ref[...].astype(F32) * _gelu_grad(y_ref[...])
        dd_ref[...] += _rowsum8(dy * hv)
        dyb = dy.astype(BF)
        _set_cols(g, 0, 2 * NC, slice(None), jnp.dot(dyb, ctw[...], preferred_element_type=F32))
        ar, ai = pw_ref[0:8, :W], pw_ref[0:8, W:]
        gr = gi = jnp.zeros((8, W), F32)
        for j in reversed(range(CH)):
            rows = pl.ds(j, 8, stride=CH)
            gr, gi = _cmul_add(_cols(g, 0, NC, rows), _cols(g, NC, NC, rows), ar, ai, gr, gi)
            _set_cols(g, 0, NC, rows, gr)
            _set_cols(g, NC, NC, rows, gi)
        for k, off in ((1, 8), (2, 16), (4, 24)):
            gr, gi = _cmul_add(gr, gi, pw_ref[off:off + 8, :W], pw_ref[off:off + 8, W:],
                               pltpu.roll(gr, 8 - k, 0), pltpu.roll(gi, 8 - k, 0))
        gr, gi = _cmul_add(gr, gi, pw_ref[32:40, :W], pw_ref[32:40, W:], carry[0], carry[1])
        sub = lax.broadcasted_iota(jnp.int32, (8, W), 0)
        cr = jnp.where(sub == 7, carry[0], pltpu.roll(gr, 7, 0))
        ci = jnp.where(sub == 7, carry[1], pltpu.roll(gi, 7, 0))
        carry[0] = jnp.broadcast_to(gr[0:1], (8, W))
        carry[1] = jnp.broadcast_to(gi[0:1], (8, W))
        live = jnp.where(t == nt - 1, 0.0, 1.0)
        accr = acci = jnp.zeros((8, W), F32)
        for j in reversed(range(CH)):
            rows = pl.ds(j, 8, stride=CH)
            cr, ci = _cmul(ar, ai, cr, ci)
            gr, gi = _cols(g, 0, NC, rows) + cr, _cols(g, NC, NC, rows) + ci
            _set_cols(g, 0, NC, rows, gr)
            _set_cols(g, NC, NC, rows, gi)
            if j > 0:
                before = pl.ds(j - 1, 8, stride=CH)
                pr, pi = _cols(s_ref, 0, NC, before), _cols(s_ref, NC, NC, before)
            else:
                last = pl.ds(CH - 1, 8, stride=CH)
                pr = jnp.where(sub == 0, _cols(sp_ref, 0, NC, slice(7, 8)) * live,
                               pltpu.roll(_cols(s_ref, 0, NC, last), 1, 0))
                pi = jnp.where(sub == 0, _cols(sp_ref, NC, NC, slice(7, 8)) * live,
                               pltpu.roll(_cols(s_ref, NC, NC, last), 1, 0))
            accr = accr + pr * gr + pi * gi
            acci = acci + pr * gi - pi * gr
        da_ref[:, :W] += accr
        da_ref[:, W:] += acci

        gb = _cols(g, 0, 2 * NC, slice(None)).astype(BF)
        dh_ref[...] = dy * d_ref[...] + jnp.dot(gb, btw[...], preferred_element_type=F32)
        tn = (((0,), (0,)), ((), ()))
        dbacc[...] += lax.dot_general(hv.astype(BF), gb, tn, preferred_element_type=F32)
        dcacc[...] += lax.dot_general(dyb, _cols(s_ref, 0, 2 * NC, slice(None)).astype(BF), tn,
                                      preferred_element_type=F32)

        @pl.when(t == nt - 1)
        def _():
            db_ref[...] = _extract(dbacc[...], dims)
            dc_ref[...] = _extract(dcacc[...], dims)

    rev = lambda kb, t: (nt - 1 - t, kb)
    per_kb = lambda kb, t: (kb, 0, 0)
    return pl.pallas_call(
        body, name=name, grid=(nkb, nt),
        in_specs=[pl.BlockSpec((TL, KB), rev), pl.BlockSpec((TL, KB), rev), pl.BlockSpec((TL, KB), rev),
                  pl.BlockSpec((2 * NC, TL, LANES), lambda kb, t: (kb, nt - 1 - t, 0)),
                  pl.BlockSpec((2 * NC, 8, LANES), lambda kb, t: (kb, jnp.maximum((nt - 1 - t) * CH - 1, 0), 0)),
                  pl.BlockSpec((None, KB, P2), per_kb), pl.BlockSpec((None, P2, KB), per_kb),
                  pl.BlockSpec((None, 40, W2), per_kb), pl.BlockSpec((1, KB), lambda kb, t: (0, kb))],
        out_specs=[pl.BlockSpec((TL, KB), rev), pl.BlockSpec((8, KB), lambda kb, t: (0, kb)),
                   pl.BlockSpec((None, 8, W2), per_kb), pl.BlockSpec((None, KB, P2), per_kb),
                   pl.BlockSpec((None, KB, P2), per_kb)],
        out_shape=[jax.ShapeDtypeStruct((L, D), F32), jax.ShapeDtypeStruct((8, D), F32),
                   jax.ShapeDtypeStruct((nkb, 8, W2), F32), jax.ShapeDtypeStruct((nkb, KB, P2), F32),
                   jax.ShapeDtypeStruct((nkb, KB, P2), F32)],
        scratch_shapes=[pltpu.VMEM((2 * NC, TL, LANES), F32), pltpu.VMEM((KB, W2), BF), pltpu.VMEM((W2, KB), BF),
                        pltpu.VMEM((KB, W2), F32), pltpu.VMEM((KB, W2), F32), pltpu.VMEM((2, 8, W), F32)],
        compiler_params=_cp("parallel", "arbitrary"))(dz, y, h, s, s, tc, tbt, pwr, dvec)


def _chunk_order(TL, CH, transposed):
    out_row = lax.broadcasted_iota(jnp.int32, (TL, TL), 1 if transposed else 0)
    in_row = lax.broadcasted_iota(jnp.int32, (TL, TL), 0 if transposed else 1)
    return in_row == ((out_row & 7) << _log2(CH)) + (out_row >> 3)


def _reorder(perm, v):
    hi = v.astype(perm.dtype)
    lo = (v - hi.astype(F32)).astype(perm.dtype)
    return jnp.dot(perm, hi, preferred_element_type=F32) + jnp.dot(perm, lo, preferred_element_type=F32)


def _interleave(main, side):
    n, m, k = len(main), len(side), 0
    for i, step in enumerate(main):
        step()
        while k < m and (k + 1) * n <= (i + 1) * m:
            side[k]()
            k += 1
    for step in side[k:]:
        step()


S5_CHUNK = 512


def s5_fwd(h, tb, tct, pw, dvec, name):
    L, D = h.shape
    nkb, KB, P2 = tb.shape
    P = P2 // 2
    W = (KB // SSM_GROUP) * P
    W2 = 2 * W
    dims = (KB, W2, SSM_GROUP, P, W)
    TL = _tile(L, (512, 256))
    CH = TL // 8
    NB = 2 if nkb % 2 == 0 else 1
    CK = min(S5_CHUNK, W2)

    def body(h_ref, tb_ref, tct_ref, pw_ref, d_ref, s_ref, y_ref, z_ref, bw, cw, perm, unperm, carry):
        t = pl.program_id(1)

        @pl.when(t == 0)
        def _():
            carry[...] = jnp.zeros_like(carry)
            for b in range(NB):
                bw[b] = _expand(tb_ref[b], dims, False)
                cw[b] = _expand(tct_ref[b], dims, True)
            perm[...] = _chunk_order(TL, CH, False).astype(perm.dtype)
            unperm[...] = _chunk_order(TL, CH, True).astype(perm.dtype)

        hp = _reorder(perm[...], h_ref[...])
        hpb = hp.astype(BF)
        first = lax.broadcasted_iota(jnp.int32, (8, W), 0) == 0

        def project(b):
            def chunk(c):
                def emit():
                    s_ref[:, b * W2 + c:b * W2 + c + CK] = jnp.dot(hpb[:, b * KB:(b + 1) * KB], bw[b, :, c:c + CK],
                                                                   preferred_element_type=F32)
                return emit
            return [chunk(c) for c in range(0, W2, CK)]

        def scan(b):
            re, im = slice(b * W2, b * W2 + W), slice(b * W2 + W, (b + 1) * W2)
            ar, ai = pw_ref[b, 0:8, :W], pw_ref[b, 0:8, W:]
            st = {"x": (jnp.zeros((8, W), F32), jnp.zeros((8, W), F32))}

            def own(j):
                def emit():
                    rows = slice(j * 8, j * 8 + 8)
                    xr, xi = _cmul_add(s_ref[rows, re], s_ref[rows, im], ar, ai, *st["x"])
                    s_ref[rows, re] = xr
                    s_ref[rows, im] = xi
                    st["x"] = (xr, xi)
                return emit

            def ends():
                xr, xi = st["x"]
                for k, off in ((1, 8), (2, 16), (4, 24)):
                    xr, xi = _cmul_add(xr, xi, pw_ref[b, off:off + 8, :W], pw_ref[b, off:off + 8, W:],
                                       pltpu.roll(xr, k, 0), pltpu.roll(xi, k, 0))
                xr, xi = _cmul_add(xr, xi, pw_ref[b, 32:40, :W], pw_ref[b, 32:40, W:], carry[b, 0], carry[b, 1])
                st["c"] = (jnp.where(first, carry[b, 0], pltpu.roll(xr, 1, 0)),
                           jnp.where(first, carry[b, 1], pltpu.roll(xi, 1, 0)))
                carry[b, 0] = jnp.broadcast_to(xr[7:8], (8, W))
                carry[b, 1] = jnp.broadcast_to(xi[7:8], (8, W))

            def carried(j):
                def emit():
                    rows = slice(j * 8, j * 8 + 8)
                    cr, ci = _cmul(ar, ai, *st["c"])
                    s_ref[rows, re] = s_ref[rows, re] + cr
                    s_ref[rows, im] = s_ref[rows, im] + ci
                    st["c"] = (cr, ci)
                return emit

            return [own(j) for j in range(CH)] + [ends] + [carried(j) for j in range(CH)]

        def readout(b):
            cols = slice(b * KB, (b + 1) * KB)
            acc = {}

            def chunk(c):
                def emit():
                    part = jnp.dot(s_ref[:, b * W2 + c:b * W2 + c + CK].astype(BF), cw[b, c:c + CK, :],
                                   preferred_element_type=F32)
                    acc["y"] = part if c == 0 else acc["y"] + part
                return emit

            def finish():
                y = acc["y"] + d_ref[:, cols] * hp[:, cols]
                y_ref[:, cols] = y
                z_ref[:, cols] = jnp.dot(unperm[...], _gelu(y).astype(BF),
                                         preferred_element_type=F32).astype(z_ref.dtype)

            return [chunk(c) for c in range(0, W2, CK)] + [finish]

        for emit in project(0):
            emit()
        for b in range(NB):
            side = (project(b + 1) if b + 1 < NB else []) + (readout(b - 1) if b > 0 else [])
            _interleave(scan(b), side)
        for emit in readout(NB - 1):
            emit()

    blk = lambda kb, t: (t, kb)
    per_kb = lambda kb, t: (kb, 0, 0)
    return pl.pallas_call(
        body, name=name, grid=(nkb // NB, L // TL),
        in_specs=[pl.BlockSpec((TL, NB * KB), blk), pl.BlockSpec((NB, KB, P2), per_kb),
                  pl.BlockSpec((NB, P2, KB), per_kb), pl.BlockSpec((NB, 40, W2), per_kb),
                  pl.BlockSpec((1, NB * KB), lambda kb, t: (0, kb))],
        out_specs=[pl.BlockSpec((TL, NB * W2), blk), pl.BlockSpec((TL, NB * KB), blk),
                   pl.BlockSpec((TL, NB * KB), blk)],
        out_shape=[jax.ShapeDtypeStruct((L, nkb * W2), F32), jax.ShapeDtypeStruct((L, D), F32),
                   jax.ShapeDtypeStruct((L, D), BF)],
        scratch_shapes=[pltpu.VMEM((NB, KB, W2), BF), pltpu.VMEM((NB, W2, KB), BF), pltpu.VMEM((TL, TL), BF),
                        pltpu.VMEM((TL, TL), BF), pltpu.VMEM((NB, 2, 8, W), F32)],
        compiler_params=_cp("parallel", "arbitrary"))(h, tb, tct, pw, dvec)


def _s5_fwd_one_block(h, tb, tct, pw, dvec, name):
    L, D = h.shape
    nkb, KB, P2 = tb.shape
    P = P2 // 2
    W = (KB // SSM_GROUP) * P
    W2 = 2 * W
    dims = (KB, W2, SSM_GROUP, P, W)
    TL = _tile(L, (512, 256))
    CH = TL // 8

    def body(h_ref, tb_ref, tct_ref, pw_ref, d_ref, s_ref, y_ref, z_ref, bw, cw, perm, unperm, carry):
        t = pl.program_id(1)

        @pl.when(t == 0)
        def _():
            carry[...] = jnp.zeros_like(carry)
            bw[...] = _expand(tb_ref[...], dims, False)
            cw[...] = _expand(tct_ref[...], dims, True)
            perm[...] = _chunk_order(TL, CH, False).astype(perm.dtype)
            unperm[...] = _chunk_order(TL, CH, True).astype(perm.dtype)

        hp = _reorder(perm[...], h_ref[...])
        s_ref[...] = jnp.dot(hp.astype(BF), bw[...], preferred_element_type=F32)
        ar, ai = pw_ref[0:8, :W], pw_ref[0:8, W:]

        def own(j, x):
            rows = pl.ds(pl.multiple_of(j * 8, 8), 8)
            xr, xi = _cmul_add(s_ref[rows, :W], s_ref[rows, W:], ar, ai, x[0], x[1])
            s_ref[rows, :W] = xr
            s_ref[rows, W:] = xi
            return xr, xi

        zero = jnp.zeros((8, W), F32)
        xr, xi = lax.fori_loop(0, CH, own, (zero, zero))
        for k, off in ((1, 8), (2, 16), (4, 24)):
            xr, xi = _cmul_add(xr, xi, pw_ref[off:off + 8, :W], pw_ref[off:off + 8, W:],
                               pltpu.roll(xr, k, 0), pltpu.roll(xi, k, 0))
        xr, xi = _cmul_add(xr, xi, pw_ref[32:40, :W], pw_ref[32:40, W:], carry[0], carry[1])
        first = lax.broadcasted_iota(jnp.int32, (8, W), 0) == 0
        cr = jnp.where(first, carry[0], pltpu.roll(xr, 1, 0))
        ci = jnp.where(first, carry[1], pltpu.roll(xi, 1, 0))
        carry[0] = jnp.broadcast_to(xr[7:8], (8, W))
        carry[1] = jnp.broadcast_to(xi[7:8], (8, W))

        def carried(j, c):
            rows = pl.ds(pl.multiple_of(j * 8, 8), 8)
            cr, ci = _cmul(ar, ai, c[0], c[1])
            s_ref[rows, :W] = s_ref[rows, :W] + cr
            s_ref[rows, W:] = s_ref[rows, W:] + ci
            return cr, ci

        lax.fori_loop(0, CH, carried, (cr, ci))
        y = jnp.dot(s_ref[...].astype(BF), cw[...], preferred_element_type=F32) + d_ref[...] * hp
        y_ref[...] = y
        z_ref[...] = jnp.dot(unperm[...], _gelu(y).astype(BF), preferred_element_type=F32).astype(z_ref.dtype)

    blk = lambda kb, t: (t, kb)
    per_kb = lambda kb, t: (kb, 0, 0)
    return pl.pallas_call(
        body, name=name, grid=(nkb, L // TL),
        in_specs=[pl.BlockSpec((TL, KB), blk), pl.BlockSpec((None, KB, P2), per_kb),
                  pl.BlockSpec((None, P2, KB), per_kb), pl.BlockSpec((None, 40, W2), per_kb),
                  pl.BlockSpec((1, KB), lambda kb, t: (0, kb))],
        out_specs=[pl.BlockSpec((TL, W2), blk), pl.BlockSpec((TL, KB), blk), pl.BlockSpec((TL, KB), blk)],
        out_shape=[jax.ShapeDtypeStruct((L, nkb * W2), F32), jax.ShapeDtypeStruct((L, D), F32),
                   jax.ShapeDtypeStruct((L, D), BF)],
        scratch_shapes=[pltpu.VMEM((KB, W2), BF), pltpu.VMEM((W2, KB), BF), pltpu.VMEM((TL, TL), BF),
                        pltpu.VMEM((TL, TL), BF), pltpu.VMEM((2, 8, W), F32)],
        compiler_params=_cp("parallel", "arbitrary"))(h, tb, tct, pw, dvec)


def s5_bwd(dz, y, h, s, tc, tbt, pwr, dvec, name):
    L, D = h.shape
    nkb, KB, P2 = tc.shape
    P = P2 // 2
    W = (KB // SSM_GROUP) * P
    W2 = 2 * W
    dims = (KB, W2, SSM_GROUP, P, W)
    TL = _tile(L, (512, 256))
    CH = TL // 8
    nt = L // TL
    NB = 2 if nkb % 2 == 0 else 1
    CK = min(S5_CHUNK, W2)
    tn = (((0,), (0,)), ((), ()))

    def body(dz_ref, y_ref, h_ref, s_ref, sp_ref, tc_ref, tbt_ref, pw_ref, d_ref,
             dh_ref, dd_ref, da_ref, db_ref, dc_ref, g, ctw, btw, dbacc, dcacc, dys, perm, unperm, carry):
        t = pl.program_id(1)

        @pl.when(t == 0)
        def _():
            carry[...] = jnp.zeros_like(carry)
            dd_ref[...] = jnp.zeros_like(dd_ref)
            da_ref[...] = jnp.zeros_like(da_ref)
            dbacc[...] = jnp.zeros_like(dbacc)
            dcacc[...] = jnp.zeros_like(dcacc)
            for b in range(NB):
                ctw[b] = _expand(tc_ref[b], dims, False)
                btw[b] = _expand(tbt_ref[b], dims, True)
            perm[...] = _chunk_order(TL, CH, False).astype(perm.dtype)
            unperm[...] = _chunk_order(TL, CH, True).astype(perm.dtype)

        hp = jnp.dot(perm[...], h_ref[...].astype(BF), preferred_element_type=F32)
        dy = jnp.dot(perm[...], dz_ref[...].astype(BF), preferred_element_type=F32) * _gelu_grad(y_ref[...])
        dd_ref[...] += _rowsum8(dy * hp)
        dys[...] = dy
        dyb = dy.astype(BF)
        hpb = hp.astype(BF)
        sub = lax.broadcasted_iota(jnp.int32, (8, W), 0)
        live = jnp.where(t == nt - 1, 0.0, 1.0)

        def lead(b):
            cols = slice(b * KB, (b + 1) * KB)

            def to_states(c):
                def emit():
                    g[b, :, c:c + CK] = jnp.dot(dyb[:, cols], ctw[b, :, c:c + CK], preferred_element_type=F32)
                return emit

            def d_c(c):
                def emit():
                    dcacc[b, :, c:c + CK] += lax.dot_general(dyb[:, cols],
                                                             s_ref[:, b * W2 + c:b * W2 + c + CK].astype(BF), tn,
                                                             preferred_element_type=F32)
                return emit

            return [f(c) for c in range(0, W2, CK) for f in (to_states, d_c)]

        def scan(b):
            re, im = slice(b * W2, b * W2 + W), slice(b * W2 + W, (b + 1) * W2)
            ar, ai = pw_ref[b, 0:8, :W], pw_ref[b, 0:8, W:]
            zero = jnp.zeros((8, W), F32)
            st = {"g": (zero, zero), "acc": (zero, zero)}

            def own(j):
                def emit():
                    rows = slice(j * 8, j * 8 + 8)
                    gr, gi = _cmul_add(g[b, rows, :W], g[b, rows, W:], ar, ai, *st["g"])
                    g[b, rows, :W] = gr
                    g[b, rows, W:] = gi
                    st["g"] = (gr, gi)
                return emit

            def ends():
                gr, gi = st["g"]
                for k, off in ((1, 8), (2, 16), (4, 24)):
                    gr, gi = _cmul_add(gr, gi, pw_ref[b, off:off + 8, :W], pw_ref[b, off:off + 8, W:],
                                       pltpu.roll(gr, 8 - k, 0), pltpu.roll(gi, 8 - k, 0))
                gr, gi = _cmul_add(gr, gi, pw_ref[b, 32:40, :W], pw_ref[b, 32:40, W:], carry[b, 0], carry[b, 1])
                st["c"] = (jnp.where(sub == 7, carry[b, 0], pltpu.roll(gr, 7, 0)),
                           jnp.where(sub == 7, carry[b, 1], pltpu.roll(gi, 7, 0)))
                carry[b, 0] = jnp.broadcast_to(gr[0:1], (8, W))
                carry[b, 1] = jnp.broadcast_to(gi[0:1], (8, W))

            def carried(j):
                def emit():
                    rows = slice(j * 8, j * 8 + 8)
                    cr, ci = _cmul(ar, ai, *st["c"])
                    gr, gi = g[b, rows, :W] + cr, g[b, rows, W:] + ci
                    g[b, rows, :W] = gr
                    g[b, rows, W:] = gi
                    if j > 0:
                        before = slice(j * 8 - 8, j * 8)
                        pr, pi = s_ref[before, re], s_ref[before, im]
                    else:
                        last = slice(TL - 8, TL)
                        pr = jnp.where(sub == 0, sp_ref[7:8, re] * live, pltpu.roll(s_ref[last, re], 1, 0))
                        pi = jnp.where(sub == 0, sp_ref[7:8, im] * live, pltpu.roll(s_ref[last, im], 1, 0))
                    accr, acci = st["acc"]
                    st["c"] = (cr, ci)
                    st["acc"] = (accr + pr * gr + pi * gi, acci + pr * gi - pi * gr)
                return emit

            def done():
                da_ref[b, :, :W] += st["acc"][0]
                da_ref[b, :, W:] += st["acc"][1]

            return ([own(j) for j in reversed(range(CH))] + [ends] + [carried(j) for j in reversed(range(CH))]
                    + [done])

        def tail(b):
            cols = slice(b * KB, (b + 1) * KB)
            acc = {}

            def d_u(c):
                def emit():
                    part = jnp.dot(g[b, :, c:c + CK].astype(BF), btw[b, c:c + CK, :], preferred_element_type=F32)
                    acc["u"] = part if c == 0 else acc["u"] + part
                return emit

            def d_b(c):
                def emit():
                    dbacc[b, :, c:c + CK] += lax.dot_general(hpb[:, cols], g[b, :, c:c + CK].astype(BF), tn,
                                                             preferred_element_type=F32)
                return emit

            def finish():
                dh_ref[:, cols] = _reorder(unperm[...], dys[:, cols] * d_ref[:, cols] + acc["u"])

            return [f(c) for c in range(0, W2, CK) for f in (d_u, d_b)] + [finish]

        for emit in lead(0):
            emit()
        for b in range(NB):
            side = (lead(b + 1) if b + 1 < NB else []) + (tail(b - 1) if b > 0 else [])
            _interleave(scan(b), side)
        for emit in tail(NB - 1):
            emit()

        @pl.when(t == nt - 1)
        def _():
            for b in range(NB):
                db_ref[b] = _extract(dbacc[b], dims)
                dc_ref[b] = _extract(dcacc[b], dims)

    rev = lambda kb, t: (nt - 1 - t, kb)
    prev = lambda kb, t: (jnp.maximum((nt - 1 - t) * CH - 1, 0), kb)
    per_kb = lambda kb, t: (kb, 0, 0)
    return pl.pallas_call(
        body, name=name, grid=(nkb // NB, nt),
        in_specs=[pl.BlockSpec((TL, NB * KB), rev), pl.BlockSpec((TL, NB * KB), rev),
                  pl.BlockSpec((TL, NB * KB), rev), pl.BlockSpec((TL, NB * W2), rev),
                  pl.BlockSpec((8, NB * W2), prev), pl.BlockSpec((NB, KB, P2), per_kb),
                  pl.BlockSpec((NB, P2, KB), per_kb), pl.BlockSpec((NB, 40, W2), per_kb),
                  pl.BlockSpec((1, NB * KB), lambda kb, t: (0, kb))],
        out_specs=[pl.BlockSpec((TL, NB * KB), rev), pl.BlockSpec((8, NB * KB), lambda kb, t: (0, kb)),
                   pl.BlockSpec((NB, 8, W2), per_kb), pl.BlockSpec((NB, KB, P2), per_kb),
                   pl.BlockSpec((NB, KB, P2), per_kb)],
        out_shape=[jax.ShapeDtypeStruct((L, D), F32), jax.ShapeDtypeStruct((8, D), F32),
                   jax.ShapeDtypeStruct((nkb, 8, W2), F32), jax.ShapeDtypeStruct((nkb, KB, P2), F32),
                   jax.ShapeDtypeStruct((nkb, KB, P2), F32)],
        scratch_shapes=[pltpu.VMEM((NB, TL, W2), F32), pltpu.VMEM((NB, KB, W2), BF), pltpu.VMEM((NB, W2, KB), BF),
                        pltpu.VMEM((NB, KB, W2), F32), pltpu.VMEM((NB, KB, W2), F32), pltpu.VMEM((TL, NB * KB), F32),
                        pltpu.VMEM((TL, TL), BF), pltpu.VMEM((TL, TL), BF), pltpu.VMEM((NB, 2, 8, W), F32)],
        compiler_params=pltpu.CompilerParams(dimension_semantics=("parallel", "arbitrary"),
                                             vmem_limit_bytes=V7X_VMEM_BYTES - 4 * 1024 * 1024),
    )(dz, y, h, s, s, tc, tbt, pwr, dvec)


def _s5_bwd_one_block(dz, y, h, s, tc, tbt, pwr, dvec, name):
    L, D = h.shape
    nkb, KB, P2 = tc.shape
    P = P2 // 2
    W = (KB // SSM_GROUP) * P
    W2 = 2 * W
    dims = (KB, W2, SSM_GROUP, P, W)
    TL = _tile(L, (512, 256))
    CH = TL // 8
    nt = L // TL

    def body(dz_ref, y_ref, h_ref, s_ref, sp_ref, tc_ref, tbt_ref, pw_ref, d_ref,
             dh_ref, dd_ref, da_ref, db_ref, dc_ref, g, ctw, btw, dbacc, dcacc, perm, unperm, carry):
        t = pl.program_id(1)

        @pl.when(t == 0)
        def _():
            carry[...] = jnp.zeros_like(carry)
            dd_ref[...] = jnp.zeros_like(dd_ref)
            da_ref[...] = jnp.zeros_like(da_ref)
            dbacc[...] = jnp.zeros_like(dbacc)
            dcacc[...] = jnp.zeros_like(dcacc)
            ctw[...] = _expand(tc_ref[...], dims, False)
            btw[...] = _expand(tbt_ref[...], dims, True)
            perm[...] = _chunk_order(TL, CH, False).astype(perm.dtype)
            unperm[...] = _chunk_order(TL, CH, True).astype(perm.dtype)

        hp = jnp.dot(perm[...], h_ref[...].astype(BF), preferred_element_type=F32)
        dy = jnp.dot(perm[...], dz_ref[...].astype(BF), preferred_element_type=F32) * _gelu_grad(y_ref[...])
        dd_ref[...] += _rowsum8(dy * hp)
        dyb = dy.astype(BF)
        g[...] = jnp.dot(dyb, ctw[...], preferred_element_type=F32)
        ar, ai = pw_ref[0:8, :W], pw_ref[0:8, W:]

        def own(jj, x):
            rows = pl.ds(pl.multiple_of((CH - 1 - jj) * 8, 8), 8)
            gr, gi = _cmul_add(g[rows, :W], g[rows, W:], ar, ai, x[0], x[1])
            g[rows, :W] = gr
            g[rows, W:] = gi
            return gr, gi

        zero = jnp.zeros((8, W), F32)
        gr, gi = lax.fori_loop(0, CH, own, (zero, zero))
        for k, off in ((1, 8), (2, 16), (4, 24)):
            gr, gi = _cmul_add(gr, gi, pw_ref[off:off + 8, :W], pw_ref[off:off + 8, W:],
                               pltpu.roll(gr, 8 - k, 0), pltpu.roll(gi, 8 - k, 0))
        gr, gi = _cmul_add(gr, gi, pw_ref[32:40, :W], pw_ref[32:40, W:], carry[0], carry[1])
        sub = lax.broadcasted_iota(jnp.int32, (8, W), 0)
        cr = jnp.where(sub == 7, carry[0], pltpu.roll(gr, 7, 0))
        ci = jnp.where(sub == 7, carry[1], pltpu.roll(gi, 7, 0))
        carry[0] = jnp.broadcast_to(gr[0:1], (8, W))
        carry[1] = jnp.broadcast_to(gi[0:1], (8, W))

        def carried(jj, c):
            j = CH - 1 - jj
            rows = pl.ds(pl.multiple_of(j * 8, 8), 8)
            before = pl.ds(pl.multiple_of(j * 8 - 8, 8), 8)
            cr, ci = _cmul(ar, ai, c[0], c[1])
            gr, gi = g[rows, :W] + cr, g[rows, W:] + ci
            g[rows, :W] = gr
            g[rows, W:] = gi
            pr, pi = s_ref[before, :W], s_ref[before, W:]
            return cr, ci, c[2] + pr * gr + pi * gi, c[3] + pr * gi - pi * gr

        cr, ci, accr, acci = lax.fori_loop(0, CH - 1, carried, (cr, ci, zero, zero))
        live = jnp.where(t == nt - 1, 0.0, 1.0)
        cr, ci = _cmul(ar, ai, cr, ci)
        gr, gi = g[0:8, :W] + cr, g[0:8, W:] + ci
        g[0:8, :W] = gr
        g[0:8, W:] = gi
        pr = jnp.where(sub == 0, sp_ref[7:8, :W] * live, pltpu.roll(s_ref[TL - 8:TL, :W], 1, 0))
        pi = jnp.where(sub == 0, sp_ref[7:8, W:] * live, pltpu.roll(s_ref[TL - 8:TL, W:], 1, 0))
        da_ref[:, :W] += accr + pr * gr + pi * gi
        da_ref[:, W:] += acci + pr * gi - pi * gr

        gb = g[...].astype(BF)
        dh = dy * d_ref[...] + jnp.dot(gb, btw[...], preferred_element_type=F32)
        dh_ref[...] = _reorder(unperm[...], dh)
        tn = (((0,), (0,)), ((), ()))
        dbacc[...] += lax.dot_general(hp.astype(BF), gb, tn, preferred_element_type=F32)
        dcacc[...] += lax.dot_general(dyb, s_ref[...].astype(BF), tn, preferred_element_type=F32)

        @pl.when(t == nt - 1)
        def _():
            db_ref[...] = _extract(dbacc[...], dims)
            dc_ref[...] = _extract(dcacc[...], dims)

    rev = lambda kb, t: (nt - 1 - t, kb)
    prev = lambda kb, t: (jnp.maximum((nt - 1 - t) * CH - 1, 0), kb)
    per_kb = lambda kb, t: (kb, 0, 0)
    return pl.pallas_call(
        body, name=name, grid=(nkb, nt),
        in_specs=[pl.BlockSpec((TL, KB), rev), pl.BlockSpec((TL, KB), rev), pl.BlockSpec((TL, KB), rev),
                  pl.BlockSpec((TL, W2), rev), pl.BlockSpec((8, W2), prev),
                  pl.BlockSpec((None, KB, P2), per_kb), pl.BlockSpec((None, P2, KB), per_kb),
                  pl.BlockSpec((None, 40, W2), per_kb), pl.BlockSpec((1, KB), lambda kb, t: (0, kb))],
        out_specs=[pl.BlockSpec((TL, KB), rev), pl.BlockSpec((8, KB), lambda kb, t: (0, kb)),
                   pl.BlockSpec((None, 8, W2), per_kb), pl.BlockSpec((None, KB, P2), per_kb),
                   pl.BlockSpec((None, KB, P2), per_kb)],
        out_shape=[jax.ShapeDtypeStruct((L, D), F32), jax.ShapeDtypeStruct((8, D), F32),
                   jax.ShapeDtypeStruct((nkb, 8, W2), F32), jax.ShapeDtypeStruct((nkb, KB, P2), F32),
                   jax.ShapeDtypeStruct((nkb, KB, P2), F32)],
        scratch_shapes=[pltpu.VMEM((TL, W2), F32), pltpu.VMEM((KB, W2), BF), pltpu.VMEM((W2, KB), BF),
                        pltpu.VMEM((KB, W2), F32), pltpu.VMEM((KB, W2), F32), pltpu.VMEM((TL, TL), BF),
                        pltpu.VMEM((TL, TL), BF), pltpu.VMEM((2, 8, W), F32)],
        compiler_params=_cp("parallel", "arbitrary"))(dz, y, h, s, s, tc, tbt, pwr, dvec)


def _discretise(a_re, a_im, log_step, b_re, b_im):
    lr = jnp.minimum(a_re, -1e-4)
    li = a_im
    dt = jnp.exp(log_step)[:, None]
    mag = jnp.exp(lr * dt)
    abr = mag * jnp.cos(li * dt)
    abi = mag * jnp.sin(li * dt)
    den = lr * lr + li * li
    qr = ((abr - 1.0) * lr + abi * li) / den
    qi = (abi * lr - (abr - 1.0) * li) / den
    bbar_re = qr[..., None] * b_re - qi[..., None] * b_im
    bbar_im = qr[..., None] * b_im + qi[..., None] * b_re
    return abr, abi, bbar_re, bbar_im


def _compact(m_re, m_im, nkb):
    G, H, P = m_re.shape
    t = jnp.stack([m_re, m_im], axis=2).reshape(nkb, (G // nkb) * H, 2 * P).astype(BF)
    return t, jnp.swapaxes(t, 1, 2)


def _scan_powers(abr, abi, nkb, conj, CH):
    G, P = abr.shape
    if conj:
        abi = -abi

    def cmul(u, v):
        return u[0] * v[0] - u[1] * v[1], u[0] * v[1] + u[1] * v[0]

    q = (abr, abi)
    for _ in range(_log2(CH)):
        q = cmul(q, q)
    pows = [q]
    for _ in range(7):
        pows.append(cmul(pows[-1], q))
    row = jnp.arange(8)[:, None, None]

    def table(part):
        out = [jnp.broadcast_to((abr, abi)[part][None], (8, G, P))]
        for k in (1, 2, 4):
            keep = (row <= 7 - k) if conj else (row >= k)
            out.append(jnp.where(keep, pows[k - 1][part][None], 0.0))
        ends = jnp.stack([p[part] for p in pows])
        out.append(ends[::-1] if conj else ends)
        return jnp.concatenate(out, axis=0)

    GL = G // nkb
    t = jnp.stack([table(0), table(1)], axis=1)
    t = t.reshape(40, 2, nkb, GL * P).transpose(2, 0, 1, 3)
    return t.reshape(nkb, 40, 2 * GL * P)


def ada_mods(c_all, w_ada, b_sh, name):
    nl, D, NA = w_ada.shape

    def body(c_ref, w_ref, b_ref, o_ref):
        cv = c_ref[...]
        act = cv * jax.nn.sigmoid(cv)
        o_ref[...] = jnp.dot(act, w_ref[...], preferred_element_type=F32, precision=lax.Precision.HIGHEST) + b_ref[...]

    return pl.pallas_call(
        body, name=name, grid=(nl,),
        in_specs=[pl.BlockSpec((8, D), lambda i: (0, 0)), pl.BlockSpec((None, D, NA), lambda i: (i, 0, 0)),
                  pl.BlockSpec((None, 1, NA), lambda i: (i, 0, 0))],
        out_specs=pl.BlockSpec((None, 8, NA), lambda i: (i, 0, 0)),
        out_shape=jax.ShapeDtypeStruct((nl, 8, NA), F32), compiler_params=_cp("parallel"))(c_all, w_ada, b_sh)


def _adamw(w, g, m, v):
    m = ADAM_B1 * m + (1.0 - ADAM_B1) * g
    v = ADAM_B2 * v + (1.0 - ADAM_B2) * (g * g)
    m_hat = m / (1.0 - ADAM_B1 ** ADAM_STEP)
    v_hat = v / (1.0 - ADAM_B2 ** ADAM_STEP)
    return -ADAM_LR * (m_hat / (jnp.sqrt(v_hat) + ADAM_EPS) + ADAM_WD * w), m, v


def _adam_rows(R, C):
    cap = max(8, (256 * 1024) // C)
    for t in range(min(R, cap), 0, -1):
        if R % t == 0 and (t % 8 == 0 or t == R):
            return t
    return R


def adamw_ada(c_t, dm, w, m, v, name):
    nl, D, NA = w.shape
    TK = _tile(D, (256, 128))

    def body(c_ref, dm_ref, w_ref, m_ref, v_ref, g_ref, d_ref, nm_ref, nv_ref):
        cv = c_ref[...]
        act = cv * jax.nn.sigmoid(cv)
        g = jnp.dot(act, dm_ref[...], preferred_element_type=F32, precision=lax.Precision.HIGHEST)
        g_ref[...] = g
        d_ref[...], nm_ref[...], nv_ref[...] = _adamw(w_ref[...], g, m_ref[...], v_ref[...])

    big = pl.BlockSpec((None, TK, NA), lambda i, k: (i, k, 0))
    shape = jax.ShapeDtypeStruct(w.shape, F32)
    return pl.pallas_call(
        body, name=name, grid=(nl, D // TK),
        in_specs=[pl.BlockSpec((TK, 8), lambda i, k: (k, 0)), pl.BlockSpec((None, 8, NA), lambda i, k: (i, 0, 0)),
                  big, big, big],
        out_specs=[big] * 4, out_shape=[shape] * 4, compiler_params=_cp("parallel", "parallel"))(c_t, dm, w, m, v)


def adamw_sharded(w, m, v, ga, gb, name):
    nl, R, C = w.shape
    TR = _adam_rows(R, C)

    def body(w_ref, m_ref, v_ref, a_ref, b_ref, g_ref, d_ref, nm_ref, nv_ref):
        g = a_ref[...] + b_ref[...]
        g_ref[...] = g
        d_ref[...], nm_ref[...], nv_ref[...] = _adamw(w_ref[...], g, m_ref[...], v_ref[...])

    big = pl.BlockSpec((None, TR, C), lambda i, r: (i, r, 0))
    shape = jax.ShapeDtypeStruct(w.shape, F32)
    return pl.pallas_call(
        body, name=name, grid=(nl, R // TR), in_specs=[big] * 5,
        out_specs=[big] * 4, out_shape=[shape] * 4, compiler_params=_cp("parallel", "parallel"))(w, m, v, ga, gb)


def adamw_slab(g, w, m, v, name):
    R, C = g.shape
    TR = _tile(R, (160, 80, 40, 8))

    def body(g_ref, w_ref, m_ref, v_ref, d_ref, nm_ref, nv_ref):
        d_ref[...], nm_ref[...], nv_ref[...] = _adamw(w_ref[...], g_ref[...], m_ref[...], v_ref[...])

    big = pl.BlockSpec((TR, C), lambda r: (r, 0))
    shape = jax.ShapeDtypeStruct((R, C), F32)
    return pl.pallas_call(
        body, name=name, grid=(R // TR,), in_specs=[big] * 4,
        out_specs=[big] * 3, out_shape=[shape] * 3, compiler_params=_cp("parallel"))(g, w, m, v)


def adamw_plain(w, m, v, g, name):
    def body(w_ref, m_ref, v_ref, g_ref, d_ref, nm_ref, nv_ref):
        d_ref[...], nm_ref[...], nv_ref[...] = _adamw(w_ref[...], g_ref[...], m_ref[...], v_ref[...])

    shape = jax.ShapeDtypeStruct(w.shape, F32)
    return pl.pallas_call(body, name=name, out_shape=[shape] * 3,
                          compiler_params=pltpu.CompilerParams(vmem_limit_bytes=VMEM_LIMIT))(w, m, v, g)


def _slab_rows(a):
    n = a.size
    rows = -(-n // SLAB_W)
    return -(-rows // 8) * 8


def _pack(arrs, pad_rows_to=0):
    out = []
    for a in arrs:
        rows = _slab_rows(a)
        flat = a.reshape(-1).astype(F32)
        flat = jnp.pad(flat, (0, rows * SLAB_W - flat.shape[0]))
        out.append(flat.reshape(rows, SLAB_W))
    total = sum(o.shape[0] for o in out)
    if pad_rows_to and total % pad_rows_to:
        out.append(jnp.zeros((pad_rows_to - total % pad_rows_to, SLAB_W), F32))
    return jnp.concatenate(out, axis=0)


def _unpack(slab, like):
    out, r = [], 0
    for a in like:
        rows = _slab_rows(a)
        out.append(slab[r:r + rows].reshape(-1)[:a.size].reshape(a.shape))
        r += rows
    return out


WEIGHTS = ['norm1_g', 'norm2_g', 'w_ada', 'b_ada', 'ssm_a_re', 'ssm_a_im', 'ssm_log_step', 'ssm_b_re', 'ssm_b_im',
           'ssm_c_re', 'ssm_c_im', 'ssm_d', 'ssm_w_out', 'conv_w_in', 'conv_w', 'conv_w_out', 'w_ffn_in',
           'w_ffn_out', 'final_g']
SLAB = ['norm1_g', 'norm2_g', 'b_ada', 'ssm_a_re', 'ssm_a_im', 'ssm_log_step', 'ssm_b_re', 'ssm_b_im', 'ssm_c_re',
        'ssm_c_im', 'ssm_d', 'final_g']
SHARDED = ['ssm_w_out', 'conv_w_in', 'conv_w_out', 'w_ffn_in', 'w_ffn_out']


def kernel(x, c, norm1_g, norm2_g, w_ada, b_ada, ssm_a_re, ssm_a_im, ssm_log_step, ssm_b_re, ssm_b_im, ssm_c_re, ssm_c_im, ssm_d, ssm_w_out, conv_w_in, conv_w, conv_w_out, w_ffn_in, w_ffn_out, final_g, loss_target, m_norm1_g, m_norm2_g, m_w_ada, m_b_ada, m_ssm_a_re, m_ssm_a_im, m_ssm_log_step, m_ssm_b_re, m_ssm_b_im, m_ssm_c_re, m_ssm_c_im, m_ssm_d, m_ssm_w_out, m_conv_w_in, m_conv_w, m_conv_w_out, m_w_ffn_in, m_w_ffn_out, m_final_g, v_norm1_g, v_norm2_g, v_w_ada, v_b_ada, v_ssm_a_re, v_ssm_a_im, v_ssm_log_step, v_ssm_b_re, v_ssm_b_im, v_ssm_c_re, v_ssm_c_im, v_ssm_d, v_ssm_w_out, v_conv_w_in, v_conv_w, v_conv_w_out, v_w_ffn_in, v_w_ffn_out, v_final_g):
    given = dict(locals())
    W = {n: given[n] for n in WEIGHTS}
    Mo = {n: given["m_" + n] for n in WEIGHTS}
    Vo = {n: given["v_" + n] for n in WEIGHTS}

    xs = x[0]
    tgt = loss_target[0]
    L, D = xs.shape
    nlayer = norm1_g.shape[0]
    NA = w_ada.shape[2]
    G = ssm_a_re.shape[1]
    nkb = D // S5_BLOCK
    ax, ay, ac = _axes()
    me = 4 * ax + 2 * ay + ac
    chip = 2 * ax + ay

    c_all = gather8(jnp.broadcast_to(c, (8, D)), "gather_c")[:, 0, :]
    b_sh = lax.dynamic_slice_in_dim(b_ada, chip * NA, NA, axis=1)[:, None, :]
    mods_part = ada_mods(c_all, w_ada, b_sh, "ada_mods")
    mg = gather8(mods_part.reshape(nlayer * 8, NA), "gather_mods")
    mg = mg.reshape(N_CHIP, 2, nlayer, 8, NA)[:, 0]
    mods_all = lax.dynamic_index_in_dim(mg, me, axis=2, keepdims=False)
    mods_all = jnp.transpose(mods_all, (1, 0, 2)).reshape(nlayer, 6, D)

    cw_parts = gather8(_pack([conv_w]), "gather_conv_w")
    nconv = conv_w.shape[0]
    cw_full = jnp.stack([_unpack(cw_parts[2 * q], [conv_w])[0] for q in range(N_CHIP)], axis=2)
    cw_full = cw_full.reshape(nconv, 3, D)

    in_flight_w = {}

    def start_weights(i, after):
        names = (["ssm_w_out"] if i % 2 == 0 else ["conv_w_in", "conv_w_out"]) + ["w_ffn_in", "w_ffn_out"]
        shards = [W[n][i if n.startswith("w_ffn") else i // 2].astype(BF) for n in names]
        sems, srcs, lands, tok = gather_start(shards, after, "gather_start%d" % i)
        in_flight_w[i] = (names, sems, srcs, lands)
        return tok

    def layer_weights(i, after):
        names, sems, srcs, lands = in_flight_w[i]
        got = gather_wait(sems, srcs, lands, list(range(len(names))), after, "gather_wait%d" % i)
        return dict(zip(names, got))

    token = start_weights(0, cw_full + mods_all[0, 0:3])
    mods_all = mods_all + token[0:1, 0:1]

    s5 = []
    for j in range(ssm_a_re.shape[0]):
        disc, disc_vjp = jax.vjp(_discretise, ssm_a_re[j], ssm_a_im[j], ssm_log_step[j], ssm_b_re[j], ssm_b_im[j])
        abr, abi, bbar_re, bbar_im = disc
        tb, tbt = _compact(jnp.swapaxes(bbar_re, 1, 2), jnp.swapaxes(bbar_im, 1, 2), nkb)
        tc, tct = _compact(ssm_c_re[j], -ssm_c_im[j], nkb)
        chunk = _tile(L, (512, 256)) // 8
        s5.append(dict(vjp=disc_vjp, tb=tb, tbt=tbt, tc=tc, tct=tct, pw=_scan_powers(abr, abi, nkb, False, chunk),
                       pwr=_scan_powers(abr, abi, nkb, True, chunk)))

    saved = []
    xcur = xs
    for i in range(nlayer):
        j = i // 2
        mods = mods_all[i]
        sv = dict(x=xcur)
        nxt = i + 1 < nlayer
        if i % 2 == 0:
            if nxt and i > 0:
                mods = mods + start_weights(i + 1, xcur)[0:1, 0:1]
            h = norm_mod(xcur, norm1_g[i:i + 1], mods, 0, F32, "norm_mod_s5")
            states, yv, z = s5_fwd(h, s5[j]["tb"], s5[j]["tct"], s5[j]["pw"], ssm_d[j:j + 1], "s5_fwd")
            full = layer_weights(i, z)
            if nxt and i == 0:
                mods = mods + start_weights(i + 1, full["ssm_w_out"])[0:1, 0:1]
            o, mix, x2 = ssm_out_glu(z, full["ssm_w_out"], xcur, mods, 2, "ssm_out_glu")
            sv.update(h=h, states=states, y=yv, z=z, o=o)
        else:
            h = norm_mod(xcur, norm1_g[i:i + 1], mods, 0, BF, "norm_mod")
            full = layer_weights(i, h)
            if nxt:
                mods = mods + start_weights(i + 1, full["conv_w_in"])[0:1, 0:1]
            p = mm_nn(h, full["conv_w_in"], BF, "mm_conv_in")
            mc = conv_fwd(p, cw_full[j], "conv_fwd")
            mix, x2 = mm_nn(mc, full["conv_w_out"].reshape(1, D, D), BF, "mm_conv_out", res=xcur, gate=mods[2:3])
            sv.update(h=h, p=p, mc=mc)
        h2 = norm_mod(x2, norm2_g[i:i + 1], mods, 3, BF, "norm_mod")
        gu, act = ffn_in_act(h2, full["w_ffn_in"], "ffn_in_act")
        F = act.shape[1]
        ff, x3 = mm_nn(act, full["w_ffn_out"].reshape(1, F, D), BF, "mm_ffn_out", res=x2, gate=mods[5:6])
        sv.update(mix=mix, x2=x2, h2=h2, gu=gu, act=act, ff=ff, w=full)
        saved.append(sv)
        xcur = x3

    loss_blk, dx, dfinal, dff = final_loss(xcur, tgt, final_g[None, :], saved[-1]["ff"], mods_all[nlayer - 1], 5,
                                           "final_loss")
    dg2 = dfinal[1:2]

    gland = {n: lax.empty((W[n].shape[0], N_CHIP) + W[n].shape[1:], BF) for n in SHARDED}
    in_flight = []
    dmods = [None] * nlayer
    dnorm1, dnorm2 = [None] * nlayer, [None] * nlayer
    dconv_w = [None] * nconv
    ds5 = [None] * ssm_a_re.shape[0]
    token = jnp.zeros((8, 128), F32)

    def send_grads(names, grads, slot, after, name):
        sems, thru, lands, tok = scatter_start([grads[n] for n in names], [gland[n] for n in names], slot, after, name)
        gland.update(zip(names, lands))
        in_flight.append((names, slot, sems, thru, name))
        return tok

    def land_grads(group, after):
        for names, slot, sems, thru, name in in_flight:
            if names[0] in group:
                got = scatter_wait(sems, thru, [gland[n] for n in names], slot, after, name.replace("scatter", "landed"))
                gland.update(zip(names, got))

    for i in reversed(range(nlayer)):
        j = i // 2
        mods = mods_all[i] + token[0:1, 0:1]
        sv = saved[i]
        full = sv["w"]
        gfull = {}
        F = sv["act"].shape[1]
        gfull["w_ffn_out"] = mm_tn(sv["act"], dff, 1, "mm_tn_ffn_out").reshape(N_CHIP, F // N_CHIP, D)
        dgu = ffn_out_bwd(dff, full["w_ffn_out"].reshape(F, D), sv["gu"], "ffn_out_bwd")
        gfull["w_ffn_in"] = mm_tn(sv["h2"], dgu, N_CHIP, "mm_tn_ffn_in")
        dh2 = mm_nt(dgu, full["w_ffn_in"], F32, "mm_nt_ffn_in")
        token = send_grads(["w_ffn_out", "w_ffn_in"], gfull, [i, i], dh2, "scatter_ffn%d" % i)
        mods = mods + token[0:1, 0:1]
        dx2, s2, dmix = norm_bwd(dh2, sv["x2"], dx, norm2_g[i:i + 1], mods, 3, "norm_bwd_mix",
                                 branch=(sv["mix"], mods, 2))
        dg1 = s2[3:4]
        if i % 2 == 0:
            do = glu_bwd(dmix, sv["o"], "glu_bwd")
            gfull["ssm_w_out"] = mm_tn(sv["z"], do, N_CHIP, "mm_tn_ssm_out")
            dz = mm_nt(do, full["ssm_w_out"], BF, "mm_nt_ssm_out")
            dh, dd, dab, db, dc = s5_bwd(dz, sv["y"], sv["h"], sv["states"], s5[j]["tc"], s5[j]["tbt"], s5[j]["pwr"],
                                         ssm_d[j:j + 1], "s5_bwd")
            ds5[j] = (dd, dab, db, dc)
        else:
            gfull["conv_w_out"] = mm_tn(sv["mc"], dmix, 1, "mm_tn_conv_out").reshape(N_CHIP, D // N_CHIP, D)
            dmc = mm_nt(dmix, full["conv_w_out"].reshape(1, D, D), BF, "mm_nt_conv_out")
            dbg, dcg, dvv, dcw = conv_bwd(dmc, sv["p"], cw_full[j], "conv_bwd")
            dp = jnp.concatenate([dbg, dcg, dvv], axis=1)
            gfull["conv_w_in"] = mm_tn(sv["h"], dp, N_CHIP, "mm_tn_conv_in")
            dh = mm_nt(dp, full["conv_w_in"], F32, "mm_nt_conv_in")
            dconv_w[j] = dcw[0:3]
        dmods_i = [s2[0:2], dg2]
        if i > 0:
            dx, s1, dff = norm_bwd(dh, sv["x"], dx2, norm1_g[i:i + 1], mods, 0, "norm_bwd_ffn",
                                   branch=(saved[i - 1]["ff"], mods_all[i - 1], 5))
            dg2 = s1[3:4]
        else:
            dx, s1 = norm_bwd(dh, sv["x"], dx2, norm1_g[i:i + 1], mods, 0, "norm_bwd")
        dmods[i] = jnp.concatenate([s1[0:2], dg1] + dmods_i, axis=0).reshape(6 * D)
        dnorm1[i], dnorm2[i] = s1[2], s2[2]
        names = ["ssm_w_out"] if i % 2 == 0 else ["conv_w_out", "conv_w_in"]
        token = send_grads(names, gfull, [j] * len(names), dx, "scatter_mix%d" % i)

    small = dict(norm1_g=jnp.stack(dnorm1), norm2_g=jnp.stack(dnorm2), b_ada=jnp.stack(dmods), final_g=dfinal[0])
    per = {n: [] for n in ('ssm_a_re', 'ssm_a_im', 'ssm_log_step', 'ssm_b_re', 'ssm_b_im', 'ssm_c_re', 'ssm_c_im', 'ssm_d')}
    GL = G // nkb
    for j, (dd, dab, db, dc) in enumerate(ds5):
        dab = jnp.sum(dab, axis=1).reshape(nkb, 2, GL, SSM_STATE)
        g_abr, g_abi = dab[:, 0].reshape(G, SSM_STATE), dab[:, 1].reshape(G, SSM_STATE)
        db, dc = db.reshape(G, SSM_GROUP, 2, SSM_STATE), dc.reshape(G, SSM_GROUP, 2, SSM_STATE)
        gb_re, gb_im, gc_re, gc_im = db[:, :, 0], db[:, :, 1], dc[:, :, 0], dc[:, :, 1]
        ga_re, ga_im, gls, gbr, gbi = s5[j]["vjp"]((g_abr, g_abi, jnp.swapaxes(gb_re, 1, 2), jnp.swapaxes(gb_im, 1, 2)))
        for n, val in zip(per, (ga_re, ga_im, gls, gbr, gbi, gc_re, -gc_im, jnp.sum(dd, axis=0))):
            per[n].append(val)
    small.update({n: jnp.stack(vals) for n, vals in per.items()})
    dcw_full = jnp.stack(dconv_w)

    slab_like = [W[n] for n in SLAB] + [dcw_full]
    rows64 = 8 * N_DEV
    slab = _pack([small[n] for n in SLAB] + [dcw_full], rows64)
    per_dev = slab.shape[0] // N_DEV
    x_sems, x_srcs, x_lands, token = exchange_start(
        [(slab.reshape(N_DEV, per_dev, SLAB_W), True), (_pack([small["b_ada"]]), False)], dx, "small_scatter")

    early = [n for n in SHARDED if n != "ssm_w_out"]
    land_grads(early, token)
    mine = [reduce4(gland[n], "reduce4_" + n) for n in early]

    parts, dm_all = exchange_wait(x_sems, x_srcs, x_lands, [True, False], mine[-1][0, :8, :128], "small_landed")
    t_sems, t_srcs, t_lands, token = exchange_start([(sum8(parts, "sum_small"), False)], dm_all, "small_gather")
    out = {}

    w_sems, w_srcs, w_lands, token2 = swap_start(mine, "swap_start")
    dm_all = dm_all.reshape(N_DEV, -1)[:, :b_ada.size].reshape(N_DEV, nlayer, N_CHIP, NA)
    dm_sh = jnp.transpose(lax.dynamic_index_in_dim(dm_all, chip, axis=2, keepdims=False), (1, 0, 2))
    res = adamw_ada(jnp.transpose(c_all) + token[0:1, 0:1] + token2[0:1, 0:1], dm_sh, w_ada, m_w_ada, v_w_ada,
                    "adamw_ada")
    out["g", "w_ada"], out["d", "w_ada"], out["m", "w_ada"], out["v", "w_ada"] = res

    g_slab = exchange_wait(t_sems, t_srcs, t_lands, [False], out["g", "w_ada"], "small_total")[0]
    g_slab = g_slab.reshape(slab.shape)
    d_slab, m_slab, v_slab = adamw_slab(
        g_slab, _pack([W[n] for n in SLAB] + [jnp.zeros_like(dcw_full)], rows64),
        _pack([Mo[n] for n in SLAB] + [jnp.zeros_like(dcw_full)], rows64),
        _pack([Vo[n] for n in SLAB] + [jnp.ones_like(dcw_full)], rows64), "adamw_slab")
    for k, slab in zip(("g", "d", "m", "v"), (g_slab, d_slab, m_slab, v_slab)):
        for n, val in zip(SLAB, _unpack(slab, slab_like)):
            out[k, n] = val
    g_cw = lax.dynamic_slice_in_dim(_unpack(g_slab, slab_like)[-1], chip * conv_w.shape[2], conv_w.shape[2], axis=2)
    out["g", "conv_w"] = g_cw
    out["d", "conv_w"], out["m", "conv_w"], out["v", "conv_w"] = [
        r.reshape(conv_w.shape) for r in adamw_plain(conv_w.reshape(-1, conv_w.shape[2]), m_conv_w.reshape(-1, conv_w.shape[2]),
                                                     v_conv_w.reshape(-1, conv_w.shape[2]), g_cw.reshape(-1, conv_w.shape[2]),
                                                     "adamw_conv_w")]

    mine, theirs = swap_wait(w_sems, w_srcs, w_lands, d_slab, "swap_wait")
    for n, ga, gb in zip(early, mine, theirs):
        r = adamw_sharded(W[n], Mo[n], Vo[n], ga, gb, "adamw_" + n)
        out["g", n], out["d", n], out["m", n], out["v", n] = r

    land_grads(["ssm_w_out"], out["g", "w_ffn_out"])
    ga = reduce4(gland["ssm_w_out"], "reduce4_ssm_w_out")
    gb = swap_siblings([ga], "swap_siblings")[0]
    r = adamw_sharded(ssm_w_out, m_ssm_w_out, v_ssm_w_out, ga, gb, "adamw_ssm_w_out")
    out["g", "ssm_w_out"], out["d", "ssm_w_out"], out["m", "ssm_w_out"], out["v", "ssm_w_out"] = r

    loss = lax.psum(loss_blk[0, 0], ("x", "y", "c"))
    return (loss, dx[None], *[out["g", n] for n in WEIGHTS], *[out["d", n] for n in WEIGHTS],
            *[out["m", n] for n in WEIGHTS], *[out["v", n] for n in WEIGHTS])
```

```python
import functools
import math

import jax
import jax.numpy as jnp
from jax import lax
from jax.experimental import pallas as pl
from jax.experimental.pallas import tpu as pltpu

F32 = jnp.float32
BF = jnp.bfloat16
MESH = pl.DeviceIdType.MESH
ANY = pl.BlockSpec(memory_space=pl.ANY)

N_DEV = 8
N_CHIP = 4
DEPTH = 4
SSM_GROUP = 16
SSM_STATE = 64
S5_BLOCK = 256
RMS_EPS = 1e-6
ADAM_LR, ADAM_B1, ADAM_B2, ADAM_EPS, ADAM_WD, ADAM_STEP = 0.001, 0.9, 0.999, 1e-08, 0.01, 10
V7X_VMEM_BYTES = 64 * 1024 * 1024
VMEM_LIMIT = V7X_VMEM_BYTES - 12 * 1024 * 1024
SLAB_W = 1024
GELU_C = math.sqrt(2.0 / math.pi)
GELU_A = 0.044715


def _cp(*sem):
    return pltpu.CompilerParams(dimension_semantics=sem if sem else None, vmem_limit_bytes=VMEM_LIMIT)


def _tile(n, prefs):
    for p in prefs:
        if p <= n and n % p == 0:
            return p
    return n


def _axes():
    return lax.axis_index("x"), lax.axis_index("y"), lax.axis_index("c")


def _flip(v, k):
    return 1 - v if k else v


def gather8(v, name):
    R, C = v.shape

    def body(v_ref, o_ref, ssem, rsem, lsem):
        x, y, c = _axes()
        me = 4 * x + 2 * y + c
        loc = pltpu.make_async_copy(v_ref, o_ref.at[me], lsem)
        loc.start()
        copies = []
        for k in range(1, N_DEV):
            peer = (_flip(x, (k >> 2) & 1), _flip(y, (k >> 1) & 1), _flip(c, k & 1))
            cp = pltpu.make_async_remote_copy(src_ref=v_ref, dst_ref=o_ref.at[me], send_sem=ssem.at[k - 1],
                                              recv_sem=rsem.at[k - 1], device_id=peer, device_id_type=MESH)
            cp.start()
            copies.append(cp)
        for cp in copies:
            cp.wait()
        loc.wait()

    return pl.pallas_call(
        body, name=name,
        out_shape=jax.ShapeDtypeStruct((N_DEV, R, C), v.dtype),
        in_specs=[pl.BlockSpec(memory_space=pltpu.VMEM)],
        out_specs=pl.BlockSpec(memory_space=pltpu.VMEM),
        scratch_shapes=[pltpu.SemaphoreType.DMA((N_DEV - 1,)), pltpu.SemaphoreType.DMA((N_DEV - 1,)),
                        pltpu.SemaphoreType.DMA],
        compiler_params=pltpu.CompilerParams(vmem_limit_bytes=VMEM_LIMIT),
    )(v)


HBM = pl.BlockSpec(memory_space=pltpu.HBM)
SEM = pl.BlockSpec(memory_space=pltpu.SEMAPHORE)
EFFECT = pltpu.SideEffectType.DATAFLOW_SIDE_EFFECTING


def _in_hbm(a):
    return pltpu.with_memory_space_constraint(a, pltpu.HBM)


def _chip_peers(x, y, c):
    out = []
    for k in range(1, N_CHIP):
        px, py = _flip(x, k >> 1), _flip(y, k & 1)
        out.append(((px, py, c), 2 * px + py))
    return out


def _my_half(ref, c):
    rows = ref.shape[0] // 2
    return pl.ds(pl.multiple_of(c * rows, 16), rows)


def relay_start(lands, after, name):
    n = len(lands)

    def body(*refs):
        land = refs[:n]
        ssem, rsem = refs[n + 1:n + 3]
        token = refs[-1]
        x, y, c = _axes()
        for a in range(n):
            half = _my_half(land[a].at[0], c)
            for k, (_, pchip) in enumerate(_chip_peers(x, y, c)):
                pltpu.make_async_remote_copy(src_ref=land[a].at[pchip, half], dst_ref=land[a].at[pchip, half],
                                             send_sem=ssem.at[3 * a + k], recv_sem=rsem.at[3 * a + k],
                                             device_id=(x, y, 1 - c), device_id_type=MESH).start()
        token[...] = jnp.zeros_like(token)

    out_shape = ([pltpu.SemaphoreType.DMA((3 * n,)), pltpu.SemaphoreType.DMA((3 * n,))]
                 + [pltpu.HBM(l.shape, l.dtype) for l in lands] + [jax.ShapeDtypeStruct((8, 128), F32)])
    res = pl.pallas_call(
        body, name=name, out_shape=out_shape, in_specs=[HBM] * n + [ANY],
        out_specs=[SEM, SEM] + [HBM] * n + [pl.BlockSpec(memory_space=pltpu.VMEM)],
        input_output_aliases={a: 2 + a for a in range(n)},
        compiler_params=pltpu.CompilerParams(has_side_effects=EFFECT),
    )(*lands, after)
    return tuple(res[:2]), list(res[2:2 + n]), res[-1]


def relay_wait(sems, lands, after, name):
    n = len(lands)

    def body(*refs):
        land = refs[:n]
        ssem, rsem = refs[n:n + 2]
        x, y, c = _axes()
        for a in range(n):
            mine, theirs = _my_half(land[a].at[0], c), _my_half(land[a].at[0], 1 - c)
            for k, (_, pchip) in enumerate(_chip_peers(x, y, c)):
                cp = pltpu.make_async_remote_copy(src_ref=land[a].at[pchip, mine], dst_ref=land[a].at[pchip, theirs],
                                                  send_sem=ssem.at[3 * a + k], recv_sem=rsem.at[3 * a + k],
                                                  device_id=(x, y, 1 - c), device_id_type=MESH)
                cp.wait_send()
                cp.wait_recv()

    res = pl.pallas_call(
        body, name=name, out_shape=[pltpu.HBM(l.shape, l.dtype) for l in lands],
        in_specs=[HBM] * n + [SEM, SEM, ANY], out_specs=[HBM] * n,
        input_output_aliases={a: a for a in range(n)},
        compiler_params=pltpu.CompilerParams(has_side_effects=EFFECT),
    )(*lands, *sems, after)
    return list(res)


def gather_start(shards, after, name):
    n = len(shards)

    def body(*refs):
        src, land = refs[:n], refs[n:2 * n]
        ssem, rsem, lsem = refs[2 * n + 1:2 * n + 4]
        token = refs[-1]
        x, y, c = _axes()
        chip = 2 * x + y
        for a in range(n):
            pltpu.make_async_copy(src[a], land[a].at[chip], lsem.at[a]).start()
            half = _my_half(src[a], c)
            for k, (peer, _) in enumerate(_chip_peers(x, y, c)):
                pltpu.make_async_remote_copy(src_ref=src[a].at[half], dst_ref=land[a].at[chip, half],
                                             send_sem=ssem.at[3 * a + k], recv_sem=rsem.at[3 * a + k],
                                             device_id=peer, device_id_type=MESH).start()
        token[...] = jnp.zeros_like(token)

    lands = [lax.empty((N_CHIP,) + s.shape, s.dtype) for s in shards]
    out_shape = ([pltpu.SemaphoreType.DMA((3 * n,)), pltpu.SemaphoreType.DMA((3 * n,)), pltpu.SemaphoreType.DMA((n,))]
                 + [pltpu.HBM(s.shape, s.dtype) for s in shards] + [pltpu.HBM(l.shape, l.dtype) for l in lands]
                 + [jax.ShapeDtypeStruct((8, 128), F32)])
    res = pl.pallas_call(
        body, name=name, out_shape=out_shape, in_specs=[HBM] * (2 * n) + [ANY],
        out_specs=[SEM, SEM, SEM] + [HBM] * (2 * n) + [pl.BlockSpec(memory_space=pltpu.VMEM)],
        input_output_aliases={a: 3 + a for a in range(2 * n)},
        compiler_params=pltpu.CompilerParams(has_side_effects=EFFECT),
    )(*[_in_hbm(s) for s in shards], *[_in_hbm(l) for l in lands], after)
    return tuple(res[:3]), list(res[3:3 + n]), list(res[3 + n:3 + 2 * n]), res[-1]


def gather_wait(sems, srcs, lands, idx, after, name):
    m = len(idx)

    def body(*refs):
        src, land = refs[:m], refs[m:2 * m]
        ssem, rsem, lsem = refs[2 * m:2 * m + 3]
        x, y, c = _axes()
        chip = 2 * x + y
        for j, a in enumerate(idx):
            half = _my_half(src[j], c)
            for k, (peer, pchip) in enumerate(_chip_peers(x, y, c)):
                cp = pltpu.make_async_remote_copy(src_ref=src[j].at[half], dst_ref=land[j].at[pchip, half],
                                                  send_sem=ssem.at[3 * a + k], recv_sem=rsem.at[3 * a + k],
                                                  device_id=peer, device_id_type=MESH)
                cp.wait_send()
                cp.wait_recv()
            pltpu.make_async_copy(src[j], land[j].at[chip], lsem.at[a]).wait()

    s_in = [srcs[a] for a in idx]
    l_in = [lands[a] for a in idx]
    res = pl.pallas_call(
        body, name=name,
        out_shape=[pltpu.HBM(s.shape, s.dtype) for s in s_in] + [pltpu.HBM(l.shape, l.dtype) for l in l_in],
        in_specs=[HBM] * (2 * m) + [SEM, SEM, SEM, ANY], out_specs=[HBM] * (2 * m),
        input_output_aliases={a: a for a in range(2 * m)},
        compiler_params=pltpu.CompilerParams(has_side_effects=EFFECT),
    )(*s_in, *l_in, *sems, after)
    return list(res[m:])


def scatter_start(grads, lands, slot, after, name):
    n = len(grads)

    def body(*refs):
        src, land = refs[:n], refs[n:2 * n]
        ssem, rsem, lsem = refs[2 * n + 1:2 * n + 4]
        token = refs[-1]
        x, y, c = _axes()
        chip = 2 * x + y
        for a in range(n):
            pltpu.make_async_copy(src[a].at[chip], land[a].at[slot[a], chip], lsem.at[a]).start()
            for k, (peer, pchip) in enumerate(_chip_peers(x, y, c)):
                pltpu.make_async_remote_copy(src_ref=src[a].at[pchip], dst_ref=land[a].at[slot[a], chip],
                                             send_sem=ssem.at[3 * a + k], recv_sem=rsem.at[3 * a + k],
                                             device_id=peer, device_id_type=MESH).start()
        token[...] = jnp.zeros_like(token)

    out_shape = ([pltpu.SemaphoreType.DMA((3 * n,)), pltpu.SemaphoreType.DMA((3 * n,)), pltpu.SemaphoreType.DMA((n,))]
                 + [pltpu.HBM(g.shape, g.dtype) for g in grads] + [pltpu.HBM(l.shape, l.dtype) for l in lands]
                 + [jax.ShapeDtypeStruct((8, 128), F32)])
    res = pl.pallas_call(
        body, name=name, out_shape=out_shape, in_specs=[HBM] * (2 * n) + [ANY],
        out_specs=[SEM, SEM, SEM] + [HBM] * (2 * n) + [pl.BlockSpec(memory_space=pltpu.VMEM)],
        input_output_aliases={a: 3 + a for a in range(2 * n)},
        compiler_params=pltpu.CompilerParams(has_side_effects=EFFECT),
    )(*[_in_hbm(g) for g in grads], *[_in_hbm(l) for l in lands], after)
    return tuple(res[:3]), list(res[3:3 + n]), list(res[3 + n:3 + 2 * n]), res[-1]


def scatter_wait(sems, grads, lands, slot, after, name):
    n = len(grads)

    def body(*refs):
        src, land = refs[:n], refs[n:2 * n]
        ssem, rsem, lsem = refs[2 * n:2 * n + 3]
        x, y, c = _axes()
        chip = 2 * x + y
        for a in range(n):
            for k, (peer, pchip) in enumerate(_chip_peers(x, y, c)):
                cp = pltpu.make_async_remote_copy(src_ref=src[a].at[pchip], dst_ref=land[a].at[slot[a], pchip],
                                                  send_sem=ssem.at[3 * a + k], recv_sem=rsem.at[3 * a + k],
                                                  device_id=peer, device_id_type=MESH)
                cp.wait_send()
                cp.wait_recv()
            pltpu.make_async_copy(src[a].at[chip], land[a].at[slot[a], chip], lsem.at[a]).wait()

    res = pl.pallas_call(
        body, name=name,
        out_shape=[pltpu.HBM(g.shape, g.dtype) for g in grads] + [pltpu.HBM(l.shape, l.dtype) for l in lands],
        in_specs=[HBM] * (2 * n) + [SEM, SEM, SEM, ANY], out_specs=[HBM] * (2 * n),
        input_output_aliases={a: a for a in range(2 * n)},
        compiler_params=pltpu.CompilerParams(has_side_effects=EFFECT),
    )(*grads, *lands, *sems, after)
    return list(res[n:])


def reduce4(land, name):
    nl, _, R, C = land.shape
    TR = _adam_rows(R, C)

    def body(l_ref, o_ref):
        o_ref[...] = ((l_ref[0].astype(F32) + l_ref[1].astype(F32)) + l_ref[2].astype(F32)) + l_ref[3].astype(F32)

    return pl.pallas_call(
        body, name=name, grid=(nl, R // TR),
        in_specs=[pl.BlockSpec((None, N_CHIP, TR, C), lambda i, r: (i, 0, r, 0))],
        out_specs=pl.BlockSpec((None, TR, C), lambda i, r: (i, r, 0)),
        out_shape=jax.ShapeDtypeStruct((nl, R, C), F32), compiler_params=_cp("parallel", "parallel"))(land)


def swap_siblings(arrs, name):
    n = len(arrs)

    def body(*refs):
        src, dst = refs[:n], refs[n:2 * n]
        ssem, rsem = refs[2 * n:]
        x, y, c = _axes()
        cps = [pltpu.make_async_remote_copy(src_ref=src[a], dst_ref=dst[a], send_sem=ssem.at[a], recv_sem=rsem.at[a],
                                            device_id=(x, y, 1 - c), device_id_type=MESH) for a in range(n)]
        for cp in cps:
            cp.start()
        for cp in cps:
            cp.wait()

    return pl.pallas_call(
        body, name=name, out_shape=[jax.ShapeDtypeStruct(a.shape, a.dtype) for a in arrs],
        in_specs=[ANY] * n, out_specs=[ANY] * n,
        scratch_shapes=[pltpu.SemaphoreType.DMA((n,)), pltpu.SemaphoreType.DMA((n,))],
        compiler_params=pltpu.CompilerParams(vmem_limit_bytes=VMEM_LIMIT),
    )(*arrs)


def swap_start(arrs, name):
    n = len(arrs)

    def body(*refs):
        src, land = refs[:n], refs[n:2 * n]
        ssem, rsem = refs[2 * n:2 * n + 2]
        token = refs[-1]
        x, y, c = _axes()
        for a in range(n):
            pltpu.make_async_remote_copy(src_ref=src[a], dst_ref=land[a], send_sem=ssem.at[a], recv_sem=rsem.at[a],
                                         device_id=(x, y, 1 - c), device_id_type=MESH).start()
        token[...] = jnp.zeros_like(token)

    lands = [lax.empty(a.shape, a.dtype) for a in arrs]
    out_shape = ([pltpu.SemaphoreType.DMA((n,)), pltpu.SemaphoreType.DMA((n,))]
                 + [pltpu.HBM(a.shape, a.dtype) for a in arrs] * 2 + [jax.ShapeDtypeStruct((8, 128), F32)])
    res = pl.pallas_call(
        body, name=name, out_shape=out_shape, in_specs=[HBM] * (2 * n),
        out_specs=[SEM, SEM] + [HBM] * (2 * n) + [pl.BlockSpec(memory_space=pltpu.VMEM)],
        input_output_aliases={a: 2 + a for a in range(2 * n)},
        compiler_params=pltpu.CompilerParams(has_side_effects=EFFECT),
    )(*[_in_hbm(a) for a in arrs], *[_in_hbm(l) for l in lands])
    return tuple(res[:2]), list(res[2:2 + n]), list(res[2 + n:2 + 2 * n]), res[-1]


def swap_wait(sems, srcs, lands, after, name):
    n = len(srcs)

    def body(*refs):
        src, land = refs[:n], refs[n:2 * n]
        ssem, rsem = refs[2 * n:2 * n + 2]
        x, y, c = _axes()
        for a in range(n):
            cp = pltpu.make_async_remote_copy(src_ref=src[a], dst_ref=land[a], send_sem=ssem.at[a],
                                              recv_sem=rsem.at[a], device_id=(x, y, 1 - c), device_id_type=MESH)
            cp.wait_send()
            cp.wait_recv()

    res = pl.pallas_call(
        body, name=name, out_shape=[pltpu.HBM(a.shape, a.dtype) for a in srcs] * 2,
        in_specs=[HBM] * (2 * n) + [SEM, SEM, ANY], out_specs=[HBM] * (2 * n),
        input_output_aliases={a: a for a in range(2 * n)},
        compiler_params=pltpu.CompilerParams(has_side_effects=EFFECT),
    )(*srcs, *lands, *sems, after)
    return list(res[:n]), list(res[n:])


def _all_peers(x, y, c):
    out = []
    for k in range(1, N_DEV):
        px, py, pc = _flip(x, (k >> 2) & 1), _flip(y, (k >> 1) & 1), _flip(c, k & 1)
        out.append(((px, py, pc), 4 * px + 2 * py + pc))
    return out


def exchange_start(items, after, name):
    n = len(items)

    def body(*refs):
        src, land = refs[:n], refs[n:2 * n]
        ssem, rsem, lsem = refs[2 * n + 1:2 * n + 4]
        token = refs[-1]
        x, y, c = _axes()
        me = 4 * x + 2 * y + c
        for a, (_, scatter) in enumerate(items):
            pltpu.make_async_copy(src[a].at[me] if scatter else src[a], land[a].at[me], lsem.at[a]).start()
            for k, (peer, p) in enumerate(_all_peers(x, y, c)):
                pltpu.make_async_remote_copy(src_ref=src[a].at[p] if scatter else src[a], dst_ref=land[a].at[me],
                                             send_sem=ssem.at[7 * a + k], recv_sem=rsem.at[7 * a + k],
                                             device_id=peer, device_id_type=MESH).start()
        token[...] = jnp.zeros_like(token)

    srcs = [s for s, _ in items]
    lands = [lax.empty(s.shape if sc else (N_DEV,) + s.shape, s.dtype) for s, sc in items]
    out_shape = ([pltpu.SemaphoreType.DMA((7 * n,)), pltpu.SemaphoreType.DMA((7 * n,)), pltpu.SemaphoreType.DMA((n,))]
                 + [pltpu.HBM(s.shape, s.dtype) for s in srcs] + [pltpu.HBM(l.shape, l.dtype) for l in lands]
                 + [jax.ShapeDtypeStruct((8, 128), F32)])
    res = pl.pallas_call(
        body, name=name, out_shape=out_shape, in_specs=[HBM] * (2 * n) + [ANY],
        out_specs=[SEM, SEM, SEM] + [HBM] * (2 * n) + [pl.BlockSpec(memory_space=pltpu.VMEM)],
        input_output_aliases={a: 3 + a for a in range(2 * n)},
        compiler_params=pltpu.CompilerParams(has_side_effects=EFFECT),
    )(*[_in_hbm(s) for s in srcs], *[_in_hbm(l) for l in lands], after)
    return tuple(res[:3]), list(res[3:3 + n]), list(res[3 + n:3 + 2 * n]), res[-1]


def exchange_wait(sems, srcs, lands, scatter, after, name):
    n = len(srcs)

    def body(*refs):
        src, land = refs[:n], refs[n:2 * n]
        ssem, rsem, lsem = refs[2 * n:2 * n + 3]
        x, y, c = _axes()
        me = 4 * x + 2 * y + c
        for a in range(n):
            for k, (peer, p) in enumerate(_all_peers(x, y, c)):
                cp = pltpu.make_async_remote_copy(src_ref=src[a].at[p] if scatter[a] else src[a],
                                                  dst_ref=land[a].at[p], send_sem=ssem.at[7 * a + k],
                                                  recv_sem=rsem.at[7 * a + k], device_id=peer, device_id_type=MESH)
                cp.wait_send()
                cp.wait_recv()
            pltpu.make_async_copy(src[a].at[me] if scatter[a] else src[a], land[a].at[me], lsem.at[a]).wait()

    res = pl.pallas_call(
        body, name=name,
        out_shape=[pltpu.HBM(s.shape, s.dtype) for s in srcs] + [pltpu.HBM(l.shape, l.dtype) for l in lands],
        in_specs=[HBM] * (2 * n) + [SEM, SEM, SEM, ANY], out_specs=[HBM] * (2 * n),
        input_output_aliases={a: a for a in range(2 * n)},
        compiler_params=pltpu.CompilerParams(has_side_effects=EFFECT),
    )(*srcs, *lands, *sems, after)
    return list(res[n:])


def sum8(parts, name):
    _, P, C = parts.shape

    def body(p_ref, o_ref):
        tot = p_ref[0]
        for d in range(1, N_DEV):
            tot = tot + p_ref[d]
        o_ref[...] = tot

    return pl.pallas_call(body, name=name, out_shape=jax.ShapeDtypeStruct((P, C), F32),
                          compiler_params=pltpu.CompilerParams(vmem_limit_bytes=VMEM_LIMIT))(parts)


def reduce8(slab, dm, name):
    RT, C = slab.shape
    P = RT // N_DEV
    R = dm.shape[0]

    def body(s_ref, dm_ref, o_ref, dmo_ref, recv, s1, r1, s2, r2, s3, r3):
        x, y, c = _axes()
        me = 4 * x + 2 * y + c
        mine = pl.ds(pl.multiple_of(me * P, 8), P)
        parts, dms = [], []
        for k in range(1, N_DEV):
            px, py, pc = _flip(x, (k >> 2) & 1), _flip(y, (k >> 1) & 1), _flip(c, k & 1)
            theirs = pl.ds(pl.multiple_of((4 * px + 2 * py + pc) * P, 8), P)
            cp = pltpu.make_async_remote_copy(src_ref=s_ref.at[theirs], dst_ref=recv.at[me], send_sem=s1.at[k - 1],
                                              recv_sem=r1.at[k - 1], device_id=(px, py, pc), device_id_type=MESH)
            cp.start()
            parts.append(cp)
            cd = pltpu.make_async_remote_copy(src_ref=dm_ref, dst_ref=dmo_ref.at[me], send_sem=s3.at[k - 1],
                                              recv_sem=r3.at[k - 1], device_id=(px, py, pc), device_id_type=MESH)
            cd.start()
            dms.append(cd)
        dmo_ref[me] = dm_ref[...]
        recv[me] = s_ref[mine, :]
        for cp in parts:
            cp.wait()
        tot = recv[0]
        for d in range(1, N_DEV):
            tot = tot + recv[d]
        o_ref[mine, :] = tot
        out = []
        for k in range(1, N_DEV):
            peer = (_flip(x, (k >> 2) & 1), _flip(y, (k >> 1) & 1), _flip(c, k & 1))
            cp = pltpu.make_async_remote_copy(src_ref=o_ref.at[mine], dst_ref=o_ref.at[mine], send_sem=s2.at[k - 1],
                                              recv_sem=r2.at[k - 1], device_id=peer, device_id_type=MESH)
            cp.start()
            out.append(cp)
        for cp in out + dms:
            cp.wait()

    sems = [pltpu.SemaphoreType.DMA((N_DEV - 1,))] * 6
    return pl.pallas_call(
        body, name=name,
        out_shape=[jax.ShapeDtypeStruct((RT, C), F32), jax.ShapeDtypeStruct((N_DEV, R, C), F32)],
        in_specs=[pl.BlockSpec(memory_space=pltpu.VMEM)] * 2, out_specs=[pl.BlockSpec(memory_space=pltpu.VMEM)] * 2,
        scratch_shapes=[pltpu.VMEM((N_DEV, P, C), F32)] + sems,
        compiler_params=pltpu.CompilerParams(vmem_limit_bytes=VMEM_LIMIT),
    )(slab, dm)


def mm_nn(a, w, out_dtype, name, res=None, gate=None):
    M, K = a.shape
    S, _, Ns = w.shape
    TM = _tile(M, (1024, 512, 256) if K <= 1024 else (512, 256))
    TN = _tile(Ns, (1408, 1024, 768, 512, 256, 128))
    nj = Ns // TN
    fused = res is not None

    def body(*refs):
        if fused:
            a_ref, w_ref, r_ref, g_ref, f_ref, o_ref = refs
        else:
            a_ref, w_ref, f_ref = refs
        f = jnp.dot(a_ref[...], w_ref[...], preferred_element_type=F32)
        f_ref[...] = f.astype(f_ref.dtype)
        if fused:
            o_ref[...] = r_ref[...] + g_ref[...] * f

    col = lambda s, j, i: (i, s * nj + j)
    in_specs = [pl.BlockSpec((TM, K), lambda s, j, i: (i, 0)), pl.BlockSpec((None, K, TN), lambda s, j, i: (s, 0, j))]
    out_specs = [pl.BlockSpec((TM, TN), col)]
    out_shape = [jax.ShapeDtypeStruct((M, S * Ns), out_dtype)]
    args = [a, w]
    if fused:
        in_specs += [pl.BlockSpec((TM, TN), col), pl.BlockSpec((1, TN), lambda s, j, i: (0, s * nj + j))]
        out_specs.append(pl.BlockSpec((TM, TN), col))
        out_shape.append(jax.ShapeDtypeStruct((M, S * Ns), F32))
        args += [res, gate]
    out = pl.pallas_call(body, name=name, grid=(S, nj, M // TM), in_specs=in_specs, out_specs=out_specs,
                         out_shape=out_shape, compiler_params=_cp("parallel", "parallel", "parallel"))(*args)
    return tuple(out) if fused else out[0]


def mm_nt(g, w, out_dtype, name):
    g3 = g if g.ndim == 3 else g[None]
    Q, M, F = g3.shape
    S, K, Ns = w.shape
    TM = _tile(M, (1024, 512, 256) if K <= 1024 else (512, 256))
    TN = _tile(Ns, (1408, 1024, 768, 512, 256, 128))
    nj = Ns // TN
    nred = S * nj
    per_part = F // TN

    def body(g_ref, w_ref, o_ref, acc):
        n = pl.program_id(1)

        @pl.when(n == 0)
        def _():
            acc[...] = jnp.zeros_like(acc)

        acc[...] += lax.dot_general(g_ref[...], w_ref[...], (((1,), (1,)), ((), ())), preferred_element_type=F32)

        @pl.when(n == nred - 1)
        def _():
            o_ref[...] = acc[...].astype(o_ref.dtype)

    return pl.pallas_call(
        body, name=name, grid=(M // TM, nred),
        in_specs=[pl.BlockSpec((None, TM, TN), lambda i, n: (n // per_part, i, n % per_part)),
                  pl.BlockSpec((None, K, TN), lambda i, n: (n // nj, 0, n % nj))],
        out_specs=pl.BlockSpec((TM, K), lambda i, n: (i, 0)),
        out_shape=jax.ShapeDtypeStruct((M, K), out_dtype),
        scratch_shapes=[pltpu.VMEM((TM, K), F32)],
        compiler_params=_cp("parallel", "arbitrary"))(g3, w)


def mm_tn(a, g, S, name):
    M, K = a.shape
    g3 = g if g.ndim == 3 else g[None]
    Q, _, F = g3.shape
    Ns = Q * F // S
    TK = _tile(K, (256, 128))
    TN = _tile(Ns, (1408, 1024, 768, 512, 256, 128))
    nj = Ns // TN
    per_part = F // TN

    def body(a_ref, g_ref, o_ref):
        o_ref[...] = lax.dot_general(a_ref[...], g_ref[...], (((0,), (0,)), ((), ())),
                                     preferred_element_type=F32).astype(o_ref.dtype)

    return pl.pallas_call(
        body, name=name, grid=(S * nj, K // TK),
        in_specs=[pl.BlockSpec((M, TK), lambda n, k: (0, k)),
                  pl.BlockSpec((None, M, TN), lambda n, k: (n // per_part, 0, n % per_part))],
        out_specs=pl.BlockSpec((None, TK, TN), lambda n, k: (n // nj, k, n % nj)),
        out_shape=jax.ShapeDtypeStruct((S, K, Ns), BF),
        compiler_params=_cp("parallel", "parallel"))(a, g3)


def _rows(TL, D):
    return pl.BlockSpec((TL, D), lambda i: (i, 0))


def _fixed(R, D):
    return pl.BlockSpec((R, D), lambda i: (0, 0))


def _rowsum8(v):
    T, D = v.shape
    return jnp.sum(v.reshape(T // 8, 8, D), axis=0)


def _norm_parts(xv):
    r = lax.rsqrt(jnp.mean(xv * xv, axis=-1, keepdims=True) + RMS_EPS)
    return xv * r, r


def norm_mod(x, gamma, mods, k_shift, out_dtype, name):
    L, D = x.shape
    TL = _tile(L, (512, 256))

    def body(x_ref, g_ref, m_ref, o_ref):
        xn, _ = _norm_parts(x_ref[...])
        sh, sc = m_ref[k_shift:k_shift + 1, :], m_ref[k_shift + 1:k_shift + 2, :]
        o_ref[...] = ((xn * g_ref[...]) * (1.0 + sc) + sh).astype(o_ref.dtype)

    return pl.pallas_call(body, name=name, grid=(L // TL,),
                          in_specs=[_rows(TL, D), _fixed(1, D), _fixed(6, D)], out_specs=_rows(TL, D),
                          out_shape=jax.ShapeDtypeStruct((L, D), out_dtype), compiler_params=_cp("parallel"))(x, gamma, mods)


def norm_bwd(dh, x, dres, gamma, mods, k_shift, name, branch=None):
    L, D = x.shape
    TL = _tile(L, (512, 256))
    nacc = 4 if branch else 3

    def body(*refs):
        if branch:
            dh_ref, x_ref, dr_ref, g_ref, m_ref, f_ref, fm_ref, dx_ref, s_ref, df_ref, acc = refs
        else:
            dh_ref, x_ref, dr_ref, g_ref, m_ref, dx_ref, s_ref, acc = refs
        i = pl.program_id(0)

        @pl.when(i == 0)
        def _():
            acc[...] = jnp.zeros_like(acc)

        xn, r = _norm_parts(x_ref[...])
        dh_v = dh_ref[...].astype(F32)
        gam = g_ref[...]
        sc = m_ref[k_shift + 1:k_shift + 2, :]
        dn = dh_v * (1.0 + sc)
        dxn = dn * gam
        dx = dr_ref[...] + r * (dxn - xn * jnp.mean(dxn * xn, axis=-1, keepdims=True))
        dx_ref[...] = dx
        acc[0] += _rowsum8(dh_v)
        acc[1] += _rowsum8(dh_v * (xn * gam))
        acc[2] += _rowsum8(dn * xn)
        if branch:
            df_ref[...] = (dx * fm_ref[branch[2]:branch[2] + 1, :]).astype(df_ref.dtype)
            acc[3] += _rowsum8(dx * f_ref[...].astype(F32))

        @pl.when(i == pl.num_programs(0) - 1)
        def _():
            s_ref[...] = jnp.zeros_like(s_ref)
            for q in range(nacc):
                s_ref[q:q + 1, :] = jnp.sum(acc[q], axis=0, keepdims=True)

    in_specs = [_rows(TL, D), _rows(TL, D), _rows(TL, D), _fixed(1, D), _fixed(6, D)]
    out_specs = [_rows(TL, D), _fixed(8, D)]
    out_shape = [jax.ShapeDtypeStruct((L, D), F32), jax.ShapeDtypeStruct((8, D), F32)]
    args = [dh, x, dres, gamma, mods]
    if branch:
        in_specs += [_rows(TL, D), _fixed(6, D)]
        out_specs.append(_rows(TL, D))
        out_shape.append(jax.ShapeDtypeStruct((L, D), BF))
        args += [branch[0], branch[1]]
    return pl.pallas_call(
        body, name=name, grid=(L // TL,), in_specs=in_specs, out_specs=out_specs, out_shape=out_shape,
        scratch_shapes=[pltpu.VMEM((nacc, 8, D), F32)], compiler_params=_cp("arbitrary"))(*args)


def gate_bwd(dx, f, mods, k_gate, name):
    L, D = dx.shape
    TL = _tile(L, (512, 256))

    def body(dx_ref, f_ref, m_ref, o_ref, s_ref, acc):
        i = pl.program_id(0)

        @pl.when(i == 0)
        def _():
            acc[...] = jnp.zeros_like(acc)

        dxv = dx_ref[...]
        o_ref[...] = (dxv * m_ref[k_gate:k_gate + 1, :]).astype(o_ref.dtype)
        acc[...] += _rowsum8(dxv * f_ref[...].astype(F32))

        @pl.when(i == pl.num_programs(0) - 1)
        def _():
            s_ref[...] = jnp.zeros_like(s_ref)
            s_ref[0:1, :] = jnp.sum(acc[...], axis=0, keepdims=True)

    return pl.pallas_call(
        body, name=name, grid=(L // TL,), in_specs=[_rows(TL, D), _rows(TL, D), _fixed(6, D)],
        out_specs=[_rows(TL, D), _fixed(8, D)],
        out_shape=[jax.ShapeDtypeStruct((L, D), BF), jax.ShapeDtypeStruct((8, D), F32)],
        scratch_shapes=[pltpu.VMEM((8, D), F32)], compiler_params=_cp("arbitrary"))(dx, f, mods)


def ffn_in_act(a, w, name):
    M, K = a.shape
    S, _, Ns = w.shape
    half = S // 2
    TM = _tile(M, (512, 256))
    TN = _tile(Ns, (1408, 1024, 768, 512, 256, 128))
    nj = Ns // TN

    def body(a_ref, wg_ref, wu_ref, gu_ref, act_ref):
        av = a_ref[...]
        g = jnp.dot(av, wg_ref[...], preferred_element_type=F32)
        u = jnp.dot(av, wu_ref[...], preferred_element_type=F32)
        gu_ref[0] = g.astype(gu_ref.dtype)
        gu_ref[1] = u.astype(gu_ref.dtype)
        act_ref[...] = (g * jax.nn.sigmoid(g) * u).astype(act_ref.dtype)

    return pl.pallas_call(
        body, name=name, grid=(half, nj, M // TM),
        in_specs=[pl.BlockSpec((TM, K), lambda s, j, i: (i, 0)),
                  pl.BlockSpec((None, K, TN), lambda s, j, i: (s, 0, j)),
                  pl.BlockSpec((None, K, TN), lambda s, j, i: (s + half, 0, j))],
        out_specs=[pl.BlockSpec((2, TM, TN), lambda s, j, i: (0, i, s * nj + j)),
                   pl.BlockSpec((TM, TN), lambda s, j, i: (i, s * nj + j))],
        out_shape=[jax.ShapeDtypeStruct((2, M, half * Ns), BF), jax.ShapeDtypeStruct((M, half * Ns), BF)],
        compiler_params=_cp("parallel", "parallel", "parallel"))(a, w, w)


def ffn_out_bwd(dff, w2, gu, name):
    M, D = dff.shape
    F = w2.shape[0]
    TM = _tile(M, (512, 256))
    CW = _tile(F, (256, 128))

    def body(d_ref, w_ref, gu_ref, o_ref):
        dv = d_ref[...]
        for c in range(0, F, CW):
            da = lax.dot_general(dv, w_ref[c:c + CW, :], (((1,), (1,)), ((), ())), preferred_element_type=F32)
            g = gu_ref[0, :, c:c + CW].astype(F32)
            u = gu_ref[1, :, c:c + CW].astype(F32)
            s = jax.nn.sigmoid(g)
            o_ref[0, :, c:c + CW] = (da * u * (s + g * s * (1.0 - s))).astype(o_ref.dtype)
            o_ref[1, :, c:c + CW] = (da * g * s).astype(o_ref.dtype)

    part = pl.BlockSpec((2, TM, F), lambda i: (0, i, 0))
    return pl.pallas_call(
        body, name=name, grid=(M // TM,),
        in_specs=[pl.BlockSpec((TM, D), lambda i: (i, 0)), pl.BlockSpec((F, D), lambda i: (0, 0)), part],
        out_specs=part, out_shape=jax.ShapeDtypeStruct((2, M, F), BF),
        compiler_params=_cp("parallel"))(dff, w2, gu)


def swiglu_act(gu, name):
    L, F2 = gu.shape
    F = F2 // 2
    TL = _tile(L, (256,))

    def body(gu_ref, o_ref):
        g = gu_ref[:, :F].astype(F32)
        u = gu_ref[:, F:].astype(F32)
        o_ref[...] = (g * jax.nn.sigmoid(g) * u).astype(o_ref.dtype)

    return pl.pallas_call(body, name=name, grid=(L // TL,), in_specs=[_rows(TL, F2)], out_specs=_rows(TL, F),
                          out_shape=jax.ShapeDtypeStruct((L, F), BF), compiler_params=_cp("parallel"))(gu)


def swiglu_bwd(da, gu, name):
    L, F2 = gu.shape
    F = F2 // 2
    TL = _tile(L, (256,))

    def body(da_ref, gu_ref, o_ref):
        g = gu_ref[:, :F].astype(F32)
        u = gu_ref[:, F:].astype(F32)
        d = da_ref[...].astype(F32)
        s = jax.nn.sigmoid(g)
        o_ref[:, :F] = (d * u * (s + g * s * (1.0 - s))).astype(o_ref.dtype)
        o_ref[:, F:] = (d * g * s).astype(o_ref.dtype)

    return pl.pallas_call(body, name=name, grid=(L // TL,), in_specs=[_rows(TL, F), _rows(TL, F2)],
                          out_specs=_rows(TL, F2), out_shape=jax.ShapeDtypeStruct((L, F2), BF),
                          compiler_params=_cp("parallel"))(da, gu)


def glu_res(o, x, mods, k_gate, name):
    L, D = x.shape
    TL = _tile(L, (512, 256))

    def body(o_ref, x_ref, m_ref, mix_ref, y_ref):
        mix = o_ref[:, :D].astype(F32) * jax.nn.sigmoid(o_ref[:, D:].astype(F32))
        mix_ref[...] = mix.astype(mix_ref.dtype)
        y_ref[...] = x_ref[...] + m_ref[k_gate:k_gate + 1, :] * mix

    return pl.pallas_call(
        body, name=name, grid=(L // TL,), in_specs=[_rows(TL, 2 * D), _rows(TL, D), _fixed(6, D)],
        out_specs=[_rows(TL, D), _rows(TL, D)],
        out_shape=[jax.ShapeDtypeStruct((L, D), BF), jax.ShapeDtypeStruct((L, D), F32)],
        compiler_params=_cp("parallel"))(o, x, mods)


def ssm_out_glu(z, w, x, mods, k_gate, name):
    M, K = z.shape
    S, _, Ns = w.shape
    half = S // 2
    TM = _tile(M, (1024, 512, 256))
    TN = _tile(Ns, (512, 256, 128))
    nj = Ns // TN

    def body(z_ref, wv_ref, wg_ref, x_ref, m_ref, o_ref, mix_ref, y_ref):
        zv = z_ref[...]
        val = jnp.dot(zv, wv_ref[...], preferred_element_type=F32)
        gate = jnp.dot(zv, wg_ref[...], preferred_element_type=F32)
        o_ref[0] = val.astype(o_ref.dtype)
        o_ref[1] = gate.astype(o_ref.dtype)
        mix = val * jax.nn.sigmoid(gate)
        mix_ref[...] = mix.astype(mix_ref.dtype)
        y_ref[...] = x_ref[...] + m_ref[k_gate:k_gate + 1, :] * mix

    col = lambda s, j, i: (i, s * nj + j)
    return pl.pallas_call(
        body, name=name, grid=(half, nj, M // TM),
        in_specs=[pl.BlockSpec((TM, K), lambda s, j, i: (i, 0)),
                  pl.BlockSpec((None, K, TN), lambda s, j, i: (s, 0, j)),
                  pl.BlockSpec((None, K, TN), lambda s, j, i: (s + half, 0, j)),
                  pl.BlockSpec((TM, TN), col), pl.BlockSpec((6, TN), lambda s, j, i: (0, s * nj + j))],
        out_specs=[pl.BlockSpec((2, TM, TN), lambda s, j, i: (0, i, s * nj + j)), pl.BlockSpec((TM, TN), col),
                   pl.BlockSpec((TM, TN), col)],
        out_shape=[jax.ShapeDtypeStruct((2, M, half * Ns), BF), jax.ShapeDtypeStruct((M, half * Ns), BF),
                   jax.ShapeDtypeStruct((M, half * Ns), F32)],
        compiler_params=_cp("parallel", "parallel", "parallel"))(z, w, w, x, mods)


def glu_bwd(dmix, o, name):
    _, L, D = o.shape
    TL = _tile(L, (512, 256))

    def body(d_ref, o_ref, do_ref):
        d = d_ref[...].astype(F32)
        val = o_ref[0].astype(F32)
        s = jax.nn.sigmoid(o_ref[1].astype(F32))
        do_ref[0] = (d * s).astype(do_ref.dtype)
        do_ref[1] = (d * val * s * (1.0 - s)).astype(do_ref.dtype)

    part = pl.BlockSpec((2, TL, D), lambda i: (0, i, 0))
    return pl.pallas_call(body, name=name, grid=(L // TL,), in_specs=[_rows(TL, D), part],
                          out_specs=part, out_shape=jax.ShapeDtypeStruct((2, L, D), BF),
                          compiler_params=_cp("parallel"))(dmix, o)


def final_loss(x, target, gamma, f, fmods, k_gate, name):
    L, D = x.shape
    TL = _tile(L, (512, 256))

    def body(x_ref, t_ref, g_ref, f_ref, fm_ref, l_ref, dx_ref, s_ref, df_ref, acc, lacc):
        i = pl.program_id(0)

        @pl.when(i == 0)
        def _():
            acc[...] = jnp.zeros_like(acc)
            lacc[...] = jnp.zeros_like(lacc)

        xn, r = _norm_parts(x_ref[...])
        gam = g_ref[...]
        e = xn * gam - t_ref[...]
        lacc[...] += jnp.sum(0.5 * jnp.mean(e * e, axis=-1, keepdims=True), axis=0, keepdims=True)
        dy = e * (1.0 / D)
        dxn = dy * gam
        dx = r * (dxn - xn * jnp.mean(dxn * xn, axis=-1, keepdims=True))
        dx_ref[...] = dx
        df_ref[...] = (dx * fm_ref[k_gate:k_gate + 1, :]).astype(df_ref.dtype)
        acc[0] += _rowsum8(dy * xn)
        acc[1] += _rowsum8(dx * f_ref[...].astype(F32))

        @pl.when(i == pl.num_programs(0) - 1)
        def _():
            s_ref[...] = jnp.zeros_like(s_ref)
            for q in range(2):
                s_ref[q:q + 1, :] = jnp.sum(acc[q], axis=0, keepdims=True)
            l_ref[...] = jnp.broadcast_to(lacc[...], l_ref.shape)

    return pl.pallas_call(
        body, name=name, grid=(L // TL,),
        in_specs=[_rows(TL, D), _rows(TL, D), _fixed(1, D), _rows(TL, D), _fixed(6, D)],
        out_specs=[_fixed(8, 128), _rows(TL, D), _fixed(8, D), _rows(TL, D)],
        out_shape=[jax.ShapeDtypeStruct((8, 128), F32), jax.ShapeDtypeStruct((L, D), F32),
                   jax.ShapeDtypeStruct((8, D), F32), jax.ShapeDtypeStruct((L, D), BF)],
        scratch_shapes=[pltpu.VMEM((2, 8, D), F32), pltpu.VMEM((1, 1), F32)],
        compiler_params=_cp("arbitrary"))(x, target, gamma, f, fmods)


def _col(L, TC, off):
    return pl.BlockSpec((L, TC), lambda j: (0, off + j))


def _shift_down(v, k, row):
    return jnp.where(row >= k, pltpu.roll(v, k, 0), 0.0)


def _shift_up(v, k, row, L):
    return jnp.where(row < L - k, pltpu.roll(v, L - k, 0), 0.0)


def conv_fwd(p, w, name):
    L, D3 = p.shape
    D = D3 // 3
    TC = _tile(D, (128,))
    nc = D // TC

    def body(b_ref, c_ref, v_ref, w_ref, o_ref):
        row = lax.broadcasted_iota(jnp.int32, (L, TC), 0)
        cv = c_ref[...].astype(F32) * v_ref[...].astype(F32)
        conv = w_ref[2:3, :] * cv + w_ref[1:2, :] * _shift_down(cv, 1, row) + w_ref[0:1, :] * _shift_down(cv, 2, row)
        o_ref[...] = (b_ref[...].astype(F32) * conv).astype(o_ref.dtype)

    return pl.pallas_call(
        body, name=name, grid=(nc,),
        in_specs=[_col(L, TC, 0), _col(L, TC, nc), _col(L, TC, 2 * nc), pl.BlockSpec((3, TC), lambda j: (0, j))],
        out_specs=_col(L, TC, 0), out_shape=jax.ShapeDtypeStruct((L, D), BF), compiler_params=_cp("parallel"))(p, p, p, w)


def conv_bwd(dm, p, w, name):
    L, D3 = p.shape
    D = D3 // 3
    TC = _tile(D, (128,))
    nc = D // TC

    def body(dm_ref, b_ref, c_ref, v_ref, w_ref, db_ref, dc_ref, dv_ref, dw_ref):
        row = lax.broadcasted_iota(jnp.int32, (L, TC), 0)
        cg, vv = c_ref[...].astype(F32), v_ref[...].astype(F32)
        cv = cg * vv
        cv1, cv2 = _shift_down(cv, 1, row), _shift_down(cv, 2, row)
        conv = w_ref[2:3, :] * cv + w_ref[1:2, :] * cv1 + w_ref[0:1, :] * cv2
        dmv = dm_ref[...].astype(F32)
        db_ref[...] = (dmv * conv).astype(db_ref.dtype)
        dconv = dmv * b_ref[...].astype(F32)
        dcv = (w_ref[2:3, :] * dconv + w_ref[1:2, :] * _shift_up(dconv, 1, row, L)
               + w_ref[0:1, :] * _shift_up(dconv, 2, row, L))
        dc_ref[...] = (dcv * vv).astype(dc_ref.dtype)
        dv_ref[...] = (dcv * cg).astype(dv_ref.dtype)
        dw_ref[...] = jnp.zeros_like(dw_ref)
        dw_ref[0:1, :] = jnp.sum(dconv * cv2, axis=0, keepdims=True)
        dw_ref[1:2, :] = jnp.sum(dconv * cv1, axis=0, keepdims=True)
        dw_ref[2:3, :] = jnp.sum(dconv * cv, axis=0, keepdims=True)

    one = jax.ShapeDtypeStruct((L, D), BF)
    return pl.pallas_call(
        body, name=name, grid=(nc,),
        in_specs=[_col(L, TC, 0), _col(L, TC, 0), _col(L, TC, nc), _col(L, TC, 2 * nc),
                  pl.BlockSpec((3, TC), lambda j: (0, j))],
        out_specs=[_col(L, TC, 0), _col(L, TC, 0), _col(L, TC, 0), pl.BlockSpec((8, TC), lambda j: (0, j))],
        out_shape=[one, one, one, jax.ShapeDtypeStruct((8, D), F32)],
        compiler_params=_cp("parallel"))(dm, p, p, p, w)


def _gelu(y):
    return 0.5 * y * (1.0 + jnp.tanh(GELU_C * (y + GELU_A * y * y * y)))


def _gelu_grad(y):
    th = jnp.tanh(GELU_C * (y + GELU_A * y * y * y))
    return 0.5 * (1.0 + th) + 0.5 * y * (1.0 - th * th) * GELU_C * (1.0 + 3.0 * GELU_A * y * y)


def _cmul_add(br, bi, ar, ai, sr, si):
    return br + ar * sr - ai * si, bi + ar * si + ai * sr


def _log2(n):
    k = n.bit_length() - 1
    assert 1 << k == n
    return k


def _replicate(P2, W2, P, GLP, transposed):
    shape = (W2, P2) if transposed else (P2, W2)
    k = lax.broadcasted_iota(jnp.int32, shape, 1 if transposed else 0)
    c = lax.broadcasted_iota(jnp.int32, shape, 0 if transposed else 1)
    return ((k >> _log2(P)) == (c >> _log2(GLP))) & ((k & (P - 1)) == (c & (P - 1)))


def _on_diagonal(KB, W2, H, P, GLP, transposed):
    shape = (W2, KB) if transposed else (KB, W2)
    r = lax.broadcasted_iota(jnp.int32, shape, 1 if transposed else 0)
    c = lax.broadcasted_iota(jnp.int32, shape, 0 if transposed else 1)
    return (r >> _log2(H)) == ((c & (GLP - 1)) >> _log2(P))


def _expand(t, dims, transposed):
    KB, W2, H, P, GLP = dims
    rep = _replicate(2 * P, W2, P, GLP, transposed).astype(t.dtype)
    wide = jnp.dot(rep, t, preferred_element_type=F32) if transposed else jnp.dot(t, rep, preferred_element_type=F32)
    return jnp.where(_on_diagonal(KB, W2, H, P, GLP, transposed), wide, 0.0).astype(t.dtype)


def _extract(acc, dims):
    KB, W2, H, P, GLP = dims
    rep = _replicate(2 * P, W2, P, GLP, True).astype(F32)
    kept = jnp.where(_on_diagonal(KB, W2, H, P, GLP, False), acc, 0.0)
    return jnp.dot(kept, rep, preferred_element_type=F32, precision=lax.Precision.HIGHEST)


def _cmul(ar, ai, sr, si):
    return ar * sr - ai * si, ar * si + ai * sr


LANES = 128


def _cols(ref, base, n, rows):
    return jnp.concatenate([ref[base + q, rows, :] for q in range(n)], axis=1)


def _set_cols(ref, base, n, rows, val):
    for q in range(n):
        ref[base + q, rows, :] = val[:, q * LANES:(q + 1) * LANES]


def _strided_s5_fwd(h, tb, tct, pw, dvec, name):
    L, D = h.shape
    nkb, KB, P2 = tb.shape
    P = P2 // 2
    W = (KB // SSM_GROUP) * P
    W2 = 2 * W
    dims = (KB, W2, SSM_GROUP, P, W)
    TL = _tile(L, (512, 256))
    CH = TL // 8
    NC = W // LANES

    def body(h_ref, tb_ref, tct_ref, pw_ref, d_ref, s_ref, y_ref, z_ref, bw, cw, carry):
        t = pl.program_id(1)

        @pl.when(t == 0)
        def _():
            carry[...] = jnp.zeros_like(carry)
            bw[...] = _expand(tb_ref[...], dims, False)
            cw[...] = _expand(tct_ref[...], dims, True)

        hv = h_ref[...]
        _set_cols(s_ref, 0, 2 * NC, slice(None), jnp.dot(hv.astype(BF), bw[...], preferred_element_type=F32))
        ar, ai = pw_ref[0:8, :W], pw_ref[0:8, W:]
        xr = xi = jnp.zeros((8, W), F32)
        for j in range(CH):
            rows = pl.ds(j, 8, stride=CH)
            xr, xi = _cmul_add(_cols(s_ref, 0, NC, rows), _cols(s_ref, NC, NC, rows), ar, ai, xr, xi)
            _set_cols(s_ref, 0, NC, rows, xr)
            _set_cols(s_ref, NC, NC, rows, xi)
        for k, off in ((1, 8), (2, 16), (4, 24)):
            xr, xi = _cmul_add(xr, xi, pw_ref[off:off + 8, :W], pw_ref[off:off + 8, W:],
                               pltpu.roll(xr, k, 0), pltpu.roll(xi, k, 0))
        xr, xi = _cmul_add(xr, xi, pw_ref[32:40, :W], pw_ref[32:40, W:], carry[0], carry[1])
        first = lax.broadcasted_iota(jnp.int32, (8, W), 0) == 0
        cr = jnp.where(first, carry[0], pltpu.roll(xr, 1, 0))
        ci = jnp.where(first, carry[1], pltpu.roll(xi, 1, 0))
        carry[0] = jnp.broadcast_to(xr[7:8], (8, W))
        carry[1] = jnp.broadcast_to(xi[7:8], (8, W))
        for j in range(CH):
            rows = pl.ds(j, 8, stride=CH)
            cr, ci = _cmul(ar, ai, cr, ci)
            _set_cols(s_ref, 0, NC, rows, _cols(s_ref, 0, NC, rows) + cr)
            _set_cols(s_ref, NC, NC, rows, _cols(s_ref, NC, NC, rows) + ci)
        sv = _cols(s_ref, 0, 2 * NC, slice(None))
        y = jnp.dot(sv.astype(BF), cw[...], preferred_element_type=F32) + d_ref[...] * hv
        y_ref[...] = y
        z_ref[...] = _gelu(y).astype(z_ref.dtype)

    blk = lambda kb, t: (t, kb)
    per_kb = lambda kb, t: (kb, 0, 0)
    return pl.pallas_call(
        body, name=name, grid=(nkb, L // TL),
        in_specs=[pl.BlockSpec((TL, KB), blk), pl.BlockSpec((None, KB, P2), per_kb),
                  pl.BlockSpec((None, P2, KB), per_kb), pl.BlockSpec((None, 40, W2), per_kb),
                  pl.BlockSpec((1, KB), lambda kb, t: (0, kb))],
        out_specs=[pl.BlockSpec((2 * NC, TL, LANES), lambda kb, t: (kb, t, 0)), pl.BlockSpec((TL, KB), blk),
                   pl.BlockSpec((TL, KB), blk)],
        out_shape=[jax.ShapeDtypeStruct((nkb * 2 * NC, L, LANES), F32), jax.ShapeDtypeStruct((L, D), F32),
                   jax.ShapeDtypeStruct((L, D), BF)],
        scratch_shapes=[pltpu.VMEM((KB, W2), BF), pltpu.VMEM((W2, KB), BF), pltpu.VMEM((2, 8, W), F32)],
        compiler_params=_cp("parallel", "arbitrary"))(h, tb, tct, pw, dvec)


def _strided_s5_bwd(dz, y, h, s, tc, tbt, pwr, dvec, name):
    L, D = h.shape
    nkb, KB, P2 = tc.shape
    P = P2 // 2
    W = (KB // SSM_GROUP) * P
    W2 = 2 * W
    dims = (KB, W2, SSM_GROUP, P, W)
    TL = _tile(L, (512, 256))
    CH = TL // 8
    NC = W // LANES
    nt = L // TL

    def body(dz_ref, y_ref, h_ref, s_ref, sp_ref, tc_ref, tbt_ref, pw_ref, d_ref,
             dh_ref, dd_ref, da_ref, db_ref, dc_ref, g, ctw, btw, dbacc, dcacc, carry):
        t = pl.program_id(1)

        @pl.when(t == 0)
        def _():
            carry[...] = jnp.zeros_like(carry)
            dd_ref[...] = jnp.zeros_like(dd_ref)
            da_ref[...] = jnp.zeros_like(da_ref)
            dbacc[...] = jnp.zeros_like(dbacc)
            dcacc[...] = jnp.zeros_like(dcacc)
            ctw[...] = _expand(tc_ref[...], dims, False)
            btw[...] = _expand(tbt_ref[...], dims, True)

        hv = h_ref[...]
        dy = dz_ref[...].astype(F32) * _gelu_grad(y_ref[...])
        dd_ref[...] += _rowsum8(dy * hv)
        dyb = dy.astype(BF)
        _set_cols(g, 0, 2 * NC, slice(None), jnp.dot(dyb, ctw[...], preferred_element_type=F32))
        ar, ai = pw_ref[0:8, :W], pw_ref[0:8, W:]
        gr = gi = jnp.zeros((8, W), F32)
        for j in reversed(range(CH)):
            rows = pl.ds(j, 8, stride=CH)
            gr, gi = _cmul_add(_cols(g, 0, NC, rows), _cols(g, NC, NC, rows), ar, ai, gr, gi)
            _set_cols(g, 0, NC, rows, gr)
            _set_cols(g, NC, NC, rows, gi)
        for k, off in ((1, 8), (2, 16), (4, 24)):
            gr, gi = _cmul_add(gr, gi, pw_ref[off:off + 8, :W], pw_ref[off:off + 8, W:],
                               pltpu.roll(gr, 8 - k, 0), pltpu.roll(gi, 8 - k, 0))
        gr, gi = _cmul_add(gr, gi, pw_ref[32:40, :W], pw_ref[32:40, W:], carry[0], carry[1])
        sub = lax.broadcasted_iota(jnp.int32, (8, W), 0)
        cr = jnp.where(sub == 7, carry[0], pltpu.roll(gr, 7, 0))
        ci = jnp.where(sub == 7, carry[1], pltpu.roll(gi, 7, 0))
        carry[0] = jnp.broadcast_to(gr[0:1], (8, W))
        carry[1] = jnp.broadcast_to(gi[0:1], (8, W))
        live = jnp.where(t == nt - 1, 0.0, 1.0)
        accr = acci = jnp.zeros((8, W), F32)
        for j in reversed(range(CH)):
            rows = pl.ds(j, 8, stride=CH)
            cr, ci = _cmul(ar, ai, cr, ci)
            gr, gi = _cols(g, 0, NC, rows) + cr, _cols(g, NC, NC, rows) + ci
            _set_cols(g, 0, NC, rows, gr)
            _set_cols(g, NC, NC, rows, gi)
            if j > 0:
                before = pl.ds(j - 1, 8, stride=CH)
                pr, pi = _cols(s_ref, 0, NC, before), _cols(s_ref, NC, NC, before)
            else:
                last = pl.ds(CH - 1, 8, stride=CH)
                pr = jnp.where(sub == 0, _cols(sp_ref, 0, NC, slice(7, 8)) * live,
                               pltpu.roll(_cols(s_ref, 0, NC, last), 1, 0))
                pi = jnp.where(sub == 0, _cols(sp_ref, NC, NC, slice(7, 8)) * live,
                               pltpu.roll(_cols(s_ref, NC, NC, last), 1, 0))
            accr = accr + pr * gr + pi * gi
            acci = acci + pr * gi - pi * gr
        da_ref[:, :W] += accr
        da_ref[:, W:] += acci

        gb = _cols(g, 0, 2 * NC, slice(None)).astype(BF)
        dh_ref[...] = dy * d_ref[...] + jnp.dot(gb, btw[...], preferred_element_type=F32)
        tn = (((0,), (0,)), ((), ()))
        dbacc[...] += lax.dot_general(hv.astype(BF), gb, tn, preferred_element_type=F32)
        dcacc[...] += lax.dot_general(dyb, _cols(s_ref, 0, 2 * NC, slice(None)).astype(BF), tn,
                                      preferred_element_type=F32)

        @pl.when(t == nt - 1)
        def _():
            db_ref[...] = _extract(dbacc[...], dims)
            dc_ref[...] = _extract(dcacc[...], dims)

    rev = lambda kb, t: (nt - 1 - t, kb)
    per_kb = lambda kb, t: (kb, 0, 0)
    return pl.pallas_call(
        body, name=name, grid=(nkb, nt),
        in_specs=[pl.BlockSpec((TL, KB), rev), pl.BlockSpec((TL, KB), rev), pl.BlockSpec((TL, KB), rev),
                  pl.BlockSpec((2 * NC, TL, LANES), lambda kb, t: (kb, nt - 1 - t, 0)),
                  pl.BlockSpec((2 * NC, 8, LANES), lambda kb, t: (kb, jnp.maximum((nt - 1 - t) * CH - 1, 0), 0)),
                  pl.BlockSpec((None, KB, P2), per_kb), pl.BlockSpec((None, P2, KB), per_kb),
                  pl.BlockSpec((None, 40, W2), per_kb), pl.BlockSpec((1, KB), lambda kb, t: (0, kb))],
        out_specs=[pl.BlockSpec((TL, KB), rev), pl.BlockSpec((8, KB), lambda kb, t: (0, kb)),
                   pl.BlockSpec((None, 8, W2), per_kb), pl.BlockSpec((None, KB, P2), per_kb),
                   pl.BlockSpec((None, KB, P2), per_kb)],
        out_shape=[jax.ShapeDtypeStruct((L, D), F32), jax.ShapeDtypeStruct((8, D), F32),
                   jax.ShapeDtypeStruct((nkb, 8, W2), F32), jax.ShapeDtypeStruct((nkb, KB, P2), F32),
                   jax.ShapeDtypeStruct((nkb, KB, P2), F32)],
        scratch_shapes=[pltpu.VMEM((2 * NC, TL, LANES), F32), pltpu.VMEM((KB, W2), BF), pltpu.VMEM((W2, KB), BF),
                        pltpu.VMEM((KB, W2), F32), pltpu.VMEM((KB, W2), F32), pltpu.VMEM((2, 8, W), F32)],
        compiler_params=_cp("parallel", "arbitrary"))(dz, y, h, s, s, tc, tbt, pwr, dvec)


def _chunk_order(TL, CH, transposed):
    out_row = lax.broadcasted_iota(jnp.int32, (TL, TL), 1 if transposed else 0)
    in_row = lax.broadcasted_iota(jnp.int32, (TL, TL), 0 if transposed else 1)
    return in_row == ((out_row & 7) << _log2(CH)) + (out_row >> 3)


def _reorder(perm, v):
    hi = v.astype(perm.dtype)
    lo = (v - hi.astype(F32)).astype(perm.dtype)
    return jnp.dot(perm, hi, preferred_element_type=F32) + jnp.dot(perm, lo, preferred_element_type=F32)


def _interleave(main, side):
    n, m, k = len(main), len(side), 0
    for i, step in enumerate(main):
        step()
        while k < m and (k + 1) * n <= (i + 1) * m:
            side[k]()
            k += 1
    for step in side[k:]:
        step()


S5_CHUNK = 512


def s5_fwd(h, tb, tct, pw, dvec, name):
    L, D = h.shape
    nkb, KB, P2 = tb.shape
    P = P2 // 2
    W = (KB // SSM_GROUP) * P
    W2 = 2 * W
    dims = (KB, W2, SSM_GROUP, P, W)
    TL = _tile(L, (512, 256))
    CH = TL // 8
    NB = 2 if nkb % 2 == 0 else 1
    CK = min(S5_CHUNK, W2)

    def body(h_ref, tb_ref, tct_ref, pw_ref, d_ref, s_ref, y_ref, z_ref, bw, cw, perm, unperm, carry):
        t = pl.program_id(1)

        @pl.when(t == 0)
        def _():
            carry[...] = jnp.zeros_like(carry)
            for b in range(NB):
                bw[b] = _expand(tb_ref[b], dims, False)
                cw[b] = _expand(tct_ref[b], dims, True)
            perm[...] = _chunk_order(TL, CH, False).astype(perm.dtype)
            unperm[...] = _chunk_order(TL, CH, True).astype(perm.dtype)

        hp = _reorder(perm[...], h_ref[...])
        hpb = hp.astype(BF)
        first = lax.broadcasted_iota(jnp.int32, (8, W), 0) == 0

        def project(b):
            def chunk(c):
                def emit():
                    s_ref[:, b * W2 + c:b * W2 + c + CK] = jnp.dot(hpb[:, b * KB:(b + 1) * KB], bw[b, :, c:c + CK],
                                                                   preferred_element_type=F32)
                return emit
            return [chunk(c) for c in range(0, W2, CK)]

        def scan(b):
            re, im = slice(b * W2, b * W2 + W), slice(b * W2 + W, (b + 1) * W2)
            ar, ai = pw_ref[b, 0:8, :W], pw_ref[b, 0:8, W:]
            st = {"x": (jnp.zeros((8, W), F32), jnp.zeros((8, W), F32))}

            def own(j):
                def emit():
                    rows = slice(j * 8, j * 8 + 8)
                    xr, xi = _cmul_add(s_ref[rows, re], s_ref[rows, im], ar, ai, *st["x"])
                    s_ref[rows, re] = xr
                    s_ref[rows, im] = xi
                    st["x"] = (xr, xi)
                return emit

            def ends():
                xr, xi = st["x"]
                for k, off in ((1, 8), (2, 16), (4, 24)):
                    xr, xi = _cmul_add(xr, xi, pw_ref[b, off:off + 8, :W], pw_ref[b, off:off + 8, W:],
                                       pltpu.roll(xr, k, 0), pltpu.roll(xi, k, 0))
                xr, xi = _cmul_add(xr, xi, pw_ref[b, 32:40, :W], pw_ref[b, 32:40, W:], carry[b, 0], carry[b, 1])
                st["c"] = (jnp.where(first, carry[b, 0], pltpu.roll(xr, 1, 0)),
                           jnp.where(first, carry[b, 1], pltpu.roll(xi, 1, 0)))
                carry[b, 0] = jnp.broadcast_to(xr[7:8], (8, W))
                carry[b, 1] = jnp.broadcast_to(xi[7:8], (8, W))

            def carried(j):
                def emit():
                    rows = slice(j * 8, j * 8 + 8)
                    cr, ci = _cmul(ar, ai, *st["c"])
                    s_ref[rows, re] = s_ref[rows, re] + cr
                    s_ref[rows, im] = s_ref[rows, im] + ci
                    st["c"] = (cr, ci)
                return emit

            return [own(j) for j in range(CH)] + [ends] + [carried(j) for j in range(CH)]

        def readout(b):
            cols = slice(b * KB, (b + 1) * KB)
            acc = {}

            def chunk(c):
                def emit():
                    part = jnp.dot(s_ref[:, b * W2 + c:b * W2 + c + CK].astype(BF), cw[b, c:c + CK, :],
                                   preferred_element_type=F32)
                    acc["y"] = part if c == 0 else acc["y"] + part
                return emit

            def finish():
                y = acc["y"] + d_ref[:, cols] * hp[:, cols]
                y_ref[:, cols] = y
                z_ref[:, cols] = jnp.dot(unperm[...], _gelu(y).astype(BF),
                                         preferred_element_type=F32).astype(z_ref.dtype)

            return [chunk(c) for c in range(0, W2, CK)] + [finish]

        for emit in project(0):
            emit()
        for b in range(NB):
            side = (project(b + 1) if b + 1 < NB else []) + (readout(b - 1) if b > 0 else [])
            _interleave(scan(b), side)
        for emit in readout(NB - 1):
            emit()

    blk = lambda kb, t: (t, kb)
    per_kb = lambda kb, t: (kb, 0, 0)
    return pl.pallas_call(
        body, name=name, grid=(nkb // NB, L // TL),
        in_specs=[pl.BlockSpec((TL, NB * KB), blk), pl.BlockSpec((NB, KB, P2), per_kb),
                  pl.BlockSpec((NB, P2, KB), per_kb), pl.BlockSpec((NB, 40, W2), per_kb),
                  pl.BlockSpec((1, NB * KB), lambda kb, t: (0, kb))],
        out_specs=[pl.BlockSpec((TL, NB * W2), blk), pl.BlockSpec((TL, NB * KB), blk),
                   pl.BlockSpec((TL, NB * KB), blk)],
        out_shape=[jax.ShapeDtypeStruct((L, nkb * W2), F32), jax.ShapeDtypeStruct((L, D), F32),
                   jax.ShapeDtypeStruct((L, D), BF)],
        scratch_shapes=[pltpu.VMEM((NB, KB, W2), BF), pltpu.VMEM((NB, W2, KB), BF), pltpu.VMEM((TL, TL), BF),
                        pltpu.VMEM((TL, TL), BF), pltpu.VMEM((NB, 2, 8, W), F32)],
        compiler_params=_cp("parallel", "arbitrary"))(h, tb, tct, pw, dvec)


def _s5_fwd_one_block(h, tb, tct, pw, dvec, name):
    L, D = h.shape
    nkb, KB, P2 = tb.shape
    P = P2 // 2
    W = (KB // SSM_GROUP) * P
    W2 = 2 * W
    dims = (KB, W2, SSM_GROUP, P, W)
    TL = _tile(L, (512, 256))
    CH = TL // 8

    def body(h_ref, tb_ref, tct_ref, pw_ref, d_ref, s_ref, y_ref, z_ref, bw, cw, perm, unperm, carry):
        t = pl.program_id(1)

        @pl.when(t == 0)
        def _():
            carry[...] = jnp.zeros_like(carry)
            bw[...] = _expand(tb_ref[...], dims, False)
            cw[...] = _expand(tct_ref[...], dims, True)
            perm[...] = _chunk_order(TL, CH, False).astype(perm.dtype)
            unperm[...] = _chunk_order(TL, CH, True).astype(perm.dtype)

        hp = _reorder(perm[...], h_ref[...])
        s_ref[...] = jnp.dot(hp.astype(BF), bw[...], preferred_element_type=F32)
        ar, ai = pw_ref[0:8, :W], pw_ref[0:8, W:]

        def own(j, x):
            rows = pl.ds(pl.multiple_of(j * 8, 8), 8)
            xr, xi = _cmul_add(s_ref[rows, :W], s_ref[rows, W:], ar, ai, x[0], x[1])
            s_ref[rows, :W] = xr
            s_ref[rows, W:] = xi
            return xr, xi

        zero = jnp.zeros((8, W), F32)
        xr, xi = lax.fori_loop(0, CH, own, (zero, zero))
        for k, off in ((1, 8), (2, 16), (4, 24)):
            xr, xi = _cmul_add(xr, xi, pw_ref[off:off + 8, :W], pw_ref[off:off + 8, W:],
                               pltpu.roll(xr, k, 0), pltpu.roll(xi, k, 0))
        xr, xi = _cmul_add(xr, xi, pw_ref[32:40, :W], pw_ref[32:40, W:], carry[0], carry[1])
        first = lax.broadcasted_iota(jnp.int32, (8, W), 0) == 0
        cr = jnp.where(first, carry[0], pltpu.roll(xr, 1, 0))
        ci = jnp.where(first, carry[1], pltpu.roll(xi, 1, 0))
        carry[0] = jnp.broadcast_to(xr[7:8], (8, W))
        carry[1] = jnp.broadcast_to(xi[7:8], (8, W))

        def carried(j, c):
            rows = pl.ds(pl.multiple_of(j * 8, 8), 8)
            cr, ci = _cmul(ar, ai, c[0], c[1])
            s_ref[rows, :W] = s_ref[rows, :W] + cr
            s_ref[rows, W:] = s_ref[rows, W:] + ci
            return cr, ci

        lax.fori_loop(0, CH, carried, (cr, ci))
        y = jnp.dot(s_ref[...].astype(BF), cw[...], preferred_element_type=F32) + d_ref[...] * hp
        y_ref[...] = y
        z_ref[...] = jnp.dot(unperm[...], _gelu(y).astype(BF), preferred_element_type=F32).astype(z_ref.dtype)

    blk = lambda kb, t: (t, kb)
    per_kb = lambda kb, t: (kb, 0, 0)
    return pl.pallas_call(
        body, name=name, grid=(nkb, L // TL),
        in_specs=[pl.BlockSpec((TL, KB), blk), pl.BlockSpec((None, KB, P2), per_kb),
                  pl.BlockSpec((None, P2, KB), per_kb), pl.BlockSpec((None, 40, W2), per_kb),
                  pl.BlockSpec((1, KB), lambda kb, t: (0, kb))],
        out_specs=[pl.BlockSpec((TL, W2), blk), pl.BlockSpec((TL, KB), blk), pl.BlockSpec((TL, KB), blk)],
        out_shape=[jax.ShapeDtypeStruct((L, nkb * W2), F32), jax.ShapeDtypeStruct((L, D), F32),
                   jax.ShapeDtypeStruct((L, D), BF)],
        scratch_shapes=[pltpu.VMEM((KB, W2), BF), pltpu.VMEM((W2, KB), BF), pltpu.VMEM((TL, TL), BF),
                        pltpu.VMEM((TL, TL), BF), pltpu.VMEM((2, 8, W), F32)],
        compiler_params=_cp("parallel", "arbitrary"))(h, tb, tct, pw, dvec)


def s5_bwd(dz, y, h, s, tc, tbt, pwr, dvec, name):
    L, D = h.shape
    nkb, KB, P2 = tc.shape
    P = P2 // 2
    W = (KB // SSM_GROUP) * P
    W2 = 2 * W
    dims = (KB, W2, SSM_GROUP, P, W)
    TL = _tile(L, (512, 256))
    CH = TL // 8
    nt = L // TL
    NB = 2 if nkb % 2 == 0 else 1
    CK = min(S5_CHUNK, W2)
    tn = (((0,), (0,)), ((), ()))

    def body(dz_ref, y_ref, h_ref, s_ref, sp_ref, tc_ref, tbt_ref, pw_ref, d_ref,
             dh_ref, dd_ref, da_ref, db_ref, dc_ref, g, ctw, btw, dbacc, dcacc, dys, perm, unperm, carry):
        t = pl.program_id(1)

        @pl.when(t == 0)
        def _():
            carry[...] = jnp.zeros_like(carry)
            dd_ref[...] = jnp.zeros_like(dd_ref)
            da_ref[...] = jnp.zeros_like(da_ref)
            dbacc[...] = jnp.zeros_like(dbacc)
            dcacc[...] = jnp.zeros_like(dcacc)
            for b in range(NB):
                ctw[b] = _expand(tc_ref[b], dims, False)
                btw[b] = _expand(tbt_ref[b], dims, True)
            perm[...] = _chunk_order(TL, CH, False).astype(perm.dtype)
            unperm[...] = _chunk_order(TL, CH, True).astype(perm.dtype)

        hp = jnp.dot(perm[...], h_ref[...].astype(BF), preferred_element_type=F32)
        dy = jnp.dot(perm[...], dz_ref[...].astype(BF), preferred_element_type=F32) * _gelu_grad(y_ref[...])
        dd_ref[...] += _rowsum8(dy * hp)
        dys[...] = dy
        dyb = dy.astype(BF)
        hpb = hp.astype(BF)
        sub = lax.broadcasted_iota(jnp.int32, (8, W), 0)
        live = jnp.where(t == nt - 1, 0.0, 1.0)

        def lead(b):
            cols = slice(b * KB, (b + 1) * KB)

            def to_states(c):
                def emit():
                    g[b, :, c:c + CK] = jnp.dot(dyb[:, cols], ctw[b, :, c:c + CK], preferred_element_type=F32)
                return emit

            def d_c(c):
                def emit():
                    dcacc[b, :, c:c + CK] += lax.dot_general(dyb[:, cols],
                                                             s_ref[:, b * W2 + c:b * W2 + c + CK].astype(BF), tn,
                                                             preferred_element_type=F32)
                return emit

            return [f(c) for c in range(0, W2, CK) for f in (to_states, d_c)]

        def scan(b):
            re, im = slice(b * W2, b * W2 + W), slice(b * W2 + W, (b + 1) * W2)
            ar, ai = pw_ref[b, 0:8, :W], pw_ref[b, 0:8, W:]
            zero = jnp.zeros((8, W), F32)
            st = {"g": (zero, zero), "acc": (zero, zero)}

            def own(j):
                def emit():
                    rows = slice(j * 8, j * 8 + 8)
                    gr, gi = _cmul_add(g[b, rows, :W], g[b, rows, W:], ar, ai, *st["g"])
                    g[b, rows, :W] = gr
                    g[b, rows, W:] = gi
                    st["g"] = (gr, gi)
                return emit

            def ends():
                gr, gi = st["g"]
                for k, off in ((1, 8), (2, 16), (4, 24)):
                    gr, gi = _cmul_add(gr, gi, pw_ref[b, off:off + 8, :W], pw_ref[b, off:off + 8, W:],
                                       pltpu.roll(gr, 8 - k, 0), pltpu.roll(gi, 8 - k, 0))
                gr, gi = _cmul_add(gr, gi, pw_ref[b, 32:40, :W], pw_ref[b, 32:40, W:], carry[b, 0], carry[b, 1])
                st["c"] = (jnp.where(sub == 7, carry[b, 0], pltpu.roll(gr, 7, 0)),
                           jnp.where(sub == 7, carry[b, 1], pltpu.roll(gi, 7, 0)))
                carry[b, 0] = jnp.broadcast_to(gr[0:1], (8, W))
                carry[b, 1] = jnp.broadcast_to(gi[0:1], (8, W))

            def carried(j):
                def emit():
                    rows = slice(j * 8, j * 8 + 8)
                    cr, ci = _cmul(ar, ai, *st["c"])
                    gr, gi = g[b, rows, :W] + cr, g[b, rows, W:] + ci
                    g[b, rows, :W] = gr
                    g[b, rows, W:] = gi
                    if j > 0:
                        before = slice(j * 8 - 8, j * 8)
                        pr, pi = s_ref[before, re], s_ref[before, im]
                    else:
                        last = slice(TL - 8, TL)
                        pr = jnp.where(sub == 0, sp_ref[7:8, re] * live, pltpu.roll(s_ref[last, re], 1, 0))
                        pi = jnp.where(sub == 0, sp_ref[7:8, im] * live, pltpu.roll(s_ref[last, im], 1, 0))
                    accr, acci = st["acc"]
                    st["c"] = (cr, ci)
                    st["acc"] = (accr + pr * gr + pi * gi, acci + pr * gi - pi * gr)
                return emit

            def done():
                da_ref[b, :, :W] += st["acc"][0]
                da_ref[b, :, W:] += st["acc"][1]

            return ([own(j) for j in reversed(range(CH))] + [ends] + [carried(j) for j in reversed(range(CH))]
                    + [done])

        def tail(b):
            cols = slice(b * KB, (b + 1) * KB)
            acc = {}

            def d_u(c):
                def emit():
                    part = jnp.dot(g[b, :, c:c + CK].astype(BF), btw[b, c:c + CK, :], preferred_element_type=F32)
                    acc["u"] = part if c == 0 else acc["u"] + part
                return emit

            def d_b(c):
                def emit():
                    dbacc[b, :, c:c + CK] += lax.dot_general(hpb[:, cols], g[b, :, c:c + CK].astype(BF), tn,
                                                             preferred_element_type=F32)
                return emit

            def finish():
                dh_ref[:, cols] = _reorder(unperm[...], dys[:, cols] * d_ref[:, cols] + acc["u"])

            return [f(c) for c in range(0, W2, CK) for f in (d_u, d_b)] + [finish]

        for emit in lead(0):
            emit()
        for b in range(NB):
            side = (lead(b + 1) if b + 1 < NB else []) + (tail(b - 1) if b > 0 else [])
            _interleave(scan(b), side)
        for emit in tail(NB - 1):
            emit()

        @pl.when(t == nt - 1)
        def _():
            for b in range(NB):
                db_ref[b] = _extract(dbacc[b], dims)
                dc_ref[b] = _extract(dcacc[b], dims)

    rev = lambda kb, t: (nt - 1 - t, kb)
    prev = lambda kb, t: (jnp.maximum((nt - 1 - t) * CH - 1, 0), kb)
    per_kb = lambda kb, t: (kb, 0, 0)
    return pl.pallas_call(
        body, name=name, grid=(nkb // NB, nt),
        in_specs=[pl.BlockSpec((TL, NB * KB), rev), pl.BlockSpec((TL, NB * KB), rev),
                  pl.BlockSpec((TL, NB * KB), rev), pl.BlockSpec((TL, NB * W2), rev),
                  pl.BlockSpec((8, NB * W2), prev), pl.BlockSpec((NB, KB, P2), per_kb),
                  pl.BlockSpec((NB, P2, KB), per_kb), pl.BlockSpec((NB, 40, W2), per_kb),
                  pl.BlockSpec((1, NB * KB), lambda kb, t: (0, kb))],
        out_specs=[pl.BlockSpec((TL, NB * KB), rev), pl.BlockSpec((8, NB * KB), lambda kb, t: (0, kb)),
                   pl.BlockSpec((NB, 8, W2), per_kb), pl.BlockSpec((NB, KB, P2), per_kb),
                   pl.BlockSpec((NB, KB, P2), per_kb)],
        out_shape=[jax.ShapeDtypeStruct((L, D), F32), jax.ShapeDtypeStruct((8, D), F32),
                   jax.ShapeDtypeStruct((nkb, 8, W2), F32), jax.ShapeDtypeStruct((nkb, KB, P2), F32),
                   jax.ShapeDtypeStruct((nkb, KB, P2), F32)],
        scratch_shapes=[pltpu.VMEM((NB, TL, W2), F32), pltpu.VMEM((NB, KB, W2), BF), pltpu.VMEM((NB, W2, KB), BF),
                        pltpu.VMEM((NB, KB, W2), F32), pltpu.VMEM((NB, KB, W2), F32), pltpu.VMEM((TL, NB * KB), F32),
                        pltpu.VMEM((TL, TL), BF), pltpu.VMEM((TL, TL), BF), pltpu.VMEM((NB, 2, 8, W), F32)],
        compiler_params=pltpu.CompilerParams(dimension_semantics=("parallel", "arbitrary"),
                                             vmem_limit_bytes=V7X_VMEM_BYTES - 4 * 1024 * 1024),
    )(dz, y, h, s, s, tc, tbt, pwr, dvec)


def _s5_bwd_one_block(dz, y, h, s, tc, tbt, pwr, dvec, name):
    L, D = h.shape
    nkb, KB, P2 = tc.shape
    P = P2 // 2
    W = (KB // SSM_GROUP) * P
    W2 = 2 * W
    dims = (KB, W2, SSM_GROUP, P, W)
    TL = _tile(L, (512, 256))
    CH = TL // 8
    nt = L // TL

    def body(dz_ref, y_ref, h_ref, s_ref, sp_ref, tc_ref, tbt_ref, pw_ref, d_ref,
             dh_ref, dd_ref, da_ref, db_ref, dc_ref, g, ctw, btw, dbacc, dcacc, perm, unperm, carry):
        t = pl.program_id(1)

        @pl.when(t == 0)
        def _():
            carry[...] = jnp.zeros_like(carry)
            dd_ref[...] = jnp.zeros_like(dd_ref)
            da_ref[...] = jnp.zeros_like(da_ref)
            dbacc[...] = jnp.zeros_like(dbacc)
            dcacc[...] = jnp.zeros_like(dcacc)
            ctw[...] = _expand(tc_ref[...], dims, False)
            btw[...] = _expand(tbt_ref[...], dims, True)
            perm[...] = _chunk_order(TL, CH, False).astype(perm.dtype)
            unperm[...] = _chunk_order(TL, CH, True).astype(perm.dtype)

        hp = jnp.dot(perm[...], h_ref[...].astype(BF), preferred_element_type=F32)
        dy = jnp.dot(perm[...], dz_ref[...].astype(BF), preferred_element_type=F32) * _gelu_grad(y_ref[...])
        dd_ref[...] += _rowsum8(dy * hp)
        dyb = dy.astype(BF)
        g[...] = jnp.dot(dyb, ctw[...], preferred_element_type=F32)
        ar, ai = pw_ref[0:8, :W], pw_ref[0:8, W:]

        def own(jj, x):
            rows = pl.ds(pl.multiple_of((CH - 1 - jj) * 8, 8), 8)
            gr, gi = _cmul_add(g[rows, :W], g[rows, W:], ar, ai, x[0], x[1])
            g[rows, :W] = gr
            g[rows, W:] = gi
            return gr, gi

        zero = jnp.zeros((8, W), F32)
        gr, gi = lax.fori_loop(0, CH, own, (zero, zero))
        for k, off in ((1, 8), (2, 16), (4, 24)):
            gr, gi = _cmul_add(gr, gi, pw_ref[off:off + 8, :W], pw_ref[off:off + 8, W:],
                               pltpu.roll(gr, 8 - k, 0), pltpu.roll(gi, 8 - k, 0))
        gr, gi = _cmul_add(gr, gi, pw_ref[32:40, :W], pw_ref[32:40, W:], carry[0], carry[1])
        sub = lax.broadcasted_iota(jnp.int32, (8, W), 0)
        cr = jnp.where(sub == 7, carry[0], pltpu.roll(gr, 7, 0))
        ci = jnp.where(sub == 7, carry[1], pltpu.roll(gi, 7, 0))
        carry[0] = jnp.broadcast_to(gr[0:1], (8, W))
        carry[1] = jnp.broadcast_to(gi[0:1], (8, W))

        def carried(jj, c):
            j = CH - 1 - jj
            rows = pl.ds(pl.multiple_of(j * 8, 8), 8)
            before = pl.ds(pl.multiple_of(j * 8 - 8, 8), 8)
            cr, ci = _cmul(ar, ai, c[0], c[1])
            gr, gi = g[rows, :W] + cr, g[rows, W:] + ci
            g[rows, :W] = gr
            g[rows, W:] = gi
            pr, pi = s_ref[before, :W], s_ref[before, W:]
            return cr, ci, c[2] + pr * gr + pi * gi, c[3] + pr * gi - pi * gr

        cr, ci, accr, acci = lax.fori_loop(0, CH - 1, carried, (cr, ci, zero, zero))
        live = jnp.where(t == nt - 1, 0.0, 1.0)
        cr, ci = _cmul(ar, ai, cr, ci)
        gr, gi = g[0:8, :W] + cr, g[0:8, W:] + ci
        g[0:8, :W] = gr
        g[0:8, W:] = gi
        pr = jnp.where(sub == 0, sp_ref[7:8, :W] * live, pltpu.roll(s_ref[TL - 8:TL, :W], 1, 0))
        pi = jnp.where(sub == 0, sp_ref[7:8, W:] * live, pltpu.roll(s_ref[TL - 8:TL, W:], 1, 0))
        da_ref[:, :W] += accr + pr * gr + pi * gi
        da_ref[:, W:] += acci + pr * gi - pi * gr

        gb = g[...].astype(BF)
        dh = dy * d_ref[...] + jnp.dot(gb, btw[...], preferred_element_type=F32)
        dh_ref[...] = _reorder(unperm[...], dh)
        tn = (((0,), (0,)), ((), ()))
        dbacc[...] += lax.dot_general(hp.astype(BF), gb, tn, preferred_element_type=F32)
        dcacc[...] += lax.dot_general(dyb, s_ref[...].astype(BF), tn, preferred_element_type=F32)

        @pl.when(t == nt - 1)
        def _():
            db_ref[...] = _extract(dbacc[...], dims)
            dc_ref[...] = _extract(dcacc[...], dims)

    rev = lambda kb, t: (nt - 1 - t, kb)
    prev = lambda kb, t: (jnp.maximum((nt - 1 - t) * CH - 1, 0), kb)
    per_kb = lambda kb, t: (kb, 0, 0)
    return pl.pallas_call(
        body, name=name, grid=(nkb, nt),
        in_specs=[pl.BlockSpec((TL, KB), rev), pl.BlockSpec((TL, KB), rev), pl.BlockSpec((TL, KB), rev),
                  pl.BlockSpec((TL, W2), rev), pl.BlockSpec((8, W2), prev),
                  pl.BlockSpec((None, KB, P2), per_kb), pl.BlockSpec((None, P2, KB), per_kb),
                  pl.BlockSpec((None, 40, W2), per_kb), pl.BlockSpec((1, KB), lambda kb, t: (0, kb))],
        out_specs=[pl.BlockSpec((TL, KB), rev), pl.BlockSpec((8, KB), lambda kb, t: (0, kb)),
                   pl.BlockSpec((None, 8, W2), per_kb), pl.BlockSpec((None, KB, P2), per_kb),
                   pl.BlockSpec((None, KB, P2), per_kb)],
        out_shape=[jax.ShapeDtypeStruct((L, D), F32), jax.ShapeDtypeStruct((8, D), F32),
                   jax.ShapeDtypeStruct((nkb, 8, W2), F32), jax.ShapeDtypeStruct((nkb, KB, P2), F32),
                   jax.ShapeDtypeStruct((nkb, KB, P2), F32)],
        scratch_shapes=[pltpu.VMEM((TL, W2), F32), pltpu.VMEM((KB, W2), BF), pltpu.VMEM((W2, KB), BF),
                        pltpu.VMEM((KB, W2), F32), pltpu.VMEM((KB, W2), F32), pltpu.VMEM((TL, TL), BF),
                        pltpu.VMEM((TL, TL), BF), pltpu.VMEM((2, 8, W), F32)],
        compiler_params=_cp("parallel", "arbitrary"))(dz, y, h, s, s, tc, tbt, pwr, dvec)


def _discretise(a_re, a_im, log_step, b_re, b_im):
    lr = jnp.minimum(a_re, -1e-4)
    li = a_im
    dt = jnp.exp(log_step)[:, None]
    mag = jnp.exp(lr * dt)
    abr = mag * jnp.cos(li * dt)
    abi = mag * jnp.sin(li * dt)
    den = lr * lr + li * li
    qr = ((abr - 1.0) * lr + abi * li) / den
    qi = (abi * lr - (abr - 1.0) * li) / den
    bbar_re = qr[..., None] * b_re - qi[..., None] * b_im
    bbar_im = qr[..., None] * b_im + qi[..., None] * b_re
    return abr, abi, bbar_re, bbar_im


def _compact(m_re, m_im, nkb):
    G, H, P = m_re.shape
    t = jnp.stack([m_re, m_im], axis=2).reshape(nkb, (G // nkb) * H, 2 * P).astype(BF)
    return t, jnp.swapaxes(t, 1, 2)


def _scan_powers(abr, abi, nkb, conj, CH):
    G, P = abr.shape
    if conj:
        abi = -abi

    def cmul(u, v):
        return u[0] * v[0] - u[1] * v[1], u[0] * v[1] + u[1] * v[0]

    q = (abr, abi)
    for _ in range(_log2(CH)):
        q = cmul(q, q)
    pows = [q]
    for _ in range(7):
        pows.append(cmul(pows[-1], q))
    row = jnp.arange(8)[:, None, None]

    def table(part):
        out = [jnp.broadcast_to((abr, abi)[part][None], (8, G, P))]
        for k in (1, 2, 4):
            keep = (row <= 7 - k) if conj else (row >= k)
            out.append(jnp.where(keep, pows[k - 1][part][None], 0.0))
        ends = jnp.stack([p[part] for p in pows])
        out.append(ends[::-1] if conj else ends)
        return jnp.concatenate(out, axis=0)

    GL = G // nkb
    t = jnp.stack([table(0), table(1)], axis=1)
    t = t.reshape(40, 2, nkb, GL * P).transpose(2, 0, 1, 3)
    return t.reshape(nkb, 40, 2 * GL * P)


def ada_mods(c_all, w_ada, b_sh, name):
    nl, D, NA = w_ada.shape

    def body(c_ref, w_ref, b_ref, o_ref):
        cv = c_ref[...]
        act = cv * jax.nn.sigmoid(cv)
        o_ref[...] = jnp.dot(act, w_ref[...], preferred_element_type=F32, precision=lax.Precision.HIGHEST) + b_ref[...]

    return pl.pallas_call(
        body, name=name, grid=(nl,),
        in_specs=[pl.BlockSpec((8, D), lambda i: (0, 0)), pl.BlockSpec((None, D, NA), lambda i: (i, 0, 0)),
                  pl.BlockSpec((None, 1, NA), lambda i: (i, 0, 0))],
        out_specs=pl.BlockSpec((None, 8, NA), lambda i: (i, 0, 0)),
        out_shape=jax.ShapeDtypeStruct((nl, 8, NA), F32), compiler_params=_cp("parallel"))(c_all, w_ada, b_sh)


def _adamw(w, g, m, v):
    m = ADAM_B1 * m + (1.0 - ADAM_B1) * g
    v = ADAM_B2 * v + (1.0 - ADAM_B2) * (g * g)
    m_hat = m / (1.0 - ADAM_B1 ** ADAM_STEP)
    v_hat = v / (1.0 - ADAM_B2 ** ADAM_STEP)
    return -ADAM_LR * (m_hat / (jnp.sqrt(v_hat) + ADAM_EPS) + ADAM_WD * w), m, v


def _adam_rows(R, C):
    cap = max(8, (256 * 1024) // C)
    for t in range(min(R, cap), 0, -1):
        if R % t == 0 and (t % 8 == 0 or t == R):
            return t
    return R


def adamw_ada(c_t, dm, w, m, v, name):
    nl, D, NA = w.shape
    TK = _tile(D, (256, 128))

    def body(c_ref, dm_ref, w_ref, m_ref, v_ref, g_ref, d_ref, nm_ref, nv_ref):
        cv = c_ref[...]
        act = cv * jax.nn.sigmoid(cv)
        g = jnp.dot(act, dm_ref[...], preferred_element_type=F32, precision=lax.Precision.HIGHEST)
        g_ref[...] = g
        d_ref[...], nm_ref[...], nv_ref[...] = _adamw(w_ref[...], g, m_ref[...], v_ref[...])

    big = pl.BlockSpec((None, TK, NA), lambda i, k: (i, k, 0))
    shape = jax.ShapeDtypeStruct(w.shape, F32)
    return pl.pallas_call(
        body, name=name, grid=(nl, D // TK),
        in_specs=[pl.BlockSpec((TK, 8), lambda i, k: (k, 0)), pl.BlockSpec((None, 8, NA), lambda i, k: (i, 0, 0)),
                  big, big, big],
        out_specs=[big] * 4, out_shape=[shape] * 4, compiler_params=_cp("parallel", "parallel"))(c_t, dm, w, m, v)


def adamw_sharded(w, m, v, ga, gb, name):
    nl, R, C = w.shape
    TR = _adam_rows(R, C)

    def body(w_ref, m_ref, v_ref, a_ref, b_ref, g_ref, d_ref, nm_ref, nv_ref):
        g = a_ref[...] + b_ref[...]
        g_ref[...] = g
        d_ref[...], nm_ref[...], nv_ref[...] = _adamw(w_ref[...], g, m_ref[...], v_ref[...])

    big = pl.BlockSpec((None, TR, C), lambda i, r: (i, r, 0))
    shape = jax.ShapeDtypeStruct(w.shape, F32)
    return pl.pallas_call(
        body, name=name, grid=(nl, R // TR), in_specs=[big] * 5,
        out_specs=[big] * 4, out_shape=[shape] * 4, compiler_params=_cp("parallel", "parallel"))(w, m, v, ga, gb)


def adamw_slab(g, w, m, v, name):
    R, C = g.shape
    TR = _tile(R, (160, 80, 40, 8))

    def body(g_ref, w_ref, m_ref, v_ref, d_ref, nm_ref, nv_ref):
        d_ref[...], nm_ref[...], nv_ref[...] = _adamw(w_ref[...], g_ref[...], m_ref[...], v_ref[...])

    big = pl.BlockSpec((TR, C), lambda r: (r, 0))
    shape = jax.ShapeDtypeStruct((R, C), F32)
    return pl.pallas_call(
        body, name=name, grid=(R // TR,), in_specs=[big] * 4,
        out_specs=[big] * 3, out_shape=[shape] * 3, compiler_params=_cp("parallel"))(g, w, m, v)


def adamw_plain(w, m, v, g, name):
    def body(w_ref, m_ref, v_ref, g_ref, d_ref, nm_ref, nv_ref):
        d_ref[...], nm_ref[...], nv_ref[...] = _adamw(w_ref[...], g_ref[...], m_ref[...], v_ref[...])

    shape = jax.ShapeDtypeStruct(w.shape, F32)
    return pl.pallas_call(body, name=name, out_shape=[shape] * 3,
                          compiler_params=pltpu.CompilerParams(vmem_limit_bytes=VMEM_LIMIT))(w, m, v, g)


def _slab_rows(a):
    n = a.size
    rows = -(-n // SLAB_W)
    return -(-rows // 8) * 8


def _pack(arrs, pad_rows_to=0):
    out = []
    for a in arrs:
        rows = _slab_rows(a)
        flat = a.reshape(-1).astype(F32)
        flat = jnp.pad(flat, (0, rows * SLAB_W - flat.shape[0]))
        out.append(flat.reshape(rows, SLAB_W))
    total = sum(o.shape[0] for o in out)
    if pad_rows_to and total % pad_rows_to:
        out.append(jnp.zeros((pad_rows_to - total % pad_rows_to, SLAB_W), F32))
    return jnp.concatenate(out, axis=0)


def _unpack(slab, like):
    out, r = [], 0
    for a in like:
        rows = _slab_rows(a)
        out.append(slab[r:r + rows].reshape(-1)[:a.size].reshape(a.shape))
        r += rows
    return out


WEIGHTS = ['norm1_g', 'norm2_g', 'w_ada', 'b_ada', 'ssm_a_re', 'ssm_a_im', 'ssm_log_step', 'ssm_b_re', 'ssm_b_im',
           'ssm_c_re', 'ssm_c_im', 'ssm_d', 'ssm_w_out', 'conv_w_in', 'conv_w', 'conv_w_out', 'w_ffn_in',
           'w_ffn_out', 'final_g']
SLAB = ['norm1_g', 'norm2_g', 'b_ada', 'ssm_a_re', 'ssm_a_im', 'ssm_log_step', 'ssm_b_re', 'ssm_b_im', 'ssm_c_re',
        'ssm_c_im', 'ssm_d', 'final_g']
SHARDED = ['ssm_w_out', 'conv_w_in', 'conv_w_out', 'w_ffn_in', 'w_ffn_out']


def kernel(x, c, norm1_g, norm2_g, w_ada, b_ada, ssm_a_re, ssm_a_im, ssm_log_step, ssm_b_re, ssm_b_im, ssm_c_re, ssm_c_im, ssm_d, ssm_w_out, conv_w_in, conv_w, conv_w_out, w_ffn_in, w_ffn_out, final_g, loss_target, m_norm1_g, m_norm2_g, m_w_ada, m_b_ada, m_ssm_a_re, m_ssm_a_im, m_ssm_log_step, m_ssm_b_re, m_ssm_b_im, m_ssm_c_re, m_ssm_c_im, m_ssm_d, m_ssm_w_out, m_conv_w_in, m_conv_w, m_conv_w_out, m_w_ffn_in, m_w_ffn_out, m_final_g, v_norm1_g, v_norm2_g, v_w_ada, v_b_ada, v_ssm_a_re, v_ssm_a_im, v_ssm_log_step, v_ssm_b_re, v_ssm_b_im, v_ssm_c_re, v_ssm_c_im, v_ssm_d, v_ssm_w_out, v_conv_w_in, v_conv_w, v_conv_w_out, v_w_ffn_in, v_w_ffn_out, v_final_g):
    given = dict(locals())
    W = {n: given[n] for n in WEIGHTS}
    Mo = {n: given["m_" + n] for n in WEIGHTS}
    Vo = {n: given["v_" + n] for n in WEIGHTS}

    xs = x[0]
    tgt = loss_target[0]
    L, D = xs.shape
    nlayer = norm1_g.shape[0]
    NA = w_ada.shape[2]
    G = ssm_a_re.shape[1]
    nkb = D // S5_BLOCK
    ax, ay, ac = _axes()
    me = 4 * ax + 2 * ay + ac
    chip = 2 * ax + ay

    c_all = gather8(jnp.broadcast_to(c, (8, D)), "gather_c")[:, 0, :]
    b_sh = lax.dynamic_slice_in_dim(b_ada, chip * NA, NA, axis=1)[:, None, :]
    mods_part = ada_mods(c_all, w_ada, b_sh, "ada_mods")
    mg = gather8(mods_part.reshape(nlayer * 8, NA), "gather_mods")
    mg = mg.reshape(N_CHIP, 2, nlayer, 8, NA)[:, 0]
    mods_all = lax.dynamic_index_in_dim(mg, me, axis=2, keepdims=False)
    mods_all = jnp.transpose(mods_all, (1, 0, 2)).reshape(nlayer, 6, D)

    cw_parts = gather8(_pack([conv_w]), "gather_conv_w")
    nconv = conv_w.shape[0]
    cw_full = jnp.stack([_unpack(cw_parts[2 * q], [conv_w])[0] for q in range(N_CHIP)], axis=2)
    cw_full = cw_full.reshape(nconv, 3, D)

    in_flight_w = {}

    def start_weights(i, after):
        names = (["ssm_w_out"] if i % 2 == 0 else ["conv_w_in", "conv_w_out"]) + ["w_ffn_in", "w_ffn_out"]
        shards = [W[n][i if n.startswith("w_ffn") else i // 2].astype(BF) for n in names]
        sems, srcs, lands, tok = gather_start(shards, after, "gather_start%d" % i)
        in_flight_w[i] = (names, sems, srcs, lands)
        return tok

    def relay_weights(i, after):
        names, sems, srcs, lands = in_flight_w[i]
        got = gather_wait(sems, srcs, lands, list(range(len(names))), after, "gather_wait%d" % i)
        rsems, rlands, tok = relay_start(got, got[0], "relay_start%d" % i)
        in_flight_w[i] = (names, rsems, rlands)
        return tok

    def layer_weights(i, after):
        names, rsems, rlands = in_flight_w[i]
        return dict(zip(names, relay_wait(rsems, rlands, after, "relay_wait%d" % i)))

    token = start_weights(0, cw_full + mods_all[0, 0:3])
    mods_all = mods_all + token[0:1, 0:1]

    s5 = []
    for j in range(ssm_a_re.shape[0]):
        disc, disc_vjp = jax.vjp(_discretise, ssm_a_re[j], ssm_a_im[j], ssm_log_step[j], ssm_b_re[j], ssm_b_im[j])
        abr, abi, bbar_re, bbar_im = disc
        tb, tbt = _compact(jnp.swapaxes(bbar_re, 1, 2), jnp.swapaxes(bbar_im, 1, 2), nkb)
        tc, tct = _compact(ssm_c_re[j], -ssm_c_im[j], nkb)
        chunk = _tile(L, (512, 256)) // 8
        s5.append(dict(vjp=disc_vjp, tb=tb, tbt=tbt, tc=tc, tct=tct, pw=_scan_powers(abr, abi, nkb, False, chunk),
                       pwr=_scan_powers(abr, abi, nkb, True, chunk)))

    saved = []
    xcur = xs
    for i in range(nlayer):
        j = i // 2
        mods = mods_all[i]
        sv = dict(x=xcur)
        if i % 2 == 0:
            h = norm_mod(xcur, norm1_g[i:i + 1], mods, 0, F32, "norm_mod_s5")
            states, yv, z = s5_fwd(h, s5[j]["tb"], s5[j]["tct"], s5[j]["pw"], ssm_d[j:j + 1], "s5_fwd")
            if i == 0:
                token = relay_weights(0, z)
                token = token + start_weights(1, token)
                mods = mods + token[0:1, 0:1]
            full = layer_weights(i, z)
            o, mix, x2 = ssm_out_glu(z, full["ssm_w_out"], xcur, mods, 2, "ssm_out_glu")
            sv.update(h=h, states=states, y=yv, z=z, o=o)
        else:
            h = norm_mod(xcur, norm1_g[i:i + 1], mods, 0, BF, "norm_mod")
            full = layer_weights(i, h)
            p = mm_nn(h, full["conv_w_in"], BF, "mm_conv_in")
            mc = conv_fwd(p, cw_full[j], "conv_fwd")
            mix, x2 = mm_nn(mc, full["conv_w_out"].reshape(1, D, D), BF, "mm_conv_out", res=xcur, gate=mods[2:3])
            sv.update(h=h, p=p, mc=mc)
        h2 = norm_mod(x2, norm2_g[i:i + 1], mods, 3, BF, "norm_mod")
        gu, act = ffn_in_act(h2, full["w_ffn_in"], "ffn_in_act")
        if i + 1 < nlayer:
            token = relay_weights(i + 1, act)
            if i + 2 < nlayer:
                token = token + start_weights(i + 2, token)
            mods = mods + token[0:1, 0:1]
        F = act.shape[1]
        ff, x3 = mm_nn(act, full["w_ffn_out"].reshape(1, F, D), BF, "mm_ffn_out", res=x2, gate=mods[5:6])
        sv.update(mix=mix, x2=x2, h2=h2, gu=gu, act=act, ff=ff, w=full)
        saved.append(sv)
        xcur = x3

    loss_blk, dx, dfinal, dff = final_loss(xcur, tgt, final_g[None, :], saved[-1]["ff"], mods_all[nlayer - 1], 5,
                                           "final_loss")
    dg2 = dfinal[1:2]

    gland = {n: lax.empty((W[n].shape[0], N_CHIP) + W[n].shape[1:], BF) for n in SHARDED}
    in_flight = []
    dmods = [None] * nlayer
    dnorm1, dnorm2 = [None] * nlayer, [None] * nlayer
    dconv_w = [None] * nconv
    ds5 = [None] * ssm_a_re.shape[0]
    token = jnp.zeros((8, 128), F32)

    def send_grads(names, grads, slot, after, name):
        sems, thru, lands, tok = scatter_start([grads[n] for n in names], [gland[n] for n in names], slot, after, name)
        gland.update(zip(names, lands))
        in_flight.append((names, slot, sems, thru, name))
        return tok

    def land_grads(group, after):
        for names, slot, sems, thru, name in in_flight:
            if names[0] in group:
                got = scatter_wait(sems, thru, [gland[n] for n in names], slot, after, name.replace("scatter", "landed"))
                gland.update(zip(names, got))

    for i in reversed(range(nlayer)):
        j = i // 2
        mods = mods_all[i] + token[0:1, 0:1]
        sv = saved[i]
        full = sv["w"]
        gfull = {}
        F = sv["act"].shape[1]
        gfull["w_ffn_out"] = mm_tn(sv["act"], dff, 1, "mm_tn_ffn_out").reshape(N_CHIP, F // N_CHIP, D)
        dgu = ffn_out_bwd(dff, full["w_ffn_out"].reshape(F, D), sv["gu"], "ffn_out_bwd")
        gfull["w_ffn_in"] = mm_tn(sv["h2"], dgu, N_CHIP, "mm_tn_ffn_in")
        dh2 = mm_nt(dgu, full["w_ffn_in"], F32, "mm_nt_ffn_in")
        token = send_grads(["w_ffn_out", "w_ffn_in"], gfull, [i, i], dh2, "scatter_ffn%d" % i)
        mods = mods + token[0:1, 0:1]
        dx2, s2, dmix = norm_bwd(dh2, sv["x2"], dx, norm2_g[i:i + 1], mods, 3, "norm_bwd_mix",
                                 branch=(sv["mix"], mods, 2))
        dg1 = s2[3:4]
        if i % 2 == 0:
            do = glu_bwd(dmix, sv["o"], "glu_bwd")
            gfull["ssm_w_out"] = mm_tn(sv["z"], do, N_CHIP, "mm_tn_ssm_out")
            dz = mm_nt(do, full["ssm_w_out"], BF, "mm_nt_ssm_out")
            dh, dd, dab, db, dc = s5_bwd(dz, sv["y"], sv["h"], sv["states"], s5[j]["tc"], s5[j]["tbt"], s5[j]["pwr"],
                                         ssm_d[j:j + 1], "s5_bwd")
            ds5[j] = (dd, dab, db, dc)
        else:
            gfull["conv_w_out"] = mm_tn(sv["mc"], dmix, 1, "mm_tn_conv_out").reshape(N_CHIP, D // N_CHIP, D)
            dmc = mm_nt(dmix, full["conv_w_out"].reshape(1, D, D), BF, "mm_nt_conv_out")
            dbg, dcg, dvv, dcw = conv_bwd(dmc, sv["p"], cw_full[j], "conv_bwd")
            dp = jnp.concatenate([dbg, dcg, dvv], axis=1)
            gfull["conv_w_in"] = mm_tn(sv["h"], dp, N_CHIP, "mm_tn_conv_in")
            dh = mm_nt(dp, full["conv_w_in"], F32, "mm_nt_conv_in")
            dconv_w[j] = dcw[0:3]
        dmods_i = [s2[0:2], dg2]
        if i > 0:
            dx, s1, dff = norm_bwd(dh, sv["x"], dx2, norm1_g[i:i + 1], mods, 0, "norm_bwd_ffn",
                                   branch=(saved[i - 1]["ff"], mods_all[i - 1], 5))
            dg2 = s1[3:4]
        else:
            dx, s1 = norm_bwd(dh, sv["x"], dx2, norm1_g[i:i + 1], mods, 0, "norm_bwd")
        dmods[i] = jnp.concatenate([s1[0:2], dg1] + dmods_i, axis=0).reshape(6 * D)
        dnorm1[i], dnorm2[i] = s1[2], s2[2]
        names = ["ssm_w_out"] if i % 2 == 0 else ["conv_w_out", "conv_w_in"]
        token = send_grads(names, gfull, [j] * len(names), dx, "scatter_mix%d" % i)

    small = dict(norm1_g=jnp.stack(dnorm1), norm2_g=jnp.stack(dnorm2), b_ada=jnp.stack(dmods), final_g=dfinal[0])
    per = {n: [] for n in ('ssm_a_re', 'ssm_a_im', 'ssm_log_step', 'ssm_b_re', 'ssm_b_im', 'ssm_c_re', 'ssm_c_im', 'ssm_d')}
    GL = G // nkb
    for j, (dd, dab, db, dc) in enumerate(ds5):
        dab = jnp.sum(dab, axis=1).reshape(nkb, 2, GL, SSM_STATE)
        g_abr, g_abi = dab[:, 0].reshape(G, SSM_STATE), dab[:, 1].reshape(G, SSM_STATE)
        db, dc = db.reshape(G, SSM_GROUP, 2, SSM_STATE), dc.reshape(G, SSM_GROUP, 2, SSM_STATE)
        gb_re, gb_im, gc_re, gc_im = db[:, :, 0], db[:, :, 1], dc[:, :, 0], dc[:, :, 1]
        ga_re, ga_im, gls, gbr, gbi = s5[j]["vjp"]((g_abr, g_abi, jnp.swapaxes(gb_re, 1, 2), jnp.swapaxes(gb_im, 1, 2)))
        for n, val in zip(per, (ga_re, ga_im, gls, gbr, gbi, gc_re, -gc_im, jnp.sum(dd, axis=0))):
            per[n].append(val)
    small.update({n: jnp.stack(vals) for n, vals in per.items()})
    dcw_full = jnp.stack(dconv_w)

    slab_like = [W[n] for n in SLAB] + [dcw_full]
    rows64 = 8 * N_DEV
    slab = _pack([small[n] for n in SLAB] + [dcw_full], rows64)
    per_dev = slab.shape[0] // N_DEV
    x_sems, x_srcs, x_lands, token = exchange_start(
        [(slab.reshape(N_DEV, per_dev, SLAB_W), True), (_pack([small["b_ada"]]), False)], dx, "small_scatter")

    early = [n for n in SHARDED if n != "ssm_w_out"]
    land_grads(early, token)
    mine = [reduce4(gland[n], "reduce4_" + n) for n in early]

    parts, dm_all = exchange_wait(x_sems, x_srcs, x_lands, [True, False], mine[-1][0, :8, :128], "small_landed")
    t_sems, t_srcs, t_lands, token = exchange_start([(sum8(parts, "sum_small"), False)], dm_all, "small_gather")
    out = {}

    w_sems, w_srcs, w_lands, token2 = swap_start(mine, "swap_start")
    dm_all = dm_all.reshape(N_DEV, -1)[:, :b_ada.size].reshape(N_DEV, nlayer, N_CHIP, NA)
    dm_sh = jnp.transpose(lax.dynamic_index_in_dim(dm_all, chip, axis=2, keepdims=False), (1, 0, 2))
    res = adamw_ada(jnp.transpose(c_all) + token[0:1, 0:1] + token2[0:1, 0:1], dm_sh, w_ada, m_w_ada, v_w_ada,
                    "adamw_ada")
    out["g", "w_ada"], out["d", "w_ada"], out["m", "w_ada"], out["v", "w_ada"] = res

    g_slab = exchange_wait(t_sems, t_srcs, t_lands, [False], out["g", "w_ada"], "small_total")[0]
    g_slab = g_slab.reshape(slab.shape)
    d_slab, m_slab, v_slab = adamw_slab(
        g_slab, _pack([W[n] for n in SLAB] + [jnp.zeros_like(dcw_full)], rows64),
        _pack([Mo[n] for n in SLAB] + [jnp.zeros_like(dcw_full)], rows64),
        _pack([Vo[n] for n in SLAB] + [jnp.ones_like(dcw_full)], rows64), "adamw_slab")
    for k, slab in zip(("g", "d", "m", "v"), (g_slab, d_slab, m_slab, v_slab)):
        for n, val in zip(SLAB, _unpack(slab, slab_like)):
            out[k, n] = val
    g_cw = lax.dynamic_slice_in_dim(_unpack(g_slab, slab_like)[-1], chip * conv_w.shape[2], conv_w.shape[2], axis=2)
    out["g", "conv_w"] = g_cw
    out["d", "conv_w"], out["m", "conv_w"], out["v", "conv_w"] = [
        r.reshape(conv_w.shape) for r in adamw_plain(conv_w.reshape(-1, conv_w.shape[2]), m_conv_w.reshape(-1, conv_w.shape[2]),
                                                     v_conv_w.reshape(-1, conv_w.shape[2]), g_cw.reshape(-1, conv_w.shape[2]),
                                                     "adamw_conv_w")]

    mine, theirs = swap_wait(w_sems, w_srcs, w_lands, d_slab, "swap_wait")
    for n, ga, gb in zip(early, mine, theirs):
        r = adamw_sharded(W[n], Mo[n], Vo[n], ga, gb, "adamw_" + n)
        out["g", n], out["d", n], out["m", n], out["v", n] = r

    land_grads(["ssm_w_out"], out["g", "w_ffn_out"])
    ga = reduce4(gland["ssm_w_out"], "reduce4_ssm_w_out")
    gb = swap_siblings([ga], "swap_siblings")[0]
    r = adamw_sharded(ssm_w_out, m_ssm_w_out, v_ssm_w_out, ga, gb, "adamw_ssm_w_out")
    out["g", "ssm_w_out"], out["d", "ssm_w_out"], out["m", "ssm_w_out"], out["v", "ssm_w_out"] = r

    loss = lax.psum(loss_blk[0, 0], ("x", "y", "c"))
    return (loss, dx[None], *[out["g", n] for n in WEIGHTS], *[out["d", n] for n in WEIGHTS],
            *[out["m", n] for n in WEIGHTS], *[out["v", n] for n in WEIGHTS])
```

```python
import functools
import math

import jax
import jax.numpy as jnp
from jax import lax
from jax.experimental import pallas as pl
from jax.experimental.pallas import tpu as pltpu

F32 = jnp.float32
BF = jnp.bfloat16
MESH = pl.DeviceIdType.MESH
ANY = pl.BlockSpec(memory_space=pl.ANY)

N_DEV = 8
N_CHIP = 4
DEPTH = 4
SSM_GROUP = 16
SSM_STATE = 64
S5_BLOCK = 256
RMS_EPS = 1e-6
ADAM_LR, ADAM_B1, ADAM_B2, ADAM_EPS, ADAM_WD, ADAM_STEP = 0.001, 0.9, 0.999, 1e-08, 0.01, 10
V7X_VMEM_BYTES = 64 * 1024 * 1024
VMEM_LIMIT = V7X_VMEM_BYTES - 12 * 1024 * 1024
SLAB_W = 1024
GELU_C = math.sqrt(2.0 / math.pi)
GELU_A = 0.044715


def _cp(*sem):
    return pltpu.CompilerParams(dimension_semantics=sem if sem else None, vmem_limit_bytes=VMEM_LIMIT)


def _tile(n, prefs):
    for p in prefs:
        if p <= n and n % p == 0:
            return p
    return n


def _axes():
    return lax.axis_index("x"), lax.axis_index("y"), lax.axis_index("c")


def _flip(v, k):
    return 1 - v if k else v


def gather8(v, name):
    R, C = v.shape

    def body(v_ref, o_ref, ssem, rsem, lsem):
        x, y, c = _axes()
        me = 4 * x + 2 * y + c
        loc = pltpu.make_async_copy(v_ref, o_ref.at[me], lsem)
        loc.start()
        copies = []
        for k in range(1, N_DEV):
            peer = (_flip(x, (k >> 2) & 1), _flip(y, (k >> 1) & 1), _flip(c, k & 1))
            cp = pltpu.make_async_remote_copy(src_ref=v_ref, dst_ref=o_ref.at[me], send_sem=ssem.at[k - 1],
                                              recv_sem=rsem.at[k - 1], device_id=peer, device_id_type=MESH)
            cp.start()
            copies.append(cp)
        for cp in copies:
            cp.wait()
        loc.wait()

    return pl.pallas_call(
        body, name=name,
        out_shape=jax.ShapeDtypeStruct((N_DEV, R, C), v.dtype),
        in_specs=[pl.BlockSpec(memory_space=pltpu.VMEM)],
        out_specs=pl.BlockSpec(memory_space=pltpu.VMEM),
        scratch_shapes=[pltpu.SemaphoreType.DMA((N_DEV - 1,)), pltpu.SemaphoreType.DMA((N_DEV - 1,)),
                        pltpu.SemaphoreType.DMA],
        compiler_params=pltpu.CompilerParams(vmem_limit_bytes=VMEM_LIMIT),
    )(v)


HBM = pl.BlockSpec(memory_space=pltpu.HBM)
SEM = pl.BlockSpec(memory_space=pltpu.SEMAPHORE)
EFFECT = pltpu.SideEffectType.DATAFLOW_SIDE_EFFECTING


def _in_hbm(a):
    return pltpu.with_memory_space_constraint(a, pltpu.HBM)


def _chip_peers(x, y, c):
    out = []
    for k in range(1, N_CHIP):
        px, py = _flip(x, k >> 1), _flip(y, k & 1)
        out.append(((px, py, c), 2 * px + py))
    return out


def _my_half(ref, c):
    rows = ref.shape[0] // 2
    return pl.ds(pl.multiple_of(c * rows, 16), rows)


def relay_start(lands, after, name):
    n = len(lands)

    def body(*refs):
        land = refs[:n]
        ssem, rsem = refs[n + 1:n + 3]
        token = refs[-1]
        x, y, c = _axes()
        for a in range(n):
            half = _my_half(land[a].at[0], c)
            for k, (_, pchip) in enumerate(_chip_peers(x, y, c)):
                pltpu.make_async_remote_copy(src_ref=land[a].at[pchip, half], dst_ref=land[a].at[pchip, half],
                                             send_sem=ssem.at[3 * a + k], recv_sem=rsem.at[3 * a + k],
                                             device_id=(x, y, 1 - c), device_id_type=MESH).start()
        token[...] = jnp.zeros_like(token)

    out_shape = ([pltpu.SemaphoreType.DMA((3 * n,)), pltpu.SemaphoreType.DMA((3 * n,))]
                 + [pltpu.HBM(l.shape, l.dtype) for l in lands] + [jax.ShapeDtypeStruct((8, 128), F32)])
    res = pl.pallas_call(
        body, name=name, out_shape=out_shape, in_specs=[HBM] * n + [ANY],
        out_specs=[SEM, SEM] + [HBM] * n + [pl.BlockSpec(memory_space=pltpu.VMEM)],
        input_output_aliases={a: 2 + a for a in range(n)},
        compiler_params=pltpu.CompilerParams(has_side_effects=EFFECT),
    )(*lands, after)
    return tuple(res[:2]), list(res[2:2 + n]), res[-1]


def relay_wait(sems, lands, after, name):
    n = len(lands)

    def body(*refs):
        land = refs[:n]
        ssem, rsem = refs[n:n + 2]
        x, y, c = _axes()
        for a in range(n):
            mine, theirs = _my_half(land[a].at[0], c), _my_half(land[a].at[0], 1 - c)
            for k, (_, pchip) in enumerate(_chip_peers(x, y, c)):
                cp = pltpu.make_async_remote_copy(src_ref=land[a].at[pchip, mine], dst_ref=land[a].at[pchip, theirs],
                                                  send_sem=ssem.at[3 * a + k], recv_sem=rsem.at[3 * a + k],
                                                  device_id=(x, y, 1 - c), device_id_type=MESH)
                cp.wait_send()
                cp.wait_recv()

    res = pl.pallas_call(
        body, name=name, out_shape=[pltpu.HBM(l.shape, l.dtype) for l in lands],
        in_specs=[HBM] * n + [SEM, SEM, ANY], out_specs=[HBM] * n,
        input_output_aliases={a: a for a in range(n)},
        compiler_params=pltpu.CompilerParams(has_side_effects=EFFECT),
    )(*lands, *sems, after)
    return list(res)


def gather_start(shards, after, name):
    n = len(shards)

    def body(*refs):
        src, land = refs[:n], refs[n:2 * n]
        ssem, rsem, lsem = refs[2 * n + 1:2 * n + 4]
        token = refs[-1]
        x, y, c = _axes()
        chip = 2 * x + y
        for a in range(n):
            pltpu.make_async_copy(src[a], land[a].at[chip], lsem.at[a]).start()
            half = _my_half(src[a], c)
            for k, (peer, _) in enumerate(_chip_peers(x, y, c)):
                pltpu.make_async_remote_copy(src_ref=src[a].at[half], dst_ref=land[a].at[chip, half],
                                             send_sem=ssem.at[3 * a + k], recv_sem=rsem.at[3 * a + k],
                                             device_id=peer, device_id_type=MESH).start()
        token[...] = jnp.zeros_like(token)

    lands = [lax.empty((N_CHIP,) + s.shape, s.dtype) for s in shards]
    out_shape = ([pltpu.SemaphoreType.DMA((3 * n,)), pltpu.SemaphoreType.DMA((3 * n,)), pltpu.SemaphoreType.DMA((n,))]
                 + [pltpu.HBM(s.shape, s.dtype) for s in shards] + [pltpu.HBM(l.shape, l.dtype) for l in lands]
                 + [jax.ShapeDtypeStruct((8, 128), F32)])
    res = pl.pallas_call(
        body, name=name, out_shape=out_shape, in_specs=[HBM] * (2 * n) + [ANY],
        out_specs=[SEM, SEM, SEM] + [HBM] * (2 * n) + [pl.BlockSpec(memory_space=pltpu.VMEM)],
        input_output_aliases={a: 3 + a for a in range(2 * n)},
        compiler_params=pltpu.CompilerParams(has_side_effects=EFFECT),
    )(*[_in_hbm(s) for s in shards], *[_in_hbm(l) for l in lands], after)
    return tuple(res[:3]), list(res[3:3 + n]), list(res[3 + n:3 + 2 * n]), res[-1]


def gather_wait(sems, srcs, lands, idx, after, name):
    m = len(idx)

    def body(*refs):
        src, land = refs[:m], refs[m:2 * m]
        ssem, rsem, lsem = refs[2 * m:2 * m + 3]
        x, y, c = _axes()
        chip = 2 * x + y
        for j, a in enumerate(idx):
            half = _my_half(src[j], c)
            for k, (peer, pchip) in enumerate(_chip_peers(x, y, c)):
                cp = pltpu.make_async_remote_copy(src_ref=src[j].at[half], dst_ref=land[j].at[pchip, half],
                                                  send_sem=ssem.at[3 * a + k], recv_sem=rsem.at[3 * a + k],
                                                  device_id=peer, device_id_type=MESH)
                cp.wait_send()
                cp.wait_recv()
            pltpu.make_async_copy(src[j], land[j].at[chip], lsem.at[a]).wait()

    s_in = [srcs[a] for a in idx]
    l_in = [lands[a] for a in idx]
    res = pl.pallas_call(
        body, name=name,
        out_shape=[pltpu.HBM(s.shape, s.dtype) for s in s_in] + [pltpu.HBM(l.shape, l.dtype) for l in l_in],
        in_specs=[HBM] * (2 * m) + [SEM, SEM, SEM, ANY], out_specs=[HBM] * (2 * m),
        input_output_aliases={a: a for a in range(2 * m)},
        compiler_params=pltpu.CompilerParams(has_side_effects=EFFECT),
    )(*s_in, *l_in, *sems, after)
    return list(res[m:])


def scatter_start(grads, lands, slot, after, name):
    n = len(grads)

    def body(*refs):
        src, land = refs[:n], refs[n:2 * n]
        ssem, rsem, lsem = refs[2 * n + 1:2 * n + 4]
        token = refs[-1]
        x, y, c = _axes()
        chip = 2 * x + y
        for a in range(n):
            pltpu.make_async_copy(src[a].at[chip], land[a].at[slot[a], chip], lsem.at[a]).start()
            for k, (peer, pchip) in enumerate(_chip_peers(x, y, c)):
                pltpu.make_async_remote_copy(src_ref=src[a].at[pchip], dst_ref=land[a].at[slot[a], chip],
                                             send_sem=ssem.at[3 * a + k], recv_sem=rsem.at[3 * a + k],
                                             device_id=peer, device_id_type=MESH).start()
        token[...] = jnp.zeros_like(token)

    out_shape = ([pltpu.SemaphoreType.DMA((3 * n,)), pltpu.SemaphoreType.DMA((3 * n,)), pltpu.SemaphoreType.DMA((n,))]
                 + [pltpu.HBM(g.shape, g.dtype) for g in grads] + [pltpu.HBM(l.shape, l.dtype) for l in lands]
                 + [jax.ShapeDtypeStruct((8, 128), F32)])
    res = pl.pallas_call(
        body, name=name, out_shape=out_shape, in_specs=[HBM] * (2 * n) + [ANY],
        out_specs=[SEM, SEM, SEM] + [HBM] * (2 * n) + [pl.BlockSpec(memory_space=pltpu.VMEM)],
        input_output_aliases={a: 3 + a for a in range(2 * n)},
        compiler_params=pltpu.CompilerParams(has_side_effects=EFFECT),
    )(*[_in_hbm(g) for g in grads], *[_in_hbm(l) for l in lands], after)
    return tuple(res[:3]), list(res[3:3 + n]), list(res[3 + n:3 + 2 * n]), res[-1]


def scatter_wait(sems, grads, lands, slot, after, name):
    n = len(grads)

    def body(*refs):
        src, land = refs[:n], refs[n:2 * n]
        ssem, rsem, lsem = refs[2 * n:2 * n + 3]
        x, y, c = _axes()
        chip = 2 * x + y
        for a in range(n):
            for k, (peer, pchip) in enumerate(_chip_peers(x, y, c)):
                cp = pltpu.make_async_remote_copy(src_ref=src[a].at[pchip], dst_ref=land[a].at[slot[a], pchip],
                                                  send_sem=ssem.at[3 * a + k], recv_sem=rsem.at[3 * a + k],
                                                  device_id=peer, device_id_type=MESH)
                cp.wait_send()
                cp.wait_recv()
            pltpu.make_async_copy(src[a].at[chip], land[a].at[slot[a], chip], lsem.at[a]).wait()

    res = pl.pallas_call(
        body, name=name,
        out_shape=[pltpu.HBM(g.shape, g.dtype) for g in grads] + [pltpu.HBM(l.shape, l.dtype) for l in lands],
        in_specs=[HBM] * (2 * n) + [SEM, SEM, SEM, ANY], out_specs=[HBM] * (2 * n),
        input_output_aliases={a: a for a in range(2 * n)},
        compiler_params=pltpu.CompilerParams(has_side_effects=EFFECT),
    )(*grads, *lands, *sems, after)
    return list(res[n:])


def reduce4(land, name):
    nl, _, R, C = land.shape
    TR = _adam_rows(R, C)

    def body(l_ref, o_ref):
        o_ref[...] = ((l_ref[0].astype(F32) + l_ref[1].astype(F32)) + l_ref[2].astype(F32)) + l_ref[3].astype(F32)

    return pl.pallas_call(
        body, name=name, grid=(nl, R // TR),
        in_specs=[pl.BlockSpec((None, N_CHIP, TR, C), lambda i, r: (i, 0, r, 0))],
        out_specs=pl.BlockSpec((None, TR, C), lambda i, r: (i, r, 0)),
        out_shape=jax.ShapeDtypeStruct((nl, R, C), F32), compiler_params=_cp("parallel", "parallel"))(land)


def swap_siblings(arrs, name):
    n = len(arrs)

    def body(*refs):
        src, dst = refs[:n], refs[n:2 * n]
        ssem, rsem = refs[2 * n:]
        x, y, c = _axes()
        cps = [pltpu.make_async_remote_copy(src_ref=src[a], dst_ref=dst[a], send_sem=ssem.at[a], recv_sem=rsem.at[a],
                                            device_id=(x, y, 1 - c), device_id_type=MESH) for a in range(n)]
        for cp in cps:
            cp.start()
        for cp in cps:
            cp.wait()

    return pl.pallas_call(
        body, name=name, out_shape=[jax.ShapeDtypeStruct(a.shape, a.dtype) for a in arrs],
        in_specs=[ANY] * n, out_specs=[ANY] * n,
        scratch_shapes=[pltpu.SemaphoreType.DMA((n,)), pltpu.SemaphoreType.DMA((n,))],
        compiler_params=pltpu.CompilerParams(vmem_limit_bytes=VMEM_LIMIT),
    )(*arrs)


def swap_start(arrs, name):
    n = len(arrs)

    def body(*refs):
        src, land = refs[:n], refs[n:2 * n]
        ssem, rsem = refs[2 * n:2 * n + 2]
        token = refs[-1]
        x, y, c = _axes()
        for a in range(n):
            pltpu.make_async_remote_copy(src_ref=src[a], dst_ref=land[a], send_sem=ssem.at[a], recv_sem=rsem.at[a],
                                         device_id=(x, y, 1 - c), device_id_type=MESH).start()
        token[...] = jnp.zeros_like(token)

    lands = [lax.empty(a.shape, a.dtype) for a in arrs]
    out_shape = ([pltpu.SemaphoreType.DMA((n,)), pltpu.SemaphoreType.DMA((n,))]
                 + [pltpu.HBM(a.shape, a.dtype) for a in arrs] * 2 + [jax.ShapeDtypeStruct((8, 128), F32)])
    res = pl.pallas_call(
        body, name=name, out_shape=out_shape, in_specs=[HBM] * (2 * n),
        out_specs=[SEM, SEM] + [HBM] * (2 * n) + [pl.BlockSpec(memory_space=pltpu.VMEM)],
        input_output_aliases={a: 2 + a for a in range(2 * n)},
        compiler_params=pltpu.CompilerParams(has_side_effects=EFFECT),
    )(*[_in_hbm(a) for a in arrs], *[_in_hbm(l) for l in lands])
    return tuple(res[:2]), list(res[2:2 + n]), list(res[2 + n:2 + 2 * n]), res[-1]


def swap_wait(sems, srcs, lands, after, name):
    n = len(srcs)

    def body(*refs):
        src, land = refs[:n], refs[n:2 * n]
        ssem, rsem = refs[2 * n:2 * n + 2]
        x, y, c = _axes()
        for a in range(n):
            cp = pltpu.make_async_remote_copy(src_ref=src[a], dst_ref=land[a], send_sem=ssem.at[a],
                                              recv_sem=rsem.at[a], device_id=(x, y, 1 - c), device_id_type=MESH)
            cp.wait_send()
            cp.wait_recv()

    res = pl.pallas_call(
        body, name=name, out_shape=[pltpu.HBM(a.shape, a.dtype) for a in srcs] * 2,
        in_specs=[HBM] * (2 * n) + [SEM, SEM, ANY], out_specs=[HBM] * (2 * n),
        input_output_aliases={a: a for a in range(2 * n)},
        compiler_params=pltpu.CompilerParams(has_side_effects=EFFECT),
    )(*srcs, *lands, *sems, after)
    return list(res[:n]), list(res[n:])


def _all_peers(x, y, c):
    out = []
    for k in range(1, N_DEV):
        px, py, pc = _flip(x, (k >> 2) & 1), _flip(y, (k >> 1) & 1), _flip(c, k & 1)
        out.append(((px, py, pc), 4 * px + 2 * py + pc))
    return out


def exchange_start(items, after, name):
    n = len(items)

    def body(*refs):
        src, land = refs[:n], refs[n:2 * n]
        ssem, rsem, lsem = refs[2 * n + 1:2 * n + 4]
        token = refs[-1]
        x, y, c = _axes()
        me = 4 * x + 2 * y + c
        for a, (_, scatter) in enumerate(items):
            pltpu.make_async_copy(src[a].at[me] if scatter else src[a], land[a].at[me], lsem.at[a]).start()
            for k, (peer, p) in enumerate(_all_peers(x, y, c)):
                pltpu.make_async_remote_copy(src_ref=src[a].at[p] if scatter else src[a], dst_ref=land[a].at[me],
                                             send_sem=ssem.at[7 * a + k], recv_sem=rsem.at[7 * a + k],
                                             device_id=peer, device_id_type=MESH).start()
        token[...] = jnp.zeros_like(token)

    srcs = [s for s, _ in items]
    lands = [lax.empty(s.shape if sc else (N_DEV,) + s.shape, s.dtype) for s, sc in items]
    out_shape = ([pltpu.SemaphoreType.DMA((7 * n,)), pltpu.SemaphoreType.DMA((7 * n,)), pltpu.SemaphoreType.DMA((n,))]
                 + [pltpu.HBM(s.shape, s.dtype) for s in srcs] + [pltpu.HBM(l.shape, l.dtype) for l in lands]
                 + [jax.ShapeDtypeStruct((8, 128), F32)])
    res = pl.pallas_call(
        body, name=name, out_shape=out_shape, in_specs=[HBM] * (2 * n) + [ANY],
        out_specs=[SEM, SEM, SEM] + [HBM] * (2 * n) + [pl.BlockSpec(memory_space=pltpu.VMEM)],
        input_output_aliases={a: 3 + a for a in range(2 * n)},
        compiler_params=pltpu.CompilerParams(has_side_effects=EFFECT),
    )(*[_in_hbm(s) for s in srcs], *[_in_hbm(l) for l in lands], after)
    return tuple(res[:3]), list(res[3:3 + n]), list(res[3 + n:3 + 2 * n]), res[-1]


def exchange_wait(sems, srcs, lands, scatter, after, name):
    n = len(srcs)

    def body(*refs):
        src, land = refs[:n], refs[n:2 * n]
        ssem, rsem, lsem = refs[2 * n:2 * n + 3]
        x, y, c = _axes()
        me = 4 * x + 2 * y + c
        for a in range(n):
            for k, (peer, p) in enumerate(_all_peers(x, y, c)):
                cp = pltpu.make_async_remote_copy(src_ref=src[a].at[p] if scatter[a] else src[a],
                                                  dst_ref=land[a].at[p], send_sem=ssem.at[7 * a + k],
                                                  recv_sem=rsem.at[7 * a + k], device_id=peer, device_id_type=MESH)
                cp.wait_send()
                cp.wait_recv()
            pltpu.make_async_copy(src[a].at[me] if scatter[a] else src[a], land[a].at[me], lsem.at[a]).wait()

    res = pl.pallas_call(
        body, name=name,
        out_shape=[pltpu.HBM(s.shape, s.dtype) for s in srcs] + [pltpu.HBM(l.shape, l.dtype) for l in lands],
        in_specs=[HBM] * (2 * n) + [SEM, SEM, SEM, ANY], out_specs=[HBM] * (2 * n),
        input_output_aliases={a: a for a in range(2 * n)},
        compiler_params=pltpu.CompilerParams(has_side_effects=EFFECT),
    )(*srcs, *lands, *sems, after)
    return list(res[n:])


def sum8(parts, name):
    _, P, C = parts.shape

    def body(p_ref, o_ref):
        tot = p_ref[0]
        for d in range(1, N_DEV):
            tot = tot + p_ref[d]
        o_ref[...] = tot

    return pl.pallas_call(body, name=name, out_shape=jax.ShapeDtypeStruct((P, C), F32),
                          compiler_params=pltpu.CompilerParams(vmem_limit_bytes=VMEM_LIMIT))(parts)


def reduce8(slab, dm, name):
    RT, C = slab.shape
    P = RT // N_DEV
    R = dm.shape[0]

    def body(s_ref, dm_ref, o_ref, dmo_ref, recv, s1, r1, s2, r2, s3, r3):
        x, y, c = _axes()
        me = 4 * x + 2 * y + c
        mine = pl.ds(pl.multiple_of(me * P, 8), P)
        parts, dms = [], []
        for k in range(1, N_DEV):
            px, py, pc = _flip(x, (k >> 2) & 1), _flip(y, (k >> 1) & 1), _flip(c, k & 1)
            theirs = pl.ds(pl.multiple_of((4 * px + 2 * py + pc) * P, 8), P)
            cp = pltpu.make_async_remote_copy(src_ref=s_ref.at[theirs], dst_ref=recv.at[me], send_sem=s1.at[k - 1],
                                              recv_sem=r1.at[k - 1], device_id=(px, py, pc), device_id_type=MESH)
            cp.start()
            parts.append(cp)
            cd = pltpu.make_async_remote_copy(src_ref=dm_ref, dst_ref=dmo_ref.at[me], send_sem=s3.at[k - 1],
                                              recv_sem=r3.at[k - 1], device_id=(px, py, pc), device_id_type=MESH)
            cd.start()
            dms.append(cd)
        dmo_ref[me] = dm_ref[...]
        recv[me] = s_ref[mine, :]
        for cp in parts:
            cp.wait()
        tot = recv[0]
        for d in range(1, N_DEV):
            tot = tot + recv[d]
        o_ref[mine, :] = tot
        out = []
        for k in range(1, N_DEV):
            peer = (_flip(x, (k >> 2) & 1), _flip(y, (k >> 1) & 1), _flip(c, k & 1))
            cp = pltpu.make_async_remote_copy(src_ref=o_ref.at[mine], dst_ref=o_ref.at[mine], send_sem=s2.at[k - 1],
                                              recv_sem=r2.at[k - 1], device_id=peer, device_id_type=MESH)
            cp.start()
            out.append(cp)
        for cp in out + dms:
            cp.wait()

    sems = [pltpu.SemaphoreType.DMA((N_DEV - 1,))] * 6
    return pl.pallas_call(
        body, name=name,
        out_shape=[jax.ShapeDtypeStruct((RT, C), F32), jax.ShapeDtypeStruct((N_DEV, R, C), F32)],
        in_specs=[pl.BlockSpec(memory_space=pltpu.VMEM)] * 2, out_specs=[pl.BlockSpec(memory_space=pltpu.VMEM)] * 2,
        scratch_shapes=[pltpu.VMEM((N_DEV, P, C), F32)] + sems,
        compiler_params=pltpu.CompilerParams(vmem_limit_bytes=VMEM_LIMIT),
    )(slab, dm)


def mm_nn(a, w, out_dtype, name, res=None, gate=None):
    M, K = a.shape
    S, _, Ns = w.shape
    TM = _tile(M, (1024, 512, 256) if K <= 1024 else (512, 256))
    TN = _tile(Ns, (1408, 1024, 768, 512, 256, 128))
    nj = Ns // TN
    fused = res is not None

    def body(*refs):
        if fused:
            a_ref, w_ref, r_ref, g_ref, f_ref, o_ref = refs
        else:
            a_ref, w_ref, f_ref = refs
        f = jnp.dot(a_ref[...], w_ref[...], preferred_element_type=F32)
        f_ref[...] = f.astype(f_ref.dtype)
        if fused:
            o_ref[...] = r_ref[...] + g_ref[...] * f

    col = lambda s, j, i: (i, s * nj + j)
    in_specs = [pl.BlockSpec((TM, K), lambda s, j, i: (i, 0)), pl.BlockSpec((None, K, TN), lambda s, j, i: (s, 0, j))]
    out_specs = [pl.BlockSpec((TM, TN), col)]
    out_shape = [jax.ShapeDtypeStruct((M, S * Ns), out_dtype)]
    args = [a, w]
    if fused:
        in_specs += [pl.BlockSpec((TM, TN), col), pl.BlockSpec((1, TN), lambda s, j, i: (0, s * nj + j))]
        out_specs.append(pl.BlockSpec((TM, TN), col))
        out_shape.append(jax.ShapeDtypeStruct((M, S * Ns), F32))
        args += [res, gate]
    out = pl.pallas_call(body, name=name, grid=(S, nj, M // TM), in_specs=in_specs, out_specs=out_specs,
                         out_shape=out_shape, compiler_params=_cp("parallel", "parallel", "parallel"))(*args)
    return tuple(out) if fused else out[0]


def mm_nt(g, w, out_dtype, name):
    g3 = g if g.ndim == 3 else g[None]
    Q, M, F = g3.shape
    S, K, Ns = w.shape
    TM = _tile(M, (1024, 512, 256) if K <= 1024 else (512, 256))
    TN = _tile(Ns, (1408, 1024, 768, 512, 256, 128))
    nj = Ns // TN
    nred = S * nj
    per_part = F // TN

    def body(g_ref, w_ref, o_ref, acc):
        n = pl.program_id(1)

        @pl.when(n == 0)
        def _():
            acc[...] = jnp.zeros_like(acc)

        acc[...] += lax.dot_general(g_ref[...], w_ref[...], (((1,), (1,)), ((), ())), preferred_element_type=F32)

        @pl.when(n == nred - 1)
        def _():
            o_ref[...] = acc[...].astype(o_ref.dtype)

    return pl.pallas_call(
        body, name=name, grid=(M // TM, nred),
        in_specs=[pl.BlockSpec((None, TM, TN), lambda i, n: (n // per_part, i, n % per_part)),
                  pl.BlockSpec((None, K, TN), lambda i, n: (n // nj, 0, n % nj))],
        out_specs=pl.BlockSpec((TM, K), lambda i, n: (i, 0)),
        out_shape=jax.ShapeDtypeStruct((M, K), out_dtype),
        scratch_shapes=[pltpu.VMEM((TM, K), F32)],
        compiler_params=_cp("parallel", "arbitrary"))(g3, w)


def mm_tn(a, g, S, name):
    M, K = a.shape
    g3 = g if g.ndim == 3 else g[None]
    Q, _, F = g3.shape
    Ns = Q * F // S
    TK = _tile(K, (256, 128))
    TN = _tile(Ns, (1408, 1024, 768, 512, 256, 128))
    nj = Ns // TN
    per_part = F // TN

    def body(a_ref, g_ref, o_ref):
        o_ref[...] = lax.dot_general(a_ref[...], g_ref[...], (((0,), (0,)), ((), ())),
                                     preferred_element_type=F32).astype(o_ref.dtype)

    return pl.pallas_call(
        body, name=name, grid=(S * nj, K // TK),
        in_specs=[pl.BlockSpec((M, TK), lambda n, k: (0, k)),
                  pl.BlockSpec((None, M, TN), lambda n, k: (n // per_part, 0, n % per_part))],
        out_specs=pl.BlockSpec((None, TK, TN), lambda n, k: (n // nj, k, n % nj)),
        out_shape=jax.ShapeDtypeStruct((S, K, Ns), BF),
        compiler_params=_cp("parallel", "parallel"))(a, g3)


def _rows(TL, D):
    return pl.BlockSpec((TL, D), lambda i: (i, 0))


def _fixed(R, D):
    return pl.BlockSpec((R, D), lambda i: (0, 0))


def _rowsum8(v):
    T, D = v.shape
    return jnp.sum(v.reshape(T // 8, 8, D), axis=0)


def _norm_parts(xv):
    r = lax.rsqrt(jnp.mean(xv * xv, axis=-1, keepdims=True) + RMS_EPS)
    return xv * r, r


def norm_mod(x, gamma, mods, k_shift, out_dtype, name):
    L, D = x.shape
    TL = _tile(L, (512, 256))

    def body(x_ref, g_ref, m_ref, o_ref):
        xn, _ = _norm_parts(x_ref[...])
        sh, sc = m_ref[k_shift:k_shift + 1, :], m_ref[k_shift + 1:k_shift + 2, :]
        o_ref[...] = ((xn * g_ref[...]) * (1.0 + sc) + sh).astype(o_ref.dtype)

    return pl.pallas_call(body, name=name, grid=(L // TL,),
                          in_specs=[_rows(TL, D), _fixed(1, D), _fixed(6, D)], out_specs=_rows(TL, D),
                          out_shape=jax.ShapeDtypeStruct((L, D), out_dtype), compiler_params=_cp("parallel"))(x, gamma, mods)


def norm_bwd(dh, x, dres, gamma, mods, k_shift, name, branch=None):
    L, D = x.shape
    TL = _tile(L, (512, 256))
    nacc = 4 if branch else 3

    def body(*refs):
        if branch:
            dh_ref, x_ref, dr_ref, g_ref, m_ref, f_ref, fm_ref, dx_ref, s_ref, df_ref, acc = refs
        else:
            dh_ref, x_ref, dr_ref, g_ref, m_ref, dx_ref, s_ref, acc = refs
        i = pl.program_id(0)

        @pl.when(i == 0)
        def _():
            acc[...] = jnp.zeros_like(acc)

        xn, r = _norm_parts(x_ref[...])
        dh_v = dh_ref[...].astype(F32)
        gam = g_ref[...]
        sc = m_ref[k_shift + 1:k_shift + 2, :]
        dn = dh_v * (1.0 + sc)
        dxn = dn * gam
        dx = dr_ref[...] + r * (dxn - xn * jnp.mean(dxn * xn, axis=-1, keepdims=True))
        dx_ref[...] = dx
        acc[0] += _rowsum8(dh_v)
        acc[1] += _rowsum8(dh_v * (xn * gam))
        acc[2] += _rowsum8(dn * xn)
        if branch:
            df_ref[...] = (dx * fm_ref[branch[2]:branch[2] + 1, :]).astype(df_ref.dtype)
            acc[3] += _rowsum8(dx * f_ref[...].astype(F32))

        @pl.when(i == pl.num_programs(0) - 1)
        def _():
            s_ref[...] = jnp.zeros_like(s_ref)
            for q in range(nacc):
                s_ref[q:q + 1, :] = jnp.sum(acc[q], axis=0, keepdims=True)

    in_specs = [_rows(TL, D), _rows(TL, D), _rows(TL, D), _fixed(1, D), _fixed(6, D)]
    out_specs = [_rows(TL, D), _fixed(8, D)]
    out_shape = [jax.ShapeDtypeStruct((L, D), F32), jax.ShapeDtypeStruct((8, D), F32)]
    args = [dh, x, dres, gamma, mods]
    if branch:
        in_specs += [_rows(TL, D), _fixed(6, D)]
        out_specs.append(_rows(TL, D))
        out_shape.append(jax.ShapeDtypeStruct((L, D), BF))
        args += [branch[0], branch[1]]
    return pl.pallas_call(
        body, name=name, grid=(L // TL,), in_specs=in_specs, out_specs=out_specs, out_shape=out_shape,
        scratch_shapes=[pltpu.VMEM((nacc, 8, D), F32)], compiler_params=_cp("arbitrary"))(*args)


def gate_bwd(dx, f, mods, k_gate, name):
    L, D = dx.shape
    TL = _tile(L, (512, 256))

    def body(dx_ref, f_ref, m_ref, o_ref, s_ref, acc):
        i = pl.program_id(0)

        @pl.when(i == 0)
        def _():
            acc[...] = jnp.zeros_like(acc)

        dxv = dx_ref[...]
        o_ref[...] = (dxv * m_ref[k_gate:k_gate + 1, :]).astype(o_ref.dtype)
        acc[...] += _rowsum8(dxv * f_ref[...].astype(F32))

        @pl.when(i == pl.num_programs(0) - 1)
        def _():
            s_ref[...] = jnp.zeros_like(s_ref)
            s_ref[0:1, :] = jnp.sum(acc[...], axis=0, keepdims=True)

    return pl.pallas_call(
        body, name=name, grid=(L // TL,), in_specs=[_rows(TL, D), _rows(TL, D), _fixed(6, D)],
        out_specs=[_rows(TL, D), _fixed(8, D)],
        out_shape=[jax.ShapeDtypeStruct((L, D), BF), jax.ShapeDtypeStruct((8, D), F32)],
        scratch_shapes=[pltpu.VMEM((8, D), F32)], compiler_params=_cp("arbitrary"))(dx, f, mods)


def ffn_in_act(a, w, name):
    M, K = a.shape
    S, _, Ns = w.shape
    half = S // 2
    TM = _tile(M, (512, 256))
    TN = _tile(Ns, (1408, 1024, 768, 512, 256, 128))
    nj = Ns // TN

    def body(a_ref, wg_ref, wu_ref, gu_ref, act_ref):
        av = a_ref[...]
        g = jnp.dot(av, wg_ref[...], preferred_element_type=F32)
        u = jnp.dot(av, wu_ref[...], preferred_element_type=F32)
        gu_ref[0] = g.astype(gu_ref.dtype)
        gu_ref[1] = u.astype(gu_ref.dtype)
        act_ref[...] = (g * jax.nn.sigmoid(g) * u).astype(act_ref.dtype)

    return pl.pallas_call(
        body, name=name, grid=(half, nj, M // TM),
        in_specs=[pl.BlockSpec((TM, K), lambda s, j, i: (i, 0)),
                  pl.BlockSpec((None, K, TN), lambda s, j, i: (s, 0, j)),
                  pl.BlockSpec((None, K, TN), lambda s, j, i: (s + half, 0, j))],
        out_specs=[pl.BlockSpec((2, TM, TN), lambda s, j, i: (0, i, s * nj + j)),
                   pl.BlockSpec((TM, TN), lambda s, j, i: (i, s * nj + j))],
        out_shape=[jax.ShapeDtypeStruct((2, M, half * Ns), BF), jax.ShapeDtypeStruct((M, half * Ns), BF)],
        compiler_params=_cp("parallel", "parallel", "parallel"))(a, w, w)


def ffn_out_bwd(dff, w2, gu, name):
    M, D = dff.shape
    F = w2.shape[0]
    TM = _tile(M, (512, 256))
    CW = _tile(F, (256, 128))

    def body(d_ref, w_ref, gu_ref, o_ref):
        dv = d_ref[...]
        for c in range(0, F, CW):
            da = lax.dot_general(dv, w_ref[c:c + CW, :], (((1,), (1,)), ((), ())), preferred_element_type=F32)
            g = gu_ref[0, :, c:c + CW].astype(F32)
            u = gu_ref[1, :, c:c + CW].astype(F32)
            s = jax.nn.sigmoid(g)
            o_ref[0, :, c:c + CW] = (da * u * (s + g * s * (1.0 - s))).astype(o_ref.dtype)
            o_ref[1, :, c:c + CW] = (da * g * s).astype(o_ref.dtype)

    part = pl.BlockSpec((2, TM, F), lambda i: (0, i, 0))
    return pl.pallas_call(
        body, name=name, grid=(M // TM,),
        in_specs=[pl.BlockSpec((TM, D), lambda i: (i, 0)), pl.BlockSpec((F, D), lambda i: (0, 0)), part],
        out_specs=part, out_shape=jax.ShapeDtypeStruct((2, M, F), BF),
        compiler_params=_cp("parallel"))(dff, w2, gu)


def swiglu_act(gu, name):
    L, F2 = gu.shape
    F = F2 // 2
    TL = _tile(L, (256,))

    def body(gu_ref, o_ref):
        g = gu_ref[:, :F].astype(F32)
        u = gu_ref[:, F:].astype(F32)
        o_ref[...] = (g * jax.nn.sigmoid(g) * u).astype(o_ref.dtype)

    return pl.pallas_call(body, name=name, grid=(L // TL,), in_specs=[_rows(TL, F2)], out_specs=_rows(TL, F),
                          out_shape=jax.ShapeDtypeStruct((L, F), BF), compiler_params=_cp("parallel"))(gu)


def swiglu_bwd(da, gu, name):
    L, F2 = gu.shape
    F = F2 // 2
    TL = _tile(L, (256,))

    def body(da_ref, gu_ref, o_ref):
        g = gu_ref[:, :F].astype(F32)
        u = gu_ref[:, F:].astype(F32)
        d = da_ref[...].astype(F32)
        s = jax.nn.sigmoid(g)
        o_ref[:, :F] = (d * u * (s + g * s * (1.0 - s))).astype(o_ref.dtype)
        o_ref[:, F:] = (d * g * s).astype(o_ref.dtype)

    return pl.pallas_call(body, name=name, grid=(L // TL,), in_specs=[_rows(TL, F), _rows(TL, F2)],
                          out_specs=_rows(TL, F2), out_shape=jax.ShapeDtypeStruct((L, F2), BF),
                          compiler_params=_cp("parallel"))(da, gu)


def glu_res(o, x, mods, k_gate, name):
    L, D = x.shape
    TL = _tile(L, (512, 256))

    def body(o_ref, x_ref, m_ref, mix_ref, y_ref):
        mix = o_ref[:, :D].astype(F32) * jax.nn.sigmoid(o_ref[:, D:].astype(F32))
        mix_ref[...] = mix.astype(mix_ref.dtype)
        y_ref[...] = x_ref[...] + m_ref[k_gate:k_gate + 1, :] * mix

    return pl.pallas_call(
        body, name=name, grid=(L // TL,), in_specs=[_rows(TL, 2 * D), _rows(TL, D), _fixed(6, D)],
        out_specs=[_rows(TL, D), _rows(TL, D)],
        out_shape=[jax.ShapeDtypeStruct((L, D), BF), jax.ShapeDtypeStruct((L, D), F32)],
        compiler_params=_cp("parallel"))(o, x, mods)


def ssm_out_glu(z, w, x, mods, k_gate, name):
    M, K = z.shape
    S, _, Ns = w.shape
    half = S // 2
    TM = _tile(M, (1024, 512, 256))
    TN = _tile(Ns, (512, 256, 128))
    nj = Ns // TN

    def body(z_ref, wv_ref, wg_ref, x_ref, m_ref, o_ref, mix_ref, y_ref):
        zv = z_ref[...]
        val = jnp.dot(zv, wv_ref[...], preferred_element_type=F32)
        gate = jnp.dot(zv, wg_ref[...], preferred_element_type=F32)
        o_ref[0] = val.astype(o_ref.dtype)
        o_ref[1] = gate.astype(o_ref.dtype)
        mix = val * jax.nn.sigmoid(gate)
        mix_ref[...] = mix.astype(mix_ref.dtype)
        y_ref[...] = x_ref[...] + m_ref[k_gate:k_gate + 1, :] * mix

    col = lambda s, j, i: (i, s * nj + j)
    return pl.pallas_call(
        body, name=name, grid=(half, nj, M // TM),
        in_specs=[pl.BlockSpec((TM, K), lambda s, j, i: (i, 0)),
                  pl.BlockSpec((None, K, TN), lambda s, j, i: (s, 0, j)),
                  pl.BlockSpec((None, K, TN), lambda s, j, i: (s + half, 0, j)),
                  pl.BlockSpec((TM, TN), col), pl.BlockSpec((6, TN), lambda s, j, i: (0, s * nj + j))],
        out_specs=[pl.BlockSpec((2, TM, TN), lambda s, j, i: (0, i, s * nj + j)), pl.BlockSpec((TM, TN), col),
                   pl.BlockSpec((TM, TN), col)],
        out_shape=[jax.ShapeDtypeStruct((2, M, half * Ns), BF), jax.ShapeDtypeStruct((M, half * Ns), BF),
                   jax.ShapeDtypeStruct((M, half * Ns), F32)],
        compiler_params=_cp("parallel", "parallel", "parallel"))(z, w, w, x, mods)


def glu_bwd(dmix, o, name):
    _, L, D = o.shape
    TL = _tile(L, (512, 256))

    def body(d_ref, o_ref, do_ref):
        d = d_ref[...].astype(F32)
        val = o_ref[0].astype(F32)
        s = jax.nn.sigmoid(o_ref[1].astype(F32))
        do_ref[0] = (d * s).astype(do_ref.dtype)
        do_ref[1] = (d * val * s * (1.0 - s)).astype(do_ref.dtype)

    part = pl.BlockSpec((2, TL, D), lambda i: (0, i, 0))
    return pl.pallas_call(body, name=name, grid=(L // TL,), in_specs=[_rows(TL, D), part],
                          out_specs=part, out_shape=jax.ShapeDtypeStruct((2, L, D), BF),
                          compiler_params=_cp("parallel"))(dmix, o)


def final_loss(x, target, gamma, f, fmods, k_gate, name):
    L, D = x.shape
    TL = _tile(L, (512, 256))

    def body(x_ref, t_ref, g_ref, f_ref, fm_ref, l_ref, dx_ref, s_ref, df_ref, acc, lacc):
        i = pl.program_id(0)

        @pl.when(i == 0)
        def _():
            acc[...] = jnp.zeros_like(acc)
            lacc[...] = jnp.zeros_like(lacc)

        xn, r = _norm_parts(x_ref[...])
        gam = g_ref[...]
        e = xn * gam - t_ref[...]
        lacc[...] += jnp.sum(0.5 * jnp.mean(e * e, axis=-1, keepdims=True), axis=0, keepdims=True)
        dy = e * (1.0 / D)
        dxn = dy * gam
        dx = r * (dxn - xn * jnp.mean(dxn * xn, axis=-1, keepdims=True))
        dx_ref[...] = dx
        df_ref[...] = (dx * fm_ref[k_gate:k_gate + 1, :]).astype(df_ref.dtype)
        acc[0] += _rowsum8(dy * xn)
        acc[1] += _rowsum8(dx * f_ref[...].astype(F32))

        @pl.when(i == pl.num_programs(0) - 1)
        def _():
            s_ref[...] = jnp.zeros_like(s_ref)
            for q in range(2):
                s_ref[q:q + 1, :] = jnp.sum(acc[q], axis=0, keepdims=True)
            l_ref[...] = jnp.broadcast_to(lacc[...], l_ref.shape)

    return pl.pallas_call(
        body, name=name, grid=(L // TL,),
        in_specs=[_rows(TL, D), _rows(TL, D), _fixed(1, D), _rows(TL, D), _fixed(6, D)],
        out_specs=[_fixed(8, 128), _rows(TL, D), _fixed(8, D), _rows(TL, D)],
        out_shape=[jax.ShapeDtypeStruct((8, 128), F32), jax.ShapeDtypeStruct((L, D), F32),
                   jax.ShapeDtypeStruct((8, D), F32), jax.ShapeDtypeStruct((L, D), BF)],
        scratch_shapes=[pltpu.VMEM((2, 8, D), F32), pltpu.VMEM((1, 1), F32)],
        compiler_params=_cp("arbitrary"))(x, target, gamma, f, fmods)


def _col(L, TC, off):
    return pl.BlockSpec((L, TC), lambda j: (0, off + j))


def _shift_down(v, k, row):
    return jnp.where(row >= k, pltpu.roll(v, k, 0), 0.0)


def _shift_up(v, k, row, L):
    return jnp.where(row < L - k, pltpu.roll(v, L - k, 0), 0.0)


def conv_fwd(p, w, name):
    L, D3 = p.shape
    D = D3 // 3
    TC = _tile(D, (128,))
    nc = D // TC

    def body(b_ref, c_ref, v_ref, w_ref, o_ref):
        row = lax.broadcasted_iota(jnp.int32, (L, TC), 0)
        cv = c_ref[...].astype(F32) * v_ref[...].astype(F32)
        conv = w_ref[2:3, :] * cv + w_ref[1:2, :] * _shift_down(cv, 1, row) + w_ref[0:1, :] * _shift_down(cv, 2, row)
        o_ref[...] = (b_ref[...].astype(F32) * conv).astype(o_ref.dtype)

    return pl.pallas_call(
        body, name=name, grid=(nc,),
        in_specs=[_col(L, TC, 0), _col(L, TC, nc), _col(L, TC, 2 * nc), pl.BlockSpec((3, TC), lambda j: (0, j))],
        out_specs=_col(L, TC, 0), out_shape=jax.ShapeDtypeStruct((L, D), BF), compiler_params=_cp("parallel"))(p, p, p, w)


def conv_bwd(dm, p, w, name):
    L, D3 = p.shape
    D = D3 // 3
    TC = _tile(D, (128,))
    nc = D // TC

    def body(dm_ref, b_ref, c_ref, v_ref, w_ref, db_ref, dc_ref, dv_ref, dw_ref):
        row = lax.broadcasted_iota(jnp.int32, (L, TC), 0)
        cg, vv = c_ref[...].astype(F32), v_ref[...].astype(F32)
        cv = cg * vv
        cv1, cv2 = _shift_down(cv, 1, row), _shift_down(cv, 2, row)
        conv = w_ref[2:3, :] * cv + w_ref[1:2, :] * cv1 + w_ref[0:1, :] * cv2
        dmv = dm_ref[...].astype(F32)
        db_ref[...] = (dmv * conv).astype(db_ref.dtype)
        dconv = dmv * b_ref[...].astype(F32)
        dcv = (w_ref[2:3, :] * dconv + w_ref[1:2, :] * _shift_up(dconv, 1, row, L)
               + w_ref[0:1, :] * _shift_up(dconv, 2, row, L))
        dc_ref[...] = (dcv * vv).astype(dc_ref.dtype)
        dv_ref[...] = (dcv * cg).astype(dv_ref.dtype)
        dw_ref[...] = jnp.zeros_like(dw_ref)
        dw_ref[0:1, :] = jnp.sum(dconv * cv2, axis=0, keepdims=True)
        dw_ref[1:2, :] = jnp.sum(dconv * cv1, axis=0, keepdims=True)
        dw_ref[2:3, :] = jnp.sum(dconv * cv, axis=0, keepdims=True)

    one = jax.ShapeDtypeStruct((L, D), BF)
    return pl.pallas_call(
        body, name=name, grid=(nc,),
        in_specs=[_col(L, TC, 0), _col(L, TC, 0), _col(L, TC, nc), _col(L, TC, 2 * nc),
                  pl.BlockSpec((3, TC), lambda j: (0, j))],
        out_specs=[_col(L, TC, 0), _col(L, TC, 0), _col(L, TC, 0), pl.BlockSpec((8, TC), lambda j: (0, j))],
        out_shape=[one, one, one, jax.ShapeDtypeStruct((8, D), F32)],
        compiler_params=_cp("parallel"))(dm, p, p, p, w)


def _gelu(y):
    return 0.5 * y * (1.0 + jnp.tanh(GELU_C * (y + GELU_A * y * y * y)))


def _gelu_grad(y):
    th = jnp.tanh(GELU_C * (y + GELU_A * y * y * y))
    return 0.5 * (1.0 + th) + 0.5 * y * (1.0 - th * th) * GELU_C * (1.0 + 3.0 * GELU_A * y * y)


def _cmul_add(br, bi, ar, ai, sr, si):
    return br + ar * sr - ai * si, bi + ar * si + ai * sr


def _log2(n):
    k = n.bit_length() - 1
    assert 1 << k == n
    return k


def _replicate(P2, W2, P, GLP, transposed):
    shape = (W2, P2) if transposed else (P2, W2)
    k = lax.broadcasted_iota(jnp.int32, shape, 1 if transposed else 0)
    c = lax.broadcasted_iota(jnp.int32, shape, 0 if transposed else 1)
    return ((k >> _log2(P)) == (c >> _log2(GLP))) & ((k & (P - 1)) == (c & (P - 1)))


def _on_diagonal(KB, W2, H, P, GLP, transposed):
    shape = (W2, KB) if transposed else (KB, W2)
    r = lax.broadcasted_iota(jnp.int32, shape, 1 if transposed else 0)
    c = lax.broadcasted_iota(jnp.int32, shape, 0 if transposed else 1)
    return (r >> _log2(H)) == ((c & (GLP - 1)) >> _log2(P))


def _expand(t, dims, transposed):
    KB, W2, H, P, GLP = dims
    rep = _replicate(2 * P, W2, P, GLP, transposed).astype(t.dtype)
    wide = jnp.dot(rep, t, preferred_element_type=F32) if transposed else jnp.dot(t, rep, preferred_element_type=F32)
    return jnp.where(_on_diagonal(KB, W2, H, P, GLP, transposed), wide, 0.0).astype(t.dtype)


def _extract(acc, dims):
    KB, W2, H, P, GLP = dims
    rep = _replicate(2 * P, W2, P, GLP, True).astype(F32)
    kept = jnp.where(_on_diagonal(KB, W2, H, P, GLP, False), acc, 0.0)
    return jnp.dot(kept, rep, preferred_element_type=F32, precision=lax.Precision.HIGHEST)


def _cmul(ar, ai, sr, si):
    return ar * sr - ai * si, ar * si + ai * sr


LANES = 128


def _cols(ref, base, n, rows):
    return jnp.concatenate([ref[base + q, rows, :] for q in range(n)], axis=1)


def _set_cols(ref, base, n, rows, val):
    for q in range(n):
        ref[base + q, rows, :] = val[:, q * LANES:(q + 1) * LANES]


def _strided_s5_fwd(h, tb, tct, pw, dvec, name):
    L, D = h.shape
    nkb, KB, P2 = tb.shape
    P = P2 // 2
    W = (KB // SSM_GROUP) * P
    W2 = 2 * W
    dims = (KB, W2, SSM_GROUP, P, W)
    TL = _tile(L, (512, 256))
    CH = TL // 8
    NC = W // LANES

    def body(h_ref, tb_ref, tct_ref, pw_ref, d_ref, s_ref, y_ref, z_ref, bw, cw, carry):
        t = pl.program_id(1)

        @pl.when(t == 0)
        def _():
            carry[...] = jnp.zeros_like(carry)
            bw[...] = _expand(tb_ref[...], dims, False)
            cw[...] = _expand(tct_ref[...], dims, True)

        hv = h_ref[...]
        _set_cols(s_ref, 0, 2 * NC, slice(None), jnp.dot(hv.astype(BF), bw[...], preferred_element_type=F32))
        ar, ai = pw_ref[0:8, :W], pw_ref[0:8, W:]
        xr = xi = jnp.zeros((8, W), F32)
        for j in range(CH):
            rows = pl.ds(j, 8, stride=CH)
            xr, xi = _cmul_add(_cols(s_ref, 0, NC, rows), _cols(s_ref, NC, NC, rows), ar, ai, xr, xi)
            _set_cols(s_ref, 0, NC, rows, xr)
            _set_cols(s_ref, NC, NC, rows, xi)
        for k, off in ((1, 8), (2, 16), (4, 24)):
            xr, xi = _cmul_add(xr, xi, pw_ref[off:off + 8, :W], pw_ref[off:off + 8, W:],
                               pltpu.roll(xr, k, 0), pltpu.roll(xi, k, 0))
        xr, xi = _cmul_add(xr, xi, pw_ref[32:40, :W], pw_ref[32:40, W:], carry[0], carry[1])
        first = lax.broadcasted_iota(jnp.int32, (8, W), 0) == 0
        cr = jnp.where(first, carry[0], pltpu.roll(xr, 1, 0))
        ci = jnp.where(first, carry[1], pltpu.roll(xi, 1, 0))
        carry[0] = jnp.broadcast_to(xr[7:8], (8, W))
        carry[1] = jnp.broadcast_to(xi[7:8], (8, W))
        for j in range(CH):
            rows = pl.ds(j, 8, stride=CH)
            cr, ci = _cmul(ar, ai, cr, ci)
            _set_cols(s_ref, 0, NC, rows, _cols(s_ref, 0, NC, rows) + cr)
            _set_cols(s_ref, NC, NC, rows, _cols(s_ref, NC, NC, rows) + ci)
        sv = _cols(s_ref, 0, 2 * NC, slice(None))
        y = jnp.dot(sv.astype(BF), cw[...], preferred_element_type=F32) + d_ref[...] * hv
        y_ref[...] = y
        z_ref[...] = _gelu(y).astype(z_ref.dtype)

    blk = lambda kb, t: (t, kb)
    per_kb = lambda kb, t: (kb, 0, 0)
    return pl.pallas_call(
        body, name=name, grid=(nkb, L // TL),
        in_specs=[pl.BlockSpec((TL, KB), blk), pl.BlockSpec((None, KB, P2), per_kb),
                  pl.BlockSpec((None, P2, KB), per_kb), pl.BlockSpec((None, 40, W2), per_kb),
                  pl.BlockSpec((1, KB), lambda kb, t: (0, kb))],
        out_specs=[pl.BlockSpec((2 * NC, TL, LANES), lambda kb, t: (kb, t, 0)), pl.BlockSpec((TL, KB), blk),
                   pl.BlockSpec((TL, KB), blk)],
        out_shape=[jax.ShapeDtypeStruct((nkb * 2 * NC, L, LANES), F32), jax.ShapeDtypeStruct((L, D), F32),
                   jax.ShapeDtypeStruct((L, D), BF)],
        scratch_shapes=[pltpu.VMEM((KB, W2), BF), pltpu.VMEM((W2, KB), BF), pltpu.VMEM((2, 8, W), F32)],
        compiler_params=_cp("parallel", "arbitrary"))(h, tb, tct, pw, dvec)


def _strided_s5_bwd(dz, y, h, s, tc, tbt, pwr, dvec, name):
    L, D = h.shape
    nkb, KB, P2 = tc.shape
    P = P2 // 2
    W = (KB // SSM_GROUP) * P
    W2 = 2 * W
    dims = (KB, W2, SSM_GROUP, P, W)
    TL = _tile(L, (512, 256))
    CH = TL // 8
    NC = W // LANES
    nt = L // TL

    def body(dz_ref, y_ref, h_ref, s_ref, sp_ref, tc_ref, tbt_ref, pw_ref, d_ref,
             dh_ref, dd_ref, da_ref, db_ref, dc_ref, g, ctw, btw, dbacc, dcacc, carry):
        t = pl.program_id(1)

        @pl.when(t == 0)
        def _():
            carry[...] = jnp.zeros_like(carry)
            dd_ref[...] = jnp.zeros_like(dd_ref)
            da_ref[...] = jnp.zeros_like(da_ref)
            dbacc[...] = jnp.zeros_like(dbacc)
            dcacc[...] = jnp.zeros_like(dcacc)
            ctw[...] = _expand(tc_ref[...], dims, False)
            btw[...] = _expand(tbt_ref[...], dims, True)

        hv = h_ref[...]
        dy = dz_ref[...].astype(F32) * _gelu_grad(y_ref[...])
        dd_ref[...] += _rowsum8(dy * hv)
        dyb = dy.astype(BF)
        _set_cols(g, 0, 2 * NC, slice(None), jnp.dot(dyb, ctw[...], preferred_element_type=F32))
        ar, ai = pw_ref[0:8, :W], pw_ref[0:8, W:]
        gr = gi = jnp.zeros((8, W), F32)
        for j in reversed(range(CH)):
            rows = pl.ds(j, 8, stride=CH)
            gr, gi = _cmul_add(_cols(g, 0, NC, rows), _cols(g, NC, NC, rows), ar, ai, gr, gi)
            _set_cols(g, 0, NC, rows, gr)
            _set_cols(g, NC, NC, rows, gi)
        for k, off in ((1, 8), (2, 16), (4, 24)):
            gr, gi = _cmul_add(gr, gi, pw_ref[off:off + 8, :W], pw_ref[off:off + 8, W:],
                               pltpu.roll(gr, 8 - k, 0), pltpu.roll(gi, 8 - k, 0))
        gr, gi = _cmul_add(gr, gi, pw_ref[32:40, :W], pw_ref[32:40, W:], carry[0], carry[1])
        sub = lax.broadcasted_iota(jnp.int32, (8, W), 0)
        cr = jnp.where(sub == 7, carry[0], pltpu.roll(gr, 7, 0))
        ci = jnp.where(sub == 7, carry[1], pltpu.roll(gi, 7, 0))
        carry[0] = jnp.broadcast_to(gr[0:1], (8, W))
        carry[1] = jnp.broadcast_to(gi[0:1], (8, W))
        live = jnp.where(t == nt - 1, 0.0, 1.0)
        accr = acci = jnp.zeros((8, W), F32)
        for j in reversed(range(CH)):
            rows = pl.ds(j, 8, stride=CH)
            cr, ci = _cmul(ar, ai, cr, ci)
            gr, gi = _cols(g, 0, NC, rows) + cr, _cols(g, NC, NC, rows) + ci
            _set_cols(g, 0, NC, rows, gr)
            _set_cols(g, NC, NC, rows, gi)
            if j > 0:
                before = pl.ds(j - 1, 8, stride=CH)
                pr, pi = _cols(s_ref, 0, NC, before), _cols(s_ref, NC, NC, before)
            else:
                last = pl.ds(CH - 1, 8, stride=CH)
                pr = jnp.where(sub == 0, _cols(sp_ref, 0, NC, slice(7, 8)) * live,
                               pltpu.roll(_cols(s_ref, 0, NC, last), 1, 0))
                pi = jnp.where(sub == 0, _cols(sp_ref, NC, NC, slice(7, 8)) * live,
                               pltpu.roll(_cols(s_ref, NC, NC, last), 1, 0))
            accr = accr + pr * gr + pi * gi
            acci = acci + pr * gi - pi * gr
        da_ref[:, :W] += accr
        da_ref[:, W:] += acci

        gb = _cols(g, 0, 2 * NC, slice(None)).astype(BF)
        dh_ref[...] = dy * d_ref[...] + jnp.dot(gb, btw[...], preferred_element_type=F32)
        tn = (((0,), (0,)), ((), ()))
        dbacc[...] += lax.dot_general(hv.astype(BF), gb, tn, preferred_element_type=F32)
        dcacc[...] += lax.dot_general(dyb, _cols(s_ref, 0, 2 * NC, slice(None)).astype(BF), tn,
                                      preferred_element_type=F32)

        @pl.when(t == nt - 1)
        def _():
            db_ref[...] = _extract(dbacc[...], dims)
            dc_ref[...] = _extract(dcacc[...], dims)

    rev = lambda kb, t: (nt - 1 - t, kb)
    per_kb = lambda kb, t: (kb, 0, 0)
    return pl.pallas_call(
        body, name=name, grid=(nkb, nt),
        in_specs=[pl.BlockSpec((TL, KB), rev), pl.BlockSpec((TL, KB), rev), pl.BlockSpec((TL, KB), rev),
                  pl.BlockSpec((2 * NC, TL, LANES), lambda kb, t: (kb, nt - 1 - t, 0)),
                  pl.BlockSpec((2 * NC, 8, LANES), lambda kb, t: (kb, jnp.maximum((nt - 1 - t) * CH - 1, 0), 0)),
                  pl.BlockSpec((None, KB, P2), per_kb), pl.BlockSpec((None, P2, KB), per_kb),
                  pl.BlockSpec((None, 40, W2), per_kb), pl.BlockSpec((1, KB), lambda kb, t: (0, kb))],
        out_specs=[pl.BlockSpec((TL, KB), rev), pl.BlockSpec((8, KB), lambda kb, t: (0, kb)),
                   pl.BlockSpec((None, 8, W2), per_kb), pl.BlockSpec((None, KB, P2), per_kb),
                   pl.BlockSpec((None, KB, P2), per_kb)],
        out_shape=[jax.ShapeDtypeStruct((L, D), F32), jax.ShapeDtypeStruct((8, D), F32),
                   jax.ShapeDtypeStruct((nkb, 8, W2), F32), jax.ShapeDtypeStruct((nkb, KB, P2), F32),
                   jax.ShapeDtypeStruct((nkb, KB, P2), F32)],
        scratch_shapes=[pltpu.VMEM((2 * NC, TL, LANES), F32), pltpu.VMEM((KB, W2), BF), pltpu.VMEM((W2, KB), BF),
                        pltpu.VMEM((KB, W2), F32), pltpu.VMEM((KB, W2), F32), pltpu.VMEM((2, 8, W), F32)],
        compiler_params=_cp("parallel", "arbitrary"))(dz, y, h, s, s, tc, tbt, pwr, dvec)


def _chunk_order(TL, CH, transposed):
    out_row = lax.broadcasted_iota(jnp.int32, (TL, TL), 1 if transposed else 0)
    in_row = lax.broadcasted_iota(jnp.int32, (TL, TL), 0 if transposed else 1)
    return in_row == ((out_row & 7) << _log2(CH)) + (out_row >> 3)


def _reorder(perm, v):
    hi = v.astype(perm.dtype)
    lo = (v - hi.astype(F32)).astype(perm.dtype)
    return jnp.dot(perm, hi, preferred_element_type=F32) + jnp.dot(perm, lo, preferred_element_type=F32)


def _interleave(main, side):
    n, m, k = len(main), len(side), 0
    for i, step in enumerate(main):
        step()
        while k < m and (k + 1) * n <= (i + 1) * m:
            side[k]()
            k += 1
    for step in side[k:]:
        step()


S5_CHUNK = 512


def s5_fwd(h, tb, tct, pw, dvec, name):
    L, D = h.shape
    nkb, KB, P2 = tb.shape
    P = P2 // 2
    W = (KB // SSM_GROUP) * P
    W2 = 2 * W
    dims = (KB, W2, SSM_GROUP, P, W)
    TL = _tile(L, (512, 256))
    CH = TL // 8
    NB = 2 if nkb % 2 == 0 else 1
    CK = min(S5_CHUNK, W2)

    def body(h_ref, tb_ref, tct_ref, pw_ref, d_ref, s_ref, y_ref, z_ref, bw, cw, perm, unperm, carry):
        t = pl.program_id(1)

        @pl.when(t == 0)
        def _():
            carry[...] = jnp.zeros_like(carry)
            for b in range(NB):
                bw[b] = _expand(tb_ref[b], dims, False)
                cw[b] = _expand(tct_ref[b], dims, True)
            perm[...] = _chunk_order(TL, CH, False).astype(perm.dtype)
            unperm[...] = _chunk_order(TL, CH, True).astype(perm.dtype)

        hp = _reorder(perm[...], h_ref[...])
        hpb = hp.astype(BF)
        first = lax.broadcasted_iota(jnp.int32, (8, W), 0) == 0

        def project(b):
            def chunk(c):
                def emit():
                    s_ref[:, b * W2 + c:b * W2 + c + CK] = jnp.dot(hpb[:, b * KB:(b + 1) * KB], bw[b, :, c:c + CK],
                                                                   preferred_element_type=F32)
                return emit
            return [chunk(c) for c in range(0, W2, CK)]

        def scan(b):
            re, im = slice(b * W2, b * W2 + W), slice(b * W2 + W, (b + 1) * W2)
            ar, ai = pw_ref[b, 0:8, :W], pw_ref[b, 0:8, W:]
            st = {"x": (jnp.zeros((8, W), F32), jnp.zeros((8, W), F32))}

            def own(j):
                def emit():
                    rows = slice(j * 8, j * 8 + 8)
                    xr, xi = _cmul_add(s_ref[rows, re], s_ref[rows, im], ar, ai, *st["x"])
                    s_ref[rows, re] = xr
                    s_ref[rows, im] = xi
                    st["x"] = (xr, xi)
                return emit

            def ends():
                xr, xi = st["x"]
                for k, off in ((1, 8), (2, 16), (4, 24)):
                    xr, xi = _cmul_add(xr, xi, pw_ref[b, off:off + 8, :W], pw_ref[b, off:off + 8, W:],
                                       pltpu.roll(xr, k, 0), pltpu.roll(xi, k, 0))
                xr, xi = _cmul_add(xr, xi, pw_ref[b, 32:40, :W], pw_ref[b, 32:40, W:], carry[b, 0], carry[b, 1])
                st["c"] = (jnp.where(first, carry[b, 0], pltpu.roll(xr, 1, 0)),
                           jnp.where(first, carry[b, 1], pltpu.roll(xi, 1, 0)))
                carry[b, 0] = jnp.broadcast_to(xr[7:8], (8, W))
                carry[b, 1] = jnp.broadcast_to(xi[7:8], (8, W))

            def carried(j):
                def emit():
                    rows = slice(j * 8, j * 8 + 8)
                    cr, ci = _cmul(ar, ai, *st["c"])
                    s_ref[rows, re] = s_ref[rows, re] + cr
                    s_ref[rows, im] = s_ref[rows, im] + ci
                    st["c"] = (cr, ci)
                return emit

            return [own(j) for j in range(CH)] + [ends] + [carried(j) for j in range(CH)]

        def readout(b):
            cols = slice(b * KB, (b + 1) * KB)
            acc = {}

            def chunk(c):
                def emit():
                    part = jnp.dot(s_ref[:, b * W2 + c:b * W2 + c + CK].astype(BF), cw[b, c:c + CK, :],
                                   preferred_element_type=F32)
                    acc["y"] = part if c == 0 else acc["y"] + part
                return emit

            def finish():
                y = acc["y"] + d_ref[:, cols] * hp[:, cols]
                y_ref[:, cols] = y
                z_ref[:, cols] = jnp.dot(unperm[...], _gelu(y).astype(BF),
                                         preferred_element_type=F32).astype(z_ref.dtype)

            return [chunk(c) for c in range(0, W2, CK)] + [finish]

        for emit in project(0):
            emit()
        for b in range(NB):
            side = (project(b + 1) if b + 1 < NB else []) + (readout(b - 1) if b > 0 else [])
            _interleave(scan(b), side)
        for emit in readout(NB - 1):
            emit()

    blk = lambda kb, t: (t, kb)
    per_kb = lambda kb, t: (kb, 0, 0)
    return pl.pallas_call(
        body, name=name, grid=(nkb // NB, L // TL),
        in_specs=[pl.BlockSpec((TL, NB * KB), blk), pl.BlockSpec((NB, KB, P2), per_kb),
                  pl.BlockSpec((NB, P2, KB), per_kb), pl.BlockSpec((NB, 40, W2), per_kb),
                  pl.BlockSpec((1, NB * KB), lambda kb, t: (0, kb))],
        out_specs=[pl.BlockSpec((TL, NB * W2), blk), pl.BlockSpec((TL, NB * KB), blk),
                   pl.BlockSpec((TL, NB * KB), blk)],
        out_shape=[jax.ShapeDtypeStruct((L, nkb * W2), F32), jax.ShapeDtypeStruct((L, D), F32),
                   jax.ShapeDtypeStruct((L, D), BF)],
        scratch_shapes=[pltpu.VMEM((NB, KB, W2), BF), pltpu.VMEM((NB, W2, KB), BF), pltpu.VMEM((TL, TL), BF),
                        pltpu.VMEM((TL, TL), BF), pltpu.VMEM((NB, 2, 8, W), F32)],
        compiler_params=_cp("parallel", "arbitrary"))(h, tb, tct, pw, dvec)


def _s5_fwd_one_block(h, tb, tct, pw, dvec, name):
    L, D = h.shape
    nkb, KB, P2 = tb.shape
    P = P2 // 2
    W = (KB // SSM_GROUP) * P
    W2 = 2 * W
    dims = (KB, W2, SSM_GROUP, P, W)
    TL = _tile(L, (512, 256))
    CH = TL // 8

    def body(h_ref, tb_ref, tct_ref, pw_ref, d_ref, s_ref, y_ref, z_ref, bw, cw, perm, unperm, carry):
        t = pl.program_id(1)

        @pl.when(t == 0)
        def _():
            carry[...] = jnp.zeros_like(carry)
            bw[...] = _expand(tb_ref[...], dims, False)
            cw[...] = _expand(tct_ref[...], dims, True)
            perm[...] = _chunk_order(TL, CH, False).astype(perm.dtype)
            unperm[...] = _chunk_order(TL, CH, True).astype(perm.dtype)

        hp = _reorder(perm[...], h_ref[...])
        s_ref[...] = jnp.dot(hp.astype(BF), bw[...], preferred_element_type=F32)
        ar, ai = pw_ref[0:8, :W], pw_ref[0:8, W:]

        def own(j, x):
            rows = pl.ds(pl.multiple_of(j * 8, 8), 8)
            xr, xi = _cmul_add(s_ref[rows, :W], s_ref[rows, W:], ar, ai, x[0], x[1])
            s_ref[rows, :W] = xr
            s_ref[rows, W:] = xi
            return xr, xi

        zero = jnp.zeros((8, W), F32)
        xr, xi = lax.fori_loop(0, CH, own, (zero, zero))
        for k, off in ((1, 8), (2, 16), (4, 24)):
            xr, xi = _cmul_add(xr, xi, pw_ref[off:off + 8, :W], pw_ref[off:off + 8, W:],
                               pltpu.roll(xr, k, 0), pltpu.roll(xi, k, 0))
        xr, xi = _cmul_add(xr, xi, pw_ref[32:40, :W], pw_ref[32:40, W:], carry[0], carry[1])
        first = lax.broadcasted_iota(jnp.int32, (8, W), 0) == 0
        cr = jnp.where(first, carry[0], pltpu.roll(xr, 1, 0))
        ci = jnp.where(first, carry[1], pltpu.roll(xi, 1, 0))
        carry[0] = jnp.broadcast_to(xr[7:8], (8, W))
        carry[1] = jnp.broadcast_to(xi[7:8], (8, W))

        def carried(j, c):
            rows = pl.ds(pl.multiple_of(j * 8, 8), 8)
            cr, ci = _cmul(ar, ai, c[0], c[1])
            s_ref[rows, :W] = s_ref[rows, :W] + cr
            s_ref[rows, W:] = s_ref[rows, W:] + ci
            return cr, ci

        lax.fori_loop(0, CH, carried, (cr, ci))
        y = jnp.dot(s_ref[...].astype(BF), cw[...], preferred_element_type=F32) + d_ref[...] * hp
        y_ref[...] = y
        z_ref[...] = jnp.dot(unperm[...], _gelu(y).astype(BF), preferred_element_type=F32).astype(z_ref.dtype)

    blk = lambda kb, t: (t, kb)
    per_kb = lambda kb, t: (kb, 0, 0)
    return pl.pallas_call(
        body, name=name, grid=(nkb, L // TL),
        in_specs=[pl.BlockSpec((TL, KB), blk), pl.BlockSpec((None, KB, P2), per_kb),
                  pl.BlockSpec((None, P2, KB), per_kb), pl.BlockSpec((None, 40, W2), per_kb),
                  pl.BlockSpec((1, KB), lambda kb, t: (0, kb))],
        out_specs=[pl.BlockSpec((TL, W2), blk), pl.BlockSpec((TL, KB), blk), pl.BlockSpec((TL, KB), blk)],
        out_shape=[jax.ShapeDtypeStruct((L, nkb * W2), F32), jax.ShapeDtypeStruct((L, D), F32),
                   jax.ShapeDtypeStruct((L, D), BF)],
        scratch_shapes=[pltpu.VMEM((KB, W2), BF), pltpu.VMEM((W2, KB), BF), pltpu.VMEM((TL, TL), BF),
                        pltpu.VMEM((TL, TL), BF), pltpu.VMEM((2, 8, W), F32)],
        compiler_params=_cp("parallel", "arbitrary"))(h, tb, tct, pw, dvec)


def s5_bwd(dz, y, h, s, tc, tbt, pwr, dvec, name):
    L, D = h.shape
    nkb, KB, P2 = tc.shape
    P = P2 // 2
    W = (KB // SSM_GROUP) * P
    W2 = 2 * W
    dims = (KB, W2, SSM_GROUP, P, W)
    TL = _tile(L, (512, 256))
    CH = TL // 8
    nt = L // TL
    NB = 2 if nkb % 2 == 0 else 1
    CK = min(S5_CHUNK, W2)
    tn = (((0,), (0,)), ((), ()))

    def body(dz_ref, y_ref, h_ref, s_ref, sp_ref, tc_ref, tbt_ref, pw_ref, d_ref,
             dh_ref, dd_ref, da_ref, db_ref, dc_ref, g, ctw, btw, dbacc, dcacc, dys, perm, unperm, carry):
        t = pl.program_id(1)

        @pl.when(t == 0)
        def _():
            carry[...] = jnp.zeros_like(carry)
            dd_ref[...] = jnp.zeros_like(dd_ref)
            da_ref[...] = jnp.zeros_like(da_ref)
            dbacc[...] = jnp.zeros_like(dbacc)
            dcacc[...] = jnp.zeros_like(dcacc)
            for b in range(NB):
                ctw[b] = _expand(tc_ref[b], dims, False)
                btw[b] = _expand(tbt_ref[b], dims, True)
            perm[...] = _chunk_order(TL, CH, False).astype(perm.dtype)
            unperm[...] = _chunk_order(TL, CH, True).astype(perm.dtype)

        hp = jnp.dot(perm[...], h_ref[...].astype(BF), preferred_element_type=F32)
        dy = jnp.dot(perm[...], dz_ref[...].astype(BF), preferred_element_type=F32) * _gelu_grad(y_ref[...])
        dd_ref[...] += _rowsum8(dy * hp)
        dys[...] = dy
        dyb = dy.astype(BF)
        hpb = hp.astype(BF)
        sub = lax.broadcasted_iota(jnp.int32, (8, W), 0)
        live = jnp.where(t == nt - 1, 0.0, 1.0)

        def lead(b):
            cols = slice(b * KB, (b + 1) * KB)

            def to_states(c):
                def emit():
                    g[b, :, c:c + CK] = jnp.dot(dyb[:, cols], ctw[b, :, c:c + CK], preferred_element_type=F32)
                return emit

            def d_c(c):
                def emit():
                    dcacc[b, :, c:c + CK] += lax.dot_general(dyb[:, cols],
                                                             s_ref[:, b * W2 + c:b * W2 + c + CK].astype(BF), tn,
                                                             preferred_element_type=F32)
                return emit

            return [f(c) for c in range(0, W2, CK) for f in (to_states, d_c)]

        def scan(b):
            re, im = slice(b * W2, b * W2 + W), slice(b * W2 + W, (b + 1) * W2)
            ar, ai = pw_ref[b, 0:8, :W], pw_ref[b, 0:8, W:]
            zero = jnp.zeros((8, W), F32)
            st = {"g": (zero, zero), "acc": (zero, zero)}

            def own(j):
                def emit():
                    rows = slice(j * 8, j * 8 + 8)
                    gr, gi = _cmul_add(g[b, rows, :W], g[b, rows, W:], ar, ai, *st["g"])
                    g[b, rows, :W] = gr
                    g[b, rows, W:] = gi
                    st["g"] = (gr, gi)
                return emit

            def ends():
                gr, gi = st["g"]
                for k, off in ((1, 8), (2, 16), (4, 24)):
                    gr, gi = _cmul_add(gr, gi, pw_ref[b, off:off + 8, :W], pw_ref[b, off:off + 8, W:],
                                       pltpu.roll(gr, 8 - k, 0), pltpu.roll(gi, 8 - k, 0))
                gr, gi = _cmul_add(gr, gi, pw_ref[b, 32:40, :W], pw_ref[b, 32:40, W:], carry[b, 0], carry[b, 1])
                st["c"] = (jnp.where(sub == 7, carry[b, 0], pltpu.roll(gr, 7, 0)),
                           jnp.where(sub == 7, carry[b, 1], pltpu.roll(gi, 7, 0)))
                carry[b, 0] = jnp.broadcast_to(gr[0:1], (8, W))
                carry[b, 1] = jnp.broadcast_to(gi[0:1], (8, W))

            def carried(j):
                def emit():
                    rows = slice(j * 8, j * 8 + 8)
                    cr, ci = _cmul(ar, ai, *st["c"])
                    gr, gi = g[b, rows, :W] + cr, g[b, rows, W:] + ci
                    g[b, rows, :W] = gr
                    g[b, rows, W:] = gi
                    if j > 0:
                        before = slice(j * 8 - 8, j * 8)
                        pr, pi = s_ref[before, re], s_ref[before, im]
                    else:
                        last = slice(TL - 8, TL)
                        pr = jnp.where(sub == 0, sp_ref[7:8, re] * live, pltpu.roll(s_ref[last, re], 1, 0))
                        pi = jnp.where(sub == 0, sp_ref[7:8, im] * live, pltpu.roll(s_ref[last, im], 1, 0))
                    accr, acci = st["acc"]
                    st["c"] = (cr, ci)
                    st["acc"] = (accr + pr * gr + pi * gi, acci + pr * gi - pi * gr)
                return emit

            def done():
                da_ref[b, :, :W] += st["acc"][0]
                da_ref[b, :, W:] += st["acc"][1]

            return ([own(j) for j in reversed(range(CH))] + [ends] + [carried(j) for j in reversed(range(CH))]
                    + [done])

        def tail(b):
            cols = slice(b * KB, (b + 1) * KB)
            acc = {}

            def d_u(c):
                def emit():
                    part = jnp.dot(g[b, :, c:c + CK].astype(BF), btw[b, c:c + CK, :], preferred_element_type=F32)
                    acc["u"] = part if c == 0 else acc["u"] + part
                return emit

            def d_b(c):
                def emit():
                    dbacc[b, :, c:c + CK] += lax.dot_general(hpb[:, cols], g[b, :, c:c + CK].astype(BF), tn,
                                                             preferred_element_type=F32)
                return emit

            def finish():
                dh = (dys[:, cols] * d_ref[:, cols] + acc["u"]).astype(BF)
                dh_ref[:, cols] = jnp.dot(unperm[...], dh, preferred_element_type=F32).astype(dh_ref.dtype)

            return [f(c) for c in range(0, W2, CK) for f in (d_u, d_b)] + [finish]

        for emit in lead(0):
            emit()
        for b in range(NB):
            side = (lead(b + 1) if b + 1 < NB else []) + (tail(b - 1) if b > 0 else [])
            _interleave(scan(b), side)
        for emit in tail(NB - 1):
            emit()

        @pl.when(t == nt - 1)
        def _():
            for b in range(NB):
                db_ref[b] = _extract(dbacc[b], dims)
                dc_ref[b] = _extract(dcacc[b], dims)

    rev = lambda kb, t: (nt - 1 - t, kb)
    prev = lambda kb, t: (jnp.maximum((nt - 1 - t) * CH - 1, 0), kb)
    per_kb = lambda kb, t: (kb, 0, 0)
    return pl.pallas_call(
        body, name=name, grid=(nkb // NB, nt),
        in_specs=[pl.BlockSpec((TL, NB * KB), rev), pl.BlockSpec((TL, NB * KB), rev),
                  pl.BlockSpec((TL, NB * KB), rev), pl.BlockSpec((TL, NB * W2), rev),
                  pl.BlockSpec((8, NB * W2), prev), pl.BlockSpec((NB, KB, P2), per_kb),
                  pl.BlockSpec((NB, P2, KB), per_kb), pl.BlockSpec((NB, 40, W2), per_kb),
                  pl.BlockSpec((1, NB * KB), lambda kb, t: (0, kb))],
        out_specs=[pl.BlockSpec((TL, NB * KB), rev), pl.BlockSpec((8, NB * KB), lambda kb, t: (0, kb)),
                   pl.BlockSpec((NB, 8, W2), per_kb), pl.BlockSpec((NB, KB, P2), per_kb),
                   pl.BlockSpec((NB, KB, P2), per_kb)],
        out_shape=[jax.ShapeDtypeStruct((L, D), BF), jax.ShapeDtypeStruct((8, D), F32),
                   jax.ShapeDtypeStruct((nkb, 8, W2), F32), jax.ShapeDtypeStruct((nkb, KB, P2), F32),
                   jax.ShapeDtypeStruct((nkb, KB, P2), F32)],
        scratch_shapes=[pltpu.VMEM((NB, TL, W2), F32), pltpu.VMEM((NB, KB, W2), BF), pltpu.VMEM((NB, W2, KB), BF),
                        pltpu.VMEM((NB, KB, W2), F32), pltpu.VMEM((NB, KB, W2), F32), pltpu.VMEM((TL, NB * KB), F32),
                        pltpu.VMEM((TL, TL), BF), pltpu.VMEM((TL, TL), BF), pltpu.VMEM((NB, 2, 8, W), F32)],
        compiler_params=pltpu.CompilerParams(dimension_semantics=("parallel", "arbitrary"),
                                             vmem_limit_bytes=V7X_VMEM_BYTES - 4 * 1024 * 1024),
    )(dz, y, h, s, s, tc, tbt, pwr, dvec)


def _s5_bwd_one_block(dz, y, h, s, tc, tbt, pwr, dvec, name):
    L, D = h.shape
    nkb, KB, P2 = tc.shape
    P = P2 // 2
    W = (KB // SSM_GROUP) * P
    W2 = 2 * W
    dims = (KB, W2, SSM_GROUP, P, W)
    TL = _tile(L, (512, 256))
    CH = TL // 8
    nt = L // TL

    def body(dz_ref, y_ref, h_ref, s_ref, sp_ref, tc_ref, tbt_ref, pw_ref, d_ref,
             dh_ref, dd_ref, da_ref, db_ref, dc_ref, g, ctw, btw, dbacc, dcacc, perm, unperm, carry):
        t = pl.program_id(1)

        @pl.when(t == 0)
        def _():
            carry[...] = jnp.zeros_like(carry)
            dd_ref[...] = jnp.zeros_like(dd_ref)
            da_ref[...] = jnp.zeros_like(da_ref)
            dbacc[...] = jnp.zeros_like(dbacc)
            dcacc[...] = jnp.zeros_like(dcacc)
            ctw[...] = _expand(tc_ref[...], dims, False)
            btw[...] = _expand(tbt_ref[...], dims, True)
            perm[...] = _chunk_order(TL, CH, False).astype(perm.dtype)
            unperm[...] = _chunk_order(TL, CH, True).astype(perm.dtype)

        hp = jnp.dot(perm[...], h_ref[...].astype(BF), preferred_element_type=F32)
        dy = jnp.dot(perm[...], dz_ref[...].astype(BF), preferred_element_type=F32) * _gelu_grad(y_ref[...])
        dd_ref[...] += _rowsum8(dy * hp)
        dyb = dy.astype(BF)
        g[...] = jnp.dot(dyb, ctw[...], preferred_element_type=F32)
        ar, ai = pw_ref[0:8, :W], pw_ref[0:8, W:]

        def own(jj, x):
            rows = pl.ds(pl.multiple_of((CH - 1 - jj) * 8, 8), 8)
            gr, gi = _cmul_add(g[rows, :W], g[rows, W:], ar, ai, x[0], x[1])
            g[rows, :W] = gr
            g[rows, W:] = gi
            return gr, gi

        zero = jnp.zeros((8, W), F32)
        gr, gi = lax.fori_loop(0, CH, own, (zero, zero))
        for k, off in ((1, 8), (2, 16), (4, 24)):
            gr, gi = _cmul_add(gr, gi, pw_ref[off:off + 8, :W], pw_ref[off:off + 8, W:],
                               pltpu.roll(gr, 8 - k, 0), pltpu.roll(gi, 8 - k, 0))
        gr, gi = _cmul_add(gr, gi, pw_ref[32:40, :W], pw_ref[32:40, W:], carry[0], carry[1])
        sub = lax.broadcasted_iota(jnp.int32, (8, W), 0)
        cr = jnp.where(sub == 7, carry[0], pltpu.roll(gr, 7, 0))
        ci = jnp.where(sub == 7, carry[1], pltpu.roll(gi, 7, 0))
        carry[0] = jnp.broadcast_to(gr[0:1], (8, W))
        carry[1] = jnp.broadcast_to(gi[0:1], (8, W))

        def carried(jj, c):
            j = CH - 1 - jj
            rows = pl.ds(pl.multiple_of(j * 8, 8), 8)
            before = pl.ds(pl.multiple_of(j * 8 - 8, 8), 8)
            cr, ci = _cmul(ar, ai, c[0], c[1])
            gr, gi = g[rows, :W] + cr, g[rows, W:] + ci
            g[rows, :W] = gr
            g[rows, W:] = gi
            pr, pi = s_ref[before, :W], s_ref[before, W:]
            return cr, ci, c[2] + pr * gr + pi * gi, c[3] + pr * gi - pi * gr

        cr, ci, accr, acci = lax.fori_loop(0, CH - 1, carried, (cr, ci, zero, zero))
        live = jnp.where(t == nt - 1, 0.0, 1.0)
        cr, ci = _cmul(ar, ai, cr, ci)
        gr, gi = g[0:8, :W] + cr, g[0:8, W:] + ci
        g[0:8, :W] = gr
        g[0:8, W:] = gi
        pr = jnp.where(sub == 0, sp_ref[7:8, :W] * live, pltpu.roll(s_ref[TL - 8:TL, :W], 1, 0))
        pi = jnp.where(sub == 0, sp_ref[7:8, W:] * live, pltpu.roll(s_ref[TL - 8:TL, W:], 1, 0))
        da_ref[:, :W] += accr + pr * gr + pi * gi
        da_ref[:, W:] += acci + pr * gi - pi * gr

        gb = g[...].astype(BF)
        dh = dy * d_ref[...] + jnp.dot(gb, btw[...], preferred_element_type=F32)
        dh_ref[...] = _reorder(unperm[...], dh)
        tn = (((0,), (0,)), ((), ()))
        dbacc[...] += lax.dot_general(hp.astype(BF), gb, tn, preferred_element_type=F32)
        dcacc[...] += lax.dot_general(dyb, s_ref[...].astype(BF), tn, preferred_element_type=F32)

        @pl.when(t == nt - 1)
        def _():
            db_ref[...] = _extract(dbacc[...], dims)
            dc_ref[...] = _extract(dcacc[...], dims)

    rev = lambda kb, t: (nt - 1 - t, kb)
    prev = lambda kb, t: (jnp.maximum((nt - 1 - t) * CH - 1, 0), kb)
    per_kb = lambda kb, t: (kb, 0, 0)
    return pl.pallas_call(
        body, name=name, grid=(nkb, nt),
        in_specs=[pl.BlockSpec((TL, KB), rev), pl.BlockSpec((TL, KB), rev), pl.BlockSpec((TL, KB), rev),
                  pl.BlockSpec((TL, W2), rev), pl.BlockSpec((8, W2), prev),
                  pl.BlockSpec((None, KB, P2), per_kb), pl.BlockSpec((None, P2, KB), per_kb),
                  pl.BlockSpec((None, 40, W2), per_kb), pl.BlockSpec((1, KB), lambda kb, t: (0, kb))],
        out_specs=[pl.BlockSpec((TL, KB), rev), pl.BlockSpec((8, KB), lambda kb, t: (0, kb)),
                   pl.BlockSpec((None, 8, W2), per_kb), pl.BlockSpec((None, KB, P2), per_kb),
                   pl.BlockSpec((None, KB, P2), per_kb)],
        out_shape=[jax.ShapeDtypeStruct((L, D), F32), jax.ShapeDtypeStruct((8, D), F32),
                   jax.ShapeDtypeStruct((nkb, 8, W2), F32), jax.ShapeDtypeStruct((nkb, KB, P2), F32),
                   jax.ShapeDtypeStruct((nkb, KB, P2), F32)],
        scratch_shapes=[pltpu.VMEM((TL, W2), F32), pltpu.VMEM((KB, W2), BF), pltpu.VMEM((W2, KB), BF),
                        pltpu.VMEM((KB, W2), F32), pltpu.VMEM((KB, W2), F32), pltpu.VMEM((TL, TL), BF),
                        pltpu.VMEM((TL, TL), BF), pltpu.VMEM((2, 8, W), F32)],
        compiler_params=_cp("parallel", "arbitrary"))(dz, y, h, s, s, tc, tbt, pwr, dvec)


def _discretise(a_re, a_im, log_step, b_re, b_im):
    lr = jnp.minimum(a_re, -1e-4)
    li = a_im
    dt = jnp.exp(log_step)[:, None]
    mag = jnp.exp(lr * dt)
    abr = mag * jnp.cos(li * dt)
    abi = mag * jnp.sin(li * dt)
    den = lr * lr + li * li
    qr = ((abr - 1.0) * lr + abi * li) / den
    qi = (abi * lr - (abr - 1.0) * li) / den
    bbar_re = qr[..., None] * b_re - qi[..., None] * b_im
    bbar_im = qr[..., None] * b_im + qi[..., None] * b_re
    return abr, abi, bbar_re, bbar_im


def _compact(m_re, m_im, nkb):
    G, H, P = m_re.shape
    t = jnp.stack([m_re, m_im], axis=2).reshape(nkb, (G // nkb) * H, 2 * P).astype(BF)
    return t, jnp.swapaxes(t, 1, 2)


def _scan_powers(abr, abi, nkb, conj, CH):
    G, P = abr.shape
    if conj:
        abi = -abi

    def cmul(u, v):
        return u[0] * v[0] - u[1] * v[1], u[0] * v[1] + u[1] * v[0]

    q = (abr, abi)
    for _ in range(_log2(CH)):
        q = cmul(q, q)
    pows = [q]
    for _ in range(7):
        pows.append(cmul(pows[-1], q))
    row = jnp.arange(8)[:, None, None]

    def table(part):
        out = [jnp.broadcast_to((abr, abi)[part][None], (8, G, P))]
        for k in (1, 2, 4):
            keep = (row <= 7 - k) if conj else (row >= k)
            out.append(jnp.where(keep, pows[k - 1][part][None], 0.0))
        ends = jnp.stack([p[part] for p in pows])
        out.append(ends[::-1] if conj else ends)
        return jnp.concatenate(out, axis=0)

    GL = G // nkb
    t = jnp.stack([table(0), table(1)], axis=1)
    t = t.reshape(40, 2, nkb, GL * P).transpose(2, 0, 1, 3)
    return t.reshape(nkb, 40, 2 * GL * P)


def ada_mods(c_all, w_ada, b_sh, name):
    nl, D, NA = w_ada.shape

    def body(c_ref, w_ref, b_ref, o_ref):
        cv = c_ref[...]
        act = cv * jax.nn.sigmoid(cv)
        o_ref[...] = jnp.dot(act, w_ref[...], preferred_element_type=F32, precision=lax.Precision.HIGHEST) + b_ref[...]

    return pl.pallas_call(
        body, name=name, grid=(nl,),
        in_specs=[pl.BlockSpec((8, D), lambda i: (0, 0)), pl.BlockSpec((None, D, NA), lambda i: (i, 0, 0)),
                  pl.BlockSpec((None, 1, NA), lambda i: (i, 0, 0))],
        out_specs=pl.BlockSpec((None, 8, NA), lambda i: (i, 0, 0)),
        out_shape=jax.ShapeDtypeStruct((nl, 8, NA), F32), compiler_params=_cp("parallel"))(c_all, w_ada, b_sh)


def _adamw(w, g, m, v):
    m = ADAM_B1 * m + (1.0 - ADAM_B1) * g
    v = ADAM_B2 * v + (1.0 - ADAM_B2) * (g * g)
    m_hat = m / (1.0 - ADAM_B1 ** ADAM_STEP)
    v_hat = v / (1.0 - ADAM_B2 ** ADAM_STEP)
    return -ADAM_LR * (m_hat / (jnp.sqrt(v_hat) + ADAM_EPS) + ADAM_WD * w), m, v


def _adam_rows(R, C):
    cap = max(8, (256 * 1024) // C)
    for t in range(min(R, cap), 0, -1):
        if R % t == 0 and (t % 8 == 0 or t == R):
            return t
    return R


def adamw_ada(c_t, dm, w, m, v, name):
    nl, D, NA = w.shape
    TK = _tile(D, (256, 128))

    def body(c_ref, dm_ref, w_ref, m_ref, v_ref, g_ref, d_ref, nm_ref, nv_ref):
        cv = c_ref[...]
        act = cv * jax.nn.sigmoid(cv)
        g = jnp.dot(act, dm_ref[...], preferred_element_type=F32, precision=lax.Precision.HIGHEST)
        g_ref[...] = g
        d_ref[...], nm_ref[...], nv_ref[...] = _adamw(w_ref[...], g, m_ref[...], v_ref[...])

    big = pl.BlockSpec((None, TK, NA), lambda i, k: (i, k, 0))
    shape = jax.ShapeDtypeStruct(w.shape, F32)
    return pl.pallas_call(
        body, name=name, grid=(nl, D // TK),
        in_specs=[pl.BlockSpec((TK, 8), lambda i, k: (k, 0)), pl.BlockSpec((None, 8, NA), lambda i, k: (i, 0, 0)),
                  big, big, big],
        out_specs=[big] * 4, out_shape=[shape] * 4, compiler_params=_cp("parallel", "parallel"))(c_t, dm, w, m, v)


def adamw_sharded(w, m, v, ga, gb, name):
    nl, R, C = w.shape
    TR = _adam_rows(R, C)

    def body(w_ref, m_ref, v_ref, a_ref, b_ref, g_ref, d_ref, nm_ref, nv_ref):
        g = a_ref[...] + b_ref[...]
        g_ref[...] = g
        d_ref[...], nm_ref[...], nv_ref[...] = _adamw(w_ref[...], g, m_ref[...], v_ref[...])

    big = pl.BlockSpec((None, TR, C), lambda i, r: (i, r, 0))
    shape = jax.ShapeDtypeStruct(w.shape, F32)
    return pl.pallas_call(
        body, name=name, grid=(nl, R // TR), in_specs=[big] * 5,
        out_specs=[big] * 4, out_shape=[shape] * 4, compiler_params=_cp("parallel", "parallel"))(w, m, v, ga, gb)


def adamw_slab(g, w, m, v, name):
    R, C = g.shape
    TR = _tile(R, (160, 80, 40, 8))

    def body(g_ref, w_ref, m_ref, v_ref, d_ref, nm_ref, nv_ref):
        d_ref[...], nm_ref[...], nv_ref[...] = _adamw(w_ref[...], g_ref[...], m_ref[...], v_ref[...])

    big = pl.BlockSpec((TR, C), lambda r: (r, 0))
    shape = jax.ShapeDtypeStruct((R, C), F32)
    return pl.pallas_call(
        body, name=name, grid=(R // TR,), in_specs=[big] * 4,
        out_specs=[big] * 3, out_shape=[shape] * 3, compiler_params=_cp("parallel"))(g, w, m, v)


def adamw_plain(w, m, v, g, name):
    def body(w_ref, m_ref, v_ref, g_ref, d_ref, nm_ref, nv_ref):
        d_ref[...], nm_ref[...], nv_ref[...] = _adamw(w_ref[...], g_ref[...], m_ref[...], v_ref[...])

    shape = jax.ShapeDtypeStruct(w.shape, F32)
    return pl.pallas_call(body, name=name, out_shape=[shape] * 3,
                          compiler_params=pltpu.CompilerParams(vmem_limit_bytes=VMEM_LIMIT))(w, m, v, g)


def _slab_rows(a):
    n = a.size
    rows = -(-n // SLAB_W)
    return -(-rows // 8) * 8


def _pack(arrs, pad_rows_to=0):
    out = []
    for a in arrs:
        rows = _slab_rows(a)
        flat = a.reshape(-1).astype(F32)
        flat = jnp.pad(flat, (0, rows * SLAB_W - flat.shape[0]))
        out.append(flat.reshape(rows, SLAB_W))
    total = sum(o.shape[0] for o in out)
    if pad_rows_to and total % pad_rows_to:
        out.append(jnp.zeros((pad_rows_to - total % pad_rows_to, SLAB_W), F32))
    return jnp.concatenate(out, axis=0)


def _unpack(slab, like):
    out, r = [], 0
    for a in like:
        rows = _slab_rows(a)
        out.append(slab[r:r + rows].reshape(-1)[:a.size].reshape(a.shape))
        r += rows
    return out


WEIGHTS = ['norm1_g', 'norm2_g', 'w_ada', 'b_ada', 'ssm_a_re', 'ssm_a_im', 'ssm_log_step', 'ssm_b_re', 'ssm_b_im',
           'ssm_c_re', 'ssm_c_im', 'ssm_d', 'ssm_w_out', 'conv_w_in', 'conv_w', 'conv_w_out', 'w_ffn_in',
           'w_ffn_out', 'final_g']
SLAB = ['norm1_g', 'norm2_g', 'b_ada', 'ssm_a_re', 'ssm_a_im', 'ssm_log_step', 'ssm_b_re', 'ssm_b_im', 'ssm_c_re',
        'ssm_c_im', 'ssm_d', 'final_g']
SHARDED = ['ssm_w_out', 'conv_w_in', 'conv_w_out', 'w_ffn_in', 'w_ffn_out']


def kernel(x, c, norm1_g, norm2_g, w_ada, b_ada, ssm_a_re, ssm_a_im, ssm_log_step, ssm_b_re, ssm_b_im, ssm_c_re, ssm_c_im, ssm_d, ssm_w_out, conv_w_in, conv_w, conv_w_out, w_ffn_in, w_ffn_out, final_g, loss_target, m_norm1_g, m_norm2_g, m_w_ada, m_b_ada, m_ssm_a_re, m_ssm_a_im, m_ssm_log_step, m_ssm_b_re, m_ssm_b_im, m_ssm_c_re, m_ssm_c_im, m_ssm_d, m_ssm_w_out, m_conv_w_in, m_conv_w, m_conv_w_out, m_w_ffn_in, m_w_ffn_out, m_final_g, v_norm1_g, v_norm2_g, v_w_ada, v_b_ada, v_ssm_a_re, v_ssm_a_im, v_ssm_log_step, v_ssm_b_re, v_ssm_b_im, v_ssm_c_re, v_ssm_c_im, v_ssm_d, v_ssm_w_out, v_conv_w_in, v_conv_w, v_conv_w_out, v_w_ffn_in, v_w_ffn_out, v_final_g):
    given = dict(locals())
    W = {n: given[n] for n in WEIGHTS}
    Mo = {n: given["m_" + n] for n in WEIGHTS}
    Vo = {n: given["v_" + n] for n in WEIGHTS}

    xs = x[0]
    tgt = loss_target[0]
    L, D = xs.shape
    nlayer = norm1_g.shape[0]
    NA = w_ada.shape[2]
    G = ssm_a_re.shape[1]
    nkb = D // S5_BLOCK
    ax, ay, ac = _axes()
    me = 4 * ax + 2 * ay + ac
    chip = 2 * ax + ay

    c_all = gather8(jnp.broadcast_to(c, (8, D)), "gather_c")[:, 0, :]
    b_sh = lax.dynamic_slice_in_dim(b_ada, chip * NA, NA, axis=1)[:, None, :]
    mods_part = ada_mods(c_all, w_ada, b_sh, "ada_mods")
    mg = gather8(mods_part.reshape(nlayer * 8, NA), "gather_mods")
    mg = mg.reshape(N_CHIP, 2, nlayer, 8, NA)[:, 0]
    mods_all = lax.dynamic_index_in_dim(mg, me, axis=2, keepdims=False)
    mods_all = jnp.transpose(mods_all, (1, 0, 2)).reshape(nlayer, 6, D)

    cw_parts = gather8(_pack([conv_w]), "gather_conv_w")
    nconv = conv_w.shape[0]
    cw_full = jnp.stack([_unpack(cw_parts[2 * q], [conv_w])[0] for q in range(N_CHIP)], axis=2)
    cw_full = cw_full.reshape(nconv, 3, D)

    in_flight_w = {}

    def start_weights(i, after):
        names = (["ssm_w_out"] if i % 2 == 0 else ["conv_w_in", "conv_w_out"]) + ["w_ffn_in", "w_ffn_out"]
        shards = [W[n][i if n.startswith("w_ffn") else i // 2].astype(BF) for n in names]
        sems, srcs, lands, tok = gather_start(shards, after, "gather_start%d" % i)
        in_flight_w[i] = (names, sems, srcs, lands)
        return tok

    def relay_weights(i, after):
        names, sems, srcs, lands = in_flight_w[i]
        got = gather_wait(sems, srcs, lands, list(range(len(names))), after, "gather_wait%d" % i)
        rsems, rlands, tok = relay_start(got, after, "relay_start%d" % i)
        in_flight_w[i] = (names, rsems, rlands)
        return tok

    def layer_weights(i, after):
        names, rsems, rlands = in_flight_w[i]
        return dict(zip(names, relay_wait(rsems, rlands, after, "relay_wait%d" % i)))

    token = start_weights(0, cw_full + mods_all[0, 0:3])
    mods_all = mods_all + token[0:1, 0:1]

    s5 = []
    for j in range(ssm_a_re.shape[0]):
        disc, disc_vjp = jax.vjp(_discretise, ssm_a_re[j], ssm_a_im[j], ssm_log_step[j], ssm_b_re[j], ssm_b_im[j])
        abr, abi, bbar_re, bbar_im = disc
        tb, tbt = _compact(jnp.swapaxes(bbar_re, 1, 2), jnp.swapaxes(bbar_im, 1, 2), nkb)
        tc, tct = _compact(ssm_c_re[j], -ssm_c_im[j], nkb)
        chunk = _tile(L, (512, 256)) // 8
        s5.append(dict(vjp=disc_vjp, tb=tb, tbt=tbt, tc=tc, tct=tct, pw=_scan_powers(abr, abi, nkb, False, chunk),
                       pwr=_scan_powers(abr, abi, nkb, True, chunk)))

    saved = []
    xcur = xs
    for i in range(nlayer):
        j = i // 2
        mods = mods_all[i]
        sv = dict(x=xcur)
        if i % 2 == 0:
            h = norm_mod(xcur, norm1_g[i:i + 1], mods, 0, F32, "norm_mod_s5")
            dvec = ssm_d[j:j + 1]
            if i == 0:
                dvec = dvec + start_weights(1, h)[0:1, 0:1]
            states, yv, z = s5_fwd(h, s5[j]["tb"], s5[j]["tct"], s5[j]["pw"], dvec, "s5_fwd")
            if i == 0:
                mods = mods + relay_weights(0, z)[0:1, 0:1]
            full = layer_weights(i, z)
            o, mix, x2 = ssm_out_glu(z, full["ssm_w_out"], xcur, mods, 2, "ssm_out_glu")
            sv.update(h=h, states=states, y=yv, z=z, o=o)
        else:
            h = norm_mod(xcur, norm1_g[i:i + 1], mods, 0, BF, "norm_mod")
            full = layer_weights(i, h)
            p = mm_nn(h, full["conv_w_in"], BF, "mm_conv_in")
            mc = conv_fwd(p, cw_full[j], "conv_fwd")
            mix, x2 = mm_nn(mc, full["conv_w_out"].reshape(1, D, D), BF, "mm_conv_out", res=xcur, gate=mods[2:3])
            sv.update(h=h, p=p, mc=mc)
        h2 = norm_mod(x2, norm2_g[i:i + 1], mods, 3, BF, "norm_mod")
        gu, act = ffn_in_act(h2, full["w_ffn_in"], "ffn_in_act")
        if i + 1 < nlayer:
            token = relay_weights(i + 1, act)
            if i + 2 < nlayer:
                token = token + start_weights(i + 2, token)
            mods = mods + token[0:1, 0:1]
        F = act.shape[1]
        ff, x3 = mm_nn(act, full["w_ffn_out"].reshape(1, F, D), BF, "mm_ffn_out", res=x2, gate=mods[5:6])
        sv.update(mix=mix, x2=x2, h2=h2, gu=gu, act=act, ff=ff, w=full)
        saved.append(sv)
        xcur = x3

    loss_blk, dx, dfinal, dff = final_loss(xcur, tgt, final_g[None, :], saved[-1]["ff"], mods_all[nlayer - 1], 5,
                                           "final_loss")
    dg2 = dfinal[1:2]

    gland = {n: lax.empty((W[n].shape[0], N_CHIP) + W[n].shape[1:], BF) for n in SHARDED}
    in_flight = []
    dmods = [None] * nlayer
    dnorm1, dnorm2 = [None] * nlayer, [None] * nlayer
    dconv_w = [None] * nconv
    ds5 = [None] * ssm_a_re.shape[0]
    token = jnp.zeros((8, 128), F32)

    def send_grads(names, grads, slot, after, name):
        sems, thru, lands, tok = scatter_start([grads[n] for n in names], [gland[n] for n in names], slot, after, name)
        gland.update(zip(names, lands))
        in_flight.append((names, slot, sems, thru, name))
        return tok

    def land_grads(group, after):
        for names, slot, sems, thru, name in in_flight:
            if names[0] in group:
                got = scatter_wait(sems, thru, [gland[n] for n in names], slot, after, name.replace("scatter", "landed"))
                gland.update(zip(names, got))

    for i in reversed(range(nlayer)):
        j = i // 2
        mods = mods_all[i] + token[0:1, 0:1]
        sv = saved[i]
        full = sv["w"]
        gfull = {}
        F = sv["act"].shape[1]
        gfull["w_ffn_out"] = mm_tn(sv["act"], dff, 1, "mm_tn_ffn_out").reshape(N_CHIP, F // N_CHIP, D)
        dgu = ffn_out_bwd(dff, full["w_ffn_out"].reshape(F, D), sv["gu"], "ffn_out_bwd")
        gfull["w_ffn_in"] = mm_tn(sv["h2"], dgu, N_CHIP, "mm_tn_ffn_in")
        dh2 = mm_nt(dgu, full["w_ffn_in"], BF, "mm_nt_ffn_in")
        token = send_grads(["w_ffn_out", "w_ffn_in"], gfull, [i, i], dh2, "scatter_ffn%d" % i)
        mods = mods + token[0:1, 0:1]
        dx2, s2, dmix = norm_bwd(dh2, sv["x2"], dx, norm2_g[i:i + 1], mods, 3, "norm_bwd_mix",
                                 branch=(sv["mix"], mods, 2))
        dg1 = s2[3:4]
        if i % 2 == 0:
            do = glu_bwd(dmix, sv["o"], "glu_bwd")
            gfull["ssm_w_out"] = mm_tn(sv["z"], do, N_CHIP, "mm_tn_ssm_out")
            dz = mm_nt(do, full["ssm_w_out"], BF, "mm_nt_ssm_out")
            dh, dd, dab, db, dc = s5_bwd(dz, sv["y"], sv["h"], sv["states"], s5[j]["tc"], s5[j]["tbt"], s5[j]["pwr"],
                                         ssm_d[j:j + 1], "s5_bwd")
            ds5[j] = (dd, dab, db, dc)
        else:
            gfull["conv_w_out"] = mm_tn(sv["mc"], dmix, 1, "mm_tn_conv_out").reshape(N_CHIP, D // N_CHIP, D)
            dmc = mm_nt(dmix, full["conv_w_out"].reshape(1, D, D), BF, "mm_nt_conv_out")
            dbg, dcg, dvv, dcw = conv_bwd(dmc, sv["p"], cw_full[j], "conv_bwd")
            dp = jnp.concatenate([dbg, dcg, dvv], axis=1)
            gfull["conv_w_in"] = mm_tn(sv["h"], dp, N_CHIP, "mm_tn_conv_in")
            dh = mm_nt(dp, full["conv_w_in"], BF, "mm_nt_conv_in")
            dconv_w[j] = dcw[0:3]
        dmods_i = [s2[0:2], dg2]
        if i > 0:
            dx, s1, dff = norm_bwd(dh, sv["x"], dx2, norm1_g[i:i + 1], mods, 0, "norm_bwd_ffn",
                                   branch=(saved[i - 1]["ff"], mods_all[i - 1], 5))
            dg2 = s1[3:4]
        else:
            dx, s1 = norm_bwd(dh, sv["x"], dx2, norm1_g[i:i + 1], mods, 0, "norm_bwd")
        dmods[i] = jnp.concatenate([s1[0:2], dg1] + dmods_i, axis=0).reshape(6 * D)
        dnorm1[i], dnorm2[i] = s1[2], s2[2]
        names = ["ssm_w_out"] if i % 2 == 0 else ["conv_w_out", "conv_w_in"]
        token = send_grads(names, gfull, [j] * len(names), dx, "scatter_mix%d" % i)

    small = dict(norm1_g=jnp.stack(dnorm1), norm2_g=jnp.stack(dnorm2), b_ada=jnp.stack(dmods), final_g=dfinal[0])
    per = {n: [] for n in ('ssm_a_re', 'ssm_a_im', 'ssm_log_step', 'ssm_b_re', 'ssm_b_im', 'ssm_c_re', 'ssm_c_im', 'ssm_d')}
    GL = G // nkb
    for j, (dd, dab, db, dc) in enumerate(ds5):
        dab = jnp.sum(dab, axis=1).reshape(nkb, 2, GL, SSM_STATE)
        g_abr, g_abi = dab[:, 0].reshape(G, SSM_STATE), dab[:, 1].reshape(G, SSM_STATE)
        db, dc = db.reshape(G, SSM_GROUP, 2, SSM_STATE), dc.reshape(G, SSM_GROUP, 2, SSM_STATE)
        gb_re, gb_im, gc_re, gc_im = db[:, :, 0], db[:, :, 1], dc[:, :, 0], dc[:, :, 1]
        ga_re, ga_im, gls, gbr, gbi = s5[j]["vjp"]((g_abr, g_abi, jnp.swapaxes(gb_re, 1, 2), jnp.swapaxes(gb_im, 1, 2)))
        for n, val in zip(per, (ga_re, ga_im, gls, gbr, gbi, gc_re, -gc_im, jnp.sum(dd, axis=0))):
            per[n].append(val)
    small.update({n: jnp.stack(vals) for n, vals in per.items()})
    dcw_full = jnp.stack(dconv_w)

    slab_like = [W[n] for n in SLAB] + [dcw_full]
    rows64 = 8 * N_DEV
    slab = _pack([small[n] for n in SLAB] + [dcw_full], rows64)
    per_dev = slab.shape[0] // N_DEV
    x_sems, x_srcs, x_lands, token = exchange_start(
        [(slab.reshape(N_DEV, per_dev, SLAB_W), True), (_pack([small["b_ada"]]), False)], dx, "small_scatter")

    early = [n for n in SHARDED if n != "ssm_w_out"]
    land_grads(early, token)
    mine = [reduce4(gland[n], "reduce4_" + n) for n in early]

    parts, dm_all = exchange_wait(x_sems, x_srcs, x_lands, [True, False], mine[-1][0, :8, :128], "small_landed")
    t_sems, t_srcs, t_lands, token = exchange_start([(sum8(parts, "sum_small"), False)], dm_all, "small_gather")
    out = {}

    w_sems, w_srcs, w_lands, token2 = swap_start(mine, "swap_start")
    dm_all = dm_all.reshape(N_DEV, -1)[:, :b_ada.size].reshape(N_DEV, nlayer, N_CHIP, NA)
    dm_sh = jnp.transpose(lax.dynamic_index_in_dim(dm_all, chip, axis=2, keepdims=False), (1, 0, 2))
    res = adamw_ada(jnp.transpose(c_all) + token[0:1, 0:1] + token2[0:1, 0:1], dm_sh, w_ada, m_w_ada, v_w_ada,
                    "adamw_ada")
    out["g", "w_ada"], out["d", "w_ada"], out["m", "w_ada"], out["v", "w_ada"] = res

    g_slab = exchange_wait(t_sems, t_srcs, t_lands, [False], out["g", "w_ada"], "small_total")[0]
    g_slab = g_slab.reshape(slab.shape)
    d_slab, m_slab, v_slab = adamw_slab(
        g_slab, _pack([W[n] for n in SLAB] + [jnp.zeros_like(dcw_full)], rows64),
        _pack([Mo[n] for n in SLAB] + [jnp.zeros_like(dcw_full)], rows64),
        _pack([Vo[n] for n in SLAB] + [jnp.ones_like(dcw_full)], rows64), "adamw_slab")
    for k, slab in zip(("g", "d", "m", "v"), (g_slab, d_slab, m_slab, v_slab)):
        for n, val in zip(SLAB, _unpack(slab, slab_like)):
            out[k, n] = val
    g_cw = lax.dynamic_slice_in_dim(_unpack(g_slab, slab_like)[-1], chip * conv_w.shape[2], conv_w.shape[2], axis=2)
    out["g", "conv_w"] = g_cw
    out["d", "conv_w"], out["m", "conv_w"], out["v", "conv_w"] = [
        r.reshape(conv_w.shape) for r in adamw_plain(conv_w.reshape(-1, conv_w.shape[2]), m_conv_w.reshape(-1, conv_w.shape[2]),
                                                     v_conv_w.reshape(-1, conv_w.shape[2]), g_cw.reshape(-1, conv_w.shape[2]),
                                                     "adamw_conv_w")]

    mine, theirs = swap_wait(w_sems, w_srcs, w_lands, d_slab, "swap_wait")
    for n, ga, gb in zip(early, mine, theirs):
        r = adamw_sharded(W[n], Mo[n], Vo[n], ga, gb, "adamw_" + n)
        out["g", n], out["d", n], out["m", n], out["v", n] = r

    land_grads(["ssm_w_out"], out["g", "w_ffn_out"])
    ga = reduce4(gland["ssm_w_out"], "reduce4_ssm_w_out")
    gb = swap_siblings([ga], "swap_siblings")[0]
    r = adamw_sharded(ssm_w_out, m_ssm_w_out, v_ssm_w_out, ga, gb, "adamw_ssm_w_out")
    out["g", "ssm_w_out"], out["d", "ssm_w_out"], out["m", "ssm_w_out"], out["v", "ssm_w_out"] = r

    loss = lax.psum(loss_blk[0, 0], ("x", "y", "c"))
    return (loss, dx[None], *[out["g", n] for n in WEIGHTS], *[out["d", n] for n in WEIGHTS],
            *[out["m", n] for n in WEIGHTS], *[out["v", n] for n in WEIGHTS])
```

```python
import functools
import math

import jax
import jax.numpy as jnp
from jax import lax
from jax.experimental import pallas as pl
from jax.experimental.pallas import tpu as pltpu

F32 = jnp.float32
BF = jnp.bfloat16
MESH = pl.DeviceIdType.MESH
ANY = pl.BlockSpec(memory_space=pl.ANY)

N_DEV = 8
N_CHIP = 4
DEPTH = 4
SSM_GROUP = 16
SSM_STATE = 64
S5_BLOCK = 256
RMS_EPS = 1e-6
ADAM_LR, ADAM_B1, ADAM_B2, ADAM_EPS, ADAM_WD, ADAM_STEP = 0.001, 0.9, 0.999, 1e-08, 0.01, 10
V7X_VMEM_BYTES = 64 * 1024 * 1024
VMEM_LIMIT = V7X_VMEM_BYTES - 12 * 1024 * 1024
SLAB_W = 1024
GELU_C = math.sqrt(2.0 / math.pi)
GELU_A = 0.044715


def _cp(*sem):
    return pltpu.CompilerParams(dimension_semantics=sem if sem else None, vmem_limit_bytes=VMEM_LIMIT)


def _tile(n, prefs):
    for p in prefs:
        if p <= n and n % p == 0:
            return p
    return n


def _axes():
    return lax.axis_index("x"), lax.axis_index("y"), lax.axis_index("c")


def _flip(v, k):
    return 1 - v if k else v


def gather8(v, name):
    R, C = v.shape

    def body(v_ref, o_ref, ssem, rsem, lsem):
        x, y, c = _axes()
        me = 4 * x + 2 * y + c
        loc = pltpu.make_async_copy(v_ref, o_ref.at[me], lsem)
        loc.start()
        copies = []
        for k in range(1, N_DEV):
            peer = (_flip(x, (k >> 2) & 1), _flip(y, (k >> 1) & 1), _flip(c, k & 1))
            cp = pltpu.make_async_remote_copy(src_ref=v_ref, dst_ref=o_ref.at[me], send_sem=ssem.at[k - 1],
                                              recv_sem=rsem.at[k - 1], device_id=peer, device_id_type=MESH)
            cp.start()
            copies.append(cp)
        for cp in copies:
            cp.wait()
        loc.wait()

    return pl.pallas_call(
        body, name=name,
        out_shape=jax.ShapeDtypeStruct((N_DEV, R, C), v.dtype),
        in_specs=[pl.BlockSpec(memory_space=pltpu.VMEM)],
        out_specs=pl.BlockSpec(memory_space=pltpu.VMEM),
        scratch_shapes=[pltpu.SemaphoreType.DMA((N_DEV - 1,)), pltpu.SemaphoreType.DMA((N_DEV - 1,)),
                        pltpu.SemaphoreType.DMA],
        compiler_params=pltpu.CompilerParams(vmem_limit_bytes=VMEM_LIMIT),
    )(v)


HBM = pl.BlockSpec(memory_space=pltpu.HBM)
SEM = pl.BlockSpec(memory_space=pltpu.SEMAPHORE)
EFFECT = pltpu.SideEffectType.DATAFLOW_SIDE_EFFECTING


def _in_hbm(a):
    return pltpu.with_memory_space_constraint(a, pltpu.HBM)


def _chip_peers(x, y, c):
    out = []
    for k in range(1, N_CHIP):
        px, py = _flip(x, k >> 1), _flip(y, k & 1)
        out.append(((px, py, c), 2 * px + py))
    return out


def _my_half(ref, c):
    rows = ref.shape[0] // 2
    return pl.ds(pl.multiple_of(c * rows, 16), rows)


def relay_start(lands, after, name):
    n = len(lands)

    def body(*refs):
        land = refs[:n]
        ssem, rsem = refs[n + 1:n + 3]
        token = refs[-1]
        x, y, c = _axes()
        for a in range(n):
            half = _my_half(land[a].at[0], c)
            for k, (_, pchip) in enumerate(_chip_peers(x, y, c)):
                pltpu.make_async_remote_copy(src_ref=land[a].at[pchip, half], dst_ref=land[a].at[pchip, half],
                                             send_sem=ssem.at[3 * a + k], recv_sem=rsem.at[3 * a + k],
                                             device_id=(x, y, 1 - c), device_id_type=MESH).start()
        token[...] = jnp.zeros_like(token)

    out_shape = ([pltpu.SemaphoreType.DMA((3 * n,)), pltpu.SemaphoreType.DMA((3 * n,))]
                 + [pltpu.HBM(l.shape, l.dtype) for l in lands] + [jax.ShapeDtypeStruct((8, 128), F32)])
    res = pl.pallas_call(
        body, name=name, out_shape=out_shape, in_specs=[HBM] * n + [ANY],
        out_specs=[SEM, SEM] + [HBM] * n + [pl.BlockSpec(memory_space=pltpu.VMEM)],
        input_output_aliases={a: 2 + a for a in range(n)},
        compiler_params=pltpu.CompilerParams(has_side_effects=EFFECT),
    )(*lands, after)
    return tuple(res[:2]), list(res[2:2 + n]), res[-1]


def relay_wait(sems, lands, after, name):
    n = len(lands)

    def body(*refs):
        land = refs[:n]
        ssem, rsem = refs[n:n + 2]
        x, y, c = _axes()
        for a in range(n):
            mine, theirs = _my_half(land[a].at[0], c), _my_half(land[a].at[0], 1 - c)
            for k, (_, pchip) in enumerate(_chip_peers(x, y, c)):
                cp = pltpu.make_async_remote_copy(src_ref=land[a].at[pchip, mine], dst_ref=land[a].at[pchip, theirs],
                                                  send_sem=ssem.at[3 * a + k], recv_sem=rsem.at[3 * a + k],
                                                  device_id=(x, y, 1 - c), device_id_type=MESH)
                cp.wait_send()
                cp.wait_recv()

    res = pl.pallas_call(
        body, name=name, out_shape=[pltpu.HBM(l.shape, l.dtype) for l in lands],
        in_specs=[HBM] * n + [SEM, SEM, ANY], out_specs=[HBM] * n,
        input_output_aliases={a: a for a in range(n)},
        compiler_params=pltpu.CompilerParams(has_side_effects=EFFECT),
    )(*lands, *sems, after)
    return list(res)


def gather_start(shards, after, name):
    n = len(shards)

    def body(*refs):
        src, land = refs[:n], refs[n:2 * n]
        ssem, rsem, lsem = refs[2 * n + 1:2 * n + 4]
        token = refs[-1]
        x, y, c = _axes()
        chip = 2 * x + y
        for a in range(n):
            pltpu.make_async_copy(src[a], land[a].at[chip], lsem.at[a]).start()
            half = _my_half(src[a], c)
            for k, (peer, _) in enumerate(_chip_peers(x, y, c)):
                pltpu.make_async_remote_copy(src_ref=src[a].at[half], dst_ref=land[a].at[chip, half],
                                             send_sem=ssem.at[3 * a + k], recv_sem=rsem.at[3 * a + k],
                                             device_id=peer, device_id_type=MESH).start()
        token[...] = jnp.zeros_like(token)

    lands = [lax.empty((N_CHIP,) + s.shape, s.dtype) for s in shards]
    out_shape = ([pltpu.SemaphoreType.DMA((3 * n,)), pltpu.SemaphoreType.DMA((3 * n,)), pltpu.SemaphoreType.DMA((n,))]
                 + [pltpu.HBM(s.shape, s.dtype) for s in shards] + [pltpu.HBM(l.shape, l.dtype) for l in lands]
                 + [jax.ShapeDtypeStruct((8, 128), F32)])
    res = pl.pallas_call(
        body, name=name, out_shape=out_shape, in_specs=[HBM] * (2 * n) + [ANY],
        out_specs=[SEM, SEM, SEM] + [HBM] * (2 * n) + [pl.BlockSpec(memory_space=pltpu.VMEM)],
        input_output_aliases={a: 3 + a for a in range(2 * n)},
        compiler_params=pltpu.CompilerParams(has_side_effects=EFFECT),
    )(*[_in_hbm(s) for s in shards], *[_in_hbm(l) for l in lands], after)
    return tuple(res[:3]), list(res[3:3 + n]), list(res[3 + n:3 + 2 * n]), res[-1]


def gather_wait(sems, srcs, lands, idx, after, name):
    m = len(idx)

    def body(*refs):
        src, land = refs[:m], refs[m:2 * m]
        ssem, rsem, lsem = refs[2 * m:2 * m + 3]
        x, y, c = _axes()
        chip = 2 * x + y
        for j, a in enumerate(idx):
            half = _my_half(src[j], c)
            for k, (peer, pchip) in enumerate(_chip_peers(x, y, c)):
                cp = pltpu.make_async_remote_copy(src_ref=src[j].at[half], dst_ref=land[j].at[pchip, half],
                                                  send_sem=ssem.at[3 * a + k], recv_sem=rsem.at[3 * a + k],
                                                  device_id=peer, device_id_type=MESH)
                cp.wait_send()
                cp.wait_recv()
            pltpu.make_async_copy(src[j], land[j].at[chip], lsem.at[a]).wait()

    s_in = [srcs[a] for a in idx]
    l_in = [lands[a] for a in idx]
    res = pl.pallas_call(
        body, name=name,
        out_shape=[pltpu.HBM(s.shape, s.dtype) for s in s_in] + [pltpu.HBM(l.shape, l.dtype) for l in l_in],
        in_specs=[HBM] * (2 * m) + [SEM, SEM, SEM, ANY], out_specs=[HBM] * (2 * m),
        input_output_aliases={a: a for a in range(2 * m)},
        compiler_params=pltpu.CompilerParams(has_side_effects=EFFECT),
    )(*s_in, *l_in, *sems, after)
    return list(res[m:])


def scatter_start(grads, lands, slot, after, name):
    n = len(grads)

    def body(*refs):
        src, land = refs[:n], refs[n:2 * n]
        ssem, rsem, lsem = refs[2 * n + 1:2 * n + 4]
        token = refs[-1]
        x, y, c = _axes()
        chip = 2 * x + y
        for a in range(n):
            pltpu.make_async_copy(src[a].at[chip], land[a].at[slot[a], chip], lsem.at[a]).start()
            for k, (peer, pchip) in enumerate(_chip_peers(x, y, c)):
                pltpu.make_async_remote_copy(src_ref=src[a].at[pchip], dst_ref=land[a].at[slot[a], chip],
                                             send_sem=ssem.at[3 * a + k], recv_sem=rsem.at[3 * a + k],
                                             device_id=peer, device_id_type=MESH).start()
        token[...] = jnp.zeros_like(token)

    out_shape = ([pltpu.SemaphoreType.DMA((3 * n,)), pltpu.SemaphoreType.DMA((3 * n,)), pltpu.SemaphoreType.DMA((n,))]
                 + [pltpu.HBM(g.shape, g.dtype) for g in grads] + [pltpu.HBM(l.shape, l.dtype) for l in lands]
                 + [jax.ShapeDtypeStruct((8, 128), F32)])
    res = pl.pallas_call(
        body, name=name, out_shape=out_shape, in_specs=[HBM] * (2 * n) + [ANY],
        out_specs=[SEM, SEM, SEM] + [HBM] * (2 * n) + [pl.BlockSpec(memory_space=pltpu.VMEM)],
        input_output_aliases={a: 3 + a for a in range(2 * n)},
        compiler_params=pltpu.CompilerParams(has_side_effects=EFFECT),
    )(*[_in_hbm(g) for g in grads], *[_in_hbm(l) for l in lands], after)
    return tuple(res[:3]), list(res[3:3 + n]), list(res[3 + n:3 + 2 * n]), res[-1]


def scatter_wait(sems, grads, lands, slot, after, name):
    n = len(grads)

    def body(*refs):
        src, land = refs[:n], refs[n:2 * n]
        ssem, rsem, lsem = refs[2 * n:2 * n + 3]
        x, y, c = _axes()
        chip = 2 * x + y
        for a in range(n):
            for k, (peer, pchip) in enumerate(_chip_peers(x, y, c)):
                cp = pltpu.make_async_remote_copy(src_ref=src[a].at[pchip], dst_ref=land[a].at[slot[a], pchip],
                                                  send_sem=ssem.at[3 * a + k], recv_sem=rsem.at[3 * a + k],
                                                  device_id=peer, device_id_type=MESH)
                cp.wait_send()
                cp.wait_recv()
            pltpu.make_async_copy(src[a].at[chip], land[a].at[slot[a], chip], lsem.at[a]).wait()

    res = pl.pallas_call(
        body, name=name,
        out_shape=[pltpu.HBM(g.shape, g.dtype) for g in grads] + [pltpu.HBM(l.shape, l.dtype) for l in lands],
        in_specs=[HBM] * (2 * n) + [SEM, SEM, SEM, ANY], out_specs=[HBM] * (2 * n),
        input_output_aliases={a: a for a in range(2 * n)},
        compiler_params=pltpu.CompilerParams(has_side_effects=EFFECT),
    )(*grads, *lands, *sems, after)
    return list(res[n:])


def reduce4(land, name):
    nl, _, R, C = land.shape
    TR = _adam_rows(R, C)

    def body(l_ref, o_ref):
        o_ref[...] = ((l_ref[0].astype(F32) + l_ref[1].astype(F32)) + l_ref[2].astype(F32)) + l_ref[3].astype(F32)

    return pl.pallas_call(
        body, name=name, grid=(nl, R // TR),
        in_specs=[pl.BlockSpec((None, N_CHIP, TR, C), lambda i, r: (i, 0, r, 0))],
        out_specs=pl.BlockSpec((None, TR, C), lambda i, r: (i, r, 0)),
        out_shape=jax.ShapeDtypeStruct((nl, R, C), F32), compiler_params=_cp("parallel", "parallel"))(land)


def swap_siblings(arrs, name):
    n = len(arrs)

    def body(*refs):
        src, dst = refs[:n], refs[n:2 * n]
        ssem, rsem = refs[2 * n:]
        x, y, c = _axes()
        cps = [pltpu.make_async_remote_copy(src_ref=src[a], dst_ref=dst[a], send_sem=ssem.at[a], recv_sem=rsem.at[a],
                                            device_id=(x, y, 1 - c), device_id_type=MESH) for a in range(n)]
        for cp in cps:
            cp.start()
        for cp in cps:
            cp.wait()

    return pl.pallas_call(
        body, name=name, out_shape=[jax.ShapeDtypeStruct(a.shape, a.dtype) for a in arrs],
        in_specs=[ANY] * n, out_specs=[ANY] * n,
        scratch_shapes=[pltpu.SemaphoreType.DMA((n,)), pltpu.SemaphoreType.DMA((n,))],
        compiler_params=pltpu.CompilerParams(vmem_limit_bytes=VMEM_LIMIT),
    )(*arrs)


def swap_start(arrs, name):
    n = len(arrs)

    def body(*refs):
        src, land = refs[:n], refs[n:2 * n]
        ssem, rsem = refs[2 * n:2 * n + 2]
        token = refs[-1]
        x, y, c = _axes()
        for a in range(n):
            pltpu.make_async_remote_copy(src_ref=src[a], dst_ref=land[a], send_sem=ssem.at[a], recv_sem=rsem.at[a],
                                         device_id=(x, y, 1 - c), device_id_type=MESH).start()
        token[...] = jnp.zeros_like(token)

    lands = [lax.empty(a.shape, a.dtype) for a in arrs]
    out_shape = ([pltpu.SemaphoreType.DMA((n,)), pltpu.SemaphoreType.DMA((n,))]
                 + [pltpu.HBM(a.shape, a.dtype) for a in arrs] * 2 + [jax.ShapeDtypeStruct((8, 128), F32)])
    res = pl.pallas_call(
        body, name=name, out_shape=out_shape, in_specs=[HBM] * (2 * n),
        out_specs=[SEM, SEM] + [HBM] * (2 * n) + [pl.BlockSpec(memory_space=pltpu.VMEM)],
        input_output_aliases={a: 2 + a for a in range(2 * n)},
        compiler_params=pltpu.CompilerParams(has_side_effects=EFFECT),
    )(*[_in_hbm(a) for a in arrs], *[_in_hbm(l) for l in lands])
    return tuple(res[:2]), list(res[2:2 + n]), list(res[2 + n:2 + 2 * n]), res[-1]


def swap_wait(sems, srcs, lands, after, name):
    n = len(srcs)

    def body(*refs):
        src, land = refs[:n], refs[n:2 * n]
        ssem, rsem = refs[2 * n:2 * n + 2]
        x, y, c = _axes()
        for a in range(n):
            cp = pltpu.make_async_remote_copy(src_ref=src[a], dst_ref=land[a], send_sem=ssem.at[a],
                                              recv_sem=rsem.at[a], device_id=(x, y, 1 - c), device_id_type=MESH)
            cp.wait_send()
            cp.wait_recv()

    res = pl.pallas_call(
        body, name=name, out_shape=[pltpu.HBM(a.shape, a.dtype) for a in srcs] * 2,
        in_specs=[HBM] * (2 * n) + [SEM, SEM, ANY], out_specs=[HBM] * (2 * n),
        input_output_aliases={a: a for a in range(2 * n)},
        compiler_params=pltpu.CompilerParams(has_side_effects=EFFECT),
    )(*srcs, *lands, *sems, after)
    return list(res[:n]), list(res[n:])


def _all_peers(x, y, c):
    out = []
    for k in range(1, N_DEV):
        px, py, pc = _flip(x, (k >> 2) & 1), _flip(y, (k >> 1) & 1), _flip(c, k & 1)
        out.append(((px, py, pc), 4 * px + 2 * py + pc))
    return out


def exchange_start(items, after, name):
    n = len(items)

    def body(*refs):
        src, land = refs[:n], refs[n:2 * n]
        ssem, rsem, lsem = refs[2 * n + 1:2 * n + 4]
        token = refs[-1]
        x, y, c = _axes()
        me = 4 * x + 2 * y + c
        for a, (_, scatter) in enumerate(items):
            pltpu.make_async_copy(src[a].at[me] if scatter else src[a], land[a].at[me], lsem.at[a]).start()
            for k, (peer, p) in enumerate(_all_peers(x, y, c)):
                pltpu.make_async_remote_copy(src_ref=src[a].at[p] if scatter else src[a], dst_ref=land[a].at[me],
                                             send_sem=ssem.at[7 * a + k], recv_sem=rsem.at[7 * a + k],
                                             device_id=peer, device_id_type=MESH).start()
        token[...] = jnp.zeros_like(token)

    srcs = [s for s, _ in items]
    lands = [lax.empty(s.shape if sc else (N_DEV,) + s.shape, s.dtype) for s, sc in items]
    out_shape = ([pltpu.SemaphoreType.DMA((7 * n,)), pltpu.SemaphoreType.DMA((7 * n,)), pltpu.SemaphoreType.DMA((n,))]
                 + [pltpu.HBM(s.shape, s.dtype) for s in srcs] + [pltpu.HBM(l.shape, l.dtype) for l in lands]
                 + [jax.ShapeDtypeStruct((8, 128), F32)])
    res = pl.pallas_call(
        body, name=name, out_shape=out_shape, in_specs=[HBM] * (2 * n) + [ANY],
        out_specs=[SEM, SEM, SEM] + [HBM] * (2 * n) + [pl.BlockSpec(memory_space=pltpu.VMEM)],
        input_output_aliases={a: 3 + a for a in range(2 * n)},
        compiler_params=pltpu.CompilerParams(has_side_effects=EFFECT),
    )(*[_in_hbm(s) for s in srcs], *[_in_hbm(l) for l in lands], after)
    return tuple(res[:3]), list(res[3:3 + n]), list(res[3 + n:3 + 2 * n]), res[-1]


def exchange_wait(sems, srcs, lands, scatter, after, name):
    n = len(srcs)

    def body(*refs):
        src, land = refs[:n], refs[n:2 * n]
        ssem, rsem, lsem = refs[2 * n:2 * n + 3]
        x, y, c = _axes()
        me = 4 * x + 2 * y + c
        for a in range(n):
            for k, (peer, p) in enumerate(_all_peers(x, y, c)):
                cp = pltpu.make_async_remote_copy(src_ref=src[a].at[p] if scatter[a] else src[a],
                                                  dst_ref=land[a].at[p], send_sem=ssem.at[7 * a + k],
                                                  recv_sem=rsem.at[7 * a + k], device_id=peer, device_id_type=MESH)
                cp.wait_send()
                cp.wait_recv()
            pltpu.make_async_copy(src[a].at[me] if scatter[a] else src[a], land[a].at[me], lsem.at[a]).wait()

    res = pl.pallas_call(
        body, name=name,
        out_shape=[pltpu.HBM(s.shape, s.dtype) for s in srcs] + [pltpu.HBM(l.shape, l.dtype) for l in lands],
        in_specs=[HBM] * (2 * n) + [SEM, SEM, SEM, ANY], out_specs=[HBM] * (2 * n),
        input_output_aliases={a: a for a in range(2 * n)},
        compiler_params=pltpu.CompilerParams(has_side_effects=EFFECT),
    )(*srcs, *lands, *sems, after)
    return list(res[n:])


def sum8(parts, name):
    _, P, C = parts.shape

    def body(p_ref, o_ref):
        tot = p_ref[0]
        for d in range(1, N_DEV):
            tot = tot + p_ref[d]
        o_ref[...] = tot

    return pl.pallas_call(body, name=name, out_shape=jax.ShapeDtypeStruct((P, C), F32),
                          compiler_params=pltpu.CompilerParams(vmem_limit_bytes=VMEM_LIMIT))(parts)


def reduce8(slab, dm, name):
    RT, C = slab.shape
    P = RT // N_DEV
    R = dm.shape[0]

    def body(s_ref, dm_ref, o_ref, dmo_ref, recv, s1, r1, s2, r2, s3, r3):
        x, y, c = _axes()
        me = 4 * x + 2 * y + c
        mine = pl.ds(pl.multiple_of(me * P, 8), P)
        parts, dms = [], []
        for k in range(1, N_DEV):
            px, py, pc = _flip(x, (k >> 2) & 1), _flip(y, (k >> 1) & 1), _flip(c, k & 1)
            theirs = pl.ds(pl.multiple_of((4 * px + 2 * py + pc) * P, 8), P)
            cp = pltpu.make_async_remote_copy(src_ref=s_ref.at[theirs], dst_ref=recv.at[me], send_sem=s1.at[k - 1],
                                              recv_sem=r1.at[k - 1], device_id=(px, py, pc), device_id_type=MESH)
            cp.start()
            parts.append(cp)
            cd = pltpu.make_async_remote_copy(src_ref=dm_ref, dst_ref=dmo_ref.at[me], send_sem=s3.at[k - 1],
                                              recv_sem=r3.at[k - 1], device_id=(px, py, pc), device_id_type=MESH)
            cd.start()
            dms.append(cd)
        dmo_ref[me] = dm_ref[...]
        recv[me] = s_ref[mine, :]
        for cp in parts:
            cp.wait()
        tot = recv[0]
        for d in range(1, N_DEV):
            tot = tot + recv[d]
        o_ref[mine, :] = tot
        out = []
        for k in range(1, N_DEV):
            peer = (_flip(x, (k >> 2) & 1), _flip(y, (k >> 1) & 1), _flip(c, k & 1))
            cp = pltpu.make_async_remote_copy(src_ref=o_ref.at[mine], dst_ref=o_ref.at[mine], send_sem=s2.at[k - 1],
                                              recv_sem=r2.at[k - 1], device_id=peer, device_id_type=MESH)
            cp.start()
            out.append(cp)
        for cp in out + dms:
            cp.wait()

    sems = [pltpu.SemaphoreType.DMA((N_DEV - 1,))] * 6
    return pl.pallas_call(
        body, name=name,
        out_shape=[jax.ShapeDtypeStruct((RT, C), F32), jax.ShapeDtypeStruct((N_DEV, R, C), F32)],
        in_specs=[pl.BlockSpec(memory_space=pltpu.VMEM)] * 2, out_specs=[pl.BlockSpec(memory_space=pltpu.VMEM)] * 2,
        scratch_shapes=[pltpu.VMEM((N_DEV, P, C), F32)] + sems,
        compiler_params=pltpu.CompilerParams(vmem_limit_bytes=VMEM_LIMIT),
    )(slab, dm)


def mm_nn(a, w, out_dtype, name, res=None, gate=None):
    M, K = a.shape
    S, _, Ns = w.shape
    TM = _tile(M, (1024, 512, 256) if K <= 1024 else (512, 256))
    TN = _tile(Ns, (1408, 1024, 768, 512, 256, 128))
    nj = Ns // TN
    fused = res is not None

    def body(*refs):
        if fused:
            a_ref, w_ref, r_ref, g_ref, f_ref, o_ref = refs
        else:
            a_ref, w_ref, f_ref = refs
        f = jnp.dot(a_ref[...], w_ref[...], preferred_element_type=F32)
        f_ref[...] = f.astype(f_ref.dtype)
        if fused:
            o_ref[...] = r_ref[...] + g_ref[...] * f

    col = lambda s, j, i: (i, s * nj + j)
    in_specs = [pl.BlockSpec((TM, K), lambda s, j, i: (i, 0)), pl.BlockSpec((None, K, TN), lambda s, j, i: (s, 0, j))]
    out_specs = [pl.BlockSpec((TM, TN), col)]
    out_shape = [jax.ShapeDtypeStruct((M, S * Ns), out_dtype)]
    args = [a, w]
    if fused:
        in_specs += [pl.BlockSpec((TM, TN), col), pl.BlockSpec((1, TN), lambda s, j, i: (0, s * nj + j))]
        out_specs.append(pl.BlockSpec((TM, TN), col))
        out_shape.append(jax.ShapeDtypeStruct((M, S * Ns), F32))
        args += [res, gate]
    out = pl.pallas_call(body, name=name, grid=(S, nj, M // TM), in_specs=in_specs, out_specs=out_specs,
                         out_shape=out_shape, compiler_params=_cp("parallel", "parallel", "parallel"))(*args)
    return tuple(out) if fused else out[0]


def mm_nt(g, w, out_dtype, name):
    g3 = g if g.ndim == 3 else g[None]
    Q, M, F = g3.shape
    S, K, Ns = w.shape
    TM = _tile(M, (1024, 512, 256) if K <= 1024 else (512, 256))
    TN = _tile(Ns, (1408, 1024, 768, 512, 256, 128))
    nj = Ns // TN
    nred = S * nj
    per_part = F // TN

    def body(g_ref, w_ref, o_ref, acc):
        n = pl.program_id(1)

        @pl.when(n == 0)
        def _():
            acc[...] = jnp.zeros_like(acc)

        acc[...] += lax.dot_general(g_ref[...], w_ref[...], (((1,), (1,)), ((), ())), preferred_element_type=F32)

        @pl.when(n == nred - 1)
        def _():
            o_ref[...] = acc[...].astype(o_ref.dtype)

    return pl.pallas_call(
        body, name=name, grid=(M // TM, nred),
        in_specs=[pl.BlockSpec((None, TM, TN), lambda i, n: (n // per_part, i, n % per_part)),
                  pl.BlockSpec((None, K, TN), lambda i, n: (n // nj, 0, n % nj))],
        out_specs=pl.BlockSpec((TM, K), lambda i, n: (i, 0)),
        out_shape=jax.ShapeDtypeStruct((M, K), out_dtype),
        scratch_shapes=[pltpu.VMEM((TM, K), F32)],
        compiler_params=_cp("parallel", "arbitrary"))(g3, w)


def mm_nt_norm_bwd(g, w, x, dres, gamma, mods, k_shift, branch, name):
    g3 = g if g.ndim == 3 else g[None]
    Q, M, F = g3.shape
    S, K, Ns = w.shape
    TM = _tile(M, (512, 256))
    TN = _tile(Ns, (1408, 1024, 768, 512, 256, 128))
    nj = Ns // TN
    nred = S * nj
    per_part = F // TN
    nm = M // TM
    f, fmods, k_gate = branch

    def body(g_ref, w_ref, x_ref, dr_ref, gam_ref, m_ref, f_ref, fm_ref, dx_ref, s_ref, df_ref, acc, sums):
        i, n = pl.program_id(0), pl.program_id(1)

        @pl.when(n == 0)
        def _():
            acc[...] = jnp.zeros_like(acc)

        @pl.when((n == 0) & (i == 0))
        def _():
            sums[...] = jnp.zeros_like(sums)

        acc[...] += lax.dot_general(g_ref[...], w_ref[...], (((1,), (1,)), ((), ())), preferred_element_type=F32)

        @pl.when(n == nred - 1)
        def _():
            xn, r = _norm_parts(x_ref[...])
            dh_v = acc[...]
            gam = gam_ref[...]
            sc = m_ref[k_shift + 1:k_shift + 2, :]
            dn = dh_v * (1.0 + sc)
            dxn = dn * gam
            dx = dr_ref[...] + r * (dxn - xn * jnp.mean(dxn * xn, axis=-1, keepdims=True))
            dx_ref[...] = dx
            df_ref[...] = (dx * fm_ref[k_gate:k_gate + 1, :]).astype(df_ref.dtype)
            sums[0] += _rowsum8(dh_v)
            sums[1] += _rowsum8(dh_v * (xn * gam))
            sums[2] += _rowsum8(dn * xn)
            sums[3] += _rowsum8(dx * f_ref[...].astype(F32))

        @pl.when((n == nred - 1) & (i == nm - 1))
        def _():
            s_ref[...] = jnp.zeros_like(s_ref)
            for q in range(4):
                s_ref[q:q + 1, :] = jnp.sum(sums[q], axis=0, keepdims=True)

    rows = pl.BlockSpec((TM, K), lambda i, n: (i, 0))
    fixed = lambda r: pl.BlockSpec((r, K), lambda i, n: (0, 0))
    return pl.pallas_call(
        body, name=name, grid=(nm, nred),
        in_specs=[pl.BlockSpec((None, TM, TN), lambda i, n: (n // per_part, i, n % per_part)),
                  pl.BlockSpec((None, K, TN), lambda i, n: (n // nj, 0, n % nj)),
                  rows, rows, fixed(1), fixed(6), rows, fixed(6)],
        out_specs=[rows, fixed(8), rows],
        out_shape=[jax.ShapeDtypeStruct((M, K), F32), jax.ShapeDtypeStruct((8, K), F32),
                   jax.ShapeDtypeStruct((M, K), BF)],
        scratch_shapes=[pltpu.VMEM((TM, K), F32), pltpu.VMEM((4, 8, K), F32)],
        compiler_params=_cp("arbitrary", "arbitrary"))(g3, w, x, dres, gamma, mods, f, fmods)


def mm_tn(a, g, S, name):
    M, K = a.shape
    g3 = g if g.ndim == 3 else g[None]
    Q, _, F = g3.shape
    Ns = Q * F // S
    TK = _tile(K, (256, 128))
    TN = _tile(Ns, (1408, 1024, 768, 512, 256, 128))
    nj = Ns // TN
    per_part = F // TN

    def body(a_ref, g_ref, o_ref):
        o_ref[...] = lax.dot_general(a_ref[...], g_ref[...], (((0,), (0,)), ((), ())),
                                     preferred_element_type=F32).astype(o_ref.dtype)

    return pl.pallas_call(
        body, name=name, grid=(S * nj, K // TK),
        in_specs=[pl.BlockSpec((M, TK), lambda n, k: (0, k)),
                  pl.BlockSpec((None, M, TN), lambda n, k: (n // per_part, 0, n % per_part))],
        out_specs=pl.BlockSpec((None, TK, TN), lambda n, k: (n // nj, k, n % nj)),
        out_shape=jax.ShapeDtypeStruct((S, K, Ns), BF),
        compiler_params=_cp("parallel", "parallel"))(a, g3)


def _rows(TL, D):
    return pl.BlockSpec((TL, D), lambda i: (i, 0))


def _fixed(R, D):
    return pl.BlockSpec((R, D), lambda i: (0, 0))


def _rowsum8(v):
    T, D = v.shape
    return jnp.sum(v.reshape(T // 8, 8, D), axis=0)


def _norm_parts(xv):
    r = lax.rsqrt(jnp.mean(xv * xv, axis=-1, keepdims=True) + RMS_EPS)
    return xv * r, r


def norm_mod(x, gamma, mods, k_shift, out_dtype, name):
    L, D = x.shape
    TL = _tile(L, (512, 256))

    def body(x_ref, g_ref, m_ref, o_ref):
        xn, _ = _norm_parts(x_ref[...])
        sh, sc = m_ref[k_shift:k_shift + 1, :], m_ref[k_shift + 1:k_shift + 2, :]
        o_ref[...] = ((xn * g_ref[...]) * (1.0 + sc) + sh).astype(o_ref.dtype)

    return pl.pallas_call(body, name=name, grid=(L // TL,),
                          in_specs=[_rows(TL, D), _fixed(1, D), _fixed(6, D)], out_specs=_rows(TL, D),
                          out_shape=jax.ShapeDtypeStruct((L, D), out_dtype), compiler_params=_cp("parallel"))(x, gamma, mods)


def norm_bwd(dh, x, dres, gamma, mods, k_shift, name, branch=None):
    L, D = x.shape
    TL = _tile(L, (512, 256))
    nacc = 4 if branch else 3

    def body(*refs):
        if branch:
            dh_ref, x_ref, dr_ref, g_ref, m_ref, f_ref, fm_ref, dx_ref, s_ref, df_ref, acc = refs
        else:
            dh_ref, x_ref, dr_ref, g_ref, m_ref, dx_ref, s_ref, acc = refs
        i = pl.program_id(0)

        @pl.when(i == 0)
        def _():
            acc[...] = jnp.zeros_like(acc)

        xn, r = _norm_parts(x_ref[...])
        dh_v = dh_ref[...].astype(F32)
        gam = g_ref[...]
        sc = m_ref[k_shift + 1:k_shift + 2, :]
        dn = dh_v * (1.0 + sc)
        dxn = dn * gam
        dx = dr_ref[...] + r * (dxn - xn * jnp.mean(dxn * xn, axis=-1, keepdims=True))
        dx_ref[...] = dx
        acc[0] += _rowsum8(dh_v)
        acc[1] += _rowsum8(dh_v * (xn * gam))
        acc[2] += _rowsum8(dn * xn)
        if branch:
            df_ref[...] = (dx * fm_ref[branch[2]:branch[2] + 1, :]).astype(df_ref.dtype)
            acc[3] += _rowsum8(dx * f_ref[...].astype(F32))

        @pl.when(i == pl.num_programs(0) - 1)
        def _():
            s_ref[...] = jnp.zeros_like(s_ref)
            for q in range(nacc):
                s_ref[q:q + 1, :] = jnp.sum(acc[q], axis=0, keepdims=True)

    in_specs = [_rows(TL, D), _rows(TL, D), _rows(TL, D), _fixed(1, D), _fixed(6, D)]
    out_specs = [_rows(TL, D), _fixed(8, D)]
    out_shape = [jax.ShapeDtypeStruct((L, D), F32), jax.ShapeDtypeStruct((8, D), F32)]
    args = [dh, x, dres, gamma, mods]
    if branch:
        in_specs += [_rows(TL, D), _fixed(6, D)]
        out_specs.append(_rows(TL, D))
        out_shape.append(jax.ShapeDtypeStruct((L, D), BF))
        args += [branch[0], branch[1]]
    return pl.pallas_call(
        body, name=name, grid=(L // TL,), in_specs=in_specs, out_specs=out_specs, out_shape=out_shape,
        scratch_shapes=[pltpu.VMEM((nacc, 8, D), F32)], compiler_params=_cp("arbitrary"))(*args)


def gate_bwd(dx, f, mods, k_gate, name):
    L, D = dx.shape
    TL = _tile(L, (512, 256))

    def body(dx_ref, f_ref, m_ref, o_ref, s_ref, acc):
        i = pl.program_id(0)

        @pl.when(i == 0)
        def _():
            acc[...] = jnp.zeros_like(acc)

        dxv = dx_ref[...]
        o_ref[...] = (dxv * m_ref[k_gate:k_gate + 1, :]).astype(o_ref.dtype)
        acc[...] += _rowsum8(dxv * f_ref[...].astype(F32))

        @pl.when(i == pl.num_programs(0) - 1)
        def _():
            s_ref[...] = jnp.zeros_like(s_ref)
            s_ref[0:1, :] = jnp.sum(acc[...], axis=0, keepdims=True)

    return pl.pallas_call(
        body, name=name, grid=(L // TL,), in_specs=[_rows(TL, D), _rows(TL, D), _fixed(6, D)],
        out_specs=[_rows(TL, D), _fixed(8, D)],
        out_shape=[jax.ShapeDtypeStruct((L, D), BF), jax.ShapeDtypeStruct((8, D), F32)],
        scratch_shapes=[pltpu.VMEM((8, D), F32)], compiler_params=_cp("arbitrary"))(dx, f, mods)


def ffn_in_act(a, w, name):
    M, K = a.shape
    S, _, Ns = w.shape
    half = S // 2
    TM = _tile(M, (512, 256))
    TN = _tile(Ns, (1408, 1024, 768, 512, 256, 128))
    nj = Ns // TN

    def body(a_ref, wg_ref, wu_ref, gu_ref, act_ref):
        av = a_ref[...]
        g = jnp.dot(av, wg_ref[...], preferred_element_type=F32)
        u = jnp.dot(av, wu_ref[...], preferred_element_type=F32)
        gu_ref[0] = g.astype(gu_ref.dtype)
        gu_ref[1] = u.astype(gu_ref.dtype)
        act_ref[...] = (g * jax.nn.sigmoid(g) * u).astype(act_ref.dtype)

    return pl.pallas_call(
        body, name=name, grid=(half, nj, M // TM),
        in_specs=[pl.BlockSpec((TM, K), lambda s, j, i: (i, 0)),
                  pl.BlockSpec((None, K, TN), lambda s, j, i: (s, 0, j)),
                  pl.BlockSpec((None, K, TN), lambda s, j, i: (s + half, 0, j))],
        out_specs=[pl.BlockSpec((2, TM, TN), lambda s, j, i: (0, i, s * nj + j)),
                   pl.BlockSpec((TM, TN), lambda s, j, i: (i, s * nj + j))],
        out_shape=[jax.ShapeDtypeStruct((2, M, half * Ns), BF), jax.ShapeDtypeStruct((M, half * Ns), BF)],
        compiler_params=_cp("parallel", "parallel", "parallel"))(a, w, w)


def ffn_out_bwd(dff, w2, gu, name):
    M, D = dff.shape
    F = w2.shape[0]
    TM = _tile(M, (512, 256))
    CW = _tile(F, (256, 128))

    def body(d_ref, w_ref, gu_ref, o_ref):
        dv = d_ref[...]
        for c in range(0, F, CW):
            da = lax.dot_general(dv, w_ref[c:c + CW, :], (((1,), (1,)), ((), ())), preferred_element_type=F32)
            g = gu_ref[0, :, c:c + CW].astype(F32)
            u = gu_ref[1, :, c:c + CW].astype(F32)
            s = jax.nn.sigmoid(g)
            o_ref[0, :, c:c + CW] = (da * u * (s + g * s * (1.0 - s))).astype(o_ref.dtype)
            o_ref[1, :, c:c + CW] = (da * g * s).astype(o_ref.dtype)

    part = pl.BlockSpec((2, TM, F), lambda i: (0, i, 0))
    return pl.pallas_call(
        body, name=name, grid=(M // TM,),
        in_specs=[pl.BlockSpec((TM, D), lambda i: (i, 0)), pl.BlockSpec((F, D), lambda i: (0, 0)), part],
        out_specs=part, out_shape=jax.ShapeDtypeStruct((2, M, F), BF),
        compiler_params=_cp("parallel"))(dff, w2, gu)


def swiglu_act(gu, name):
    L, F2 = gu.shape
    F = F2 // 2
    TL = _tile(L, (256,))

    def body(gu_ref, o_ref):
        g = gu_ref[:, :F].astype(F32)
        u = gu_ref[:, F:].astype(F32)
        o_ref[...] = (g * jax.nn.sigmoid(g) * u).astype(o_ref.dtype)

    return pl.pallas_call(body, name=name, grid=(L // TL,), in_specs=[_rows(TL, F2)], out_specs=_rows(TL, F),
                          out_shape=jax.ShapeDtypeStruct((L, F), BF), compiler_params=_cp("parallel"))(gu)


def swiglu_bwd(da, gu, name):
    L, F2 = gu.shape
    F = F2 // 2
    TL = _tile(L, (256,))

    def body(da_ref, gu_ref, o_ref):
        g = gu_ref[:, :F].astype(F32)
        u = gu_ref[:, F:].astype(F32)
        d = da_ref[...].astype(F32)
        s = jax.nn.sigmoid(g)
        o_ref[:, :F] = (d * u * (s + g * s * (1.0 - s))).astype(o_ref.dtype)
        o_ref[:, F:] = (d * g * s).astype(o_ref.dtype)

    return pl.pallas_call(body, name=name, grid=(L // TL,), in_specs=[_rows(TL, F), _rows(TL, F2)],
                          out_specs=_rows(TL, F2), out_shape=jax.ShapeDtypeStruct((L, F2), BF),
                          compiler_params=_cp("parallel"))(da, gu)


def glu_res(o, x, mods, k_gate, name):
    L, D = x.shape
    TL = _tile(L, (512, 256))

    def body(o_ref, x_ref, m_ref, mix_ref, y_ref):
        mix = o_ref[:, :D].astype(F32) * jax.nn.sigmoid(o_ref[:, D:].astype(F32))
        mix_ref[...] = mix.astype(mix_ref.dtype)
        y_ref[...] = x_ref[...] + m_ref[k_gate:k_gate + 1, :] * mix

    return pl.pallas_call(
        body, name=name, grid=(L // TL,), in_specs=[_rows(TL, 2 * D), _rows(TL, D), _fixed(6, D)],
        out_specs=[_rows(TL, D), _rows(TL, D)],
        out_shape=[jax.ShapeDtypeStruct((L, D), BF), jax.ShapeDtypeStruct((L, D), F32)],
        compiler_params=_cp("parallel"))(o, x, mods)


def ssm_out_glu(z, w, x, mods, k_gate, name):
    M, K = z.shape
    S, _, Ns = w.shape
    half = S // 2
    TM = _tile(M, (1024, 512, 256))
    TN = _tile(Ns, (512, 256, 128))
    nj = Ns // TN

    def body(z_ref, wv_ref, wg_ref, x_ref, m_ref, o_ref, mix_ref, y_ref):
        zv = z_ref[...]
        val = jnp.dot(zv, wv_ref[...], preferred_element_type=F32)
        gate = jnp.dot(zv, wg_ref[...], preferred_element_type=F32)
        o_ref[0] = val.astype(o_ref.dtype)
        o_ref[1] = gate.astype(o_ref.dtype)
        mix = val * jax.nn.sigmoid(gate)
        mix_ref[...] = mix.astype(mix_ref.dtype)
        y_ref[...] = x_ref[...] + m_ref[k_gate:k_gate + 1, :] * mix

    col = lambda s, j, i: (i, s * nj + j)
    return pl.pallas_call(
        body, name=name, grid=(half, nj, M // TM),
        in_specs=[pl.BlockSpec((TM, K), lambda s, j, i: (i, 0)),
                  pl.BlockSpec((None, K, TN), lambda s, j, i: (s, 0, j)),
                  pl.BlockSpec((None, K, TN), lambda s, j, i: (s + half, 0, j)),
                  pl.BlockSpec((TM, TN), col), pl.BlockSpec((6, TN), lambda s, j, i: (0, s * nj + j))],
        out_specs=[pl.BlockSpec((2, TM, TN), lambda s, j, i: (0, i, s * nj + j)), pl.BlockSpec((TM, TN), col),
                   pl.BlockSpec((TM, TN), col)],
        out_shape=[jax.ShapeDtypeStruct((2, M, half * Ns), BF), jax.ShapeDtypeStruct((M, half * Ns), BF),
                   jax.ShapeDtypeStruct((M, half * Ns), F32)],
        compiler_params=_cp("parallel", "parallel", "parallel"))(z, w, w, x, mods)


def glu_bwd(dmix, o, name):
    _, L, D = o.shape
    TL = _tile(L, (512, 256))

    def body(d_ref, o_ref, do_ref):
        d = d_ref[...].astype(F32)
        val = o_ref[0].astype(F32)
        s = jax.nn.sigmoid(o_ref[1].astype(F32))
        do_ref[0] = (d * s).astype(do_ref.dtype)
        do_ref[1] = (d * val * s * (1.0 - s)).astype(do_ref.dtype)

    part = pl.BlockSpec((2, TL, D), lambda i: (0, i, 0))
    return pl.pallas_call(body, name=name, grid=(L // TL,), in_specs=[_rows(TL, D), part],
                          out_specs=part, out_shape=jax.ShapeDtypeStruct((2, L, D), BF),
                          compiler_params=_cp("parallel"))(dmix, o)


def final_loss(x, target, gamma, f, fmods, k_gate, name):
    L, D = x.shape
    TL = _tile(L, (512, 256))

    def body(x_ref, t_ref, g_ref, f_ref, fm_ref, l_ref, dx_ref, s_ref, df_ref, acc, lacc):
        i = pl.program_id(0)

        @pl.when(i == 0)
        def _():
            acc[...] = jnp.zeros_like(acc)
            lacc[...] = jnp.zeros_like(lacc)

        xn, r = _norm_parts(x_ref[...])
        gam = g_ref[...]
        e = xn * gam - t_ref[...]
        lacc[...] += jnp.sum(0.5 * jnp.mean(e * e, axis=-1, keepdims=True), axis=0, keepdims=True)
        dy = e * (1.0 / D)
        dxn = dy * gam
        dx = r * (dxn - xn * jnp.mean(dxn * xn, axis=-1, keepdims=True))
        dx_ref[...] = dx
        df_ref[...] = (dx * fm_ref[k_gate:k_gate + 1, :]).astype(df_ref.dtype)
        acc[0] += _rowsum8(dy * xn)
        acc[1] += _rowsum8(dx * f_ref[...].astype(F32))

        @pl.when(i == pl.num_programs(0) - 1)
        def _():
            s_ref[...] = jnp.zeros_like(s_ref)
            for q in range(2):
                s_ref[q:q + 1, :] = jnp.sum(acc[q], axis=0, keepdims=True)
            l_ref[...] = jnp.broadcast_to(lacc[...], l_ref.shape)

    return pl.pallas_call(
        body, name=name, grid=(L // TL,),
        in_specs=[_rows(TL, D), _rows(TL, D), _fixed(1, D), _rows(TL, D), _fixed(6, D)],
        out_specs=[_fixed(8, 128), _rows(TL, D), _fixed(8, D), _rows(TL, D)],
        out_shape=[jax.ShapeDtypeStruct((8, 128), F32), jax.ShapeDtypeStruct((L, D), F32),
                   jax.ShapeDtypeStruct((8, D), F32), jax.ShapeDtypeStruct((L, D), BF)],
        scratch_shapes=[pltpu.VMEM((2, 8, D), F32), pltpu.VMEM((1, 1), F32)],
        compiler_params=_cp("arbitrary"))(x, target, gamma, f, fmods)


def _col(L, TC, off):
    return pl.BlockSpec((L, TC), lambda j: (0, off + j))


def _shift_down(v, k, row):
    return jnp.where(row >= k, pltpu.roll(v, k, 0), 0.0)


def _shift_up(v, k, row, L):
    return jnp.where(row < L - k, pltpu.roll(v, L - k, 0), 0.0)


def conv_fwd(p, w, name):
    L, D3 = p.shape
    D = D3 // 3
    TC = _tile(D, (128,))
    nc = D // TC

    def body(b_ref, c_ref, v_ref, w_ref, o_ref):
        row = lax.broadcasted_iota(jnp.int32, (L, TC), 0)
        cv = c_ref[...].astype(F32) * v_ref[...].astype(F32)
        conv = w_ref[2:3, :] * cv + w_ref[1:2, :] * _shift_down(cv, 1, row) + w_ref[0:1, :] * _shift_down(cv, 2, row)
        o_ref[...] = (b_ref[...].astype(F32) * conv).astype(o_ref.dtype)

    return pl.pallas_call(
        body, name=name, grid=(nc,),
        in_specs=[_col(L, TC, 0), _col(L, TC, nc), _col(L, TC, 2 * nc), pl.BlockSpec((3, TC), lambda j: (0, j))],
        out_specs=_col(L, TC, 0), out_shape=jax.ShapeDtypeStruct((L, D), BF), compiler_params=_cp("parallel"))(p, p, p, w)


def conv_bwd(dm, p, w, name):
    L, D3 = p.shape
    D = D3 // 3
    TC = _tile(D, (128,))
    nc = D // TC

    def body(dm_ref, b_ref, c_ref, v_ref, w_ref, db_ref, dc_ref, dv_ref, dw_ref):
        row = lax.broadcasted_iota(jnp.int32, (L, TC), 0)
        cg, vv = c_ref[...].astype(F32), v_ref[...].astype(F32)
        cv = cg * vv
        cv1, cv2 = _shift_down(cv, 1, row), _shift_down(cv, 2, row)
        conv = w_ref[2:3, :] * cv + w_ref[1:2, :] * cv1 + w_ref[0:1, :] * cv2
        dmv = dm_ref[...].astype(F32)
        db_ref[...] = (dmv * conv).astype(db_ref.dtype)
        dconv = dmv * b_ref[...].astype(F32)
        dcv = (w_ref[2:3, :] * dconv + w_ref[1:2, :] * _shift_up(dconv, 1, row, L)
               + w_ref[0:1, :] * _shift_up(dconv, 2, row, L))
        dc_ref[...] = (dcv * vv).astype(dc_ref.dtype)
        dv_ref[...] = (dcv * cg).astype(dv_ref.dtype)
        dw_ref[...] = jnp.zeros_like(dw_ref)
        dw_ref[0:1, :] = jnp.sum(dconv * cv2, axis=0, keepdims=True)
        dw_ref[1:2, :] = jnp.sum(dconv * cv1, axis=0, keepdims=True)
        dw_ref[2:3, :] = jnp.sum(dconv * cv, axis=0, keepdims=True)

    one = jax.ShapeDtypeStruct((L, D), BF)
    return pl.pallas_call(
        body, name=name, grid=(nc,),
        in_specs=[_col(L, TC, 0), _col(L, TC, 0), _col(L, TC, nc), _col(L, TC, 2 * nc),
                  pl.BlockSpec((3, TC), lambda j: (0, j))],
        out_specs=[_col(L, TC, 0), _col(L, TC, 0), _col(L, TC, 0), pl.BlockSpec((8, TC), lambda j: (0, j))],
        out_shape=[one, one, one, jax.ShapeDtypeStruct((8, D), F32)],
        compiler_params=_cp("parallel"))(dm, p, p, p, w)


def _gelu(y):
    return 0.5 * y * (1.0 + jnp.tanh(GELU_C * (y + GELU_A * y * y * y)))


def _gelu_grad(y):
    th = jnp.tanh(GELU_C * (y + GELU_A * y * y * y))
    return 0.5 * (1.0 + th) + 0.5 * y * (1.0 - th * th) * GELU_C * (1.0 + 3.0 * GELU_A * y * y)


def _cmul_add(br, bi, ar, ai, sr, si):
    return br + ar * sr - ai * si, bi + ar * si + ai * sr


def _log2(n):
    k = n.bit_length() - 1
    assert 1 << k == n
    return k


def _replicate(P2, W2, P, GLP, transposed):
    shape = (W2, P2) if transposed else (P2, W2)
    k = lax.broadcasted_iota(jnp.int32, shape, 1 if transposed else 0)
    c = lax.broadcasted_iota(jnp.int32, shape, 0 if transposed else 1)
    return ((k >> _log2(P)) == (c >> _log2(GLP))) & ((k & (P - 1)) == (c & (P - 1)))


def _on_diagonal(KB, W2, H, P, GLP, transposed):
    shape = (W2, KB) if transposed else (KB, W2)
    r = lax.broadcasted_iota(jnp.int32, shape, 1 if transposed else 0)
    c = lax.broadcasted_iota(jnp.int32, shape, 0 if transposed else 1)
    return (r >> _log2(H)) == ((c & (GLP - 1)) >> _log2(P))


def _expand(t, dims, transposed):
    KB, W2, H, P, GLP = dims
    rep = _replicate(2 * P, W2, P, GLP, transposed).astype(t.dtype)
    wide = jnp.dot(rep, t, preferred_element_type=F32) if transposed else jnp.dot(t, rep, preferred_element_type=F32)
    return jnp.where(_on_diagonal(KB, W2, H, P, GLP, transposed), wide, 0.0).astype(t.dtype)


def _extract(acc, dims):
    KB, W2, H, P, GLP = dims
    rep = _replicate(2 * P, W2, P, GLP, True).astype(F32)
    kept = jnp.where(_on_diagonal(KB, W2, H, P, GLP, False), acc, 0.0)
    return jnp.dot(kept, rep, preferred_element_type=F32, precision=lax.Precision.HIGHEST)


def _cmul(ar, ai, sr, si):
    return ar * sr - ai * si, ar * si + ai * sr


LANES = 128


def _cols(ref, base, n, rows):
    return jnp.concatenate([ref[base + q, rows, :] for q in range(n)], axis=1)


def _set_cols(ref, base, n, rows, val):
    for q in range(n):
        ref[base + q, rows, :] = val[:, q * LANES:(q + 1) * LANES]


def _strided_s5_fwd(h, tb, tct, pw, dvec, name):
    L, D = h.shape
    nkb, KB, P2 = tb.shape
    P = P2 // 2
    W = (KB // SSM_GROUP) * P
    W2 = 2 * W
    dims = (KB, W2, SSM_GROUP, P, W)
    TL = _tile(L, (512, 256))
    CH = TL // 8
    NC = W // LANES

    def body(h_ref, tb_ref, tct_ref, pw_ref, d_ref, s_ref, y_ref, z_ref, bw, cw, carry):
        t = pl.program_id(1)

        @pl.when(t == 0)
        def _():
            carry[...] = jnp.zeros_like(carry)
            bw[...] = _expand(tb_ref[...], dims, False)
            cw[...] = _expand(tct_ref[...], dims, True)

        hv = h_ref[...]
        _set_cols(s_ref, 0, 2 * NC, slice(None), jnp.dot(hv.astype(BF), bw[...], preferred_element_type=F32))
        ar, ai = pw_ref[0:8, :W], pw_ref[0:8, W:]
        xr = xi = jnp.zeros((8, W), F32)
        for j in range(CH):
            rows = pl.ds(j, 8, stride=CH)
            xr, xi = _cmul_add(_cols(s_ref, 0, NC, rows), _cols(s_ref, NC, NC, rows), ar, ai, xr, xi)
            _set_cols(s_ref, 0, NC, rows, xr)
            _set_cols(s_ref, NC, NC, rows, xi)
        for k, off in ((1, 8), (2, 16), (4, 24)):
            xr, xi = _cmul_add(xr, xi, pw_ref[off:off + 8, :W], pw_ref[off:off + 8, W:],
                               pltpu.roll(xr, k, 0), pltpu.roll(xi, k, 0))
        xr, xi = _cmul_add(xr, xi, pw_ref[32:40, :W], pw_ref[32:40, W:], carry[0], carry[1])
        first = lax.broadcasted_iota(jnp.int32, (8, W), 0) == 0
        cr = jnp.where(first, carry[0], pltpu.roll(xr, 1, 0))
        ci = jnp.where(first, carry[1], pltpu.roll(xi, 1, 0))
        carry[0] = jnp.broadcast_to(xr[7:8], (8, W))
        carry[1] = jnp.broadcast_to(xi[7:8], (8, W))
        for j in range(CH):
            rows = pl.ds(j, 8, stride=CH)
            cr, ci = _cmul(ar, ai, cr, ci)
            _set_cols(s_ref, 0, NC, rows, _cols(s_ref, 0, NC, rows) + cr)
            _set_cols(s_ref, NC, NC, rows, _cols(s_ref, NC, NC, rows) + ci)
        sv = _cols(s_ref, 0, 2 * NC, slice(None))
        y = jnp.dot(sv.astype(BF), cw[...], preferred_element_type=F32) + d_ref[...] * hv
        y_ref[...] = y
        z_ref[...] = _gelu(y).astype(z_ref.dtype)

    blk = lambda kb, t: (t, kb)
    per_kb = lambda kb, t: (kb, 0, 0)
    return pl.pallas_call(
        body, name=name, grid=(nkb, L // TL),
        in_specs=[pl.BlockSpec((TL, KB), blk), pl.BlockSpec((None, KB, P2), per_kb),
                  pl.BlockSpec((None, P2, KB), per_kb), pl.BlockSpec((None, 40, W2), per_kb),
                  pl.BlockSpec((1, KB), lambda kb, t: (0, kb))],
        out_specs=[pl.BlockSpec((2 * NC, TL, LANES), lambda kb, t: (kb, t, 0)), pl.BlockSpec((TL, KB), blk),
                   pl.BlockSpec((TL, KB), blk)],
        out_shape=[jax.ShapeDtypeStruct((nkb * 2 * NC, L, LANES), F32), jax.ShapeDtypeStruct((L, D), F32),
                   jax.ShapeDtypeStruct((L, D), BF)],
        scratch_shapes=[pltpu.VMEM((KB, W2), BF), pltpu.VMEM((W2, KB), BF), pltpu.VMEM((2, 8, W), F32)],
        compiler_params=_cp("parallel", "arbitrary"))(h, tb, tct, pw, dvec)


def _strided_s5_bwd(dz, y, h, s, tc, tbt, pwr, dvec, name):
    L, D = h.shape
    nkb, KB, P2 = tc.shape
    P = P2 // 2
    W = (KB // SSM_GROUP) * P
    W2 = 2 * W
    dims = (KB, W2, SSM_GROUP, P, W)
    TL = _tile(L, (512, 256))
    CH = TL // 8
    NC = W // LANES
    nt = L // TL

    def body(dz_ref, y_ref, h_ref, s_ref, sp_ref, tc_ref, tbt_ref, pw_ref, d_ref,
             dh_ref, dd_ref, da_ref, db_ref, dc_ref, g, ctw, btw, dbacc, dcacc, carry):
        t = pl.program_id(1)

        @pl.when(t == 0)
        def _():
            carry[...] = jnp.zeros_like(carry)
            dd_ref[...] = jnp.zeros_like(dd_ref)
            da_ref[...] = jnp.zeros_like(da_ref)
            dbacc[...] = jnp.zeros_like(dbacc)
            dcacc[...] = jnp.zeros_like(dcacc)
            ctw[...] = _expand(tc_ref[...], dims, False)
            btw[...] = _expand(tbt_ref[...], dims, True)

        hv = h_ref[...]
        dy = dz_ref[...].astype(F32) * _gelu_grad(y_ref[...])
        dd_ref[...] += _rowsum8(dy * hv)
        dyb = dy.astype(BF)
        _set_cols(g, 0, 2 * NC, slice(None), jnp.dot(dyb, ctw[...], preferred_element_type=F32))
        ar, ai = pw_ref[0:8, :W], pw_ref[0:8, W:]
        gr = gi = jnp.zeros((8, W), F32)
        for j in reversed(range(CH)):
            rows = pl.ds(j, 8, stride=CH)
            gr, gi = _cmul_add(_cols(g, 0, NC, rows), _cols(g, NC, NC, rows), ar, ai, gr, gi)
            _set_cols(g, 0, NC, rows, gr)
            _set_cols(g, NC, NC, rows, gi)
        for k, off in ((1, 8), (2, 16), (4, 24)):
            gr, gi = _cmul_add(gr, gi, pw_ref[off:off + 8, :W], pw_ref[off:off + 8, W:],
                               pltpu.roll(gr, 8 - k, 0), pltpu.roll(gi, 8 - k, 0))
        gr, gi = _cmul_add(gr, gi, pw_ref[32:40, :W], pw_ref[32:40, W:], carry[0], carry[1])
        sub = lax.broadcasted_iota(jnp.int32, (8, W), 0)
        cr = jnp.where(sub == 7, carry[0], pltpu.roll(gr, 7, 0))
        ci = jnp.where(sub == 7, carry[1], pltpu.roll(gi, 7, 0))
        carry[0] = jnp.broadcast_to(gr[0:1], (8, W))
        carry[1] = jnp.broadcast_to(gi[0:1], (8, W))
        live = jnp.where(t == nt - 1, 0.0, 1.0)
        accr = acci = jnp.zeros((8, W), F32)
        for j in reversed(range(CH)):
            rows = pl.ds(j, 8, stride=CH)
            cr, ci = _cmul(ar, ai, cr, ci)
            gr, gi = _cols(g, 0, NC, rows) + cr, _cols(g, NC, NC, rows) + ci
            _set_cols(g, 0, NC, rows, gr)
            _set_cols(g, NC, NC, rows, gi)
            if j > 0:
                before = pl.ds(j - 1, 8, stride=CH)
                pr, pi = _cols(s_ref, 0, NC, before), _cols(s_ref, NC, NC, before)
            else:
                last = pl.ds(CH - 1, 8, stride=CH)
                pr = jnp.where(sub == 0, _cols(sp_ref, 0, NC, slice(7, 8)) * live,
                               pltpu.roll(_cols(s_ref, 0, NC, last), 1, 0))
                pi = jnp.where(sub == 0, _cols(sp_ref, NC, NC, slice(7, 8)) * live,
                               pltpu.roll(_cols(s_ref, NC, NC, last), 1, 0))
            accr = accr + pr * gr + pi * gi
            acci = acci + pr * gi - pi * gr
        da_ref[:, :W] += accr
        da_ref[:, W:] += acci

        gb = _cols(g, 0, 2 * NC, slice(None)).astype(BF)
        dh_ref[...] = dy * d_ref[...] + jnp.dot(gb, btw[...], preferred_element_type=F32)
        tn = (((0,), (0,)), ((), ()))
        dbacc[...] += lax.dot_general(hv.astype(BF), gb, tn, preferred_element_type=F32)
        dcacc[...] += lax.dot_general(dyb, _cols(s_ref, 0, 2 * NC, slice(None)).astype(BF), tn,
                                      preferred_element_type=F32)

        @pl.when(t == nt - 1)
        def _():
            db_ref[...] = _extract(dbacc[...], dims)
            dc_ref[...] = _extract(dcacc[...], dims)

    rev = lambda kb, t: (nt - 1 - t, kb)
    per_kb = lambda kb, t: (kb, 0, 0)
    return pl.pallas_call(
        body, name=name, grid=(nkb, nt),
        in_specs=[pl.BlockSpec((TL, KB), rev), pl.BlockSpec((TL, KB), rev), pl.BlockSpec((TL, KB), rev),
                  pl.BlockSpec((2 * NC, TL, LANES), lambda kb, t: (kb, nt - 1 - t, 0)),
                  pl.BlockSpec((2 * NC, 8, LANES), lambda kb, t: (kb, jnp.maximum((nt - 1 - t) * CH - 1, 0), 0)),
                  pl.BlockSpec((None, KB, P2), per_kb), pl.BlockSpec((None, P2, KB), per_kb),
                  pl.BlockSpec((None, 40, W2), per_kb), pl.BlockSpec((1, KB), lambda kb, t: (0, kb))],
        out_specs=[pl.BlockSpec((TL, KB), rev), pl.BlockSpec((8, KB), lambda kb, t: (0, kb)),
                   pl.BlockSpec((None, 8, W2), per_kb), pl.BlockSpec((None, KB, P2), per_kb),
                   pl.BlockSpec((None, KB, P2), per_kb)],
        out_shape=[jax.ShapeDtypeStruct((L, D), F32), jax.ShapeDtypeStruct((8, D), F32),
                   jax.ShapeDtypeStruct((nkb, 8, W2), F32), jax.ShapeDtypeStruct((nkb, KB, P2), F32),
                   jax.ShapeDtypeStruct((nkb, KB, P2), F32)],
        scratch_shapes=[pltpu.VMEM((2 * NC, TL, LANES), F32), pltpu.VMEM((KB, W2), BF), pltpu.VMEM((W2, KB), BF),
                        pltpu.VMEM((KB, W2), F32), pltpu.VMEM((KB, W2), F32), pltpu.VMEM((2, 8, W), F32)],
        compiler_params=_cp("parallel", "arbitrary"))(dz, y, h, s, s, tc, tbt, pwr, dvec)


def _chunk_order(TL, CH, transposed):
    out_row = lax.broadcasted_iota(jnp.int32, (TL, TL), 1 if transposed else 0)
    in_row = lax.broadcasted_iota(jnp.int32, (TL, TL), 0 if transposed else 1)
    return in_row == ((out_row & 7) << _log2(CH)) + (out_row >> 3)


def _reorder(perm, v):
    hi = v.astype(perm.dtype)
    lo = (v - hi.astype(F32)).astype(perm.dtype)
    return jnp.dot(perm, hi, preferred_element_type=F32) + jnp.dot(perm, lo, preferred_element_type=F32)


def _interleave(main, side):
    n, m, k = len(main), len(side), 0
    for i, step in enumerate(main):
        step()
        while k < m and (k + 1) * n <= (i + 1) * m:
            side[k]()
            k += 1
    for step in side[k:]:
        step()


S5_CHUNK = 512


def s5_fwd(h, tb, tct, pw, dvec, name):
    L, D = h.shape
    nkb, KB, P2 = tb.shape
    P = P2 // 2
    W = (KB // SSM_GROUP) * P
    W2 = 2 * W
    dims = (KB, W2, SSM_GROUP, P, W)
    TL = _tile(L, (512, 256))
    CH = TL // 8
    NB = 2 if nkb % 2 == 0 else 1
    CK = min(S5_CHUNK, W2)

    def body(h_ref, tb_ref, tct_ref, pw_ref, d_ref, s_ref, y_ref, z_ref, bw, cw, perm, unperm, carry):
        t = pl.program_id(1)

        @pl.when(t == 0)
        def _():
            carry[...] = jnp.zeros_like(carry)
            for b in range(NB):
                bw[b] = _expand(tb_ref[b], dims, False)
                cw[b] = _expand(tct_ref[b], dims, True)
            perm[...] = _chunk_order(TL, CH, False).astype(perm.dtype)
            unperm[...] = _chunk_order(TL, CH, True).astype(perm.dtype)

        hp = _reorder(perm[...], h_ref[...])
        hpb = hp.astype(BF)
        first = lax.broadcasted_iota(jnp.int32, (8, W), 0) == 0

        def project(b):
            def chunk(c):
                def emit():
                    s_ref[:, b * W2 + c:b * W2 + c + CK] = jnp.dot(hpb[:, b * KB:(b + 1) * KB], bw[b, :, c:c + CK],
                                                                   preferred_element_type=F32)
                return emit
            return [chunk(c) for c in range(0, W2, CK)]

        def scan(b):
            re, im = slice(b * W2, b * W2 + W), slice(b * W2 + W, (b + 1) * W2)
            ar, ai = pw_ref[b, 0:8, :W], pw_ref[b, 0:8, W:]
            st = {"x": (jnp.zeros((8, W), F32), jnp.zeros((8, W), F32))}

            def own(j):
                def emit():
                    rows = slice(j * 8, j * 8 + 8)
                    xr, xi = _cmul_add(s_ref[rows, re], s_ref[rows, im], ar, ai, *st["x"])
                    s_ref[rows, re] = xr
                    s_ref[rows, im] = xi
                    st["x"] = (xr, xi)
                return emit

            def ends():
                xr, xi = st["x"]
                for k, off in ((1, 8), (2, 16), (4, 24)):
                    xr, xi = _cmul_add(xr, xi, pw_ref[b, off:off + 8, :W], pw_ref[b, off:off + 8, W:],
                                       pltpu.roll(xr, k, 0), pltpu.roll(xi, k, 0))
                xr, xi = _cmul_add(xr, xi, pw_ref[b, 32:40, :W], pw_ref[b, 32:40, W:], carry[b, 0], carry[b, 1])
                st["c"] = (jnp.where(first, carry[b, 0], pltpu.roll(xr, 1, 0)),
                           jnp.where(first, carry[b, 1], pltpu.roll(xi, 1, 0)))
                carry[b, 0] = jnp.broadcast_to(xr[7:8], (8, W))
                carry[b, 1] = jnp.broadcast_to(xi[7:8], (8, W))

            def carried(j):
                def emit():
                    rows = slice(j * 8, j * 8 + 8)
                    cr, ci = _cmul(ar, ai, *st["c"])
                    s_ref[rows, re] = s_ref[rows, re] + cr
                    s_ref[rows, im] = s_ref[rows, im] + ci
                    st["c"] = (cr, ci)
                return emit

            return [own(j) for j in range(CH)] + [ends] + [carried(j) for j in range(CH)]

        def readout(b):
            cols = slice(b * KB, (b + 1) * KB)
            acc = {}

            def chunk(c):
                def emit():
                    part = jnp.dot(s_ref[:, b * W2 + c:b * W2 + c + CK].astype(BF), cw[b, c:c + CK, :],
                                   preferred_element_type=F32)
                    acc["y"] = part if c == 0 else acc["y"] + part
                return emit

            def finish():
                y = acc["y"] + d_ref[:, cols] * hp[:, cols]
                y_ref[:, cols] = y
                z_ref[:, cols] = jnp.dot(unperm[...], _gelu(y).astype(BF),
                                         preferred_element_type=F32).astype(z_ref.dtype)

            return [chunk(c) for c in range(0, W2, CK)] + [finish]

        for emit in project(0):
            emit()
        for b in range(NB):
            side = (project(b + 1) if b + 1 < NB else []) + (readout(b - 1) if b > 0 else [])
            _interleave(scan(b), side)
        for emit in readout(NB - 1):
            emit()

    blk = lambda kb, t: (t, kb)
    per_kb = lambda kb, t: (kb, 0, 0)
    return pl.pallas_call(
        body, name=name, grid=(nkb // NB, L // TL),
        in_specs=[pl.BlockSpec((TL, NB * KB), blk), pl.BlockSpec((NB, KB, P2), per_kb),
                  pl.BlockSpec((NB, P2, KB), per_kb), pl.BlockSpec((NB, 40, W2), per_kb),
                  pl.BlockSpec((1, NB * KB), lambda kb, t: (0, kb))],
        out_specs=[pl.BlockSpec((TL, NB * W2), blk), pl.BlockSpec((TL, NB * KB), blk),
                   pl.BlockSpec((TL, NB * KB), blk)],
        out_shape=[jax.ShapeDtypeStruct((L, nkb * W2), F32), jax.ShapeDtypeStruct((L, D), F32),
                   jax.ShapeDtypeStruct((L, D), BF)],
        scratch_shapes=[pltpu.VMEM((NB, KB, W2), BF), pltpu.VMEM((NB, W2, KB), BF), pltpu.VMEM((TL, TL), BF),
                        pltpu.VMEM((TL, TL), BF), pltpu.VMEM((NB, 2, 8, W), F32)],
        compiler_params=_cp("parallel", "arbitrary"))(h, tb, tct, pw, dvec)


def _s5_fwd_one_block(h, tb, tct, pw, dvec, name):
    L, D = h.shape
    nkb, KB, P2 = tb.shape
    P = P2 // 2
    W = (KB // SSM_GROUP) * P
    W2 = 2 * W
    dims = (KB, W2, SSM_GROUP, P, W)
    TL = _tile(L, (512, 256))
    CH = TL // 8

    def body(h_ref, tb_ref, tct_ref, pw_ref, d_ref, s_ref, y_ref, z_ref, bw, cw, perm, unperm, carry):
        t = pl.program_id(1)

        @pl.when(t == 0)
        def _():
            carry[...] = jnp.zeros_like(carry)
            bw[...] = _expand(tb_ref[...], dims, False)
            cw[...] = _expand(tct_ref[...], dims, True)
            perm[...] = _chunk_order(TL, CH, False).astype(perm.dtype)
            unperm[...] = _chunk_order(TL, CH, True).astype(perm.dtype)

        hp = _reorder(perm[...], h_ref[...])
        s_ref[...] = jnp.dot(hp.astype(BF), bw[...], preferred_element_type=F32)
        ar, ai = pw_ref[0:8, :W], pw_ref[0:8, W:]

        def own(j, x):
            rows = pl.ds(pl.multiple_of(j * 8, 8), 8)
            xr, xi = _cmul_add(s_ref[rows, :W], s_ref[rows, W:], ar, ai, x[0], x[1])
            s_ref[rows, :W] = xr
            s_ref[rows, W:] = xi
            return xr, xi

        zero = jnp.zeros((8, W), F32)
        xr, xi = lax.fori_loop(0, CH, own, (zero, zero))
        for k, off in ((1, 8), (2, 16), (4, 24)):
            xr, xi = _cmul_add(xr, xi, pw_ref[off:off + 8, :W], pw_ref[off:off + 8, W:],
                               pltpu.roll(xr, k, 0), pltpu.roll(xi, k, 0))
        xr, xi = _cmul_add(xr, xi, pw_ref[32:40, :W], pw_ref[32:40, W:], carry[0], carry[1])
        first = lax.broadcasted_iota(jnp.int32, (8, W), 0) == 0
        cr = jnp.where(first, carry[0], pltpu.roll(xr, 1, 0))
        ci = jnp.where(first, carry[1], pltpu.roll(xi, 1, 0))
        carry[0] = jnp.broadcast_to(xr[7:8], (8, W))
        carry[1] = jnp.broadcast_to(xi[7:8], (8, W))

        def carried(j, c):
            rows = pl.ds(pl.multiple_of(j * 8, 8), 8)
            cr, ci = _cmul(ar, ai, c[0], c[1])
            s_ref[rows, :W] = s_ref[rows, :W] + cr
            s_ref[rows, W:] = s_ref[rows, W:] + ci
            return cr, ci

        lax.fori_loop(0, CH, carried, (cr, ci))
        y = jnp.dot(s_ref[...].astype(BF), cw[...], preferred_element_type=F32) + d_ref[...] * hp
        y_ref[...] = y
        z_ref[...] = jnp.dot(unperm[...], _gelu(y).astype(BF), preferred_element_type=F32).astype(z_ref.dtype)

    blk = lambda kb, t: (t, kb)
    per_kb = lambda kb, t: (kb, 0, 0)
    return pl.pallas_call(
        body, name=name, grid=(nkb, L // TL),
        in_specs=[pl.BlockSpec((TL, KB), blk), pl.BlockSpec((None, KB, P2), per_kb),
                  pl.BlockSpec((None, P2, KB), per_kb), pl.BlockSpec((None, 40, W2), per_kb),
                  pl.BlockSpec((1, KB), lambda kb, t: (0, kb))],
        out_specs=[pl.BlockSpec((TL, W2), blk), pl.BlockSpec((TL, KB), blk), pl.BlockSpec((TL, KB), blk)],
        out_shape=[jax.ShapeDtypeStruct((L, nkb * W2), F32), jax.ShapeDtypeStruct((L, D), F32),
                   jax.ShapeDtypeStruct((L, D), BF)],
        scratch_shapes=[pltpu.VMEM((KB, W2), BF), pltpu.VMEM((W2, KB), BF), pltpu.VMEM((TL, TL), BF),
                        pltpu.VMEM((TL, TL), BF), pltpu.VMEM((2, 8, W), F32)],
        compiler_params=_cp("parallel", "arbitrary"))(h, tb, tct, pw, dvec)


def s5_bwd(dz, y, h, s, tc, tbt, pwr, dvec, name):
    L, D = h.shape
    nkb, KB, P2 = tc.shape
    P = P2 // 2
    W = (KB // SSM_GROUP) * P
    W2 = 2 * W
    dims = (KB, W2, SSM_GROUP, P, W)
    TL = _tile(L, (512, 256))
    CH = TL // 8
    nt = L // TL
    NB = 2 if nkb % 2 == 0 else 1
    CK = min(S5_CHUNK, W2)
    tn = (((0,), (0,)), ((), ()))

    def body(dz_ref, y_ref, h_ref, s_ref, sp_ref, tc_ref, tbt_ref, pw_ref, d_ref,
             dh_ref, dd_ref, da_ref, db_ref, dc_ref, g, ctw, btw, dbacc, dcacc, dys, perm, unperm, carry):
        t = pl.program_id(1)

        @pl.when(t == 0)
        def _():
            carry[...] = jnp.zeros_like(carry)
            dd_ref[...] = jnp.zeros_like(dd_ref)
            da_ref[...] = jnp.zeros_like(da_ref)
            dbacc[...] = jnp.zeros_like(dbacc)
            dcacc[...] = jnp.zeros_like(dcacc)
            for b in range(NB):
                ctw[b] = _expand(tc_ref[b], dims, False)
                btw[b] = _expand(tbt_ref[b], dims, True)
            perm[...] = _chunk_order(TL, CH, False).astype(perm.dtype)
            unperm[...] = _chunk_order(TL, CH, True).astype(perm.dtype)

        hp = jnp.dot(perm[...], h_ref[...].astype(BF), preferred_element_type=F32)
        dy = jnp.dot(perm[...], dz_ref[...].astype(BF), preferred_element_type=F32) * _gelu_grad(y_ref[...])
        dd_ref[...] += _rowsum8(dy * hp)
        dys[...] = dy
        dyb = dy.astype(BF)
        hpb = hp.astype(BF)
        sub = lax.broadcasted_iota(jnp.int32, (8, W), 0)
        live = jnp.where(t == nt - 1, 0.0, 1.0)

        def lead(b):
            cols = slice(b * KB, (b + 1) * KB)

            def to_states(c):
                def emit():
                    g[b, :, c:c + CK] = jnp.dot(dyb[:, cols], ctw[b, :, c:c + CK], preferred_element_type=F32)
                return emit

            def d_c(c):
                def emit():
                    dcacc[b, :, c:c + CK] += lax.dot_general(dyb[:, cols],
                                                             s_ref[:, b * W2 + c:b * W2 + c + CK].astype(BF), tn,
                                                             preferred_element_type=F32)
                return emit

            return [f(c) for c in range(0, W2, CK) for f in (to_states, d_c)]

        def scan(b):
            re, im = slice(b * W2, b * W2 + W), slice(b * W2 + W, (b + 1) * W2)
            ar, ai = pw_ref[b, 0:8, :W], pw_ref[b, 0:8, W:]
            zero = jnp.zeros((8, W), F32)
            st = {"g": (zero, zero), "acc": (zero, zero)}

            def own(j):
                def emit():
                    rows = slice(j * 8, j * 8 + 8)
                    gr, gi = _cmul_add(g[b, rows, :W], g[b, rows, W:], ar, ai, *st["g"])
                    g[b, rows, :W] = gr
                    g[b, rows, W:] = gi
                    st["g"] = (gr, gi)
                return emit

            def ends():
                gr, gi = st["g"]
                for k, off in ((1, 8), (2, 16), (4, 24)):
                    gr, gi = _cmul_add(gr, gi, pw_ref[b, off:off + 8, :W], pw_ref[b, off:off + 8, W:],
                                       pltpu.roll(gr, 8 - k, 0), pltpu.roll(gi, 8 - k, 0))
                gr, gi = _cmul_add(gr, gi, pw_ref[b, 32:40, :W], pw_ref[b, 32:40, W:], carry[b, 0], carry[b, 1])
                st["c"] = (jnp.where(sub == 7, carry[b, 0], pltpu.roll(gr, 7, 0)),
                           jnp.where(sub == 7, carry[b, 1], pltpu.roll(gi, 7, 0)))
                carry[b, 0] = jnp.broadcast_to(gr[0:1], (8, W))
                carry[b, 1] = jnp.broadcast_to(gi[0:1], (8, W))

            def carried(j):
                def emit():
                    rows = slice(j * 8, j * 8 + 8)
                    cr, ci = _cmul(ar, ai, *st["c"])
                    gr, gi = g[b, rows, :W] + cr, g[b, rows, W:] + ci
                    g[b, rows, :W] = gr
                    g[b, rows, W:] = gi
                    if j > 0:
                        before = slice(j * 8 - 8, j * 8)
                        pr, pi = s_ref[before, re], s_ref[before, im]
                    else:
                        last = slice(TL - 8, TL)
                        pr = jnp.where(sub == 0, sp_ref[7:8, re] * live, pltpu.roll(s_ref[last, re], 1, 0))
                        pi = jnp.where(sub == 0, sp_ref[7:8, im] * live, pltpu.roll(s_ref[last, im], 1, 0))
                    accr, acci = st["acc"]
                    st["c"] = (cr, ci)
                    st["acc"] = (accr + pr * gr + pi * gi, acci + pr * gi - pi * gr)
                return emit

            def done():
                da_ref[b, :, :W] += st["acc"][0]
                da_ref[b, :, W:] += st["acc"][1]

            return ([own(j) for j in reversed(range(CH))] + [ends] + [carried(j) for j in reversed(range(CH))]
                    + [done])

        def tail(b):
            cols = slice(b * KB, (b + 1) * KB)
            acc = {}

            def d_u(c):
                def emit():
                    part = jnp.dot(g[b, :, c:c + CK].astype(BF), btw[b, c:c + CK, :], preferred_element_type=F32)
                    acc["u"] = part if c == 0 else acc["u"] + part
                return emit

            def d_b(c):
                def emit():
                    dbacc[b, :, c:c + CK] += lax.dot_general(hpb[:, cols], g[b, :, c:c + CK].astype(BF), tn,
                                                             preferred_element_type=F32)
                return emit

            def finish():
                dh = (dys[:, cols] * d_ref[:, cols] + acc["u"]).astype(BF)
                dh_ref[:, cols] = jnp.dot(unperm[...], dh, preferred_element_type=F32).astype(dh_ref.dtype)

            return [f(c) for c in range(0, W2, CK) for f in (d_u, d_b)] + [finish]

        for emit in lead(0):
            emit()
        for b in range(NB):
            side = (lead(b + 1) if b + 1 < NB else []) + (tail(b - 1) if b > 0 else [])
            _interleave(scan(b), side)
        for emit in tail(NB - 1):
            emit()

        @pl.when(t == nt - 1)
        def _():
            for b in range(NB):
                db_ref[b] = _extract(dbacc[b], dims)
                dc_ref[b] = _extract(dcacc[b], dims)

    rev = lambda kb, t: (nt - 1 - t, kb)
    prev = lambda kb, t: (jnp.maximum((nt - 1 - t) * CH - 1, 0), kb)
    per_kb = lambda kb, t: (kb, 0, 0)
    return pl.pallas_call(
        body, name=name, grid=(nkb // NB, nt),
        in_specs=[pl.BlockSpec((TL, NB * KB), rev), pl.BlockSpec((TL, NB * KB), rev),
                  pl.BlockSpec((TL, NB * KB), rev), pl.BlockSpec((TL, NB * W2), rev),
                  pl.BlockSpec((8, NB * W2), prev), pl.BlockSpec((NB, KB, P2), per_kb),
                  pl.BlockSpec((NB, P2, KB), per_kb), pl.BlockSpec((NB, 40, W2), per_kb),
                  pl.BlockSpec((1, NB * KB), lambda kb, t: (0, kb))],
        out_specs=[pl.BlockSpec((TL, NB * KB), rev), pl.BlockSpec((8, NB * KB), lambda kb, t: (0, kb)),
                   pl.BlockSpec((NB, 8, W2), per_kb), pl.BlockSpec((NB, KB, P2), per_kb),
                   pl.BlockSpec((NB, KB, P2), per_kb)],
        out_shape=[jax.ShapeDtypeStruct((L, D), BF), jax.ShapeDtypeStruct((8, D), F32),
                   jax.ShapeDtypeStruct((nkb, 8, W2), F32), jax.ShapeDtypeStruct((nkb, KB, P2), F32),
                   jax.ShapeDtypeStruct((nkb, KB, P2), F32)],
        scratch_shapes=[pltpu.VMEM((NB, TL, W2), F32), pltpu.VMEM((NB, KB, W2), BF), pltpu.VMEM((NB, W2, KB), BF),
                        pltpu.VMEM((NB, KB, W2), F32), pltpu.VMEM((NB, KB, W2), F32), pltpu.VMEM((TL, NB * KB), F32),
                        pltpu.VMEM((TL, TL), BF), pltpu.VMEM((TL, TL), BF), pltpu.VMEM((NB, 2, 8, W), F32)],
        compiler_params=pltpu.CompilerParams(dimension_semantics=("parallel", "arbitrary"),
                                             vmem_limit_bytes=V7X_VMEM_BYTES - 4 * 1024 * 1024),
    )(dz, y, h, s, s, tc, tbt, pwr, dvec)


def _s5_bwd_one_block(dz, y, h, s, tc, tbt, pwr, dvec, name):
    L, D = h.shape
    nkb, KB, P2 = tc.shape
    P = P2 // 2
    W = (KB // SSM_GROUP) * P
    W2 = 2 * W
    dims = (KB, W2, SSM_GROUP, P, W)
    TL = _tile(L, (512, 256))
    CH = TL // 8
    nt = L // TL

    def body(dz_ref, y_ref, h_ref, s_ref, sp_ref, tc_ref, tbt_ref, pw_ref, d_ref,
             dh_ref, dd_ref, da_ref, db_ref, dc_ref, g, ctw, btw, dbacc, dcacc, perm, unperm, carry):
        t = pl.program_id(1)

        @pl.when(t == 0)
        def _():
            carry[...] = jnp.zeros_like(carry)
            dd_ref[...] = jnp.zeros_like(dd_ref)
            da_ref[...] = jnp.zeros_like(da_ref)
            dbacc[...] = jnp.zeros_like(dbacc)
            dcacc[...] = jnp.zeros_like(dcacc)
            ctw[...] = _expand(tc_ref[...], dims, False)
            btw[...] = _expand(tbt_ref[...], dims, True)
            perm[...] = _chunk_order(TL, CH, False).astype(perm.dtype)
            unperm[...] = _chunk_order(TL, CH, True).astype(perm.dtype)

        hp = jnp.dot(perm[...], h_ref[...].astype(BF), preferred_element_type=F32)
        dy = jnp.dot(perm[...], dz_ref[...].astype(BF), preferred_element_type=F32) * _gelu_grad(y_ref[...])
        dd_ref[...] += _rowsum8(dy * hp)
        dyb = dy.astype(BF)
        g[...] = jnp.dot(dyb, ctw[...], preferred_element_type=F32)
        ar, ai = pw_ref[0:8, :W], pw_ref[0:8, W:]

        def own(jj, x):
            rows = pl.ds(pl.multiple_of((CH - 1 - jj) * 8, 8), 8)
            gr, gi = _cmul_add(g[rows, :W], g[rows, W:], ar, ai, x[0], x[1])
            g[rows, :W] = gr
            g[rows, W:] = gi
            return gr, gi

        zero = jnp.zeros((8, W), F32)
        gr, gi = lax.fori_loop(0, CH, own, (zero, zero))
        for k, off in ((1, 8), (2, 16), (4, 24)):
            gr, gi = _cmul_add(gr, gi, pw_ref[off:off + 8, :W], pw_ref[off:off + 8, W:],
                               pltpu.roll(gr, 8 - k, 0), pltpu.roll(gi, 8 - k, 0))
        gr, gi = _cmul_add(gr, gi, pw_ref[32:40, :W], pw_ref[32:40, W:], carry[0], carry[1])
        sub = lax.broadcasted_iota(jnp.int32, (8, W), 0)
        cr = jnp.where(sub == 7, carry[0], pltpu.roll(gr, 7, 0))
        ci = jnp.where(sub == 7, carry[1], pltpu.roll(gi, 7, 0))
        carry[0] = jnp.broadcast_to(gr[0:1], (8, W))
        carry[1] = jnp.broadcast_to(gi[0:1], (8, W))

        def carried(jj, c):
            j = CH - 1 - jj
            rows = pl.ds(pl.multiple_of(j * 8, 8), 8)
            before = pl.ds(pl.multiple_of(j * 8 - 8, 8), 8)
            cr, ci = _cmul(ar, ai, c[0], c[1])
            gr, gi = g[rows, :W] + cr, g[rows, W:] + ci
            g[rows, :W] = gr
            g[rows, W:] = gi
            pr, pi = s_ref[before, :W], s_ref[before, W:]
            return cr, ci, c[2] + pr * gr + pi * gi, c[3] + pr * gi - pi * gr

        cr, ci, accr, acci = lax.fori_loop(0, CH - 1, carried, (cr, ci, zero, zero))
        live = jnp.where(t == nt - 1, 0.0, 1.0)
        cr, ci = _cmul(ar, ai, cr, ci)
        gr, gi = g[0:8, :W] + cr, g[0:8, W:] + ci
        g[0:8, :W] = gr
        g[0:8, W:] = gi
        pr = jnp.where(sub == 0, sp_ref[7:8, :W] * live, pltpu.roll(s_ref[TL - 8:TL, :W], 1, 0))
        pi = jnp.where(sub == 0, sp_ref[7:8, W:] * live, pltpu.roll(s_ref[TL - 8:TL, W:], 1, 0))
        da_ref[:, :W] += accr + pr * gr + pi * gi
        da_ref[:, W:] += acci + pr * gi - pi * gr

        gb = g[...].astype(BF)
        dh = dy * d_ref[...] + jnp.dot(gb, btw[...], preferred_element_type=F32)
        dh_ref[...] = _reorder(unperm[...], dh)
        tn = (((0,), (0,)), ((), ()))
        dbacc[...] += lax.dot_general(hp.astype(BF), gb, tn, preferred_element_type=F32)
        dcacc[...] += lax.dot_general(dyb, s_ref[...].astype(BF), tn, preferred_element_type=F32)

        @pl.when(t == nt - 1)
        def _():
            db_ref[...] = _extract(dbacc[...], dims)
            dc_ref[...] = _extract(dcacc[...], dims)

    rev = lambda kb, t: (nt - 1 - t, kb)
    prev = lambda kb, t: (jnp.maximum((nt - 1 - t) * CH - 1, 0), kb)
    per_kb = lambda kb, t: (kb, 0, 0)
    return pl.pallas_call(
        body, name=name, grid=(nkb, nt),
        in_specs=[pl.BlockSpec((TL, KB), rev), pl.BlockSpec((TL, KB), rev), pl.BlockSpec((TL, KB), rev),
                  pl.BlockSpec((TL, W2), rev), pl.BlockSpec((8, W2), prev),
                  pl.BlockSpec((None, KB, P2), per_kb), pl.BlockSpec((None, P2, KB), per_kb),
                  pl.BlockSpec((None, 40, W2), per_kb), pl.BlockSpec((1, KB), lambda kb, t: (0, kb))],
        out_specs=[pl.BlockSpec((TL, KB), rev), pl.BlockSpec((8, KB), lambda kb, t: (0, kb)),
                   pl.BlockSpec((None, 8, W2), per_kb), pl.BlockSpec((None, KB, P2), per_kb),
                   pl.BlockSpec((None, KB, P2), per_kb)],
        out_shape=[jax.ShapeDtypeStruct((L, D), F32), jax.ShapeDtypeStruct((8, D), F32),
                   jax.ShapeDtypeStruct((nkb, 8, W2), F32), jax.ShapeDtypeStruct((nkb, KB, P2), F32),
                   jax.ShapeDtypeStruct((nkb, KB, P2), F32)],
        scratch_shapes=[pltpu.VMEM((TL, W2), F32), pltpu.VMEM((KB, W2), BF), pltpu.VMEM((W2, KB), BF),
                        pltpu.VMEM((KB, W2), F32), pltpu.VMEM((KB, W2), F32), pltpu.VMEM((TL, TL), BF),
                        pltpu.VMEM((TL, TL), BF), pltpu.VMEM((2, 8, W), F32)],
        compiler_params=_cp("parallel", "arbitrary"))(dz, y, h, s, s, tc, tbt, pwr, dvec)


def _discretise(a_re, a_im, log_step, b_re, b_im):
    lr = jnp.minimum(a_re, -1e-4)
    li = a_im
    dt = jnp.exp(log_step)[:, None]
    mag = jnp.exp(lr * dt)
    abr = mag * jnp.cos(li * dt)
    abi = mag * jnp.sin(li * dt)
    den = lr * lr + li * li
    qr = ((abr - 1.0) * lr + abi * li) / den
    qi = (abi * lr - (abr - 1.0) * li) / den
    bbar_re = qr[..., None] * b_re - qi[..., None] * b_im
    bbar_im = qr[..., None] * b_im + qi[..., None] * b_re
    return abr, abi, bbar_re, bbar_im


def _compact(m_re, m_im, nkb):
    G, H, P = m_re.shape
    t = jnp.stack([m_re, m_im], axis=2).reshape(nkb, (G // nkb) * H, 2 * P).astype(BF)
    return t, jnp.swapaxes(t, 1, 2)


def _scan_powers(abr, abi, nkb, conj, CH):
    G, P = abr.shape
    if conj:
        abi = -abi

    def cmul(u, v):
        return u[0] * v[0] - u[1] * v[1], u[0] * v[1] + u[1] * v[0]

    q = (abr, abi)
    for _ in range(_log2(CH)):
        q = cmul(q, q)
    pows = [q]
    for _ in range(7):
        pows.append(cmul(pows[-1], q))
    row = jnp.arange(8)[:, None, None]

    def table(part):
        out = [jnp.broadcast_to((abr, abi)[part][None], (8, G, P))]
        for k in (1, 2, 4):
            keep = (row <= 7 - k) if conj else (row >= k)
            out.append(jnp.where(keep, pows[k - 1][part][None], 0.0))
        ends = jnp.stack([p[part] for p in pows])
        out.append(ends[::-1] if conj else ends)
        return jnp.concatenate(out, axis=0)

    GL = G // nkb
    t = jnp.stack([table(0), table(1)], axis=1)
    t = t.reshape(40, 2, nkb, GL * P).transpose(2, 0, 1, 3)
    return t.reshape(nkb, 40, 2 * GL * P)


def ada_mods(c_all, w_ada, b_sh, name):
    nl, D, NA = w_ada.shape

    def body(c_ref, w_ref, b_ref, o_ref):
        cv = c_ref[...]
        act = cv * jax.nn.sigmoid(cv)
        o_ref[...] = jnp.dot(act, w_ref[...], preferred_element_type=F32, precision=lax.Precision.HIGHEST) + b_ref[...]

    return pl.pallas_call(
        body, name=name, grid=(nl,),
        in_specs=[pl.BlockSpec((8, D), lambda i: (0, 0)), pl.BlockSpec((None, D, NA), lambda i: (i, 0, 0)),
                  pl.BlockSpec((None, 1, NA), lambda i: (i, 0, 0))],
        out_specs=pl.BlockSpec((None, 8, NA), lambda i: (i, 0, 0)),
        out_shape=jax.ShapeDtypeStruct((nl, 8, NA), F32), compiler_params=_cp("parallel"))(c_all, w_ada, b_sh)


def _adamw(w, g, m, v):
    m = ADAM_B1 * m + (1.0 - ADAM_B1) * g
    v = ADAM_B2 * v + (1.0 - ADAM_B2) * (g * g)
    m_hat = m / (1.0 - ADAM_B1 ** ADAM_STEP)
    v_hat = v / (1.0 - ADAM_B2 ** ADAM_STEP)
    return -ADAM_LR * (m_hat / (jnp.sqrt(v_hat) + ADAM_EPS) + ADAM_WD * w), m, v


def _adam_rows(R, C):
    cap = max(8, (256 * 1024) // C)
    for t in range(min(R, cap), 0, -1):
        if R % t == 0 and (t % 8 == 0 or t == R):
            return t
    return R


def adamw_ada(c_t, dm, w, m, v, name):
    nl, D, NA = w.shape
    TK = _tile(D, (256, 128))

    def body(c_ref, dm_ref, w_ref, m_ref, v_ref, g_ref, d_ref, nm_ref, nv_ref):
        cv = c_ref[...]
        act = cv * jax.nn.sigmoid(cv)
        g = jnp.dot(act, dm_ref[...], preferred_element_type=F32, precision=lax.Precision.HIGHEST)
        g_ref[...] = g
        d_ref[...], nm_ref[...], nv_ref[...] = _adamw(w_ref[...], g, m_ref[...], v_ref[...])

    big = pl.BlockSpec((None, TK, NA), lambda i, k: (i, k, 0))
    shape = jax.ShapeDtypeStruct(w.shape, F32)
    return pl.pallas_call(
        body, name=name, grid=(nl, D // TK),
        in_specs=[pl.BlockSpec((TK, 8), lambda i, k: (k, 0)), pl.BlockSpec((None, 8, NA), lambda i, k: (i, 0, 0)),
                  big, big, big],
        out_specs=[big] * 4, out_shape=[shape] * 4, compiler_params=_cp("parallel", "parallel"))(c_t, dm, w, m, v)


def adamw_sharded(w, m, v, ga, gb, name):
    nl, R, C = w.shape
    TR = _adam_rows(R, C)

    def body(w_ref, m_ref, v_ref, a_ref, b_ref, g_ref, d_ref, nm_ref, nv_ref):
        g = a_ref[...] + b_ref[...]
        g_ref[...] = g
        d_ref[...], nm_ref[...], nv_ref[...] = _adamw(w_ref[...], g, m_ref[...], v_ref[...])

    big = pl.BlockSpec((None, TR, C), lambda i, r: (i, r, 0))
    shape = jax.ShapeDtypeStruct(w.shape, F32)
    return pl.pallas_call(
        body, name=name, grid=(nl, R // TR), in_specs=[big] * 5,
        out_specs=[big] * 4, out_shape=[shape] * 4, compiler_params=_cp("parallel", "parallel"))(w, m, v, ga, gb)


def adamw_slab(g, w, m, v, name):
    R, C = g.shape
    TR = _tile(R, (160, 80, 40, 8))

    def body(g_ref, w_ref, m_ref, v_ref, d_ref, nm_ref, nv_ref):
        d_ref[...], nm_ref[...], nv_ref[...] = _adamw(w_ref[...], g_ref[...], m_ref[...], v_ref[...])

    big = pl.BlockSpec((TR, C), lambda r: (r, 0))
    shape = jax.ShapeDtypeStruct((R, C), F32)
    return pl.pallas_call(
        body, name=name, grid=(R // TR,), in_specs=[big] * 4,
        out_specs=[big] * 3, out_shape=[shape] * 3, compiler_params=_cp("parallel"))(g, w, m, v)


def adamw_plain(w, m, v, g, name):
    def body(w_ref, m_ref, v_ref, g_ref, d_ref, nm_ref, nv_ref):
        d_ref[...], nm_ref[...], nv_ref[...] = _adamw(w_ref[...], g_ref[...], m_ref[...], v_ref[...])

    shape = jax.ShapeDtypeStruct(w.shape, F32)
    return pl.pallas_call(body, name=name, out_shape=[shape] * 3,
                          compiler_params=pltpu.CompilerParams(vmem_limit_bytes=VMEM_LIMIT))(w, m, v, g)


def _slab_rows(a):
    n = a.size
    rows = -(-n // SLAB_W)
    return -(-rows // 8) * 8


def _pack(arrs, pad_rows_to=0):
    out = []
    for a in arrs:
        rows = _slab_rows(a)
        flat = a.reshape(-1).astype(F32)
        flat = jnp.pad(flat, (0, rows * SLAB_W - flat.shape[0]))
        out.append(flat.reshape(rows, SLAB_W))
    total = sum(o.shape[0] for o in out)
    if pad_rows_to and total % pad_rows_to:
        out.append(jnp.zeros((pad_rows_to - total % pad_rows_to, SLAB_W), F32))
    return jnp.concatenate(out, axis=0)


def _unpack(slab, like):
    out, r = [], 0
    for a in like:
        rows = _slab_rows(a)
        out.append(slab[r:r + rows].reshape(-1)[:a.size].reshape(a.shape))
        r += rows
    return out


WEIGHTS = ['norm1_g', 'norm2_g', 'w_ada', 'b_ada', 'ssm_a_re', 'ssm_a_im', 'ssm_log_step', 'ssm_b_re', 'ssm_b_im',
           'ssm_c_re', 'ssm_c_im', 'ssm_d', 'ssm_w_out', 'conv_w_in', 'conv_w', 'conv_w_out', 'w_ffn_in',
           'w_ffn_out', 'final_g']
SLAB = ['norm1_g', 'norm2_g', 'b_ada', 'ssm_a_re', 'ssm_a_im', 'ssm_log_step', 'ssm_b_re', 'ssm_b_im', 'ssm_c_re',
        'ssm_c_im', 'ssm_d', 'final_g']
SHARDED = ['ssm_w_out', 'conv_w_in', 'conv_w_out', 'w_ffn_in', 'w_ffn_out']


def kernel(x, c, norm1_g, norm2_g, w_ada, b_ada, ssm_a_re, ssm_a_im, ssm_log_step, ssm_b_re, ssm_b_im, ssm_c_re, ssm_c_im, ssm_d, ssm_w_out, conv_w_in, conv_w, conv_w_out, w_ffn_in, w_ffn_out, final_g, loss_target, m_norm1_g, m_norm2_g, m_w_ada, m_b_ada, m_ssm_a_re, m_ssm_a_im, m_ssm_log_step, m_ssm_b_re, m_ssm_b_im, m_ssm_c_re, m_ssm_c_im, m_ssm_d, m_ssm_w_out, m_conv_w_in, m_conv_w, m_conv_w_out, m_w_ffn_in, m_w_ffn_out, m_final_g, v_norm1_g, v_norm2_g, v_w_ada, v_b_ada, v_ssm_a_re, v_ssm_a_im, v_ssm_log_step, v_ssm_b_re, v_ssm_b_im, v_ssm_c_re, v_ssm_c_im, v_ssm_d, v_ssm_w_out, v_conv_w_in, v_conv_w, v_conv_w_out, v_w_ffn_in, v_w_ffn_out, v_final_g):
    given = dict(locals())
    W = {n: given[n] for n in WEIGHTS}
    Mo = {n: given["m_" + n] for n in WEIGHTS}
    Vo = {n: given["v_" + n] for n in WEIGHTS}

    xs = x[0]
    tgt = loss_target[0]
    L, D = xs.shape
    nlayer = norm1_g.shape[0]
    NA = w_ada.shape[2]
    G = ssm_a_re.shape[1]
    nkb = D // S5_BLOCK
    ax, ay, ac = _axes()
    me = 4 * ax + 2 * ay + ac
    chip = 2 * ax + ay

    c_all = gather8(jnp.broadcast_to(c, (8, D)), "gather_c")[:, 0, :]
    b_sh = lax.dynamic_slice_in_dim(b_ada, chip * NA, NA, axis=1)[:, None, :]
    mods_part = ada_mods(c_all, w_ada, b_sh, "ada_mods")
    mg = gather8(mods_part.reshape(nlayer * 8, NA), "gather_mods")
    mg = mg.reshape(N_CHIP, 2, nlayer, 8, NA)[:, 0]
    mods_all = lax.dynamic_index_in_dim(mg, me, axis=2, keepdims=False)
    mods_all = jnp.transpose(mods_all, (1, 0, 2)).reshape(nlayer, 6, D)

    cw_parts = gather8(_pack([conv_w]), "gather_conv_w")
    nconv = conv_w.shape[0]
    cw_full = jnp.stack([_unpack(cw_parts[2 * q], [conv_w])[0] for q in range(N_CHIP)], axis=2)
    cw_full = cw_full.reshape(nconv, 3, D)

    in_flight_w = {}

    def start_weights(i, after):
        names = (["ssm_w_out"] if i % 2 == 0 else ["conv_w_in", "conv_w_out"]) + ["w_ffn_in", "w_ffn_out"]
        shards = [W[n][i if n.startswith("w_ffn") else i // 2].astype(BF) for n in names]
        sems, srcs, lands, tok = gather_start(shards, after, "gather_start%d" % i)
        in_flight_w[i] = (names, sems, srcs, lands)
        return tok

    def relay_weights(i, after):
        names, sems, srcs, lands = in_flight_w[i]
        got = gather_wait(sems, srcs, lands, list(range(len(names))), after, "gather_wait%d" % i)
        rsems, rlands, tok = relay_start(got, after, "relay_start%d" % i)
        in_flight_w[i] = (names, rsems, rlands)
        return tok

    def layer_weights(i, after):
        names, rsems, rlands = in_flight_w[i]
        return dict(zip(names, relay_wait(rsems, rlands, after, "relay_wait%d" % i)))

    token = start_weights(0, cw_full + mods_all[0, 0:3])
    mods_all = mods_all + token[0:1, 0:1]

    s5 = []
    for j in range(ssm_a_re.shape[0]):
        disc, disc_vjp = jax.vjp(_discretise, ssm_a_re[j], ssm_a_im[j], ssm_log_step[j], ssm_b_re[j], ssm_b_im[j])
        abr, abi, bbar_re, bbar_im = disc
        tb, tbt = _compact(jnp.swapaxes(bbar_re, 1, 2), jnp.swapaxes(bbar_im, 1, 2), nkb)
        tc, tct = _compact(ssm_c_re[j], -ssm_c_im[j], nkb)
        chunk = _tile(L, (512, 256)) // 8
        s5.append(dict(vjp=disc_vjp, tb=tb, tbt=tbt, tc=tc, tct=tct, pw=_scan_powers(abr, abi, nkb, False, chunk),
                       pwr=_scan_powers(abr, abi, nkb, True, chunk)))

    saved = []
    xcur = xs
    for i in range(nlayer):
        j = i // 2
        mods = mods_all[i]
        sv = dict(x=xcur)
        if i % 2 == 0:
            h = norm_mod(xcur, norm1_g[i:i + 1], mods, 0, F32, "norm_mod_s5")
            dvec = ssm_d[j:j + 1]
            if i == 0:
                dvec = dvec + start_weights(1, h)[0:1, 0:1]
            states, yv, z = s5_fwd(h, s5[j]["tb"], s5[j]["tct"], s5[j]["pw"], dvec, "s5_fwd")
            if i == 0:
                mods = mods + relay_weights(0, z)[0:1, 0:1]
            full = layer_weights(i, z)
            o, mix, x2 = ssm_out_glu(z, full["ssm_w_out"], xcur, mods, 2, "ssm_out_glu")
            sv.update(h=h, states=states, y=yv, z=z, o=o)
        else:
            h = norm_mod(xcur, norm1_g[i:i + 1], mods, 0, BF, "norm_mod")
            full = layer_weights(i, h)
            p = mm_nn(h, full["conv_w_in"], BF, "mm_conv_in")
            mc = conv_fwd(p, cw_full[j], "conv_fwd")
            mix, x2 = mm_nn(mc, full["conv_w_out"].reshape(1, D, D), BF, "mm_conv_out", res=xcur, gate=mods[2:3])
            sv.update(h=h, p=p, mc=mc)
        h2 = norm_mod(x2, norm2_g[i:i + 1], mods, 3, BF, "norm_mod")
        gu, act = ffn_in_act(h2, full["w_ffn_in"], "ffn_in_act")
        if i + 1 < nlayer:
            token = relay_weights(i + 1, act)
            if i + 2 < nlayer:
                token = token + start_weights(i + 2, token)
            mods = mods + token[0:1, 0:1]
        F = act.shape[1]
        ff, x3 = mm_nn(act, full["w_ffn_out"].reshape(1, F, D), BF, "mm_ffn_out", res=x2, gate=mods[5:6])
        sv.update(mix=mix, x2=x2, h2=h2, gu=gu, act=act, ff=ff, w=full)
        saved.append(sv)
        xcur = x3

    loss_blk, dx, dfinal, dff = final_loss(xcur, tgt, final_g[None, :], saved[-1]["ff"], mods_all[nlayer - 1], 5,
                                           "final_loss")
    dg2 = dfinal[1:2]

    gland = {n: lax.empty((W[n].shape[0], N_CHIP) + W[n].shape[1:], BF) for n in SHARDED}
    in_flight = []
    dmods = [None] * nlayer
    dnorm1, dnorm2 = [None] * nlayer, [None] * nlayer
    dconv_w = [None] * nconv
    ds5 = [None] * ssm_a_re.shape[0]
    token = jnp.zeros((8, 128), F32)

    def send_grads(names, grads, slot, after, name):
        sems, thru, lands, tok = scatter_start([grads[n] for n in names], [gland[n] for n in names], slot, after, name)
        gland.update(zip(names, lands))
        in_flight.append((names, slot, sems, thru, name))
        return tok

    def land_grads(group, after):
        for names, slot, sems, thru, name in in_flight:
            if names[0] in group:
                got = scatter_wait(sems, thru, [gland[n] for n in names], slot, after, name.replace("scatter", "landed"))
                gland.update(zip(names, got))

    for i in reversed(range(nlayer)):
        j = i // 2
        mods = mods_all[i] + token[0:1, 0:1]
        sv = saved[i]
        full = sv["w"]
        gfull = {}
        F = sv["act"].shape[1]
        gfull["w_ffn_out"] = mm_tn(sv["act"], dff, 1, "mm_tn_ffn_out").reshape(N_CHIP, F // N_CHIP, D)
        dgu = ffn_out_bwd(dff, full["w_ffn_out"].reshape(F, D), sv["gu"], "ffn_out_bwd")
        gfull["w_ffn_in"] = mm_tn(sv["h2"], dgu, N_CHIP, "mm_tn_ffn_in")
        token = send_grads(["w_ffn_out", "w_ffn_in"], gfull, [i, i], dgu, "scatter_ffn%d" % i)
        mods = mods + token[0:1, 0:1]
        dx2, s2, dmix = mm_nt_norm_bwd(dgu, full["w_ffn_in"], sv["x2"], dx, norm2_g[i:i + 1], mods, 3,
                                       (sv["mix"], mods, 2), "ffn_in_norm_bwd")
        dg1 = s2[3:4]
        if i % 2 == 0:
            do = glu_bwd(dmix, sv["o"], "glu_bwd")
            gfull["ssm_w_out"] = mm_tn(sv["z"], do, N_CHIP, "mm_tn_ssm_out")
            dz = mm_nt(do, full["ssm_w_out"], BF, "mm_nt_ssm_out")
            dh, dd, dab, db, dc = s5_bwd(dz, sv["y"], sv["h"], sv["states"], s5[j]["tc"], s5[j]["tbt"], s5[j]["pwr"],
                                         ssm_d[j:j + 1], "s5_bwd")
            ds5[j] = (dd, dab, db, dc)
        else:
            gfull["conv_w_out"] = mm_tn(sv["mc"], dmix, 1, "mm_tn_conv_out").reshape(N_CHIP, D // N_CHIP, D)
            dmc = mm_nt(dmix, full["conv_w_out"].reshape(1, D, D), BF, "mm_nt_conv_out")
            dbg, dcg, dvv, dcw = conv_bwd(dmc, sv["p"], cw_full[j], "conv_bwd")
            dp = jnp.concatenate([dbg, dcg, dvv], axis=1)
            gfull["conv_w_in"] = mm_tn(sv["h"], dp, N_CHIP, "mm_tn_conv_in")
            dconv_w[j] = dcw[0:3]
        dmods_i = [s2[0:2], dg2]
        if i % 2 == 1:
            dx, s1, dff = mm_nt_norm_bwd(dp, full["conv_w_in"], sv["x"], dx2, norm1_g[i:i + 1], mods, 0,
                                         (saved[i - 1]["ff"], mods_all[i - 1], 5), "conv_in_norm_bwd")
            dg2 = s1[3:4]
        elif i > 0:
            dx, s1, dff = norm_bwd(dh, sv["x"], dx2, norm1_g[i:i + 1], mods, 0, "norm_bwd_ffn",
                                   branch=(saved[i - 1]["ff"], mods_all[i - 1], 5))
            dg2 = s1[3:4]
        else:
            dx, s1 = norm_bwd(dh, sv["x"], dx2, norm1_g[i:i + 1], mods, 0, "norm_bwd")
        dmods[i] = jnp.concatenate([s1[0:2], dg1] + dmods_i, axis=0).reshape(6 * D)
        dnorm1[i], dnorm2[i] = s1[2], s2[2]
        names = ["ssm_w_out"] if i % 2 == 0 else ["conv_w_out", "conv_w_in"]
        token = send_grads(names, gfull, [j] * len(names), dx, "scatter_mix%d" % i)

    small = dict(norm1_g=jnp.stack(dnorm1), norm2_g=jnp.stack(dnorm2), b_ada=jnp.stack(dmods), final_g=dfinal[0])
    per = {n: [] for n in ('ssm_a_re', 'ssm_a_im', 'ssm_log_step', 'ssm_b_re', 'ssm_b_im', 'ssm_c_re', 'ssm_c_im', 'ssm_d')}
    GL = G // nkb
    for j, (dd, dab, db, dc) in enumerate(ds5):
        dab = jnp.sum(dab, axis=1).reshape(nkb, 2, GL, SSM_STATE)
        g_abr, g_abi = dab[:, 0].reshape(G, SSM_STATE), dab[:, 1].reshape(G, SSM_STATE)
        db, dc = db.reshape(G, SSM_GROUP, 2, SSM_STATE), dc.reshape(G, SSM_GROUP, 2, SSM_STATE)
        gb_re, gb_im, gc_re, gc_im = db[:, :, 0], db[:, :, 1], dc[:, :, 0], dc[:, :, 1]
        ga_re, ga_im, gls, gbr, gbi = s5[j]["vjp"]((g_abr, g_abi, jnp.swapaxes(gb_re, 1, 2), jnp.swapaxes(gb_im, 1, 2)))
        for n, val in zip(per, (ga_re, ga_im, gls, gbr, gbi, gc_re, -gc_im, jnp.sum(dd, axis=0))):
            per[n].append(val)
    small.update({n: jnp.stack(vals) for n, vals in per.items()})
    dcw_full = jnp.stack(dconv_w)

    slab_like = [W[n] for n in SLAB] + [dcw_full]
    rows64 = 8 * N_DEV
    slab = _pack([small[n] for n in SLAB] + [dcw_full], rows64)
    per_dev = slab.shape[0] // N_DEV
    x_sems, x_srcs, x_lands, token = exchange_start(
        [(slab.reshape(N_DEV, per_dev, SLAB_W), True), (_pack([small["b_ada"]]), False)], dx, "small_scatter")

    early = [n for n in SHARDED if n != "ssm_w_out"]
    land_grads(early, token)
    mine = [reduce4(gland[n], "reduce4_" + n) for n in early]

    parts, dm_all = exchange_wait(x_sems, x_srcs, x_lands, [True, False], mine[-1][0, :8, :128], "small_landed")
    t_sems, t_srcs, t_lands, token = exchange_start([(sum8(parts, "sum_small"), False)], dm_all, "small_gather")
    out = {}

    w_sems, w_srcs, w_lands, token2 = swap_start(mine, "swap_start")
    dm_all = dm_all.reshape(N_DEV, -1)[:, :b_ada.size].reshape(N_DEV, nlayer, N_CHIP, NA)
    dm_sh = jnp.transpose(lax.dynamic_index_in_dim(dm_all, chip, axis=2, keepdims=False), (1, 0, 2))
    res = adamw_ada(jnp.transpose(c_all) + token[0:1, 0:1] + token2[0:1, 0:1], dm_sh, w_ada, m_w_ada, v_w_ada,
                    "adamw_ada")
    out["g", "w_ada"], out["d", "w_ada"], out["m", "w_ada"], out["v", "w_ada"] = res

    g_slab = exchange_wait(t_sems, t_srcs, t_lands, [False], out["g", "w_ada"], "small_total")[0]
    g_slab = g_slab.reshape(slab.shape)
    d_slab, m_slab, v_slab = adamw_slab(
        g_slab, _pack([W[n] for n in SLAB] + [jnp.zeros_like(dcw_full)], rows64),
        _pack([Mo[n] for n in SLAB] + [jnp.zeros_like(dcw_full)], rows64),
        _pack([Vo[n] for n in SLAB] + [jnp.ones_like(dcw_full)], rows64), "adamw_slab")
    for k, slab in zip(("g", "d", "m", "v"), (g_slab, d_slab, m_slab, v_slab)):
        for n, val in zip(SLAB, _unpack(slab, slab_like)):
            out[k, n] = val
    g_cw = lax.dynamic_slice_in_dim(_unpack(g_slab, slab_like)[-1], chip * conv_w.shape[2], conv_w.shape[2], axis=2)
    out["g", "conv_w"] = g_cw
    out["d", "conv_w"], out["m", "conv_w"], out["v", "conv_w"] = [
        r.reshape(conv_w.shape) for r in adamw_plain(conv_w.reshape(-1, conv_w.shape[2]), m_conv_w.reshape(-1, conv_w.shape[2]),
                                                     v_conv_w.reshape(-1, conv_w.shape[2]), g_cw.reshape(-1, conv_w.shape[2]),
                                                     "adamw_conv_w")]

    mine, theirs = swap_wait(w_sems, w_srcs, w_lands, d_slab, "swap_wait")
    for n, ga, gb in zip(early, mine, theirs):
        r = adamw_sharded(W[n], Mo[n], Vo[n], ga, gb, "adamw_" + n)
        out["g", n], out["d", n], out["m", n], out["v", n] = r

    land_grads(["ssm_w_out"], out["g", "w_ffn_out"])
    ga = reduce4(gland["ssm_w_out"], "reduce4_ssm_w_out")
    gb = swap_siblings([ga], "swap_siblings")[0]
    r = adamw_sharded(ssm_w_out, m_ssm_w_out, v_ssm_w_out, ga, gb, "adamw_ssm_w_out")
    out["g", "ssm_w_out"], out["d", "ssm_w_out"], out["m", "ssm_w_out"], out["v", "ssm_w_out"] = r

    loss = lax.psum(loss_blk[0, 0], ("x", "y", "c"))
    return (loss, dx[None], *[out["g", n] for n in WEIGHTS], *[out["d", n] for n in WEIGHTS],
            *[out["m", n] for n in WEIGHTS], *[out["v", n] for n in WEIGHTS])
```

```python
import functools
import math

import jax
import jax.numpy as jnp
from jax import lax
from jax.experimental import pallas as pl
from jax.experimental.pallas import tpu as pltpu

F32 = jnp.float32
BF = jnp.bfloat16
MESH = pl.DeviceIdType.MESH
ANY = pl.BlockSpec(memory_space=pl.ANY)

N_DEV = 8
N_CHIP = 4
DEPTH = 4
SSM_GROUP = 16
SSM_STATE = 64
S5_BLOCK = 256
RMS_EPS = 1e-6
ADAM_LR, ADAM_B1, ADAM_B2, ADAM_EPS, ADAM_WD, ADAM_STEP = 0.001, 0.9, 0.999, 1e-08, 0.01, 10
V7X_VMEM_BYTES = 64 * 1024 * 1024
VMEM_LIMIT = V7X_VMEM_BYTES - 12 * 1024 * 1024
SLAB_W = 1024
GELU_C = math.sqrt(2.0 / math.pi)
GELU_A = 0.044715


def _cp(*sem):
    return pltpu.CompilerParams(dimension_semantics=sem if sem else None, vmem_limit_bytes=VMEM_LIMIT)


def _tile(n, prefs):
    for p in prefs:
        if p <= n and n % p == 0:
            return p
    return n


def _axes():
    return lax.axis_index("x"), lax.axis_index("y"), lax.axis_index("c")


def _flip(v, k):
    return 1 - v if k else v


def gather8(v, name):
    R, C = v.shape

    def body(v_ref, o_ref, ssem, rsem, lsem):
        x, y, c = _axes()
        me = 4 * x + 2 * y + c
        loc = pltpu.make_async_copy(v_ref, o_ref.at[me], lsem)
        loc.start()
        copies = []
        for k in range(1, N_DEV):
            peer = (_flip(x, (k >> 2) & 1), _flip(y, (k >> 1) & 1), _flip(c, k & 1))
            cp = pltpu.make_async_remote_copy(src_ref=v_ref, dst_ref=o_ref.at[me], send_sem=ssem.at[k - 1],
                                              recv_sem=rsem.at[k - 1], device_id=peer, device_id_type=MESH)
            cp.start()
            copies.append(cp)
        for cp in copies:
            cp.wait()
        loc.wait()

    return pl.pallas_call(
        body, name=name,
        out_shape=jax.ShapeDtypeStruct((N_DEV, R, C), v.dtype),
        in_specs=[pl.BlockSpec(memory_space=pltpu.VMEM)],
        out_specs=pl.BlockSpec(memory_space=pltpu.VMEM),
        scratch_shapes=[pltpu.SemaphoreType.DMA((N_DEV - 1,)), pltpu.SemaphoreType.DMA((N_DEV - 1,)),
                        pltpu.SemaphoreType.DMA],
        compiler_params=pltpu.CompilerParams(vmem_limit_bytes=VMEM_LIMIT),
    )(v)


HBM = pl.BlockSpec(memory_space=pltpu.HBM)
SEM = pl.BlockSpec(memory_space=pltpu.SEMAPHORE)
EFFECT = pltpu.SideEffectType.DATAFLOW_SIDE_EFFECTING


def _in_hbm(a):
    return pltpu.with_memory_space_constraint(a, pltpu.HBM)


def _chip_peers(x, y, c):
    out = []
    for k in range(1, N_CHIP):
        px, py = _flip(x, k >> 1), _flip(y, k & 1)
        out.append(((px, py, c), 2 * px + py))
    return out


def _my_half(ref, c):
    rows = ref.shape[0] // 2
    return pl.ds(pl.multiple_of(c * rows, 16), rows)


def relay_start(lands, after, name):
    n = len(lands)

    def body(*refs):
        land = refs[:n]
        ssem, rsem = refs[n + 1:n + 3]
        token = refs[-1]
        x, y, c = _axes()
        for a in range(n):
            half = _my_half(land[a].at[0], c)
            for k, (_, pchip) in enumerate(_chip_peers(x, y, c)):
                pltpu.make_async_remote_copy(src_ref=land[a].at[pchip, half], dst_ref=land[a].at[pchip, half],
                                             send_sem=ssem.at[3 * a + k], recv_sem=rsem.at[3 * a + k],
                                             device_id=(x, y, 1 - c), device_id_type=MESH).start()
        token[...] = jnp.zeros_like(token)

    out_shape = ([pltpu.SemaphoreType.DMA((3 * n,)), pltpu.SemaphoreType.DMA((3 * n,))]
                 + [pltpu.HBM(l.shape, l.dtype) for l in lands] + [jax.ShapeDtypeStruct((8, 128), F32)])
    res = pl.pallas_call(
        body, name=name, out_shape=out_shape, in_specs=[HBM] * n + [ANY],
        out_specs=[SEM, SEM] + [HBM] * n + [pl.BlockSpec(memory_space=pltpu.VMEM)],
        input_output_aliases={a: 2 + a for a in range(n)},
        compiler_params=pltpu.CompilerParams(has_side_effects=EFFECT),
    )(*lands, after)
    return tuple(res[:2]), list(res[2:2 + n]), res[-1]


def relay_wait(sems, lands, after, name):
    n = len(lands)

    def body(*refs):
        land = refs[:n]
        ssem, rsem = refs[n:n + 2]
        x, y, c = _axes()
        for a in range(n):
            mine, theirs = _my_half(land[a].at[0], c), _my_half(land[a].at[0], 1 - c)
            for k, (_, pchip) in enumerate(_chip_peers(x, y, c)):
                cp = pltpu.make_async_remote_copy(src_ref=land[a].at[pchip, mine], dst_ref=land[a].at[pchip, theirs],
                                                  send_sem=ssem.at[3 * a + k], recv_sem=rsem.at[3 * a + k],
                                                  device_id=(x, y, 1 - c), device_id_type=MESH)
                cp.wait_send()
                cp.wait_recv()

    res = pl.pallas_call(
        body, name=name, out_shape=[pltpu.HBM(l.shape, l.dtype) for l in lands],
        in_specs=[HBM] * n + [SEM, SEM, ANY], out_specs=[HBM] * n,
        input_output_aliases={a: a for a in range(n)},
        compiler_params=pltpu.CompilerParams(has_side_effects=EFFECT),
    )(*lands, *sems, after)
    return list(res)


def gather_start(shards, after, name):
    n = len(shards)

    def body(*refs):
        src, land = refs[:n], refs[n:2 * n]
        ssem, rsem, lsem = refs[2 * n + 1:2 * n + 4]
        token = refs[-1]
        x, y, c = _axes()
        chip = 2 * x + y
        for a in range(n):
            pltpu.make_async_copy(src[a], land[a].at[chip], lsem.at[a]).start()
            half = _my_half(src[a], c)
            for k, (peer, _) in enumerate(_chip_peers(x, y, c)):
                pltpu.make_async_remote_copy(src_ref=src[a].at[half], dst_ref=land[a].at[chip, half],
                                             send_sem=ssem.at[3 * a + k], recv_sem=rsem.at[3 * a + k],
                                             device_id=peer, device_id_type=MESH).start()
        token[...] = jnp.zeros_like(token)

    lands = [lax.empty((N_CHIP,) + s.shape, s.dtype) for s in shards]
    out_shape = ([pltpu.SemaphoreType.DMA((3 * n,)), pltpu.SemaphoreType.DMA((3 * n,)), pltpu.SemaphoreType.DMA((n,))]
                 + [pltpu.HBM(s.shape, s.dtype) for s in shards] + [pltpu.HBM(l.shape, l.dtype) for l in lands]
                 + [jax.ShapeDtypeStruct((8, 128), F32)])
    res = pl.pallas_call(
        body, name=name, out_shape=out_shape, in_specs=[HBM] * (2 * n) + [ANY],
        out_specs=[SEM, SEM, SEM] + [HBM] * (2 * n) + [pl.BlockSpec(memory_space=pltpu.VMEM)],
        input_output_aliases={a: 3 + a for a in range(2 * n)},
        compiler_params=pltpu.CompilerParams(has_side_effects=EFFECT),
    )(*[_in_hbm(s) for s in shards], *[_in_hbm(l) for l in lands], after)
    return tuple(res[:3]), list(res[3:3 + n]), list(res[3 + n:3 + 2 * n]), res[-1]


def gather_wait(sems, srcs, lands, idx, after, name):
    m = len(idx)

    def body(*refs):
        src, land = refs[:m], refs[m:2 * m]
        ssem, rsem, lsem = refs[2 * m:2 * m + 3]
        x, y, c = _axes()
        chip = 2 * x + y
        for j, a in enumerate(idx):
            half = _my_half(src[j], c)
            for k, (peer, pchip) in enumerate(_chip_peers(x, y, c)):
                cp = pltpu.make_async_remote_copy(src_ref=src[j].at[half], dst_ref=land[j].at[pchip, half],
                                                  send_sem=ssem.at[3 * a + k], recv_sem=rsem.at[3 * a + k],
                                                  device_id=peer, device_id_type=MESH)
                cp.wait_send()
                cp.wait_recv()
            pltpu.make_async_copy(src[j], land[j].at[chip], lsem.at[a]).wait()

    s_in = [srcs[a] for a in idx]
    l_in = [lands[a] for a in idx]
    res = pl.pallas_call(
        body, name=name,
        out_shape=[pltpu.HBM(s.shape, s.dtype) for s in s_in] + [pltpu.HBM(l.shape, l.dtype) for l in l_in],
        in_specs=[HBM] * (2 * m) + [SEM, SEM, SEM, ANY], out_specs=[HBM] * (2 * m),
        input_output_aliases={a: a for a in range(2 * m)},
        compiler_params=pltpu.CompilerParams(has_side_effects=EFFECT),
    )(*s_in, *l_in, *sems, after)
    return list(res[m:])


def scatter_start(grads, lands, slot, after, name):
    n = len(grads)

    def body(*refs):
        src, land = refs[:n], refs[n:2 * n]
        ssem, rsem, lsem = refs[2 * n + 1:2 * n + 4]
        token = refs[-1]
        x, y, c = _axes()
        chip = 2 * x + y
        for a in range(n):
            pltpu.make_async_copy(src[a].at[chip], land[a].at[slot[a], chip], lsem.at[a]).start()
            for k, (peer, pchip) in enumerate(_chip_peers(x, y, c)):
                pltpu.make_async_remote_copy(src_ref=src[a].at[pchip], dst_ref=land[a].at[slot[a], chip],
                                             send_sem=ssem.at[3 * a + k], recv_sem=rsem.at[3 * a + k],
                                             device_id=peer, device_id_type=MESH).start()
        token[...] = jnp.zeros_like(token)

    out_shape = ([pltpu.SemaphoreType.DMA((3 * n,)), pltpu.SemaphoreType.DMA((3 * n,)), pltpu.SemaphoreType.DMA((n,))]
                 + [pltpu.HBM(g.shape, g.dtype) for g in grads] + [pltpu.HBM(l.shape, l.dtype) for l in lands]
                 + [jax.ShapeDtypeStruct((8, 128), F32)])
    res = pl.pallas_call(
        body, name=name, out_shape=out_shape, in_specs=[HBM] * (2 * n) + [ANY],
        out_specs=[SEM, SEM, SEM] + [HBM] * (2 * n) + [pl.BlockSpec(memory_space=pltpu.VMEM)],
        input_output_aliases={a: 3 + a for a in range(2 * n)},
        compiler_params=pltpu.CompilerParams(has_side_effects=EFFECT),
    )(*[_in_hbm(g) for g in grads], *[_in_hbm(l) for l in lands], after)
    return tuple(res[:3]), list(res[3:3 + n]), list(res[3 + n:3 + 2 * n]), res[-1]


def scatter_wait(sems, grads, lands, slot, after, name):
    n = len(grads)

    def body(*refs):
        src, land = refs[:n], refs[n:2 * n]
        ssem, rsem, lsem = refs[2 * n:2 * n + 3]
        x, y, c = _axes()
        chip = 2 * x + y
        for a in range(n):
            for k, (peer, pchip) in enumerate(_chip_peers(x, y, c)):
                cp = pltpu.make_async_remote_copy(src_ref=src[a].at[pchip], dst_ref=land[a].at[slot[a], pchip],
                                                  send_sem=ssem.at[3 * a + k], recv_sem=rsem.at[3 * a + k],
                                                  device_id=peer, device_id_type=MESH)
                cp.wait_send()
                cp.wait_recv()
            pltpu.make_async_copy(src[a].at[chip], land[a].at[slot[a], chip], lsem.at[a]).wait()

    res = pl.pallas_call(
        body, name=name,
        out_shape=[pltpu.HBM(g.shape, g.dtype) for g in grads] + [pltpu.HBM(l.shape, l.dtype) for l in lands],
        in_specs=[HBM] * (2 * n) + [SEM, SEM, SEM, ANY], out_specs=[HBM] * (2 * n),
        input_output_aliases={a: a for a in range(2 * n)},
        compiler_params=pltpu.CompilerParams(has_side_effects=EFFECT),
    )(*grads, *lands, *sems, after)
    return list(res[n:])


def reduce4(land, name):
    nl, _, R, C = land.shape
    TR = _adam_rows(R, C)

    def body(l_ref, o_ref):
        o_ref[...] = ((l_ref[0].astype(F32) + l_ref[1].astype(F32)) + l_ref[2].astype(F32)) + l_ref[3].astype(F32)

    return pl.pallas_call(
        body, name=name, grid=(nl, R // TR),
        in_specs=[pl.BlockSpec((None, N_CHIP, TR, C), lambda i, r: (i, 0, r, 0))],
        out_specs=pl.BlockSpec((None, TR, C), lambda i, r: (i, r, 0)),
        out_shape=jax.ShapeDtypeStruct((nl, R, C), F32), compiler_params=_cp("parallel", "parallel"))(land)


def swap_siblings(arrs, name):
    n = len(arrs)

    def body(*refs):
        src, dst = refs[:n], refs[n:2 * n]
        ssem, rsem = refs[2 * n:]
        x, y, c = _axes()
        cps = [pltpu.make_async_remote_copy(src_ref=src[a], dst_ref=dst[a], send_sem=ssem.at[a], recv_sem=rsem.at[a],
                                            device_id=(x, y, 1 - c), device_id_type=MESH) for a in range(n)]
        for cp in cps:
            cp.start()
        for cp in cps:
            cp.wait()

    return pl.pallas_call(
        body, name=name, out_shape=[jax.ShapeDtypeStruct(a.shape, a.dtype) for a in arrs],
        in_specs=[ANY] * n, out_specs=[ANY] * n,
        scratch_shapes=[pltpu.SemaphoreType.DMA((n,)), pltpu.SemaphoreType.DMA((n,))],
        compiler_params=pltpu.CompilerParams(vmem_limit_bytes=VMEM_LIMIT),
    )(*arrs)


def swap_start(arrs, name):
    n = len(arrs)

    def body(*refs):
        src, land = refs[:n], refs[n:2 * n]
        ssem, rsem = refs[2 * n:2 * n + 2]
        token = refs[-1]
        x, y, c = _axes()
        for a in range(n):
            pltpu.make_async_remote_copy(src_ref=src[a], dst_ref=land[a], send_sem=ssem.at[a], recv_sem=rsem.at[a],
                                         device_id=(x, y, 1 - c), device_id_type=MESH).start()
        token[...] = jnp.zeros_like(token)

    lands = [lax.empty(a.shape, a.dtype) for a in arrs]
    out_shape = ([pltpu.SemaphoreType.DMA((n,)), pltpu.SemaphoreType.DMA((n,))]
                 + [pltpu.HBM(a.shape, a.dtype) for a in arrs] * 2 + [jax.ShapeDtypeStruct((8, 128), F32)])
    res = pl.pallas_call(
        body, name=name, out_shape=out_shape, in_specs=[HBM] * (2 * n),
        out_specs=[SEM, SEM] + [HBM] * (2 * n) + [pl.BlockSpec(memory_space=pltpu.VMEM)],
        input_output_aliases={a: 2 + a for a in range(2 * n)},
        compiler_params=pltpu.CompilerParams(has_side_effects=EFFECT),
    )(*[_in_hbm(a) for a in arrs], *[_in_hbm(l) for l in lands])
    return tuple(res[:2]), list(res[2:2 + n]), list(res[2 + n:2 + 2 * n]), res[-1]


def swap_wait(sems, srcs, lands, after, name):
    n = len(srcs)

    def body(*refs):
        src, land = refs[:n], refs[n:2 * n]
        ssem, rsem = refs[2 * n:2 * n + 2]
        x, y, c = _axes()
        for a in range(n):
            cp = pltpu.make_async_remote_copy(src_ref=src[a], dst_ref=land[a], send_sem=ssem.at[a],
                                              recv_sem=rsem.at[a], device_id=(x, y, 1 - c), device_id_type=MESH)
            cp.wait_send()
            cp.wait_recv()

    res = pl.pallas_call(
        body, name=name, out_shape=[pltpu.HBM(a.shape, a.dtype) for a in srcs] * 2,
        in_specs=[HBM] * (2 * n) + [SEM, SEM, ANY], out_specs=[HBM] * (2 * n),
        input_output_aliases={a: a for a in range(2 * n)},
        compiler_params=pltpu.CompilerParams(has_side_effects=EFFECT),
    )(*srcs, *lands, *sems, after)
    return list(res[:n]), list(res[n:])


def _all_peers(x, y, c):
    out = []
    for k in range(1, N_DEV):
        px, py, pc = _flip(x, (k >> 2) & 1), _flip(y, (k >> 1) & 1), _flip(c, k & 1)
        out.append(((px, py, pc), 4 * px + 2 * py + pc))
    return out


def exchange_start(items, after, name):
    n = len(items)

    def body(*refs):
        src, land = refs[:n], refs[n:2 * n]
        ssem, rsem, lsem = refs[2 * n + 1:2 * n + 4]
        token = refs[-1]
        x, y, c = _axes()
        me = 4 * x + 2 * y + c
        for a, (_, scatter) in enumerate(items):
            pltpu.make_async_copy(src[a].at[me] if scatter else src[a], land[a].at[me], lsem.at[a]).start()
            for k, (peer, p) in enumerate(_all_peers(x, y, c)):
                pltpu.make_async_remote_copy(src_ref=src[a].at[p] if scatter else src[a], dst_ref=land[a].at[me],
                                             send_sem=ssem.at[7 * a + k], recv_sem=rsem.at[7 * a + k],
                                             device_id=peer, device_id_type=MESH).start()
        token[...] = jnp.zeros_like(token)

    srcs = [s for s, _ in items]
    lands = [lax.empty(s.shape if sc else (N_DEV,) + s.shape, s.dtype) for s, sc in items]
    out_shape = ([pltpu.SemaphoreType.DMA((7 * n,)), pltpu.SemaphoreType.DMA((7 * n,)), pltpu.SemaphoreType.DMA((n,))]
                 + [pltpu.HBM(s.shape, s.dtype) for s in srcs] + [pltpu.HBM(l.shape, l.dtype) for l in lands]
                 + [jax.ShapeDtypeStruct((8, 128), F32)])
    res = pl.pallas_call(
        body, name=name, out_shape=out_shape, in_specs=[HBM] * (2 * n) + [ANY],
        out_specs=[SEM, SEM, SEM] + [HBM] * (2 * n) + [pl.BlockSpec(memory_space=pltpu.VMEM)],
        input_output_aliases={a: 3 + a for a in range(2 * n)},
        compiler_params=pltpu.CompilerParams(has_side_effects=EFFECT),
    )(*[_in_hbm(s) for s in srcs], *[_in_hbm(l) for l in lands], after)
    return tuple(res[:3]), list(res[3:3 + n]), list(res[3 + n:3 + 2 * n]), res[-1]


def exchange_wait(sems, srcs, lands, scatter, after, name):
    n = len(srcs)

    def body(*refs):
        src, land = refs[:n], refs[n:2 * n]
        ssem, rsem, lsem = refs[2 * n:2 * n + 3]
        x, y, c = _axes()
        me = 4 * x + 2 * y + c
        for a in range(n):
            for k, (peer, p) in enumerate(_all_peers(x, y, c)):
                cp = pltpu.make_async_remote_copy(src_ref=src[a].at[p] if scatter[a] else src[a],
                                                  dst_ref=land[a].at[p], send_sem=ssem.at[7 * a + k],
                                                  recv_sem=rsem.at[7 * a + k], device_id=peer, device_id_type=MESH)
                cp.wait_send()
                cp.wait_recv()
            pltpu.make_async_copy(src[a].at[me] if scatter[a] else src[a], land[a].at[me], lsem.at[a]).wait()

    res = pl.pallas_call(
        body, name=name,
        out_shape=[pltpu.HBM(s.shape, s.dtype) for s in srcs] + [pltpu.HBM(l.shape, l.dtype) for l in lands],
        in_specs=[HBM] * (2 * n) + [SEM, SEM, SEM, ANY], out_specs=[HBM] * (2 * n),
        input_output_aliases={a: a for a in range(2 * n)},
        compiler_params=pltpu.CompilerParams(has_side_effects=EFFECT),
    )(*srcs, *lands, *sems, after)
    return list(res[n:])


def sum8(parts, name):
    _, P, C = parts.shape

    def body(p_ref, o_ref):
        tot = p_ref[0]
        for d in range(1, N_DEV):
            tot = tot + p_ref[d]
        o_ref[...] = tot

    return pl.pallas_call(body, name=name, out_shape=jax.ShapeDtypeStruct((P, C), F32),
                          compiler_params=pltpu.CompilerParams(vmem_limit_bytes=VMEM_LIMIT))(parts)


def reduce8(slab, dm, name):
    RT, C = slab.shape
    P = RT // N_DEV
    R = dm.shape[0]

    def body(s_ref, dm_ref, o_ref, dmo_ref, recv, s1, r1, s2, r2, s3, r3):
        x, y, c = _axes()
        me = 4 * x + 2 * y + c
        mine = pl.ds(pl.multiple_of(me * P, 8), P)
        parts, dms = [], []
        for k in range(1, N_DEV):
            px, py, pc = _flip(x, (k >> 2) & 1), _flip(y, (k >> 1) & 1), _flip(c, k & 1)
            theirs = pl.ds(pl.multiple_of((4 * px + 2 * py + pc) * P, 8), P)
            cp = pltpu.make_async_remote_copy(src_ref=s_ref.at[theirs], dst_ref=recv.at[me], send_sem=s1.at[k - 1],
                                              recv_sem=r1.at[k - 1], device_id=(px, py, pc), device_id_type=MESH)
            cp.start()
            parts.append(cp)
            cd = pltpu.make_async_remote_copy(src_ref=dm_ref, dst_ref=dmo_ref.at[me], send_sem=s3.at[k - 1],
                                              recv_sem=r3.at[k - 1], device_id=(px, py, pc), device_id_type=MESH)
            cd.start()
            dms.append(cd)
        dmo_ref[me] = dm_ref[...]
        recv[me] = s_ref[mine, :]
        for cp in parts:
            cp.wait()
        tot = recv[0]
        for d in range(1, N_DEV):
            tot = tot + recv[d]
        o_ref[mine, :] = tot
        out = []
        for k in range(1, N_DEV):
            peer = (_flip(x, (k >> 2) & 1), _flip(y, (k >> 1) & 1), _flip(c, k & 1))
            cp = pltpu.make_async_remote_copy(src_ref=o_ref.at[mine], dst_ref=o_ref.at[mine], send_sem=s2.at[k - 1],
                                              recv_sem=r2.at[k - 1], device_id=peer, device_id_type=MESH)
            cp.start()
            out.append(cp)
        for cp in out + dms:
            cp.wait()

    sems = [pltpu.SemaphoreType.DMA((N_DEV - 1,))] * 6
    return pl.pallas_call(
        body, name=name,
        out_shape=[jax.ShapeDtypeStruct((RT, C), F32), jax.ShapeDtypeStruct((N_DEV, R, C), F32)],
        in_specs=[pl.BlockSpec(memory_space=pltpu.VMEM)] * 2, out_specs=[pl.BlockSpec(memory_space=pltpu.VMEM)] * 2,
        scratch_shapes=[pltpu.VMEM((N_DEV, P, C), F32)] + sems,
        compiler_params=pltpu.CompilerParams(vmem_limit_bytes=VMEM_LIMIT),
    )(slab, dm)


def mm_nn(a, w, out_dtype, name, res=None, gate=None):
    M, K = a.shape
    S, _, Ns = w.shape
    TM = _tile(M, (1024, 512, 256) if K <= 1024 else (512, 256))
    TN = _tile(Ns, (1408, 1024, 768, 512, 256, 128))
    nj = Ns // TN
    fused = res is not None

    def body(*refs):
        if fused:
            a_ref, w_ref, r_ref, g_ref, f_ref, o_ref = refs
        else:
            a_ref, w_ref, f_ref = refs
        f = jnp.dot(a_ref[...], w_ref[...], preferred_element_type=F32)
        f_ref[...] = f.astype(f_ref.dtype)
        if fused:
            o_ref[...] = r_ref[...] + g_ref[...] * f

    col = lambda s, j, i: (i, s * nj + j)
    in_specs = [pl.BlockSpec((TM, K), lambda s, j, i: (i, 0)), pl.BlockSpec((None, K, TN), lambda s, j, i: (s, 0, j))]
    out_specs = [pl.BlockSpec((TM, TN), col)]
    out_shape = [jax.ShapeDtypeStruct((M, S * Ns), out_dtype)]
    args = [a, w]
    if fused:
        in_specs += [pl.BlockSpec((TM, TN), col), pl.BlockSpec((1, TN), lambda s, j, i: (0, s * nj + j))]
        out_specs.append(pl.BlockSpec((TM, TN), col))
        out_shape.append(jax.ShapeDtypeStruct((M, S * Ns), F32))
        args += [res, gate]
    out = pl.pallas_call(body, name=name, grid=(S, nj, M // TM), in_specs=in_specs, out_specs=out_specs,
                         out_shape=out_shape, compiler_params=_cp("parallel", "parallel", "parallel"))(*args)
    return tuple(out) if fused else out[0]


def mm_nt(g, w, out_dtype, name):
    g3 = g if g.ndim == 3 else g[None]
    Q, M, F = g3.shape
    S, K, Ns = w.shape
    TM = _tile(M, (1024, 512, 256) if K <= 1024 else (512, 256))
    TN = _tile(Ns, (1408, 1024, 768, 512, 256, 128))
    nj = Ns // TN
    nred = S * nj
    per_part = F // TN

    def body(g_ref, w_ref, o_ref, acc):
        n = pl.program_id(1)

        @pl.when(n == 0)
        def _():
            acc[...] = jnp.zeros_like(acc)

        acc[...] += lax.dot_general(g_ref[...], w_ref[...], (((1,), (1,)), ((), ())), preferred_element_type=F32)

        @pl.when(n == nred - 1)
        def _():
            o_ref[...] = acc[...].astype(o_ref.dtype)

    return pl.pallas_call(
        body, name=name, grid=(M // TM, nred),
        in_specs=[pl.BlockSpec((None, TM, TN), lambda i, n: (n // per_part, i, n % per_part)),
                  pl.BlockSpec((None, K, TN), lambda i, n: (n // nj, 0, n % nj))],
        out_specs=pl.BlockSpec((TM, K), lambda i, n: (i, 0)),
        out_shape=jax.ShapeDtypeStruct((M, K), out_dtype),
        scratch_shapes=[pltpu.VMEM((TM, K), F32)],
        compiler_params=_cp("parallel", "arbitrary"))(g3, w)


def mm_nt_norm_bwd(g, w, x, dres, gamma, mods, k_shift, branch, name):
    g3 = g if g.ndim == 3 else g[None]
    Q, M, F = g3.shape
    S, K, Ns = w.shape
    TM = _tile(M, (512, 256))
    TN = _tile(Ns, (1408, 1024, 768, 512, 256, 128))
    nj = Ns // TN
    nred = S * nj
    per_part = F // TN
    nm = M // TM
    f, fmods, k_gate = branch

    def body(g_ref, w_ref, x_ref, dr_ref, gam_ref, m_ref, f_ref, fm_ref, dx_ref, s_ref, df_ref, acc, sums):
        i, n = pl.program_id(0), pl.program_id(1)

        @pl.when(n == 0)
        def _():
            acc[...] = jnp.zeros_like(acc)

        @pl.when((n == 0) & (i == 0))
        def _():
            sums[...] = jnp.zeros_like(sums)

        acc[...] += lax.dot_general(g_ref[...], w_ref[...], (((1,), (1,)), ((), ())), preferred_element_type=F32)

        @pl.when(n == nred - 1)
        def _():
            xn, r = _norm_parts(x_ref[...])
            dh_v = acc[...]
            gam = gam_ref[...]
            sc = m_ref[k_shift + 1:k_shift + 2, :]
            dn = dh_v * (1.0 + sc)
            dxn = dn * gam
            dx = dr_ref[...] + r * (dxn - xn * jnp.mean(dxn * xn, axis=-1, keepdims=True))
            dx_ref[...] = dx
            df_ref[...] = (dx * fm_ref[k_gate:k_gate + 1, :]).astype(df_ref.dtype)
            sums[0] += _rowsum8(dh_v)
            sums[1] += _rowsum8(dh_v * (xn * gam))
            sums[2] += _rowsum8(dn * xn)
            sums[3] += _rowsum8(dx * f_ref[...].astype(F32))

        @pl.when((n == nred - 1) & (i == nm - 1))
        def _():
            s_ref[...] = jnp.zeros_like(s_ref)
            for q in range(4):
                s_ref[q:q + 1, :] = jnp.sum(sums[q], axis=0, keepdims=True)

    rows = pl.BlockSpec((TM, K), lambda i, n: (i, 0))
    fixed = lambda r: pl.BlockSpec((r, K), lambda i, n: (0, 0))
    return pl.pallas_call(
        body, name=name, grid=(nm, nred),
        in_specs=[pl.BlockSpec((None, TM, TN), lambda i, n: (n // per_part, i, n % per_part)),
                  pl.BlockSpec((None, K, TN), lambda i, n: (n // nj, 0, n % nj)),
                  rows, rows, fixed(1), fixed(6), rows, fixed(6)],
        out_specs=[rows, fixed(8), rows],
        out_shape=[jax.ShapeDtypeStruct((M, K), F32), jax.ShapeDtypeStruct((8, K), F32),
                   jax.ShapeDtypeStruct((M, K), BF)],
        scratch_shapes=[pltpu.VMEM((TM, K), F32), pltpu.VMEM((4, 8, K), F32)],
        compiler_params=_cp("arbitrary", "arbitrary"))(g3, w, x, dres, gamma, mods, f, fmods)


def mm_tn(a, g, S, name):
    M, K = a.shape
    g3 = g if g.ndim == 3 else g[None]
    Q, _, F = g3.shape
    Ns = Q * F // S
    TK = _tile(K, (256, 128))
    TN = _tile(Ns, (1408, 1024, 768, 512, 256, 128))
    nj = Ns // TN
    per_part = F // TN

    def body(a_ref, g_ref, o_ref):
        o_ref[...] = lax.dot_general(a_ref[...], g_ref[...], (((0,), (0,)), ((), ())),
                                     preferred_element_type=F32).astype(o_ref.dtype)

    return pl.pallas_call(
        body, name=name, grid=(S * nj, K // TK),
        in_specs=[pl.BlockSpec((M, TK), lambda n, k: (0, k)),
                  pl.BlockSpec((None, M, TN), lambda n, k: (n // per_part, 0, n % per_part))],
        out_specs=pl.BlockSpec((None, TK, TN), lambda n, k: (n // nj, k, n % nj)),
        out_shape=jax.ShapeDtypeStruct((S, K, Ns), BF),
        compiler_params=_cp("parallel", "parallel"))(a, g3)


ROW_TILE = (256,)


def _rows(TL, D):
    return pl.BlockSpec((TL, D), lambda i: (i, 0))


def _fixed(R, D):
    return pl.BlockSpec((R, D), lambda i: (0, 0))


def _rowsum8(v):
    T, D = v.shape
    return jnp.sum(v.reshape(T // 8, 8, D), axis=0)


def _norm_parts(xv):
    r = lax.rsqrt(jnp.mean(xv * xv, axis=-1, keepdims=True) + RMS_EPS)
    return xv * r, r


def norm_mod(x, gamma, mods, k_shift, out_dtype, name):
    L, D = x.shape
    TL = _tile(L, ROW_TILE)

    def body(x_ref, g_ref, m_ref, o_ref):
        xn, _ = _norm_parts(x_ref[...])
        sh, sc = m_ref[k_shift:k_shift + 1, :], m_ref[k_shift + 1:k_shift + 2, :]
        o_ref[...] = ((xn * g_ref[...]) * (1.0 + sc) + sh).astype(o_ref.dtype)

    return pl.pallas_call(body, name=name, grid=(L // TL,),
                          in_specs=[_rows(TL, D), _fixed(1, D), _fixed(6, D)], out_specs=_rows(TL, D),
                          out_shape=jax.ShapeDtypeStruct((L, D), out_dtype), compiler_params=_cp("parallel"))(x, gamma, mods)


def norm_bwd(dh, x, dres, gamma, mods, k_shift, name, branch=None):
    L, D = x.shape
    TL = _tile(L, ROW_TILE)
    nacc = 4 if branch else 3

    def body(*refs):
        if branch:
            dh_ref, x_ref, dr_ref, g_ref, m_ref, f_ref, fm_ref, dx_ref, s_ref, df_ref, acc = refs
        else:
            dh_ref, x_ref, dr_ref, g_ref, m_ref, dx_ref, s_ref, acc = refs
        i = pl.program_id(0)

        @pl.when(i == 0)
        def _():
            acc[...] = jnp.zeros_like(acc)

        xn, r = _norm_parts(x_ref[...])
        dh_v = dh_ref[...].astype(F32)
        gam = g_ref[...]
        sc = m_ref[k_shift + 1:k_shift + 2, :]
        dn = dh_v * (1.0 + sc)
        dxn = dn * gam
        dx = dr_ref[...] + r * (dxn - xn * jnp.mean(dxn * xn, axis=-1, keepdims=True))
        dx_ref[...] = dx
        acc[0] += _rowsum8(dh_v)
        acc[1] += _rowsum8(dh_v * (xn * gam))
        acc[2] += _rowsum8(dn * xn)
        if branch:
            df_ref[...] = (dx * fm_ref[branch[2]:branch[2] + 1, :]).astype(df_ref.dtype)
            acc[3] += _rowsum8(dx * f_ref[...].astype(F32))

        @pl.when(i == pl.num_programs(0) - 1)
        def _():
            s_ref[...] = jnp.zeros_like(s_ref)
            for q in range(nacc):
                s_ref[q:q + 1, :] = jnp.sum(acc[q], axis=0, keepdims=True)

    in_specs = [_rows(TL, D), _rows(TL, D), _rows(TL, D), _fixed(1, D), _fixed(6, D)]
    out_specs = [_rows(TL, D), _fixed(8, D)]
    out_shape = [jax.ShapeDtypeStruct((L, D), F32), jax.ShapeDtypeStruct((8, D), F32)]
    args = [dh, x, dres, gamma, mods]
    if branch:
        in_specs += [_rows(TL, D), _fixed(6, D)]
        out_specs.append(_rows(TL, D))
        out_shape.append(jax.ShapeDtypeStruct((L, D), BF))
        args += [branch[0], branch[1]]
    return pl.pallas_call(
        body, name=name, grid=(L // TL,), in_specs=in_specs, out_specs=out_specs, out_shape=out_shape,
        scratch_shapes=[pltpu.VMEM((nacc, 8, D), F32)], compiler_params=_cp("arbitrary"))(*args)


def gate_bwd(dx, f, mods, k_gate, name):
    L, D = dx.shape
    TL = _tile(L, (512, 256))

    def body(dx_ref, f_ref, m_ref, o_ref, s_ref, acc):
        i = pl.program_id(0)

        @pl.when(i == 0)
        def _():
            acc[...] = jnp.zeros_like(acc)

        dxv = dx_ref[...]
        o_ref[...] = (dxv * m_ref[k_gate:k_gate + 1, :]).astype(o_ref.dtype)
        acc[...] += _rowsum8(dxv * f_ref[...].astype(F32))

        @pl.when(i == pl.num_programs(0) - 1)
        def _():
            s_ref[...] = jnp.zeros_like(s_ref)
            s_ref[0:1, :] = jnp.sum(acc[...], axis=0, keepdims=True)

    return pl.pallas_call(
        body, name=name, grid=(L // TL,), in_specs=[_rows(TL, D), _rows(TL, D), _fixed(6, D)],
        out_specs=[_rows(TL, D), _fixed(8, D)],
        out_shape=[jax.ShapeDtypeStruct((L, D), BF), jax.ShapeDtypeStruct((8, D), F32)],
        scratch_shapes=[pltpu.VMEM((8, D), F32)], compiler_params=_cp("arbitrary"))(dx, f, mods)


def ffn_in_act(a, w, name):
    M, K = a.shape
    S, _, Ns = w.shape
    half = S // 2
    TM = _tile(M, (512, 256))
    TN = _tile(Ns, (1408, 1024, 768, 512, 256, 128))
    nj = Ns // TN

    def body(a_ref, wg_ref, wu_ref, gu_ref, act_ref):
        av = a_ref[...]
        g = jnp.dot(av, wg_ref[...], preferred_element_type=F32)
        u = jnp.dot(av, wu_ref[...], preferred_element_type=F32)
        gu_ref[0] = g.astype(gu_ref.dtype)
        gu_ref[1] = u.astype(gu_ref.dtype)
        act_ref[...] = (g * jax.nn.sigmoid(g) * u).astype(act_ref.dtype)

    return pl.pallas_call(
        body, name=name, grid=(half, nj, M // TM),
        in_specs=[pl.BlockSpec((TM, K), lambda s, j, i: (i, 0)),
                  pl.BlockSpec((None, K, TN), lambda s, j, i: (s, 0, j)),
                  pl.BlockSpec((None, K, TN), lambda s, j, i: (s + half, 0, j))],
        out_specs=[pl.BlockSpec((2, TM, TN), lambda s, j, i: (0, i, s * nj + j)),
                   pl.BlockSpec((TM, TN), lambda s, j, i: (i, s * nj + j))],
        out_shape=[jax.ShapeDtypeStruct((2, M, half * Ns), BF), jax.ShapeDtypeStruct((M, half * Ns), BF)],
        compiler_params=_cp("parallel", "parallel", "parallel"))(a, w, w)


def ffn_out_bwd(dff, w2, gu, name):
    M, D = dff.shape
    F = w2.shape[0]
    TM = _tile(M, (512, 256))
    CW = _tile(F, (256, 128))

    def body(d_ref, w_ref, gu_ref, o_ref):
        dv = d_ref[...]
        for c in range(0, F, CW):
            da = lax.dot_general(dv, w_ref[c:c + CW, :], (((1,), (1,)), ((), ())), preferred_element_type=F32)
            g = gu_ref[0, :, c:c + CW].astype(F32)
            u = gu_ref[1, :, c:c + CW].astype(F32)
            s = jax.nn.sigmoid(g)
            o_ref[0, :, c:c + CW] = (da * u * (s + g * s * (1.0 - s))).astype(o_ref.dtype)
            o_ref[1, :, c:c + CW] = (da * g * s).astype(o_ref.dtype)

    part = pl.BlockSpec((2, TM, F), lambda i: (0, i, 0))
    return pl.pallas_call(
        body, name=name, grid=(M // TM,),
        in_specs=[pl.BlockSpec((TM, D), lambda i: (i, 0)), pl.BlockSpec((F, D), lambda i: (0, 0)), part],
        out_specs=part, out_shape=jax.ShapeDtypeStruct((2, M, F), BF),
        compiler_params=_cp("parallel"))(dff, w2, gu)


def swiglu_act(gu, name):
    L, F2 = gu.shape
    F = F2 // 2
    TL = _tile(L, (256,))

    def body(gu_ref, o_ref):
        g = gu_ref[:, :F].astype(F32)
        u = gu_ref[:, F:].astype(F32)
        o_ref[...] = (g * jax.nn.sigmoid(g) * u).astype(o_ref.dtype)

    return pl.pallas_call(body, name=name, grid=(L // TL,), in_specs=[_rows(TL, F2)], out_specs=_rows(TL, F),
                          out_shape=jax.ShapeDtypeStruct((L, F), BF), compiler_params=_cp("parallel"))(gu)


def swiglu_bwd(da, gu, name):
    L, F2 = gu.shape
    F = F2 // 2
    TL = _tile(L, (256,))

    def body(da_ref, gu_ref, o_ref):
        g = gu_ref[:, :F].astype(F32)
        u = gu_ref[:, F:].astype(F32)
        d = da_ref[...].astype(F32)
        s = jax.nn.sigmoid(g)
        o_ref[:, :F] = (d * u * (s + g * s * (1.0 - s))).astype(o_ref.dtype)
        o_ref[:, F:] = (d * g * s).astype(o_ref.dtype)

    return pl.pallas_call(body, name=name, grid=(L // TL,), in_specs=[_rows(TL, F), _rows(TL, F2)],
                          out_specs=_rows(TL, F2), out_shape=jax.ShapeDtypeStruct((L, F2), BF),
                          compiler_params=_cp("parallel"))(da, gu)


def glu_res(o, x, mods, k_gate, name):
    L, D = x.shape
    TL = _tile(L, (512, 256))

    def body(o_ref, x_ref, m_ref, mix_ref, y_ref):
        mix = o_ref[:, :D].astype(F32) * jax.nn.sigmoid(o_ref[:, D:].astype(F32))
        mix_ref[...] = mix.astype(mix_ref.dtype)
        y_ref[...] = x_ref[...] + m_ref[k_gate:k_gate + 1, :] * mix

    return pl.pallas_call(
        body, name=name, grid=(L // TL,), in_specs=[_rows(TL, 2 * D), _rows(TL, D), _fixed(6, D)],
        out_specs=[_rows(TL, D), _rows(TL, D)],
        out_shape=[jax.ShapeDtypeStruct((L, D), BF), jax.ShapeDtypeStruct((L, D), F32)],
        compiler_params=_cp("parallel"))(o, x, mods)


def ssm_out_glu(z, w, x, mods, k_gate, name):
    M, K = z.shape
    S, _, Ns = w.shape
    half = S // 2
    TM = _tile(M, (1024, 512, 256))
    TN = _tile(Ns, (512, 256, 128))
    nj = Ns // TN

    def body(z_ref, wv_ref, wg_ref, x_ref, m_ref, o_ref, mix_ref, y_ref):
        zv = z_ref[...]
        val = jnp.dot(zv, wv_ref[...], preferred_element_type=F32)
        gate = jnp.dot(zv, wg_ref[...], preferred_element_type=F32)
        o_ref[0] = val.astype(o_ref.dtype)
        o_ref[1] = gate.astype(o_ref.dtype)
        mix = val * jax.nn.sigmoid(gate)
        mix_ref[...] = mix.astype(mix_ref.dtype)
        y_ref[...] = x_ref[...] + m_ref[k_gate:k_gate + 1, :] * mix

    col = lambda s, j, i: (i, s * nj + j)
    return pl.pallas_call(
        body, name=name, grid=(half, nj, M // TM),
        in_specs=[pl.BlockSpec((TM, K), lambda s, j, i: (i, 0)),
                  pl.BlockSpec((None, K, TN), lambda s, j, i: (s, 0, j)),
                  pl.BlockSpec((None, K, TN), lambda s, j, i: (s + half, 0, j)),
                  pl.BlockSpec((TM, TN), col), pl.BlockSpec((6, TN), lambda s, j, i: (0, s * nj + j))],
        out_specs=[pl.BlockSpec((2, TM, TN), lambda s, j, i: (0, i, s * nj + j)), pl.BlockSpec((TM, TN), col),
                   pl.BlockSpec((TM, TN), col)],
        out_shape=[jax.ShapeDtypeStruct((2, M, half * Ns), BF), jax.ShapeDtypeStruct((M, half * Ns), BF),
                   jax.ShapeDtypeStruct((M, half * Ns), F32)],
        compiler_params=_cp("parallel", "parallel", "parallel"))(z, w, w, x, mods)


def glu_bwd(dmix, o, name):
    _, L, D = o.shape
    TL = _tile(L, ROW_TILE)

    def body(d_ref, o_ref, do_ref):
        d = d_ref[...].astype(F32)
        val = o_ref[0].astype(F32)
        s = jax.nn.sigmoid(o_ref[1].astype(F32))
        do_ref[0] = (d * s).astype(do_ref.dtype)
        do_ref[1] = (d * val * s * (1.0 - s)).astype(do_ref.dtype)

    part = pl.BlockSpec((2, TL, D), lambda i: (0, i, 0))
    return pl.pallas_call(body, name=name, grid=(L // TL,), in_specs=[_rows(TL, D), part],
                          out_specs=part, out_shape=jax.ShapeDtypeStruct((2, L, D), BF),
                          compiler_params=_cp("parallel"))(dmix, o)


def final_loss(x, target, gamma, f, fmods, k_gate, name):
    L, D = x.shape
    TL = _tile(L, ROW_TILE)

    def body(x_ref, t_ref, g_ref, f_ref, fm_ref, l_ref, dx_ref, s_ref, df_ref, acc, lacc):
        i = pl.program_id(0)

        @pl.when(i == 0)
        def _():
            acc[...] = jnp.zeros_like(acc)
            lacc[...] = jnp.zeros_like(lacc)

        xn, r = _norm_parts(x_ref[...])
        gam = g_ref[...]
        e = xn * gam - t_ref[...]
        lacc[...] += jnp.sum(0.5 * jnp.mean(e * e, axis=-1, keepdims=True), axis=0, keepdims=True)
        dy = e * (1.0 / D)
        dxn = dy * gam
        dx = r * (dxn - xn * jnp.mean(dxn * xn, axis=-1, keepdims=True))
        dx_ref[...] = dx
        df_ref[...] = (dx * fm_ref[k_gate:k_gate + 1, :]).astype(df_ref.dtype)
        acc[0] += _rowsum8(dy * xn)
        acc[1] += _rowsum8(dx * f_ref[...].astype(F32))

        @pl.when(i == pl.num_programs(0) - 1)
        def _():
            s_ref[...] = jnp.zeros_like(s_ref)
            for q in range(2):
                s_ref[q:q + 1, :] = jnp.sum(acc[q], axis=0, keepdims=True)
            l_ref[...] = jnp.broadcast_to(lacc[...], l_ref.shape)

    return pl.pallas_call(
        body, name=name, grid=(L // TL,),
        in_specs=[_rows(TL, D), _rows(TL, D), _fixed(1, D), _rows(TL, D), _fixed(6, D)],
        out_specs=[_fixed(8, 128), _rows(TL, D), _fixed(8, D), _rows(TL, D)],
        out_shape=[jax.ShapeDtypeStruct((8, 128), F32), jax.ShapeDtypeStruct((L, D), F32),
                   jax.ShapeDtypeStruct((8, D), F32), jax.ShapeDtypeStruct((L, D), BF)],
        scratch_shapes=[pltpu.VMEM((2, 8, D), F32), pltpu.VMEM((1, 1), F32)],
        compiler_params=_cp("arbitrary"))(x, target, gamma, f, fmods)


def _col(L, TC, off):
    return pl.BlockSpec((L, TC), lambda j: (0, off + j))


def _shift_down(v, k, row):
    return jnp.where(row >= k, pltpu.roll(v, k, 0), 0.0)


def _shift_up(v, k, row, L):
    return jnp.where(row < L - k, pltpu.roll(v, L - k, 0), 0.0)


def conv_fwd(p, w, name):
    L, D3 = p.shape
    D = D3 // 3
    TC = _tile(D, (128,))
    nc = D // TC

    def body(b_ref, c_ref, v_ref, w_ref, o_ref):
        row = lax.broadcasted_iota(jnp.int32, (L, TC), 0)
        cv = c_ref[...].astype(F32) * v_ref[...].astype(F32)
        conv = w_ref[2:3, :] * cv + w_ref[1:2, :] * _shift_down(cv, 1, row) + w_ref[0:1, :] * _shift_down(cv, 2, row)
        o_ref[...] = (b_ref[...].astype(F32) * conv).astype(o_ref.dtype)

    return pl.pallas_call(
        body, name=name, grid=(nc,),
        in_specs=[_col(L, TC, 0), _col(L, TC, nc), _col(L, TC, 2 * nc), pl.BlockSpec((3, TC), lambda j: (0, j))],
        out_specs=_col(L, TC, 0), out_shape=jax.ShapeDtypeStruct((L, D), BF), compiler_params=_cp("parallel"))(p, p, p, w)


def conv_bwd(dm, p, w, name):
    L, D3 = p.shape
    D = D3 // 3
    TC = _tile(D, (128,))
    nc = D // TC

    def body(dm_ref, b_ref, c_ref, v_ref, w_ref, db_ref, dc_ref, dv_ref, dw_ref):
        row = lax.broadcasted_iota(jnp.int32, (L, TC), 0)
        cg, vv = c_ref[...].astype(F32), v_ref[...].astype(F32)
        cv = cg * vv
        cv1, cv2 = _shift_down(cv, 1, row), _shift_down(cv, 2, row)
        conv = w_ref[2:3, :] * cv + w_ref[1:2, :] * cv1 + w_ref[0:1, :] * cv2
        dmv = dm_ref[...].astype(F32)
        db_ref[...] = (dmv * conv).astype(db_ref.dtype)
        dconv = dmv * b_ref[...].astype(F32)
        dcv = (w_ref[2:3, :] * dconv + w_ref[1:2, :] * _shift_up(dconv, 1, row, L)
               + w_ref[0:1, :] * _shift_up(dconv, 2, row, L))
        dc_ref[...] = (dcv * vv).astype(dc_ref.dtype)
        dv_ref[...] = (dcv * cg).astype(dv_ref.dtype)
        dw_ref[...] = jnp.zeros_like(dw_ref)
        dw_ref[0:1, :] = jnp.sum(dconv * cv2, axis=0, keepdims=True)
        dw_ref[1:2, :] = jnp.sum(dconv * cv1, axis=0, keepdims=True)
        dw_ref[2:3, :] = jnp.sum(dconv * cv, axis=0, keepdims=True)

    one = jax.ShapeDtypeStruct((L, D), BF)
    return pl.pallas_call(
        body, name=name, grid=(nc,),
        in_specs=[_col(L, TC, 0), _col(L, TC, 0), _col(L, TC, nc), _col(L, TC, 2 * nc),
                  pl.BlockSpec((3, TC), lambda j: (0, j))],
        out_specs=[_col(L, TC, 0), _col(L, TC, 0), _col(L, TC, 0), pl.BlockSpec((8, TC), lambda j: (0, j))],
        out_shape=[one, one, one, jax.ShapeDtypeStruct((8, D), F32)],
        compiler_params=_cp("parallel"))(dm, p, p, p, w)


def _gelu(y):
    return 0.5 * y * (1.0 + jnp.tanh(GELU_C * (y + GELU_A * y * y * y)))


def _gelu_grad(y):
    th = jnp.tanh(GELU_C * (y + GELU_A * y * y * y))
    return 0.5 * (1.0 + th) + 0.5 * y * (1.0 - th * th) * GELU_C * (1.0 + 3.0 * GELU_A * y * y)


def _cmul_add(br, bi, ar, ai, sr, si):
    return br + ar * sr - ai * si, bi + ar * si + ai * sr


def _log2(n):
    k = n.bit_length() - 1
    assert 1 << k == n
    return k


def _replicate(P2, W2, P, GLP, transposed):
    shape = (W2, P2) if transposed else (P2, W2)
    k = lax.broadcasted_iota(jnp.int32, shape, 1 if transposed else 0)
    c = lax.broadcasted_iota(jnp.int32, shape, 0 if transposed else 1)
    return ((k >> _log2(P)) == (c >> _log2(GLP))) & ((k & (P - 1)) == (c & (P - 1)))


def _on_diagonal(KB, W2, H, P, GLP, transposed):
    shape = (W2, KB) if transposed else (KB, W2)
    r = lax.broadcasted_iota(jnp.int32, shape, 1 if transposed else 0)
    c = lax.broadcasted_iota(jnp.int32, shape, 0 if transposed else 1)
    return (r >> _log2(H)) == ((c & (GLP - 1)) >> _log2(P))


def _expand(t, dims, transposed):
    KB, W2, H, P, GLP = dims
    rep = _replicate(2 * P, W2, P, GLP, transposed).astype(t.dtype)
    wide = jnp.dot(rep, t, preferred_element_type=F32) if transposed else jnp.dot(t, rep, preferred_element_type=F32)
    return jnp.where(_on_diagonal(KB, W2, H, P, GLP, transposed), wide, 0.0).astype(t.dtype)


def _extract(acc, dims):
    KB, W2, H, P, GLP = dims
    rep = _replicate(2 * P, W2, P, GLP, True).astype(F32)
    kept = jnp.where(_on_diagonal(KB, W2, H, P, GLP, False), acc, 0.0)
    return jnp.dot(kept, rep, preferred_element_type=F32, precision=lax.Precision.HIGHEST)


def _cmul(ar, ai, sr, si):
    return ar * sr - ai * si, ar * si + ai * sr


LANES = 128


def _cols(ref, base, n, rows):
    return jnp.concatenate([ref[base + q, rows, :] for q in range(n)], axis=1)


def _set_cols(ref, base, n, rows, val):
    for q in range(n):
        ref[base + q, rows, :] = val[:, q * LANES:(q + 1) * LANES]


def _strided_s5_fwd(h, tb, tct, pw, dvec, name):
    L, D = h.shape
    nkb, KB, P2 = tb.shape
    P = P2 // 2
    W = (KB // SSM_GROUP) * P
    W2 = 2 * W
    dims = (KB, W2, SSM_GROUP, P, W)
    TL = _tile(L, (512, 256))
    CH = TL // 8
    NC = W // LANES

    def body(h_ref, tb_ref, tct_ref, pw_ref, d_ref, s_ref, y_ref, z_ref, bw, cw, carry):
        t = pl.program_id(1)

        @pl.when(t == 0)
        def _():
            carry[...] = jnp.zeros_like(carry)
            bw[...] = _expand(tb_ref[...], dims, False)
            cw[...] = _expand(tct_ref[...], dims, True)

        hv = h_ref[...]
        _set_cols(s_ref, 0, 2 * NC, slice(None), jnp.dot(hv.astype(BF), bw[...], preferred_element_type=F32))
        ar, ai = pw_ref[0:8, :W], pw_ref[0:8, W:]
        xr = xi = jnp.zeros((8, W), F32)
        for j in range(CH):
            rows = pl.ds(j, 8, stride=CH)
            xr, xi = _cmul_add(_cols(s_ref, 0, NC, rows), _cols(s_ref, NC, NC, rows), ar, ai, xr, xi)
            _set_cols(s_ref, 0, NC, rows, xr)
            _set_cols(s_ref, NC, NC, rows, xi)
        for k, off in ((1, 8), (2, 16), (4, 24)):
            xr, xi = _cmul_add(xr, xi, pw_ref[off:off + 8, :W], pw_ref[off:off + 8, W:],
                               pltpu.roll(xr, k, 0), pltpu.roll(xi, k, 0))
        xr, xi = _cmul_add(xr, xi, pw_ref[32:40, :W], pw_ref[32:40, W:], carry[0], carry[1])
        first = lax.broadcasted_iota(jnp.int32, (8, W), 0) == 0
        cr = jnp.where(first, carry[0], pltpu.roll(xr, 1, 0))
        ci = jnp.where(first, carry[1], pltpu.roll(xi, 1, 0))
        carry[0] = jnp.broadcast_to(xr[7:8], (8, W))
        carry[1] = jnp.broadcast_to(xi[7:8], (8, W))
        for j in range(CH):
            rows = pl.ds(j, 8, stride=CH)
            cr, ci = _cmul(ar, ai, cr, ci)
            _set_cols(s_ref, 0, NC, rows, _cols(s_ref, 0, NC, rows) + cr)
            _set_cols(s_ref, NC, NC, rows, _cols(s_ref, NC, NC, rows) + ci)
        sv = _cols(s_ref, 0, 2 * NC, slice(None))
        y = jnp.dot(sv.astype(BF), cw[...], preferred_element_type=F32) + d_ref[...] * hv
        y_ref[...] = y
        z_ref[...] = _gelu(y).astype(z_ref.dtype)

    blk = lambda kb, t: (t, kb)
    per_kb = lambda kb, t: (kb, 0, 0)
    return pl.pallas_call(
        body, name=name, grid=(nkb, L // TL),
        in_specs=[pl.BlockSpec((TL, KB), blk), pl.BlockSpec((None, KB, P2), per_kb),
                  pl.BlockSpec((None, P2, KB), per_kb), pl.BlockSpec((None, 40, W2), per_kb),
                  pl.BlockSpec((1, KB), lambda kb, t: (0, kb))],
        out_specs=[pl.BlockSpec((2 * NC, TL, LANES), lambda kb, t: (kb, t, 0)), pl.BlockSpec((TL, KB), blk),
                   pl.BlockSpec((TL, KB), blk)],
        out_shape=[jax.ShapeDtypeStruct((nkb * 2 * NC, L, LANES), F32), jax.ShapeDtypeStruct((L, D), F32),
                   jax.ShapeDtypeStruct((L, D), BF)],
        scratch_shapes=[pltpu.VMEM((KB, W2), BF), pltpu.VMEM((W2, KB), BF), pltpu.VMEM((2, 8, W), F32)],
        compiler_params=_cp("parallel", "arbitrary"))(h, tb, tct, pw, dvec)


def _strided_s5_bwd(dz, y, h, s, tc, tbt, pwr, dvec, name):
    L, D = h.shape
    nkb, KB, P2 = tc.shape
    P = P2 // 2
    W = (KB // SSM_GROUP) * P
    W2 = 2 * W
    dims = (KB, W2, SSM_GROUP, P, W)
    TL = _tile(L, (512, 256))
    CH = TL // 8
    NC = W // LANES
    nt = L // TL

    def body(dz_ref, y_ref, h_ref, s_ref, sp_ref, tc_ref, tbt_ref, pw_ref, d_ref,
             dh_ref, dd_ref, da_ref, db_ref, dc_ref, g, ctw, btw, dbacc, dcacc, carry):
        t = pl.program_id(1)

        @pl.when(t == 0)
        def _():
            carry[...] = jnp.zeros_like(carry)
            dd_ref[...] = jnp.zeros_like(dd_ref)
            da_ref[...] = jnp.zeros_like(da_ref)
            dbacc[...] = jnp.zeros_like(dbacc)
            dcacc[...] = jnp.zeros_like(dcacc)
            ctw[...] = _expand(tc_ref[...], dims, False)
            btw[...] = _expand(tbt_ref[...], dims, True)

        hv = h_ref[...]
        dy = dz_ref[...].astype(F32) * _gelu_grad(y_ref[...])
        dd_ref[...] += _rowsum8(dy * hv)
        dyb = dy.astype(BF)
        _set_cols(g, 0, 2 * NC, slice(None), jnp.dot(dyb, ctw[...], preferred_element_type=F32))
        ar, ai = pw_ref[0:8, :W], pw_ref[0:8, W:]
        gr = gi = jnp.zeros((8, W), F32)
        for j in reversed(range(CH)):
            rows = pl.ds(j, 8, stride=CH)
            gr, gi = _cmul_add(_cols(g, 0, NC, rows), _cols(g, NC, NC, rows), ar, ai, gr, gi)
            _set_cols(g, 0, NC, rows, gr)
            _set_cols(g, NC, NC, rows, gi)
        for k, off in ((1, 8), (2, 16), (4, 24)):
            gr, gi = _cmul_add(gr, gi, pw_ref[off:off + 8, :W], pw_ref[off:off + 8, W:],
                               pltpu.roll(gr, 8 - k, 0), pltpu.roll(gi, 8 - k, 0))
        gr, gi = _cmul_add(gr, gi, pw_ref[32:40, :W], pw_ref[32:40, W:], carry[0], carry[1])
        sub = lax.broadcasted_iota(jnp.int32, (8, W), 0)
        cr = jnp.where(sub == 7, carry[0], pltpu.roll(gr, 7, 0))
        ci = jnp.where(sub == 7, carry[1], pltpu.roll(gi, 7, 0))
        carry[0] = jnp.broadcast_to(gr[0:1], (8, W))
        carry[1] = jnp.broadcast_to(gi[0:1], (8, W))
        live = jnp.where(t == nt - 1, 0.0, 1.0)
        accr = acci = jnp.zeros((8, W), F32)
        for j in reversed(range(CH)):
            rows = pl.ds(j, 8, stride=CH)
            cr, ci = _cmul(ar, ai, cr, ci)
            gr, gi = _cols(g, 0, NC, rows) + cr, _cols(g, NC, NC, rows) + ci
            _set_cols(g, 0, NC, rows, gr)
            _set_cols(g, NC, NC, rows, gi)
            if j > 0:
                before = pl.ds(j - 1, 8, stride=CH)
                pr, pi = _cols(s_ref, 0, NC, before), _cols(s_ref, NC, NC, before)
            else:
                last = pl.ds(CH - 1, 8, stride=CH)
                pr = jnp.where(sub == 0, _cols(sp_ref, 0, NC, slice(7, 8)) * live,
                               pltpu.roll(_cols(s_ref, 0, NC, last), 1, 0))
                pi = jnp.where(sub == 0, _cols(sp_ref, NC, NC, slice(7, 8)) * live,
                               pltpu.roll(_cols(s_ref, NC, NC, last), 1, 0))
            accr = accr + pr * gr + pi * gi
            acci = acci + pr * gi - pi * gr
        da_ref[:, :W] += accr
        da_ref[:, W:] += acci

        gb = _cols(g, 0, 2 * NC, slice(None)).astype(BF)
        dh_ref[...] = dy * d_ref[...] + jnp.dot(gb, btw[...], preferred_element_type=F32)
        tn = (((0,), (0,)), ((), ()))
        dbacc[...] += lax.dot_general(hv.astype(BF), gb, tn, preferred_element_type=F32)
        dcacc[...] += lax.dot_general(dyb, _cols(s_ref, 0, 2 * NC, slice(None)).astype(BF), tn,
                                      preferred_element_type=F32)

        @pl.when(t == nt - 1)
        def _():
            db_ref[...] = _extract(dbacc[...], dims)
            dc_ref[...] = _extract(dcacc[...], dims)

    rev = lambda kb, t: (nt - 1 - t, kb)
    per_kb = lambda kb, t: (kb, 0, 0)
    return pl.pallas_call(
        body, name=name, grid=(nkb, nt),
        in_specs=[pl.BlockSpec((TL, KB), rev), pl.BlockSpec((TL, KB), rev), pl.BlockSpec((TL, KB), rev),
                  pl.BlockSpec((2 * NC, TL, LANES), lambda kb, t: (kb, nt - 1 - t, 0)),
                  pl.BlockSpec((2 * NC, 8, LANES), lambda kb, t: (kb, jnp.maximum((nt - 1 - t) * CH - 1, 0), 0)),
                  pl.BlockSpec((None, KB, P2), per_kb), pl.BlockSpec((None, P2, KB), per_kb),
                  pl.BlockSpec((None, 40, W2), per_kb), pl.BlockSpec((1, KB), lambda kb, t: (0, kb))],
        out_specs=[pl.BlockSpec((TL, KB), rev), pl.BlockSpec((8, KB), lambda kb, t: (0, kb)),
                   pl.BlockSpec((None, 8, W2), per_kb), pl.BlockSpec((None, KB, P2), per_kb),
                   pl.BlockSpec((None, KB, P2), per_kb)],
        out_shape=[jax.ShapeDtypeStruct((L, D), F32), jax.ShapeDtypeStruct((8, D), F32),
                   jax.ShapeDtypeStruct((nkb, 8, W2), F32), jax.ShapeDtypeStruct((nkb, KB, P2), F32),
                   jax.ShapeDtypeStruct((nkb, KB, P2), F32)],
        scratch_shapes=[pltpu.VMEM((2 * NC, TL, LANES), F32), pltpu.VMEM((KB, W2), BF), pltpu.VMEM((W2, KB), BF),
                        pltpu.VMEM((KB, W2), F32), pltpu.VMEM((KB, W2), F32), pltpu.VMEM((2, 8, W), F32)],
        compiler_params=_cp("parallel", "arbitrary"))(dz, y, h, s, s, tc, tbt, pwr, dvec)


def _chunk_order(TL, CH, transposed):
    out_row = lax.broadcasted_iota(jnp.int32, (TL, TL), 1 if transposed else 0)
    in_row = lax.broadcasted_iota(jnp.int32, (TL, TL), 0 if transposed else 1)
    return in_row == ((out_row & 7) << _log2(CH)) + (out_row >> 3)


def _reorder(perm, v):
    hi = v.astype(perm.dtype)
    lo = (v - hi.astype(F32)).astype(perm.dtype)
    return jnp.dot(perm, hi, preferred_element_type=F32) + jnp.dot(perm, lo, preferred_element_type=F32)


def _interleave(main, side):
    n, m, k = len(main), len(side), 0
    for i, step in enumerate(main):
        step()
        while k < m and (k + 1) * n <= (i + 1) * m:
            side[k]()
            k += 1
    for step in side[k:]:
        step()


S5_CHUNK = 512


def s5_fwd(h, tb, tct, pw, dvec, name):
    L, D = h.shape
    nkb, KB, P2 = tb.shape
    P = P2 // 2
    W = (KB // SSM_GROUP) * P
    W2 = 2 * W
    dims = (KB, W2, SSM_GROUP, P, W)
    TL = _tile(L, (512, 256))
    CH = TL // 8
    NB = 2 if nkb % 2 == 0 else 1
    CK = min(S5_CHUNK, W2)

    def body(h_ref, tb_ref, tct_ref, pw_ref, d_ref, s_ref, y_ref, z_ref, hq_ref, bw, cw, perm, unperm, carry):
        t = pl.program_id(1)

        @pl.when(t == 0)
        def _():
            carry[...] = jnp.zeros_like(carry)
            for b in range(NB):
                bw[b] = _expand(tb_ref[b], dims, False)
                cw[b] = _expand(tct_ref[b], dims, True)
            perm[...] = _chunk_order(TL, CH, False).astype(perm.dtype)
            unperm[...] = _chunk_order(TL, CH, True).astype(perm.dtype)

        hp = _reorder(perm[...], h_ref[...])
        hpb = hp.astype(BF)
        hq_ref[...] = hpb
        first = lax.broadcasted_iota(jnp.int32, (8, W), 0) == 0

        def project(b):
            def chunk(c):
                def emit():
                    s_ref[:, b * W2 + c:b * W2 + c + CK] = jnp.dot(hpb[:, b * KB:(b + 1) * KB], bw[b, :, c:c + CK],
                                                                   preferred_element_type=F32)
                return emit
            return [chunk(c) for c in range(0, W2, CK)]

        def scan(b):
            re, im = slice(b * W2, b * W2 + W), slice(b * W2 + W, (b + 1) * W2)
            ar, ai = pw_ref[b, 0:8, :W], pw_ref[b, 0:8, W:]
            st = {"x": (jnp.zeros((8, W), F32), jnp.zeros((8, W), F32))}

            def own(j):
                def emit():
                    rows = slice(j * 8, j * 8 + 8)
                    xr, xi = _cmul_add(s_ref[rows, re], s_ref[rows, im], ar, ai, *st["x"])
                    s_ref[rows, re] = xr
                    s_ref[rows, im] = xi
                    st["x"] = (xr, xi)
                return emit

            def ends():
                xr, xi = st["x"]
                for k, off in ((1, 8), (2, 16), (4, 24)):
                    xr, xi = _cmul_add(xr, xi, pw_ref[b, off:off + 8, :W], pw_ref[b, off:off + 8, W:],
                                       pltpu.roll(xr, k, 0), pltpu.roll(xi, k, 0))
                xr, xi = _cmul_add(xr, xi, pw_ref[b, 32:40, :W], pw_ref[b, 32:40, W:], carry[b, 0], carry[b, 1])
                st["c"] = (jnp.where(first, carry[b, 0], pltpu.roll(xr, 1, 0)),
                           jnp.where(first, carry[b, 1], pltpu.roll(xi, 1, 0)))
                carry[b, 0] = jnp.broadcast_to(xr[7:8], (8, W))
                carry[b, 1] = jnp.broadcast_to(xi[7:8], (8, W))

            def carried(j):
                def emit():
                    rows = slice(j * 8, j * 8 + 8)
                    cr, ci = _cmul(ar, ai, *st["c"])
                    s_ref[rows, re] = s_ref[rows, re] + cr
                    s_ref[rows, im] = s_ref[rows, im] + ci
                    st["c"] = (cr, ci)
                return emit

            return [own(j) for j in range(CH)] + [ends] + [carried(j) for j in range(CH)]

        def readout(b):
            cols = slice(b * KB, (b + 1) * KB)
            acc = {}

            def chunk(c):
                def emit():
                    part = jnp.dot(s_ref[:, b * W2 + c:b * W2 + c + CK].astype(BF), cw[b, c:c + CK, :],
                                   preferred_element_type=F32)
                    acc["y"] = part if c == 0 else acc["y"] + part
                return emit

            def finish():
                y = acc["y"] + d_ref[:, cols] * hp[:, cols]
                y_ref[:, cols] = y
                z_ref[:, cols] = jnp.dot(unperm[...], _gelu(y).astype(BF),
                                         preferred_element_type=F32).astype(z_ref.dtype)

            return [chunk(c) for c in range(0, W2, CK)] + [finish]

        for emit in project(0):
            emit()
        for b in range(NB):
            side = (project(b + 1) if b + 1 < NB else []) + (readout(b - 1) if b > 0 else [])
            _interleave(scan(b), side)
        for emit in readout(NB - 1):
            emit()

    blk = lambda kb, t: (t, kb)
    per_kb = lambda kb, t: (kb, 0, 0)
    return pl.pallas_call(
        body, name=name, grid=(nkb // NB, L // TL),
        in_specs=[pl.BlockSpec((TL, NB * KB), blk), pl.BlockSpec((NB, KB, P2), per_kb),
                  pl.BlockSpec((NB, P2, KB), per_kb), pl.BlockSpec((NB, 40, W2), per_kb),
                  pl.BlockSpec((1, NB * KB), lambda kb, t: (0, kb))],
        out_specs=[pl.BlockSpec((TL, NB * W2), blk), pl.BlockSpec((TL, NB * KB), blk),
                   pl.BlockSpec((TL, NB * KB), blk), pl.BlockSpec((TL, NB * KB), blk)],
        out_shape=[jax.ShapeDtypeStruct((L, nkb * W2), F32), jax.ShapeDtypeStruct((L, D), F32),
                   jax.ShapeDtypeStruct((L, D), BF), jax.ShapeDtypeStruct((L, D), BF)],
        scratch_shapes=[pltpu.VMEM((NB, KB, W2), BF), pltpu.VMEM((NB, W2, KB), BF), pltpu.VMEM((TL, TL), BF),
                        pltpu.VMEM((TL, TL), BF), pltpu.VMEM((NB, 2, 8, W), F32)],
        compiler_params=_cp("parallel", "arbitrary"))(h, tb, tct, pw, dvec)


def _s5_fwd_one_block(h, tb, tct, pw, dvec, name):
    L, D = h.shape
    nkb, KB, P2 = tb.shape
    P = P2 // 2
    W = (KB // SSM_GROUP) * P
    W2 = 2 * W
    dims = (KB, W2, SSM_GROUP, P, W)
    TL = _tile(L, (512, 256))
    CH = TL // 8

    def body(h_ref, tb_ref, tct_ref, pw_ref, d_ref, s_ref, y_ref, z_ref, hq_ref, bw, cw, perm, unperm, carry):
        t = pl.program_id(1)

        @pl.when(t == 0)
        def _():
            carry[...] = jnp.zeros_like(carry)
            bw[...] = _expand(tb_ref[...], dims, False)
            cw[...] = _expand(tct_ref[...], dims, True)
            perm[...] = _chunk_order(TL, CH, False).astype(perm.dtype)
            unperm[...] = _chunk_order(TL, CH, True).astype(perm.dtype)

        hp = _reorder(perm[...], h_ref[...])
        s_ref[...] = jnp.dot(hp.astype(BF), bw[...], preferred_element_type=F32)
        ar, ai = pw_ref[0:8, :W], pw_ref[0:8, W:]

        def own(j, x):
            rows = pl.ds(pl.multiple_of(j * 8, 8), 8)
            xr, xi = _cmul_add(s_ref[rows, :W], s_ref[rows, W:], ar, ai, x[0], x[1])
            s_ref[rows, :W] = xr
            s_ref[rows, W:] = xi
            return xr, xi

        zero = jnp.zeros((8, W), F32)
        xr, xi = lax.fori_loop(0, CH, own, (zero, zero))
        for k, off in ((1, 8), (2, 16), (4, 24)):
            xr, xi = _cmul_add(xr, xi, pw_ref[off:off + 8, :W], pw_ref[off:off + 8, W:],
                               pltpu.roll(xr, k, 0), pltpu.roll(xi, k, 0))
        xr, xi = _cmul_add(xr, xi, pw_ref[32:40, :W], pw_ref[32:40, W:], carry[0], carry[1])
        first = lax.broadcasted_iota(jnp.int32, (8, W), 0) == 0
        cr = jnp.where(first, carry[0], pltpu.roll(xr, 1, 0))
        ci = jnp.where(first, carry[1], pltpu.roll(xi, 1, 0))
        carry[0] = jnp.broadcast_to(xr[7:8], (8, W))
        carry[1] = jnp.broadcast_to(xi[7:8], (8, W))

        def carried(j, c):
            rows = pl.ds(pl.multiple_of(j * 8, 8), 8)
            cr, ci = _cmul(ar, ai, c[0], c[1])
            s_ref[rows, :W] = s_ref[rows, :W] + cr
            s_ref[rows, W:] = s_ref[rows, W:] + ci
            return cr, ci

        lax.fori_loop(0, CH, carried, (cr, ci))
        y = jnp.dot(s_ref[...].astype(BF), cw[...], preferred_element_type=F32) + d_ref[...] * hp
        y_ref[...] = y
        z_ref[...] = jnp.dot(unperm[...], _gelu(y).astype(BF), preferred_element_type=F32).astype(z_ref.dtype)

    blk = lambda kb, t: (t, kb)
    per_kb = lambda kb, t: (kb, 0, 0)
    return pl.pallas_call(
        body, name=name, grid=(nkb, L // TL),
        in_specs=[pl.BlockSpec((TL, KB), blk), pl.BlockSpec((None, KB, P2), per_kb),
                  pl.BlockSpec((None, P2, KB), per_kb), pl.BlockSpec((None, 40, W2), per_kb),
                  pl.BlockSpec((1, KB), lambda kb, t: (0, kb))],
        out_specs=[pl.BlockSpec((TL, W2), blk), pl.BlockSpec((TL, KB), blk), pl.BlockSpec((TL, KB), blk)],
        out_shape=[jax.ShapeDtypeStruct((L, nkb * W2), F32), jax.ShapeDtypeStruct((L, D), F32),
                   jax.ShapeDtypeStruct((L, D), BF)],
        scratch_shapes=[pltpu.VMEM((KB, W2), BF), pltpu.VMEM((W2, KB), BF), pltpu.VMEM((TL, TL), BF),
                        pltpu.VMEM((TL, TL), BF), pltpu.VMEM((2, 8, W), F32)],
        compiler_params=_cp("parallel", "arbitrary"))(h, tb, tct, pw, dvec)


def s5_bwd(dz, y, h, s, tc, tbt, pwr, dvec, name):
    L, D = h.shape
    nkb, KB, P2 = tc.shape
    P = P2 // 2
    W = (KB // SSM_GROUP) * P
    W2 = 2 * W
    dims = (KB, W2, SSM_GROUP, P, W)
    TL = _tile(L, (512, 256))
    CH = TL // 8
    nt = L // TL
    NB = 2 if nkb % 2 == 0 else 1
    CK = min(S5_CHUNK, W2)
    tn = (((0,), (0,)), ((), ()))

    def body(dz_ref, y_ref, h_ref, s_ref, sp_ref, tc_ref, tbt_ref, pw_ref, d_ref,
             dh_ref, dd_ref, da_ref, db_ref, dc_ref, g, ctw, btw, dbacc, dcacc, dys, perm, unperm, carry):
        t = pl.program_id(1)

        @pl.when(t == 0)
        def _():
            carry[...] = jnp.zeros_like(carry)
            dd_ref[...] = jnp.zeros_like(dd_ref)
            da_ref[...] = jnp.zeros_like(da_ref)
            dbacc[...] = jnp.zeros_like(dbacc)
            dcacc[...] = jnp.zeros_like(dcacc)
            for b in range(NB):
                ctw[b] = _expand(tc_ref[b], dims, False)
                btw[b] = _expand(tbt_ref[b], dims, True)
            perm[...] = _chunk_order(TL, CH, False).astype(perm.dtype)
            unperm[...] = _chunk_order(TL, CH, True).astype(perm.dtype)

        hpb = h_ref[...]
        hp = hpb.astype(F32)
        dy = jnp.dot(perm[...], dz_ref[...].astype(BF), preferred_element_type=F32) * _gelu_grad(y_ref[...])
        dd_ref[...] += _rowsum8(dy * hp)
        dys[...] = dy
        dyb = dy.astype(BF)
        sub = lax.broadcasted_iota(jnp.int32, (8, W), 0)
        live = jnp.where(t == nt - 1, 0.0, 1.0)

        def lead(b):
            cols = slice(b * KB, (b + 1) * KB)

            def to_states(c):
                def emit():
                    g[b, :, c:c + CK] = jnp.dot(dyb[:, cols], ctw[b, :, c:c + CK], preferred_element_type=F32)
                return emit

            def d_c(c):
                def emit():
                    dcacc[b, :, c:c + CK] += lax.dot_general(dyb[:, cols],
                                                             s_ref[:, b * W2 + c:b * W2 + c + CK].astype(BF), tn,
                                                             preferred_element_type=F32)
                return emit

            return [f(c) for c in range(0, W2, CK) for f in (to_states, d_c)]

        def scan(b):
            re, im = slice(b * W2, b * W2 + W), slice(b * W2 + W, (b + 1) * W2)
            ar, ai = pw_ref[b, 0:8, :W], pw_ref[b, 0:8, W:]
            zero = jnp.zeros((8, W), F32)
            st = {"g": (zero, zero), "acc": (zero, zero)}

            def own(j):
                def emit():
                    rows = slice(j * 8, j * 8 + 8)
                    gr, gi = _cmul_add(g[b, rows, :W], g[b, rows, W:], ar, ai, *st["g"])
                    g[b, rows, :W] = gr
                    g[b, rows, W:] = gi
                    st["g"] = (gr, gi)
                return emit

            def ends():
                gr, gi = st["g"]
                for k, off in ((1, 8), (2, 16), (4, 24)):
                    gr, gi = _cmul_add(gr, gi, pw_ref[b, off:off + 8, :W], pw_ref[b, off:off + 8, W:],
                                       pltpu.roll(gr, 8 - k, 0), pltpu.roll(gi, 8 - k, 0))
                gr, gi = _cmul_add(gr, gi, pw_ref[b, 32:40, :W], pw_ref[b, 32:40, W:], carry[b, 0], carry[b, 1])
                st["c"] = (jnp.where(sub == 7, carry[b, 0], pltpu.roll(gr, 7, 0)),
                           jnp.where(sub == 7, carry[b, 1], pltpu.roll(gi, 7, 0)))
                carry[b, 0] = jnp.broadcast_to(gr[0:1], (8, W))
                carry[b, 1] = jnp.broadcast_to(gi[0:1], (8, W))

            def carried(j):
                def emit():
                    rows = slice(j * 8, j * 8 + 8)
                    cr, ci = _cmul(ar, ai, *st["c"])
                    gr, gi = g[b, rows, :W] + cr, g[b, rows, W:] + ci
                    g[b, rows, :W] = gr
                    g[b, rows, W:] = gi
                    if j > 0:
                        before = slice(j * 8 - 8, j * 8)
                        pr, pi = s_ref[before, re], s_ref[before, im]
                    else:
                        last = slice(TL - 8, TL)
                        pr = jnp.where(sub == 0, sp_ref[7:8, re] * live, pltpu.roll(s_ref[last, re], 1, 0))
                        pi = jnp.where(sub == 0, sp_ref[7:8, im] * live, pltpu.roll(s_ref[last, im], 1, 0))
                    accr, acci = st["acc"]
                    st["c"] = (cr, ci)
                    st["acc"] = (accr + pr * gr + pi * gi, acci + pr * gi - pi * gr)
                return emit

            def done():
                da_ref[b, :, :W] += st["acc"][0]
                da_ref[b, :, W:] += st["acc"][1]

            return ([own(j) for j in reversed(range(CH))] + [ends] + [carried(j) for j in reversed(range(CH))]
                    + [done])

        def tail(b):
            cols = slice(b * KB, (b + 1) * KB)
            acc = {}

            def d_u(c):
                def emit():
                    part = jnp.dot(g[b, :, c:c + CK].astype(BF), btw[b, c:c + CK, :], preferred_element_type=F32)
                    acc["u"] = part if c == 0 else acc["u"] + part
                return emit

            def d_b(c):
                def emit():
                    dbacc[b, :, c:c + CK] += lax.dot_general(hpb[:, cols], g[b, :, c:c + CK].astype(BF), tn,
                                                             preferred_element_type=F32)
                return emit

            def finish():
                dh = (dys[:, cols] * d_ref[:, cols] + acc["u"]).astype(BF)
                dh_ref[:, cols] = jnp.dot(unperm[...], dh, preferred_element_type=F32).astype(dh_ref.dtype)

            return [f(c) for c in range(0, W2, CK) for f in (d_u, d_b)] + [finish]

        for emit in lead(0):
            emit()
        for b in range(NB):
            side = (lead(b + 1) if b + 1 < NB else []) + (tail(b - 1) if b > 0 else [])
            _interleave(scan(b), side)
        for emit in tail(NB - 1):
            emit()

        @pl.when(t == nt - 1)
        def _():
            for b in range(NB):
                db_ref[b] = _extract(dbacc[b], dims)
                dc_ref[b] = _extract(dcacc[b], dims)

    rev = lambda kb, t: (nt - 1 - t, kb)
    prev = lambda kb, t: (jnp.maximum((nt - 1 - t) * CH - 1, 0), kb)
    per_kb = lambda kb, t: (kb, 0, 0)
    return pl.pallas_call(
        body, name=name, grid=(nkb // NB, nt),
        in_specs=[pl.BlockSpec((TL, NB * KB), rev), pl.BlockSpec((TL, NB * KB), rev),
                  pl.BlockSpec((TL, NB * KB), rev), pl.BlockSpec((TL, NB * W2), rev),
                  pl.BlockSpec((8, NB * W2), prev), pl.BlockSpec((NB, KB, P2), per_kb),
                  pl.BlockSpec((NB, P2, KB), per_kb), pl.BlockSpec((NB, 40, W2), per_kb),
                  pl.BlockSpec((1, NB * KB), lambda kb, t: (0, kb))],
        out_specs=[pl.BlockSpec((TL, NB * KB), rev), pl.BlockSpec((8, NB * KB), lambda kb, t: (0, kb)),
                   pl.BlockSpec((NB, 8, W2), per_kb), pl.BlockSpec((NB, KB, P2), per_kb),
                   pl.BlockSpec((NB, KB, P2), per_kb)],
        out_shape=[jax.ShapeDtypeStruct((L, D), BF), jax.ShapeDtypeStruct((8, D), F32),
                   jax.ShapeDtypeStruct((nkb, 8, W2), F32), jax.ShapeDtypeStruct((nkb, KB, P2), F32),
                   jax.ShapeDtypeStruct((nkb, KB, P2), F32)],
        scratch_shapes=[pltpu.VMEM((NB, TL, W2), F32), pltpu.VMEM((NB, KB, W2), BF), pltpu.VMEM((NB, W2, KB), BF),
                        pltpu.VMEM((NB, KB, W2), F32), pltpu.VMEM((NB, KB, W2), F32), pltpu.VMEM((TL, NB * KB), F32),
                        pltpu.VMEM((TL, TL), BF), pltpu.VMEM((TL, TL), BF), pltpu.VMEM((NB, 2, 8, W), F32)],
        compiler_params=pltpu.CompilerParams(dimension_semantics=("parallel", "arbitrary"),
                                             vmem_limit_bytes=V7X_VMEM_BYTES - 4 * 1024 * 1024),
    )(dz, y, h, s, s, tc, tbt, pwr, dvec)


def _s5_bwd_one_block(dz, y, h, s, tc, tbt, pwr, dvec, name):
    L, D = h.shape
    nkb, KB, P2 = tc.shape
    P = P2 // 2
    W = (KB // SSM_GROUP) * P
    W2 = 2 * W
    dims = (KB, W2, SSM_GROUP, P, W)
    TL = _tile(L, (512, 256))
    CH = TL // 8
    nt = L // TL

    def body(dz_ref, y_ref, h_ref, s_ref, sp_ref, tc_ref, tbt_ref, pw_ref, d_ref,
             dh_ref, dd_ref, da_ref, db_ref, dc_ref, g, ctw, btw, dbacc, dcacc, perm, unperm, carry):
        t = pl.program_id(1)

        @pl.when(t == 0)
        def _():
            carry[...] = jnp.zeros_like(carry)
            dd_ref[...] = jnp.zeros_like(dd_ref)
            da_ref[...] = jnp.zeros_like(da_ref)
            dbacc[...] = jnp.zeros_like(dbacc)
            dcacc[...] = jnp.zeros_like(dcacc)
            ctw[...] = _expand(tc_ref[...], dims, False)
            btw[...] = _expand(tbt_ref[...], dims, True)
            perm[...] = _chunk_order(TL, CH, False).astype(perm.dtype)
            unperm[...] = _chunk_order(TL, CH, True).astype(perm.dtype)

        hp = jnp.dot(perm[...], h_ref[...].astype(BF), preferred_element_type=F32)
        dy = jnp.dot(perm[...], dz_ref[...].astype(BF), preferred_element_type=F32) * _gelu_grad(y_ref[...])
        dd_ref[...] += _rowsum8(dy * hp)
        dyb = dy.astype(BF)
        g[...] = jnp.dot(dyb, ctw[...], preferred_element_type=F32)
        ar, ai = pw_ref[0:8, :W], pw_ref[0:8, W:]

        def own(jj, x):
            rows = pl.ds(pl.multiple_of((CH - 1 - jj) * 8, 8), 8)
            gr, gi = _cmul_add(g[rows, :W], g[rows, W:], ar, ai, x[0], x[1])
            g[rows, :W] = gr
            g[rows, W:] = gi
            return gr, gi

        zero = jnp.zeros((8, W), F32)
        gr, gi = lax.fori_loop(0, CH, own, (zero, zero))
        for k, off in ((1, 8), (2, 16), (4, 24)):
            gr, gi = _cmul_add(gr, gi, pw_ref[off:off + 8, :W], pw_ref[off:off + 8, W:],
                               pltpu.roll(gr, 8 - k, 0), pltpu.roll(gi, 8 - k, 0))
        gr, gi = _cmul_add(gr, gi, pw_ref[32:40, :W], pw_ref[32:40, W:], carry[0], carry[1])
        sub = lax.broadcasted_iota(jnp.int32, (8, W), 0)
        cr = jnp.where(sub == 7, carry[0], pltpu.roll(gr, 7, 0))
        ci = jnp.where(sub == 7, carry[1], pltpu.roll(gi, 7, 0))
        carry[0] = jnp.broadcast_to(gr[0:1], (8, W))
        carry[1] = jnp.broadcast_to(gi[0:1], (8, W))

        def carried(jj, c):
            j = CH - 1 - jj
            rows = pl.ds(pl.multiple_of(j * 8, 8), 8)
            before = pl.ds(pl.multiple_of(j * 8 - 8, 8), 8)
            cr, ci = _cmul(ar, ai, c[0], c[1])
            gr, gi = g[rows, :W] + cr, g[rows, W:] + ci
            g[rows, :W] = gr
            g[rows, W:] = gi
            pr, pi = s_ref[before, :W], s_ref[before, W:]
            return cr, ci, c[2] + pr * gr + pi * gi, c[3] + pr * gi - pi * gr

        cr, ci, accr, acci = lax.fori_loop(0, CH - 1, carried, (cr, ci, zero, zero))
        live = jnp.where(t == nt - 1, 0.0, 1.0)
        cr, ci = _cmul(ar, ai, cr, ci)
        gr, gi = g[0:8, :W] + cr, g[0:8, W:] + ci
        g[0:8, :W] = gr
        g[0:8, W:] = gi
        pr = jnp.where(sub == 0, sp_ref[7:8, :W] * live, pltpu.roll(s_ref[TL - 8:TL, :W], 1, 0))
        pi = jnp.where(sub == 0, sp_ref[7:8, W:] * live, pltpu.roll(s_ref[TL - 8:TL, W:], 1, 0))
        da_ref[:, :W] += accr + pr * gr + pi * gi
        da_ref[:, W:] += acci + pr * gi - pi * gr

        gb = g[...].astype(BF)
        dh = dy * d_ref[...] + jnp.dot(gb, btw[...], preferred_element_type=F32)
        dh_ref[...] = _reorder(unperm[...], dh)
        tn = (((0,), (0,)), ((), ()))
        dbacc[...] += lax.dot_general(hp.astype(BF), gb, tn, preferred_element_type=F32)
        dcacc[...] += lax.dot_general(dyb, s_ref[...].astype(BF), tn, preferred_element_type=F32)

        @pl.when(t == nt - 1)
        def _():
            db_ref[...] = _extract(dbacc[...], dims)
            dc_ref[...] = _extract(dcacc[...], dims)

    rev = lambda kb, t: (nt - 1 - t, kb)
    prev = lambda kb, t: (jnp.maximum((nt - 1 - t) * CH - 1, 0), kb)
    per_kb = lambda kb, t: (kb, 0, 0)
    return pl.pallas_call(
        body, name=name, grid=(nkb, nt),
        in_specs=[pl.BlockSpec((TL, KB), rev), pl.BlockSpec((TL, KB), rev), pl.BlockSpec((TL, KB), rev),
                  pl.BlockSpec((TL, W2), rev), pl.BlockSpec((8, W2), prev),
                  pl.BlockSpec((None, KB, P2), per_kb), pl.BlockSpec((None, P2, KB), per_kb),
                  pl.BlockSpec((None, 40, W2), per_kb), pl.BlockSpec((1, KB), lambda kb, t: (0, kb))],
        out_specs=[pl.BlockSpec((TL, KB), rev), pl.BlockSpec((8, KB), lambda kb, t: (0, kb)),
                   pl.BlockSpec((None, 8, W2), per_kb), pl.BlockSpec((None, KB, P2), per_kb),
                   pl.BlockSpec((None, KB, P2), per_kb)],
        out_shape=[jax.ShapeDtypeStruct((L, D), F32), jax.ShapeDtypeStruct((8, D), F32),
                   jax.ShapeDtypeStruct((nkb, 8, W2), F32), jax.ShapeDtypeStruct((nkb, KB, P2), F32),
                   jax.ShapeDtypeStruct((nkb, KB, P2), F32)],
        scratch_shapes=[pltpu.VMEM((TL, W2), F32), pltpu.VMEM((KB, W2), BF), pltpu.VMEM((W2, KB), BF),
                        pltpu.VMEM((KB, W2), F32), pltpu.VMEM((KB, W2), F32), pltpu.VMEM((TL, TL), BF),
                        pltpu.VMEM((TL, TL), BF), pltpu.VMEM((2, 8, W), F32)],
        compiler_params=_cp("parallel", "arbitrary"))(dz, y, h, s, s, tc, tbt, pwr, dvec)


def _discretise(a_re, a_im, log_step, b_re, b_im):
    lr = jnp.minimum(a_re, -1e-4)
    li = a_im
    dt = jnp.exp(log_step)[:, None]
    mag = jnp.exp(lr * dt)
    abr = mag * jnp.cos(li * dt)
    abi = mag * jnp.sin(li * dt)
    den = lr * lr + li * li
    qr = ((abr - 1.0) * lr + abi * li) / den
    qi = (abi * lr - (abr - 1.0) * li) / den
    bbar_re = qr[..., None] * b_re - qi[..., None] * b_im
    bbar_im = qr[..., None] * b_im + qi[..., None] * b_re
    return abr, abi, bbar_re, bbar_im


def _compact(m_re, m_im, nkb):
    G, H, P = m_re.shape
    t = jnp.stack([m_re, m_im], axis=2).reshape(nkb, (G // nkb) * H, 2 * P).astype(BF)
    return t, jnp.swapaxes(t, 1, 2)


def _scan_powers(abr, abi, nkb, conj, CH):
    G, P = abr.shape
    if conj:
        abi = -abi

    def cmul(u, v):
        return u[0] * v[0] - u[1] * v[1], u[0] * v[1] + u[1] * v[0]

    q = (abr, abi)
    for _ in range(_log2(CH)):
        q = cmul(q, q)
    pows = [q]
    for _ in range(7):
        pows.append(cmul(pows[-1], q))
    row = jnp.arange(8)[:, None, None]

    def table(part):
        out = [jnp.broadcast_to((abr, abi)[part][None], (8, G, P))]
        for k in (1, 2, 4):
            keep = (row <= 7 - k) if conj else (row >= k)
            out.append(jnp.where(keep, pows[k - 1][part][None], 0.0))
        ends = jnp.stack([p[part] for p in pows])
        out.append(ends[::-1] if conj else ends)
        return jnp.concatenate(out, axis=0)

    GL = G // nkb
    t = jnp.stack([table(0), table(1)], axis=1)
    t = t.reshape(40, 2, nkb, GL * P).transpose(2, 0, 1, 3)
    return t.reshape(nkb, 40, 2 * GL * P)


def ada_mods(c_all, w_ada, b_sh, name):
    nl, D, NA = w_ada.shape

    def body(c_ref, w_ref, b_ref, o_ref):
        cv = c_ref[...]
        act = cv * jax.nn.sigmoid(cv)
        o_ref[...] = jnp.dot(act, w_ref[...], preferred_element_type=F32, precision=lax.Precision.HIGHEST) + b_ref[...]

    return pl.pallas_call(
        body, name=name, grid=(nl,),
        in_specs=[pl.BlockSpec((8, D), lambda i: (0, 0)), pl.BlockSpec((None, D, NA), lambda i: (i, 0, 0)),
                  pl.BlockSpec((None, 1, NA), lambda i: (i, 0, 0))],
        out_specs=pl.BlockSpec((None, 8, NA), lambda i: (i, 0, 0)),
        out_shape=jax.ShapeDtypeStruct((nl, 8, NA), F32), compiler_params=_cp("parallel"))(c_all, w_ada, b_sh)


def _adamw(w, g, m, v):
    m = ADAM_B1 * m + (1.0 - ADAM_B1) * g
    v = ADAM_B2 * v + (1.0 - ADAM_B2) * (g * g)
    m_hat = m / (1.0 - ADAM_B1 ** ADAM_STEP)
    v_hat = v / (1.0 - ADAM_B2 ** ADAM_STEP)
    return -ADAM_LR * (m_hat / (jnp.sqrt(v_hat) + ADAM_EPS) + ADAM_WD * w), m, v


def _adam_rows(R, C):
    cap = max(8, (256 * 1024) // C)
    for t in range(min(R, cap), 0, -1):
        if R % t == 0 and (t % 8 == 0 or t == R):
            return t
    return R


def adamw_ada(c_t, dm, w, m, v, name):
    nl, D, NA = w.shape
    TK = _tile(D, (256, 128))

    def body(c_ref, dm_ref, w_ref, m_ref, v_ref, g_ref, d_ref, nm_ref, nv_ref):
        cv = c_ref[...]
        act = cv * jax.nn.sigmoid(cv)
        g = jnp.dot(act, dm_ref[...], preferred_element_type=F32, precision=lax.Precision.HIGHEST)
        g_ref[...] = g
        d_ref[...], nm_ref[...], nv_ref[...] = _adamw(w_ref[...], g, m_ref[...], v_ref[...])

    big = pl.BlockSpec((None, TK, NA), lambda i, k: (i, k, 0))
    shape = jax.ShapeDtypeStruct(w.shape, F32)
    return pl.pallas_call(
        body, name=name, grid=(nl, D // TK),
        in_specs=[pl.BlockSpec((TK, 8), lambda i, k: (k, 0)), pl.BlockSpec((None, 8, NA), lambda i, k: (i, 0, 0)),
                  big, big, big],
        out_specs=[big] * 4, out_shape=[shape] * 4, compiler_params=_cp("parallel", "parallel"))(c_t, dm, w, m, v)


def adamw_sharded(w, m, v, ga, gb, name):
    nl, R, C = w.shape
    TR = _adam_rows(R, C)

    def body(w_ref, m_ref, v_ref, a_ref, b_ref, g_ref, d_ref, nm_ref, nv_ref):
        g = a_ref[...] + b_ref[...]
        g_ref[...] = g
        d_ref[...], nm_ref[...], nv_ref[...] = _adamw(w_ref[...], g, m_ref[...], v_ref[...])

    big = pl.BlockSpec((None, TR, C), lambda i, r: (i, r, 0))
    shape = jax.ShapeDtypeStruct(w.shape, F32)
    return pl.pallas_call(
        body, name=name, grid=(nl, R // TR), in_specs=[big] * 5,
        out_specs=[big] * 4, out_shape=[shape] * 4, compiler_params=_cp("parallel", "parallel"))(w, m, v, ga, gb)


def adamw_slab(g, w, m, v, name):
    R, C = g.shape
    TR = _tile(R, (160, 80, 40, 8))

    def body(g_ref, w_ref, m_ref, v_ref, d_ref, nm_ref, nv_ref):
        d_ref[...], nm_ref[...], nv_ref[...] = _adamw(w_ref[...], g_ref[...], m_ref[...], v_ref[...])

    big = pl.BlockSpec((TR, C), lambda r: (r, 0))
    shape = jax.ShapeDtypeStruct((R, C), F32)
    return pl.pallas_call(
        body, name=name, grid=(R // TR,), in_specs=[big] * 4,
        out_specs=[big] * 3, out_shape=[shape] * 3, compiler_params=_cp("parallel"))(g, w, m, v)


def adamw_plain(w, m, v, g, name):
    def body(w_ref, m_ref, v_ref, g_ref, d_ref, nm_ref, nv_ref):
        d_ref[...], nm_ref[...], nv_ref[...] = _adamw(w_ref[...], g_ref[...], m_ref[...], v_ref[...])

    shape = jax.ShapeDtypeStruct(w.shape, F32)
    return pl.pallas_call(body, name=name, out_shape=[shape] * 3,
                          compiler_params=pltpu.CompilerParams(vmem_limit_bytes=VMEM_LIMIT))(w, m, v, g)


def _slab_rows(a):
    n = a.size
    rows = -(-n // SLAB_W)
    return -(-rows // 8) * 8


def _pack(arrs, pad_rows_to=0):
    out = []
    for a in arrs:
        rows = _slab_rows(a)
        flat = a.reshape(-1).astype(F32)
        flat = jnp.pad(flat, (0, rows * SLAB_W - flat.shape[0]))
        out.append(flat.reshape(rows, SLAB_W))
    total = sum(o.shape[0] for o in out)
    if pad_rows_to and total % pad_rows_to:
        out.append(jnp.zeros((pad_rows_to - total % pad_rows_to, SLAB_W), F32))
    return jnp.concatenate(out, axis=0)


def _unpack(slab, like):
    out, r = [], 0
    for a in like:
        rows = _slab_rows(a)
        out.append(slab[r:r + rows].reshape(-1)[:a.size].reshape(a.shape))
        r += rows
    return out


WEIGHTS = ['norm1_g', 'norm2_g', 'w_ada', 'b_ada', 'ssm_a_re', 'ssm_a_im', 'ssm_log_step', 'ssm_b_re', 'ssm_b_im',
           'ssm_c_re', 'ssm_c_im', 'ssm_d', 'ssm_w_out', 'conv_w_in', 'conv_w', 'conv_w_out', 'w_ffn_in',
           'w_ffn_out', 'final_g']
SLAB = ['norm1_g', 'norm2_g', 'b_ada', 'ssm_a_re', 'ssm_a_im', 'ssm_log_step', 'ssm_b_re', 'ssm_b_im', 'ssm_c_re',
        'ssm_c_im', 'ssm_d', 'final_g']
SHARDED = ['ssm_w_out', 'conv_w_in', 'conv_w_out', 'w_ffn_in', 'w_ffn_out']


def kernel(x, c, norm1_g, norm2_g, w_ada, b_ada, ssm_a_re, ssm_a_im, ssm_log_step, ssm_b_re, ssm_b_im, ssm_c_re, ssm_c_im, ssm_d, ssm_w_out, conv_w_in, conv_w, conv_w_out, w_ffn_in, w_ffn_out, final_g, loss_target, m_norm1_g, m_norm2_g, m_w_ada, m_b_ada, m_ssm_a_re, m_ssm_a_im, m_ssm_log_step, m_ssm_b_re, m_ssm_b_im, m_ssm_c_re, m_ssm_c_im, m_ssm_d, m_ssm_w_out, m_conv_w_in, m_conv_w, m_conv_w_out, m_w_ffn_in, m_w_ffn_out, m_final_g, v_norm1_g, v_norm2_g, v_w_ada, v_b_ada, v_ssm_a_re, v_ssm_a_im, v_ssm_log_step, v_ssm_b_re, v_ssm_b_im, v_ssm_c_re, v_ssm_c_im, v_ssm_d, v_ssm_w_out, v_conv_w_in, v_conv_w, v_conv_w_out, v_w_ffn_in, v_w_ffn_out, v_final_g):
    given = dict(locals())
    W = {n: given[n] for n in WEIGHTS}
    Mo = {n: given["m_" + n] for n in WEIGHTS}
    Vo = {n: given["v_" + n] for n in WEIGHTS}

    xs = x[0]
    tgt = loss_target[0]
    L, D = xs.shape
    nlayer = norm1_g.shape[0]
    NA = w_ada.shape[2]
    G = ssm_a_re.shape[1]
    nkb = D // S5_BLOCK
    ax, ay, ac = _axes()
    me = 4 * ax + 2 * ay + ac
    chip = 2 * ax + ay

    assert D == SLAB_W
    first = gather8(jnp.concatenate([jnp.broadcast_to(c, (8, D)), _pack([conv_w])], axis=0), "gather_c_conv_w")
    c_all = first[:, 0, :]
    b_sh = lax.dynamic_slice_in_dim(b_ada, chip * NA, NA, axis=1)[:, None, :]
    mods_part = ada_mods(c_all, w_ada, b_sh, "ada_mods")
    mg = gather8(mods_part.reshape(nlayer * 8, NA), "gather_mods")
    mg = mg.reshape(N_CHIP, 2, nlayer, 8, NA)[:, 0]
    mods_all = lax.dynamic_index_in_dim(mg, me, axis=2, keepdims=False)
    mods_all = jnp.transpose(mods_all, (1, 0, 2)).reshape(nlayer, 6, D)

    cw_parts = first[:, 8:]
    nconv = conv_w.shape[0]
    cw_full = jnp.stack([_unpack(cw_parts[2 * q], [conv_w])[0] for q in range(N_CHIP)], axis=2)
    cw_full = cw_full.reshape(nconv, 3, D)

    in_flight_w = {}

    def start_weights(i, after):
        names = (["ssm_w_out"] if i % 2 == 0 else ["conv_w_in", "conv_w_out"]) + ["w_ffn_in", "w_ffn_out"]
        shards = [W[n][i if n.startswith("w_ffn") else i // 2].astype(BF) for n in names]
        sems, srcs, lands, tok = gather_start(shards, after, "gather_start%d" % i)
        in_flight_w[i] = (names, sems, srcs, lands)
        return tok

    def relay_weights(i, after):
        names, sems, srcs, lands = in_flight_w[i]
        got = gather_wait(sems, srcs, lands, list(range(len(names))), after, "gather_wait%d" % i)
        rsems, rlands, tok = relay_start(got, after, "relay_start%d" % i)
        in_flight_w[i] = (names, rsems, rlands)
        return tok

    def layer_weights(i, after):
        names, rsems, rlands = in_flight_w[i]
        return dict(zip(names, relay_wait(rsems, rlands, after, "relay_wait%d" % i)))

    token = start_weights(0, cw_full + mods_all[0, 0:3])
    mods_all = mods_all + token[0:1, 0:1]

    s5 = []
    for j in range(ssm_a_re.shape[0]):
        disc, disc_vjp = jax.vjp(_discretise, ssm_a_re[j], ssm_a_im[j], ssm_log_step[j], ssm_b_re[j], ssm_b_im[j])
        abr, abi, bbar_re, bbar_im = disc
        tb, tbt = _compact(jnp.swapaxes(bbar_re, 1, 2), jnp.swapaxes(bbar_im, 1, 2), nkb)
        tc, tct = _compact(ssm_c_re[j], -ssm_c_im[j], nkb)
        chunk = _tile(L, (512, 256)) // 8
        s5.append(dict(vjp=disc_vjp, tb=tb, tbt=tbt, tc=tc, tct=tct, pw=_scan_powers(abr, abi, nkb, False, chunk),
                       pwr=_scan_powers(abr, abi, nkb, True, chunk)))

    saved = []
    xcur = xs
    for i in range(nlayer):
        j = i // 2
        mods = mods_all[i]
        sv = dict(x=xcur)
        if i % 2 == 0:
            h = norm_mod(xcur, norm1_g[i:i + 1], mods, 0, F32, "norm_mod_s5")
            dvec = ssm_d[j:j + 1]
            if i == 0:
                dvec = dvec + start_weights(1, h)[0:1, 0:1]
            states, yv, z, h = s5_fwd(h, s5[j]["tb"], s5[j]["tct"], s5[j]["pw"], dvec, "s5_fwd")
            if i == 0:
                mods = mods + relay_weights(0, z)[0:1, 0:1]
            full = layer_weights(i, z)
            o, mix, x2 = ssm_out_glu(z, full["ssm_w_out"], xcur, mods, 2, "ssm_out_glu")
            sv.update(h=h, states=states, y=yv, z=z, o=o)
        else:
            h = norm_mod(xcur, norm1_g[i:i + 1], mods, 0, BF, "norm_mod")
            full = layer_weights(i, h)
            p = mm_nn(h, full["conv_w_in"], BF, "mm_conv_in")
            mc = conv_fwd(p, cw_full[j], "conv_fwd")
            mix, x2 = mm_nn(mc, full["conv_w_out"].reshape(1, D, D), BF, "mm_conv_out", res=xcur, gate=mods[2:3])
            sv.update(h=h, p=p, mc=mc)
        h2 = norm_mod(x2, norm2_g[i:i + 1], mods, 3, BF, "norm_mod")
        gu, act = ffn_in_act(h2, full["w_ffn_in"], "ffn_in_act")
        if i + 1 < nlayer:
            token = relay_weights(i + 1, act)
            if i + 2 < nlayer:
                token = token + start_weights(i + 2, token)
            mods = mods + token[0:1, 0:1]
        F = act.shape[1]
        ff, x3 = mm_nn(act, full["w_ffn_out"].reshape(1, F, D), BF, "mm_ffn_out", res=x2, gate=mods[5:6])
        sv.update(mix=mix, x2=x2, h2=h2, gu=gu, act=act, ff=ff, w=full)
        saved.append(sv)
        xcur = x3

    loss_blk, dx, dfinal, dff = final_loss(xcur, tgt, final_g[None, :], saved[-1]["ff"], mods_all[nlayer - 1], 5,
                                           "final_loss")
    dg2 = dfinal[1:2]

    gland = {n: lax.empty((W[n].shape[0], N_CHIP) + W[n].shape[1:], BF) for n in SHARDED}
    in_flight = []
    dmods = [None] * nlayer
    dnorm1, dnorm2 = [None] * nlayer, [None] * nlayer
    dconv_w = [None] * nconv
    ds5 = [None] * ssm_a_re.shape[0]
    token = jnp.zeros((8, 128), F32)

    def send_grads(names, grads, slot, after, name):
        sems, thru, lands, tok = scatter_start([grads[n] for n in names], [gland[n] for n in names], slot, after, name)
        gland.update(zip(names, lands))
        in_flight.append((names, slot, sems, thru, name))
        return tok

    def land_grads(group, after):
        for names, slot, sems, thru, name in in_flight:
            if names[0] in group:
                got = scatter_wait(sems, thru, [gland[n] for n in names], slot, after, name.replace("scatter", "landed"))
                gland.update(zip(names, got))

    for i in reversed(range(nlayer)):
        j = i // 2
        mods = mods_all[i] + token[0:1, 0:1]
        sv = saved[i]
        full = sv["w"]
        gfull = {}
        F = sv["act"].shape[1]
        gfull["w_ffn_out"] = mm_tn(sv["act"], dff, 1, "mm_tn_ffn_out").reshape(N_CHIP, F // N_CHIP, D)
        dgu = ffn_out_bwd(dff, full["w_ffn_out"].reshape(F, D), sv["gu"], "ffn_out_bwd")
        gfull["w_ffn_in"] = mm_tn(sv["h2"], dgu, N_CHIP, "mm_tn_ffn_in")
        dh2 = mm_nt(dgu, full["w_ffn_in"], BF, "mm_nt_ffn_in")
        token = send_grads(["w_ffn_out", "w_ffn_in"], gfull, [i, i], dh2, "scatter_ffn%d" % i)
        mods = mods + token[0:1, 0:1]
        dx2, s2, dmix = norm_bwd(dh2, sv["x2"], dx, norm2_g[i:i + 1], mods, 3, "norm_bwd_mix",
                                 branch=(sv["mix"], mods, 2))
        dg1 = s2[3:4]
        if i % 2 == 0:
            do = glu_bwd(dmix, sv["o"], "glu_bwd")
            gfull["ssm_w_out"] = mm_tn(sv["z"], do, N_CHIP, "mm_tn_ssm_out")
            dz = mm_nt(do, full["ssm_w_out"], BF, "mm_nt_ssm_out")
            dh, dd, dab, db, dc = s5_bwd(dz, sv["y"], sv["h"], sv["states"], s5[j]["tc"], s5[j]["tbt"], s5[j]["pwr"],
                                         ssm_d[j:j + 1], "s5_bwd")
            ds5[j] = (dd, dab, db, dc)
        else:
            gfull["conv_w_out"] = mm_tn(sv["mc"], dmix, 1, "mm_tn_conv_out").reshape(N_CHIP, D // N_CHIP, D)
            dmc = mm_nt(dmix, full["conv_w_out"].reshape(1, D, D), BF, "mm_nt_conv_out")
            dbg, dcg, dvv, dcw = conv_bwd(dmc, sv["p"], cw_full[j], "conv_bwd")
            dp = jnp.concatenate([dbg, dcg, dvv], axis=1)
            gfull["conv_w_in"] = mm_tn(sv["h"], dp, N_CHIP, "mm_tn_conv_in")
            dh = mm_nt(dp, full["conv_w_in"], BF, "mm_nt_conv_in")
            dconv_w[j] = dcw[0:3]
        dmods_i = [s2[0:2], dg2]
        if i > 0:
            dx, s1, dff = norm_bwd(dh, sv["x"], dx2, norm1_g[i:i + 1], mods, 0, "norm_bwd_ffn",
                                   branch=(saved[i - 1]["ff"], mods_all[i - 1], 5))
            dg2 = s1[3:4]
        else:
            dx, s1 = norm_bwd(dh, sv["x"], dx2, norm1_g[i:i + 1], mods, 0, "norm_bwd")
        dmods[i] = jnp.concatenate([s1[0:2], dg1] + dmods_i, axis=0).reshape(6 * D)
        dnorm1[i], dnorm2[i] = s1[2], s2[2]
        names = ["ssm_w_out"] if i % 2 == 0 else ["conv_w_out", "conv_w_in"]
        token = send_grads(names, gfull, [j] * len(names), dx, "scatter_mix%d" % i)

    small = dict(norm1_g=jnp.stack(dnorm1), norm2_g=jnp.stack(dnorm2), b_ada=jnp.stack(dmods), final_g=dfinal[0])
    per = {n: [] for n in ('ssm_a_re', 'ssm_a_im', 'ssm_log_step', 'ssm_b_re', 'ssm_b_im', 'ssm_c_re', 'ssm_c_im', 'ssm_d')}
    GL = G // nkb
    for j, (dd, dab, db, dc) in enumerate(ds5):
        dab = jnp.sum(dab, axis=1).reshape(nkb, 2, GL, SSM_STATE)
        g_abr, g_abi = dab[:, 0].reshape(G, SSM_STATE), dab[:, 1].reshape(G, SSM_STATE)
        db, dc = db.reshape(G, SSM_GROUP, 2, SSM_STATE), dc.reshape(G, SSM_GROUP, 2, SSM_STATE)
        gb_re, gb_im, gc_re, gc_im = db[:, :, 0], db[:, :, 1], dc[:, :, 0], dc[:, :, 1]
        ga_re, ga_im, gls, gbr, gbi = s5[j]["vjp"]((g_abr, g_abi, jnp.swapaxes(gb_re, 1, 2), jnp.swapaxes(gb_im, 1, 2)))
        for n, val in zip(per, (ga_re, ga_im, gls, gbr, gbi, gc_re, -gc_im, jnp.sum(dd, axis=0))):
            per[n].append(val)
    small.update({n: jnp.stack(vals) for n, vals in per.items()})
    dcw_full = jnp.stack(dconv_w)

    my_loss = loss_blk[0:1, 0:1]
    slab_like = [W[n] for n in SLAB] + [dcw_full, my_loss]
    rows64 = 8 * N_DEV
    slab = _pack([small[n] for n in SLAB] + [dcw_full, my_loss], rows64)
    per_dev = slab.shape[0] // N_DEV
    x_sems, x_srcs, x_lands, token = exchange_start(
        [(slab.reshape(N_DEV, per_dev, SLAB_W), True), (_pack([small["b_ada"]]), False)], dx, "small_scatter")

    early = [n for n in SHARDED if n != "ssm_w_out"]
    land_grads(early, token)
    mine = [reduce4(gland[n], "reduce4_" + n) for n in early]

    parts, dm_all = exchange_wait(x_sems, x_srcs, x_lands, [True, False], mine[-1][0, :8, :128], "small_landed")
    t_sems, t_srcs, t_lands, token = exchange_start([(sum8(parts, "sum_small"), False)], dm_all, "small_gather")
    out = {}

    w_sems, w_srcs, w_lands, token2 = swap_start(mine, "swap_start")
    dm_all = dm_all.reshape(N_DEV, -1)[:, :b_ada.size].reshape(N_DEV, nlayer, N_CHIP, NA)
    dm_sh = jnp.transpose(lax.dynamic_index_in_dim(dm_all, chip, axis=2, keepdims=False), (1, 0, 2))
    res = adamw_ada(jnp.transpose(c_all) + token[0:1, 0:1] + token2[0:1, 0:1], dm_sh, w_ada, m_w_ada, v_w_ada,
                    "adamw_ada")
    out["g", "w_ada"], out["d", "w_ada"], out["m", "w_ada"], out["v", "w_ada"] = res

    g_slab = exchange_wait(t_sems, t_srcs, t_lands, [False], out["g", "w_ada"], "small_total")[0]
    g_slab = g_slab.reshape(slab.shape)
    d_slab, m_slab, v_slab = adamw_slab(
        g_slab, _pack([W[n] for n in SLAB] + [jnp.zeros_like(dcw_full)], rows64),
        _pack([Mo[n] for n in SLAB] + [jnp.zeros_like(dcw_full)], rows64),
        _pack([Vo[n] for n in SLAB] + [jnp.ones_like(dcw_full)], rows64), "adamw_slab")
    for k, slab in zip(("g", "d", "m", "v"), (g_slab, d_slab, m_slab, v_slab)):
        for n, val in zip(SLAB, _unpack(slab, slab_like)):
            out[k, n] = val
    g_cw = lax.dynamic_slice_in_dim(_unpack(g_slab, slab_like)[-2], chip * conv_w.shape[2], conv_w.shape[2], axis=2)
    out["g", "conv_w"] = g_cw
    out["d", "conv_w"], out["m", "conv_w"], out["v", "conv_w"] = [
        r.reshape(conv_w.shape) for r in adamw_plain(conv_w.reshape(-1, conv_w.shape[2]), m_conv_w.reshape(-1, conv_w.shape[2]),
                                                     v_conv_w.reshape(-1, conv_w.shape[2]), g_cw.reshape(-1, conv_w.shape[2]),
                                                     "adamw_conv_w")]

    mine, theirs = swap_wait(w_sems, w_srcs, w_lands, d_slab, "swap_wait")
    for n, ga, gb in zip(early, mine, theirs):
        r = adamw_sharded(W[n], Mo[n], Vo[n], ga, gb, "adamw_" + n)
        out["g", n], out["d", n], out["m", n], out["v", n] = r

    land_grads(["ssm_w_out"], out["g", "w_ffn_out"])
    ga = reduce4(gland["ssm_w_out"], "reduce4_ssm_w_out")
    gb = swap_siblings([ga], "swap_siblings")[0]
    r = adamw_sharded(ssm_w_out, m_ssm_w_out, v_ssm_w_out, ga, gb, "adamw_ssm_w_out")
    out["g", "ssm_w_out"], out["d", "ssm_w_out"], out["m", "ssm_w_out"], out["v", "ssm_w_out"] = r

    loss = _unpack(g_slab, slab_like)[-1][0, 0]
    return (loss, dx[None], *[out["g", n] for n in WEIGHTS], *[out["d", n] for n in WEIGHTS],
            *[out["m", n] for n in WEIGHTS], *[out["v", n] for n in WEIGHTS])
```

```python
import math

import jax
import jax.numpy as jnp
from jax import lax
from jax.experimental import pallas as pl
from jax.experimental.pallas import tpu as pltpu

F32 = jnp.float32
BF = jnp.bfloat16
MESH = pl.DeviceIdType.MESH
ANY = pl.BlockSpec(memory_space=pl.ANY)

N_DEV = 8
N_CHIP = 4
SSM_GROUP = 16
SSM_STATE = 64
S5_BLOCK = 256
RMS_EPS = 1e-6
ADAM_LR, ADAM_B1, ADAM_B2, ADAM_EPS, ADAM_WD, ADAM_STEP = 0.001, 0.9, 0.999, 1e-08, 0.01, 10
V7X_VMEM_BYTES = 64 * 1024 * 1024
VMEM_LIMIT = V7X_VMEM_BYTES - 12 * 1024 * 1024
SLAB_W = 1024
GELU_C = math.sqrt(2.0 / math.pi)
GELU_A = 0.044715


def _cp(*sem):
    return pltpu.CompilerParams(dimension_semantics=sem if sem else None, vmem_limit_bytes=VMEM_LIMIT)


def _tile(n, prefs):
    for p in prefs:
        if p <= n and n % p == 0:
            return p
    return n


def _axes():
    return lax.axis_index("x"), lax.axis_index("y"), lax.axis_index("c")


def _flip(v, k):
    return 1 - v if k else v


def gather8(v, name):
    R, C = v.shape

    def body(v_ref, o_ref, ssem, rsem, lsem):
        x, y, c = _axes()
        me = 4 * x + 2 * y + c
        loc = pltpu.make_async_copy(v_ref, o_ref.at[me], lsem)
        loc.start()
        copies = []
        for k in range(1, N_DEV):
            peer = (_flip(x, (k >> 2) & 1), _flip(y, (k >> 1) & 1), _flip(c, k & 1))
            cp = pltpu.make_async_remote_copy(src_ref=v_ref, dst_ref=o_ref.at[me], send_sem=ssem.at[k - 1],
                                              recv_sem=rsem.at[k - 1], device_id=peer, device_id_type=MESH)
            cp.start()
            copies.append(cp)
        for cp in copies:
            cp.wait()
        loc.wait()

    return pl.pallas_call(
        body, name=name,
        out_shape=jax.ShapeDtypeStruct((N_DEV, R, C), v.dtype),
        in_specs=[pl.BlockSpec(memory_space=pltpu.VMEM)],
        out_specs=pl.BlockSpec(memory_space=pltpu.VMEM),
        scratch_shapes=[pltpu.SemaphoreType.DMA((N_DEV - 1,)), pltpu.SemaphoreType.DMA((N_DEV - 1,)),
                        pltpu.SemaphoreType.DMA],
        compiler_params=pltpu.CompilerParams(vmem_limit_bytes=VMEM_LIMIT),
    )(v)


HBM = pl.BlockSpec(memory_space=pltpu.HBM)
SEM = pl.BlockSpec(memory_space=pltpu.SEMAPHORE)
EFFECT = pltpu.SideEffectType.DATAFLOW_SIDE_EFFECTING


def _in_hbm(a):
    return pltpu.with_memory_space_constraint(a, pltpu.HBM)


def _chip_peers(x, y, c):
    out = []
    for k in range(1, N_CHIP):
        px, py = _flip(x, k >> 1), _flip(y, k & 1)
        out.append(((px, py, c), 2 * px + py))
    return out


def _my_half(ref, c):
    rows = ref.shape[0] // 2
    return pl.ds(pl.multiple_of(c * rows, 16), rows)


def relay_start(lands, after, name):
    n = len(lands)

    def body(*refs):
        land = refs[:n]
        ssem, rsem = refs[n + 1:n + 3]
        token = refs[-1]
        x, y, c = _axes()
        for a in range(n):
            half = _my_half(land[a].at[0], c)
            for k, (_, pchip) in enumerate(_chip_peers(x, y, c)):
                pltpu.make_async_remote_copy(src_ref=land[a].at[pchip, half], dst_ref=land[a].at[pchip, half],
                                             send_sem=ssem.at[3 * a + k], recv_sem=rsem.at[3 * a + k],
                                             device_id=(x, y, 1 - c), device_id_type=MESH).start()
        token[...] = jnp.zeros_like(token)

    out_shape = ([pltpu.SemaphoreType.DMA((3 * n,)), pltpu.SemaphoreType.DMA((3 * n,))]
                 + [pltpu.HBM(l.shape, l.dtype) for l in lands] + [jax.ShapeDtypeStruct((8, 128), F32)])
    res = pl.pallas_call(
        body, name=name, out_shape=out_shape, in_specs=[HBM] * n + [ANY],
        out_specs=[SEM, SEM] + [HBM] * n + [pl.BlockSpec(memory_space=pltpu.VMEM)],
        input_output_aliases={a: 2 + a for a in range(n)},
        compiler_params=pltpu.CompilerParams(has_side_effects=EFFECT),
    )(*lands, after)
    return tuple(res[:2]), list(res[2:2 + n]), res[-1]


def relay_wait(sems, lands, after, name):
    n = len(lands)

    def body(*refs):
        land = refs[:n]
        ssem, rsem = refs[n:n + 2]
        x, y, c = _axes()
        for a in range(n):
            mine, theirs = _my_half(land[a].at[0], c), _my_half(land[a].at[0], 1 - c)
            for k, (_, pchip) in enumerate(_chip_peers(x, y, c)):
                cp = pltpu.make_async_remote_copy(src_ref=land[a].at[pchip, mine], dst_ref=land[a].at[pchip, theirs],
                                                  send_sem=ssem.at[3 * a + k], recv_sem=rsem.at[3 * a + k],
                                                  device_id=(x, y, 1 - c), device_id_type=MESH)
                cp.wait_send()
                cp.wait_recv()

    res = pl.pallas_call(
        body, name=name, out_shape=[pltpu.HBM(l.shape, l.dtype) for l in lands],
        in_specs=[HBM] * n + [SEM, SEM, ANY], out_specs=[HBM] * n,
        input_output_aliases={a: a for a in range(n)},
        compiler_params=pltpu.CompilerParams(has_side_effects=EFFECT),
    )(*lands, *sems, after)
    return list(res)


def gather_start(shards, after, name):
    n = len(shards)

    def body(*refs):
        src, land = refs[:n], refs[n:2 * n]
        ssem, rsem, lsem = refs[2 * n + 1:2 * n + 4]
        token = refs[-1]
        x, y, c = _axes()
        chip = 2 * x + y
        for a in range(n):
            pltpu.make_async_copy(src[a], land[a].at[chip], lsem.at[a]).start()
            half = _my_half(src[a], c)
            for k, (peer, _) in enumerate(_chip_peers(x, y, c)):
                pltpu.make_async_remote_copy(src_ref=src[a].at[half], dst_ref=land[a].at[chip, half],
                                             send_sem=ssem.at[3 * a + k], recv_sem=rsem.at[3 * a + k],
                                             device_id=peer, device_id_type=MESH).start()
        token[...] = jnp.zeros_like(token)

    lands = [lax.empty((N_CHIP,) + s.shape, s.dtype) for s in shards]
    out_shape = ([pltpu.SemaphoreType.DMA((3 * n,)), pltpu.SemaphoreType.DMA((3 * n,)), pltpu.SemaphoreType.DMA((n,))]
                 + [pltpu.HBM(s.shape, s.dtype) for s in shards] + [pltpu.HBM(l.shape, l.dtype) for l in lands]
                 + [jax.ShapeDtypeStruct((8, 128), F32)])
    res = pl.pallas_call(
        body, name=name, out_shape=out_shape, in_specs=[HBM] * (2 * n) + [ANY],
        out_specs=[SEM, SEM, SEM] + [HBM] * (2 * n) + [pl.BlockSpec(memory_space=pltpu.VMEM)],
        input_output_aliases={a: 3 + a for a in range(2 * n)},
        compiler_params=pltpu.CompilerParams(has_side_effects=EFFECT),
    )(*[_in_hbm(s) for s in shards], *[_in_hbm(l) for l in lands], after)
    return tuple(res[:3]), list(res[3:3 + n]), list(res[3 + n:3 + 2 * n]), res[-1]


def gather_wait(sems, srcs, lands, idx, after, name):
    m = len(idx)

    def body(*refs):
        src, land = refs[:m], refs[m:2 * m]
        ssem, rsem, lsem = refs[2 * m:2 * m + 3]
        x, y, c = _axes()
        chip = 2 * x + y
        for j, a in enumerate(idx):
            half = _my_half(src[j], c)
            for k, (peer, pchip) in enumerate(_chip_peers(x, y, c)):
                cp = pltpu.make_async_remote_copy(src_ref=src[j].at[half], dst_ref=land[j].at[pchip, half],
                                                  send_sem=ssem.at[3 * a + k], recv_sem=rsem.at[3 * a + k],
                                                  device_id=peer, device_id_type=MESH)
                cp.wait_send()
                cp.wait_recv()
            pltpu.make_async_copy(src[j], land[j].at[chip], lsem.at[a]).wait()

    s_in = [srcs[a] for a in idx]
    l_in = [lands[a] for a in idx]
    res = pl.pallas_call(
        body, name=name,
        out_shape=[pltpu.HBM(s.shape, s.dtype) for s in s_in] + [pltpu.HBM(l.shape, l.dtype) for l in l_in],
        in_specs=[HBM] * (2 * m) + [SEM, SEM, SEM, ANY], out_specs=[HBM] * (2 * m),
        input_output_aliases={a: a for a in range(2 * m)},
        compiler_params=pltpu.CompilerParams(has_side_effects=EFFECT),
    )(*s_in, *l_in, *sems, after)
    return list(res[m:])


def scatter_start(grads, lands, slot, after, name):
    n = len(grads)

    def body(*refs):
        src, land = refs[:n], refs[n:2 * n]
        ssem, rsem, lsem = refs[2 * n + 1:2 * n + 4]
        token = refs[-1]
        x, y, c = _axes()
        chip = 2 * x + y
        for a in range(n):
            pltpu.make_async_copy(src[a].at[chip], land[a].at[slot[a], chip], lsem.at[a]).start()
            for k, (peer, pchip) in enumerate(_chip_peers(x, y, c)):
                pltpu.make_async_remote_copy(src_ref=src[a].at[pchip], dst_ref=land[a].at[slot[a], chip],
                                             send_sem=ssem.at[3 * a + k], recv_sem=rsem.at[3 * a + k],
                                             device_id=peer, device_id_type=MESH).start()
        token[...] = jnp.zeros_like(token)

    out_shape = ([pltpu.SemaphoreType.DMA((3 * n,)), pltpu.SemaphoreType.DMA((3 * n,)), pltpu.SemaphoreType.DMA((n,))]
                 + [pltpu.HBM(g.shape, g.dtype) for g in grads] + [pltpu.HBM(l.shape, l.dtype) for l in lands]
                 + [jax.ShapeDtypeStruct((8, 128), F32)])
    res = pl.pallas_call(
        body, name=name, out_shape=out_shape, in_specs=[HBM] * (2 * n) + [ANY],
        out_specs=[SEM, SEM, SEM] + [HBM] * (2 * n) + [pl.BlockSpec(memory_space=pltpu.VMEM)],
        input_output_aliases={a: 3 + a for a in range(2 * n)},
        compiler_params=pltpu.CompilerParams(has_side_effects=EFFECT),
    )(*[_in_hbm(g) for g in grads], *[_in_hbm(l) for l in lands], after)
    return tuple(res[:3]), list(res[3:3 + n]), list(res[3 + n:3 + 2 * n]), res[-1]


def scatter_wait(sems, grads, lands, slot, after, name):
    n = len(grads)

    def body(*refs):
        src, land = refs[:n], refs[n:2 * n]
        ssem, rsem, lsem = refs[2 * n:2 * n + 3]
        x, y, c = _axes()
        chip = 2 * x + y
        for a in range(n):
            for k, (peer, pchip) in enumerate(_chip_peers(x, y, c)):
                cp = pltpu.make_async_remote_copy(src_ref=src[a].at[pchip], dst_ref=land[a].at[slot[a], pchip],
                                                  send_sem=ssem.at[3 * a + k], recv_sem=rsem.at[3 * a + k],
                                                  device_id=peer, device_id_type=MESH)
                cp.wait_send()
                cp.wait_recv()
            pltpu.make_async_copy(src[a].at[chip], land[a].at[slot[a], chip], lsem.at[a]).wait()

    res = pl.pallas_call(
        body, name=name,
        out_shape=[pltpu.HBM(g.shape, g.dtype) for g in grads] + [pltpu.HBM(l.shape, l.dtype) for l in lands],
        in_specs=[HBM] * (2 * n) + [SEM, SEM, SEM, ANY], out_specs=[HBM] * (2 * n),
        input_output_aliases={a: a for a in range(2 * n)},
        compiler_params=pltpu.CompilerParams(has_side_effects=EFFECT),
    )(*grads, *lands, *sems, after)
    return list(res[n:])


def reduce4(land, name):
    nl, _, R, C = land.shape
    TR = _adam_rows(R, C)

    def body(l_ref, o_ref):
        o_ref[...] = ((l_ref[0].astype(F32) + l_ref[1].astype(F32)) + l_ref[2].astype(F32)) + l_ref[3].astype(F32)

    return pl.pallas_call(
        body, name=name, grid=(nl, R // TR),
        in_specs=[pl.BlockSpec((None, N_CHIP, TR, C), lambda i, r: (i, 0, r, 0))],
        out_specs=pl.BlockSpec((None, TR, C), lambda i, r: (i, r, 0)),
        out_shape=jax.ShapeDtypeStruct((nl, R, C), F32), compiler_params=_cp("parallel", "parallel"))(land)


def swap_siblings(arrs, name):
    n = len(arrs)

    def body(*refs):
        src, dst = refs[:n], refs[n:2 * n]
        ssem, rsem = refs[2 * n:]
        x, y, c = _axes()
        cps = [pltpu.make_async_remote_copy(src_ref=src[a], dst_ref=dst[a], send_sem=ssem.at[a], recv_sem=rsem.at[a],
                                            device_id=(x, y, 1 - c), device_id_type=MESH) for a in range(n)]
        for cp in cps:
            cp.start()
        for cp in cps:
            cp.wait()

    return pl.pallas_call(
        body, name=name, out_shape=[jax.ShapeDtypeStruct(a.shape, a.dtype) for a in arrs],
        in_specs=[ANY] * n, out_specs=[ANY] * n,
        scratch_shapes=[pltpu.SemaphoreType.DMA((n,)), pltpu.SemaphoreType.DMA((n,))],
        compiler_params=pltpu.CompilerParams(vmem_limit_bytes=VMEM_LIMIT),
    )(*arrs)


def swap_start(arrs, name):
    n = len(arrs)

    def body(*refs):
        src, land = refs[:n], refs[n:2 * n]
        ssem, rsem = refs[2 * n:2 * n + 2]
        token = refs[-1]
        x, y, c = _axes()
        for a in range(n):
            pltpu.make_async_remote_copy(src_ref=src[a], dst_ref=land[a], send_sem=ssem.at[a], recv_sem=rsem.at[a],
                                         device_id=(x, y, 1 - c), device_id_type=MESH).start()
        token[...] = jnp.zeros_like(token)

    lands = [lax.empty(a.shape, a.dtype) for a in arrs]
    out_shape = ([pltpu.SemaphoreType.DMA((n,)), pltpu.SemaphoreType.DMA((n,))]
                 + [pltpu.HBM(a.shape, a.dtype) for a in arrs] * 2 + [jax.ShapeDtypeStruct((8, 128), F32)])
    res = pl.pallas_call(
        body, name=name, out_shape=out_shape, in_specs=[HBM] * (2 * n),
        out_specs=[SEM, SEM] + [HBM] * (2 * n) + [pl.BlockSpec(memory_space=pltpu.VMEM)],
        input_output_aliases={a: 2 + a for a in range(2 * n)},
        compiler_params=pltpu.CompilerParams(has_side_effects=EFFECT),
    )(*[_in_hbm(a) for a in arrs], *[_in_hbm(l) for l in lands])
    return tuple(res[:2]), list(res[2:2 + n]), list(res[2 + n:2 + 2 * n]), res[-1]


def swap_wait(sems, srcs, lands, after, name):
    n = len(srcs)

    def body(*refs):
        src, land = refs[:n], refs[n:2 * n]
        ssem, rsem = refs[2 * n:2 * n + 2]
        x, y, c = _axes()
        for a in range(n):
            cp = pltpu.make_async_remote_copy(src_ref=src[a], dst_ref=land[a], send_sem=ssem.at[a],
                                              recv_sem=rsem.at[a], device_id=(x, y, 1 - c), device_id_type=MESH)
            cp.wait_send()
            cp.wait_recv()

    res = pl.pallas_call(
        body, name=name, out_shape=[pltpu.HBM(a.shape, a.dtype) for a in srcs] * 2,
        in_specs=[HBM] * (2 * n) + [SEM, SEM, ANY], out_specs=[HBM] * (2 * n),
        input_output_aliases={a: a for a in range(2 * n)},
        compiler_params=pltpu.CompilerParams(has_side_effects=EFFECT),
    )(*srcs, *lands, *sems, after)
    return list(res[:n]), list(res[n:])


def _all_peers(x, y, c):
    out = []
    for k in range(1, N_DEV):
        px, py, pc = _flip(x, (k >> 2) & 1), _flip(y, (k >> 1) & 1), _flip(c, k & 1)
        out.append(((px, py, pc), 4 * px + 2 * py + pc))
    return out


def exchange_start(items, after, name):
    n = len(items)

    def body(*refs):
        src, land = refs[:n], refs[n:2 * n]
        ssem, rsem, lsem = refs[2 * n + 1:2 * n + 4]
        token = refs[-1]
        x, y, c = _axes()
        me = 4 * x + 2 * y + c
        for a, (_, scatter) in enumerate(items):
            pltpu.make_async_copy(src[a].at[me] if scatter else src[a], land[a].at[me], lsem.at[a]).start()
            for k, (peer, p) in enumerate(_all_peers(x, y, c)):
                pltpu.make_async_remote_copy(src_ref=src[a].at[p] if scatter else src[a], dst_ref=land[a].at[me],
                                             send_sem=ssem.at[7 * a + k], recv_sem=rsem.at[7 * a + k],
                                             device_id=peer, device_id_type=MESH).start()
        token[...] = jnp.zeros_like(token)

    srcs = [s for s, _ in items]
    lands = [lax.empty(s.shape if sc else (N_DEV,) + s.shape, s.dtype) for s, sc in items]
    out_shape = ([pltpu.SemaphoreType.DMA((7 * n,)), pltpu.SemaphoreType.DMA((7 * n,)), pltpu.SemaphoreType.DMA((n,))]
                 + [pltpu.HBM(s.shape, s.dtype) for s in srcs] + [pltpu.HBM(l.shape, l.dtype) for l in lands]
                 + [jax.ShapeDtypeStruct((8, 128), F32)])
    res = pl.pallas_call(
        body, name=name, out_shape=out_shape, in_specs=[HBM] * (2 * n) + [ANY],
        out_specs=[SEM, SEM, SEM] + [HBM] * (2 * n) + [pl.BlockSpec(memory_space=pltpu.VMEM)],
        input_output_aliases={a: 3 + a for a in range(2 * n)},
        compiler_params=pltpu.CompilerParams(has_side_effects=EFFECT),
    )(*[_in_hbm(s) for s in srcs], *[_in_hbm(l) for l in lands], after)
    return tuple(res[:3]), list(res[3:3 + n]), list(res[3 + n:3 + 2 * n]), res[-1]


def exchange_wait(sems, srcs, lands, scatter, after, name):
    n = len(srcs)

    def body(*refs):
        src, land = refs[:n], refs[n:2 * n]
        ssem, rsem, lsem = refs[2 * n:2 * n + 3]
        x, y, c = _axes()
        me = 4 * x + 2 * y + c
        for a in range(n):
            for k, (peer, p) in enumerate(_all_peers(x, y, c)):
                cp = pltpu.make_async_remote_copy(src_ref=src[a].at[p] if scatter[a] else src[a],
                                                  dst_ref=land[a].at[p], send_sem=ssem.at[7 * a + k],
                                                  recv_sem=rsem.at[7 * a + k], device_id=peer, device_id_type=MESH)
                cp.wait_send()
                cp.wait_recv()
            pltpu.make_async_copy(src[a].at[me] if scatter[a] else src[a], land[a].at[me], lsem.at[a]).wait()

    res = pl.pallas_call(
        body, name=name,
        out_shape=[pltpu.HBM(s.shape, s.dtype) for s in srcs] + [pltpu.HBM(l.shape, l.dtype) for l in lands],
        in_specs=[HBM] * (2 * n) + [SEM, SEM, SEM, ANY], out_specs=[HBM] * (2 * n),
        input_output_aliases={a: a for a in range(2 * n)},
        compiler_params=pltpu.CompilerParams(has_side_effects=EFFECT),
    )(*srcs, *lands, *sems, after)
    return list(res[n:])


def sum8(parts, name):
    _, P, C = parts.shape

    def body(p_ref, o_ref):
        tot = p_ref[0]
        for d in range(1, N_DEV):
            tot = tot + p_ref[d]
        o_ref[...] = tot

    return pl.pallas_call(body, name=name, out_shape=jax.ShapeDtypeStruct((P, C), F32),
                          compiler_params=pltpu.CompilerParams(vmem_limit_bytes=VMEM_LIMIT))(parts)


def mm_nn(a, w, out_dtype, name, res=None, gate=None):
    M, K = a.shape
    S, _, Ns = w.shape
    TM = _tile(M, (1024, 512, 256) if K <= 1024 else (512, 256))
    TN = _tile(Ns, (1408, 1024, 768, 512, 256, 128))
    nj = Ns // TN
    fused = res is not None

    def body(*refs):
        if fused:
            a_ref, w_ref, r_ref, g_ref, f_ref, o_ref = refs
        else:
            a_ref, w_ref, f_ref = refs
        f = jnp.dot(a_ref[...], w_ref[...], preferred_element_type=F32)
        f_ref[...] = f.astype(f_ref.dtype)
        if fused:
            o_ref[...] = r_ref[...] + g_ref[...] * f

    col = lambda s, j, i: (i, s * nj + j)
    in_specs = [pl.BlockSpec((TM, K), lambda s, j, i: (i, 0)), pl.BlockSpec((None, K, TN), lambda s, j, i: (s, 0, j))]
    out_specs = [pl.BlockSpec((TM, TN), col)]
    out_shape = [jax.ShapeDtypeStruct((M, S * Ns), out_dtype)]
    args = [a, w]
    if fused:
        in_specs += [pl.BlockSpec((TM, TN), col), pl.BlockSpec((1, TN), lambda s, j, i: (0, s * nj + j))]
        out_specs.append(pl.BlockSpec((TM, TN), col))
        out_shape.append(jax.ShapeDtypeStruct((M, S * Ns), F32))
        args += [res, gate]
    out = pl.pallas_call(body, name=name, grid=(S, nj, M // TM), in_specs=in_specs, out_specs=out_specs,
                         out_shape=out_shape, compiler_params=_cp("parallel", "parallel", "parallel"))(*args)
    return tuple(out) if fused else out[0]


def mm_nt(g, w, out_dtype, name):
    g3 = g if g.ndim == 3 else g[None]
    Q, M, F = g3.shape
    S, K, Ns = w.shape
    TM = _tile(M, (1024, 512, 256) if K <= 1024 else (512, 256))
    TN = _tile(Ns, (1408, 1024, 768, 512, 256, 128))
    nj = Ns // TN
    nred = S * nj
    per_part = F // TN

    def body(g_ref, w_ref, o_ref, acc):
        n = pl.program_id(1)

        @pl.when(n == 0)
        def _():
            acc[...] = jnp.zeros_like(acc)

        acc[...] += lax.dot_general(g_ref[...], w_ref[...], (((1,), (1,)), ((), ())), preferred_element_type=F32)

        @pl.when(n == nred - 1)
        def _():
            o_ref[...] = acc[...].astype(o_ref.dtype)

    return pl.pallas_call(
        body, name=name, grid=(M // TM, nred),
        in_specs=[pl.BlockSpec((None, TM, TN), lambda i, n: (n // per_part, i, n % per_part)),
                  pl.BlockSpec((None, K, TN), lambda i, n: (n // nj, 0, n % nj))],
        out_specs=pl.BlockSpec((TM, K), lambda i, n: (i, 0)),
        out_shape=jax.ShapeDtypeStruct((M, K), out_dtype),
        scratch_shapes=[pltpu.VMEM((TM, K), F32)],
        compiler_params=_cp("parallel", "arbitrary"))(g3, w)


def mm_tn(a, g, S, name):
    M, K = a.shape
    g3 = g if g.ndim == 3 else g[None]
    Q, _, F = g3.shape
    Ns = Q * F // S
    TK = _tile(K, (256, 128))
    TN = _tile(Ns, (1408, 1024, 768, 512, 256, 128))
    nj = Ns // TN
    per_part = F // TN

    def body(a_ref, g_ref, o_ref):
        o_ref[...] = lax.dot_general(a_ref[...], g_ref[...], (((0,), (0,)), ((), ())),
                                     preferred_element_type=F32).astype(o_ref.dtype)

    return pl.pallas_call(
        body, name=name, grid=(S * nj, K // TK),
        in_specs=[pl.BlockSpec((M, TK), lambda n, k: (0, k)),
                  pl.BlockSpec((None, M, TN), lambda n, k: (n // per_part, 0, n % per_part))],
        out_specs=pl.BlockSpec((None, TK, TN), lambda n, k: (n // nj, k, n % nj)),
        out_shape=jax.ShapeDtypeStruct((S, K, Ns), BF),
        compiler_params=_cp("parallel", "parallel"))(a, g3)


ROW_TILE = (512, 256)


def _rows(TL, D):
    return pl.BlockSpec((TL, D), lambda i: (i, 0))


def _fixed(R, D):
    return pl.BlockSpec((R, D), lambda i: (0, 0))


def _rowsum8(v):
    T, D = v.shape
    return jnp.sum(v.reshape(T // 8, 8, D), axis=0)


def _norm_parts(xv):
    r = lax.rsqrt(jnp.mean(xv * xv, axis=-1, keepdims=True) + RMS_EPS)
    return xv * r, r


def norm_mod(x, gamma, mods, k_shift, out_dtype, name):
    L, D = x.shape
    TL = _tile(L, ROW_TILE)

    def body(x_ref, g_ref, m_ref, o_ref):
        xn, _ = _norm_parts(x_ref[...])
        sh, sc = m_ref[k_shift:k_shift + 1, :], m_ref[k_shift + 1:k_shift + 2, :]
        o_ref[...] = ((xn * g_ref[...]) * (1.0 + sc) + sh).astype(o_ref.dtype)

    return pl.pallas_call(body, name=name, grid=(L // TL,),
                          in_specs=[_rows(TL, D), _fixed(1, D), _fixed(6, D)], out_specs=_rows(TL, D),
                          out_shape=jax.ShapeDtypeStruct((L, D), out_dtype), compiler_params=_cp("parallel"))(x, gamma, mods)


def norm_bwd(dh, x, dres, gamma, mods, k_shift, name, branch=None):
    L, D = x.shape
    TL = _tile(L, ROW_TILE)
    nacc = 4 if branch else 3

    def body(*refs):
        if branch:
            dh_ref, x_ref, dr_ref, g_ref, m_ref, f_ref, fm_ref, dx_ref, s_ref, df_ref, acc = refs
        else:
            dh_ref, x_ref, dr_ref, g_ref, m_ref, dx_ref, s_ref, acc = refs
        i = pl.program_id(0)

        @pl.when(i == 0)
        def _():
            acc[...] = jnp.zeros_like(acc)

        xn, r = _norm_parts(x_ref[...])
        dh_v = dh_ref[...].astype(F32)
        gam = g_ref[...]
        sc = m_ref[k_shift + 1:k_shift + 2, :]
        dn = dh_v * (1.0 + sc)
        dxn = dn * gam
        dx = dr_ref[...] + r * (dxn - xn * jnp.mean(dxn * xn, axis=-1, keepdims=True))
        dx_ref[...] = dx
        acc[0] += _rowsum8(dh_v)
        acc[1] += _rowsum8(dh_v * (xn * gam))
        acc[2] += _rowsum8(dn * xn)
        if branch:
            df_ref[...] = (dx * fm_ref[branch[2]:branch[2] + 1, :]).astype(df_ref.dtype)
            acc[3] += _rowsum8(dx * f_ref[...].astype(F32))

        @pl.when(i == pl.num_programs(0) - 1)
        def _():
            s_ref[...] = jnp.zeros_like(s_ref)
            for q in range(nacc):
                s_ref[q:q + 1, :] = jnp.sum(acc[q], axis=0, keepdims=True)

    in_specs = [_rows(TL, D), _rows(TL, D), _rows(TL, D), _fixed(1, D), _fixed(6, D)]
    out_specs = [_rows(TL, D), _fixed(8, D)]
    out_shape = [jax.ShapeDtypeStruct((L, D), F32), jax.ShapeDtypeStruct((8, D), F32)]
    args = [dh, x, dres, gamma, mods]
    if branch:
        in_specs += [_rows(TL, D), _fixed(6, D)]
        out_specs.append(_rows(TL, D))
        out_shape.append(jax.ShapeDtypeStruct((L, D), BF))
        args += [branch[0], branch[1]]
    return pl.pallas_call(
        body, name=name, grid=(L // TL,), in_specs=in_specs, out_specs=out_specs, out_shape=out_shape,
        scratch_shapes=[pltpu.VMEM((nacc, 8, D), F32)], compiler_params=_cp("arbitrary"))(*args)


def ffn_in_act(a, w, name):
    M, K = a.shape
    S, _, Ns = w.shape
    half = S // 2
    TM = _tile(M, (512, 256))
    TN = _tile(Ns, (1408, 1024, 768, 512, 256, 128))
    nj = Ns // TN

    def body(a_ref, wg_ref, wu_ref, gu_ref, act_ref):
        av = a_ref[...]
        g = jnp.dot(av, wg_ref[...], preferred_element_type=F32)
        u = jnp.dot(av, wu_ref[...], preferred_element_type=F32)
        gu_ref[0] = g.astype(gu_ref.dtype)
        gu_ref[1] = u.astype(gu_ref.dtype)
        act_ref[...] = (g * jax.nn.sigmoid(g) * u).astype(act_ref.dtype)

    return pl.pallas_call(
        body, name=name, grid=(half, nj, M // TM),
        in_specs=[pl.BlockSpec((TM, K), lambda s, j, i: (i, 0)),
                  pl.BlockSpec((None, K, TN), lambda s, j, i: (s, 0, j)),
                  pl.BlockSpec((None, K, TN), lambda s, j, i: (s + half, 0, j))],
        out_specs=[pl.BlockSpec((2, TM, TN), lambda s, j, i: (0, i, s * nj + j)),
                   pl.BlockSpec((TM, TN), lambda s, j, i: (i, s * nj + j))],
        out_shape=[jax.ShapeDtypeStruct((2, M, half * Ns), BF), jax.ShapeDtypeStruct((M, half * Ns), BF)],
        compiler_params=_cp("parallel", "parallel", "parallel"))(a, w, w)


def ffn_out_bwd(dff, w2, gu, name):
    M, D = dff.shape
    F = w2.shape[0]
    TM = _tile(M, (512, 256))
    CW = _tile(F, (256, 128))

    def body(d_ref, w_ref, gu_ref, o_ref):
        dv = d_ref[...]
        for c in range(0, F, CW):
            da = lax.dot_general(dv, w_ref[c:c + CW, :], (((1,), (1,)), ((), ())), preferred_element_type=F32)
            g = gu_ref[0, :, c:c + CW].astype(F32)
            u = gu_ref[1, :, c:c + CW].astype(F32)
            s = jax.nn.sigmoid(g)
            o_ref[0, :, c:c + CW] = (da * u * (s + g * s * (1.0 - s))).astype(o_ref.dtype)
            o_ref[1, :, c:c + CW] = (da * g * s).astype(o_ref.dtype)

    part = pl.BlockSpec((2, TM, F), lambda i: (0, i, 0))
    return pl.pallas_call(
        body, name=name, grid=(M // TM,),
        in_specs=[pl.BlockSpec((TM, D), lambda i: (i, 0)), pl.BlockSpec((F, D), lambda i: (0, 0)), part],
        out_specs=part, out_shape=jax.ShapeDtypeStruct((2, M, F), BF),
        compiler_params=_cp("parallel"))(dff, w2, gu)


def ssm_out_glu(z, w, x, mods, k_gate, name):
    M, K = z.shape
    S, _, Ns = w.shape
    half = S // 2
    TM = _tile(M, (1024, 512, 256))
    TN = _tile(Ns, (512, 256, 128))
    nj = Ns // TN

    def body(z_ref, wv_ref, wg_ref, x_ref, m_ref, o_ref, mix_ref, y_ref):
        zv = z_ref[...]
        val = jnp.dot(zv, wv_ref[...], preferred_element_type=F32)
        gate = jnp.dot(zv, wg_ref[...], preferred_element_type=F32)
        o_ref[0] = val.astype(o_ref.dtype)
        o_ref[1] = gate.astype(o_ref.dtype)
        mix = val * jax.nn.sigmoid(gate)
        mix_ref[...] = mix.astype(mix_ref.dtype)
        y_ref[...] = x_ref[...] + m_ref[k_gate:k_gate + 1, :] * mix

    col = lambda s, j, i: (i, s * nj + j)
    return pl.pallas_call(
        body, name=name, grid=(half, nj, M // TM),
        in_specs=[pl.BlockSpec((TM, K), lambda s, j, i: (i, 0)),
                  pl.BlockSpec((None, K, TN), lambda s, j, i: (s, 0, j)),
                  pl.BlockSpec((None, K, TN), lambda s, j, i: (s + half, 0, j)),
                  pl.BlockSpec((TM, TN), col), pl.BlockSpec((6, TN), lambda s, j, i: (0, s * nj + j))],
        out_specs=[pl.BlockSpec((2, TM, TN), lambda s, j, i: (0, i, s * nj + j)), pl.BlockSpec((TM, TN), col),
                   pl.BlockSpec((TM, TN), col)],
        out_shape=[jax.ShapeDtypeStruct((2, M, half * Ns), BF), jax.ShapeDtypeStruct((M, half * Ns), BF),
                   jax.ShapeDtypeStruct((M, half * Ns), F32)],
        compiler_params=_cp("parallel", "parallel", "parallel"))(z, w, w, x, mods)


def glu_bwd(dmix, o, name):
    _, L, D = o.shape
    TL = _tile(L, ROW_TILE)

    def body(d_ref, o_ref, do_ref):
        d = d_ref[...].astype(F32)
        val = o_ref[0].astype(F32)
        s = jax.nn.sigmoid(o_ref[1].astype(F32))
        do_ref[0] = (d * s).astype(do_ref.dtype)
        do_ref[1] = (d * val * s * (1.0 - s)).astype(do_ref.dtype)

    part = pl.BlockSpec((2, TL, D), lambda i: (0, i, 0))
    return pl.pallas_call(body, name=name, grid=(L // TL,), in_specs=[_rows(TL, D), part],
                          out_specs=part, out_shape=jax.ShapeDtypeStruct((2, L, D), BF),
                          compiler_params=_cp("parallel"))(dmix, o)


def final_loss(x, target, gamma, f, fmods, k_gate, name):
    L, D = x.shape
    TL = _tile(L, ROW_TILE)

    def body(x_ref, t_ref, g_ref, f_ref, fm_ref, l_ref, dx_ref, s_ref, df_ref, acc, lacc):
        i = pl.program_id(0)

        @pl.when(i == 0)
        def _():
            acc[...] = jnp.zeros_like(acc)
            lacc[...] = jnp.zeros_like(lacc)

        xn, r = _norm_parts(x_ref[...])
        gam = g_ref[...]
        e = xn * gam - t_ref[...]
        lacc[...] += jnp.sum(0.5 * jnp.mean(e * e, axis=-1, keepdims=True), axis=0, keepdims=True)
        dy = e * (1.0 / D)
        dxn = dy * gam
        dx = r * (dxn - xn * jnp.mean(dxn * xn, axis=-1, keepdims=True))
        dx_ref[...] = dx
        df_ref[...] = (dx * fm_ref[k_gate:k_gate + 1, :]).astype(df_ref.dtype)
        acc[0] += _rowsum8(dy * xn)
        acc[1] += _rowsum8(dx * f_ref[...].astype(F32))

        @pl.when(i == pl.num_programs(0) - 1)
        def _():
            s_ref[...] = jnp.zeros_like(s_ref)
            for q in range(2):
                s_ref[q:q + 1, :] = jnp.sum(acc[q], axis=0, keepdims=True)
            l_ref[...] = jnp.broadcast_to(lacc[...], l_ref.shape)

    return pl.pallas_call(
        body, name=name, grid=(L // TL,),
        in_specs=[_rows(TL, D), _rows(TL, D), _fixed(1, D), _rows(TL, D), _fixed(6, D)],
        out_specs=[_fixed(8, 128), _rows(TL, D), _fixed(8, D), _rows(TL, D)],
        out_shape=[jax.ShapeDtypeStruct((8, 128), F32), jax.ShapeDtypeStruct((L, D), F32),
                   jax.ShapeDtypeStruct((8, D), F32), jax.ShapeDtypeStruct((L, D), BF)],
        scratch_shapes=[pltpu.VMEM((2, 8, D), F32), pltpu.VMEM((1, 1), F32)],
        compiler_params=_cp("arbitrary"))(x, target, gamma, f, fmods)


def _col(L, TC, off):
    return pl.BlockSpec((L, TC), lambda j: (0, off + j))


def _shift_down(v, k, row):
    return jnp.where(row >= k, pltpu.roll(v, k, 0), 0.0)


def _shift_up(v, k, row, L):
    return jnp.where(row < L - k, pltpu.roll(v, L - k, 0), 0.0)


def conv_fwd(p, w, name):
    L, D3 = p.shape
    D = D3 // 3
    TC = _tile(D, (128,))
    nc = D // TC

    def body(b_ref, c_ref, v_ref, w_ref, o_ref):
        row = lax.broadcasted_iota(jnp.int32, (L, TC), 0)
        cv = c_ref[...].astype(F32) * v_ref[...].astype(F32)
        conv = w_ref[2:3, :] * cv + w_ref[1:2, :] * _shift_down(cv, 1, row) + w_ref[0:1, :] * _shift_down(cv, 2, row)
        o_ref[...] = (b_ref[...].astype(F32) * conv).astype(o_ref.dtype)

    return pl.pallas_call(
        body, name=name, grid=(nc,),
        in_specs=[_col(L, TC, 0), _col(L, TC, nc), _col(L, TC, 2 * nc), pl.BlockSpec((3, TC), lambda j: (0, j))],
        out_specs=_col(L, TC, 0), out_shape=jax.ShapeDtypeStruct((L, D), BF), compiler_params=_cp("parallel"))(p, p, p, w)


def conv_bwd(dm, p, w, name):
    L, D3 = p.shape
    D = D3 // 3
    TC = _tile(D, (128,))
    nc = D // TC

    def body(dm_ref, b_ref, c_ref, v_ref, w_ref, db_ref, dc_ref, dv_ref, dw_ref):
        row = lax.broadcasted_iota(jnp.int32, (L, TC), 0)
        cg, vv = c_ref[...].astype(F32), v_ref[...].astype(F32)
        cv = cg * vv
        cv1, cv2 = _shift_down(cv, 1, row), _shift_down(cv, 2, row)
        conv = w_ref[2:3, :] * cv + w_ref[1:2, :] * cv1 + w_ref[0:1, :] * cv2
        dmv = dm_ref[...].astype(F32)
        db_ref[...] = (dmv * conv).astype(db_ref.dtype)
        dconv = dmv * b_ref[...].astype(F32)
        dcv = (w_ref[2:3, :] * dconv + w_ref[1:2, :] * _shift_up(dconv, 1, row, L)
               + w_ref[0:1, :] * _shift_up(dconv, 2, row, L))
        dc_ref[...] = (dcv * vv).astype(dc_ref.dtype)
        dv_ref[...] = (dcv * cg).astype(dv_ref.dtype)
        dw_ref[...] = jnp.zeros_like(dw_ref)
        dw_ref[0:1, :] = jnp.sum(dconv * cv2, axis=0, keepdims=True)
        dw_ref[1:2, :] = jnp.sum(dconv * cv1, axis=0, keepdims=True)
        dw_ref[2:3, :] = jnp.sum(dconv * cv, axis=0, keepdims=True)

    one = jax.ShapeDtypeStruct((L, D), BF)
    return pl.pallas_call(
        body, name=name, grid=(nc,),
        in_specs=[_col(L, TC, 0), _col(L, TC, 0), _col(L, TC, nc), _col(L, TC, 2 * nc),
                  pl.BlockSpec((3, TC), lambda j: (0, j))],
        out_specs=[_col(L, TC, 0), _col(L, TC, 0), _col(L, TC, 0), pl.BlockSpec((8, TC), lambda j: (0, j))],
        out_shape=[one, one, one, jax.ShapeDtypeStruct((8, D), F32)],
        compiler_params=_cp("parallel"))(dm, p, p, p, w)


def _gelu(y):
    return 0.5 * y * (1.0 + jnp.tanh(GELU_C * (y + GELU_A * y * y * y)))


def _gelu_grad(y):
    th = jnp.tanh(GELU_C * (y + GELU_A * y * y * y))
    return 0.5 * (1.0 + th) + 0.5 * y * (1.0 - th * th) * GELU_C * (1.0 + 3.0 * GELU_A * y * y)


def _cmul_add(br, bi, ar, ai, sr, si):
    return br + ar * sr - ai * si, bi + ar * si + ai * sr


def _log2(n):
    k = n.bit_length() - 1
    assert 1 << k == n
    return k


def _replicate(P2, W2, P, GLP, transposed):
    shape = (W2, P2) if transposed else (P2, W2)
    k = lax.broadcasted_iota(jnp.int32, shape, 1 if transposed else 0)
    c = lax.broadcasted_iota(jnp.int32, shape, 0 if transposed else 1)
    return ((k >> _log2(P)) == (c >> _log2(GLP))) & ((k & (P - 1)) == (c & (P - 1)))


def _on_diagonal(KB, W2, H, P, GLP, transposed):
    shape = (W2, KB) if transposed else (KB, W2)
    r = lax.broadcasted_iota(jnp.int32, shape, 1 if transposed else 0)
    c = lax.broadcasted_iota(jnp.int32, shape, 0 if transposed else 1)
    return (r >> _log2(H)) == ((c & (GLP - 1)) >> _log2(P))


def _expand(t, dims, transposed):
    KB, W2, H, P, GLP = dims
    rep = _replicate(2 * P, W2, P, GLP, transposed).astype(t.dtype)
    wide = jnp.dot(rep, t, preferred_element_type=F32) if transposed else jnp.dot(t, rep, preferred_element_type=F32)
    return jnp.where(_on_diagonal(KB, W2, H, P, GLP, transposed), wide, 0.0).astype(t.dtype)


def _extract(acc, dims):
    KB, W2, H, P, GLP = dims
    rep = _replicate(2 * P, W2, P, GLP, True).astype(F32)
    kept = jnp.where(_on_diagonal(KB, W2, H, P, GLP, False), acc, 0.0)
    return jnp.dot(kept, rep, preferred_element_type=F32, precision=lax.Precision.HIGHEST)


def _cmul(ar, ai, sr, si):
    return ar * sr - ai * si, ar * si + ai * sr


def _chunk_order(TL, CH, transposed):
    out_row = lax.broadcasted_iota(jnp.int32, (TL, TL), 1 if transposed else 0)
    in_row = lax.broadcasted_iota(jnp.int32, (TL, TL), 0 if transposed else 1)
    return in_row == ((out_row & 7) << _log2(CH)) + (out_row >> 3)


def _reorder(perm, v):
    hi = v.astype(perm.dtype)
    lo = (v - hi.astype(F32)).astype(perm.dtype)
    return jnp.dot(perm, hi, preferred_element_type=F32) + jnp.dot(perm, lo, preferred_element_type=F32)


def _interleave(main, side):
    n, m, k = len(main), len(side), 0
    for i, step in enumerate(main):
        step()
        while k < m and (k + 1) * n <= (i + 1) * m:
            side[k]()
            k += 1
    for step in side[k:]:
        step()


S5_CHUNK = 512


def s5_fwd(h, tb, tct, pw, dvec, name):
    L, D = h.shape
    nkb, KB, P2 = tb.shape
    P = P2 // 2
    W = (KB // SSM_GROUP) * P
    W2 = 2 * W
    dims = (KB, W2, SSM_GROUP, P, W)
    TL = _tile(L, (512, 256))
    CH = TL // 8
    NB = 2 if nkb % 2 == 0 else 1
    CK = min(S5_CHUNK, W2)

    def body(h_ref, tb_ref, tct_ref, pw_ref, d_ref, s_ref, y_ref, z_ref, bw, cw, perm, unperm, carry):
        t = pl.program_id(1)

        @pl.when(t == 0)
        def _():
            carry[...] = jnp.zeros_like(carry)
            for b in range(NB):
                bw[b] = _expand(tb_ref[b], dims, False)
                cw[b] = _expand(tct_ref[b], dims, True)
            perm[...] = _chunk_order(TL, CH, False).astype(perm.dtype)
            unperm[...] = _chunk_order(TL, CH, True).astype(perm.dtype)

        hp = _reorder(perm[...], h_ref[...])
        hpb = hp.astype(BF)
        first = lax.broadcasted_iota(jnp.int32, (8, W), 0) == 0

        def project(b):
            def chunk(c):
                def emit():
                    s_ref[:, b * W2 + c:b * W2 + c + CK] = jnp.dot(hpb[:, b * KB:(b + 1) * KB], bw[b, :, c:c + CK],
                                                                   preferred_element_type=F32)
                return emit
            return [chunk(c) for c in range(0, W2, CK)]

        def scan(b):
            re, im = slice(b * W2, b * W2 + W), slice(b * W2 + W, (b + 1) * W2)
            ar, ai = pw_ref[b, 0:8, :W], pw_ref[b, 0:8, W:]
            st = {"x": (jnp.zeros((8, W), F32), jnp.zeros((8, W), F32))}

            def own(j):
                def emit():
                    rows = slice(j * 8, j * 8 + 8)
                    xr, xi = _cmul_add(s_ref[rows, re], s_ref[rows, im], ar, ai, *st["x"])
                    s_ref[rows, re] = xr
                    s_ref[rows, im] = xi
                    st["x"] = (xr, xi)
                return emit

            def ends():
                xr, xi = st["x"]
                for k, off in ((1, 8), (2, 16), (4, 24)):
                    xr, xi = _cmul_add(xr, xi, pw_ref[b, off:off + 8, :W], pw_ref[b, off:off + 8, W:],
                                       pltpu.roll(xr, k, 0), pltpu.roll(xi, k, 0))
                xr, xi = _cmul_add(xr, xi, pw_ref[b, 32:40, :W], pw_ref[b, 32:40, W:], carry[b, 0], carry[b, 1])
                st["c"] = (jnp.where(first, carry[b, 0], pltpu.roll(xr, 1, 0)),
                           jnp.where(first, carry[b, 1], pltpu.roll(xi, 1, 0)))
                carry[b, 0] = jnp.broadcast_to(xr[7:8], (8, W))
                carry[b, 1] = jnp.broadcast_to(xi[7:8], (8, W))

            def carried(j):
                def emit():
                    rows = slice(j * 8, j * 8 + 8)
                    cr, ci = _cmul(ar, ai, *st["c"])
                    s_ref[rows, re] = s_ref[rows, re] + cr
                    s_ref[rows, im] = s_ref[rows, im] + ci
                    st["c"] = (cr, ci)
                return emit

            return [own(j) for j in range(CH)] + [ends] + [carried(j) for j in range(CH)]

        def readout(b):
            cols = slice(b * KB, (b + 1) * KB)
            acc = {}

            def chunk(c):
                def emit():
                    part = jnp.dot(s_ref[:, b * W2 + c:b * W2 + c + CK].astype(BF), cw[b, c:c + CK, :],
                                   preferred_element_type=F32)
                    acc["y"] = part if c == 0 else acc["y"] + part
                return emit

            def finish():
                y = acc["y"] + d_ref[:, cols] * hp[:, cols]
                y_ref[:, cols] = y
                z_ref[:, cols] = jnp.dot(unperm[...], _gelu(y).astype(BF),
                                         preferred_element_type=F32).astype(z_ref.dtype)

            return [chunk(c) for c in range(0, W2, CK)] + [finish]

        for emit in project(0):
            emit()
        for b in range(NB):
            side = (project(b + 1) if b + 1 < NB else []) + (readout(b - 1) if b > 0 else [])
            _interleave(scan(b), side)
        for emit in readout(NB - 1):
            emit()

    blk = lambda kb, t: (t, kb)
    per_kb = lambda kb, t: (kb, 0, 0)
    return pl.pallas_call(
        body, name=name, grid=(nkb // NB, L // TL),
        in_specs=[pl.BlockSpec((TL, NB * KB), blk), pl.BlockSpec((NB, KB, P2), per_kb),
                  pl.BlockSpec((NB, P2, KB), per_kb), pl.BlockSpec((NB, 40, W2), per_kb),
                  pl.BlockSpec((1, NB * KB), lambda kb, t: (0, kb))],
        out_specs=[pl.BlockSpec((TL, NB * W2), blk), pl.BlockSpec((TL, NB * KB), blk),
                   pl.BlockSpec((TL, NB * KB), blk)],
        out_shape=[jax.ShapeDtypeStruct((L, nkb * W2), F32), jax.ShapeDtypeStruct((L, D), F32),
                   jax.ShapeDtypeStruct((L, D), BF)],
        scratch_shapes=[pltpu.VMEM((NB, KB, W2), BF), pltpu.VMEM((NB, W2, KB), BF), pltpu.VMEM((TL, TL), BF),
                        pltpu.VMEM((TL, TL), BF), pltpu.VMEM((NB, 2, 8, W), F32)],
        compiler_params=_cp("parallel", "arbitrary"))(h, tb, tct, pw, dvec)


def s5_bwd(dz, y, h, s, tc, tbt, pwr, dvec, name):
    L, D = h.shape
    nkb, KB, P2 = tc.shape
    P = P2 // 2
    W = (KB // SSM_GROUP) * P
    W2 = 2 * W
    dims = (KB, W2, SSM_GROUP, P, W)
    TL = _tile(L, (512, 256))
    CH = TL // 8
    nt = L // TL
    NB = 2 if nkb % 2 == 0 else 1
    CK = min(S5_CHUNK, W2)
    tn = (((0,), (0,)), ((), ()))

    def body(dz_ref, y_ref, h_ref, s_ref, sp_ref, tc_ref, tbt_ref, pw_ref, d_ref,
             dh_ref, dd_ref, da_ref, db_ref, dc_ref, g, ctw, btw, dbacc, dcacc, dys, perm, unperm, carry):
        t = pl.program_id(1)

        @pl.when(t == 0)
        def _():
            carry[...] = jnp.zeros_like(carry)
            dd_ref[...] = jnp.zeros_like(dd_ref)
            da_ref[...] = jnp.zeros_like(da_ref)
            dbacc[...] = jnp.zeros_like(dbacc)
            dcacc[...] = jnp.zeros_like(dcacc)
            for b in range(NB):
                ctw[b] = _expand(tc_ref[b], dims, False)
                btw[b] = _expand(tbt_ref[b], dims, True)
            perm[...] = _chunk_order(TL, CH, False).astype(perm.dtype)
            unperm[...] = _chunk_order(TL, CH, True).astype(perm.dtype)

        hp = jnp.dot(perm[...], h_ref[...].astype(BF), preferred_element_type=F32)
        dy = jnp.dot(perm[...], dz_ref[...].astype(BF), preferred_element_type=F32) * _gelu_grad(y_ref[...])
        dd_ref[...] += _rowsum8(dy * hp)
        dys[...] = dy
        dyb = dy.astype(BF)
        hpb = hp.astype(BF)
        sub = lax.broadcasted_iota(jnp.int32, (8, W), 0)
        live = jnp.where(t == nt - 1, 0.0, 1.0)

        def lead(b):
            cols = slice(b * KB, (b + 1) * KB)

            def to_states(c):
                def emit():
                    g[b, :, c:c + CK] = jnp.dot(dyb[:, cols], ctw[b, :, c:c + CK], preferred_element_type=F32)
                return emit

            def d_c(c):
                def emit():
                    dcacc[b, :, c:c + CK] += lax.dot_general(dyb[:, cols],
                                                             s_ref[:, b * W2 + c:b * W2 + c + CK].astype(BF), tn,
                                                             preferred_element_type=F32)
                return emit

            return [f(c) for c in range(0, W2, CK) for f in (to_states, d_c)]

        def scan(b):
            re, im = slice(b * W2, b * W2 + W), slice(b * W2 + W, (b + 1) * W2)
            ar, ai = pw_ref[b, 0:8, :W], pw_ref[b, 0:8, W:]
            zero = jnp.zeros((8, W), F32)
            st = {"g": (zero, zero), "acc": (zero, zero)}

            def own(j):
                def emit():
                    rows = slice(j * 8, j * 8 + 8)
                    gr, gi = _cmul_add(g[b, rows, :W], g[b, rows, W:], ar, ai, *st["g"])
                    g[b, rows, :W] = gr
                    g[b, rows, W:] = gi
                    st["g"] = (gr, gi)
                return emit

            def ends():
                gr, gi = st["g"]
                for k, off in ((1, 8), (2, 16), (4, 24)):
                    gr, gi = _cmul_add(gr, gi, pw_ref[b, off:off + 8, :W], pw_ref[b, off:off + 8, W:],
                                       pltpu.roll(gr, 8 - k, 0), pltpu.roll(gi, 8 - k, 0))
                gr, gi = _cmul_add(gr, gi, pw_ref[b, 32:40, :W], pw_ref[b, 32:40, W:], carry[b, 0], carry[b, 1])
                st["c"] = (jnp.where(sub == 7, carry[b, 0], pltpu.roll(gr, 7, 0)),
                           jnp.where(sub == 7, carry[b, 1], pltpu.roll(gi, 7, 0)))
                carry[b, 0] = jnp.broadcast_to(gr[0:1], (8, W))
                carry[b, 1] = jnp.broadcast_to(gi[0:1], (8, W))

            def carried(j):
                def emit():
                    rows = slice(j * 8, j * 8 + 8)
                    cr, ci = _cmul(ar, ai, *st["c"])
                    gr, gi = g[b, rows, :W] + cr, g[b, rows, W:] + ci
                    g[b, rows, :W] = gr
                    g[b, rows, W:] = gi
                    if j > 0:
                        before = slice(j * 8 - 8, j * 8)
                        pr, pi = s_ref[before, re], s_ref[before, im]
                    else:
                        last = slice(TL - 8, TL)
                        pr = jnp.where(sub == 0, sp_ref[7:8, re] * live, pltpu.roll(s_ref[last, re], 1, 0))
                        pi = jnp.where(sub == 0, sp_ref[7:8, im] * live, pltpu.roll(s_ref[last, im], 1, 0))
                    accr, acci = st["acc"]
                    st["c"] = (cr, ci)
                    st["acc"] = (accr + pr * gr + pi * gi, acci + pr * gi - pi * gr)
                return emit

            def done():
                da_ref[b, :, :W] += st["acc"][0]
                da_ref[b, :, W:] += st["acc"][1]

            return ([own(j) for j in reversed(range(CH))] + [ends] + [carried(j) for j in reversed(range(CH))]
                    + [done])

        def tail(b):
            cols = slice(b * KB, (b + 1) * KB)
            acc = {}

            def d_u(c):
                def emit():
                    part = jnp.dot(g[b, :, c:c + CK].astype(BF), btw[b, c:c + CK, :], preferred_element_type=F32)
                    acc["u"] = part if c == 0 else acc["u"] + part
                return emit

            def d_b(c):
                def emit():
                    dbacc[b, :, c:c + CK] += lax.dot_general(hpb[:, cols], g[b, :, c:c + CK].astype(BF), tn,
                                                             preferred_element_type=F32)
                return emit

            def finish():
                dh = (dys[:, cols] * d_ref[:, cols] + acc["u"]).astype(BF)
                dh_ref[:, cols] = jnp.dot(unperm[...], dh, preferred_element_type=F32).astype(dh_ref.dtype)

            return [f(c) for c in range(0, W2, CK) for f in (d_u, d_b)] + [finish]

        for emit in lead(0):
            emit()
        for b in range(NB):
            side = (lead(b + 1) if b + 1 < NB else []) + (tail(b - 1) if b > 0 else [])
            _interleave(scan(b), side)
        for emit in tail(NB - 1):
            emit()

        @pl.when(t == nt - 1)
        def _():
            for b in range(NB):
                db_ref[b] = _extract(dbacc[b], dims)
                dc_ref[b] = _extract(dcacc[b], dims)

    rev = lambda kb, t: (nt - 1 - t, kb)
    prev = lambda kb, t: (jnp.maximum((nt - 1 - t) * CH - 1, 0), kb)
    per_kb = lambda kb, t: (kb, 0, 0)
    return pl.pallas_call(
        body, name=name, grid=(nkb // NB, nt),
        in_specs=[pl.BlockSpec((TL, NB * KB), rev), pl.BlockSpec((TL, NB * KB), rev),
                  pl.BlockSpec((TL, NB * KB), rev), pl.BlockSpec((TL, NB * W2), rev),
                  pl.BlockSpec((8, NB * W2), prev), pl.BlockSpec((NB, KB, P2), per_kb),
                  pl.BlockSpec((NB, P2, KB), per_kb), pl.BlockSpec((NB, 40, W2), per_kb),
                  pl.BlockSpec((1, NB * KB), lambda kb, t: (0, kb))],
        out_specs=[pl.BlockSpec((TL, NB * KB), rev), pl.BlockSpec((8, NB * KB), lambda kb, t: (0, kb)),
                   pl.BlockSpec((NB, 8, W2), per_kb), pl.BlockSpec((NB, KB, P2), per_kb),
                   pl.BlockSpec((NB, KB, P2), per_kb)],
        out_shape=[jax.ShapeDtypeStruct((L, D), BF), jax.ShapeDtypeStruct((8, D), F32),
                   jax.ShapeDtypeStruct((nkb, 8, W2), F32), jax.ShapeDtypeStruct((nkb, KB, P2), F32),
                   jax.ShapeDtypeStruct((nkb, KB, P2), F32)],
        scratch_shapes=[pltpu.VMEM((NB, TL, W2), F32), pltpu.VMEM((NB, KB, W2), BF), pltpu.VMEM((NB, W2, KB), BF),
                        pltpu.VMEM((NB, KB, W2), F32), pltpu.VMEM((NB, KB, W2), F32), pltpu.VMEM((TL, NB * KB), F32),
                        pltpu.VMEM((TL, TL), BF), pltpu.VMEM((TL, TL), BF), pltpu.VMEM((NB, 2, 8, W), F32)],
        compiler_params=pltpu.CompilerParams(dimension_semantics=("parallel", "arbitrary"),
                                             vmem_limit_bytes=V7X_VMEM_BYTES - 4 * 1024 * 1024),
    )(dz, y, h, s, s, tc, tbt, pwr, dvec)


def _discretise(a_re, a_im, log_step, b_re, b_im):
    lr = jnp.minimum(a_re, -1e-4)
    li = a_im
    dt = jnp.exp(log_step)[:, None]
    mag = jnp.exp(lr * dt)
    abr = mag * jnp.cos(li * dt)
    abi = mag * jnp.sin(li * dt)
    den = lr * lr + li * li
    qr = ((abr - 1.0) * lr + abi * li) / den
    qi = (abi * lr - (abr - 1.0) * li) / den
    bbar_re = qr[..., None] * b_re - qi[..., None] * b_im
    bbar_im = qr[..., None] * b_im + qi[..., None] * b_re
    return abr, abi, bbar_re, bbar_im


def _compact(m_re, m_im, nkb):
    G, H, P = m_re.shape
    t = jnp.stack([m_re, m_im], axis=2).reshape(nkb, (G // nkb) * H, 2 * P).astype(BF)
    return t, jnp.swapaxes(t, 1, 2)


def _scan_powers(abr, abi, nkb, conj, CH):
    G, P = abr.shape
    if conj:
        abi = -abi

    def cmul(u, v):
        return u[0] * v[0] - u[1] * v[1], u[0] * v[1] + u[1] * v[0]

    q = (abr, abi)
    for _ in range(_log2(CH)):
        q = cmul(q, q)
    pows = [q]
    for _ in range(7):
        pows.append(cmul(pows[-1], q))
    row = jnp.arange(8)[:, None, None]

    def table(part):
        out = [jnp.broadcast_to((abr, abi)[part][None], (8, G, P))]
        for k in (1, 2, 4):
            keep = (row <= 7 - k) if conj else (row >= k)
            out.append(jnp.where(keep, pows[k - 1][part][None], 0.0))
        ends = jnp.stack([p[part] for p in pows])
        out.append(ends[::-1] if conj else ends)
        return jnp.concatenate(out, axis=0)

    GL = G // nkb
    t = jnp.stack([table(0), table(1)], axis=1)
    t = t.reshape(40, 2, nkb, GL * P).transpose(2, 0, 1, 3)
    return t.reshape(nkb, 40, 2 * GL * P)


def ada_mods(c_all, w_ada, b_sh, name):
    nl, D, NA = w_ada.shape

    def body(c_ref, w_ref, b_ref, o_ref):
        cv = c_ref[...]
        act = cv * jax.nn.sigmoid(cv)
        o_ref[...] = jnp.dot(act, w_ref[...], preferred_element_type=F32, precision=lax.Precision.HIGHEST) + b_ref[...]

    return pl.pallas_call(
        body, name=name, grid=(nl,),
        in_specs=[pl.BlockSpec((8, D), lambda i: (0, 0)), pl.BlockSpec((None, D, NA), lambda i: (i, 0, 0)),
                  pl.BlockSpec((None, 1, NA), lambda i: (i, 0, 0))],
        out_specs=pl.BlockSpec((None, 8, NA), lambda i: (i, 0, 0)),
        out_shape=jax.ShapeDtypeStruct((nl, 8, NA), F32), compiler_params=_cp("parallel"))(c_all, w_ada, b_sh)


def _adamw(w, g, m, v):
    m = ADAM_B1 * m + (1.0 - ADAM_B1) * g
    v = ADAM_B2 * v + (1.0 - ADAM_B2) * (g * g)
    m_hat = m / (1.0 - ADAM_B1 ** ADAM_STEP)
    v_hat = v / (1.0 - ADAM_B2 ** ADAM_STEP)
    return -ADAM_LR * (m_hat / (jnp.sqrt(v_hat) + ADAM_EPS) + ADAM_WD * w), m, v


def _adam_rows(R, C):
    cap = max(8, (256 * 1024) // C)
    for t in range(min(R, cap), 0, -1):
        if R % t == 0 and (t % 8 == 0 or t == R):
            return t
    return R


def adamw_ada(c_t, dm, w, m, v, name):
    nl, D, NA = w.shape
    TK = _tile(D, (256, 128))

    def body(c_ref, dm_ref, w_ref, m_ref, v_ref, g_ref, d_ref, nm_ref, nv_ref):
        cv = c_ref[...]
        act = cv * jax.nn.sigmoid(cv)
        g = jnp.dot(act, dm_ref[...], preferred_element_type=F32, precision=lax.Precision.HIGHEST)
        g_ref[...] = g
        d_ref[...], nm_ref[...], nv_ref[...] = _adamw(w_ref[...], g, m_ref[...], v_ref[...])

    big = pl.BlockSpec((None, TK, NA), lambda i, k: (i, k, 0))
    shape = jax.ShapeDtypeStruct(w.shape, F32)
    return pl.pallas_call(
        body, name=name, grid=(nl, D // TK),
        in_specs=[pl.BlockSpec((TK, 8), lambda i, k: (k, 0)), pl.BlockSpec((None, 8, NA), lambda i, k: (i, 0, 0)),
                  big, big, big],
        out_specs=[big] * 4, out_shape=[shape] * 4, compiler_params=_cp("parallel", "parallel"))(c_t, dm, w, m, v)


def adamw_sharded(w, m, v, ga, gb, name):
    nl, R, C = w.shape
    TR = _adam_rows(R, C)

    def body(w_ref, m_ref, v_ref, a_ref, b_ref, g_ref, d_ref, nm_ref, nv_ref):
        g = a_ref[...] + b_ref[...]
        g_ref[...] = g
        d_ref[...], nm_ref[...], nv_ref[...] = _adamw(w_ref[...], g, m_ref[...], v_ref[...])

    big = pl.BlockSpec((None, TR, C), lambda i, r: (i, r, 0))
    shape = jax.ShapeDtypeStruct(w.shape, F32)
    return pl.pallas_call(
        body, name=name, grid=(nl, R // TR), in_specs=[big] * 5,
        out_specs=[big] * 4, out_shape=[shape] * 4, compiler_params=_cp("parallel", "parallel"))(w, m, v, ga, gb)


def adamw_slab(g, w, m, v, name):
    R, C = g.shape
    TR = _tile(R, (160, 80, 40, 8))

    def body(g_ref, w_ref, m_ref, v_ref, d_ref, nm_ref, nv_ref):
        d_ref[...], nm_ref[...], nv_ref[...] = _adamw(w_ref[...], g_ref[...], m_ref[...], v_ref[...])

    big = pl.BlockSpec((TR, C), lambda r: (r, 0))
    shape = jax.ShapeDtypeStruct((R, C), F32)
    return pl.pallas_call(
        body, name=name, grid=(R // TR,), in_specs=[big] * 4,
        out_specs=[big] * 3, out_shape=[shape] * 3, compiler_params=_cp("parallel"))(g, w, m, v)


def adamw_plain(w, m, v, g, name):
    def body(w_ref, m_ref, v_ref, g_ref, d_ref, nm_ref, nv_ref):
        d_ref[...], nm_ref[...], nv_ref[...] = _adamw(w_ref[...], g_ref[...], m_ref[...], v_ref[...])

    shape = jax.ShapeDtypeStruct(w.shape, F32)
    return pl.pallas_call(body, name=name, out_shape=[shape] * 3,
                          compiler_params=pltpu.CompilerParams(vmem_limit_bytes=VMEM_LIMIT))(w, m, v, g)


def _slab_rows(a):
    n = a.size
    rows = -(-n // SLAB_W)
    return -(-rows // 8) * 8


def _pack(arrs, pad_rows_to=0):
    out = []
    for a in arrs:
        rows = _slab_rows(a)
        flat = a.reshape(-1).astype(F32)
        flat = jnp.pad(flat, (0, rows * SLAB_W - flat.shape[0]))
        out.append(flat.reshape(rows, SLAB_W))
    total = sum(o.shape[0] for o in out)
    if pad_rows_to and total % pad_rows_to:
        out.append(jnp.zeros((pad_rows_to - total % pad_rows_to, SLAB_W), F32))
    return jnp.concatenate(out, axis=0)


def _unpack(slab, like):
    out, r = [], 0
    for a in like:
        rows = _slab_rows(a)
        out.append(slab[r:r + rows].reshape(-1)[:a.size].reshape(a.shape))
        r += rows
    return out


WEIGHTS = ['norm1_g', 'norm2_g', 'w_ada', 'b_ada', 'ssm_a_re', 'ssm_a_im', 'ssm_log_step', 'ssm_b_re', 'ssm_b_im',
           'ssm_c_re', 'ssm_c_im', 'ssm_d', 'ssm_w_out', 'conv_w_in', 'conv_w', 'conv_w_out', 'w_ffn_in',
           'w_ffn_out', 'final_g']
SLAB = ['norm1_g', 'norm2_g', 'b_ada', 'ssm_a_re', 'ssm_a_im', 'ssm_log_step', 'ssm_b_re', 'ssm_b_im', 'ssm_c_re',
        'ssm_c_im', 'ssm_d', 'final_g']
SHARDED = ['ssm_w_out', 'conv_w_in', 'conv_w_out', 'w_ffn_in', 'w_ffn_out']


def kernel(x, c, norm1_g, norm2_g, w_ada, b_ada, ssm_a_re, ssm_a_im, ssm_log_step, ssm_b_re, ssm_b_im, ssm_c_re, ssm_c_im, ssm_d, ssm_w_out, conv_w_in, conv_w, conv_w_out, w_ffn_in, w_ffn_out, final_g, loss_target, m_norm1_g, m_norm2_g, m_w_ada, m_b_ada, m_ssm_a_re, m_ssm_a_im, m_ssm_log_step, m_ssm_b_re, m_ssm_b_im, m_ssm_c_re, m_ssm_c_im, m_ssm_d, m_ssm_w_out, m_conv_w_in, m_conv_w, m_conv_w_out, m_w_ffn_in, m_w_ffn_out, m_final_g, v_norm1_g, v_norm2_g, v_w_ada, v_b_ada, v_ssm_a_re, v_ssm_a_im, v_ssm_log_step, v_ssm_b_re, v_ssm_b_im, v_ssm_c_re, v_ssm_c_im, v_ssm_d, v_ssm_w_out, v_conv_w_in, v_conv_w, v_conv_w_out, v_w_ffn_in, v_w_ffn_out, v_final_g):
    given = dict(locals())
    W = {n: given[n] for n in WEIGHTS}
    Mo = {n: given["m_" + n] for n in WEIGHTS}
    Vo = {n: given["v_" + n] for n in WEIGHTS}

    xs = x[0]
    tgt = loss_target[0]
    L, D = xs.shape
    nlayer = norm1_g.shape[0]
    NA = w_ada.shape[2]
    G = ssm_a_re.shape[1]
    nkb = D // S5_BLOCK
    ax, ay, ac = _axes()
    me = 4 * ax + 2 * ay + ac
    chip = 2 * ax + ay

    assert D == SLAB_W
    first = gather8(jnp.concatenate([jnp.broadcast_to(c, (8, D)), _pack([conv_w])], axis=0), "gather_c_conv_w")
    c_all = first[:, 0, :]
    b_sh = lax.dynamic_slice_in_dim(b_ada, chip * NA, NA, axis=1)[:, None, :]
    mods_part = ada_mods(c_all, w_ada, b_sh, "ada_mods")
    mg = gather8(mods_part.reshape(nlayer * 8, NA), "gather_mods")
    mg = mg.reshape(N_CHIP, 2, nlayer, 8, NA)[:, 0]
    mods_all = lax.dynamic_index_in_dim(mg, me, axis=2, keepdims=False)
    mods_all = jnp.transpose(mods_all, (1, 0, 2)).reshape(nlayer, 6, D)

    cw_parts = first[:, 8:]
    nconv = conv_w.shape[0]
    cw_full = jnp.stack([_unpack(cw_parts[2 * q], [conv_w])[0] for q in range(N_CHIP)], axis=2)
    cw_full = cw_full.reshape(nconv, 3, D)

    in_flight_w = {}

    def start_weights(i, after):
        names = (["ssm_w_out"] if i % 2 == 0 else ["conv_w_in", "conv_w_out"]) + ["w_ffn_in", "w_ffn_out"]
        shards = [W[n][i if n.startswith("w_ffn") else i // 2].astype(BF) for n in names]
        sems, srcs, lands, tok = gather_start(shards, after, "gather_start%d" % i)
        in_flight_w[i] = (names, sems, srcs, lands)
        return tok

    def relay_weights(i, after):
        names, sems, srcs, lands = in_flight_w[i]
        got = gather_wait(sems, srcs, lands, list(range(len(names))), after, "gather_wait%d" % i)
        rsems, rlands, tok = relay_start(got, after, "relay_start%d" % i)
        in_flight_w[i] = (names, rsems, rlands)
        return tok

    def layer_weights(i, after):
        names, rsems, rlands = in_flight_w[i]
        return dict(zip(names, relay_wait(rsems, rlands, after, "relay_wait%d" % i)))

    token = start_weights(0, cw_full + mods_all[0, 0:3])
    mods_all = mods_all + token[0:1, 0:1]

    s5 = []
    for j in range(ssm_a_re.shape[0]):
        disc, disc_vjp = jax.vjp(_discretise, ssm_a_re[j], ssm_a_im[j], ssm_log_step[j], ssm_b_re[j], ssm_b_im[j])
        abr, abi, bbar_re, bbar_im = disc
        tb, tbt = _compact(jnp.swapaxes(bbar_re, 1, 2), jnp.swapaxes(bbar_im, 1, 2), nkb)
        tc, tct = _compact(ssm_c_re[j], -ssm_c_im[j], nkb)
        chunk = _tile(L, (512, 256)) // 8
        s5.append(dict(vjp=disc_vjp, tb=tb, tbt=tbt, tc=tc, tct=tct, pw=_scan_powers(abr, abi, nkb, False, chunk),
                       pwr=_scan_powers(abr, abi, nkb, True, chunk)))

    saved = []
    xcur = xs
    for i in range(nlayer):
        j = i // 2
        mods = mods_all[i]
        sv = dict(x=xcur)
        if i % 2 == 0:
            h = norm_mod(xcur, norm1_g[i:i + 1], mods, 0, F32, "norm_mod_s5")
            dvec = ssm_d[j:j + 1]
            if i == 0:
                dvec = dvec + start_weights(1, h)[0:1, 0:1]
            states, yv, z = s5_fwd(h, s5[j]["tb"], s5[j]["tct"], s5[j]["pw"], dvec, "s5_fwd")
            if i == 0:
                mods = mods + relay_weights(0, z)[0:1, 0:1]
            full = layer_weights(i, z)
            o, mix, x2 = ssm_out_glu(z, full["ssm_w_out"], xcur, mods, 2, "ssm_out_glu")
            sv.update(h=h, states=states, y=yv, z=z, o=o)
        else:
            h = norm_mod(xcur, norm1_g[i:i + 1], mods, 0, BF, "norm_mod")
            full = layer_weights(i, h)
            p = mm_nn(h, full["conv_w_in"], BF, "mm_conv_in")
            mc = conv_fwd(p, cw_full[j], "conv_fwd")
            mix, x2 = mm_nn(mc, full["conv_w_out"].reshape(1, D, D), BF, "mm_conv_out", res=xcur, gate=mods[2:3])
            sv.update(h=h, p=p, mc=mc)
        h2 = norm_mod(x2, norm2_g[i:i + 1], mods, 3, BF, "norm_mod")
        gu, act = ffn_in_act(h2, full["w_ffn_in"], "ffn_in_act")
        if i + 1 < nlayer:
            token = relay_weights(i + 1, act)
            if i + 2 < nlayer:
                token = token + start_weights(i + 2, token)
            mods = mods + token[0:1, 0:1]
        F = act.shape[1]
        ff, x3 = mm_nn(act, full["w_ffn_out"].reshape(1, F, D), BF, "mm_ffn_out", res=x2, gate=mods[5:6])
        sv.update(mix=mix, x2=x2, h2=h2, gu=gu, act=act, ff=ff, w=full)
        saved.append(sv)
        xcur = x3

    loss_blk, dx, dfinal, dff = final_loss(xcur, tgt, final_g[None, :], saved[-1]["ff"], mods_all[nlayer - 1], 5,
                                           "final_loss")
    dg2 = dfinal[1:2]

    gland = {n: lax.empty((W[n].shape[0], N_CHIP) + W[n].shape[1:], BF) for n in SHARDED}
    in_flight = []
    dmods = [None] * nlayer
    dnorm1, dnorm2 = [None] * nlayer, [None] * nlayer
    dconv_w = [None] * nconv
    ds5 = [None] * ssm_a_re.shape[0]
    token = jnp.zeros((8, 128), F32)

    def send_grads(names, grads, slot, after, name):
        sems, thru, lands, tok = scatter_start([grads[n] for n in names], [gland[n] for n in names], slot, after, name)
        gland.update(zip(names, lands))
        in_flight.append((names, slot, sems, thru, name))
        return tok

    def land_grads(group, after):
        for names, slot, sems, thru, name in in_flight:
            if names[0] in group:
                got = scatter_wait(sems, thru, [gland[n] for n in names], slot, after, name.replace("scatter", "landed"))
                gland.update(zip(names, got))

    for i in reversed(range(nlayer)):
        j = i // 2
        mods = mods_all[i] + token[0:1, 0:1]
        sv = saved[i]
        full = sv["w"]
        gfull = {}
        F = sv["act"].shape[1]
        gfull["w_ffn_out"] = mm_tn(sv["act"], dff, 1, "mm_tn_ffn_out").reshape(N_CHIP, F // N_CHIP, D)
        dgu = ffn_out_bwd(dff, full["w_ffn_out"].reshape(F, D), sv["gu"], "ffn_out_bwd")
        gfull["w_ffn_in"] = mm_tn(sv["h2"], dgu, N_CHIP, "mm_tn_ffn_in")
        dh2 = mm_nt(dgu, full["w_ffn_in"], BF, "mm_nt_ffn_in")
        token = send_grads(["w_ffn_out", "w_ffn_in"], gfull, [i, i], dh2, "scatter_ffn%d" % i)
        mods = mods + token[0:1, 0:1]
        dx2, s2, dmix = norm_bwd(dh2, sv["x2"], dx, norm2_g[i:i + 1], mods, 3, "norm_bwd_mix",
                                 branch=(sv["mix"], mods, 2))
        dg1 = s2[3:4]
        if i % 2 == 0:
            do = glu_bwd(dmix, sv["o"], "glu_bwd")
            gfull["ssm_w_out"] = mm_tn(sv["z"], do, N_CHIP, "mm_tn_ssm_out")
            dz = mm_nt(do, full["ssm_w_out"], BF, "mm_nt_ssm_out")
            dh, dd, dab, db, dc = s5_bwd(dz, sv["y"], sv["h"], sv["states"], s5[j]["tc"], s5[j]["tbt"], s5[j]["pwr"],
                                         ssm_d[j:j + 1], "s5_bwd")
            ds5[j] = (dd, dab, db, dc)
        else:
            gfull["conv_w_out"] = mm_tn(sv["mc"], dmix, 1, "mm_tn_conv_out").reshape(N_CHIP, D // N_CHIP, D)
            dmc = mm_nt(dmix, full["conv_w_out"].reshape(1, D, D), BF, "mm_nt_conv_out")
            dbg, dcg, dvv, dcw = conv_bwd(dmc, sv["p"], cw_full[j], "conv_bwd")
            dp = jnp.concatenate([dbg, dcg, dvv], axis=1)
            gfull["conv_w_in"] = mm_tn(sv["h"], dp, N_CHIP, "mm_tn_conv_in")
            dh = mm_nt(dp, full["conv_w_in"], BF, "mm_nt_conv_in")
            dconv_w[j] = dcw[0:3]
        dmods_i = [s2[0:2], dg2]
        if i > 0:
            dx, s1, dff = norm_bwd(dh, sv["x"], dx2, norm1_g[i:i + 1], mods, 0, "norm_bwd_ffn",
                                   branch=(saved[i - 1]["ff"], mods_all[i - 1], 5))
            dg2 = s1[3:4]
        else:
            dx, s1 = norm_bwd(dh, sv["x"], dx2, norm1_g[i:i + 1], mods, 0, "norm_bwd")
        dmods[i] = jnp.concatenate([s1[0:2], dg1] + dmods_i, axis=0).reshape(6 * D)
        dnorm1[i], dnorm2[i] = s1[2], s2[2]
        names = ["ssm_w_out"] if i % 2 == 0 else ["conv_w_out", "conv_w_in"]
        token = send_grads(names, gfull, [j] * len(names), dx, "scatter_mix%d" % i)

    small = dict(norm1_g=jnp.stack(dnorm1), norm2_g=jnp.stack(dnorm2), b_ada=jnp.stack(dmods),
                 final_g=dfinal[0] + token[0, 0])
    per = {n: [] for n in ('ssm_a_re', 'ssm_a_im', 'ssm_log_step', 'ssm_b_re', 'ssm_b_im', 'ssm_c_re', 'ssm_c_im', 'ssm_d')}
    GL = G // nkb
    for j, (dd, dab, db, dc) in enumerate(ds5):
        dab = jnp.sum(dab, axis=1).reshape(nkb, 2, GL, SSM_STATE)
        g_abr, g_abi = dab[:, 0].reshape(G, SSM_STATE), dab[:, 1].reshape(G, SSM_STATE)
        db, dc = db.reshape(G, SSM_GROUP, 2, SSM_STATE), dc.reshape(G, SSM_GROUP, 2, SSM_STATE)
        gb_re, gb_im, gc_re, gc_im = db[:, :, 0], db[:, :, 1], dc[:, :, 0], dc[:, :, 1]
        ga_re, ga_im, gls, gbr, gbi = s5[j]["vjp"]((g_abr, g_abi, jnp.swapaxes(gb_re, 1, 2), jnp.swapaxes(gb_im, 1, 2)))
        for n, val in zip(per, (ga_re, ga_im, gls, gbr, gbi, gc_re, -gc_im, jnp.sum(dd, axis=0))):
            per[n].append(val)
    small.update({n: jnp.stack(vals) for n, vals in per.items()})
    dcw_full = jnp.stack(dconv_w)

    my_loss = loss_blk[0:1, 0:1]
    slab_like = [W[n] for n in SLAB] + [dcw_full, my_loss]
    rows64 = 8 * N_DEV
    slab = _pack([small[n] for n in SLAB] + [dcw_full, my_loss], rows64)
    per_dev = slab.shape[0] // N_DEV
    x_sems, x_srcs, x_lands, token = exchange_start(
        [(slab.reshape(N_DEV, per_dev, SLAB_W), True), (_pack([small["b_ada"]]), False)], dx, "small_scatter")

    early = [n for n in SHARDED if n != "ssm_w_out"]
    land_grads(early, token)
    mine = [reduce4(gland[n], "reduce4_" + n) for n in early]

    parts, dm_all = exchange_wait(x_sems, x_srcs, x_lands, [True, False], mine[-1][0, :8, :128], "small_landed")
    t_sems, t_srcs, t_lands, token = exchange_start([(sum8(parts, "sum_small"), False)], dm_all, "small_gather")
    out = {}

    w_sems, w_srcs, w_lands, token2 = swap_start(mine, "swap_start")
    dm_all = dm_all.reshape(N_DEV, -1)[:, :b_ada.size].reshape(N_DEV, nlayer, N_CHIP, NA)
    dm_sh = jnp.transpose(lax.dynamic_index_in_dim(dm_all, chip, axis=2, keepdims=False), (1, 0, 2))
    res = adamw_ada(jnp.transpose(c_all) + token[0:1, 0:1] + token2[0:1, 0:1], dm_sh, w_ada, m_w_ada, v_w_ada,
                    "adamw_ada")
    out["g", "w_ada"], out["d", "w_ada"], out["m", "w_ada"], out["v", "w_ada"] = res

    g_slab = exchange_wait(t_sems, t_srcs, t_lands, [False], out["g", "w_ada"], "small_total")[0]
    g_slab = g_slab.reshape(slab.shape)
    d_slab, m_slab, v_slab = adamw_slab(
        g_slab, _pack([W[n] for n in SLAB] + [jnp.zeros_like(dcw_full)], rows64),
        _pack([Mo[n] for n in SLAB] + [jnp.zeros_like(dcw_full)], rows64),
        _pack([Vo[n] for n in SLAB] + [jnp.ones_like(dcw_full)], rows64), "adamw_slab")
    for k, slab in zip(("g", "d", "m", "v"), (g_slab, d_slab, m_slab, v_slab)):
        for n, val in zip(SLAB, _unpack(slab, slab_like)):
            out[k, n] = val
    g_cw = lax.dynamic_slice_in_dim(_unpack(g_slab, slab_like)[-2], chip * conv_w.shape[2], conv_w.shape[2], axis=2)
    out["g", "conv_w"] = g_cw
    out["d", "conv_w"], out["m", "conv_w"], out["v", "conv_w"] = [
        r.reshape(conv_w.shape) for r in adamw_plain(conv_w.reshape(-1, conv_w.shape[2]), m_conv_w.reshape(-1, conv_w.shape[2]),
                                                     v_conv_w.reshape(-1, conv_w.shape[2]), g_cw.reshape(-1, conv_w.shape[2]),
                                                     "adamw_conv_w")]

    mine, theirs = swap_wait(w_sems, w_srcs, w_lands, d_slab, "swap_wait")
    for n, ga, gb in zip(early, mine, theirs):
        r = adamw_sharded(W[n], Mo[n], Vo[n], ga, gb, "adamw_" + n)
        out["g", n], out["d", n], out["m", n], out["v", n] = r

    land_grads(["ssm_w_out"], out["g", "w_ffn_out"])
    ga = reduce4(gland["ssm_w_out"], "reduce4_ssm_w_out")
    gb = swap_siblings([ga], "swap_siblings")[0]
    r = adamw_sharded(ssm_w_out, m_ssm_w_out, v_ssm_w_out, ga, gb, "adamw_ssm_w_out")
    out["g", "ssm_w_out"], out["d", "ssm_w_out"], out["m", "ssm_w_out"], out["v", "ssm_w_out"] = r

    loss = _unpack(g_slab, slab_like)[-1][0, 0]
    return (loss, dx[None], *[out["g", n] for n in WEIGHTS], *[out["d", n] for n in WEIGHTS],
            *[out["m", n] for n in WEIGHTS], *[out["v", n] for n in WEIGHTS])
```

```python
import math

import jax
import jax.numpy as jnp
from jax import lax
from jax.experimental import pallas as pl
from jax.experimental.pallas import tpu as pltpu

F32 = jnp.float32
BF = jnp.bfloat16
MESH = pl.DeviceIdType.MESH
ANY = pl.BlockSpec(memory_space=pl.ANY)

N_DEV = 8
N_CHIP = 4
SSM_GROUP = 16
SSM_STATE = 64
S5_BLOCK = 256
RMS_EPS = 1e-6
ADAM_LR, ADAM_B1, ADAM_B2, ADAM_EPS, ADAM_WD, ADAM_STEP = 0.001, 0.9, 0.999, 1e-08, 0.01, 10
V7X_VMEM_BYTES = 64 * 1024 * 1024
VMEM_LIMIT = V7X_VMEM_BYTES - 12 * 1024 * 1024
SLAB_W = 1024
GELU_C = math.sqrt(2.0 / math.pi)
GELU_A = 0.044715


def _cp(*sem):
    return pltpu.CompilerParams(dimension_semantics=sem if sem else None, vmem_limit_bytes=VMEM_LIMIT)


def _tile(n, prefs):
    for p in prefs:
        if p <= n and n % p == 0:
            return p
    return n


def _axes():
    return lax.axis_index("x"), lax.axis_index("y"), lax.axis_index("c")


def _flip(v, k):
    return 1 - v if k else v


def gather8(v, name):
    R, C = v.shape

    def body(v_ref, o_ref, ssem, rsem, lsem):
        x, y, c = _axes()
        me = 4 * x + 2 * y + c
        loc = pltpu.make_async_copy(v_ref, o_ref.at[me], lsem)
        loc.start()
        copies = []
        for k in range(1, N_DEV):
            peer = (_flip(x, (k >> 2) & 1), _flip(y, (k >> 1) & 1), _flip(c, k & 1))
            cp = pltpu.make_async_remote_copy(src_ref=v_ref, dst_ref=o_ref.at[me], send_sem=ssem.at[k - 1],
                                              recv_sem=rsem.at[k - 1], device_id=peer, device_id_type=MESH)
            cp.start()
            copies.append(cp)
        for cp in copies:
            cp.wait()
        loc.wait()

    return pl.pallas_call(
        body, name=name,
        out_shape=jax.ShapeDtypeStruct((N_DEV, R, C), v.dtype),
        in_specs=[pl.BlockSpec(memory_space=pltpu.VMEM)],
        out_specs=pl.BlockSpec(memory_space=pltpu.VMEM),
        scratch_shapes=[pltpu.SemaphoreType.DMA((N_DEV - 1,)), pltpu.SemaphoreType.DMA((N_DEV - 1,)),
                        pltpu.SemaphoreType.DMA],
        compiler_params=pltpu.CompilerParams(vmem_limit_bytes=VMEM_LIMIT),
    )(v)


HBM = pl.BlockSpec(memory_space=pltpu.HBM)
SEM = pl.BlockSpec(memory_space=pltpu.SEMAPHORE)
EFFECT = pltpu.SideEffectType.DATAFLOW_SIDE_EFFECTING


def _in_hbm(a):
    return pltpu.with_memory_space_constraint(a, pltpu.HBM)


def _chip_peers(x, y, c):
    out = []
    for k in range(1, N_CHIP):
        px, py = _flip(x, k >> 1), _flip(y, k & 1)
        out.append(((px, py, c), 2 * px + py))
    return out


def _my_half(ref, c):
    rows = ref.shape[0] // 2
    return pl.ds(pl.multiple_of(c * rows, 16), rows)


def relay_start(lands, after, name):
    n = len(lands)

    def body(*refs):
        land = refs[:n]
        ssem, rsem = refs[n + 1:n + 3]
        token = refs[-1]
        x, y, c = _axes()
        for a in range(n):
            half = _my_half(land[a].at[0], c)
            for k, (_, pchip) in enumerate(_chip_peers(x, y, c)):
                pltpu.make_async_remote_copy(src_ref=land[a].at[pchip, half], dst_ref=land[a].at[pchip, half],
                                             send_sem=ssem.at[3 * a + k], recv_sem=rsem.at[3 * a + k],
                                             device_id=(x, y, 1 - c), device_id_type=MESH).start()
        token[...] = jnp.zeros_like(token)

    out_shape = ([pltpu.SemaphoreType.DMA((3 * n,)), pltpu.SemaphoreType.DMA((3 * n,))]
                 + [pltpu.HBM(l.shape, l.dtype) for l in lands] + [jax.ShapeDtypeStruct((8, 128), F32)])
    res = pl.pallas_call(
        body, name=name, out_shape=out_shape, in_specs=[HBM] * n + [ANY],
        out_specs=[SEM, SEM] + [HBM] * n + [pl.BlockSpec(memory_space=pltpu.VMEM)],
        input_output_aliases={a: 2 + a for a in range(n)},
        compiler_params=pltpu.CompilerParams(has_side_effects=EFFECT),
    )(*lands, after)
    return tuple(res[:2]), list(res[2:2 + n]), res[-1]


def relay_wait(sems, lands, after, name):
    n = len(lands)

    def body(*refs):
        land = refs[:n]
        ssem, rsem = refs[n:n + 2]
        x, y, c = _axes()
        for a in range(n):
            mine, theirs = _my_half(land[a].at[0], c), _my_half(land[a].at[0], 1 - c)
            for k, (_, pchip) in enumerate(_chip_peers(x, y, c)):
                cp = pltpu.make_async_remote_copy(src_ref=land[a].at[pchip, mine], dst_ref=land[a].at[pchip, theirs],
                                                  send_sem=ssem.at[3 * a + k], recv_sem=rsem.at[3 * a + k],
                                                  device_id=(x, y, 1 - c), device_id_type=MESH)
                cp.wait_send()
                cp.wait_recv()

    res = pl.pallas_call(
        body, name=name, out_shape=[pltpu.HBM(l.shape, l.dtype) for l in lands],
        in_specs=[HBM] * n + [SEM, SEM, ANY], out_specs=[HBM] * n,
        input_output_aliases={a: a for a in range(n)},
        compiler_params=pltpu.CompilerParams(has_side_effects=EFFECT),
    )(*lands, *sems, after)
    return list(res)


def gather_start(shards, after, name):
    n = len(shards)

    def body(*refs):
        src, land = refs[:n], refs[n:2 * n]
        ssem, rsem, lsem = refs[2 * n + 1:2 * n + 4]
        token = refs[-1]
        x, y, c = _axes()
        chip = 2 * x + y
        for a in range(n):
            pltpu.make_async_copy(src[a], land[a].at[chip], lsem.at[a]).start()
            half = _my_half(src[a], c)
            for k, (peer, _) in enumerate(_chip_peers(x, y, c)):
                pltpu.make_async_remote_copy(src_ref=src[a].at[half], dst_ref=land[a].at[chip, half],
                                             send_sem=ssem.at[3 * a + k], recv_sem=rsem.at[3 * a + k],
                                             device_id=peer, device_id_type=MESH).start()
        token[...] = jnp.zeros_like(token)

    lands = [lax.empty((N_CHIP,) + s.shape, s.dtype) for s in shards]
    out_shape = ([pltpu.SemaphoreType.DMA((3 * n,)), pltpu.SemaphoreType.DMA((3 * n,)), pltpu.SemaphoreType.DMA((n,))]
                 + [pltpu.HBM(s.shape, s.dtype) for s in shards] + [pltpu.HBM(l.shape, l.dtype) for l in lands]
                 + [jax.ShapeDtypeStruct((8, 128), F32)])
    res = pl.pallas_call(
        body, name=name, out_shape=out_shape, in_specs=[HBM] * (2 * n) + [ANY],
        out_specs=[SEM, SEM, SEM] + [HBM] * (2 * n) + [pl.BlockSpec(memory_space=pltpu.VMEM)],
        input_output_aliases={a: 3 + a for a in range(2 * n)},
        compiler_params=pltpu.CompilerParams(has_side_effects=EFFECT),
    )(*[_in_hbm(s) for s in shards], *[_in_hbm(l) for l in lands], after)
    return tuple(res[:3]), list(res[3:3 + n]), list(res[3 + n:3 + 2 * n]), res[-1]


def gather_wait(sems, srcs, lands, idx, after, name):
    m = len(idx)

    def body(*refs):
        src, land = refs[:m], refs[m:2 * m]
        ssem, rsem, lsem = refs[2 * m:2 * m + 3]
        x, y, c = _axes()
        chip = 2 * x + y
        for j, a in enumerate(idx):
            half = _my_half(src[j], c)
            for k, (peer, pchip) in enumerate(_chip_peers(x, y, c)):
                cp = pltpu.make_async_remote_copy(src_ref=src[j].at[half], dst_ref=land[j].at[pchip, half],
                                                  send_sem=ssem.at[3 * a + k], recv_sem=rsem.at[3 * a + k],
                                                  device_id=peer, device_id_type=MESH)
                cp.wait_send()
                cp.wait_recv()
            pltpu.make_async_copy(src[j], land[j].at[chip], lsem.at[a]).wait()

    s_in = [srcs[a] for a in idx]
    l_in = [lands[a] for a in idx]
    res = pl.pallas_call(
        body, name=name,
        out_shape=[pltpu.HBM(s.shape, s.dtype) for s in s_in] + [pltpu.HBM(l.shape, l.dtype) for l in l_in],
        in_specs=[HBM] * (2 * m) + [SEM, SEM, SEM, ANY], out_specs=[HBM] * (2 * m),
        input_output_aliases={a: a for a in range(2 * m)},
        compiler_params=pltpu.CompilerParams(has_side_effects=EFFECT),
    )(*s_in, *l_in, *sems, after)
    return list(res[m:])


def scatter_start(grads, lands, slot, after, name):
    n = len(grads)

    def body(*refs):
        src, land = refs[:n], refs[n:2 * n]
        ssem, rsem, lsem = refs[2 * n + 1:2 * n + 4]
        token = refs[-1]
        x, y, c = _axes()
        chip = 2 * x + y
        for a in range(n):
            pltpu.make_async_copy(src[a].at[chip], land[a].at[slot[a], chip], lsem.at[a]).start()
            for k, (peer, pchip) in enumerate(_chip_peers(x, y, c)):
                pltpu.make_async_remote_copy(src_ref=src[a].at[pchip], dst_ref=land[a].at[slot[a], chip],
                                             send_sem=ssem.at[3 * a + k], recv_sem=rsem.at[3 * a + k],
                                             device_id=peer, device_id_type=MESH).start()
        token[...] = jnp.zeros_like(token)

    out_shape = ([pltpu.SemaphoreType.DMA((3 * n,)), pltpu.SemaphoreType.DMA((3 * n,)), pltpu.SemaphoreType.DMA((n,))]
                 + [pltpu.HBM(g.shape, g.dtype) for g in grads] + [pltpu.HBM(l.shape, l.dtype) for l in lands]
                 + [jax.ShapeDtypeStruct((8, 128), F32)])
    res = pl.pallas_call(
        body, name=name, out_shape=out_shape, in_specs=[HBM] * (2 * n) + [ANY],
        out_specs=[SEM, SEM, SEM] + [HBM] * (2 * n) + [pl.BlockSpec(memory_space=pltpu.VMEM)],
        input_output_aliases={a: 3 + a for a in range(2 * n)},
        compiler_params=pltpu.CompilerParams(has_side_effects=EFFECT),
    )(*[_in_hbm(g) for g in grads], *[_in_hbm(l) for l in lands], after)
    return tuple(res[:3]), list(res[3:3 + n]), list(res[3 + n:3 + 2 * n]), res[-1]


def scatter_wait(sems, grads, lands, slot, after, name):
    n = len(grads)

    def body(*refs):
        src, land = refs[:n], refs[n:2 * n]
        ssem, rsem, lsem = refs[2 * n:2 * n + 3]
        x, y, c = _axes()
        chip = 2 * x + y
        for a in range(n):
            for k, (peer, pchip) in enumerate(_chip_peers(x, y, c)):
                cp = pltpu.make_async_remote_copy(src_ref=src[a].at[pchip], dst_ref=land[a].at[slot[a], pchip],
                                                  send_sem=ssem.at[3 * a + k], recv_sem=rsem.at[3 * a + k],
                                                  device_id=peer, device_id_type=MESH)
                cp.wait_send()
                cp.wait_recv()
            pltpu.make_async_copy(src[a].at[chip], land[a].at[slot[a], chip], lsem.at[a]).wait()

    res = pl.pallas_call(
        body, name=name,
        out_shape=[pltpu.HBM(g.shape, g.dtype) for g in grads] + [pltpu.HBM(l.shape, l.dtype) for l in lands],
        in_specs=[HBM] * (2 * n) + [SEM, SEM, SEM, ANY], out_specs=[HBM] * (2 * n),
        input_output_aliases={a: a for a in range(2 * n)},
        compiler_params=pltpu.CompilerParams(has_side_effects=EFFECT),
    )(*grads, *lands, *sems, after)
    return list(res[n:])


def reduce4(land, name):
    nl, _, R, C = land.shape
    TR = _adam_rows(R, C)

    def body(l_ref, o_ref):
        o_ref[...] = ((l_ref[0].astype(F32) + l_ref[1].astype(F32)) + l_ref[2].astype(F32)) + l_ref[3].astype(F32)

    return pl.pallas_call(
        body, name=name, grid=(nl, R // TR),
        in_specs=[pl.BlockSpec((None, N_CHIP, TR, C), lambda i, r: (i, 0, r, 0))],
        out_specs=pl.BlockSpec((None, TR, C), lambda i, r: (i, r, 0)),
        out_shape=jax.ShapeDtypeStruct((nl, R, C), F32), compiler_params=_cp("parallel", "parallel"))(land)


def swap_siblings(arrs, name):
    n = len(arrs)

    def body(*refs):
        src, dst = refs[:n], refs[n:2 * n]
        ssem, rsem = refs[2 * n:]
        x, y, c = _axes()
        cps = [pltpu.make_async_remote_copy(src_ref=src[a], dst_ref=dst[a], send_sem=ssem.at[a], recv_sem=rsem.at[a],
                                            device_id=(x, y, 1 - c), device_id_type=MESH) for a in range(n)]
        for cp in cps:
            cp.start()
        for cp in cps:
            cp.wait()

    return pl.pallas_call(
        body, name=name, out_shape=[jax.ShapeDtypeStruct(a.shape, a.dtype) for a in arrs],
        in_specs=[ANY] * n, out_specs=[ANY] * n,
        scratch_shapes=[pltpu.SemaphoreType.DMA((n,)), pltpu.SemaphoreType.DMA((n,))],
        compiler_params=pltpu.CompilerParams(vmem_limit_bytes=VMEM_LIMIT),
    )(*arrs)


def swap_start(arrs, name):
    n = len(arrs)

    def body(*refs):
        src, land = refs[:n], refs[n:2 * n]
        ssem, rsem = refs[2 * n:2 * n + 2]
        token = refs[-1]
        x, y, c = _axes()
        for a in range(n):
            pltpu.make_async_remote_copy(src_ref=src[a], dst_ref=land[a], send_sem=ssem.at[a], recv_sem=rsem.at[a],
                                         device_id=(x, y, 1 - c), device_id_type=MESH).start()
        token[...] = jnp.zeros_like(token)

    lands = [lax.empty(a.shape, a.dtype) for a in arrs]
    out_shape = ([pltpu.SemaphoreType.DMA((n,)), pltpu.SemaphoreType.DMA((n,))]
                 + [pltpu.HBM(a.shape, a.dtype) for a in arrs] * 2 + [jax.ShapeDtypeStruct((8, 128), F32)])
    res = pl.pallas_call(
        body, name=name, out_shape=out_shape, in_specs=[HBM] * (2 * n),
        out_specs=[SEM, SEM] + [HBM] * (2 * n) + [pl.BlockSpec(memory_space=pltpu.VMEM)],
        input_output_aliases={a: 2 + a for a in range(2 * n)},
        compiler_params=pltpu.CompilerParams(has_side_effects=EFFECT),
    )(*[_in_hbm(a) for a in arrs], *[_in_hbm(l) for l in lands])
    return tuple(res[:2]), list(res[2:2 + n]), list(res[2 + n:2 + 2 * n]), res[-1]


def swap_wait(sems, srcs, lands, after, name):
    n = len(srcs)

    def body(*refs):
        src, land = refs[:n], refs[n:2 * n]
        ssem, rsem = refs[2 * n:2 * n + 2]
        x, y, c = _axes()
        for a in range(n):
            cp = pltpu.make_async_remote_copy(src_ref=src[a], dst_ref=land[a], send_sem=ssem.at[a],
                                              recv_sem=rsem.at[a], device_id=(x, y, 1 - c), device_id_type=MESH)
            cp.wait_send()
            cp.wait_recv()

    res = pl.pallas_call(
        body, name=name, out_shape=[pltpu.HBM(a.shape, a.dtype) for a in srcs] * 2,
        in_specs=[HBM] * (2 * n) + [SEM, SEM, ANY], out_specs=[HBM] * (2 * n),
        input_output_aliases={a: a for a in range(2 * n)},
        compiler_params=pltpu.CompilerParams(has_side_effects=EFFECT),
    )(*srcs, *lands, *sems, after)
    return list(res[:n]), list(res[n:])


def _all_peers(x, y, c):
    out = []
    for k in range(1, N_DEV):
        px, py, pc = _flip(x, (k >> 2) & 1), _flip(y, (k >> 1) & 1), _flip(c, k & 1)
        out.append(((px, py, pc), 4 * px + 2 * py + pc))
    return out


def exchange_start(items, after, name):
    n = len(items)

    def body(*refs):
        src, land = refs[:n], refs[n:2 * n]
        ssem, rsem, lsem = refs[2 * n + 1:2 * n + 4]
        token = refs[-1]
        x, y, c = _axes()
        me = 4 * x + 2 * y + c
        for a, (_, scatter) in enumerate(items):
            pltpu.make_async_copy(src[a].at[me] if scatter else src[a], land[a].at[me], lsem.at[a]).start()
            for k, (peer, p) in enumerate(_all_peers(x, y, c)):
                pltpu.make_async_remote_copy(src_ref=src[a].at[p] if scatter else src[a], dst_ref=land[a].at[me],
                                             send_sem=ssem.at[7 * a + k], recv_sem=rsem.at[7 * a + k],
                                             device_id=peer, device_id_type=MESH).start()
        token[...] = jnp.zeros_like(token)

    srcs = [s for s, _ in items]
    lands = [lax.empty(s.shape if sc else (N_DEV,) + s.shape, s.dtype) for s, sc in items]
    out_shape = ([pltpu.SemaphoreType.DMA((7 * n,)), pltpu.SemaphoreType.DMA((7 * n,)), pltpu.SemaphoreType.DMA((n,))]
                 + [pltpu.HBM(s.shape, s.dtype) for s in srcs] + [pltpu.HBM(l.shape, l.dtype) for l in lands]
                 + [jax.ShapeDtypeStruct((8, 128), F32)])
    res = pl.pallas_call(
        body, name=name, out_shape=out_shape, in_specs=[HBM] * (2 * n) + [ANY],
        out_specs=[SEM, SEM, SEM] + [HBM] * (2 * n) + [pl.BlockSpec(memory_space=pltpu.VMEM)],
        input_output_aliases={a: 3 + a for a in range(2 * n)},
        compiler_params=pltpu.CompilerParams(has_side_effects=EFFECT),
    )(*[_in_hbm(s) for s in srcs], *[_in_hbm(l) for l in lands], after)
    return tuple(res[:3]), list(res[3:3 + n]), list(res[3 + n:3 + 2 * n]), res[-1]


def exchange_wait(sems, srcs, lands, scatter, after, name):
    n = len(srcs)

    def body(*refs):
        src, land = refs[:n], refs[n:2 * n]
        ssem, rsem, lsem = refs[2 * n:2 * n + 3]
        x, y, c = _axes()
        me = 4 * x + 2 * y + c
        for a in range(n):
            for k, (peer, p) in enumerate(_all_peers(x, y, c)):
                cp = pltpu.make_async_remote_copy(src_ref=src[a].at[p] if scatter[a] else src[a],
                                                  dst_ref=land[a].at[p], send_sem=ssem.at[7 * a + k],
                                                  recv_sem=rsem.at[7 * a + k], device_id=peer, device_id_type=MESH)
                cp.wait_send()
                cp.wait_recv()
            pltpu.make_async_copy(src[a].at[me] if scatter[a] else src[a], land[a].at[me], lsem.at[a]).wait()

    res = pl.pallas_call(
        body, name=name,
        out_shape=[pltpu.HBM(s.shape, s.dtype) for s in srcs] + [pltpu.HBM(l.shape, l.dtype) for l in lands],
        in_specs=[HBM] * (2 * n) + [SEM, SEM, SEM, ANY], out_specs=[HBM] * (2 * n),
        input_output_aliases={a: a for a in range(2 * n)},
        compiler_params=pltpu.CompilerParams(has_side_effects=EFFECT),
    )(*srcs, *lands, *sems, after)
    return list(res[n:])


def sum8(parts, name):
    _, P, C = parts.shape

    def body(p_ref, o_ref):
        tot = p_ref[0]
        for d in range(1, N_DEV):
            tot = tot + p_ref[d]
        o_ref[...] = tot

    return pl.pallas_call(body, name=name, out_shape=jax.ShapeDtypeStruct((P, C), F32),
                          compiler_params=pltpu.CompilerParams(vmem_limit_bytes=VMEM_LIMIT))(parts)


def mm_nn(a, w, out_dtype, name, res=None, gate=None):
    M, K = a.shape
    S, _, Ns = w.shape
    TM = _tile(M, (1024, 512, 256) if K <= 1024 else (512, 256))
    TN = _tile(Ns, (1408, 1024, 768, 512, 256, 128))
    nj = Ns // TN
    fused = res is not None

    def body(*refs):
        if fused:
            a_ref, w_ref, r_ref, g_ref, f_ref, o_ref = refs
        else:
            a_ref, w_ref, f_ref = refs
        f = jnp.dot(a_ref[...], w_ref[...], preferred_element_type=F32)
        f_ref[...] = f.astype(f_ref.dtype)
        if fused:
            o_ref[...] = r_ref[...] + g_ref[...] * f

    col = lambda s, j, i: (i, s * nj + j)
    in_specs = [pl.BlockSpec((TM, K), lambda s, j, i: (i, 0)), pl.BlockSpec((None, K, TN), lambda s, j, i: (s, 0, j))]
    out_specs = [pl.BlockSpec((TM, TN), col)]
    out_shape = [jax.ShapeDtypeStruct((M, S * Ns), out_dtype)]
    args = [a, w]
    if fused:
        in_specs += [pl.BlockSpec((TM, TN), col), pl.BlockSpec((1, TN), lambda s, j, i: (0, s * nj + j))]
        out_specs.append(pl.BlockSpec((TM, TN), col))
        out_shape.append(jax.ShapeDtypeStruct((M, S * Ns), F32))
        args += [res, gate]
    out = pl.pallas_call(body, name=name, grid=(S, nj, M // TM), in_specs=in_specs, out_specs=out_specs,
                         out_shape=out_shape, compiler_params=_cp("parallel", "parallel", "parallel"))(*args)
    return tuple(out) if fused else out[0]


def mm_nt(g, w, out_dtype, name):
    g3 = g if g.ndim == 3 else g[None]
    Q, M, F = g3.shape
    S, K, Ns = w.shape
    TM = _tile(M, (1024, 512, 256) if K <= 1024 else (512, 256))
    TN = _tile(Ns, (1408, 1024, 768, 512, 256, 128))
    nj = Ns // TN
    nred = S * nj
    per_part = F // TN

    def body(g_ref, w_ref, o_ref, acc):
        n = pl.program_id(1)

        @pl.when(n == 0)
        def _():
            acc[...] = jnp.zeros_like(acc)

        acc[...] += lax.dot_general(g_ref[...], w_ref[...], (((1,), (1,)), ((), ())), preferred_element_type=F32)

        @pl.when(n == nred - 1)
        def _():
            o_ref[...] = acc[...].astype(o_ref.dtype)

    return pl.pallas_call(
        body, name=name, grid=(M // TM, nred),
        in_specs=[pl.BlockSpec((None, TM, TN), lambda i, n: (n // per_part, i, n % per_part)),
                  pl.BlockSpec((None, K, TN), lambda i, n: (n // nj, 0, n % nj))],
        out_specs=pl.BlockSpec((TM, K), lambda i, n: (i, 0)),
        out_shape=jax.ShapeDtypeStruct((M, K), out_dtype),
        scratch_shapes=[pltpu.VMEM((TM, K), F32)],
        compiler_params=_cp("parallel", "arbitrary"))(g3, w)


def mm_tn(a, g, S, name):
    M, K = a.shape
    g3 = g if g.ndim == 3 else g[None]
    Q, _, F = g3.shape
    Ns = Q * F // S
    TK = _tile(K, (256, 128))
    TN = _tile(Ns, (1408, 1024, 768, 512, 256, 128))
    nj = Ns // TN
    per_part = F // TN

    def body(a_ref, g_ref, o_ref):
        o_ref[...] = lax.dot_general(a_ref[...], g_ref[...], (((0,), (0,)), ((), ())),
                                     preferred_element_type=F32).astype(o_ref.dtype)

    return pl.pallas_call(
        body, name=name, grid=(S * nj, K // TK),
        in_specs=[pl.BlockSpec((M, TK), lambda n, k: (0, k)),
                  pl.BlockSpec((None, M, TN), lambda n, k: (n // per_part, 0, n % per_part))],
        out_specs=pl.BlockSpec((None, TK, TN), lambda n, k: (n // nj, k, n % nj)),
        out_shape=jax.ShapeDtypeStruct((S, K, Ns), BF),
        compiler_params=_cp("parallel", "parallel"))(a, g3)


ROW_TILE = (512, 256)


def _rows(TL, D):
    return pl.BlockSpec((TL, D), lambda i: (i, 0))


def _fixed(R, D):
    return pl.BlockSpec((R, D), lambda i: (0, 0))


def _rowsum8(v):
    T, D = v.shape
    return jnp.sum(v.reshape(T // 8, 8, D), axis=0)


def _norm_parts(xv):
    r = lax.rsqrt(jnp.mean(xv * xv, axis=-1, keepdims=True) + RMS_EPS)
    return xv * r, r


def norm_mod(x, gamma, mods, k_shift, out_dtype, name):
    L, D = x.shape
    TL = _tile(L, ROW_TILE)

    def body(x_ref, g_ref, m_ref, o_ref):
        xn, _ = _norm_parts(x_ref[...])
        sh, sc = m_ref[k_shift:k_shift + 1, :], m_ref[k_shift + 1:k_shift + 2, :]
        o_ref[...] = ((xn * g_ref[...]) * (1.0 + sc) + sh).astype(o_ref.dtype)

    return pl.pallas_call(body, name=name, grid=(L // TL,),
                          in_specs=[_rows(TL, D), _fixed(1, D), _fixed(6, D)], out_specs=_rows(TL, D),
                          out_shape=jax.ShapeDtypeStruct((L, D), out_dtype), compiler_params=_cp("parallel"))(x, gamma, mods)


def norm_bwd(dh, x, dres, gamma, mods, k_shift, name, branch=None):
    L, D = x.shape
    TL = _tile(L, ROW_TILE)
    nacc = 4 if branch else 3

    def body(*refs):
        if branch:
            dh_ref, x_ref, dr_ref, g_ref, m_ref, f_ref, fm_ref, dx_ref, s_ref, df_ref, acc = refs
        else:
            dh_ref, x_ref, dr_ref, g_ref, m_ref, dx_ref, s_ref, acc = refs
        i = pl.program_id(0)

        @pl.when(i == 0)
        def _():
            acc[...] = jnp.zeros_like(acc)

        xn, r = _norm_parts(x_ref[...])
        dh_v = dh_ref[...].astype(F32)
        gam = g_ref[...]
        sc = m_ref[k_shift + 1:k_shift + 2, :]
        dn = dh_v * (1.0 + sc)
        dxn = dn * gam
        dx = dr_ref[...] + r * (dxn - xn * jnp.mean(dxn * xn, axis=-1, keepdims=True))
        dx_ref[...] = dx
        acc[0] += _rowsum8(dh_v)
        acc[1] += _rowsum8(dh_v * (xn * gam))
        acc[2] += _rowsum8(dn * xn)
        if branch:
            df_ref[...] = (dx * fm_ref[branch[2]:branch[2] + 1, :]).astype(df_ref.dtype)
            acc[3] += _rowsum8(dx * f_ref[...].astype(F32))

        @pl.when(i == pl.num_programs(0) - 1)
        def _():
            s_ref[...] = jnp.zeros_like(s_ref)
            for q in range(nacc):
                s_ref[q:q + 1, :] = jnp.sum(acc[q], axis=0, keepdims=True)

    in_specs = [_rows(TL, D), _rows(TL, D), _rows(TL, D), _fixed(1, D), _fixed(6, D)]
    out_specs = [_rows(TL, D), _fixed(8, D)]
    out_shape = [jax.ShapeDtypeStruct((L, D), F32), jax.ShapeDtypeStruct((8, D), F32)]
    args = [dh, x, dres, gamma, mods]
    if branch:
        in_specs += [_rows(TL, D), _fixed(6, D)]
        out_specs.append(_rows(TL, D))
        out_shape.append(jax.ShapeDtypeStruct((L, D), BF))
        args += [branch[0], branch[1]]
    return pl.pallas_call(
        body, name=name, grid=(L // TL,), in_specs=in_specs, out_specs=out_specs, out_shape=out_shape,
        scratch_shapes=[pltpu.VMEM((nacc, 8, D), F32)], compiler_params=_cp("arbitrary"))(*args)


def ffn_in_act(a, w, name):
    M, K = a.shape
    S, _, Ns = w.shape
    half = S // 2
    TM = _tile(M, (512, 256))
    TN = _tile(Ns, (1408, 1024, 768, 512, 256, 128))
    nj = Ns // TN

    def body(a_ref, wg_ref, wu_ref, gu_ref, act_ref):
        av = a_ref[...]
        g = jnp.dot(av, wg_ref[...], preferred_element_type=F32)
        u = jnp.dot(av, wu_ref[...], preferred_element_type=F32)
        gu_ref[0] = g.astype(gu_ref.dtype)
        gu_ref[1] = u.astype(gu_ref.dtype)
        act_ref[...] = (g * jax.nn.sigmoid(g) * u).astype(act_ref.dtype)

    return pl.pallas_call(
        body, name=name, grid=(half, nj, M // TM),
        in_specs=[pl.BlockSpec((TM, K), lambda s, j, i: (i, 0)),
                  pl.BlockSpec((None, K, TN), lambda s, j, i: (s, 0, j)),
                  pl.BlockSpec((None, K, TN), lambda s, j, i: (s + half, 0, j))],
        out_specs=[pl.BlockSpec((2, TM, TN), lambda s, j, i: (0, i, s * nj + j)),
                   pl.BlockSpec((TM, TN), lambda s, j, i: (i, s * nj + j))],
        out_shape=[jax.ShapeDtypeStruct((2, M, half * Ns), BF), jax.ShapeDtypeStruct((M, half * Ns), BF)],
        compiler_params=_cp("parallel", "parallel", "parallel"))(a, w, w)


def ffn_out_bwd(dff, w2, gu, name):
    M, D = dff.shape
    F = w2.shape[0]
    TM = _tile(M, (512, 256))
    CW = _tile(F, (256, 128))

    def body(d_ref, w_ref, gu_ref, o_ref):
        dv = d_ref[...]

        def product(c):
            return lax.dot_general(dv, w_ref[c:c + CW, :], (((1,), (1,)), ((), ())), preferred_element_type=F32)

        da = product(0)
        for c in range(0, F, CW):
            ahead = product(c + CW) if c + CW < F else None
            g = gu_ref[0, :, c:c + CW].astype(F32)
            u = gu_ref[1, :, c:c + CW].astype(F32)
            s = jax.nn.sigmoid(g)
            o_ref[0, :, c:c + CW] = (da * u * (s + g * s * (1.0 - s))).astype(o_ref.dtype)
            o_ref[1, :, c:c + CW] = (da * g * s).astype(o_ref.dtype)
            da = ahead

    part = pl.BlockSpec((2, TM, F), lambda i: (0, i, 0))
    return pl.pallas_call(
        body, name=name, grid=(M // TM,),
        in_specs=[pl.BlockSpec((TM, D), lambda i: (i, 0)), pl.BlockSpec((F, D), lambda i: (0, 0)), part],
        out_specs=part, out_shape=jax.ShapeDtypeStruct((2, M, F), BF),
        compiler_params=_cp("parallel"))(dff, w2, gu)


def ssm_out_glu(z, w, x, mods, k_gate, name):
    M, K = z.shape
    S, _, Ns = w.shape
    half = S // 2
    TM = _tile(M, (1024, 512, 256))
    TN = _tile(Ns, (512, 256, 128))
    nj = Ns // TN

    def body(z_ref, wv_ref, wg_ref, x_ref, m_ref, o_ref, mix_ref, y_ref):
        zv = z_ref[...]
        CW = _tile(TN, (256, 128))

        def products(c):
            return (jnp.dot(zv, wv_ref[:, c:c + CW], preferred_element_type=F32),
                    jnp.dot(zv, wg_ref[:, c:c + CW], preferred_element_type=F32))

        cur = products(0)
        for c in range(0, TN, CW):
            ahead = products(c + CW) if c + CW < TN else None
            val, gate = cur
            o_ref[0, :, c:c + CW] = val.astype(o_ref.dtype)
            o_ref[1, :, c:c + CW] = gate.astype(o_ref.dtype)
            mix = val * jax.nn.sigmoid(gate)
            mix_ref[:, c:c + CW] = mix.astype(mix_ref.dtype)
            y_ref[:, c:c + CW] = x_ref[:, c:c + CW] + m_ref[k_gate:k_gate + 1, c:c + CW] * mix
            cur = ahead

    col = lambda s, j, i: (i, s * nj + j)
    return pl.pallas_call(
        body, name=name, grid=(half, nj, M // TM),
        in_specs=[pl.BlockSpec((TM, K), lambda s, j, i: (i, 0)),
                  pl.BlockSpec((None, K, TN), lambda s, j, i: (s, 0, j)),
                  pl.BlockSpec((None, K, TN), lambda s, j, i: (s + half, 0, j)),
                  pl.BlockSpec((TM, TN), col), pl.BlockSpec((6, TN), lambda s, j, i: (0, s * nj + j))],
        out_specs=[pl.BlockSpec((2, TM, TN), lambda s, j, i: (0, i, s * nj + j)), pl.BlockSpec((TM, TN), col),
                   pl.BlockSpec((TM, TN), col)],
        out_shape=[jax.ShapeDtypeStruct((2, M, half * Ns), BF), jax.ShapeDtypeStruct((M, half * Ns), BF),
                   jax.ShapeDtypeStruct((M, half * Ns), F32)],
        compiler_params=_cp("parallel", "parallel", "parallel"))(z, w, w, x, mods)


def glu_bwd(dmix, o, name):
    _, L, D = o.shape
    TL = _tile(L, ROW_TILE)

    def body(d_ref, o_ref, do_ref):
        d = d_ref[...].astype(F32)
        val = o_ref[0].astype(F32)
        s = jax.nn.sigmoid(o_ref[1].astype(F32))
        do_ref[0] = (d * s).astype(do_ref.dtype)
        do_ref[1] = (d * val * s * (1.0 - s)).astype(do_ref.dtype)

    part = pl.BlockSpec((2, TL, D), lambda i: (0, i, 0))
    return pl.pallas_call(body, name=name, grid=(L // TL,), in_specs=[_rows(TL, D), part],
                          out_specs=part, out_shape=jax.ShapeDtypeStruct((2, L, D), BF),
                          compiler_params=_cp("parallel"))(dmix, o)


def final_loss(x, target, gamma, f, fmods, k_gate, name):
    L, D = x.shape
    TL = _tile(L, ROW_TILE)

    def body(x_ref, t_ref, g_ref, f_ref, fm_ref, l_ref, dx_ref, s_ref, df_ref, acc, lacc):
        i = pl.program_id(0)

        @pl.when(i == 0)
        def _():
            acc[...] = jnp.zeros_like(acc)
            lacc[...] = jnp.zeros_like(lacc)

        xn, r = _norm_parts(x_ref[...])
        gam = g_ref[...]
        e = xn * gam - t_ref[...]
        lacc[...] += jnp.sum(0.5 * jnp.mean(e * e, axis=-1, keepdims=True), axis=0, keepdims=True)
        dy = e * (1.0 / D)
        dxn = dy * gam
        dx = r * (dxn - xn * jnp.mean(dxn * xn, axis=-1, keepdims=True))
        dx_ref[...] = dx
        df_ref[...] = (dx * fm_ref[k_gate:k_gate + 1, :]).astype(df_ref.dtype)
        acc[0] += _rowsum8(dy * xn)
        acc[1] += _rowsum8(dx * f_ref[...].astype(F32))

        @pl.when(i == pl.num_programs(0) - 1)
        def _():
            s_ref[...] = jnp.zeros_like(s_ref)
            for q in range(2):
                s_ref[q:q + 1, :] = jnp.sum(acc[q], axis=0, keepdims=True)
            l_ref[...] = jnp.broadcast_to(lacc[...], l_ref.shape)

    return pl.pallas_call(
        body, name=name, grid=(L // TL,),
        in_specs=[_rows(TL, D), _rows(TL, D), _fixed(1, D), _rows(TL, D), _fixed(6, D)],
        out_specs=[_fixed(8, 128), _rows(TL, D), _fixed(8, D), _rows(TL, D)],
        out_shape=[jax.ShapeDtypeStruct((8, 128), F32), jax.ShapeDtypeStruct((L, D), F32),
                   jax.ShapeDtypeStruct((8, D), F32), jax.ShapeDtypeStruct((L, D), BF)],
        scratch_shapes=[pltpu.VMEM((2, 8, D), F32), pltpu.VMEM((1, 1), F32)],
        compiler_params=_cp("arbitrary"))(x, target, gamma, f, fmods)


def _col(L, TC, off):
    return pl.BlockSpec((L, TC), lambda j: (0, off + j))


def _shift_down(v, k, row):
    return jnp.where(row >= k, pltpu.roll(v, k, 0), 0.0)


def _shift_up(v, k, row, L):
    return jnp.where(row < L - k, pltpu.roll(v, L - k, 0), 0.0)


def conv_fwd(p, w, name):
    L, D3 = p.shape
    D = D3 // 3
    TC = _tile(D, (128,))
    nc = D // TC

    def body(b_ref, c_ref, v_ref, w_ref, o_ref):
        row = lax.broadcasted_iota(jnp.int32, (L, TC), 0)
        cv = c_ref[...].astype(F32) * v_ref[...].astype(F32)
        conv = w_ref[2:3, :] * cv + w_ref[1:2, :] * _shift_down(cv, 1, row) + w_ref[0:1, :] * _shift_down(cv, 2, row)
        o_ref[...] = (b_ref[...].astype(F32) * conv).astype(o_ref.dtype)

    return pl.pallas_call(
        body, name=name, grid=(nc,),
        in_specs=[_col(L, TC, 0), _col(L, TC, nc), _col(L, TC, 2 * nc), pl.BlockSpec((3, TC), lambda j: (0, j))],
        out_specs=_col(L, TC, 0), out_shape=jax.ShapeDtypeStruct((L, D), BF), compiler_params=_cp("parallel"))(p, p, p, w)


def conv_bwd(dm, p, w, name):
    L, D3 = p.shape
    D = D3 // 3
    TC = _tile(D, (128,))
    nc = D // TC

    def body(dm_ref, b_ref, c_ref, v_ref, w_ref, db_ref, dc_ref, dv_ref, dw_ref):
        row = lax.broadcasted_iota(jnp.int32, (L, TC), 0)
        cg, vv = c_ref[...].astype(F32), v_ref[...].astype(F32)
        cv = cg * vv
        cv1, cv2 = _shift_down(cv, 1, row), _shift_down(cv, 2, row)
        conv = w_ref[2:3, :] * cv + w_ref[1:2, :] * cv1 + w_ref[0:1, :] * cv2
        dmv = dm_ref[...].astype(F32)
        db_ref[...] = (dmv * conv).astype(db_ref.dtype)
        dconv = dmv * b_ref[...].astype(F32)
        dcv = (w_ref[2:3, :] * dconv + w_ref[1:2, :] * _shift_up(dconv, 1, row, L)
               + w_ref[0:1, :] * _shift_up(dconv, 2, row, L))
        dc_ref[...] = (dcv * vv).astype(dc_ref.dtype)
        dv_ref[...] = (dcv * cg).astype(dv_ref.dtype)
        dw_ref[...] = jnp.zeros_like(dw_ref)
        dw_ref[0:1, :] = jnp.sum(dconv * cv2, axis=0, keepdims=True)
        dw_ref[1:2, :] = jnp.sum(dconv * cv1, axis=0, keepdims=True)
        dw_ref[2:3, :] = jnp.sum(dconv * cv, axis=0, keepdims=True)

    one = jax.ShapeDtypeStruct((L, D), BF)
    return pl.pallas_call(
        body, name=name, grid=(nc,),
        in_specs=[_col(L, TC, 0), _col(L, TC, 0), _col(L, TC, nc), _col(L, TC, 2 * nc),
                  pl.BlockSpec((3, TC), lambda j: (0, j))],
        out_specs=[_col(L, TC, 0), _col(L, TC, 0), _col(L, TC, 0), pl.BlockSpec((8, TC), lambda j: (0, j))],
        out_shape=[one, one, one, jax.ShapeDtypeStruct((8, D), F32)],
        compiler_params=_cp("parallel"))(dm, p, p, p, w)


def _gelu(y):
    return 0.5 * y * (1.0 + jnp.tanh(GELU_C * (y + GELU_A * y * y * y)))


def _gelu_grad(y):
    th = jnp.tanh(GELU_C * (y + GELU_A * y * y * y))
    return 0.5 * (1.0 + th) + 0.5 * y * (1.0 - th * th) * GELU_C * (1.0 + 3.0 * GELU_A * y * y)


def _cmul_add(br, bi, ar, ai, sr, si):
    return br + ar * sr - ai * si, bi + ar * si + ai * sr


def _log2(n):
    k = n.bit_length() - 1
    assert 1 << k == n
    return k


def _replicate(P2, W2, P, GLP, transposed):
    shape = (W2, P2) if transposed else (P2, W2)
    k = lax.broadcasted_iota(jnp.int32, shape, 1 if transposed else 0)
    c = lax.broadcasted_iota(jnp.int32, shape, 0 if transposed else 1)
    return ((k >> _log2(P)) == (c >> _log2(GLP))) & ((k & (P - 1)) == (c & (P - 1)))


def _on_diagonal(KB, W2, H, P, GLP, transposed):
    shape = (W2, KB) if transposed else (KB, W2)
    r = lax.broadcasted_iota(jnp.int32, shape, 1 if transposed else 0)
    c = lax.broadcasted_iota(jnp.int32, shape, 0 if transposed else 1)
    return (r >> _log2(H)) == ((c & (GLP - 1)) >> _log2(P))


def _expand(t, dims, transposed):
    KB, W2, H, P, GLP = dims
    rep = _replicate(2 * P, W2, P, GLP, transposed).astype(t.dtype)
    wide = jnp.dot(rep, t, preferred_element_type=F32) if transposed else jnp.dot(t, rep, preferred_element_type=F32)
    return jnp.where(_on_diagonal(KB, W2, H, P, GLP, transposed), wide, 0.0).astype(t.dtype)


def _extract(acc, dims):
    KB, W2, H, P, GLP = dims
    rep = _replicate(2 * P, W2, P, GLP, True).astype(BF)
    kept = jnp.where(_on_diagonal(KB, W2, H, P, GLP, False), acc, 0.0)
    hi = kept.astype(BF)
    lo = (kept - hi.astype(F32)).astype(BF)
    return jnp.dot(hi, rep, preferred_element_type=F32) + jnp.dot(lo, rep, preferred_element_type=F32)


def _cmul(ar, ai, sr, si):
    return ar * sr - ai * si, ar * si + ai * sr


def _chunk_order(TL, CH, transposed):
    out_row = lax.broadcasted_iota(jnp.int32, (TL, TL), 1 if transposed else 0)
    in_row = lax.broadcasted_iota(jnp.int32, (TL, TL), 0 if transposed else 1)
    return in_row == ((out_row & 7) << _log2(CH)) + (out_row >> 3)


def _reorder(perm, v):
    hi = v.astype(perm.dtype)
    lo = (v - hi.astype(F32)).astype(perm.dtype)
    return jnp.dot(perm, hi, preferred_element_type=F32) + jnp.dot(perm, lo, preferred_element_type=F32)


def _interleave(main, side):
    n, m, k = len(main), len(side), 0
    for i, step in enumerate(main):
        step()
        while k < m and (k + 1) * n <= (i + 1) * m:
            side[k]()
            k += 1
    for step in side[k:]:
        step()


S5_CHUNK = 512


def s5_fwd(h, tb, tct, pw, dvec, name):
    L, D = h.shape
    nkb, KB, P2 = tb.shape
    P = P2 // 2
    W = (KB // SSM_GROUP) * P
    W2 = 2 * W
    dims = (KB, W2, SSM_GROUP, P, W)
    TL = _tile(L, (512, 256))
    CH = TL // 8
    NB = 2 if nkb % 2 == 0 else 1
    CK = min(S5_CHUNK, W2)

    def body(h_ref, tb_ref, tct_ref, pw_ref, d_ref, s_ref, y_ref, z_ref, bw, cw, perm, unperm, carry):
        t = pl.program_id(1)

        @pl.when(t == 0)
        def _():
            carry[...] = jnp.zeros_like(carry)
            for b in range(NB):
                bw[b] = _expand(tb_ref[b], dims, False)
                cw[b] = _expand(tct_ref[b], dims, True)
            perm[...] = _chunk_order(TL, CH, False).astype(perm.dtype)
            unperm[...] = _chunk_order(TL, CH, True).astype(perm.dtype)

        hp = _reorder(perm[...], h_ref[...])
        hpb = hp.astype(BF)
        first = lax.broadcasted_iota(jnp.int32, (8, W), 0) == 0

        def project(b):
            def chunk(c):
                def emit():
                    s_ref[:, b * W2 + c:b * W2 + c + CK] = jnp.dot(hpb[:, b * KB:(b + 1) * KB], bw[b, :, c:c + CK],
                                                                   preferred_element_type=F32)
                return emit
            return [chunk(c) for c in range(0, W2, CK)]

        def scan(b):
            re, im = slice(b * W2, b * W2 + W), slice(b * W2 + W, (b + 1) * W2)
            ar, ai = pw_ref[b, 0:8, :W], pw_ref[b, 0:8, W:]
            st = {"x": (jnp.zeros((8, W), F32), jnp.zeros((8, W), F32))}

            def own(j):
                def emit():
                    rows = slice(j * 8, j * 8 + 8)
                    xr, xi = _cmul_add(s_ref[rows, re], s_ref[rows, im], ar, ai, *st["x"])
                    s_ref[rows, re] = xr
                    s_ref[rows, im] = xi
                    st["x"] = (xr, xi)
                return emit

            def ends():
                xr, xi = st["x"]
                for k, off in ((1, 8), (2, 16), (4, 24)):
                    xr, xi = _cmul_add(xr, xi, pw_ref[b, off:off + 8, :W], pw_ref[b, off:off + 8, W:],
                                       pltpu.roll(xr, k, 0), pltpu.roll(xi, k, 0))
                xr, xi = _cmul_add(xr, xi, pw_ref[b, 32:40, :W], pw_ref[b, 32:40, W:], carry[b, 0], carry[b, 1])
                st["c"] = (jnp.where(first, carry[b, 0], pltpu.roll(xr, 1, 0)),
                           jnp.where(first, carry[b, 1], pltpu.roll(xi, 1, 0)))
                carry[b, 0] = jnp.broadcast_to(xr[7:8], (8, W))
                carry[b, 1] = jnp.broadcast_to(xi[7:8], (8, W))

            def carried(j):
                def emit():
                    rows = slice(j * 8, j * 8 + 8)
                    cr, ci = _cmul(ar, ai, *st["c"])
                    s_ref[rows, re] = s_ref[rows, re] + cr
                    s_ref[rows, im] = s_ref[rows, im] + ci
                    st["c"] = (cr, ci)
                return emit

            return [own(j) for j in range(CH)] + [ends] + [carried(j) for j in range(CH)]

        def readout(b):
            cols = slice(b * KB, (b + 1) * KB)
            acc = {}

            def chunk(c):
                def emit():
                    part = jnp.dot(s_ref[:, b * W2 + c:b * W2 + c + CK].astype(BF), cw[b, c:c + CK, :],
                                   preferred_element_type=F32)
                    acc["y"] = part if c == 0 else acc["y"] + part
                return emit

            def finish():
                y = acc["y"] + d_ref[:, cols] * hp[:, cols]
                y_ref[:, cols] = y
                z_ref[:, cols] = jnp.dot(unperm[...], _gelu(y).astype(BF),
                                         preferred_element_type=F32).astype(z_ref.dtype)

            return [chunk(c) for c in range(0, W2, CK)] + [finish]

        for emit in project(0):
            emit()
        for b in range(NB):
            side = (project(b + 1) if b + 1 < NB else []) + (readout(b - 1) if b > 0 else [])
            _interleave(scan(b), side)
        for emit in readout(NB - 1):
            emit()

    blk = lambda kb, t: (t, kb)
    per_kb = lambda kb, t: (kb, 0, 0)
    return pl.pallas_call(
        body, name=name, grid=(nkb // NB, L // TL),
        in_specs=[pl.BlockSpec((TL, NB * KB), blk), pl.BlockSpec((NB, KB, P2), per_kb),
                  pl.BlockSpec((NB, P2, KB), per_kb), pl.BlockSpec((NB, 40, W2), per_kb),
                  pl.BlockSpec((1, NB * KB), lambda kb, t: (0, kb))],
        out_specs=[pl.BlockSpec((TL, NB * W2), blk), pl.BlockSpec((TL, NB * KB), blk),
                   pl.BlockSpec((TL, NB * KB), blk)],
        out_shape=[jax.ShapeDtypeStruct((L, nkb * W2), F32), jax.ShapeDtypeStruct((L, D), F32),
                   jax.ShapeDtypeStruct((L, D), BF)],
        scratch_shapes=[pltpu.VMEM((NB, KB, W2), BF), pltpu.VMEM((NB, W2, KB), BF), pltpu.VMEM((TL, TL), BF),
                        pltpu.VMEM((TL, TL), BF), pltpu.VMEM((NB, 2, 8, W), F32)],
        compiler_params=_cp("parallel", "arbitrary"))(h, tb, tct, pw, dvec)


def s5_bwd(dz, y, h, s, tc, tbt, pwr, dvec, name):
    L, D = h.shape
    nkb, KB, P2 = tc.shape
    P = P2 // 2
    W = (KB // SSM_GROUP) * P
    W2 = 2 * W
    dims = (KB, W2, SSM_GROUP, P, W)
    TL = _tile(L, (512, 256))
    CH = TL // 8
    nt = L // TL
    NB = 2 if nkb % 2 == 0 else 1
    CK = min(S5_CHUNK, W2)
    tn = (((0,), (0,)), ((), ()))

    def body(dz_ref, y_ref, h_ref, s_ref, sp_ref, tc_ref, tbt_ref, pw_ref, d_ref,
             dh_ref, dd_ref, da_ref, db_ref, dc_ref, g, ctw, btw, dbacc, dcacc, dys, perm, unperm, carry):
        t = pl.program_id(1)

        @pl.when(t == 0)
        def _():
            carry[...] = jnp.zeros_like(carry)
            dd_ref[...] = jnp.zeros_like(dd_ref)
            da_ref[...] = jnp.zeros_like(da_ref)
            dbacc[...] = jnp.zeros_like(dbacc)
            dcacc[...] = jnp.zeros_like(dcacc)
            for b in range(NB):
                ctw[b] = _expand(tc_ref[b], dims, False)
                btw[b] = _expand(tbt_ref[b], dims, True)
            perm[...] = _chunk_order(TL, CH, False).astype(perm.dtype)
            unperm[...] = _chunk_order(TL, CH, True).astype(perm.dtype)

        hp = jnp.dot(perm[...], h_ref[...].astype(BF), preferred_element_type=F32)
        dy = jnp.dot(perm[...], dz_ref[...].astype(BF), preferred_element_type=F32) * _gelu_grad(y_ref[...])
        dd_ref[...] += _rowsum8(dy * hp)
        dys[...] = dy
        dyb = dy.astype(BF)
        hpb = hp.astype(BF)
        sub = lax.broadcasted_iota(jnp.int32, (8, W), 0)
        live = jnp.where(t == nt - 1, 0.0, 1.0)

        def lead(b):
            cols = slice(b * KB, (b + 1) * KB)

            def to_states(c):
                def emit():
                    g[b, :, c:c + CK] = jnp.dot(dyb[:, cols], ctw[b, :, c:c + CK], preferred_element_type=F32)
                return emit

            def d_c(c):
                def emit():
                    dcacc[b, :, c:c + CK] += lax.dot_general(dyb[:, cols],
                                                             s_ref[:, b * W2 + c:b * W2 + c + CK].astype(BF), tn,
                                                             preferred_element_type=F32)
                return emit

            return [f(c) for c in range(0, W2, CK) for f in (to_states, d_c)]

        def scan(b):
            re, im = slice(b * W2, b * W2 + W), slice(b * W2 + W, (b + 1) * W2)
            ar, ai = pw_ref[b, 0:8, :W], pw_ref[b, 0:8, W:]
            zero = jnp.zeros((8, W), F32)
            st = {"g": (zero, zero), "acc": (zero, zero)}

            def own(j):
                def emit():
                    rows = slice(j * 8, j * 8 + 8)
                    gr, gi = _cmul_add(g[b, rows, :W], g[b, rows, W:], ar, ai, *st["g"])
                    g[b, rows, :W] = gr
                    g[b, rows, W:] = gi
                    st["g"] = (gr, gi)
                return emit

            def ends():
                gr, gi = st["g"]
                for k, off in ((1, 8), (2, 16), (4, 24)):
                    gr, gi = _cmul_add(gr, gi, pw_ref[b, off:off + 8, :W], pw_ref[b, off:off + 8, W:],
                                       pltpu.roll(gr, 8 - k, 0), pltpu.roll(gi, 8 - k, 0))
                gr, gi = _cmul_add(gr, gi, pw_ref[b, 32:40, :W], pw_ref[b, 32:40, W:], carry[b, 0], carry[b, 1])
                st["c"] = (jnp.where(sub == 7, carry[b, 0], pltpu.roll(gr, 7, 0)),
                           jnp.where(sub == 7, carry[b, 1], pltpu.roll(gi, 7, 0)))
                carry[b, 0] = jnp.broadcast_to(gr[0:1], (8, W))
                carry[b, 1] = jnp.broadcast_to(gi[0:1], (8, W))

            def carried(j):
                def emit():
                    rows = slice(j * 8, j * 8 + 8)
                    cr, ci = _cmul(ar, ai, *st["c"])
                    gr, gi = g[b, rows, :W] + cr, g[b, rows, W:] + ci
                    g[b, rows, :W] = gr
                    g[b, rows, W:] = gi
                    if j > 0:
                        before = slice(j * 8 - 8, j * 8)
                        pr, pi = s_ref[before, re], s_ref[before, im]
                    else:
                        last = slice(TL - 8, TL)
                        pr = jnp.where(sub == 0, sp_ref[7:8, re] * live, pltpu.roll(s_ref[last, re], 1, 0))
                        pi = jnp.where(sub == 0, sp_ref[7:8, im] * live, pltpu.roll(s_ref[last, im], 1, 0))
                    accr, acci = st["acc"]
                    st["c"] = (cr, ci)
                    st["acc"] = (accr + pr * gr + pi * gi, acci + pr * gi - pi * gr)
                return emit

            def done():
                da_ref[b, :, :W] += st["acc"][0]
                da_ref[b, :, W:] += st["acc"][1]

            return ([own(j) for j in reversed(range(CH))] + [ends] + [carried(j) for j in reversed(range(CH))]
                    + [done])

        def tail(b):
            cols = slice(b * KB, (b + 1) * KB)
            acc = {}

            def d_u(c):
                def emit():
                    part = jnp.dot(g[b, :, c:c + CK].astype(BF), btw[b, c:c + CK, :], preferred_element_type=F32)
                    acc["u"] = part if c == 0 else acc["u"] + part
                return emit

            def d_b(c):
                def emit():
                    dbacc[b, :, c:c + CK] += lax.dot_general(hpb[:, cols], g[b, :, c:c + CK].astype(BF), tn,
                                                             preferred_element_type=F32)
                return emit

            def finish():
                dh = (dys[:, cols] * d_ref[:, cols] + acc["u"]).astype(BF)
                dh_ref[:, cols] = jnp.dot(unperm[...], dh, preferred_element_type=F32).astype(dh_ref.dtype)

            return [f(c) for c in range(0, W2, CK) for f in (d_u, d_b)] + [finish]

        for emit in lead(0):
            emit()
        for b in range(NB):
            side = (lead(b + 1) if b + 1 < NB else []) + (tail(b - 1) if b > 0 else [])
            _interleave(scan(b), side)
        for emit in tail(NB - 1):
            emit()

        @pl.when(t == nt - 1)
        def _():
            for b in range(NB):
                db_ref[b] = _extract(dbacc[b], dims)
                dc_ref[b] = _extract(dcacc[b], dims)

    rev = lambda kb, t: (nt - 1 - t, kb)
    prev = lambda kb, t: (jnp.maximum((nt - 1 - t) * CH - 1, 0), kb)
    per_kb = lambda kb, t: (kb, 0, 0)
    return pl.pallas_call(
        body, name=name, grid=(nkb // NB, nt),
        in_specs=[pl.BlockSpec((TL, NB * KB), rev), pl.BlockSpec((TL, NB * KB), rev),
                  pl.BlockSpec((TL, NB * KB), rev), pl.BlockSpec((TL, NB * W2), rev),
                  pl.BlockSpec((8, NB * W2), prev), pl.BlockSpec((NB, KB, P2), per_kb),
                  pl.BlockSpec((NB, P2, KB), per_kb), pl.BlockSpec((NB, 40, W2), per_kb),
                  pl.BlockSpec((1, NB * KB), lambda kb, t: (0, kb))],
        out_specs=[pl.BlockSpec((TL, NB * KB), rev), pl.BlockSpec((8, NB * KB), lambda kb, t: (0, kb)),
                   pl.BlockSpec((NB, 8, W2), per_kb), pl.BlockSpec((NB, KB, P2), per_kb),
                   pl.BlockSpec((NB, KB, P2), per_kb)],
        out_shape=[jax.ShapeDtypeStruct((L, D), BF), jax.ShapeDtypeStruct((8, D), F32),
                   jax.ShapeDtypeStruct((nkb, 8, W2), F32), jax.ShapeDtypeStruct((nkb, KB, P2), F32),
                   jax.ShapeDtypeStruct((nkb, KB, P2), F32)],
        scratch_shapes=[pltpu.VMEM((NB, TL, W2), F32), pltpu.VMEM((NB, KB, W2), BF), pltpu.VMEM((NB, W2, KB), BF),
                        pltpu.VMEM((NB, KB, W2), F32), pltpu.VMEM((NB, KB, W2), F32), pltpu.VMEM((TL, NB * KB), F32),
                        pltpu.VMEM((TL, TL), BF), pltpu.VMEM((TL, TL), BF), pltpu.VMEM((NB, 2, 8, W), F32)],
        compiler_params=pltpu.CompilerParams(dimension_semantics=("parallel", "arbitrary"),
                                             vmem_limit_bytes=V7X_VMEM_BYTES - 4 * 1024 * 1024),
    )(dz, y, h, s, s, tc, tbt, pwr, dvec)


def _discretise(a_re, a_im, log_step, b_re, b_im):
    lr = jnp.minimum(a_re, -1e-4)
    li = a_im
    dt = jnp.exp(log_step)[:, None]
    mag = jnp.exp(lr * dt)
    abr = mag * jnp.cos(li * dt)
    abi = mag * jnp.sin(li * dt)
    den = lr * lr + li * li
    qr = ((abr - 1.0) * lr + abi * li) / den
    qi = (abi * lr - (abr - 1.0) * li) / den
    bbar_re = qr[..., None] * b_re - qi[..., None] * b_im
    bbar_im = qr[..., None] * b_im + qi[..., None] * b_re
    return abr, abi, bbar_re, bbar_im


def _compact(m_re, m_im, nkb):
    G, H, P = m_re.shape
    t = jnp.stack([m_re, m_im], axis=2).reshape(nkb, (G // nkb) * H, 2 * P).astype(BF)
    return t, jnp.swapaxes(t, 1, 2)


def _scan_powers(abr, abi, nkb, conj, CH):
    G, P = abr.shape
    if conj:
        abi = -abi

    def cmul(u, v):
        return u[0] * v[0] - u[1] * v[1], u[0] * v[1] + u[1] * v[0]

    q = (abr, abi)
    for _ in range(_log2(CH)):
        q = cmul(q, q)
    pows = [q]
    for _ in range(7):
        pows.append(cmul(pows[-1], q))
    row = jnp.arange(8)[:, None, None]

    def table(part):
        out = [jnp.broadcast_to((abr, abi)[part][None], (8, G, P))]
        for k in (1, 2, 4):
            keep = (row <= 7 - k) if conj else (row >= k)
            out.append(jnp.where(keep, pows[k - 1][part][None], 0.0))
        ends = jnp.stack([p[part] for p in pows])
        out.append(ends[::-1] if conj else ends)
        return jnp.concatenate(out, axis=0)

    GL = G // nkb
    t = jnp.stack([table(0), table(1)], axis=1)
    t = t.reshape(40, 2, nkb, GL * P).transpose(2, 0, 1, 3)
    return t.reshape(nkb, 40, 2 * GL * P)


def ada_mods(c_all, w_ada, b_sh, name):
    nl, D, NA = w_ada.shape

    def body(c_ref, w_ref, b_ref, o_ref):
        cv = c_ref[...]
        act = cv * jax.nn.sigmoid(cv)
        o_ref[...] = jnp.dot(act, w_ref[...], preferred_element_type=F32, precision=lax.Precision.HIGHEST) + b_ref[...]

    return pl.pallas_call(
        body, name=name, grid=(nl,),
        in_specs=[pl.BlockSpec((8, D), lambda i: (0, 0)), pl.BlockSpec((None, D, NA), lambda i: (i, 0, 0)),
                  pl.BlockSpec((None, 1, NA), lambda i: (i, 0, 0))],
        out_specs=pl.BlockSpec((None, 8, NA), lambda i: (i, 0, 0)),
        out_shape=jax.ShapeDtypeStruct((nl, 8, NA), F32), compiler_params=_cp("parallel"))(c_all, w_ada, b_sh)


def _adamw(w, g, m, v):
    m = ADAM_B1 * m + (1.0 - ADAM_B1) * g
    v = ADAM_B2 * v + (1.0 - ADAM_B2) * (g * g)
    m_hat = m / (1.0 - ADAM_B1 ** ADAM_STEP)
    v_hat = v / (1.0 - ADAM_B2 ** ADAM_STEP)
    return -ADAM_LR * (m_hat / (jnp.sqrt(v_hat) + ADAM_EPS) + ADAM_WD * w), m, v


def _adam_rows(R, C):
    cap = max(8, (256 * 1024) // C)
    for t in range(min(R, cap), 0, -1):
        if R % t == 0 and (t % 8 == 0 or t == R):
            return t
    return R


def adamw_ada(c_t, dm, w, m, v, name):
    nl, D, NA = w.shape
    TK = _tile(D, (256, 128))

    def body(c_ref, dm_ref, w_ref, m_ref, v_ref, g_ref, d_ref, nm_ref, nv_ref):
        cv = c_ref[...]
        act = cv * jax.nn.sigmoid(cv)
        g = jnp.dot(act, dm_ref[...], preferred_element_type=F32, precision=lax.Precision.HIGHEST)
        g_ref[...] = g
        d_ref[...], nm_ref[...], nv_ref[...] = _adamw(w_ref[...], g, m_ref[...], v_ref[...])

    big = pl.BlockSpec((None, TK, NA), lambda i, k: (i, k, 0))
    shape = jax.ShapeDtypeStruct(w.shape, F32)
    return pl.pallas_call(
        body, name=name, grid=(nl, D // TK),
        in_specs=[pl.BlockSpec((TK, 8), lambda i, k: (k, 0)), pl.BlockSpec((None, 8, NA), lambda i, k: (i, 0, 0)),
                  big, big, big],
        out_specs=[big] * 4, out_shape=[shape] * 4, compiler_params=_cp("parallel", "parallel"))(c_t, dm, w, m, v)


def adamw_sharded(w, m, v, ga, gb, name):
    nl, R, C = w.shape
    TR = _adam_rows(R, C)

    def body(w_ref, m_ref, v_ref, a_ref, b_ref, g_ref, d_ref, nm_ref, nv_ref):
        g = a_ref[...] + b_ref[...]
        g_ref[...] = g
        d_ref[...], nm_ref[...], nv_ref[...] = _adamw(w_ref[...], g, m_ref[...], v_ref[...])

    big = pl.BlockSpec((None, TR, C), lambda i, r: (i, r, 0))
    shape = jax.ShapeDtypeStruct(w.shape, F32)
    return pl.pallas_call(
        body, name=name, grid=(nl, R // TR), in_specs=[big] * 5,
        out_specs=[big] * 4, out_shape=[shape] * 4, compiler_params=_cp("parallel", "parallel"))(w, m, v, ga, gb)


def adamw_slab(g, w, m, v, name):
    R, C = g.shape
    TR = _tile(R, (160, 80, 40, 8))

    def body(g_ref, w_ref, m_ref, v_ref, d_ref, nm_ref, nv_ref):
        d_ref[...], nm_ref[...], nv_ref[...] = _adamw(w_ref[...], g_ref[...], m_ref[...], v_ref[...])

    big = pl.BlockSpec((TR, C), lambda r: (r, 0))
    shape = jax.ShapeDtypeStruct((R, C), F32)
    return pl.pallas_call(
        body, name=name, grid=(R // TR,), in_specs=[big] * 4,
        out_specs=[big] * 3, out_shape=[shape] * 3, compiler_params=_cp("parallel"))(g, w, m, v)


def adamw_plain(w, m, v, g, name):
    def body(w_ref, m_ref, v_ref, g_ref, d_ref, nm_ref, nv_ref):
        d_ref[...], nm_ref[...], nv_ref[...] = _adamw(w_ref[...], g_ref[...], m_ref[...], v_ref[...])

    shape = jax.ShapeDtypeStruct(w.shape, F32)
    return pl.pallas_call(body, name=name, out_shape=[shape] * 3,
                          compiler_params=pltpu.CompilerParams(vmem_limit_bytes=VMEM_LIMIT))(w, m, v, g)


def _slab_rows(a):
    n = a.size
    rows = -(-n // SLAB_W)
    return -(-rows // 8) * 8


def _pack(arrs, pad_rows_to=0):
    out = []
    for a in arrs:
        rows = _slab_rows(a)
        flat = a.reshape(-1).astype(F32)
        flat = jnp.pad(flat, (0, rows * SLAB_W - flat.shape[0]))
        out.append(flat.reshape(rows, SLAB_W))
    total = sum(o.shape[0] for o in out)
    if pad_rows_to and total % pad_rows_to:
        out.append(jnp.zeros((pad_rows_to - total % pad_rows_to, SLAB_W), F32))
    return jnp.concatenate(out, axis=0)


def _unpack(slab, like):
    out, r = [], 0
    for a in like:
        rows = _slab_rows(a)
        out.append(slab[r:r + rows].reshape(-1)[:a.size].reshape(a.shape))
        r += rows
    return out


WEIGHTS = ['norm1_g', 'norm2_g', 'w_ada', 'b_ada', 'ssm_a_re', 'ssm_a_im', 'ssm_log_step', 'ssm_b_re', 'ssm_b_im',
           'ssm_c_re', 'ssm_c_im', 'ssm_d', 'ssm_w_out', 'conv_w_in', 'conv_w', 'conv_w_out', 'w_ffn_in',
           'w_ffn_out', 'final_g']
SLAB = ['norm1_g', 'norm2_g', 'b_ada', 'ssm_a_re', 'ssm_a_im', 'ssm_log_step', 'ssm_b_re', 'ssm_b_im', 'ssm_c_re',
        'ssm_c_im', 'ssm_d', 'final_g']
SHARDED = ['ssm_w_out', 'conv_w_in', 'conv_w_out', 'w_ffn_in', 'w_ffn_out']


def kernel(x, c, norm1_g, norm2_g, w_ada, b_ada, ssm_a_re, ssm_a_im, ssm_log_step, ssm_b_re, ssm_b_im, ssm_c_re, ssm_c_im, ssm_d, ssm_w_out, conv_w_in, conv_w, conv_w_out, w_ffn_in, w_ffn_out, final_g, loss_target, m_norm1_g, m_norm2_g, m_w_ada, m_b_ada, m_ssm_a_re, m_ssm_a_im, m_ssm_log_step, m_ssm_b_re, m_ssm_b_im, m_ssm_c_re, m_ssm_c_im, m_ssm_d, m_ssm_w_out, m_conv_w_in, m_conv_w, m_conv_w_out, m_w_ffn_in, m_w_ffn_out, m_final_g, v_norm1_g, v_norm2_g, v_w_ada, v_b_ada, v_ssm_a_re, v_ssm_a_im, v_ssm_log_step, v_ssm_b_re, v_ssm_b_im, v_ssm_c_re, v_ssm_c_im, v_ssm_d, v_ssm_w_out, v_conv_w_in, v_conv_w, v_conv_w_out, v_w_ffn_in, v_w_ffn_out, v_final_g):
    given = dict(locals())
    W = {n: given[n] for n in WEIGHTS}
    Mo = {n: given["m_" + n] for n in WEIGHTS}
    Vo = {n: given["v_" + n] for n in WEIGHTS}

    xs = x[0]
    tgt = loss_target[0]
    L, D = xs.shape
    nlayer = norm1_g.shape[0]
    NA = w_ada.shape[2]
    G = ssm_a_re.shape[1]
    nkb = D // S5_BLOCK
    ax, ay, ac = _axes()
    me = 4 * ax + 2 * ay + ac
    chip = 2 * ax + ay

    assert D == SLAB_W
    first = gather8(jnp.concatenate([jnp.broadcast_to(c, (8, D)), _pack([conv_w])], axis=0), "gather_c_conv_w")
    c_all = first[:, 0, :]
    b_sh = lax.dynamic_slice_in_dim(b_ada, chip * NA, NA, axis=1)[:, None, :]
    mods_part = ada_mods(c_all, w_ada, b_sh, "ada_mods")
    mg = gather8(mods_part.reshape(nlayer * 8, NA), "gather_mods")
    mg = mg.reshape(N_CHIP, 2, nlayer, 8, NA)[:, 0]
    mods_all = lax.dynamic_index_in_dim(mg, me, axis=2, keepdims=False)
    mods_all = jnp.transpose(mods_all, (1, 0, 2)).reshape(nlayer, 6, D)

    cw_parts = first[:, 8:]
    nconv = conv_w.shape[0]
    cw_full = jnp.stack([_unpack(cw_parts[2 * q], [conv_w])[0] for q in range(N_CHIP)], axis=2)
    cw_full = cw_full.reshape(nconv, 3, D)

    in_flight_w = {}

    def start_weights(i, after):
        names = (["ssm_w_out"] if i % 2 == 0 else ["conv_w_in", "conv_w_out"]) + ["w_ffn_in", "w_ffn_out"]
        shards = [W[n][i if n.startswith("w_ffn") else i // 2].astype(BF) for n in names]
        sems, srcs, lands, tok = gather_start(shards, after, "gather_start%d" % i)
        in_flight_w[i] = (names, sems, srcs, lands)
        return tok

    def relay_weights(i, after):
        names, sems, srcs, lands = in_flight_w[i]
        got = gather_wait(sems, srcs, lands, list(range(len(names))), after, "gather_wait%d" % i)
        rsems, rlands, tok = relay_start(got, after, "relay_start%d" % i)
        in_flight_w[i] = (names, rsems, rlands)
        return tok

    def layer_weights(i, after):
        names, rsems, rlands = in_flight_w[i]
        return dict(zip(names, relay_wait(rsems, rlands, after, "relay_wait%d" % i)))

    token = start_weights(0, cw_full + mods_all[0, 0:3])
    mods_all = mods_all + token[0:1, 0:1]

    s5 = []
    for j in range(ssm_a_re.shape[0]):
        disc, disc_vjp = jax.vjp(_discretise, ssm_a_re[j], ssm_a_im[j], ssm_log_step[j], ssm_b_re[j], ssm_b_im[j])
        abr, abi, bbar_re, bbar_im = disc
        tb, tbt = _compact(jnp.swapaxes(bbar_re, 1, 2), jnp.swapaxes(bbar_im, 1, 2), nkb)
        tc, tct = _compact(ssm_c_re[j], -ssm_c_im[j], nkb)
        chunk = _tile(L, (512, 256)) // 8
        s5.append(dict(vjp=disc_vjp, tb=tb, tbt=tbt, tc=tc, tct=tct, pw=_scan_powers(abr, abi, nkb, False, chunk),
                       pwr=_scan_powers(abr, abi, nkb, True, chunk)))

    saved = []
    xcur = xs
    for i in range(nlayer):
        j = i // 2
        mods = mods_all[i]
        sv = dict(x=xcur)
        if i % 2 == 0:
            h = norm_mod(xcur, norm1_g[i:i + 1], mods, 0, F32, "norm_mod_s5")
            dvec = ssm_d[j:j + 1]
            if i == 0:
                dvec = dvec + start_weights(1, h)[0:1, 0:1]
            states, yv, z = s5_fwd(h, s5[j]["tb"], s5[j]["tct"], s5[j]["pw"], dvec, "s5_fwd")
            if i == 0:
                mods = mods + relay_weights(0, z)[0:1, 0:1]
            full = layer_weights(i, z)
            o, mix, x2 = ssm_out_glu(z, full["ssm_w_out"], xcur, mods, 2, "ssm_out_glu")
            sv.update(h=h, states=states, y=yv, z=z, o=o)
        else:
            h = norm_mod(xcur, norm1_g[i:i + 1], mods, 0, BF, "norm_mod")
            full = layer_weights(i, h)
            p = mm_nn(h, full["conv_w_in"], BF, "mm_conv_in")
            mc = conv_fwd(p, cw_full[j], "conv_fwd")
            mix, x2 = mm_nn(mc, full["conv_w_out"].reshape(1, D, D), BF, "mm_conv_out", res=xcur, gate=mods[2:3])
            sv.update(h=h, p=p, mc=mc)
        h2 = norm_mod(x2, norm2_g[i:i + 1], mods, 3, BF, "norm_mod")
        gu, act = ffn_in_act(h2, full["w_ffn_in"], "ffn_in_act")
        if i + 1 < nlayer:
            token = relay_weights(i + 1, act)
            if i + 2 < nlayer:
                token = token + start_weights(i + 2, token)
            mods = mods + token[0:1, 0:1]
        F = act.shape[1]
        ff, x3 = mm_nn(act, full["w_ffn_out"].reshape(1, F, D), BF, "mm_ffn_out", res=x2, gate=mods[5:6])
        sv.update(mix=mix, x2=x2, h2=h2, gu=gu, act=act, ff=ff, w=full)
        saved.append(sv)
        xcur = x3

    loss_blk, dx, dfinal, dff = final_loss(xcur, tgt, final_g[None, :], saved[-1]["ff"], mods_all[nlayer - 1], 5,
                                           "final_loss")
    dg2 = dfinal[1:2]

    gland = {n: lax.empty((W[n].shape[0], N_CHIP) + W[n].shape[1:], BF) for n in SHARDED}
    in_flight = []
    dmods = [None] * nlayer
    dnorm1, dnorm2 = [None] * nlayer, [None] * nlayer
    dconv_w = [None] * nconv
    ds5 = [None] * ssm_a_re.shape[0]
    token = jnp.zeros((8, 128), F32)

    def send_grads(names, grads, slot, after, name):
        sems, thru, lands, tok = scatter_start([grads[n] for n in names], [gland[n] for n in names], slot, after, name)
        gland.update(zip(names, lands))
        in_flight.append((names, slot, sems, thru, name))
        return tok

    def land_grads(group, after):
        for names, slot, sems, thru, name in in_flight:
            if names[0] in group:
                got = scatter_wait(sems, thru, [gland[n] for n in names], slot, after, name.replace("scatter", "landed"))
                gland.update(zip(names, got))

    for i in reversed(range(nlayer)):
        j = i // 2
        mods = mods_all[i] + token[0:1, 0:1]
        sv = saved[i]
        full = sv["w"]
        gfull = {}
        F = sv["act"].shape[1]
        gfull["w_ffn_out"] = mm_tn(sv["act"], dff, 1, "mm_tn_ffn_out").reshape(N_CHIP, F // N_CHIP, D)
        dgu = ffn_out_bwd(dff, full["w_ffn_out"].reshape(F, D), sv["gu"], "ffn_out_bwd")
        gfull["w_ffn_in"] = mm_tn(sv["h2"], dgu, N_CHIP, "mm_tn_ffn_in")
        dh2 = mm_nt(dgu, full["w_ffn_in"], BF, "mm_nt_ffn_in")
        token = send_grads(["w_ffn_out", "w_ffn_in"], gfull, [i, i], dh2, "scatter_ffn%d" % i)
        mods = mods + token[0:1, 0:1]
        dx2, s2, dmix = norm_bwd(dh2, sv["x2"], dx, norm2_g[i:i + 1], mods, 3, "norm_bwd_mix",
                                 branch=(sv["mix"], mods, 2))
        dg1 = s2[3:4]
        if i % 2 == 0:
            do = glu_bwd(dmix, sv["o"], "glu_bwd")
            gfull["ssm_w_out"] = mm_tn(sv["z"], do, N_CHIP, "mm_tn_ssm_out")
            dz = mm_nt(do, full["ssm_w_out"], BF, "mm_nt_ssm_out")
            dh, dd, dab, db, dc = s5_bwd(dz, sv["y"], sv["h"], sv["states"], s5[j]["tc"], s5[j]["tbt"], s5[j]["pwr"],
                                         ssm_d[j:j + 1], "s5_bwd")
            ds5[j] = (dd, dab, db, dc)
        else:
            gfull["conv_w_out"] = mm_tn(sv["mc"], dmix, 1, "mm_tn_conv_out").reshape(N_CHIP, D // N_CHIP, D)
            dmc = mm_nt(dmix, full["conv_w_out"].reshape(1, D, D), BF, "mm_nt_conv_out")
            dbg, dcg, dvv, dcw = conv_bwd(dmc, sv["p"], cw_full[j], "conv_bwd")
            dp = jnp.concatenate([dbg, dcg, dvv], axis=1)
            gfull["conv_w_in"] = mm_tn(sv["h"], dp, N_CHIP, "mm_tn_conv_in")
            dh = mm_nt(dp, full["conv_w_in"], BF, "mm_nt_conv_in")
            dconv_w[j] = dcw[0:3]
        dmods_i = [s2[0:2], dg2]
        if i > 0:
            dx, s1, dff = norm_bwd(dh, sv["x"], dx2, norm1_g[i:i + 1], mods, 0, "norm_bwd_ffn",
                                   branch=(saved[i - 1]["ff"], mods_all[i - 1], 5))
            dg2 = s1[3:4]
        else:
            dx, s1 = norm_bwd(dh, sv["x"], dx2, norm1_g[i:i + 1], mods, 0, "norm_bwd")
        dmods[i] = jnp.concatenate([s1[0:2], dg1] + dmods_i, axis=0).reshape(6 * D)
        dnorm1[i], dnorm2[i] = s1[2], s2[2]
        names = ["ssm_w_out"] if i % 2 == 0 else ["conv_w_out", "conv_w_in"]
        token = send_grads(names, gfull, [j] * len(names), dx, "scatter_mix%d" % i)

    small = dict(norm1_g=jnp.stack(dnorm1), norm2_g=jnp.stack(dnorm2), b_ada=jnp.stack(dmods),
                 final_g=dfinal[0] + token[0, 0])
    per = {n: [] for n in ('ssm_a_re', 'ssm_a_im', 'ssm_log_step', 'ssm_b_re', 'ssm_b_im', 'ssm_c_re', 'ssm_c_im', 'ssm_d')}
    GL = G // nkb
    for j, (dd, dab, db, dc) in enumerate(ds5):
        dab = jnp.sum(dab, axis=1).reshape(nkb, 2, GL, SSM_STATE)
        g_abr, g_abi = dab[:, 0].reshape(G, SSM_STATE), dab[:, 1].reshape(G, SSM_STATE)
        db, dc = db.reshape(G, SSM_GROUP, 2, SSM_STATE), dc.reshape(G, SSM_GROUP, 2, SSM_STATE)
        gb_re, gb_im, gc_re, gc_im = db[:, :, 0], db[:, :, 1], dc[:, :, 0], dc[:, :, 1]
        ga_re, ga_im, gls, gbr, gbi = s5[j]["vjp"]((g_abr, g_abi, jnp.swapaxes(gb_re, 1, 2), jnp.swapaxes(gb_im, 1, 2)))
        for n, val in zip(per, (ga_re, ga_im, gls, gbr, gbi, gc_re, -gc_im, jnp.sum(dd, axis=0))):
            per[n].append(val)
    small.update({n: jnp.stack(vals) for n, vals in per.items()})
    dcw_full = jnp.stack(dconv_w)

    my_loss = loss_blk[0:1, 0:1]
    slab_like = [W[n] for n in SLAB] + [dcw_full, my_loss]
    rows64 = 8 * N_DEV
    slab = _pack([small[n] for n in SLAB] + [dcw_full, my_loss], rows64)
    per_dev = slab.shape[0] // N_DEV
    x_sems, x_srcs, x_lands, token = exchange_start(
        [(slab.reshape(N_DEV, per_dev, SLAB_W), True), (_pack([small["b_ada"]]), False)], dx, "small_scatter")

    early = [n for n in SHARDED if n != "ssm_w_out"]
    land_grads(early, token)
    mine = [reduce4(gland[n], "reduce4_" + n) for n in early]

    parts, dm_all = exchange_wait(x_sems, x_srcs, x_lands, [True, False], mine[-1][0, :8, :128], "small_landed")
    t_sems, t_srcs, t_lands, token = exchange_start([(sum8(parts, "sum_small"), False)], dm_all, "small_gather")
    out = {}

    w_sems, w_srcs, w_lands, token2 = swap_start(mine, "swap_start")
    dm_all = dm_all.reshape(N_DEV, -1)[:, :b_ada.size].reshape(N_DEV, nlayer, N_CHIP, NA)
    dm_sh = jnp.transpose(lax.dynamic_index_in_dim(dm_all, chip, axis=2, keepdims=False), (1, 0, 2))
    res = adamw_ada(jnp.transpose(c_all) + token[0:1, 0:1] + token2[0:1, 0:1], dm_sh, w_ada, m_w_ada, v_w_ada,
                    "adamw_ada")
    out["g", "w_ada"], out["d", "w_ada"], out["m", "w_ada"], out["v", "w_ada"] = res

    g_slab = exchange_wait(t_sems, t_srcs, t_lands, [False], out["g", "w_ada"], "small_total")[0]
    g_slab = g_slab.reshape(slab.shape)
    d_slab, m_slab, v_slab = adamw_slab(
        g_slab, _pack([W[n] for n in SLAB] + [jnp.zeros_like(dcw_full)], rows64),
        _pack([Mo[n] for n in SLAB] + [jnp.zeros_like(dcw_full)], rows64),
        _pack([Vo[n] for n in SLAB] + [jnp.ones_like(dcw_full)], rows64), "adamw_slab")
    for k, slab in zip(("g", "d", "m", "v"), (g_slab, d_slab, m_slab, v_slab)):
        for n, val in zip(SLAB, _unpack(slab, slab_like)):
            out[k, n] = val
    g_cw = lax.dynamic_slice_in_dim(_unpack(g_slab, slab_like)[-2], chip * conv_w.shape[2], conv_w.shape[2], axis=2)
    out["g", "conv_w"] = g_cw
    out["d", "conv_w"], out["m", "conv_w"], out["v", "conv_w"] = [
        r.reshape(conv_w.shape) for r in adamw_plain(conv_w.reshape(-1, conv_w.shape[2]), m_conv_w.reshape(-1, conv_w.shape[2]),
                                                     v_conv_w.reshape(-1, conv_w.shape[2]), g_cw.reshape(-1, conv_w.shape[2]),
                                                     "adamw_conv_w")]

    mine, theirs = swap_wait(w_sems, w_srcs, w_lands, d_slab, "swap_wait")
    for n, ga, gb in zip(early, mine, theirs):
        r = adamw_sharded(W[n], Mo[n], Vo[n], ga, gb, "adamw_" + n)
        out["g", n], out["d", n], out["m", n], out["v", n] = r

    land_grads(["ssm_w_out"], out["g", "w_ffn_out"])
    ga = reduce4(gland["ssm_w_out"], "reduce4_ssm_w_out")
    gb = swap_siblings([ga], "swap_siblings")[0]
    r = adamw_sharded(ssm_w_out, m_ssm_w_out, v_ssm_w_out, ga, gb, "adamw_ssm_w_out")
    out["g", "ssm_w_out"], out["d", "ssm_w_out"], out["m", "ssm_w_out"], out["v", "ssm_w_out"] = r

    loss = _unpack(g_slab, slab_like)[-1][0, 0]
    return (loss, dx[None], *[out["g", n] for n in WEIGHTS], *[out["d", n] for n in WEIGHTS],
            *[out["m", n] for n in WEIGHTS], *[out["v", n] for n in WEIGHTS])
```

```python
import math

import jax
import jax.numpy as jnp
from jax import lax
from jax.experimental import pallas as pl
from jax.experimental.pallas import tpu as pltpu

F32 = jnp.float32
BF = jnp.bfloat16
MESH = pl.DeviceIdType.MESH
ANY = pl.BlockSpec(memory_space=pl.ANY)

N_DEV = 8
N_CHIP = 4
SSM_GROUP = 16
SSM_STATE = 64
S5_BLOCK = 256
RMS_EPS = 1e-6
ADAM_LR, ADAM_B1, ADAM_B2, ADAM_EPS, ADAM_WD, ADAM_STEP = 0.001, 0.9, 0.999, 1e-08, 0.01, 10
V7X_VMEM_BYTES = 64 * 1024 * 1024
VMEM_LIMIT = V7X_VMEM_BYTES - 12 * 1024 * 1024
SLAB_W = 1024
GELU_C = math.sqrt(2.0 / math.pi)
GELU_A = 0.044715


def _cp(*sem):
    return pltpu.CompilerParams(dimension_semantics=sem if sem else None, vmem_limit_bytes=VMEM_LIMIT)


def _tile(n, prefs):
    for p in prefs:
        if p <= n and n % p == 0:
            return p
    return n


def _axes():
    return lax.axis_index("x"), lax.axis_index("y"), lax.axis_index("c")


def _flip(v, k):
    return 1 - v if k else v


def gather8(v, name):
    R, C = v.shape

    def body(v_ref, o_ref, ssem, rsem, lsem):
        x, y, c = _axes()
        me = 4 * x + 2 * y + c
        loc = pltpu.make_async_copy(v_ref, o_ref.at[me], lsem)
        loc.start()
        copies = []
        for k in range(1, N_DEV):
            peer = (_flip(x, (k >> 2) & 1), _flip(y, (k >> 1) & 1), _flip(c, k & 1))
            cp = pltpu.make_async_remote_copy(src_ref=v_ref, dst_ref=o_ref.at[me], send_sem=ssem.at[k - 1],
                                              recv_sem=rsem.at[k - 1], device_id=peer, device_id_type=MESH)
            cp.start()
            copies.append(cp)
        for cp in copies:
            cp.wait()
        loc.wait()

    return pl.pallas_call(
        body, name=name,
        out_shape=jax.ShapeDtypeStruct((N_DEV, R, C), v.dtype),
        in_specs=[pl.BlockSpec(memory_space=pltpu.VMEM)],
        out_specs=pl.BlockSpec(memory_space=pltpu.VMEM),
        scratch_shapes=[pltpu.SemaphoreType.DMA((N_DEV - 1,)), pltpu.SemaphoreType.DMA((N_DEV - 1,)),
                        pltpu.SemaphoreType.DMA],
        compiler_params=pltpu.CompilerParams(vmem_limit_bytes=VMEM_LIMIT),
    )(v)


HBM = pl.BlockSpec(memory_space=pltpu.HBM)
SEM = pl.BlockSpec(memory_space=pltpu.SEMAPHORE)
EFFECT = pltpu.SideEffectType.DATAFLOW_SIDE_EFFECTING


def _in_hbm(a):
    return pltpu.with_memory_space_constraint(a, pltpu.HBM)


def _chip_peers(x, y, c):
    out = []
    for k in range(1, N_CHIP):
        px, py = _flip(x, k >> 1), _flip(y, k & 1)
        out.append(((px, py, c), 2 * px + py))
    return out


def _my_half(ref, c):
    rows = ref.shape[0] // 2
    return pl.ds(pl.multiple_of(c * rows, 16), rows)


def relay_start(lands, after, name):
    n = len(lands)

    def body(*refs):
        land = refs[:n]
        ssem, rsem = refs[n + 1:n + 3]
        token = refs[-1]
        x, y, c = _axes()
        for a in range(n):
            half = _my_half(land[a].at[0], c)
            for k, (_, pchip) in enumerate(_chip_peers(x, y, c)):
                pltpu.make_async_remote_copy(src_ref=land[a].at[pchip, half], dst_ref=land[a].at[pchip, half],
                                             send_sem=ssem.at[3 * a + k], recv_sem=rsem.at[3 * a + k],
                                             device_id=(x, y, 1 - c), device_id_type=MESH).start()
        token[...] = jnp.zeros_like(token)

    out_shape = ([pltpu.SemaphoreType.DMA((3 * n,)), pltpu.SemaphoreType.DMA((3 * n,))]
                 + [pltpu.HBM(l.shape, l.dtype) for l in lands] + [jax.ShapeDtypeStruct((8, 128), F32)])
    res = pl.pallas_call(
        body, name=name, out_shape=out_shape, in_specs=[HBM] * n + [ANY],
        out_specs=[SEM, SEM] + [HBM] * n + [pl.BlockSpec(memory_space=pltpu.VMEM)],
        input_output_aliases={a: 2 + a for a in range(n)},
        compiler_params=pltpu.CompilerParams(has_side_effects=EFFECT),
    )(*lands, after)
    return tuple(res[:2]), list(res[2:2 + n]), res[-1]


def relay_wait(sems, lands, after, name):
    n = len(lands)

    def body(*refs):
        land = refs[:n]
        ssem, rsem = refs[n:n + 2]
        x, y, c = _axes()
        for a in range(n):
            mine, theirs = _my_half(land[a].at[0], c), _my_half(land[a].at[0], 1 - c)
            for k, (_, pchip) in enumerate(_chip_peers(x, y, c)):
                cp = pltpu.make_async_remote_copy(src_ref=land[a].at[pchip, mine], dst_ref=land[a].at[pchip, theirs],
                                                  send_sem=ssem.at[3 * a + k], recv_sem=rsem.at[3 * a + k],
                                                  device_id=(x, y, 1 - c), device_id_type=MESH)
                cp.wait_send()
                cp.wait_recv()

    res = pl.pallas_call(
        body, name=name, out_shape=[pltpu.HBM(l.shape, l.dtype) for l in lands],
        in_specs=[HBM] * n + [SEM, SEM, ANY], out_specs=[HBM] * n,
        input_output_aliases={a: a for a in range(n)},
        compiler_params=pltpu.CompilerParams(has_side_effects=EFFECT),
    )(*lands, *sems, after)
    return list(res)


def gather_start(shards, after, name):
    n = len(shards)

    def body(*refs):
        src, land = refs[:n], refs[n:2 * n]
        ssem, rsem, lsem = refs[2 * n + 1:2 * n + 4]
        token = refs[-1]
        x, y, c = _axes()
        chip = 2 * x + y
        for a in range(n):
            pltpu.make_async_copy(src[a], land[a].at[chip], lsem.at[a]).start()
            half = _my_half(src[a], c)
            for k, (peer, _) in enumerate(_chip_peers(x, y, c)):
                pltpu.make_async_remote_copy(src_ref=src[a].at[half], dst_ref=land[a].at[chip, half],
                                             send_sem=ssem.at[3 * a + k], recv_sem=rsem.at[3 * a + k],
                                             device_id=peer, device_id_type=MESH).start()
        token[...] = jnp.zeros_like(token)

    lands = [lax.empty((N_CHIP,) + s.shape, s.dtype) for s in shards]
    out_shape = ([pltpu.SemaphoreType.DMA((3 * n,)), pltpu.SemaphoreType.DMA((3 * n,)), pltpu.SemaphoreType.DMA((n,))]
                 + [pltpu.HBM(s.shape, s.dtype) for s in shards] + [pltpu.HBM(l.shape, l.dtype) for l in lands]
                 + [jax.ShapeDtypeStruct((8, 128), F32)])
    res = pl.pallas_call(
        body, name=name, out_shape=out_shape, in_specs=[HBM] * (2 * n) + [ANY],
        out_specs=[SEM, SEM, SEM] + [HBM] * (2 * n) + [pl.BlockSpec(memory_space=pltpu.VMEM)],
        input_output_aliases={a: 3 + a for a in range(2 * n)},
        compiler_params=pltpu.CompilerParams(has_side_effects=EFFECT),
    )(*[_in_hbm(s) for s in shards], *[_in_hbm(l) for l in lands], after)
    return tuple(res[:3]), list(res[3:3 + n]), list(res[3 + n:3 + 2 * n]), res[-1]


def gather_wait(sems, srcs, lands, idx, after, name):
    m = len(idx)

    def body(*refs):
        src, land = refs[:m], refs[m:2 * m]
        ssem, rsem, lsem = refs[2 * m:2 * m + 3]
        x, y, c = _axes()
        chip = 2 * x + y
        for j, a in enumerate(idx):
            half = _my_half(src[j], c)
            for k, (peer, pchip) in enumerate(_chip_peers(x, y, c)):
                cp = pltpu.make_async_remote_copy(src_ref=src[j].at[half], dst_ref=land[j].at[pchip, half],
                                                  send_sem=ssem.at[3 * a + k], recv_sem=rsem.at[3 * a + k],
                                                  device_id=peer, device_id_type=MESH)
                cp.wait_send()
                cp.wait_recv()
            pltpu.make_async_copy(src[j], land[j].at[chip], lsem.at[a]).wait()

    s_in = [srcs[a] for a in idx]
    l_in = [lands[a] for a in idx]
    res = pl.pallas_call(
        body, name=name,
        out_shape=[pltpu.HBM(s.shape, s.dtype) for s in s_in] + [pltpu.HBM(l.shape, l.dtype) for l in l_in],
        in_specs=[HBM] * (2 * m) + [SEM, SEM, SEM, ANY], out_specs=[HBM] * (2 * m),
        input_output_aliases={a: a for a in range(2 * m)},
        compiler_params=pltpu.CompilerParams(has_side_effects=EFFECT),
    )(*s_in, *l_in, *sems, after)
    return list(res[m:])


def scatter_start(grads, lands, slot, after, name):
    n = len(grads)

    def body(*refs):
        src, land = refs[:n], refs[n:2 * n]
        ssem, rsem, lsem = refs[2 * n + 1:2 * n + 4]
        token = refs[-1]
        x, y, c = _axes()
        chip = 2 * x + y
        for a in range(n):
            pltpu.make_async_copy(src[a].at[chip], land[a].at[slot[a], chip], lsem.at[a]).start()
            for k, (peer, pchip) in enumerate(_chip_peers(x, y, c)):
                pltpu.make_async_remote_copy(src_ref=src[a].at[pchip], dst_ref=land[a].at[slot[a], chip],
                                             send_sem=ssem.at[3 * a + k], recv_sem=rsem.at[3 * a + k],
                                             device_id=peer, device_id_type=MESH).start()
        token[...] = jnp.zeros_like(token)

    out_shape = ([pltpu.SemaphoreType.DMA((3 * n,)), pltpu.SemaphoreType.DMA((3 * n,)), pltpu.SemaphoreType.DMA((n,))]
                 + [pltpu.HBM(g.shape, g.dtype) for g in grads] + [pltpu.HBM(l.shape, l.dtype) for l in lands]
                 + [jax.ShapeDtypeStruct((8, 128), F32)])
    res = pl.pallas_call(
        body, name=name, out_shape=out_shape, in_specs=[HBM] * (2 * n) + [ANY],
        out_specs=[SEM, SEM, SEM] + [HBM] * (2 * n) + [pl.BlockSpec(memory_space=pltpu.VMEM)],
        input_output_aliases={a: 3 + a for a in range(2 * n)},
        compiler_params=pltpu.CompilerParams(has_side_effects=EFFECT),
    )(*[_in_hbm(g) for g in grads], *[_in_hbm(l) for l in lands], after)
    return tuple(res[:3]), list(res[3:3 + n]), list(res[3 + n:3 + 2 * n]), res[-1]


def scatter_wait(sems, grads, lands, slot, after, name):
    n = len(grads)

    def body(*refs):
        src, land = refs[:n], refs[n:2 * n]
        ssem, rsem, lsem = refs[2 * n:2 * n + 3]
        x, y, c = _axes()
        chip = 2 * x + y
        for a in range(n):
            for k, (peer, pchip) in enumerate(_chip_peers(x, y, c)):
                cp = pltpu.make_async_remote_copy(src_ref=src[a].at[pchip], dst_ref=land[a].at[slot[a], pchip],
                                                  send_sem=ssem.at[3 * a + k], recv_sem=rsem.at[3 * a + k],
                                                  device_id=peer, device_id_type=MESH)
                cp.wait_send()
                cp.wait_recv()
            pltpu.make_async_copy(src[a].at[chip], land[a].at[slot[a], chip], lsem.at[a]).wait()

    res = pl.pallas_call(
        body, name=name,
        out_shape=[pltpu.HBM(g.shape, g.dtype) for g in grads] + [pltpu.HBM(l.shape, l.dtype) for l in lands],
        in_specs=[HBM] * (2 * n) + [SEM, SEM, SEM, ANY], out_specs=[HBM] * (2 * n),
        input_output_aliases={a: a for a in range(2 * n)},
        compiler_params=pltpu.CompilerParams(has_side_effects=EFFECT),
    )(*grads, *lands, *sems, after)
    return list(res[n:])


def reduce4(land, name):
    nl, _, R, C = land.shape
    TR = _adam_rows(R, C)

    def body(l_ref, o_ref):
        o_ref[...] = ((l_ref[0].astype(F32) + l_ref[1].astype(F32)) + l_ref[2].astype(F32)) + l_ref[3].astype(F32)

    return pl.pallas_call(
        body, name=name, grid=(nl, R // TR),
        in_specs=[pl.BlockSpec((None, N_CHIP, TR, C), lambda i, r: (i, 0, r, 0))],
        out_specs=pl.BlockSpec((None, TR, C), lambda i, r: (i, r, 0)),
        out_shape=jax.ShapeDtypeStruct((nl, R, C), F32), compiler_params=_cp("parallel", "parallel"))(land)


def swap_siblings(arrs, name):
    n = len(arrs)

    def body(*refs):
        src, dst = refs[:n], refs[n:2 * n]
        ssem, rsem = refs[2 * n:]
        x, y, c = _axes()
        cps = [pltpu.make_async_remote_copy(src_ref=src[a], dst_ref=dst[a], send_sem=ssem.at[a], recv_sem=rsem.at[a],
                                            device_id=(x, y, 1 - c), device_id_type=MESH) for a in range(n)]
        for cp in cps:
            cp.start()
        for cp in cps:
            cp.wait()

    return pl.pallas_call(
        body, name=name, out_shape=[jax.ShapeDtypeStruct(a.shape, a.dtype) for a in arrs],
        in_specs=[ANY] * n, out_specs=[ANY] * n,
        scratch_shapes=[pltpu.SemaphoreType.DMA((n,)), pltpu.SemaphoreType.DMA((n,))],
        compiler_params=pltpu.CompilerParams(vmem_limit_bytes=VMEM_LIMIT),
    )(*arrs)


def swap_start(arrs, name):
    n = len(arrs)

    def body(*refs):
        src, land = refs[:n], refs[n:2 * n]
        ssem, rsem = refs[2 * n:2 * n + 2]
        token = refs[-1]
        x, y, c = _axes()
        for a in range(n):
            pltpu.make_async_remote_copy(src_ref=src[a], dst_ref=land[a], send_sem=ssem.at[a], recv_sem=rsem.at[a],
                                         device_id=(x, y, 1 - c), device_id_type=MESH).start()
        token[...] = jnp.zeros_like(token)

    lands = [lax.empty(a.shape, a.dtype) for a in arrs]
    out_shape = ([pltpu.SemaphoreType.DMA((n,)), pltpu.SemaphoreType.DMA((n,))]
                 + [pltpu.HBM(a.shape, a.dtype) for a in arrs] * 2 + [jax.ShapeDtypeStruct((8, 128), F32)])
    res = pl.pallas_call(
        body, name=name, out_shape=out_shape, in_specs=[HBM] * (2 * n),
        out_specs=[SEM, SEM] + [HBM] * (2 * n) + [pl.BlockSpec(memory_space=pltpu.VMEM)],
        input_output_aliases={a: 2 + a for a in range(2 * n)},
        compiler_params=pltpu.CompilerParams(has_side_effects=EFFECT),
    )(*[_in_hbm(a) for a in arrs], *[_in_hbm(l) for l in lands])
    return tuple(res[:2]), list(res[2:2 + n]), list(res[2 + n:2 + 2 * n]), res[-1]


def swap_wait(sems, srcs, lands, after, name):
    n = len(srcs)

    def body(*refs):
        src, land = refs[:n], refs[n:2 * n]
        ssem, rsem = refs[2 * n:2 * n + 2]
        x, y, c = _axes()
        for a in range(n):
            cp = pltpu.make_async_remote_copy(src_ref=src[a], dst_ref=land[a], send_sem=ssem.at[a],
                                              recv_sem=rsem.at[a], device_id=(x, y, 1 - c), device_id_type=MESH)
            cp.wait_send()
            cp.wait_recv()

    res = pl.pallas_call(
        body, name=name, out_shape=[pltpu.HBM(a.shape, a.dtype) for a in srcs] * 2,
        in_specs=[HBM] * (2 * n) + [SEM, SEM, ANY], out_specs=[HBM] * (2 * n),
        input_output_aliases={a: a for a in range(2 * n)},
        compiler_params=pltpu.CompilerParams(has_side_effects=EFFECT),
    )(*srcs, *lands, *sems, after)
    return list(res[:n]), list(res[n:])


def _all_peers(x, y, c):
    out = []
    for k in range(1, N_DEV):
        px, py, pc = _flip(x, (k >> 2) & 1), _flip(y, (k >> 1) & 1), _flip(c, k & 1)
        out.append(((px, py, pc), 4 * px + 2 * py + pc))
    return out


def exchange_start(items, after, name):
    n = len(items)

    def body(*refs):
        src, land = refs[:n], refs[n:2 * n]
        ssem, rsem, lsem = refs[2 * n + 1:2 * n + 4]
        token = refs[-1]
        x, y, c = _axes()
        me = 4 * x + 2 * y + c
        for a, (_, scatter) in enumerate(items):
            pltpu.make_async_copy(src[a].at[me] if scatter else src[a], land[a].at[me], lsem.at[a]).start()
            for k, (peer, p) in enumerate(_all_peers(x, y, c)):
                pltpu.make_async_remote_copy(src_ref=src[a].at[p] if scatter else src[a], dst_ref=land[a].at[me],
                                             send_sem=ssem.at[7 * a + k], recv_sem=rsem.at[7 * a + k],
                                             device_id=peer, device_id_type=MESH).start()
        token[...] = jnp.zeros_like(token)

    srcs = [s for s, _ in items]
    lands = [lax.empty(s.shape if sc else (N_DEV,) + s.shape, s.dtype) for s, sc in items]
    out_shape = ([pltpu.SemaphoreType.DMA((7 * n,)), pltpu.SemaphoreType.DMA((7 * n,)), pltpu.SemaphoreType.DMA((n,))]
                 + [pltpu.HBM(s.shape, s.dtype) for s in srcs] + [pltpu.HBM(l.shape, l.dtype) for l in lands]
                 + [jax.ShapeDtypeStruct((8, 128), F32)])
    res = pl.pallas_call(
        body, name=name, out_shape=out_shape, in_specs=[HBM] * (2 * n) + [ANY],
        out_specs=[SEM, SEM, SEM] + [HBM] * (2 * n) + [pl.BlockSpec(memory_space=pltpu.VMEM)],
        input_output_aliases={a: 3 + a for a in range(2 * n)},
        compiler_params=pltpu.CompilerParams(has_side_effects=EFFECT),
    )(*[_in_hbm(s) for s in srcs], *[_in_hbm(l) for l in lands], after)
    return tuple(res[:3]), list(res[3:3 + n]), list(res[3 + n:3 + 2 * n]), res[-1]


def exchange_wait(sems, srcs, lands, scatter, after, name):
    n = len(srcs)

    def body(*refs):
        src, land = refs[:n], refs[n:2 * n]
        ssem, rsem, lsem = refs[2 * n:2 * n + 3]
        x, y, c = _axes()
        me = 4 * x + 2 * y + c
        for a in range(n):
            for k, (peer, p) in enumerate(_all_peers(x, y, c)):
                cp = pltpu.make_async_remote_copy(src_ref=src[a].at[p] if scatter[a] else src[a],
                                                  dst_ref=land[a].at[p], send_sem=ssem.at[7 * a + k],
                                                  recv_sem=rsem.at[7 * a + k], device_id=peer, device_id_type=MESH)
                cp.wait_send()
                cp.wait_recv()
            pltpu.make_async_copy(src[a].at[me] if scatter[a] else src[a], land[a].at[me], lsem.at[a]).wait()

    res = pl.pallas_call(
        body, name=name,
        out_shape=[pltpu.HBM(s.shape, s.dtype) for s in srcs] + [pltpu.HBM(l.shape, l.dtype) for l in lands],
        in_specs=[HBM] * (2 * n) + [SEM, SEM, SEM, ANY], out_specs=[HBM] * (2 * n),
        input_output_aliases={a: a for a in range(2 * n)},
        compiler_params=pltpu.CompilerParams(has_side_effects=EFFECT),
    )(*srcs, *lands, *sems, after)
    return list(res[n:])


def sum8(parts, name):
    _, P, C = parts.shape

    def body(p_ref, o_ref):
        tot = p_ref[0]
        for d in range(1, N_DEV):
            tot = tot + p_ref[d]
        o_ref[...] = tot

    return pl.pallas_call(body, name=name, out_shape=jax.ShapeDtypeStruct((P, C), F32),
                          compiler_params=pltpu.CompilerParams(vmem_limit_bytes=VMEM_LIMIT))(parts)


def mm_nn(a, w, out_dtype, name, res=None, gate=None):
    M, K = a.shape
    S, _, Ns = w.shape
    TM = _tile(M, (1024, 512, 256) if K <= 1024 else (512, 256))
    TN = _tile(Ns, (1408, 1024, 768, 512, 256, 128))
    nj = Ns // TN
    fused = res is not None

    def body(*refs):
        if fused:
            a_ref, w_ref, r_ref, g_ref, f_ref, o_ref = refs
        else:
            a_ref, w_ref, f_ref = refs
        f = jnp.dot(a_ref[...], w_ref[...], preferred_element_type=F32)
        f_ref[...] = f.astype(f_ref.dtype)
        if fused:
            o_ref[...] = r_ref[...] + g_ref[...] * f

    col = lambda s, j, i: (i, s * nj + j)
    in_specs = [pl.BlockSpec((TM, K), lambda s, j, i: (i, 0)), pl.BlockSpec((None, K, TN), lambda s, j, i: (s, 0, j))]
    out_specs = [pl.BlockSpec((TM, TN), col)]
    out_shape = [jax.ShapeDtypeStruct((M, S * Ns), out_dtype)]
    args = [a, w]
    if fused:
        in_specs += [pl.BlockSpec((TM, TN), col), pl.BlockSpec((1, TN), lambda s, j, i: (0, s * nj + j))]
        out_specs.append(pl.BlockSpec((TM, TN), col))
        out_shape.append(jax.ShapeDtypeStruct((M, S * Ns), F32))
        args += [res, gate]
    out = pl.pallas_call(body, name=name, grid=(S, nj, M // TM), in_specs=in_specs, out_specs=out_specs,
                         out_shape=out_shape, compiler_params=_cp("parallel", "parallel", "parallel"))(*args)
    return tuple(out) if fused else out[0]


def mm_nt(g, w, out_dtype, name):
    g3 = g if g.ndim == 3 else g[None]
    Q, M, F = g3.shape
    S, K, Ns = w.shape
    TM = _tile(M, (1024, 512, 256) if K <= 1024 else (512, 256))
    TN = _tile(Ns, (1408, 1024, 768, 512, 256, 128))
    nj = Ns // TN
    nred = S * nj
    per_part = F // TN

    def body(g_ref, w_ref, o_ref, acc):
        n = pl.program_id(1)

        @pl.when(n == 0)
        def _():
            acc[...] = jnp.zeros_like(acc)

        acc[...] += lax.dot_general(g_ref[...], w_ref[...], (((1,), (1,)), ((), ())), preferred_element_type=F32)

        @pl.when(n == nred - 1)
        def _():
            o_ref[...] = acc[...].astype(o_ref.dtype)

    return pl.pallas_call(
        body, name=name, grid=(M // TM, nred),
        in_specs=[pl.BlockSpec((None, TM, TN), lambda i, n: (n // per_part, i, n % per_part)),
                  pl.BlockSpec((None, K, TN), lambda i, n: (n // nj, 0, n % nj))],
        out_specs=pl.BlockSpec((TM, K), lambda i, n: (i, 0)),
        out_shape=jax.ShapeDtypeStruct((M, K), out_dtype),
        scratch_shapes=[pltpu.VMEM((TM, K), F32)],
        compiler_params=_cp("parallel", "arbitrary"))(g3, w)


def mm_tn(a, g, S, name):
    M, K = a.shape
    g3 = g if g.ndim == 3 else g[None]
    Q, _, F = g3.shape
    Ns = Q * F // S
    TK = _tile(K, (256, 128))
    TN = _tile(Ns, (1408, 1024, 768, 512, 256, 128))
    nj = Ns // TN
    per_part = F // TN

    def body(a_ref, g_ref, o_ref):
        o_ref[...] = lax.dot_general(a_ref[...], g_ref[...], (((0,), (0,)), ((), ())),
                                     preferred_element_type=F32).astype(o_ref.dtype)

    return pl.pallas_call(
        body, name=name, grid=(S * nj, K // TK),
        in_specs=[pl.BlockSpec((M, TK), lambda n, k: (0, k)),
                  pl.BlockSpec((None, M, TN), lambda n, k: (n // per_part, 0, n % per_part))],
        out_specs=pl.BlockSpec((None, TK, TN), lambda n, k: (n // nj, k, n % nj)),
        out_shape=jax.ShapeDtypeStruct((S, K, Ns), BF),
        compiler_params=_cp("parallel", "parallel"))(a, g3)


ROW_TILE = (512, 256)


def _rows(TL, D):
    return pl.BlockSpec((TL, D), lambda i: (i, 0))


def _fixed(R, D):
    return pl.BlockSpec((R, D), lambda i: (0, 0))


def _rowsum8(v):
    T, D = v.shape
    return jnp.sum(v.reshape(T // 8, 8, D), axis=0)


def _norm_parts(xv):
    r = lax.rsqrt(jnp.mean(xv * xv, axis=-1, keepdims=True) + RMS_EPS)
    return xv * r, r


def norm_mod(x, gamma, mods, k_shift, out_dtype, name):
    L, D = x.shape
    TL = _tile(L, ROW_TILE)

    def body(x_ref, g_ref, m_ref, o_ref):
        xn, _ = _norm_parts(x_ref[...])
        sh, sc = m_ref[k_shift:k_shift + 1, :], m_ref[k_shift + 1:k_shift + 2, :]
        o_ref[...] = ((xn * g_ref[...]) * (1.0 + sc) + sh).astype(o_ref.dtype)

    return pl.pallas_call(body, name=name, grid=(L // TL,),
                          in_specs=[_rows(TL, D), _fixed(1, D), _fixed(6, D)], out_specs=_rows(TL, D),
                          out_shape=jax.ShapeDtypeStruct((L, D), out_dtype), compiler_params=_cp("parallel"))(x, gamma, mods)


def norm_bwd(dh, x, dres, gamma, mods, k_shift, name, branch=None):
    L, D = x.shape
    TL = _tile(L, ROW_TILE)
    nacc = 4 if branch else 3

    def body(*refs):
        if branch:
            dh_ref, x_ref, dr_ref, g_ref, m_ref, f_ref, fm_ref, dx_ref, s_ref, df_ref, acc = refs
        else:
            dh_ref, x_ref, dr_ref, g_ref, m_ref, dx_ref, s_ref, acc = refs
        i = pl.program_id(0)

        @pl.when(i == 0)
        def _():
            acc[...] = jnp.zeros_like(acc)

        xn, r = _norm_parts(x_ref[...])
        dh_v = dh_ref[...].astype(F32)
        gam = g_ref[...]
        sc = m_ref[k_shift + 1:k_shift + 2, :]
        dn = dh_v * (1.0 + sc)
        dxn = dn * gam
        dx = dr_ref[...] + r * (dxn - xn * jnp.mean(dxn * xn, axis=-1, keepdims=True))
        dx_ref[...] = dx
        acc[0] += _rowsum8(dh_v)
        acc[1] += _rowsum8(dh_v * (xn * gam))
        acc[2] += _rowsum8(dn * xn)
        if branch:
            df_ref[...] = (dx * fm_ref[branch[2]:branch[2] + 1, :]).astype(df_ref.dtype)
            acc[3] += _rowsum8(dx * f_ref[...].astype(F32))

        @pl.when(i == pl.num_programs(0) - 1)
        def _():
            s_ref[...] = jnp.zeros_like(s_ref)
            for q in range(nacc):
                s_ref[q:q + 1, :] = jnp.sum(acc[q], axis=0, keepdims=True)

    in_specs = [_rows(TL, D), _rows(TL, D), _rows(TL, D), _fixed(1, D), _fixed(6, D)]
    out_specs = [_rows(TL, D), _fixed(8, D)]
    out_shape = [jax.ShapeDtypeStruct((L, D), F32), jax.ShapeDtypeStruct((8, D), F32)]
    args = [dh, x, dres, gamma, mods]
    if branch:
        in_specs += [_rows(TL, D), _fixed(6, D)]
        out_specs.append(_rows(TL, D))
        out_shape.append(jax.ShapeDtypeStruct((L, D), BF))
        args += [branch[0], branch[1]]
    return pl.pallas_call(
        body, name=name, grid=(L // TL,), in_specs=in_specs, out_specs=out_specs, out_shape=out_shape,
        scratch_shapes=[pltpu.VMEM((nacc, 8, D), F32)], compiler_params=_cp("arbitrary"))(*args)


def ffn_in_act(a, w, name):
    M, K = a.shape
    S, _, Ns = w.shape
    half = S // 2
    TM = _tile(M, (512, 256))
    TN = _tile(Ns, (1408, 1024, 768, 512, 256, 128))
    nj = Ns // TN

    def body(a_ref, wg_ref, wu_ref, gu_ref, act_ref):
        av = a_ref[...]
        g = jnp.dot(av, wg_ref[...], preferred_element_type=F32)
        u = jnp.dot(av, wu_ref[...], preferred_element_type=F32)
        gu_ref[0] = g.astype(gu_ref.dtype)
        gu_ref[1] = u.astype(gu_ref.dtype)
        act_ref[...] = (g * jax.nn.sigmoid(g) * u).astype(act_ref.dtype)

    return pl.pallas_call(
        body, name=name, grid=(half, nj, M // TM),
        in_specs=[pl.BlockSpec((TM, K), lambda s, j, i: (i, 0)),
                  pl.BlockSpec((None, K, TN), lambda s, j, i: (s, 0, j)),
                  pl.BlockSpec((None, K, TN), lambda s, j, i: (s + half, 0, j))],
        out_specs=[pl.BlockSpec((2, TM, TN), lambda s, j, i: (0, i, s * nj + j)),
                   pl.BlockSpec((TM, TN), lambda s, j, i: (i, s * nj + j))],
        out_shape=[jax.ShapeDtypeStruct((2, M, half * Ns), BF), jax.ShapeDtypeStruct((M, half * Ns), BF)],
        compiler_params=_cp("parallel", "parallel", "parallel"))(a, w, w)


def ffn_out_bwd(dff, w2, gu, name):
    M, D = dff.shape
    F = w2.shape[0]
    TM = _tile(M, (512, 256))
    CW = _tile(F, (256, 128))

    def body(d_ref, w_ref, gu_ref, o_ref):
        dv = d_ref[...]

        def product(c):
            return lax.dot_general(dv, w_ref[c:c + CW, :], (((1,), (1,)), ((), ())), preferred_element_type=F32)

        da = product(0)
        for c in range(0, F, CW):
            ahead = product(c + CW) if c + CW < F else None
            g = gu_ref[0, :, c:c + CW].astype(F32)
            u = gu_ref[1, :, c:c + CW].astype(F32)
            s = jax.nn.sigmoid(g)
            o_ref[0, :, c:c + CW] = (da * u * (s + g * s * (1.0 - s))).astype(o_ref.dtype)
            o_ref[1, :, c:c + CW] = (da * g * s).astype(o_ref.dtype)
            da = ahead

    part = pl.BlockSpec((2, TM, F), lambda i: (0, i, 0))
    return pl.pallas_call(
        body, name=name, grid=(M // TM,),
        in_specs=[pl.BlockSpec((TM, D), lambda i: (i, 0)), pl.BlockSpec((F, D), lambda i: (0, 0)), part],
        out_specs=part, out_shape=jax.ShapeDtypeStruct((2, M, F), BF),
        compiler_params=_cp("parallel"))(dff, w2, gu)


def ssm_out_glu(z, w, x, mods, k_gate, name):
    M, K = z.shape
    S, _, Ns = w.shape
    half = S // 2
    TM = _tile(M, (1024, 512, 256))
    TN = _tile(Ns, (512, 256, 128))
    nj = Ns // TN

    def body(z_ref, wv_ref, wg_ref, x_ref, m_ref, o_ref, mix_ref, y_ref):
        zv = z_ref[...]
        CW = _tile(TN, (256, 128))

        def products(c):
            return (jnp.dot(zv, wv_ref[:, c:c + CW], preferred_element_type=F32),
                    jnp.dot(zv, wg_ref[:, c:c + CW], preferred_element_type=F32))

        cur = products(0)
        for c in range(0, TN, CW):
            ahead = products(c + CW) if c + CW < TN else None
            val, gate = cur
            o_ref[0, :, c:c + CW] = val.astype(o_ref.dtype)
            o_ref[1, :, c:c + CW] = gate.astype(o_ref.dtype)
            mix = val * jax.nn.sigmoid(gate)
            mix_ref[:, c:c + CW] = mix.astype(mix_ref.dtype)
            y_ref[:, c:c + CW] = x_ref[:, c:c + CW] + m_ref[k_gate:k_gate + 1, c:c + CW] * mix
            cur = ahead

    col = lambda s, j, i: (i, s * nj + j)
    return pl.pallas_call(
        body, name=name, grid=(half, nj, M // TM),
        in_specs=[pl.BlockSpec((TM, K), lambda s, j, i: (i, 0)),
                  pl.BlockSpec((None, K, TN), lambda s, j, i: (s, 0, j)),
                  pl.BlockSpec((None, K, TN), lambda s, j, i: (s + half, 0, j)),
                  pl.BlockSpec((TM, TN), col), pl.BlockSpec((6, TN), lambda s, j, i: (0, s * nj + j))],
        out_specs=[pl.BlockSpec((2, TM, TN), lambda s, j, i: (0, i, s * nj + j)), pl.BlockSpec((TM, TN), col),
                   pl.BlockSpec((TM, TN), col)],
        out_shape=[jax.ShapeDtypeStruct((2, M, half * Ns), BF), jax.ShapeDtypeStruct((M, half * Ns), BF),
                   jax.ShapeDtypeStruct((M, half * Ns), F32)],
        compiler_params=_cp("parallel", "parallel", "parallel"))(z, w, w, x, mods)


def glu_bwd(dmix, o, name):
    _, L, D = o.shape
    TL = _tile(L, ROW_TILE)

    def body(d_ref, o_ref, do_ref):
        d = d_ref[...].astype(F32)
        val = o_ref[0].astype(F32)
        s = jax.nn.sigmoid(o_ref[1].astype(F32))
        do_ref[0] = (d * s).astype(do_ref.dtype)
        do_ref[1] = (d * val * s * (1.0 - s)).astype(do_ref.dtype)

    part = pl.BlockSpec((2, TL, D), lambda i: (0, i, 0))
    return pl.pallas_call(body, name=name, grid=(L // TL,), in_specs=[_rows(TL, D), part],
                          out_specs=part, out_shape=jax.ShapeDtypeStruct((2, L, D), BF),
                          compiler_params=_cp("parallel"))(dmix, o)


def final_loss(x, target, gamma, f, fmods, k_gate, name):
    L, D = x.shape
    TL = _tile(L, ROW_TILE)

    def body(x_ref, t_ref, g_ref, f_ref, fm_ref, l_ref, dx_ref, s_ref, df_ref, acc, lacc):
        i = pl.program_id(0)

        @pl.when(i == 0)
        def _():
            acc[...] = jnp.zeros_like(acc)
            lacc[...] = jnp.zeros_like(lacc)

        xn, r = _norm_parts(x_ref[...])
        gam = g_ref[...]
        e = xn * gam - t_ref[...]
        lacc[...] += jnp.sum(0.5 * jnp.mean(e * e, axis=-1, keepdims=True), axis=0, keepdims=True)
        dy = e * (1.0 / D)
        dxn = dy * gam
        dx = r * (dxn - xn * jnp.mean(dxn * xn, axis=-1, keepdims=True))
        dx_ref[...] = dx
        df_ref[...] = (dx * fm_ref[k_gate:k_gate + 1, :]).astype(df_ref.dtype)
        acc[0] += _rowsum8(dy * xn)
        acc[1] += _rowsum8(dx * f_ref[...].astype(F32))

        @pl.when(i == pl.num_programs(0) - 1)
        def _():
            s_ref[...] = jnp.zeros_like(s_ref)
            for q in range(2):
                s_ref[q:q + 1, :] = jnp.sum(acc[q], axis=0, keepdims=True)
            l_ref[...] = jnp.broadcast_to(lacc[...], l_ref.shape)

    return pl.pallas_call(
        body, name=name, grid=(L // TL,),
        in_specs=[_rows(TL, D), _rows(TL, D), _fixed(1, D), _rows(TL, D), _fixed(6, D)],
        out_specs=[_fixed(8, 128), _rows(TL, D), _fixed(8, D), _rows(TL, D)],
        out_shape=[jax.ShapeDtypeStruct((8, 128), F32), jax.ShapeDtypeStruct((L, D), F32),
                   jax.ShapeDtypeStruct((8, D), F32), jax.ShapeDtypeStruct((L, D), BF)],
        scratch_shapes=[pltpu.VMEM((2, 8, D), F32), pltpu.VMEM((1, 1), F32)],
        compiler_params=_cp("arbitrary"))(x, target, gamma, f, fmods)


def _col(L, TC, off):
    return pl.BlockSpec((L, TC), lambda j: (0, off + j))


def _shift_down(v, k, row):
    return jnp.where(row >= k, pltpu.roll(v, k, 0), 0.0)


def _shift_up(v, k, row, L):
    return jnp.where(row < L - k, pltpu.roll(v, L - k, 0), 0.0)


def conv_fwd(p, w, name):
    L, D3 = p.shape
    D = D3 // 3
    TC = _tile(D, (128,))
    nc = D // TC

    def body(b_ref, c_ref, v_ref, w_ref, o_ref):
        row = lax.broadcasted_iota(jnp.int32, (L, TC), 0)
        cv = c_ref[...].astype(F32) * v_ref[...].astype(F32)
        conv = w_ref[2:3, :] * cv + w_ref[1:2, :] * _shift_down(cv, 1, row) + w_ref[0:1, :] * _shift_down(cv, 2, row)
        o_ref[...] = (b_ref[...].astype(F32) * conv).astype(o_ref.dtype)

    return pl.pallas_call(
        body, name=name, grid=(nc,),
        in_specs=[_col(L, TC, 0), _col(L, TC, nc), _col(L, TC, 2 * nc), pl.BlockSpec((3, TC), lambda j: (0, j))],
        out_specs=_col(L, TC, 0), out_shape=jax.ShapeDtypeStruct((L, D), BF), compiler_params=_cp("parallel"))(p, p, p, w)


def conv_bwd(dm, p, w, name):
    L, D3 = p.shape
    D = D3 // 3
    TC = _tile(D, (128,))
    nc = D // TC

    def body(dm_ref, b_ref, c_ref, v_ref, w_ref, db_ref, dc_ref, dv_ref, dw_ref):
        row = lax.broadcasted_iota(jnp.int32, (L, TC), 0)
        cg, vv = c_ref[...].astype(F32), v_ref[...].astype(F32)
        cv = cg * vv
        cv1, cv2 = _shift_down(cv, 1, row), _shift_down(cv, 2, row)
        conv = w_ref[2:3, :] * cv + w_ref[1:2, :] * cv1 + w_ref[0:1, :] * cv2
        dmv = dm_ref[...].astype(F32)
        db_ref[...] = (dmv * conv).astype(db_ref.dtype)
        dconv = dmv * b_ref[...].astype(F32)
        dcv = (w_ref[2:3, :] * dconv + w_ref[1:2, :] * _shift_up(dconv, 1, row, L)
               + w_ref[0:1, :] * _shift_up(dconv, 2, row, L))
        dc_ref[...] = (dcv * vv).astype(dc_ref.dtype)
        dv_ref[...] = (dcv * cg).astype(dv_ref.dtype)
        dw_ref[...] = jnp.zeros_like(dw_ref)
        dw_ref[0:1, :] = jnp.sum(dconv * cv2, axis=0, keepdims=True)
        dw_ref[1:2, :] = jnp.sum(dconv * cv1, axis=0, keepdims=True)
        dw_ref[2:3, :] = jnp.sum(dconv * cv, axis=0, keepdims=True)

    one = jax.ShapeDtypeStruct((L, D), BF)
    return pl.pallas_call(
        body, name=name, grid=(nc,),
        in_specs=[_col(L, TC, 0), _col(L, TC, 0), _col(L, TC, nc), _col(L, TC, 2 * nc),
                  pl.BlockSpec((3, TC), lambda j: (0, j))],
        out_specs=[_col(L, TC, 0), _col(L, TC, 0), _col(L, TC, 0), pl.BlockSpec((8, TC), lambda j: (0, j))],
        out_shape=[one, one, one, jax.ShapeDtypeStruct((8, D), F32)],
        compiler_params=_cp("parallel"))(dm, p, p, p, w)


def _gelu(y):
    return 0.5 * y * (1.0 + jnp.tanh(GELU_C * (y + GELU_A * y * y * y)))


def _gelu_grad(y):
    th = jnp.tanh(GELU_C * (y + GELU_A * y * y * y))
    return 0.5 * (1.0 + th) + 0.5 * y * (1.0 - th * th) * GELU_C * (1.0 + 3.0 * GELU_A * y * y)


def _cmul_add(br, bi, ar, ai, sr, si):
    return br + ar * sr - ai * si, bi + ar * si + ai * sr


def _log2(n):
    k = n.bit_length() - 1
    assert 1 << k == n
    return k


def _replicate(P2, W2, P, GLP, transposed):
    shape = (W2, P2) if transposed else (P2, W2)
    k = lax.broadcasted_iota(jnp.int32, shape, 1 if transposed else 0)
    c = lax.broadcasted_iota(jnp.int32, shape, 0 if transposed else 1)
    return ((k >> _log2(P)) == (c >> _log2(GLP))) & ((k & (P - 1)) == (c & (P - 1)))


def _on_diagonal(KB, W2, H, P, GLP, transposed):
    shape = (W2, KB) if transposed else (KB, W2)
    r = lax.broadcasted_iota(jnp.int32, shape, 1 if transposed else 0)
    c = lax.broadcasted_iota(jnp.int32, shape, 0 if transposed else 1)
    return (r >> _log2(H)) == ((c & (GLP - 1)) >> _log2(P))


def _expand(t, dims, transposed):
    KB, W2, H, P, GLP = dims
    rep = _replicate(2 * P, W2, P, GLP, transposed).astype(t.dtype)
    wide = jnp.dot(rep, t, preferred_element_type=F32) if transposed else jnp.dot(t, rep, preferred_element_type=F32)
    return jnp.where(_on_diagonal(KB, W2, H, P, GLP, transposed), wide, 0.0).astype(t.dtype)


def _extract(acc, dims):
    KB, W2, H, P, GLP = dims
    rep = _replicate(2 * P, W2, P, GLP, True).astype(BF)
    kept = jnp.where(_on_diagonal(KB, W2, H, P, GLP, False), acc, 0.0)
    hi = kept.astype(BF)
    lo = (kept - hi.astype(F32)).astype(BF)
    return jnp.dot(hi, rep, preferred_element_type=F32) + jnp.dot(lo, rep, preferred_element_type=F32)


def _cmul(ar, ai, sr, si):
    return ar * sr - ai * si, ar * si + ai * sr


def _chunk_order(TL, CH, transposed):
    out_row = lax.broadcasted_iota(jnp.int32, (TL, TL), 1 if transposed else 0)
    in_row = lax.broadcasted_iota(jnp.int32, (TL, TL), 0 if transposed else 1)
    return in_row == ((out_row & 7) << _log2(CH)) + (out_row >> 3)


def _reorder(perm, v):
    hi = v.astype(perm.dtype)
    lo = (v - hi.astype(F32)).astype(perm.dtype)
    return jnp.dot(perm, hi, preferred_element_type=F32) + jnp.dot(perm, lo, preferred_element_type=F32)


def _interleave(main, side):
    n, m, k = len(main), len(side), 0
    for i, step in enumerate(main):
        step()
        while k < m and (k + 1) * n <= (i + 1) * m:
            side[k]()
            k += 1
    for step in side[k:]:
        step()


S5_CHUNK = 512


def s5_fwd(h, tb, tct, pw, dvec, name):
    L, D = h.shape
    nkb, KB, P2 = tb.shape
    P = P2 // 2
    W = (KB // SSM_GROUP) * P
    W2 = 2 * W
    dims = (KB, W2, SSM_GROUP, P, W)
    TL = _tile(L, (512, 256))
    CH = TL // 8
    NB = 2 if nkb % 2 == 0 else 1
    CK = min(S5_CHUNK, W2)

    def body(h_ref, tb_ref, tct_ref, pw_ref, d_ref, s_ref, y_ref, z_ref, bw, cw, perm, unperm, carry):
        t = pl.program_id(1)

        @pl.when(t == 0)
        def _():
            carry[...] = jnp.zeros_like(carry)
            for b in range(NB):
                bw[b] = _expand(tb_ref[b], dims, False)
                cw[b] = _expand(tct_ref[b], dims, True)
            perm[...] = _chunk_order(TL, CH, False).astype(perm.dtype)
            unperm[...] = _chunk_order(TL, CH, True).astype(perm.dtype)

        hp = _reorder(perm[...], h_ref[...])
        hpb = hp.astype(BF)
        first = lax.broadcasted_iota(jnp.int32, (8, W), 0) == 0

        def project(b):
            def chunk(c):
                def emit():
                    s_ref[:, b * W2 + c:b * W2 + c + CK] = jnp.dot(hpb[:, b * KB:(b + 1) * KB], bw[b, :, c:c + CK],
                                                                   preferred_element_type=F32)
                return emit
            return [chunk(c) for c in range(0, W2, CK)]

        def scan(b):
            re, im = slice(b * W2, b * W2 + W), slice(b * W2 + W, (b + 1) * W2)
            ar, ai = pw_ref[b, 0:8, :W], pw_ref[b, 0:8, W:]
            st = {"x": (jnp.zeros((8, W), F32), jnp.zeros((8, W), F32))}

            def own(j):
                def emit():
                    rows = slice(j * 8, j * 8 + 8)
                    xr, xi = _cmul_add(s_ref[rows, re], s_ref[rows, im], ar, ai, *st["x"])
                    s_ref[rows, re] = xr
                    s_ref[rows, im] = xi
                    st["x"] = (xr, xi)
                return emit

            def ends():
                xr, xi = st["x"]
                for k, off in ((1, 8), (2, 16), (4, 24)):
                    xr, xi = _cmul_add(xr, xi, pw_ref[b, off:off + 8, :W], pw_ref[b, off:off + 8, W:],
                                       pltpu.roll(xr, k, 0), pltpu.roll(xi, k, 0))
                xr, xi = _cmul_add(xr, xi, pw_ref[b, 32:40, :W], pw_ref[b, 32:40, W:], carry[b, 0], carry[b, 1])
                st["c"] = (jnp.where(first, carry[b, 0], pltpu.roll(xr, 1, 0)),
                           jnp.where(first, carry[b, 1], pltpu.roll(xi, 1, 0)))
                carry[b, 0] = jnp.broadcast_to(xr[7:8], (8, W))
                carry[b, 1] = jnp.broadcast_to(xi[7:8], (8, W))

            def carried(j):
                def emit():
                    rows = slice(j * 8, j * 8 + 8)
                    cr, ci = _cmul(ar, ai, *st["c"])
                    s_ref[rows, re] = s_ref[rows, re] + cr
                    s_ref[rows, im] = s_ref[rows, im] + ci
                    st["c"] = (cr, ci)
                return emit

            return [own(j) for j in range(CH)] + [ends] + [carried(j) for j in range(CH)]

        def readout(b):
            cols = slice(b * KB, (b + 1) * KB)
            acc = {}

            def chunk(c):
                def emit():
                    part = jnp.dot(s_ref[:, b * W2 + c:b * W2 + c + CK].astype(BF), cw[b, c:c + CK, :],
                                   preferred_element_type=F32)
                    acc["y"] = part if c == 0 else acc["y"] + part
                return emit

            def finish():
                y = acc["y"] + d_ref[:, cols] * hp[:, cols]
                y_ref[:, cols] = y
                z_ref[:, cols] = jnp.dot(unperm[...], _gelu(y).astype(BF),
                                         preferred_element_type=F32).astype(z_ref.dtype)

            return [chunk(c) for c in range(0, W2, CK)] + [finish]

        for emit in project(0):
            emit()
        for b in range(NB):
            side = (project(b + 1) if b + 1 < NB else []) + (readout(b - 1) if b > 0 else [])
            _interleave(scan(b), side)
        for emit in readout(NB - 1):
            emit()

    blk = lambda kb, t: (t, kb)
    per_kb = lambda kb, t: (kb, 0, 0)
    return pl.pallas_call(
        body, name=name, grid=(nkb // NB, L // TL),
        in_specs=[pl.BlockSpec((TL, NB * KB), blk), pl.BlockSpec((NB, KB, P2), per_kb),
                  pl.BlockSpec((NB, P2, KB), per_kb), pl.BlockSpec((NB, 40, W2), per_kb),
                  pl.BlockSpec((1, NB * KB), lambda kb, t: (0, kb))],
        out_specs=[pl.BlockSpec((TL, NB * W2), blk), pl.BlockSpec((TL, NB * KB), blk),
                   pl.BlockSpec((TL, NB * KB), blk)],
        out_shape=[jax.ShapeDtypeStruct((L, nkb * W2), F32), jax.ShapeDtypeStruct((L, D), F32),
                   jax.ShapeDtypeStruct((L, D), BF)],
        scratch_shapes=[pltpu.VMEM((NB, KB, W2), BF), pltpu.VMEM((NB, W2, KB), BF), pltpu.VMEM((TL, TL), BF),
                        pltpu.VMEM((TL, TL), BF), pltpu.VMEM((NB, 2, 8, W), F32)],
        compiler_params=_cp("parallel", "arbitrary"))(h, tb, tct, pw, dvec)


def s5_bwd(dz, y, h, s, tc, tbt, pwr, dvec, name):
    L, D = h.shape
    nkb, KB, P2 = tc.shape
    P = P2 // 2
    W = (KB // SSM_GROUP) * P
    W2 = 2 * W
    dims = (KB, W2, SSM_GROUP, P, W)
    TL = _tile(L, (512, 256))
    CH = TL // 8
    nt = L // TL
    NB = 2 if nkb % 2 == 0 else 1
    CK = min(S5_CHUNK, W2)
    tn = (((0,), (0,)), ((), ()))

    def body(dz_ref, y_ref, h_ref, s_ref, sp_ref, tc_ref, tbt_ref, pw_ref, d_ref,
             dh_ref, dd_ref, da_ref, db_ref, dc_ref, g, ctw, btw, dbacc, dcacc, dys, perm, unperm, carry):
        t = pl.program_id(1)

        @pl.when(t == 0)
        def _():
            carry[...] = jnp.zeros_like(carry)
            dd_ref[...] = jnp.zeros_like(dd_ref)
            da_ref[...] = jnp.zeros_like(da_ref)
            dbacc[...] = jnp.zeros_like(dbacc)
            dcacc[...] = jnp.zeros_like(dcacc)
            for b in range(NB):
                ctw[b] = _expand(tc_ref[b], dims, False)
                btw[b] = _expand(tbt_ref[b], dims, True)
            perm[...] = _chunk_order(TL, CH, False).astype(perm.dtype)
            unperm[...] = _chunk_order(TL, CH, True).astype(perm.dtype)

        hp = jnp.dot(perm[...], h_ref[...].astype(BF), preferred_element_type=F32)
        dy = jnp.dot(perm[...], dz_ref[...].astype(BF), preferred_element_type=F32) * _gelu_grad(y_ref[...])
        dd_ref[...] += _rowsum8(dy * hp)
        dys[...] = dy
        dyb = dy.astype(BF)
        hpb = hp.astype(BF)
        sub = lax.broadcasted_iota(jnp.int32, (8, W), 0)
        live = jnp.where(t == nt - 1, 0.0, 1.0)

        def lead(b):
            cols = slice(b * KB, (b + 1) * KB)

            def to_states(c):
                def emit():
                    g[b, :, c:c + CK] = jnp.dot(dyb[:, cols], ctw[b, :, c:c + CK], preferred_element_type=F32)
                return emit

            def d_c(c):
                def emit():
                    dcacc[b, :, c:c + CK] += lax.dot_general(dyb[:, cols],
                                                             s_ref[:, b * W2 + c:b * W2 + c + CK].astype(BF), tn,
                                                             preferred_element_type=F32)
                return emit

            return [f(c) for c in range(0, W2, CK) for f in (to_states, d_c)]

        def scan(b):
            re, im = slice(b * W2, b * W2 + W), slice(b * W2 + W, (b + 1) * W2)
            ar, ai = pw_ref[b, 0:8, :W], pw_ref[b, 0:8, W:]
            zero = jnp.zeros((8, W), F32)
            st = {"g": (zero, zero), "acc": (zero, zero)}

            def own(j):
                def emit():
                    rows = slice(j * 8, j * 8 + 8)
                    gr, gi = _cmul_add(g[b, rows, :W], g[b, rows, W:], ar, ai, *st["g"])
                    g[b, rows, :W] = gr
                    g[b, rows, W:] = gi
                    st["g"] = (gr, gi)
                return emit

            def ends():
                gr, gi = st["g"]
                for k, off in ((1, 8), (2, 16), (4, 24)):
                    gr, gi = _cmul_add(gr, gi, pw_ref[b, off:off + 8, :W], pw_ref[b, off:off + 8, W:],
                                       pltpu.roll(gr, 8 - k, 0), pltpu.roll(gi, 8 - k, 0))
                gr, gi = _cmul_add(gr, gi, pw_ref[b, 32:40, :W], pw_ref[b, 32:40, W:], carry[b, 0], carry[b, 1])
                st["c"] = (jnp.where(sub == 7, carry[b, 0], pltpu.roll(gr, 7, 0)),
                           jnp.where(sub == 7, carry[b, 1], pltpu.roll(gi, 7, 0)))
                carry[b, 0] = jnp.broadcast_to(gr[0:1], (8, W))
                carry[b, 1] = jnp.broadcast_to(gi[0:1], (8, W))

            def carried(j):
                def emit():
                    rows = slice(j * 8, j * 8 + 8)
                    cr, ci = _cmul(ar, ai, *st["c"])
                    gr, gi = g[b, rows, :W] + cr, g[b, rows, W:] + ci
                    g[b, rows, :W] = gr
                    g[b, rows, W:] = gi
                    if j > 0:
                        before = slice(j * 8 - 8, j * 8)
                        pr, pi = s_ref[before, re], s_ref[before, im]
                    else:
                        last = slice(TL - 8, TL)
                        pr = jnp.where(sub == 0, sp_ref[7:8, re] * live, pltpu.roll(s_ref[last, re], 1, 0))
                        pi = jnp.where(sub == 0, sp_ref[7:8, im] * live, pltpu.roll(s_ref[last, im], 1, 0))
                    accr, acci = st["acc"]
                    st["c"] = (cr, ci)
                    st["acc"] = (accr + pr * gr + pi * gi, acci + pr * gi - pi * gr)
                return emit

            def done():
                da_ref[b, :, :W] += st["acc"][0]
                da_ref[b, :, W:] += st["acc"][1]

            return ([own(j) for j in reversed(range(CH))] + [ends] + [carried(j) for j in reversed(range(CH))]
                    + [done])

        def tail(b):
            cols = slice(b * KB, (b + 1) * KB)
            acc = {}

            def d_u(c):
                def emit():
                    part = jnp.dot(g[b, :, c:c + CK].astype(BF), btw[b, c:c + CK, :], preferred_element_type=F32)
                    acc["u"] = part if c == 0 else acc["u"] + part
                return emit

            def d_b(c):
                def emit():
                    dbacc[b, :, c:c + CK] += lax.dot_general(hpb[:, cols], g[b, :, c:c + CK].astype(BF), tn,
                                                             preferred_element_type=F32)
                return emit

            def finish():
                dh = (dys[:, cols] * d_ref[:, cols] + acc["u"]).astype(BF)
                dh_ref[:, cols] = jnp.dot(unperm[...], dh, preferred_element_type=F32).astype(dh_ref.dtype)

            return [f(c) for c in range(0, W2, CK) for f in (d_u, d_b)] + [finish]

        for emit in lead(0):
            emit()
        for b in range(NB):
            side = (lead(b + 1) if b + 1 < NB else []) + (tail(b - 1) if b > 0 else [])
            _interleave(scan(b), side)
        for emit in tail(NB - 1):
            emit()

        @pl.when(t == nt - 1)
        def _():
            for b in range(NB):
                db_ref[b] = _extract(dbacc[b], dims)
                dc_ref[b] = _extract(dcacc[b], dims)

    rev = lambda kb, t: (nt - 1 - t, kb)
    prev = lambda kb, t: (jnp.maximum((nt - 1 - t) * CH - 1, 0), kb)
    per_kb = lambda kb, t: (kb, 0, 0)
    return pl.pallas_call(
        body, name=name, grid=(nkb // NB, nt),
        in_specs=[pl.BlockSpec((TL, NB * KB), rev), pl.BlockSpec((TL, NB * KB), rev),
                  pl.BlockSpec((TL, NB * KB), rev), pl.BlockSpec((TL, NB * W2), rev),
                  pl.BlockSpec((8, NB * W2), prev), pl.BlockSpec((NB, KB, P2), per_kb),
                  pl.BlockSpec((NB, P2, KB), per_kb), pl.BlockSpec((NB, 40, W2), per_kb),
                  pl.BlockSpec((1, NB * KB), lambda kb, t: (0, kb))],
        out_specs=[pl.BlockSpec((TL, NB * KB), rev), pl.BlockSpec((8, NB * KB), lambda kb, t: (0, kb)),
                   pl.BlockSpec((NB, 8, W2), per_kb), pl.BlockSpec((NB, KB, P2), per_kb),
                   pl.BlockSpec((NB, KB, P2), per_kb)],
        out_shape=[jax.ShapeDtypeStruct((L, D), BF), jax.ShapeDtypeStruct((8, D), F32),
                   jax.ShapeDtypeStruct((nkb, 8, W2), F32), jax.ShapeDtypeStruct((nkb, KB, P2), F32),
                   jax.ShapeDtypeStruct((nkb, KB, P2), F32)],
        scratch_shapes=[pltpu.VMEM((NB, TL, W2), F32), pltpu.VMEM((NB, KB, W2), BF), pltpu.VMEM((NB, W2, KB), BF),
                        pltpu.VMEM((NB, KB, W2), F32), pltpu.VMEM((NB, KB, W2), F32), pltpu.VMEM((TL, NB * KB), F32),
                        pltpu.VMEM((TL, TL), BF), pltpu.VMEM((TL, TL), BF), pltpu.VMEM((NB, 2, 8, W), F32)],
        compiler_params=pltpu.CompilerParams(dimension_semantics=("parallel", "arbitrary"),
                                             vmem_limit_bytes=V7X_VMEM_BYTES - 4 * 1024 * 1024),
    )(dz, y, h, s, s, tc, tbt, pwr, dvec)


def _discretise(a_re, a_im, log_step, b_re, b_im):
    lr = jnp.minimum(a_re, -1e-4)
    li = a_im
    dt = jnp.exp(log_step)[:, None]
    mag = jnp.exp(lr * dt)
    abr = mag * jnp.cos(li * dt)
    abi = mag * jnp.sin(li * dt)
    den = lr * lr + li * li
    qr = ((abr - 1.0) * lr + abi * li) / den
    qi = (abi * lr - (abr - 1.0) * li) / den
    bbar_re = qr[..., None] * b_re - qi[..., None] * b_im
    bbar_im = qr[..., None] * b_im + qi[..., None] * b_re
    return abr, abi, bbar_re, bbar_im


def _compact(m_re, m_im, nkb):
    G, H, P = m_re.shape
    t = jnp.stack([m_re, m_im], axis=2).reshape(nkb, (G // nkb) * H, 2 * P).astype(BF)
    return t, jnp.swapaxes(t, 1, 2)


def _scan_powers(abr, abi, nkb, conj, CH):
    G, P = abr.shape
    if conj:
        abi = -abi

    def cmul(u, v):
        return u[0] * v[0] - u[1] * v[1], u[0] * v[1] + u[1] * v[0]

    q = (abr, abi)
    for _ in range(_log2(CH)):
        q = cmul(q, q)
    pows = [q]
    for _ in range(7):
        pows.append(cmul(pows[-1], q))
    row = jnp.arange(8)[:, None, None]

    def table(part):
        out = [jnp.broadcast_to((abr, abi)[part][None], (8, G, P))]
        for k in (1, 2, 4):
            keep = (row <= 7 - k) if conj else (row >= k)
            out.append(jnp.where(keep, pows[k - 1][part][None], 0.0))
        ends = jnp.stack([p[part] for p in pows])
        out.append(ends[::-1] if conj else ends)
        return jnp.concatenate(out, axis=0)

    GL = G // nkb
    t = jnp.stack([table(0), table(1)], axis=1)
    t = t.reshape(40, 2, nkb, GL * P).transpose(2, 0, 1, 3)
    return t.reshape(nkb, 40, 2 * GL * P)


def ada_mods(c_all, w_ada, b_sh, name):
    nl, D, NA = w_ada.shape

    def body(c_ref, w_ref, b_ref, o_ref):
        cv = c_ref[...]
        act = cv * jax.nn.sigmoid(cv)
        o_ref[...] = jnp.dot(act, w_ref[...], preferred_element_type=F32, precision=lax.Precision.HIGHEST) + b_ref[...]

    return pl.pallas_call(
        body, name=name, grid=(nl,),
        in_specs=[pl.BlockSpec((8, D), lambda i: (0, 0)), pl.BlockSpec((None, D, NA), lambda i: (i, 0, 0)),
                  pl.BlockSpec((None, 1, NA), lambda i: (i, 0, 0))],
        out_specs=pl.BlockSpec((None, 8, NA), lambda i: (i, 0, 0)),
        out_shape=jax.ShapeDtypeStruct((nl, 8, NA), F32), compiler_params=_cp("parallel"))(c_all, w_ada, b_sh)


def _adamw(w, g, m, v):
    m = ADAM_B1 * m + (1.0 - ADAM_B1) * g
    v = ADAM_B2 * v + (1.0 - ADAM_B2) * (g * g)
    m_hat = m / (1.0 - ADAM_B1 ** ADAM_STEP)
    v_hat = v / (1.0 - ADAM_B2 ** ADAM_STEP)
    return -ADAM_LR * (m_hat / (jnp.sqrt(v_hat) + ADAM_EPS) + ADAM_WD * w), m, v


def _adam_rows(R, C):
    cap = max(8, (256 * 1024) // C)
    for t in range(min(R, cap), 0, -1):
        if R % t == 0 and (t % 8 == 0 or t == R):
            return t
    return R


def adamw_ada(c_t, dm, w, m, v, name):
    nl, D, NA = w.shape
    TK = _tile(D, (256, 128))

    def body(c_ref, dm_ref, w_ref, m_ref, v_ref, g_ref, d_ref, nm_ref, nv_ref):
        cv = c_ref[...]
        act = cv * jax.nn.sigmoid(cv)
        g = jnp.dot(act, dm_ref[...], preferred_element_type=F32, precision=lax.Precision.HIGHEST)
        g_ref[...] = g
        d_ref[...], nm_ref[...], nv_ref[...] = _adamw(w_ref[...], g, m_ref[...], v_ref[...])

    big = pl.BlockSpec((None, TK, NA), lambda i, k: (i, k, 0))
    shape = jax.ShapeDtypeStruct(w.shape, F32)
    return pl.pallas_call(
        body, name=name, grid=(nl, D // TK),
        in_specs=[pl.BlockSpec((TK, 8), lambda i, k: (k, 0)), pl.BlockSpec((None, 8, NA), lambda i, k: (i, 0, 0)),
                  big, big, big],
        out_specs=[big] * 4, out_shape=[shape] * 4, compiler_params=_cp("parallel", "parallel"))(c_t, dm, w, m, v)


def adamw_sharded(w, m, v, ga, gb, name):
    nl, R, C = w.shape
    TR = _adam_rows(R, C)

    def body(w_ref, m_ref, v_ref, a_ref, b_ref, g_ref, d_ref, nm_ref, nv_ref):
        g = a_ref[...] + b_ref[...]
        g_ref[...] = g
        d_ref[...], nm_ref[...], nv_ref[...] = _adamw(w_ref[...], g, m_ref[...], v_ref[...])

    big = pl.BlockSpec((None, TR, C), lambda i, r: (i, r, 0))
    shape = jax.ShapeDtypeStruct(w.shape, F32)
    return pl.pallas_call(
        body, name=name, grid=(nl, R // TR), in_specs=[big] * 5,
        out_specs=[big] * 4, out_shape=[shape] * 4, compiler_params=_cp("parallel", "parallel"))(w, m, v, ga, gb)


def adamw_slab(g, w, m, v, name):
    R, C = g.shape
    TR = _tile(R, (160, 80, 40, 8))

    def body(g_ref, w_ref, m_ref, v_ref, d_ref, nm_ref, nv_ref):
        d_ref[...], nm_ref[...], nv_ref[...] = _adamw(w_ref[...], g_ref[...], m_ref[...], v_ref[...])

    big = pl.BlockSpec((TR, C), lambda r: (r, 0))
    shape = jax.ShapeDtypeStruct((R, C), F32)
    return pl.pallas_call(
        body, name=name, grid=(R // TR,), in_specs=[big] * 4,
        out_specs=[big] * 3, out_shape=[shape] * 3, compiler_params=_cp("parallel"))(g, w, m, v)


def adamw_plain(w, m, v, g, name):
    def body(w_ref, m_ref, v_ref, g_ref, d_ref, nm_ref, nv_ref):
        d_ref[...], nm_ref[...], nv_ref[...] = _adamw(w_ref[...], g_ref[...], m_ref[...], v_ref[...])

    shape = jax.ShapeDtypeStruct(w.shape, F32)
    return pl.pallas_call(body, name=name, out_shape=[shape] * 3,
                          compiler_params=pltpu.CompilerParams(vmem_limit_bytes=VMEM_LIMIT))(w, m, v, g)


def _slab_rows(a):
    n = a.size
    rows = -(-n // SLAB_W)
    return -(-rows // 8) * 8


def _pack(arrs, pad_rows_to=0):
    out = []
    for a in arrs:
        rows = _slab_rows(a)
        flat = a.reshape(-1).astype(F32)
        flat = jnp.pad(flat, (0, rows * SLAB_W - flat.shape[0]))
        out.append(flat.reshape(rows, SLAB_W))
    total = sum(o.shape[0] for o in out)
    if pad_rows_to and total % pad_rows_to:
        out.append(jnp.zeros((pad_rows_to - total % pad_rows_to, SLAB_W), F32))
    return jnp.concatenate(out, axis=0)


def _unpack(slab, like):
    out, r = [], 0
    for a in like:
        rows = _slab_rows(a)
        out.append(slab[r:r + rows].reshape(-1)[:a.size].reshape(a.shape))
        r += rows
    return out


WEIGHTS = ['norm1_g', 'norm2_g', 'w_ada', 'b_ada', 'ssm_a_re', 'ssm_a_im', 'ssm_log_step', 'ssm_b_re', 'ssm_b_im',
           'ssm_c_re', 'ssm_c_im', 'ssm_d', 'ssm_w_out', 'conv_w_in', 'conv_w', 'conv_w_out', 'w_ffn_in',
           'w_ffn_out', 'final_g']
SLAB = ['norm1_g', 'norm2_g', 'b_ada', 'ssm_a_re', 'ssm_a_im', 'ssm_log_step', 'ssm_b_re', 'ssm_b_im', 'ssm_c_re',
        'ssm_c_im', 'ssm_d', 'final_g']
SHARDED = ['ssm_w_out', 'conv_w_in', 'conv_w_out', 'w_ffn_in', 'w_ffn_out']


def kernel(x, c, norm1_g, norm2_g, w_ada, b_ada, ssm_a_re, ssm_a_im, ssm_log_step, ssm_b_re, ssm_b_im, ssm_c_re, ssm_c_im, ssm_d, ssm_w_out, conv_w_in, conv_w, conv_w_out, w_ffn_in, w_ffn_out, final_g, loss_target, m_norm1_g, m_norm2_g, m_w_ada, m_b_ada, m_ssm_a_re, m_ssm_a_im, m_ssm_log_step, m_ssm_b_re, m_ssm_b_im, m_ssm_c_re, m_ssm_c_im, m_ssm_d, m_ssm_w_out, m_conv_w_in, m_conv_w, m_conv_w_out, m_w_ffn_in, m_w_ffn_out, m_final_g, v_norm1_g, v_norm2_g, v_w_ada, v_b_ada, v_ssm_a_re, v_ssm_a_im, v_ssm_log_step, v_ssm_b_re, v_ssm_b_im, v_ssm_c_re, v_ssm_c_im, v_ssm_d, v_ssm_w_out, v_conv_w_in, v_conv_w, v_conv_w_out, v_w_ffn_in, v_w_ffn_out, v_final_g):
    given = dict(locals())
    W = {n: given[n] for n in WEIGHTS}
    Mo = {n: given["m_" + n] for n in WEIGHTS}
    Vo = {n: given["v_" + n] for n in WEIGHTS}

    xs = x[0]
    tgt = loss_target[0]
    L, D = xs.shape
    nlayer = norm1_g.shape[0]
    NA = w_ada.shape[2]
    G = ssm_a_re.shape[1]
    nkb = D // S5_BLOCK
    ax, ay, ac = _axes()
    me = 4 * ax + 2 * ay + ac
    chip = 2 * ax + ay

    assert D == SLAB_W
    first = gather8(jnp.concatenate([jnp.broadcast_to(c, (8, D)), _pack([conv_w])], axis=0), "gather_c_conv_w")
    c_all = first[:, 0, :]
    b_sh = lax.dynamic_slice_in_dim(b_ada, chip * NA, NA, axis=1)[:, None, :]
    mods_part = ada_mods(c_all, w_ada, b_sh, "ada_mods")
    mg = gather8(mods_part.reshape(nlayer * 8, NA), "gather_mods")
    mg = mg.reshape(N_CHIP, 2, nlayer, 8, NA)[:, 0]
    mods_all = lax.dynamic_index_in_dim(mg, me, axis=2, keepdims=False)
    mods_all = jnp.transpose(mods_all, (1, 0, 2)).reshape(nlayer, 6, D)

    cw_parts = first[:, 8:]
    nconv = conv_w.shape[0]
    cw_full = jnp.stack([_unpack(cw_parts[2 * q], [conv_w])[0] for q in range(N_CHIP)], axis=2)
    cw_full = cw_full.reshape(nconv, 3, D)

    in_flight_w = {}

    def start_weights(i, after):
        names = (["ssm_w_out"] if i % 2 == 0 else ["conv_w_in", "conv_w_out"]) + ["w_ffn_in", "w_ffn_out"]
        shards = [W[n][i if n.startswith("w_ffn") else i // 2].astype(BF) for n in names]
        sems, srcs, lands, tok = gather_start(shards, after, "gather_start%d" % i)
        in_flight_w[i] = (names, sems, srcs, lands)
        return tok

    def relay_weights(i, after):
        names, sems, srcs, lands = in_flight_w[i]
        got = gather_wait(sems, srcs, lands, list(range(len(names))), after, "gather_wait%d" % i)
        rsems, rlands, tok = relay_start(got, after, "relay_start%d" % i)
        in_flight_w[i] = (names, rsems, rlands)
        return tok

    def layer_weights(i, after):
        names, rsems, rlands = in_flight_w[i]
        return dict(zip(names, relay_wait(rsems, rlands, after, "relay_wait%d" % i)))

    token = start_weights(0, cw_full + mods_all[0, 0:3])
    mods_all = mods_all + token[0:1, 0:1]

    s5 = []
    for j in range(ssm_a_re.shape[0]):
        disc, disc_vjp = jax.vjp(_discretise, ssm_a_re[j], ssm_a_im[j], ssm_log_step[j], ssm_b_re[j], ssm_b_im[j])
        abr, abi, bbar_re, bbar_im = disc
        tb, tbt = _compact(jnp.swapaxes(bbar_re, 1, 2), jnp.swapaxes(bbar_im, 1, 2), nkb)
        tc, tct = _compact(ssm_c_re[j], -ssm_c_im[j], nkb)
        chunk = _tile(L, (512, 256)) // 8
        s5.append(dict(vjp=disc_vjp, tb=tb, tbt=tbt, tc=tc, tct=tct, pw=_scan_powers(abr, abi, nkb, False, chunk),
                       pwr=_scan_powers(abr, abi, nkb, True, chunk)))

    saved = []
    xcur = xs
    for i in range(nlayer):
        j = i // 2
        mods = mods_all[i]
        sv = dict(x=xcur)
        if i % 2 == 0:
            h = norm_mod(xcur, norm1_g[i:i + 1], mods, 0, F32, "norm_mod_s5")
            dvec = ssm_d[j:j + 1]
            if i == 0:
                dvec = dvec + start_weights(1, h)[0:1, 0:1]
            states, yv, z = s5_fwd(h, s5[j]["tb"], s5[j]["tct"], s5[j]["pw"], dvec, "s5_fwd")
            if i == 0:
                mods = mods + relay_weights(0, z)[0:1, 0:1]
            full = layer_weights(i, z)
            o, mix, x2 = ssm_out_glu(z, full["ssm_w_out"], xcur, mods, 2, "ssm_out_glu")
            sv.update(h=h, states=states, y=yv, z=z, o=o)
        else:
            h = norm_mod(xcur, norm1_g[i:i + 1], mods, 0, BF, "norm_mod")
            full = layer_weights(i, h)
            p = mm_nn(h, full["conv_w_in"], BF, "mm_conv_in")
            mc = conv_fwd(p, cw_full[j], "conv_fwd")
            mix, x2 = mm_nn(mc, full["conv_w_out"].reshape(1, D, D), BF, "mm_conv_out", res=xcur, gate=mods[2:3])
            sv.update(h=h, p=p, mc=mc)
        h2 = norm_mod(x2, norm2_g[i:i + 1], mods, 3, BF, "norm_mod")
        gu, act = ffn_in_act(h2, full["w_ffn_in"], "ffn_in_act")
        F = act.shape[1]
        ff, x3 = mm_nn(act, full["w_ffn_out"].reshape(1, F, D), BF, "mm_ffn_out", res=x2, gate=mods[5:6])
        sv.update(mix=mix, x2=x2, h2=h2, gu=gu, act=act, ff=ff, w=full)
        saved.append(sv)
        xcur = x3
        if i + 1 < nlayer:
            token = relay_weights(i + 1, ff)
            if i + 2 < nlayer:
                token = token + start_weights(i + 2, token)
            mods_all = mods_all + token[0:1, 0:1]

    loss_blk, dx, dfinal, dff = final_loss(xcur, tgt, final_g[None, :], saved[-1]["ff"], mods_all[nlayer - 1], 5,
                                           "final_loss")
    dg2 = dfinal[1:2]

    gland = {n: lax.empty((W[n].shape[0], N_CHIP) + W[n].shape[1:], BF) for n in SHARDED}
    in_flight = []
    dmods = [None] * nlayer
    dnorm1, dnorm2 = [None] * nlayer, [None] * nlayer
    dconv_w = [None] * nconv
    ds5 = [None] * ssm_a_re.shape[0]
    token = jnp.zeros((8, 128), F32)

    def send_grads(names, grads, slot, after, name):
        sems, thru, lands, tok = scatter_start([grads[n] for n in names], [gland[n] for n in names], slot, after, name)
        gland.update(zip(names, lands))
        in_flight.append((names, slot, sems, thru, name))
        return tok

    def land_grads(group, after):
        for names, slot, sems, thru, name in in_flight:
            if names[0] in group:
                got = scatter_wait(sems, thru, [gland[n] for n in names], slot, after, name.replace("scatter", "landed"))
                gland.update(zip(names, got))

    for i in reversed(range(nlayer)):
        j = i // 2
        mods = mods_all[i] + token[0:1, 0:1]
        sv = saved[i]
        full = sv["w"]
        gfull = {}
        F = sv["act"].shape[1]
        gfull["w_ffn_out"] = mm_tn(sv["act"], dff, 1, "mm_tn_ffn_out").reshape(N_CHIP, F // N_CHIP, D)
        dgu = ffn_out_bwd(dff, full["w_ffn_out"].reshape(F, D), sv["gu"], "ffn_out_bwd")
        gfull["w_ffn_in"] = mm_tn(sv["h2"], dgu, N_CHIP, "mm_tn_ffn_in")
        dh2 = mm_nt(dgu, full["w_ffn_in"], BF, "mm_nt_ffn_in")
        token = send_grads(["w_ffn_out", "w_ffn_in"], gfull, [i, i], dh2, "scatter_ffn%d" % i)
        mods = mods + token[0:1, 0:1]
        dx2, s2, dmix = norm_bwd(dh2, sv["x2"], dx, norm2_g[i:i + 1], mods, 3, "norm_bwd_mix",
                                 branch=(sv["mix"], mods, 2))
        dg1 = s2[3:4]
        if i % 2 == 0:
            do = glu_bwd(dmix, sv["o"], "glu_bwd")
            gfull["ssm_w_out"] = mm_tn(sv["z"], do, N_CHIP, "mm_tn_ssm_out")
            dz = mm_nt(do, full["ssm_w_out"], BF, "mm_nt_ssm_out")
            dh, dd, dab, db, dc = s5_bwd(dz, sv["y"], sv["h"], sv["states"], s5[j]["tc"], s5[j]["tbt"], s5[j]["pwr"],
                                         ssm_d[j:j + 1], "s5_bwd")
            ds5[j] = (dd, dab, db, dc)
        else:
            gfull["conv_w_out"] = mm_tn(sv["mc"], dmix, 1, "mm_tn_conv_out").reshape(N_CHIP, D // N_CHIP, D)
            dmc = mm_nt(dmix, full["conv_w_out"].reshape(1, D, D), BF, "mm_nt_conv_out")
            dbg, dcg, dvv, dcw = conv_bwd(dmc, sv["p"], cw_full[j], "conv_bwd")
            dp = jnp.concatenate([dbg, dcg, dvv], axis=1)
            gfull["conv_w_in"] = mm_tn(sv["h"], dp, N_CHIP, "mm_tn_conv_in")
            dh = mm_nt(dp, full["conv_w_in"], BF, "mm_nt_conv_in")
            dconv_w[j] = dcw[0:3]
        dmods_i = [s2[0:2], dg2]
        if i > 0:
            dx, s1, dff = norm_bwd(dh, sv["x"], dx2, norm1_g[i:i + 1], mods, 0, "norm_bwd_ffn",
                                   branch=(saved[i - 1]["ff"], mods_all[i - 1], 5))
            dg2 = s1[3:4]
        else:
            dx, s1 = norm_bwd(dh, sv["x"], dx2, norm1_g[i:i + 1], mods, 0, "norm_bwd")
        dmods[i] = jnp.concatenate([s1[0:2], dg1] + dmods_i, axis=0).reshape(6 * D)
        dnorm1[i], dnorm2[i] = s1[2], s2[2]
        names = ["ssm_w_out"] if i % 2 == 0 else ["conv_w_out", "conv_w_in"]
        token = send_grads(names, gfull, [j] * len(names), dx, "scatter_mix%d" % i)

    small = dict(norm1_g=jnp.stack(dnorm1), norm2_g=jnp.stack(dnorm2), b_ada=jnp.stack(dmods),
                 final_g=dfinal[0] + token[0, 0])
    per = {n: [] for n in ('ssm_a_re', 'ssm_a_im', 'ssm_log_step', 'ssm_b_re', 'ssm_b_im', 'ssm_c_re', 'ssm_c_im', 'ssm_d')}
    GL = G // nkb
    for j, (dd, dab, db, dc) in enumerate(ds5):
        dab = jnp.sum(dab, axis=1).reshape(nkb, 2, GL, SSM_STATE)
        g_abr, g_abi = dab[:, 0].reshape(G, SSM_STATE), dab[:, 1].reshape(G, SSM_STATE)
        db, dc = db.reshape(G, SSM_GROUP, 2, SSM_STATE), dc.reshape(G, SSM_GROUP, 2, SSM_STATE)
        gb_re, gb_im, gc_re, gc_im = db[:, :, 0], db[:, :, 1], dc[:, :, 0], dc[:, :, 1]
        ga_re, ga_im, gls, gbr, gbi = s5[j]["vjp"]((g_abr, g_abi, jnp.swapaxes(gb_re, 1, 2), jnp.swapaxes(gb_im, 1, 2)))
        for n, val in zip(per, (ga_re, ga_im, gls, gbr, gbi, gc_re, -gc_im, jnp.sum(dd, axis=0))):
            per[n].append(val)
    small.update({n: jnp.stack(vals) for n, vals in per.items()})
    dcw_full = jnp.stack(dconv_w)

    my_loss = loss_blk[0:1, 0:1]
    slab_like = [W[n] for n in SLAB] + [dcw_full, my_loss]
    rows64 = 8 * N_DEV
    slab = _pack([small[n] for n in SLAB] + [dcw_full, my_loss], rows64)
    per_dev = slab.shape[0] // N_DEV
    x_sems, x_srcs, x_lands, token = exchange_start(
        [(slab.reshape(N_DEV, per_dev, SLAB_W), True), (_pack([small["b_ada"]]), False)], dx, "small_scatter")

    early = [n for n in SHARDED if n != "ssm_w_out"]
    land_grads(early, token)
    mine = [reduce4(gland[n], "reduce4_" + n) for n in early]

    parts, dm_all = exchange_wait(x_sems, x_srcs, x_lands, [True, False], mine[-1][0, :8, :128], "small_landed")
    t_sems, t_srcs, t_lands, token = exchange_start([(sum8(parts, "sum_small"), False)], dm_all, "small_gather")
    out = {}

    w_sems, w_srcs, w_lands, token2 = swap_start(mine, "swap_start")
    dm_all = dm_all.reshape(N_DEV, -1)[:, :b_ada.size].reshape(N_DEV, nlayer, N_CHIP, NA)
    dm_sh = jnp.transpose(lax.dynamic_index_in_dim(dm_all, chip, axis=2, keepdims=False), (1, 0, 2))
    res = adamw_ada(jnp.transpose(c_all) + token[0:1, 0:1] + token2[0:1, 0:1], dm_sh, w_ada, m_w_ada, v_w_ada,
                    "adamw_ada")
    out["g", "w_ada"], out["d", "w_ada"], out["m", "w_ada"], out["v", "w_ada"] = res

    g_slab = exchange_wait(t_sems, t_srcs, t_lands, [False], out["g", "w_ada"], "small_total")[0]
    g_slab = g_slab.reshape(slab.shape)
    d_slab, m_slab, v_slab = adamw_slab(
        g_slab, _pack([W[n] for n in SLAB] + [jnp.zeros_like(dcw_full)], rows64),
        _pack([Mo[n] for n in SLAB] + [jnp.zeros_like(dcw_full)], rows64),
        _pack([Vo[n] for n in SLAB] + [jnp.ones_like(dcw_full)], rows64), "adamw_slab")
    for k, slab in zip(("g", "d", "m", "v"), (g_slab, d_slab, m_slab, v_slab)):
        for n, val in zip(SLAB, _unpack(slab, slab_like)):
            out[k, n] = val
    g_cw = lax.dynamic_slice_in_dim(_unpack(g_slab, slab_like)[-2], chip * conv_w.shape[2], conv_w.shape[2], axis=2)
    out["g", "conv_w"] = g_cw
    out["d", "conv_w"], out["m", "conv_w"], out["v", "conv_w"] = [
        r.reshape(conv_w.shape) for r in adamw_plain(conv_w.reshape(-1, conv_w.shape[2]), m_conv_w.reshape(-1, conv_w.shape[2]),
                                                     v_conv_w.reshape(-1, conv_w.shape[2]), g_cw.reshape(-1, conv_w.shape[2]),
                                                     "adamw_conv_w")]

    mine, theirs = swap_wait(w_sems, w_srcs, w_lands, d_slab, "swap_wait")
    for n, ga, gb in zip(early, mine, theirs):
        r = adamw_sharded(W[n], Mo[n], Vo[n], ga, gb, "adamw_" + n)
        out["g", n], out["d", n], out["m", n], out["v", n] = r

    land_grads(["ssm_w_out"], out["g", "w_ffn_out"])
    ga = reduce4(gland["ssm_w_out"], "reduce4_ssm_w_out")
    gb = swap_siblings([ga], "swap_siblings")[0]
    r = adamw_sharded(ssm_w_out, m_ssm_w_out, v_ssm_w_out, ga, gb, "adamw_ssm_w_out")
    out["g", "ssm_w_out"], out["d", "ssm_w_out"], out["m", "ssm_w_out"], out["v", "ssm_w_out"] = r

    loss = _unpack(g_slab, slab_like)[-1][0, 0]
    return (loss, dx[None], *[out["g", n] for n in WEIGHTS], *[out["d", n] for n in WEIGHTS],
            *[out["m", n] for n in WEIGHTS], *[out["v", n] for n in WEIGHTS])
```

```python
import math

import jax
import jax.numpy as jnp
from jax import lax
from jax.experimental import pallas as pl
from jax.experimental.pallas import tpu as pltpu

F32 = jnp.float32
BF = jnp.bfloat16
MESH = pl.DeviceIdType.MESH
ANY = pl.BlockSpec(memory_space=pl.ANY)

N_DEV = 8
N_CHIP = 4
SSM_GROUP = 16
SSM_STATE = 64
S5_BLOCK = 256
RMS_EPS = 1e-6
ADAM_LR, ADAM_B1, ADAM_B2, ADAM_EPS, ADAM_WD, ADAM_STEP = 0.001, 0.9, 0.999, 1e-08, 0.01, 10
V7X_VMEM_BYTES = 64 * 1024 * 1024
VMEM_LIMIT = V7X_VMEM_BYTES - 12 * 1024 * 1024
SLAB_W = 1024
GELU_C = math.sqrt(2.0 / math.pi)
GELU_A = 0.044715


def _cp(*sem):
    return pltpu.CompilerParams(dimension_semantics=sem if sem else None, vmem_limit_bytes=VMEM_LIMIT)


def _tile(n, prefs):
    for p in prefs:
        if p <= n and n % p == 0:
            return p
    return n


def _sigmoid(v):
    return 0.5 * jnp.tanh(0.5 * v) + 0.5


def _axes():
    return lax.axis_index("x"), lax.axis_index("y"), lax.axis_index("c")


def _flip(v, k):
    return 1 - v if k else v


def gather8(v, name):
    R, C = v.shape

    def body(v_ref, o_ref, ssem, rsem, lsem):
        x, y, c = _axes()
        me = 4 * x + 2 * y + c
        loc = pltpu.make_async_copy(v_ref, o_ref.at[me], lsem)
        loc.start()
        copies = []
        for k in range(1, N_DEV):
            peer = (_flip(x, (k >> 2) & 1), _flip(y, (k >> 1) & 1), _flip(c, k & 1))
            cp = pltpu.make_async_remote_copy(src_ref=v_ref, dst_ref=o_ref.at[me], send_sem=ssem.at[k - 1],
                                              recv_sem=rsem.at[k - 1], device_id=peer, device_id_type=MESH)
            cp.start()
            copies.append(cp)
        for cp in copies:
            cp.wait()
        loc.wait()

    return pl.pallas_call(
        body, name=name,
        out_shape=jax.ShapeDtypeStruct((N_DEV, R, C), v.dtype),
        in_specs=[pl.BlockSpec(memory_space=pltpu.VMEM)],
        out_specs=pl.BlockSpec(memory_space=pltpu.VMEM),
        scratch_shapes=[pltpu.SemaphoreType.DMA((N_DEV - 1,)), pltpu.SemaphoreType.DMA((N_DEV - 1,)),
                        pltpu.SemaphoreType.DMA],
        compiler_params=pltpu.CompilerParams(vmem_limit_bytes=VMEM_LIMIT),
    )(v)


HBM = pl.BlockSpec(memory_space=pltpu.HBM)
SEM = pl.BlockSpec(memory_space=pltpu.SEMAPHORE)
EFFECT = pltpu.SideEffectType.DATAFLOW_SIDE_EFFECTING


def _in_hbm(a):
    return pltpu.with_memory_space_constraint(a, pltpu.HBM)


def _chip_peers(x, y, c):
    out = []
    for k in range(1, N_CHIP):
        px, py = _flip(x, k >> 1), _flip(y, k & 1)
        out.append(((px, py, c), 2 * px + py))
    return out


def _my_half(ref, c):
    rows = ref.shape[0] // 2
    return pl.ds(pl.multiple_of(c * rows, 16), rows)


def relay_start(lands, after, name):
    n = len(lands)

    def body(*refs):
        land = refs[:n]
        ssem, rsem = refs[n + 1:n + 3]
        token = refs[-1]
        x, y, c = _axes()
        for a in range(n):
            half = _my_half(land[a].at[0], c)
            for k, (_, pchip) in enumerate(_chip_peers(x, y, c)):
                pltpu.make_async_remote_copy(src_ref=land[a].at[pchip, half], dst_ref=land[a].at[pchip, half],
                                             send_sem=ssem.at[3 * a + k], recv_sem=rsem.at[3 * a + k],
                                             device_id=(x, y, 1 - c), device_id_type=MESH).start()
        token[...] = jnp.zeros_like(token)

    out_shape = ([pltpu.SemaphoreType.DMA((3 * n,)), pltpu.SemaphoreType.DMA((3 * n,))]
                 + [pltpu.HBM(l.shape, l.dtype) for l in lands] + [jax.ShapeDtypeStruct((8, 128), F32)])
    res = pl.pallas_call(
        body, name=name, out_shape=out_shape, in_specs=[HBM] * n + [ANY],
        out_specs=[SEM, SEM] + [HBM] * n + [pl.BlockSpec(memory_space=pltpu.VMEM)],
        input_output_aliases={a: 2 + a for a in range(n)},
        compiler_params=pltpu.CompilerParams(has_side_effects=EFFECT),
    )(*lands, after)
    return tuple(res[:2]), list(res[2:2 + n]), res[-1]


def relay_wait(sems, lands, after, name):
    n = len(lands)

    def body(*refs):
        land = refs[:n]
        ssem, rsem = refs[n:n + 2]
        x, y, c = _axes()
        for a in range(n):
            mine, theirs = _my_half(land[a].at[0], c), _my_half(land[a].at[0], 1 - c)
            for k, (_, pchip) in enumerate(_chip_peers(x, y, c)):
                cp = pltpu.make_async_remote_copy(src_ref=land[a].at[pchip, mine], dst_ref=land[a].at[pchip, theirs],
                                                  send_sem=ssem.at[3 * a + k], recv_sem=rsem.at[3 * a + k],
                                                  device_id=(x, y, 1 - c), device_id_type=MESH)
                cp.wait_send()
                cp.wait_recv()

    res = pl.pallas_call(
        body, name=name, out_shape=[pltpu.HBM(l.shape, l.dtype) for l in lands],
        in_specs=[HBM] * n + [SEM, SEM, ANY], out_specs=[HBM] * n,
        input_output_aliases={a: a for a in range(n)},
        compiler_params=pltpu.CompilerParams(has_side_effects=EFFECT),
    )(*lands, *sems, after)
    return list(res)


def gather_start(shards, after, name):
    n = len(shards)

    def body(*refs):
        src, land = refs[:n], refs[n:2 * n]
        ssem, rsem, lsem = refs[2 * n + 1:2 * n + 4]
        token = refs[-1]
        x, y, c = _axes()
        chip = 2 * x + y
        for a in range(n):
            pltpu.make_async_copy(src[a], land[a].at[chip], lsem.at[a]).start()
            half = _my_half(src[a], c)
            for k, (peer, _) in enumerate(_chip_peers(x, y, c)):
                pltpu.make_async_remote_copy(src_ref=src[a].at[half], dst_ref=land[a].at[chip, half],
                                             send_sem=ssem.at[3 * a + k], recv_sem=rsem.at[3 * a + k],
                                             device_id=peer, device_id_type=MESH).start()
        token[...] = jnp.zeros_like(token)

    lands = [lax.empty((N_CHIP,) + s.shape, s.dtype) for s in shards]
    out_shape = ([pltpu.SemaphoreType.DMA((3 * n,)), pltpu.SemaphoreType.DMA((3 * n,)), pltpu.SemaphoreType.DMA((n,))]
                 + [pltpu.HBM(s.shape, s.dtype) for s in shards] + [pltpu.HBM(l.shape, l.dtype) for l in lands]
                 + [jax.ShapeDtypeStruct((8, 128), F32)])
    res = pl.pallas_call(
        body, name=name, out_shape=out_shape, in_specs=[HBM] * (2 * n) + [ANY],
        out_specs=[SEM, SEM, SEM] + [HBM] * (2 * n) + [pl.BlockSpec(memory_space=pltpu.VMEM)],
        input_output_aliases={a: 3 + a for a in range(2 * n)},
        compiler_params=pltpu.CompilerParams(has_side_effects=EFFECT),
    )(*[_in_hbm(s) for s in shards], *[_in_hbm(l) for l in lands], after)
    return tuple(res[:3]), list(res[3:3 + n]), list(res[3 + n:3 + 2 * n]), res[-1]


def gather_wait(sems, srcs, lands, idx, after, name):
    m = len(idx)

    def body(*refs):
        src, land = refs[:m], refs[m:2 * m]
        ssem, rsem, lsem = refs[2 * m:2 * m + 3]
        x, y, c = _axes()
        chip = 2 * x + y
        for j, a in enumerate(idx):
            half = _my_half(src[j], c)
            for k, (peer, pchip) in enumerate(_chip_peers(x, y, c)):
                cp = pltpu.make_async_remote_copy(src_ref=src[j].at[half], dst_ref=land[j].at[pchip, half],
                                                  send_sem=ssem.at[3 * a + k], recv_sem=rsem.at[3 * a + k],
                                                  device_id=peer, device_id_type=MESH)
                cp.wait_send()
                cp.wait_recv()
            pltpu.make_async_copy(src[j], land[j].at[chip], lsem.at[a]).wait()

    s_in = [srcs[a] for a in idx]
    l_in = [lands[a] for a in idx]
    res = pl.pallas_call(
        body, name=name,
        out_shape=[pltpu.HBM(s.shape, s.dtype) for s in s_in] + [pltpu.HBM(l.shape, l.dtype) for l in l_in],
        in_specs=[HBM] * (2 * m) + [SEM, SEM, SEM, ANY], out_specs=[HBM] * (2 * m),
        input_output_aliases={a: a for a in range(2 * m)},
        compiler_params=pltpu.CompilerParams(has_side_effects=EFFECT),
    )(*s_in, *l_in, *sems, after)
    return list(res[m:])


def scatter_start(grads, lands, slot, after, name):
    n = len(grads)

    def body(*refs):
        src, land = refs[:n], refs[n:2 * n]
        ssem, rsem, lsem = refs[2 * n + 1:2 * n + 4]
        token = refs[-1]
        x, y, c = _axes()
        chip = 2 * x + y
        for a in range(n):
            pltpu.make_async_copy(src[a].at[chip], land[a].at[slot[a], chip], lsem.at[a]).start()
            for k, (peer, pchip) in enumerate(_chip_peers(x, y, c)):
                pltpu.make_async_remote_copy(src_ref=src[a].at[pchip], dst_ref=land[a].at[slot[a], chip],
                                             send_sem=ssem.at[3 * a + k], recv_sem=rsem.at[3 * a + k],
                                             device_id=peer, device_id_type=MESH).start()
        token[...] = jnp.zeros_like(token)

    out_shape = ([pltpu.SemaphoreType.DMA((3 * n,)), pltpu.SemaphoreType.DMA((3 * n,)), pltpu.SemaphoreType.DMA((n,))]
                 + [pltpu.HBM(g.shape, g.dtype) for g in grads] + [pltpu.HBM(l.shape, l.dtype) for l in lands]
                 + [jax.ShapeDtypeStruct((8, 128), F32)])
    res = pl.pallas_call(
        body, name=name, out_shape=out_shape, in_specs=[HBM] * (2 * n) + [ANY],
        out_specs=[SEM, SEM, SEM] + [HBM] * (2 * n) + [pl.BlockSpec(memory_space=pltpu.VMEM)],
        input_output_aliases={a: 3 + a for a in range(2 * n)},
        compiler_params=pltpu.CompilerParams(has_side_effects=EFFECT),
    )(*[_in_hbm(g) for g in grads], *[_in_hbm(l) for l in lands], after)
    return tuple(res[:3]), list(res[3:3 + n]), list(res[3 + n:3 + 2 * n]), res[-1]


def scatter_wait(sems, grads, lands, slot, after, name):
    n = len(grads)

    def body(*refs):
        src, land = refs[:n], refs[n:2 * n]
        ssem, rsem, lsem = refs[2 * n:2 * n + 3]
        x, y, c = _axes()
        chip = 2 * x + y
        for a in range(n):
            for k, (peer, pchip) in enumerate(_chip_peers(x, y, c)):
                cp = pltpu.make_async_remote_copy(src_ref=src[a].at[pchip], dst_ref=land[a].at[slot[a], pchip],
                                                  send_sem=ssem.at[3 * a + k], recv_sem=rsem.at[3 * a + k],
                                                  device_id=peer, device_id_type=MESH)
                cp.wait_send()
                cp.wait_recv()
            pltpu.make_async_copy(src[a].at[chip], land[a].at[slot[a], chip], lsem.at[a]).wait()

    res = pl.pallas_call(
        body, name=name,
        out_shape=[pltpu.HBM(g.shape, g.dtype) for g in grads] + [pltpu.HBM(l.shape, l.dtype) for l in lands],
        in_specs=[HBM] * (2 * n) + [SEM, SEM, SEM, ANY], out_specs=[HBM] * (2 * n),
        input_output_aliases={a: a for a in range(2 * n)},
        compiler_params=pltpu.CompilerParams(has_side_effects=EFFECT),
    )(*grads, *lands, *sems, after)
    return list(res[n:])


def reduce4(land, name):
    nl, _, R, C = land.shape
    TR = _adam_rows(R, C)

    def body(l_ref, o_ref):
        o_ref[...] = ((l_ref[0].astype(F32) + l_ref[1].astype(F32)) + l_ref[2].astype(F32)) + l_ref[3].astype(F32)

    return pl.pallas_call(
        body, name=name, grid=(nl, R // TR),
        in_specs=[pl.BlockSpec((None, N_CHIP, TR, C), lambda i, r: (i, 0, r, 0))],
        out_specs=pl.BlockSpec((None, TR, C), lambda i, r: (i, r, 0)),
        out_shape=jax.ShapeDtypeStruct((nl, R, C), F32), compiler_params=_cp("parallel", "parallel"))(land)


def swap_siblings(arrs, name):
    n = len(arrs)

    def body(*refs):
        src, dst = refs[:n], refs[n:2 * n]
        ssem, rsem = refs[2 * n:]
        x, y, c = _axes()
        cps = [pltpu.make_async_remote_copy(src_ref=src[a], dst_ref=dst[a], send_sem=ssem.at[a], recv_sem=rsem.at[a],
                                            device_id=(x, y, 1 - c), device_id_type=MESH) for a in range(n)]
        for cp in cps:
            cp.start()
        for cp in cps:
            cp.wait()

    return pl.pallas_call(
        body, name=name, out_shape=[jax.ShapeDtypeStruct(a.shape, a.dtype) for a in arrs],
        in_specs=[ANY] * n, out_specs=[ANY] * n,
        scratch_shapes=[pltpu.SemaphoreType.DMA((n,)), pltpu.SemaphoreType.DMA((n,))],
        compiler_params=pltpu.CompilerParams(vmem_limit_bytes=VMEM_LIMIT),
    )(*arrs)


def swap_start(arrs, name):
    n = len(arrs)

    def body(*refs):
        src, land = refs[:n], refs[n:2 * n]
        ssem, rsem = refs[2 * n:2 * n + 2]
        token = refs[-1]
        x, y, c = _axes()
        for a in range(n):
            pltpu.make_async_remote_copy(src_ref=src[a], dst_ref=land[a], send_sem=ssem.at[a], recv_sem=rsem.at[a],
                                         device_id=(x, y, 1 - c), device_id_type=MESH).start()
        token[...] = jnp.zeros_like(token)

    lands = [lax.empty(a.shape, a.dtype) for a in arrs]
    out_shape = ([pltpu.SemaphoreType.DMA((n,)), pltpu.SemaphoreType.DMA((n,))]
                 + [pltpu.HBM(a.shape, a.dtype) for a in arrs] * 2 + [jax.ShapeDtypeStruct((8, 128), F32)])
    res = pl.pallas_call(
        body, name=name, out_shape=out_shape, in_specs=[HBM] * (2 * n),
        out_specs=[SEM, SEM] + [HBM] * (2 * n) + [pl.BlockSpec(memory_space=pltpu.VMEM)],
        input_output_aliases={a: 2 + a for a in range(2 * n)},
        compiler_params=pltpu.CompilerParams(has_side_effects=EFFECT),
    )(*[_in_hbm(a) for a in arrs], *[_in_hbm(l) for l in lands])
    return tuple(res[:2]), list(res[2:2 + n]), list(res[2 + n:2 + 2 * n]), res[-1]


def swap_wait(sems, srcs, lands, after, name):
    n = len(srcs)

    def body(*refs):
        src, land = refs[:n], refs[n:2 * n]
        ssem, rsem = refs[2 * n:2 * n + 2]
        x, y, c = _axes()
        for a in range(n):
            cp = pltpu.make_async_remote_copy(src_ref=src[a], dst_ref=land[a], send_sem=ssem.at[a],
                                              recv_sem=rsem.at[a], device_id=(x, y, 1 - c), device_id_type=MESH)
            cp.wait_send()
            cp.wait_recv()

    res = pl.pallas_call(
        body, name=name, out_shape=[pltpu.HBM(a.shape, a.dtype) for a in srcs] * 2,
        in_specs=[HBM] * (2 * n) + [SEM, SEM, ANY], out_specs=[HBM] * (2 * n),
        input_output_aliases={a: a for a in range(2 * n)},
        compiler_params=pltpu.CompilerParams(has_side_effects=EFFECT),
    )(*srcs, *lands, *sems, after)
    return list(res[:n]), list(res[n:])


def _all_peers(x, y, c):
    out = []
    for k in range(1, N_DEV):
        px, py, pc = _flip(x, (k >> 2) & 1), _flip(y, (k >> 1) & 1), _flip(c, k & 1)
        out.append(((px, py, pc), 4 * px + 2 * py + pc))
    return out


def exchange_start(items, after, name):
    n = len(items)

    def body(*refs):
        src, land = refs[:n], refs[n:2 * n]
        ssem, rsem, lsem = refs[2 * n + 1:2 * n + 4]
        token = refs[-1]
        x, y, c = _axes()
        me = 4 * x + 2 * y + c
        for a, (_, scatter) in enumerate(items):
            pltpu.make_async_copy(src[a].at[me] if scatter else src[a], land[a].at[me], lsem.at[a]).start()
            for k, (peer, p) in enumerate(_all_peers(x, y, c)):
                pltpu.make_async_remote_copy(src_ref=src[a].at[p] if scatter else src[a], dst_ref=land[a].at[me],
                                             send_sem=ssem.at[7 * a + k], recv_sem=rsem.at[7 * a + k],
                                             device_id=peer, device_id_type=MESH).start()
        token[...] = jnp.zeros_like(token)

    srcs = [s for s, _ in items]
    lands = [lax.empty(s.shape if sc else (N_DEV,) + s.shape, s.dtype) for s, sc in items]
    out_shape = ([pltpu.SemaphoreType.DMA((7 * n,)), pltpu.SemaphoreType.DMA((7 * n,)), pltpu.SemaphoreType.DMA((n,))]
                 + [pltpu.HBM(s.shape, s.dtype) for s in srcs] + [pltpu.HBM(l.shape, l.dtype) for l in lands]
                 + [jax.ShapeDtypeStruct((8, 128), F32)])
    res = pl.pallas_call(
        body, name=name, out_shape=out_shape, in_specs=[HBM] * (2 * n) + [ANY],
        out_specs=[SEM, SEM, SEM] + [HBM] * (2 * n) + [pl.BlockSpec(memory_space=pltpu.VMEM)],
        input_output_aliases={a: 3 + a for a in range(2 * n)},
        compiler_params=pltpu.CompilerParams(has_side_effects=EFFECT),
    )(*[_in_hbm(s) for s in srcs], *[_in_hbm(l) for l in lands], after)
    return tuple(res[:3]), list(res[3:3 + n]), list(res[3 + n:3 + 2 * n]), res[-1]


def exchange_wait(sems, srcs, lands, scatter, after, name):
    n = len(srcs)

    def body(*refs):
        src, land = refs[:n], refs[n:2 * n]
        ssem, rsem, lsem = refs[2 * n:2 * n + 3]
        x, y, c = _axes()
        me = 4 * x + 2 * y + c
        for a in range(n):
            for k, (peer, p) in enumerate(_all_peers(x, y, c)):
                cp = pltpu.make_async_remote_copy(src_ref=src[a].at[p] if scatter[a] else src[a],
                                                  dst_ref=land[a].at[p], send_sem=ssem.at[7 * a + k],
                                                  recv_sem=rsem.at[7 * a + k], device_id=peer, device_id_type=MESH)
                cp.wait_send()
                cp.wait_recv()
            pltpu.make_async_copy(src[a].at[me] if scatter[a] else src[a], land[a].at[me], lsem.at[a]).wait()

    res = pl.pallas_call(
        body, name=name,
        out_shape=[pltpu.HBM(s.shape, s.dtype) for s in srcs] + [pltpu.HBM(l.shape, l.dtype) for l in lands],
        in_specs=[HBM] * (2 * n) + [SEM, SEM, SEM, ANY], out_specs=[HBM] * (2 * n),
        input_output_aliases={a: a for a in range(2 * n)},
        compiler_params=pltpu.CompilerParams(has_side_effects=EFFECT),
    )(*srcs, *lands, *sems, after)
    return list(res[n:])


def sum8(parts, name):
    _, P, C = parts.shape

    def body(p_ref, o_ref):
        tot = p_ref[0]
        for d in range(1, N_DEV):
            tot = tot + p_ref[d]
        o_ref[...] = tot

    return pl.pallas_call(body, name=name, out_shape=jax.ShapeDtypeStruct((P, C), F32),
                          compiler_params=pltpu.CompilerParams(vmem_limit_bytes=VMEM_LIMIT))(parts)


def mm_nn(a, w, out_dtype, name, res=None, gate=None):
    M, K = a.shape
    S, _, Ns = w.shape
    TM = _tile(M, (1024, 512, 256) if K <= 1024 else (512, 256))
    TN = _tile(Ns, (1408, 1024, 768, 512, 256, 128))
    nj = Ns // TN
    fused = res is not None

    def body(*refs):
        if fused:
            a_ref, w_ref, r_ref, g_ref, f_ref, o_ref = refs
        else:
            a_ref, w_ref, f_ref = refs
        f = jnp.dot(a_ref[...], w_ref[...], preferred_element_type=F32)
        f_ref[...] = f.astype(f_ref.dtype)
        if fused:
            o_ref[...] = r_ref[...] + g_ref[...] * f

    col = lambda s, j, i: (i, s * nj + j)
    in_specs = [pl.BlockSpec((TM, K), lambda s, j, i: (i, 0)), pl.BlockSpec((None, K, TN), lambda s, j, i: (s, 0, j))]
    out_specs = [pl.BlockSpec((TM, TN), col)]
    out_shape = [jax.ShapeDtypeStruct((M, S * Ns), out_dtype)]
    args = [a, w]
    if fused:
        in_specs += [pl.BlockSpec((TM, TN), col), pl.BlockSpec((1, TN), lambda s, j, i: (0, s * nj + j))]
        out_specs.append(pl.BlockSpec((TM, TN), col))
        out_shape.append(jax.ShapeDtypeStruct((M, S * Ns), F32))
        args += [res, gate]
    out = pl.pallas_call(body, name=name, grid=(S, nj, M // TM), in_specs=in_specs, out_specs=out_specs,
                         out_shape=out_shape, compiler_params=_cp("parallel", "parallel", "parallel"))(*args)
    return tuple(out) if fused else out[0]


def mm_nt(g, w, out_dtype, name):
    g3 = g if g.ndim == 3 else g[None]
    Q, M, F = g3.shape
    S, K, Ns = w.shape
    TM = _tile(M, (1024, 512, 256) if K <= 1024 else (512, 256))
    TN = _tile(Ns, (1408, 1024, 768, 512, 256, 128))
    nj = Ns // TN
    nred = S * nj
    per_part = F // TN

    def body(g_ref, w_ref, o_ref, acc):
        n = pl.program_id(1)

        @pl.when(n == 0)
        def _():
            acc[...] = jnp.zeros_like(acc)

        acc[...] += lax.dot_general(g_ref[...], w_ref[...], (((1,), (1,)), ((), ())), preferred_element_type=F32)

        @pl.when(n == nred - 1)
        def _():
            o_ref[...] = acc[...].astype(o_ref.dtype)

    return pl.pallas_call(
        body, name=name, grid=(M // TM, nred),
        in_specs=[pl.BlockSpec((None, TM, TN), lambda i, n: (n // per_part, i, n % per_part)),
                  pl.BlockSpec((None, K, TN), lambda i, n: (n // nj, 0, n % nj))],
        out_specs=pl.BlockSpec((TM, K), lambda i, n: (i, 0)),
        out_shape=jax.ShapeDtypeStruct((M, K), out_dtype),
        scratch_shapes=[pltpu.VMEM((TM, K), F32)],
        compiler_params=_cp("parallel", "arbitrary"))(g3, w)


def mm_tn(a, g, S, name):
    M, K = a.shape
    g3 = g if g.ndim == 3 else g[None]
    Q, _, F = g3.shape
    Ns = Q * F // S
    TK = _tile(K, (256, 128))
    TN = _tile(Ns, (1408, 1024, 768, 512, 256, 128))
    nj = Ns // TN
    per_part = F // TN

    def body(a_ref, g_ref, o_ref):
        o_ref[...] = lax.dot_general(a_ref[...], g_ref[...], (((0,), (0,)), ((), ())),
                                     preferred_element_type=F32).astype(o_ref.dtype)

    return pl.pallas_call(
        body, name=name, grid=(S * nj, K // TK),
        in_specs=[pl.BlockSpec((M, TK), lambda n, k: (0, k)),
                  pl.BlockSpec((None, M, TN), lambda n, k: (n // per_part, 0, n % per_part))],
        out_specs=pl.BlockSpec((None, TK, TN), lambda n, k: (n // nj, k, n % nj)),
        out_shape=jax.ShapeDtypeStruct((S, K, Ns), BF),
        compiler_params=_cp("parallel", "parallel"))(a, g3)


ROW_TILE = (512, 256)


def _rows(TL, D):
    return pl.BlockSpec((TL, D), lambda i: (i, 0))


def _fixed(R, D):
    return pl.BlockSpec((R, D), lambda i: (0, 0))


def _rowsum8(v):
    T, D = v.shape
    return jnp.sum(v.reshape(T // 8, 8, D), axis=0)


def _norm_parts(xv):
    r = lax.rsqrt(jnp.mean(xv * xv, axis=-1, keepdims=True) + RMS_EPS)
    return xv * r, r


def norm_mod(x, gamma, mods, k_shift, out_dtype, name):
    L, D = x.shape
    TL = _tile(L, ROW_TILE)

    def body(x_ref, g_ref, m_ref, o_ref):
        xn, _ = _norm_parts(x_ref[...])
        sh, sc = m_ref[k_shift:k_shift + 1, :], m_ref[k_shift + 1:k_shift + 2, :]
        o_ref[...] = ((xn * g_ref[...]) * (1.0 + sc) + sh).astype(o_ref.dtype)

    return pl.pallas_call(body, name=name, grid=(L // TL,),
                          in_specs=[_rows(TL, D), _fixed(1, D), _fixed(6, D)], out_specs=_rows(TL, D),
                          out_shape=jax.ShapeDtypeStruct((L, D), out_dtype), compiler_params=_cp("parallel"))(x, gamma, mods)


def norm_bwd(dh, x, dres, gamma, mods, k_shift, name, branch=None):
    L, D = x.shape
    TL = _tile(L, ROW_TILE)
    nacc = 4 if branch else 3

    def body(*refs):
        if branch:
            dh_ref, x_ref, dr_ref, g_ref, m_ref, f_ref, fm_ref, dx_ref, s_ref, df_ref, acc = refs
        else:
            dh_ref, x_ref, dr_ref, g_ref, m_ref, dx_ref, s_ref, acc = refs
        i = pl.program_id(0)

        @pl.when(i == 0)
        def _():
            acc[...] = jnp.zeros_like(acc)

        xn, r = _norm_parts(x_ref[...])
        dh_v = dh_ref[...].astype(F32)
        gam = g_ref[...]
        sc = m_ref[k_shift + 1:k_shift + 2, :]
        dn = dh_v * (1.0 + sc)
        dxn = dn * gam
        dx = dr_ref[...] + r * (dxn - xn * jnp.mean(dxn * xn, axis=-1, keepdims=True))
        dx_ref[...] = dx
        acc[0] += _rowsum8(dh_v)
        acc[1] += _rowsum8(dh_v * (xn * gam))
        acc[2] += _rowsum8(dn * xn)
        if branch:
            df_ref[...] = (dx * fm_ref[branch[2]:branch[2] + 1, :]).astype(df_ref.dtype)
            acc[3] += _rowsum8(dx * f_ref[...].astype(F32))

        @pl.when(i == pl.num_programs(0) - 1)
        def _():
            s_ref[...] = jnp.zeros_like(s_ref)
            for q in range(nacc):
                s_ref[q:q + 1, :] = jnp.sum(acc[q], axis=0, keepdims=True)

    in_specs = [_rows(TL, D), _rows(TL, D), _rows(TL, D), _fixed(1, D), _fixed(6, D)]
    out_specs = [_rows(TL, D), _fixed(8, D)]
    out_shape = [jax.ShapeDtypeStruct((L, D), F32), jax.ShapeDtypeStruct((8, D), F32)]
    args = [dh, x, dres, gamma, mods]
    if branch:
        in_specs += [_rows(TL, D), _fixed(6, D)]
        out_specs.append(_rows(TL, D))
        out_shape.append(jax.ShapeDtypeStruct((L, D), BF))
        args += [branch[0], branch[1]]
    return pl.pallas_call(
        body, name=name, grid=(L // TL,), in_specs=in_specs, out_specs=out_specs, out_shape=out_shape,
        scratch_shapes=[pltpu.VMEM((nacc, 8, D), F32)], compiler_params=_cp("arbitrary"))(*args)


def ffn_in_act(a, w, name):
    M, K = a.shape
    S, _, Ns = w.shape
    half = S // 2
    TM = _tile(M, (512, 256))
    TN = _tile(Ns, (1408, 1024, 768, 512, 256, 128))
    nj = Ns // TN

    def body(a_ref, wg_ref, wu_ref, gu_ref, act_ref):
        av = a_ref[...]
        g = jnp.dot(av, wg_ref[...], preferred_element_type=F32)
        u = jnp.dot(av, wu_ref[...], preferred_element_type=F32)
        gu_ref[0] = g.astype(gu_ref.dtype)
        gu_ref[1] = u.astype(gu_ref.dtype)
        act_ref[...] = (g * _sigmoid(g) * u).astype(act_ref.dtype)

    return pl.pallas_call(
        body, name=name, grid=(half, nj, M // TM),
        in_specs=[pl.BlockSpec((TM, K), lambda s, j, i: (i, 0)),
                  pl.BlockSpec((None, K, TN), lambda s, j, i: (s, 0, j)),
                  pl.BlockSpec((None, K, TN), lambda s, j, i: (s + half, 0, j))],
        out_specs=[pl.BlockSpec((2, TM, TN), lambda s, j, i: (0, i, s * nj + j)),
                   pl.BlockSpec((TM, TN), lambda s, j, i: (i, s * nj + j))],
        out_shape=[jax.ShapeDtypeStruct((2, M, half * Ns), BF), jax.ShapeDtypeStruct((M, half * Ns), BF)],
        compiler_params=_cp("parallel", "parallel", "parallel"))(a, w, w)


def ffn_out_bwd(dff, w2, gu, name):
    M, D = dff.shape
    F = w2.shape[0]
    TM = _tile(M, (512, 256))
    CW = _tile(F, (256, 128))

    def body(d_ref, w_ref, gu_ref, o_ref):
        dv = d_ref[...]

        def product(c):
            return lax.dot_general(dv, w_ref[c:c + CW, :], (((1,), (1,)), ((), ())), preferred_element_type=F32)

        da = product(0)
        for c in range(0, F, CW):
            ahead = product(c + CW) if c + CW < F else None
            g = gu_ref[0, :, c:c + CW].astype(F32)
            u = gu_ref[1, :, c:c + CW].astype(F32)
            s = _sigmoid(g)
            o_ref[0, :, c:c + CW] = (da * u * (s + g * s * (1.0 - s))).astype(o_ref.dtype)
            o_ref[1, :, c:c + CW] = (da * g * s).astype(o_ref.dtype)
            da = ahead

    part = pl.BlockSpec((2, TM, F), lambda i: (0, i, 0))
    return pl.pallas_call(
        body, name=name, grid=(M // TM,),
        in_specs=[pl.BlockSpec((TM, D), lambda i: (i, 0)), pl.BlockSpec((F, D), lambda i: (0, 0)), part],
        out_specs=part, out_shape=jax.ShapeDtypeStruct((2, M, F), BF),
        compiler_params=_cp("parallel"))(dff, w2, gu)


def ssm_out_glu(z, w, x, mods, k_gate, name):
    M, K = z.shape
    S, _, Ns = w.shape
    half = S // 2
    TM = _tile(M, (1024, 512, 256))
    TN = _tile(Ns, (512, 256, 128))
    nj = Ns // TN

    def body(z_ref, wv_ref, wg_ref, x_ref, m_ref, o_ref, mix_ref, y_ref):
        zv = z_ref[...]
        CW = _tile(TN, (256, 128))

        def products(c):
            return (jnp.dot(zv, wv_ref[:, c:c + CW], preferred_element_type=F32),
                    jnp.dot(zv, wg_ref[:, c:c + CW], preferred_element_type=F32))

        cur = products(0)
        for c in range(0, TN, CW):
            ahead = products(c + CW) if c + CW < TN else None
            val, gate = cur
            o_ref[0, :, c:c + CW] = val.astype(o_ref.dtype)
            o_ref[1, :, c:c + CW] = gate.astype(o_ref.dtype)
            mix = val * _sigmoid(gate)
            mix_ref[:, c:c + CW] = mix.astype(mix_ref.dtype)
            y_ref[:, c:c + CW] = x_ref[:, c:c + CW] + m_ref[k_gate:k_gate + 1, c:c + CW] * mix
            cur = ahead

    col = lambda s, j, i: (i, s * nj + j)
    return pl.pallas_call(
        body, name=name, grid=(half, nj, M // TM),
        in_specs=[pl.BlockSpec((TM, K), lambda s, j, i: (i, 0)),
                  pl.BlockSpec((None, K, TN), lambda s, j, i: (s, 0, j)),
                  pl.BlockSpec((None, K, TN), lambda s, j, i: (s + half, 0, j)),
                  pl.BlockSpec((TM, TN), col), pl.BlockSpec((6, TN), lambda s, j, i: (0, s * nj + j))],
        out_specs=[pl.BlockSpec((2, TM, TN), lambda s, j, i: (0, i, s * nj + j)), pl.BlockSpec((TM, TN), col),
                   pl.BlockSpec((TM, TN), col)],
        out_shape=[jax.ShapeDtypeStruct((2, M, half * Ns), BF), jax.ShapeDtypeStruct((M, half * Ns), BF),
                   jax.ShapeDtypeStruct((M, half * Ns), F32)],
        compiler_params=_cp("parallel", "parallel", "parallel"))(z, w, w, x, mods)


def glu_bwd(dmix, o, name):
    _, L, D = o.shape
    TL = _tile(L, ROW_TILE)

    def body(d_ref, o_ref, do_ref):
        d = d_ref[...].astype(F32)
        val = o_ref[0].astype(F32)
        s = _sigmoid(o_ref[1].astype(F32))
        do_ref[0] = (d * s).astype(do_ref.dtype)
        do_ref[1] = (d * val * s * (1.0 - s)).astype(do_ref.dtype)

    part = pl.BlockSpec((2, TL, D), lambda i: (0, i, 0))
    return pl.pallas_call(body, name=name, grid=(L // TL,), in_specs=[_rows(TL, D), part],
                          out_specs=part, out_shape=jax.ShapeDtypeStruct((2, L, D), BF),
                          compiler_params=_cp("parallel"))(dmix, o)


def final_loss(x, target, gamma, f, fmods, k_gate, name):
    L, D = x.shape
    TL = _tile(L, ROW_TILE)

    def body(x_ref, t_ref, g_ref, f_ref, fm_ref, l_ref, dx_ref, s_ref, df_ref, acc, lacc):
        i = pl.program_id(0)

        @pl.when(i == 0)
        def _():
            acc[...] = jnp.zeros_like(acc)
            lacc[...] = jnp.zeros_like(lacc)

        xn, r = _norm_parts(x_ref[...])
        gam = g_ref[...]
        e = xn * gam - t_ref[...]
        lacc[...] += jnp.sum(0.5 * jnp.mean(e * e, axis=-1, keepdims=True), axis=0, keepdims=True)
        dy = e * (1.0 / D)
        dxn = dy * gam
        dx = r * (dxn - xn * jnp.mean(dxn * xn, axis=-1, keepdims=True))
        dx_ref[...] = dx
        df_ref[...] = (dx * fm_ref[k_gate:k_gate + 1, :]).astype(df_ref.dtype)
        acc[0] += _rowsum8(dy * xn)
        acc[1] += _rowsum8(dx * f_ref[...].astype(F32))

        @pl.when(i == pl.num_programs(0) - 1)
        def _():
            s_ref[...] = jnp.zeros_like(s_ref)
            for q in range(2):
                s_ref[q:q + 1, :] = jnp.sum(acc[q], axis=0, keepdims=True)
            l_ref[...] = jnp.broadcast_to(lacc[...], l_ref.shape)

    return pl.pallas_call(
        body, name=name, grid=(L // TL,),
        in_specs=[_rows(TL, D), _rows(TL, D), _fixed(1, D), _rows(TL, D), _fixed(6, D)],
        out_specs=[_fixed(8, 128), _rows(TL, D), _fixed(8, D), _rows(TL, D)],
        out_shape=[jax.ShapeDtypeStruct((8, 128), F32), jax.ShapeDtypeStruct((L, D), F32),
                   jax.ShapeDtypeStruct((8, D), F32), jax.ShapeDtypeStruct((L, D), BF)],
        scratch_shapes=[pltpu.VMEM((2, 8, D), F32), pltpu.VMEM((1, 1), F32)],
        compiler_params=_cp("arbitrary"))(x, target, gamma, f, fmods)


def _col(L, TC, off):
    return pl.BlockSpec((L, TC), lambda j: (0, off + j))


def _shift_down(v, k, row):
    return jnp.where(row >= k, pltpu.roll(v, k, 0), 0.0)


def _shift_up(v, k, row, L):
    return jnp.where(row < L - k, pltpu.roll(v, L - k, 0), 0.0)


def conv_fwd(p, w, name):
    L, D3 = p.shape
    D = D3 // 3
    TC = _tile(D, (128,))
    nc = D // TC

    def body(b_ref, c_ref, v_ref, w_ref, o_ref):
        row = lax.broadcasted_iota(jnp.int32, (L, TC), 0)
        cv = c_ref[...].astype(F32) * v_ref[...].astype(F32)
        conv = w_ref[2:3, :] * cv + w_ref[1:2, :] * _shift_down(cv, 1, row) + w_ref[0:1, :] * _shift_down(cv, 2, row)
        o_ref[...] = (b_ref[...].astype(F32) * conv).astype(o_ref.dtype)

    return pl.pallas_call(
        body, name=name, grid=(nc,),
        in_specs=[_col(L, TC, 0), _col(L, TC, nc), _col(L, TC, 2 * nc), pl.BlockSpec((3, TC), lambda j: (0, j))],
        out_specs=_col(L, TC, 0), out_shape=jax.ShapeDtypeStruct((L, D), BF), compiler_params=_cp("parallel"))(p, p, p, w)


def conv_bwd(dm, p, w, name):
    L, D3 = p.shape
    D = D3 // 3
    TC = _tile(D, (128,))
    nc = D // TC

    def body(dm_ref, b_ref, c_ref, v_ref, w_ref, db_ref, dc_ref, dv_ref, dw_ref):
        row = lax.broadcasted_iota(jnp.int32, (L, TC), 0)
        cg, vv = c_ref[...].astype(F32), v_ref[...].astype(F32)
        cv = cg * vv
        cv1, cv2 = _shift_down(cv, 1, row), _shift_down(cv, 2, row)
        conv = w_ref[2:3, :] * cv + w_ref[1:2, :] * cv1 + w_ref[0:1, :] * cv2
        dmv = dm_ref[...].astype(F32)
        db_ref[...] = (dmv * conv).astype(db_ref.dtype)
        dconv = dmv * b_ref[...].astype(F32)
        dcv = (w_ref[2:3, :] * dconv + w_ref[1:2, :] * _shift_up(dconv, 1, row, L)
               + w_ref[0:1, :] * _shift_up(dconv, 2, row, L))
        dc_ref[...] = (dcv * vv).astype(dc_ref.dtype)
        dv_ref[...] = (dcv * cg).astype(dv_ref.dtype)
        dw_ref[...] = jnp.zeros_like(dw_ref)
        dw_ref[0:1, :] = jnp.sum(dconv * cv2, axis=0, keepdims=True)
        dw_ref[1:2, :] = jnp.sum(dconv * cv1, axis=0, keepdims=True)
        dw_ref[2:3, :] = jnp.sum(dconv * cv, axis=0, keepdims=True)

    one = jax.ShapeDtypeStruct((L, D), BF)
    return pl.pallas_call(
        body, name=name, grid=(nc,),
        in_specs=[_col(L, TC, 0), _col(L, TC, 0), _col(L, TC, nc), _col(L, TC, 2 * nc),
                  pl.BlockSpec((3, TC), lambda j: (0, j))],
        out_specs=[_col(L, TC, 0), _col(L, TC, 0), _col(L, TC, 0), pl.BlockSpec((8, TC), lambda j: (0, j))],
        out_shape=[one, one, one, jax.ShapeDtypeStruct((8, D), F32)],
        compiler_params=_cp("parallel"))(dm, p, p, p, w)


def _gelu(y):
    return 0.5 * y * (1.0 + jnp.tanh(GELU_C * (y + GELU_A * y * y * y)))


def _gelu_grad(y):
    th = jnp.tanh(GELU_C * (y + GELU_A * y * y * y))
    return 0.5 * (1.0 + th) + 0.5 * y * (1.0 - th * th) * GELU_C * (1.0 + 3.0 * GELU_A * y * y)


def _cmul_add(br, bi, ar, ai, sr, si):
    return br + ar * sr - ai * si, bi + ar * si + ai * sr


def _log2(n):
    k = n.bit_length() - 1
    assert 1 << k == n
    return k


def _replicate(P2, W2, P, GLP, transposed):
    shape = (W2, P2) if transposed else (P2, W2)
    k = lax.broadcasted_iota(jnp.int32, shape, 1 if transposed else 0)
    c = lax.broadcasted_iota(jnp.int32, shape, 0 if transposed else 1)
    return ((k >> _log2(P)) == (c >> _log2(GLP))) & ((k & (P - 1)) == (c & (P - 1)))


def _on_diagonal(KB, W2, H, P, GLP, transposed):
    shape = (W2, KB) if transposed else (KB, W2)
    r = lax.broadcasted_iota(jnp.int32, shape, 1 if transposed else 0)
    c = lax.broadcasted_iota(jnp.int32, shape, 0 if transposed else 1)
    return (r >> _log2(H)) == ((c & (GLP - 1)) >> _log2(P))


def _expand(t, dims, transposed):
    KB, W2, H, P, GLP = dims
    rep = _replicate(2 * P, W2, P, GLP, transposed).astype(t.dtype)
    wide = jnp.dot(rep, t, preferred_element_type=F32) if transposed else jnp.dot(t, rep, preferred_element_type=F32)
    return jnp.where(_on_diagonal(KB, W2, H, P, GLP, transposed), wide, 0.0).astype(t.dtype)


def _extract(acc, dims):
    KB, W2, H, P, GLP = dims
    rep = _replicate(2 * P, W2, P, GLP, True).astype(BF)
    kept = jnp.where(_on_diagonal(KB, W2, H, P, GLP, False), acc, 0.0)
    hi = kept.astype(BF)
    lo = (kept - hi.astype(F32)).astype(BF)
    return jnp.dot(hi, rep, preferred_element_type=F32) + jnp.dot(lo, rep, preferred_element_type=F32)


def _cmul(ar, ai, sr, si):
    return ar * sr - ai * si, ar * si + ai * sr


def _chunk_order(TL, CH, transposed):
    out_row = lax.broadcasted_iota(jnp.int32, (TL, TL), 1 if transposed else 0)
    in_row = lax.broadcasted_iota(jnp.int32, (TL, TL), 0 if transposed else 1)
    return in_row == ((out_row & 7) << _log2(CH)) + (out_row >> 3)


def _reorder(perm, v):
    hi = v.astype(perm.dtype)
    lo = (v - hi.astype(F32)).astype(perm.dtype)
    return jnp.dot(perm, hi, preferred_element_type=F32) + jnp.dot(perm, lo, preferred_element_type=F32)


def _interleave(main, side):
    n, m, k = len(main), len(side), 0
    for i, step in enumerate(main):
        step()
        while k < m and (k + 1) * n <= (i + 1) * m:
            side[k]()
            k += 1
    for step in side[k:]:
        step()


S5_CHUNK = 512


def s5_fwd(h, tb, tct, pw, dvec, name):
    L, D = h.shape
    nkb, KB, P2 = tb.shape
    P = P2 // 2
    W = (KB // SSM_GROUP) * P
    W2 = 2 * W
    dims = (KB, W2, SSM_GROUP, P, W)
    TL = _tile(L, (512, 256))
    CH = TL // 8
    NB = 2 if nkb % 2 == 0 else 1
    CK = min(S5_CHUNK, W2)

    def body(h_ref, tb_ref, tct_ref, pw_ref, d_ref, s_ref, y_ref, z_ref, bw, cw, perm, unperm, carry):
        t = pl.program_id(1)

        @pl.when(t == 0)
        def _():
            carry[...] = jnp.zeros_like(carry)
            for b in range(NB):
                bw[b] = _expand(tb_ref[b], dims, False)
                cw[b] = _expand(tct_ref[b], dims, True)
            perm[...] = _chunk_order(TL, CH, False).astype(perm.dtype)
            unperm[...] = _chunk_order(TL, CH, True).astype(perm.dtype)

        hp = _reorder(perm[...], h_ref[...])
        hpb = hp.astype(BF)
        first = lax.broadcasted_iota(jnp.int32, (8, W), 0) == 0

        def project(b):
            def chunk(c):
                def emit():
                    s_ref[:, b * W2 + c:b * W2 + c + CK] = jnp.dot(hpb[:, b * KB:(b + 1) * KB], bw[b, :, c:c + CK],
                                                                   preferred_element_type=F32)
                return emit
            return [chunk(c) for c in range(0, W2, CK)]

        def scan(b):
            re, im = slice(b * W2, b * W2 + W), slice(b * W2 + W, (b + 1) * W2)
            ar, ai = pw_ref[b, 0:8, :W], pw_ref[b, 0:8, W:]
            st = {"x": (jnp.zeros((8, W), F32), jnp.zeros((8, W), F32))}

            def own(j):
                def emit():
                    rows = slice(j * 8, j * 8 + 8)
                    xr, xi = _cmul_add(s_ref[rows, re], s_ref[rows, im], ar, ai, *st["x"])
                    s_ref[rows, re] = xr
                    s_ref[rows, im] = xi
                    st["x"] = (xr, xi)
                return emit

            def ends():
                xr, xi = st["x"]
                for k, off in ((1, 8), (2, 16), (4, 24)):
                    xr, xi = _cmul_add(xr, xi, pw_ref[b, off:off + 8, :W], pw_ref[b, off:off + 8, W:],
                                       pltpu.roll(xr, k, 0), pltpu.roll(xi, k, 0))
                xr, xi = _cmul_add(xr, xi, pw_ref[b, 32:40, :W], pw_ref[b, 32:40, W:], carry[b, 0], carry[b, 1])
                st["c"] = (jnp.where(first, carry[b, 0], pltpu.roll(xr, 1, 0)),
                           jnp.where(first, carry[b, 1], pltpu.roll(xi, 1, 0)))
                carry[b, 0] = jnp.broadcast_to(xr[7:8], (8, W))
                carry[b, 1] = jnp.broadcast_to(xi[7:8], (8, W))

            def carried(j):
                def emit():
                    rows = slice(j * 8, j * 8 + 8)
                    cr, ci = _cmul(ar, ai, *st["c"])
                    s_ref[rows, re] = s_ref[rows, re] + cr
                    s_ref[rows, im] = s_ref[rows, im] + ci
                    st["c"] = (cr, ci)
                return emit

            return [own(j) for j in range(CH)] + [ends] + [carried(j) for j in range(CH)]

        def readout(b):
            cols = slice(b * KB, (b + 1) * KB)
            acc = {}

            def chunk(c):
                def emit():
                    part = jnp.dot(s_ref[:, b * W2 + c:b * W2 + c + CK].astype(BF), cw[b, c:c + CK, :],
                                   preferred_element_type=F32)
                    acc["y"] = part if c == 0 else acc["y"] + part
                return emit

            def finish():
                y = acc["y"] + d_ref[:, cols] * hp[:, cols]
                y_ref[:, cols] = y
                z_ref[:, cols] = jnp.dot(unperm[...], _gelu(y).astype(BF),
                                         preferred_element_type=F32).astype(z_ref.dtype)

            return [chunk(c) for c in range(0, W2, CK)] + [finish]

        for emit in project(0):
            emit()
        for b in range(NB):
            side = (project(b + 1) if b + 1 < NB else []) + (readout(b - 1) if b > 0 else [])
            _interleave(scan(b), side)
        for emit in readout(NB - 1):
            emit()

    blk = lambda kb, t: (t, kb)
    per_kb = lambda kb, t: (kb, 0, 0)
    return pl.pallas_call(
        body, name=name, grid=(nkb // NB, L // TL),
        in_specs=[pl.BlockSpec((TL, NB * KB), blk), pl.BlockSpec((NB, KB, P2), per_kb),
                  pl.BlockSpec((NB, P2, KB), per_kb), pl.BlockSpec((NB, 40, W2), per_kb),
                  pl.BlockSpec((1, NB * KB), lambda kb, t: (0, kb))],
        out_specs=[pl.BlockSpec((TL, NB * W2), blk), pl.BlockSpec((TL, NB * KB), blk),
                   pl.BlockSpec((TL, NB * KB), blk)],
        out_shape=[jax.ShapeDtypeStruct((L, nkb * W2), F32), jax.ShapeDtypeStruct((L, D), F32),
                   jax.ShapeDtypeStruct((L, D), BF)],
        scratch_shapes=[pltpu.VMEM((NB, KB, W2), BF), pltpu.VMEM((NB, W2, KB), BF), pltpu.VMEM((TL, TL), BF),
                        pltpu.VMEM((TL, TL), BF), pltpu.VMEM((NB, 2, 8, W), F32)],
        compiler_params=_cp("parallel", "arbitrary"))(h, tb, tct, pw, dvec)


def s5_bwd(dz, y, h, s, tc, tbt, pwr, dvec, name):
    L, D = h.shape
    nkb, KB, P2 = tc.shape
    P = P2 // 2
    W = (KB // SSM_GROUP) * P
    W2 = 2 * W
    dims = (KB, W2, SSM_GROUP, P, W)
    TL = _tile(L, (512, 256))
    CH = TL // 8
    nt = L // TL
    NB = 2 if nkb % 2 == 0 else 1
    CK = min(S5_CHUNK, W2)
    tn = (((0,), (0,)), ((), ()))

    def body(dz_ref, y_ref, h_ref, s_ref, sp_ref, tc_ref, tbt_ref, pw_ref, d_ref,
             dh_ref, dd_ref, da_ref, db_ref, dc_ref, g, ctw, btw, dbacc, dcacc, dys, perm, unperm, carry):
        t = pl.program_id(1)

        @pl.when(t == 0)
        def _():
            carry[...] = jnp.zeros_like(carry)
            dd_ref[...] = jnp.zeros_like(dd_ref)
            da_ref[...] = jnp.zeros_like(da_ref)
            dbacc[...] = jnp.zeros_like(dbacc)
            dcacc[...] = jnp.zeros_like(dcacc)
            for b in range(NB):
                ctw[b] = _expand(tc_ref[b], dims, False)
                btw[b] = _expand(tbt_ref[b], dims, True)
            perm[...] = _chunk_order(TL, CH, False).astype(perm.dtype)
            unperm[...] = _chunk_order(TL, CH, True).astype(perm.dtype)

        hp = jnp.dot(perm[...], h_ref[...].astype(BF), preferred_element_type=F32)
        dy = jnp.dot(perm[...], dz_ref[...].astype(BF), preferred_element_type=F32) * _gelu_grad(y_ref[...])
        dd_ref[...] += _rowsum8(dy * hp)
        dys[...] = dy
        dyb = dy.astype(BF)
        hpb = hp.astype(BF)
        sub = lax.broadcasted_iota(jnp.int32, (8, W), 0)
        live = jnp.where(t == nt - 1, 0.0, 1.0)

        def lead(b):
            cols = slice(b * KB, (b + 1) * KB)

            def to_states(c):
                def emit():
                    g[b, :, c:c + CK] = jnp.dot(dyb[:, cols], ctw[b, :, c:c + CK], preferred_element_type=F32)
                return emit

            def d_c(c):
                def emit():
                    dcacc[b, :, c:c + CK] += lax.dot_general(dyb[:, cols],
                                                             s_ref[:, b * W2 + c:b * W2 + c + CK].astype(BF), tn,
                                                             preferred_element_type=F32)
                return emit

            return [f(c) for c in range(0, W2, CK) for f in (to_states, d_c)]

        def scan(b):
            re, im = slice(b * W2, b * W2 + W), slice(b * W2 + W, (b + 1) * W2)
            ar, ai = pw_ref[b, 0:8, :W], pw_ref[b, 0:8, W:]
            zero = jnp.zeros((8, W), F32)
            st = {"g": (zero, zero), "acc": (zero, zero)}

            def own(j):
                def emit():
                    rows = slice(j * 8, j * 8 + 8)
                    gr, gi = _cmul_add(g[b, rows, :W], g[b, rows, W:], ar, ai, *st["g"])
                    g[b, rows, :W] = gr
                    g[b, rows, W:] = gi
                    st["g"] = (gr, gi)
                return emit

            def ends():
                gr, gi = st["g"]
                for k, off in ((1, 8), (2, 16), (4, 24)):
                    gr, gi = _cmul_add(gr, gi, pw_ref[b, off:off + 8, :W], pw_ref[b, off:off + 8, W:],
                                       pltpu.roll(gr, 8 - k, 0), pltpu.roll(gi, 8 - k, 0))
                gr, gi = _cmul_add(gr, gi, pw_ref[b, 32:40, :W], pw_ref[b, 32:40, W:], carry[b, 0], carry[b, 1])
                st["c"] = (jnp.where(sub == 7, carry[b, 0], pltpu.roll(gr, 7, 0)),
                           jnp.where(sub == 7, carry[b, 1], pltpu.roll(gi, 7, 0)))
                carry[b, 0] = jnp.broadcast_to(gr[0:1], (8, W))
                carry[b, 1] = jnp.broadcast_to(gi[0:1], (8, W))

            def carried(j):
                def emit():
                    rows = slice(j * 8, j * 8 + 8)
                    cr, ci = _cmul(ar, ai, *st["c"])
                    gr, gi = g[b, rows, :W] + cr, g[b, rows, W:] + ci
                    g[b, rows, :W] = gr
                    g[b, rows, W:] = gi
                    if j > 0:
                        before = slice(j * 8 - 8, j * 8)
                        pr, pi = s_ref[before, re], s_ref[before, im]
                    else:
                        last = slice(TL - 8, TL)
                        pr = jnp.where(sub == 0, sp_ref[7:8, re] * live, pltpu.roll(s_ref[last, re], 1, 0))
                        pi = jnp.where(sub == 0, sp_ref[7:8, im] * live, pltpu.roll(s_ref[last, im], 1, 0))
                    accr, acci = st["acc"]
                    st["c"] = (cr, ci)
                    st["acc"] = (accr + pr * gr + pi * gi, acci + pr * gi - pi * gr)
                return emit

            def done():
                da_ref[b, :, :W] += st["acc"][0]
                da_ref[b, :, W:] += st["acc"][1]

            return ([own(j) for j in reversed(range(CH))] + [ends] + [carried(j) for j in reversed(range(CH))]
                    + [done])

        def tail(b):
            cols = slice(b * KB, (b + 1) * KB)
            acc = {}

            def d_u(c):
                def emit():
                    part = jnp.dot(g[b, :, c:c + CK].astype(BF), btw[b, c:c + CK, :], preferred_element_type=F32)
                    acc["u"] = part if c == 0 else acc["u"] + part
                return emit

            def d_b(c):
                def emit():
                    dbacc[b, :, c:c + CK] += lax.dot_general(hpb[:, cols], g[b, :, c:c + CK].astype(BF), tn,
                                                             preferred_element_type=F32)
                return emit

            def finish():
                dh = (dys[:, cols] * d_ref[:, cols] + acc["u"]).astype(BF)
                dh_ref[:, cols] = jnp.dot(unperm[...], dh, preferred_element_type=F32).astype(dh_ref.dtype)

            return [f(c) for c in range(0, W2, CK) for f in (d_u, d_b)] + [finish]

        for emit in lead(0):
            emit()
        for b in range(NB):
            side = (lead(b + 1) if b + 1 < NB else []) + (tail(b - 1) if b > 0 else [])
            _interleave(scan(b), side)
        for emit in tail(NB - 1):
            emit()

        @pl.when(t == nt - 1)
        def _():
            for b in range(NB):
                db_ref[b] = _extract(dbacc[b], dims)
                dc_ref[b] = _extract(dcacc[b], dims)

    rev = lambda kb, t: (nt - 1 - t, kb)
    prev = lambda kb, t: (jnp.maximum((nt - 1 - t) * CH - 1, 0), kb)
    per_kb = lambda kb, t: (kb, 0, 0)
    return pl.pallas_call(
        body, name=name, grid=(nkb // NB, nt),
        in_specs=[pl.BlockSpec((TL, NB * KB), rev), pl.BlockSpec((TL, NB * KB), rev),
                  pl.BlockSpec((TL, NB * KB), rev), pl.BlockSpec((TL, NB * W2), rev),
                  pl.BlockSpec((8, NB * W2), prev), pl.BlockSpec((NB, KB, P2), per_kb),
                  pl.BlockSpec((NB, P2, KB), per_kb), pl.BlockSpec((NB, 40, W2), per_kb),
                  pl.BlockSpec((1, NB * KB), lambda kb, t: (0, kb))],
        out_specs=[pl.BlockSpec((TL, NB * KB), rev), pl.BlockSpec((8, NB * KB), lambda kb, t: (0, kb)),
                   pl.BlockSpec((NB, 8, W2), per_kb), pl.BlockSpec((NB, KB, P2), per_kb),
                   pl.BlockSpec((NB, KB, P2), per_kb)],
        out_shape=[jax.ShapeDtypeStruct((L, D), BF), jax.ShapeDtypeStruct((8, D), F32),
                   jax.ShapeDtypeStruct((nkb, 8, W2), F32), jax.ShapeDtypeStruct((nkb, KB, P2), F32),
                   jax.ShapeDtypeStruct((nkb, KB, P2), F32)],
        scratch_shapes=[pltpu.VMEM((NB, TL, W2), F32), pltpu.VMEM((NB, KB, W2), BF), pltpu.VMEM((NB, W2, KB), BF),
                        pltpu.VMEM((NB, KB, W2), F32), pltpu.VMEM((NB, KB, W2), F32), pltpu.VMEM((TL, NB * KB), F32),
                        pltpu.VMEM((TL, TL), BF), pltpu.VMEM((TL, TL), BF), pltpu.VMEM((NB, 2, 8, W), F32)],
        compiler_params=pltpu.CompilerParams(dimension_semantics=("parallel", "arbitrary"),
                                             vmem_limit_bytes=V7X_VMEM_BYTES - 4 * 1024 * 1024),
    )(dz, y, h, s, s, tc, tbt, pwr, dvec)


def _discretise(a_re, a_im, log_step, b_re, b_im):
    lr = jnp.minimum(a_re, -1e-4)
    li = a_im
    dt = jnp.exp(log_step)[:, None]
    mag = jnp.exp(lr * dt)
    abr = mag * jnp.cos(li * dt)
    abi = mag * jnp.sin(li * dt)
    den = lr * lr + li * li
    qr = ((abr - 1.0) * lr + abi * li) / den
    qi = (abi * lr - (abr - 1.0) * li) / den
    bbar_re = qr[..., None] * b_re - qi[..., None] * b_im
    bbar_im = qr[..., None] * b_im + qi[..., None] * b_re
    return abr, abi, bbar_re, bbar_im


def _compact(m_re, m_im, nkb):
    G, H, P = m_re.shape
    t = jnp.stack([m_re, m_im], axis=2).reshape(nkb, (G // nkb) * H, 2 * P).astype(BF)
    return t, jnp.swapaxes(t, 1, 2)


def _scan_powers(abr, abi, nkb, conj, CH):
    G, P = abr.shape
    if conj:
        abi = -abi

    def cmul(u, v):
        return u[0] * v[0] - u[1] * v[1], u[0] * v[1] + u[1] * v[0]

    q = (abr, abi)
    for _ in range(_log2(CH)):
        q = cmul(q, q)
    pows = [q]
    for _ in range(7):
        pows.append(cmul(pows[-1], q))
    row = jnp.arange(8)[:, None, None]

    def table(part):
        out = [jnp.broadcast_to((abr, abi)[part][None], (8, G, P))]
        for k in (1, 2, 4):
            keep = (row <= 7 - k) if conj else (row >= k)
            out.append(jnp.where(keep, pows[k - 1][part][None], 0.0))
        ends = jnp.stack([p[part] for p in pows])
        out.append(ends[::-1] if conj else ends)
        return jnp.concatenate(out, axis=0)

    GL = G // nkb
    t = jnp.stack([table(0), table(1)], axis=1)
    t = t.reshape(40, 2, nkb, GL * P).transpose(2, 0, 1, 3)
    return t.reshape(nkb, 40, 2 * GL * P)


def ada_mods(c_all, w_ada, b_sh, name):
    nl, D, NA = w_ada.shape

    def body(c_ref, w_ref, b_ref, o_ref):
        cv = c_ref[...]
        act = cv * _sigmoid(cv)
        o_ref[...] = jnp.dot(act, w_ref[...], preferred_element_type=F32, precision=lax.Precision.HIGHEST) + b_ref[...]

    return pl.pallas_call(
        body, name=name, grid=(nl,),
        in_specs=[pl.BlockSpec((8, D), lambda i: (0, 0)), pl.BlockSpec((None, D, NA), lambda i: (i, 0, 0)),
                  pl.BlockSpec((None, 1, NA), lambda i: (i, 0, 0))],
        out_specs=pl.BlockSpec((None, 8, NA), lambda i: (i, 0, 0)),
        out_shape=jax.ShapeDtypeStruct((nl, 8, NA), F32), compiler_params=_cp("parallel"))(c_all, w_ada, b_sh)


def _adamw(w, g, m, v):
    m = ADAM_B1 * m + (1.0 - ADAM_B1) * g
    v = ADAM_B2 * v + (1.0 - ADAM_B2) * (g * g)
    m_hat = m / (1.0 - ADAM_B1 ** ADAM_STEP)
    v_hat = v / (1.0 - ADAM_B2 ** ADAM_STEP)
    return -ADAM_LR * (m_hat / (jnp.sqrt(v_hat) + ADAM_EPS) + ADAM_WD * w), m, v


def _adam_rows(R, C):
    cap = max(8, (256 * 1024) // C)
    for t in range(min(R, cap), 0, -1):
        if R % t == 0 and (t % 8 == 0 or t == R):
            return t
    return R


def adamw_ada(c_t, dm, w, m, v, name):
    nl, D, NA = w.shape
    TK = _tile(D, (256, 128))

    def body(c_ref, dm_ref, w_ref, m_ref, v_ref, g_ref, d_ref, nm_ref, nv_ref):
        cv = c_ref[...]
        act = cv * _sigmoid(cv)
        g = jnp.dot(act, dm_ref[...], preferred_element_type=F32, precision=lax.Precision.HIGHEST)
        g_ref[...] = g
        d_ref[...], nm_ref[...], nv_ref[...] = _adamw(w_ref[...], g, m_ref[...], v_ref[...])

    big = pl.BlockSpec((None, TK, NA), lambda i, k: (i, k, 0))
    shape = jax.ShapeDtypeStruct(w.shape, F32)
    return pl.pallas_call(
        body, name=name, grid=(nl, D // TK),
        in_specs=[pl.BlockSpec((TK, 8), lambda i, k: (k, 0)), pl.BlockSpec((None, 8, NA), lambda i, k: (i, 0, 0)),
                  big, big, big],
        out_specs=[big] * 4, out_shape=[shape] * 4, compiler_params=_cp("parallel", "parallel"))(c_t, dm, w, m, v)


def adamw_sharded(w, m, v, ga, gb, name):
    nl, R, C = w.shape
    TR = _adam_rows(R, C)

    def body(w_ref, m_ref, v_ref, a_ref, b_ref, g_ref, d_ref, nm_ref, nv_ref):
        g = a_ref[...] + b_ref[...]
        g_ref[...] = g
        d_ref[...], nm_ref[...], nv_ref[...] = _adamw(w_ref[...], g, m_ref[...], v_ref[...])

    big = pl.BlockSpec((None, TR, C), lambda i, r: (i, r, 0))
    shape = jax.ShapeDtypeStruct(w.shape, F32)
    return pl.pallas_call(
        body, name=name, grid=(nl, R // TR), in_specs=[big] * 5,
        out_specs=[big] * 4, out_shape=[shape] * 4, compiler_params=_cp("parallel", "parallel"))(w, m, v, ga, gb)


def adamw_slab(g, w, m, v, name):
    R, C = g.shape
    TR = _tile(R, (160, 80, 40, 8))

    def body(g_ref, w_ref, m_ref, v_ref, d_ref, nm_ref, nv_ref):
        d_ref[...], nm_ref[...], nv_ref[...] = _adamw(w_ref[...], g_ref[...], m_ref[...], v_ref[...])

    big = pl.BlockSpec((TR, C), lambda r: (r, 0))
    shape = jax.ShapeDtypeStruct((R, C), F32)
    return pl.pallas_call(
        body, name=name, grid=(R // TR,), in_specs=[big] * 4,
        out_specs=[big] * 3, out_shape=[shape] * 3, compiler_params=_cp("parallel"))(g, w, m, v)


def adamw_plain(w, m, v, g, name):
    def body(w_ref, m_ref, v_ref, g_ref, d_ref, nm_ref, nv_ref):
        d_ref[...], nm_ref[...], nv_ref[...] = _adamw(w_ref[...], g_ref[...], m_ref[...], v_ref[...])

    shape = jax.ShapeDtypeStruct(w.shape, F32)
    return pl.pallas_call(body, name=name, out_shape=[shape] * 3,
                          compiler_params=pltpu.CompilerParams(vmem_limit_bytes=VMEM_LIMIT))(w, m, v, g)


def _slab_rows(a):
    n = a.size
    rows = -(-n // SLAB_W)
    return -(-rows // 8) * 8


def _pack(arrs, pad_rows_to=0):
    out = []
    for a in arrs:
        rows = _slab_rows(a)
        flat = a.reshape(-1).astype(F32)
        flat = jnp.pad(flat, (0, rows * SLAB_W - flat.shape[0]))
        out.append(flat.reshape(rows, SLAB_W))
    total = sum(o.shape[0] for o in out)
    if pad_rows_to and total % pad_rows_to:
        out.append(jnp.zeros((pad_rows_to - total % pad_rows_to, SLAB_W), F32))
    return jnp.concatenate(out, axis=0)


def _unpack(slab, like):
    out, r = [], 0
    for a in like:
        rows = _slab_rows(a)
        out.append(slab[r:r + rows].reshape(-1)[:a.size].reshape(a.shape))
        r += rows
    return out


WEIGHTS = ['norm1_g', 'norm2_g', 'w_ada', 'b_ada', 'ssm_a_re', 'ssm_a_im', 'ssm_log_step', 'ssm_b_re', 'ssm_b_im',
           'ssm_c_re', 'ssm_c_im', 'ssm_d', 'ssm_w_out', 'conv_w_in', 'conv_w', 'conv_w_out', 'w_ffn_in',
           'w_ffn_out', 'final_g']
SLAB = ['norm1_g', 'norm2_g', 'b_ada', 'ssm_a_re', 'ssm_a_im', 'ssm_log_step', 'ssm_b_re', 'ssm_b_im', 'ssm_c_re',
        'ssm_c_im', 'ssm_d', 'final_g']
SHARDED = ['ssm_w_out', 'conv_w_in', 'conv_w_out', 'w_ffn_in', 'w_ffn_out']


def kernel(x, c, norm1_g, norm2_g, w_ada, b_ada, ssm_a_re, ssm_a_im, ssm_log_step, ssm_b_re, ssm_b_im, ssm_c_re, ssm_c_im, ssm_d, ssm_w_out, conv_w_in, conv_w, conv_w_out, w_ffn_in, w_ffn_out, final_g, loss_target, m_norm1_g, m_norm2_g, m_w_ada, m_b_ada, m_ssm_a_re, m_ssm_a_im, m_ssm_log_step, m_ssm_b_re, m_ssm_b_im, m_ssm_c_re, m_ssm_c_im, m_ssm_d, m_ssm_w_out, m_conv_w_in, m_conv_w, m_conv_w_out, m_w_ffn_in, m_w_ffn_out, m_final_g, v_norm1_g, v_norm2_g, v_w_ada, v_b_ada, v_ssm_a_re, v_ssm_a_im, v_ssm_log_step, v_ssm_b_re, v_ssm_b_im, v_ssm_c_re, v_ssm_c_im, v_ssm_d, v_ssm_w_out, v_conv_w_in, v_conv_w, v_conv_w_out, v_w_ffn_in, v_w_ffn_out, v_final_g):
    given = dict(locals())
    W = {n: given[n] for n in WEIGHTS}
    Mo = {n: given["m_" + n] for n in WEIGHTS}
    Vo = {n: given["v_" + n] for n in WEIGHTS}

    xs = x[0]
    tgt = loss_target[0]
    L, D = xs.shape
    nlayer = norm1_g.shape[0]
    NA = w_ada.shape[2]
    G = ssm_a_re.shape[1]
    nkb = D // S5_BLOCK
    ax, ay, ac = _axes()
    me = 4 * ax + 2 * ay + ac
    chip = 2 * ax + ay

    assert D == SLAB_W
    first = gather8(jnp.concatenate([jnp.broadcast_to(c, (8, D)), _pack([conv_w])], axis=0), "gather_c_conv_w")
    c_all = first[:, 0, :]
    b_sh = lax.dynamic_slice_in_dim(b_ada, chip * NA, NA, axis=1)[:, None, :]
    mods_part = ada_mods(c_all, w_ada, b_sh, "ada_mods")
    mg = gather8(mods_part.reshape(nlayer * 8, NA), "gather_mods")
    mg = mg.reshape(N_CHIP, 2, nlayer, 8, NA)[:, 0]
    mods_all = lax.dynamic_index_in_dim(mg, me, axis=2, keepdims=False)
    mods_all = jnp.transpose(mods_all, (1, 0, 2)).reshape(nlayer, 6, D)

    cw_parts = first[:, 8:]
    nconv = conv_w.shape[0]
    cw_full = jnp.stack([_unpack(cw_parts[2 * q], [conv_w])[0] for q in range(N_CHIP)], axis=2)
    cw_full = cw_full.reshape(nconv, 3, D)

    in_flight_w = {}

    def start_weights(i, after):
        names = (["ssm_w_out"] if i % 2 == 0 else ["conv_w_in", "conv_w_out"]) + ["w_ffn_in", "w_ffn_out"]
        shards = [W[n][i if n.startswith("w_ffn") else i // 2].astype(BF) for n in names]
        sems, srcs, lands, tok = gather_start(shards, after, "gather_start%d" % i)
        in_flight_w[i] = (names, sems, srcs, lands)
        return tok

    def relay_weights(i, after):
        names, sems, srcs, lands = in_flight_w[i]
        got = gather_wait(sems, srcs, lands, list(range(len(names))), after, "gather_wait%d" % i)
        rsems, rlands, tok = relay_start(got, after, "relay_start%d" % i)
        in_flight_w[i] = (names, rsems, rlands)
        return tok

    def layer_weights(i, after):
        names, rsems, rlands = in_flight_w[i]
        return dict(zip(names, relay_wait(rsems, rlands, after, "relay_wait%d" % i)))

    token = start_weights(0, cw_full + mods_all[0, 0:3])
    mods_all = mods_all + token[0:1, 0:1]

    s5 = []
    for j in range(ssm_a_re.shape[0]):
        disc, disc_vjp = jax.vjp(_discretise, ssm_a_re[j], ssm_a_im[j], ssm_log_step[j], ssm_b_re[j], ssm_b_im[j])
        abr, abi, bbar_re, bbar_im = disc
        tb, tbt = _compact(jnp.swapaxes(bbar_re, 1, 2), jnp.swapaxes(bbar_im, 1, 2), nkb)
        tc, tct = _compact(ssm_c_re[j], -ssm_c_im[j], nkb)
        chunk = _tile(L, (512, 256)) // 8
        s5.append(dict(vjp=disc_vjp, tb=tb, tbt=tbt, tc=tc, tct=tct, pw=_scan_powers(abr, abi, nkb, False, chunk),
                       pwr=_scan_powers(abr, abi, nkb, True, chunk)))

    saved = []
    xcur = xs
    for i in range(nlayer):
        j = i // 2
        mods = mods_all[i]
        sv = dict(x=xcur)
        if i % 2 == 0:
            h = norm_mod(xcur, norm1_g[i:i + 1], mods, 0, F32, "norm_mod_s5")
            dvec = ssm_d[j:j + 1]
            if i == 0:
                dvec = dvec + start_weights(1, h)[0:1, 0:1]
            states, yv, z = s5_fwd(h, s5[j]["tb"], s5[j]["tct"], s5[j]["pw"], dvec, "s5_fwd")
            if i == 0:
                mods = mods + relay_weights(0, z)[0:1, 0:1]
            full = layer_weights(i, z)
            o, mix, x2 = ssm_out_glu(z, full["ssm_w_out"], xcur, mods, 2, "ssm_out_glu")
            sv.update(h=h, states=states, y=yv, z=z, o=o)
        else:
            h = norm_mod(xcur, norm1_g[i:i + 1], mods, 0, BF, "norm_mod")
            full = layer_weights(i, h)
            p = mm_nn(h, full["conv_w_in"], BF, "mm_conv_in")
            mc = conv_fwd(p, cw_full[j], "conv_fwd")
            mix, x2 = mm_nn(mc, full["conv_w_out"].reshape(1, D, D), BF, "mm_conv_out", res=xcur, gate=mods[2:3])
            sv.update(h=h, p=p, mc=mc)
        h2 = norm_mod(x2, norm2_g[i:i + 1], mods, 3, BF, "norm_mod")
        gu, act = ffn_in_act(h2, full["w_ffn_in"], "ffn_in_act")
        F = act.shape[1]
        ff, x3 = mm_nn(act, full["w_ffn_out"].reshape(1, F, D), BF, "mm_ffn_out", res=x2, gate=mods[5:6])
        sv.update(mix=mix, x2=x2, h2=h2, gu=gu, act=act, ff=ff, w=full)
        saved.append(sv)
        xcur = x3
        if i + 1 < nlayer:
            token = relay_weights(i + 1, ff)
            if i + 2 < nlayer:
                token = token + start_weights(i + 2, token)
            mods_all = mods_all + token[0:1, 0:1]

    loss_blk, dx, dfinal, dff = final_loss(xcur, tgt, final_g[None, :], saved[-1]["ff"], mods_all[nlayer - 1], 5,
                                           "final_loss")
    dg2 = dfinal[1:2]

    gland = {n: lax.empty((W[n].shape[0], N_CHIP) + W[n].shape[1:], BF) for n in SHARDED}
    in_flight = []
    dmods = [None] * nlayer
    dnorm1, dnorm2 = [None] * nlayer, [None] * nlayer
    dconv_w = [None] * nconv
    ds5 = [None] * ssm_a_re.shape[0]
    token = jnp.zeros((8, 128), F32)

    def send_grads(names, grads, slot, after, name):
        sems, thru, lands, tok = scatter_start([grads[n] for n in names], [gland[n] for n in names], slot, after, name)
        gland.update(zip(names, lands))
        in_flight.append((names, slot, sems, thru, name))
        return tok

    def land_grads(group, after):
        for names, slot, sems, thru, name in in_flight:
            if names[0] in group:
                got = scatter_wait(sems, thru, [gland[n] for n in names], slot, after, name.replace("scatter", "landed"))
                gland.update(zip(names, got))

    for i in reversed(range(nlayer)):
        j = i // 2
        mods = mods_all[i] + token[0:1, 0:1]
        sv = saved[i]
        full = sv["w"]
        gfull = {}
        F = sv["act"].shape[1]
        gfull["w_ffn_out"] = mm_tn(sv["act"], dff, 1, "mm_tn_ffn_out").reshape(N_CHIP, F // N_CHIP, D)
        dgu = ffn_out_bwd(dff, full["w_ffn_out"].reshape(F, D), sv["gu"], "ffn_out_bwd")
        gfull["w_ffn_in"] = mm_tn(sv["h2"], dgu, N_CHIP, "mm_tn_ffn_in")
        dh2 = mm_nt(dgu, full["w_ffn_in"], BF, "mm_nt_ffn_in")
        token = send_grads(["w_ffn_out", "w_ffn_in"], gfull, [i, i], dh2, "scatter_ffn%d" % i)
        mods = mods + token[0:1, 0:1]
        dx2, s2, dmix = norm_bwd(dh2, sv["x2"], dx, norm2_g[i:i + 1], mods, 3, "norm_bwd_mix",
                                 branch=(sv["mix"], mods, 2))
        dg1 = s2[3:4]
        if i % 2 == 0:
            do = glu_bwd(dmix, sv["o"], "glu_bwd")
            gfull["ssm_w_out"] = mm_tn(sv["z"], do, N_CHIP, "mm_tn_ssm_out")
            dz = mm_nt(do, full["ssm_w_out"], BF, "mm_nt_ssm_out")
            dh, dd, dab, db, dc = s5_bwd(dz, sv["y"], sv["h"], sv["states"], s5[j]["tc"], s5[j]["tbt"], s5[j]["pwr"],
                                         ssm_d[j:j + 1], "s5_bwd")
            ds5[j] = (dd, dab, db, dc)
        else:
            gfull["conv_w_out"] = mm_tn(sv["mc"], dmix, 1, "mm_tn_conv_out").reshape(N_CHIP, D // N_CHIP, D)
            dmc = mm_nt(dmix, full["conv_w_out"].reshape(1, D, D), BF, "mm_nt_conv_out")
            dbg, dcg, dvv, dcw = conv_bwd(dmc, sv["p"], cw_full[j], "conv_bwd")
            dp = jnp.concatenate([dbg, dcg, dvv], axis=1)
            gfull["conv_w_in"] = mm_tn(sv["h"], dp, N_CHIP, "mm_tn_conv_in")
            dh = mm_nt(dp, full["conv_w_in"], BF, "mm_nt_conv_in")
            dconv_w[j] = dcw[0:3]
        dmods_i = [s2[0:2], dg2]
        if i > 0:
            dx, s1, dff = norm_bwd(dh, sv["x"], dx2, norm1_g[i:i + 1], mods, 0, "norm_bwd_ffn",
                                   branch=(saved[i - 1]["ff"], mods_all[i - 1], 5))
            dg2 = s1[3:4]
        else:
            dx, s1 = norm_bwd(dh, sv["x"], dx2, norm1_g[i:i + 1], mods, 0, "norm_bwd")
        dmods[i] = jnp.concatenate([s1[0:2], dg1] + dmods_i, axis=0).reshape(6 * D)
        dnorm1[i], dnorm2[i] = s1[2], s2[2]
        names = ["ssm_w_out"] if i % 2 == 0 else ["conv_w_out", "conv_w_in"]
        token = send_grads(names, gfull, [j] * len(names), dx, "scatter_mix%d" % i)

    small = dict(norm1_g=jnp.stack(dnorm1), norm2_g=jnp.stack(dnorm2), b_ada=jnp.stack(dmods),
                 final_g=dfinal[0] + token[0, 0])
    per = {n: [] for n in ('ssm_a_re', 'ssm_a_im', 'ssm_log_step', 'ssm_b_re', 'ssm_b_im', 'ssm_c_re', 'ssm_c_im', 'ssm_d')}
    GL = G // nkb
    for j, (dd, dab, db, dc) in enumerate(ds5):
        dab = jnp.sum(dab, axis=1).reshape(nkb, 2, GL, SSM_STATE)
        g_abr, g_abi = dab[:, 0].reshape(G, SSM_STATE), dab[:, 1].reshape(G, SSM_STATE)
        db, dc = db.reshape(G, SSM_GROUP, 2, SSM_STATE), dc.reshape(G, SSM_GROUP, 2, SSM_STATE)
        gb_re, gb_im, gc_re, gc_im = db[:, :, 0], db[:, :, 1], dc[:, :, 0], dc[:, :, 1]
        ga_re, ga_im, gls, gbr, gbi = s5[j]["vjp"]((g_abr, g_abi, jnp.swapaxes(gb_re, 1, 2), jnp.swapaxes(gb_im, 1, 2)))
        for n, val in zip(per, (ga_re, ga_im, gls, gbr, gbi, gc_re, -gc_im, jnp.sum(dd, axis=0))):
            per[n].append(val)
    small.update({n: jnp.stack(vals) for n, vals in per.items()})
    dcw_full = jnp.stack(dconv_w)

    my_loss = loss_blk[0:1, 0:1]
    slab_like = [W[n] for n in SLAB] + [dcw_full, my_loss]
    rows64 = 8 * N_DEV
    slab = _pack([small[n] for n in SLAB] + [dcw_full, my_loss], rows64)
    per_dev = slab.shape[0] // N_DEV
    x_sems, x_srcs, x_lands, token = exchange_start(
        [(slab.reshape(N_DEV, per_dev, SLAB_W), True), (_pack([small["b_ada"]]), False)], dx, "small_scatter")

    early = [n for n in SHARDED if n != "ssm_w_out"]
    land_grads(early, token)
    mine = [reduce4(gland[n], "reduce4_" + n) for n in early]

    parts, dm_all = exchange_wait(x_sems, x_srcs, x_lands, [True, False], mine[-1][0, :8, :128], "small_landed")
    t_sems, t_srcs, t_lands, token = exchange_start([(sum8(parts, "sum_small"), False)], dm_all, "small_gather")
    out = {}

    w_sems, w_srcs, w_lands, token2 = swap_start(mine, "swap_start")
    dm_all = dm_all.reshape(N_DEV, -1)[:, :b_ada.size].reshape(N_DEV, nlayer, N_CHIP, NA)
    dm_sh = jnp.transpose(lax.dynamic_index_in_dim(dm_all, chip, axis=2, keepdims=False), (1, 0, 2))
    res = adamw_ada(jnp.transpose(c_all) + token[0:1, 0:1] + token2[0:1, 0:1], dm_sh, w_ada, m_w_ada, v_w_ada,
                    "adamw_ada")
    out["g", "w_ada"], out["d", "w_ada"], out["m", "w_ada"], out["v", "w_ada"] = res

    g_slab = exchange_wait(t_sems, t_srcs, t_lands, [False], out["g", "w_ada"], "small_total")[0]
    g_slab = g_slab.reshape(slab.shape)
    d_slab, m_slab, v_slab = adamw_slab(
        g_slab, _pack([W[n] for n in SLAB] + [jnp.zeros_like(dcw_full)], rows64),
        _pack([Mo[n] for n in SLAB] + [jnp.zeros_like(dcw_full)], rows64),
        _pack([Vo[n] for n in SLAB] + [jnp.ones_like(dcw_full)], rows64), "adamw_slab")
    for k, slab in zip(("g", "d", "m", "v"), (g_slab, d_slab, m_slab, v_slab)):
        for n, val in zip(SLAB, _unpack(slab, slab_like)):
            out[k, n] = val
    g_cw = lax.dynamic_slice_in_dim(_unpack(g_slab, slab_like)[-2], chip * conv_w.shape[2], conv_w.shape[2], axis=2)
    out["g", "conv_w"] = g_cw
    out["d", "conv_w"], out["m", "conv_w"], out["v", "conv_w"] = [
        r.reshape(conv_w.shape) for r in adamw_plain(conv_w.reshape(-1, conv_w.shape[2]), m_conv_w.reshape(-1, conv_w.shape[2]),
                                                     v_conv_w.reshape(-1, conv_w.shape[2]), g_cw.reshape(-1, conv_w.shape[2]),
                                                     "adamw_conv_w")]

    mine, theirs = swap_wait(w_sems, w_srcs, w_lands, d_slab, "swap_wait")
    for n, ga, gb in zip(early, mine, theirs):
        r = adamw_sharded(W[n], Mo[n], Vo[n], ga, gb, "adamw_" + n)
        out["g", n], out["d", n], out["m", n], out["v", n] = r

    land_grads(["ssm_w_out"], out["g", "w_ffn_out"])
    ga = reduce4(gland["ssm_w_out"], "reduce4_ssm_w_out")
    gb = swap_siblings([ga], "swap_siblings")[0]
    r = adamw_sharded(ssm_w_out, m_ssm_w_out, v_ssm_w_out, ga, gb, "adamw_ssm_w_out")
    out["g", "ssm_w_out"], out["d", "ssm_w_out"], out["m", "ssm_w_out"], out["v", "ssm_w_out"] = r

    loss = _unpack(g_slab, slab_like)[-1][0, 0]
    return (loss, dx[None], *[out["g", n] for n in WEIGHTS], *[out["d", n] for n in WEIGHTS],
            *[out["m", n] for n in WEIGHTS], *[out["v", n] for n in WEIGHTS])
```

```python
import math

import jax
import jax.numpy as jnp
from jax import lax
from jax.experimental import pallas as pl
from jax.experimental.pallas import tpu as pltpu

F32 = jnp.float32
BF = jnp.bfloat16
MESH = pl.DeviceIdType.MESH
ANY = pl.BlockSpec(memory_space=pl.ANY)

N_DEV = 8
N_CHIP = 4
SSM_GROUP = 16
SSM_STATE = 64
S5_BLOCK = 256
RMS_EPS = 1e-6
ADAM_LR, ADAM_B1, ADAM_B2, ADAM_EPS, ADAM_WD, ADAM_STEP = 0.001, 0.9, 0.999, 1e-08, 0.01, 10
V7X_VMEM_BYTES = 64 * 1024 * 1024
VMEM_LIMIT = V7X_VMEM_BYTES - 12 * 1024 * 1024
SLAB_W = 1024
GELU_C = math.sqrt(2.0 / math.pi)
GELU_A = 0.044715


def _cp(*sem):
    return pltpu.CompilerParams(dimension_semantics=sem if sem else None, vmem_limit_bytes=VMEM_LIMIT)


def _tile(n, prefs):
    for p in prefs:
        if p <= n and n % p == 0:
            return p
    return n


def _sigmoid(v):
    return 0.5 * jnp.tanh(0.5 * v) + 0.5


def _axes():
    return lax.axis_index("x"), lax.axis_index("y"), lax.axis_index("c")


def _flip(v, k):
    return 1 - v if k else v


def gather8(v, name):
    R, C = v.shape

    def body(v_ref, o_ref, ssem, rsem, lsem):
        x, y, c = _axes()
        me = 4 * x + 2 * y + c
        loc = pltpu.make_async_copy(v_ref, o_ref.at[me], lsem)
        loc.start()
        copies = []
        for k in range(1, N_DEV):
            peer = (_flip(x, (k >> 2) & 1), _flip(y, (k >> 1) & 1), _flip(c, k & 1))
            cp = pltpu.make_async_remote_copy(src_ref=v_ref, dst_ref=o_ref.at[me], send_sem=ssem.at[k - 1],
                                              recv_sem=rsem.at[k - 1], device_id=peer, device_id_type=MESH)
            cp.start()
            copies.append(cp)
        for cp in copies:
            cp.wait()
        loc.wait()

    return pl.pallas_call(
        body, name=name,
        out_shape=jax.ShapeDtypeStruct((N_DEV, R, C), v.dtype),
        in_specs=[pl.BlockSpec(memory_space=pltpu.VMEM)],
        out_specs=pl.BlockSpec(memory_space=pltpu.VMEM),
        scratch_shapes=[pltpu.SemaphoreType.DMA((N_DEV - 1,)), pltpu.SemaphoreType.DMA((N_DEV - 1,)),
                        pltpu.SemaphoreType.DMA],
        compiler_params=pltpu.CompilerParams(vmem_limit_bytes=VMEM_LIMIT),
    )(v)


HBM = pl.BlockSpec(memory_space=pltpu.HBM)
SEM = pl.BlockSpec(memory_space=pltpu.SEMAPHORE)
EFFECT = pltpu.SideEffectType.DATAFLOW_SIDE_EFFECTING


def _in_hbm(a):
    return pltpu.with_memory_space_constraint(a, pltpu.HBM)


def _chip_peers(x, y, c):
    out = []
    for k in range(1, N_CHIP):
        px, py = _flip(x, k >> 1), _flip(y, k & 1)
        out.append(((px, py, c), 2 * px + py))
    return out


def _my_half(ref, c):
    rows = ref.shape[0] // 2
    return pl.ds(pl.multiple_of(c * rows, 16), rows)


def relay_start(lands, after, name):
    n = len(lands)

    def body(*refs):
        land = refs[:n]
        ssem, rsem = refs[n + 1:n + 3]
        token = refs[-1]
        x, y, c = _axes()
        for a in range(n):
            half = _my_half(land[a].at[0], c)
            for k, (_, pchip) in enumerate(_chip_peers(x, y, c)):
                pltpu.make_async_remote_copy(src_ref=land[a].at[pchip, half], dst_ref=land[a].at[pchip, half],
                                             send_sem=ssem.at[3 * a + k], recv_sem=rsem.at[3 * a + k],
                                             device_id=(x, y, 1 - c), device_id_type=MESH).start()
        token[...] = jnp.zeros_like(token)

    out_shape = ([pltpu.SemaphoreType.DMA((3 * n,)), pltpu.SemaphoreType.DMA((3 * n,))]
                 + [pltpu.HBM(l.shape, l.dtype) for l in lands] + [jax.ShapeDtypeStruct((8, 128), F32)])
    res = pl.pallas_call(
        body, name=name, out_shape=out_shape, in_specs=[HBM] * n + [ANY],
        out_specs=[SEM, SEM] + [HBM] * n + [pl.BlockSpec(memory_space=pltpu.VMEM)],
        input_output_aliases={a: 2 + a for a in range(n)},
        compiler_params=pltpu.CompilerParams(has_side_effects=EFFECT),
    )(*lands, after)
    return tuple(res[:2]), list(res[2:2 + n]), res[-1]


def relay_wait(sems, lands, after, name):
    n = len(lands)

    def body(*refs):
        land = refs[:n]
        ssem, rsem = refs[n:n + 2]
        x, y, c = _axes()
        for a in range(n):
            mine, theirs = _my_half(land[a].at[0], c), _my_half(land[a].at[0], 1 - c)
            for k, (_, pchip) in enumerate(_chip_peers(x, y, c)):
                cp = pltpu.make_async_remote_copy(src_ref=land[a].at[pchip, mine], dst_ref=land[a].at[pchip, theirs],
                                                  send_sem=ssem.at[3 * a + k], recv_sem=rsem.at[3 * a + k],
                                                  device_id=(x, y, 1 - c), device_id_type=MESH)
                cp.wait_send()
                cp.wait_recv()

    res = pl.pallas_call(
        body, name=name, out_shape=[pltpu.HBM(l.shape, l.dtype) for l in lands],
        in_specs=[HBM] * n + [SEM, SEM, ANY], out_specs=[HBM] * n,
        input_output_aliases={a: a for a in range(n)},
        compiler_params=pltpu.CompilerParams(has_side_effects=EFFECT),
    )(*lands, *sems, after)
    return list(res)


def gather_start(shards, after, name):
    n = len(shards)

    def body(*refs):
        src, land = refs[:n], refs[n:2 * n]
        ssem, rsem, lsem = refs[2 * n + 1:2 * n + 4]
        token = refs[-1]
        x, y, c = _axes()
        chip = 2 * x + y
        for a in range(n):
            pltpu.make_async_copy(src[a], land[a].at[chip], lsem.at[a]).start()
            half = _my_half(src[a], c)
            for k, (peer, _) in enumerate(_chip_peers(x, y, c)):
                pltpu.make_async_remote_copy(src_ref=src[a].at[half], dst_ref=land[a].at[chip, half],
                                             send_sem=ssem.at[3 * a + k], recv_sem=rsem.at[3 * a + k],
                                             device_id=peer, device_id_type=MESH).start()
        token[...] = jnp.zeros_like(token)

    lands = [lax.empty((N_CHIP,) + s.shape, s.dtype) for s in shards]
    out_shape = ([pltpu.SemaphoreType.DMA((3 * n,)), pltpu.SemaphoreType.DMA((3 * n,)), pltpu.SemaphoreType.DMA((n,))]
                 + [pltpu.HBM(s.shape, s.dtype) for s in shards] + [pltpu.HBM(l.shape, l.dtype) for l in lands]
                 + [jax.ShapeDtypeStruct((8, 128), F32)])
    res = pl.pallas_call(
        body, name=name, out_shape=out_shape, in_specs=[HBM] * (2 * n) + [ANY],
        out_specs=[SEM, SEM, SEM] + [HBM] * (2 * n) + [pl.BlockSpec(memory_space=pltpu.VMEM)],
        input_output_aliases={a: 3 + a for a in range(2 * n)},
        compiler_params=pltpu.CompilerParams(has_side_effects=EFFECT),
    )(*[_in_hbm(s) for s in shards], *[_in_hbm(l) for l in lands], after)
    return tuple(res[:3]), list(res[3:3 + n]), list(res[3 + n:3 + 2 * n]), res[-1]


def gather_wait(sems, srcs, lands, idx, after, name):
    m = len(idx)

    def body(*refs):
        src, land = refs[:m], refs[m:2 * m]
        ssem, rsem, lsem = refs[2 * m:2 * m + 3]
        x, y, c = _axes()
        chip = 2 * x + y
        for j, a in enumerate(idx):
            half = _my_half(src[j], c)
            for k, (peer, pchip) in enumerate(_chip_peers(x, y, c)):
                cp = pltpu.make_async_remote_copy(src_ref=src[j].at[half], dst_ref=land[j].at[pchip, half],
                                                  send_sem=ssem.at[3 * a + k], recv_sem=rsem.at[3 * a + k],
                                                  device_id=peer, device_id_type=MESH)
                cp.wait_send()
                cp.wait_recv()
            pltpu.make_async_copy(src[j], land[j].at[chip], lsem.at[a]).wait()

    s_in = [srcs[a] for a in idx]
    l_in = [lands[a] for a in idx]
    res = pl.pallas_call(
        body, name=name,
        out_shape=[pltpu.HBM(s.shape, s.dtype) for s in s_in] + [pltpu.HBM(l.shape, l.dtype) for l in l_in],
        in_specs=[HBM] * (2 * m) + [SEM, SEM, SEM, ANY], out_specs=[HBM] * (2 * m),
        input_output_aliases={a: a for a in range(2 * m)},
        compiler_params=pltpu.CompilerParams(has_side_effects=EFFECT),
    )(*s_in, *l_in, *sems, after)
    return list(res[m:])


def scatter_start(grads, lands, slot, after, name):
    n = len(grads)

    def body(*refs):
        src, land = refs[:n], refs[n:2 * n]
        ssem, rsem, lsem = refs[2 * n + 1:2 * n + 4]
        token = refs[-1]
        x, y, c = _axes()
        chip = 2 * x + y
        for a in range(n):
            pltpu.make_async_copy(src[a].at[chip], land[a].at[slot[a], chip], lsem.at[a]).start()
            for k, (peer, pchip) in enumerate(_chip_peers(x, y, c)):
                pltpu.make_async_remote_copy(src_ref=src[a].at[pchip], dst_ref=land[a].at[slot[a], chip],
                                             send_sem=ssem.at[3 * a + k], recv_sem=rsem.at[3 * a + k],
                                             device_id=peer, device_id_type=MESH).start()
        token[...] = jnp.zeros_like(token)

    out_shape = ([pltpu.SemaphoreType.DMA((3 * n,)), pltpu.SemaphoreType.DMA((3 * n,)), pltpu.SemaphoreType.DMA((n,))]
                 + [pltpu.HBM(g.shape, g.dtype) for g in grads] + [pltpu.HBM(l.shape, l.dtype) for l in lands]
                 + [jax.ShapeDtypeStruct((8, 128), F32)])
    res = pl.pallas_call(
        body, name=name, out_shape=out_shape, in_specs=[HBM] * (2 * n) + [ANY],
        out_specs=[SEM, SEM, SEM] + [HBM] * (2 * n) + [pl.BlockSpec(memory_space=pltpu.VMEM)],
        input_output_aliases={a: 3 + a for a in range(2 * n)},
        compiler_params=pltpu.CompilerParams(has_side_effects=EFFECT),
    )(*[_in_hbm(g) for g in grads], *[_in_hbm(l) for l in lands], after)
    return tuple(res[:3]), list(res[3:3 + n]), list(res[3 + n:3 + 2 * n]), res[-1]


def scatter_wait(sems, grads, lands, slot, after, name):
    n = len(grads)

    def body(*refs):
        src, land = refs[:n], refs[n:2 * n]
        ssem, rsem, lsem = refs[2 * n:2 * n + 3]
        x, y, c = _axes()
        chip = 2 * x + y
        for a in range(n):
            for k, (peer, pchip) in enumerate(_chip_peers(x, y, c)):
                cp = pltpu.make_async_remote_copy(src_ref=src[a].at[pchip], dst_ref=land[a].at[slot[a], pchip],
                                                  send_sem=ssem.at[3 * a + k], recv_sem=rsem.at[3 * a + k],
                                                  device_id=peer, device_id_type=MESH)
                cp.wait_send()
                cp.wait_recv()
            pltpu.make_async_copy(src[a].at[chip], land[a].at[slot[a], chip], lsem.at[a]).wait()

    res = pl.pallas_call(
        body, name=name,
        out_shape=[pltpu.HBM(g.shape, g.dtype) for g in grads] + [pltpu.HBM(l.shape, l.dtype) for l in lands],
        in_specs=[HBM] * (2 * n) + [SEM, SEM, SEM, ANY], out_specs=[HBM] * (2 * n),
        input_output_aliases={a: a for a in range(2 * n)},
        compiler_params=pltpu.CompilerParams(has_side_effects=EFFECT),
    )(*grads, *lands, *sems, after)
    return list(res[n:])


def reduce4(land, name):
    nl, _, R, C = land.shape
    TR = _adam_rows(R, C)

    def body(l_ref, o_ref):
        o_ref[...] = ((l_ref[0].astype(F32) + l_ref[1].astype(F32)) + l_ref[2].astype(F32)) + l_ref[3].astype(F32)

    return pl.pallas_call(
        body, name=name, grid=(nl, R // TR),
        in_specs=[pl.BlockSpec((None, N_CHIP, TR, C), lambda i, r: (i, 0, r, 0))],
        out_specs=pl.BlockSpec((None, TR, C), lambda i, r: (i, r, 0)),
        out_shape=jax.ShapeDtypeStruct((nl, R, C), F32), compiler_params=_cp("parallel", "parallel"))(land)


def swap_siblings(arrs, name):
    n = len(arrs)

    def body(*refs):
        src, dst = refs[:n], refs[n:2 * n]
        ssem, rsem = refs[2 * n:]
        x, y, c = _axes()
        cps = [pltpu.make_async_remote_copy(src_ref=src[a], dst_ref=dst[a], send_sem=ssem.at[a], recv_sem=rsem.at[a],
                                            device_id=(x, y, 1 - c), device_id_type=MESH) for a in range(n)]
        for cp in cps:
            cp.start()
        for cp in cps:
            cp.wait()

    return pl.pallas_call(
        body, name=name, out_shape=[jax.ShapeDtypeStruct(a.shape, a.dtype) for a in arrs],
        in_specs=[ANY] * n, out_specs=[ANY] * n,
        scratch_shapes=[pltpu.SemaphoreType.DMA((n,)), pltpu.SemaphoreType.DMA((n,))],
        compiler_params=pltpu.CompilerParams(vmem_limit_bytes=VMEM_LIMIT),
    )(*arrs)


def swap_start(arrs, name):
    n = len(arrs)

    def body(*refs):
        src, land = refs[:n], refs[n:2 * n]
        ssem, rsem = refs[2 * n:2 * n + 2]
        token = refs[-1]
        x, y, c = _axes()
        for a in range(n):
            pltpu.make_async_remote_copy(src_ref=src[a], dst_ref=land[a], send_sem=ssem.at[a], recv_sem=rsem.at[a],
                                         device_id=(x, y, 1 - c), device_id_type=MESH).start()
        token[...] = jnp.zeros_like(token)

    lands = [lax.empty(a.shape, a.dtype) for a in arrs]
    out_shape = ([pltpu.SemaphoreType.DMA((n,)), pltpu.SemaphoreType.DMA((n,))]
                 + [pltpu.HBM(a.shape, a.dtype) for a in arrs] * 2 + [jax.ShapeDtypeStruct((8, 128), F32)])
    res = pl.pallas_call(
        body, name=name, out_shape=out_shape, in_specs=[HBM] * (2 * n),
        out_specs=[SEM, SEM] + [HBM] * (2 * n) + [pl.BlockSpec(memory_space=pltpu.VMEM)],
        input_output_aliases={a: 2 + a for a in range(2 * n)},
        compiler_params=pltpu.CompilerParams(has_side_effects=EFFECT),
    )(*[_in_hbm(a) for a in arrs], *[_in_hbm(l) for l in lands])
    return tuple(res[:2]), list(res[2:2 + n]), list(res[2 + n:2 + 2 * n]), res[-1]


def swap_wait(sems, srcs, lands, after, name):
    n = len(srcs)

    def body(*refs):
        src, land = refs[:n], refs[n:2 * n]
        ssem, rsem = refs[2 * n:2 * n + 2]
        x, y, c = _axes()
        for a in range(n):
            cp = pltpu.make_async_remote_copy(src_ref=src[a], dst_ref=land[a], send_sem=ssem.at[a],
                                              recv_sem=rsem.at[a], device_id=(x, y, 1 - c), device_id_type=MESH)
            cp.wait_send()
            cp.wait_recv()

    res = pl.pallas_call(
        body, name=name, out_shape=[pltpu.HBM(a.shape, a.dtype) for a in srcs] * 2,
        in_specs=[HBM] * (2 * n) + [SEM, SEM, ANY], out_specs=[HBM] * (2 * n),
        input_output_aliases={a: a for a in range(2 * n)},
        compiler_params=pltpu.CompilerParams(has_side_effects=EFFECT),
    )(*srcs, *lands, *sems, after)
    return list(res[:n]), list(res[n:])


def _all_peers(x, y, c):
    out = []
    for k in range(1, N_DEV):
        px, py, pc = _flip(x, (k >> 2) & 1), _flip(y, (k >> 1) & 1), _flip(c, k & 1)
        out.append(((px, py, pc), 4 * px + 2 * py + pc))
    return out


def exchange_start(items, after, name):
    n = len(items)

    def body(*refs):
        src, land = refs[:n], refs[n:2 * n]
        ssem, rsem, lsem = refs[2 * n + 1:2 * n + 4]
        token = refs[-1]
        x, y, c = _axes()
        me = 4 * x + 2 * y + c
        for a, (_, scatter) in enumerate(items):
            pltpu.make_async_copy(src[a].at[me] if scatter else src[a], land[a].at[me], lsem.at[a]).start()
            for k, (peer, p) in enumerate(_all_peers(x, y, c)):
                pltpu.make_async_remote_copy(src_ref=src[a].at[p] if scatter else src[a], dst_ref=land[a].at[me],
                                             send_sem=ssem.at[7 * a + k], recv_sem=rsem.at[7 * a + k],
                                             device_id=peer, device_id_type=MESH).start()
        token[...] = jnp.zeros_like(token)

    srcs = [s for s, _ in items]
    lands = [lax.empty(s.shape if sc else (N_DEV,) + s.shape, s.dtype) for s, sc in items]
    out_shape = ([pltpu.SemaphoreType.DMA((7 * n,)), pltpu.SemaphoreType.DMA((7 * n,)), pltpu.SemaphoreType.DMA((n,))]
                 + [pltpu.HBM(s.shape, s.dtype) for s in srcs] + [pltpu.HBM(l.shape, l.dtype) for l in lands]
                 + [jax.ShapeDtypeStruct((8, 128), F32)])
    res = pl.pallas_call(
        body, name=name, out_shape=out_shape, in_specs=[HBM] * (2 * n) + [ANY],
        out_specs=[SEM, SEM, SEM] + [HBM] * (2 * n) + [pl.BlockSpec(memory_space=pltpu.VMEM)],
        input_output_aliases={a: 3 + a for a in range(2 * n)},
        compiler_params=pltpu.CompilerParams(has_side_effects=EFFECT),
    )(*[_in_hbm(s) for s in srcs], *[_in_hbm(l) for l in lands], after)
    return tuple(res[:3]), list(res[3:3 + n]), list(res[3 + n:3 + 2 * n]), res[-1]


def exchange_wait(sems, srcs, lands, scatter, after, name):
    n = len(srcs)

    def body(*refs):
        src, land = refs[:n], refs[n:2 * n]
        ssem, rsem, lsem = refs[2 * n:2 * n + 3]
        x, y, c = _axes()
        me = 4 * x + 2 * y + c
        for a in range(n):
            for k, (peer, p) in enumerate(_all_peers(x, y, c)):
                cp = pltpu.make_async_remote_copy(src_ref=src[a].at[p] if scatter[a] else src[a],
                                                  dst_ref=land[a].at[p], send_sem=ssem.at[7 * a + k],
                                                  recv_sem=rsem.at[7 * a + k], device_id=peer, device_id_type=MESH)
                cp.wait_send()
                cp.wait_recv()
            pltpu.make_async_copy(src[a].at[me] if scatter[a] else src[a], land[a].at[me], lsem.at[a]).wait()

    res = pl.pallas_call(
        body, name=name,
        out_shape=[pltpu.HBM(s.shape, s.dtype) for s in srcs] + [pltpu.HBM(l.shape, l.dtype) for l in lands],
        in_specs=[HBM] * (2 * n) + [SEM, SEM, SEM, ANY], out_specs=[HBM] * (2 * n),
        input_output_aliases={a: a for a in range(2 * n)},
        compiler_params=pltpu.CompilerParams(has_side_effects=EFFECT),
    )(*srcs, *lands, *sems, after)
    return list(res[n:])


def sum8(parts, name):
    _, P, C = parts.shape

    def body(p_ref, o_ref):
        tot = p_ref[0]
        for d in range(1, N_DEV):
            tot = tot + p_ref[d]
        o_ref[...] = tot

    return pl.pallas_call(body, name=name, out_shape=jax.ShapeDtypeStruct((P, C), F32),
                          compiler_params=pltpu.CompilerParams(vmem_limit_bytes=VMEM_LIMIT))(parts)


def mm_nn(a, w, out_dtype, name, res=None, gate=None):
    M, K = a.shape
    S, _, Ns = w.shape
    TM = _tile(M, (1024, 512, 256) if K <= 1024 else (512, 256))
    TN = _tile(Ns, (1408, 1024, 768, 512, 256, 128))
    nj = Ns // TN
    fused = res is not None

    def body(*refs):
        if fused:
            a_ref, w_ref, r_ref, g_ref, f_ref, o_ref = refs
        else:
            a_ref, w_ref, f_ref = refs
        f = jnp.dot(a_ref[...], w_ref[...], preferred_element_type=F32)
        f_ref[...] = f.astype(f_ref.dtype)
        if fused:
            o_ref[...] = r_ref[...] + g_ref[...] * f

    col = lambda s, j, i: (i, s * nj + j)
    in_specs = [pl.BlockSpec((TM, K), lambda s, j, i: (i, 0)), pl.BlockSpec((None, K, TN), lambda s, j, i: (s, 0, j))]
    out_specs = [pl.BlockSpec((TM, TN), col)]
    out_shape = [jax.ShapeDtypeStruct((M, S * Ns), out_dtype)]
    args = [a, w]
    if fused:
        in_specs += [pl.BlockSpec((TM, TN), col), pl.BlockSpec((1, TN), lambda s, j, i: (0, s * nj + j))]
        out_specs.append(pl.BlockSpec((TM, TN), col))
        out_shape.append(jax.ShapeDtypeStruct((M, S * Ns), F32))
        args += [res, gate]
    out = pl.pallas_call(body, name=name, grid=(S, nj, M // TM), in_specs=in_specs, out_specs=out_specs,
                         out_shape=out_shape, compiler_params=_cp("parallel", "parallel", "parallel"))(*args)
    return tuple(out) if fused else out[0]


def mm_nt(g, w, out_dtype, name):
    g3 = g if g.ndim == 3 else g[None]
    Q, M, F = g3.shape
    S, K, Ns = w.shape
    TM = _tile(M, (1024, 512, 256) if K <= 1024 else (512, 256))
    TN = _tile(Ns, (1408, 1024, 768, 512, 256, 128))
    nj = Ns // TN
    nred = S * nj
    per_part = F // TN

    def body(g_ref, w_ref, o_ref, acc):
        n = pl.program_id(1)

        @pl.when(n == 0)
        def _():
            acc[...] = jnp.zeros_like(acc)

        acc[...] += lax.dot_general(g_ref[...], w_ref[...], (((1,), (1,)), ((), ())), preferred_element_type=F32)

        @pl.when(n == nred - 1)
        def _():
            o_ref[...] = acc[...].astype(o_ref.dtype)

    return pl.pallas_call(
        body, name=name, grid=(M // TM, nred),
        in_specs=[pl.BlockSpec((None, TM, TN), lambda i, n: (n // per_part, i, n % per_part)),
                  pl.BlockSpec((None, K, TN), lambda i, n: (n // nj, 0, n % nj))],
        out_specs=pl.BlockSpec((TM, K), lambda i, n: (i, 0)),
        out_shape=jax.ShapeDtypeStruct((M, K), out_dtype),
        scratch_shapes=[pltpu.VMEM((TM, K), F32)],
        compiler_params=_cp("parallel", "arbitrary"))(g3, w)


def mm_tn(a, g, S, name):
    M, K = a.shape
    g3 = g if g.ndim == 3 else g[None]
    Q, _, F = g3.shape
    Ns = Q * F // S
    TK = _tile(K, (256, 128))
    TN = _tile(Ns, (1408, 1024, 768, 512, 256, 128))
    nj = Ns // TN
    per_part = F // TN

    def body(a_ref, g_ref, o_ref):
        o_ref[...] = lax.dot_general(a_ref[...], g_ref[...], (((0,), (0,)), ((), ())),
                                     preferred_element_type=F32).astype(o_ref.dtype)

    return pl.pallas_call(
        body, name=name, grid=(S * nj, K // TK),
        in_specs=[pl.BlockSpec((M, TK), lambda n, k: (0, k)),
                  pl.BlockSpec((None, M, TN), lambda n, k: (n // per_part, 0, n % per_part))],
        out_specs=pl.BlockSpec((None, TK, TN), lambda n, k: (n // nj, k, n % nj)),
        out_shape=jax.ShapeDtypeStruct((S, K, Ns), BF),
        compiler_params=_cp("parallel", "parallel"))(a, g3)


ROW_TILE = (512, 256)


def _rows(TL, D):
    return pl.BlockSpec((TL, D), lambda i: (i, 0))


def _fixed(R, D):
    return pl.BlockSpec((R, D), lambda i: (0, 0))


def _rowsum8(v):
    T, D = v.shape
    return jnp.sum(v.reshape(T // 8, 8, D), axis=0)


def _norm_parts(xv):
    r = lax.rsqrt(jnp.mean(xv * xv, axis=-1, keepdims=True) + RMS_EPS)
    return xv * r, r


def norm_mod(x, gamma, mods, k_shift, out_dtype, name):
    L, D = x.shape
    TL = _tile(L, ROW_TILE)
    H = D // 2

    def body(xl_ref, xr_ref, g_ref, m_ref, o_ref):
        xl, xr = xl_ref[...], xr_ref[...]
        ms = (jnp.sum(xl * xl, axis=-1, keepdims=True) + jnp.sum(xr * xr, axis=-1, keepdims=True)) * (1.0 / D)
        r = lax.rsqrt(ms + RMS_EPS)
        for xv, cols in ((xl, slice(0, H)), (xr, slice(H, D))):
            sh, sc = m_ref[k_shift:k_shift + 1, cols], m_ref[k_shift + 1:k_shift + 2, cols]
            o_ref[:, cols] = ((xv * r * g_ref[:, cols]) * (1.0 + sc) + sh).astype(o_ref.dtype)

    half = lambda q: pl.BlockSpec((TL, H), lambda i: (i, q))
    return pl.pallas_call(body, name=name, grid=(L // TL,),
                          in_specs=[half(0), half(1), _fixed(1, D), _fixed(6, D)], out_specs=_rows(TL, D),
                          out_shape=jax.ShapeDtypeStruct((L, D), out_dtype),
                          compiler_params=_cp("parallel"))(x, x, gamma, mods)


def norm_bwd(dh, x, dres, gamma, mods, k_shift, name, branch=None):
    L, D = x.shape
    TL = _tile(L, ROW_TILE)
    nacc = 4 if branch else 3

    def body(*refs):
        if branch:
            dh_ref, x_ref, dr_ref, g_ref, m_ref, f_ref, fm_ref, dx_ref, s_ref, df_ref, acc = refs
        else:
            dh_ref, x_ref, dr_ref, g_ref, m_ref, dx_ref, s_ref, acc = refs
        i = pl.program_id(0)

        @pl.when(i == 0)
        def _():
            acc[...] = jnp.zeros_like(acc)

        xn, r = _norm_parts(x_ref[...])
        dh_v = dh_ref[...].astype(F32)
        gam = g_ref[...]
        sc = m_ref[k_shift + 1:k_shift + 2, :]
        dn = dh_v * (1.0 + sc)
        dxn = dn * gam
        dx = dr_ref[...] + r * (dxn - xn * jnp.mean(dxn * xn, axis=-1, keepdims=True))
        dx_ref[...] = dx
        acc[0] += _rowsum8(dh_v)
        acc[1] += _rowsum8(dh_v * (xn * gam))
        acc[2] += _rowsum8(dn * xn)
        if branch:
            df_ref[...] = (dx * fm_ref[branch[2]:branch[2] + 1, :]).astype(df_ref.dtype)
            acc[3] += _rowsum8(dx * f_ref[...].astype(F32))

        @pl.when(i == pl.num_programs(0) - 1)
        def _():
            s_ref[...] = jnp.zeros_like(s_ref)
            for q in range(nacc):
                s_ref[q:q + 1, :] = jnp.sum(acc[q], axis=0, keepdims=True)

    in_specs = [_rows(TL, D), _rows(TL, D), _rows(TL, D), _fixed(1, D), _fixed(6, D)]
    out_specs = [_rows(TL, D), _fixed(8, D)]
    out_shape = [jax.ShapeDtypeStruct((L, D), F32), jax.ShapeDtypeStruct((8, D), F32)]
    args = [dh, x, dres, gamma, mods]
    if branch:
        in_specs += [_rows(TL, D), _fixed(6, D)]
        out_specs.append(_rows(TL, D))
        out_shape.append(jax.ShapeDtypeStruct((L, D), BF))
        args += [branch[0], branch[1]]
    return pl.pallas_call(
        body, name=name, grid=(L // TL,), in_specs=in_specs, out_specs=out_specs, out_shape=out_shape,
        scratch_shapes=[pltpu.VMEM((nacc, 8, D), F32)], compiler_params=_cp("arbitrary"))(*args)


def ffn_in_act(a, w, name):
    M, K = a.shape
    S, _, Ns = w.shape
    half = S // 2
    TM = _tile(M, (512, 256))
    TN = _tile(Ns, (1408, 1024, 768, 512, 256, 128))
    nj = Ns // TN

    def body(a_ref, wg_ref, wu_ref, gu_ref, act_ref):
        av = a_ref[...]
        g = jnp.dot(av, wg_ref[...], preferred_element_type=F32)
        u = jnp.dot(av, wu_ref[...], preferred_element_type=F32)
        gu_ref[0] = g.astype(gu_ref.dtype)
        gu_ref[1] = u.astype(gu_ref.dtype)
        act_ref[...] = (g * _sigmoid(g) * u).astype(act_ref.dtype)

    return pl.pallas_call(
        body, name=name, grid=(half, nj, M // TM),
        in_specs=[pl.BlockSpec((TM, K), lambda s, j, i: (i, 0)),
                  pl.BlockSpec((None, K, TN), lambda s, j, i: (s, 0, j)),
                  pl.BlockSpec((None, K, TN), lambda s, j, i: (s + half, 0, j))],
        out_specs=[pl.BlockSpec((2, TM, TN), lambda s, j, i: (0, i, s * nj + j)),
                   pl.BlockSpec((TM, TN), lambda s, j, i: (i, s * nj + j))],
        out_shape=[jax.ShapeDtypeStruct((2, M, half * Ns), BF), jax.ShapeDtypeStruct((M, half * Ns), BF)],
        compiler_params=_cp("parallel", "parallel", "parallel"))(a, w, w)


def ffn_out_bwd(dff, w2, gu, name):
    M, D = dff.shape
    F = w2.shape[0]
    TM = _tile(M, (512, 256))
    CW = _tile(F, (256, 128))

    def body(d_ref, w_ref, gu_ref, o_ref):
        dv = d_ref[...]

        def product(c):
            return lax.dot_general(dv, w_ref[c:c + CW, :], (((1,), (1,)), ((), ())), preferred_element_type=F32)

        da = product(0)
        for c in range(0, F, CW):
            ahead = product(c + CW) if c + CW < F else None
            g = gu_ref[0, :, c:c + CW].astype(F32)
            u = gu_ref[1, :, c:c + CW].astype(F32)
            s = _sigmoid(g)
            o_ref[0, :, c:c + CW] = (da * u * (s + g * s * (1.0 - s))).astype(o_ref.dtype)
            o_ref[1, :, c:c + CW] = (da * g * s).astype(o_ref.dtype)
            da = ahead

    part = pl.BlockSpec((2, TM, F), lambda i: (0, i, 0))
    return pl.pallas_call(
        body, name=name, grid=(M // TM,),
        in_specs=[pl.BlockSpec((TM, D), lambda i: (i, 0)), pl.BlockSpec((F, D), lambda i: (0, 0)), part],
        out_specs=part, out_shape=jax.ShapeDtypeStruct((2, M, F), BF),
        compiler_params=_cp("parallel"))(dff, w2, gu)


def ssm_out_glu(z, w, x, mods, k_gate, name):
    M, K = z.shape
    S, _, Ns = w.shape
    half = S // 2
    TM = _tile(M, (1024, 512, 256))
    TN = _tile(Ns, (512, 256, 128))
    nj = Ns // TN

    def body(z_ref, wv_ref, wg_ref, x_ref, m_ref, o_ref, mix_ref, y_ref):
        zv = z_ref[...]
        CW = _tile(TN, (256, 128))

        def products(c):
            return (jnp.dot(zv, wv_ref[:, c:c + CW], preferred_element_type=F32),
                    jnp.dot(zv, wg_ref[:, c:c + CW], preferred_element_type=F32))

        cur = products(0)
        for c in range(0, TN, CW):
            ahead = products(c + CW) if c + CW < TN else None
            val, gate = cur
            o_ref[0, :, c:c + CW] = val.astype(o_ref.dtype)
            o_ref[1, :, c:c + CW] = gate.astype(o_ref.dtype)
            mix = val * _sigmoid(gate)
            mix_ref[:, c:c + CW] = mix.astype(mix_ref.dtype)
            y_ref[:, c:c + CW] = x_ref[:, c:c + CW] + m_ref[k_gate:k_gate + 1, c:c + CW] * mix
            cur = ahead

    col = lambda s, j, i: (i, s * nj + j)
    return pl.pallas_call(
        body, name=name, grid=(half, nj, M // TM),
        in_specs=[pl.BlockSpec((TM, K), lambda s, j, i: (i, 0)),
                  pl.BlockSpec((None, K, TN), lambda s, j, i: (s, 0, j)),
                  pl.BlockSpec((None, K, TN), lambda s, j, i: (s + half, 0, j)),
                  pl.BlockSpec((TM, TN), col), pl.BlockSpec((6, TN), lambda s, j, i: (0, s * nj + j))],
        out_specs=[pl.BlockSpec((2, TM, TN), lambda s, j, i: (0, i, s * nj + j)), pl.BlockSpec((TM, TN), col),
                   pl.BlockSpec((TM, TN), col)],
        out_shape=[jax.ShapeDtypeStruct((2, M, half * Ns), BF), jax.ShapeDtypeStruct((M, half * Ns), BF),
                   jax.ShapeDtypeStruct((M, half * Ns), F32)],
        compiler_params=_cp("parallel", "parallel", "parallel"))(z, w, w, x, mods)


def glu_bwd(dmix, o, name):
    _, L, D = o.shape
    TL = _tile(L, ROW_TILE)

    def body(d_ref, o_ref, do_ref):
        d = d_ref[...].astype(F32)
        val = o_ref[0].astype(F32)
        s = _sigmoid(o_ref[1].astype(F32))
        do_ref[0] = (d * s).astype(do_ref.dtype)
        do_ref[1] = (d * val * s * (1.0 - s)).astype(do_ref.dtype)

    part = pl.BlockSpec((2, TL, D), lambda i: (0, i, 0))
    return pl.pallas_call(body, name=name, grid=(L // TL,), in_specs=[_rows(TL, D), part],
                          out_specs=part, out_shape=jax.ShapeDtypeStruct((2, L, D), BF),
                          compiler_params=_cp("parallel"))(dmix, o)


def final_loss(x, target, gamma, f, fmods, k_gate, name):
    L, D = x.shape
    TL = _tile(L, ROW_TILE)

    def body(x_ref, t_ref, g_ref, f_ref, fm_ref, l_ref, dx_ref, s_ref, df_ref, acc, lacc):
        i = pl.program_id(0)

        @pl.when(i == 0)
        def _():
            acc[...] = jnp.zeros_like(acc)
            lacc[...] = jnp.zeros_like(lacc)

        xn, r = _norm_parts(x_ref[...])
        gam = g_ref[...]
        e = xn * gam - t_ref[...]
        lacc[...] += jnp.sum(0.5 * jnp.mean(e * e, axis=-1, keepdims=True), axis=0, keepdims=True)
        dy = e * (1.0 / D)
        dxn = dy * gam
        dx = r * (dxn - xn * jnp.mean(dxn * xn, axis=-1, keepdims=True))
        dx_ref[...] = dx
        df_ref[...] = (dx * fm_ref[k_gate:k_gate + 1, :]).astype(df_ref.dtype)
        acc[0] += _rowsum8(dy * xn)
        acc[1] += _rowsum8(dx * f_ref[...].astype(F32))

        @pl.when(i == pl.num_programs(0) - 1)
        def _():
            s_ref[...] = jnp.zeros_like(s_ref)
            for q in range(2):
                s_ref[q:q + 1, :] = jnp.sum(acc[q], axis=0, keepdims=True)
            l_ref[...] = jnp.broadcast_to(lacc[...], l_ref.shape)

    return pl.pallas_call(
        body, name=name, grid=(L // TL,),
        in_specs=[_rows(TL, D), _rows(TL, D), _fixed(1, D), _rows(TL, D), _fixed(6, D)],
        out_specs=[_fixed(8, 128), _rows(TL, D), _fixed(8, D), _rows(TL, D)],
        out_shape=[jax.ShapeDtypeStruct((8, 128), F32), jax.ShapeDtypeStruct((L, D), F32),
                   jax.ShapeDtypeStruct((8, D), F32), jax.ShapeDtypeStruct((L, D), BF)],
        scratch_shapes=[pltpu.VMEM((2, 8, D), F32), pltpu.VMEM((1, 1), F32)],
        compiler_params=_cp("arbitrary"))(x, target, gamma, f, fmods)


def _col(L, TC, off):
    return pl.BlockSpec((L, TC), lambda j: (0, off + j))


def _shift_down(v, k, row):
    return jnp.where(row >= k, pltpu.roll(v, k, 0), 0.0)


def _shift_up(v, k, row, L):
    return jnp.where(row < L - k, pltpu.roll(v, L - k, 0), 0.0)


def conv_fwd(p, w, name):
    L, D3 = p.shape
    D = D3 // 3
    TC = _tile(D, (128,))
    nc = D // TC

    def body(b_ref, c_ref, v_ref, w_ref, o_ref):
        row = lax.broadcasted_iota(jnp.int32, (L, TC), 0)
        cv = c_ref[...].astype(F32) * v_ref[...].astype(F32)
        conv = w_ref[2:3, :] * cv + w_ref[1:2, :] * _shift_down(cv, 1, row) + w_ref[0:1, :] * _shift_down(cv, 2, row)
        o_ref[...] = (b_ref[...].astype(F32) * conv).astype(o_ref.dtype)

    return pl.pallas_call(
        body, name=name, grid=(nc,),
        in_specs=[_col(L, TC, 0), _col(L, TC, nc), _col(L, TC, 2 * nc), pl.BlockSpec((3, TC), lambda j: (0, j))],
        out_specs=_col(L, TC, 0), out_shape=jax.ShapeDtypeStruct((L, D), BF), compiler_params=_cp("parallel"))(p, p, p, w)


def conv_bwd(dm, p, w, name):
    L, D3 = p.shape
    D = D3 // 3
    TC = _tile(D, (128,))
    nc = D // TC

    def body(dm_ref, b_ref, c_ref, v_ref, w_ref, db_ref, dc_ref, dv_ref, dw_ref):
        row = lax.broadcasted_iota(jnp.int32, (L, TC), 0)
        cg, vv = c_ref[...].astype(F32), v_ref[...].astype(F32)
        cv = cg * vv
        cv1, cv2 = _shift_down(cv, 1, row), _shift_down(cv, 2, row)
        conv = w_ref[2:3, :] * cv + w_ref[1:2, :] * cv1 + w_ref[0:1, :] * cv2
        dmv = dm_ref[...].astype(F32)
        db_ref[...] = (dmv * conv).astype(db_ref.dtype)
        dconv = dmv * b_ref[...].astype(F32)
        dcv = (w_ref[2:3, :] * dconv + w_ref[1:2, :] * _shift_up(dconv, 1, row, L)
               + w_ref[0:1, :] * _shift_up(dconv, 2, row, L))
        dc_ref[...] = (dcv * vv).astype(dc_ref.dtype)
        dv_ref[...] = (dcv * cg).astype(dv_ref.dtype)
        dw_ref[...] = jnp.zeros_like(dw_ref)
        dw_ref[0:1, :] = jnp.sum(dconv * cv2, axis=0, keepdims=True)
        dw_ref[1:2, :] = jnp.sum(dconv * cv1, axis=0, keepdims=True)
        dw_ref[2:3, :] = jnp.sum(dconv * cv, axis=0, keepdims=True)

    one = jax.ShapeDtypeStruct((L, D), BF)
    return pl.pallas_call(
        body, name=name, grid=(nc,),
        in_specs=[_col(L, TC, 0), _col(L, TC, 0), _col(L, TC, nc), _col(L, TC, 2 * nc),
                  pl.BlockSpec((3, TC), lambda j: (0, j))],
        out_specs=[_col(L, TC, 0), _col(L, TC, 0), _col(L, TC, 0), pl.BlockSpec((8, TC), lambda j: (0, j))],
        out_shape=[one, one, one, jax.ShapeDtypeStruct((8, D), F32)],
        compiler_params=_cp("parallel"))(dm, p, p, p, w)


def _gelu(y):
    return 0.5 * y * (1.0 + jnp.tanh(GELU_C * (y + GELU_A * y * y * y)))


def _gelu_grad(y):
    th = jnp.tanh(GELU_C * (y + GELU_A * y * y * y))
    return 0.5 * (1.0 + th) + 0.5 * y * (1.0 - th * th) * GELU_C * (1.0 + 3.0 * GELU_A * y * y)


def _cmul_add(br, bi, ar, ai, sr, si):
    return br + ar * sr - ai * si, bi + ar * si + ai * sr


def _log2(n):
    k = n.bit_length() - 1
    assert 1 << k == n
    return k


def _replicate(P2, W2, P, GLP, transposed):
    shape = (W2, P2) if transposed else (P2, W2)
    k = lax.broadcasted_iota(jnp.int32, shape, 1 if transposed else 0)
    c = lax.broadcasted_iota(jnp.int32, shape, 0 if transposed else 1)
    return ((k >> _log2(P)) == (c >> _log2(GLP))) & ((k & (P - 1)) == (c & (P - 1)))


def _on_diagonal(KB, W2, H, P, GLP, transposed):
    shape = (W2, KB) if transposed else (KB, W2)
    r = lax.broadcasted_iota(jnp.int32, shape, 1 if transposed else 0)
    c = lax.broadcasted_iota(jnp.int32, shape, 0 if transposed else 1)
    return (r >> _log2(H)) == ((c & (GLP - 1)) >> _log2(P))


def _expand(t, dims, transposed):
    KB, W2, H, P, GLP = dims
    rep = _replicate(2 * P, W2, P, GLP, transposed).astype(t.dtype)
    wide = jnp.dot(rep, t, preferred_element_type=F32) if transposed else jnp.dot(t, rep, preferred_element_type=F32)
    return jnp.where(_on_diagonal(KB, W2, H, P, GLP, transposed), wide, 0.0).astype(t.dtype)


def _extract(acc, dims):
    KB, W2, H, P, GLP = dims
    rep = _replicate(2 * P, W2, P, GLP, True).astype(BF)
    kept = jnp.where(_on_diagonal(KB, W2, H, P, GLP, False), acc, 0.0)
    hi = kept.astype(BF)
    lo = (kept - hi.astype(F32)).astype(BF)
    return jnp.dot(hi, rep, preferred_element_type=F32) + jnp.dot(lo, rep, preferred_element_type=F32)


def _cmul(ar, ai, sr, si):
    return ar * sr - ai * si, ar * si + ai * sr


def _chunk_order(TL, CH, transposed):
    out_row = lax.broadcasted_iota(jnp.int32, (TL, TL), 1 if transposed else 0)
    in_row = lax.broadcasted_iota(jnp.int32, (TL, TL), 0 if transposed else 1)
    return in_row == ((out_row & 7) << _log2(CH)) + (out_row >> 3)


def _reorder(perm, v):
    hi = v.astype(perm.dtype)
    lo = (v - hi.astype(F32)).astype(perm.dtype)
    return jnp.dot(perm, hi, preferred_element_type=F32) + jnp.dot(perm, lo, preferred_element_type=F32)


def _interleave(main, side):
    n, m, k = len(main), len(side), 0
    for i, step in enumerate(main):
        step()
        while k < m and (k + 1) * n <= (i + 1) * m:
            side[k]()
            k += 1
    for step in side[k:]:
        step()


S5_CHUNK = 512


def s5_fwd(h, tb, tct, pw, dvec, name):
    L, D = h.shape
    nkb, KB, P2 = tb.shape
    P = P2 // 2
    W = (KB // SSM_GROUP) * P
    W2 = 2 * W
    dims = (KB, W2, SSM_GROUP, P, W)
    TL = _tile(L, (512, 256))
    CH = TL // 8
    NB = 2 if nkb % 2 == 0 else 1
    CK = min(S5_CHUNK, W2)

    def body(h_ref, tb_ref, tct_ref, pw_ref, d_ref, s_ref, y_ref, z_ref, bw, cw, perm, unperm, carry):
        t = pl.program_id(1)

        @pl.when(t == 0)
        def _():
            carry[...] = jnp.zeros_like(carry)
            for b in range(NB):
                bw[b] = _expand(tb_ref[b], dims, False)
                cw[b] = _expand(tct_ref[b], dims, True)
            perm[...] = _chunk_order(TL, CH, False).astype(perm.dtype)
            unperm[...] = _chunk_order(TL, CH, True).astype(perm.dtype)

        hp = _reorder(perm[...], h_ref[...])
        hpb = hp.astype(BF)
        first = lax.broadcasted_iota(jnp.int32, (8, W), 0) == 0

        def project(b):
            def chunk(c):
                def emit():
                    s_ref[:, b * W2 + c:b * W2 + c + CK] = jnp.dot(hpb[:, b * KB:(b + 1) * KB], bw[b, :, c:c + CK],
                                                                   preferred_element_type=F32)
                return emit
            return [chunk(c) for c in range(0, W2, CK)]

        def scan(b):
            re, im = slice(b * W2, b * W2 + W), slice(b * W2 + W, (b + 1) * W2)
            ar, ai = pw_ref[b, 0:8, :W], pw_ref[b, 0:8, W:]
            st = {"x": (jnp.zeros((8, W), F32), jnp.zeros((8, W), F32))}

            def own(j):
                def emit():
                    rows = slice(j * 8, j * 8 + 8)
                    xr, xi = _cmul_add(s_ref[rows, re], s_ref[rows, im], ar, ai, *st["x"])
                    s_ref[rows, re] = xr
                    s_ref[rows, im] = xi
                    st["x"] = (xr, xi)
                return emit

            def ends():
                xr, xi = st["x"]
                for k, off in ((1, 8), (2, 16), (4, 24)):
                    xr, xi = _cmul_add(xr, xi, pw_ref[b, off:off + 8, :W], pw_ref[b, off:off + 8, W:],
                                       pltpu.roll(xr, k, 0), pltpu.roll(xi, k, 0))
                xr, xi = _cmul_add(xr, xi, pw_ref[b, 32:40, :W], pw_ref[b, 32:40, W:], carry[b, 0], carry[b, 1])
                st["c"] = (jnp.where(first, carry[b, 0], pltpu.roll(xr, 1, 0)),
                           jnp.where(first, carry[b, 1], pltpu.roll(xi, 1, 0)))
                carry[b, 0] = jnp.broadcast_to(xr[7:8], (8, W))
                carry[b, 1] = jnp.broadcast_to(xi[7:8], (8, W))

            def carried(j):
                def emit():
                    rows = slice(j * 8, j * 8 + 8)
                    cr, ci = _cmul(ar, ai, *st["c"])
                    s_ref[rows, re] = s_ref[rows, re] + cr
                    s_ref[rows, im] = s_ref[rows, im] + ci
                    st["c"] = (cr, ci)
                return emit

            return [own(j) for j in range(CH)] + [ends] + [carried(j) for j in range(CH)]

        def readout(b):
            cols = slice(b * KB, (b + 1) * KB)
            acc = {}

            def chunk(c):
                def emit():
                    part = jnp.dot(s_ref[:, b * W2 + c:b * W2 + c + CK].astype(BF), cw[b, c:c + CK, :],
                                   preferred_element_type=F32)
                    acc["y"] = part if c == 0 else acc["y"] + part
                return emit

            def finish():
                y = acc["y"] + d_ref[:, cols] * hp[:, cols]
                y_ref[:, cols] = y
                z_ref[:, cols] = jnp.dot(unperm[...], _gelu(y).astype(BF),
                                         preferred_element_type=F32).astype(z_ref.dtype)

            return [chunk(c) for c in range(0, W2, CK)] + [finish]

        for emit in project(0):
            emit()
        for b in range(NB):
            side = (project(b + 1) if b + 1 < NB else []) + (readout(b - 1) if b > 0 else [])
            _interleave(scan(b), side)
        for emit in readout(NB - 1):
            emit()

    blk = lambda kb, t: (t, kb)
    per_kb = lambda kb, t: (kb, 0, 0)
    return pl.pallas_call(
        body, name=name, grid=(nkb // NB, L // TL),
        in_specs=[pl.BlockSpec((TL, NB * KB), blk), pl.BlockSpec((NB, KB, P2), per_kb),
                  pl.BlockSpec((NB, P2, KB), per_kb), pl.BlockSpec((NB, 40, W2), per_kb),
                  pl.BlockSpec((1, NB * KB), lambda kb, t: (0, kb))],
        out_specs=[pl.BlockSpec((TL, NB * W2), blk), pl.BlockSpec((TL, NB * KB), blk),
                   pl.BlockSpec((TL, NB * KB), blk)],
        out_shape=[jax.ShapeDtypeStruct((L, nkb * W2), F32), jax.ShapeDtypeStruct((L, D), F32),
                   jax.ShapeDtypeStruct((L, D), BF)],
        scratch_shapes=[pltpu.VMEM((NB, KB, W2), BF), pltpu.VMEM((NB, W2, KB), BF), pltpu.VMEM((TL, TL), BF),
                        pltpu.VMEM((TL, TL), BF), pltpu.VMEM((NB, 2, 8, W), F32)],
        compiler_params=_cp("parallel", "arbitrary"))(h, tb, tct, pw, dvec)


def s5_bwd(dz, y, h, s, tc, tbt, pwr, dvec, name):
    L, D = h.shape
    nkb, KB, P2 = tc.shape
    P = P2 // 2
    W = (KB // SSM_GROUP) * P
    W2 = 2 * W
    dims = (KB, W2, SSM_GROUP, P, W)
    TL = _tile(L, (512, 256))
    CH = TL // 8
    nt = L // TL
    NB = 2 if nkb % 2 == 0 else 1
    CK = min(S5_CHUNK, W2)
    tn = (((0,), (0,)), ((), ()))

    def body(dz_ref, y_ref, h_ref, s_ref, sp_ref, tc_ref, tbt_ref, pw_ref, d_ref,
             dh_ref, dd_ref, da_ref, db_ref, dc_ref, g, ctw, btw, dbacc, dcacc, dys, perm, unperm, carry):
        t = pl.program_id(1)

        @pl.when(t == 0)
        def _():
            carry[...] = jnp.zeros_like(carry)
            dd_ref[...] = jnp.zeros_like(dd_ref)
            da_ref[...] = jnp.zeros_like(da_ref)
            dbacc[...] = jnp.zeros_like(dbacc)
            dcacc[...] = jnp.zeros_like(dcacc)
            for b in range(NB):
                ctw[b] = _expand(tc_ref[b], dims, False)
                btw[b] = _expand(tbt_ref[b], dims, True)
            perm[...] = _chunk_order(TL, CH, False).astype(perm.dtype)
            unperm[...] = _chunk_order(TL, CH, True).astype(perm.dtype)

        hp = jnp.dot(perm[...], h_ref[...].astype(BF), preferred_element_type=F32)
        dy = jnp.dot(perm[...], dz_ref[...].astype(BF), preferred_element_type=F32) * _gelu_grad(y_ref[...])
        dd_ref[...] += _rowsum8(dy * hp)
        dys[...] = dy
        dyb = dy.astype(BF)
        hpb = hp.astype(BF)
        sub = lax.broadcasted_iota(jnp.int32, (8, W), 0)
        live = jnp.where(t == nt - 1, 0.0, 1.0)

        def lead(b):
            cols = slice(b * KB, (b + 1) * KB)

            def to_states(c):
                def emit():
                    g[b, :, c:c + CK] = jnp.dot(dyb[:, cols], ctw[b, :, c:c + CK], preferred_element_type=F32)
                return emit

            def d_c(c):
                def emit():
                    dcacc[b, :, c:c + CK] += lax.dot_general(dyb[:, cols],
                                                             s_ref[:, b * W2 + c:b * W2 + c + CK].astype(BF), tn,
                                                             preferred_element_type=F32)
                return emit

            return [f(c) for c in range(0, W2, CK) for f in (to_states, d_c)]

        def scan(b):
            re, im = slice(b * W2, b * W2 + W), slice(b * W2 + W, (b + 1) * W2)
            ar, ai = pw_ref[b, 0:8, :W], pw_ref[b, 0:8, W:]
            zero = jnp.zeros((8, W), F32)
            st = {"g": (zero, zero), "acc": (zero, zero)}

            def own(j):
                def emit():
                    rows = slice(j * 8, j * 8 + 8)
                    gr, gi = _cmul_add(g[b, rows, :W], g[b, rows, W:], ar, ai, *st["g"])
                    g[b, rows, :W] = gr
                    g[b, rows, W:] = gi
                    st["g"] = (gr, gi)
                return emit

            def ends():
                gr, gi = st["g"]
                for k, off in ((1, 8), (2, 16), (4, 24)):
                    gr, gi = _cmul_add(gr, gi, pw_ref[b, off:off + 8, :W], pw_ref[b, off:off + 8, W:],
                                       pltpu.roll(gr, 8 - k, 0), pltpu.roll(gi, 8 - k, 0))
                gr, gi = _cmul_add(gr, gi, pw_ref[b, 32:40, :W], pw_ref[b, 32:40, W:], carry[b, 0], carry[b, 1])
                st["c"] = (jnp.where(sub == 7, carry[b, 0], pltpu.roll(gr, 7, 0)),
                           jnp.where(sub == 7, carry[b, 1], pltpu.roll(gi, 7, 0)))
                carry[b, 0] = jnp.broadcast_to(gr[0:1], (8, W))
                carry[b, 1] = jnp.broadcast_to(gi[0:1], (8, W))

            def carried(j):
                def emit():
                    rows = slice(j * 8, j * 8 + 8)
                    cr, ci = _cmul(ar, ai, *st["c"])
                    gr, gi = g[b, rows, :W] + cr, g[b, rows, W:] + ci
                    g[b, rows, :W] = gr
                    g[b, rows, W:] = gi
                    if j > 0:
                        before = slice(j * 8 - 8, j * 8)
                        pr, pi = s_ref[before, re], s_ref[before, im]
                    else:
                        last = slice(TL - 8, TL)
                        pr = jnp.where(sub == 0, sp_ref[7:8, re] * live, pltpu.roll(s_ref[last, re], 1, 0))
                        pi = jnp.where(sub == 0, sp_ref[7:8, im] * live, pltpu.roll(s_ref[last, im], 1, 0))
                    accr, acci = st["acc"]
                    st["c"] = (cr, ci)
                    st["acc"] = (accr + pr * gr + pi * gi, acci + pr * gi - pi * gr)
                return emit

            def done():
                da_ref[b, :, :W] += st["acc"][0]
                da_ref[b, :, W:] += st["acc"][1]

            return ([own(j) for j in reversed(range(CH))] + [ends] + [carried(j) for j in reversed(range(CH))]
                    + [done])

        def tail(b):
            cols = slice(b * KB, (b + 1) * KB)
            acc = {}

            def d_u(c):
                def emit():
                    part = jnp.dot(g[b, :, c:c + CK].astype(BF), btw[b, c:c + CK, :], preferred_element_type=F32)
                    acc["u"] = part if c == 0 else acc["u"] + part
                return emit

            def d_b(c):
                def emit():
                    dbacc[b, :, c:c + CK] += lax.dot_general(hpb[:, cols], g[b, :, c:c + CK].astype(BF), tn,
                                                             preferred_element_type=F32)
                return emit

            def finish():
                dh = (dys[:, cols] * d_ref[:, cols] + acc["u"]).astype(BF)
                dh_ref[:, cols] = jnp.dot(unperm[...], dh, preferred_element_type=F32).astype(dh_ref.dtype)

            return [f(c) for c in range(0, W2, CK) for f in (d_u, d_b)] + [finish]

        for emit in lead(0):
            emit()
        for b in range(NB):
            side = (lead(b + 1) if b + 1 < NB else []) + (tail(b - 1) if b > 0 else [])
            _interleave(scan(b), side)
        for emit in tail(NB - 1):
            emit()

        @pl.when(t == nt - 1)
        def _():
            for b in range(NB):
                db_ref[b] = _extract(dbacc[b], dims)
                dc_ref[b] = _extract(dcacc[b], dims)

    rev = lambda kb, t: (nt - 1 - t, kb)
    prev = lambda kb, t: (jnp.maximum((nt - 1 - t) * CH - 1, 0), kb)
    per_kb = lambda kb, t: (kb, 0, 0)
    return pl.pallas_call(
        body, name=name, grid=(nkb // NB, nt),
        in_specs=[pl.BlockSpec((TL, NB * KB), rev), pl.BlockSpec((TL, NB * KB), rev),
                  pl.BlockSpec((TL, NB * KB), rev), pl.BlockSpec((TL, NB * W2), rev),
                  pl.BlockSpec((8, NB * W2), prev), pl.BlockSpec((NB, KB, P2), per_kb),
                  pl.BlockSpec((NB, P2, KB), per_kb), pl.BlockSpec((NB, 40, W2), per_kb),
                  pl.BlockSpec((1, NB * KB), lambda kb, t: (0, kb))],
        out_specs=[pl.BlockSpec((TL, NB * KB), rev), pl.BlockSpec((8, NB * KB), lambda kb, t: (0, kb)),
                   pl.BlockSpec((NB, 8, W2), per_kb), pl.BlockSpec((NB, KB, P2), per_kb),
                   pl.BlockSpec((NB, KB, P2), per_kb)],
        out_shape=[jax.ShapeDtypeStruct((L, D), BF), jax.ShapeDtypeStruct((8, D), F32),
                   jax.ShapeDtypeStruct((nkb, 8, W2), F32), jax.ShapeDtypeStruct((nkb, KB, P2), F32),
                   jax.ShapeDtypeStruct((nkb, KB, P2), F32)],
        scratch_shapes=[pltpu.VMEM((NB, TL, W2), F32), pltpu.VMEM((NB, KB, W2), BF), pltpu.VMEM((NB, W2, KB), BF),
                        pltpu.VMEM((NB, KB, W2), F32), pltpu.VMEM((NB, KB, W2), F32), pltpu.VMEM((TL, NB * KB), F32),
                        pltpu.VMEM((TL, TL), BF), pltpu.VMEM((TL, TL), BF), pltpu.VMEM((NB, 2, 8, W), F32)],
        compiler_params=pltpu.CompilerParams(dimension_semantics=("parallel", "arbitrary"),
                                             vmem_limit_bytes=V7X_VMEM_BYTES - 4 * 1024 * 1024),
    )(dz, y, h, s, s, tc, tbt, pwr, dvec)


def _discretise(a_re, a_im, log_step, b_re, b_im):
    lr = jnp.minimum(a_re, -1e-4)
    li = a_im
    dt = jnp.exp(log_step)[:, None]
    mag = jnp.exp(lr * dt)
    abr = mag * jnp.cos(li * dt)
    abi = mag * jnp.sin(li * dt)
    den = lr * lr + li * li
    qr = ((abr - 1.0) * lr + abi * li) / den
    qi = (abi * lr - (abr - 1.0) * li) / den
    bbar_re = qr[..., None] * b_re - qi[..., None] * b_im
    bbar_im = qr[..., None] * b_im + qi[..., None] * b_re
    return abr, abi, bbar_re, bbar_im


def _compact(m_re, m_im, nkb):
    G, H, P = m_re.shape
    t = jnp.stack([m_re, m_im], axis=2).reshape(nkb, (G // nkb) * H, 2 * P).astype(BF)
    return t, jnp.swapaxes(t, 1, 2)


def _scan_powers(abr, abi, nkb, conj, CH):
    G, P = abr.shape
    if conj:
        abi = -abi

    def cmul(u, v):
        return u[0] * v[0] - u[1] * v[1], u[0] * v[1] + u[1] * v[0]

    q = (abr, abi)
    for _ in range(_log2(CH)):
        q = cmul(q, q)
    pows = [q]
    for _ in range(7):
        pows.append(cmul(pows[-1], q))
    row = jnp.arange(8)[:, None, None]

    def table(part):
        out = [jnp.broadcast_to((abr, abi)[part][None], (8, G, P))]
        for k in (1, 2, 4):
            keep = (row <= 7 - k) if conj else (row >= k)
            out.append(jnp.where(keep, pows[k - 1][part][None], 0.0))
        ends = jnp.stack([p[part] for p in pows])
        out.append(ends[::-1] if conj else ends)
        return jnp.concatenate(out, axis=0)

    GL = G // nkb
    t = jnp.stack([table(0), table(1)], axis=1)
    t = t.reshape(40, 2, nkb, GL * P).transpose(2, 0, 1, 3)
    return t.reshape(nkb, 40, 2 * GL * P)


def ada_mods(c_all, w_ada, b_sh, name):
    nl, D, NA = w_ada.shape

    def body(c_ref, w_ref, b_ref, o_ref):
        cv = c_ref[...]
        act = cv * _sigmoid(cv)
        o_ref[...] = jnp.dot(act, w_ref[...], preferred_element_type=F32, precision=lax.Precision.HIGHEST) + b_ref[...]

    return pl.pallas_call(
        body, name=name, grid=(nl,),
        in_specs=[pl.BlockSpec((8, D), lambda i: (0, 0)), pl.BlockSpec((None, D, NA), lambda i: (i, 0, 0)),
                  pl.BlockSpec((None, 1, NA), lambda i: (i, 0, 0))],
        out_specs=pl.BlockSpec((None, 8, NA), lambda i: (i, 0, 0)),
        out_shape=jax.ShapeDtypeStruct((nl, 8, NA), F32), compiler_params=_cp("parallel"))(c_all, w_ada, b_sh)


def _adamw(w, g, m, v):
    m = ADAM_B1 * m + (1.0 - ADAM_B1) * g
    v = ADAM_B2 * v + (1.0 - ADAM_B2) * (g * g)
    m_hat = m / (1.0 - ADAM_B1 ** ADAM_STEP)
    v_hat = v / (1.0 - ADAM_B2 ** ADAM_STEP)
    return -ADAM_LR * (m_hat / (jnp.sqrt(v_hat) + ADAM_EPS) + ADAM_WD * w), m, v


def _adam_rows(R, C):
    cap = max(8, (256 * 1024) // C)
    for t in range(min(R, cap), 0, -1):
        if R % t == 0 and (t % 8 == 0 or t == R):
            return t
    return R


def adamw_ada(c_t, dm, w, m, v, name):
    nl, D, NA = w.shape
    TK = _tile(D, (256, 128))

    def body(c_ref, dm_ref, w_ref, m_ref, v_ref, g_ref, d_ref, nm_ref, nv_ref):
        cv = c_ref[...]
        act = cv * _sigmoid(cv)
        g = jnp.dot(act, dm_ref[...], preferred_element_type=F32, precision=lax.Precision.HIGHEST)
        g_ref[...] = g
        d_ref[...], nm_ref[...], nv_ref[...] = _adamw(w_ref[...], g, m_ref[...], v_ref[...])

    big = pl.BlockSpec((None, TK, NA), lambda i, k: (i, k, 0))
    shape = jax.ShapeDtypeStruct(w.shape, F32)
    return pl.pallas_call(
        body, name=name, grid=(nl, D // TK),
        in_specs=[pl.BlockSpec((TK, 8), lambda i, k: (k, 0)), pl.BlockSpec((None, 8, NA), lambda i, k: (i, 0, 0)),
                  big, big, big],
        out_specs=[big] * 4, out_shape=[shape] * 4, compiler_params=_cp("parallel", "parallel"))(c_t, dm, w, m, v)


def adamw_sharded(w, m, v, ga, gb, name):
    nl, R, C = w.shape
    TR = _adam_rows(R, C)

    def body(w_ref, m_ref, v_ref, a_ref, b_ref, g_ref, d_ref, nm_ref, nv_ref):
        g = a_ref[...] + b_ref[...]
        g_ref[...] = g
        d_ref[...], nm_ref[...], nv_ref[...] = _adamw(w_ref[...], g, m_ref[...], v_ref[...])

    big = pl.BlockSpec((None, TR, C), lambda i, r: (i, r, 0))
    shape = jax.ShapeDtypeStruct(w.shape, F32)
    return pl.pallas_call(
        body, name=name, grid=(nl, R // TR), in_specs=[big] * 5,
        out_specs=[big] * 4, out_shape=[shape] * 4, compiler_params=_cp("parallel", "parallel"))(w, m, v, ga, gb)


def adamw_slab(g, w, m, v, name):
    R, C = g.shape
    TR = _tile(R, (160, 80, 40, 8))

    def body(g_ref, w_ref, m_ref, v_ref, d_ref, nm_ref, nv_ref):
        d_ref[...], nm_ref[...], nv_ref[...] = _adamw(w_ref[...], g_ref[...], m_ref[...], v_ref[...])

    big = pl.BlockSpec((TR, C), lambda r: (r, 0))
    shape = jax.ShapeDtypeStruct((R, C), F32)
    return pl.pallas_call(
        body, name=name, grid=(R // TR,), in_specs=[big] * 4,
        out_specs=[big] * 3, out_shape=[shape] * 3, compiler_params=_cp("parallel"))(g, w, m, v)


def adamw_plain(w, m, v, g, name):
    def body(w_ref, m_ref, v_ref, g_ref, d_ref, nm_ref, nv_ref):
        d_ref[...], nm_ref[...], nv_ref[...] = _adamw(w_ref[...], g_ref[...], m_ref[...], v_ref[...])

    shape = jax.ShapeDtypeStruct(w.shape, F32)
    return pl.pallas_call(body, name=name, out_shape=[shape] * 3,
                          compiler_params=pltpu.CompilerParams(vmem_limit_bytes=VMEM_LIMIT))(w, m, v, g)


def _slab_rows(a):
    n = a.size
    rows = -(-n // SLAB_W)
    return -(-rows // 8) * 8


def _pack(arrs, pad_rows_to=0):
    out = []
    for a in arrs:
        rows = _slab_rows(a)
        flat = a.reshape(-1).astype(F32)
        flat = jnp.pad(flat, (0, rows * SLAB_W - flat.shape[0]))
        out.append(flat.reshape(rows, SLAB_W))
    total = sum(o.shape[0] for o in out)
    if pad_rows_to and total % pad_rows_to:
        out.append(jnp.zeros((pad_rows_to - total % pad_rows_to, SLAB_W), F32))
    return jnp.concatenate(out, axis=0)


def _unpack(slab, like):
    out, r = [], 0
    for a in like:
        rows = _slab_rows(a)
        out.append(slab[r:r + rows].reshape(-1)[:a.size].reshape(a.shape))
        r += rows
    return out


WEIGHTS = ['norm1_g', 'norm2_g', 'w_ada', 'b_ada', 'ssm_a_re', 'ssm_a_im', 'ssm_log_step', 'ssm_b_re', 'ssm_b_im',
           'ssm_c_re', 'ssm_c_im', 'ssm_d', 'ssm_w_out', 'conv_w_in', 'conv_w', 'conv_w_out', 'w_ffn_in',
           'w_ffn_out', 'final_g']
SLAB = ['norm1_g', 'norm2_g', 'b_ada', 'ssm_a_re', 'ssm_a_im', 'ssm_log_step', 'ssm_b_re', 'ssm_b_im', 'ssm_c_re',
        'ssm_c_im', 'ssm_d', 'final_g']
SHARDED = ['ssm_w_out', 'conv_w_in', 'conv_w_out', 'w_ffn_in', 'w_ffn_out']


def kernel(x, c, norm1_g, norm2_g, w_ada, b_ada, ssm_a_re, ssm_a_im, ssm_log_step, ssm_b_re, ssm_b_im, ssm_c_re, ssm_c_im, ssm_d, ssm_w_out, conv_w_in, conv_w, conv_w_out, w_ffn_in, w_ffn_out, final_g, loss_target, m_norm1_g, m_norm2_g, m_w_ada, m_b_ada, m_ssm_a_re, m_ssm_a_im, m_ssm_log_step, m_ssm_b_re, m_ssm_b_im, m_ssm_c_re, m_ssm_c_im, m_ssm_d, m_ssm_w_out, m_conv_w_in, m_conv_w, m_conv_w_out, m_w_ffn_in, m_w_ffn_out, m_final_g, v_norm1_g, v_norm2_g, v_w_ada, v_b_ada, v_ssm_a_re, v_ssm_a_im, v_ssm_log_step, v_ssm_b_re, v_ssm_b_im, v_ssm_c_re, v_ssm_c_im, v_ssm_d, v_ssm_w_out, v_conv_w_in, v_conv_w, v_conv_w_out, v_w_ffn_in, v_w_ffn_out, v_final_g):
    given = dict(locals())
    W = {n: given[n] for n in WEIGHTS}
    Mo = {n: given["m_" + n] for n in WEIGHTS}
    Vo = {n: given["v_" + n] for n in WEIGHTS}

    xs = x[0]
    tgt = loss_target[0]
    L, D = xs.shape
    nlayer = norm1_g.shape[0]
    NA = w_ada.shape[2]
    G = ssm_a_re.shape[1]
    nkb = D // S5_BLOCK
    ax, ay, ac = _axes()
    me = 4 * ax + 2 * ay + ac
    chip = 2 * ax + ay

    assert D == SLAB_W
    first = gather8(jnp.concatenate([jnp.broadcast_to(c, (8, D)), _pack([conv_w])], axis=0), "gather_c_conv_w")
    c_all = first[:, 0, :]
    b_sh = lax.dynamic_slice_in_dim(b_ada, chip * NA, NA, axis=1)[:, None, :]
    mods_part = ada_mods(c_all, w_ada, b_sh, "ada_mods")
    mg = gather8(mods_part.reshape(nlayer * 8, NA), "gather_mods")
    mg = mg.reshape(N_CHIP, 2, nlayer, 8, NA)[:, 0]
    mods_all = lax.dynamic_index_in_dim(mg, me, axis=2, keepdims=False)
    mods_all = jnp.transpose(mods_all, (1, 0, 2)).reshape(nlayer, 6, D)

    cw_parts = first[:, 8:]
    nconv = conv_w.shape[0]
    cw_full = jnp.stack([_unpack(cw_parts[2 * q], [conv_w])[0] for q in range(N_CHIP)], axis=2)
    cw_full = cw_full.reshape(nconv, 3, D)

    in_flight_w = {}

    def start_weights(i, after):
        names = (["ssm_w_out"] if i % 2 == 0 else ["conv_w_in", "conv_w_out"]) + ["w_ffn_in", "w_ffn_out"]
        shards = [W[n][i if n.startswith("w_ffn") else i // 2].astype(BF) for n in names]
        sems, srcs, lands, tok = gather_start(shards, after, "gather_start%d" % i)
        in_flight_w[i] = (names, sems, srcs, lands)
        return tok

    def relay_weights(i, after):
        names, sems, srcs, lands = in_flight_w[i]
        got = gather_wait(sems, srcs, lands, list(range(len(names))), after, "gather_wait%d" % i)
        rsems, rlands, tok = relay_start(got, after, "relay_start%d" % i)
        in_flight_w[i] = (names, rsems, rlands)
        return tok

    def layer_weights(i, after):
        names, rsems, rlands = in_flight_w[i]
        return dict(zip(names, relay_wait(rsems, rlands, after, "relay_wait%d" % i)))

    token = start_weights(0, cw_full + mods_all[0, 0:3])
    mods_all = mods_all + token[0:1, 0:1]

    s5 = []
    for j in range(ssm_a_re.shape[0]):
        disc, disc_vjp = jax.vjp(_discretise, ssm_a_re[j], ssm_a_im[j], ssm_log_step[j], ssm_b_re[j], ssm_b_im[j])
        abr, abi, bbar_re, bbar_im = disc
        tb, tbt = _compact(jnp.swapaxes(bbar_re, 1, 2), jnp.swapaxes(bbar_im, 1, 2), nkb)
        tc, tct = _compact(ssm_c_re[j], -ssm_c_im[j], nkb)
        chunk = _tile(L, (512, 256)) // 8
        s5.append(dict(vjp=disc_vjp, tb=tb, tbt=tbt, tc=tc, tct=tct, pw=_scan_powers(abr, abi, nkb, False, chunk),
                       pwr=_scan_powers(abr, abi, nkb, True, chunk)))

    saved = []
    xcur = xs
    for i in range(nlayer):
        j = i // 2
        mods = mods_all[i]
        sv = dict(x=xcur)
        if i % 2 == 0:
            h = norm_mod(xcur, norm1_g[i:i + 1], mods, 0, F32, "norm_mod_s5")
            dvec = ssm_d[j:j + 1]
            if i == 0:
                dvec = dvec + start_weights(1, h)[0:1, 0:1]
            states, yv, z = s5_fwd(h, s5[j]["tb"], s5[j]["tct"], s5[j]["pw"], dvec, "s5_fwd")
            if i == 0:
                mods = mods + relay_weights(0, z)[0:1, 0:1]
            full = layer_weights(i, z)
            o, mix, x2 = ssm_out_glu(z, full["ssm_w_out"], xcur, mods, 2, "ssm_out_glu")
            sv.update(h=h, states=states, y=yv, z=z, o=o)
        else:
            h = norm_mod(xcur, norm1_g[i:i + 1], mods, 0, BF, "norm_mod")
            full = layer_weights(i, h)
            p = mm_nn(h, full["conv_w_in"], BF, "mm_conv_in")
            mc = conv_fwd(p, cw_full[j], "conv_fwd")
            mix, x2 = mm_nn(mc, full["conv_w_out"].reshape(1, D, D), BF, "mm_conv_out", res=xcur, gate=mods[2:3])
            sv.update(h=h, p=p, mc=mc)
        h2 = norm_mod(x2, norm2_g[i:i + 1], mods, 3, BF, "norm_mod")
        gu, act = ffn_in_act(h2, full["w_ffn_in"], "ffn_in_act")
        F = act.shape[1]
        ff, x3 = mm_nn(act, full["w_ffn_out"].reshape(1, F, D), BF, "mm_ffn_out", res=x2, gate=mods[5:6])
        sv.update(mix=mix, x2=x2, h2=h2, gu=gu, act=act, ff=ff, w=full)
        saved.append(sv)
        xcur = x3
        if i + 1 < nlayer:
            token = relay_weights(i + 1, ff)
            if i + 2 < nlayer:
                token = token + start_weights(i + 2, token)
            mods_all = mods_all + token[0:1, 0:1]

    loss_blk, dx, dfinal, dff = final_loss(xcur, tgt, final_g[None, :], saved[-1]["ff"], mods_all[nlayer - 1], 5,
                                           "final_loss")
    dg2 = dfinal[1:2]

    gland = {n: lax.empty((W[n].shape[0], N_CHIP) + W[n].shape[1:], BF) for n in SHARDED}
    in_flight = []
    dmods = [None] * nlayer
    dnorm1, dnorm2 = [None] * nlayer, [None] * nlayer
    dconv_w = [None] * nconv
    ds5 = [None] * ssm_a_re.shape[0]
    token = jnp.zeros((8, 128), F32)

    def send_grads(names, grads, slot, after, name):
        sems, thru, lands, tok = scatter_start([grads[n] for n in names], [gland[n] for n in names], slot, after, name)
        gland.update(zip(names, lands))
        in_flight.append((names, slot, sems, thru, name))
        return tok

    def land_grads(group, after):
        for names, slot, sems, thru, name in in_flight:
            if names[0] in group:
                got = scatter_wait(sems, thru, [gland[n] for n in names], slot, after, name.replace("scatter", "landed"))
                gland.update(zip(names, got))

    for i in reversed(range(nlayer)):
        j = i // 2
        mods = mods_all[i] + token[0:1, 0:1]
        sv = saved[i]
        full = sv["w"]
        gfull = {}
        F = sv["act"].shape[1]
        gfull["w_ffn_out"] = mm_tn(sv["act"], dff, 1, "mm_tn_ffn_out").reshape(N_CHIP, F // N_CHIP, D)
        dgu = ffn_out_bwd(dff, full["w_ffn_out"].reshape(F, D), sv["gu"], "ffn_out_bwd")
        gfull["w_ffn_in"] = mm_tn(sv["h2"], dgu, N_CHIP, "mm_tn_ffn_in")
        dh2 = mm_nt(dgu, full["w_ffn_in"], BF, "mm_nt_ffn_in")
        token = send_grads(["w_ffn_out", "w_ffn_in"], gfull, [i, i], dh2, "scatter_ffn%d" % i)
        mods = mods + token[0:1, 0:1]
        dx2, s2, dmix = norm_bwd(dh2, sv["x2"], dx, norm2_g[i:i + 1], mods, 3, "norm_bwd_mix",
                                 branch=(sv["mix"], mods, 2))
        dg1 = s2[3:4]
        if i % 2 == 0:
            do = glu_bwd(dmix, sv["o"], "glu_bwd")
            gfull["ssm_w_out"] = mm_tn(sv["z"], do, N_CHIP, "mm_tn_ssm_out")
            dz = mm_nt(do, full["ssm_w_out"], BF, "mm_nt_ssm_out")
            dh, dd, dab, db, dc = s5_bwd(dz, sv["y"], sv["h"], sv["states"], s5[j]["tc"], s5[j]["tbt"], s5[j]["pwr"],
                                         ssm_d[j:j + 1], "s5_bwd")
            ds5[j] = (dd, dab, db, dc)
        else:
            gfull["conv_w_out"] = mm_tn(sv["mc"], dmix, 1, "mm_tn_conv_out").reshape(N_CHIP, D // N_CHIP, D)
            dmc = mm_nt(dmix, full["conv_w_out"].reshape(1, D, D), BF, "mm_nt_conv_out")
            dbg, dcg, dvv, dcw = conv_bwd(dmc, sv["p"], cw_full[j], "conv_bwd")
            dp = jnp.concatenate([dbg, dcg, dvv], axis=1)
            gfull["conv_w_in"] = mm_tn(sv["h"], dp, N_CHIP, "mm_tn_conv_in")
            dh = mm_nt(dp, full["conv_w_in"], BF, "mm_nt_conv_in")
            dconv_w[j] = dcw[0:3]
        dmods_i = [s2[0:2], dg2]
        if i > 0:
            dx, s1, dff = norm_bwd(dh, sv["x"], dx2, norm1_g[i:i + 1], mods, 0, "norm_bwd_ffn",
                                   branch=(saved[i - 1]["ff"], mods_all[i - 1], 5))
            dg2 = s1[3:4]
        else:
            dx, s1 = norm_bwd(dh, sv["x"], dx2, norm1_g[i:i + 1], mods, 0, "norm_bwd")
        dmods[i] = jnp.concatenate([s1[0:2], dg1] + dmods_i, axis=0).reshape(6 * D)
        dnorm1[i], dnorm2[i] = s1[2], s2[2]
        names = ["ssm_w_out"] if i % 2 == 0 else ["conv_w_out", "conv_w_in"]
        token = send_grads(names, gfull, [j] * len(names), dx, "scatter_mix%d" % i)

    small = dict(norm1_g=jnp.stack(dnorm1), norm2_g=jnp.stack(dnorm2), b_ada=jnp.stack(dmods),
                 final_g=dfinal[0] + token[0, 0])
    per = {n: [] for n in ('ssm_a_re', 'ssm_a_im', 'ssm_log_step', 'ssm_b_re', 'ssm_b_im', 'ssm_c_re', 'ssm_c_im', 'ssm_d')}
    GL = G // nkb
    for j, (dd, dab, db, dc) in enumerate(ds5):
        dab = jnp.sum(dab, axis=1).reshape(nkb, 2, GL, SSM_STATE)
        g_abr, g_abi = dab[:, 0].reshape(G, SSM_STATE), dab[:, 1].reshape(G, SSM_STATE)
        db, dc = db.reshape(G, SSM_GROUP, 2, SSM_STATE), dc.reshape(G, SSM_GROUP, 2, SSM_STATE)
        gb_re, gb_im, gc_re, gc_im = db[:, :, 0], db[:, :, 1], dc[:, :, 0], dc[:, :, 1]
        ga_re, ga_im, gls, gbr, gbi = s5[j]["vjp"]((g_abr, g_abi, jnp.swapaxes(gb_re, 1, 2), jnp.swapaxes(gb_im, 1, 2)))
        for n, val in zip(per, (ga_re, ga_im, gls, gbr, gbi, gc_re, -gc_im, jnp.sum(dd, axis=0))):
            per[n].append(val)
    small.update({n: jnp.stack(vals) for n, vals in per.items()})
    dcw_full = jnp.stack(dconv_w)

    my_loss = loss_blk[0:1, 0:1]
    slab_like = [W[n] for n in SLAB] + [dcw_full, my_loss]
    rows64 = 8 * N_DEV
    slab = _pack([small[n] for n in SLAB] + [dcw_full, my_loss], rows64)
    per_dev = slab.shape[0] // N_DEV
    x_sems, x_srcs, x_lands, token = exchange_start(
        [(slab.reshape(N_DEV, per_dev, SLAB_W), True), (_pack([small["b_ada"]]), False)], dx, "small_scatter")

    early = [n for n in SHARDED if n != "ssm_w_out"]
    land_grads(early, token)
    mine = [reduce4(gland[n], "reduce4_" + n) for n in early]

    parts, dm_all = exchange_wait(x_sems, x_srcs, x_lands, [True, False], mine[-1][0, :8, :128], "small_landed")
    t_sems, t_srcs, t_lands, token = exchange_start([(sum8(parts, "sum_small"), False)], dm_all, "small_gather")
    out = {}

    w_sems, w_srcs, w_lands, token2 = swap_start(mine, "swap_start")
    dm_all = dm_all.reshape(N_DEV, -1)[:, :b_ada.size].reshape(N_DEV, nlayer, N_CHIP, NA)
    dm_sh = jnp.transpose(lax.dynamic_index_in_dim(dm_all, chip, axis=2, keepdims=False), (1, 0, 2))
    res = adamw_ada(jnp.transpose(c_all) + token[0:1, 0:1] + token2[0:1, 0:1], dm_sh, w_ada, m_w_ada, v_w_ada,
                    "adamw_ada")
    out["g", "w_ada"], out["d", "w_ada"], out["m", "w_ada"], out["v", "w_ada"] = res

    g_slab = exchange_wait(t_sems, t_srcs, t_lands, [False], out["g", "w_ada"], "small_total")[0]
    g_slab = g_slab.reshape(slab.shape)
    d_slab, m_slab, v_slab = adamw_slab(
        g_slab, _pack([W[n] for n in SLAB] + [jnp.zeros_like(dcw_full)], rows64),
        _pack([Mo[n] for n in SLAB] + [jnp.zeros_like(dcw_full)], rows64),
        _pack([Vo[n] for n in SLAB] + [jnp.ones_like(dcw_full)], rows64), "adamw_slab")
    for k, slab in zip(("g", "d", "m", "v"), (g_slab, d_slab, m_slab, v_slab)):
        for n, val in zip(SLAB, _unpack(slab, slab_like)):
            out[k, n] = val
    g_cw = lax.dynamic_slice_in_dim(_unpack(g_slab, slab_like)[-2], chip * conv_w.shape[2], conv_w.shape[2], axis=2)
    out["g", "conv_w"] = g_cw
    out["d", "conv_w"], out["m", "conv_w"], out["v", "conv_w"] = [
        r.reshape(conv_w.shape) for r in adamw_plain(conv_w.reshape(-1, conv_w.shape[2]), m_conv_w.reshape(-1, conv_w.shape[2]),
                                                     v_conv_w.reshape(-1, conv_w.shape[2]), g_cw.reshape(-1, conv_w.shape[2]),
                                                     "adamw_conv_w")]

    mine, theirs = swap_wait(w_sems, w_srcs, w_lands, d_slab, "swap_wait")
    for n, ga, gb in zip(early, mine, theirs):
        r = adamw_sharded(W[n], Mo[n], Vo[n], ga, gb, "adamw_" + n)
        out["g", n], out["d", n], out["m", n], out["v", n] = r

    land_grads(["ssm_w_out"], out["g", "w_ffn_out"])
    ga = reduce4(gland["ssm_w_out"], "reduce4_ssm_w_out")
    gb = swap_siblings([ga], "swap_siblings")[0]
    r = adamw_sharded(ssm_w_out, m_ssm_w_out, v_ssm_w_out, ga, gb, "adamw_ssm_w_out")
    out["g", "ssm_w_out"], out["d", "ssm_w_out"], out["m", "ssm_w_out"], out["v", "ssm_w_out"] = r

    loss = _unpack(g_slab, slab_like)[-1][0, 0]
    return (loss, dx[None], *[out["g", n] for n in WEIGHTS], *[out["d", n] for n in WEIGHTS],
            *[out["m", n] for n in WEIGHTS], *[out["v", n] for n in WEIGHTS])
```

```python
import math

import jax
import jax.numpy as jnp
from jax import lax
from jax.experimental import pallas as pl
from jax.experimental.pallas import tpu as pltpu

F32 = jnp.float32
BF = jnp.bfloat16
MESH = pl.DeviceIdType.MESH
ANY = pl.BlockSpec(memory_space=pl.ANY)

N_DEV = 8
N_CHIP = 4
SSM_GROUP = 16
SSM_STATE = 64
S5_BLOCK = 256
RMS_EPS = 1e-6
ADAM_LR, ADAM_B1, ADAM_B2, ADAM_EPS, ADAM_WD, ADAM_STEP = 0.001, 0.9, 0.999, 1e-08, 0.01, 10
V7X_VMEM_BYTES = 64 * 1024 * 1024
VMEM_LIMIT = V7X_VMEM_BYTES - 12 * 1024 * 1024
SLAB_W = 1024
GELU_C = math.sqrt(2.0 / math.pi)
GELU_A = 0.044715


def _cp(*sem):
    return pltpu.CompilerParams(dimension_semantics=sem if sem else None, vmem_limit_bytes=VMEM_LIMIT)


def _tile(n, prefs):
    for p in prefs:
        if p <= n and n % p == 0:
            return p
    return n


def _sigmoid(v):
    return 0.5 * jnp.tanh(0.5 * v) + 0.5


def _axes():
    return lax.axis_index("x"), lax.axis_index("y"), lax.axis_index("c")


def _flip(v, k):
    return 1 - v if k else v


def gather8(v, name):
    R, C = v.shape

    def body(v_ref, o_ref, ssem, rsem, lsem):
        x, y, c = _axes()
        me = 4 * x + 2 * y + c
        loc = pltpu.make_async_copy(v_ref, o_ref.at[me], lsem)
        loc.start()
        copies = []
        for k in range(1, N_DEV):
            peer = (_flip(x, (k >> 2) & 1), _flip(y, (k >> 1) & 1), _flip(c, k & 1))
            cp = pltpu.make_async_remote_copy(src_ref=v_ref, dst_ref=o_ref.at[me], send_sem=ssem.at[k - 1],
                                              recv_sem=rsem.at[k - 1], device_id=peer, device_id_type=MESH)
            cp.start()
            copies.append(cp)
        for cp in copies:
            cp.wait()
        loc.wait()

    return pl.pallas_call(
        body, name=name,
        out_shape=jax.ShapeDtypeStruct((N_DEV, R, C), v.dtype),
        in_specs=[pl.BlockSpec(memory_space=pltpu.VMEM)],
        out_specs=pl.BlockSpec(memory_space=pltpu.VMEM),
        scratch_shapes=[pltpu.SemaphoreType.DMA((N_DEV - 1,)), pltpu.SemaphoreType.DMA((N_DEV - 1,)),
                        pltpu.SemaphoreType.DMA],
        compiler_params=pltpu.CompilerParams(vmem_limit_bytes=VMEM_LIMIT),
    )(v)


HBM = pl.BlockSpec(memory_space=pltpu.HBM)
SEM = pl.BlockSpec(memory_space=pltpu.SEMAPHORE)
EFFECT = pltpu.SideEffectType.DATAFLOW_SIDE_EFFECTING


def _in_hbm(a):
    return pltpu.with_memory_space_constraint(a, pltpu.HBM)


def _chip_peers(x, y, c):
    out = []
    for k in range(1, N_CHIP):
        px, py = _flip(x, k >> 1), _flip(y, k & 1)
        out.append(((px, py, c), 2 * px + py))
    return out


def _my_half(ref, c):
    rows = ref.shape[0] // 2
    return pl.ds(pl.multiple_of(c * rows, 16), rows)


def relay_start(lands, after, name):
    n = len(lands)

    def body(*refs):
        land = refs[:n]
        ssem, rsem = refs[n + 1:n + 3]
        token = refs[-1]
        x, y, c = _axes()
        for a in range(n):
            half = _my_half(land[a].at[0], c)
            for k, (_, pchip) in enumerate(_chip_peers(x, y, c)):
                pltpu.make_async_remote_copy(src_ref=land[a].at[pchip, half], dst_ref=land[a].at[pchip, half],
                                             send_sem=ssem.at[3 * a + k], recv_sem=rsem.at[3 * a + k],
                                             device_id=(x, y, 1 - c), device_id_type=MESH).start()
        token[...] = jnp.zeros_like(token)

    out_shape = ([pltpu.SemaphoreType.DMA((3 * n,)), pltpu.SemaphoreType.DMA((3 * n,))]
                 + [pltpu.HBM(l.shape, l.dtype) for l in lands] + [jax.ShapeDtypeStruct((8, 128), F32)])
    res = pl.pallas_call(
        body, name=name, out_shape=out_shape, in_specs=[HBM] * n + [ANY],
        out_specs=[SEM, SEM] + [HBM] * n + [pl.BlockSpec(memory_space=pltpu.VMEM)],
        input_output_aliases={a: 2 + a for a in range(n)},
        compiler_params=pltpu.CompilerParams(has_side_effects=EFFECT),
    )(*lands, after)
    return tuple(res[:2]), list(res[2:2 + n]), res[-1]


def relay_wait(sems, lands, after, name):
    n = len(lands)

    def body(*refs):
        land = refs[:n]
        ssem, rsem = refs[n:n + 2]
        x, y, c = _axes()
        for a in range(n):
            mine, theirs = _my_half(land[a].at[0], c), _my_half(land[a].at[0], 1 - c)
            for k, (_, pchip) in enumerate(_chip_peers(x, y, c)):
                cp = pltpu.make_async_remote_copy(src_ref=land[a].at[pchip, mine], dst_ref=land[a].at[pchip, theirs],
                                                  send_sem=ssem.at[3 * a + k], recv_sem=rsem.at[3 * a + k],
                                                  device_id=(x, y, 1 - c), device_id_type=MESH)
                cp.wait_send()
                cp.wait_recv()

    res = pl.pallas_call(
        body, name=name, out_shape=[pltpu.HBM(l.shape, l.dtype) for l in lands],
        in_specs=[HBM] * n + [SEM, SEM, ANY], out_specs=[HBM] * n,
        input_output_aliases={a: a for a in range(n)},
        compiler_params=pltpu.CompilerParams(has_side_effects=EFFECT),
    )(*lands, *sems, after)
    return list(res)


def gather_start(shards, after, name):
    n = len(shards)

    def body(*refs):
        src, land = refs[:n], refs[n:2 * n]
        ssem, rsem, lsem = refs[2 * n + 1:2 * n + 4]
        token = refs[-1]
        x, y, c = _axes()
        chip = 2 * x + y
        for a in range(n):
            pltpu.make_async_copy(src[a], land[a].at[chip], lsem.at[a]).start()
            half = _my_half(src[a], c)
            for k, (peer, _) in enumerate(_chip_peers(x, y, c)):
                pltpu.make_async_remote_copy(src_ref=src[a].at[half], dst_ref=land[a].at[chip, half],
                                             send_sem=ssem.at[3 * a + k], recv_sem=rsem.at[3 * a + k],
                                             device_id=peer, device_id_type=MESH).start()
        token[...] = jnp.zeros_like(token)

    lands = [lax.empty((N_CHIP,) + s.shape, s.dtype) for s in shards]
    out_shape = ([pltpu.SemaphoreType.DMA((3 * n,)), pltpu.SemaphoreType.DMA((3 * n,)), pltpu.SemaphoreType.DMA((n,))]
                 + [pltpu.HBM(s.shape, s.dtype) for s in shards] + [pltpu.HBM(l.shape, l.dtype) for l in lands]
                 + [jax.ShapeDtypeStruct((8, 128), F32)])
    res = pl.pallas_call(
        body, name=name, out_shape=out_shape, in_specs=[HBM] * (2 * n) + [ANY],
        out_specs=[SEM, SEM, SEM] + [HBM] * (2 * n) + [pl.BlockSpec(memory_space=pltpu.VMEM)],
        input_output_aliases={a: 3 + a for a in range(2 * n)},
        compiler_params=pltpu.CompilerParams(has_side_effects=EFFECT),
    )(*[_in_hbm(s) for s in shards], *[_in_hbm(l) for l in lands], after)
    return tuple(res[:3]), list(res[3:3 + n]), list(res[3 + n:3 + 2 * n]), res[-1]


def gather_wait(sems, srcs, lands, idx, after, name):
    m = len(idx)

    def body(*refs):
        src, land = refs[:m], refs[m:2 * m]
        ssem, rsem, lsem = refs[2 * m:2 * m + 3]
        x, y, c = _axes()
        chip = 2 * x + y
        for j, a in enumerate(idx):
            half = _my_half(src[j], c)
            for k, (peer, pchip) in enumerate(_chip_peers(x, y, c)):
                cp = pltpu.make_async_remote_copy(src_ref=src[j].at[half], dst_ref=land[j].at[pchip, half],
                                                  send_sem=ssem.at[3 * a + k], recv_sem=rsem.at[3 * a + k],
                                                  device_id=peer, device_id_type=MESH)
                cp.wait_send()
                cp.wait_recv()
            pltpu.make_async_copy(src[j], land[j].at[chip], lsem.at[a]).wait()

    s_in = [srcs[a] for a in idx]
    l_in = [lands[a] for a in idx]
    res = pl.pallas_call(
        body, name=name,
        out_shape=[pltpu.HBM(s.shape, s.dtype) for s in s_in] + [pltpu.HBM(l.shape, l.dtype) for l in l_in],
        in_specs=[HBM] * (2 * m) + [SEM, SEM, SEM, ANY], out_specs=[HBM] * (2 * m),
        input_output_aliases={a: a for a in range(2 * m)},
        compiler_params=pltpu.CompilerParams(has_side_effects=EFFECT),
    )(*s_in, *l_in, *sems, after)
    return list(res[m:])


def scatter_start(grads, lands, slot, after, name):
    n = len(grads)

    def body(*refs):
        src, land = refs[:n], refs[n:2 * n]
        ssem, rsem, lsem = refs[2 * n + 1:2 * n + 4]
        token = refs[-1]
        x, y, c = _axes()
        chip = 2 * x + y
        for a in range(n):
            pltpu.make_async_copy(src[a].at[chip], land[a].at[slot[a], chip], lsem.at[a]).start()
            for k, (peer, pchip) in enumerate(_chip_peers(x, y, c)):
                pltpu.make_async_remote_copy(src_ref=src[a].at[pchip], dst_ref=land[a].at[slot[a], chip],
                                             send_sem=ssem.at[3 * a + k], recv_sem=rsem.at[3 * a + k],
                                             device_id=peer, device_id_type=MESH).start()
        token[...] = jnp.zeros_like(token)

    out_shape = ([pltpu.SemaphoreType.DMA((3 * n,)), pltpu.SemaphoreType.DMA((3 * n,)), pltpu.SemaphoreType.DMA((n,))]
                 + [pltpu.HBM(g.shape, g.dtype) for g in grads] + [pltpu.HBM(l.shape, l.dtype) for l in lands]
                 + [jax.ShapeDtypeStruct((8, 128), F32)])
    res = pl.pallas_call(
        body, name=name, out_shape=out_shape, in_specs=[HBM] * (2 * n) + [ANY],
        out_specs=[SEM, SEM, SEM] + [HBM] * (2 * n) + [pl.BlockSpec(memory_space=pltpu.VMEM)],
        input_output_aliases={a: 3 + a for a in range(2 * n)},
        compiler_params=pltpu.CompilerParams(has_side_effects=EFFECT),
    )(*[_in_hbm(g) for g in grads], *[_in_hbm(l) for l in lands], after)
    return tuple(res[:3]), list(res[3:3 + n]), list(res[3 + n:3 + 2 * n]), res[-1]


def scatter_wait(sems, grads, lands, slot, after, name):
    n = len(grads)

    def body(*refs):
        src, land = refs[:n], refs[n:2 * n]
        ssem, rsem, lsem = refs[2 * n:2 * n + 3]
        x, y, c = _axes()
        chip = 2 * x + y
        for a in range(n):
            for k, (peer, pchip) in enumerate(_chip_peers(x, y, c)):
                cp = pltpu.make_async_remote_copy(src_ref=src[a].at[pchip], dst_ref=land[a].at[slot[a], pchip],
                                                  send_sem=ssem.at[3 * a + k], recv_sem=rsem.at[3 * a + k],
                                                  device_id=peer, device_id_type=MESH)
                cp.wait_send()
                cp.wait_recv()
            pltpu.make_async_copy(src[a].at[chip], land[a].at[slot[a], chip], lsem.at[a]).wait()

    res = pl.pallas_call(
        body, name=name,
        out_shape=[pltpu.HBM(g.shape, g.dtype) for g in grads] + [pltpu.HBM(l.shape, l.dtype) for l in lands],
        in_specs=[HBM] * (2 * n) + [SEM, SEM, SEM, ANY], out_specs=[HBM] * (2 * n),
        input_output_aliases={a: a for a in range(2 * n)},
        compiler_params=pltpu.CompilerParams(has_side_effects=EFFECT),
    )(*grads, *lands, *sems, after)
    return list(res[n:])


def reduce4(land, name):
    nl, _, R, C = land.shape
    TR = _adam_rows(R, C)

    def body(l_ref, o_ref):
        o_ref[...] = ((l_ref[0].astype(F32) + l_ref[1].astype(F32)) + l_ref[2].astype(F32)) + l_ref[3].astype(F32)

    return pl.pallas_call(
        body, name=name, grid=(nl, R // TR),
        in_specs=[pl.BlockSpec((None, N_CHIP, TR, C), lambda i, r: (i, 0, r, 0))],
        out_specs=pl.BlockSpec((None, TR, C), lambda i, r: (i, r, 0)),
        out_shape=jax.ShapeDtypeStruct((nl, R, C), F32), compiler_params=_cp("parallel", "parallel"))(land)


def swap_siblings(arrs, name):
    n = len(arrs)

    def body(*refs):
        src, dst = refs[:n], refs[n:2 * n]
        ssem, rsem = refs[2 * n:]
        x, y, c = _axes()
        cps = [pltpu.make_async_remote_copy(src_ref=src[a], dst_ref=dst[a], send_sem=ssem.at[a], recv_sem=rsem.at[a],
                                            device_id=(x, y, 1 - c), device_id_type=MESH) for a in range(n)]
        for cp in cps:
            cp.start()
        for cp in cps:
            cp.wait()

    return pl.pallas_call(
        body, name=name, out_shape=[jax.ShapeDtypeStruct(a.shape, a.dtype) for a in arrs],
        in_specs=[ANY] * n, out_specs=[ANY] * n,
        scratch_shapes=[pltpu.SemaphoreType.DMA((n,)), pltpu.SemaphoreType.DMA((n,))],
        compiler_params=pltpu.CompilerParams(vmem_limit_bytes=VMEM_LIMIT),
    )(*arrs)


def swap_start(arrs, name):
    n = len(arrs)

    def body(*refs):
        src, land = refs[:n], refs[n:2 * n]
        ssem, rsem = refs[2 * n:2 * n + 2]
        token = refs[-1]
        x, y, c = _axes()
        for a in range(n):
            pltpu.make_async_remote_copy(src_ref=src[a], dst_ref=land[a], send_sem=ssem.at[a], recv_sem=rsem.at[a],
                                         device_id=(x, y, 1 - c), device_id_type=MESH).start()
        token[...] = jnp.zeros_like(token)

    lands = [lax.empty(a.shape, a.dtype) for a in arrs]
    out_shape = ([pltpu.SemaphoreType.DMA((n,)), pltpu.SemaphoreType.DMA((n,))]
                 + [pltpu.HBM(a.shape, a.dtype) for a in arrs] * 2 + [jax.ShapeDtypeStruct((8, 128), F32)])
    res = pl.pallas_call(
        body, name=name, out_shape=out_shape, in_specs=[HBM] * (2 * n),
        out_specs=[SEM, SEM] + [HBM] * (2 * n) + [pl.BlockSpec(memory_space=pltpu.VMEM)],
        input_output_aliases={a: 2 + a for a in range(2 * n)},
        compiler_params=pltpu.CompilerParams(has_side_effects=EFFECT),
    )(*[_in_hbm(a) for a in arrs], *[_in_hbm(l) for l in lands])
    return tuple(res[:2]), list(res[2:2 + n]), list(res[2 + n:2 + 2 * n]), res[-1]


def swap_wait(sems, srcs, lands, after, name):
    n = len(srcs)

    def body(*refs):
        src, land = refs[:n], refs[n:2 * n]
        ssem, rsem = refs[2 * n:2 * n + 2]
        x, y, c = _axes()
        for a in range(n):
            cp = pltpu.make_async_remote_copy(src_ref=src[a], dst_ref=land[a], send_sem=ssem.at[a],
                                              recv_sem=rsem.at[a], device_id=(x, y, 1 - c), device_id_type=MESH)
            cp.wait_send()
            cp.wait_recv()

    res = pl.pallas_call(
        body, name=name, out_shape=[pltpu.HBM(a.shape, a.dtype) for a in srcs] * 2,
        in_specs=[HBM] * (2 * n) + [SEM, SEM, ANY], out_specs=[HBM] * (2 * n),
        input_output_aliases={a: a for a in range(2 * n)},
        compiler_params=pltpu.CompilerParams(has_side_effects=EFFECT),
    )(*srcs, *lands, *sems, after)
    return list(res[:n]), list(res[n:])


def _all_peers(x, y, c):
    out = []
    for k in range(1, N_DEV):
        px, py, pc = _flip(x, (k >> 2) & 1), _flip(y, (k >> 1) & 1), _flip(c, k & 1)
        out.append(((px, py, pc), 4 * px + 2 * py + pc))
    return out


def exchange_start(items, after, name):
    n = len(items)

    def body(*refs):
        src, land = refs[:n], refs[n:2 * n]
        ssem, rsem, lsem = refs[2 * n + 1:2 * n + 4]
        token = refs[-1]
        x, y, c = _axes()
        me = 4 * x + 2 * y + c
        for a, (_, scatter) in enumerate(items):
            pltpu.make_async_copy(src[a].at[me] if scatter else src[a], land[a].at[me], lsem.at[a]).start()
            for k, (peer, p) in enumerate(_all_peers(x, y, c)):
                pltpu.make_async_remote_copy(src_ref=src[a].at[p] if scatter else src[a], dst_ref=land[a].at[me],
                                             send_sem=ssem.at[7 * a + k], recv_sem=rsem.at[7 * a + k],
                                             device_id=peer, device_id_type=MESH).start()
        token[...] = jnp.zeros_like(token)

    srcs = [s for s, _ in items]
    lands = [lax.empty(s.shape if sc else (N_DEV,) + s.shape, s.dtype) for s, sc in items]
    out_shape = ([pltpu.SemaphoreType.DMA((7 * n,)), pltpu.SemaphoreType.DMA((7 * n,)), pltpu.SemaphoreType.DMA((n,))]
                 + [pltpu.HBM(s.shape, s.dtype) for s in srcs] + [pltpu.HBM(l.shape, l.dtype) for l in lands]
                 + [jax.ShapeDtypeStruct((8, 128), F32)])
    res = pl.pallas_call(
        body, name=name, out_shape=out_shape, in_specs=[HBM] * (2 * n) + [ANY],
        out_specs=[SEM, SEM, SEM] + [HBM] * (2 * n) + [pl.BlockSpec(memory_space=pltpu.VMEM)],
        input_output_aliases={a: 3 + a for a in range(2 * n)},
        compiler_params=pltpu.CompilerParams(has_side_effects=EFFECT),
    )(*[_in_hbm(s) for s in srcs], *[_in_hbm(l) for l in lands], after)
    return tuple(res[:3]), list(res[3:3 + n]), list(res[3 + n:3 + 2 * n]), res[-1]


def exchange_wait(sems, srcs, lands, scatter, after, name):
    n = len(srcs)

    def body(*refs):
        src, land = refs[:n], refs[n:2 * n]
        ssem, rsem, lsem = refs[2 * n:2 * n + 3]
        x, y, c = _axes()
        me = 4 * x + 2 * y + c
        for a in range(n):
            for k, (peer, p) in enumerate(_all_peers(x, y, c)):
                cp = pltpu.make_async_remote_copy(src_ref=src[a].at[p] if scatter[a] else src[a],
                                                  dst_ref=land[a].at[p], send_sem=ssem.at[7 * a + k],
                                                  recv_sem=rsem.at[7 * a + k], device_id=peer, device_id_type=MESH)
                cp.wait_send()
                cp.wait_recv()
            pltpu.make_async_copy(src[a].at[me] if scatter[a] else src[a], land[a].at[me], lsem.at[a]).wait()

    res = pl.pallas_call(
        body, name=name,
        out_shape=[pltpu.HBM(s.shape, s.dtype) for s in srcs] + [pltpu.HBM(l.shape, l.dtype) for l in lands],
        in_specs=[HBM] * (2 * n) + [SEM, SEM, SEM, ANY], out_specs=[HBM] * (2 * n),
        input_output_aliases={a: a for a in range(2 * n)},
        compiler_params=pltpu.CompilerParams(has_side_effects=EFFECT),
    )(*srcs, *lands, *sems, after)
    return list(res[n:])


def sum8(parts, name):
    _, P, C = parts.shape

    def body(p_ref, o_ref):
        tot = p_ref[0]
        for d in range(1, N_DEV):
            tot = tot + p_ref[d]
        o_ref[...] = tot

    return pl.pallas_call(body, name=name, out_shape=jax.ShapeDtypeStruct((P, C), F32),
                          compiler_params=pltpu.CompilerParams(vmem_limit_bytes=VMEM_LIMIT))(parts)


def mm_nn(a, w, out_dtype, name, res=None, gate=None):
    M, K = a.shape
    S, _, Ns = w.shape
    TM = _tile(M, (1024, 512, 256) if K <= 1024 else (512, 256))
    TN = _tile(Ns, (1408, 1024, 768, 512, 256, 128))
    nj = Ns // TN
    fused = res is not None

    def body(*refs):
        if fused:
            a_ref, w_ref, r_ref, g_ref, f_ref, o_ref = refs
        else:
            a_ref, w_ref, f_ref = refs
        f = jnp.dot(a_ref[...], w_ref[...], preferred_element_type=F32)
        f_ref[...] = f.astype(f_ref.dtype)
        if fused:
            o_ref[...] = r_ref[...] + g_ref[...] * f

    col = lambda s, j, i: (i, s * nj + j)
    in_specs = [pl.BlockSpec((TM, K), lambda s, j, i: (i, 0)), pl.BlockSpec((None, K, TN), lambda s, j, i: (s, 0, j))]
    out_specs = [pl.BlockSpec((TM, TN), col)]
    out_shape = [jax.ShapeDtypeStruct((M, S * Ns), out_dtype)]
    args = [a, w]
    if fused:
        in_specs += [pl.BlockSpec((TM, TN), col), pl.BlockSpec((1, TN), lambda s, j, i: (0, s * nj + j))]
        out_specs.append(pl.BlockSpec((TM, TN), col))
        out_shape.append(jax.ShapeDtypeStruct((M, S * Ns), F32))
        args += [res, gate]
    out = pl.pallas_call(body, name=name, grid=(S, nj, M // TM), in_specs=in_specs, out_specs=out_specs,
                         out_shape=out_shape, compiler_params=_cp("parallel", "parallel", "parallel"))(*args)
    return tuple(out) if fused else out[0]


def mm_nt(g, w, out_dtype, name):
    g3 = g if g.ndim == 3 else g[None]
    Q, M, F = g3.shape
    S, K, Ns = w.shape
    TM = _tile(M, (1024, 512, 256) if K <= 1024 else (512, 256))
    TN = _tile(Ns, (1408, 1024, 768, 512, 256, 128))
    nj = Ns // TN
    nred = S * nj
    per_part = F // TN

    def body(g_ref, w_ref, o_ref, acc):
        n = pl.program_id(1)

        @pl.when(n == 0)
        def _():
            acc[...] = jnp.zeros_like(acc)

        acc[...] += lax.dot_general(g_ref[...], w_ref[...], (((1,), (1,)), ((), ())), preferred_element_type=F32)

        @pl.when(n == nred - 1)
        def _():
            o_ref[...] = acc[...].astype(o_ref.dtype)

    return pl.pallas_call(
        body, name=name, grid=(M // TM, nred),
        in_specs=[pl.BlockSpec((None, TM, TN), lambda i, n: (n // per_part, i, n % per_part)),
                  pl.BlockSpec((None, K, TN), lambda i, n: (n // nj, 0, n % nj))],
        out_specs=pl.BlockSpec((TM, K), lambda i, n: (i, 0)),
        out_shape=jax.ShapeDtypeStruct((M, K), out_dtype),
        scratch_shapes=[pltpu.VMEM((TM, K), F32)],
        compiler_params=_cp("parallel", "arbitrary"))(g3, w)


def mm_tn(a, g, S, name):
    M, K = a.shape
    g3 = g if g.ndim == 3 else g[None]
    Q, _, F = g3.shape
    Ns = Q * F // S
    TK = _tile(K, (512, 256, 128))
    TN = _tile(Ns, (1408, 1024, 768, 512, 256, 128))
    nj = Ns // TN
    per_part = F // TN

    def body(a_ref, g_ref, o_ref):
        o_ref[...] = lax.dot_general(a_ref[...], g_ref[...], (((0,), (0,)), ((), ())),
                                     preferred_element_type=F32).astype(o_ref.dtype)

    return pl.pallas_call(
        body, name=name, grid=(S * nj, K // TK),
        in_specs=[pl.BlockSpec((M, TK), lambda n, k: (0, k)),
                  pl.BlockSpec((None, M, TN), lambda n, k: (n // per_part, 0, n % per_part))],
        out_specs=pl.BlockSpec((None, TK, TN), lambda n, k: (n // nj, k, n % nj)),
        out_shape=jax.ShapeDtypeStruct((S, K, Ns), BF),
        compiler_params=_cp("parallel", "parallel"))(a, g3)


ROW_TILE = (512, 256)


def _rows(TL, D):
    return pl.BlockSpec((TL, D), lambda i: (i, 0))


def _fixed(R, D):
    return pl.BlockSpec((R, D), lambda i: (0, 0))


def _rowsum8(v):
    T, D = v.shape
    return jnp.sum(v.reshape(T // 8, 8, D), axis=0)


def _norm_parts(xv):
    r = lax.rsqrt(jnp.mean(xv * xv, axis=-1, keepdims=True) + RMS_EPS)
    return xv * r, r


def norm_mod(x, gamma, mods, k_shift, out_dtype, name):
    L, D = x.shape
    TL = _tile(L, ROW_TILE)

    def body(x_ref, g_ref, m_ref, o_ref):
        xn, _ = _norm_parts(x_ref[...])
        sh, sc = m_ref[k_shift:k_shift + 1, :], m_ref[k_shift + 1:k_shift + 2, :]
        o_ref[...] = ((xn * g_ref[...]) * (1.0 + sc) + sh).astype(o_ref.dtype)

    return pl.pallas_call(body, name=name, grid=(L // TL,),
                          in_specs=[_rows(TL, D), _fixed(1, D), _fixed(6, D)], out_specs=_rows(TL, D),
                          out_shape=jax.ShapeDtypeStruct((L, D), out_dtype), compiler_params=_cp("parallel"))(x, gamma, mods)


def norm_bwd(dh, x, dres, gamma, mods, k_shift, name, branch=None):
    L, D = x.shape
    TL = _tile(L, ROW_TILE)
    nacc = 4 if branch else 3

    def body(*refs):
        if branch:
            dh_ref, x_ref, dr_ref, g_ref, m_ref, f_ref, fm_ref, dx_ref, s_ref, df_ref, acc = refs
        else:
            dh_ref, x_ref, dr_ref, g_ref, m_ref, dx_ref, s_ref, acc = refs
        i = pl.program_id(0)

        @pl.when(i == 0)
        def _():
            acc[...] = jnp.zeros_like(acc)

        xn, r = _norm_parts(x_ref[...])
        dh_v = dh_ref[...].astype(F32)
        gam = g_ref[...]
        sc = m_ref[k_shift + 1:k_shift + 2, :]
        dn = dh_v * (1.0 + sc)
        dxn = dn * gam
        dx = dr_ref[...] + r * (dxn - xn * jnp.mean(dxn * xn, axis=-1, keepdims=True))
        dx_ref[...] = dx
        acc[0] += _rowsum8(dh_v)
        acc[1] += _rowsum8(dh_v * (xn * gam))
        acc[2] += _rowsum8(dn * xn)
        if branch:
            df_ref[...] = (dx * fm_ref[branch[2]:branch[2] + 1, :]).astype(df_ref.dtype)
            acc[3] += _rowsum8(dx * f_ref[...].astype(F32))

        @pl.when(i == pl.num_programs(0) - 1)
        def _():
            s_ref[...] = jnp.zeros_like(s_ref)
            for q in range(nacc):
                s_ref[q:q + 1, :] = jnp.sum(acc[q], axis=0, keepdims=True)

    in_specs = [_rows(TL, D), _rows(TL, D), _rows(TL, D), _fixed(1, D), _fixed(6, D)]
    out_specs = [_rows(TL, D), _fixed(8, D)]
    out_shape = [jax.ShapeDtypeStruct((L, D), F32), jax.ShapeDtypeStruct((8, D), F32)]
    args = [dh, x, dres, gamma, mods]
    if branch:
        in_specs += [_rows(TL, D), _fixed(6, D)]
        out_specs.append(_rows(TL, D))
        out_shape.append(jax.ShapeDtypeStruct((L, D), BF))
        args += [branch[0], branch[1]]
    return pl.pallas_call(
        body, name=name, grid=(L // TL,), in_specs=in_specs, out_specs=out_specs, out_shape=out_shape,
        scratch_shapes=[pltpu.VMEM((nacc, 8, D), F32)], compiler_params=_cp("arbitrary"))(*args)


def ffn_in_act(a, w, name):
    M, K = a.shape
    S, _, Ns = w.shape
    half = S // 2
    TM = _tile(M, (512, 256))
    TN = _tile(Ns, (1408, 1024, 768, 512, 256, 128))
    nj = Ns // TN

    def body(a_ref, wg_ref, wu_ref, gu_ref, act_ref):
        av = a_ref[...]
        g = jnp.dot(av, wg_ref[...], preferred_element_type=F32)
        u = jnp.dot(av, wu_ref[...], preferred_element_type=F32)
        gu_ref[0] = g.astype(gu_ref.dtype)
        gu_ref[1] = u.astype(gu_ref.dtype)
        act_ref[...] = (g * _sigmoid(g) * u).astype(act_ref.dtype)

    return pl.pallas_call(
        body, name=name, grid=(half, nj, M // TM),
        in_specs=[pl.BlockSpec((TM, K), lambda s, j, i: (i, 0)),
                  pl.BlockSpec((None, K, TN), lambda s, j, i: (s, 0, j)),
                  pl.BlockSpec((None, K, TN), lambda s, j, i: (s + half, 0, j))],
        out_specs=[pl.BlockSpec((2, TM, TN), lambda s, j, i: (0, i, s * nj + j)),
                   pl.BlockSpec((TM, TN), lambda s, j, i: (i, s * nj + j))],
        out_shape=[jax.ShapeDtypeStruct((2, M, half * Ns), BF), jax.ShapeDtypeStruct((M, half * Ns), BF)],
        compiler_params=_cp("parallel", "parallel", "parallel"))(a, w, w)


def ffn_out_bwd(dff, w2, gu, name):
    M, D = dff.shape
    F = w2.shape[0]
    TM = _tile(M, (512, 256))
    CW = _tile(F, (256, 128))

    def body(d_ref, w_ref, gu_ref, o_ref):
        dv = d_ref[...]

        def product(c):
            return lax.dot_general(dv, w_ref[c:c + CW, :], (((1,), (1,)), ((), ())), preferred_element_type=F32)

        da = product(0)
        for c in range(0, F, CW):
            ahead = product(c + CW) if c + CW < F else None
            g = gu_ref[0, :, c:c + CW].astype(F32)
            u = gu_ref[1, :, c:c + CW].astype(F32)
            s = _sigmoid(g)
            o_ref[0, :, c:c + CW] = (da * u * (s + g * s * (1.0 - s))).astype(o_ref.dtype)
            o_ref[1, :, c:c + CW] = (da * g * s).astype(o_ref.dtype)
            da = ahead

    part = pl.BlockSpec((2, TM, F), lambda i: (0, i, 0))
    return pl.pallas_call(
        body, name=name, grid=(M // TM,),
        in_specs=[pl.BlockSpec((TM, D), lambda i: (i, 0)), pl.BlockSpec((F, D), lambda i: (0, 0)), part],
        out_specs=part, out_shape=jax.ShapeDtypeStruct((2, M, F), BF),
        compiler_params=_cp("parallel"))(dff, w2, gu)


def ssm_out_glu(z, w, x, mods, k_gate, name):
    M, K = z.shape
    S, _, Ns = w.shape
    half = S // 2
    TM = _tile(M, (1024, 512, 256))
    TN = _tile(Ns, (512, 256, 128))
    nj = Ns // TN

    def body(z_ref, wv_ref, wg_ref, x_ref, m_ref, o_ref, mix_ref, y_ref):
        zv = z_ref[...]
        CW = _tile(TN, (256, 128))

        def products(c):
            return (jnp.dot(zv, wv_ref[:, c:c + CW], preferred_element_type=F32),
                    jnp.dot(zv, wg_ref[:, c:c + CW], preferred_element_type=F32))

        cur = products(0)
        for c in range(0, TN, CW):
            ahead = products(c + CW) if c + CW < TN else None
            val, gate = cur
            o_ref[0, :, c:c + CW] = val.astype(o_ref.dtype)
            o_ref[1, :, c:c + CW] = gate.astype(o_ref.dtype)
            mix = val * _sigmoid(gate)
            mix_ref[:, c:c + CW] = mix.astype(mix_ref.dtype)
            y_ref[:, c:c + CW] = x_ref[:, c:c + CW] + m_ref[k_gate:k_gate + 1, c:c + CW] * mix
            cur = ahead

    col = lambda s, j, i: (i, s * nj + j)
    return pl.pallas_call(
        body, name=name, grid=(half, nj, M // TM),
        in_specs=[pl.BlockSpec((TM, K), lambda s, j, i: (i, 0)),
                  pl.BlockSpec((None, K, TN), lambda s, j, i: (s, 0, j)),
                  pl.BlockSpec((None, K, TN), lambda s, j, i: (s + half, 0, j)),
                  pl.BlockSpec((TM, TN), col), pl.BlockSpec((6, TN), lambda s, j, i: (0, s * nj + j))],
        out_specs=[pl.BlockSpec((2, TM, TN), lambda s, j, i: (0, i, s * nj + j)), pl.BlockSpec((TM, TN), col),
                   pl.BlockSpec((TM, TN), col)],
        out_shape=[jax.ShapeDtypeStruct((2, M, half * Ns), BF), jax.ShapeDtypeStruct((M, half * Ns), BF),
                   jax.ShapeDtypeStruct((M, half * Ns), F32)],
        compiler_params=_cp("parallel", "parallel", "parallel"))(z, w, w, x, mods)


def glu_bwd(dmix, o, name):
    _, L, D = o.shape
    TL = _tile(L, ROW_TILE)

    def body(d_ref, o_ref, do_ref):
        d = d_ref[...].astype(F32)
        val = o_ref[0].astype(F32)
        s = _sigmoid(o_ref[1].astype(F32))
        do_ref[0] = (d * s).astype(do_ref.dtype)
        do_ref[1] = (d * val * s * (1.0 - s)).astype(do_ref.dtype)

    part = pl.BlockSpec((2, TL, D), lambda i: (0, i, 0))
    return pl.pallas_call(body, name=name, grid=(L // TL,), in_specs=[_rows(TL, D), part],
                          out_specs=part, out_shape=jax.ShapeDtypeStruct((2, L, D), BF),
                          compiler_params=_cp("parallel"))(dmix, o)


def final_loss(x, target, gamma, f, fmods, k_gate, name):
    L, D = x.shape
    TL = _tile(L, ROW_TILE)

    def body(x_ref, t_ref, g_ref, f_ref, fm_ref, l_ref, dx_ref, s_ref, df_ref, acc, lacc):
        i = pl.program_id(0)

        @pl.when(i == 0)
        def _():
            acc[...] = jnp.zeros_like(acc)
            lacc[...] = jnp.zeros_like(lacc)

        xn, r = _norm_parts(x_ref[...])
        gam = g_ref[...]
        e = xn * gam - t_ref[...]
        lacc[...] += jnp.sum(0.5 * jnp.mean(e * e, axis=-1, keepdims=True), axis=0, keepdims=True)
        dy = e * (1.0 / D)
        dxn = dy * gam
        dx = r * (dxn - xn * jnp.mean(dxn * xn, axis=-1, keepdims=True))
        dx_ref[...] = dx
        df_ref[...] = (dx * fm_ref[k_gate:k_gate + 1, :]).astype(df_ref.dtype)
        acc[0] += _rowsum8(dy * xn)
        acc[1] += _rowsum8(dx * f_ref[...].astype(F32))

        @pl.when(i == pl.num_programs(0) - 1)
        def _():
            s_ref[...] = jnp.zeros_like(s_ref)
            for q in range(2):
                s_ref[q:q + 1, :] = jnp.sum(acc[q], axis=0, keepdims=True)
            l_ref[...] = jnp.broadcast_to(lacc[...], l_ref.shape)

    return pl.pallas_call(
        body, name=name, grid=(L // TL,),
        in_specs=[_rows(TL, D), _rows(TL, D), _fixed(1, D), _rows(TL, D), _fixed(6, D)],
        out_specs=[_fixed(8, 128), _rows(TL, D), _fixed(8, D), _rows(TL, D)],
        out_shape=[jax.ShapeDtypeStruct((8, 128), F32), jax.ShapeDtypeStruct((L, D), F32),
                   jax.ShapeDtypeStruct((8, D), F32), jax.ShapeDtypeStruct((L, D), BF)],
        scratch_shapes=[pltpu.VMEM((2, 8, D), F32), pltpu.VMEM((1, 1), F32)],
        compiler_params=_cp("arbitrary"))(x, target, gamma, f, fmods)


def _col(L, TC, off):
    return pl.BlockSpec((L, TC), lambda j: (0, off + j))


def _shift_down(v, k, row):
    return jnp.where(row >= k, pltpu.roll(v, k, 0), 0.0)


def _shift_up(v, k, row, L):
    return jnp.where(row < L - k, pltpu.roll(v, L - k, 0), 0.0)


def conv_fwd(p, w, name):
    L, D3 = p.shape
    D = D3 // 3
    TC = _tile(D, (128,))
    nc = D // TC

    def body(b_ref, c_ref, v_ref, w_ref, o_ref):
        row = lax.broadcasted_iota(jnp.int32, (L, TC), 0)
        cv = c_ref[...].astype(F32) * v_ref[...].astype(F32)
        conv = w_ref[2:3, :] * cv + w_ref[1:2, :] * _shift_down(cv, 1, row) + w_ref[0:1, :] * _shift_down(cv, 2, row)
        o_ref[...] = (b_ref[...].astype(F32) * conv).astype(o_ref.dtype)

    return pl.pallas_call(
        body, name=name, grid=(nc,),
        in_specs=[_col(L, TC, 0), _col(L, TC, nc), _col(L, TC, 2 * nc), pl.BlockSpec((3, TC), lambda j: (0, j))],
        out_specs=_col(L, TC, 0), out_shape=jax.ShapeDtypeStruct((L, D), BF), compiler_params=_cp("parallel"))(p, p, p, w)


def conv_bwd(dm, p, w, name):
    L, D3 = p.shape
    D = D3 // 3
    TC = _tile(D, (128,))
    nc = D // TC

    def body(dm_ref, b_ref, c_ref, v_ref, w_ref, db_ref, dc_ref, dv_ref, dw_ref):
        row = lax.broadcasted_iota(jnp.int32, (L, TC), 0)
        cg, vv = c_ref[...].astype(F32), v_ref[...].astype(F32)
        cv = cg * vv
        cv1, cv2 = _shift_down(cv, 1, row), _shift_down(cv, 2, row)
        conv = w_ref[2:3, :] * cv + w_ref[1:2, :] * cv1 + w_ref[0:1, :] * cv2
        dmv = dm_ref[...].astype(F32)
        db_ref[...] = (dmv * conv).astype(db_ref.dtype)
        dconv = dmv * b_ref[...].astype(F32)
        dcv = (w_ref[2:3, :] * dconv + w_ref[1:2, :] * _shift_up(dconv, 1, row, L)
               + w_ref[0:1, :] * _shift_up(dconv, 2, row, L))
        dc_ref[...] = (dcv * vv).astype(dc_ref.dtype)
        dv_ref[...] = (dcv * cg).astype(dv_ref.dtype)
        dw_ref[...] = jnp.zeros_like(dw_ref)
        dw_ref[0:1, :] = jnp.sum(dconv * cv2, axis=0, keepdims=True)
        dw_ref[1:2, :] = jnp.sum(dconv * cv1, axis=0, keepdims=True)
        dw_ref[2:3, :] = jnp.sum(dconv * cv, axis=0, keepdims=True)

    one = jax.ShapeDtypeStruct((L, D), BF)
    return pl.pallas_call(
        body, name=name, grid=(nc,),
        in_specs=[_col(L, TC, 0), _col(L, TC, 0), _col(L, TC, nc), _col(L, TC, 2 * nc),
                  pl.BlockSpec((3, TC), lambda j: (0, j))],
        out_specs=[_col(L, TC, 0), _col(L, TC, 0), _col(L, TC, 0), pl.BlockSpec((8, TC), lambda j: (0, j))],
        out_shape=[one, one, one, jax.ShapeDtypeStruct((8, D), F32)],
        compiler_params=_cp("parallel"))(dm, p, p, p, w)


def _gelu(y):
    return 0.5 * y * (1.0 + jnp.tanh(GELU_C * (y + GELU_A * y * y * y)))


def _gelu_grad(y):
    th = jnp.tanh(GELU_C * (y + GELU_A * y * y * y))
    return 0.5 * (1.0 + th) + 0.5 * y * (1.0 - th * th) * GELU_C * (1.0 + 3.0 * GELU_A * y * y)


def _cmul_add(br, bi, ar, ai, sr, si):
    return br + ar * sr - ai * si, bi + ar * si + ai * sr


def _log2(n):
    k = n.bit_length() - 1
    assert 1 << k == n
    return k


def _replicate(P2, W2, P, GLP, transposed):
    shape = (W2, P2) if transposed else (P2, W2)
    k = lax.broadcasted_iota(jnp.int32, shape, 1 if transposed else 0)
    c = lax.broadcasted_iota(jnp.int32, shape, 0 if transposed else 1)
    return ((k >> _log2(P)) == (c >> _log2(GLP))) & ((k & (P - 1)) == (c & (P - 1)))


def _on_diagonal(KB, W2, H, P, GLP, transposed):
    shape = (W2, KB) if transposed else (KB, W2)
    r = lax.broadcasted_iota(jnp.int32, shape, 1 if transposed else 0)
    c = lax.broadcasted_iota(jnp.int32, shape, 0 if transposed else 1)
    return (r >> _log2(H)) == ((c & (GLP - 1)) >> _log2(P))


def _expand(t, dims, transposed):
    KB, W2, H, P, GLP = dims
    rep = _replicate(2 * P, W2, P, GLP, transposed).astype(t.dtype)
    wide = jnp.dot(rep, t, preferred_element_type=F32) if transposed else jnp.dot(t, rep, preferred_element_type=F32)
    return jnp.where(_on_diagonal(KB, W2, H, P, GLP, transposed), wide, 0.0).astype(t.dtype)


def _extract(acc, dims):
    KB, W2, H, P, GLP = dims
    rep = _replicate(2 * P, W2, P, GLP, True).astype(BF)
    kept = jnp.where(_on_diagonal(KB, W2, H, P, GLP, False), acc, 0.0)
    hi = kept.astype(BF)
    lo = (kept - hi.astype(F32)).astype(BF)
    return jnp.dot(hi, rep, preferred_element_type=F32) + jnp.dot(lo, rep, preferred_element_type=F32)


def _cmul(ar, ai, sr, si):
    return ar * sr - ai * si, ar * si + ai * sr


def _chunk_order(TL, CH, transposed):
    out_row = lax.broadcasted_iota(jnp.int32, (TL, TL), 1 if transposed else 0)
    in_row = lax.broadcasted_iota(jnp.int32, (TL, TL), 0 if transposed else 1)
    return in_row == ((out_row & 7) << _log2(CH)) + (out_row >> 3)


def _reorder(perm, v):
    hi = v.astype(perm.dtype)
    lo = (v - hi.astype(F32)).astype(perm.dtype)
    return jnp.dot(perm, hi, preferred_element_type=F32) + jnp.dot(perm, lo, preferred_element_type=F32)


def _interleave(main, side):
    n, m, k = len(main), len(side), 0
    for i, step in enumerate(main):
        step()
        while k < m and (k + 1) * n <= (i + 1) * m:
            side[k]()
            k += 1
    for step in side[k:]:
        step()


S5_CHUNK = 512


def s5_fwd(h, tb, tct, pw, dvec, name):
    L, D = h.shape
    nkb, KB, P2 = tb.shape
    P = P2 // 2
    W = (KB // SSM_GROUP) * P
    W2 = 2 * W
    dims = (KB, W2, SSM_GROUP, P, W)
    TL = _tile(L, (512, 256))
    CH = TL // 8
    NB = 2 if nkb % 2 == 0 else 1
    CK = min(S5_CHUNK, W2)

    def body(h_ref, tb_ref, tct_ref, pw_ref, d_ref, s_ref, y_ref, z_ref, bw, cw, perm, unperm, carry):
        t = pl.program_id(1)

        @pl.when(t == 0)
        def _():
            carry[...] = jnp.zeros_like(carry)
            for b in range(NB):
                bw[b] = _expand(tb_ref[b], dims, False)
                cw[b] = _expand(tct_ref[b], dims, True)
            perm[...] = _chunk_order(TL, CH, False).astype(perm.dtype)
            unperm[...] = _chunk_order(TL, CH, True).astype(perm.dtype)

        hp = _reorder(perm[...], h_ref[...])
        hpb = hp.astype(BF)
        first = lax.broadcasted_iota(jnp.int32, (8, W), 0) == 0

        def project(b):
            def chunk(c):
                def emit():
                    s_ref[:, b * W2 + c:b * W2 + c + CK] = jnp.dot(hpb[:, b * KB:(b + 1) * KB], bw[b, :, c:c + CK],
                                                                   preferred_element_type=F32)
                return emit
            return [chunk(c) for c in range(0, W2, CK)]

        def scan(b):
            re, im = slice(b * W2, b * W2 + W), slice(b * W2 + W, (b + 1) * W2)
            ar, ai = pw_ref[b, 0:8, :W], pw_ref[b, 0:8, W:]
            st = {"x": (jnp.zeros((8, W), F32), jnp.zeros((8, W), F32))}

            def own(j):
                def emit():
                    rows = slice(j * 8, j * 8 + 8)
                    xr, xi = _cmul_add(s_ref[rows, re], s_ref[rows, im], ar, ai, *st["x"])
                    s_ref[rows, re] = xr
                    s_ref[rows, im] = xi
                    st["x"] = (xr, xi)
                return emit

            def ends():
                xr, xi = st["x"]
                for k, off in ((1, 8), (2, 16), (4, 24)):
                    xr, xi = _cmul_add(xr, xi, pw_ref[b, off:off + 8, :W], pw_ref[b, off:off + 8, W:],
                                       pltpu.roll(xr, k, 0), pltpu.roll(xi, k, 0))
                xr, xi = _cmul_add(xr, xi, pw_ref[b, 32:40, :W], pw_ref[b, 32:40, W:], carry[b, 0], carry[b, 1])
                st["c"] = (jnp.where(first, carry[b, 0], pltpu.roll(xr, 1, 0)),
                           jnp.where(first, carry[b, 1], pltpu.roll(xi, 1, 0)))
                carry[b, 0] = jnp.broadcast_to(xr[7:8], (8, W))
                carry[b, 1] = jnp.broadcast_to(xi[7:8], (8, W))

            def carried(j):
                def emit():
                    rows = slice(j * 8, j * 8 + 8)
                    cr, ci = _cmul(ar, ai, *st["c"])
                    s_ref[rows, re] = s_ref[rows, re] + cr
                    s_ref[rows, im] = s_ref[rows, im] + ci
                    st["c"] = (cr, ci)
                return emit

            return [own(j) for j in range(CH)] + [ends] + [carried(j) for j in range(CH)]

        def readout(b):
            cols = slice(b * KB, (b + 1) * KB)
            acc = {}

            def chunk(c):
                def emit():
                    part = jnp.dot(s_ref[:, b * W2 + c:b * W2 + c + CK].astype(BF), cw[b, c:c + CK, :],
                                   preferred_element_type=F32)
                    acc["y"] = part if c == 0 else acc["y"] + part
                return emit

            def finish():
                y = acc["y"] + d_ref[:, cols] * hp[:, cols]
                y_ref[:, cols] = y
                z_ref[:, cols] = jnp.dot(unperm[...], _gelu(y).astype(BF),
                                         preferred_element_type=F32).astype(z_ref.dtype)

            return [chunk(c) for c in range(0, W2, CK)] + [finish]

        for emit in project(0):
            emit()
        for b in range(NB):
            side = (project(b + 1) if b + 1 < NB else []) + (readout(b - 1) if b > 0 else [])
            _interleave(scan(b), side)
        for emit in readout(NB - 1):
            emit()

    blk = lambda kb, t: (t, kb)
    per_kb = lambda kb, t: (kb, 0, 0)
    return pl.pallas_call(
        body, name=name, grid=(nkb // NB, L // TL),
        in_specs=[pl.BlockSpec((TL, NB * KB), blk), pl.BlockSpec((NB, KB, P2), per_kb),
                  pl.BlockSpec((NB, P2, KB), per_kb), pl.BlockSpec((NB, 40, W2), per_kb),
                  pl.BlockSpec((1, NB * KB), lambda kb, t: (0, kb))],
        out_specs=[pl.BlockSpec((TL, NB * W2), blk), pl.BlockSpec((TL, NB * KB), blk),
                   pl.BlockSpec((TL, NB * KB), blk)],
        out_shape=[jax.ShapeDtypeStruct((L, nkb * W2), F32), jax.ShapeDtypeStruct((L, D), F32),
                   jax.ShapeDtypeStruct((L, D), BF)],
        scratch_shapes=[pltpu.VMEM((NB, KB, W2), BF), pltpu.VMEM((NB, W2, KB), BF), pltpu.VMEM((TL, TL), BF),
                        pltpu.VMEM((TL, TL), BF), pltpu.VMEM((NB, 2, 8, W), F32)],
        compiler_params=_cp("parallel", "arbitrary"))(h, tb, tct, pw, dvec)


def s5_bwd(dz, y, h, s, tc, tbt, pwr, dvec, name):
    L, D = h.shape
    nkb, KB, P2 = tc.shape
    P = P2 // 2
    W = (KB // SSM_GROUP) * P
    W2 = 2 * W
    dims = (KB, W2, SSM_GROUP, P, W)
    TL = _tile(L, (512, 256))
    CH = TL // 8
    nt = L // TL
    NB = 2 if nkb % 2 == 0 else 1
    CK = min(S5_CHUNK, W2)
    tn = (((0,), (0,)), ((), ()))

    def body(dz_ref, y_ref, h_ref, s_ref, sp_ref, tc_ref, tbt_ref, pw_ref, d_ref,
             dh_ref, dd_ref, da_ref, db_ref, dc_ref, g, ctw, btw, dbacc, dcacc, dys, perm, unperm, carry):
        t = pl.program_id(1)

        @pl.when(t == 0)
        def _():
            carry[...] = jnp.zeros_like(carry)
            dd_ref[...] = jnp.zeros_like(dd_ref)
            da_ref[...] = jnp.zeros_like(da_ref)
            dbacc[...] = jnp.zeros_like(dbacc)
            dcacc[...] = jnp.zeros_like(dcacc)
            for b in range(NB):
                ctw[b] = _expand(tc_ref[b], dims, False)
                btw[b] = _expand(tbt_ref[b], dims, True)
            perm[...] = _chunk_order(TL, CH, False).astype(perm.dtype)
            unperm[...] = _chunk_order(TL, CH, True).astype(perm.dtype)

        hp = jnp.dot(perm[...], h_ref[...].astype(BF), preferred_element_type=F32)
        dy = jnp.dot(perm[...], dz_ref[...].astype(BF), preferred_element_type=F32) * _gelu_grad(y_ref[...])
        dd_ref[...] += _rowsum8(dy * hp)
        dys[...] = dy
        dyb = dy.astype(BF)
        hpb = hp.astype(BF)
        sub = lax.broadcasted_iota(jnp.int32, (8, W), 0)
        live = jnp.where(t == nt - 1, 0.0, 1.0)

        def lead(b):
            cols = slice(b * KB, (b + 1) * KB)

            def to_states(c):
                def emit():
                    g[b, :, c:c + CK] = jnp.dot(dyb[:, cols], ctw[b, :, c:c + CK], preferred_element_type=F32)
                return emit

            def d_c(c):
                def emit():
                    dcacc[b, :, c:c + CK] += lax.dot_general(dyb[:, cols],
                                                             s_ref[:, b * W2 + c:b * W2 + c + CK].astype(BF), tn,
                                                             preferred_element_type=F32)
                return emit

            return [f(c) for c in range(0, W2, CK) for f in (to_states, d_c)]

        def scan(b):
            re, im = slice(b * W2, b * W2 + W), slice(b * W2 + W, (b + 1) * W2)
            ar, ai = pw_ref[b, 0:8, :W], pw_ref[b, 0:8, W:]
            zero = jnp.zeros((8, W), F32)
            st = {"g": (zero, zero), "acc": (zero, zero)}

            def own(j):
                def emit():
                    rows = slice(j * 8, j * 8 + 8)
                    gr, gi = _cmul_add(g[b, rows, :W], g[b, rows, W:], ar, ai, *st["g"])
                    g[b, rows, :W] = gr
                    g[b, rows, W:] = gi
                    st["g"] = (gr, gi)
                return emit

            def ends():
                gr, gi = st["g"]
                for k, off in ((1, 8), (2, 16), (4, 24)):
                    gr, gi = _cmul_add(gr, gi, pw_ref[b, off:off + 8, :W], pw_ref[b, off:off + 8, W:],
                                       pltpu.roll(gr, 8 - k, 0), pltpu.roll(gi, 8 - k, 0))
                gr, gi = _cmul_add(gr, gi, pw_ref[b, 32:40, :W], pw_ref[b, 32:40, W:], carry[b, 0], carry[b, 1])
                st["c"] = (jnp.where(sub == 7, carry[b, 0], pltpu.roll(gr, 7, 0)),
                           jnp.where(sub == 7, carry[b, 1], pltpu.roll(gi, 7, 0)))
                carry[b, 0] = jnp.broadcast_to(gr[0:1], (8, W))
                carry[b, 1] = jnp.broadcast_to(gi[0:1], (8, W))

            def carried(j):
                def emit():
                    rows = slice(j * 8, j * 8 + 8)
                    cr, ci = _cmul(ar, ai, *st["c"])
                    gr, gi = g[b, rows, :W] + cr, g[b, rows, W:] + ci
                    g[b, rows, :W] = gr
                    g[b, rows, W:] = gi
                    if j > 0:
                        before = slice(j * 8 - 8, j * 8)
                        pr, pi = s_ref[before, re], s_ref[before, im]
                    else:
                        last = slice(TL - 8, TL)
                        pr = jnp.where(sub == 0, sp_ref[7:8, re] * live, pltpu.roll(s_ref[last, re], 1, 0))
                        pi = jnp.where(sub == 0, sp_ref[7:8, im] * live, pltpu.roll(s_ref[last, im], 1, 0))
                    accr, acci = st["acc"]
                    st["c"] = (cr, ci)
                    st["acc"] = (accr + pr * gr + pi * gi, acci + pr * gi - pi * gr)
                return emit

            def done():
                da_ref[b, :, :W] += st["acc"][0]
                da_ref[b, :, W:] += st["acc"][1]

            return ([own(j) for j in reversed(range(CH))] + [ends] + [carried(j) for j in reversed(range(CH))]
                    + [done])

        def tail(b):
            cols = slice(b * KB, (b + 1) * KB)
            acc = {}

            def d_u(c):
                def emit():
                    part = jnp.dot(g[b, :, c:c + CK].astype(BF), btw[b, c:c + CK, :], preferred_element_type=F32)
                    acc["u"] = part if c == 0 else acc["u"] + part
                return emit

            def d_b(c):
                def emit():
                    dbacc[b, :, c:c + CK] += lax.dot_general(hpb[:, cols], g[b, :, c:c + CK].astype(BF), tn,
                                                             preferred_element_type=F32)
                return emit

            def finish():
                dh = (dys[:, cols] * d_ref[:, cols] + acc["u"]).astype(BF)
                dh_ref[:, cols] = jnp.dot(unperm[...], dh, preferred_element_type=F32).astype(dh_ref.dtype)

            return [f(c) for c in range(0, W2, CK) for f in (d_u, d_b)] + [finish]

        for emit in lead(0):
            emit()
        for b in range(NB):
            side = (lead(b + 1) if b + 1 < NB else []) + (tail(b - 1) if b > 0 else [])
            _interleave(scan(b), side)
        for emit in tail(NB - 1):
            emit()

        @pl.when(t == nt - 1)
        def _():
            for b in range(NB):
                db_ref[b] = _extract(dbacc[b], dims)
                dc_ref[b] = _extract(dcacc[b], dims)

    rev = lambda kb, t: (nt - 1 - t, kb)
    prev = lambda kb, t: (jnp.maximum((nt - 1 - t) * CH - 1, 0), kb)
    per_kb = lambda kb, t: (kb, 0, 0)
    return pl.pallas_call(
        body, name=name, grid=(nkb // NB, nt),
        in_specs=[pl.BlockSpec((TL, NB * KB), rev), pl.BlockSpec((TL, NB * KB), rev),
                  pl.BlockSpec((TL, NB * KB), rev), pl.BlockSpec((TL, NB * W2), rev),
                  pl.BlockSpec((8, NB * W2), prev), pl.BlockSpec((NB, KB, P2), per_kb),
                  pl.BlockSpec((NB, P2, KB), per_kb), pl.BlockSpec((NB, 40, W2), per_kb),
                  pl.BlockSpec((1, NB * KB), lambda kb, t: (0, kb))],
        out_specs=[pl.BlockSpec((TL, NB * KB), rev), pl.BlockSpec((8, NB * KB), lambda kb, t: (0, kb)),
                   pl.BlockSpec((NB, 8, W2), per_kb), pl.BlockSpec((NB, KB, P2), per_kb),
                   pl.BlockSpec((NB, KB, P2), per_kb)],
        out_shape=[jax.ShapeDtypeStruct((L, D), BF), jax.ShapeDtypeStruct((8, D), F32),
                   jax.ShapeDtypeStruct((nkb, 8, W2), F32), jax.ShapeDtypeStruct((nkb, KB, P2), F32),
                   jax.ShapeDtypeStruct((nkb, KB, P2), F32)],
        scratch_shapes=[pltpu.VMEM((NB, TL, W2), F32), pltpu.VMEM((NB, KB, W2), BF), pltpu.VMEM((NB, W2, KB), BF),
                        pltpu.VMEM((NB, KB, W2), F32), pltpu.VMEM((NB, KB, W2), F32), pltpu.VMEM((TL, NB * KB), F32),
                        pltpu.VMEM((TL, TL), BF), pltpu.VMEM((TL, TL), BF), pltpu.VMEM((NB, 2, 8, W), F32)],
        compiler_params=pltpu.CompilerParams(dimension_semantics=("parallel", "arbitrary"),
                                             vmem_limit_bytes=V7X_VMEM_BYTES - 4 * 1024 * 1024),
    )(dz, y, h, s, s, tc, tbt, pwr, dvec)


def _discretise(a_re, a_im, log_step, b_re, b_im):
    lr = jnp.minimum(a_re, -1e-4)
    li = a_im
    dt = jnp.exp(log_step)[:, None]
    mag = jnp.exp(lr * dt)
    abr = mag * jnp.cos(li * dt)
    abi = mag * jnp.sin(li * dt)
    den = lr * lr + li * li
    qr = ((abr - 1.0) * lr + abi * li) / den
    qi = (abi * lr - (abr - 1.0) * li) / den
    bbar_re = qr[..., None] * b_re - qi[..., None] * b_im
    bbar_im = qr[..., None] * b_im + qi[..., None] * b_re
    return abr, abi, bbar_re, bbar_im


def _compact(m_re, m_im, nkb):
    G, H, P = m_re.shape
    t = jnp.stack([m_re, m_im], axis=2).reshape(nkb, (G // nkb) * H, 2 * P).astype(BF)
    return t, jnp.swapaxes(t, 1, 2)


def _scan_powers(abr, abi, nkb, conj, CH):
    G, P = abr.shape
    if conj:
        abi = -abi

    def cmul(u, v):
        return u[0] * v[0] - u[1] * v[1], u[0] * v[1] + u[1] * v[0]

    q = (abr, abi)
    for _ in range(_log2(CH)):
        q = cmul(q, q)
    pows = [q]
    for _ in range(7):
        pows.append(cmul(pows[-1], q))
    row = jnp.arange(8)[:, None, None]

    def table(part):
        out = [jnp.broadcast_to((abr, abi)[part][None], (8, G, P))]
        for k in (1, 2, 4):
            keep = (row <= 7 - k) if conj else (row >= k)
            out.append(jnp.where(keep, pows[k - 1][part][None], 0.0))
        ends = jnp.stack([p[part] for p in pows])
        out.append(ends[::-1] if conj else ends)
        return jnp.concatenate(out, axis=0)

    GL = G // nkb
    t = jnp.stack([table(0), table(1)], axis=1)
    t = t.reshape(40, 2, nkb, GL * P).transpose(2, 0, 1, 3)
    return t.reshape(nkb, 40, 2 * GL * P)


def ada_mods(c_all, w_ada, b_sh, name):
    nl, D, NA = w_ada.shape

    def body(c_ref, w_ref, b_ref, o_ref):
        cv = c_ref[...]
        act = cv * _sigmoid(cv)
        o_ref[...] = jnp.dot(act, w_ref[...], preferred_element_type=F32, precision=lax.Precision.HIGHEST) + b_ref[...]

    return pl.pallas_call(
        body, name=name, grid=(nl,),
        in_specs=[pl.BlockSpec((8, D), lambda i: (0, 0)), pl.BlockSpec((None, D, NA), lambda i: (i, 0, 0)),
                  pl.BlockSpec((None, 1, NA), lambda i: (i, 0, 0))],
        out_specs=pl.BlockSpec((None, 8, NA), lambda i: (i, 0, 0)),
        out_shape=jax.ShapeDtypeStruct((nl, 8, NA), F32), compiler_params=_cp("parallel"))(c_all, w_ada, b_sh)


def _adamw(w, g, m, v):
    m = ADAM_B1 * m + (1.0 - ADAM_B1) * g
    v = ADAM_B2 * v + (1.0 - ADAM_B2) * (g * g)
    m_hat = m / (1.0 - ADAM_B1 ** ADAM_STEP)
    v_hat = v / (1.0 - ADAM_B2 ** ADAM_STEP)
    return -ADAM_LR * (m_hat / (jnp.sqrt(v_hat) + ADAM_EPS) + ADAM_WD * w), m, v


def _adam_rows(R, C):
    cap = max(8, (256 * 1024) // C)
    for t in range(min(R, cap), 0, -1):
        if R % t == 0 and (t % 8 == 0 or t == R):
            return t
    return R


def adamw_ada(c_t, dm, w, m, v, name):
    nl, D, NA = w.shape
    TK = _tile(D, (256, 128))

    def body(c_ref, dm_ref, w_ref, m_ref, v_ref, g_ref, d_ref, nm_ref, nv_ref):
        cv = c_ref[...]
        act = cv * _sigmoid(cv)
        g = jnp.dot(act, dm_ref[...], preferred_element_type=F32, precision=lax.Precision.HIGHEST)
        g_ref[...] = g
        d_ref[...], nm_ref[...], nv_ref[...] = _adamw(w_ref[...], g, m_ref[...], v_ref[...])

    big = pl.BlockSpec((None, TK, NA), lambda i, k: (i, k, 0))
    shape = jax.ShapeDtypeStruct(w.shape, F32)
    return pl.pallas_call(
        body, name=name, grid=(nl, D // TK),
        in_specs=[pl.BlockSpec((TK, 8), lambda i, k: (k, 0)), pl.BlockSpec((None, 8, NA), lambda i, k: (i, 0, 0)),
                  big, big, big],
        out_specs=[big] * 4, out_shape=[shape] * 4, compiler_params=_cp("parallel", "parallel"))(c_t, dm, w, m, v)


def adamw_sharded(w, m, v, ga, gb, name):
    nl, R, C = w.shape
    TR = _adam_rows(R, C)

    def body(w_ref, m_ref, v_ref, a_ref, b_ref, g_ref, d_ref, nm_ref, nv_ref):
        g = a_ref[...] + b_ref[...]
        g_ref[...] = g
        d_ref[...], nm_ref[...], nv_ref[...] = _adamw(w_ref[...], g, m_ref[...], v_ref[...])

    big = pl.BlockSpec((None, TR, C), lambda i, r: (i, r, 0))
    shape = jax.ShapeDtypeStruct(w.shape, F32)
    return pl.pallas_call(
        body, name=name, grid=(nl, R // TR), in_specs=[big] * 5,
        out_specs=[big] * 4, out_shape=[shape] * 4, compiler_params=_cp("parallel", "parallel"))(w, m, v, ga, gb)


def adamw_slab(g, w, m, v, name):
    R, C = g.shape
    TR = _tile(R, (160, 80, 40, 8))

    def body(g_ref, w_ref, m_ref, v_ref, d_ref, nm_ref, nv_ref):
        d_ref[...], nm_ref[...], nv_ref[...] = _adamw(w_ref[...], g_ref[...], m_ref[...], v_ref[...])

    big = pl.BlockSpec((TR, C), lambda r: (r, 0))
    shape = jax.ShapeDtypeStruct((R, C), F32)
    return pl.pallas_call(
        body, name=name, grid=(R // TR,), in_specs=[big] * 4,
        out_specs=[big] * 3, out_shape=[shape] * 3, compiler_params=_cp("parallel"))(g, w, m, v)


def adamw_plain(w, m, v, g, name):
    def body(w_ref, m_ref, v_ref, g_ref, d_ref, nm_ref, nv_ref):
        d_ref[...], nm_ref[...], nv_ref[...] = _adamw(w_ref[...], g_ref[...], m_ref[...], v_ref[...])

    shape = jax.ShapeDtypeStruct(w.shape, F32)
    return pl.pallas_call(body, name=name, out_shape=[shape] * 3,
                          compiler_params=pltpu.CompilerParams(vmem_limit_bytes=VMEM_LIMIT))(w, m, v, g)


def _slab_rows(a):
    n = a.size
    rows = -(-n // SLAB_W)
    return -(-rows // 8) * 8


def _pack(arrs, pad_rows_to=0):
    out = []
    for a in arrs:
        rows = _slab_rows(a)
        flat = a.reshape(-1).astype(F32)
        flat = jnp.pad(flat, (0, rows * SLAB_W - flat.shape[0]))
        out.append(flat.reshape(rows, SLAB_W))
    total = sum(o.shape[0] for o in out)
    if pad_rows_to and total % pad_rows_to:
        out.append(jnp.zeros((pad_rows_to - total % pad_rows_to, SLAB_W), F32))
    return jnp.concatenate(out, axis=0)


def _unpack(slab, like):
    out, r = [], 0
    for a in like:
        rows = _slab_rows(a)
        out.append(slab[r:r + rows].reshape(-1)[:a.size].reshape(a.shape))
        r += rows
    return out


WEIGHTS = ['norm1_g', 'norm2_g', 'w_ada', 'b_ada', 'ssm_a_re', 'ssm_a_im', 'ssm_log_step', 'ssm_b_re', 'ssm_b_im',
           'ssm_c_re', 'ssm_c_im', 'ssm_d', 'ssm_w_out', 'conv_w_in', 'conv_w', 'conv_w_out', 'w_ffn_in',
           'w_ffn_out', 'final_g']
SLAB = ['norm1_g', 'norm2_g', 'b_ada', 'ssm_a_re', 'ssm_a_im', 'ssm_log_step', 'ssm_b_re', 'ssm_b_im', 'ssm_c_re',
        'ssm_c_im', 'ssm_d', 'final_g']
SHARDED = ['ssm_w_out', 'conv_w_in', 'conv_w_out', 'w_ffn_in', 'w_ffn_out']


def kernel(x, c, norm1_g, norm2_g, w_ada, b_ada, ssm_a_re, ssm_a_im, ssm_log_step, ssm_b_re, ssm_b_im, ssm_c_re, ssm_c_im, ssm_d, ssm_w_out, conv_w_in, conv_w, conv_w_out, w_ffn_in, w_ffn_out, final_g, loss_target, m_norm1_g, m_norm2_g, m_w_ada, m_b_ada, m_ssm_a_re, m_ssm_a_im, m_ssm_log_step, m_ssm_b_re, m_ssm_b_im, m_ssm_c_re, m_ssm_c_im, m_ssm_d, m_ssm_w_out, m_conv_w_in, m_conv_w, m_conv_w_out, m_w_ffn_in, m_w_ffn_out, m_final_g, v_norm1_g, v_norm2_g, v_w_ada, v_b_ada, v_ssm_a_re, v_ssm_a_im, v_ssm_log_step, v_ssm_b_re, v_ssm_b_im, v_ssm_c_re, v_ssm_c_im, v_ssm_d, v_ssm_w_out, v_conv_w_in, v_conv_w, v_conv_w_out, v_w_ffn_in, v_w_ffn_out, v_final_g):
    given = dict(locals())
    W = {n: given[n] for n in WEIGHTS}
    Mo = {n: given["m_" + n] for n in WEIGHTS}
    Vo = {n: given["v_" + n] for n in WEIGHTS}

    xs = x[0]
    tgt = loss_target[0]
    L, D = xs.shape
    nlayer = norm1_g.shape[0]
    NA = w_ada.shape[2]
    G = ssm_a_re.shape[1]
    nkb = D // S5_BLOCK
    ax, ay, ac = _axes()
    me = 4 * ax + 2 * ay + ac
    chip = 2 * ax + ay

    assert D == SLAB_W
    first = gather8(jnp.concatenate([jnp.broadcast_to(c, (8, D)), _pack([conv_w])], axis=0), "gather_c_conv_w")
    c_all = first[:, 0, :]
    b_sh = lax.dynamic_slice_in_dim(b_ada, chip * NA, NA, axis=1)[:, None, :]
    mods_part = ada_mods(c_all, w_ada, b_sh, "ada_mods")
    mg = gather8(mods_part.reshape(nlayer * 8, NA), "gather_mods")
    mg = mg.reshape(N_CHIP, 2, nlayer, 8, NA)[:, 0]
    mods_all = lax.dynamic_index_in_dim(mg, me, axis=2, keepdims=False)
    mods_all = jnp.transpose(mods_all, (1, 0, 2)).reshape(nlayer, 6, D)

    cw_parts = first[:, 8:]
    nconv = conv_w.shape[0]
    cw_full = jnp.stack([_unpack(cw_parts[2 * q], [conv_w])[0] for q in range(N_CHIP)], axis=2)
    cw_full = cw_full.reshape(nconv, 3, D)

    in_flight_w = {}

    def start_weights(i, after):
        names = (["ssm_w_out"] if i % 2 == 0 else ["conv_w_in", "conv_w_out"]) + ["w_ffn_in", "w_ffn_out"]
        shards = [W[n][i if n.startswith("w_ffn") else i // 2].astype(BF) for n in names]
        sems, srcs, lands, tok = gather_start(shards, after, "gather_start%d" % i)
        in_flight_w[i] = (names, sems, srcs, lands)
        return tok

    def relay_weights(i, after):
        names, sems, srcs, lands = in_flight_w[i]
        got = gather_wait(sems, srcs, lands, list(range(len(names))), after, "gather_wait%d" % i)
        rsems, rlands, tok = relay_start(got, after, "relay_start%d" % i)
        in_flight_w[i] = (names, rsems, rlands)
        return tok

    def layer_weights(i, after):
        names, rsems, rlands = in_flight_w[i]
        return dict(zip(names, relay_wait(rsems, rlands, after, "relay_wait%d" % i)))

    token = start_weights(0, cw_full + mods_all[0, 0:3])
    mods_all = mods_all + token[0:1, 0:1]

    s5 = []
    for j in range(ssm_a_re.shape[0]):
        disc, disc_vjp = jax.vjp(_discretise, ssm_a_re[j], ssm_a_im[j], ssm_log_step[j], ssm_b_re[j], ssm_b_im[j])
        abr, abi, bbar_re, bbar_im = disc
        tb, tbt = _compact(jnp.swapaxes(bbar_re, 1, 2), jnp.swapaxes(bbar_im, 1, 2), nkb)
        tc, tct = _compact(ssm_c_re[j], -ssm_c_im[j], nkb)
        chunk = _tile(L, (512, 256)) // 8
        s5.append(dict(vjp=disc_vjp, tb=tb, tbt=tbt, tc=tc, tct=tct, pw=_scan_powers(abr, abi, nkb, False, chunk),
                       pwr=_scan_powers(abr, abi, nkb, True, chunk)))

    saved = []
    xcur = xs
    for i in range(nlayer):
        j = i // 2
        mods = mods_all[i]
        sv = dict(x=xcur)
        if i % 2 == 0:
            h = norm_mod(xcur, norm1_g[i:i + 1], mods, 0, F32, "norm_mod_s5")
            dvec = ssm_d[j:j + 1]
            if i == 0:
                dvec = dvec + start_weights(1, h)[0:1, 0:1]
            states, yv, z = s5_fwd(h, s5[j]["tb"], s5[j]["tct"], s5[j]["pw"], dvec, "s5_fwd")
            if i == 0:
                mods = mods + relay_weights(0, z)[0:1, 0:1]
            full = layer_weights(i, z)
            o, mix, x2 = ssm_out_glu(z, full["ssm_w_out"], xcur, mods, 2, "ssm_out_glu")
            sv.update(h=h, states=states, y=yv, z=z, o=o)
        else:
            h = norm_mod(xcur, norm1_g[i:i + 1], mods, 0, BF, "norm_mod")
            full = layer_weights(i, h)
            p = mm_nn(h, full["conv_w_in"], BF, "mm_conv_in")
            mc = conv_fwd(p, cw_full[j], "conv_fwd")
            mix, x2 = mm_nn(mc, full["conv_w_out"].reshape(1, D, D), BF, "mm_conv_out", res=xcur, gate=mods[2:3])
            sv.update(h=h, p=p, mc=mc)
        h2 = norm_mod(x2, norm2_g[i:i + 1], mods, 3, BF, "norm_mod")
        gu, act = ffn_in_act(h2, full["w_ffn_in"], "ffn_in_act")
        F = act.shape[1]
        ff, x3 = mm_nn(act, full["w_ffn_out"].reshape(1, F, D), BF, "mm_ffn_out", res=x2, gate=mods[5:6])
        sv.update(mix=mix, x2=x2, h2=h2, gu=gu, act=act, ff=ff, w=full)
        saved.append(sv)
        xcur = x3
        if i + 1 < nlayer:
            token = relay_weights(i + 1, ff)
            if i + 2 < nlayer:
                token = token + start_weights(i + 2, token)
            mods_all = mods_all + token[0:1, 0:1]

    loss_blk, dx, dfinal, dff = final_loss(xcur, tgt, final_g[None, :], saved[-1]["ff"], mods_all[nlayer - 1], 5,
                                           "final_loss")
    dg2 = dfinal[1:2]

    gland = {n: lax.empty((W[n].shape[0], N_CHIP) + W[n].shape[1:], BF) for n in SHARDED}
    in_flight = []
    dmods = [None] * nlayer
    dnorm1, dnorm2 = [None] * nlayer, [None] * nlayer
    dconv_w = [None] * nconv
    ds5 = [None] * ssm_a_re.shape[0]
    token = jnp.zeros((8, 128), F32)

    def send_grads(names, grads, slot, after, name):
        sems, thru, lands, tok = scatter_start([grads[n] for n in names], [gland[n] for n in names], slot, after, name)
        gland.update(zip(names, lands))
        in_flight.append((names, slot, sems, thru, name))
        return tok

    def land_grads(group, after):
        for names, slot, sems, thru, name in in_flight:
            if names[0] in group:
                got = scatter_wait(sems, thru, [gland[n] for n in names], slot, after, name.replace("scatter", "landed"))
                gland.update(zip(names, got))

    for i in reversed(range(nlayer)):
        j = i // 2
        mods = mods_all[i] + token[0:1, 0:1]
        sv = saved[i]
        full = sv["w"]
        gfull = {}
        F = sv["act"].shape[1]
        gfull["w_ffn_out"] = mm_tn(sv["act"], dff, 1, "mm_tn_ffn_out").reshape(N_CHIP, F // N_CHIP, D)
        dgu = ffn_out_bwd(dff, full["w_ffn_out"].reshape(F, D), sv["gu"], "ffn_out_bwd")
        gfull["w_ffn_in"] = mm_tn(sv["h2"], dgu, N_CHIP, "mm_tn_ffn_in")
        dh2 = mm_nt(dgu, full["w_ffn_in"], BF, "mm_nt_ffn_in")
        token = send_grads(["w_ffn_out", "w_ffn_in"], gfull, [i, i], dh2, "scatter_ffn%d" % i)
        mods = mods + token[0:1, 0:1]
        dx2, s2, dmix = norm_bwd(dh2, sv["x2"], dx, norm2_g[i:i + 1], mods, 3, "norm_bwd_mix",
                                 branch=(sv["mix"], mods, 2))
        dg1 = s2[3:4]
        if i % 2 == 0:
            do = glu_bwd(dmix, sv["o"], "glu_bwd")
            gfull["ssm_w_out"] = mm_tn(sv["z"], do, N_CHIP, "mm_tn_ssm_out")
            dz = mm_nt(do, full["ssm_w_out"], BF, "mm_nt_ssm_out")
            dh, dd, dab, db, dc = s5_bwd(dz, sv["y"], sv["h"], sv["states"], s5[j]["tc"], s5[j]["tbt"], s5[j]["pwr"],
                                         ssm_d[j:j + 1], "s5_bwd")
            ds5[j] = (dd, dab, db, dc)
        else:
            gfull["conv_w_out"] = mm_tn(sv["mc"], dmix, 1, "mm_tn_conv_out").reshape(N_CHIP, D // N_CHIP, D)
            dmc = mm_nt(dmix, full["conv_w_out"].reshape(1, D, D), BF, "mm_nt_conv_out")
            dbg, dcg, dvv, dcw = conv_bwd(dmc, sv["p"], cw_full[j], "conv_bwd")
            dp = jnp.concatenate([dbg, dcg, dvv], axis=1)
            gfull["conv_w_in"] = mm_tn(sv["h"], dp, N_CHIP, "mm_tn_conv_in")
            dh = mm_nt(dp, full["conv_w_in"], BF, "mm_nt_conv_in")
            dconv_w[j] = dcw[0:3]
        dmods_i = [s2[0:2], dg2]
        if i > 0:
            dx, s1, dff = norm_bwd(dh, sv["x"], dx2, norm1_g[i:i + 1], mods, 0, "norm_bwd_ffn",
                                   branch=(saved[i - 1]["ff"], mods_all[i - 1], 5))
            dg2 = s1[3:4]
        else:
            dx, s1 = norm_bwd(dh, sv["x"], dx2, norm1_g[i:i + 1], mods, 0, "norm_bwd")
        dmods[i] = jnp.concatenate([s1[0:2], dg1] + dmods_i, axis=0).reshape(6 * D)
        dnorm1[i], dnorm2[i] = s1[2], s2[2]
        names = ["ssm_w_out"] if i % 2 == 0 else ["conv_w_out", "conv_w_in"]
        token = send_grads(names, gfull, [j] * len(names), dx, "scatter_mix%d" % i)

    small = dict(norm1_g=jnp.stack(dnorm1), norm2_g=jnp.stack(dnorm2), b_ada=jnp.stack(dmods),
                 final_g=dfinal[0] + token[0, 0])
    per = {n: [] for n in ('ssm_a_re', 'ssm_a_im', 'ssm_log_step', 'ssm_b_re', 'ssm_b_im', 'ssm_c_re', 'ssm_c_im', 'ssm_d')}
    GL = G // nkb
    for j, (dd, dab, db, dc) in enumerate(ds5):
        dab = jnp.sum(dab, axis=1).reshape(nkb, 2, GL, SSM_STATE)
        g_abr, g_abi = dab[:, 0].reshape(G, SSM_STATE), dab[:, 1].reshape(G, SSM_STATE)
        db, dc = db.reshape(G, SSM_GROUP, 2, SSM_STATE), dc.reshape(G, SSM_GROUP, 2, SSM_STATE)
        gb_re, gb_im, gc_re, gc_im = db[:, :, 0], db[:, :, 1], dc[:, :, 0], dc[:, :, 1]
        ga_re, ga_im, gls, gbr, gbi = s5[j]["vjp"]((g_abr, g_abi, jnp.swapaxes(gb_re, 1, 2), jnp.swapaxes(gb_im, 1, 2)))
        for n, val in zip(per, (ga_re, ga_im, gls, gbr, gbi, gc_re, -gc_im, jnp.sum(dd, axis=0))):
            per[n].append(val)
    small.update({n: jnp.stack(vals) for n, vals in per.items()})
    dcw_full = jnp.stack(dconv_w)

    my_loss = loss_blk[0:1, 0:1]
    slab_like = [W[n] for n in SLAB] + [dcw_full, my_loss]
    rows64 = 8 * N_DEV
    slab = _pack([small[n] for n in SLAB] + [dcw_full, my_loss], rows64)
    per_dev = slab.shape[0] // N_DEV
    x_sems, x_srcs, x_lands, token = exchange_start(
        [(slab.reshape(N_DEV, per_dev, SLAB_W), True), (_pack([small["b_ada"]]), False)], dx, "small_scatter")

    early = [n for n in SHARDED if n != "ssm_w_out"]
    land_grads(early, token)
    mine = [reduce4(gland[n], "reduce4_" + n) for n in early]

    parts, dm_all = exchange_wait(x_sems, x_srcs, x_lands, [True, False], mine[-1][0, :8, :128], "small_landed")
    t_sems, t_srcs, t_lands, token = exchange_start([(sum8(parts, "sum_small"), False)], dm_all, "small_gather")
    out = {}

    w_sems, w_srcs, w_lands, token2 = swap_start(mine, "swap_start")
    dm_all = dm_all.reshape(N_DEV, -1)[:, :b_ada.size].reshape(N_DEV, nlayer, N_CHIP, NA)
    dm_sh = jnp.transpose(lax.dynamic_index_in_dim(dm_all, chip, axis=2, keepdims=False), (1, 0, 2))
    res = adamw_ada(jnp.transpose(c_all) + token[0:1, 0:1] + token2[0:1, 0:1], dm_sh, w_ada, m_w_ada, v_w_ada,
                    "adamw_ada")
    out["g", "w_ada"], out["d", "w_ada"], out["m", "w_ada"], out["v", "w_ada"] = res

    g_slab = exchange_wait(t_sems, t_srcs, t_lands, [False], out["g", "w_ada"], "small_total")[0]
    g_slab = g_slab.reshape(slab.shape)
    d_slab, m_slab, v_slab = adamw_slab(
        g_slab, _pack([W[n] for n in SLAB] + [jnp.zeros_like(dcw_full)], rows64),
        _pack([Mo[n] for n in SLAB] + [jnp.zeros_like(dcw_full)], rows64),
        _pack([Vo[n] for n in SLAB] + [jnp.ones_like(dcw_full)], rows64), "adamw_slab")
    for k, slab in zip(("g", "d", "m", "v"), (g_slab, d_slab, m_slab, v_slab)):
        for n, val in zip(SLAB, _unpack(slab, slab_like)):
            out[k, n] = val
    g_cw = lax.dynamic_slice_in_dim(_unpack(g_slab, slab_like)[-2], chip * conv_w.shape[2], conv_w.shape[2], axis=2)
    out["g", "conv_w"] = g_cw
    out["d", "conv_w"], out["m", "conv_w"], out["v", "conv_w"] = [
        r.reshape(conv_w.shape) for r in adamw_plain(conv_w.reshape(-1, conv_w.shape[2]), m_conv_w.reshape(-1, conv_w.shape[2]),
                                                     v_conv_w.reshape(-1, conv_w.shape[2]), g_cw.reshape(-1, conv_w.shape[2]),
                                                     "adamw_conv_w")]

    mine, theirs = swap_wait(w_sems, w_srcs, w_lands, d_slab, "swap_wait")
    for n, ga, gb in zip(early, mine, theirs):
        r = adamw_sharded(W[n], Mo[n], Vo[n], ga, gb, "adamw_" + n)
        out["g", n], out["d", n], out["m", n], out["v", n] = r

    land_grads(["ssm_w_out"], out["g", "w_ffn_out"])
    ga = reduce4(gland["ssm_w_out"], "reduce4_ssm_w_out")
    gb = swap_siblings([ga], "swap_siblings")[0]
    r = adamw_sharded(ssm_w_out, m_ssm_w_out, v_ssm_w_out, ga, gb, "adamw_ssm_w_out")
    out["g", "ssm_w_out"], out["d", "ssm_w_out"], out["m", "ssm_w_out"], out["v", "ssm_w_out"] = r

    loss = _unpack(g_slab, slab_like)[-1][0, 0]
    return (loss, dx[None], *[out["g", n] for n in WEIGHTS], *[out["d", n] for n in WEIGHTS],
            *[out["m", n] for n in WEIGHTS], *[out["v", n] for n in WEIGHTS])
```

```python
import math

import jax
import jax.numpy as jnp
from jax import lax
from jax.experimental import pallas as pl
from jax.experimental.pallas import tpu as pltpu

F32 = jnp.float32
BF = jnp.bfloat16
MESH = pl.DeviceIdType.MESH
ANY = pl.BlockSpec(memory_space=pl.ANY)

N_DEV = 8
N_CHIP = 4
SSM_GROUP = 16
SSM_STATE = 64
S5_BLOCK = 256
RMS_EPS = 1e-6
ADAM_LR, ADAM_B1, ADAM_B2, ADAM_EPS, ADAM_WD, ADAM_STEP = 0.001, 0.9, 0.999, 1e-08, 0.01, 10
V7X_VMEM_BYTES = 64 * 1024 * 1024
VMEM_LIMIT = V7X_VMEM_BYTES - 12 * 1024 * 1024
SLAB_W = 1024
GELU_C = math.sqrt(2.0 / math.pi)
GELU_A = 0.044715


def _cp(*sem):
    return pltpu.CompilerParams(dimension_semantics=sem if sem else None, vmem_limit_bytes=VMEM_LIMIT)


def _tile(n, prefs):
    for p in prefs:
        if p <= n and n % p == 0:
            return p
    return n


def _sigmoid(v):
    return 0.5 * jnp.tanh(0.5 * v) + 0.5


def _axes():
    return lax.axis_index("x"), lax.axis_index("y"), lax.axis_index("c")


def _flip(v, k):
    return 1 - v if k else v


def gather8(v, name):
    R, C = v.shape

    def body(v_ref, o_ref, ssem, rsem, lsem):
        x, y, c = _axes()
        me = 4 * x + 2 * y + c
        loc = pltpu.make_async_copy(v_ref, o_ref.at[me], lsem)
        loc.start()
        copies = []
        for k in range(1, N_DEV):
            peer = (_flip(x, (k >> 2) & 1), _flip(y, (k >> 1) & 1), _flip(c, k & 1))
            cp = pltpu.make_async_remote_copy(src_ref=v_ref, dst_ref=o_ref.at[me], send_sem=ssem.at[k - 1],
                                              recv_sem=rsem.at[k - 1], device_id=peer, device_id_type=MESH)
            cp.start()
            copies.append(cp)
        for cp in copies:
            cp.wait()
        loc.wait()

    return pl.pallas_call(
        body, name=name,
        out_shape=jax.ShapeDtypeStruct((N_DEV, R, C), v.dtype),
        in_specs=[pl.BlockSpec(memory_space=pltpu.VMEM)],
        out_specs=pl.BlockSpec(memory_space=pltpu.VMEM),
        scratch_shapes=[pltpu.SemaphoreType.DMA((N_DEV - 1,)), pltpu.SemaphoreType.DMA((N_DEV - 1,)),
                        pltpu.SemaphoreType.DMA],
        compiler_params=pltpu.CompilerParams(vmem_limit_bytes=VMEM_LIMIT),
    )(v)


HBM = pl.BlockSpec(memory_space=pltpu.HBM)
SEM = pl.BlockSpec(memory_space=pltpu.SEMAPHORE)
EFFECT = pltpu.SideEffectType.DATAFLOW_SIDE_EFFECTING


def _in_hbm(a):
    return pltpu.with_memory_space_constraint(a, pltpu.HBM)


def _chip_peers(x, y, c):
    out = []
    for k in range(1, N_CHIP):
        px, py = _flip(x, k >> 1), _flip(y, k & 1)
        out.append(((px, py, c), 2 * px + py))
    return out


def _my_half(ref, c):
    rows = ref.shape[0] // 2
    return pl.ds(pl.multiple_of(c * rows, 16), rows)


def relay_start(lands, after, name):
    n = len(lands)

    def body(*refs):
        land = refs[:n]
        ssem, rsem = refs[n + 1:n + 3]
        token = refs[-1]
        x, y, c = _axes()
        for a in range(n):
            half = _my_half(land[a].at[0], c)
            for k, (_, pchip) in enumerate(_chip_peers(x, y, c)):
                pltpu.make_async_remote_copy(src_ref=land[a].at[pchip, half], dst_ref=land[a].at[pchip, half],
                                             send_sem=ssem.at[3 * a + k], recv_sem=rsem.at[3 * a + k],
                                             device_id=(x, y, 1 - c), device_id_type=MESH).start()
        token[...] = jnp.zeros_like(token)

    out_shape = ([pltpu.SemaphoreType.DMA((3 * n,)), pltpu.SemaphoreType.DMA((3 * n,))]
                 + [pltpu.HBM(l.shape, l.dtype) for l in lands] + [jax.ShapeDtypeStruct((8, 128), F32)])
    res = pl.pallas_call(
        body, name=name, out_shape=out_shape, in_specs=[HBM] * n + [ANY],
        out_specs=[SEM, SEM] + [HBM] * n + [pl.BlockSpec(memory_space=pltpu.VMEM)],
        input_output_aliases={a: 2 + a for a in range(n)},
        compiler_params=pltpu.CompilerParams(has_side_effects=EFFECT),
    )(*lands, after)
    return tuple(res[:2]), list(res[2:2 + n]), res[-1]


def relay_wait(sems, lands, after, name):
    n = len(lands)

    def body(*refs):
        land = refs[:n]
        ssem, rsem = refs[n:n + 2]
        x, y, c = _axes()
        for a in range(n):
            mine, theirs = _my_half(land[a].at[0], c), _my_half(land[a].at[0], 1 - c)
            for k, (_, pchip) in enumerate(_chip_peers(x, y, c)):
                cp = pltpu.make_async_remote_copy(src_ref=land[a].at[pchip, mine], dst_ref=land[a].at[pchip, theirs],
                                                  send_sem=ssem.at[3 * a + k], recv_sem=rsem.at[3 * a + k],
                                                  device_id=(x, y, 1 - c), device_id_type=MESH)
                cp.wait_send()
                cp.wait_recv()

    res = pl.pallas_call(
        body, name=name, out_shape=[pltpu.HBM(l.shape, l.dtype) for l in lands],
        in_specs=[HBM] * n + [SEM, SEM, ANY], out_specs=[HBM] * n,
        input_output_aliases={a: a for a in range(n)},
        compiler_params=pltpu.CompilerParams(has_side_effects=EFFECT),
    )(*lands, *sems, after)
    return list(res)


def gather_start(shards, after, name):
    n = len(shards)

    def body(*refs):
        src, land = refs[:n], refs[n:2 * n]
        ssem, rsem, lsem = refs[2 * n + 1:2 * n + 4]
        token = refs[-1]
        x, y, c = _axes()
        chip = 2 * x + y
        for a in range(n):
            pltpu.make_async_copy(src[a], land[a].at[chip], lsem.at[a]).start()
            half = _my_half(src[a], c)
            for k, (peer, _) in enumerate(_chip_peers(x, y, c)):
                pltpu.make_async_remote_copy(src_ref=src[a].at[half], dst_ref=land[a].at[chip, half],
                                             send_sem=ssem.at[3 * a + k], recv_sem=rsem.at[3 * a + k],
                                             device_id=peer, device_id_type=MESH).start()
        token[...] = jnp.zeros_like(token)

    lands = [lax.empty((N_CHIP,) + s.shape, s.dtype) for s in shards]
    out_shape = ([pltpu.SemaphoreType.DMA((3 * n,)), pltpu.SemaphoreType.DMA((3 * n,)), pltpu.SemaphoreType.DMA((n,))]
                 + [pltpu.HBM(s.shape, s.dtype) for s in shards] + [pltpu.HBM(l.shape, l.dtype) for l in lands]
                 + [jax.ShapeDtypeStruct((8, 128), F32)])
    res = pl.pallas_call(
        body, name=name, out_shape=out_shape, in_specs=[HBM] * (2 * n) + [ANY],
        out_specs=[SEM, SEM, SEM] + [HBM] * (2 * n) + [pl.BlockSpec(memory_space=pltpu.VMEM)],
        input_output_aliases={a: 3 + a for a in range(2 * n)},
        compiler_params=pltpu.CompilerParams(has_side_effects=EFFECT),
    )(*[_in_hbm(s) for s in shards], *[_in_hbm(l) for l in lands], after)
    return tuple(res[:3]), list(res[3:3 + n]), list(res[3 + n:3 + 2 * n]), res[-1]


def gather_wait(sems, srcs, lands, idx, after, name):
    m = len(idx)

    def body(*refs):
        src, land = refs[:m], refs[m:2 * m]
        ssem, rsem, lsem = refs[2 * m:2 * m + 3]
        x, y, c = _axes()
        chip = 2 * x + y
        for j, a in enumerate(idx):
            half = _my_half(src[j], c)
            for k, (peer, pchip) in enumerate(_chip_peers(x, y, c)):
                cp = pltpu.make_async_remote_copy(src_ref=src[j].at[half], dst_ref=land[j].at[pchip, half],
                                                  send_sem=ssem.at[3 * a + k], recv_sem=rsem.at[3 * a + k],
                                                  device_id=peer, device_id_type=MESH)
                cp.wait_send()
                cp.wait_recv()
            pltpu.make_async_copy(src[j], land[j].at[chip], lsem.at[a]).wait()

    s_in = [srcs[a] for a in idx]
    l_in = [lands[a] for a in idx]
    res = pl.pallas_call(
        body, name=name,
        out_shape=[pltpu.HBM(s.shape, s.dtype) for s in s_in] + [pltpu.HBM(l.shape, l.dtype) for l in l_in],
        in_specs=[HBM] * (2 * m) + [SEM, SEM, SEM, ANY], out_specs=[HBM] * (2 * m),
        input_output_aliases={a: a for a in range(2 * m)},
        compiler_params=pltpu.CompilerParams(has_side_effects=EFFECT),
    )(*s_in, *l_in, *sems, after)
    return list(res[m:])


def scatter_start(grads, lands, slot, after, name):
    n = len(grads)

    def body(*refs):
        src, land = refs[:n], refs[n:2 * n]
        ssem, rsem, lsem = refs[2 * n + 1:2 * n + 4]
        token = refs[-1]
        x, y, c = _axes()
        chip = 2 * x + y
        for a in range(n):
            pltpu.make_async_copy(src[a].at[chip], land[a].at[slot[a], chip], lsem.at[a]).start()
            for k, (peer, pchip) in enumerate(_chip_peers(x, y, c)):
                pltpu.make_async_remote_copy(src_ref=src[a].at[pchip], dst_ref=land[a].at[slot[a], chip],
                                             send_sem=ssem.at[3 * a + k], recv_sem=rsem.at[3 * a + k],
                                             device_id=peer, device_id_type=MESH).start()
        token[...] = jnp.zeros_like(token)

    out_shape = ([pltpu.SemaphoreType.DMA((3 * n,)), pltpu.SemaphoreType.DMA((3 * n,)), pltpu.SemaphoreType.DMA((n,))]
                 + [pltpu.HBM(g.shape, g.dtype) for g in grads] + [pltpu.HBM(l.shape, l.dtype) for l in lands]
                 + [jax.ShapeDtypeStruct((8, 128), F32)])
    res = pl.pallas_call(
        body, name=name, out_shape=out_shape, in_specs=[HBM] * (2 * n) + [ANY],
        out_specs=[SEM, SEM, SEM] + [HBM] * (2 * n) + [pl.BlockSpec(memory_space=pltpu.VMEM)],
        input_output_aliases={a: 3 + a for a in range(2 * n)},
        compiler_params=pltpu.CompilerParams(has_side_effects=EFFECT),
    )(*[_in_hbm(g) for g in grads], *[_in_hbm(l) for l in lands], after)
    return tuple(res[:3]), list(res[3:3 + n]), list(res[3 + n:3 + 2 * n]), res[-1]


def scatter_wait(sems, grads, lands, slot, after, name):
    n = len(grads)

    def body(*refs):
        src, land = refs[:n], refs[n:2 * n]
        ssem, rsem, lsem = refs[2 * n:2 * n + 3]
        x, y, c = _axes()
        chip = 2 * x + y
        for a in range(n):
            for k, (peer, pchip) in enumerate(_chip_peers(x, y, c)):
                cp = pltpu.make_async_remote_copy(src_ref=src[a].at[pchip], dst_ref=land[a].at[slot[a], pchip],
                                                  send_sem=ssem.at[3 * a + k], recv_sem=rsem.at[3 * a + k],
                                                  device_id=peer, device_id_type=MESH)
                cp.wait_send()
                cp.wait_recv()
            pltpu.make_async_copy(src[a].at[chip], land[a].at[slot[a], chip], lsem.at[a]).wait()

    res = pl.pallas_call(
        body, name=name,
        out_shape=[pltpu.HBM(g.shape, g.dtype) for g in grads] + [pltpu.HBM(l.shape, l.dtype) for l in lands],
        in_specs=[HBM] * (2 * n) + [SEM, SEM, SEM, ANY], out_specs=[HBM] * (2 * n),
        input_output_aliases={a: a for a in range(2 * n)},
        compiler_params=pltpu.CompilerParams(has_side_effects=EFFECT),
    )(*grads, *lands, *sems, after)
    return list(res[n:])


def reduce4(land, name):
    nl, _, R, C = land.shape
    TR = _adam_rows(R, C)

    def body(l_ref, o_ref):
        o_ref[...] = ((l_ref[0].astype(F32) + l_ref[1].astype(F32)) + l_ref[2].astype(F32)) + l_ref[3].astype(F32)

    return pl.pallas_call(
        body, name=name, grid=(nl, R // TR),
        in_specs=[pl.BlockSpec((None, N_CHIP, TR, C), lambda i, r: (i, 0, r, 0))],
        out_specs=pl.BlockSpec((None, TR, C), lambda i, r: (i, r, 0)),
        out_shape=jax.ShapeDtypeStruct((nl, R, C), F32), compiler_params=_cp("parallel", "parallel"))(land)


def swap_siblings(arrs, name):
    n = len(arrs)

    def body(*refs):
        src, dst = refs[:n], refs[n:2 * n]
        ssem, rsem = refs[2 * n:]
        x, y, c = _axes()
        cps = [pltpu.make_async_remote_copy(src_ref=src[a], dst_ref=dst[a], send_sem=ssem.at[a], recv_sem=rsem.at[a],
                                            device_id=(x, y, 1 - c), device_id_type=MESH) for a in range(n)]
        for cp in cps:
            cp.start()
        for cp in cps:
            cp.wait()

    return pl.pallas_call(
        body, name=name, out_shape=[jax.ShapeDtypeStruct(a.shape, a.dtype) for a in arrs],
        in_specs=[ANY] * n, out_specs=[ANY] * n,
        scratch_shapes=[pltpu.SemaphoreType.DMA((n,)), pltpu.SemaphoreType.DMA((n,))],
        compiler_params=pltpu.CompilerParams(vmem_limit_bytes=VMEM_LIMIT),
    )(*arrs)


def swap_start(arrs, name):
    n = len(arrs)

    def body(*refs):
        src, land = refs[:n], refs[n:2 * n]
        ssem, rsem = refs[2 * n:2 * n + 2]
        token = refs[-1]
        x, y, c = _axes()
        for a in range(n):
            pltpu.make_async_remote_copy(src_ref=src[a], dst_ref=land[a], send_sem=ssem.at[a], recv_sem=rsem.at[a],
                                         device_id=(x, y, 1 - c), device_id_type=MESH).start()
        token[...] = jnp.zeros_like(token)

    lands = [lax.empty(a.shape, a.dtype) for a in arrs]
    out_shape = ([pltpu.SemaphoreType.DMA((n,)), pltpu.SemaphoreType.DMA((n,))]
                 + [pltpu.HBM(a.shape, a.dtype) for a in arrs] * 2 + [jax.ShapeDtypeStruct((8, 128), F32)])
    res = pl.pallas_call(
        body, name=name, out_shape=out_shape, in_specs=[HBM] * (2 * n),
        out_specs=[SEM, SEM] + [HBM] * (2 * n) + [pl.BlockSpec(memory_space=pltpu.VMEM)],
        input_output_aliases={a: 2 + a for a in range(2 * n)},
        compiler_params=pltpu.CompilerParams(has_side_effects=EFFECT),
    )(*[_in_hbm(a) for a in arrs], *[_in_hbm(l) for l in lands])
    return tuple(res[:2]), list(res[2:2 + n]), list(res[2 + n:2 + 2 * n]), res[-1]


def swap_wait(sems, srcs, lands, after, name):
    n = len(srcs)

    def body(*refs):
        src, land = refs[:n], refs[n:2 * n]
        ssem, rsem = refs[2 * n:2 * n + 2]
        x, y, c = _axes()
        for a in range(n):
            cp = pltpu.make_async_remote_copy(src_ref=src[a], dst_ref=land[a], send_sem=ssem.at[a],
                                              recv_sem=rsem.at[a], device_id=(x, y, 1 - c), device_id_type=MESH)
            cp.wait_send()
            cp.wait_recv()

    res = pl.pallas_call(
        body, name=name, out_shape=[pltpu.HBM(a.shape, a.dtype) for a in srcs] * 2,
        in_specs=[HBM] * (2 * n) + [SEM, SEM, ANY], out_specs=[HBM] * (2 * n),
        input_output_aliases={a: a for a in range(2 * n)},
        compiler_params=pltpu.CompilerParams(has_side_effects=EFFECT),
    )(*srcs, *lands, *sems, after)
    return list(res[:n]), list(res[n:])


def _all_peers(x, y, c):
    out = []
    for k in range(1, N_DEV):
        px, py, pc = _flip(x, (k >> 2) & 1), _flip(y, (k >> 1) & 1), _flip(c, k & 1)
        out.append(((px, py, pc), 4 * px + 2 * py + pc))
    return out


def exchange_start(items, after, name):
    n = len(items)

    def body(*refs):
        src, land = refs[:n], refs[n:2 * n]
        ssem, rsem, lsem = refs[2 * n + 1:2 * n + 4]
        token = refs[-1]
        x, y, c = _axes()
        me = 4 * x + 2 * y + c
        for a, (_, scatter) in enumerate(items):
            pltpu.make_async_copy(src[a].at[me] if scatter else src[a], land[a].at[me], lsem.at[a]).start()
            for k, (peer, p) in enumerate(_all_peers(x, y, c)):
                pltpu.make_async_remote_copy(src_ref=src[a].at[p] if scatter else src[a], dst_ref=land[a].at[me],
                                             send_sem=ssem.at[7 * a + k], recv_sem=rsem.at[7 * a + k],
                                             device_id=peer, device_id_type=MESH).start()
        token[...] = jnp.zeros_like(token)

    srcs = [s for s, _ in items]
    lands = [lax.empty(s.shape if sc else (N_DEV,) + s.shape, s.dtype) for s, sc in items]
    out_shape = ([pltpu.SemaphoreType.DMA((7 * n,)), pltpu.SemaphoreType.DMA((7 * n,)), pltpu.SemaphoreType.DMA((n,))]
                 + [pltpu.HBM(s.shape, s.dtype) for s in srcs] + [pltpu.HBM(l.shape, l.dtype) for l in lands]
                 + [jax.ShapeDtypeStruct((8, 128), F32)])
    res = pl.pallas_call(
        body, name=name, out_shape=out_shape, in_specs=[HBM] * (2 * n) + [ANY],
        out_specs=[SEM, SEM, SEM] + [HBM] * (2 * n) + [pl.BlockSpec(memory_space=pltpu.VMEM)],
        input_output_aliases={a: 3 + a for a in range(2 * n)},
        compiler_params=pltpu.CompilerParams(has_side_effects=EFFECT),
    )(*[_in_hbm(s) for s in srcs], *[_in_hbm(l) for l in lands], after)
    return tuple(res[:3]), list(res[3:3 + n]), list(res[3 + n:3 + 2 * n]), res[-1]


def exchange_wait(sems, srcs, lands, scatter, after, name):
    n = len(srcs)

    def body(*refs):
        src, land = refs[:n], refs[n:2 * n]
        ssem, rsem, lsem = refs[2 * n:2 * n + 3]
        x, y, c = _axes()
        me = 4 * x + 2 * y + c
        for a in range(n):
            for k, (peer, p) in enumerate(_all_peers(x, y, c)):
                cp = pltpu.make_async_remote_copy(src_ref=src[a].at[p] if scatter[a] else src[a],
                                                  dst_ref=land[a].at[p], send_sem=ssem.at[7 * a + k],
                                                  recv_sem=rsem.at[7 * a + k], device_id=peer, device_id_type=MESH)
                cp.wait_send()
                cp.wait_recv()
            pltpu.make_async_copy(src[a].at[me] if scatter[a] else src[a], land[a].at[me], lsem.at[a]).wait()

    res = pl.pallas_call(
        body, name=name,
        out_shape=[pltpu.HBM(s.shape, s.dtype) for s in srcs] + [pltpu.HBM(l.shape, l.dtype) for l in lands],
        in_specs=[HBM] * (2 * n) + [SEM, SEM, SEM, ANY], out_specs=[HBM] * (2 * n),
        input_output_aliases={a: a for a in range(2 * n)},
        compiler_params=pltpu.CompilerParams(has_side_effects=EFFECT),
    )(*srcs, *lands, *sems, after)
    return list(res[n:])


def sum8(parts, name):
    _, P, C = parts.shape

    def body(p_ref, o_ref):
        tot = p_ref[0]
        for d in range(1, N_DEV):
            tot = tot + p_ref[d]
        o_ref[...] = tot

    return pl.pallas_call(body, name=name, out_shape=jax.ShapeDtypeStruct((P, C), F32),
                          compiler_params=pltpu.CompilerParams(vmem_limit_bytes=VMEM_LIMIT))(parts)


def mm_nn(a, w, out_dtype, name, res=None, gate=None):
    M, K = a.shape
    S, _, Ns = w.shape
    TM = _tile(M, (1024, 512, 256) if K <= 1024 else (512, 256))
    TN = _tile(Ns, (1408, 1024, 768, 512, 256, 128))
    nj = Ns // TN
    fused = res is not None

    def body(*refs):
        if fused:
            a_ref, w_ref, r_ref, g_ref, f_ref, o_ref = refs
        else:
            a_ref, w_ref, f_ref = refs
        f = jnp.dot(a_ref[...], w_ref[...], preferred_element_type=F32)
        f_ref[...] = f.astype(f_ref.dtype)
        if fused:
            o_ref[...] = r_ref[...] + g_ref[...] * f

    col = lambda s, j, i: (i, s * nj + j)
    in_specs = [pl.BlockSpec((TM, K), lambda s, j, i: (i, 0)), pl.BlockSpec((None, K, TN), lambda s, j, i: (s, 0, j))]
    out_specs = [pl.BlockSpec((TM, TN), col)]
    out_shape = [jax.ShapeDtypeStruct((M, S * Ns), out_dtype)]
    args = [a, w]
    if fused:
        in_specs += [pl.BlockSpec((TM, TN), col), pl.BlockSpec((1, TN), lambda s, j, i: (0, s * nj + j))]
        out_specs.append(pl.BlockSpec((TM, TN), col))
        out_shape.append(jax.ShapeDtypeStruct((M, S * Ns), F32))
        args += [res, gate]
    out = pl.pallas_call(body, name=name, grid=(S, nj, M // TM), in_specs=in_specs, out_specs=out_specs,
                         out_shape=out_shape, compiler_params=_cp("parallel", "parallel", "parallel"))(*args)
    return tuple(out) if fused else out[0]


def mm_nt(g, w, out_dtype, name):
    g3 = g if g.ndim == 3 else g[None]
    Q, M, F = g3.shape
    S, K, Ns = w.shape
    TM = _tile(M, (1024, 512, 256) if K <= 1024 else (512, 256))
    TN = _tile(Ns, (1408, 1024, 768, 512, 256, 128))
    nj = Ns // TN
    nred = S * nj
    per_part = F // TN

    def body(g_ref, w_ref, o_ref, acc):
        n = pl.program_id(1)

        @pl.when(n == 0)
        def _():
            acc[...] = jnp.zeros_like(acc)

        acc[...] += lax.dot_general(g_ref[...], w_ref[...], (((1,), (1,)), ((), ())), preferred_element_type=F32)

        @pl.when(n == nred - 1)
        def _():
            o_ref[...] = acc[...].astype(o_ref.dtype)

    return pl.pallas_call(
        body, name=name, grid=(M // TM, nred),
        in_specs=[pl.BlockSpec((None, TM, TN), lambda i, n: (n // per_part, i, n % per_part)),
                  pl.BlockSpec((None, K, TN), lambda i, n: (n // nj, 0, n % nj))],
        out_specs=pl.BlockSpec((TM, K), lambda i, n: (i, 0)),
        out_shape=jax.ShapeDtypeStruct((M, K), out_dtype),
        scratch_shapes=[pltpu.VMEM((TM, K), F32)],
        compiler_params=_cp("parallel", "arbitrary"))(g3, w)


def mm_tn(a, g, S, name):
    M, K = a.shape
    g3 = g if g.ndim == 3 else g[None]
    Q, _, F = g3.shape
    Ns = Q * F // S
    TN = _tile(Ns, (1408, 1024, 768, 512, 256, 128))
    TK = next(t for t in (1024, 512, 256, 128) if t <= K and K % t == 0
              and 4 * M * (t + TN) + 8 * t * TN <= VMEM_LIMIT * 3 // 4)
    nj = Ns // TN
    per_part = F // TN

    def body(a_ref, g_ref, o_ref):
        o_ref[...] = lax.dot_general(a_ref[...], g_ref[...], (((0,), (0,)), ((), ())),
                                     preferred_element_type=F32).astype(o_ref.dtype)

    return pl.pallas_call(
        body, name=name, grid=(S * nj, K // TK),
        in_specs=[pl.BlockSpec((M, TK), lambda n, k: (0, k)),
                  pl.BlockSpec((None, M, TN), lambda n, k: (n // per_part, 0, n % per_part))],
        out_specs=pl.BlockSpec((None, TK, TN), lambda n, k: (n // nj, k, n % nj)),
        out_shape=jax.ShapeDtypeStruct((S, K, Ns), BF),
        compiler_params=_cp("parallel", "parallel"))(a, g3)


ROW_TILE = (512, 256)


def _rows(TL, D):
    return pl.BlockSpec((TL, D), lambda i: (i, 0))


def _fixed(R, D):
    return pl.BlockSpec((R, D), lambda i: (0, 0))


def _rowsum8(v):
    T, D = v.shape
    return jnp.sum(v.reshape(T // 8, 8, D), axis=0)


def _norm_parts(xv):
    r = lax.rsqrt(jnp.mean(xv * xv, axis=-1, keepdims=True) + RMS_EPS)
    return xv * r, r


def norm_mod(x, gamma, mods, k_shift, out_dtype, name):
    L, D = x.shape
    TL = _tile(L, ROW_TILE)

    def body(x_ref, g_ref, m_ref, o_ref):
        xn, _ = _norm_parts(x_ref[...])
        sh, sc = m_ref[k_shift:k_shift + 1, :], m_ref[k_shift + 1:k_shift + 2, :]
        o_ref[...] = ((xn * g_ref[...]) * (1.0 + sc) + sh).astype(o_ref.dtype)

    return pl.pallas_call(body, name=name, grid=(L // TL,),
                          in_specs=[_rows(TL, D), _fixed(1, D), _fixed(6, D)], out_specs=_rows(TL, D),
                          out_shape=jax.ShapeDtypeStruct((L, D), out_dtype), compiler_params=_cp("parallel"))(x, gamma, mods)


def norm_bwd(dh, x, dres, gamma, mods, k_shift, name, branch=None):
    L, D = x.shape
    TL = _tile(L, ROW_TILE)
    nacc = 4 if branch else 3

    def body(*refs):
        if branch:
            dh_ref, x_ref, dr_ref, g_ref, m_ref, f_ref, fm_ref, dx_ref, s_ref, df_ref, acc = refs
        else:
            dh_ref, x_ref, dr_ref, g_ref, m_ref, dx_ref, s_ref, acc = refs
        i = pl.program_id(0)

        @pl.when(i == 0)
        def _():
            acc[...] = jnp.zeros_like(acc)

        xn, r = _norm_parts(x_ref[...])
        dh_v = dh_ref[...].astype(F32)
        gam = g_ref[...]
        sc = m_ref[k_shift + 1:k_shift + 2, :]
        dn = dh_v * (1.0 + sc)
        dxn = dn * gam
        dx = dr_ref[...] + r * (dxn - xn * jnp.mean(dxn * xn, axis=-1, keepdims=True))
        dx_ref[...] = dx
        acc[0] += _rowsum8(dh_v)
        acc[1] += _rowsum8(dh_v * (xn * gam))
        acc[2] += _rowsum8(dn * xn)
        if branch:
            df_ref[...] = (dx * fm_ref[branch[2]:branch[2] + 1, :]).astype(df_ref.dtype)
            acc[3] += _rowsum8(dx * f_ref[...].astype(F32))

        @pl.when(i == pl.num_programs(0) - 1)
        def _():
            s_ref[...] = jnp.zeros_like(s_ref)
            for q in range(nacc):
                s_ref[q:q + 1, :] = jnp.sum(acc[q], axis=0, keepdims=True)

    in_specs = [_rows(TL, D), _rows(TL, D), _rows(TL, D), _fixed(1, D), _fixed(6, D)]
    out_specs = [_rows(TL, D), _fixed(8, D)]
    out_shape = [jax.ShapeDtypeStruct((L, D), F32), jax.ShapeDtypeStruct((8, D), F32)]
    args = [dh, x, dres, gamma, mods]
    if branch:
        in_specs += [_rows(TL, D), _fixed(6, D)]
        out_specs.append(_rows(TL, D))
        out_shape.append(jax.ShapeDtypeStruct((L, D), BF))
        args += [branch[0], branch[1]]
    return pl.pallas_call(
        body, name=name, grid=(L // TL,), in_specs=in_specs, out_specs=out_specs, out_shape=out_shape,
        scratch_shapes=[pltpu.VMEM((nacc, 8, D), F32)], compiler_params=_cp("arbitrary"))(*args)


def ffn_in_act(a, w, name):
    M, K = a.shape
    S, _, Ns = w.shape
    half = S // 2
    TM = _tile(M, (512, 256))
    TN = _tile(Ns, (1408, 1024, 768, 512, 256, 128))
    nj = Ns // TN

    def body(a_ref, wg_ref, wu_ref, gu_ref, act_ref):
        av = a_ref[...]
        g = jnp.dot(av, wg_ref[...], preferred_element_type=F32)
        u = jnp.dot(av, wu_ref[...], preferred_element_type=F32)
        gu_ref[0] = g.astype(gu_ref.dtype)
        gu_ref[1] = u.astype(gu_ref.dtype)
        act_ref[...] = (g * _sigmoid(g) * u).astype(act_ref.dtype)

    return pl.pallas_call(
        body, name=name, grid=(half, nj, M // TM),
        in_specs=[pl.BlockSpec((TM, K), lambda s, j, i: (i, 0)),
                  pl.BlockSpec((None, K, TN), lambda s, j, i: (s, 0, j)),
                  pl.BlockSpec((None, K, TN), lambda s, j, i: (s + half, 0, j))],
        out_specs=[pl.BlockSpec((2, TM, TN), lambda s, j, i: (0, i, s * nj + j)),
                   pl.BlockSpec((TM, TN), lambda s, j, i: (i, s * nj + j))],
        out_shape=[jax.ShapeDtypeStruct((2, M, half * Ns), BF), jax.ShapeDtypeStruct((M, half * Ns), BF)],
        compiler_params=_cp("parallel", "parallel", "parallel"))(a, w, w)


def ffn_out_bwd(dff, w2, gu, name):
    M, D = dff.shape
    F = w2.shape[0]
    TM = _tile(M, (512, 256))
    CW = _tile(F, (256, 128))

    def body(d_ref, w_ref, gu_ref, o_ref):
        dv = d_ref[...]

        def product(c):
            return lax.dot_general(dv, w_ref[c:c + CW, :], (((1,), (1,)), ((), ())), preferred_element_type=F32)

        da = product(0)
        for c in range(0, F, CW):
            ahead = product(c + CW) if c + CW < F else None
            g = gu_ref[0, :, c:c + CW].astype(F32)
            u = gu_ref[1, :, c:c + CW].astype(F32)
            s = _sigmoid(g)
            o_ref[0, :, c:c + CW] = (da * u * (s + g * s * (1.0 - s))).astype(o_ref.dtype)
            o_ref[1, :, c:c + CW] = (da * g * s).astype(o_ref.dtype)
            da = ahead

    part = pl.BlockSpec((2, TM, F), lambda i: (0, i, 0))
    return pl.pallas_call(
        body, name=name, grid=(M // TM,),
        in_specs=[pl.BlockSpec((TM, D), lambda i: (i, 0)), pl.BlockSpec((F, D), lambda i: (0, 0)), part],
        out_specs=part, out_shape=jax.ShapeDtypeStruct((2, M, F), BF),
        compiler_params=_cp("parallel"))(dff, w2, gu)


def ssm_out_glu(z, w, x, mods, k_gate, name):
    M, K = z.shape
    S, _, Ns = w.shape
    half = S // 2
    TM = _tile(M, (1024, 512, 256))
    TN = _tile(Ns, (512, 256, 128))
    nj = Ns // TN

    def body(z_ref, wv_ref, wg_ref, x_ref, m_ref, o_ref, mix_ref, y_ref):
        zv = z_ref[...]
        CW = _tile(TN, (256, 128))

        def products(c):
            return (jnp.dot(zv, wv_ref[:, c:c + CW], preferred_element_type=F32),
                    jnp.dot(zv, wg_ref[:, c:c + CW], preferred_element_type=F32))

        cur = products(0)
        for c in range(0, TN, CW):
            ahead = products(c + CW) if c + CW < TN else None
            val, gate = cur
            o_ref[0, :, c:c + CW] = val.astype(o_ref.dtype)
            o_ref[1, :, c:c + CW] = gate.astype(o_ref.dtype)
            mix = val * _sigmoid(gate)
            mix_ref[:, c:c + CW] = mix.astype(mix_ref.dtype)
            y_ref[:, c:c + CW] = x_ref[:, c:c + CW] + m_ref[k_gate:k_gate + 1, c:c + CW] * mix
            cur = ahead

    col = lambda s, j, i: (i, s * nj + j)
    return pl.pallas_call(
        body, name=name, grid=(half, nj, M // TM),
        in_specs=[pl.BlockSpec((TM, K), lambda s, j, i: (i, 0)),
                  pl.BlockSpec((None, K, TN), lambda s, j, i: (s, 0, j)),
                  pl.BlockSpec((None, K, TN), lambda s, j, i: (s + half, 0, j)),
                  pl.BlockSpec((TM, TN), col), pl.BlockSpec((6, TN), lambda s, j, i: (0, s * nj + j))],
        out_specs=[pl.BlockSpec((2, TM, TN), lambda s, j, i: (0, i, s * nj + j)), pl.BlockSpec((TM, TN), col),
                   pl.BlockSpec((TM, TN), col)],
        out_shape=[jax.ShapeDtypeStruct((2, M, half * Ns), BF), jax.ShapeDtypeStruct((M, half * Ns), BF),
                   jax.ShapeDtypeStruct((M, half * Ns), F32)],
        compiler_params=_cp("parallel", "parallel", "parallel"))(z, w, w, x, mods)


def glu_bwd(dmix, o, name):
    _, L, D = o.shape
    TL = _tile(L, ROW_TILE)

    def body(d_ref, o_ref, do_ref):
        d = d_ref[...].astype(F32)
        val = o_ref[0].astype(F32)
        s = _sigmoid(o_ref[1].astype(F32))
        do_ref[0] = (d * s).astype(do_ref.dtype)
        do_ref[1] = (d * val * s * (1.0 - s)).astype(do_ref.dtype)

    part = pl.BlockSpec((2, TL, D), lambda i: (0, i, 0))
    return pl.pallas_call(body, name=name, grid=(L // TL,), in_specs=[_rows(TL, D), part],
                          out_specs=part, out_shape=jax.ShapeDtypeStruct((2, L, D), BF),
                          compiler_params=_cp("parallel"))(dmix, o)


def final_loss(x, target, gamma, f, fmods, k_gate, name):
    L, D = x.shape
    TL = _tile(L, ROW_TILE)

    def body(x_ref, t_ref, g_ref, f_ref, fm_ref, l_ref, dx_ref, s_ref, df_ref, acc, lacc):
        i = pl.program_id(0)

        @pl.when(i == 0)
        def _():
            acc[...] = jnp.zeros_like(acc)
            lacc[...] = jnp.zeros_like(lacc)

        xn, r = _norm_parts(x_ref[...])
        gam = g_ref[...]
        e = xn * gam - t_ref[...]
        lacc[...] += jnp.sum(0.5 * jnp.mean(e * e, axis=-1, keepdims=True), axis=0, keepdims=True)
        dy = e * (1.0 / D)
        dxn = dy * gam
        dx = r * (dxn - xn * jnp.mean(dxn * xn, axis=-1, keepdims=True))
        dx_ref[...] = dx
        df_ref[...] = (dx * fm_ref[k_gate:k_gate + 1, :]).astype(df_ref.dtype)
        acc[0] += _rowsum8(dy * xn)
        acc[1] += _rowsum8(dx * f_ref[...].astype(F32))

        @pl.when(i == pl.num_programs(0) - 1)
        def _():
            s_ref[...] = jnp.zeros_like(s_ref)
            for q in range(2):
                s_ref[q:q + 1, :] = jnp.sum(acc[q], axis=0, keepdims=True)
            l_ref[...] = jnp.broadcast_to(lacc[...], l_ref.shape)

    return pl.pallas_call(
        body, name=name, grid=(L // TL,),
        in_specs=[_rows(TL, D), _rows(TL, D), _fixed(1, D), _rows(TL, D), _fixed(6, D)],
        out_specs=[_fixed(8, 128), _rows(TL, D), _fixed(8, D), _rows(TL, D)],
        out_shape=[jax.ShapeDtypeStruct((8, 128), F32), jax.ShapeDtypeStruct((L, D), F32),
                   jax.ShapeDtypeStruct((8, D), F32), jax.ShapeDtypeStruct((L, D), BF)],
        scratch_shapes=[pltpu.VMEM((2, 8, D), F32), pltpu.VMEM((1, 1), F32)],
        compiler_params=_cp("arbitrary"))(x, target, gamma, f, fmods)


def _col(L, TC, off):
    return pl.BlockSpec((L, TC), lambda j: (0, off + j))


def _shift_down(v, k, row):
    return jnp.where(row >= k, pltpu.roll(v, k, 0), 0.0)


def _shift_up(v, k, row, L):
    return jnp.where(row < L - k, pltpu.roll(v, L - k, 0), 0.0)


def conv_fwd(p, w, name):
    L, D3 = p.shape
    D = D3 // 3
    TC = _tile(D, (128,))
    nc = D // TC

    def body(b_ref, c_ref, v_ref, w_ref, o_ref):
        row = lax.broadcasted_iota(jnp.int32, (L, TC), 0)
        cv = c_ref[...].astype(F32) * v_ref[...].astype(F32)
        conv = w_ref[2:3, :] * cv + w_ref[1:2, :] * _shift_down(cv, 1, row) + w_ref[0:1, :] * _shift_down(cv, 2, row)
        o_ref[...] = (b_ref[...].astype(F32) * conv).astype(o_ref.dtype)

    return pl.pallas_call(
        body, name=name, grid=(nc,),
        in_specs=[_col(L, TC, 0), _col(L, TC, nc), _col(L, TC, 2 * nc), pl.BlockSpec((3, TC), lambda j: (0, j))],
        out_specs=_col(L, TC, 0), out_shape=jax.ShapeDtypeStruct((L, D), BF), compiler_params=_cp("parallel"))(p, p, p, w)


def conv_bwd(dm, p, w, name):
    L, D3 = p.shape
    D = D3 // 3
    TC = _tile(D, (128,))
    nc = D // TC

    def body(dm_ref, b_ref, c_ref, v_ref, w_ref, db_ref, dc_ref, dv_ref, dw_ref):
        row = lax.broadcasted_iota(jnp.int32, (L, TC), 0)
        cg, vv = c_ref[...].astype(F32), v_ref[...].astype(F32)
        cv = cg * vv
        cv1, cv2 = _shift_down(cv, 1, row), _shift_down(cv, 2, row)
        conv = w_ref[2:3, :] * cv + w_ref[1:2, :] * cv1 + w_ref[0:1, :] * cv2
        dmv = dm_ref[...].astype(F32)
        db_ref[...] = (dmv * conv).astype(db_ref.dtype)
        dconv = dmv * b_ref[...].astype(F32)
        dcv = (w_ref[2:3, :] * dconv + w_ref[1:2, :] * _shift_up(dconv, 1, row, L)
               + w_ref[0:1, :] * _shift_up(dconv, 2, row, L))
        dc_ref[...] = (dcv * vv).astype(dc_ref.dtype)
        dv_ref[...] = (dcv * cg).astype(dv_ref.dtype)
        dw_ref[...] = jnp.zeros_like(dw_ref)
        dw_ref[0:1, :] = jnp.sum(dconv * cv2, axis=0, keepdims=True)
        dw_ref[1:2, :] = jnp.sum(dconv * cv1, axis=0, keepdims=True)
        dw_ref[2:3, :] = jnp.sum(dconv * cv, axis=0, keepdims=True)

    one = jax.ShapeDtypeStruct((L, D), BF)
    return pl.pallas_call(
        body, name=name, grid=(nc,),
        in_specs=[_col(L, TC, 0), _col(L, TC, 0), _col(L, TC, nc), _col(L, TC, 2 * nc),
                  pl.BlockSpec((3, TC), lambda j: (0, j))],
        out_specs=[_col(L, TC, 0), _col(L, TC, 0), _col(L, TC, 0), pl.BlockSpec((8, TC), lambda j: (0, j))],
        out_shape=[one, one, one, jax.ShapeDtypeStruct((8, D), F32)],
        compiler_params=_cp("parallel"))(dm, p, p, p, w)


def _gelu(y):
    return 0.5 * y * (1.0 + jnp.tanh(GELU_C * (y + GELU_A * y * y * y)))


def _gelu_grad(y):
    th = jnp.tanh(GELU_C * (y + GELU_A * y * y * y))
    return 0.5 * (1.0 + th) + 0.5 * y * (1.0 - th * th) * GELU_C * (1.0 + 3.0 * GELU_A * y * y)


def _cmul_add(br, bi, ar, ai, sr, si):
    return br + ar * sr - ai * si, bi + ar * si + ai * sr


def _log2(n):
    k = n.bit_length() - 1
    assert 1 << k == n
    return k


def _replicate(P2, W2, P, GLP, transposed):
    shape = (W2, P2) if transposed else (P2, W2)
    k = lax.broadcasted_iota(jnp.int32, shape, 1 if transposed else 0)
    c = lax.broadcasted_iota(jnp.int32, shape, 0 if transposed else 1)
    return ((k >> _log2(P)) == (c >> _log2(GLP))) & ((k & (P - 1)) == (c & (P - 1)))


def _on_diagonal(KB, W2, H, P, GLP, transposed):
    shape = (W2, KB) if transposed else (KB, W2)
    r = lax.broadcasted_iota(jnp.int32, shape, 1 if transposed else 0)
    c = lax.broadcasted_iota(jnp.int32, shape, 0 if transposed else 1)
    return (r >> _log2(H)) == ((c & (GLP - 1)) >> _log2(P))


def _expand(t, dims, transposed):
    KB, W2, H, P, GLP = dims
    rep = _replicate(2 * P, W2, P, GLP, transposed).astype(t.dtype)
    wide = jnp.dot(rep, t, preferred_element_type=F32) if transposed else jnp.dot(t, rep, preferred_element_type=F32)
    return jnp.where(_on_diagonal(KB, W2, H, P, GLP, transposed), wide, 0.0).astype(t.dtype)


def _extract(acc, dims):
    KB, W2, H, P, GLP = dims
    rep = _replicate(2 * P, W2, P, GLP, True).astype(BF)
    kept = jnp.where(_on_diagonal(KB, W2, H, P, GLP, False), acc, 0.0)
    hi = kept.astype(BF)
    lo = (kept - hi.astype(F32)).astype(BF)
    return jnp.dot(hi, rep, preferred_element_type=F32) + jnp.dot(lo, rep, preferred_element_type=F32)


def _cmul(ar, ai, sr, si):
    return ar * sr - ai * si, ar * si + ai * sr


def _chunk_order(TL, CH, transposed):
    out_row = lax.broadcasted_iota(jnp.int32, (TL, TL), 1 if transposed else 0)
    in_row = lax.broadcasted_iota(jnp.int32, (TL, TL), 0 if transposed else 1)
    return in_row == ((out_row & 7) << _log2(CH)) + (out_row >> 3)


def _reorder(perm, v):
    hi = v.astype(perm.dtype)
    lo = (v - hi.astype(F32)).astype(perm.dtype)
    return jnp.dot(perm, hi, preferred_element_type=F32) + jnp.dot(perm, lo, preferred_element_type=F32)


def _interleave(main, side):
    n, m, k = len(main), len(side), 0
    for i, step in enumerate(main):
        step()
        while k < m and (k + 1) * n <= (i + 1) * m:
            side[k]()
            k += 1
    for step in side[k:]:
        step()


S5_CHUNK = 512


def s5_fwd(h, tb, tct, pw, dvec, name):
    L, D = h.shape
    nkb, KB, P2 = tb.shape
    P = P2 // 2
    W = (KB // SSM_GROUP) * P
    W2 = 2 * W
    dims = (KB, W2, SSM_GROUP, P, W)
    TL = _tile(L, (512, 256))
    CH = TL // 8
    NB = 2 if nkb % 2 == 0 else 1
    CK = min(S5_CHUNK, W2)

    def body(h_ref, tb_ref, tct_ref, pw_ref, d_ref, s_ref, y_ref, z_ref, bw, cw, perm, unperm, carry):
        t = pl.program_id(1)

        @pl.when(t == 0)
        def _():
            carry[...] = jnp.zeros_like(carry)
            for b in range(NB):
                bw[b] = _expand(tb_ref[b], dims, False)
                cw[b] = _expand(tct_ref[b], dims, True)
            perm[...] = _chunk_order(TL, CH, False).astype(perm.dtype)
            unperm[...] = _chunk_order(TL, CH, True).astype(perm.dtype)

        hp = _reorder(perm[...], h_ref[...])
        hpb = hp.astype(BF)
        first = lax.broadcasted_iota(jnp.int32, (8, W), 0) == 0

        def project(b):
            def chunk(c):
                def emit():
                    s_ref[:, b * W2 + c:b * W2 + c + CK] = jnp.dot(hpb[:, b * KB:(b + 1) * KB], bw[b, :, c:c + CK],
                                                                   preferred_element_type=F32)
                return emit
            return [chunk(c) for c in range(0, W2, CK)]

        def scan(b):
            re, im = slice(b * W2, b * W2 + W), slice(b * W2 + W, (b + 1) * W2)
            ar, ai = pw_ref[b, 0:8, :W], pw_ref[b, 0:8, W:]
            st = {"x": (jnp.zeros((8, W), F32), jnp.zeros((8, W), F32))}

            def own(j):
                def emit():
                    rows = slice(j * 8, j * 8 + 8)
                    xr, xi = _cmul_add(s_ref[rows, re], s_ref[rows, im], ar, ai, *st["x"])
                    s_ref[rows, re] = xr
                    s_ref[rows, im] = xi
                    st["x"] = (xr, xi)
                return emit

            def ends():
                xr, xi = st["x"]
                for k, off in ((1, 8), (2, 16), (4, 24)):
                    xr, xi = _cmul_add(xr, xi, pw_ref[b, off:off + 8, :W], pw_ref[b, off:off + 8, W:],
                                       pltpu.roll(xr, k, 0), pltpu.roll(xi, k, 0))
                xr, xi = _cmul_add(xr, xi, pw_ref[b, 32:40, :W], pw_ref[b, 32:40, W:], carry[b, 0], carry[b, 1])
                st["c"] = (jnp.where(first, carry[b, 0], pltpu.roll(xr, 1, 0)),
                           jnp.where(first, carry[b, 1], pltpu.roll(xi, 1, 0)))
                carry[b, 0] = jnp.broadcast_to(xr[7:8], (8, W))
                carry[b, 1] = jnp.broadcast_to(xi[7:8], (8, W))

            def carried(j):
                def emit():
                    rows = slice(j * 8, j * 8 + 8)
                    cr, ci = _cmul(ar, ai, *st["c"])
                    s_ref[rows, re] = s_ref[rows, re] + cr
                    s_ref[rows, im] = s_ref[rows, im] + ci
                    st["c"] = (cr, ci)
                return emit

            return [own(j) for j in range(CH)] + [ends] + [carried(j) for j in range(CH)]

        def readout(b):
            cols = slice(b * KB, (b + 1) * KB)
            acc = {}

            def chunk(c):
                def emit():
                    part = jnp.dot(s_ref[:, b * W2 + c:b * W2 + c + CK].astype(BF), cw[b, c:c + CK, :],
                                   preferred_element_type=F32)
                    acc["y"] = part if c == 0 else acc["y"] + part
                return emit

            def finish():
                y = acc["y"] + d_ref[:, cols] * hp[:, cols]
                y_ref[:, cols] = y
                z_ref[:, cols] = jnp.dot(unperm[...], _gelu(y).astype(BF),
                                         preferred_element_type=F32).astype(z_ref.dtype)

            return [chunk(c) for c in range(0, W2, CK)] + [finish]

        for emit in project(0):
            emit()
        for b in range(NB):
            side = (project(b + 1) if b + 1 < NB else []) + (readout(b - 1) if b > 0 else [])
            _interleave(scan(b), side)
        for emit in readout(NB - 1):
            emit()

    blk = lambda kb, t: (t, kb)
    per_kb = lambda kb, t: (kb, 0, 0)
    return pl.pallas_call(
        body, name=name, grid=(nkb // NB, L // TL),
        in_specs=[pl.BlockSpec((TL, NB * KB), blk), pl.BlockSpec((NB, KB, P2), per_kb),
                  pl.BlockSpec((NB, P2, KB), per_kb), pl.BlockSpec((NB, 40, W2), per_kb),
                  pl.BlockSpec((1, NB * KB), lambda kb, t: (0, kb))],
        out_specs=[pl.BlockSpec((TL, NB * W2), blk), pl.BlockSpec((TL, NB * KB), blk),
                   pl.BlockSpec((TL, NB * KB), blk)],
        out_shape=[jax.ShapeDtypeStruct((L, nkb * W2), F32), jax.ShapeDtypeStruct((L, D), F32),
                   jax.ShapeDtypeStruct((L, D), BF)],
        scratch_shapes=[pltpu.VMEM((NB, KB, W2), BF), pltpu.VMEM((NB, W2, KB), BF), pltpu.VMEM((TL, TL), BF),
                        pltpu.VMEM((TL, TL), BF), pltpu.VMEM((NB, 2, 8, W), F32)],
        compiler_params=_cp("parallel", "arbitrary"))(h, tb, tct, pw, dvec)


def s5_bwd(dz, y, h, s, tc, tbt, pwr, dvec, name):
    L, D = h.shape
    nkb, KB, P2 = tc.shape
    P = P2 // 2
    W = (KB // SSM_GROUP) * P
    W2 = 2 * W
    dims = (KB, W2, SSM_GROUP, P, W)
    TL = _tile(L, (512, 256))
    CH = TL // 8
    nt = L // TL
    NB = 2 if nkb % 2 == 0 else 1
    CK = min(S5_CHUNK, W2)
    tn = (((0,), (0,)), ((), ()))

    def body(dz_ref, y_ref, h_ref, s_ref, sp_ref, tc_ref, tbt_ref, pw_ref, d_ref,
             dh_ref, dd_ref, da_ref, db_ref, dc_ref, g, ctw, btw, dbacc, dcacc, dys, perm, unperm, carry):
        t = pl.program_id(1)

        @pl.when(t == 0)
        def _():
            carry[...] = jnp.zeros_like(carry)
            dd_ref[...] = jnp.zeros_like(dd_ref)
            da_ref[...] = jnp.zeros_like(da_ref)
            dbacc[...] = jnp.zeros_like(dbacc)
            dcacc[...] = jnp.zeros_like(dcacc)
            for b in range(NB):
                ctw[b] = _expand(tc_ref[b], dims, False)
                btw[b] = _expand(tbt_ref[b], dims, True)
            perm[...] = _chunk_order(TL, CH, False).astype(perm.dtype)
            unperm[...] = _chunk_order(TL, CH, True).astype(perm.dtype)

        hp = jnp.dot(perm[...], h_ref[...].astype(BF), preferred_element_type=F32)
        dy = jnp.dot(perm[...], dz_ref[...].astype(BF), preferred_element_type=F32) * _gelu_grad(y_ref[...])
        dd_ref[...] += _rowsum8(dy * hp)
        dys[...] = dy
        dyb = dy.astype(BF)
        hpb = hp.astype(BF)
        sub = lax.broadcasted_iota(jnp.int32, (8, W), 0)
        live = jnp.where(t == nt - 1, 0.0, 1.0)

        def lead(b):
            cols = slice(b * KB, (b + 1) * KB)

            def to_states(c):
                def emit():
                    g[b, :, c:c + CK] = jnp.dot(dyb[:, cols], ctw[b, :, c:c + CK], preferred_element_type=F32)
                return emit

            def d_c(c):
                def emit():
                    dcacc[b, :, c:c + CK] += lax.dot_general(dyb[:, cols],
                                                             s_ref[:, b * W2 + c:b * W2 + c + CK].astype(BF), tn,
                                                             preferred_element_type=F32)
                return emit

            return [f(c) for c in range(0, W2, CK) for f in (to_states, d_c)]

        def scan(b):
            re, im = slice(b * W2, b * W2 + W), slice(b * W2 + W, (b + 1) * W2)
            ar, ai = pw_ref[b, 0:8, :W], pw_ref[b, 0:8, W:]
            zero = jnp.zeros((8, W), F32)
            st = {"g": (zero, zero), "acc": (zero, zero)}

            def own(j):
                def emit():
                    rows = slice(j * 8, j * 8 + 8)
                    gr, gi = _cmul_add(g[b, rows, :W], g[b, rows, W:], ar, ai, *st["g"])
                    g[b, rows, :W] = gr
                    g[b, rows, W:] = gi
                    st["g"] = (gr, gi)
                return emit

            def ends():
                gr, gi = st["g"]
                for k, off in ((1, 8), (2, 16), (4, 24)):
                    gr, gi = _cmul_add(gr, gi, pw_ref[b, off:off + 8, :W], pw_ref[b, off:off + 8, W:],
                                       pltpu.roll(gr, 8 - k, 0), pltpu.roll(gi, 8 - k, 0))
                gr, gi = _cmul_add(gr, gi, pw_ref[b, 32:40, :W], pw_ref[b, 32:40, W:], carry[b, 0], carry[b, 1])
                st["c"] = (jnp.where(sub == 7, carry[b, 0], pltpu.roll(gr, 7, 0)),
                           jnp.where(sub == 7, carry[b, 1], pltpu.roll(gi, 7, 0)))
                carry[b, 0] = jnp.broadcast_to(gr[0:1], (8, W))
                carry[b, 1] = jnp.broadcast_to(gi[0:1], (8, W))

            def carried(j):
                def emit():
                    rows = slice(j * 8, j * 8 + 8)
                    cr, ci = _cmul(ar, ai, *st["c"])
                    gr, gi = g[b, rows, :W] + cr, g[b, rows, W:] + ci
                    g[b, rows, :W] = gr
                    g[b, rows, W:] = gi
                    if j > 0:
                        before = slice(j * 8 - 8, j * 8)
                        pr, pi = s_ref[before, re], s_ref[before, im]
                    else:
                        last = slice(TL - 8, TL)
                        pr = jnp.where(sub == 0, sp_ref[7:8, re] * live, pltpu.roll(s_ref[last, re], 1, 0))
                        pi = jnp.where(sub == 0, sp_ref[7:8, im] * live, pltpu.roll(s_ref[last, im], 1, 0))
                    accr, acci = st["acc"]
                    st["c"] = (cr, ci)
                    st["acc"] = (accr + pr * gr + pi * gi, acci + pr * gi - pi * gr)
                return emit

            def done():
                da_ref[b, :, :W] += st["acc"][0]
                da_ref[b, :, W:] += st["acc"][1]

            return ([own(j) for j in reversed(range(CH))] + [ends] + [carried(j) for j in reversed(range(CH))]
                    + [done])

        def tail(b):
            cols = slice(b * KB, (b + 1) * KB)
            acc = {}

            def d_u(c):
                def emit():
                    part = jnp.dot(g[b, :, c:c + CK].astype(BF), btw[b, c:c + CK, :], preferred_element_type=F32)
                    acc["u"] = part if c == 0 else acc["u"] + part
                return emit

            def d_b(c):
                def emit():
                    dbacc[b, :, c:c + CK] += lax.dot_general(hpb[:, cols], g[b, :, c:c + CK].astype(BF), tn,
                                                             preferred_element_type=F32)
                return emit

            def finish():
                dh = (dys[:, cols] * d_ref[:, cols] + acc["u"]).astype(BF)
                dh_ref[:, cols] = jnp.dot(unperm[...], dh, preferred_element_type=F32).astype(dh_ref.dtype)

            return [f(c) for c in range(0, W2, CK) for f in (d_u, d_b)] + [finish]

        for emit in lead(0):
            emit()
        for b in range(NB):
            side = (lead(b + 1) if b + 1 < NB else []) + (tail(b - 1) if b > 0 else [])
            _interleave(scan(b), side)
        for emit in tail(NB - 1):
            emit()

        @pl.when(t == nt - 1)
        def _():
            for b in range(NB):
                db_ref[b] = _extract(dbacc[b], dims)
                dc_ref[b] = _extract(dcacc[b], dims)

    rev = lambda kb, t: (nt - 1 - t, kb)
    prev = lambda kb, t: (jnp.maximum((nt - 1 - t) * CH - 1, 0), kb)
    per_kb = lambda kb, t: (kb, 0, 0)
    return pl.pallas_call(
        body, name=name, grid=(nkb // NB, nt),
        in_specs=[pl.BlockSpec((TL, NB * KB), rev), pl.BlockSpec((TL, NB * KB), rev),
                  pl.BlockSpec((TL, NB * KB), rev), pl.BlockSpec((TL, NB * W2), rev),
                  pl.BlockSpec((8, NB * W2), prev), pl.BlockSpec((NB, KB, P2), per_kb),
                  pl.BlockSpec((NB, P2, KB), per_kb), pl.BlockSpec((NB, 40, W2), per_kb),
                  pl.BlockSpec((1, NB * KB), lambda kb, t: (0, kb))],
        out_specs=[pl.BlockSpec((TL, NB * KB), rev), pl.BlockSpec((8, NB * KB), lambda kb, t: (0, kb)),
                   pl.BlockSpec((NB, 8, W2), per_kb), pl.BlockSpec((NB, KB, P2), per_kb),
                   pl.BlockSpec((NB, KB, P2), per_kb)],
        out_shape=[jax.ShapeDtypeStruct((L, D), BF), jax.ShapeDtypeStruct((8, D), F32),
                   jax.ShapeDtypeStruct((nkb, 8, W2), F32), jax.ShapeDtypeStruct((nkb, KB, P2), F32),
                   jax.ShapeDtypeStruct((nkb, KB, P2), F32)],
        scratch_shapes=[pltpu.VMEM((NB, TL, W2), F32), pltpu.VMEM((NB, KB, W2), BF), pltpu.VMEM((NB, W2, KB), BF),
                        pltpu.VMEM((NB, KB, W2), F32), pltpu.VMEM((NB, KB, W2), F32), pltpu.VMEM((TL, NB * KB), F32),
                        pltpu.VMEM((TL, TL), BF), pltpu.VMEM((TL, TL), BF), pltpu.VMEM((NB, 2, 8, W), F32)],
        compiler_params=pltpu.CompilerParams(dimension_semantics=("parallel", "arbitrary"),
                                             vmem_limit_bytes=V7X_VMEM_BYTES - 4 * 1024 * 1024),
    )(dz, y, h, s, s, tc, tbt, pwr, dvec)


def _discretise(a_re, a_im, log_step, b_re, b_im):
    lr = jnp.minimum(a_re, -1e-4)
    li = a_im
    dt = jnp.exp(log_step)[:, None]
    mag = jnp.exp(lr * dt)
    abr = mag * jnp.cos(li * dt)
    abi = mag * jnp.sin(li * dt)
    den = lr * lr + li * li
    qr = ((abr - 1.0) * lr + abi * li) / den
    qi = (abi * lr - (abr - 1.0) * li) / den
    bbar_re = qr[..., None] * b_re - qi[..., None] * b_im
    bbar_im = qr[..., None] * b_im + qi[..., None] * b_re
    return abr, abi, bbar_re, bbar_im


def _compact(m_re, m_im, nkb):
    G, H, P = m_re.shape
    t = jnp.stack([m_re, m_im], axis=2).reshape(nkb, (G // nkb) * H, 2 * P).astype(BF)
    return t, jnp.swapaxes(t, 1, 2)


def _scan_powers(abr, abi, nkb, conj, CH):
    G, P = abr.shape
    if conj:
        abi = -abi

    def cmul(u, v):
        return u[0] * v[0] - u[1] * v[1], u[0] * v[1] + u[1] * v[0]

    q = (abr, abi)
    for _ in range(_log2(CH)):
        q = cmul(q, q)
    pows = [q]
    for _ in range(7):
        pows.append(cmul(pows[-1], q))
    row = jnp.arange(8)[:, None, None]

    def table(part):
        out = [jnp.broadcast_to((abr, abi)[part][None], (8, G, P))]
        for k in (1, 2, 4):
            keep = (row <= 7 - k) if conj else (row >= k)
            out.append(jnp.where(keep, pows[k - 1][part][None], 0.0))
        ends = jnp.stack([p[part] for p in pows])
        out.append(ends[::-1] if conj else ends)
        return jnp.concatenate(out, axis=0)

    GL = G // nkb
    t = jnp.stack([table(0), table(1)], axis=1)
    t = t.reshape(40, 2, nkb, GL * P).transpose(2, 0, 1, 3)
    return t.reshape(nkb, 40, 2 * GL * P)


def ada_mods(c_all, w_ada, b_sh, name):
    nl, D, NA = w_ada.shape

    def body(c_ref, w_ref, b_ref, o_ref):
        cv = c_ref[...]
        act = cv * _sigmoid(cv)
        o_ref[...] = jnp.dot(act, w_ref[...], preferred_element_type=F32, precision=lax.Precision.HIGHEST) + b_ref[...]

    return pl.pallas_call(
        body, name=name, grid=(nl,),
        in_specs=[pl.BlockSpec((8, D), lambda i: (0, 0)), pl.BlockSpec((None, D, NA), lambda i: (i, 0, 0)),
                  pl.BlockSpec((None, 1, NA), lambda i: (i, 0, 0))],
        out_specs=pl.BlockSpec((None, 8, NA), lambda i: (i, 0, 0)),
        out_shape=jax.ShapeDtypeStruct((nl, 8, NA), F32), compiler_params=_cp("parallel"))(c_all, w_ada, b_sh)


def _adamw(w, g, m, v):
    m = ADAM_B1 * m + (1.0 - ADAM_B1) * g
    v = ADAM_B2 * v + (1.0 - ADAM_B2) * (g * g)
    m_hat = m / (1.0 - ADAM_B1 ** ADAM_STEP)
    v_hat = v / (1.0 - ADAM_B2 ** ADAM_STEP)
    return -ADAM_LR * (m_hat / (jnp.sqrt(v_hat) + ADAM_EPS) + ADAM_WD * w), m, v


def _adam_rows(R, C):
    cap = max(8, (256 * 1024) // C)
    for t in range(min(R, cap), 0, -1):
        if R % t == 0 and (t % 8 == 0 or t == R):
            return t
    return R


def adamw_ada(c_t, dm, w, m, v, name):
    nl, D, NA = w.shape
    TK = _tile(D, (256, 128))

    def body(c_ref, dm_ref, w_ref, m_ref, v_ref, g_ref, d_ref, nm_ref, nv_ref):
        cv = c_ref[...]
        act = cv * _sigmoid(cv)
        g = jnp.dot(act, dm_ref[...], preferred_element_type=F32, precision=lax.Precision.HIGHEST)
        g_ref[...] = g
        d_ref[...], nm_ref[...], nv_ref[...] = _adamw(w_ref[...], g, m_ref[...], v_ref[...])

    big = pl.BlockSpec((None, TK, NA), lambda i, k: (i, k, 0))
    shape = jax.ShapeDtypeStruct(w.shape, F32)
    return pl.pallas_call(
        body, name=name, grid=(nl, D // TK),
        in_specs=[pl.BlockSpec((TK, 8), lambda i, k: (k, 0)), pl.BlockSpec((None, 8, NA), lambda i, k: (i, 0, 0)),
                  big, big, big],
        out_specs=[big] * 4, out_shape=[shape] * 4, compiler_params=_cp("parallel", "parallel"))(c_t, dm, w, m, v)


def adamw_sharded(w, m, v, ga, gb, name):
    nl, R, C = w.shape
    TR = _adam_rows(R, C)

    def body(w_ref, m_ref, v_ref, a_ref, b_ref, g_ref, d_ref, nm_ref, nv_ref):
        g = a_ref[...] + b_ref[...]
        g_ref[...] = g
        d_ref[...], nm_ref[...], nv_ref[...] = _adamw(w_ref[...], g, m_ref[...], v_ref[...])

    big = pl.BlockSpec((None, TR, C), lambda i, r: (i, r, 0))
    shape = jax.ShapeDtypeStruct(w.shape, F32)
    return pl.pallas_call(
        body, name=name, grid=(nl, R // TR), in_specs=[big] * 5,
        out_specs=[big] * 4, out_shape=[shape] * 4, compiler_params=_cp("parallel", "parallel"))(w, m, v, ga, gb)


def adamw_slab(g, w, m, v, name):
    R, C = g.shape
    TR = _tile(R, (160, 80, 40, 8))

    def body(g_ref, w_ref, m_ref, v_ref, d_ref, nm_ref, nv_ref):
        d_ref[...], nm_ref[...], nv_ref[...] = _adamw(w_ref[...], g_ref[...], m_ref[...], v_ref[...])

    big = pl.BlockSpec((TR, C), lambda r: (r, 0))
    shape = jax.ShapeDtypeStruct((R, C), F32)
    return pl.pallas_call(
        body, name=name, grid=(R // TR,), in_specs=[big] * 4,
        out_specs=[big] * 3, out_shape=[shape] * 3, compiler_params=_cp("parallel"))(g, w, m, v)


def adamw_plain(w, m, v, g, name):
    def body(w_ref, m_ref, v_ref, g_ref, d_ref, nm_ref, nv_ref):
        d_ref[...], nm_ref[...], nv_ref[...] = _adamw(w_ref[...], g_ref[...], m_ref[...], v_ref[...])

    shape = jax.ShapeDtypeStruct(w.shape, F32)
    return pl.pallas_call(body, name=name, out_shape=[shape] * 3,
                          compiler_params=pltpu.CompilerParams(vmem_limit_bytes=VMEM_LIMIT))(w, m, v, g)


def _slab_rows(a):
    n = a.size
    rows = -(-n // SLAB_W)
    return -(-rows // 8) * 8


def _pack(arrs, pad_rows_to=0):
    out = []
    for a in arrs:
        rows = _slab_rows(a)
        flat = a.reshape(-1).astype(F32)
        flat = jnp.pad(flat, (0, rows * SLAB_W - flat.shape[0]))
        out.append(flat.reshape(rows, SLAB_W))
    total = sum(o.shape[0] for o in out)
    if pad_rows_to and total % pad_rows_to:
        out.append(jnp.zeros((pad_rows_to - total % pad_rows_to, SLAB_W), F32))
    return jnp.concatenate(out, axis=0)


def _unpack(slab, like):
    out, r = [], 0
    for a in like:
        rows = _slab_rows(a)
        out.append(slab[r:r + rows].reshape(-1)[:a.size].reshape(a.shape))
        r += rows
    return out


WEIGHTS = ['norm1_g', 'norm2_g', 'w_ada', 'b_ada', 'ssm_a_re', 'ssm_a_im', 'ssm_log_step', 'ssm_b_re', 'ssm_b_im',
           'ssm_c_re', 'ssm_c_im', 'ssm_d', 'ssm_w_out', 'conv_w_in', 'conv_w', 'conv_w_out', 'w_ffn_in',
           'w_ffn_out', 'final_g']
SLAB = ['norm1_g', 'norm2_g', 'b_ada', 'ssm_a_re', 'ssm_a_im', 'ssm_log_step', 'ssm_b_re', 'ssm_b_im', 'ssm_c_re',
        'ssm_c_im', 'ssm_d', 'final_g']
SHARDED = ['ssm_w_out', 'conv_w_in', 'conv_w_out', 'w_ffn_in', 'w_ffn_out']


def kernel(x, c, norm1_g, norm2_g, w_ada, b_ada, ssm_a_re, ssm_a_im, ssm_log_step, ssm_b_re, ssm_b_im, ssm_c_re, ssm_c_im, ssm_d, ssm_w_out, conv_w_in, conv_w, conv_w_out, w_ffn_in, w_ffn_out, final_g, loss_target, m_norm1_g, m_norm2_g, m_w_ada, m_b_ada, m_ssm_a_re, m_ssm_a_im, m_ssm_log_step, m_ssm_b_re, m_ssm_b_im, m_ssm_c_re, m_ssm_c_im, m_ssm_d, m_ssm_w_out, m_conv_w_in, m_conv_w, m_conv_w_out, m_w_ffn_in, m_w_ffn_out, m_final_g, v_norm1_g, v_norm2_g, v_w_ada, v_b_ada, v_ssm_a_re, v_ssm_a_im, v_ssm_log_step, v_ssm_b_re, v_ssm_b_im, v_ssm_c_re, v_ssm_c_im, v_ssm_d, v_ssm_w_out, v_conv_w_in, v_conv_w, v_conv_w_out, v_w_ffn_in, v_w_ffn_out, v_final_g):
    given = dict(locals())
    W = {n: given[n] for n in WEIGHTS}
    Mo = {n: given["m_" + n] for n in WEIGHTS}
    Vo = {n: given["v_" + n] for n in WEIGHTS}

    xs = x[0]
    tgt = loss_target[0]
    L, D = xs.shape
    nlayer = norm1_g.shape[0]
    NA = w_ada.shape[2]
    G = ssm_a_re.shape[1]
    nkb = D // S5_BLOCK
    ax, ay, ac = _axes()
    me = 4 * ax + 2 * ay + ac
    chip = 2 * ax + ay

    assert D == SLAB_W
    first = gather8(jnp.concatenate([jnp.broadcast_to(c, (8, D)), _pack([conv_w])], axis=0), "gather_c_conv_w")
    c_all = first[:, 0, :]
    b_sh = lax.dynamic_slice_in_dim(b_ada, chip * NA, NA, axis=1)[:, None, :]
    mods_part = ada_mods(c_all, w_ada, b_sh, "ada_mods")
    mg = gather8(mods_part.reshape(nlayer * 8, NA), "gather_mods")
    mg = mg.reshape(N_CHIP, 2, nlayer, 8, NA)[:, 0]
    mods_all = lax.dynamic_index_in_dim(mg, me, axis=2, keepdims=False)
    mods_all = jnp.transpose(mods_all, (1, 0, 2)).reshape(nlayer, 6, D)

    cw_parts = first[:, 8:]
    nconv = conv_w.shape[0]
    cw_full = jnp.stack([_unpack(cw_parts[2 * q], [conv_w])[0] for q in range(N_CHIP)], axis=2)
    cw_full = cw_full.reshape(nconv, 3, D)

    in_flight_w = {}

    def start_weights(i, after):
        names = (["ssm_w_out"] if i % 2 == 0 else ["conv_w_in", "conv_w_out"]) + ["w_ffn_in", "w_ffn_out"]
        shards = [W[n][i if n.startswith("w_ffn") else i // 2].astype(BF) for n in names]
        sems, srcs, lands, tok = gather_start(shards, after, "gather_start%d" % i)
        in_flight_w[i] = (names, sems, srcs, lands)
        return tok

    def relay_weights(i, after):
        names, sems, srcs, lands = in_flight_w[i]
        got = gather_wait(sems, srcs, lands, list(range(len(names))), after, "gather_wait%d" % i)
        rsems, rlands, tok = relay_start(got, after, "relay_start%d" % i)
        in_flight_w[i] = (names, rsems, rlands)
        return tok

    def layer_weights(i, after):
        names, rsems, rlands = in_flight_w[i]
        return dict(zip(names, relay_wait(rsems, rlands, after, "relay_wait%d" % i)))

    token = start_weights(0, cw_full + mods_all[0, 0:3])
    mods_all = mods_all + token[0:1, 0:1]

    s5 = []
    for j in range(ssm_a_re.shape[0]):
        disc, disc_vjp = jax.vjp(_discretise, ssm_a_re[j], ssm_a_im[j], ssm_log_step[j], ssm_b_re[j], ssm_b_im[j])
        abr, abi, bbar_re, bbar_im = disc
        tb, tbt = _compact(jnp.swapaxes(bbar_re, 1, 2), jnp.swapaxes(bbar_im, 1, 2), nkb)
        tc, tct = _compact(ssm_c_re[j], -ssm_c_im[j], nkb)
        chunk = _tile(L, (512, 256)) // 8
        s5.append(dict(vjp=disc_vjp, tb=tb, tbt=tbt, tc=tc, tct=tct, pw=_scan_powers(abr, abi, nkb, False, chunk),
                       pwr=_scan_powers(abr, abi, nkb, True, chunk)))

    saved = []
    xcur = xs
    for i in range(nlayer):
        j = i // 2
        mods = mods_all[i]
        sv = dict(x=xcur)
        if i % 2 == 0:
            h = norm_mod(xcur, norm1_g[i:i + 1], mods, 0, F32, "norm_mod_s5")
            dvec = ssm_d[j:j + 1]
            if i == 0:
                dvec = dvec + start_weights(1, h)[0:1, 0:1]
            states, yv, z = s5_fwd(h, s5[j]["tb"], s5[j]["tct"], s5[j]["pw"], dvec, "s5_fwd")
            if i == 0:
                mods = mods + relay_weights(0, z)[0:1, 0:1]
            full = layer_weights(i, z)
            o, mix, x2 = ssm_out_glu(z, full["ssm_w_out"], xcur, mods, 2, "ssm_out_glu")
            sv.update(h=h, states=states, y=yv, z=z, o=o)
        else:
            h = norm_mod(xcur, norm1_g[i:i + 1], mods, 0, BF, "norm_mod")
            full = layer_weights(i, h)
            p = mm_nn(h, full["conv_w_in"], BF, "mm_conv_in")
            mc = conv_fwd(p, cw_full[j], "conv_fwd")
            mix, x2 = mm_nn(mc, full["conv_w_out"].reshape(1, D, D), BF, "mm_conv_out", res=xcur, gate=mods[2:3])
            sv.update(h=h, p=p, mc=mc)
        h2 = norm_mod(x2, norm2_g[i:i + 1], mods, 3, BF, "norm_mod")
        gu, act = ffn_in_act(h2, full["w_ffn_in"], "ffn_in_act")
        F = act.shape[1]
        ff, x3 = mm_nn(act, full["w_ffn_out"].reshape(1, F, D), BF, "mm_ffn_out", res=x2, gate=mods[5:6])
        sv.update(mix=mix, x2=x2, h2=h2, gu=gu, act=act, ff=ff, w=full)
        saved.append(sv)
        xcur = x3
        if i + 1 < nlayer:
            token = relay_weights(i + 1, ff)
            if i + 2 < nlayer:
                token = token + start_weights(i + 2, token)
            mods_all = mods_all + token[0:1, 0:1]

    loss_blk, dx, dfinal, dff = final_loss(xcur, tgt, final_g[None, :], saved[-1]["ff"], mods_all[nlayer - 1], 5,
                                           "final_loss")
    dg2 = dfinal[1:2]

    gland = {n: lax.empty((W[n].shape[0], N_CHIP) + W[n].shape[1:], BF) for n in SHARDED}
    in_flight = []
    dmods = [None] * nlayer
    dnorm1, dnorm2 = [None] * nlayer, [None] * nlayer
    dconv_w = [None] * nconv
    ds5 = [None] * ssm_a_re.shape[0]
    token = jnp.zeros((8, 128), F32)

    def send_grads(names, grads, slot, after, name):
        sems, thru, lands, tok = scatter_start([grads[n] for n in names], [gland[n] for n in names], slot, after, name)
        gland.update(zip(names, lands))
        in_flight.append((names, slot, sems, thru, name))
        return tok

    def land_grads(group, after):
        for names, slot, sems, thru, name in in_flight:
            if names[0] in group:
                got = scatter_wait(sems, thru, [gland[n] for n in names], slot, after, name.replace("scatter", "landed"))
                gland.update(zip(names, got))

    for i in reversed(range(nlayer)):
        j = i // 2
        mods = mods_all[i] + token[0:1, 0:1]
        sv = saved[i]
        full = sv["w"]
        gfull = {}
        F = sv["act"].shape[1]
        gfull["w_ffn_out"] = mm_tn(sv["act"], dff, 1, "mm_tn_ffn_out").reshape(N_CHIP, F // N_CHIP, D)
        dgu = ffn_out_bwd(dff, full["w_ffn_out"].reshape(F, D), sv["gu"], "ffn_out_bwd")
        gfull["w_ffn_in"] = mm_tn(sv["h2"], dgu, N_CHIP, "mm_tn_ffn_in")
        dh2 = mm_nt(dgu, full["w_ffn_in"], BF, "mm_nt_ffn_in")
        token = send_grads(["w_ffn_out", "w_ffn_in"], gfull, [i, i], dh2, "scatter_ffn%d" % i)
        mods = mods + token[0:1, 0:1]
        dx2, s2, dmix = norm_bwd(dh2, sv["x2"], dx, norm2_g[i:i + 1], mods, 3, "norm_bwd_mix",
                                 branch=(sv["mix"], mods, 2))
        dg1 = s2[3:4]
        if i % 2 == 0:
            do = glu_bwd(dmix, sv["o"], "glu_bwd")
            gfull["ssm_w_out"] = mm_tn(sv["z"], do, N_CHIP, "mm_tn_ssm_out")
            dz = mm_nt(do, full["ssm_w_out"], BF, "mm_nt_ssm_out")
            dh, dd, dab, db, dc = s5_bwd(dz, sv["y"], sv["h"], sv["states"], s5[j]["tc"], s5[j]["tbt"], s5[j]["pwr"],
                                         ssm_d[j:j + 1], "s5_bwd")
            ds5[j] = (dd, dab, db, dc)
        else:
            gfull["conv_w_out"] = mm_tn(sv["mc"], dmix, 1, "mm_tn_conv_out").reshape(N_CHIP, D // N_CHIP, D)
            dmc = mm_nt(dmix, full["conv_w_out"].reshape(1, D, D), BF, "mm_nt_conv_out")
            dbg, dcg, dvv, dcw = conv_bwd(dmc, sv["p"], cw_full[j], "conv_bwd")
            dp = jnp.concatenate([dbg, dcg, dvv], axis=1)
            gfull["conv_w_in"] = mm_tn(sv["h"], dp, N_CHIP, "mm_tn_conv_in")
            dh = mm_nt(dp, full["conv_w_in"], BF, "mm_nt_conv_in")
            dconv_w[j] = dcw[0:3]
        dmods_i = [s2[0:2], dg2]
        if i > 0:
            dx, s1, dff = norm_bwd(dh, sv["x"], dx2, norm1_g[i:i + 1], mods, 0, "norm_bwd_ffn",
                                   branch=(saved[i - 1]["ff"], mods_all[i - 1], 5))
            dg2 = s1[3:4]
        else:
            dx, s1 = norm_bwd(dh, sv["x"], dx2, norm1_g[i:i + 1], mods, 0, "norm_bwd")
        dmods[i] = jnp.concatenate([s1[0:2], dg1] + dmods_i, axis=0).reshape(6 * D)
        dnorm1[i], dnorm2[i] = s1[2], s2[2]
        names = ["ssm_w_out"] if i % 2 == 0 else ["conv_w_out", "conv_w_in"]
        token = send_grads(names, gfull, [j] * len(names), dx, "scatter_mix%d" % i)

    small = dict(norm1_g=jnp.stack(dnorm1), norm2_g=jnp.stack(dnorm2), b_ada=jnp.stack(dmods),
                 final_g=dfinal[0] + token[0, 0])
    per = {n: [] for n in ('ssm_a_re', 'ssm_a_im', 'ssm_log_step', 'ssm_b_re', 'ssm_b_im', 'ssm_c_re', 'ssm_c_im', 'ssm_d')}
    GL = G // nkb
    for j, (dd, dab, db, dc) in enumerate(ds5):
        dab = jnp.sum(dab, axis=1).reshape(nkb, 2, GL, SSM_STATE)
        g_abr, g_abi = dab[:, 0].reshape(G, SSM_STATE), dab[:, 1].reshape(G, SSM_STATE)
        db, dc = db.reshape(G, SSM_GROUP, 2, SSM_STATE), dc.reshape(G, SSM_GROUP, 2, SSM_STATE)
        gb_re, gb_im, gc_re, gc_im = db[:, :, 0], db[:, :, 1], dc[:, :, 0], dc[:, :, 1]
        ga_re, ga_im, gls, gbr, gbi = s5[j]["vjp"]((g_abr, g_abi, jnp.swapaxes(gb_re, 1, 2), jnp.swapaxes(gb_im, 1, 2)))
        for n, val in zip(per, (ga_re, ga_im, gls, gbr, gbi, gc_re, -gc_im, jnp.sum(dd, axis=0))):
            per[n].append(val)
    small.update({n: jnp.stack(vals) for n, vals in per.items()})
    dcw_full = jnp.stack(dconv_w)

    my_loss = loss_blk[0:1, 0:1]
    slab_like = [W[n] for n in SLAB] + [dcw_full, my_loss]
    rows64 = 8 * N_DEV
    slab = _pack([small[n] for n in SLAB] + [dcw_full, my_loss], rows64)
    per_dev = slab.shape[0] // N_DEV
    x_sems, x_srcs, x_lands, token = exchange_start(
        [(slab.reshape(N_DEV, per_dev, SLAB_W), True), (_pack([small["b_ada"]]), False)], dx, "small_scatter")

    early = [n for n in SHARDED if n != "ssm_w_out"]
    land_grads(early, token)
    mine = [reduce4(gland[n], "reduce4_" + n) for n in early]

    parts, dm_all = exchange_wait(x_sems, x_srcs, x_lands, [True, False], mine[-1][0, :8, :128], "small_landed")
    t_sems, t_srcs, t_lands, token = exchange_start([(sum8(parts, "sum_small"), False)], dm_all, "small_gather")
    out = {}

    w_sems, w_srcs, w_lands, token2 = swap_start(mine, "swap_start")
    dm_all = dm_all.reshape(N_DEV, -1)[:, :b_ada.size].reshape(N_DEV, nlayer, N_CHIP, NA)
    dm_sh = jnp.transpose(lax.dynamic_index_in_dim(dm_all, chip, axis=2, keepdims=False), (1, 0, 2))
    res = adamw_ada(jnp.transpose(c_all) + token[0:1, 0:1] + token2[0:1, 0:1], dm_sh, w_ada, m_w_ada, v_w_ada,
                    "adamw_ada")
    out["g", "w_ada"], out["d", "w_ada"], out["m", "w_ada"], out["v", "w_ada"] = res

    g_slab = exchange_wait(t_sems, t_srcs, t_lands, [False], out["g", "w_ada"], "small_total")[0]
    g_slab = g_slab.reshape(slab.shape)
    d_slab, m_slab, v_slab = adamw_slab(
        g_slab, _pack([W[n] for n in SLAB] + [jnp.zeros_like(dcw_full)], rows64),
        _pack([Mo[n] for n in SLAB] + [jnp.zeros_like(dcw_full)], rows64),
        _pack([Vo[n] for n in SLAB] + [jnp.ones_like(dcw_full)], rows64), "adamw_slab")
    for k, slab in zip(("g", "d", "m", "v"), (g_slab, d_slab, m_slab, v_slab)):
        for n, val in zip(SLAB, _unpack(slab, slab_like)):
            out[k, n] = val
    g_cw = lax.dynamic_slice_in_dim(_unpack(g_slab, slab_like)[-2], chip * conv_w.shape[2], conv_w.shape[2], axis=2)
    out["g", "conv_w"] = g_cw
    out["d", "conv_w"], out["m", "conv_w"], out["v", "conv_w"] = [
        r.reshape(conv_w.shape) for r in adamw_plain(conv_w.reshape(-1, conv_w.shape[2]), m_conv_w.reshape(-1, conv_w.shape[2]),
                                                     v_conv_w.reshape(-1, conv_w.shape[2]), g_cw.reshape(-1, conv_w.shape[2]),
                                                     "adamw_conv_w")]

    mine, theirs = swap_wait(w_sems, w_srcs, w_lands, d_slab, "swap_wait")
    for n, ga, gb in zip(early, mine, theirs):
        r = adamw_sharded(W[n], Mo[n], Vo[n], ga, gb, "adamw_" + n)
        out["g", n], out["d", n], out["m", n], out["v", n] = r

    land_grads(["ssm_w_out"], out["g", "w_ffn_out"])
    ga = reduce4(gland["ssm_w_out"], "reduce4_ssm_w_out")
    gb = swap_siblings([ga], "swap_siblings")[0]
    r = adamw_sharded(ssm_w_out, m_ssm_w_out, v_ssm_w_out, ga, gb, "adamw_ssm_w_out")
    out["g", "ssm_w_out"], out["d", "ssm_w_out"], out["m", "ssm_w_out"], out["v", "ssm_w_out"] = r

    loss = _unpack(g_slab, slab_like)[-1][0, 0]
    return (loss, dx[None], *[out["g", n] for n in WEIGHTS], *[out["d", n] for n in WEIGHTS],
            *[out["m", n] for n in WEIGHTS], *[out["v", n] for n in WEIGHTS])
```

```python
import math

import jax
import jax.numpy as jnp
from jax import lax
from jax.experimental import pallas as pl
from jax.experimental.pallas import tpu as pltpu

F32 = jnp.float32
BF = jnp.bfloat16
MESH = pl.DeviceIdType.MESH
ANY = pl.BlockSpec(memory_space=pl.ANY)

N_DEV = 8
N_CHIP = 4
SSM_GROUP = 16
SSM_STATE = 64
S5_BLOCK = 256
RMS_EPS = 1e-6
ADAM_LR, ADAM_B1, ADAM_B2, ADAM_EPS, ADAM_WD, ADAM_STEP = 0.001, 0.9, 0.999, 1e-08, 0.01, 10
V7X_VMEM_BYTES = 64 * 1024 * 1024
VMEM_LIMIT = V7X_VMEM_BYTES - 12 * 1024 * 1024
SLAB_W = 1024
GELU_C = math.sqrt(2.0 / math.pi)
GELU_A = 0.044715


def _cp(*sem):
    return pltpu.CompilerParams(dimension_semantics=sem if sem else None, vmem_limit_bytes=VMEM_LIMIT)


def _tile(n, prefs):
    for p in prefs:
        if p <= n and n % p == 0:
            return p
    return n


def _sigmoid(v):
    return 0.5 * jnp.tanh(0.5 * v) + 0.5


def _axes():
    return lax.axis_index("x"), lax.axis_index("y"), lax.axis_index("c")


def _flip(v, k):
    return 1 - v if k else v


def gather8(v, name):
    R, C = v.shape

    def body(v_ref, o_ref, ssem, rsem, lsem):
        x, y, c = _axes()
        me = 4 * x + 2 * y + c
        loc = pltpu.make_async_copy(v_ref, o_ref.at[me], lsem)
        loc.start()
        copies = []
        for k in range(1, N_DEV):
            peer = (_flip(x, (k >> 2) & 1), _flip(y, (k >> 1) & 1), _flip(c, k & 1))
            cp = pltpu.make_async_remote_copy(src_ref=v_ref, dst_ref=o_ref.at[me], send_sem=ssem.at[k - 1],
                                              recv_sem=rsem.at[k - 1], device_id=peer, device_id_type=MESH)
            cp.start()
            copies.append(cp)
        for cp in copies:
            cp.wait()
        loc.wait()

    return pl.pallas_call(
        body, name=name,
        out_shape=jax.ShapeDtypeStruct((N_DEV, R, C), v.dtype),
        in_specs=[pl.BlockSpec(memory_space=pltpu.VMEM)],
        out_specs=pl.BlockSpec(memory_space=pltpu.VMEM),
        scratch_shapes=[pltpu.SemaphoreType.DMA((N_DEV - 1,)), pltpu.SemaphoreType.DMA((N_DEV - 1,)),
                        pltpu.SemaphoreType.DMA],
        compiler_params=pltpu.CompilerParams(vmem_limit_bytes=VMEM_LIMIT),
    )(v)


HBM = pl.BlockSpec(memory_space=pltpu.HBM)
SEM = pl.BlockSpec(memory_space=pltpu.SEMAPHORE)
EFFECT = pltpu.SideEffectType.DATAFLOW_SIDE_EFFECTING


def _in_hbm(a):
    return pltpu.with_memory_space_constraint(a, pltpu.HBM)


def _chip_peers(x, y, c):
    out = []
    for k in range(1, N_CHIP):
        px, py = _flip(x, k >> 1), _flip(y, k & 1)
        out.append(((px, py, c), 2 * px + py))
    return out


def _my_half(ref, c):
    rows = ref.shape[0] // 2
    return pl.ds(pl.multiple_of(c * rows, 16), rows)


def relay_start(lands, after, name):
    n = len(lands)

    def body(*refs):
        land = refs[:n]
        ssem, rsem = refs[n + 1:n + 3]
        token = refs[-1]
        x, y, c = _axes()
        for a in range(n):
            half = _my_half(land[a].at[0], c)
            for k, (_, pchip) in enumerate(_chip_peers(x, y, c)):
                pltpu.make_async_remote_copy(src_ref=land[a].at[pchip, half], dst_ref=land[a].at[pchip, half],
                                             send_sem=ssem.at[3 * a + k], recv_sem=rsem.at[3 * a + k],
                                             device_id=(x, y, 1 - c), device_id_type=MESH).start()
        token[...] = jnp.zeros_like(token)

    out_shape = ([pltpu.SemaphoreType.DMA((3 * n,)), pltpu.SemaphoreType.DMA((3 * n,))]
                 + [pltpu.HBM(l.shape, l.dtype) for l in lands] + [jax.ShapeDtypeStruct((8, 128), F32)])
    res = pl.pallas_call(
        body, name=name, out_shape=out_shape, in_specs=[HBM] * n + [ANY],
        out_specs=[SEM, SEM] + [HBM] * n + [pl.BlockSpec(memory_space=pltpu.VMEM)],
        input_output_aliases={a: 2 + a for a in range(n)},
        compiler_params=pltpu.CompilerParams(has_side_effects=EFFECT),
    )(*lands, after)
    return tuple(res[:2]), list(res[2:2 + n]), res[-1]


def relay_wait(sems, lands, after, name):
    n = len(lands)

    def body(*refs):
        land = refs[:n]
        ssem, rsem = refs[n:n + 2]
        x, y, c = _axes()
        for a in range(n):
            mine, theirs = _my_half(land[a].at[0], c), _my_half(land[a].at[0], 1 - c)
            for k, (_, pchip) in enumerate(_chip_peers(x, y, c)):
                cp = pltpu.make_async_remote_copy(src_ref=land[a].at[pchip, mine], dst_ref=land[a].at[pchip, theirs],
                                                  send_sem=ssem.at[3 * a + k], recv_sem=rsem.at[3 * a + k],
                                                  device_id=(x, y, 1 - c), device_id_type=MESH)
                cp.wait_send()
                cp.wait_recv()

    res = pl.pallas_call(
        body, name=name, out_shape=[pltpu.HBM(l.shape, l.dtype) for l in lands],
        in_specs=[HBM] * n + [SEM, SEM, ANY], out_specs=[HBM] * n,
        input_output_aliases={a: a for a in range(n)},
        compiler_params=pltpu.CompilerParams(has_side_effects=EFFECT),
    )(*lands, *sems, after)
    return list(res)


def gather_start(shards, after, name):
    n = len(shards)

    def body(*refs):
        src, land = refs[:n], refs[n:2 * n]
        ssem, rsem, lsem = refs[2 * n + 1:2 * n + 4]
        token = refs[-1]
        x, y, c = _axes()
        chip = 2 * x + y
        for a in range(n):
            pltpu.make_async_copy(src[a], land[a].at[chip], lsem.at[a]).start()
            half = _my_half(src[a], c)
            for k, (peer, _) in enumerate(_chip_peers(x, y, c)):
                pltpu.make_async_remote_copy(src_ref=src[a].at[half], dst_ref=land[a].at[chip, half],
                                             send_sem=ssem.at[3 * a + k], recv_sem=rsem.at[3 * a + k],
                                             device_id=peer, device_id_type=MESH).start()
        token[...] = jnp.zeros_like(token)

    lands = [lax.empty((N_CHIP,) + s.shape, s.dtype) for s in shards]
    out_shape = ([pltpu.SemaphoreType.DMA((3 * n,)), pltpu.SemaphoreType.DMA((3 * n,)), pltpu.SemaphoreType.DMA((n,))]
                 + [pltpu.HBM(s.shape, s.dtype) for s in shards] + [pltpu.HBM(l.shape, l.dtype) for l in lands]
                 + [jax.ShapeDtypeStruct((8, 128), F32)])
    res = pl.pallas_call(
        body, name=name, out_shape=out_shape, in_specs=[HBM] * (2 * n) + [ANY],
        out_specs=[SEM, SEM, SEM] + [HBM] * (2 * n) + [pl.BlockSpec(memory_space=pltpu.VMEM)],
        input_output_aliases={a: 3 + a for a in range(2 * n)},
        compiler_params=pltpu.CompilerParams(has_side_effects=EFFECT),
    )(*[_in_hbm(s) for s in shards], *[_in_hbm(l) for l in lands], after)
    return tuple(res[:3]), list(res[3:3 + n]), list(res[3 + n:3 + 2 * n]), res[-1]


def gather_wait(sems, srcs, lands, idx, after, name):
    m = len(idx)

    def body(*refs):
        src, land = refs[:m], refs[m:2 * m]
        ssem, rsem, lsem = refs[2 * m:2 * m + 3]
        x, y, c = _axes()
        chip = 2 * x + y
        for j, a in enumerate(idx):
            half = _my_half(src[j], c)
            for k, (peer, pchip) in enumerate(_chip_peers(x, y, c)):
                cp = pltpu.make_async_remote_copy(src_ref=src[j].at[half], dst_ref=land[j].at[pchip, half],
                                                  send_sem=ssem.at[3 * a + k], recv_sem=rsem.at[3 * a + k],
                                                  device_id=peer, device_id_type=MESH)
                cp.wait_send()
                cp.wait_recv()
            pltpu.make_async_copy(src[j], land[j].at[chip], lsem.at[a]).wait()

    s_in = [srcs[a] for a in idx]
    l_in = [lands[a] for a in idx]
    res = pl.pallas_call(
        body, name=name,
        out_shape=[pltpu.HBM(s.shape, s.dtype) for s in s_in] + [pltpu.HBM(l.shape, l.dtype) for l in l_in],
        in_specs=[HBM] * (2 * m) + [SEM, SEM, SEM, ANY], out_specs=[HBM] * (2 * m),
        input_output_aliases={a: a for a in range(2 * m)},
        compiler_params=pltpu.CompilerParams(has_side_effects=EFFECT),
    )(*s_in, *l_in, *sems, after)
    return list(res[m:])


def scatter_start(grads, lands, slot, after, name):
    n = len(grads)

    def body(*refs):
        src, land = refs[:n], refs[n:2 * n]
        ssem, rsem, lsem = refs[2 * n + 1:2 * n + 4]
        token = refs[-1]
        x, y, c = _axes()
        chip = 2 * x + y
        for a in range(n):
            pltpu.make_async_copy(src[a].at[chip], land[a].at[slot[a], chip], lsem.at[a]).start()
            for k, (peer, pchip) in enumerate(_chip_peers(x, y, c)):
                pltpu.make_async_remote_copy(src_ref=src[a].at[pchip], dst_ref=land[a].at[slot[a], chip],
                                             send_sem=ssem.at[3 * a + k], recv_sem=rsem.at[3 * a + k],
                                             device_id=peer, device_id_type=MESH).start()
        token[...] = jnp.zeros_like(token)

    out_shape = ([pltpu.SemaphoreType.DMA((3 * n,)), pltpu.SemaphoreType.DMA((3 * n,)), pltpu.SemaphoreType.DMA((n,))]
                 + [pltpu.HBM(g.shape, g.dtype) for g in grads] + [pltpu.HBM(l.shape, l.dtype) for l in lands]
                 + [jax.ShapeDtypeStruct((8, 128), F32)])
    res = pl.pallas_call(
        body, name=name, out_shape=out_shape, in_specs=[HBM] * (2 * n) + [ANY],
        out_specs=[SEM, SEM, SEM] + [HBM] * (2 * n) + [pl.BlockSpec(memory_space=pltpu.VMEM)],
        input_output_aliases={a: 3 + a for a in range(2 * n)},
        compiler_params=pltpu.CompilerParams(has_side_effects=EFFECT),
    )(*[_in_hbm(g) for g in grads], *[_in_hbm(l) for l in lands], after)
    return tuple(res[:3]), list(res[3:3 + n]), list(res[3 + n:3 + 2 * n]), res[-1]


def scatter_wait(sems, grads, lands, slot, after, name):
    n = len(grads)

    def body(*refs):
        src, land = refs[:n], refs[n:2 * n]
        ssem, rsem, lsem = refs[2 * n:2 * n + 3]
        x, y, c = _axes()
        chip = 2 * x + y
        for a in range(n):
            for k, (peer, pchip) in enumerate(_chip_peers(x, y, c)):
                cp = pltpu.make_async_remote_copy(src_ref=src[a].at[pchip], dst_ref=land[a].at[slot[a], pchip],
                                                  send_sem=ssem.at[3 * a + k], recv_sem=rsem.at[3 * a + k],
                                                  device_id=peer, device_id_type=MESH)
                cp.wait_send()
                cp.wait_recv()
            pltpu.make_async_copy(src[a].at[chip], land[a].at[slot[a], chip], lsem.at[a]).wait()

    res = pl.pallas_call(
        body, name=name,
        out_shape=[pltpu.HBM(g.shape, g.dtype) for g in grads] + [pltpu.HBM(l.shape, l.dtype) for l in lands],
        in_specs=[HBM] * (2 * n) + [SEM, SEM, SEM, ANY], out_specs=[HBM] * (2 * n),
        input_output_aliases={a: a for a in range(2 * n)},
        compiler_params=pltpu.CompilerParams(has_side_effects=EFFECT),
    )(*grads, *lands, *sems, after)
    return list(res[n:])


def reduce4(land, name):
    nl, _, R, C = land.shape
    TR = _adam_rows(R, C)

    def body(l_ref, o_ref):
        o_ref[...] = ((l_ref[0].astype(F32) + l_ref[1].astype(F32)) + l_ref[2].astype(F32)) + l_ref[3].astype(F32)

    return pl.pallas_call(
        body, name=name, grid=(nl, R // TR),
        in_specs=[pl.BlockSpec((None, N_CHIP, TR, C), lambda i, r: (i, 0, r, 0))],
        out_specs=pl.BlockSpec((None, TR, C), lambda i, r: (i, r, 0)),
        out_shape=jax.ShapeDtypeStruct((nl, R, C), F32), compiler_params=_cp("parallel", "parallel"))(land)


def swap_siblings(arrs, name):
    n = len(arrs)

    def body(*refs):
        src, dst = refs[:n], refs[n:2 * n]
        ssem, rsem = refs[2 * n:]
        x, y, c = _axes()
        cps = [pltpu.make_async_remote_copy(src_ref=src[a], dst_ref=dst[a], send_sem=ssem.at[a], recv_sem=rsem.at[a],
                                            device_id=(x, y, 1 - c), device_id_type=MESH) for a in range(n)]
        for cp in cps:
            cp.start()
        for cp in cps:
            cp.wait()

    return pl.pallas_call(
        body, name=name, out_shape=[jax.ShapeDtypeStruct(a.shape, a.dtype) for a in arrs],
        in_specs=[ANY] * n, out_specs=[ANY] * n,
        scratch_shapes=[pltpu.SemaphoreType.DMA((n,)), pltpu.SemaphoreType.DMA((n,))],
        compiler_params=pltpu.CompilerParams(vmem_limit_bytes=VMEM_LIMIT),
    )(*arrs)


def swap_start(arrs, name):
    n = len(arrs)

    def body(*refs):
        src, land = refs[:n], refs[n:2 * n]
        ssem, rsem = refs[2 * n:2 * n + 2]
        token = refs[-1]
        x, y, c = _axes()
        for a in range(n):
            pltpu.make_async_remote_copy(src_ref=src[a], dst_ref=land[a], send_sem=ssem.at[a], recv_sem=rsem.at[a],
                                         device_id=(x, y, 1 - c), device_id_type=MESH).start()
        token[...] = jnp.zeros_like(token)

    lands = [lax.empty(a.shape, a.dtype) for a in arrs]
    out_shape = ([pltpu.SemaphoreType.DMA((n,)), pltpu.SemaphoreType.DMA((n,))]
                 + [pltpu.HBM(a.shape, a.dtype) for a in arrs] * 2 + [jax.ShapeDtypeStruct((8, 128), F32)])
    res = pl.pallas_call(
        body, name=name, out_shape=out_shape, in_specs=[HBM] * (2 * n),
        out_specs=[SEM, SEM] + [HBM] * (2 * n) + [pl.BlockSpec(memory_space=pltpu.VMEM)],
        input_output_aliases={a: 2 + a for a in range(2 * n)},
        compiler_params=pltpu.CompilerParams(has_side_effects=EFFECT),
    )(*[_in_hbm(a) for a in arrs], *[_in_hbm(l) for l in lands])
    return tuple(res[:2]), list(res[2:2 + n]), list(res[2 + n:2 + 2 * n]), res[-1]


def swap_wait(sems, srcs, lands, after, name):
    n = len(srcs)

    def body(*refs):
        src, land = refs[:n], refs[n:2 * n]
        ssem, rsem = refs[2 * n:2 * n + 2]
        x, y, c = _axes()
        for a in range(n):
            cp = pltpu.make_async_remote_copy(src_ref=src[a], dst_ref=land[a], send_sem=ssem.at[a],
                                              recv_sem=rsem.at[a], device_id=(x, y, 1 - c), device_id_type=MESH)
            cp.wait_send()
            cp.wait_recv()

    res = pl.pallas_call(
        body, name=name, out_shape=[pltpu.HBM(a.shape, a.dtype) for a in srcs] * 2,
        in_specs=[HBM] * (2 * n) + [SEM, SEM, ANY], out_specs=[HBM] * (2 * n),
        input_output_aliases={a: a for a in range(2 * n)},
        compiler_params=pltpu.CompilerParams(has_side_effects=EFFECT),
    )(*srcs, *lands, *sems, after)
    return list(res[:n]), list(res[n:])


def _all_peers(x, y, c):
    out = []
    for k in range(1, N_DEV):
        px, py, pc = _flip(x, (k >> 2) & 1), _flip(y, (k >> 1) & 1), _flip(c, k & 1)
        out.append(((px, py, pc), 4 * px + 2 * py + pc))
    return out


def exchange_start(items, after, name):
    n = len(items)

    def body(*refs):
        src, land = refs[:n], refs[n:2 * n]
        ssem, rsem, lsem = refs[2 * n + 1:2 * n + 4]
        token = refs[-1]
        x, y, c = _axes()
        me = 4 * x + 2 * y + c
        for a, (_, scatter) in enumerate(items):
            pltpu.make_async_copy(src[a].at[me] if scatter else src[a], land[a].at[me], lsem.at[a]).start()
            for k, (peer, p) in enumerate(_all_peers(x, y, c)):
                pltpu.make_async_remote_copy(src_ref=src[a].at[p] if scatter else src[a], dst_ref=land[a].at[me],
                                             send_sem=ssem.at[7 * a + k], recv_sem=rsem.at[7 * a + k],
                                             device_id=peer, device_id_type=MESH).start()
        token[...] = jnp.zeros_like(token)

    srcs = [s for s, _ in items]
    lands = [lax.empty(s.shape if sc else (N_DEV,) + s.shape, s.dtype) for s, sc in items]
    out_shape = ([pltpu.SemaphoreType.DMA((7 * n,)), pltpu.SemaphoreType.DMA((7 * n,)), pltpu.SemaphoreType.DMA((n,))]
                 + [pltpu.HBM(s.shape, s.dtype) for s in srcs] + [pltpu.HBM(l.shape, l.dtype) for l in lands]
                 + [jax.ShapeDtypeStruct((8, 128), F32)])
    res = pl.pallas_call(
        body, name=name, out_shape=out_shape, in_specs=[HBM] * (2 * n) + [ANY],
        out_specs=[SEM, SEM, SEM] + [HBM] * (2 * n) + [pl.BlockSpec(memory_space=pltpu.VMEM)],
        input_output_aliases={a: 3 + a for a in range(2 * n)},
        compiler_params=pltpu.CompilerParams(has_side_effects=EFFECT),
    )(*[_in_hbm(s) for s in srcs], *[_in_hbm(l) for l in lands], after)
    return tuple(res[:3]), list(res[3:3 + n]), list(res[3 + n:3 + 2 * n]), res[-1]


def exchange_wait(sems, srcs, lands, scatter, after, name):
    n = len(srcs)

    def body(*refs):
        src, land = refs[:n], refs[n:2 * n]
        ssem, rsem, lsem = refs[2 * n:2 * n + 3]
        x, y, c = _axes()
        me = 4 * x + 2 * y + c
        for a in range(n):
            for k, (peer, p) in enumerate(_all_peers(x, y, c)):
                cp = pltpu.make_async_remote_copy(src_ref=src[a].at[p] if scatter[a] else src[a],
                                                  dst_ref=land[a].at[p], send_sem=ssem.at[7 * a + k],
                                                  recv_sem=rsem.at[7 * a + k], device_id=peer, device_id_type=MESH)
                cp.wait_send()
                cp.wait_recv()
            pltpu.make_async_copy(src[a].at[me] if scatter[a] else src[a], land[a].at[me], lsem.at[a]).wait()

    res = pl.pallas_call(
        body, name=name,
        out_shape=[pltpu.HBM(s.shape, s.dtype) for s in srcs] + [pltpu.HBM(l.shape, l.dtype) for l in lands],
        in_specs=[HBM] * (2 * n) + [SEM, SEM, SEM, ANY], out_specs=[HBM] * (2 * n),
        input_output_aliases={a: a for a in range(2 * n)},
        compiler_params=pltpu.CompilerParams(has_side_effects=EFFECT),
    )(*srcs, *lands, *sems, after)
    return list(res[n:])


def sum8(parts, name):
    _, P, C = parts.shape

    def body(p_ref, o_ref):
        tot = p_ref[0]
        for d in range(1, N_DEV):
            tot = tot + p_ref[d]
        o_ref[...] = tot

    return pl.pallas_call(body, name=name, out_shape=jax.ShapeDtypeStruct((P, C), F32),
                          compiler_params=pltpu.CompilerParams(vmem_limit_bytes=VMEM_LIMIT))(parts)


def mm_nn(a, w, out_dtype, name, res=None, gate=None):
    M, K = a.shape
    S, _, Ns = w.shape
    TM = _tile(M, (1024, 512, 256) if K <= 1024 else (512, 256))
    TN = _tile(Ns, (1408, 1024, 768, 512, 256, 128))
    nj = Ns // TN
    fused = res is not None

    def body(*refs):
        if fused:
            a_ref, w_ref, r_ref, g_ref, f_ref, o_ref = refs
        else:
            a_ref, w_ref, f_ref = refs
        f = jnp.dot(a_ref[...], w_ref[...], preferred_element_type=F32)
        f_ref[...] = f.astype(f_ref.dtype)
        if fused:
            o_ref[...] = r_ref[...] + g_ref[...] * f

    col = lambda s, j, i: (i, s * nj + j)
    in_specs = [pl.BlockSpec((TM, K), lambda s, j, i: (i, 0)), pl.BlockSpec((None, K, TN), lambda s, j, i: (s, 0, j))]
    out_specs = [pl.BlockSpec((TM, TN), col)]
    out_shape = [jax.ShapeDtypeStruct((M, S * Ns), out_dtype)]
    args = [a, w]
    if fused:
        in_specs += [pl.BlockSpec((TM, TN), col), pl.BlockSpec((1, TN), lambda s, j, i: (0, s * nj + j))]
        out_specs.append(pl.BlockSpec((TM, TN), col))
        out_shape.append(jax.ShapeDtypeStruct((M, S * Ns), F32))
        args += [res, gate]
    out = pl.pallas_call(body, name=name, grid=(S, nj, M // TM), in_specs=in_specs, out_specs=out_specs,
                         out_shape=out_shape, compiler_params=_cp("parallel", "parallel", "parallel"))(*args)
    return tuple(out) if fused else out[0]


def mm_nt(g, w, out_dtype, name):
    g3 = g if g.ndim == 3 else g[None]
    Q, M, F = g3.shape
    S, K, Ns = w.shape
    TM = _tile(M, (2048, 1024, 512, 256) if K <= 1024 else (512, 256))
    TN = _tile(Ns, (1408, 1024, 768, 512, 256, 128))
    nj = Ns // TN
    nred = S * nj
    per_part = F // TN

    def body(g_ref, w_ref, o_ref, acc):
        n = pl.program_id(1)

        @pl.when(n == 0)
        def _():
            acc[...] = jnp.zeros_like(acc)

        acc[...] += lax.dot_general(g_ref[...], w_ref[...], (((1,), (1,)), ((), ())), preferred_element_type=F32)

        @pl.when(n == nred - 1)
        def _():
            o_ref[...] = acc[...].astype(o_ref.dtype)

    return pl.pallas_call(
        body, name=name, grid=(M // TM, nred),
        in_specs=[pl.BlockSpec((None, TM, TN), lambda i, n: (n // per_part, i, n % per_part)),
                  pl.BlockSpec((None, K, TN), lambda i, n: (n // nj, 0, n % nj))],
        out_specs=pl.BlockSpec((TM, K), lambda i, n: (i, 0)),
        out_shape=jax.ShapeDtypeStruct((M, K), out_dtype),
        scratch_shapes=[pltpu.VMEM((TM, K), F32)],
        compiler_params=_cp("parallel", "arbitrary"))(g3, w)


def mm_tn(a, g, S, name):
    M, K = a.shape
    g3 = g if g.ndim == 3 else g[None]
    Q, _, F = g3.shape
    Ns = Q * F // S
    TN = _tile(Ns, (1408, 1024, 768, 512, 256, 128))
    TK = next(t for t in (1024, 512, 256, 128) if t <= K and K % t == 0
              and 4 * M * (t + TN) + 8 * t * TN <= VMEM_LIMIT * 3 // 4)
    nj = Ns // TN
    per_part = F // TN

    def body(a_ref, g_ref, o_ref):
        o_ref[...] = lax.dot_general(a_ref[...], g_ref[...], (((0,), (0,)), ((), ())),
                                     preferred_element_type=F32).astype(o_ref.dtype)

    return pl.pallas_call(
        body, name=name, grid=(S * nj, K // TK),
        in_specs=[pl.BlockSpec((M, TK), lambda n, k: (0, k)),
                  pl.BlockSpec((None, M, TN), lambda n, k: (n // per_part, 0, n % per_part))],
        out_specs=pl.BlockSpec((None, TK, TN), lambda n, k: (n // nj, k, n % nj)),
        out_shape=jax.ShapeDtypeStruct((S, K, Ns), BF),
        compiler_params=_cp("parallel", "parallel"))(a, g3)


ROW_TILE = (512, 256)


def _rows(TL, D):
    return pl.BlockSpec((TL, D), lambda i: (i, 0))


def _fixed(R, D):
    return pl.BlockSpec((R, D), lambda i: (0, 0))


def _rowsum8(v):
    T, D = v.shape
    return jnp.sum(v.reshape(T // 8, 8, D), axis=0)


def _norm_parts(xv):
    r = lax.rsqrt(jnp.mean(xv * xv, axis=-1, keepdims=True) + RMS_EPS)
    return xv * r, r


def norm_mod(x, gamma, mods, k_shift, out_dtype, name):
    L, D = x.shape
    TL = _tile(L, ROW_TILE)

    def body(x_ref, g_ref, m_ref, o_ref):
        xn, _ = _norm_parts(x_ref[...])
        sh, sc = m_ref[k_shift:k_shift + 1, :], m_ref[k_shift + 1:k_shift + 2, :]
        o_ref[...] = ((xn * g_ref[...]) * (1.0 + sc) + sh).astype(o_ref.dtype)

    return pl.pallas_call(body, name=name, grid=(L // TL,),
                          in_specs=[_rows(TL, D), _fixed(1, D), _fixed(6, D)], out_specs=_rows(TL, D),
                          out_shape=jax.ShapeDtypeStruct((L, D), out_dtype), compiler_params=_cp("parallel"))(x, gamma, mods)


def norm_bwd(dh, x, dres, gamma, mods, k_shift, name, branch=None):
    L, D = x.shape
    TL = _tile(L, ROW_TILE)
    nacc = 4 if branch else 3

    def body(*refs):
        if branch:
            dh_ref, x_ref, dr_ref, g_ref, m_ref, f_ref, fm_ref, dx_ref, s_ref, df_ref, acc = refs
        else:
            dh_ref, x_ref, dr_ref, g_ref, m_ref, dx_ref, s_ref, acc = refs
        i = pl.program_id(0)

        @pl.when(i == 0)
        def _():
            acc[...] = jnp.zeros_like(acc)

        xn, r = _norm_parts(x_ref[...])
        dh_v = dh_ref[...].astype(F32)
        gam = g_ref[...]
        sc = m_ref[k_shift + 1:k_shift + 2, :]
        dn = dh_v * (1.0 + sc)
        dxn = dn * gam
        dx = dr_ref[...] + r * (dxn - xn * jnp.mean(dxn * xn, axis=-1, keepdims=True))
        dx_ref[...] = dx
        acc[0] += _rowsum8(dh_v)
        acc[1] += _rowsum8(dh_v * (xn * gam))
        acc[2] += _rowsum8(dn * xn)
        if branch:
            df_ref[...] = (dx * fm_ref[branch[2]:branch[2] + 1, :]).astype(df_ref.dtype)
            acc[3] += _rowsum8(dx * f_ref[...].astype(F32))

        @pl.when(i == pl.num_programs(0) - 1)
        def _():
            s_ref[...] = jnp.zeros_like(s_ref)
            for q in range(nacc):
                s_ref[q:q + 1, :] = jnp.sum(acc[q], axis=0, keepdims=True)

    in_specs = [_rows(TL, D), _rows(TL, D), _rows(TL, D), _fixed(1, D), _fixed(6, D)]
    out_specs = [_rows(TL, D), _fixed(8, D)]
    out_shape = [jax.ShapeDtypeStruct((L, D), F32), jax.ShapeDtypeStruct((8, D), F32)]
    args = [dh, x, dres, gamma, mods]
    if branch:
        in_specs += [_rows(TL, D), _fixed(6, D)]
        out_specs.append(_rows(TL, D))
        out_shape.append(jax.ShapeDtypeStruct((L, D), BF))
        args += [branch[0], branch[1]]
    return pl.pallas_call(
        body, name=name, grid=(L // TL,), in_specs=in_specs, out_specs=out_specs, out_shape=out_shape,
        scratch_shapes=[pltpu.VMEM((nacc, 8, D), F32)], compiler_params=_cp("arbitrary"))(*args)


def ffn_in_act(a, w, name):
    M, K = a.shape
    S, _, Ns = w.shape
    half = S // 2
    TM = _tile(M, (512, 256))
    TN = _tile(Ns, (1408, 1024, 768, 512, 256, 128))
    nj = Ns // TN

    def body(a_ref, wg_ref, wu_ref, gu_ref, act_ref):
        av = a_ref[...]
        g = jnp.dot(av, wg_ref[...], preferred_element_type=F32)
        u = jnp.dot(av, wu_ref[...], preferred_element_type=F32)
        gu_ref[0] = g.astype(gu_ref.dtype)
        gu_ref[1] = u.astype(gu_ref.dtype)
        act_ref[...] = (g * _sigmoid(g) * u).astype(act_ref.dtype)

    return pl.pallas_call(
        body, name=name, grid=(half, nj, M // TM),
        in_specs=[pl.BlockSpec((TM, K), lambda s, j, i: (i, 0)),
                  pl.BlockSpec((None, K, TN), lambda s, j, i: (s, 0, j)),
                  pl.BlockSpec((None, K, TN), lambda s, j, i: (s + half, 0, j))],
        out_specs=[pl.BlockSpec((2, TM, TN), lambda s, j, i: (0, i, s * nj + j)),
                   pl.BlockSpec((TM, TN), lambda s, j, i: (i, s * nj + j))],
        out_shape=[jax.ShapeDtypeStruct((2, M, half * Ns), BF), jax.ShapeDtypeStruct((M, half * Ns), BF)],
        compiler_params=_cp("parallel", "parallel", "parallel"))(a, w, w)


def ffn_out_bwd(dff, w2, gu, name):
    M, D = dff.shape
    F = w2.shape[0]
    TM = _tile(M, (512, 256))
    CW = _tile(F, (256, 128))

    def body(d_ref, w_ref, gu_ref, o_ref):
        dv = d_ref[...]

        def product(c):
            return lax.dot_general(dv, w_ref[c:c + CW, :], (((1,), (1,)), ((), ())), preferred_element_type=F32)

        da = product(0)
        for c in range(0, F, CW):
            ahead = product(c + CW) if c + CW < F else None
            g = gu_ref[0, :, c:c + CW].astype(F32)
            u = gu_ref[1, :, c:c + CW].astype(F32)
            s = _sigmoid(g)
            o_ref[0, :, c:c + CW] = (da * u * (s + g * s * (1.0 - s))).astype(o_ref.dtype)
            o_ref[1, :, c:c + CW] = (da * g * s).astype(o_ref.dtype)
            da = ahead

    part = pl.BlockSpec((2, TM, F), lambda i: (0, i, 0))
    return pl.pallas_call(
        body, name=name, grid=(M // TM,),
        in_specs=[pl.BlockSpec((TM, D), lambda i: (i, 0)), pl.BlockSpec((F, D), lambda i: (0, 0)), part],
        out_specs=part, out_shape=jax.ShapeDtypeStruct((2, M, F), BF),
        compiler_params=_cp("parallel"))(dff, w2, gu)


def ssm_out_glu(z, w, x, mods, k_gate, name):
    M, K = z.shape
    S, _, Ns = w.shape
    half = S // 2
    TM = _tile(M, (1024, 512, 256))
    TN = _tile(Ns, (512, 256, 128))
    nj = Ns // TN

    def body(z_ref, wv_ref, wg_ref, x_ref, m_ref, o_ref, mix_ref, y_ref):
        zv = z_ref[...]
        CW = _tile(TN, (256, 128))

        def products(c):
            return (jnp.dot(zv, wv_ref[:, c:c + CW], preferred_element_type=F32),
                    jnp.dot(zv, wg_ref[:, c:c + CW], preferred_element_type=F32))

        cur = products(0)
        for c in range(0, TN, CW):
            ahead = products(c + CW) if c + CW < TN else None
            val, gate = cur
            o_ref[0, :, c:c + CW] = val.astype(o_ref.dtype)
            o_ref[1, :, c:c + CW] = gate.astype(o_ref.dtype)
            mix = val * _sigmoid(gate)
            mix_ref[:, c:c + CW] = mix.astype(mix_ref.dtype)
            y_ref[:, c:c + CW] = x_ref[:, c:c + CW] + m_ref[k_gate:k_gate + 1, c:c + CW] * mix
            cur = ahead

    col = lambda s, j, i: (i, s * nj + j)
    return pl.pallas_call(
        body, name=name, grid=(half, nj, M // TM),
        in_specs=[pl.BlockSpec((TM, K), lambda s, j, i: (i, 0)),
                  pl.BlockSpec((None, K, TN), lambda s, j, i: (s, 0, j)),
                  pl.BlockSpec((None, K, TN), lambda s, j, i: (s + half, 0, j)),
                  pl.BlockSpec((TM, TN), col), pl.BlockSpec((6, TN), lambda s, j, i: (0, s * nj + j))],
        out_specs=[pl.BlockSpec((2, TM, TN), lambda s, j, i: (0, i, s * nj + j)), pl.BlockSpec((TM, TN), col),
                   pl.BlockSpec((TM, TN), col)],
        out_shape=[jax.ShapeDtypeStruct((2, M, half * Ns), BF), jax.ShapeDtypeStruct((M, half * Ns), BF),
                   jax.ShapeDtypeStruct((M, half * Ns), F32)],
        compiler_params=_cp("parallel", "parallel", "parallel"))(z, w, w, x, mods)


def glu_bwd(dmix, o, name):
    _, L, D = o.shape
    TL = _tile(L, ROW_TILE)

    def body(d_ref, o_ref, do_ref):
        d = d_ref[...].astype(F32)
        val = o_ref[0].astype(F32)
        s = _sigmoid(o_ref[1].astype(F32))
        do_ref[0] = (d * s).astype(do_ref.dtype)
        do_ref[1] = (d * val * s * (1.0 - s)).astype(do_ref.dtype)

    part = pl.BlockSpec((2, TL, D), lambda i: (0, i, 0))
    return pl.pallas_call(body, name=name, grid=(L // TL,), in_specs=[_rows(TL, D), part],
                          out_specs=part, out_shape=jax.ShapeDtypeStruct((2, L, D), BF),
                          compiler_params=_cp("parallel"))(dmix, o)


def final_loss(x, target, gamma, f, fmods, k_gate, name):
    L, D = x.shape
    TL = _tile(L, ROW_TILE)

    def body(x_ref, t_ref, g_ref, f_ref, fm_ref, l_ref, dx_ref, s_ref, df_ref, acc, lacc):
        i = pl.program_id(0)

        @pl.when(i == 0)
        def _():
            acc[...] = jnp.zeros_like(acc)
            lacc[...] = jnp.zeros_like(lacc)

        xn, r = _norm_parts(x_ref[...])
        gam = g_ref[...]
        e = xn * gam - t_ref[...]
        lacc[...] += jnp.sum(0.5 * jnp.mean(e * e, axis=-1, keepdims=True), axis=0, keepdims=True)
        dy = e * (1.0 / D)
        dxn = dy * gam
        dx = r * (dxn - xn * jnp.mean(dxn * xn, axis=-1, keepdims=True))
        dx_ref[...] = dx
        df_ref[...] = (dx * fm_ref[k_gate:k_gate + 1, :]).astype(df_ref.dtype)
        acc[0] += _rowsum8(dy * xn)
        acc[1] += _rowsum8(dx * f_ref[...].astype(F32))

        @pl.when(i == pl.num_programs(0) - 1)
        def _():
            s_ref[...] = jnp.zeros_like(s_ref)
            for q in range(2):
                s_ref[q:q + 1, :] = jnp.sum(acc[q], axis=0, keepdims=True)
            l_ref[...] = jnp.broadcast_to(lacc[...], l_ref.shape)

    return pl.pallas_call(
        body, name=name, grid=(L // TL,),
        in_specs=[_rows(TL, D), _rows(TL, D), _fixed(1, D), _rows(TL, D), _fixed(6, D)],
        out_specs=[_fixed(8, 128), _rows(TL, D), _fixed(8, D), _rows(TL, D)],
        out_shape=[jax.ShapeDtypeStruct((8, 128), F32), jax.ShapeDtypeStruct((L, D), F32),
                   jax.ShapeDtypeStruct((8, D), F32), jax.ShapeDtypeStruct((L, D), BF)],
        scratch_shapes=[pltpu.VMEM((2, 8, D), F32), pltpu.VMEM((1, 1), F32)],
        compiler_params=_cp("arbitrary"))(x, target, gamma, f, fmods)


def _col(L, TC, off):
    return pl.BlockSpec((L, TC), lambda j: (0, off + j))


def _shift_down(v, k, row):
    return jnp.where(row >= k, pltpu.roll(v, k, 0), 0.0)


def _shift_up(v, k, row, L):
    return jnp.where(row < L - k, pltpu.roll(v, L - k, 0), 0.0)


def conv_fwd(p, w, name):
    L, D3 = p.shape
    D = D3 // 3
    TC = _tile(D, (128,))
    nc = D // TC

    def body(b_ref, c_ref, v_ref, w_ref, o_ref):
        row = lax.broadcasted_iota(jnp.int32, (L, TC), 0)
        cv = c_ref[...].astype(F32) * v_ref[...].astype(F32)
        conv = w_ref[2:3, :] * cv + w_ref[1:2, :] * _shift_down(cv, 1, row) + w_ref[0:1, :] * _shift_down(cv, 2, row)
        o_ref[...] = (b_ref[...].astype(F32) * conv).astype(o_ref.dtype)

    return pl.pallas_call(
        body, name=name, grid=(nc,),
        in_specs=[_col(L, TC, 0), _col(L, TC, nc), _col(L, TC, 2 * nc), pl.BlockSpec((3, TC), lambda j: (0, j))],
        out_specs=_col(L, TC, 0), out_shape=jax.ShapeDtypeStruct((L, D), BF), compiler_params=_cp("parallel"))(p, p, p, w)


def conv_bwd(dm, p, w, name):
    L, D3 = p.shape
    D = D3 // 3
    TC = _tile(D, (128,))
    nc = D // TC

    def body(dm_ref, b_ref, c_ref, v_ref, w_ref, db_ref, dc_ref, dv_ref, dw_ref):
        row = lax.broadcasted_iota(jnp.int32, (L, TC), 0)
        cg, vv = c_ref[...].astype(F32), v_ref[...].astype(F32)
        cv = cg * vv
        cv1, cv2 = _shift_down(cv, 1, row), _shift_down(cv, 2, row)
        conv = w_ref[2:3, :] * cv + w_ref[1:2, :] * cv1 + w_ref[0:1, :] * cv2
        dmv = dm_ref[...].astype(F32)
        db_ref[...] = (dmv * conv).astype(db_ref.dtype)
        dconv = dmv * b_ref[...].astype(F32)
        dcv = (w_ref[2:3, :] * dconv + w_ref[1:2, :] * _shift_up(dconv, 1, row, L)
               + w_ref[0:1, :] * _shift_up(dconv, 2, row, L))
        dc_ref[...] = (dcv * vv).astype(dc_ref.dtype)
        dv_ref[...] = (dcv * cg).astype(dv_ref.dtype)
        dw_ref[...] = jnp.zeros_like(dw_ref)
        dw_ref[0:1, :] = jnp.sum(dconv * cv2, axis=0, keepdims=True)
        dw_ref[1:2, :] = jnp.sum(dconv * cv1, axis=0, keepdims=True)
        dw_ref[2:3, :] = jnp.sum(dconv * cv, axis=0, keepdims=True)

    one = jax.ShapeDtypeStruct((L, D), BF)
    return pl.pallas_call(
        body, name=name, grid=(nc,),
        in_specs=[_col(L, TC, 0), _col(L, TC, 0), _col(L, TC, nc), _col(L, TC, 2 * nc),
                  pl.BlockSpec((3, TC), lambda j: (0, j))],
        out_specs=[_col(L, TC, 0), _col(L, TC, 0), _col(L, TC, 0), pl.BlockSpec((8, TC), lambda j: (0, j))],
        out_shape=[one, one, one, jax.ShapeDtypeStruct((8, D), F32)],
        compiler_params=_cp("parallel"))(dm, p, p, p, w)


def _gelu(y):
    return 0.5 * y * (1.0 + jnp.tanh(GELU_C * (y + GELU_A * y * y * y)))


def _gelu_grad(y):
    th = jnp.tanh(GELU_C * (y + GELU_A * y * y * y))
    return 0.5 * (1.0 + th) + 0.5 * y * (1.0 - th * th) * GELU_C * (1.0 + 3.0 * GELU_A * y * y)


def _cmul_add(br, bi, ar, ai, sr, si):
    return br + ar * sr - ai * si, bi + ar * si + ai * sr


def _log2(n):
    k = n.bit_length() - 1
    assert 1 << k == n
    return k


def _replicate(P2, W2, P, GLP, transposed):
    shape = (W2, P2) if transposed else (P2, W2)
    k = lax.broadcasted_iota(jnp.int32, shape, 1 if transposed else 0)
    c = lax.broadcasted_iota(jnp.int32, shape, 0 if transposed else 1)
    return ((k >> _log2(P)) == (c >> _log2(GLP))) & ((k & (P - 1)) == (c & (P - 1)))


def _on_diagonal(KB, W2, H, P, GLP, transposed):
    shape = (W2, KB) if transposed else (KB, W2)
    r = lax.broadcasted_iota(jnp.int32, shape, 1 if transposed else 0)
    c = lax.broadcasted_iota(jnp.int32, shape, 0 if transposed else 1)
    return (r >> _log2(H)) == ((c & (GLP - 1)) >> _log2(P))


def _expand(t, dims, transposed):
    KB, W2, H, P, GLP = dims
    rep = _replicate(2 * P, W2, P, GLP, transposed).astype(t.dtype)
    wide = jnp.dot(rep, t, preferred_element_type=F32) if transposed else jnp.dot(t, rep, preferred_element_type=F32)
    return jnp.where(_on_diagonal(KB, W2, H, P, GLP, transposed), wide, 0.0).astype(t.dtype)


def _extract(acc, dims):
    KB, W2, H, P, GLP = dims
    rep = _replicate(2 * P, W2, P, GLP, True).astype(BF)
    kept = jnp.where(_on_diagonal(KB, W2, H, P, GLP, False), acc, 0.0)
    hi = kept.astype(BF)
    lo = (kept - hi.astype(F32)).astype(BF)
    return jnp.dot(hi, rep, preferred_element_type=F32) + jnp.dot(lo, rep, preferred_element_type=F32)


def _cmul(ar, ai, sr, si):
    return ar * sr - ai * si, ar * si + ai * sr


def _chunk_order(TL, CH, transposed):
    out_row = lax.broadcasted_iota(jnp.int32, (TL, TL), 1 if transposed else 0)
    in_row = lax.broadcasted_iota(jnp.int32, (TL, TL), 0 if transposed else 1)
    return in_row == ((out_row & 7) << _log2(CH)) + (out_row >> 3)


def _reorder(perm, v):
    hi = v.astype(perm.dtype)
    lo = (v - hi.astype(F32)).astype(perm.dtype)
    return jnp.dot(perm, hi, preferred_element_type=F32) + jnp.dot(perm, lo, preferred_element_type=F32)


def _interleave(main, side):
    n, m, k = len(main), len(side), 0
    for i, step in enumerate(main):
        step()
        while k < m and (k + 1) * n <= (i + 1) * m:
            side[k]()
            k += 1
    for step in side[k:]:
        step()


S5_CHUNK = 512


def s5_fwd(h, tb, tct, pw, dvec, name):
    L, D = h.shape
    nkb, KB, P2 = tb.shape
    P = P2 // 2
    W = (KB // SSM_GROUP) * P
    W2 = 2 * W
    dims = (KB, W2, SSM_GROUP, P, W)
    TL = _tile(L, (512, 256))
    CH = TL // 8
    NB = 2 if nkb % 2 == 0 else 1
    CK = min(S5_CHUNK, W2)

    def body(h_ref, tb_ref, tct_ref, pw_ref, d_ref, s_ref, y_ref, z_ref, bw, cw, perm, unperm, carry):
        t = pl.program_id(1)

        @pl.when(t == 0)
        def _():
            carry[...] = jnp.zeros_like(carry)
            for b in range(NB):
                bw[b] = _expand(tb_ref[b], dims, False)
                cw[b] = _expand(tct_ref[b], dims, True)
            perm[...] = _chunk_order(TL, CH, False).astype(perm.dtype)
            unperm[...] = _chunk_order(TL, CH, True).astype(perm.dtype)

        hp = _reorder(perm[...], h_ref[...])
        hpb = hp.astype(BF)
        first = lax.broadcasted_iota(jnp.int32, (8, W), 0) == 0

        def project(b):
            def chunk(c):
                def emit():
                    s_ref[:, b * W2 + c:b * W2 + c + CK] = jnp.dot(hpb[:, b * KB:(b + 1) * KB], bw[b, :, c:c + CK],
                                                                   preferred_element_type=F32)
                return emit
            return [chunk(c) for c in range(0, W2, CK)]

        def scan(b):
            re, im = slice(b * W2, b * W2 + W), slice(b * W2 + W, (b + 1) * W2)
            ar, ai = pw_ref[b, 0:8, :W], pw_ref[b, 0:8, W:]
            st = {"x": (jnp.zeros((8, W), F32), jnp.zeros((8, W), F32))}

            def own(j):
                def emit():
                    rows = slice(j * 8, j * 8 + 8)
                    xr, xi = _cmul_add(s_ref[rows, re], s_ref[rows, im], ar, ai, *st["x"])
                    s_ref[rows, re] = xr
                    s_ref[rows, im] = xi
                    st["x"] = (xr, xi)
                return emit

            def ends():
                xr, xi = st["x"]
                for k, off in ((1, 8), (2, 16), (4, 24)):
                    xr, xi = _cmul_add(xr, xi, pw_ref[b, off:off + 8, :W], pw_ref[b, off:off + 8, W:],
                                       pltpu.roll(xr, k, 0), pltpu.roll(xi, k, 0))
                xr, xi = _cmul_add(xr, xi, pw_ref[b, 32:40, :W], pw_ref[b, 32:40, W:], carry[b, 0], carry[b, 1])
                st["c"] = (jnp.where(first, carry[b, 0], pltpu.roll(xr, 1, 0)),
                           jnp.where(first, carry[b, 1], pltpu.roll(xi, 1, 0)))
                carry[b, 0] = jnp.broadcast_to(xr[7:8], (8, W))
                carry[b, 1] = jnp.broadcast_to(xi[7:8], (8, W))

            def carried(j):
                def emit():
                    rows = slice(j * 8, j * 8 + 8)
                    cr, ci = _cmul(ar, ai, *st["c"])
                    s_ref[rows, re] = s_ref[rows, re] + cr
                    s_ref[rows, im] = s_ref[rows, im] + ci
                    st["c"] = (cr, ci)
                return emit

            return [own(j) for j in range(CH)] + [ends] + [carried(j) for j in range(CH)]

        def readout(b):
            cols = slice(b * KB, (b + 1) * KB)
            acc = {}

            def chunk(c):
                def emit():
                    part = jnp.dot(s_ref[:, b * W2 + c:b * W2 + c + CK].astype(BF), cw[b, c:c + CK, :],
                                   preferred_element_type=F32)
                    acc["y"] = part if c == 0 else acc["y"] + part
                return emit

            def finish():
                y = acc["y"] + d_ref[:, cols] * hp[:, cols]
                y_ref[:, cols] = y
                z_ref[:, cols] = jnp.dot(unperm[...], _gelu(y).astype(BF),
                                         preferred_element_type=F32).astype(z_ref.dtype)

            return [chunk(c) for c in range(0, W2, CK)] + [finish]

        for emit in project(0):
            emit()
        for b in range(NB):
            side = (project(b + 1) if b + 1 < NB else []) + (readout(b - 1) if b > 0 else [])
            _interleave(scan(b), side)
        for emit in readout(NB - 1):
            emit()

    blk = lambda kb, t: (t, kb)
    per_kb = lambda kb, t: (kb, 0, 0)
    return pl.pallas_call(
        body, name=name, grid=(nkb // NB, L // TL),
        in_specs=[pl.BlockSpec((TL, NB * KB), blk), pl.BlockSpec((NB, KB, P2), per_kb),
                  pl.BlockSpec((NB, P2, KB), per_kb), pl.BlockSpec((NB, 40, W2), per_kb),
                  pl.BlockSpec((1, NB * KB), lambda kb, t: (0, kb))],
        out_specs=[pl.BlockSpec((TL, NB * W2), blk), pl.BlockSpec((TL, NB * KB), blk),
                   pl.BlockSpec((TL, NB * KB), blk)],
        out_shape=[jax.ShapeDtypeStruct((L, nkb * W2), F32), jax.ShapeDtypeStruct((L, D), F32),
                   jax.ShapeDtypeStruct((L, D), BF)],
        scratch_shapes=[pltpu.VMEM((NB, KB, W2), BF), pltpu.VMEM((NB, W2, KB), BF), pltpu.VMEM((TL, TL), BF),
                        pltpu.VMEM((TL, TL), BF), pltpu.VMEM((NB, 2, 8, W), F32)],
        compiler_params=_cp("parallel", "arbitrary"))(h, tb, tct, pw, dvec)


def s5_bwd(dz, y, h, s, tc, tbt, pwr, dvec, name):
    L, D = h.shape
    nkb, KB, P2 = tc.shape
    P = P2 // 2
    W = (KB // SSM_GROUP) * P
    W2 = 2 * W
    dims = (KB, W2, SSM_GROUP, P, W)
    TL = _tile(L, (512, 256))
    CH = TL // 8
    nt = L // TL
    NB = 2 if nkb % 2 == 0 else 1
    CK = min(S5_CHUNK, W2)
    tn = (((0,), (0,)), ((), ()))

    def body(dz_ref, y_ref, h_ref, s_ref, sp_ref, tc_ref, tbt_ref, pw_ref, d_ref,
             dh_ref, dd_ref, da_ref, db_ref, dc_ref, g, ctw, btw, dbacc, dcacc, dys, perm, unperm, carry):
        t = pl.program_id(1)

        @pl.when(t == 0)
        def _():
            carry[...] = jnp.zeros_like(carry)
            dd_ref[...] = jnp.zeros_like(dd_ref)
            da_ref[...] = jnp.zeros_like(da_ref)
            dbacc[...] = jnp.zeros_like(dbacc)
            dcacc[...] = jnp.zeros_like(dcacc)
            for b in range(NB):
                ctw[b] = _expand(tc_ref[b], dims, False)
                btw[b] = _expand(tbt_ref[b], dims, True)
            perm[...] = _chunk_order(TL, CH, False).astype(perm.dtype)
            unperm[...] = _chunk_order(TL, CH, True).astype(perm.dtype)

        hp = jnp.dot(perm[...], h_ref[...].astype(BF), preferred_element_type=F32)
        dy = jnp.dot(perm[...], dz_ref[...].astype(BF), preferred_element_type=F32) * _gelu_grad(y_ref[...])
        dd_ref[...] += _rowsum8(dy * hp)
        dys[...] = dy
        dyb = dy.astype(BF)
        hpb = hp.astype(BF)
        sub = lax.broadcasted_iota(jnp.int32, (8, W), 0)
        live = jnp.where(t == nt - 1, 0.0, 1.0)

        def lead(b):
            cols = slice(b * KB, (b + 1) * KB)

            def to_states(c):
                def emit():
                    g[b, :, c:c + CK] = jnp.dot(dyb[:, cols], ctw[b, :, c:c + CK], preferred_element_type=F32)
                return emit

            def d_c(c):
                def emit():
                    dcacc[b, :, c:c + CK] += lax.dot_general(dyb[:, cols],
                                                             s_ref[:, b * W2 + c:b * W2 + c + CK].astype(BF), tn,
                                                             preferred_element_type=F32)
                return emit

            return [f(c) for c in range(0, W2, CK) for f in (to_states, d_c)]

        def scan(b):
            re, im = slice(b * W2, b * W2 + W), slice(b * W2 + W, (b + 1) * W2)
            ar, ai = pw_ref[b, 0:8, :W], pw_ref[b, 0:8, W:]
            zero = jnp.zeros((8, W), F32)
            st = {"g": (zero, zero), "acc": (zero, zero)}

            def own(j):
                def emit():
                    rows = slice(j * 8, j * 8 + 8)
                    gr, gi = _cmul_add(g[b, rows, :W], g[b, rows, W:], ar, ai, *st["g"])
                    g[b, rows, :W] = gr
                    g[b, rows, W:] = gi
                    st["g"] = (gr, gi)
                return emit

            def ends():
                gr, gi = st["g"]
                for k, off in ((1, 8), (2, 16), (4, 24)):
                    gr, gi = _cmul_add(gr, gi, pw_ref[b, off:off + 8, :W], pw_ref[b, off:off + 8, W:],
                                       pltpu.roll(gr, 8 - k, 0), pltpu.roll(gi, 8 - k, 0))
                gr, gi = _cmul_add(gr, gi, pw_ref[b, 32:40, :W], pw_ref[b, 32:40, W:], carry[b, 0], carry[b, 1])
                st["c"] = (jnp.where(sub == 7, carry[b, 0], pltpu.roll(gr, 7, 0)),
                           jnp.where(sub == 7, carry[b, 1], pltpu.roll(gi, 7, 0)))
                carry[b, 0] = jnp.broadcast_to(gr[0:1], (8, W))
                carry[b, 1] = jnp.broadcast_to(gi[0:1], (8, W))

            def carried(j):
                def emit():
                    rows = slice(j * 8, j * 8 + 8)
                    cr, ci = _cmul(ar, ai, *st["c"])
                    gr, gi = g[b, rows, :W] + cr, g[b, rows, W:] + ci
                    g[b, rows, :W] = gr
                    g[b, rows, W:] = gi
                    if j > 0:
                        before = slice(j * 8 - 8, j * 8)
                        pr, pi = s_ref[before, re], s_ref[before, im]
                    else:
                        last = slice(TL - 8, TL)
                        pr = jnp.where(sub == 0, sp_ref[7:8, re] * live, pltpu.roll(s_ref[last, re], 1, 0))
                        pi = jnp.where(sub == 0, sp_ref[7:8, im] * live, pltpu.roll(s_ref[last, im], 1, 0))
                    accr, acci = st["acc"]
                    st["c"] = (cr, ci)
                    st["acc"] = (accr + pr * gr + pi * gi, acci + pr * gi - pi * gr)
                return emit

            def done():
                da_ref[b, :, :W] += st["acc"][0]
                da_ref[b, :, W:] += st["acc"][1]

            return ([own(j) for j in reversed(range(CH))] + [ends] + [carried(j) for j in reversed(range(CH))]
                    + [done])

        def tail(b):
            cols = slice(b * KB, (b + 1) * KB)
            acc = {}

            def d_u(c):
                def emit():
                    part = jnp.dot(g[b, :, c:c + CK].astype(BF), btw[b, c:c + CK, :], preferred_element_type=F32)
                    acc["u"] = part if c == 0 else acc["u"] + part
                return emit

            def d_b(c):
                def emit():
                    dbacc[b, :, c:c + CK] += lax.dot_general(hpb[:, cols], g[b, :, c:c + CK].astype(BF), tn,
                                                             preferred_element_type=F32)
                return emit

            def finish():
                dh = (dys[:, cols] * d_ref[:, cols] + acc["u"]).astype(BF)
                dh_ref[:, cols] = jnp.dot(unperm[...], dh, preferred_element_type=F32).astype(dh_ref.dtype)

            return [f(c) for c in range(0, W2, CK) for f in (d_u, d_b)] + [finish]

        for emit in lead(0):
            emit()
        for b in range(NB):
            side = (lead(b + 1) if b + 1 < NB else []) + (tail(b - 1) if b > 0 else [])
            _interleave(scan(b), side)
        for emit in tail(NB - 1):
            emit()

        @pl.when(t == nt - 1)
        def _():
            for b in range(NB):
                db_ref[b] = _extract(dbacc[b], dims)
                dc_ref[b] = _extract(dcacc[b], dims)

    rev = lambda kb, t: (nt - 1 - t, kb)
    prev = lambda kb, t: (jnp.maximum((nt - 1 - t) * CH - 1, 0), kb)
    per_kb = lambda kb, t: (kb, 0, 0)
    return pl.pallas_call(
        body, name=name, grid=(nkb // NB, nt),
        in_specs=[pl.BlockSpec((TL, NB * KB), rev), pl.BlockSpec((TL, NB * KB), rev),
                  pl.BlockSpec((TL, NB * KB), rev), pl.BlockSpec((TL, NB * W2), rev),
                  pl.BlockSpec((8, NB * W2), prev), pl.BlockSpec((NB, KB, P2), per_kb),
                  pl.BlockSpec((NB, P2, KB), per_kb), pl.BlockSpec((NB, 40, W2), per_kb),
                  pl.BlockSpec((1, NB * KB), lambda kb, t: (0, kb))],
        out_specs=[pl.BlockSpec((TL, NB * KB), rev), pl.BlockSpec((8, NB * KB), lambda kb, t: (0, kb)),
                   pl.BlockSpec((NB, 8, W2), per_kb), pl.BlockSpec((NB, KB, P2), per_kb),
                   pl.BlockSpec((NB, KB, P2), per_kb)],
        out_shape=[jax.ShapeDtypeStruct((L, D), BF), jax.ShapeDtypeStruct((8, D), F32),
                   jax.ShapeDtypeStruct((nkb, 8, W2), F32), jax.ShapeDtypeStruct((nkb, KB, P2), F32),
                   jax.ShapeDtypeStruct((nkb, KB, P2), F32)],
        scratch_shapes=[pltpu.VMEM((NB, TL, W2), F32), pltpu.VMEM((NB, KB, W2), BF), pltpu.VMEM((NB, W2, KB), BF),
                        pltpu.VMEM((NB, KB, W2), F32), pltpu.VMEM((NB, KB, W2), F32), pltpu.VMEM((TL, NB * KB), F32),
                        pltpu.VMEM((TL, TL), BF), pltpu.VMEM((TL, TL), BF), pltpu.VMEM((NB, 2, 8, W), F32)],
        compiler_params=pltpu.CompilerParams(dimension_semantics=("parallel", "arbitrary"),
                                             vmem_limit_bytes=V7X_VMEM_BYTES - 4 * 1024 * 1024),
    )(dz, y, h, s, s, tc, tbt, pwr, dvec)


def _discretise(a_re, a_im, log_step, b_re, b_im):
    lr = jnp.minimum(a_re, -1e-4)
    li = a_im
    dt = jnp.exp(log_step)[:, None]
    mag = jnp.exp(lr * dt)
    abr = mag * jnp.cos(li * dt)
    abi = mag * jnp.sin(li * dt)
    den = lr * lr + li * li
    qr = ((abr - 1.0) * lr + abi * li) / den
    qi = (abi * lr - (abr - 1.0) * li) / den
    bbar_re = qr[..., None] * b_re - qi[..., None] * b_im
    bbar_im = qr[..., None] * b_im + qi[..., None] * b_re
    return abr, abi, bbar_re, bbar_im


def _compact(m_re, m_im, nkb):
    G, H, P = m_re.shape
    t = jnp.stack([m_re, m_im], axis=2).reshape(nkb, (G // nkb) * H, 2 * P).astype(BF)
    return t, jnp.swapaxes(t, 1, 2)


def _scan_powers(abr, abi, nkb, conj, CH):
    G, P = abr.shape
    if conj:
        abi = -abi

    def cmul(u, v):
        return u[0] * v[0] - u[1] * v[1], u[0] * v[1] + u[1] * v[0]

    q = (abr, abi)
    for _ in range(_log2(CH)):
        q = cmul(q, q)
    pows = [q]
    for _ in range(7):
        pows.append(cmul(pows[-1], q))
    row = jnp.arange(8)[:, None, None]

    def table(part):
        out = [jnp.broadcast_to((abr, abi)[part][None], (8, G, P))]
        for k in (1, 2, 4):
            keep = (row <= 7 - k) if conj else (row >= k)
            out.append(jnp.where(keep, pows[k - 1][part][None], 0.0))
        ends = jnp.stack([p[part] for p in pows])
        out.append(ends[::-1] if conj else ends)
        return jnp.concatenate(out, axis=0)

    GL = G // nkb
    t = jnp.stack([table(0), table(1)], axis=1)
    t = t.reshape(40, 2, nkb, GL * P).transpose(2, 0, 1, 3)
    return t.reshape(nkb, 40, 2 * GL * P)


def ada_mods(c_all, w_ada, b_sh, name):
    nl, D, NA = w_ada.shape

    def body(c_ref, w_ref, b_ref, o_ref):
        cv = c_ref[...]
        act = cv * _sigmoid(cv)
        o_ref[...] = jnp.dot(act, w_ref[...], preferred_element_type=F32, precision=lax.Precision.HIGHEST) + b_ref[...]

    return pl.pallas_call(
        body, name=name, grid=(nl,),
        in_specs=[pl.BlockSpec((8, D), lambda i: (0, 0)), pl.BlockSpec((None, D, NA), lambda i: (i, 0, 0)),
                  pl.BlockSpec((None, 1, NA), lambda i: (i, 0, 0))],
        out_specs=pl.BlockSpec((None, 8, NA), lambda i: (i, 0, 0)),
        out_shape=jax.ShapeDtypeStruct((nl, 8, NA), F32), compiler_params=_cp("parallel"))(c_all, w_ada, b_sh)


def _adamw(w, g, m, v):
    m = ADAM_B1 * m + (1.0 - ADAM_B1) * g
    v = ADAM_B2 * v + (1.0 - ADAM_B2) * (g * g)
    m_hat = m / (1.0 - ADAM_B1 ** ADAM_STEP)
    v_hat = v / (1.0 - ADAM_B2 ** ADAM_STEP)
    return -ADAM_LR * (m_hat / (jnp.sqrt(v_hat) + ADAM_EPS) + ADAM_WD * w), m, v


def _adam_rows(R, C):
    cap = max(8, (256 * 1024) // C)
    for t in range(min(R, cap), 0, -1):
        if R % t == 0 and (t % 8 == 0 or t == R):
            return t
    return R


def adamw_ada(c_t, dm, w, m, v, name):
    nl, D, NA = w.shape
    TK = _tile(D, (256, 128))

    def body(c_ref, dm_ref, w_ref, m_ref, v_ref, g_ref, d_ref, nm_ref, nv_ref):
        cv = c_ref[...]
        act = cv * _sigmoid(cv)
        g = jnp.dot(act, dm_ref[...], preferred_element_type=F32, precision=lax.Precision.HIGHEST)
        g_ref[...] = g
        d_ref[...], nm_ref[...], nv_ref[...] = _adamw(w_ref[...], g, m_ref[...], v_ref[...])

    big = pl.BlockSpec((None, TK, NA), lambda i, k: (i, k, 0))
    shape = jax.ShapeDtypeStruct(w.shape, F32)
    return pl.pallas_call(
        body, name=name, grid=(nl, D // TK),
        in_specs=[pl.BlockSpec((TK, 8), lambda i, k: (k, 0)), pl.BlockSpec((None, 8, NA), lambda i, k: (i, 0, 0)),
                  big, big, big],
        out_specs=[big] * 4, out_shape=[shape] * 4, compiler_params=_cp("parallel", "parallel"))(c_t, dm, w, m, v)


def adamw_sharded(w, m, v, ga, gb, name):
    nl, R, C = w.shape
    TR = _adam_rows(R, C)

    def body(w_ref, m_ref, v_ref, a_ref, b_ref, g_ref, d_ref, nm_ref, nv_ref):
        g = a_ref[...] + b_ref[...]
        g_ref[...] = g
        d_ref[...], nm_ref[...], nv_ref[...] = _adamw(w_ref[...], g, m_ref[...], v_ref[...])

    big = pl.BlockSpec((None, TR, C), lambda i, r: (i, r, 0))
    shape = jax.ShapeDtypeStruct(w.shape, F32)
    return pl.pallas_call(
        body, name=name, grid=(nl, R // TR), in_specs=[big] * 5,
        out_specs=[big] * 4, out_shape=[shape] * 4, compiler_params=_cp("parallel", "parallel"))(w, m, v, ga, gb)


def adamw_slab(g, w, m, v, name):
    R, C = g.shape
    TR = _tile(R, (160, 80, 40, 8))

    def body(g_ref, w_ref, m_ref, v_ref, d_ref, nm_ref, nv_ref):
        d_ref[...], nm_ref[...], nv_ref[...] = _adamw(w_ref[...], g_ref[...], m_ref[...], v_ref[...])

    big = pl.BlockSpec((TR, C), lambda r: (r, 0))
    shape = jax.ShapeDtypeStruct((R, C), F32)
    return pl.pallas_call(
        body, name=name, grid=(R // TR,), in_specs=[big] * 4,
        out_specs=[big] * 3, out_shape=[shape] * 3, compiler_params=_cp("parallel"))(g, w, m, v)


def adamw_plain(w, m, v, g, name):
    def body(w_ref, m_ref, v_ref, g_ref, d_ref, nm_ref, nv_ref):
        d_ref[...], nm_ref[...], nv_ref[...] = _adamw(w_ref[...], g_ref[...], m_ref[...], v_ref[...])

    shape = jax.ShapeDtypeStruct(w.shape, F32)
    return pl.pallas_call(body, name=name, out_shape=[shape] * 3,
                          compiler_params=pltpu.CompilerParams(vmem_limit_bytes=VMEM_LIMIT))(w, m, v, g)


def _slab_rows(a):
    n = a.size
    rows = -(-n // SLAB_W)
    return -(-rows // 8) * 8


def _pack(arrs, pad_rows_to=0):
    out = []
    for a in arrs:
        rows = _slab_rows(a)
        flat = a.reshape(-1).astype(F32)
        flat = jnp.pad(flat, (0, rows * SLAB_W - flat.shape[0]))
        out.append(flat.reshape(rows, SLAB_W))
    total = sum(o.shape[0] for o in out)
    if pad_rows_to and total % pad_rows_to:
        out.append(jnp.zeros((pad_rows_to - total % pad_rows_to, SLAB_W), F32))
    return jnp.concatenate(out, axis=0)


def _unpack(slab, like):
    out, r = [], 0
    for a in like:
        rows = _slab_rows(a)
        out.append(slab[r:r + rows].reshape(-1)[:a.size].reshape(a.shape))
        r += rows
    return out


WEIGHTS = ['norm1_g', 'norm2_g', 'w_ada', 'b_ada', 'ssm_a_re', 'ssm_a_im', 'ssm_log_step', 'ssm_b_re', 'ssm_b_im',
           'ssm_c_re', 'ssm_c_im', 'ssm_d', 'ssm_w_out', 'conv_w_in', 'conv_w', 'conv_w_out', 'w_ffn_in',
           'w_ffn_out', 'final_g']
SLAB = ['norm1_g', 'norm2_g', 'b_ada', 'ssm_a_re', 'ssm_a_im', 'ssm_log_step', 'ssm_b_re', 'ssm_b_im', 'ssm_c_re',
        'ssm_c_im', 'ssm_d', 'final_g']
SHARDED = ['ssm_w_out', 'conv_w_in', 'conv_w_out', 'w_ffn_in', 'w_ffn_out']


def kernel(x, c, norm1_g, norm2_g, w_ada, b_ada, ssm_a_re, ssm_a_im, ssm_log_step, ssm_b_re, ssm_b_im, ssm_c_re, ssm_c_im, ssm_d, ssm_w_out, conv_w_in, conv_w, conv_w_out, w_ffn_in, w_ffn_out, final_g, loss_target, m_norm1_g, m_norm2_g, m_w_ada, m_b_ada, m_ssm_a_re, m_ssm_a_im, m_ssm_log_step, m_ssm_b_re, m_ssm_b_im, m_ssm_c_re, m_ssm_c_im, m_ssm_d, m_ssm_w_out, m_conv_w_in, m_conv_w, m_conv_w_out, m_w_ffn_in, m_w_ffn_out, m_final_g, v_norm1_g, v_norm2_g, v_w_ada, v_b_ada, v_ssm_a_re, v_ssm_a_im, v_ssm_log_step, v_ssm_b_re, v_ssm_b_im, v_ssm_c_re, v_ssm_c_im, v_ssm_d, v_ssm_w_out, v_conv_w_in, v_conv_w, v_conv_w_out, v_w_ffn_in, v_w_ffn_out, v_final_g):
    given = dict(locals())
    W = {n: given[n] for n in WEIGHTS}
    Mo = {n: given["m_" + n] for n in WEIGHTS}
    Vo = {n: given["v_" + n] for n in WEIGHTS}

    xs = x[0]
    tgt = loss_target[0]
    L, D = xs.shape
    nlayer = norm1_g.shape[0]
    NA = w_ada.shape[2]
    G = ssm_a_re.shape[1]
    nkb = D // S5_BLOCK
    ax, ay, ac = _axes()
    me = 4 * ax + 2 * ay + ac
    chip = 2 * ax + ay

    assert D == SLAB_W
    first = gather8(jnp.concatenate([jnp.broadcast_to(c, (8, D)), _pack([conv_w])], axis=0), "gather_c_conv_w")
    c_all = first[:, 0, :]
    b_sh = lax.dynamic_slice_in_dim(b_ada, chip * NA, NA, axis=1)[:, None, :]
    mods_part = ada_mods(c_all, w_ada, b_sh, "ada_mods")
    mg = gather8(mods_part.reshape(nlayer * 8, NA), "gather_mods")
    mg = mg.reshape(N_CHIP, 2, nlayer, 8, NA)[:, 0]
    mods_all = lax.dynamic_index_in_dim(mg, me, axis=2, keepdims=False)
    mods_all = jnp.transpose(mods_all, (1, 0, 2)).reshape(nlayer, 6, D)

    cw_parts = first[:, 8:]
    nconv = conv_w.shape[0]
    cw_full = jnp.stack([_unpack(cw_parts[2 * q], [conv_w])[0] for q in range(N_CHIP)], axis=2)
    cw_full = cw_full.reshape(nconv, 3, D)

    in_flight_w = {}

    def start_weights(i, after):
        names = (["ssm_w_out"] if i % 2 == 0 else ["conv_w_in", "conv_w_out"]) + ["w_ffn_in", "w_ffn_out"]
        shards = [W[n][i if n.startswith("w_ffn") else i // 2].astype(BF) for n in names]
        sems, srcs, lands, tok = gather_start(shards, after, "gather_start%d" % i)
        in_flight_w[i] = (names, sems, srcs, lands)
        return tok

    def relay_weights(i, after):
        names, sems, srcs, lands = in_flight_w[i]
        got = gather_wait(sems, srcs, lands, list(range(len(names))), after, "gather_wait%d" % i)
        rsems, rlands, tok = relay_start(got, after, "relay_start%d" % i)
        in_flight_w[i] = (names, rsems, rlands)
        return tok

    def layer_weights(i, after):
        names, rsems, rlands = in_flight_w[i]
        return dict(zip(names, relay_wait(rsems, rlands, after, "relay_wait%d" % i)))

    token = start_weights(0, cw_full + mods_all[0, 0:3])
    mods_all = mods_all + token[0:1, 0:1]

    s5 = []
    for j in range(ssm_a_re.shape[0]):
        disc, disc_vjp = jax.vjp(_discretise, ssm_a_re[j], ssm_a_im[j], ssm_log_step[j], ssm_b_re[j], ssm_b_im[j])
        abr, abi, bbar_re, bbar_im = disc
        tb, tbt = _compact(jnp.swapaxes(bbar_re, 1, 2), jnp.swapaxes(bbar_im, 1, 2), nkb)
        tc, tct = _compact(ssm_c_re[j], -ssm_c_im[j], nkb)
        chunk = _tile(L, (512, 256)) // 8
        s5.append(dict(vjp=disc_vjp, tb=tb, tbt=tbt, tc=tc, tct=tct, pw=_scan_powers(abr, abi, nkb, False, chunk),
                       pwr=_scan_powers(abr, abi, nkb, True, chunk)))

    saved = []
    xcur = xs
    for i in range(nlayer):
        j = i // 2
        mods = mods_all[i]
        sv = dict(x=xcur)
        if i % 2 == 0:
            h = norm_mod(xcur, norm1_g[i:i + 1], mods, 0, F32, "norm_mod_s5")
            dvec = ssm_d[j:j + 1]
            if i == 0:
                dvec = dvec + start_weights(1, h)[0:1, 0:1]
            states, yv, z = s5_fwd(h, s5[j]["tb"], s5[j]["tct"], s5[j]["pw"], dvec, "s5_fwd")
            if i == 0:
                mods = mods + relay_weights(0, z)[0:1, 0:1]
            full = layer_weights(i, z)
            o, mix, x2 = ssm_out_glu(z, full["ssm_w_out"], xcur, mods, 2, "ssm_out_glu")
            sv.update(h=h, states=states, y=yv, z=z, o=o)
        else:
            h = norm_mod(xcur, norm1_g[i:i + 1], mods, 0, BF, "norm_mod")
            full = layer_weights(i, h)
            p = mm_nn(h, full["conv_w_in"], BF, "mm_conv_in")
            mc = conv_fwd(p, cw_full[j], "conv_fwd")
            mix, x2 = mm_nn(mc, full["conv_w_out"].reshape(1, D, D), BF, "mm_conv_out", res=xcur, gate=mods[2:3])
            sv.update(h=h, p=p, mc=mc)
        h2 = norm_mod(x2, norm2_g[i:i + 1], mods, 3, BF, "norm_mod")
        gu, act = ffn_in_act(h2, full["w_ffn_in"], "ffn_in_act")
        F = act.shape[1]
        ff, x3 = mm_nn(act, full["w_ffn_out"].reshape(1, F, D), BF, "mm_ffn_out", res=x2, gate=mods[5:6])
        sv.update(mix=mix, x2=x2, h2=h2, gu=gu, act=act, ff=ff, w=full)
        saved.append(sv)
        xcur = x3
        if i + 1 < nlayer:
            token = relay_weights(i + 1, ff)
            if i + 2 < nlayer:
                token = token + start_weights(i + 2, token)
            mods_all = mods_all + token[0:1, 0:1]

    loss_blk, dx, dfinal, dff = final_loss(xcur, tgt, final_g[None, :], saved[-1]["ff"], mods_all[nlayer - 1], 5,
                                           "final_loss")
    dg2 = dfinal[1:2]

    gland = {n: lax.empty((W[n].shape[0], N_CHIP) + W[n].shape[1:], BF) for n in SHARDED}
    in_flight = []
    dmods = [None] * nlayer
    dnorm1, dnorm2 = [None] * nlayer, [None] * nlayer
    dconv_w = [None] * nconv
    ds5 = [None] * ssm_a_re.shape[0]
    token = jnp.zeros((8, 128), F32)

    def send_grads(names, grads, slot, after, name):
        sems, thru, lands, tok = scatter_start([grads[n] for n in names], [gland[n] for n in names], slot, after, name)
        gland.update(zip(names, lands))
        in_flight.append((names, slot, sems, thru, name))
        return tok

    def land_grads(group, after):
        for names, slot, sems, thru, name in in_flight:
            if names[0] in group:
                got = scatter_wait(sems, thru, [gland[n] for n in names], slot, after, name.replace("scatter", "landed"))
                gland.update(zip(names, got))

    for i in reversed(range(nlayer)):
        j = i // 2
        mods = mods_all[i] + token[0:1, 0:1]
        sv = saved[i]
        full = sv["w"]
        gfull = {}
        F = sv["act"].shape[1]
        gfull["w_ffn_out"] = mm_tn(sv["act"], dff, 1, "mm_tn_ffn_out").reshape(N_CHIP, F // N_CHIP, D)
        dgu = ffn_out_bwd(dff, full["w_ffn_out"].reshape(F, D), sv["gu"], "ffn_out_bwd")
        gfull["w_ffn_in"] = mm_tn(sv["h2"], dgu, N_CHIP, "mm_tn_ffn_in")
        dh2 = mm_nt(dgu, full["w_ffn_in"], BF, "mm_nt_ffn_in")
        token = send_grads(["w_ffn_out", "w_ffn_in"], gfull, [i, i], dh2, "scatter_ffn%d" % i)
        mods = mods + token[0:1, 0:1]
        dx2, s2, dmix = norm_bwd(dh2, sv["x2"], dx, norm2_g[i:i + 1], mods, 3, "norm_bwd_mix",
                                 branch=(sv["mix"], mods, 2))
        dg1 = s2[3:4]
        if i % 2 == 0:
            do = glu_bwd(dmix, sv["o"], "glu_bwd")
            gfull["ssm_w_out"] = mm_tn(sv["z"], do, N_CHIP, "mm_tn_ssm_out")
            dz = mm_nt(do, full["ssm_w_out"], BF, "mm_nt_ssm_out")
            dh, dd, dab, db, dc = s5_bwd(dz, sv["y"], sv["h"], sv["states"], s5[j]["tc"], s5[j]["tbt"], s5[j]["pwr"],
                                         ssm_d[j:j + 1], "s5_bwd")
            ds5[j] = (dd, dab, db, dc)
        else:
            gfull["conv_w_out"] = mm_tn(sv["mc"], dmix, 1, "mm_tn_conv_out").reshape(N_CHIP, D // N_CHIP, D)
            dmc = mm_nt(dmix, full["conv_w_out"].reshape(1, D, D), BF, "mm_nt_conv_out")
            dbg, dcg, dvv, dcw = conv_bwd(dmc, sv["p"], cw_full[j], "conv_bwd")
            dp = jnp.concatenate([dbg, dcg, dvv], axis=1)
            gfull["conv_w_in"] = mm_tn(sv["h"], dp, N_CHIP, "mm_tn_conv_in")
            dh = mm_nt(dp, full["conv_w_in"], BF, "mm_nt_conv_in")
            dconv_w[j] = dcw[0:3]
        dmods_i = [s2[0:2], dg2]
        if i > 0:
            dx, s1, dff = norm_bwd(dh, sv["x"], dx2, norm1_g[i:i + 1], mods, 0, "norm_bwd_ffn",
                                   branch=(saved[i - 1]["ff"], mods_all[i - 1], 5))
            dg2 = s1[3:4]
        else:
            dx, s1 = norm_bwd(dh, sv["x"], dx2, norm1_g[i:i + 1], mods, 0, "norm_bwd")
        dmods[i] = jnp.concatenate([s1[0:2], dg1] + dmods_i, axis=0).reshape(6 * D)
        dnorm1[i], dnorm2[i] = s1[2], s2[2]
        names = ["ssm_w_out"] if i % 2 == 0 else ["conv_w_out", "conv_w_in"]
        token = send_grads(names, gfull, [j] * len(names), dx, "scatter_mix%d" % i)

    small = dict(norm1_g=jnp.stack(dnorm1), norm2_g=jnp.stack(dnorm2), b_ada=jnp.stack(dmods),
                 final_g=dfinal[0] + token[0, 0])
    per = {n: [] for n in ('ssm_a_re', 'ssm_a_im', 'ssm_log_step', 'ssm_b_re', 'ssm_b_im', 'ssm_c_re', 'ssm_c_im', 'ssm_d')}
    GL = G // nkb
    for j, (dd, dab, db, dc) in enumerate(ds5):
        dab = jnp.sum(dab, axis=1).reshape(nkb, 2, GL, SSM_STATE)
        g_abr, g_abi = dab[:, 0].reshape(G, SSM_STATE), dab[:, 1].reshape(G, SSM_STATE)
        db, dc = db.reshape(G, SSM_GROUP, 2, SSM_STATE), dc.reshape(G, SSM_GROUP, 2, SSM_STATE)
        gb_re, gb_im, gc_re, gc_im = db[:, :, 0], db[:, :, 1], dc[:, :, 0], dc[:, :, 1]
        ga_re, ga_im, gls, gbr, gbi = s5[j]["vjp"]((g_abr, g_abi, jnp.swapaxes(gb_re, 1, 2), jnp.swapaxes(gb_im, 1, 2)))
        for n, val in zip(per, (ga_re, ga_im, gls, gbr, gbi, gc_re, -gc_im, jnp.sum(dd, axis=0))):
            per[n].append(val)
    small.update({n: jnp.stack(vals) for n, vals in per.items()})
    dcw_full = jnp.stack(dconv_w)

    my_loss = loss_blk[0:1, 0:1]
    slab_like = [W[n] for n in SLAB] + [dcw_full, my_loss]
    rows64 = 8 * N_DEV
    slab = _pack([small[n] for n in SLAB] + [dcw_full, my_loss], rows64)
    per_dev = slab.shape[0] // N_DEV
    x_sems, x_srcs, x_lands, token = exchange_start(
        [(slab.reshape(N_DEV, per_dev, SLAB_W), True), (_pack([small["b_ada"]]), False)], dx, "small_scatter")

    early = [n for n in SHARDED if n != "ssm_w_out"]
    land_grads(early, token)
    mine = [reduce4(gland[n], "reduce4_" + n) for n in early]

    parts, dm_all = exchange_wait(x_sems, x_srcs, x_lands, [True, False], mine[-1][0, :8, :128], "small_landed")
    t_sems, t_srcs, t_lands, token = exchange_start([(sum8(parts, "sum_small"), False)], dm_all, "small_gather")
    out = {}

    w_sems, w_srcs, w_lands, token2 = swap_start(mine, "swap_start")
    dm_all = dm_all.reshape(N_DEV, -1)[:, :b_ada.size].reshape(N_DEV, nlayer, N_CHIP, NA)
    dm_sh = jnp.transpose(lax.dynamic_index_in_dim(dm_all, chip, axis=2, keepdims=False), (1, 0, 2))
    res = adamw_ada(jnp.transpose(c_all) + token[0:1, 0:1] + token2[0:1, 0:1], dm_sh, w_ada, m_w_ada, v_w_ada,
                    "adamw_ada")
    out["g", "w_ada"], out["d", "w_ada"], out["m", "w_ada"], out["v", "w_ada"] = res

    g_slab = exchange_wait(t_sems, t_srcs, t_lands, [False], out["g", "w_ada"], "small_total")[0]
    g_slab = g_slab.reshape(slab.shape)
    d_slab, m_slab, v_slab = adamw_slab(
        g_slab, _pack([W[n] for n in SLAB] + [jnp.zeros_like(dcw_full)], rows64),
        _pack([Mo[n] for n in SLAB] + [jnp.zeros_like(dcw_full)], rows64),
        _pack([Vo[n] for n in SLAB] + [jnp.ones_like(dcw_full)], rows64), "adamw_slab")
    for k, slab in zip(("g", "d", "m", "v"), (g_slab, d_slab, m_slab, v_slab)):
        for n, val in zip(SLAB, _unpack(slab, slab_like)):
            out[k, n] = val
    g_cw = lax.dynamic_slice_in_dim(_unpack(g_slab, slab_like)[-2], chip * conv_w.shape[2], conv_w.shape[2], axis=2)
    out["g", "conv_w"] = g_cw
    out["d", "conv_w"], out["m", "conv_w"], out["v", "conv_w"] = [
        r.reshape(conv_w.shape) for r in adamw_plain(conv_w.reshape(-1, conv_w.shape[2]), m_conv_w.reshape(-1, conv_w.shape[2]),
                                                     v_conv_w.reshape(-1, conv_w.shape[2]), g_cw.reshape(-1, conv_w.shape[2]),
                                                     "adamw_conv_w")]

    mine, theirs = swap_wait(w_sems, w_srcs, w_lands, d_slab, "swap_wait")
    for n, ga, gb in zip(early, mine, theirs):
        r = adamw_sharded(W[n], Mo[n], Vo[n], ga, gb, "adamw_" + n)
        out["g", n], out["d", n], out["m", n], out["v", n] = r

    land_grads(["ssm_w_out"], out["g", "w_ffn_out"])
    ga = reduce4(gland["ssm_w_out"], "reduce4_ssm_w_out")
    gb = swap_siblings([ga], "swap_siblings")[0]
    r = adamw_sharded(ssm_w_out, m_ssm_w_out, v_ssm_w_out, ga, gb, "adamw_ssm_w_out")
    out["g", "ssm_w_out"], out["d", "ssm_w_out"], out["m", "ssm_w_out"], out["v", "ssm_w_out"] = r

    loss = _unpack(g_slab, slab_like)[-1][0, 0]
    return (loss, dx[None], *[out["g", n] for n in WEIGHTS], *[out["d", n] for n in WEIGHTS],
            *[out["m", n] for n in WEIGHTS], *[out["v", n] for n in WEIGHTS])
```

```python
import math

import jax
import jax.numpy as jnp
from jax import lax
from jax.experimental import pallas as pl
from jax.experimental.pallas import tpu as pltpu

F32 = jnp.float32
BF = jnp.bfloat16
MESH = pl.DeviceIdType.MESH
ANY = pl.BlockSpec(memory_space=pl.ANY)

N_DEV = 8
N_CHIP = 4
SSM_GROUP = 16
SSM_STATE = 64
S5_BLOCK = 256
RMS_EPS = 1e-6
ADAM_LR, ADAM_B1, ADAM_B2, ADAM_EPS, ADAM_WD, ADAM_STEP = 0.001, 0.9, 0.999, 1e-08, 0.01, 10
V7X_VMEM_BYTES = 64 * 1024 * 1024
VMEM_LIMIT = V7X_VMEM_BYTES - 12 * 1024 * 1024
SLAB_W = 1024
GELU_C = math.sqrt(2.0 / math.pi)
GELU_A = 0.044715


def _cp(*sem):
    return pltpu.CompilerParams(dimension_semantics=sem if sem else None, vmem_limit_bytes=VMEM_LIMIT)


def _tile(n, prefs):
    for p in prefs:
        if p <= n and n % p == 0:
            return p
    return n


def _sigmoid(v):
    return 0.5 * jnp.tanh(0.5 * v) + 0.5


def _axes():
    return lax.axis_index("x"), lax.axis_index("y"), lax.axis_index("c")


def _flip(v, k):
    return 1 - v if k else v


def gather8(v, name):
    R, C = v.shape

    def body(v_ref, o_ref, ssem, rsem, lsem):
        x, y, c = _axes()
        me = 4 * x + 2 * y + c
        loc = pltpu.make_async_copy(v_ref, o_ref.at[me], lsem)
        loc.start()
        copies = []
        for k in range(1, N_DEV):
            peer = (_flip(x, (k >> 2) & 1), _flip(y, (k >> 1) & 1), _flip(c, k & 1))
            cp = pltpu.make_async_remote_copy(src_ref=v_ref, dst_ref=o_ref.at[me], send_sem=ssem.at[k - 1],
                                              recv_sem=rsem.at[k - 1], device_id=peer, device_id_type=MESH)
            cp.start()
            copies.append(cp)
        for cp in copies:
            cp.wait()
        loc.wait()

    return pl.pallas_call(
        body, name=name,
        out_shape=jax.ShapeDtypeStruct((N_DEV, R, C), v.dtype),
        in_specs=[pl.BlockSpec(memory_space=pltpu.VMEM)],
        out_specs=pl.BlockSpec(memory_space=pltpu.VMEM),
        scratch_shapes=[pltpu.SemaphoreType.DMA((N_DEV - 1,)), pltpu.SemaphoreType.DMA((N_DEV - 1,)),
                        pltpu.SemaphoreType.DMA],
        compiler_params=pltpu.CompilerParams(vmem_limit_bytes=VMEM_LIMIT),
    )(v)


HBM = pl.BlockSpec(memory_space=pltpu.HBM)
SEM = pl.BlockSpec(memory_space=pltpu.SEMAPHORE)
EFFECT = pltpu.SideEffectType.DATAFLOW_SIDE_EFFECTING


def _in_hbm(a):
    return pltpu.with_memory_space_constraint(a, pltpu.HBM)


def _chip_peers(x, y, c):
    out = []
    for k in range(1, N_CHIP):
        px, py = _flip(x, k >> 1), _flip(y, k & 1)
        out.append(((px, py, c), 2 * px + py))
    return out


def _my_half(ref, c):
    rows = ref.shape[0] // 2
    return pl.ds(pl.multiple_of(c * rows, 16), rows)


def relay_start(lands, after, name):
    n = len(lands)

    def body(*refs):
        land = refs[:n]
        ssem, rsem = refs[n + 1:n + 3]
        token = refs[-1]
        x, y, c = _axes()
        for a in range(n):
            half = _my_half(land[a].at[0], c)
            for k, (_, pchip) in enumerate(_chip_peers(x, y, c)):
                pltpu.make_async_remote_copy(src_ref=land[a].at[pchip, half], dst_ref=land[a].at[pchip, half],
                                             send_sem=ssem.at[3 * a + k], recv_sem=rsem.at[3 * a + k],
                                             device_id=(x, y, 1 - c), device_id_type=MESH).start()
        token[...] = jnp.zeros_like(token)

    out_shape = ([pltpu.SemaphoreType.DMA((3 * n,)), pltpu.SemaphoreType.DMA((3 * n,))]
                 + [pltpu.HBM(l.shape, l.dtype) for l in lands] + [jax.ShapeDtypeStruct((8, 128), F32)])
    res = pl.pallas_call(
        body, name=name, out_shape=out_shape, in_specs=[HBM] * n + [ANY],
        out_specs=[SEM, SEM] + [HBM] * n + [pl.BlockSpec(memory_space=pltpu.VMEM)],
        input_output_aliases={a: 2 + a for a in range(n)},
        compiler_params=pltpu.CompilerParams(has_side_effects=EFFECT),
    )(*lands, after)
    return tuple(res[:2]), list(res[2:2 + n]), res[-1]


def relay_wait(sems, lands, after, name):
    n = len(lands)

    def body(*refs):
        land = refs[:n]
        ssem, rsem = refs[n:n + 2]
        x, y, c = _axes()
        for a in range(n):
            mine, theirs = _my_half(land[a].at[0], c), _my_half(land[a].at[0], 1 - c)
            for k, (_, pchip) in enumerate(_chip_peers(x, y, c)):
                cp = pltpu.make_async_remote_copy(src_ref=land[a].at[pchip, mine], dst_ref=land[a].at[pchip, theirs],
                                                  send_sem=ssem.at[3 * a + k], recv_sem=rsem.at[3 * a + k],
                                                  device_id=(x, y, 1 - c), device_id_type=MESH)
                cp.wait_send()
                cp.wait_recv()

    res = pl.pallas_call(
        body, name=name, out_shape=[pltpu.HBM(l.shape, l.dtype) for l in lands],
        in_specs=[HBM] * n + [SEM, SEM, ANY], out_specs=[HBM] * n,
        input_output_aliases={a: a for a in range(n)},
        compiler_params=pltpu.CompilerParams(has_side_effects=EFFECT),
    )(*lands, *sems, after)
    return list(res)


def gather_start(shards, after, name):
    n = len(shards)

    def body(*refs):
        src, land = refs[:n], refs[n:2 * n]
        ssem, rsem, lsem = refs[2 * n + 1:2 * n + 4]
        token = refs[-1]
        x, y, c = _axes()
        chip = 2 * x + y
        for a in range(n):
            pltpu.make_async_copy(src[a], land[a].at[chip], lsem.at[a]).start()
            half = _my_half(src[a], c)
            for k, (peer, _) in enumerate(_chip_peers(x, y, c)):
                pltpu.make_async_remote_copy(src_ref=src[a].at[half], dst_ref=land[a].at[chip, half],
                                             send_sem=ssem.at[3 * a + k], recv_sem=rsem.at[3 * a + k],
                                             device_id=peer, device_id_type=MESH).start()
        token[...] = jnp.zeros_like(token)

    lands = [lax.empty((N_CHIP,) + s.shape, s.dtype) for s in shards]
    out_shape = ([pltpu.SemaphoreType.DMA((3 * n,)), pltpu.SemaphoreType.DMA((3 * n,)), pltpu.SemaphoreType.DMA((n,))]
                 + [pltpu.HBM(s.shape, s.dtype) for s in shards] + [pltpu.HBM(l.shape, l.dtype) for l in lands]
                 + [jax.ShapeDtypeStruct((8, 128), F32)])
    res = pl.pallas_call(
        body, name=name, out_shape=out_shape, in_specs=[HBM] * (2 * n) + [ANY],
        out_specs=[SEM, SEM, SEM] + [HBM] * (2 * n) + [pl.BlockSpec(memory_space=pltpu.VMEM)],
        input_output_aliases={a: 3 + a for a in range(2 * n)},
        compiler_params=pltpu.CompilerParams(has_side_effects=EFFECT),
    )(*[_in_hbm(s) for s in shards], *[_in_hbm(l) for l in lands], after)
    return tuple(res[:3]), list(res[3:3 + n]), list(res[3 + n:3 + 2 * n]), res[-1]


def gather_wait(sems, srcs, lands, idx, after, name):
    m = len(idx)

    def body(*refs):
        src, land = refs[:m], refs[m:2 * m]
        ssem, rsem, lsem = refs[2 * m:2 * m + 3]
        x, y, c = _axes()
        chip = 2 * x + y
        for j, a in enumerate(idx):
            half = _my_half(src[j], c)
            for k, (peer, pchip) in enumerate(_chip_peers(x, y, c)):
                cp = pltpu.make_async_remote_copy(src_ref=src[j].at[half], dst_ref=land[j].at[pchip, half],
                                                  send_sem=ssem.at[3 * a + k], recv_sem=rsem.at[3 * a + k],
                                                  device_id=peer, device_id_type=MESH)
                cp.wait_send()
                cp.wait_recv()
            pltpu.make_async_copy(src[j], land[j].at[chip], lsem.at[a]).wait()

    s_in = [srcs[a] for a in idx]
    l_in = [lands[a] for a in idx]
    res = pl.pallas_call(
        body, name=name,
        out_shape=[pltpu.HBM(s.shape, s.dtype) for s in s_in] + [pltpu.HBM(l.shape, l.dtype) for l in l_in],
        in_specs=[HBM] * (2 * m) + [SEM, SEM, SEM, ANY], out_specs=[HBM] * (2 * m),
        input_output_aliases={a: a for a in range(2 * m)},
        compiler_params=pltpu.CompilerParams(has_side_effects=EFFECT),
    )(*s_in, *l_in, *sems, after)
    return list(res[m:])


def scatter_start(grads, lands, slot, after, name):
    n = len(grads)

    def body(*refs):
        src, land = refs[:n], refs[n:2 * n]
        ssem, rsem, lsem = refs[2 * n + 1:2 * n + 4]
        token = refs[-1]
        x, y, c = _axes()
        chip = 2 * x + y
        for a in range(n):
            pltpu.make_async_copy(src[a].at[chip], land[a].at[slot[a], chip], lsem.at[a]).start()
            for k, (peer, pchip) in enumerate(_chip_peers(x, y, c)):
                pltpu.make_async_remote_copy(src_ref=src[a].at[pchip], dst_ref=land[a].at[slot[a], chip],
                                             send_sem=ssem.at[3 * a + k], recv_sem=rsem.at[3 * a + k],
                                             device_id=peer, device_id_type=MESH).start()
        token[...] = jnp.zeros_like(token)

    out_shape = ([pltpu.SemaphoreType.DMA((3 * n,)), pltpu.SemaphoreType.DMA((3 * n,)), pltpu.SemaphoreType.DMA((n,))]
                 + [pltpu.HBM(g.shape, g.dtype) for g in grads] + [pltpu.HBM(l.shape, l.dtype) for l in lands]
                 + [jax.ShapeDtypeStruct((8, 128), F32)])
    res = pl.pallas_call(
        body, name=name, out_shape=out_shape, in_specs=[HBM] * (2 * n) + [ANY],
        out_specs=[SEM, SEM, SEM] + [HBM] * (2 * n) + [pl.BlockSpec(memory_space=pltpu.VMEM)],
        input_output_aliases={a: 3 + a for a in range(2 * n)},
        compiler_params=pltpu.CompilerParams(has_side_effects=EFFECT),
    )(*[_in_hbm(g) for g in grads], *[_in_hbm(l) for l in lands], after)
    return tuple(res[:3]), list(res[3:3 + n]), list(res[3 + n:3 + 2 * n]), res[-1]


def scatter_wait(sems, grads, lands, slot, after, name):
    n = len(grads)

    def body(*refs):
        src, land = refs[:n], refs[n:2 * n]
        ssem, rsem, lsem = refs[2 * n:2 * n + 3]
        x, y, c = _axes()
        chip = 2 * x + y
        for a in range(n):
            for k, (peer, pchip) in enumerate(_chip_peers(x, y, c)):
                cp = pltpu.make_async_remote_copy(src_ref=src[a].at[pchip], dst_ref=land[a].at[slot[a], pchip],
                                                  send_sem=ssem.at[3 * a + k], recv_sem=rsem.at[3 * a + k],
                                                  device_id=peer, device_id_type=MESH)
                cp.wait_send()
                cp.wait_recv()
            pltpu.make_async_copy(src[a].at[chip], land[a].at[slot[a], chip], lsem.at[a]).wait()

    res = pl.pallas_call(
        body, name=name,
        out_shape=[pltpu.HBM(g.shape, g.dtype) for g in grads] + [pltpu.HBM(l.shape, l.dtype) for l in lands],
        in_specs=[HBM] * (2 * n) + [SEM, SEM, SEM, ANY], out_specs=[HBM] * (2 * n),
        input_output_aliases={a: a for a in range(2 * n)},
        compiler_params=pltpu.CompilerParams(has_side_effects=EFFECT),
    )(*grads, *lands, *sems, after)
    return list(res[n:])


def reduce4(land, name):
    nl, _, R, C = land.shape
    TR = _adam_rows(R, C)

    def body(l_ref, o_ref):
        o_ref[...] = ((l_ref[0].astype(F32) + l_ref[1].astype(F32)) + l_ref[2].astype(F32)) + l_ref[3].astype(F32)

    return pl.pallas_call(
        body, name=name, grid=(nl, R // TR),
        in_specs=[pl.BlockSpec((None, N_CHIP, TR, C), lambda i, r: (i, 0, r, 0))],
        out_specs=pl.BlockSpec((None, TR, C), lambda i, r: (i, r, 0)),
        out_shape=jax.ShapeDtypeStruct((nl, R, C), F32), compiler_params=_cp("parallel", "parallel"))(land)


def swap_siblings(arrs, name):
    n = len(arrs)

    def body(*refs):
        src, dst = refs[:n], refs[n:2 * n]
        ssem, rsem = refs[2 * n:]
        x, y, c = _axes()
        cps = [pltpu.make_async_remote_copy(src_ref=src[a], dst_ref=dst[a], send_sem=ssem.at[a], recv_sem=rsem.at[a],
                                            device_id=(x, y, 1 - c), device_id_type=MESH) for a in range(n)]
        for cp in cps:
            cp.start()
        for cp in cps:
            cp.wait()

    return pl.pallas_call(
        body, name=name, out_shape=[jax.ShapeDtypeStruct(a.shape, a.dtype) for a in arrs],
        in_specs=[ANY] * n, out_specs=[ANY] * n,
        scratch_shapes=[pltpu.SemaphoreType.DMA((n,)), pltpu.SemaphoreType.DMA((n,))],
        compiler_params=pltpu.CompilerParams(vmem_limit_bytes=VMEM_LIMIT),
    )(*arrs)


def swap_start(arrs, name):
    n = len(arrs)

    def body(*refs):
        src, land = refs[:n], refs[n:2 * n]
        ssem, rsem = refs[2 * n:2 * n + 2]
        token = refs[-1]
        x, y, c = _axes()
        for a in range(n):
            pltpu.make_async_remote_copy(src_ref=src[a], dst_ref=land[a], send_sem=ssem.at[a], recv_sem=rsem.at[a],
                                         device_id=(x, y, 1 - c), device_id_type=MESH).start()
        token[...] = jnp.zeros_like(token)

    lands = [lax.empty(a.shape, a.dtype) for a in arrs]
    out_shape = ([pltpu.SemaphoreType.DMA((n,)), pltpu.SemaphoreType.DMA((n,))]
                 + [pltpu.HBM(a.shape, a.dtype) for a in arrs] * 2 + [jax.ShapeDtypeStruct((8, 128), F32)])
    res = pl.pallas_call(
        body, name=name, out_shape=out_shape, in_specs=[HBM] * (2 * n),
        out_specs=[SEM, SEM] + [HBM] * (2 * n) + [pl.BlockSpec(memory_space=pltpu.VMEM)],
        input_output_aliases={a: 2 + a for a in range(2 * n)},
        compiler_params=pltpu.CompilerParams(has_side_effects=EFFECT),
    )(*[_in_hbm(a) for a in arrs], *[_in_hbm(l) for l in lands])
    return tuple(res[:2]), list(res[2:2 + n]), list(res[2 + n:2 + 2 * n]), res[-1]


def swap_wait(sems, srcs, lands, after, name):
    n = len(srcs)

    def body(*refs):
        src, land = refs[:n], refs[n:2 * n]
        ssem, rsem = refs[2 * n:2 * n + 2]
        x, y, c = _axes()
        for a in range(n):
            cp = pltpu.make_async_remote_copy(src_ref=src[a], dst_ref=land[a], send_sem=ssem.at[a],
                                              recv_sem=rsem.at[a], device_id=(x, y, 1 - c), device_id_type=MESH)
            cp.wait_send()
            cp.wait_recv()

    res = pl.pallas_call(
        body, name=name, out_shape=[pltpu.HBM(a.shape, a.dtype) for a in srcs] * 2,
        in_specs=[HBM] * (2 * n) + [SEM, SEM, ANY], out_specs=[HBM] * (2 * n),
        input_output_aliases={a: a for a in range(2 * n)},
        compiler_params=pltpu.CompilerParams(has_side_effects=EFFECT),
    )(*srcs, *lands, *sems, after)
    return list(res[:n]), list(res[n:])


def _all_peers(x, y, c):
    out = []
    for k in range(1, N_DEV):
        px, py, pc = _flip(x, (k >> 2) & 1), _flip(y, (k >> 1) & 1), _flip(c, k & 1)
        out.append(((px, py, pc), 4 * px + 2 * py + pc))
    return out


def exchange_start(items, after, name):
    n = len(items)

    def body(*refs):
        src, land = refs[:n], refs[n:2 * n]
        ssem, rsem, lsem = refs[2 * n + 1:2 * n + 4]
        token = refs[-1]
        x, y, c = _axes()
        me = 4 * x + 2 * y + c
        for a, (_, scatter) in enumerate(items):
            pltpu.make_async_copy(src[a].at[me] if scatter else src[a], land[a].at[me], lsem.at[a]).start()
            for k, (peer, p) in enumerate(_all_peers(x, y, c)):
                pltpu.make_async_remote_copy(src_ref=src[a].at[p] if scatter else src[a], dst_ref=land[a].at[me],
                                             send_sem=ssem.at[7 * a + k], recv_sem=rsem.at[7 * a + k],
                                             device_id=peer, device_id_type=MESH).start()
        token[...] = jnp.zeros_like(token)

    srcs = [s for s, _ in items]
    lands = [lax.empty(s.shape if sc else (N_DEV,) + s.shape, s.dtype) for s, sc in items]
    out_shape = ([pltpu.SemaphoreType.DMA((7 * n,)), pltpu.SemaphoreType.DMA((7 * n,)), pltpu.SemaphoreType.DMA((n,))]
                 + [pltpu.HBM(s.shape, s.dtype) for s in srcs] + [pltpu.HBM(l.shape, l.dtype) for l in lands]
                 + [jax.ShapeDtypeStruct((8, 128), F32)])
    res = pl.pallas_call(
        body, name=name, out_shape=out_shape, in_specs=[HBM] * (2 * n) + [ANY],
        out_specs=[SEM, SEM, SEM] + [HBM] * (2 * n) + [pl.BlockSpec(memory_space=pltpu.VMEM)],
        input_output_aliases={a: 3 + a for a in range(2 * n)},
        compiler_params=pltpu.CompilerParams(has_side_effects=EFFECT),
    )(*[_in_hbm(s) for s in srcs], *[_in_hbm(l) for l in lands], after)
    return tuple(res[:3]), list(res[3:3 + n]), list(res[3 + n:3 + 2 * n]), res[-1]


def exchange_wait(sems, srcs, lands, scatter, after, name):
    n = len(srcs)

    def body(*refs):
        src, land = refs[:n], refs[n:2 * n]
        ssem, rsem, lsem = refs[2 * n:2 * n + 3]
        x, y, c = _axes()
        me = 4 * x + 2 * y + c
        for a in range(n):
            for k, (peer, p) in enumerate(_all_peers(x, y, c)):
                cp = pltpu.make_async_remote_copy(src_ref=src[a].at[p] if scatter[a] else src[a],
                                                  dst_ref=land[a].at[p], send_sem=ssem.at[7 * a + k],
                                                  recv_sem=rsem.at[7 * a + k], device_id=peer, device_id_type=MESH)
                cp.wait_send()
                cp.wait_recv()
            pltpu.make_async_copy(src[a].at[me] if scatter[a] else src[a], land[a].at[me], lsem.at[a]).wait()

    res = pl.pallas_call(
        body, name=name,
        out_shape=[pltpu.HBM(s.shape, s.dtype) for s in srcs] + [pltpu.HBM(l.shape, l.dtype) for l in lands],
        in_specs=[HBM] * (2 * n) + [SEM, SEM, SEM, ANY], out_specs=[HBM] * (2 * n),
        input_output_aliases={a: a for a in range(2 * n)},
        compiler_params=pltpu.CompilerParams(has_side_effects=EFFECT),
    )(*srcs, *lands, *sems, after)
    return list(res[n:])


def sum8(parts, name):
    _, P, C = parts.shape

    def body(p_ref, o_ref):
        tot = p_ref[0]
        for d in range(1, N_DEV):
            tot = tot + p_ref[d]
        o_ref[...] = tot

    return pl.pallas_call(body, name=name, out_shape=jax.ShapeDtypeStruct((P, C), F32),
                          compiler_params=pltpu.CompilerParams(vmem_limit_bytes=VMEM_LIMIT))(parts)


def mm_nn(a, w, out_dtype, name, res=None, gate=None):
    M, K = a.shape
    S, _, Ns = w.shape
    TM = _tile(M, (1024, 512, 256) if K <= 1024 else (512, 256))
    TN = _tile(Ns, (1408, 1024, 768, 512, 256, 128))
    nj = Ns // TN
    fused = res is not None

    def body(*refs):
        if fused:
            a_ref, w_ref, r_ref, g_ref, f_ref, o_ref = refs
        else:
            a_ref, w_ref, f_ref = refs
        f = jnp.dot(a_ref[...], w_ref[...], preferred_element_type=F32)
        f_ref[...] = f.astype(f_ref.dtype)
        if fused:
            o_ref[...] = r_ref[...] + g_ref[...] * f

    col = lambda s, j, i: (i, s * nj + j)
    in_specs = [pl.BlockSpec((TM, K), lambda s, j, i: (i, 0)), pl.BlockSpec((None, K, TN), lambda s, j, i: (s, 0, j))]
    out_specs = [pl.BlockSpec((TM, TN), col)]
    out_shape = [jax.ShapeDtypeStruct((M, S * Ns), out_dtype)]
    args = [a, w]
    if fused:
        in_specs += [pl.BlockSpec((TM, TN), col), pl.BlockSpec((1, TN), lambda s, j, i: (0, s * nj + j))]
        out_specs.append(pl.BlockSpec((TM, TN), col))
        out_shape.append(jax.ShapeDtypeStruct((M, S * Ns), F32))
        args += [res, gate]
    out = pl.pallas_call(body, name=name, grid=(S, nj, M // TM), in_specs=in_specs, out_specs=out_specs,
                         out_shape=out_shape, compiler_params=_cp("parallel", "parallel", "parallel"))(*args)
    return tuple(out) if fused else out[0]


def mm_nt(g, w, out_dtype, name):
    g3 = g if g.ndim == 3 else g[None]
    Q, M, F = g3.shape
    S, K, Ns = w.shape
    TM = _tile(M, (2048, 1024, 512, 256) if K <= 1024 else (512, 256))
    TN = _tile(Ns, (1408, 1024, 768, 512, 256, 128))
    nj = Ns // TN
    nred = S * nj
    per_part = F // TN

    def body(g_ref, w_ref, o_ref, acc):
        n = pl.program_id(1)

        @pl.when(n == 0)
        def _():
            acc[...] = jnp.zeros_like(acc)

        acc[...] += lax.dot_general(g_ref[...], w_ref[...], (((1,), (1,)), ((), ())), preferred_element_type=F32)

        @pl.when(n == nred - 1)
        def _():
            o_ref[...] = acc[...].astype(o_ref.dtype)

    return pl.pallas_call(
        body, name=name, grid=(M // TM, nred),
        in_specs=[pl.BlockSpec((None, TM, TN), lambda i, n: (n // per_part, i, n % per_part)),
                  pl.BlockSpec((None, K, TN), lambda i, n: (n // nj, 0, n % nj))],
        out_specs=pl.BlockSpec((TM, K), lambda i, n: (i, 0)),
        out_shape=jax.ShapeDtypeStruct((M, K), out_dtype),
        scratch_shapes=[pltpu.VMEM((TM, K), F32)],
        compiler_params=_cp("parallel", "arbitrary"))(g3, w)


def mm_tn(a, g, S, name):
    M, K = a.shape
    g3 = g if g.ndim == 3 else g[None]
    Q, _, F = g3.shape
    Ns = Q * F // S
    TN = _tile(Ns, (1408, 1024, 768, 512, 256, 128))
    TK = next(t for t in (1024, 512, 256, 128) if t <= K and K % t == 0
              and 4 * M * (t + TN) + 8 * t * TN <= VMEM_LIMIT * 3 // 4)
    nj = Ns // TN
    per_part = F // TN

    def body(a_ref, g_ref, o_ref):
        o_ref[...] = lax.dot_general(a_ref[...], g_ref[...], (((0,), (0,)), ((), ())),
                                     preferred_element_type=F32).astype(o_ref.dtype)

    return pl.pallas_call(
        body, name=name, grid=(S * nj, K // TK),
        in_specs=[pl.BlockSpec((M, TK), lambda n, k: (0, k)),
                  pl.BlockSpec((None, M, TN), lambda n, k: (n // per_part, 0, n % per_part))],
        out_specs=pl.BlockSpec((None, TK, TN), lambda n, k: (n // nj, k, n % nj)),
        out_shape=jax.ShapeDtypeStruct((S, K, Ns), BF),
        compiler_params=_cp("parallel", "parallel"))(a, g3)


ROW_TILE = (512, 256)


def _rows(TL, D):
    return pl.BlockSpec((TL, D), lambda i: (i, 0))


def _fixed(R, D):
    return pl.BlockSpec((R, D), lambda i: (0, 0))


def _rowsum8(v):
    T, D = v.shape
    return jnp.sum(v.reshape(T // 8, 8, D), axis=0)


def _norm_parts(xv):
    r = lax.rsqrt(jnp.mean(xv * xv, axis=-1, keepdims=True) + RMS_EPS)
    return xv * r, r


def norm_mod(x, gamma, mods, k_shift, out_dtype, name):
    L, D = x.shape
    TL = _tile(L, ROW_TILE)

    def body(x_ref, g_ref, m_ref, o_ref):
        xn, _ = _norm_parts(x_ref[...])
        sh, sc = m_ref[k_shift:k_shift + 1, :], m_ref[k_shift + 1:k_shift + 2, :]
        o_ref[...] = ((xn * g_ref[...]) * (1.0 + sc) + sh).astype(o_ref.dtype)

    return pl.pallas_call(body, name=name, grid=(L // TL,),
                          in_specs=[_rows(TL, D), _fixed(1, D), _fixed(6, D)], out_specs=_rows(TL, D),
                          out_shape=jax.ShapeDtypeStruct((L, D), out_dtype), compiler_params=_cp("parallel"))(x, gamma, mods)


def norm_bwd(dh, x, dres, gamma, mods, k_shift, name, branch=None):
    L, D = x.shape
    TL = _tile(L, ROW_TILE)
    nacc = 4 if branch else 3

    def body(*refs):
        if branch:
            dh_ref, x_ref, dr_ref, g_ref, m_ref, f_ref, fm_ref, dx_ref, s_ref, df_ref, acc = refs
        else:
            dh_ref, x_ref, dr_ref, g_ref, m_ref, dx_ref, s_ref, acc = refs
        i = pl.program_id(0)

        @pl.when(i == 0)
        def _():
            acc[...] = jnp.zeros_like(acc)

        xn, r = _norm_parts(x_ref[...])
        dh_v = dh_ref[...].astype(F32)
        gam = g_ref[...]
        sc = m_ref[k_shift + 1:k_shift + 2, :]
        dn = dh_v * (1.0 + sc)
        dxn = dn * gam
        dx = dr_ref[...] + r * (dxn - xn * jnp.mean(dxn * xn, axis=-1, keepdims=True))
        dx_ref[...] = dx
        acc[0] += _rowsum8(dh_v)
        acc[1] += _rowsum8(dh_v * (xn * gam))
        acc[2] += _rowsum8(dn * xn)
        if branch:
            df_ref[...] = (dx * fm_ref[branch[2]:branch[2] + 1, :]).astype(df_ref.dtype)
            acc[3] += _rowsum8(dx * f_ref[...].astype(F32))

        @pl.when(i == pl.num_programs(0) - 1)
        def _():
            s_ref[...] = jnp.zeros_like(s_ref)
            for q in range(nacc):
                s_ref[q:q + 1, :] = jnp.sum(acc[q], axis=0, keepdims=True)

    in_specs = [_rows(TL, D), _rows(TL, D), _rows(TL, D), _fixed(1, D), _fixed(6, D)]
    out_specs = [_rows(TL, D), _fixed(8, D)]
    out_shape = [jax.ShapeDtypeStruct((L, D), F32), jax.ShapeDtypeStruct((8, D), F32)]
    args = [dh, x, dres, gamma, mods]
    if branch:
        in_specs += [_rows(TL, D), _fixed(6, D)]
        out_specs.append(_rows(TL, D))
        out_shape.append(jax.ShapeDtypeStruct((L, D), BF))
        args += [branch[0], branch[1]]
    return pl.pallas_call(
        body, name=name, grid=(L // TL,), in_specs=in_specs, out_specs=out_specs, out_shape=out_shape,
        scratch_shapes=[pltpu.VMEM((nacc, 8, D), F32)], compiler_params=_cp("arbitrary"))(*args)


def ffn_in_act(a, w, name):
    M, K = a.shape
    S, _, Ns = w.shape
    half = S // 2
    TM = _tile(M, (1024, 512, 256))
    TN = _tile(Ns, (1408, 1024, 768, 512, 256, 128))
    nj = Ns // TN

    def body(a_ref, wg_ref, wu_ref, gu_ref, act_ref):
        av = a_ref[...]
        g = jnp.dot(av, wg_ref[...], preferred_element_type=F32)
        u = jnp.dot(av, wu_ref[...], preferred_element_type=F32)
        gu_ref[0] = g.astype(gu_ref.dtype)
        gu_ref[1] = u.astype(gu_ref.dtype)
        act_ref[...] = (g * _sigmoid(g) * u).astype(act_ref.dtype)

    return pl.pallas_call(
        body, name=name, grid=(half, nj, M // TM),
        in_specs=[pl.BlockSpec((TM, K), lambda s, j, i: (i, 0)),
                  pl.BlockSpec((None, K, TN), lambda s, j, i: (s, 0, j)),
                  pl.BlockSpec((None, K, TN), lambda s, j, i: (s + half, 0, j))],
        out_specs=[pl.BlockSpec((2, TM, TN), lambda s, j, i: (0, i, s * nj + j)),
                   pl.BlockSpec((TM, TN), lambda s, j, i: (i, s * nj + j))],
        out_shape=[jax.ShapeDtypeStruct((2, M, half * Ns), BF), jax.ShapeDtypeStruct((M, half * Ns), BF)],
        compiler_params=_cp("parallel", "parallel", "parallel"))(a, w, w)


def ffn_out_bwd(dff, w2, gu, name):
    M, D = dff.shape
    F = w2.shape[0]
    TM = _tile(M, (512, 256))
    CW = _tile(F, (256, 128))

    def body(d_ref, w_ref, gu_ref, o_ref):
        dv = d_ref[...]

        def product(c):
            return lax.dot_general(dv, w_ref[c:c + CW, :], (((1,), (1,)), ((), ())), preferred_element_type=F32)

        da = product(0)
        for c in range(0, F, CW):
            ahead = product(c + CW) if c + CW < F else None
            g = gu_ref[0, :, c:c + CW].astype(F32)
            u = gu_ref[1, :, c:c + CW].astype(F32)
            s = _sigmoid(g)
            o_ref[0, :, c:c + CW] = (da * u * (s + g * s * (1.0 - s))).astype(o_ref.dtype)
            o_ref[1, :, c:c + CW] = (da * g * s).astype(o_ref.dtype)
            da = ahead

    part = pl.BlockSpec((2, TM, F), lambda i: (0, i, 0))
    return pl.pallas_call(
        body, name=name, grid=(M // TM,),
        in_specs=[pl.BlockSpec((TM, D), lambda i: (i, 0)), pl.BlockSpec((F, D), lambda i: (0, 0)), part],
        out_specs=part, out_shape=jax.ShapeDtypeStruct((2, M, F), BF),
        compiler_params=_cp("parallel"))(dff, w2, gu)


def ssm_out_glu(z, w, x, mods, k_gate, name):
    M, K = z.shape
    S, _, Ns = w.shape
    half = S // 2
    TM = _tile(M, (1024, 512, 256))
    TN = _tile(Ns, (512, 256, 128))
    nj = Ns // TN

    def body(z_ref, wv_ref, wg_ref, x_ref, m_ref, o_ref, mix_ref, y_ref):
        zv = z_ref[...]
        CW = _tile(TN, (256, 128))

        def products(c):
            return (jnp.dot(zv, wv_ref[:, c:c + CW], preferred_element_type=F32),
                    jnp.dot(zv, wg_ref[:, c:c + CW], preferred_element_type=F32))

        cur = products(0)
        for c in range(0, TN, CW):
            ahead = products(c + CW) if c + CW < TN else None
            val, gate = cur
            o_ref[0, :, c:c + CW] = val.astype(o_ref.dtype)
            o_ref[1, :, c:c + CW] = gate.astype(o_ref.dtype)
            mix = val * _sigmoid(gate)
            mix_ref[:, c:c + CW] = mix.astype(mix_ref.dtype)
            y_ref[:, c:c + CW] = x_ref[:, c:c + CW] + m_ref[k_gate:k_gate + 1, c:c + CW] * mix
            cur = ahead

    col = lambda s, j, i: (i, s * nj + j)
    return pl.pallas_call(
        body, name=name, grid=(half, nj, M // TM),
        in_specs=[pl.BlockSpec((TM, K), lambda s, j, i: (i, 0)),
                  pl.BlockSpec((None, K, TN), lambda s, j, i: (s, 0, j)),
                  pl.BlockSpec((None, K, TN), lambda s, j, i: (s + half, 0, j)),
                  pl.BlockSpec((TM, TN), col), pl.BlockSpec((6, TN), lambda s, j, i: (0, s * nj + j))],
        out_specs=[pl.BlockSpec((2, TM, TN), lambda s, j, i: (0, i, s * nj + j)), pl.BlockSpec((TM, TN), col),
                   pl.BlockSpec((TM, TN), col)],
        out_shape=[jax.ShapeDtypeStruct((2, M, half * Ns), BF), jax.ShapeDtypeStruct((M, half * Ns), BF),
                   jax.ShapeDtypeStruct((M, half * Ns), F32)],
        compiler_params=_cp("parallel", "parallel", "parallel"))(z, w, w, x, mods)


def glu_bwd(dmix, o, name):
    _, L, D = o.shape
    TL = _tile(L, ROW_TILE)

    def body(d_ref, o_ref, do_ref):
        d = d_ref[...].astype(F32)
        val = o_ref[0].astype(F32)
        s = _sigmoid(o_ref[1].astype(F32))
        do_ref[0] = (d * s).astype(do_ref.dtype)
        do_ref[1] = (d * val * s * (1.0 - s)).astype(do_ref.dtype)

    part = pl.BlockSpec((2, TL, D), lambda i: (0, i, 0))
    return pl.pallas_call(body, name=name, grid=(L // TL,), in_specs=[_rows(TL, D), part],
                          out_specs=part, out_shape=jax.ShapeDtypeStruct((2, L, D), BF),
                          compiler_params=_cp("parallel"))(dmix, o)


def final_loss(x, target, gamma, f, fmods, k_gate, name):
    L, D = x.shape
    TL = _tile(L, ROW_TILE)

    def body(x_ref, t_ref, g_ref, f_ref, fm_ref, l_ref, dx_ref, s_ref, df_ref, acc, lacc):
        i = pl.program_id(0)

        @pl.when(i == 0)
        def _():
            acc[...] = jnp.zeros_like(acc)
            lacc[...] = jnp.zeros_like(lacc)

        xn, r = _norm_parts(x_ref[...])
        gam = g_ref[...]
        e = xn * gam - t_ref[...]
        lacc[...] += jnp.sum(0.5 * jnp.mean(e * e, axis=-1, keepdims=True), axis=0, keepdims=True)
        dy = e * (1.0 / D)
        dxn = dy * gam
        dx = r * (dxn - xn * jnp.mean(dxn * xn, axis=-1, keepdims=True))
        dx_ref[...] = dx
        df_ref[...] = (dx * fm_ref[k_gate:k_gate + 1, :]).astype(df_ref.dtype)
        acc[0] += _rowsum8(dy * xn)
        acc[1] += _rowsum8(dx * f_ref[...].astype(F32))

        @pl.when(i == pl.num_programs(0) - 1)
        def _():
            s_ref[...] = jnp.zeros_like(s_ref)
            for q in range(2):
                s_ref[q:q + 1, :] = jnp.sum(acc[q], axis=0, keepdims=True)
            l_ref[...] = jnp.broadcast_to(lacc[...], l_ref.shape)

    return pl.pallas_call(
        body, name=name, grid=(L // TL,),
        in_specs=[_rows(TL, D), _rows(TL, D), _fixed(1, D), _rows(TL, D), _fixed(6, D)],
        out_specs=[_fixed(8, 128), _rows(TL, D), _fixed(8, D), _rows(TL, D)],
        out_shape=[jax.ShapeDtypeStruct((8, 128), F32), jax.ShapeDtypeStruct((L, D), F32),
                   jax.ShapeDtypeStruct((8, D), F32), jax.ShapeDtypeStruct((L, D), BF)],
        scratch_shapes=[pltpu.VMEM((2, 8, D), F32), pltpu.VMEM((1, 1), F32)],
        compiler_params=_cp("arbitrary"))(x, target, gamma, f, fmods)


def _col(L, TC, off):
    return pl.BlockSpec((L, TC), lambda j: (0, off + j))


def _shift_down(v, k, row):
    return jnp.where(row >= k, pltpu.roll(v, k, 0), 0.0)


def _shift_up(v, k, row, L):
    return jnp.where(row < L - k, pltpu.roll(v, L - k, 0), 0.0)


def conv_fwd(p, w, name):
    L, D3 = p.shape
    D = D3 // 3
    TC = _tile(D, (128,))
    nc = D // TC

    def body(b_ref, c_ref, v_ref, w_ref, o_ref):
        row = lax.broadcasted_iota(jnp.int32, (L, TC), 0)
        cv = c_ref[...].astype(F32) * v_ref[...].astype(F32)
        conv = w_ref[2:3, :] * cv + w_ref[1:2, :] * _shift_down(cv, 1, row) + w_ref[0:1, :] * _shift_down(cv, 2, row)
        o_ref[...] = (b_ref[...].astype(F32) * conv).astype(o_ref.dtype)

    return pl.pallas_call(
        body, name=name, grid=(nc,),
        in_specs=[_col(L, TC, 0), _col(L, TC, nc), _col(L, TC, 2 * nc), pl.BlockSpec((3, TC), lambda j: (0, j))],
        out_specs=_col(L, TC, 0), out_shape=jax.ShapeDtypeStruct((L, D), BF), compiler_params=_cp("parallel"))(p, p, p, w)


def conv_bwd(dm, p, w, name):
    L, D3 = p.shape
    D = D3 // 3
    TC = _tile(D, (128,))
    nc = D // TC

    def body(dm_ref, b_ref, c_ref, v_ref, w_ref, db_ref, dc_ref, dv_ref, dw_ref):
        row = lax.broadcasted_iota(jnp.int32, (L, TC), 0)
        cg, vv = c_ref[...].astype(F32), v_ref[...].astype(F32)
        cv = cg * vv
        cv1, cv2 = _shift_down(cv, 1, row), _shift_down(cv, 2, row)
        conv = w_ref[2:3, :] * cv + w_ref[1:2, :] * cv1 + w_ref[0:1, :] * cv2
        dmv = dm_ref[...].astype(F32)
        db_ref[...] = (dmv * conv).astype(db_ref.dtype)
        dconv = dmv * b_ref[...].astype(F32)
        dcv = (w_ref[2:3, :] * dconv + w_ref[1:2, :] * _shift_up(dconv, 1, row, L)
               + w_ref[0:1, :] * _shift_up(dconv, 2, row, L))
        dc_ref[...] = (dcv * vv).astype(dc_ref.dtype)
        dv_ref[...] = (dcv * cg).astype(dv_ref.dtype)
        dw_ref[...] = jnp.zeros_like(dw_ref)
        dw_ref[0:1, :] = jnp.sum(dconv * cv2, axis=0, keepdims=True)
        dw_ref[1:2, :] = jnp.sum(dconv * cv1, axis=0, keepdims=True)
        dw_ref[2:3, :] = jnp.sum(dconv * cv, axis=0, keepdims=True)

    one = jax.ShapeDtypeStruct((L, D), BF)
    return pl.pallas_call(
        body, name=name, grid=(nc,),
        in_specs=[_col(L, TC, 0), _col(L, TC, 0), _col(L, TC, nc), _col(L, TC, 2 * nc),
                  pl.BlockSpec((3, TC), lambda j: (0, j))],
        out_specs=[_col(L, TC, 0), _col(L, TC, 0), _col(L, TC, 0), pl.BlockSpec((8, TC), lambda j: (0, j))],
        out_shape=[one, one, one, jax.ShapeDtypeStruct((8, D), F32)],
        compiler_params=_cp("parallel"))(dm, p, p, p, w)


def _gelu(y):
    return 0.5 * y * (1.0 + jnp.tanh(GELU_C * (y + GELU_A * y * y * y)))


def _gelu_grad(y):
    th = jnp.tanh(GELU_C * (y + GELU_A * y * y * y))
    return 0.5 * (1.0 + th) + 0.5 * y * (1.0 - th * th) * GELU_C * (1.0 + 3.0 * GELU_A * y * y)


def _cmul_add(br, bi, ar, ai, sr, si):
    return br + ar * sr - ai * si, bi + ar * si + ai * sr


def _log2(n):
    k = n.bit_length() - 1
    assert 1 << k == n
    return k


def _replicate(P2, W2, P, GLP, transposed):
    shape = (W2, P2) if transposed else (P2, W2)
    k = lax.broadcasted_iota(jnp.int32, shape, 1 if transposed else 0)
    c = lax.broadcasted_iota(jnp.int32, shape, 0 if transposed else 1)
    return ((k >> _log2(P)) == (c >> _log2(GLP))) & ((k & (P - 1)) == (c & (P - 1)))


def _on_diagonal(KB, W2, H, P, GLP, transposed):
    shape = (W2, KB) if transposed else (KB, W2)
    r = lax.broadcasted_iota(jnp.int32, shape, 1 if transposed else 0)
    c = lax.broadcasted_iota(jnp.int32, shape, 0 if transposed else 1)
    return (r >> _log2(H)) == ((c & (GLP - 1)) >> _log2(P))


def _expand(t, dims, transposed):
    KB, W2, H, P, GLP = dims
    rep = _replicate(2 * P, W2, P, GLP, transposed).astype(t.dtype)
    wide = jnp.dot(rep, t, preferred_element_type=F32) if transposed else jnp.dot(t, rep, preferred_element_type=F32)
    return jnp.where(_on_diagonal(KB, W2, H, P, GLP, transposed), wide, 0.0).astype(t.dtype)


def _extract(acc, dims):
    KB, W2, H, P, GLP = dims
    rep = _replicate(2 * P, W2, P, GLP, True).astype(BF)
    kept = jnp.where(_on_diagonal(KB, W2, H, P, GLP, False), acc, 0.0)
    hi = kept.astype(BF)
    lo = (kept - hi.astype(F32)).astype(BF)
    return jnp.dot(hi, rep, preferred_element_type=F32) + jnp.dot(lo, rep, preferred_element_type=F32)


def _cmul(ar, ai, sr, si):
    return ar * sr - ai * si, ar * si + ai * sr


def _chunk_order(TL, CH, transposed):
    out_row = lax.broadcasted_iota(jnp.int32, (TL, TL), 1 if transposed else 0)
    in_row = lax.broadcasted_iota(jnp.int32, (TL, TL), 0 if transposed else 1)
    return in_row == ((out_row & 7) << _log2(CH)) + (out_row >> 3)


def _reorder(perm, v):
    hi = v.astype(perm.dtype)
    lo = (v - hi.astype(F32)).astype(perm.dtype)
    return jnp.dot(perm, hi, preferred_element_type=F32) + jnp.dot(perm, lo, preferred_element_type=F32)


def _interleave(main, side):
    n, m, k = len(main), len(side), 0
    for i, step in enumerate(main):
        step()
        while k < m and (k + 1) * n <= (i + 1) * m:
            side[k]()
            k += 1
    for step in side[k:]:
        step()


S5_CHUNK = 512


def s5_fwd(h, tb, tct, pw, dvec, name):
    L, D = h.shape
    nkb, KB, P2 = tb.shape
    P = P2 // 2
    W = (KB // SSM_GROUP) * P
    W2 = 2 * W
    dims = (KB, W2, SSM_GROUP, P, W)
    TL = _tile(L, (512, 256))
    CH = TL // 8
    NB = 2 if nkb % 2 == 0 else 1
    CK = min(S5_CHUNK, W2)

    def body(h_ref, tb_ref, tct_ref, pw_ref, d_ref, s_ref, y_ref, z_ref, bw, cw, perm, unperm, carry):
        t = pl.program_id(1)

        @pl.when(t == 0)
        def _():
            carry[...] = jnp.zeros_like(carry)
            for b in range(NB):
                bw[b] = _expand(tb_ref[b], dims, False)
                cw[b] = _expand(tct_ref[b], dims, True)
            perm[...] = _chunk_order(TL, CH, False).astype(perm.dtype)
            unperm[...] = _chunk_order(TL, CH, True).astype(perm.dtype)

        hp = _reorder(perm[...], h_ref[...])
        hpb = hp.astype(BF)
        first = lax.broadcasted_iota(jnp.int32, (8, W), 0) == 0

        def project(b):
            def chunk(c):
                def emit():
                    s_ref[:, b * W2 + c:b * W2 + c + CK] = jnp.dot(hpb[:, b * KB:(b + 1) * KB], bw[b, :, c:c + CK],
                                                                   preferred_element_type=F32)
                return emit
            return [chunk(c) for c in range(0, W2, CK)]

        def scan(b):
            re, im = slice(b * W2, b * W2 + W), slice(b * W2 + W, (b + 1) * W2)
            ar, ai = pw_ref[b, 0:8, :W], pw_ref[b, 0:8, W:]
            st = {"x": (jnp.zeros((8, W), F32), jnp.zeros((8, W), F32))}

            def own(j):
                def emit():
                    rows = slice(j * 8, j * 8 + 8)
                    xr, xi = _cmul_add(s_ref[rows, re], s_ref[rows, im], ar, ai, *st["x"])
                    s_ref[rows, re] = xr
                    s_ref[rows, im] = xi
                    st["x"] = (xr, xi)
                return emit

            def ends():
                xr, xi = st["x"]
                for k, off in ((1, 8), (2, 16), (4, 24)):
                    xr, xi = _cmul_add(xr, xi, pw_ref[b, off:off + 8, :W], pw_ref[b, off:off + 8, W:],
                                       pltpu.roll(xr, k, 0), pltpu.roll(xi, k, 0))
                xr, xi = _cmul_add(xr, xi, pw_ref[b, 32:40, :W], pw_ref[b, 32:40, W:], carry[b, 0], carry[b, 1])
                st["c"] = (jnp.where(first, carry[b, 0], pltpu.roll(xr, 1, 0)),
                           jnp.where(first, carry[b, 1], pltpu.roll(xi, 1, 0)))
                carry[b, 0] = jnp.broadcast_to(xr[7:8], (8, W))
                carry[b, 1] = jnp.broadcast_to(xi[7:8], (8, W))

            def carried(j):
                def emit():
                    rows = slice(j * 8, j * 8 + 8)
                    cr, ci = _cmul(ar, ai, *st["c"])
                    s_ref[rows, re] = s_ref[rows, re] + cr
                    s_ref[rows, im] = s_ref[rows, im] + ci
                    st["c"] = (cr, ci)
                return emit

            return [own(j) for j in range(CH)] + [ends] + [carried(j) for j in range(CH)]

        def readout(b):
            cols = slice(b * KB, (b + 1) * KB)
            acc = {}

            def chunk(c):
                def emit():
                    part = jnp.dot(s_ref[:, b * W2 + c:b * W2 + c + CK].astype(BF), cw[b, c:c + CK, :],
                                   preferred_element_type=F32)
                    acc["y"] = part if c == 0 else acc["y"] + part
                return emit

            def finish():
                y = acc["y"] + d_ref[:, cols] * hp[:, cols]
                y_ref[:, cols] = y
                z_ref[:, cols] = jnp.dot(unperm[...], _gelu(y).astype(BF),
                                         preferred_element_type=F32).astype(z_ref.dtype)

            return [chunk(c) for c in range(0, W2, CK)] + [finish]

        for emit in project(0):
            emit()
        for b in range(NB):
            side = (project(b + 1) if b + 1 < NB else []) + (readout(b - 1) if b > 0 else [])
            _interleave(scan(b), side)
        for emit in readout(NB - 1):
            emit()

    blk = lambda kb, t: (t, kb)
    per_kb = lambda kb, t: (kb, 0, 0)
    return pl.pallas_call(
        body, name=name, grid=(nkb // NB, L // TL),
        in_specs=[pl.BlockSpec((TL, NB * KB), blk), pl.BlockSpec((NB, KB, P2), per_kb),
                  pl.BlockSpec((NB, P2, KB), per_kb), pl.BlockSpec((NB, 40, W2), per_kb),
                  pl.BlockSpec((1, NB * KB), lambda kb, t: (0, kb))],
        out_specs=[pl.BlockSpec((TL, NB * W2), blk), pl.BlockSpec((TL, NB * KB), blk),
                   pl.BlockSpec((TL, NB * KB), blk)],
        out_shape=[jax.ShapeDtypeStruct((L, nkb * W2), F32), jax.ShapeDtypeStruct((L, D), F32),
                   jax.ShapeDtypeStruct((L, D), BF)],
        scratch_shapes=[pltpu.VMEM((NB, KB, W2), BF), pltpu.VMEM((NB, W2, KB), BF), pltpu.VMEM((TL, TL), BF),
                        pltpu.VMEM((TL, TL), BF), pltpu.VMEM((NB, 2, 8, W), F32)],
        compiler_params=_cp("parallel", "arbitrary"))(h, tb, tct, pw, dvec)


def s5_bwd(dz, y, h, s, tc, tbt, pwr, dvec, name):
    L, D = h.shape
    nkb, KB, P2 = tc.shape
    P = P2 // 2
    W = (KB // SSM_GROUP) * P
    W2 = 2 * W
    dims = (KB, W2, SSM_GROUP, P, W)
    TL = _tile(L, (512, 256))
    CH = TL // 8
    nt = L // TL
    NB = 2 if nkb % 2 == 0 else 1
    CK = min(S5_CHUNK, W2)
    tn = (((0,), (0,)), ((), ()))

    def body(dz_ref, y_ref, h_ref, s_ref, sp_ref, tc_ref, tbt_ref, pw_ref, d_ref,
             dh_ref, dd_ref, da_ref, db_ref, dc_ref, g, ctw, btw, dbacc, dcacc, dys, perm, unperm, carry):
        t = pl.program_id(1)

        @pl.when(t == 0)
        def _():
            carry[...] = jnp.zeros_like(carry)
            dd_ref[...] = jnp.zeros_like(dd_ref)
            da_ref[...] = jnp.zeros_like(da_ref)
            dbacc[...] = jnp.zeros_like(dbacc)
            dcacc[...] = jnp.zeros_like(dcacc)
            for b in range(NB):
                ctw[b] = _expand(tc_ref[b], dims, False)
                btw[b] = _expand(tbt_ref[b], dims, True)
            perm[...] = _chunk_order(TL, CH, False).astype(perm.dtype)
            unperm[...] = _chunk_order(TL, CH, True).astype(perm.dtype)

        hp = jnp.dot(perm[...], h_ref[...].astype(BF), preferred_element_type=F32)
        dy = jnp.dot(perm[...], dz_ref[...].astype(BF), preferred_element_type=F32) * _gelu_grad(y_ref[...])
        dd_ref[...] += _rowsum8(dy * hp)
        dys[...] = dy
        dyb = dy.astype(BF)
        hpb = hp.astype(BF)
        sub = lax.broadcasted_iota(jnp.int32, (8, W), 0)
        live = jnp.where(t == nt - 1, 0.0, 1.0)

        def lead(b):
            cols = slice(b * KB, (b + 1) * KB)

            def to_states(c):
                def emit():
                    g[b, :, c:c + CK] = jnp.dot(dyb[:, cols], ctw[b, :, c:c + CK], preferred_element_type=F32)
                return emit

            def d_c(c):
                def emit():
                    dcacc[b, :, c:c + CK] += lax.dot_general(dyb[:, cols],
                                                             s_ref[:, b * W2 + c:b * W2 + c + CK].astype(BF), tn,
                                                             preferred_element_type=F32)
                return emit

            return [f(c) for c in range(0, W2, CK) for f in (to_states, d_c)]

        def scan(b):
            re, im = slice(b * W2, b * W2 + W), slice(b * W2 + W, (b + 1) * W2)
            ar, ai = pw_ref[b, 0:8, :W], pw_ref[b, 0:8, W:]
            zero = jnp.zeros((8, W), F32)
            st = {"g": (zero, zero), "acc": (zero, zero)}

            def own(j):
                def emit():
                    rows = slice(j * 8, j * 8 + 8)
                    gr, gi = _cmul_add(g[b, rows, :W], g[b, rows, W:], ar, ai, *st["g"])
                    g[b, rows, :W] = gr
                    g[b, rows, W:] = gi
                    st["g"] = (gr, gi)
                return emit

            def ends():
                gr, gi = st["g"]
                for k, off in ((1, 8), (2, 16), (4, 24)):
                    gr, gi = _cmul_add(gr, gi, pw_ref[b, off:off + 8, :W], pw_ref[b, off:off + 8, W:],
                                       pltpu.roll(gr, 8 - k, 0), pltpu.roll(gi, 8 - k, 0))
                gr, gi = _cmul_add(gr, gi, pw_ref[b, 32:40, :W], pw_ref[b, 32:40, W:], carry[b, 0], carry[b, 1])
                st["c"] = (jnp.where(sub == 7, carry[b, 0], pltpu.roll(gr, 7, 0)),
                           jnp.where(sub == 7, carry[b, 1], pltpu.roll(gi, 7, 0)))
                carry[b, 0] = jnp.broadcast_to(gr[0:1], (8, W))
                carry[b, 1] = jnp.broadcast_to(gi[0:1], (8, W))

            def carried(j):
                def emit():
                    rows = slice(j * 8, j * 8 + 8)
                    cr, ci = _cmul(ar, ai, *st["c"])
                    gr, gi = g[b, rows, :W] + cr, g[b, rows, W:] + ci
                    g[b, rows, :W] = gr
                    g[b, rows, W:] = gi
                    if j > 0:
                        before = slice(j * 8 - 8, j * 8)
                        pr, pi = s_ref[before, re], s_ref[before, im]
                    else:
                        last = slice(TL - 8, TL)
                        pr = jnp.where(sub == 0, sp_ref[7:8, re] * live, pltpu.roll(s_ref[last, re], 1, 0))
                        pi = jnp.where(sub == 0, sp_ref[7:8, im] * live, pltpu.roll(s_ref[last, im], 1, 0))
                    accr, acci = st["acc"]
                    st["c"] = (cr, ci)
                    st["acc"] = (accr + pr * gr + pi * gi, acci + pr * gi - pi * gr)
                return emit

            def done():
                da_ref[b, :, :W] += st["acc"][0]
                da_ref[b, :, W:] += st["acc"][1]

            return ([own(j) for j in reversed(range(CH))] + [ends] + [carried(j) for j in reversed(range(CH))]
                    + [done])

        def tail(b):
            cols = slice(b * KB, (b + 1) * KB)
            acc = {}

            def d_u(c):
                def emit():
                    part = jnp.dot(g[b, :, c:c + CK].astype(BF), btw[b, c:c + CK, :], preferred_element_type=F32)
                    acc["u"] = part if c == 0 else acc["u"] + part
                return emit

            def d_b(c):
                def emit():
                    dbacc[b, :, c:c + CK] += lax.dot_general(hpb[:, cols], g[b, :, c:c + CK].astype(BF), tn,
                                                             preferred_element_type=F32)
                return emit

            def finish():
                dh = (dys[:, cols] * d_ref[:, cols] + acc["u"]).astype(BF)
                dh_ref[:, cols] = jnp.dot(unperm[...], dh, preferred_element_type=F32).astype(dh_ref.dtype)

            return [f(c) for c in range(0, W2, CK) for f in (d_u, d_b)] + [finish]

        for emit in lead(0):
            emit()
        for b in range(NB):
            side = (lead(b + 1) if b + 1 < NB else []) + (tail(b - 1) if b > 0 else [])
            _interleave(scan(b), side)
        for emit in tail(NB - 1):
            emit()

        @pl.when(t == nt - 1)
        def _():
            for b in range(NB):
                db_ref[b] = _extract(dbacc[b], dims)
                dc_ref[b] = _extract(dcacc[b], dims)

    rev = lambda kb, t: (nt - 1 - t, kb)
    prev = lambda kb, t: (jnp.maximum((nt - 1 - t) * CH - 1, 0), kb)
    per_kb = lambda kb, t: (kb, 0, 0)
    return pl.pallas_call(
        body, name=name, grid=(nkb // NB, nt),
        in_specs=[pl.BlockSpec((TL, NB * KB), rev), pl.BlockSpec((TL, NB * KB), rev),
                  pl.BlockSpec((TL, NB * KB), rev), pl.BlockSpec((TL, NB * W2), rev),
                  pl.BlockSpec((8, NB * W2), prev), pl.BlockSpec((NB, KB, P2), per_kb),
                  pl.BlockSpec((NB, P2, KB), per_kb), pl.BlockSpec((NB, 40, W2), per_kb),
                  pl.BlockSpec((1, NB * KB), lambda kb, t: (0, kb))],
        out_specs=[pl.BlockSpec((TL, NB * KB), rev), pl.BlockSpec((8, NB * KB), lambda kb, t: (0, kb)),
                   pl.BlockSpec((NB, 8, W2), per_kb), pl.BlockSpec((NB, KB, P2), per_kb),
                   pl.BlockSpec((NB, KB, P2), per_kb)],
        out_shape=[jax.ShapeDtypeStruct((L, D), BF), jax.ShapeDtypeStruct((8, D), F32),
                   jax.ShapeDtypeStruct((nkb, 8, W2), F32), jax.ShapeDtypeStruct((nkb, KB, P2), F32),
                   jax.ShapeDtypeStruct((nkb, KB, P2), F32)],
        scratch_shapes=[pltpu.VMEM((NB, TL, W2), F32), pltpu.VMEM((NB, KB, W2), BF), pltpu.VMEM((NB, W2, KB), BF),
                        pltpu.VMEM((NB, KB, W2), F32), pltpu.VMEM((NB, KB, W2), F32), pltpu.VMEM((TL, NB * KB), F32),
                        pltpu.VMEM((TL, TL), BF), pltpu.VMEM((TL, TL), BF), pltpu.VMEM((NB, 2, 8, W), F32)],
        compiler_params=pltpu.CompilerParams(dimension_semantics=("parallel", "arbitrary"),
                                             vmem_limit_bytes=V7X_VMEM_BYTES - 4 * 1024 * 1024),
    )(dz, y, h, s, s, tc, tbt, pwr, dvec)


def _discretise(a_re, a_im, log_step, b_re, b_im):
    lr = jnp.minimum(a_re, -1e-4)
    li = a_im
    dt = jnp.exp(log_step)[:, None]
    mag = jnp.exp(lr * dt)
    abr = mag * jnp.cos(li * dt)
    abi = mag * jnp.sin(li * dt)
    den = lr * lr + li * li
    qr = ((abr - 1.0) * lr + abi * li) / den
    qi = (abi * lr - (abr - 1.0) * li) / den
    bbar_re = qr[..., None] * b_re - qi[..., None] * b_im
    bbar_im = qr[..., None] * b_im + qi[..., None] * b_re
    return abr, abi, bbar_re, bbar_im


def _compact(m_re, m_im, nkb):
    G, H, P = m_re.shape
    t = jnp.stack([m_re, m_im], axis=2).reshape(nkb, (G // nkb) * H, 2 * P).astype(BF)
    return t, jnp.swapaxes(t, 1, 2)


def _scan_powers(abr, abi, nkb, conj, CH):
    G, P = abr.shape
    if conj:
        abi = -abi

    def cmul(u, v):
        return u[0] * v[0] - u[1] * v[1], u[0] * v[1] + u[1] * v[0]

    q = (abr, abi)
    for _ in range(_log2(CH)):
        q = cmul(q, q)
    pows = [q]
    for _ in range(7):
        pows.append(cmul(pows[-1], q))
    row = jnp.arange(8)[:, None, None]

    def table(part):
        out = [jnp.broadcast_to((abr, abi)[part][None], (8, G, P))]
        for k in (1, 2, 4):
            keep = (row <= 7 - k) if conj else (row >= k)
            out.append(jnp.where(keep, pows[k - 1][part][None], 0.0))
        ends = jnp.stack([p[part] for p in pows])
        out.append(ends[::-1] if conj else ends)
        return jnp.concatenate(out, axis=0)

    GL = G // nkb
    t = jnp.stack([table(0), table(1)], axis=1)
    t = t.reshape(40, 2, nkb, GL * P).transpose(2, 0, 1, 3)
    return t.reshape(nkb, 40, 2 * GL * P)


def ada_mods(c_all, w_ada, b_sh, name):
    nl, D, NA = w_ada.shape

    def body(c_ref, w_ref, b_ref, o_ref):
        cv = c_ref[...]
        act = cv * _sigmoid(cv)
        o_ref[...] = jnp.dot(act, w_ref[...], preferred_element_type=F32, precision=lax.Precision.HIGHEST) + b_ref[...]

    return pl.pallas_call(
        body, name=name, grid=(nl,),
        in_specs=[pl.BlockSpec((8, D), lambda i: (0, 0)), pl.BlockSpec((None, D, NA), lambda i: (i, 0, 0)),
                  pl.BlockSpec((None, 1, NA), lambda i: (i, 0, 0))],
        out_specs=pl.BlockSpec((None, 8, NA), lambda i: (i, 0, 0)),
        out_shape=jax.ShapeDtypeStruct((nl, 8, NA), F32), compiler_params=_cp("parallel"))(c_all, w_ada, b_sh)


def _adamw(w, g, m, v):
    m = ADAM_B1 * m + (1.0 - ADAM_B1) * g
    v = ADAM_B2 * v + (1.0 - ADAM_B2) * (g * g)
    m_hat = m / (1.0 - ADAM_B1 ** ADAM_STEP)
    v_hat = v / (1.0 - ADAM_B2 ** ADAM_STEP)
    return -ADAM_LR * (m_hat / (jnp.sqrt(v_hat) + ADAM_EPS) + ADAM_WD * w), m, v


def _adam_rows(R, C):
    cap = max(8, (256 * 1024) // C)
    for t in range(min(R, cap), 0, -1):
        if R % t == 0 and (t % 8 == 0 or t == R):
            return t
    return R


def adamw_ada(c_t, dm, w, m, v, name):
    nl, D, NA = w.shape
    TK = _tile(D, (256, 128))

    def body(c_ref, dm_ref, w_ref, m_ref, v_ref, g_ref, d_ref, nm_ref, nv_ref):
        cv = c_ref[...]
        act = cv * _sigmoid(cv)
        g = jnp.dot(act, dm_ref[...], preferred_element_type=F32, precision=lax.Precision.HIGHEST)
        g_ref[...] = g
        d_ref[...], nm_ref[...], nv_ref[...] = _adamw(w_ref[...], g, m_ref[...], v_ref[...])

    big = pl.BlockSpec((None, TK, NA), lambda i, k: (i, k, 0))
    shape = jax.ShapeDtypeStruct(w.shape, F32)
    return pl.pallas_call(
        body, name=name, grid=(nl, D // TK),
        in_specs=[pl.BlockSpec((TK, 8), lambda i, k: (k, 0)), pl.BlockSpec((None, 8, NA), lambda i, k: (i, 0, 0)),
                  big, big, big],
        out_specs=[big] * 4, out_shape=[shape] * 4, compiler_params=_cp("parallel", "parallel"))(c_t, dm, w, m, v)


def adamw_sharded(w, m, v, ga, gb, name):
    nl, R, C = w.shape
    TR = _adam_rows(R, C)

    def body(w_ref, m_ref, v_ref, a_ref, b_ref, g_ref, d_ref, nm_ref, nv_ref):
        g = a_ref[...] + b_ref[...]
        g_ref[...] = g
        d_ref[...], nm_ref[...], nv_ref[...] = _adamw(w_ref[...], g, m_ref[...], v_ref[...])

    big = pl.BlockSpec((None, TR, C), lambda i, r: (i, r, 0))
    shape = jax.ShapeDtypeStruct(w.shape, F32)
    return pl.pallas_call(
        body, name=name, grid=(nl, R // TR), in_specs=[big] * 5,
        out_specs=[big] * 4, out_shape=[shape] * 4, compiler_params=_cp("parallel", "parallel"))(w, m, v, ga, gb)


def adamw_slab(g, w, m, v, name):
    R, C = g.shape
    TR = _tile(R, (160, 80, 40, 8))

    def body(g_ref, w_ref, m_ref, v_ref, d_ref, nm_ref, nv_ref):
        d_ref[...], nm_ref[...], nv_ref[...] = _adamw(w_ref[...], g_ref[...], m_ref[...], v_ref[...])

    big = pl.BlockSpec((TR, C), lambda r: (r, 0))
    shape = jax.ShapeDtypeStruct((R, C), F32)
    return pl.pallas_call(
        body, name=name, grid=(R // TR,), in_specs=[big] * 4,
        out_specs=[big] * 3, out_shape=[shape] * 3, compiler_params=_cp("parallel"))(g, w, m, v)


def adamw_plain(w, m, v, g, name):
    def body(w_ref, m_ref, v_ref, g_ref, d_ref, nm_ref, nv_ref):
        d_ref[...], nm_ref[...], nv_ref[...] = _adamw(w_ref[...], g_ref[...], m_ref[...], v_ref[...])

    shape = jax.ShapeDtypeStruct(w.shape, F32)
    return pl.pallas_call(body, name=name, out_shape=[shape] * 3,
                          compiler_params=pltpu.CompilerParams(vmem_limit_bytes=VMEM_LIMIT))(w, m, v, g)


def _slab_rows(a):
    n = a.size
    rows = -(-n // SLAB_W)
    return -(-rows // 8) * 8


def _pack(arrs, pad_rows_to=0):
    out = []
    for a in arrs:
        rows = _slab_rows(a)
        flat = a.reshape(-1).astype(F32)
        flat = jnp.pad(flat, (0, rows * SLAB_W - flat.shape[0]))
        out.append(flat.reshape(rows, SLAB_W))
    total = sum(o.shape[0] for o in out)
    if pad_rows_to and total % pad_rows_to:
        out.append(jnp.zeros((pad_rows_to - total % pad_rows_to, SLAB_W), F32))
    return jnp.concatenate(out, axis=0)


def _unpack(slab, like):
    out, r = [], 0
    for a in like:
        rows = _slab_rows(a)
        out.append(slab[r:r + rows].reshape(-1)[:a.size].reshape(a.shape))
        r += rows
    return out


WEIGHTS = ['norm1_g', 'norm2_g', 'w_ada', 'b_ada', 'ssm_a_re', 'ssm_a_im', 'ssm_log_step', 'ssm_b_re', 'ssm_b_im',
           'ssm_c_re', 'ssm_c_im', 'ssm_d', 'ssm_w_out', 'conv_w_in', 'conv_w', 'conv_w_out', 'w_ffn_in',
           'w_ffn_out', 'final_g']
SLAB = ['norm1_g', 'norm2_g', 'b_ada', 'ssm_a_re', 'ssm_a_im', 'ssm_log_step', 'ssm_b_re', 'ssm_b_im', 'ssm_c_re',
        'ssm_c_im', 'ssm_d', 'final_g']
SHARDED = ['ssm_w_out', 'conv_w_in', 'conv_w_out', 'w_ffn_in', 'w_ffn_out']


def kernel(x, c, norm1_g, norm2_g, w_ada, b_ada, ssm_a_re, ssm_a_im, ssm_log_step, ssm_b_re, ssm_b_im, ssm_c_re, ssm_c_im, ssm_d, ssm_w_out, conv_w_in, conv_w, conv_w_out, w_ffn_in, w_ffn_out, final_g, loss_target, m_norm1_g, m_norm2_g, m_w_ada, m_b_ada, m_ssm_a_re, m_ssm_a_im, m_ssm_log_step, m_ssm_b_re, m_ssm_b_im, m_ssm_c_re, m_ssm_c_im, m_ssm_d, m_ssm_w_out, m_conv_w_in, m_conv_w, m_conv_w_out, m_w_ffn_in, m_w_ffn_out, m_final_g, v_norm1_g, v_norm2_g, v_w_ada, v_b_ada, v_ssm_a_re, v_ssm_a_im, v_ssm_log_step, v_ssm_b_re, v_ssm_b_im, v_ssm_c_re, v_ssm_c_im, v_ssm_d, v_ssm_w_out, v_conv_w_in, v_conv_w, v_conv_w_out, v_w_ffn_in, v_w_ffn_out, v_final_g):
    given = dict(locals())
    W = {n: given[n] for n in WEIGHTS}
    Mo = {n: given["m_" + n] for n in WEIGHTS}
    Vo = {n: given["v_" + n] for n in WEIGHTS}

    xs = x[0]
    tgt = loss_target[0]
    L, D = xs.shape
    nlayer = norm1_g.shape[0]
    NA = w_ada.shape[2]
    G = ssm_a_re.shape[1]
    nkb = D // S5_BLOCK
    ax, ay, ac = _axes()
    me = 4 * ax + 2 * ay + ac
    chip = 2 * ax + ay

    assert D == SLAB_W
    first = gather8(jnp.concatenate([jnp.broadcast_to(c, (8, D)), _pack([conv_w])], axis=0), "gather_c_conv_w")
    c_all = first[:, 0, :]
    b_sh = lax.dynamic_slice_in_dim(b_ada, chip * NA, NA, axis=1)[:, None, :]
    mods_part = ada_mods(c_all, w_ada, b_sh, "ada_mods")
    mg = gather8(mods_part.reshape(nlayer * 8, NA), "gather_mods")
    mg = mg.reshape(N_CHIP, 2, nlayer, 8, NA)[:, 0]
    mods_all = lax.dynamic_index_in_dim(mg, me, axis=2, keepdims=False)
    mods_all = jnp.transpose(mods_all, (1, 0, 2)).reshape(nlayer, 6, D)

    cw_parts = first[:, 8:]
    nconv = conv_w.shape[0]
    cw_full = jnp.stack([_unpack(cw_parts[2 * q], [conv_w])[0] for q in range(N_CHIP)], axis=2)
    cw_full = cw_full.reshape(nconv, 3, D)

    in_flight_w = {}

    def start_weights(i, after):
        names = (["ssm_w_out"] if i % 2 == 0 else ["conv_w_in", "conv_w_out"]) + ["w_ffn_in", "w_ffn_out"]
        shards = [W[n][i if n.startswith("w_ffn") else i // 2].astype(BF) for n in names]
        sems, srcs, lands, tok = gather_start(shards, after, "gather_start%d" % i)
        in_flight_w[i] = (names, sems, srcs, lands)
        return tok

    def relay_weights(i, after):
        names, sems, srcs, lands = in_flight_w[i]
        got = gather_wait(sems, srcs, lands, list(range(len(names))), after, "gather_wait%d" % i)
        rsems, rlands, tok = relay_start(got, after, "relay_start%d" % i)
        in_flight_w[i] = (names, rsems, rlands)
        return tok

    def layer_weights(i, after):
        names, rsems, rlands = in_flight_w[i]
        return dict(zip(names, relay_wait(rsems, rlands, after, "relay_wait%d" % i)))

    token = start_weights(0, cw_full + mods_all[0, 0:3])
    mods_all = mods_all + token[0:1, 0:1]

    s5 = []
    for j in range(ssm_a_re.shape[0]):
        disc, disc_vjp = jax.vjp(_discretise, ssm_a_re[j], ssm_a_im[j], ssm_log_step[j], ssm_b_re[j], ssm_b_im[j])
        abr, abi, bbar_re, bbar_im = disc
        tb, tbt = _compact(jnp.swapaxes(bbar_re, 1, 2), jnp.swapaxes(bbar_im, 1, 2), nkb)
        tc, tct = _compact(ssm_c_re[j], -ssm_c_im[j], nkb)
        chunk = _tile(L, (512, 256)) // 8
        s5.append(dict(vjp=disc_vjp, tb=tb, tbt=tbt, tc=tc, tct=tct, pw=_scan_powers(abr, abi, nkb, False, chunk),
                       pwr=_scan_powers(abr, abi, nkb, True, chunk)))

    saved = []
    xcur = xs
    for i in range(nlayer):
        j = i // 2
        mods = mods_all[i]
        sv = dict(x=xcur)
        if i % 2 == 0:
            h = norm_mod(xcur, norm1_g[i:i + 1], mods, 0, F32, "norm_mod_s5")
            dvec = ssm_d[j:j + 1]
            if i == 0:
                dvec = dvec + start_weights(1, h)[0:1, 0:1]
            states, yv, z = s5_fwd(h, s5[j]["tb"], s5[j]["tct"], s5[j]["pw"], dvec, "s5_fwd")
            if i == 0:
                mods = mods + relay_weights(0, z)[0:1, 0:1]
            full = layer_weights(i, z)
            o, mix, x2 = ssm_out_glu(z, full["ssm_w_out"], xcur, mods, 2, "ssm_out_glu")
            sv.update(h=h, states=states, y=yv, z=z, o=o)
        else:
            h = norm_mod(xcur, norm1_g[i:i + 1], mods, 0, BF, "norm_mod")
            full = layer_weights(i, h)
            p = mm_nn(h, full["conv_w_in"], BF, "mm_conv_in")
            mc = conv_fwd(p, cw_full[j], "conv_fwd")
            mix, x2 = mm_nn(mc, full["conv_w_out"].reshape(1, D, D), BF, "mm_conv_out", res=xcur, gate=mods[2:3])
            sv.update(h=h, p=p, mc=mc)
        h2 = norm_mod(x2, norm2_g[i:i + 1], mods, 3, BF, "norm_mod")
        gu, act = ffn_in_act(h2, full["w_ffn_in"], "ffn_in_act")
        F = act.shape[1]
        ff, x3 = mm_nn(act, full["w_ffn_out"].reshape(1, F, D), BF, "mm_ffn_out", res=x2, gate=mods[5:6])
        sv.update(mix=mix, x2=x2, h2=h2, gu=gu, act=act, ff=ff, w=full)
        saved.append(sv)
        xcur = x3
        if i + 1 < nlayer:
            token = relay_weights(i + 1, ff)
            if i + 2 < nlayer:
                token = token + start_weights(i + 2, token)
            mods_all = mods_all + token[0:1, 0:1]

    loss_blk, dx, dfinal, dff = final_loss(xcur, tgt, final_g[None, :], saved[-1]["ff"], mods_all[nlayer - 1], 5,
                                           "final_loss")
    dg2 = dfinal[1:2]

    gland = {n: lax.empty((W[n].shape[0], N_CHIP) + W[n].shape[1:], BF) for n in SHARDED}
    in_flight = []
    dmods = [None] * nlayer
    dnorm1, dnorm2 = [None] * nlayer, [None] * nlayer
    dconv_w = [None] * nconv
    ds5 = [None] * ssm_a_re.shape[0]
    token = jnp.zeros((8, 128), F32)

    def send_grads(names, grads, slot, after, name):
        sems, thru, lands, tok = scatter_start([grads[n] for n in names], [gland[n] for n in names], slot, after, name)
        gland.update(zip(names, lands))
        in_flight.append((names, slot, sems, thru, name))
        return tok

    def land_grads(group, after):
        for names, slot, sems, thru, name in in_flight:
            if names[0] in group:
                got = scatter_wait(sems, thru, [gland[n] for n in names], slot, after, name.replace("scatter", "landed"))
                gland.update(zip(names, got))

    for i in reversed(range(nlayer)):
        j = i // 2
        mods = mods_all[i] + token[0:1, 0:1]
        sv = saved[i]
        full = sv["w"]
        gfull = {}
        F = sv["act"].shape[1]
        gfull["w_ffn_out"] = mm_tn(sv["act"], dff, 1, "mm_tn_ffn_out").reshape(N_CHIP, F // N_CHIP, D)
        dgu = ffn_out_bwd(dff, full["w_ffn_out"].reshape(F, D), sv["gu"], "ffn_out_bwd")
        gfull["w_ffn_in"] = mm_tn(sv["h2"], dgu, N_CHIP, "mm_tn_ffn_in")
        dh2 = mm_nt(dgu, full["w_ffn_in"], BF, "mm_nt_ffn_in")
        token = send_grads(["w_ffn_out", "w_ffn_in"], gfull, [i, i], dh2, "scatter_ffn%d" % i)
        mods = mods + token[0:1, 0:1]
        dx2, s2, dmix = norm_bwd(dh2, sv["x2"], dx, norm2_g[i:i + 1], mods, 3, "norm_bwd_mix",
                                 branch=(sv["mix"], mods, 2))
        dg1 = s2[3:4]
        if i % 2 == 0:
            do = glu_bwd(dmix, sv["o"], "glu_bwd")
            gfull["ssm_w_out"] = mm_tn(sv["z"], do, N_CHIP, "mm_tn_ssm_out")
            dz = mm_nt(do, full["ssm_w_out"], BF, "mm_nt_ssm_out")
            dh, dd, dab, db, dc = s5_bwd(dz, sv["y"], sv["h"], sv["states"], s5[j]["tc"], s5[j]["tbt"], s5[j]["pwr"],
                                         ssm_d[j:j + 1], "s5_bwd")
            ds5[j] = (dd, dab, db, dc)
        else:
            gfull["conv_w_out"] = mm_tn(sv["mc"], dmix, 1, "mm_tn_conv_out").reshape(N_CHIP, D // N_CHIP, D)
            dmc = mm_nt(dmix, full["conv_w_out"].reshape(1, D, D), BF, "mm_nt_conv_out")
            dbg, dcg, dvv, dcw = conv_bwd(dmc, sv["p"], cw_full[j], "conv_bwd")
            dp = jnp.concatenate([dbg, dcg, dvv], axis=1)
            gfull["conv_w_in"] = mm_tn(sv["h"], dp, N_CHIP, "mm_tn_conv_in")
            dh = mm_nt(dp, full["conv_w_in"], BF, "mm_nt_conv_in")
            dconv_w[j] = dcw[0:3]
        dmods_i = [s2[0:2], dg2]
        if i > 0:
            dx, s1, dff = norm_bwd(dh, sv["x"], dx2, norm1_g[i:i + 1], mods, 0, "norm_bwd_ffn",
                                   branch=(saved[i - 1]["ff"], mods_all[i - 1], 5))
            dg2 = s1[3:4]
        else:
            dx, s1 = norm_bwd(dh, sv["x"], dx2, norm1_g[i:i + 1], mods, 0, "norm_bwd")
        dmods[i] = jnp.concatenate([s1[0:2], dg1] + dmods_i, axis=0).reshape(6 * D)
        dnorm1[i], dnorm2[i] = s1[2], s2[2]
        names = ["ssm_w_out"] if i % 2 == 0 else ["conv_w_out", "conv_w_in"]
        token = send_grads(names, gfull, [j] * len(names), dx, "scatter_mix%d" % i)

    small = dict(norm1_g=jnp.stack(dnorm1), norm2_g=jnp.stack(dnorm2), b_ada=jnp.stack(dmods),
                 final_g=dfinal[0] + token[0, 0])
    per = {n: [] for n in ('ssm_a_re', 'ssm_a_im', 'ssm_log_step', 'ssm_b_re', 'ssm_b_im', 'ssm_c_re', 'ssm_c_im', 'ssm_d')}
    GL = G // nkb
    for j, (dd, dab, db, dc) in enumerate(ds5):
        dab = jnp.sum(dab, axis=1).reshape(nkb, 2, GL, SSM_STATE)
        g_abr, g_abi = dab[:, 0].reshape(G, SSM_STATE), dab[:, 1].reshape(G, SSM_STATE)
        db, dc = db.reshape(G, SSM_GROUP, 2, SSM_STATE), dc.reshape(G, SSM_GROUP, 2, SSM_STATE)
        gb_re, gb_im, gc_re, gc_im = db[:, :, 0], db[:, :, 1], dc[:, :, 0], dc[:, :, 1]
        ga_re, ga_im, gls, gbr, gbi = s5[j]["vjp"]((g_abr, g_abi, jnp.swapaxes(gb_re, 1, 2), jnp.swapaxes(gb_im, 1, 2)))
        for n, val in zip(per, (ga_re, ga_im, gls, gbr, gbi, gc_re, -gc_im, jnp.sum(dd, axis=0))):
            per[n].append(val)
    small.update({n: jnp.stack(vals) for n, vals in per.items()})
    dcw_full = jnp.stack(dconv_w)

    my_loss = loss_blk[0:1, 0:1]
    slab_like = [W[n] for n in SLAB] + [dcw_full, my_loss]
    rows64 = 8 * N_DEV
    slab = _pack([small[n] for n in SLAB] + [dcw_full, my_loss], rows64)
    per_dev = slab.shape[0] // N_DEV
    x_sems, x_srcs, x_lands, token = exchange_start(
        [(slab.reshape(N_DEV, per_dev, SLAB_W), True), (_pack([small["b_ada"]]), False)], dx, "small_scatter")

    early = [n for n in SHARDED if n != "ssm_w_out"]
    land_grads(early, token)
    mine = [reduce4(gland[n], "reduce4_" + n) for n in early]

    parts, dm_all = exchange_wait(x_sems, x_srcs, x_lands, [True, False], mine[-1][0, :8, :128], "small_landed")
    t_sems, t_srcs, t_lands, token = exchange_start([(sum8(parts, "sum_small"), False)], dm_all, "small_gather")
    out = {}

    w_sems, w_srcs, w_lands, token2 = swap_start(mine, "swap_start")
    dm_all = dm_all.reshape(N_DEV, -1)[:, :b_ada.size].reshape(N_DEV, nlayer, N_CHIP, NA)
    dm_sh = jnp.transpose(lax.dynamic_index_in_dim(dm_all, chip, axis=2, keepdims=False), (1, 0, 2))
    res = adamw_ada(jnp.transpose(c_all) + token[0:1, 0:1] + token2[0:1, 0:1], dm_sh, w_ada, m_w_ada, v_w_ada,
                    "adamw_ada")
    out["g", "w_ada"], out["d", "w_ada"], out["m", "w_ada"], out["v", "w_ada"] = res

    g_slab = exchange_wait(t_sems, t_srcs, t_lands, [False], out["g", "w_ada"], "small_total")[0]
    g_slab = g_slab.reshape(slab.shape)
    d_slab, m_slab, v_slab = adamw_slab(
        g_slab, _pack([W[n] for n in SLAB] + [jnp.zeros_like(dcw_full)], rows64),
        _pack([Mo[n] for n in SLAB] + [jnp.zeros_like(dcw_full)], rows64),
        _pack([Vo[n] for n in SLAB] + [jnp.ones_like(dcw_full)], rows64), "adamw_slab")
    for k, slab in zip(("g", "d", "m", "v"), (g_slab, d_slab, m_slab, v_slab)):
        for n, val in zip(SLAB, _unpack(slab, slab_like)):
            out[k, n] = val
    g_cw = lax.dynamic_slice_in_dim(_unpack(g_slab, slab_like)[-2], chip * conv_w.shape[2], conv_w.shape[2], axis=2)
    out["g", "conv_w"] = g_cw
    out["d", "conv_w"], out["m", "conv_w"], out["v", "conv_w"] = [
        r.reshape(conv_w.shape) for r in adamw_plain(conv_w.reshape(-1, conv_w.shape[2]), m_conv_w.reshape(-1, conv_w.shape[2]),
                                                     v_conv_w.reshape(-1, conv_w.shape[2]), g_cw.reshape(-1, conv_w.shape[2]),
                                                     "adamw_conv_w")]

    mine, theirs = swap_wait(w_sems, w_srcs, w_lands, d_slab, "swap_wait")
    for n, ga, gb in zip(early, mine, theirs):
        r = adamw_sharded(W[n], Mo[n], Vo[n], ga, gb, "adamw_" + n)
        out["g", n], out["d", n], out["m", n], out["v", n] = r

    land_grads(["ssm_w_out"], out["g", "w_ffn_out"])
    ga = reduce4(gland["ssm_w_out"], "reduce4_ssm_w_out")
    gb = swap_siblings([ga], "swap_siblings")[0]
    r = adamw_sharded(ssm_w_out, m_ssm_w_out, v_ssm_w_out, ga, gb, "adamw_ssm_w_out")
    out["g", "ssm_w_out"], out["d", "ssm_w_out"], out["m", "ssm_w_out"], out["v", "ssm_w_out"] = r

    loss = _unpack(g_slab, slab_like)[-1][0, 0]
    return (loss, dx[None], *[out["g", n] for n in WEIGHTS], *[out["d", n] for n in WEIGHTS],
            *[out["m", n] for n in WEIGHTS], *[out["v", n] for n in WEIGHTS])
```

```python
import math

import jax
import jax.numpy as jnp
from jax import lax
from jax.experimental import pallas as pl
from jax.experimental.pallas import tpu as pltpu

F32 = jnp.float32
BF = jnp.bfloat16
MESH = pl.DeviceIdType.MESH
ANY = pl.BlockSpec(memory_space=pl.ANY)

N_DEV = 8
N_CHIP = 4
SSM_GROUP = 16
SSM_STATE = 64
S5_BLOCK = 256
RMS_EPS = 1e-6
ADAM_LR, ADAM_B1, ADAM_B2, ADAM_EPS, ADAM_WD, ADAM_STEP = 0.001, 0.9, 0.999, 1e-08, 0.01, 10
V7X_VMEM_BYTES = 64 * 1024 * 1024
VMEM_LIMIT = V7X_VMEM_BYTES - 12 * 1024 * 1024
SLAB_W = 1024
GELU_C = math.sqrt(2.0 / math.pi)
GELU_A = 0.044715


def _cp(*sem):
    return pltpu.CompilerParams(dimension_semantics=sem if sem else None, vmem_limit_bytes=VMEM_LIMIT)


def _tile(n, prefs):
    for p in prefs:
        if p <= n and n % p == 0:
            return p
    return n


def _sigmoid(v):
    return 0.5 * jnp.tanh(0.5 * v) + 0.5


def _axes():
    return lax.axis_index("x"), lax.axis_index("y"), lax.axis_index("c")


def _flip(v, k):
    return 1 - v if k else v


def gather8(v, name):
    R, C = v.shape

    def body(v_ref, o_ref, ssem, rsem, lsem):
        x, y, c = _axes()
        me = 4 * x + 2 * y + c
        loc = pltpu.make_async_copy(v_ref, o_ref.at[me], lsem)
        loc.start()
        copies = []
        for k in range(1, N_DEV):
            peer = (_flip(x, (k >> 2) & 1), _flip(y, (k >> 1) & 1), _flip(c, k & 1))
            cp = pltpu.make_async_remote_copy(src_ref=v_ref, dst_ref=o_ref.at[me], send_sem=ssem.at[k - 1],
                                              recv_sem=rsem.at[k - 1], device_id=peer, device_id_type=MESH)
            cp.start()
            copies.append(cp)
        for cp in copies:
            cp.wait()
        loc.wait()

    return pl.pallas_call(
        body, name=name,
        out_shape=jax.ShapeDtypeStruct((N_DEV, R, C), v.dtype),
        in_specs=[pl.BlockSpec(memory_space=pltpu.VMEM)],
        out_specs=pl.BlockSpec(memory_space=pltpu.VMEM),
        scratch_shapes=[pltpu.SemaphoreType.DMA((N_DEV - 1,)), pltpu.SemaphoreType.DMA((N_DEV - 1,)),
                        pltpu.SemaphoreType.DMA],
        compiler_params=pltpu.CompilerParams(vmem_limit_bytes=VMEM_LIMIT),
    )(v)


HBM = pl.BlockSpec(memory_space=pltpu.HBM)
SEM = pl.BlockSpec(memory_space=pltpu.SEMAPHORE)
EFFECT = pltpu.SideEffectType.DATAFLOW_SIDE_EFFECTING


def _in_hbm(a):
    return pltpu.with_memory_space_constraint(a, pltpu.HBM)


def _chip_peers(x, y, c):
    out = []
    for k in range(1, N_CHIP):
        px, py = _flip(x, k >> 1), _flip(y, k & 1)
        out.append(((px, py, c), 2 * px + py))
    return out


def _my_half(ref, c):
    rows = ref.shape[0] // 2
    return pl.ds(pl.multiple_of(c * rows, 16), rows)


def relay_start(lands, after, name):
    n = len(lands)

    def body(*refs):
        land = refs[:n]
        ssem, rsem = refs[n + 1:n + 3]
        token = refs[-1]
        x, y, c = _axes()
        for a in range(n):
            half = _my_half(land[a].at[0], c)
            for k, (_, pchip) in enumerate(_chip_peers(x, y, c)):
                pltpu.make_async_remote_copy(src_ref=land[a].at[pchip, half], dst_ref=land[a].at[pchip, half],
                                             send_sem=ssem.at[3 * a + k], recv_sem=rsem.at[3 * a + k],
                                             device_id=(x, y, 1 - c), device_id_type=MESH).start()
        token[...] = jnp.zeros_like(token)

    out_shape = ([pltpu.SemaphoreType.DMA((3 * n,)), pltpu.SemaphoreType.DMA((3 * n,))]
                 + [pltpu.HBM(l.shape, l.dtype) for l in lands] + [jax.ShapeDtypeStruct((8, 128), F32)])
    res = pl.pallas_call(
        body, name=name, out_shape=out_shape, in_specs=[HBM] * n + [ANY],
        out_specs=[SEM, SEM] + [HBM] * n + [pl.BlockSpec(memory_space=pltpu.VMEM)],
        input_output_aliases={a: 2 + a for a in range(n)},
        compiler_params=pltpu.CompilerParams(has_side_effects=EFFECT),
    )(*lands, after)
    return tuple(res[:2]), list(res[2:2 + n]), res[-1]


def relay_wait(sems, lands, after, name):
    n = len(lands)

    def body(*refs):
        land = refs[:n]
        ssem, rsem = refs[n:n + 2]
        x, y, c = _axes()
        for a in range(n):
            mine, theirs = _my_half(land[a].at[0], c), _my_half(land[a].at[0], 1 - c)
            for k, (_, pchip) in enumerate(_chip_peers(x, y, c)):
                cp = pltpu.make_async_remote_copy(src_ref=land[a].at[pchip, mine], dst_ref=land[a].at[pchip, theirs],
                                                  send_sem=ssem.at[3 * a + k], recv_sem=rsem.at[3 * a + k],
                                                  device_id=(x, y, 1 - c), device_id_type=MESH)
                cp.wait_send()
                cp.wait_recv()

    res = pl.pallas_call(
        body, name=name, out_shape=[pltpu.HBM(l.shape, l.dtype) for l in lands],
        in_specs=[HBM] * n + [SEM, SEM, ANY], out_specs=[HBM] * n,
        input_output_aliases={a: a for a in range(n)},
        compiler_params=pltpu.CompilerParams(has_side_effects=EFFECT),
    )(*lands, *sems, after)
    return list(res)


def gather_start(shards, after, name):
    n = len(shards)

    def body(*refs):
        src, land = refs[:n], refs[n:2 * n]
        ssem, rsem, lsem = refs[2 * n + 1:2 * n + 4]
        token = refs[-1]
        x, y, c = _axes()
        chip = 2 * x + y
        for a in range(n):
            pltpu.make_async_copy(src[a], land[a].at[chip], lsem.at[a]).start()
            half = _my_half(src[a], c)
            for k, (peer, _) in enumerate(_chip_peers(x, y, c)):
                pltpu.make_async_remote_copy(src_ref=src[a].at[half], dst_ref=land[a].at[chip, half],
                                             send_sem=ssem.at[3 * a + k], recv_sem=rsem.at[3 * a + k],
                                             device_id=peer, device_id_type=MESH).start()
        token[...] = jnp.zeros_like(token)

    lands = [lax.empty((N_CHIP,) + s.shape, s.dtype) for s in shards]
    out_shape = ([pltpu.SemaphoreType.DMA((3 * n,)), pltpu.SemaphoreType.DMA((3 * n,)), pltpu.SemaphoreType.DMA((n,))]
                 + [pltpu.HBM(s.shape, s.dtype) for s in shards] + [pltpu.HBM(l.shape, l.dtype) for l in lands]
                 + [jax.ShapeDtypeStruct((8, 128), F32)])
    res = pl.pallas_call(
        body, name=name, out_shape=out_shape, in_specs=[HBM] * (2 * n) + [ANY],
        out_specs=[SEM, SEM, SEM] + [HBM] * (2 * n) + [pl.BlockSpec(memory_space=pltpu.VMEM)],
        input_output_aliases={a: 3 + a for a in range(2 * n)},
        compiler_params=pltpu.CompilerParams(has_side_effects=EFFECT),
    )(*[_in_hbm(s) for s in shards], *[_in_hbm(l) for l in lands], after)
    return tuple(res[:3]), list(res[3:3 + n]), list(res[3 + n:3 + 2 * n]), res[-1]


def gather_wait(sems, srcs, lands, idx, after, name):
    m = len(idx)

    def body(*refs):
        src, land = refs[:m], refs[m:2 * m]
        ssem, rsem, lsem = refs[2 * m:2 * m + 3]
        x, y, c = _axes()
        chip = 2 * x + y
        for j, a in enumerate(idx):
            half = _my_half(src[j], c)
            for k, (peer, pchip) in enumerate(_chip_peers(x, y, c)):
                cp = pltpu.make_async_remote_copy(src_ref=src[j].at[half], dst_ref=land[j].at[pchip, half],
                                                  send_sem=ssem.at[3 * a + k], recv_sem=rsem.at[3 * a + k],
                                                  device_id=peer, device_id_type=MESH)
                cp.wait_send()
                cp.wait_recv()
            pltpu.make_async_copy(src[j], land[j].at[chip], lsem.at[a]).wait()

    s_in = [srcs[a] for a in idx]
    l_in = [lands[a] for a in idx]
    res = pl.pallas_call(
        body, name=name,
        out_shape=[pltpu.HBM(s.shape, s.dtype) for s in s_in] + [pltpu.HBM(l.shape, l.dtype) for l in l_in],
        in_specs=[HBM] * (2 * m) + [SEM, SEM, SEM, ANY], out_specs=[HBM] * (2 * m),
        input_output_aliases={a: a for a in range(2 * m)},
        compiler_params=pltpu.CompilerParams(has_side_effects=EFFECT),
    )(*s_in, *l_in, *sems, after)
    return list(res[m:])


def scatter_start(grads, lands, slot, after, name):
    n = len(grads)

    def body(*refs):
        src, land = refs[:n], refs[n:2 * n]
        ssem, rsem, lsem = refs[2 * n + 1:2 * n + 4]
        token = refs[-1]
        x, y, c = _axes()
        chip = 2 * x + y
        for a in range(n):
            pltpu.make_async_copy(src[a].at[chip], land[a].at[slot[a], chip], lsem.at[a]).start()
            for k, (peer, pchip) in enumerate(_chip_peers(x, y, c)):
                pltpu.make_async_remote_copy(src_ref=src[a].at[pchip], dst_ref=land[a].at[slot[a], chip],
                                             send_sem=ssem.at[3 * a + k], recv_sem=rsem.at[3 * a + k],
                                             device_id=peer, device_id_type=MESH).start()
        token[...] = jnp.zeros_like(token)

    out_shape = ([pltpu.SemaphoreType.DMA((3 * n,)), pltpu.SemaphoreType.DMA((3 * n,)), pltpu.SemaphoreType.DMA((n,))]
                 + [pltpu.HBM(g.shape, g.dtype) for g in grads] + [pltpu.HBM(l.shape, l.dtype) for l in lands]
                 + [jax.ShapeDtypeStruct((8, 128), F32)])
    res = pl.pallas_call(
        body, name=name, out_shape=out_shape, in_specs=[HBM] * (2 * n) + [ANY],
        out_specs=[SEM, SEM, SEM] + [HBM] * (2 * n) + [pl.BlockSpec(memory_space=pltpu.VMEM)],
        input_output_aliases={a: 3 + a for a in range(2 * n)},
        compiler_params=pltpu.CompilerParams(has_side_effects=EFFECT),
    )(*[_in_hbm(g) for g in grads], *[_in_hbm(l) for l in lands], after)
    return tuple(res[:3]), list(res[3:3 + n]), list(res[3 + n:3 + 2 * n]), res[-1]


def scatter_wait(sems, grads, lands, slot, after, name):
    n = len(grads)

    def body(*refs):
        src, land = refs[:n], refs[n:2 * n]
        ssem, rsem, lsem = refs[2 * n:2 * n + 3]
        x, y, c = _axes()
        chip = 2 * x + y
        for a in range(n):
            for k, (peer, pchip) in enumerate(_chip_peers(x, y, c)):
                cp = pltpu.make_async_remote_copy(src_ref=src[a].at[pchip], dst_ref=land[a].at[slot[a], pchip],
                                                  send_sem=ssem.at[3 * a + k], recv_sem=rsem.at[3 * a + k],
                                                  device_id=peer, device_id_type=MESH)
                cp.wait_send()
                cp.wait_recv()
            pltpu.make_async_copy(src[a].at[chip], land[a].at[slot[a], chip], lsem.at[a]).wait()

    res = pl.pallas_call(
        body, name=name,
        out_shape=[pltpu.HBM(g.shape, g.dtype) for g in grads] + [pltpu.HBM(l.shape, l.dtype) for l in lands],
        in_specs=[HBM] * (2 * n) + [SEM, SEM, SEM, ANY], out_specs=[HBM] * (2 * n),
        input_output_aliases={a: a for a in range(2 * n)},
        compiler_params=pltpu.CompilerParams(has_side_effects=EFFECT),
    )(*grads, *lands, *sems, after)
    return list(res[n:])


def reduce4(land, name):
    nl, _, R, C = land.shape
    TR = _adam_rows(R, C)

    def body(l_ref, o_ref):
        o_ref[...] = ((l_ref[0].astype(F32) + l_ref[1].astype(F32)) + l_ref[2].astype(F32)) + l_ref[3].astype(F32)

    return pl.pallas_call(
        body, name=name, grid=(nl, R // TR),
        in_specs=[pl.BlockSpec((None, N_CHIP, TR, C), lambda i, r: (i, 0, r, 0))],
        out_specs=pl.BlockSpec((None, TR, C), lambda i, r: (i, r, 0)),
        out_shape=jax.ShapeDtypeStruct((nl, R, C), F32), compiler_params=_cp("parallel", "parallel"))(land)


def swap_siblings(arrs, name):
    n = len(arrs)

    def body(*refs):
        src, dst = refs[:n], refs[n:2 * n]
        ssem, rsem = refs[2 * n:]
        x, y, c = _axes()
        cps = [pltpu.make_async_remote_copy(src_ref=src[a], dst_ref=dst[a], send_sem=ssem.at[a], recv_sem=rsem.at[a],
                                            device_id=(x, y, 1 - c), device_id_type=MESH) for a in range(n)]
        for cp in cps:
            cp.start()
        for cp in cps:
            cp.wait()

    return pl.pallas_call(
        body, name=name, out_shape=[jax.ShapeDtypeStruct(a.shape, a.dtype) for a in arrs],
        in_specs=[ANY] * n, out_specs=[ANY] * n,
        scratch_shapes=[pltpu.SemaphoreType.DMA((n,)), pltpu.SemaphoreType.DMA((n,))],
        compiler_params=pltpu.CompilerParams(vmem_limit_bytes=VMEM_LIMIT),
    )(*arrs)


def swap_start(arrs, name):
    n = len(arrs)

    def body(*refs):
        src, land = refs[:n], refs[n:2 * n]
        ssem, rsem = refs[2 * n:2 * n + 2]
        token = refs[-1]
        x, y, c = _axes()
        for a in range(n):
            pltpu.make_async_remote_copy(src_ref=src[a], dst_ref=land[a], send_sem=ssem.at[a], recv_sem=rsem.at[a],
                                         device_id=(x, y, 1 - c), device_id_type=MESH).start()
        token[...] = jnp.zeros_like(token)

    lands = [lax.empty(a.shape, a.dtype) for a in arrs]
    out_shape = ([pltpu.SemaphoreType.DMA((n,)), pltpu.SemaphoreType.DMA((n,))]
                 + [pltpu.HBM(a.shape, a.dtype) for a in arrs] * 2 + [jax.ShapeDtypeStruct((8, 128), F32)])
    res = pl.pallas_call(
        body, name=name, out_shape=out_shape, in_specs=[HBM] * (2 * n),
        out_specs=[SEM, SEM] + [HBM] * (2 * n) + [pl.BlockSpec(memory_space=pltpu.VMEM)],
        input_output_aliases={a: 2 + a for a in range(2 * n)},
        compiler_params=pltpu.CompilerParams(has_side_effects=EFFECT),
    )(*[_in_hbm(a) for a in arrs], *[_in_hbm(l) for l in lands])
    return tuple(res[:2]), list(res[2:2 + n]), list(res[2 + n:2 + 2 * n]), res[-1]


def swap_wait(sems, srcs, lands, after, name):
    n = len(srcs)

    def body(*refs):
        src, land = refs[:n], refs[n:2 * n]
        ssem, rsem = refs[2 * n:2 * n + 2]
        x, y, c = _axes()
        for a in range(n):
            cp = pltpu.make_async_remote_copy(src_ref=src[a], dst_ref=land[a], send_sem=ssem.at[a],
                                              recv_sem=rsem.at[a], device_id=(x, y, 1 - c), device_id_type=MESH)
            cp.wait_send()
            cp.wait_recv()

    res = pl.pallas_call(
        body, name=name, out_shape=[pltpu.HBM(a.shape, a.dtype) for a in srcs] * 2,
        in_specs=[HBM] * (2 * n) + [SEM, SEM, ANY], out_specs=[HBM] * (2 * n),
        input_output_aliases={a: a for a in range(2 * n)},
        compiler_params=pltpu.CompilerParams(has_side_effects=EFFECT),
    )(*srcs, *lands, *sems, after)
    return list(res[:n]), list(res[n:])


def _all_peers(x, y, c):
    out = []
    for k in range(1, N_DEV):
        px, py, pc = _flip(x, (k >> 2) & 1), _flip(y, (k >> 1) & 1), _flip(c, k & 1)
        out.append(((px, py, pc), 4 * px + 2 * py + pc))
    return out


def exchange_start(items, after, name):
    n = len(items)

    def body(*refs):
        src, land = refs[:n], refs[n:2 * n]
        ssem, rsem, lsem = refs[2 * n + 1:2 * n + 4]
        token = refs[-1]
        x, y, c = _axes()
        me = 4 * x + 2 * y + c
        for a, (_, scatter) in enumerate(items):
            pltpu.make_async_copy(src[a].at[me] if scatter else src[a], land[a].at[me], lsem.at[a]).start()
            for k, (peer, p) in enumerate(_all_peers(x, y, c)):
                pltpu.make_async_remote_copy(src_ref=src[a].at[p] if scatter else src[a], dst_ref=land[a].at[me],
                                             send_sem=ssem.at[7 * a + k], recv_sem=rsem.at[7 * a + k],
                                             device_id=peer, device_id_type=MESH).start()
        token[...] = jnp.zeros_like(token)

    srcs = [s for s, _ in items]
    lands = [lax.empty(s.shape if sc else (N_DEV,) + s.shape, s.dtype) for s, sc in items]
    out_shape = ([pltpu.SemaphoreType.DMA((7 * n,)), pltpu.SemaphoreType.DMA((7 * n,)), pltpu.SemaphoreType.DMA((n,))]
                 + [pltpu.HBM(s.shape, s.dtype) for s in srcs] + [pltpu.HBM(l.shape, l.dtype) for l in lands]
                 + [jax.ShapeDtypeStruct((8, 128), F32)])
    res = pl.pallas_call(
        body, name=name, out_shape=out_shape, in_specs=[HBM] * (2 * n) + [ANY],
        out_specs=[SEM, SEM, SEM] + [HBM] * (2 * n) + [pl.BlockSpec(memory_space=pltpu.VMEM)],
        input_output_aliases={a: 3 + a for a in range(2 * n)},
        compiler_params=pltpu.CompilerParams(has_side_effects=EFFECT),
    )(*[_in_hbm(s) for s in srcs], *[_in_hbm(l) for l in lands], after)
    return tuple(res[:3]), list(res[3:3 + n]), list(res[3 + n:3 + 2 * n]), res[-1]


def exchange_wait(sems, srcs, lands, scatter, after, name):
    n = len(srcs)

    def body(*refs):
        src, land = refs[:n], refs[n:2 * n]
        ssem, rsem, lsem = refs[2 * n:2 * n + 3]
        x, y, c = _axes()
        me = 4 * x + 2 * y + c
        for a in range(n):
            for k, (peer, p) in enumerate(_all_peers(x, y, c)):
                cp = pltpu.make_async_remote_copy(src_ref=src[a].at[p] if scatter[a] else src[a],
                                                  dst_ref=land[a].at[p], send_sem=ssem.at[7 * a + k],
                                                  recv_sem=rsem.at[7 * a + k], device_id=peer, device_id_type=MESH)
                cp.wait_send()
                cp.wait_recv()
            pltpu.make_async_copy(src[a].at[me] if scatter[a] else src[a], land[a].at[me], lsem.at[a]).wait()

    res = pl.pallas_call(
        body, name=name,
        out_shape=[pltpu.HBM(s.shape, s.dtype) for s in srcs] + [pltpu.HBM(l.shape, l.dtype) for l in lands],
        in_specs=[HBM] * (2 * n) + [SEM, SEM, SEM, ANY], out_specs=[HBM] * (2 * n),
        input_output_aliases={a: a for a in range(2 * n)},
        compiler_params=pltpu.CompilerParams(has_side_effects=EFFECT),
    )(*srcs, *lands, *sems, after)
    return list(res[n:])


def sum8(parts, name):
    _, P, C = parts.shape

    def body(p_ref, o_ref):
        tot = p_ref[0]
        for d in range(1, N_DEV):
            tot = tot + p_ref[d]
        o_ref[...] = tot

    return pl.pallas_call(body, name=name, out_shape=jax.ShapeDtypeStruct((P, C), F32),
                          compiler_params=pltpu.CompilerParams(vmem_limit_bytes=VMEM_LIMIT))(parts)


def mm_nn(a, w, out_dtype, name, res=None, gate=None):
    M, K = a.shape
    S, _, Ns = w.shape
    TM = _tile(M, (1024, 512, 256) if K <= 1024 else (512, 256))
    TN = _tile(Ns, (1408, 1024, 768, 512, 256, 128))
    nj = Ns // TN
    fused = res is not None

    def body(*refs):
        if fused:
            a_ref, w_ref, r_ref, g_ref, f_ref, o_ref = refs
        else:
            a_ref, w_ref, f_ref = refs
        f = jnp.dot(a_ref[...], w_ref[...], preferred_element_type=F32)
        f_ref[...] = f.astype(f_ref.dtype)
        if fused:
            o_ref[...] = r_ref[...] + g_ref[...] * f

    col = lambda s, j, i: (i, s * nj + j)
    in_specs = [pl.BlockSpec((TM, K), lambda s, j, i: (i, 0)), pl.BlockSpec((None, K, TN), lambda s, j, i: (s, 0, j))]
    out_specs = [pl.BlockSpec((TM, TN), col)]
    out_shape = [jax.ShapeDtypeStruct((M, S * Ns), out_dtype)]
    args = [a, w]
    if fused:
        in_specs += [pl.BlockSpec((TM, TN), col), pl.BlockSpec((1, TN), lambda s, j, i: (0, s * nj + j))]
        out_specs.append(pl.BlockSpec((TM, TN), col))
        out_shape.append(jax.ShapeDtypeStruct((M, S * Ns), F32))
        args += [res, gate]
    out = pl.pallas_call(body, name=name, grid=(S, nj, M // TM), in_specs=in_specs, out_specs=out_specs,
                         out_shape=out_shape, compiler_params=_cp("parallel", "parallel", "parallel"))(*args)
    return tuple(out) if fused else out[0]


def mm_nt(g, w, out_dtype, name):
    g3 = g if g.ndim == 3 else g[None]
    Q, M, F = g3.shape
    S, K, Ns = w.shape
    TM = _tile(M, (2048, 1024, 512, 256) if K <= 1024 else (512, 256))
    TN = _tile(Ns, (1408, 1024, 768, 512, 256, 128))
    nj = Ns // TN
    nred = S * nj
    per_part = F // TN

    def body(g_ref, w_ref, o_ref, acc):
        n = pl.program_id(1)

        @pl.when(n == 0)
        def _():
            acc[...] = jnp.zeros_like(acc)

        acc[...] += lax.dot_general(g_ref[...], w_ref[...], (((1,), (1,)), ((), ())), preferred_element_type=F32)

        @pl.when(n == nred - 1)
        def _():
            o_ref[...] = acc[...].astype(o_ref.dtype)

    return pl.pallas_call(
        body, name=name, grid=(M // TM, nred),
        in_specs=[pl.BlockSpec((None, TM, TN), lambda i, n: (n // per_part, i, n % per_part)),
                  pl.BlockSpec((None, K, TN), lambda i, n: (n // nj, 0, n % nj))],
        out_specs=pl.BlockSpec((TM, K), lambda i, n: (i, 0)),
        out_shape=jax.ShapeDtypeStruct((M, K), out_dtype),
        scratch_shapes=[pltpu.VMEM((TM, K), F32)],
        compiler_params=_cp("parallel", "arbitrary"))(g3, w)


def mm_tn(a, g, S, name):
    M, K = a.shape
    g3 = g if g.ndim == 3 else g[None]
    Q, _, F = g3.shape
    Ns = Q * F // S
    TN = _tile(Ns, (1408, 1024, 768, 512, 256, 128))
    TK = next(t for t in (1024, 512, 256, 128) if t <= K and K % t == 0
              and 4 * M * (t + TN) + 8 * t * TN <= VMEM_LIMIT * 3 // 4)
    nj = Ns // TN
    per_part = F // TN

    def body(a_ref, g_ref, o_ref):
        o_ref[...] = lax.dot_general(a_ref[...], g_ref[...], (((0,), (0,)), ((), ())),
                                     preferred_element_type=F32).astype(o_ref.dtype)

    return pl.pallas_call(
        body, name=name, grid=(S * nj, K // TK),
        in_specs=[pl.BlockSpec((M, TK), lambda n, k: (0, k)),
                  pl.BlockSpec((None, M, TN), lambda n, k: (n // per_part, 0, n % per_part))],
        out_specs=pl.BlockSpec((None, TK, TN), lambda n, k: (n // nj, k, n % nj)),
        out_shape=jax.ShapeDtypeStruct((S, K, Ns), BF),
        compiler_params=_cp("parallel", "parallel"))(a, g3)


ROW_TILE = (512, 256)


def _rows(TL, D):
    return pl.BlockSpec((TL, D), lambda i: (i, 0))


def _fixed(R, D):
    return pl.BlockSpec((R, D), lambda i: (0, 0))


def _rowsum8(v):
    T, D = v.shape
    return jnp.sum(v.reshape(T // 8, 8, D), axis=0)


def _norm_parts(xv):
    r = lax.rsqrt(jnp.mean(xv * xv, axis=-1, keepdims=True) + RMS_EPS)
    return xv * r, r


def norm_mod(x, gamma, mods, k_shift, out_dtype, name):
    L, D = x.shape
    TL = _tile(L, (256,))
    n = L // TL
    NIN, NOUT = min(3, n), 2

    def body(x_hbm, g_ref, m_ref, o_hbm, xbuf, obuf, isem, osem):
        def fetch(i, slot):
            return pltpu.make_async_copy(x_hbm.at[pl.ds(pl.multiple_of(i * TL, TL), TL)], xbuf.at[slot], isem.at[slot])

        def put(i, slot):
            return pltpu.make_async_copy(obuf.at[slot], o_hbm.at[pl.ds(pl.multiple_of(i * TL, TL), TL)], osem.at[slot])

        for s in range(NIN):
            fetch(s, s).start()
        gam = g_ref[...]
        sh, sc = m_ref[k_shift:k_shift + 1, :], m_ref[k_shift + 1:k_shift + 2, :]

        def step(i, carry):
            slot, oslot = lax.rem(i, NIN), lax.rem(i, NOUT)
            fetch(i, slot).wait()

            @pl.when(i >= NOUT)
            def _():
                put(i - NOUT, oslot).wait()

            xn, _ = _norm_parts(xbuf[slot])
            obuf[oslot] = ((xn * gam) * (1.0 + sc) + sh).astype(obuf.dtype)
            put(i, oslot).start()

            @pl.when(i + NIN < n)
            def _():
                fetch(i + NIN, slot).start()

            return carry

        lax.fori_loop(0, n, step, 0)
        for i in range(max(n - NOUT, 0), n):
            put(i, i % NOUT).wait()

    whole = pl.BlockSpec(memory_space=pltpu.VMEM)
    return pl.pallas_call(
        body, name=name, in_specs=[ANY, whole, whole], out_specs=ANY,
        out_shape=jax.ShapeDtypeStruct((L, D), out_dtype),
        scratch_shapes=[pltpu.VMEM((NIN, TL, D), F32), pltpu.VMEM((NOUT, TL, D), out_dtype),
                        pltpu.SemaphoreType.DMA((NIN,)), pltpu.SemaphoreType.DMA((NOUT,))],
        compiler_params=pltpu.CompilerParams(vmem_limit_bytes=VMEM_LIMIT))(x, gamma, mods)


def norm_bwd(dh, x, dres, gamma, mods, k_shift, name, branch=None):
    L, D = x.shape
    TL = _tile(L, ROW_TILE)
    nacc = 4 if branch else 3

    def body(*refs):
        if branch:
            dh_ref, x_ref, dr_ref, g_ref, m_ref, f_ref, fm_ref, dx_ref, s_ref, df_ref, acc = refs
        else:
            dh_ref, x_ref, dr_ref, g_ref, m_ref, dx_ref, s_ref, acc = refs
        i = pl.program_id(0)

        @pl.when(i == 0)
        def _():
            acc[...] = jnp.zeros_like(acc)

        xn, r = _norm_parts(x_ref[...])
        dh_v = dh_ref[...].astype(F32)
        gam = g_ref[...]
        sc = m_ref[k_shift + 1:k_shift + 2, :]
        dn = dh_v * (1.0 + sc)
        dxn = dn * gam
        dx = dr_ref[...] + r * (dxn - xn * jnp.mean(dxn * xn, axis=-1, keepdims=True))
        dx_ref[...] = dx
        acc[0] += _rowsum8(dh_v)
        acc[1] += _rowsum8(dh_v * (xn * gam))
        acc[2] += _rowsum8(dn * xn)
        if branch:
            df_ref[...] = (dx * fm_ref[branch[2]:branch[2] + 1, :]).astype(df_ref.dtype)
            acc[3] += _rowsum8(dx * f_ref[...].astype(F32))

        @pl.when(i == pl.num_programs(0) - 1)
        def _():
            s_ref[...] = jnp.zeros_like(s_ref)
            for q in range(nacc):
                s_ref[q:q + 1, :] = jnp.sum(acc[q], axis=0, keepdims=True)

    in_specs = [_rows(TL, D), _rows(TL, D), _rows(TL, D), _fixed(1, D), _fixed(6, D)]
    out_specs = [_rows(TL, D), _fixed(8, D)]
    out_shape = [jax.ShapeDtypeStruct((L, D), F32), jax.ShapeDtypeStruct((8, D), F32)]
    args = [dh, x, dres, gamma, mods]
    if branch:
        in_specs += [_rows(TL, D), _fixed(6, D)]
        out_specs.append(_rows(TL, D))
        out_shape.append(jax.ShapeDtypeStruct((L, D), BF))
        args += [branch[0], branch[1]]
    return pl.pallas_call(
        body, name=name, grid=(L // TL,), in_specs=in_specs, out_specs=out_specs, out_shape=out_shape,
        scratch_shapes=[pltpu.VMEM((nacc, 8, D), F32)], compiler_params=_cp("arbitrary"))(*args)


def ffn_in_act(a, w, name):
    M, K = a.shape
    S, _, Ns = w.shape
    half = S // 2
    TM = _tile(M, (1024, 512, 256))
    TN = _tile(Ns, (1408, 1024, 768, 512, 256, 128))
    nj = Ns // TN

    def body(a_ref, wg_ref, wu_ref, gu_ref, act_ref):
        av = a_ref[...]
        g = jnp.dot(av, wg_ref[...], preferred_element_type=F32)
        u = jnp.dot(av, wu_ref[...], preferred_element_type=F32)
        gu_ref[0] = g.astype(gu_ref.dtype)
        gu_ref[1] = u.astype(gu_ref.dtype)
        act_ref[...] = (g * _sigmoid(g) * u).astype(act_ref.dtype)

    return pl.pallas_call(
        body, name=name, grid=(half, nj, M // TM),
        in_specs=[pl.BlockSpec((TM, K), lambda s, j, i: (i, 0)),
                  pl.BlockSpec((None, K, TN), lambda s, j, i: (s, 0, j)),
                  pl.BlockSpec((None, K, TN), lambda s, j, i: (s + half, 0, j))],
        out_specs=[pl.BlockSpec((2, TM, TN), lambda s, j, i: (0, i, s * nj + j)),
                   pl.BlockSpec((TM, TN), lambda s, j, i: (i, s * nj + j))],
        out_shape=[jax.ShapeDtypeStruct((2, M, half * Ns), BF), jax.ShapeDtypeStruct((M, half * Ns), BF)],
        compiler_params=_cp("parallel", "parallel", "parallel"))(a, w, w)


def ffn_out_bwd(dff, w2, gu, name):
    M, D = dff.shape
    F = w2.shape[0]
    TM = _tile(M, (512, 256))
    CW = _tile(F, (256, 128))

    def body(d_ref, w_ref, gu_ref, o_ref):
        dv = d_ref[...]

        def product(c):
            return lax.dot_general(dv, w_ref[c:c + CW, :], (((1,), (1,)), ((), ())), preferred_element_type=F32)

        da = product(0)
        for c in range(0, F, CW):
            ahead = product(c + CW) if c + CW < F else None
            g = gu_ref[0, :, c:c + CW].astype(F32)
            u = gu_ref[1, :, c:c + CW].astype(F32)
            s = _sigmoid(g)
            o_ref[0, :, c:c + CW] = (da * u * (s + g * s * (1.0 - s))).astype(o_ref.dtype)
            o_ref[1, :, c:c + CW] = (da * g * s).astype(o_ref.dtype)
            da = ahead

    part = pl.BlockSpec((2, TM, F), lambda i: (0, i, 0))
    return pl.pallas_call(
        body, name=name, grid=(M // TM,),
        in_specs=[pl.BlockSpec((TM, D), lambda i: (i, 0)), pl.BlockSpec((F, D), lambda i: (0, 0)), part],
        out_specs=part, out_shape=jax.ShapeDtypeStruct((2, M, F), BF),
        compiler_params=_cp("parallel"))(dff, w2, gu)


def ssm_out_glu(z, w, x, mods, k_gate, name):
    M, K = z.shape
    S, _, Ns = w.shape
    half = S // 2
    TM = _tile(M, (1024, 512, 256))
    TN = _tile(Ns, (512, 256, 128))
    nj = Ns // TN

    def body(z_ref, wv_ref, wg_ref, x_ref, m_ref, o_ref, mix_ref, y_ref):
        zv = z_ref[...]
        CW = _tile(TN, (256, 128))

        def products(c):
            return (jnp.dot(zv, wv_ref[:, c:c + CW], preferred_element_type=F32),
                    jnp.dot(zv, wg_ref[:, c:c + CW], preferred_element_type=F32))

        cur = products(0)
        for c in range(0, TN, CW):
            ahead = products(c + CW) if c + CW < TN else None
            val, gate = cur
            o_ref[0, :, c:c + CW] = val.astype(o_ref.dtype)
            o_ref[1, :, c:c + CW] = gate.astype(o_ref.dtype)
            mix = val * _sigmoid(gate)
            mix_ref[:, c:c + CW] = mix.astype(mix_ref.dtype)
            y_ref[:, c:c + CW] = x_ref[:, c:c + CW] + m_ref[k_gate:k_gate + 1, c:c + CW] * mix
            cur = ahead

    col = lambda s, j, i: (i, s * nj + j)
    return pl.pallas_call(
        body, name=name, grid=(half, nj, M // TM),
        in_specs=[pl.BlockSpec((TM, K), lambda s, j, i: (i, 0)),
                  pl.BlockSpec((None, K, TN), lambda s, j, i: (s, 0, j)),
                  pl.BlockSpec((None, K, TN), lambda s, j, i: (s + half, 0, j)),
                  pl.BlockSpec((TM, TN), col), pl.BlockSpec((6, TN), lambda s, j, i: (0, s * nj + j))],
        out_specs=[pl.BlockSpec((2, TM, TN), lambda s, j, i: (0, i, s * nj + j)), pl.BlockSpec((TM, TN), col),
                   pl.BlockSpec((TM, TN), col)],
        out_shape=[jax.ShapeDtypeStruct((2, M, half * Ns), BF), jax.ShapeDtypeStruct((M, half * Ns), BF),
                   jax.ShapeDtypeStruct((M, half * Ns), F32)],
        compiler_params=_cp("parallel", "parallel", "parallel"))(z, w, w, x, mods)


def glu_bwd(dmix, o, name):
    _, L, D = o.shape
    TL = _tile(L, ROW_TILE)

    def body(d_ref, o_ref, do_ref):
        d = d_ref[...].astype(F32)
        val = o_ref[0].astype(F32)
        s = _sigmoid(o_ref[1].astype(F32))
        do_ref[0] = (d * s).astype(do_ref.dtype)
        do_ref[1] = (d * val * s * (1.0 - s)).astype(do_ref.dtype)

    part = pl.BlockSpec((2, TL, D), lambda i: (0, i, 0))
    return pl.pallas_call(body, name=name, grid=(L // TL,), in_specs=[_rows(TL, D), part],
                          out_specs=part, out_shape=jax.ShapeDtypeStruct((2, L, D), BF),
                          compiler_params=_cp("parallel"))(dmix, o)


def final_loss(x, target, gamma, f, fmods, k_gate, name):
    L, D = x.shape
    TL = _tile(L, ROW_TILE)

    def body(x_ref, t_ref, g_ref, f_ref, fm_ref, l_ref, dx_ref, s_ref, df_ref, acc, lacc):
        i = pl.program_id(0)

        @pl.when(i == 0)
        def _():
            acc[...] = jnp.zeros_like(acc)
            lacc[...] = jnp.zeros_like(lacc)

        xn, r = _norm_parts(x_ref[...])
        gam = g_ref[...]
        e = xn * gam - t_ref[...]
        lacc[...] += jnp.sum(0.5 * jnp.mean(e * e, axis=-1, keepdims=True), axis=0, keepdims=True)
        dy = e * (1.0 / D)
        dxn = dy * gam
        dx = r * (dxn - xn * jnp.mean(dxn * xn, axis=-1, keepdims=True))
        dx_ref[...] = dx
        df_ref[...] = (dx * fm_ref[k_gate:k_gate + 1, :]).astype(df_ref.dtype)
        acc[0] += _rowsum8(dy * xn)
        acc[1] += _rowsum8(dx * f_ref[...].astype(F32))

        @pl.when(i == pl.num_programs(0) - 1)
        def _():
            s_ref[...] = jnp.zeros_like(s_ref)
            for q in range(2):
                s_ref[q:q + 1, :] = jnp.sum(acc[q], axis=0, keepdims=True)
            l_ref[...] = jnp.broadcast_to(lacc[...], l_ref.shape)

    return pl.pallas_call(
        body, name=name, grid=(L // TL,),
        in_specs=[_rows(TL, D), _rows(TL, D), _fixed(1, D), _rows(TL, D), _fixed(6, D)],
        out_specs=[_fixed(8, 128), _rows(TL, D), _fixed(8, D), _rows(TL, D)],
        out_shape=[jax.ShapeDtypeStruct((8, 128), F32), jax.ShapeDtypeStruct((L, D), F32),
                   jax.ShapeDtypeStruct((8, D), F32), jax.ShapeDtypeStruct((L, D), BF)],
        scratch_shapes=[pltpu.VMEM((2, 8, D), F32), pltpu.VMEM((1, 1), F32)],
        compiler_params=_cp("arbitrary"))(x, target, gamma, f, fmods)


def _col(L, TC, off):
    return pl.BlockSpec((L, TC), lambda j: (0, off + j))


def _shift_down(v, k, row):
    return jnp.where(row >= k, pltpu.roll(v, k, 0), 0.0)


def _shift_up(v, k, row, L):
    return jnp.where(row < L - k, pltpu.roll(v, L - k, 0), 0.0)


def conv_fwd(p, w, name):
    L, D3 = p.shape
    D = D3 // 3
    TC = _tile(D, (128,))
    nc = D // TC

    def body(b_ref, c_ref, v_ref, w_ref, o_ref):
        row = lax.broadcasted_iota(jnp.int32, (L, TC), 0)
        cv = c_ref[...].astype(F32) * v_ref[...].astype(F32)
        conv = w_ref[2:3, :] * cv + w_ref[1:2, :] * _shift_down(cv, 1, row) + w_ref[0:1, :] * _shift_down(cv, 2, row)
        o_ref[...] = (b_ref[...].astype(F32) * conv).astype(o_ref.dtype)

    return pl.pallas_call(
        body, name=name, grid=(nc,),
        in_specs=[_col(L, TC, 0), _col(L, TC, nc), _col(L, TC, 2 * nc), pl.BlockSpec((3, TC), lambda j: (0, j))],
        out_specs=_col(L, TC, 0), out_shape=jax.ShapeDtypeStruct((L, D), BF), compiler_params=_cp("parallel"))(p, p, p, w)


def conv_bwd(dm, p, w, name):
    L, D3 = p.shape
    D = D3 // 3
    TC = _tile(D, (128,))
    nc = D // TC

    def body(dm_ref, b_ref, c_ref, v_ref, w_ref, db_ref, dc_ref, dv_ref, dw_ref):
        row = lax.broadcasted_iota(jnp.int32, (L, TC), 0)
        cg, vv = c_ref[...].astype(F32), v_ref[...].astype(F32)
        cv = cg * vv
        cv1, cv2 = _shift_down(cv, 1, row), _shift_down(cv, 2, row)
        conv = w_ref[2:3, :] * cv + w_ref[1:2, :] * cv1 + w_ref[0:1, :] * cv2
        dmv = dm_ref[...].astype(F32)
        db_ref[...] = (dmv * conv).astype(db_ref.dtype)
        dconv = dmv * b_ref[...].astype(F32)
        dcv = (w_ref[2:3, :] * dconv + w_ref[1:2, :] * _shift_up(dconv, 1, row, L)
               + w_ref[0:1, :] * _shift_up(dconv, 2, row, L))
        dc_ref[...] = (dcv * vv).astype(dc_ref.dtype)
        dv_ref[...] = (dcv * cg).astype(dv_ref.dtype)
        dw_ref[...] = jnp.zeros_like(dw_ref)
        dw_ref[0:1, :] = jnp.sum(dconv * cv2, axis=0, keepdims=True)
        dw_ref[1:2, :] = jnp.sum(dconv * cv1, axis=0, keepdims=True)
        dw_ref[2:3, :] = jnp.sum(dconv * cv, axis=0, keepdims=True)

    one = jax.ShapeDtypeStruct((L, D), BF)
    return pl.pallas_call(
        body, name=name, grid=(nc,),
        in_specs=[_col(L, TC, 0), _col(L, TC, 0), _col(L, TC, nc), _col(L, TC, 2 * nc),
                  pl.BlockSpec((3, TC), lambda j: (0, j))],
        out_specs=[_col(L, TC, 0), _col(L, TC, 0), _col(L, TC, 0), pl.BlockSpec((8, TC), lambda j: (0, j))],
        out_shape=[one, one, one, jax.ShapeDtypeStruct((8, D), F32)],
        compiler_params=_cp("parallel"))(dm, p, p, p, w)


def _gelu(y):
    return 0.5 * y * (1.0 + jnp.tanh(GELU_C * (y + GELU_A * y * y * y)))


def _gelu_grad(y):
    th = jnp.tanh(GELU_C * (y + GELU_A * y * y * y))
    return 0.5 * (1.0 + th) + 0.5 * y * (1.0 - th * th) * GELU_C * (1.0 + 3.0 * GELU_A * y * y)


def _cmul_add(br, bi, ar, ai, sr, si):
    return br + ar * sr - ai * si, bi + ar * si + ai * sr


def _log2(n):
    k = n.bit_length() - 1
    assert 1 << k == n
    return k


def _replicate(P2, W2, P, GLP, transposed):
    shape = (W2, P2) if transposed else (P2, W2)
    k = lax.broadcasted_iota(jnp.int32, shape, 1 if transposed else 0)
    c = lax.broadcasted_iota(jnp.int32, shape, 0 if transposed else 1)
    return ((k >> _log2(P)) == (c >> _log2(GLP))) & ((k & (P - 1)) == (c & (P - 1)))


def _on_diagonal(KB, W2, H, P, GLP, transposed):
    shape = (W2, KB) if transposed else (KB, W2)
    r = lax.broadcasted_iota(jnp.int32, shape, 1 if transposed else 0)
    c = lax.broadcasted_iota(jnp.int32, shape, 0 if transposed else 1)
    return (r >> _log2(H)) == ((c & (GLP - 1)) >> _log2(P))


def _expand(t, dims, transposed):
    KB, W2, H, P, GLP = dims
    rep = _replicate(2 * P, W2, P, GLP, transposed).astype(t.dtype)
    wide = jnp.dot(rep, t, preferred_element_type=F32) if transposed else jnp.dot(t, rep, preferred_element_type=F32)
    return jnp.where(_on_diagonal(KB, W2, H, P, GLP, transposed), wide, 0.0).astype(t.dtype)


def _extract(acc, dims):
    KB, W2, H, P, GLP = dims
    rep = _replicate(2 * P, W2, P, GLP, True).astype(BF)
    kept = jnp.where(_on_diagonal(KB, W2, H, P, GLP, False), acc, 0.0)
    hi = kept.astype(BF)
    lo = (kept - hi.astype(F32)).astype(BF)
    return jnp.dot(hi, rep, preferred_element_type=F32) + jnp.dot(lo, rep, preferred_element_type=F32)


def _cmul(ar, ai, sr, si):
    return ar * sr - ai * si, ar * si + ai * sr


def _chunk_order(TL, CH, transposed):
    out_row = lax.broadcasted_iota(jnp.int32, (TL, TL), 1 if transposed else 0)
    in_row = lax.broadcasted_iota(jnp.int32, (TL, TL), 0 if transposed else 1)
    return in_row == ((out_row & 7) << _log2(CH)) + (out_row >> 3)


def _reorder(perm, v):
    hi = v.astype(perm.dtype)
    lo = (v - hi.astype(F32)).astype(perm.dtype)
    return jnp.dot(perm, hi, preferred_element_type=F32) + jnp.dot(perm, lo, preferred_element_type=F32)


def _interleave(main, side):
    n, m, k = len(main), len(side), 0
    for i, step in enumerate(main):
        step()
        while k < m and (k + 1) * n <= (i + 1) * m:
            side[k]()
            k += 1
    for step in side[k:]:
        step()


S5_CHUNK = 512


def s5_fwd(h, tb, tct, pw, dvec, name):
    L, D = h.shape
    nkb, KB, P2 = tb.shape
    P = P2 // 2
    W = (KB // SSM_GROUP) * P
    W2 = 2 * W
    dims = (KB, W2, SSM_GROUP, P, W)
    TL = _tile(L, (512, 256))
    CH = TL // 8
    NB = 2 if nkb % 2 == 0 else 1
    CK = min(S5_CHUNK, W2)

    def body(h_ref, tb_ref, tct_ref, pw_ref, d_ref, s_ref, y_ref, z_ref, bw, cw, perm, unperm, carry):
        t = pl.program_id(1)

        @pl.when(t == 0)
        def _():
            carry[...] = jnp.zeros_like(carry)
            for b in range(NB):
                bw[b] = _expand(tb_ref[b], dims, False)
                cw[b] = _expand(tct_ref[b], dims, True)
            perm[...] = _chunk_order(TL, CH, False).astype(perm.dtype)
            unperm[...] = _chunk_order(TL, CH, True).astype(perm.dtype)

        hp = _reorder(perm[...], h_ref[...])
        hpb = hp.astype(BF)
        first = lax.broadcasted_iota(jnp.int32, (8, W), 0) == 0

        def project(b):
            def chunk(c):
                def emit():
                    s_ref[:, b * W2 + c:b * W2 + c + CK] = jnp.dot(hpb[:, b * KB:(b + 1) * KB], bw[b, :, c:c + CK],
                                                                   preferred_element_type=F32)
                return emit
            return [chunk(c) for c in range(0, W2, CK)]

        def scan(b):
            re, im = slice(b * W2, b * W2 + W), slice(b * W2 + W, (b + 1) * W2)
            ar, ai = pw_ref[b, 0:8, :W], pw_ref[b, 0:8, W:]
            st = {"x": (jnp.zeros((8, W), F32), jnp.zeros((8, W), F32))}

            def own(j):
                def emit():
                    rows = slice(j * 8, j * 8 + 8)
                    xr, xi = _cmul_add(s_ref[rows, re], s_ref[rows, im], ar, ai, *st["x"])
                    s_ref[rows, re] = xr
                    s_ref[rows, im] = xi
                    st["x"] = (xr, xi)
                return emit

            def ends():
                xr, xi = st["x"]
                for k, off in ((1, 8), (2, 16), (4, 24)):
                    xr, xi = _cmul_add(xr, xi, pw_ref[b, off:off + 8, :W], pw_ref[b, off:off + 8, W:],
                                       pltpu.roll(xr, k, 0), pltpu.roll(xi, k, 0))
                xr, xi = _cmul_add(xr, xi, pw_ref[b, 32:40, :W], pw_ref[b, 32:40, W:], carry[b, 0], carry[b, 1])
                st["c"] = (jnp.where(first, carry[b, 0], pltpu.roll(xr, 1, 0)),
                           jnp.where(first, carry[b, 1], pltpu.roll(xi, 1, 0)))
                carry[b, 0] = jnp.broadcast_to(xr[7:8], (8, W))
                carry[b, 1] = jnp.broadcast_to(xi[7:8], (8, W))

            def carried(j):
                def emit():
                    rows = slice(j * 8, j * 8 + 8)
                    cr, ci = _cmul(ar, ai, *st["c"])
                    s_ref[rows, re] = s_ref[rows, re] + cr
                    s_ref[rows, im] = s_ref[rows, im] + ci
                    st["c"] = (cr, ci)
                return emit

            return [own(j) for j in range(CH)] + [ends] + [carried(j) for j in range(CH)]

        def readout(b):
            cols = slice(b * KB, (b + 1) * KB)
            acc = {}

            def chunk(c):
                def emit():
                    part = jnp.dot(s_ref[:, b * W2 + c:b * W2 + c + CK].astype(BF), cw[b, c:c + CK, :],
                                   preferred_element_type=F32)
                    acc["y"] = part if c == 0 else acc["y"] + part
                return emit

            def finish():
                y = acc["y"] + d_ref[:, cols] * hp[:, cols]
                y_ref[:, cols] = y
                z_ref[:, cols] = jnp.dot(unperm[...], _gelu(y).astype(BF),
                                         preferred_element_type=F32).astype(z_ref.dtype)

            return [chunk(c) for c in range(0, W2, CK)] + [finish]

        for emit in project(0):
            emit()
        for b in range(NB):
            side = (project(b + 1) if b + 1 < NB else []) + (readout(b - 1) if b > 0 else [])
            _interleave(scan(b), side)
        for emit in readout(NB - 1):
            emit()

    blk = lambda kb, t: (t, kb)
    per_kb = lambda kb, t: (kb, 0, 0)
    return pl.pallas_call(
        body, name=name, grid=(nkb // NB, L // TL),
        in_specs=[pl.BlockSpec((TL, NB * KB), blk), pl.BlockSpec((NB, KB, P2), per_kb),
                  pl.BlockSpec((NB, P2, KB), per_kb), pl.BlockSpec((NB, 40, W2), per_kb),
                  pl.BlockSpec((1, NB * KB), lambda kb, t: (0, kb))],
        out_specs=[pl.BlockSpec((TL, NB * W2), blk), pl.BlockSpec((TL, NB * KB), blk),
                   pl.BlockSpec((TL, NB * KB), blk)],
        out_shape=[jax.ShapeDtypeStruct((L, nkb * W2), F32), jax.ShapeDtypeStruct((L, D), F32),
                   jax.ShapeDtypeStruct((L, D), BF)],
        scratch_shapes=[pltpu.VMEM((NB, KB, W2), BF), pltpu.VMEM((NB, W2, KB), BF), pltpu.VMEM((TL, TL), BF),
                        pltpu.VMEM((TL, TL), BF), pltpu.VMEM((NB, 2, 8, W), F32)],
        compiler_params=_cp("parallel", "arbitrary"))(h, tb, tct, pw, dvec)


def s5_bwd(dz, y, h, s, tc, tbt, pwr, dvec, name):
    L, D = h.shape
    nkb, KB, P2 = tc.shape
    P = P2 // 2
    W = (KB // SSM_GROUP) * P
    W2 = 2 * W
    dims = (KB, W2, SSM_GROUP, P, W)
    TL = _tile(L, (512, 256))
    CH = TL // 8
    nt = L // TL
    NB = 2 if nkb % 2 == 0 else 1
    CK = min(S5_CHUNK, W2)
    tn = (((0,), (0,)), ((), ()))

    def body(dz_ref, y_ref, h_ref, s_ref, sp_ref, tc_ref, tbt_ref, pw_ref, d_ref,
             dh_ref, dd_ref, da_ref, db_ref, dc_ref, g, ctw, btw, dbacc, dcacc, dys, perm, unperm, carry):
        t = pl.program_id(1)

        @pl.when(t == 0)
        def _():
            carry[...] = jnp.zeros_like(carry)
            dd_ref[...] = jnp.zeros_like(dd_ref)
            da_ref[...] = jnp.zeros_like(da_ref)
            dbacc[...] = jnp.zeros_like(dbacc)
            dcacc[...] = jnp.zeros_like(dcacc)
            for b in range(NB):
                ctw[b] = _expand(tc_ref[b], dims, False)
                btw[b] = _expand(tbt_ref[b], dims, True)
            perm[...] = _chunk_order(TL, CH, False).astype(perm.dtype)
            unperm[...] = _chunk_order(TL, CH, True).astype(perm.dtype)

        hp = jnp.dot(perm[...], h_ref[...].astype(BF), preferred_element_type=F32)
        dy = jnp.dot(perm[...], dz_ref[...].astype(BF), preferred_element_type=F32) * _gelu_grad(y_ref[...])
        dd_ref[...] += _rowsum8(dy * hp)
        dys[...] = dy
        dyb = dy.astype(BF)
        hpb = hp.astype(BF)
        sub = lax.broadcasted_iota(jnp.int32, (8, W), 0)
        live = jnp.where(t == nt - 1, 0.0, 1.0)

        def lead(b):
            cols = slice(b * KB, (b + 1) * KB)

            def to_states(c):
                def emit():
                    g[b, :, c:c + CK] = jnp.dot(dyb[:, cols], ctw[b, :, c:c + CK], preferred_element_type=F32)
                return emit

            def d_c(c):
                def emit():
                    dcacc[b, :, c:c + CK] += lax.dot_general(dyb[:, cols],
                                                             s_ref[:, b * W2 + c:b * W2 + c + CK].astype(BF), tn,
                                                             preferred_element_type=F32)
                return emit

            return [f(c) for c in range(0, W2, CK) for f in (to_states, d_c)]

        def scan(b):
            re, im = slice(b * W2, b * W2 + W), slice(b * W2 + W, (b + 1) * W2)
            ar, ai = pw_ref[b, 0:8, :W], pw_ref[b, 0:8, W:]
            zero = jnp.zeros((8, W), F32)
            st = {"g": (zero, zero), "acc": (zero, zero)}

            def own(j):
                def emit():
                    rows = slice(j * 8, j * 8 + 8)
                    gr, gi = _cmul_add(g[b, rows, :W], g[b, rows, W:], ar, ai, *st["g"])
                    g[b, rows, :W] = gr
                    g[b, rows, W:] = gi
                    st["g"] = (gr, gi)
                return emit

            def ends():
                gr, gi = st["g"]
                for k, off in ((1, 8), (2, 16), (4, 24)):
                    gr, gi = _cmul_add(gr, gi, pw_ref[b, off:off + 8, :W], pw_ref[b, off:off + 8, W:],
                                       pltpu.roll(gr, 8 - k, 0), pltpu.roll(gi, 8 - k, 0))
                gr, gi = _cmul_add(gr, gi, pw_ref[b, 32:40, :W], pw_ref[b, 32:40, W:], carry[b, 0], carry[b, 1])
                st["c"] = (jnp.where(sub == 7, carry[b, 0], pltpu.roll(gr, 7, 0)),
                           jnp.where(sub == 7, carry[b, 1], pltpu.roll(gi, 7, 0)))
                carry[b, 0] = jnp.broadcast_to(gr[0:1], (8, W))
                carry[b, 1] = jnp.broadcast_to(gi[0:1], (8, W))

            def carried(j):
                def emit():
                    rows = slice(j * 8, j * 8 + 8)
                    cr, ci = _cmul(ar, ai, *st["c"])
                    gr, gi = g[b, rows, :W] + cr, g[b, rows, W:] + ci
                    g[b, rows, :W] = gr
                    g[b, rows, W:] = gi
                    if j > 0:
                        before = slice(j * 8 - 8, j * 8)
                        pr, pi = s_ref[before, re], s_ref[before, im]
                    else:
                        last = slice(TL - 8, TL)
                        pr = jnp.where(sub == 0, sp_ref[7:8, re] * live, pltpu.roll(s_ref[last, re], 1, 0))
                        pi = jnp.where(sub == 0, sp_ref[7:8, im] * live, pltpu.roll(s_ref[last, im], 1, 0))
                    accr, acci = st["acc"]
                    st["c"] = (cr, ci)
                    st["acc"] = (accr + pr * gr + pi * gi, acci + pr * gi - pi * gr)
                return emit

            def done():
                da_ref[b, :, :W] += st["acc"][0]
                da_ref[b, :, W:] += st["acc"][1]

            return ([own(j) for j in reversed(range(CH))] + [ends] + [carried(j) for j in reversed(range(CH))]
                    + [done])

        def tail(b):
            cols = slice(b * KB, (b + 1) * KB)
            acc = {}

            def d_u(c):
                def emit():
                    part = jnp.dot(g[b, :, c:c + CK].astype(BF), btw[b, c:c + CK, :], preferred_element_type=F32)
                    acc["u"] = part if c == 0 else acc["u"] + part
                return emit

            def d_b(c):
                def emit():
                    dbacc[b, :, c:c + CK] += lax.dot_general(hpb[:, cols], g[b, :, c:c + CK].astype(BF), tn,
                                                             preferred_element_type=F32)
                return emit

            def finish():
                dh = (dys[:, cols] * d_ref[:, cols] + acc["u"]).astype(BF)
                dh_ref[:, cols] = jnp.dot(unperm[...], dh, preferred_element_type=F32).astype(dh_ref.dtype)

            return [f(c) for c in range(0, W2, CK) for f in (d_u, d_b)] + [finish]

        for emit in lead(0):
            emit()
        for b in range(NB):
            side = (lead(b + 1) if b + 1 < NB else []) + (tail(b - 1) if b > 0 else [])
            _interleave(scan(b), side)
        for emit in tail(NB - 1):
            emit()

        @pl.when(t == nt - 1)
        def _():
            for b in range(NB):
                db_ref[b] = _extract(dbacc[b], dims)
                dc_ref[b] = _extract(dcacc[b], dims)

    rev = lambda kb, t: (nt - 1 - t, kb)
    prev = lambda kb, t: (jnp.maximum((nt - 1 - t) * CH - 1, 0), kb)
    per_kb = lambda kb, t: (kb, 0, 0)
    return pl.pallas_call(
        body, name=name, grid=(nkb // NB, nt),
        in_specs=[pl.BlockSpec((TL, NB * KB), rev), pl.BlockSpec((TL, NB * KB), rev),
                  pl.BlockSpec((TL, NB * KB), rev), pl.BlockSpec((TL, NB * W2), rev),
                  pl.BlockSpec((8, NB * W2), prev), pl.BlockSpec((NB, KB, P2), per_kb),
                  pl.BlockSpec((NB, P2, KB), per_kb), pl.BlockSpec((NB, 40, W2), per_kb),
                  pl.BlockSpec((1, NB * KB), lambda kb, t: (0, kb))],
        out_specs=[pl.BlockSpec((TL, NB * KB), rev), pl.BlockSpec((8, NB * KB), lambda kb, t: (0, kb)),
                   pl.BlockSpec((NB, 8, W2), per_kb), pl.BlockSpec((NB, KB, P2), per_kb),
                   pl.BlockSpec((NB, KB, P2), per_kb)],
        out_shape=[jax.ShapeDtypeStruct((L, D), BF), jax.ShapeDtypeStruct((8, D), F32),
                   jax.ShapeDtypeStruct((nkb, 8, W2), F32), jax.ShapeDtypeStruct((nkb, KB, P2), F32),
                   jax.ShapeDtypeStruct((nkb, KB, P2), F32)],
        scratch_shapes=[pltpu.VMEM((NB, TL, W2), F32), pltpu.VMEM((NB, KB, W2), BF), pltpu.VMEM((NB, W2, KB), BF),
                        pltpu.VMEM((NB, KB, W2), F32), pltpu.VMEM((NB, KB, W2), F32), pltpu.VMEM((TL, NB * KB), F32),
                        pltpu.VMEM((TL, TL), BF), pltpu.VMEM((TL, TL), BF), pltpu.VMEM((NB, 2, 8, W), F32)],
        compiler_params=pltpu.CompilerParams(dimension_semantics=("parallel", "arbitrary"),
                                             vmem_limit_bytes=V7X_VMEM_BYTES - 4 * 1024 * 1024),
    )(dz, y, h, s, s, tc, tbt, pwr, dvec)


def _discretise(a_re, a_im, log_step, b_re, b_im):
    lr = jnp.minimum(a_re, -1e-4)
    li = a_im
    dt = jnp.exp(log_step)[:, None]
    mag = jnp.exp(lr * dt)
    abr = mag * jnp.cos(li * dt)
    abi = mag * jnp.sin(li * dt)
    den = lr * lr + li * li
    qr = ((abr - 1.0) * lr + abi * li) / den
    qi = (abi * lr - (abr - 1.0) * li) / den
    bbar_re = qr[..., None] * b_re - qi[..., None] * b_im
    bbar_im = qr[..., None] * b_im + qi[..., None] * b_re
    return abr, abi, bbar_re, bbar_im


def _compact(m_re, m_im, nkb):
    G, H, P = m_re.shape
    t = jnp.stack([m_re, m_im], axis=2).reshape(nkb, (G // nkb) * H, 2 * P).astype(BF)
    return t, jnp.swapaxes(t, 1, 2)


def _scan_powers(abr, abi, nkb, conj, CH):
    G, P = abr.shape
    if conj:
        abi = -abi

    def cmul(u, v):
        return u[0] * v[0] - u[1] * v[1], u[0] * v[1] + u[1] * v[0]

    q = (abr, abi)
    for _ in range(_log2(CH)):
        q = cmul(q, q)
    pows = [q]
    for _ in range(7):
        pows.append(cmul(pows[-1], q))
    row = jnp.arange(8)[:, None, None]

    def table(part):
        out = [jnp.broadcast_to((abr, abi)[part][None], (8, G, P))]
        for k in (1, 2, 4):
            keep = (row <= 7 - k) if conj else (row >= k)
            out.append(jnp.where(keep, pows[k - 1][part][None], 0.0))
        ends = jnp.stack([p[part] for p in pows])
        out.append(ends[::-1] if conj else ends)
        return jnp.concatenate(out, axis=0)

    GL = G // nkb
    t = jnp.stack([table(0), table(1)], axis=1)
    t = t.reshape(40, 2, nkb, GL * P).transpose(2, 0, 1, 3)
    return t.reshape(nkb, 40, 2 * GL * P)


def ada_mods(c_all, w_ada, b_sh, name):
    nl, D, NA = w_ada.shape

    def body(c_ref, w_ref, b_ref, o_ref):
        cv = c_ref[...]
        act = cv * _sigmoid(cv)
        o_ref[...] = jnp.dot(act, w_ref[...], preferred_element_type=F32, precision=lax.Precision.HIGHEST) + b_ref[...]

    return pl.pallas_call(
        body, name=name, grid=(nl,),
        in_specs=[pl.BlockSpec((8, D), lambda i: (0, 0)), pl.BlockSpec((None, D, NA), lambda i: (i, 0, 0)),
                  pl.BlockSpec((None, 1, NA), lambda i: (i, 0, 0))],
        out_specs=pl.BlockSpec((None, 8, NA), lambda i: (i, 0, 0)),
        out_shape=jax.ShapeDtypeStruct((nl, 8, NA), F32), compiler_params=_cp("parallel"))(c_all, w_ada, b_sh)


def _adamw(w, g, m, v):
    m = ADAM_B1 * m + (1.0 - ADAM_B1) * g
    v = ADAM_B2 * v + (1.0 - ADAM_B2) * (g * g)
    m_hat = m / (1.0 - ADAM_B1 ** ADAM_STEP)
    v_hat = v / (1.0 - ADAM_B2 ** ADAM_STEP)
    return -ADAM_LR * (m_hat / (jnp.sqrt(v_hat) + ADAM_EPS) + ADAM_WD * w), m, v


def _adam_rows(R, C):
    cap = max(8, (256 * 1024) // C)
    for t in range(min(R, cap), 0, -1):
        if R % t == 0 and (t % 8 == 0 or t == R):
            return t
    return R


def adamw_ada(c_t, dm, w, m, v, name):
    nl, D, NA = w.shape
    TK = _tile(D, (256, 128))

    def body(c_ref, dm_ref, w_ref, m_ref, v_ref, g_ref, d_ref, nm_ref, nv_ref):
        cv = c_ref[...]
        act = cv * _sigmoid(cv)
        g = jnp.dot(act, dm_ref[...], preferred_element_type=F32, precision=lax.Precision.HIGHEST)
        g_ref[...] = g
        d_ref[...], nm_ref[...], nv_ref[...] = _adamw(w_ref[...], g, m_ref[...], v_ref[...])

    big = pl.BlockSpec((None, TK, NA), lambda i, k: (i, k, 0))
    shape = jax.ShapeDtypeStruct(w.shape, F32)
    return pl.pallas_call(
        body, name=name, grid=(nl, D // TK),
        in_specs=[pl.BlockSpec((TK, 8), lambda i, k: (k, 0)), pl.BlockSpec((None, 8, NA), lambda i, k: (i, 0, 0)),
                  big, big, big],
        out_specs=[big] * 4, out_shape=[shape] * 4, compiler_params=_cp("parallel", "parallel"))(c_t, dm, w, m, v)


def adamw_sharded(w, m, v, ga, gb, name):
    nl, R, C = w.shape
    TR = _adam_rows(R, C)

    def body(w_ref, m_ref, v_ref, a_ref, b_ref, g_ref, d_ref, nm_ref, nv_ref):
        g = a_ref[...] + b_ref[...]
        g_ref[...] = g
        d_ref[...], nm_ref[...], nv_ref[...] = _adamw(w_ref[...], g, m_ref[...], v_ref[...])

    big = pl.BlockSpec((None, TR, C), lambda i, r: (i, r, 0))
    shape = jax.ShapeDtypeStruct(w.shape, F32)
    return pl.pallas_call(
        body, name=name, grid=(nl, R // TR), in_specs=[big] * 5,
        out_specs=[big] * 4, out_shape=[shape] * 4, compiler_params=_cp("parallel", "parallel"))(w, m, v, ga, gb)


def adamw_slab(g, w, m, v, name):
    R, C = g.shape
    TR = _tile(R, (160, 80, 40, 8))

    def body(g_ref, w_ref, m_ref, v_ref, d_ref, nm_ref, nv_ref):
        d_ref[...], nm_ref[...], nv_ref[...] = _adamw(w_ref[...], g_ref[...], m_ref[...], v_ref[...])

    big = pl.BlockSpec((TR, C), lambda r: (r, 0))
    shape = jax.ShapeDtypeStruct((R, C), F32)
    return pl.pallas_call(
        body, name=name, grid=(R // TR,), in_specs=[big] * 4,
        out_specs=[big] * 3, out_shape=[shape] * 3, compiler_params=_cp("parallel"))(g, w, m, v)


def adamw_plain(w, m, v, g, name):
    def body(w_ref, m_ref, v_ref, g_ref, d_ref, nm_ref, nv_ref):
        d_ref[...], nm_ref[...], nv_ref[...] = _adamw(w_ref[...], g_ref[...], m_ref[...], v_ref[...])

    shape = jax.ShapeDtypeStruct(w.shape, F32)
    return pl.pallas_call(body, name=name, out_shape=[shape] * 3,
                          compiler_params=pltpu.CompilerParams(vmem_limit_bytes=VMEM_LIMIT))(w, m, v, g)


def _slab_rows(a):
    n = a.size
    rows = -(-n // SLAB_W)
    return -(-rows // 8) * 8


def _pack(arrs, pad_rows_to=0):
    out = []
    for a in arrs:
        rows = _slab_rows(a)
        flat = a.reshape(-1).astype(F32)
        flat = jnp.pad(flat, (0, rows * SLAB_W - flat.shape[0]))
        out.append(flat.reshape(rows, SLAB_W))
    total = sum(o.shape[0] for o in out)
    if pad_rows_to and total % pad_rows_to:
        out.append(jnp.zeros((pad_rows_to - total % pad_rows_to, SLAB_W), F32))
    return jnp.concatenate(out, axis=0)


def _unpack(slab, like):
    out, r = [], 0
    for a in like:
        rows = _slab_rows(a)
        out.append(slab[r:r + rows].reshape(-1)[:a.size].reshape(a.shape))
        r += rows
    return out


WEIGHTS = ['norm1_g', 'norm2_g', 'w_ada', 'b_ada', 'ssm_a_re', 'ssm_a_im', 'ssm_log_step', 'ssm_b_re', 'ssm_b_im',
           'ssm_c_re', 'ssm_c_im', 'ssm_d', 'ssm_w_out', 'conv_w_in', 'conv_w', 'conv_w_out', 'w_ffn_in',
           'w_ffn_out', 'final_g']
SLAB = ['norm1_g', 'norm2_g', 'b_ada', 'ssm_a_re', 'ssm_a_im', 'ssm_log_step', 'ssm_b_re', 'ssm_b_im', 'ssm_c_re',
        'ssm_c_im', 'ssm_d', 'final_g']
SHARDED = ['ssm_w_out', 'conv_w_in', 'conv_w_out', 'w_ffn_in', 'w_ffn_out']


def kernel(x, c, norm1_g, norm2_g, w_ada, b_ada, ssm_a_re, ssm_a_im, ssm_log_step, ssm_b_re, ssm_b_im, ssm_c_re, ssm_c_im, ssm_d, ssm_w_out, conv_w_in, conv_w, conv_w_out, w_ffn_in, w_ffn_out, final_g, loss_target, m_norm1_g, m_norm2_g, m_w_ada, m_b_ada, m_ssm_a_re, m_ssm_a_im, m_ssm_log_step, m_ssm_b_re, m_ssm_b_im, m_ssm_c_re, m_ssm_c_im, m_ssm_d, m_ssm_w_out, m_conv_w_in, m_conv_w, m_conv_w_out, m_w_ffn_in, m_w_ffn_out, m_final_g, v_norm1_g, v_norm2_g, v_w_ada, v_b_ada, v_ssm_a_re, v_ssm_a_im, v_ssm_log_step, v_ssm_b_re, v_ssm_b_im, v_ssm_c_re, v_ssm_c_im, v_ssm_d, v_ssm_w_out, v_conv_w_in, v_conv_w, v_conv_w_out, v_w_ffn_in, v_w_ffn_out, v_final_g):
    given = dict(locals())
    W = {n: given[n] for n in WEIGHTS}
    Mo = {n: given["m_" + n] for n in WEIGHTS}
    Vo = {n: given["v_" + n] for n in WEIGHTS}

    xs = x[0]
    tgt = loss_target[0]
    L, D = xs.shape
    nlayer = norm1_g.shape[0]
    NA = w_ada.shape[2]
    G = ssm_a_re.shape[1]
    nkb = D // S5_BLOCK
    ax, ay, ac = _axes()
    me = 4 * ax + 2 * ay + ac
    chip = 2 * ax + ay

    assert D == SLAB_W
    first = gather8(jnp.concatenate([jnp.broadcast_to(c, (8, D)), _pack([conv_w])], axis=0), "gather_c_conv_w")
    c_all = first[:, 0, :]
    b_sh = lax.dynamic_slice_in_dim(b_ada, chip * NA, NA, axis=1)[:, None, :]
    mods_part = ada_mods(c_all, w_ada, b_sh, "ada_mods")
    mg = gather8(mods_part.reshape(nlayer * 8, NA), "gather_mods")
    mg = mg.reshape(N_CHIP, 2, nlayer, 8, NA)[:, 0]
    mods_all = lax.dynamic_index_in_dim(mg, me, axis=2, keepdims=False)
    mods_all = jnp.transpose(mods_all, (1, 0, 2)).reshape(nlayer, 6, D)

    cw_parts = first[:, 8:]
    nconv = conv_w.shape[0]
    cw_full = jnp.stack([_unpack(cw_parts[2 * q], [conv_w])[0] for q in range(N_CHIP)], axis=2)
    cw_full = cw_full.reshape(nconv, 3, D)

    in_flight_w = {}

    def start_weights(i, after):
        names = (["ssm_w_out"] if i % 2 == 0 else ["conv_w_in", "conv_w_out"]) + ["w_ffn_in", "w_ffn_out"]
        shards = [W[n][i if n.startswith("w_ffn") else i // 2].astype(BF) for n in names]
        sems, srcs, lands, tok = gather_start(shards, after, "gather_start%d" % i)
        in_flight_w[i] = (names, sems, srcs, lands)
        return tok

    def relay_weights(i, after):
        names, sems, srcs, lands = in_flight_w[i]
        got = gather_wait(sems, srcs, lands, list(range(len(names))), after, "gather_wait%d" % i)
        rsems, rlands, tok = relay_start(got, after, "relay_start%d" % i)
        in_flight_w[i] = (names, rsems, rlands)
        return tok

    def layer_weights(i, after):
        names, rsems, rlands = in_flight_w[i]
        return dict(zip(names, relay_wait(rsems, rlands, after, "relay_wait%d" % i)))

    token = start_weights(0, cw_full + mods_all[0, 0:3])
    mods_all = mods_all + token[0:1, 0:1]

    s5 = []
    for j in range(ssm_a_re.shape[0]):
        disc, disc_vjp = jax.vjp(_discretise, ssm_a_re[j], ssm_a_im[j], ssm_log_step[j], ssm_b_re[j], ssm_b_im[j])
        abr, abi, bbar_re, bbar_im = disc
        tb, tbt = _compact(jnp.swapaxes(bbar_re, 1, 2), jnp.swapaxes(bbar_im, 1, 2), nkb)
        tc, tct = _compact(ssm_c_re[j], -ssm_c_im[j], nkb)
        chunk = _tile(L, (512, 256)) // 8
        s5.append(dict(vjp=disc_vjp, tb=tb, tbt=tbt, tc=tc, tct=tct, pw=_scan_powers(abr, abi, nkb, False, chunk),
                       pwr=_scan_powers(abr, abi, nkb, True, chunk)))

    saved = []
    xcur = xs
    for i in range(nlayer):
        j = i // 2
        mods = mods_all[i]
        sv = dict(x=xcur)
        if i % 2 == 0:
            h = norm_mod(xcur, norm1_g[i:i + 1], mods, 0, F32, "norm_mod_s5")
            dvec = ssm_d[j:j + 1]
            if i == 0:
                dvec = dvec + start_weights(1, h)[0:1, 0:1]
            states, yv, z = s5_fwd(h, s5[j]["tb"], s5[j]["tct"], s5[j]["pw"], dvec, "s5_fwd")
            if i == 0:
                mods = mods + relay_weights(0, z)[0:1, 0:1]
            full = layer_weights(i, z)
            o, mix, x2 = ssm_out_glu(z, full["ssm_w_out"], xcur, mods, 2, "ssm_out_glu")
            sv.update(h=h, states=states, y=yv, z=z, o=o)
        else:
            h = norm_mod(xcur, norm1_g[i:i + 1], mods, 0, BF, "norm_mod")
            full = layer_weights(i, h)
            p = mm_nn(h, full["conv_w_in"], BF, "mm_conv_in")
            mc = conv_fwd(p, cw_full[j], "conv_fwd")
            mix, x2 = mm_nn(mc, full["conv_w_out"].reshape(1, D, D), BF, "mm_conv_out", res=xcur, gate=mods[2:3])
            sv.update(h=h, p=p, mc=mc)
        h2 = norm_mod(x2, norm2_g[i:i + 1], mods, 3, BF, "norm_mod")
        gu, act = ffn_in_act(h2, full["w_ffn_in"], "ffn_in_act")
        F = act.shape[1]
        ff, x3 = mm_nn(act, full["w_ffn_out"].reshape(1, F, D), BF, "mm_ffn_out", res=x2, gate=mods[5:6])
        sv.update(mix=mix, x2=x2, h2=h2, gu=gu, act=act, ff=ff, w=full)
        saved.append(sv)
        xcur = x3
        if i + 1 < nlayer:
            token = relay_weights(i + 1, ff)
            if i + 2 < nlayer:
                token = token + start_weights(i + 2, token)
            mods_all = mods_all + token[0:1, 0:1]

    loss_blk, dx, dfinal, dff = final_loss(xcur, tgt, final_g[None, :], saved[-1]["ff"], mods_all[nlayer - 1], 5,
                                           "final_loss")
    dg2 = dfinal[1:2]

    gland = {n: lax.empty((W[n].shape[0], N_CHIP) + W[n].shape[1:], BF) for n in SHARDED}
    in_flight = []
    dmods = [None] * nlayer
    dnorm1, dnorm2 = [None] * nlayer, [None] * nlayer
    dconv_w = [None] * nconv
    ds5 = [None] * ssm_a_re.shape[0]
    token = jnp.zeros((8, 128), F32)

    def send_grads(names, grads, slot, after, name):
        sems, thru, lands, tok = scatter_start([grads[n] for n in names], [gland[n] for n in names], slot, after, name)
        gland.update(zip(names, lands))
        in_flight.append((names, slot, sems, thru, name))
        return tok

    def land_grads(group, after):
        for names, slot, sems, thru, name in in_flight:
            if names[0] in group:
                got = scatter_wait(sems, thru, [gland[n] for n in names], slot, after, name.replace("scatter", "landed"))
                gland.update(zip(names, got))

    for i in reversed(range(nlayer)):
        j = i // 2
        mods = mods_all[i] + token[0:1, 0:1]
        sv = saved[i]
        full = sv["w"]
        gfull = {}
        F = sv["act"].shape[1]
        gfull["w_ffn_out"] = mm_tn(sv["act"], dff, 1, "mm_tn_ffn_out").reshape(N_CHIP, F // N_CHIP, D)
        dgu = ffn_out_bwd(dff, full["w_ffn_out"].reshape(F, D), sv["gu"], "ffn_out_bwd")
        gfull["w_ffn_in"] = mm_tn(sv["h2"], dgu, N_CHIP, "mm_tn_ffn_in")
        dh2 = mm_nt(dgu, full["w_ffn_in"], BF, "mm_nt_ffn_in")
        token = send_grads(["w_ffn_out", "w_ffn_in"], gfull, [i, i], dh2, "scatter_ffn%d" % i)
        mods = mods + token[0:1, 0:1]
        dx2, s2, dmix = norm_bwd(dh2, sv["x2"], dx, norm2_g[i:i + 1], mods, 3, "norm_bwd_mix",
                                 branch=(sv["mix"], mods, 2))
        dg1 = s2[3:4]
        if i % 2 == 0:
            do = glu_bwd(dmix, sv["o"], "glu_bwd")
            gfull["ssm_w_out"] = mm_tn(sv["z"], do, N_CHIP, "mm_tn_ssm_out")
            dz = mm_nt(do, full["ssm_w_out"], BF, "mm_nt_ssm_out")
            dh, dd, dab, db, dc = s5_bwd(dz, sv["y"], sv["h"], sv["states"], s5[j]["tc"], s5[j]["tbt"], s5[j]["pwr"],
                                         ssm_d[j:j + 1], "s5_bwd")
            ds5[j] = (dd, dab, db, dc)
        else:
            gfull["conv_w_out"] = mm_tn(sv["mc"], dmix, 1, "mm_tn_conv_out").reshape(N_CHIP, D // N_CHIP, D)
            dmc = mm_nt(dmix, full["conv_w_out"].reshape(1, D, D), BF, "mm_nt_conv_out")
            dbg, dcg, dvv, dcw = conv_bwd(dmc, sv["p"], cw_full[j], "conv_bwd")
            dp = jnp.concatenate([dbg, dcg, dvv], axis=1)
            gfull["conv_w_in"] = mm_tn(sv["h"], dp, N_CHIP, "mm_tn_conv_in")
            dh = mm_nt(dp, full["conv_w_in"], BF, "mm_nt_conv_in")
            dconv_w[j] = dcw[0:3]
        dmods_i = [s2[0:2], dg2]
        if i > 0:
            dx, s1, dff = norm_bwd(dh, sv["x"], dx2, norm1_g[i:i + 1], mods, 0, "norm_bwd_ffn",
                                   branch=(saved[i - 1]["ff"], mods_all[i - 1], 5))
            dg2 = s1[3:4]
        else:
            dx, s1 = norm_bwd(dh, sv["x"], dx2, norm1_g[i:i + 1], mods, 0, "norm_bwd")
        dmods[i] = jnp.concatenate([s1[0:2], dg1] + dmods_i, axis=0).reshape(6 * D)
        dnorm1[i], dnorm2[i] = s1[2], s2[2]
        names = ["ssm_w_out"] if i % 2 == 0 else ["conv_w_out", "conv_w_in"]
        token = send_grads(names, gfull, [j] * len(names), dx, "scatter_mix%d" % i)

    small = dict(norm1_g=jnp.stack(dnorm1), norm2_g=jnp.stack(dnorm2), b_ada=jnp.stack(dmods),
                 final_g=dfinal[0] + token[0, 0])
    per = {n: [] for n in ('ssm_a_re', 'ssm_a_im', 'ssm_log_step', 'ssm_b_re', 'ssm_b_im', 'ssm_c_re', 'ssm_c_im', 'ssm_d')}
    GL = G // nkb
    for j, (dd, dab, db, dc) in enumerate(ds5):
        dab = jnp.sum(dab, axis=1).reshape(nkb, 2, GL, SSM_STATE)
        g_abr, g_abi = dab[:, 0].reshape(G, SSM_STATE), dab[:, 1].reshape(G, SSM_STATE)
        db, dc = db.reshape(G, SSM_GROUP, 2, SSM_STATE), dc.reshape(G, SSM_GROUP, 2, SSM_STATE)
        gb_re, gb_im, gc_re, gc_im = db[:, :, 0], db[:, :, 1], dc[:, :, 0], dc[:, :, 1]
        ga_re, ga_im, gls, gbr, gbi = s5[j]["vjp"]((g_abr, g_abi, jnp.swapaxes(gb_re, 1, 2), jnp.swapaxes(gb_im, 1, 2)))
        for n, val in zip(per, (ga_re, ga_im, gls, gbr, gbi, gc_re, -gc_im, jnp.sum(dd, axis=0))):
            per[n].append(val)
    small.update({n: jnp.stack(vals) for n, vals in per.items()})
    dcw_full = jnp.stack(dconv_w)

    my_loss = loss_blk[0:1, 0:1]
    slab_like = [W[n] for n in SLAB] + [dcw_full, my_loss]
    rows64 = 8 * N_DEV
    slab = _pack([small[n] for n in SLAB] + [dcw_full, my_loss], rows64)
    per_dev = slab.shape[0] // N_DEV
    x_sems, x_srcs, x_lands, token = exchange_start(
        [(slab.reshape(N_DEV, per_dev, SLAB_W), True), (_pack([small["b_ada"]]), False)], dx, "small_scatter")

    early = [n for n in SHARDED if n != "ssm_w_out"]
    land_grads(early, token)
    mine = [reduce4(gland[n], "reduce4_" + n) for n in early]

    parts, dm_all = exchange_wait(x_sems, x_srcs, x_lands, [True, False], mine[-1][0, :8, :128], "small_landed")
    t_sems, t_srcs, t_lands, token = exchange_start([(sum8(parts, "sum_small"), False)], dm_all, "small_gather")
    out = {}

    w_sems, w_srcs, w_lands, token2 = swap_start(mine, "swap_start")
    dm_all = dm_all.reshape(N_DEV, -1)[:, :b_ada.size].reshape(N_DEV, nlayer, N_CHIP, NA)
    dm_sh = jnp.transpose(lax.dynamic_index_in_dim(dm_all, chip, axis=2, keepdims=False), (1, 0, 2))
    res = adamw_ada(jnp.transpose(c_all) + token[0:1, 0:1] + token2[0:1, 0:1], dm_sh, w_ada, m_w_ada, v_w_ada,
                    "adamw_ada")
    out["g", "w_ada"], out["d", "w_ada"], out["m", "w_ada"], out["v", "w_ada"] = res

    g_slab = exchange_wait(t_sems, t_srcs, t_lands, [False], out["g", "w_ada"], "small_total")[0]
    g_slab = g_slab.reshape(slab.shape)
    d_slab, m_slab, v_slab = adamw_slab(
        g_slab, _pack([W[n] for n in SLAB] + [jnp.zeros_like(dcw_full)], rows64),
        _pack([Mo[n] for n in SLAB] + [jnp.zeros_like(dcw_full)], rows64),
        _pack([Vo[n] for n in SLAB] + [jnp.ones_like(dcw_full)], rows64), "adamw_slab")
    for k, slab in zip(("g", "d", "m", "v"), (g_slab, d_slab, m_slab, v_slab)):
        for n, val in zip(SLAB, _unpack(slab, slab_like)):
            out[k, n] = val
    g_cw = lax.dynamic_slice_in_dim(_unpack(g_slab, slab_like)[-2], chip * conv_w.shape[2], conv_w.shape[2], axis=2)
    out["g", "conv_w"] = g_cw
    out["d", "conv_w"], out["m", "conv_w"], out["v", "conv_w"] = [
        r.reshape(conv_w.shape) for r in adamw_plain(conv_w.reshape(-1, conv_w.shape[2]), m_conv_w.reshape(-1, conv_w.shape[2]),
                                                     v_conv_w.reshape(-1, conv_w.shape[2]), g_cw.reshape(-1, conv_w.shape[2]),
                                                     "adamw_conv_w")]

    mine, theirs = swap_wait(w_sems, w_srcs, w_lands, d_slab, "swap_wait")
    for n, ga, gb in zip(early, mine, theirs):
        r = adamw_sharded(W[n], Mo[n], Vo[n], ga, gb, "adamw_" + n)
        out["g", n], out["d", n], out["m", n], out["v", n] = r

    land_grads(["ssm_w_out"], out["g", "w_ffn_out"])
    ga = reduce4(gland["ssm_w_out"], "reduce4_ssm_w_out")
    gb = swap_siblings([ga], "swap_siblings")[0]
    r = adamw_sharded(ssm_w_out, m_ssm_w_out, v_ssm_w_out, ga, gb, "adamw_ssm_w_out")
    out["g", "ssm_w_out"], out["d", "ssm_w_out"], out["m", "ssm_w_out"], out["v", "ssm_w_out"] = r

    loss = _unpack(g_slab, slab_like)[-1][0, 0]
    return (loss, dx[None], *[out["g", n] for n in WEIGHTS], *[out["d", n] for n in WEIGHTS],
            *[out["m", n] for n in WEIGHTS], *[out["v", n] for n in WEIGHTS])
```

```python
import math

import jax
import jax.numpy as jnp
from jax import lax
from jax.experimental import pallas as pl
from jax.experimental.pallas import tpu as pltpu

F32 = jnp.float32
BF = jnp.bfloat16
MESH = pl.DeviceIdType.MESH
ANY = pl.BlockSpec(memory_space=pl.ANY)

N_DEV = 8
N_CHIP = 4
SSM_GROUP = 16
SSM_STATE = 64
S5_BLOCK = 256
RMS_EPS = 1e-6
ADAM_LR, ADAM_B1, ADAM_B2, ADAM_EPS, ADAM_WD, ADAM_STEP = 0.001, 0.9, 0.999, 1e-08, 0.01, 10
V7X_VMEM_BYTES = 64 * 1024 * 1024
VMEM_LIMIT = V7X_VMEM_BYTES - 12 * 1024 * 1024
SLAB_W = 1024
GELU_C = math.sqrt(2.0 / math.pi)
GELU_A = 0.044715


def _cp(*sem):
    return pltpu.CompilerParams(dimension_semantics=sem if sem else None, vmem_limit_bytes=VMEM_LIMIT)


def _tile(n, prefs):
    for p in prefs:
        if p <= n and n % p == 0:
            return p
    return n


def _sigmoid(v):
    return 0.5 * jnp.tanh(0.5 * v) + 0.5


def _axes():
    return lax.axis_index("x"), lax.axis_index("y"), lax.axis_index("c")


def _flip(v, k):
    return 1 - v if k else v


def gather8(v, name):
    R, C = v.shape

    def body(v_ref, o_ref, ssem, rsem, lsem):
        x, y, c = _axes()
        me = 4 * x + 2 * y + c
        loc = pltpu.make_async_copy(v_ref, o_ref.at[me], lsem)
        loc.start()
        copies = []
        for k in range(1, N_DEV):
            peer = (_flip(x, (k >> 2) & 1), _flip(y, (k >> 1) & 1), _flip(c, k & 1))
            cp = pltpu.make_async_remote_copy(src_ref=v_ref, dst_ref=o_ref.at[me], send_sem=ssem.at[k - 1],
                                              recv_sem=rsem.at[k - 1], device_id=peer, device_id_type=MESH)
            cp.start()
            copies.append(cp)
        for cp in copies:
            cp.wait()
        loc.wait()

    return pl.pallas_call(
        body, name=name,
        out_shape=jax.ShapeDtypeStruct((N_DEV, R, C), v.dtype),
        in_specs=[pl.BlockSpec(memory_space=pltpu.VMEM)],
        out_specs=pl.BlockSpec(memory_space=pltpu.VMEM),
        scratch_shapes=[pltpu.SemaphoreType.DMA((N_DEV - 1,)), pltpu.SemaphoreType.DMA((N_DEV - 1,)),
                        pltpu.SemaphoreType.DMA],
        compiler_params=pltpu.CompilerParams(vmem_limit_bytes=VMEM_LIMIT),
    )(v)


HBM = pl.BlockSpec(memory_space=pltpu.HBM)
SEM = pl.BlockSpec(memory_space=pltpu.SEMAPHORE)
EFFECT = pltpu.SideEffectType.DATAFLOW_SIDE_EFFECTING


def _in_hbm(a):
    return pltpu.with_memory_space_constraint(a, pltpu.HBM)


def _chip_peers(x, y, c):
    out = []
    for k in range(1, N_CHIP):
        px, py = _flip(x, k >> 1), _flip(y, k & 1)
        out.append(((px, py, c), 2 * px + py))
    return out


def _my_half(ref, c):
    rows = ref.shape[0] // 2
    return pl.ds(pl.multiple_of(c * rows, 16), rows)


def relay_start(lands, after, name):
    n = len(lands)

    def body(*refs):
        land = refs[:n]
        ssem, rsem = refs[n + 1:n + 3]
        token = refs[-1]
        x, y, c = _axes()
        for a in range(n):
            half = _my_half(land[a].at[0], c)
            for k, (_, pchip) in enumerate(_chip_peers(x, y, c)):
                pltpu.make_async_remote_copy(src_ref=land[a].at[pchip, half], dst_ref=land[a].at[pchip, half],
                                             send_sem=ssem.at[3 * a + k], recv_sem=rsem.at[3 * a + k],
                                             device_id=(x, y, 1 - c), device_id_type=MESH).start()
        token[...] = jnp.zeros_like(token)

    out_shape = ([pltpu.SemaphoreType.DMA((3 * n,)), pltpu.SemaphoreType.DMA((3 * n,))]
                 + [pltpu.HBM(l.shape, l.dtype) for l in lands] + [jax.ShapeDtypeStruct((8, 128), F32)])
    res = pl.pallas_call(
        body, name=name, out_shape=out_shape, in_specs=[HBM] * n + [ANY],
        out_specs=[SEM, SEM] + [HBM] * n + [pl.BlockSpec(memory_space=pltpu.VMEM)],
        input_output_aliases={a: 2 + a for a in range(n)},
        compiler_params=pltpu.CompilerParams(has_side_effects=EFFECT),
    )(*lands, after)
    return tuple(res[:2]), list(res[2:2 + n]), res[-1]


def relay_wait(sems, lands, after, name):
    n = len(lands)

    def body(*refs):
        land = refs[:n]
        ssem, rsem = refs[n:n + 2]
        x, y, c = _axes()
        for a in range(n):
            mine, theirs = _my_half(land[a].at[0], c), _my_half(land[a].at[0], 1 - c)
            for k, (_, pchip) in enumerate(_chip_peers(x, y, c)):
                cp = pltpu.make_async_remote_copy(src_ref=land[a].at[pchip, mine], dst_ref=land[a].at[pchip, theirs],
                                                  send_sem=ssem.at[3 * a + k], recv_sem=rsem.at[3 * a + k],
                                                  device_id=(x, y, 1 - c), device_id_type=MESH)
                cp.wait_send()
                cp.wait_recv()

    res = pl.pallas_call(
        body, name=name, out_shape=[pltpu.HBM(l.shape, l.dtype) for l in lands],
        in_specs=[HBM] * n + [SEM, SEM, ANY], out_specs=[HBM] * n,
        input_output_aliases={a: a for a in range(n)},
        compiler_params=pltpu.CompilerParams(has_side_effects=EFFECT),
    )(*lands, *sems, after)
    return list(res)


def gather_start(shards, after, name):
    n = len(shards)

    def body(*refs):
        src, land = refs[:n], refs[n:2 * n]
        ssem, rsem, lsem = refs[2 * n + 1:2 * n + 4]
        token = refs[-1]
        x, y, c = _axes()
        chip = 2 * x + y
        for a in range(n):
            pltpu.make_async_copy(src[a], land[a].at[chip], lsem.at[a]).start()
            half = _my_half(src[a], c)
            for k, (peer, _) in enumerate(_chip_peers(x, y, c)):
                pltpu.make_async_remote_copy(src_ref=src[a].at[half], dst_ref=land[a].at[chip, half],
                                             send_sem=ssem.at[3 * a + k], recv_sem=rsem.at[3 * a + k],
                                             device_id=peer, device_id_type=MESH).start()
        token[...] = jnp.zeros_like(token)

    lands = [lax.empty((N_CHIP,) + s.shape, s.dtype) for s in shards]
    out_shape = ([pltpu.SemaphoreType.DMA((3 * n,)), pltpu.SemaphoreType.DMA((3 * n,)), pltpu.SemaphoreType.DMA((n,))]
                 + [pltpu.HBM(s.shape, s.dtype) for s in shards] + [pltpu.HBM(l.shape, l.dtype) for l in lands]
                 + [jax.ShapeDtypeStruct((8, 128), F32)])
    res = pl.pallas_call(
        body, name=name, out_shape=out_shape, in_specs=[HBM] * (2 * n) + [ANY],
        out_specs=[SEM, SEM, SEM] + [HBM] * (2 * n) + [pl.BlockSpec(memory_space=pltpu.VMEM)],
        input_output_aliases={a: 3 + a for a in range(2 * n)},
        compiler_params=pltpu.CompilerParams(has_side_effects=EFFECT),
    )(*[_in_hbm(s) for s in shards], *[_in_hbm(l) for l in lands], after)
    return tuple(res[:3]), list(res[3:3 + n]), list(res[3 + n:3 + 2 * n]), res[-1]


def gather_wait(sems, srcs, lands, idx, after, name):
    m = len(idx)

    def body(*refs):
        src, land = refs[:m], refs[m:2 * m]
        ssem, rsem, lsem = refs[2 * m:2 * m + 3]
        x, y, c = _axes()
        chip = 2 * x + y
        for j, a in enumerate(idx):
            half = _my_half(src[j], c)
            for k, (peer, pchip) in enumerate(_chip_peers(x, y, c)):
                cp = pltpu.make_async_remote_copy(src_ref=src[j].at[half], dst_ref=land[j].at[pchip, half],
                                                  send_sem=ssem.at[3 * a + k], recv_sem=rsem.at[3 * a + k],
                                                  device_id=peer, device_id_type=MESH)
                cp.wait_send()
                cp.wait_recv()
            pltpu.make_async_copy(src[j], land[j].at[chip], lsem.at[a]).wait()

    s_in = [srcs[a] for a in idx]
    l_in = [lands[a] for a in idx]
    res = pl.pallas_call(
        body, name=name,
        out_shape=[pltpu.HBM(s.shape, s.dtype) for s in s_in] + [pltpu.HBM(l.shape, l.dtype) for l in l_in],
        in_specs=[HBM] * (2 * m) + [SEM, SEM, SEM, ANY], out_specs=[HBM] * (2 * m),
        input_output_aliases={a: a for a in range(2 * m)},
        compiler_params=pltpu.CompilerParams(has_side_effects=EFFECT),
    )(*s_in, *l_in, *sems, after)
    return list(res[m:])


def scatter_start(grads, lands, slot, after, name):
    n = len(grads)

    def body(*refs):
        src, land = refs[:n], refs[n:2 * n]
        ssem, rsem, lsem = refs[2 * n + 1:2 * n + 4]
        token = refs[-1]
        x, y, c = _axes()
        chip = 2 * x + y
        for a in range(n):
            pltpu.make_async_copy(src[a].at[chip], land[a].at[slot[a], chip], lsem.at[a]).start()
            for k, (peer, pchip) in enumerate(_chip_peers(x, y, c)):
                pltpu.make_async_remote_copy(src_ref=src[a].at[pchip], dst_ref=land[a].at[slot[a], chip],
                                             send_sem=ssem.at[3 * a + k], recv_sem=rsem.at[3 * a + k],
                                             device_id=peer, device_id_type=MESH).start()
        token[...] = jnp.zeros_like(token)

    out_shape = ([pltpu.SemaphoreType.DMA((3 * n,)), pltpu.SemaphoreType.DMA((3 * n,)), pltpu.SemaphoreType.DMA((n,))]
                 + [pltpu.HBM(g.shape, g.dtype) for g in grads] + [pltpu.HBM(l.shape, l.dtype) for l in lands]
                 + [jax.ShapeDtypeStruct((8, 128), F32)])
    res = pl.pallas_call(
        body, name=name, out_shape=out_shape, in_specs=[HBM] * (2 * n) + [ANY],
        out_specs=[SEM, SEM, SEM] + [HBM] * (2 * n) + [pl.BlockSpec(memory_space=pltpu.VMEM)],
        input_output_aliases={a: 3 + a for a in range(2 * n)},
        compiler_params=pltpu.CompilerParams(has_side_effects=EFFECT),
    )(*[_in_hbm(g) for g in grads], *[_in_hbm(l) for l in lands], after)
    return tuple(res[:3]), list(res[3:3 + n]), list(res[3 + n:3 + 2 * n]), res[-1]


def scatter_wait(sems, grads, lands, slot, after, name):
    n = len(grads)

    def body(*refs):
        src, land = refs[:n], refs[n:2 * n]
        ssem, rsem, lsem = refs[2 * n:2 * n + 3]
        x, y, c = _axes()
        chip = 2 * x + y
        for a in range(n):
            for k, (peer, pchip) in enumerate(_chip_peers(x, y, c)):
                cp = pltpu.make_async_remote_copy(src_ref=src[a].at[pchip], dst_ref=land[a].at[slot[a], pchip],
                                                  send_sem=ssem.at[3 * a + k], recv_sem=rsem.at[3 * a + k],
                                                  device_id=peer, device_id_type=MESH)
                cp.wait_send()
                cp.wait_recv()
            pltpu.make_async_copy(src[a].at[chip], land[a].at[slot[a], chip], lsem.at[a]).wait()

    res = pl.pallas_call(
        body, name=name,
        out_shape=[pltpu.HBM(g.shape, g.dtype) for g in grads] + [pltpu.HBM(l.shape, l.dtype) for l in lands],
        in_specs=[HBM] * (2 * n) + [SEM, SEM, SEM, ANY], out_specs=[HBM] * (2 * n),
        input_output_aliases={a: a for a in range(2 * n)},
        compiler_params=pltpu.CompilerParams(has_side_effects=EFFECT),
    )(*grads, *lands, *sems, after)
    return list(res[n:])


def reduce4(land, name):
    nl, _, R, C = land.shape
    TR = _adam_rows(R, C)

    def body(l_ref, o_ref):
        o_ref[...] = ((l_ref[0].astype(F32) + l_ref[1].astype(F32)) + l_ref[2].astype(F32)) + l_ref[3].astype(F32)

    return pl.pallas_call(
        body, name=name, grid=(nl, R // TR),
        in_specs=[pl.BlockSpec((None, N_CHIP, TR, C), lambda i, r: (i, 0, r, 0))],
        out_specs=pl.BlockSpec((None, TR, C), lambda i, r: (i, r, 0)),
        out_shape=jax.ShapeDtypeStruct((nl, R, C), F32), compiler_params=_cp("parallel", "parallel"))(land)


def swap_siblings(arrs, name):
    n = len(arrs)

    def body(*refs):
        src, dst = refs[:n], refs[n:2 * n]
        ssem, rsem = refs[2 * n:]
        x, y, c = _axes()
        cps = [pltpu.make_async_remote_copy(src_ref=src[a], dst_ref=dst[a], send_sem=ssem.at[a], recv_sem=rsem.at[a],
                                            device_id=(x, y, 1 - c), device_id_type=MESH) for a in range(n)]
        for cp in cps:
            cp.start()
        for cp in cps:
            cp.wait()

    return pl.pallas_call(
        body, name=name, out_shape=[jax.ShapeDtypeStruct(a.shape, a.dtype) for a in arrs],
        in_specs=[ANY] * n, out_specs=[ANY] * n,
        scratch_shapes=[pltpu.SemaphoreType.DMA((n,)), pltpu.SemaphoreType.DMA((n,))],
        compiler_params=pltpu.CompilerParams(vmem_limit_bytes=VMEM_LIMIT),
    )(*arrs)


def swap_start(arrs, name):
    n = len(arrs)

    def body(*refs):
        src, land = refs[:n], refs[n:2 * n]
        ssem, rsem = refs[2 * n:2 * n + 2]
        token = refs[-1]
        x, y, c = _axes()
        for a in range(n):
            pltpu.make_async_remote_copy(src_ref=src[a], dst_ref=land[a], send_sem=ssem.at[a], recv_sem=rsem.at[a],
                                         device_id=(x, y, 1 - c), device_id_type=MESH).start()
        token[...] = jnp.zeros_like(token)

    lands = [lax.empty(a.shape, a.dtype) for a in arrs]
    out_shape = ([pltpu.SemaphoreType.DMA((n,)), pltpu.SemaphoreType.DMA((n,))]
                 + [pltpu.HBM(a.shape, a.dtype) for a in arrs] * 2 + [jax.ShapeDtypeStruct((8, 128), F32)])
    res = pl.pallas_call(
        body, name=name, out_shape=out_shape, in_specs=[HBM] * (2 * n),
        out_specs=[SEM, SEM] + [HBM] * (2 * n) + [pl.BlockSpec(memory_space=pltpu.VMEM)],
        input_output_aliases={a: 2 + a for a in range(2 * n)},
        compiler_params=pltpu.CompilerParams(has_side_effects=EFFECT),
    )(*[_in_hbm(a) for a in arrs], *[_in_hbm(l) for l in lands])
    return tuple(res[:2]), list(res[2:2 + n]), list(res[2 + n:2 + 2 * n]), res[-1]


def swap_wait(sems, srcs, lands, after, name):
    n = len(srcs)

    def body(*refs):
        src, land = refs[:n], refs[n:2 * n]
        ssem, rsem = refs[2 * n:2 * n + 2]
        x, y, c = _axes()
        for a in range(n):
            cp = pltpu.make_async_remote_copy(src_ref=src[a], dst_ref=land[a], send_sem=ssem.at[a],
                                              recv_sem=rsem.at[a], device_id=(x, y, 1 - c), device_id_type=MESH)
            cp.wait_send()
            cp.wait_recv()

    res = pl.pallas_call(
        body, name=name, out_shape=[pltpu.HBM(a.shape, a.dtype) for a in srcs] * 2,
        in_specs=[HBM] * (2 * n) + [SEM, SEM, ANY], out_specs=[HBM] * (2 * n),
        input_output_aliases={a: a for a in range(2 * n)},
        compiler_params=pltpu.CompilerParams(has_side_effects=EFFECT),
    )(*srcs, *lands, *sems, after)
    return list(res[:n]), list(res[n:])


def _all_peers(x, y, c):
    out = []
    for k in range(1, N_DEV):
        px, py, pc = _flip(x, (k >> 2) & 1), _flip(y, (k >> 1) & 1), _flip(c, k & 1)
        out.append(((px, py, pc), 4 * px + 2 * py + pc))
    return out


def exchange_start(items, after, name):
    n = len(items)

    def body(*refs):
        src, land = refs[:n], refs[n:2 * n]
        ssem, rsem, lsem = refs[2 * n + 1:2 * n + 4]
        token = refs[-1]
        x, y, c = _axes()
        me = 4 * x + 2 * y + c
        for a, (_, scatter) in enumerate(items):
            pltpu.make_async_copy(src[a].at[me] if scatter else src[a], land[a].at[me], lsem.at[a]).start()
            for k, (peer, p) in enumerate(_all_peers(x, y, c)):
                pltpu.make_async_remote_copy(src_ref=src[a].at[p] if scatter else src[a], dst_ref=land[a].at[me],
                                             send_sem=ssem.at[7 * a + k], recv_sem=rsem.at[7 * a + k],
                                             device_id=peer, device_id_type=MESH).start()
        token[...] = jnp.zeros_like(token)

    srcs = [s for s, _ in items]
    lands = [lax.empty(s.shape if sc else (N_DEV,) + s.shape, s.dtype) for s, sc in items]
    out_shape = ([pltpu.SemaphoreType.DMA((7 * n,)), pltpu.SemaphoreType.DMA((7 * n,)), pltpu.SemaphoreType.DMA((n,))]
                 + [pltpu.HBM(s.shape, s.dtype) for s in srcs] + [pltpu.HBM(l.shape, l.dtype) for l in lands]
                 + [jax.ShapeDtypeStruct((8, 128), F32)])
    res = pl.pallas_call(
        body, name=name, out_shape=out_shape, in_specs=[HBM] * (2 * n) + [ANY],
        out_specs=[SEM, SEM, SEM] + [HBM] * (2 * n) + [pl.BlockSpec(memory_space=pltpu.VMEM)],
        input_output_aliases={a: 3 + a for a in range(2 * n)},
        compiler_params=pltpu.CompilerParams(has_side_effects=EFFECT),
    )(*[_in_hbm(s) for s in srcs], *[_in_hbm(l) for l in lands], after)
    return tuple(res[:3]), list(res[3:3 + n]), list(res[3 + n:3 + 2 * n]), res[-1]


def exchange_wait(sems, srcs, lands, scatter, after, name):
    n = len(srcs)

    def body(*refs):
        src, land = refs[:n], refs[n:2 * n]
        ssem, rsem, lsem = refs[2 * n:2 * n + 3]
        x, y, c = _axes()
        me = 4 * x + 2 * y + c
        for a in range(n):
            for k, (peer, p) in enumerate(_all_peers(x, y, c)):
                cp = pltpu.make_async_remote_copy(src_ref=src[a].at[p] if scatter[a] else src[a],
                                                  dst_ref=land[a].at[p], send_sem=ssem.at[7 * a + k],
                                                  recv_sem=rsem.at[7 * a + k], device_id=peer, device_id_type=MESH)
                cp.wait_send()
                cp.wait_recv()
            pltpu.make_async_copy(src[a].at[me] if scatter[a] else src[a], land[a].at[me], lsem.at[a]).wait()

    res = pl.pallas_call(
        body, name=name,
        out_shape=[pltpu.HBM(s.shape, s.dtype) for s in srcs] + [pltpu.HBM(l.shape, l.dtype) for l in lands],
        in_specs=[HBM] * (2 * n) + [SEM, SEM, SEM, ANY], out_specs=[HBM] * (2 * n),
        input_output_aliases={a: a for a in range(2 * n)},
        compiler_params=pltpu.CompilerParams(has_side_effects=EFFECT),
    )(*srcs, *lands, *sems, after)
    return list(res[n:])


def sum8(parts, name):
    _, P, C = parts.shape

    def body(p_ref, o_ref):
        tot = p_ref[0]
        for d in range(1, N_DEV):
            tot = tot + p_ref[d]
        o_ref[...] = tot

    return pl.pallas_call(body, name=name, out_shape=jax.ShapeDtypeStruct((P, C), F32),
                          compiler_params=pltpu.CompilerParams(vmem_limit_bytes=VMEM_LIMIT))(parts)


def mm_nn(a, w, out_dtype, name, res=None, gate=None):
    M, K = a.shape
    S, _, Ns = w.shape
    fused = res is not None
    TM = _tile(M, ((1024, 512, 256) if fused else (2048, 1024, 512, 256)) if K <= 1024 else (512, 256))
    TN = _tile(Ns, (1408, 1024, 768, 512, 256, 128))
    nj = Ns // TN

    def body(*refs):
        if fused:
            a_ref, w_ref, r_ref, g_ref, f_ref, o_ref = refs
        else:
            a_ref, w_ref, f_ref = refs
        f = jnp.dot(a_ref[...], w_ref[...], preferred_element_type=F32)
        f_ref[...] = f.astype(f_ref.dtype)
        if fused:
            o_ref[...] = r_ref[...] + g_ref[...] * f

    col = lambda s, j, i: (i, s * nj + j)
    in_specs = [pl.BlockSpec((TM, K), lambda s, j, i: (i, 0)), pl.BlockSpec((None, K, TN), lambda s, j, i: (s, 0, j))]
    out_specs = [pl.BlockSpec((TM, TN), col)]
    out_shape = [jax.ShapeDtypeStruct((M, S * Ns), out_dtype)]
    args = [a, w]
    if fused:
        in_specs += [pl.BlockSpec((TM, TN), col), pl.BlockSpec((1, TN), lambda s, j, i: (0, s * nj + j))]
        out_specs.append(pl.BlockSpec((TM, TN), col))
        out_shape.append(jax.ShapeDtypeStruct((M, S * Ns), F32))
        args += [res, gate]
    out = pl.pallas_call(body, name=name, grid=(S, nj, M // TM), in_specs=in_specs, out_specs=out_specs,
                         out_shape=out_shape, compiler_params=_cp("parallel", "parallel", "parallel"))(*args)
    return tuple(out) if fused else out[0]


def mm_nt(g, w, out_dtype, name):
    g3 = g if g.ndim == 3 else g[None]
    Q, M, F = g3.shape
    S, K, Ns = w.shape
    TM = _tile(M, (2048, 1024, 512, 256) if K <= 1024 else (512, 256))
    TN = _tile(Ns, (1408, 1024, 768, 512, 256, 128))
    nj = Ns // TN
    nred = S * nj
    per_part = F // TN

    def body(g_ref, w_ref, o_ref, acc):
        n = pl.program_id(1)

        @pl.when(n == 0)
        def _():
            acc[...] = jnp.zeros_like(acc)

        acc[...] += lax.dot_general(g_ref[...], w_ref[...], (((1,), (1,)), ((), ())), preferred_element_type=F32)

        @pl.when(n == nred - 1)
        def _():
            o_ref[...] = acc[...].astype(o_ref.dtype)

    return pl.pallas_call(
        body, name=name, grid=(M // TM, nred),
        in_specs=[pl.BlockSpec((None, TM, TN), lambda i, n: (n // per_part, i, n % per_part)),
                  pl.BlockSpec((None, K, TN), lambda i, n: (n // nj, 0, n % nj))],
        out_specs=pl.BlockSpec((TM, K), lambda i, n: (i, 0)),
        out_shape=jax.ShapeDtypeStruct((M, K), out_dtype),
        scratch_shapes=[pltpu.VMEM((TM, K), F32)],
        compiler_params=_cp("parallel", "arbitrary"))(g3, w)


def mm_tn(a, g, S, name):
    M, K = a.shape
    g3 = g if g.ndim == 3 else g[None]
    Q, _, F = g3.shape
    Ns = Q * F // S
    TN = _tile(Ns, (1408, 1024, 768, 512, 256, 128))
    TK = next(t for t in (1024, 512, 256, 128) if t <= K and K % t == 0
              and 4 * M * (t + TN) + 8 * t * TN <= VMEM_LIMIT * 3 // 4)
    nj = Ns // TN
    per_part = F // TN

    def body(a_ref, g_ref, o_ref):
        o_ref[...] = lax.dot_general(a_ref[...], g_ref[...], (((0,), (0,)), ((), ())),
                                     preferred_element_type=F32).astype(o_ref.dtype)

    return pl.pallas_call(
        body, name=name, grid=(S * nj, K // TK),
        in_specs=[pl.BlockSpec((M, TK), lambda n, k: (0, k)),
                  pl.BlockSpec((None, M, TN), lambda n, k: (n // per_part, 0, n % per_part))],
        out_specs=pl.BlockSpec((None, TK, TN), lambda n, k: (n // nj, k, n % nj)),
        out_shape=jax.ShapeDtypeStruct((S, K, Ns), BF),
        compiler_params=_cp("parallel", "parallel"))(a, g3)


ROW_TILE = (512, 256)


def _rows(TL, D):
    return pl.BlockSpec((TL, D), lambda i: (i, 0))


def _fixed(R, D):
    return pl.BlockSpec((R, D), lambda i: (0, 0))


def _rowsum8(v):
    T, D = v.shape
    return jnp.sum(v.reshape(T // 8, 8, D), axis=0)


def _norm_parts(xv):
    r = lax.rsqrt(jnp.mean(xv * xv, axis=-1, keepdims=True) + RMS_EPS)
    return xv * r, r


def norm_mod(x, gamma, mods, k_shift, out_dtype, name):
    L, D = x.shape
    TL = _tile(L, (256,))
    n = L // TL
    NIN, NOUT = min(3, n), 2

    def body(x_hbm, g_ref, m_ref, o_hbm, xbuf, obuf, isem, osem):
        def fetch(i, slot):
            return pltpu.make_async_copy(x_hbm.at[pl.ds(pl.multiple_of(i * TL, TL), TL)], xbuf.at[slot], isem.at[slot])

        def put(i, slot):
            return pltpu.make_async_copy(obuf.at[slot], o_hbm.at[pl.ds(pl.multiple_of(i * TL, TL), TL)], osem.at[slot])

        for s in range(NIN):
            fetch(s, s).start()
        gam = g_ref[...]
        sh, sc = m_ref[k_shift:k_shift + 1, :], m_ref[k_shift + 1:k_shift + 2, :]

        def step(i, carry):
            slot, oslot = lax.rem(i, NIN), lax.rem(i, NOUT)
            fetch(i, slot).wait()

            @pl.when(i >= NOUT)
            def _():
                put(i - NOUT, oslot).wait()

            xn, _ = _norm_parts(xbuf[slot])
            obuf[oslot] = ((xn * gam) * (1.0 + sc) + sh).astype(obuf.dtype)
            put(i, oslot).start()

            @pl.when(i + NIN < n)
            def _():
                fetch(i + NIN, slot).start()

            return carry

        lax.fori_loop(0, n, step, 0)
        for i in range(max(n - NOUT, 0), n):
            put(i, i % NOUT).wait()

    whole = pl.BlockSpec(memory_space=pltpu.VMEM)
    return pl.pallas_call(
        body, name=name, in_specs=[ANY, whole, whole], out_specs=ANY,
        out_shape=jax.ShapeDtypeStruct((L, D), out_dtype),
        scratch_shapes=[pltpu.VMEM((NIN, TL, D), F32), pltpu.VMEM((NOUT, TL, D), out_dtype),
                        pltpu.SemaphoreType.DMA((NIN,)), pltpu.SemaphoreType.DMA((NOUT,))],
        compiler_params=pltpu.CompilerParams(vmem_limit_bytes=VMEM_LIMIT))(x, gamma, mods)


def norm_bwd(dh, x, dres, gamma, mods, k_shift, name, branch=None):
    L, D = x.shape
    TL = _tile(L, ROW_TILE)
    nacc = 4 if branch else 3

    def body(*refs):
        if branch:
            dh_ref, x_ref, dr_ref, g_ref, m_ref, f_ref, fm_ref, dx_ref, s_ref, df_ref, acc = refs
        else:
            dh_ref, x_ref, dr_ref, g_ref, m_ref, dx_ref, s_ref, acc = refs
        i = pl.program_id(0)

        @pl.when(i == 0)
        def _():
            acc[...] = jnp.zeros_like(acc)

        xn, r = _norm_parts(x_ref[...])
        dh_v = dh_ref[...].astype(F32)
        gam = g_ref[...]
        sc = m_ref[k_shift + 1:k_shift + 2, :]
        dn = dh_v * (1.0 + sc)
        dxn = dn * gam
        dx = dr_ref[...] + r * (dxn - xn * jnp.mean(dxn * xn, axis=-1, keepdims=True))
        dx_ref[...] = dx
        acc[0] += _rowsum8(dh_v)
        acc[1] += _rowsum8(dh_v * (xn * gam))
        acc[2] += _rowsum8(dn * xn)
        if branch:
            df_ref[...] = (dx * fm_ref[branch[2]:branch[2] + 1, :]).astype(df_ref.dtype)
            acc[3] += _rowsum8(dx * f_ref[...].astype(F32))

        @pl.when(i == pl.num_programs(0) - 1)
        def _():
            s_ref[...] = jnp.zeros_like(s_ref)
            for q in range(nacc):
                s_ref[q:q + 1, :] = jnp.sum(acc[q], axis=0, keepdims=True)

    in_specs = [_rows(TL, D), _rows(TL, D), _rows(TL, D), _fixed(1, D), _fixed(6, D)]
    out_specs = [_rows(TL, D), _fixed(8, D)]
    out_shape = [jax.ShapeDtypeStruct((L, D), F32), jax.ShapeDtypeStruct((8, D), F32)]
    args = [dh, x, dres, gamma, mods]
    if branch:
        in_specs += [_rows(TL, D), _fixed(6, D)]
        out_specs.append(_rows(TL, D))
        out_shape.append(jax.ShapeDtypeStruct((L, D), BF))
        args += [branch[0], branch[1]]
    return pl.pallas_call(
        body, name=name, grid=(L // TL,), in_specs=in_specs, out_specs=out_specs, out_shape=out_shape,
        scratch_shapes=[pltpu.VMEM((nacc, 8, D), F32)], compiler_params=_cp("arbitrary"))(*args)


def ffn_in_act(a, w, name):
    M, K = a.shape
    S, _, Ns = w.shape
    half = S // 2
    TM = _tile(M, (1024, 512, 256))
    TN = _tile(Ns, (1408, 1024, 768, 512, 256, 128))
    nj = Ns // TN

    def body(a_ref, wg_ref, wu_ref, gu_ref, act_ref):
        av = a_ref[...]
        g = jnp.dot(av, wg_ref[...], preferred_element_type=F32)
        u = jnp.dot(av, wu_ref[...], preferred_element_type=F32)
        gu_ref[0] = g.astype(gu_ref.dtype)
        gu_ref[1] = u.astype(gu_ref.dtype)
        act_ref[...] = (g * _sigmoid(g) * u).astype(act_ref.dtype)

    return pl.pallas_call(
        body, name=name, grid=(half, nj, M // TM),
        in_specs=[pl.BlockSpec((TM, K), lambda s, j, i: (i, 0)),
                  pl.BlockSpec((None, K, TN), lambda s, j, i: (s, 0, j)),
                  pl.BlockSpec((None, K, TN), lambda s, j, i: (s + half, 0, j))],
        out_specs=[pl.BlockSpec((2, TM, TN), lambda s, j, i: (0, i, s * nj + j)),
                   pl.BlockSpec((TM, TN), lambda s, j, i: (i, s * nj + j))],
        out_shape=[jax.ShapeDtypeStruct((2, M, half * Ns), BF), jax.ShapeDtypeStruct((M, half * Ns), BF)],
        compiler_params=_cp("parallel", "parallel", "parallel"))(a, w, w)


def ffn_out_bwd(dff, w2, gu, name):
    M, D = dff.shape
    F = w2.shape[0]
    TM = _tile(M, (512, 256))
    CW = _tile(F, (256, 128))

    def body(d_ref, w_ref, gu_ref, o_ref):
        dv = d_ref[...]

        def product(c):
            return lax.dot_general(dv, w_ref[c:c + CW, :], (((1,), (1,)), ((), ())), preferred_element_type=F32)

        da = product(0)
        for c in range(0, F, CW):
            ahead = product(c + CW) if c + CW < F else None
            g = gu_ref[0, :, c:c + CW].astype(F32)
            u = gu_ref[1, :, c:c + CW].astype(F32)
            s = _sigmoid(g)
            o_ref[0, :, c:c + CW] = (da * u * (s + g * s * (1.0 - s))).astype(o_ref.dtype)
            o_ref[1, :, c:c + CW] = (da * g * s).astype(o_ref.dtype)
            da = ahead

    part = pl.BlockSpec((2, TM, F), lambda i: (0, i, 0))
    return pl.pallas_call(
        body, name=name, grid=(M // TM,),
        in_specs=[pl.BlockSpec((TM, D), lambda i: (i, 0)), pl.BlockSpec((F, D), lambda i: (0, 0)), part],
        out_specs=part, out_shape=jax.ShapeDtypeStruct((2, M, F), BF),
        compiler_params=_cp("parallel"))(dff, w2, gu)


def ssm_out_glu(z, w, x, mods, k_gate, name):
    M, K = z.shape
    S, _, Ns = w.shape
    half = S // 2
    TM = _tile(M, (1024, 512, 256))
    TN = _tile(Ns, (512, 256, 128))
    nj = Ns // TN

    def body(z_ref, wv_ref, wg_ref, x_ref, m_ref, o_ref, mix_ref, y_ref):
        zv = z_ref[...]
        CW = _tile(TN, (256, 128))

        def products(c):
            return (jnp.dot(zv, wv_ref[:, c:c + CW], preferred_element_type=F32),
                    jnp.dot(zv, wg_ref[:, c:c + CW], preferred_element_type=F32))

        cur = products(0)
        for c in range(0, TN, CW):
            ahead = products(c + CW) if c + CW < TN else None
            val, gate = cur
            o_ref[0, :, c:c + CW] = val.astype(o_ref.dtype)
            o_ref[1, :, c:c + CW] = gate.astype(o_ref.dtype)
            mix = val * _sigmoid(gate)
            mix_ref[:, c:c + CW] = mix.astype(mix_ref.dtype)
            y_ref[:, c:c + CW] = x_ref[:, c:c + CW] + m_ref[k_gate:k_gate + 1, c:c + CW] * mix
            cur = ahead

    col = lambda s, j, i: (i, s * nj + j)
    return pl.pallas_call(
        body, name=name, grid=(half, nj, M // TM),
        in_specs=[pl.BlockSpec((TM, K), lambda s, j, i: (i, 0)),
                  pl.BlockSpec((None, K, TN), lambda s, j, i: (s, 0, j)),
                  pl.BlockSpec((None, K, TN), lambda s, j, i: (s + half, 0, j)),
                  pl.BlockSpec((TM, TN), col), pl.BlockSpec((6, TN), lambda s, j, i: (0, s * nj + j))],
        out_specs=[pl.BlockSpec((2, TM, TN), lambda s, j, i: (0, i, s * nj + j)), pl.BlockSpec((TM, TN), col),
                   pl.BlockSpec((TM, TN), col)],
        out_shape=[jax.ShapeDtypeStruct((2, M, half * Ns), BF), jax.ShapeDtypeStruct((M, half * Ns), BF),
                   jax.ShapeDtypeStruct((M, half * Ns), F32)],
        compiler_params=_cp("parallel", "parallel", "parallel"))(z, w, w, x, mods)


def glu_bwd(dmix, o, name):
    _, L, D = o.shape
    TL = _tile(L, ROW_TILE)

    def body(d_ref, o_ref, do_ref):
        d = d_ref[...].astype(F32)
        val = o_ref[0].astype(F32)
        s = _sigmoid(o_ref[1].astype(F32))
        do_ref[0] = (d * s).astype(do_ref.dtype)
        do_ref[1] = (d * val * s * (1.0 - s)).astype(do_ref.dtype)

    part = pl.BlockSpec((2, TL, D), lambda i: (0, i, 0))
    return pl.pallas_call(body, name=name, grid=(L // TL,), in_specs=[_rows(TL, D), part],
                          out_specs=part, out_shape=jax.ShapeDtypeStruct((2, L, D), BF),
                          compiler_params=_cp("parallel"))(dmix, o)


def final_loss(x, target, gamma, f, fmods, k_gate, name):
    L, D = x.shape
    TL = _tile(L, ROW_TILE)

    def body(x_ref, t_ref, g_ref, f_ref, fm_ref, l_ref, dx_ref, s_ref, df_ref, acc, lacc):
        i = pl.program_id(0)

        @pl.when(i == 0)
        def _():
            acc[...] = jnp.zeros_like(acc)
            lacc[...] = jnp.zeros_like(lacc)

        xn, r = _norm_parts(x_ref[...])
        gam = g_ref[...]
        e = xn * gam - t_ref[...]
        lacc[...] += jnp.sum(0.5 * jnp.mean(e * e, axis=-1, keepdims=True), axis=0, keepdims=True)
        dy = e * (1.0 / D)
        dxn = dy * gam
        dx = r * (dxn - xn * jnp.mean(dxn * xn, axis=-1, keepdims=True))
        dx_ref[...] = dx
        df_ref[...] = (dx * fm_ref[k_gate:k_gate + 1, :]).astype(df_ref.dtype)
        acc[0] += _rowsum8(dy * xn)
        acc[1] += _rowsum8(dx * f_ref[...].astype(F32))

        @pl.when(i == pl.num_programs(0) - 1)
        def _():
            s_ref[...] = jnp.zeros_like(s_ref)
            for q in range(2):
                s_ref[q:q + 1, :] = jnp.sum(acc[q], axis=0, keepdims=True)
            l_ref[...] = jnp.broadcast_to(lacc[...], l_ref.shape)

    return pl.pallas_call(
        body, name=name, grid=(L // TL,),
        in_specs=[_rows(TL, D), _rows(TL, D), _fixed(1, D), _rows(TL, D), _fixed(6, D)],
        out_specs=[_fixed(8, 128), _rows(TL, D), _fixed(8, D), _rows(TL, D)],
        out_shape=[jax.ShapeDtypeStruct((8, 128), F32), jax.ShapeDtypeStruct((L, D), F32),
                   jax.ShapeDtypeStruct((8, D), F32), jax.ShapeDtypeStruct((L, D), BF)],
        scratch_shapes=[pltpu.VMEM((2, 8, D), F32), pltpu.VMEM((1, 1), F32)],
        compiler_params=_cp("arbitrary"))(x, target, gamma, f, fmods)


def _col(L, TC, off):
    return pl.BlockSpec((L, TC), lambda j: (0, off + j))


def _shift_down(v, k, row):
    return jnp.where(row >= k, pltpu.roll(v, k, 0), 0.0)


def _shift_up(v, k, row, L):
    return jnp.where(row < L - k, pltpu.roll(v, L - k, 0), 0.0)


def conv_fwd(p, w, name):
    L, D3 = p.shape
    D = D3 // 3
    TC = _tile(D, (128,))
    nc = D // TC

    def body(b_ref, c_ref, v_ref, w_ref, o_ref):
        row = lax.broadcasted_iota(jnp.int32, (L, TC), 0)
        cv = c_ref[...].astype(F32) * v_ref[...].astype(F32)
        conv = w_ref[2:3, :] * cv + w_ref[1:2, :] * _shift_down(cv, 1, row) + w_ref[0:1, :] * _shift_down(cv, 2, row)
        o_ref[...] = (b_ref[...].astype(F32) * conv).astype(o_ref.dtype)

    return pl.pallas_call(
        body, name=name, grid=(nc,),
        in_specs=[_col(L, TC, 0), _col(L, TC, nc), _col(L, TC, 2 * nc), pl.BlockSpec((3, TC), lambda j: (0, j))],
        out_specs=_col(L, TC, 0), out_shape=jax.ShapeDtypeStruct((L, D), BF), compiler_params=_cp("parallel"))(p, p, p, w)


def conv_bwd(dm, p, w, name):
    L, D3 = p.shape
    D = D3 // 3
    TC = _tile(D, (128,))
    nc = D // TC

    def body(dm_ref, b_ref, c_ref, v_ref, w_ref, db_ref, dc_ref, dv_ref, dw_ref):
        row = lax.broadcasted_iota(jnp.int32, (L, TC), 0)
        cg, vv = c_ref[...].astype(F32), v_ref[...].astype(F32)
        cv = cg * vv
        cv1, cv2 = _shift_down(cv, 1, row), _shift_down(cv, 2, row)
        conv = w_ref[2:3, :] * cv + w_ref[1:2, :] * cv1 + w_ref[0:1, :] * cv2
        dmv = dm_ref[...].astype(F32)
        db_ref[...] = (dmv * conv).astype(db_ref.dtype)
        dconv = dmv * b_ref[...].astype(F32)
        dcv = (w_ref[2:3, :] * dconv + w_ref[1:2, :] * _shift_up(dconv, 1, row, L)
               + w_ref[0:1, :] * _shift_up(dconv, 2, row, L))
        dc_ref[...] = (dcv * vv).astype(dc_ref.dtype)
        dv_ref[...] = (dcv * cg).astype(dv_ref.dtype)
        dw_ref[...] = jnp.zeros_like(dw_ref)
        dw_ref[0:1, :] = jnp.sum(dconv * cv2, axis=0, keepdims=True)
        dw_ref[1:2, :] = jnp.sum(dconv * cv1, axis=0, keepdims=True)
        dw_ref[2:3, :] = jnp.sum(dconv * cv, axis=0, keepdims=True)

    one = jax.ShapeDtypeStruct((L, D), BF)
    return pl.pallas_call(
        body, name=name, grid=(nc,),
        in_specs=[_col(L, TC, 0), _col(L, TC, 0), _col(L, TC, nc), _col(L, TC, 2 * nc),
                  pl.BlockSpec((3, TC), lambda j: (0, j))],
        out_specs=[_col(L, TC, 0), _col(L, TC, 0), _col(L, TC, 0), pl.BlockSpec((8, TC), lambda j: (0, j))],
        out_shape=[one, one, one, jax.ShapeDtypeStruct((8, D), F32)],
        compiler_params=_cp("parallel"))(dm, p, p, p, w)


def _gelu(y):
    return 0.5 * y * (1.0 + jnp.tanh(GELU_C * (y + GELU_A * y * y * y)))


def _gelu_grad(y):
    th = jnp.tanh(GELU_C * (y + GELU_A * y * y * y))
    return 0.5 * (1.0 + th) + 0.5 * y * (1.0 - th * th) * GELU_C * (1.0 + 3.0 * GELU_A * y * y)


def _cmul_add(br, bi, ar, ai, sr, si):
    return br + ar * sr - ai * si, bi + ar * si + ai * sr


def _log2(n):
    k = n.bit_length() - 1
    assert 1 << k == n
    return k


def _replicate(P2, W2, P, GLP, transposed):
    shape = (W2, P2) if transposed else (P2, W2)
    k = lax.broadcasted_iota(jnp.int32, shape, 1 if transposed else 0)
    c = lax.broadcasted_iota(jnp.int32, shape, 0 if transposed else 1)
    return ((k >> _log2(P)) == (c >> _log2(GLP))) & ((k & (P - 1)) == (c & (P - 1)))


def _on_diagonal(KB, W2, H, P, GLP, transposed):
    shape = (W2, KB) if transposed else (KB, W2)
    r = lax.broadcasted_iota(jnp.int32, shape, 1 if transposed else 0)
    c = lax.broadcasted_iota(jnp.int32, shape, 0 if transposed else 1)
    return (r >> _log2(H)) == ((c & (GLP - 1)) >> _log2(P))


def _expand(t, dims, transposed):
    KB, W2, H, P, GLP = dims
    rep = _replicate(2 * P, W2, P, GLP, transposed).astype(t.dtype)
    wide = jnp.dot(rep, t, preferred_element_type=F32) if transposed else jnp.dot(t, rep, preferred_element_type=F32)
    return jnp.where(_on_diagonal(KB, W2, H, P, GLP, transposed), wide, 0.0).astype(t.dtype)


def _extract(acc, dims):
    KB, W2, H, P, GLP = dims
    rep = _replicate(2 * P, W2, P, GLP, True).astype(BF)
    kept = jnp.where(_on_diagonal(KB, W2, H, P, GLP, False), acc, 0.0)
    hi = kept.astype(BF)
    lo = (kept - hi.astype(F32)).astype(BF)
    return jnp.dot(hi, rep, preferred_element_type=F32) + jnp.dot(lo, rep, preferred_element_type=F32)


def _cmul(ar, ai, sr, si):
    return ar * sr - ai * si, ar * si + ai * sr


def _chunk_order(TL, CH, transposed):
    out_row = lax.broadcasted_iota(jnp.int32, (TL, TL), 1 if transposed else 0)
    in_row = lax.broadcasted_iota(jnp.int32, (TL, TL), 0 if transposed else 1)
    return in_row == ((out_row & 7) << _log2(CH)) + (out_row >> 3)


def _reorder(perm, v):
    hi = v.astype(perm.dtype)
    lo = (v - hi.astype(F32)).astype(perm.dtype)
    return jnp.dot(perm, hi, preferred_element_type=F32) + jnp.dot(perm, lo, preferred_element_type=F32)


def _interleave(main, side):
    n, m, k = len(main), len(side), 0
    for i, step in enumerate(main):
        step()
        while k < m and (k + 1) * n <= (i + 1) * m:
            side[k]()
            k += 1
    for step in side[k:]:
        step()


S5_CHUNK = 512


def s5_fwd(h, tb, tct, pw, dvec, name):
    L, D = h.shape
    nkb, KB, P2 = tb.shape
    P = P2 // 2
    W = (KB // SSM_GROUP) * P
    W2 = 2 * W
    dims = (KB, W2, SSM_GROUP, P, W)
    TL = _tile(L, (512, 256))
    CH = TL // 8
    NB = 2 if nkb % 2 == 0 else 1
    CK = min(S5_CHUNK, W2)

    def body(h_ref, tb_ref, tct_ref, pw_ref, d_ref, s_ref, y_ref, z_ref, bw, cw, perm, unperm, carry):
        t = pl.program_id(1)

        @pl.when(t == 0)
        def _():
            carry[...] = jnp.zeros_like(carry)
            for b in range(NB):
                bw[b] = _expand(tb_ref[b], dims, False)
                cw[b] = _expand(tct_ref[b], dims, True)
            perm[...] = _chunk_order(TL, CH, False).astype(perm.dtype)
            unperm[...] = _chunk_order(TL, CH, True).astype(perm.dtype)

        hp = _reorder(perm[...], h_ref[...])
        hpb = hp.astype(BF)
        first = lax.broadcasted_iota(jnp.int32, (8, W), 0) == 0

        def project(b):
            def chunk(c):
                def emit():
                    s_ref[:, b * W2 + c:b * W2 + c + CK] = jnp.dot(hpb[:, b * KB:(b + 1) * KB], bw[b, :, c:c + CK],
                                                                   preferred_element_type=F32)
                return emit
            return [chunk(c) for c in range(0, W2, CK)]

        def scan(b):
            re, im = slice(b * W2, b * W2 + W), slice(b * W2 + W, (b + 1) * W2)
            ar, ai = pw_ref[b, 0:8, :W], pw_ref[b, 0:8, W:]
            st = {"x": (jnp.zeros((8, W), F32), jnp.zeros((8, W), F32))}

            def own(j):
                def emit():
                    rows = slice(j * 8, j * 8 + 8)
                    xr, xi = _cmul_add(s_ref[rows, re], s_ref[rows, im], ar, ai, *st["x"])
                    s_ref[rows, re] = xr
                    s_ref[rows, im] = xi
                    st["x"] = (xr, xi)
                return emit

            def ends():
                xr, xi = st["x"]
                for k, off in ((1, 8), (2, 16), (4, 24)):
                    xr, xi = _cmul_add(xr, xi, pw_ref[b, off:off + 8, :W], pw_ref[b, off:off + 8, W:],
                                       pltpu.roll(xr, k, 0), pltpu.roll(xi, k, 0))
                xr, xi = _cmul_add(xr, xi, pw_ref[b, 32:40, :W], pw_ref[b, 32:40, W:], carry[b, 0], carry[b, 1])
                st["c"] = (jnp.where(first, carry[b, 0], pltpu.roll(xr, 1, 0)),
                           jnp.where(first, carry[b, 1], pltpu.roll(xi, 1, 0)))
                carry[b, 0] = jnp.broadcast_to(xr[7:8], (8, W))
                carry[b, 1] = jnp.broadcast_to(xi[7:8], (8, W))

            def carried(j):
                def emit():
                    rows = slice(j * 8, j * 8 + 8)
                    cr, ci = _cmul(ar, ai, *st["c"])
                    s_ref[rows, re] = s_ref[rows, re] + cr
                    s_ref[rows, im] = s_ref[rows, im] + ci
                    st["c"] = (cr, ci)
                return emit

            return [own(j) for j in range(CH)] + [ends] + [carried(j) for j in range(CH)]

        def readout(b):
            cols = slice(b * KB, (b + 1) * KB)
            acc = {}

            def chunk(c):
                def emit():
                    part = jnp.dot(s_ref[:, b * W2 + c:b * W2 + c + CK].astype(BF), cw[b, c:c + CK, :],
                                   preferred_element_type=F32)
                    acc["y"] = part if c == 0 else acc["y"] + part
                return emit

            def finish():
                y = acc["y"] + d_ref[:, cols] * hp[:, cols]
                y_ref[:, cols] = y
                z_ref[:, cols] = jnp.dot(unperm[...], _gelu(y).astype(BF),
                                         preferred_element_type=F32).astype(z_ref.dtype)

            return [chunk(c) for c in range(0, W2, CK)] + [finish]

        for emit in project(0):
            emit()
        for b in range(NB):
            side = (project(b + 1) if b + 1 < NB else []) + (readout(b - 1) if b > 0 else [])
            _interleave(scan(b), side)
        for emit in readout(NB - 1):
            emit()

    blk = lambda kb, t: (t, kb)
    per_kb = lambda kb, t: (kb, 0, 0)
    return pl.pallas_call(
        body, name=name, grid=(nkb // NB, L // TL),
        in_specs=[pl.BlockSpec((TL, NB * KB), blk), pl.BlockSpec((NB, KB, P2), per_kb),
                  pl.BlockSpec((NB, P2, KB), per_kb), pl.BlockSpec((NB, 40, W2), per_kb),
                  pl.BlockSpec((1, NB * KB), lambda kb, t: (0, kb))],
        out_specs=[pl.BlockSpec((TL, NB * W2), blk), pl.BlockSpec((TL, NB * KB), blk),
                   pl.BlockSpec((TL, NB * KB), blk)],
        out_shape=[jax.ShapeDtypeStruct((L, nkb * W2), F32), jax.ShapeDtypeStruct((L, D), F32),
                   jax.ShapeDtypeStruct((L, D), BF)],
        scratch_shapes=[pltpu.VMEM((NB, KB, W2), BF), pltpu.VMEM((NB, W2, KB), BF), pltpu.VMEM((TL, TL), BF),
                        pltpu.VMEM((TL, TL), BF), pltpu.VMEM((NB, 2, 8, W), F32)],
        compiler_params=_cp("parallel", "arbitrary"))(h, tb, tct, pw, dvec)


def s5_bwd(dz, y, h, s, tc, tbt, pwr, dvec, name):
    L, D = h.shape
    nkb, KB, P2 = tc.shape
    P = P2 // 2
    W = (KB // SSM_GROUP) * P
    W2 = 2 * W
    dims = (KB, W2, SSM_GROUP, P, W)
    TL = _tile(L, (512, 256))
    CH = TL // 8
    nt = L // TL
    NB = 2 if nkb % 2 == 0 else 1
    CK = min(S5_CHUNK, W2)
    tn = (((0,), (0,)), ((), ()))

    def body(dz_ref, y_ref, h_ref, s_ref, sp_ref, tc_ref, tbt_ref, pw_ref, d_ref,
             dh_ref, dd_ref, da_ref, db_ref, dc_ref, g, ctw, btw, dbacc, dcacc, dys, perm, unperm, carry):
        t = pl.program_id(1)

        @pl.when(t == 0)
        def _():
            carry[...] = jnp.zeros_like(carry)
            dd_ref[...] = jnp.zeros_like(dd_ref)
            da_ref[...] = jnp.zeros_like(da_ref)
            dbacc[...] = jnp.zeros_like(dbacc)
            dcacc[...] = jnp.zeros_like(dcacc)
            for b in range(NB):
                ctw[b] = _expand(tc_ref[b], dims, False)
                btw[b] = _expand(tbt_ref[b], dims, True)
            perm[...] = _chunk_order(TL, CH, False).astype(perm.dtype)
            unperm[...] = _chunk_order(TL, CH, True).astype(perm.dtype)

        hp = jnp.dot(perm[...], h_ref[...].astype(BF), preferred_element_type=F32)
        dy = jnp.dot(perm[...], dz_ref[...].astype(BF), preferred_element_type=F32) * _gelu_grad(y_ref[...])
        dd_ref[...] += _rowsum8(dy * hp)
        dys[...] = dy
        dyb = dy.astype(BF)
        hpb = hp.astype(BF)
        sub = lax.broadcasted_iota(jnp.int32, (8, W), 0)
        live = jnp.where(t == nt - 1, 0.0, 1.0)

        def lead(b):
            cols = slice(b * KB, (b + 1) * KB)

            def to_states(c):
                def emit():
                    g[b, :, c:c + CK] = jnp.dot(dyb[:, cols], ctw[b, :, c:c + CK], preferred_element_type=F32)
                return emit

            def d_c(c):
                def emit():
                    dcacc[b, :, c:c + CK] += lax.dot_general(dyb[:, cols],
                                                             s_ref[:, b * W2 + c:b * W2 + c + CK].astype(BF), tn,
                                                             preferred_element_type=F32)
                return emit

            return [f(c) for c in range(0, W2, CK) for f in (to_states, d_c)]

        def scan(b):
            re, im = slice(b * W2, b * W2 + W), slice(b * W2 + W, (b + 1) * W2)
            ar, ai = pw_ref[b, 0:8, :W], pw_ref[b, 0:8, W:]
            zero = jnp.zeros((8, W), F32)
            st = {"g": (zero, zero), "acc": (zero, zero)}

            def own(j):
                def emit():
                    rows = slice(j * 8, j * 8 + 8)
                    gr, gi = _cmul_add(g[b, rows, :W], g[b, rows, W:], ar, ai, *st["g"])
                    g[b, rows, :W] = gr
                    g[b, rows, W:] = gi
                    st["g"] = (gr, gi)
                return emit

            def ends():
                gr, gi = st["g"]
                for k, off in ((1, 8), (2, 16), (4, 24)):
                    gr, gi = _cmul_add(gr, gi, pw_ref[b, off:off + 8, :W], pw_ref[b, off:off + 8, W:],
                                       pltpu.roll(gr, 8 - k, 0), pltpu.roll(gi, 8 - k, 0))
                gr, gi = _cmul_add(gr, gi, pw_ref[b, 32:40, :W], pw_ref[b, 32:40, W:], carry[b, 0], carry[b, 1])
                st["c"] = (jnp.where(sub == 7, carry[b, 0], pltpu.roll(gr, 7, 0)),
                           jnp.where(sub == 7, carry[b, 1], pltpu.roll(gi, 7, 0)))
                carry[b, 0] = jnp.broadcast_to(gr[0:1], (8, W))
                carry[b, 1] = jnp.broadcast_to(gi[0:1], (8, W))

            def carried(j):
                def emit():
                    rows = slice(j * 8, j * 8 + 8)
                    cr, ci = _cmul(ar, ai, *st["c"])
                    gr, gi = g[b, rows, :W] + cr, g[b, rows, W:] + ci
                    g[b, rows, :W] = gr
                    g[b, rows, W:] = gi
                    if j > 0:
                        before = slice(j * 8 - 8, j * 8)
                        pr, pi = s_ref[before, re], s_ref[before, im]
                    else:
                        last = slice(TL - 8, TL)
                        pr = jnp.where(sub == 0, sp_ref[7:8, re] * live, pltpu.roll(s_ref[last, re], 1, 0))
                        pi = jnp.where(sub == 0, sp_ref[7:8, im] * live, pltpu.roll(s_ref[last, im], 1, 0))
                    accr, acci = st["acc"]
                    st["c"] = (cr, ci)
                    st["acc"] = (accr + pr * gr + pi * gi, acci + pr * gi - pi * gr)
                return emit

            def done():
                da_ref[b, :, :W] += st["acc"][0]
                da_ref[b, :, W:] += st["acc"][1]

            return ([own(j) for j in reversed(range(CH))] + [ends] + [carried(j) for j in reversed(range(CH))]
                    + [done])

        def tail(b):
            cols = slice(b * KB, (b + 1) * KB)
            acc = {}

            def d_u(c):
                def emit():
                    part = jnp.dot(g[b, :, c:c + CK].astype(BF), btw[b, c:c + CK, :], preferred_element_type=F32)
                    acc["u"] = part if c == 0 else acc["u"] + part
                return emit

            def d_b(c):
                def emit():
                    dbacc[b, :, c:c + CK] += lax.dot_general(hpb[:, cols], g[b, :, c:c + CK].astype(BF), tn,
                                                             preferred_element_type=F32)
                return emit

            def finish():
                dh = (dys[:, cols] * d_ref[:, cols] + acc["u"]).astype(BF)
                dh_ref[:, cols] = jnp.dot(unperm[...], dh, preferred_element_type=F32).astype(dh_ref.dtype)

            return [f(c) for c in range(0, W2, CK) for f in (d_u, d_b)] + [finish]

        for emit in lead(0):
            emit()
        for b in range(NB):
            side = (lead(b + 1) if b + 1 < NB else []) + (tail(b - 1) if b > 0 else [])
            _interleave(scan(b), side)
        for emit in tail(NB - 1):
            emit()

        @pl.when(t == nt - 1)
        def _():
            for b in range(NB):
                db_ref[b] = _extract(dbacc[b], dims)
                dc_ref[b] = _extract(dcacc[b], dims)

    rev = lambda kb, t: (nt - 1 - t, kb)
    prev = lambda kb, t: (jnp.maximum((nt - 1 - t) * CH - 1, 0), kb)
    per_kb = lambda kb, t: (kb, 0, 0)
    return pl.pallas_call(
        body, name=name, grid=(nkb // NB, nt),
        in_specs=[pl.BlockSpec((TL, NB * KB), rev), pl.BlockSpec((TL, NB * KB), rev),
                  pl.BlockSpec((TL, NB * KB), rev), pl.BlockSpec((TL, NB * W2), rev),
                  pl.BlockSpec((8, NB * W2), prev), pl.BlockSpec((NB, KB, P2), per_kb),
                  pl.BlockSpec((NB, P2, KB), per_kb), pl.BlockSpec((NB, 40, W2), per_kb),
                  pl.BlockSpec((1, NB * KB), lambda kb, t: (0, kb))],
        out_specs=[pl.BlockSpec((TL, NB * KB), rev), pl.BlockSpec((8, NB * KB), lambda kb, t: (0, kb)),
                   pl.BlockSpec((NB, 8, W2), per_kb), pl.BlockSpec((NB, KB, P2), per_kb),
                   pl.BlockSpec((NB, KB, P2), per_kb)],
        out_shape=[jax.ShapeDtypeStruct((L, D), BF), jax.ShapeDtypeStruct((8, D), F32),
                   jax.ShapeDtypeStruct((nkb, 8, W2), F32), jax.ShapeDtypeStruct((nkb, KB, P2), F32),
                   jax.ShapeDtypeStruct((nkb, KB, P2), F32)],
        scratch_shapes=[pltpu.VMEM((NB, TL, W2), F32), pltpu.VMEM((NB, KB, W2), BF), pltpu.VMEM((NB, W2, KB), BF),
                        pltpu.VMEM((NB, KB, W2), F32), pltpu.VMEM((NB, KB, W2), F32), pltpu.VMEM((TL, NB * KB), F32),
                        pltpu.VMEM((TL, TL), BF), pltpu.VMEM((TL, TL), BF), pltpu.VMEM((NB, 2, 8, W), F32)],
        compiler_params=pltpu.CompilerParams(dimension_semantics=("parallel", "arbitrary"),
                                             vmem_limit_bytes=V7X_VMEM_BYTES - 4 * 1024 * 1024),
    )(dz, y, h, s, s, tc, tbt, pwr, dvec)


def _discretise(a_re, a_im, log_step, b_re, b_im):
    lr = jnp.minimum(a_re, -1e-4)
    li = a_im
    dt = jnp.exp(log_step)[:, None]
    mag = jnp.exp(lr * dt)
    abr = mag * jnp.cos(li * dt)
    abi = mag * jnp.sin(li * dt)
    den = lr * lr + li * li
    qr = ((abr - 1.0) * lr + abi * li) / den
    qi = (abi * lr - (abr - 1.0) * li) / den
    bbar_re = qr[..., None] * b_re - qi[..., None] * b_im
    bbar_im = qr[..., None] * b_im + qi[..., None] * b_re
    return abr, abi, bbar_re, bbar_im


def _compact(m_re, m_im, nkb):
    G, H, P = m_re.shape
    t = jnp.stack([m_re, m_im], axis=2).reshape(nkb, (G // nkb) * H, 2 * P).astype(BF)
    return t, jnp.swapaxes(t, 1, 2)


def _scan_powers(abr, abi, nkb, conj, CH):
    G, P = abr.shape
    if conj:
        abi = -abi

    def cmul(u, v):
        return u[0] * v[0] - u[1] * v[1], u[0] * v[1] + u[1] * v[0]

    q = (abr, abi)
    for _ in range(_log2(CH)):
        q = cmul(q, q)
    pows = [q]
    for _ in range(7):
        pows.append(cmul(pows[-1], q))
    row = jnp.arange(8)[:, None, None]

    def table(part):
        out = [jnp.broadcast_to((abr, abi)[part][None], (8, G, P))]
        for k in (1, 2, 4):
            keep = (row <= 7 - k) if conj else (row >= k)
            out.append(jnp.where(keep, pows[k - 1][part][None], 0.0))
        ends = jnp.stack([p[part] for p in pows])
        out.append(ends[::-1] if conj else ends)
        return jnp.concatenate(out, axis=0)

    GL = G // nkb
    t = jnp.stack([table(0), table(1)], axis=1)
    t = t.reshape(40, 2, nkb, GL * P).transpose(2, 0, 1, 3)
    return t.reshape(nkb, 40, 2 * GL * P)


def ada_mods(c_all, w_ada, b_sh, name):
    nl, D, NA = w_ada.shape

    def body(c_ref, w_ref, b_ref, o_ref):
        cv = c_ref[...]
        act = cv * _sigmoid(cv)
        o_ref[...] = jnp.dot(act, w_ref[...], preferred_element_type=F32, precision=lax.Precision.HIGHEST) + b_ref[...]

    return pl.pallas_call(
        body, name=name, grid=(nl,),
        in_specs=[pl.BlockSpec((8, D), lambda i: (0, 0)), pl.BlockSpec((None, D, NA), lambda i: (i, 0, 0)),
                  pl.BlockSpec((None, 1, NA), lambda i: (i, 0, 0))],
        out_specs=pl.BlockSpec((None, 8, NA), lambda i: (i, 0, 0)),
        out_shape=jax.ShapeDtypeStruct((nl, 8, NA), F32), compiler_params=_cp("parallel"))(c_all, w_ada, b_sh)


def _adamw(w, g, m, v):
    m = ADAM_B1 * m + (1.0 - ADAM_B1) * g
    v = ADAM_B2 * v + (1.0 - ADAM_B2) * (g * g)
    m_hat = m / (1.0 - ADAM_B1 ** ADAM_STEP)
    v_hat = v / (1.0 - ADAM_B2 ** ADAM_STEP)
    return -ADAM_LR * (m_hat / (jnp.sqrt(v_hat) + ADAM_EPS) + ADAM_WD * w), m, v


def _adam_rows(R, C):
    cap = max(8, (256 * 1024) // C)
    for t in range(min(R, cap), 0, -1):
        if R % t == 0 and (t % 8 == 0 or t == R):
            return t
    return R


def adamw_ada(c_t, dm, w, m, v, name):
    nl, D, NA = w.shape
    TK = _tile(D, (256, 128))

    def body(c_ref, dm_ref, w_ref, m_ref, v_ref, g_ref, d_ref, nm_ref, nv_ref):
        cv = c_ref[...]
        act = cv * _sigmoid(cv)
        g = jnp.dot(act, dm_ref[...], preferred_element_type=F32, precision=lax.Precision.HIGHEST)
        g_ref[...] = g
        d_ref[...], nm_ref[...], nv_ref[...] = _adamw(w_ref[...], g, m_ref[...], v_ref[...])

    big = pl.BlockSpec((None, TK, NA), lambda i, k: (i, k, 0))
    shape = jax.ShapeDtypeStruct(w.shape, F32)
    return pl.pallas_call(
        body, name=name, grid=(nl, D // TK),
        in_specs=[pl.BlockSpec((TK, 8), lambda i, k: (k, 0)), pl.BlockSpec((None, 8, NA), lambda i, k: (i, 0, 0)),
                  big, big, big],
        out_specs=[big] * 4, out_shape=[shape] * 4, compiler_params=_cp("parallel", "parallel"))(c_t, dm, w, m, v)


def adamw_sharded(w, m, v, ga, gb, name):
    nl, R, C = w.shape
    TR = _adam_rows(R, C)

    def body(w_ref, m_ref, v_ref, a_ref, b_ref, g_ref, d_ref, nm_ref, nv_ref):
        g = a_ref[...] + b_ref[...]
        g_ref[...] = g
        d_ref[...], nm_ref[...], nv_ref[...] = _adamw(w_ref[...], g, m_ref[...], v_ref[...])

    big = pl.BlockSpec((None, TR, C), lambda i, r: (i, r, 0))
    shape = jax.ShapeDtypeStruct(w.shape, F32)
    return pl.pallas_call(
        body, name=name, grid=(nl, R // TR), in_specs=[big] * 5,
        out_specs=[big] * 4, out_shape=[shape] * 4, compiler_params=_cp("parallel", "parallel"))(w, m, v, ga, gb)


def adamw_slab(g, w, m, v, name):
    R, C = g.shape
    TR = _tile(R, (160, 80, 40, 8))

    def body(g_ref, w_ref, m_ref, v_ref, d_ref, nm_ref, nv_ref):
        d_ref[...], nm_ref[...], nv_ref[...] = _adamw(w_ref[...], g_ref[...], m_ref[...], v_ref[...])

    big = pl.BlockSpec((TR, C), lambda r: (r, 0))
    shape = jax.ShapeDtypeStruct((R, C), F32)
    return pl.pallas_call(
        body, name=name, grid=(R // TR,), in_specs=[big] * 4,
        out_specs=[big] * 3, out_shape=[shape] * 3, compiler_params=_cp("parallel"))(g, w, m, v)


def adamw_plain(w, m, v, g, name):
    def body(w_ref, m_ref, v_ref, g_ref, d_ref, nm_ref, nv_ref):
        d_ref[...], nm_ref[...], nv_ref[...] = _adamw(w_ref[...], g_ref[...], m_ref[...], v_ref[...])

    shape = jax.ShapeDtypeStruct(w.shape, F32)
    return pl.pallas_call(body, name=name, out_shape=[shape] * 3,
                          compiler_params=pltpu.CompilerParams(vmem_limit_bytes=VMEM_LIMIT))(w, m, v, g)


def _slab_rows(a):
    n = a.size
    rows = -(-n // SLAB_W)
    return -(-rows // 8) * 8


def _pack(arrs, pad_rows_to=0):
    out = []
    for a in arrs:
        rows = _slab_rows(a)
        flat = a.reshape(-1).astype(F32)
        flat = jnp.pad(flat, (0, rows * SLAB_W - flat.shape[0]))
        out.append(flat.reshape(rows, SLAB_W))
    total = sum(o.shape[0] for o in out)
    if pad_rows_to and total % pad_rows_to:
        out.append(jnp.zeros((pad_rows_to - total % pad_rows_to, SLAB_W), F32))
    return jnp.concatenate(out, axis=0)


def _unpack(slab, like):
    out, r = [], 0
    for a in like:
        rows = _slab_rows(a)
        out.append(slab[r:r + rows].reshape(-1)[:a.size].reshape(a.shape))
        r += rows
    return out


WEIGHTS = ['norm1_g', 'norm2_g', 'w_ada', 'b_ada', 'ssm_a_re', 'ssm_a_im', 'ssm_log_step', 'ssm_b_re', 'ssm_b_im',
           'ssm_c_re', 'ssm_c_im', 'ssm_d', 'ssm_w_out', 'conv_w_in', 'conv_w', 'conv_w_out', 'w_ffn_in',
           'w_ffn_out', 'final_g']
SLAB = ['norm1_g', 'norm2_g', 'b_ada', 'ssm_a_re', 'ssm_a_im', 'ssm_log_step', 'ssm_b_re', 'ssm_b_im', 'ssm_c_re',
        'ssm_c_im', 'ssm_d', 'final_g']
SHARDED = ['ssm_w_out', 'conv_w_in', 'conv_w_out', 'w_ffn_in', 'w_ffn_out']


def kernel(x, c, norm1_g, norm2_g, w_ada, b_ada, ssm_a_re, ssm_a_im, ssm_log_step, ssm_b_re, ssm_b_im, ssm_c_re, ssm_c_im, ssm_d, ssm_w_out, conv_w_in, conv_w, conv_w_out, w_ffn_in, w_ffn_out, final_g, loss_target, m_norm1_g, m_norm2_g, m_w_ada, m_b_ada, m_ssm_a_re, m_ssm_a_im, m_ssm_log_step, m_ssm_b_re, m_ssm_b_im, m_ssm_c_re, m_ssm_c_im, m_ssm_d, m_ssm_w_out, m_conv_w_in, m_conv_w, m_conv_w_out, m_w_ffn_in, m_w_ffn_out, m_final_g, v_norm1_g, v_norm2_g, v_w_ada, v_b_ada, v_ssm_a_re, v_ssm_a_im, v_ssm_log_step, v_ssm_b_re, v_ssm_b_im, v_ssm_c_re, v_ssm_c_im, v_ssm_d, v_ssm_w_out, v_conv_w_in, v_conv_w, v_conv_w_out, v_w_ffn_in, v_w_ffn_out, v_final_g):
    given = dict(locals())
    W = {n: given[n] for n in WEIGHTS}
    Mo = {n: given["m_" + n] for n in WEIGHTS}
    Vo = {n: given["v_" + n] for n in WEIGHTS}

    xs = x[0]
    tgt = loss_target[0]
    L, D = xs.shape
    nlayer = norm1_g.shape[0]
    NA = w_ada.shape[2]
    G = ssm_a_re.shape[1]
    nkb = D // S5_BLOCK
    ax, ay, ac = _axes()
    me = 4 * ax + 2 * ay + ac
    chip = 2 * ax + ay

    assert D == SLAB_W
    first = gather8(jnp.concatenate([jnp.broadcast_to(c, (8, D)), _pack([conv_w])], axis=0), "gather_c_conv_w")
    c_all = first[:, 0, :]
    b_sh = lax.dynamic_slice_in_dim(b_ada, chip * NA, NA, axis=1)[:, None, :]
    mods_part = ada_mods(c_all, w_ada, b_sh, "ada_mods")
    mg = gather8(mods_part.reshape(nlayer * 8, NA), "gather_mods")
    mg = mg.reshape(N_CHIP, 2, nlayer, 8, NA)[:, 0]
    mods_all = lax.dynamic_index_in_dim(mg, me, axis=2, keepdims=False)
    mods_all = jnp.transpose(mods_all, (1, 0, 2)).reshape(nlayer, 6, D)

    cw_parts = first[:, 8:]
    nconv = conv_w.shape[0]
    cw_full = jnp.stack([_unpack(cw_parts[2 * q], [conv_w])[0] for q in range(N_CHIP)], axis=2)
    cw_full = cw_full.reshape(nconv, 3, D)

    in_flight_w = {}

    def start_weights(i, after):
        names = (["ssm_w_out"] if i % 2 == 0 else ["conv_w_in", "conv_w_out"]) + ["w_ffn_in", "w_ffn_out"]
        shards = [W[n][i if n.startswith("w_ffn") else i // 2].astype(BF) for n in names]
        sems, srcs, lands, tok = gather_start(shards, after, "gather_start%d" % i)
        in_flight_w[i] = (names, sems, srcs, lands)
        return tok

    def relay_weights(i, after):
        names, sems, srcs, lands = in_flight_w[i]
        got = gather_wait(sems, srcs, lands, list(range(len(names))), after, "gather_wait%d" % i)
        rsems, rlands, tok = relay_start(got, after, "relay_start%d" % i)
        in_flight_w[i] = (names, rsems, rlands)
        return tok

    def layer_weights(i, after):
        names, rsems, rlands = in_flight_w[i]
        return dict(zip(names, relay_wait(rsems, rlands, after, "relay_wait%d" % i)))

    token = start_weights(0, cw_full + mods_all[0, 0:3])
    mods_all = mods_all + token[0:1, 0:1]

    s5 = []
    for j in range(ssm_a_re.shape[0]):
        disc, disc_vjp = jax.vjp(_discretise, ssm_a_re[j], ssm_a_im[j], ssm_log_step[j], ssm_b_re[j], ssm_b_im[j])
        abr, abi, bbar_re, bbar_im = disc
        tb, tbt = _compact(jnp.swapaxes(bbar_re, 1, 2), jnp.swapaxes(bbar_im, 1, 2), nkb)
        tc, tct = _compact(ssm_c_re[j], -ssm_c_im[j], nkb)
        chunk = _tile(L, (512, 256)) // 8
        s5.append(dict(vjp=disc_vjp, tb=tb, tbt=tbt, tc=tc, tct=tct, pw=_scan_powers(abr, abi, nkb, False, chunk),
                       pwr=_scan_powers(abr, abi, nkb, True, chunk)))

    saved = []
    xcur = xs
    for i in range(nlayer):
        j = i // 2
        mods = mods_all[i]
        sv = dict(x=xcur)
        if i % 2 == 0:
            h = norm_mod(xcur, norm1_g[i:i + 1], mods, 0, F32, "norm_mod_s5")
            dvec = ssm_d[j:j + 1]
            if i == 0:
                dvec = dvec + start_weights(1, h)[0:1, 0:1]
            states, yv, z = s5_fwd(h, s5[j]["tb"], s5[j]["tct"], s5[j]["pw"], dvec, "s5_fwd")
            if i == 0:
                mods = mods + relay_weights(0, z)[0:1, 0:1]
            full = layer_weights(i, z)
            o, mix, x2 = ssm_out_glu(z, full["ssm_w_out"], xcur, mods, 2, "ssm_out_glu")
            sv.update(h=h, states=states, y=yv, z=z, o=o)
        else:
            h = norm_mod(xcur, norm1_g[i:i + 1], mods, 0, BF, "norm_mod")
            full = layer_weights(i, h)
            p = mm_nn(h, full["conv_w_in"], BF, "mm_conv_in")
            mc = conv_fwd(p, cw_full[j], "conv_fwd")
            mix, x2 = mm_nn(mc, full["conv_w_out"].reshape(1, D, D), BF, "mm_conv_out", res=xcur, gate=mods[2:3])
            sv.update(h=h, p=p, mc=mc)
        h2 = norm_mod(x2, norm2_g[i:i + 1], mods, 3, BF, "norm_mod")
        gu, act = ffn_in_act(h2, full["w_ffn_in"], "ffn_in_act")
        F = act.shape[1]
        ff, x3 = mm_nn(act, full["w_ffn_out"].reshape(1, F, D), BF, "mm_ffn_out", res=x2, gate=mods[5:6])
        sv.update(mix=mix, x2=x2, h2=h2, gu=gu, act=act, ff=ff, w=full)
        saved.append(sv)
        xcur = x3
        if i + 1 < nlayer:
            token = relay_weights(i + 1, ff)
            if i + 2 < nlayer:
                token = token + start_weights(i + 2, token)
            mods_all = mods_all + token[0:1, 0:1]

    loss_blk, dx, dfinal, dff = final_loss(xcur, tgt, final_g[None, :], saved[-1]["ff"], mods_all[nlayer - 1], 5,
                                           "final_loss")
    dg2 = dfinal[1:2]

    gland = {n: lax.empty((W[n].shape[0], N_CHIP) + W[n].shape[1:], BF) for n in SHARDED}
    in_flight = []
    dmods = [None] * nlayer
    dnorm1, dnorm2 = [None] * nlayer, [None] * nlayer
    dconv_w = [None] * nconv
    ds5 = [None] * ssm_a_re.shape[0]
    token = jnp.zeros((8, 128), F32)

    def send_grads(names, grads, slot, after, name):
        sems, thru, lands, tok = scatter_start([grads[n] for n in names], [gland[n] for n in names], slot, after, name)
        gland.update(zip(names, lands))
        in_flight.append((names, slot, sems, thru, name))
        return tok

    def land_grads(group, after):
        for names, slot, sems, thru, name in in_flight:
            if names[0] in group:
                got = scatter_wait(sems, thru, [gland[n] for n in names], slot, after, name.replace("scatter", "landed"))
                gland.update(zip(names, got))

    for i in reversed(range(nlayer)):
        j = i // 2
        mods = mods_all[i] + token[0:1, 0:1]
        sv = saved[i]
        full = sv["w"]
        gfull = {}
        F = sv["act"].shape[1]
        gfull["w_ffn_out"] = mm_tn(sv["act"], dff, 1, "mm_tn_ffn_out").reshape(N_CHIP, F // N_CHIP, D)
        dgu = ffn_out_bwd(dff, full["w_ffn_out"].reshape(F, D), sv["gu"], "ffn_out_bwd")
        gfull["w_ffn_in"] = mm_tn(sv["h2"], dgu, N_CHIP, "mm_tn_ffn_in")
        dh2 = mm_nt(dgu, full["w_ffn_in"], BF, "mm_nt_ffn_in")
        token = send_grads(["w_ffn_out", "w_ffn_in"], gfull, [i, i], dh2, "scatter_ffn%d" % i)
        mods = mods + token[0:1, 0:1]
        dx2, s2, dmix = norm_bwd(dh2, sv["x2"], dx, norm2_g[i:i + 1], mods, 3, "norm_bwd_mix",
                                 branch=(sv["mix"], mods, 2))
        dg1 = s2[3:4]
        if i % 2 == 0:
            do = glu_bwd(dmix, sv["o"], "glu_bwd")
            gfull["ssm_w_out"] = mm_tn(sv["z"], do, N_CHIP, "mm_tn_ssm_out")
            dz = mm_nt(do, full["ssm_w_out"], BF, "mm_nt_ssm_out")
            dh, dd, dab, db, dc = s5_bwd(dz, sv["y"], sv["h"], sv["states"], s5[j]["tc"], s5[j]["tbt"], s5[j]["pwr"],
                                         ssm_d[j:j + 1], "s5_bwd")
            ds5[j] = (dd, dab, db, dc)
        else:
            gfull["conv_w_out"] = mm_tn(sv["mc"], dmix, 1, "mm_tn_conv_out").reshape(N_CHIP, D // N_CHIP, D)
            dmc = mm_nt(dmix, full["conv_w_out"].reshape(1, D, D), BF, "mm_nt_conv_out")
            dbg, dcg, dvv, dcw = conv_bwd(dmc, sv["p"], cw_full[j], "conv_bwd")
            dp = jnp.concatenate([dbg, dcg, dvv], axis=1)
            gfull["conv_w_in"] = mm_tn(sv["h"], dp, N_CHIP, "mm_tn_conv_in")
            dh = mm_nt(dp, full["conv_w_in"], BF, "mm_nt_conv_in")
            dconv_w[j] = dcw[0:3]
        dmods_i = [s2[0:2], dg2]
        if i > 0:
            dx, s1, dff = norm_bwd(dh, sv["x"], dx2, norm1_g[i:i + 1], mods, 0, "norm_bwd_ffn",
                                   branch=(saved[i - 1]["ff"], mods_all[i - 1], 5))
            dg2 = s1[3:4]
        else:
            dx, s1 = norm_bwd(dh, sv["x"], dx2, norm1_g[i:i + 1], mods, 0, "norm_bwd")
        dmods[i] = jnp.concatenate([s1[0:2], dg1] + dmods_i, axis=0).reshape(6 * D)
        dnorm1[i], dnorm2[i] = s1[2], s2[2]
        names = ["ssm_w_out"] if i % 2 == 0 else ["conv_w_out", "conv_w_in"]
        token = send_grads(names, gfull, [j] * len(names), dx, "scatter_mix%d" % i)

    small = dict(norm1_g=jnp.stack(dnorm1), norm2_g=jnp.stack(dnorm2), b_ada=jnp.stack(dmods),
                 final_g=dfinal[0] + token[0, 0])
    per = {n: [] for n in ('ssm_a_re', 'ssm_a_im', 'ssm_log_step', 'ssm_b_re', 'ssm_b_im', 'ssm_c_re', 'ssm_c_im', 'ssm_d')}
    GL = G // nkb
    for j, (dd, dab, db, dc) in enumerate(ds5):
        dab = jnp.sum(dab, axis=1).reshape(nkb, 2, GL, SSM_STATE)
        g_abr, g_abi = dab[:, 0].reshape(G, SSM_STATE), dab[:, 1].reshape(G, SSM_STATE)
        db, dc = db.reshape(G, SSM_GROUP, 2, SSM_STATE), dc.reshape(G, SSM_GROUP, 2, SSM_STATE)
        gb_re, gb_im, gc_re, gc_im = db[:, :, 0], db[:, :, 1], dc[:, :, 0], dc[:, :, 1]
        ga_re, ga_im, gls, gbr, gbi = s5[j]["vjp"]((g_abr, g_abi, jnp.swapaxes(gb_re, 1, 2), jnp.swapaxes(gb_im, 1, 2)))
        for n, val in zip(per, (ga_re, ga_im, gls, gbr, gbi, gc_re, -gc_im, jnp.sum(dd, axis=0))):
            per[n].append(val)
    small.update({n: jnp.stack(vals) for n, vals in per.items()})
    dcw_full = jnp.stack(dconv_w)

    my_loss = loss_blk[0:1, 0:1]
    slab_like = [W[n] for n in SLAB] + [dcw_full, my_loss]
    rows64 = 8 * N_DEV
    slab = _pack([small[n] for n in SLAB] + [dcw_full, my_loss], rows64)
    per_dev = slab.shape[0] // N_DEV
    x_sems, x_srcs, x_lands, token = exchange_start(
        [(slab.reshape(N_DEV, per_dev, SLAB_W), True), (_pack([small["b_ada"]]), False)], dx, "small_scatter")

    early = [n for n in SHARDED if n != "ssm_w_out"]
    land_grads(early, token)
    mine = [reduce4(gland[n], "reduce4_" + n) for n in early]

    parts, dm_all = exchange_wait(x_sems, x_srcs, x_lands, [True, False], mine[-1][0, :8, :128], "small_landed")
    t_sems, t_srcs, t_lands, token = exchange_start([(sum8(parts, "sum_small"), False)], dm_all, "small_gather")
    out = {}

    w_sems, w_srcs, w_lands, token2 = swap_start(mine, "swap_start")
    dm_all = dm_all.reshape(N_DEV, -1)[:, :b_ada.size].reshape(N_DEV, nlayer, N_CHIP, NA)
    dm_sh = jnp.transpose(lax.dynamic_index_in_dim(dm_all, chip, axis=2, keepdims=False), (1, 0, 2))
    res = adamw_ada(jnp.transpose(c_all) + token[0:1, 0:1] + token2[0:1, 0:1], dm_sh, w_ada, m_w_ada, v_w_ada,
                    "adamw_ada")
    out["g", "w_ada"], out["d", "w_ada"], out["m", "w_ada"], out["v", "w_ada"] = res

    g_slab = exchange_wait(t_sems, t_srcs, t_lands, [False], out["g", "w_ada"], "small_total")[0]
    g_slab = g_slab.reshape(slab.shape)
    d_slab, m_slab, v_slab = adamw_slab(
        g_slab, _pack([W[n] for n in SLAB] + [jnp.zeros_like(dcw_full)], rows64),
        _pack([Mo[n] for n in SLAB] + [jnp.zeros_like(dcw_full)], rows64),
        _pack([Vo[n] for n in SLAB] + [jnp.ones_like(dcw_full)], rows64), "adamw_slab")
    for k, slab in zip(("g", "d", "m", "v"), (g_slab, d_slab, m_slab, v_slab)):
        for n, val in zip(SLAB, _unpack(slab, slab_like)):
            out[k, n] = val
    g_cw = lax.dynamic_slice_in_dim(_unpack(g_slab, slab_like)[-2], chip * conv_w.shape[2], conv_w.shape[2], axis=2)
    out["g", "conv_w"] = g_cw
    out["d", "conv_w"], out["m", "conv_w"], out["v", "conv_w"] = [
        r.reshape(conv_w.shape) for r in adamw_plain(conv_w.reshape(-1, conv_w.shape[2]), m_conv_w.reshape(-1, conv_w.shape[2]),
                                                     v_conv_w.reshape(-1, conv_w.shape[2]), g_cw.reshape(-1, conv_w.shape[2]),
                                                     "adamw_conv_w")]

    mine, theirs = swap_wait(w_sems, w_srcs, w_lands, d_slab, "swap_wait")
    for n, ga, gb in zip(early, mine, theirs):
        r = adamw_sharded(W[n], Mo[n], Vo[n], ga, gb, "adamw_" + n)
        out["g", n], out["d", n], out["m", n], out["v", n] = r

    land_grads(["ssm_w_out"], out["g", "w_ffn_out"])
    ga = reduce4(gland["ssm_w_out"], "reduce4_ssm_w_out")
    gb = swap_siblings([ga], "swap_siblings")[0]
    r = adamw_sharded(ssm_w_out, m_ssm_w_out, v_ssm_w_out, ga, gb, "adamw_ssm_w_out")
    out["g", "ssm_w_out"], out["d", "ssm_w_out"], out["m", "ssm_w_out"], out["v", "ssm_w_out"] = r

    loss = _unpack(g_slab, slab_like)[-1][0, 0]
    return (loss, dx[None], *[out["g", n] for n in WEIGHTS], *[out["d", n] for n in WEIGHTS],
            *[out["m", n] for n in WEIGHTS], *[out["v", n] for n in WEIGHTS])
```
